```python
import math
import jax, jax.numpy as jnp
from jax import lax
import numpy as np

D_MODEL = 1024
BATCH = 8
SEQ = 4096
DEPTH = 1

SSM_D_INNER = D_MODEL
SSM_HEAD_DIM = 64
SSM_HEADS = SSM_D_INNER // SSM_HEAD_DIM
SSM_GROUPS = 2
SSM_STATE = 128
SSM_CHUNK = 128
LRU_WIDTH = D_MODEL
LRU_HEADS = 16
LRU_BLOCK = LRU_WIDTH // LRU_HEADS
LRU_C = 8.0
CONV_WIDTH = 4
D_FF = 4 * D_MODEL
NORM_EPS = 1e-6

SSM_XBC = SSM_D_INNER + 2 * SSM_GROUPS * SSM_STATE
IN_SPLIT_SIZES = [SSM_D_INNER, SSM_XBC, SSM_HEADS, LRU_WIDTH, LRU_WIDTH, D_MODEL, D_MODEL]
IN_SPLIT_IDX = [int(v) for v in np.cumsum(IN_SPLIT_SIZES)[:-1]]
D_IN_PROJ = int(sum(IN_SPLIT_SIZES))

kernel_name = "hybrid_ssd_rglru_gated_block"


def rms_norm(x, g):
    xf = x.astype(jnp.float32)
    var = jnp.mean(xf * xf, axis=-1, keepdims=True)
    return (xf * lax.rsqrt(var + NORM_EPS) * g.astype(jnp.float32)).astype(x.dtype)


def causal_depthwise_conv(x, w, b):
    k, c = w.shape
    out = lax.conv_general_dilated(
        x, w[:, None, :].astype(x.dtype), window_strides=(1,), padding=[(k - 1, 0)],
        dimension_numbers=("NWC", "WIO", "NWC"), feature_group_count=c)
    return out + b.astype(x.dtype)


def ssd_chunked(xs, dt, a, bm, cm):
    b, l, h, p = xs.shape
    g, n = bm.shape[2], bm.shape[3]
    r = h // g
    c = l // SSM_CHUNK
    L = SSM_CHUNK
    x = xs.reshape(b, c, L, g, r, p)
    dtc = dt.reshape(b, c, L, g, r)
    bc = bm.reshape(b, c, L, g, n)
    cc = cm.reshape(b, c, L, g, n)
    a_dt = dtc * a.reshape(g, r)
    a_cum = jnp.cumsum(a_dt, axis=2)
    seg = a_cum[:, :, :, None] - a_cum[:, :, None]
    mask = jnp.tril(jnp.ones((L, L), dtype=bool))[:, :, None, None]
    decay = jnp.exp(jnp.where(mask, seg, -jnp.inf))
    cb = jnp.einsum("bclgn,bcsgn->bclsg", cc, bc)
    wts = cb[..., None] * decay * dtc[:, :, None]
    y_diag = jnp.einsum("bclsgr,bcsgrp->bclgrp", wts, x)
    decay_states = jnp.exp(a_cum[:, :, -1:] - a_cum)
    states = jnp.einsum("bcsgn,bcsgr,bcsgrp->bcgrpn", bc, decay_states * dtc, x)
    chunk_decay = jnp.exp(a_cum[:, :, -1])

    def step(h_prev, inp):
        dec, st = inp
        h_new = dec[..., None, None] * h_prev + st
        return h_new, h_prev

    h0 = jnp.zeros((b, g, r, p, n), dtype=xs.dtype)
    _, prev = lax.scan(step, h0, (jnp.moveaxis(chunk_decay, 1, 0), jnp.moveaxis(states, 1, 0)))
    prev = jnp.moveaxis(prev, 0, 1)
    y_off = jnp.einsum("bclgn,bcgrpn,bclgr->bclgrp", cc, prev, jnp.exp(a_cum))
    return (y_diag + y_off).reshape(b, l, h, p)


def mamba2_branch(z, xbc, dt_raw, conv_w, conv_b, dt_bias, a_log, d_skip, norm_g):
    b, l, _ = z.shape
    xbc = jax.nn.silu(causal_depthwise_conv(xbc, conv_w, conv_b))
    xs, bm, cm = jnp.split(xbc, [SSM_D_INNER, SSM_D_INNER + SSM_GROUPS * SSM_STATE], axis=-1)
    xs = xs.reshape(b, l, SSM_HEADS, SSM_HEAD_DIM)
    bm = bm.reshape(b, l, SSM_GROUPS, SSM_STATE)
    cm = cm.reshape(b, l, SSM_GROUPS, SSM_STATE)
    dt = jax.nn.softplus(dt_raw + dt_bias)
    a = -jnp.exp(a_log)
    y = ssd_chunked(xs, dt, a, bm, cm) + d_skip[:, None] * xs
    y = y.reshape(b, l, SSM_D_INNER) * jax.nn.silu(z)
    yg = y.reshape(b, l, SSM_GROUPS, SSM_D_INNER // SSM_GROUPS)
    yg = yg * lax.rsqrt(jnp.mean(yg * yg, axis=-1, keepdims=True) + NORM_EPS)
    return yg.reshape(b, l, SSM_D_INNER) * norm_g


def rglru_branch(g_in, x_in, conv_w, conv_b, wa, ba, wx, bx, lam):
    b, l, _ = x_in.shape
    xr = causal_depthwise_conv(x_in, conv_w, conv_b)
    xb = xr.reshape(b, l, LRU_HEADS, LRU_BLOCK)
    gate_r = jax.nn.sigmoid(jnp.einsum("blhi,hij->blhj", xb, wa) + ba).reshape(b, l, LRU_WIDTH)
    gate_i = jax.nn.sigmoid(jnp.einsum("blhi,hij->blhj", xb, wx) + bx).reshape(b, l, LRU_WIDTH)
    log_a = -LRU_C * gate_r * jax.nn.softplus(-lam)
    a = jnp.exp(log_a)
    mult = jnp.sqrt(-jnp.expm1(2.0 * log_a))
    u = mult * (gate_i * xr)

    def combine(e1, e2):
        a1, b1 = e1
        a2, b2 = e2
        return a1 * a2, a2 * b1 + b2

    _, h = lax.associative_scan(combine, (a, u), axis=1)
    return h * jax.nn.gelu(g_in, approximate=True)


def _fwd_setup_inputs(seed: int = 0) -> dict:
    key = jax.random.key(seed)
    ks = jax.random.split(key, 24)
    f32 = jnp.float32
    nrm = lambda k, shape, s: jax.random.normal(k, shape, f32) * s
    gain = lambda k, shape: 1.0 + 0.02 * jax.random.normal(k, shape, f32)
    dt0 = jnp.exp(jax.random.uniform(ks[5], (DEPTH, SSM_HEADS), f32, math.log(1e-3), math.log(1e-1)))
    a0 = jax.random.uniform(ks[15], (DEPTH, LRU_WIDTH), f32, 0.9, 0.999)
    s0 = a0 ** (1.0 / LRU_C)
    return {
        "x": jax.random.normal(ks[0], (BATCH, SEQ, D_MODEL), f32),
        "norm_mix_pre": gain(ks[1], (DEPTH, D_MODEL)),
        "w_in": nrm(ks[2], (DEPTH, D_MODEL, D_IN_PROJ), D_MODEL ** -0.5),
        "conv_ssm_w": nrm(ks[3], (DEPTH, CONV_WIDTH, SSM_XBC), CONV_WIDTH ** -0.5),
        "conv_ssm_b": nrm(ks[4], (DEPTH, SSM_XBC), 0.01),
        "dt_bias": dt0 + jnp.log(-jnp.expm1(-dt0)),
        "a_log": jnp.log(jax.random.uniform(ks[6], (DEPTH, SSM_HEADS), f32, 1.0, 16.0)),
        "d_skip": gain(ks[7], (DEPTH, SSM_HEADS)),
        "ssm_norm": gain(ks[8], (DEPTH, SSM_D_INNER)),
        "conv_lru_w": nrm(ks[9], (DEPTH, CONV_WIDTH, LRU_WIDTH), CONV_WIDTH ** -0.5),
        "conv_lru_b": nrm(ks[10], (DEPTH, LRU_WIDTH), 0.01),
        "lru_wa": nrm(ks[11], (DEPTH, LRU_HEADS, LRU_BLOCK, LRU_BLOCK), LRU_BLOCK ** -0.5),
        "lru_ba": nrm(ks[12], (DEPTH, LRU_WIDTH // LRU_BLOCK, LRU_BLOCK), 0.01),
        "lru_wx": nrm(ks[13], (DEPTH, LRU_HEADS, LRU_BLOCK, LRU_BLOCK), LRU_BLOCK ** -0.5),
        "lru_bx": nrm(ks[14], (DEPTH, LRU_WIDTH // LRU_BLOCK, LRU_BLOCK), 0.01),
        "lru_lambda": jnp.log(s0) - jnp.log1p(-s0),
        "w_out": nrm(ks[16], (DEPTH, D_MODEL, D_MODEL), D_MODEL ** -0.5),
        "norm_mix_post": gain(ks[17], (DEPTH, D_MODEL)),
        "norm_mlp_pre": gain(ks[18], (DEPTH, D_MODEL)),
        "w_up": nrm(ks[19], (DEPTH, D_MODEL, D_FF), D_MODEL ** -0.5),
        "w_down": nrm(ks[20], (DEPTH, D_FF, D_MODEL), D_FF ** -0.5),
        "norm_mlp_post": gain(ks[21], (DEPTH, D_MODEL)),
    }


def _fwd_reference(x, norm_mix_pre, w_in, conv_ssm_w, conv_ssm_b, dt_bias, a_log, d_skip,
              ssm_norm, conv_lru_w, conv_lru_b, lru_wa, lru_ba, lru_wx, lru_bx,
              lru_lambda, w_out, norm_mix_post, norm_mlp_pre, w_up, w_down, norm_mlp_post):
    f32 = jnp.float32
    h = x
    for i in range(DEPTH):
        u = rms_norm(h, norm_mix_pre[i])
        proj = jnp.einsum("bsd,de->bse", u, w_in[i]).astype(f32)
        z, xbc, dt_raw, g_lru, x_lru, gate_a, gate_b = jnp.split(proj, IN_SPLIT_IDX, axis=-1)
        y_a = mamba2_branch(z, xbc, dt_raw, conv_ssm_w[i].astype(f32), conv_ssm_b[i].astype(f32),
                            dt_bias[i].astype(f32), a_log[i].astype(f32), d_skip[i].astype(f32),
                            ssm_norm[i].astype(f32))
        y_b = rglru_branch(g_lru, x_lru, conv_lru_w[i].astype(f32), conv_lru_b[i].astype(f32),
                           lru_wa[i].astype(f32), lru_ba[i].astype(f32),
                           lru_wx[i].astype(f32), lru_bx[i].astype(f32), lru_lambda[i].astype(f32))
        merged = (jax.nn.sigmoid(gate_a) * y_a + jax.nn.sigmoid(gate_b) * y_b).astype(h.dtype)
        mix = jnp.einsum("bsd,de->bse", merged, w_out[i])
        h = h + rms_norm(mix, norm_mix_post[i])
        v = rms_norm(h, norm_mlp_pre[i])
        hid = jnp.square(jax.nn.relu(jnp.einsum("bsd,df->bsf", v, w_up[i])))
        ff = jnp.einsum("bsf,fd->bsd", hid, w_down[i])
        h = h + rms_norm(ff, norm_mlp_post[i])
    return h


import jax as _jax
import jax.numpy as _jnp

TWIN_FORMAT = 'train_step'
FWD_PARAMS = ['x', 'norm_mix_pre', 'w_in', 'conv_ssm_w', 'conv_ssm_b', 'dt_bias', 'a_log', 'd_skip', 'ssm_norm', 'conv_lru_w', 'conv_lru_b', 'lru_wa', 'lru_ba', 'lru_wx', 'lru_bx', 'lru_lambda', 'w_out', 'norm_mix_post', 'norm_mlp_pre', 'w_up', 'w_down', 'norm_mlp_post']
TWIN_WEIGHTS = ['norm_mix_pre', 'w_in', 'conv_ssm_w', 'conv_ssm_b', 'dt_bias', 'a_log', 'd_skip', 'ssm_norm', 'conv_lru_w', 'conv_lru_b', 'lru_wa', 'lru_ba', 'lru_wx', 'lru_bx', 'lru_lambda', 'w_out', 'norm_mix_post', 'norm_mlp_pre', 'w_up', 'w_down', 'norm_mlp_post']
TWIN_DIFF_INPUT = 'x'
TWIN_INPUTS = ['x', 'norm_mix_pre', 'w_in', 'conv_ssm_w', 'conv_ssm_b', 'dt_bias', 'a_log', 'd_skip', 'ssm_norm', 'conv_lru_w', 'conv_lru_b', 'lru_wa', 'lru_ba', 'lru_wx', 'lru_bx', 'lru_lambda', 'w_out', 'norm_mix_post', 'norm_mlp_pre', 'w_up', 'w_down', 'norm_mlp_post', 'loss_target', 'm_norm_mix_pre', 'm_w_in', 'm_conv_ssm_w', 'm_conv_ssm_b', 'm_dt_bias', 'm_a_log', 'm_d_skip', 'm_ssm_norm', 'm_conv_lru_w', 'm_conv_lru_b', 'm_lru_wa', 'm_lru_ba', 'm_lru_wx', 'm_lru_bx', 'm_lru_lambda', 'm_w_out', 'm_norm_mix_post', 'm_norm_mlp_pre', 'm_w_up', 'm_w_down', 'm_norm_mlp_post', 'v_norm_mix_pre', 'v_w_in', 'v_conv_ssm_w', 'v_conv_ssm_b', 'v_dt_bias', 'v_a_log', 'v_d_skip', 'v_ssm_norm', 'v_conv_lru_w', 'v_conv_lru_b', 'v_lru_wa', 'v_lru_ba', 'v_lru_wx', 'v_lru_bx', 'v_lru_lambda', 'v_w_out', 'v_norm_mix_post', 'v_norm_mlp_pre', 'v_w_up', 'v_w_down', 'v_norm_mlp_post']
TWIN_OUTPUTS = ['loss', 'grad_x', 'grad_norm_mix_pre', 'grad_w_in', 'grad_conv_ssm_w', 'grad_conv_ssm_b', 'grad_dt_bias', 'grad_a_log', 'grad_d_skip', 'grad_ssm_norm', 'grad_conv_lru_w', 'grad_conv_lru_b', 'grad_lru_wa', 'grad_lru_ba', 'grad_lru_wx', 'grad_lru_bx', 'grad_lru_lambda', 'grad_w_out', 'grad_norm_mix_post', 'grad_norm_mlp_pre', 'grad_w_up', 'grad_w_down', 'grad_norm_mlp_post', 'delta_norm_mix_pre', 'delta_w_in', 'delta_conv_ssm_w', 'delta_conv_ssm_b', 'delta_dt_bias', 'delta_a_log', 'delta_d_skip', 'delta_ssm_norm', 'delta_conv_lru_w', 'delta_conv_lru_b', 'delta_lru_wa', 'delta_lru_ba', 'delta_lru_wx', 'delta_lru_bx', 'delta_lru_lambda', 'delta_w_out', 'delta_norm_mix_post', 'delta_norm_mlp_pre', 'delta_w_up', 'delta_w_down', 'delta_norm_mlp_post', 'new_m_norm_mix_pre', 'new_m_w_in', 'new_m_conv_ssm_w', 'new_m_conv_ssm_b', 'new_m_dt_bias', 'new_m_a_log', 'new_m_d_skip', 'new_m_ssm_norm', 'new_m_conv_lru_w', 'new_m_conv_lru_b', 'new_m_lru_wa', 'new_m_lru_ba', 'new_m_lru_wx', 'new_m_lru_bx', 'new_m_lru_lambda', 'new_m_w_out', 'new_m_norm_mix_post', 'new_m_norm_mlp_pre', 'new_m_w_up', 'new_m_w_down', 'new_m_norm_mlp_post', 'new_v_norm_mix_pre', 'new_v_w_in', 'new_v_conv_ssm_w', 'new_v_conv_ssm_b', 'new_v_dt_bias', 'new_v_a_log', 'new_v_d_skip', 'new_v_ssm_norm', 'new_v_conv_lru_w', 'new_v_conv_lru_b', 'new_v_lru_wa', 'new_v_lru_ba', 'new_v_lru_wx', 'new_v_lru_bx', 'new_v_lru_lambda', 'new_v_w_out', 'new_v_norm_mix_post', 'new_v_norm_mlp_pre', 'new_v_w_up', 'new_v_w_down', 'new_v_norm_mlp_post']
TWIN_LEAF_KINDS = {'loss': 'loss', 'grad_x': 'grad_x', 'grad_norm_mix_pre': 'grad_w', 'grad_w_in': 'grad_w', 'grad_conv_ssm_w': 'grad_w', 'grad_conv_ssm_b': 'grad_w', 'grad_dt_bias': 'grad_w', 'grad_a_log': 'grad_w', 'grad_d_skip': 'grad_w', 'grad_ssm_norm': 'grad_w', 'grad_conv_lru_w': 'grad_w', 'grad_conv_lru_b': 'grad_w', 'grad_lru_wa': 'grad_w', 'grad_lru_ba': 'grad_w', 'grad_lru_wx': 'grad_w', 'grad_lru_bx': 'grad_w', 'grad_lru_lambda': 'grad_w', 'grad_w_out': 'grad_w', 'grad_norm_mix_post': 'grad_w', 'grad_norm_mlp_pre': 'grad_w', 'grad_w_up': 'grad_w', 'grad_w_down': 'grad_w', 'grad_norm_mlp_post': 'grad_w', 'delta_norm_mix_pre': 'delta_w', 'delta_w_in': 'delta_w', 'delta_conv_ssm_w': 'delta_w', 'delta_conv_ssm_b': 'delta_w', 'delta_dt_bias': 'delta_w', 'delta_a_log': 'delta_w', 'delta_d_skip': 'delta_w', 'delta_ssm_norm': 'delta_w', 'delta_conv_lru_w': 'delta_w', 'delta_conv_lru_b': 'delta_w', 'delta_lru_wa': 'delta_w', 'delta_lru_ba': 'delta_w', 'delta_lru_wx': 'delta_w', 'delta_lru_bx': 'delta_w', 'delta_lru_lambda': 'delta_w', 'delta_w_out': 'delta_w', 'delta_norm_mix_post': 'delta_w', 'delta_norm_mlp_pre': 'delta_w', 'delta_w_up': 'delta_w', 'delta_w_down': 'delta_w', 'delta_norm_mlp_post': 'delta_w', 'new_m_norm_mix_pre': 'new_m', 'new_m_w_in': 'new_m', 'new_m_conv_ssm_w': 'new_m', 'new_m_conv_ssm_b': 'new_m', 'new_m_dt_bias': 'new_m', 'new_m_a_log': 'new_m', 'new_m_d_skip': 'new_m', 'new_m_ssm_norm': 'new_m', 'new_m_conv_lru_w': 'new_m', 'new_m_conv_lru_b': 'new_m', 'new_m_lru_wa': 'new_m', 'new_m_lru_ba': 'new_m', 'new_m_lru_wx': 'new_m', 'new_m_lru_bx': 'new_m', 'new_m_lru_lambda': 'new_m', 'new_m_w_out': 'new_m', 'new_m_norm_mix_post': 'new_m', 'new_m_norm_mlp_pre': 'new_m', 'new_m_w_up': 'new_m', 'new_m_w_down': 'new_m', 'new_m_norm_mlp_post': 'new_m', 'new_v_norm_mix_pre': 'new_v', 'new_v_w_in': 'new_v', 'new_v_conv_ssm_w': 'new_v', 'new_v_conv_ssm_b': 'new_v', 'new_v_dt_bias': 'new_v', 'new_v_a_log': 'new_v', 'new_v_d_skip': 'new_v', 'new_v_ssm_norm': 'new_v', 'new_v_conv_lru_w': 'new_v', 'new_v_conv_lru_b': 'new_v', 'new_v_lru_wa': 'new_v', 'new_v_lru_ba': 'new_v', 'new_v_lru_wx': 'new_v', 'new_v_lru_bx': 'new_v', 'new_v_lru_lambda': 'new_v', 'new_v_w_out': 'new_v', 'new_v_norm_mix_post': 'new_v', 'new_v_norm_mlp_pre': 'new_v', 'new_v_w_up': 'new_v', 'new_v_w_down': 'new_v', 'new_v_norm_mlp_post': 'new_v'}


def _forward(args):
    return _fwd_reference(*[args[k] for k in FWD_PARAMS])


def _output_shape():
    def fwd():
        inp = _fwd_setup_inputs(0)
        return _fwd_reference(*[inp[k] for k in FWD_PARAMS])
    out = _jax.eval_shape(fwd)
    return out.shape, out.dtype

N_MICROBATCH = 1
ADAM_LR = 0.001
ADAM_B1 = 0.9
ADAM_B2 = 0.999
ADAM_EPS = 1e-08
ADAM_WD = 0.01
ADAM_STEP = 10
PER_EXAMPLE_BATCH_AXIS = {'x': 0, 'loss_target': 0}
SHARED_INPUTS = []
_WEIGHT_DTYPES = {'norm_mix_pre': _jnp.float32, 'w_in': _jnp.float32, 'conv_ssm_w': _jnp.float32, 'conv_ssm_b': _jnp.float32, 'dt_bias': _jnp.float32, 'a_log': _jnp.float32, 'd_skip': _jnp.float32, 'ssm_norm': _jnp.float32, 'conv_lru_w': _jnp.float32, 'conv_lru_b': _jnp.float32, 'lru_wa': _jnp.float32, 'lru_ba': _jnp.float32, 'lru_wx': _jnp.float32, 'lru_bx': _jnp.float32, 'lru_lambda': _jnp.float32, 'w_out': _jnp.float32, 'norm_mix_post': _jnp.float32, 'norm_mlp_pre': _jnp.float32, 'w_up': _jnp.float32, 'w_down': _jnp.float32, 'norm_mlp_post': _jnp.float32}
MOMENT_SCALE = {'norm_mix_pre': 7.753716e-01, 'w_in': 2.949269e-01, 'conv_ssm_w': 1.025455e+00, 'conv_ssm_b': 3.532028e+00, 'dt_bias': 1.233869e+00, 'a_log': 3.129552e+00, 'd_skip': 5.081582e+00, 'ssm_norm': 1.838033e+00, 'conv_lru_w': 9.787169e-01, 'conv_lru_b': 1.215767e+01, 'lru_wa': 4.493688e-01, 'lru_ba': 2.387567e-01, 'lru_wx': 8.218097e-01, 'lru_bx': 2.636970e-01, 'lru_lambda': 3.235678e-01, 'w_out': 2.406239e+00, 'norm_mix_post': 3.214430e+01, 'norm_mlp_pre': 1.131044e+00, 'w_up': 5.407922e-01, 'w_down': 2.524260e+00, 'norm_mlp_post': 3.292118e+01}


def _to_microbatches(a, axis):
    t = _jnp.moveaxis(a, axis, 0)
    t = t.reshape((N_MICROBATCH, t.shape[0] // N_MICROBATCH) + t.shape[1:])
    return _jnp.moveaxis(t, 1, axis + 1)


def setup_inputs(seed: int = 0) -> dict:
    inp = _fwd_setup_inputs(seed)
    key = _jax.random.fold_in(_jax.random.key(seed), 7919)
    shape, _ = _output_shape()
    out = dict(inp)
    out["loss_target"] = _jax.random.normal(_jax.random.fold_in(key, 0), shape, _jnp.float32)
    for i, name in enumerate(TWIN_WEIGHTS):
        w = inp[name].astype(_jnp.float32)
        if MOMENT_SCALE is None:
            s = _jnp.sqrt(_jnp.mean(_jnp.square(w)) + 1e-30)
        else:
            s = MOMENT_SCALE[name]
        km, kv = _jax.random.split(_jax.random.fold_in(key, i + 1))
        out[name] = w
        out["m_" + name] = s * _jax.random.normal(km, w.shape, _jnp.float32)
        out["v_" + name] = (s * s) * _jax.random.uniform(kv, w.shape, _jnp.float32, 0.5, 1.5)
    if N_MICROBATCH > 1:
        for name, axis in PER_EXAMPLE_BATCH_AXIS.items():
            out[name] = _to_microbatches(out[name], axis)
    return {'x': out['x'], 'norm_mix_pre': out['norm_mix_pre'], 'w_in': out['w_in'], 'conv_ssm_w': out['conv_ssm_w'], 'conv_ssm_b': out['conv_ssm_b'], 'dt_bias': out['dt_bias'], 'a_log': out['a_log'], 'd_skip': out['d_skip'], 'ssm_norm': out['ssm_norm'], 'conv_lru_w': out['conv_lru_w'], 'conv_lru_b': out['conv_lru_b'], 'lru_wa': out['lru_wa'], 'lru_ba': out['lru_ba'], 'lru_wx': out['lru_wx'], 'lru_bx': out['lru_bx'], 'lru_lambda': out['lru_lambda'], 'w_out': out['w_out'], 'norm_mix_post': out['norm_mix_post'], 'norm_mlp_pre': out['norm_mlp_pre'], 'w_up': out['w_up'], 'w_down': out['w_down'], 'norm_mlp_post': out['norm_mlp_post'], 'loss_target': out['loss_target'], 'm_norm_mix_pre': out['m_norm_mix_pre'], 'm_w_in': out['m_w_in'], 'm_conv_ssm_w': out['m_conv_ssm_w'], 'm_conv_ssm_b': out['m_conv_ssm_b'], 'm_dt_bias': out['m_dt_bias'], 'm_a_log': out['m_a_log'], 'm_d_skip': out['m_d_skip'], 'm_ssm_norm': out['m_ssm_norm'], 'm_conv_lru_w': out['m_conv_lru_w'], 'm_conv_lru_b': out['m_conv_lru_b'], 'm_lru_wa': out['m_lru_wa'], 'm_lru_ba': out['m_lru_ba'], 'm_lru_wx': out['m_lru_wx'], 'm_lru_bx': out['m_lru_bx'], 'm_lru_lambda': out['m_lru_lambda'], 'm_w_out': out['m_w_out'], 'm_norm_mix_post': out['m_norm_mix_post'], 'm_norm_mlp_pre': out['m_norm_mlp_pre'], 'm_w_up': out['m_w_up'], 'm_w_down': out['m_w_down'], 'm_norm_mlp_post': out['m_norm_mlp_post'], 'v_norm_mix_pre': out['v_norm_mix_pre'], 'v_w_in': out['v_w_in'], 'v_conv_ssm_w': out['v_conv_ssm_w'], 'v_conv_ssm_b': out['v_conv_ssm_b'], 'v_dt_bias': out['v_dt_bias'], 'v_a_log': out['v_a_log'], 'v_d_skip': out['v_d_skip'], 'v_ssm_norm': out['v_ssm_norm'], 'v_conv_lru_w': out['v_conv_lru_w'], 'v_conv_lru_b': out['v_conv_lru_b'], 'v_lru_wa': out['v_lru_wa'], 'v_lru_ba': out['v_lru_ba'], 'v_lru_wx': out['v_lru_wx'], 'v_lru_bx': out['v_lru_bx'], 'v_lru_lambda': out['v_lru_lambda'], 'v_w_out': out['v_w_out'], 'v_norm_mix_post': out['v_norm_mix_post'], 'v_norm_mlp_pre': out['v_norm_mlp_pre'], 'v_w_up': out['v_w_up'], 'v_w_down': out['v_w_down'], 'v_norm_mlp_post': out['v_norm_mlp_post']}


def _loss(weights, diff, rest, loss_target):
    with _jax.named_scope("forward"):
        args = {**rest, TWIN_DIFF_INPUT: diff, **{k: w.astype(_WEIGHT_DTYPES[k]) for k, w in weights.items()}}
        y = _forward(args)
    with _jax.named_scope("loss_head"):
        err = _jnp.square(y.astype(_jnp.float32) - loss_target)
        return 0.5 * _jnp.sum(_jnp.mean(err, axis=-1)) if err.ndim else 0.5 * err


def _adamw(w, g, m, v):
    m = ADAM_B1 * m + (1.0 - ADAM_B1) * g
    v = ADAM_B2 * v + (1.0 - ADAM_B2) * _jnp.square(g)
    m_hat = m / (1.0 - ADAM_B1 ** ADAM_STEP)
    v_hat = v / (1.0 - ADAM_B2 ** ADAM_STEP)
    delta = -ADAM_LR * (m_hat / (_jnp.sqrt(v_hat) + ADAM_EPS) + ADAM_WD * w)
    return delta, m, v


def reference(x, norm_mix_pre, w_in, conv_ssm_w, conv_ssm_b, dt_bias, a_log, d_skip, ssm_norm, conv_lru_w, conv_lru_b, lru_wa, lru_ba, lru_wx, lru_bx, lru_lambda, w_out, norm_mix_post, norm_mlp_pre, w_up, w_down, norm_mlp_post, loss_target, m_norm_mix_pre, m_w_in, m_conv_ssm_w, m_conv_ssm_b, m_dt_bias, m_a_log, m_d_skip, m_ssm_norm, m_conv_lru_w, m_conv_lru_b, m_lru_wa, m_lru_ba, m_lru_wx, m_lru_bx, m_lru_lambda, m_w_out, m_norm_mix_post, m_norm_mlp_pre, m_w_up, m_w_down, m_norm_mlp_post, v_norm_mix_pre, v_w_in, v_conv_ssm_w, v_conv_ssm_b, v_dt_bias, v_a_log, v_d_skip, v_ssm_norm, v_conv_lru_w, v_conv_lru_b, v_lru_wa, v_lru_ba, v_lru_wx, v_lru_bx, v_lru_lambda, v_w_out, v_norm_mix_post, v_norm_mlp_pre, v_w_up, v_w_down, v_norm_mlp_post):
    given = dict(x=x, norm_mix_pre=norm_mix_pre, w_in=w_in, conv_ssm_w=conv_ssm_w, conv_ssm_b=conv_ssm_b, dt_bias=dt_bias, a_log=a_log, d_skip=d_skip, ssm_norm=ssm_norm, conv_lru_w=conv_lru_w, conv_lru_b=conv_lru_b, lru_wa=lru_wa, lru_ba=lru_ba, lru_wx=lru_wx, lru_bx=lru_bx, lru_lambda=lru_lambda, w_out=w_out, norm_mix_post=norm_mix_post, norm_mlp_pre=norm_mlp_pre, w_up=w_up, w_down=w_down, norm_mlp_post=norm_mlp_post, loss_target=loss_target, m_norm_mix_pre=m_norm_mix_pre, m_w_in=m_w_in, m_conv_ssm_w=m_conv_ssm_w, m_conv_ssm_b=m_conv_ssm_b, m_dt_bias=m_dt_bias, m_a_log=m_a_log, m_d_skip=m_d_skip, m_ssm_norm=m_ssm_norm, m_conv_lru_w=m_conv_lru_w, m_conv_lru_b=m_conv_lru_b, m_lru_wa=m_lru_wa, m_lru_ba=m_lru_ba, m_lru_wx=m_lru_wx, m_lru_bx=m_lru_bx, m_lru_lambda=m_lru_lambda, m_w_out=m_w_out, m_norm_mix_post=m_norm_mix_post, m_norm_mlp_pre=m_norm_mlp_pre, m_w_up=m_w_up, m_w_down=m_w_down, m_norm_mlp_post=m_norm_mlp_post, v_norm_mix_pre=v_norm_mix_pre, v_w_in=v_w_in, v_conv_ssm_w=v_conv_ssm_w, v_conv_ssm_b=v_conv_ssm_b, v_dt_bias=v_dt_bias, v_a_log=v_a_log, v_d_skip=v_d_skip, v_ssm_norm=v_ssm_norm, v_conv_lru_w=v_conv_lru_w, v_conv_lru_b=v_conv_lru_b, v_lru_wa=v_lru_wa, v_lru_ba=v_lru_ba, v_lru_wx=v_lru_wx, v_lru_bx=v_lru_bx, v_lru_lambda=v_lru_lambda, v_w_out=v_w_out, v_norm_mix_post=v_norm_mix_post, v_norm_mlp_pre=v_norm_mlp_pre, v_w_up=v_w_up, v_w_down=v_w_down, v_norm_mlp_post=v_norm_mlp_post)
    weights = {n: given[n] for n in TWIN_WEIGHTS}
    shared = {n: given[n] for n in SHARED_INPUTS}
    per_example = {n: given[n] for n in ['x']}
    grad_fn = _jax.value_and_grad(_loss, argnums=(0, 1))

    def one_microbatch(ex, loss_target):
        ex = dict(ex)
        diff = ex.pop(TWIN_DIFF_INPUT)
        return grad_fn(weights, diff, {**shared, **ex}, loss_target)

    if N_MICROBATCH == 1:
        loss, (grad_w, grad_x) = one_microbatch(per_example, given["loss_target"])
    else:
        def body(carry, xs):
            loss_sum, grad_sum = carry
            l_k, (gw_k, gx_k) = one_microbatch(xs[0], xs[1])
            with _jax.named_scope("update"):
                return (loss_sum + l_k, _jax.tree.map(_jnp.add, grad_sum, gw_k)), gx_k

        init = (_jnp.zeros((), _jnp.float32), _jax.tree.map(_jnp.zeros_like, weights))
        (loss, grad_w), grad_x = _jax.lax.scan(body, init, (per_example, given["loss_target"]))
    with _jax.named_scope("update"):
        delta_w, new_m, new_v = {}, {}, {}
        for n in TWIN_WEIGHTS:
            delta_w[n], new_m[n], new_v[n] = _adamw(weights[n], grad_w[n], given["m_" + n], given["v_" + n])
    return (loss, grad_x, *[grad_w[n] for n in TWIN_WEIGHTS], *[delta_w[n] for n in TWIN_WEIGHTS],
            *[new_m[n] for n in TWIN_WEIGHTS], *[new_v[n] for n in TWIN_WEIGHTS])
```

```python
import functools

import jax
import jax.numpy as jnp
from jax import lax
from jax.experimental import pallas as pl
from jax.experimental.pallas import tpu as pltpu

F32 = jnp.float32
MXU = jnp.bfloat16
HI = lax.Precision.HIGHEST
EPS = 1e-6

D = 1024
NH = 16
NS = 128
CH = 128
FF = 4096
NP = 7168
SEG = 1024
LRU_C = 8.0
N_DEV = 8

ADAM_LR, ADAM_B1, ADAM_B2, ADAM_EPS, ADAM_WD, ADAM_STEP = 0.001, 0.9, 0.999, 1e-08, 0.01, 10

VMEM_LIMIT = 56 * 1024 * 1024


def _cp(*sem):
    return pltpu.CompilerParams(dimension_semantics=sem, vmem_limit_bytes=VMEM_LIMIT)


def _nn(a, b):
    return jnp.dot(a.astype(MXU), b.astype(MXU), preferred_element_type=F32)


def _nt(a, b):
    return lax.dot_general(a.astype(MXU), b.astype(MXU), (((1,), (1,)), ((), ())), preferred_element_type=F32)


def _tn(a, b):
    return lax.dot_general(a.astype(MXU), b.astype(MXU), (((0,), (0,)), ((), ())), preferred_element_type=F32)


def _silu(x):
    return x * jax.nn.sigmoid(x)


def _dsilu(x):
    s = jax.nn.sigmoid(x)
    return s + x * s * (1.0 - s)


def _softplus(x):
    return jnp.maximum(x, 0.0) + jnp.log(1.0 + jnp.exp(-jnp.abs(x)))


def _rms(x, g):
    r = lax.rsqrt(jnp.mean(x * x, axis=-1, keepdims=True) + EPS)
    return x * r * g


def _rms_bwd(x, g, dy):
    r = lax.rsqrt(jnp.mean(x * x, axis=-1, keepdims=True) + EPS)
    gdy = g * dy
    dx = r * gdy - x * (r * r * r) * jnp.mean(x * gdy, axis=-1, keepdims=True)
    return dx, dy * x * r


def _rowsum(x):
    return jnp.sum(x, axis=0, keepdims=True)


def _taps_past(cur, prev8):
    r_n, c_n = cur.shape
    row = lax.broadcasted_iota(jnp.int32, (r_n, c_n), 0)
    out = []
    for k in range(4):
        s = 3 - k
        if s == 0:
            out.append(cur)
            continue
        head = jnp.concatenate([pltpu.roll(prev8, s, 0), jnp.zeros((r_n - 8, c_n), F32)], axis=0)
        out.append(jnp.where(row < s, head, pltpu.roll(cur, s, 0)))
    return out


def _taps_future(cur, fut8):
    r_n, c_n = cur.shape
    row = lax.broadcasted_iota(jnp.int32, (r_n, c_n), 0)
    out = []
    for k in range(4):
        s = 3 - k
        if s == 0:
            out.append(cur)
            continue
        tail = jnp.concatenate([jnp.zeros((r_n - 8, c_n), F32), pltpu.roll(fut8, 8 - s, 0)], axis=0)
        out.append(jnp.where(row >= r_n - s, tail, pltpu.roll(cur, r_n - s, 0)))
    return out


def _conv_apply(taps, w, b):
    acc = taps[0] * w[0:1, :]
    for k in range(1, 4):
        acc = acc + taps[k] * w[k:k + 1, :]
    return acc + b


def _inproj(x, g0, w):
    t_n = x.shape[0]
    tm = min(t_n, 1024)

    def body(x_ref, g_ref, w_ref, p_ref, u_ref):
        @pl.when(pl.program_id(1) == 0)
        def _():
            u_ref[...] = _rms(x_ref[...], g_ref[...]).astype(MXU)

        p_ref[...] = jnp.dot(u_ref[...], w_ref[...], preferred_element_type=F32)

    return pl.pallas_call(
        body, name="inproj", grid=(t_n // tm, NP // SEG),
        in_specs=[pl.BlockSpec((tm, D), lambda i, j: (i, 0)), pl.BlockSpec((1, D), lambda i, j: (0, 0)),
                  pl.BlockSpec((D, SEG), lambda i, j: (0, j))],
        out_specs=[pl.BlockSpec((tm, SEG), lambda i, j: (i, j)), pl.BlockSpec((tm, D), lambda i, j: (i, 0))],
        out_shape=[jax.ShapeDtypeStruct((t_n, NP), F32), jax.ShapeDtypeStruct((t_n, D), MXU)],
        compiler_params=_cp("parallel", "arbitrary"),
    )(x, g0, w)


def _ssd_prep(dtraw, dtb, alog):
    l_n = dtraw.shape[0]
    r = lax.broadcasted_iota(jnp.int32, (l_n, l_n), 0)
    c = lax.broadcasted_iota(jnp.int32, (l_n, l_n), 1)
    tril = (r >= c).astype(F32)
    triu = (r <= c).astype(F32)
    eye = (r == c).astype(F32)
    dt = _softplus(dtraw + dtb)
    adt = dt * (-jnp.exp(alog))
    ac = jnp.dot(tril, adt, preferred_element_type=F32, precision=HI)
    tn = (((0,), (0,)), ((), ()))
    ac_t = lax.dot_general(adt, triu, tn, preferred_element_type=F32, precision=HI)
    dt_t = lax.dot_general(dt, eye, tn, preferred_element_type=F32, precision=HI)
    return dt, dt_t, ac, ac_t, _rowsum(adt)


def _ssd_pair(j, xp, bg, cg, sp, dt, dt_t, ac, ac_t, aend):
    l_n = xp.shape[0]
    lane = lax.broadcasted_iota(jnp.int32, (l_n, 128), 1)
    sub = lax.broadcasted_iota(jnp.int32, (128, l_n), 0)
    lane1 = lax.broadcasted_iota(jnp.int32, (1, 128), 1)
    tri = lax.broadcasted_iota(jnp.int32, (l_n, l_n), 0) >= lax.broadcasted_iota(jnp.int32, (l_n, l_n), 1)
    cb = _nt(cg, bg)
    cs = _nn(cg, sp)
    ys, ss = [], []
    for e in range(2):
        h = 2 * j + e
        ac_l = jnp.sum(jnp.where(lane == h, ac, 0.0), axis=1, keepdims=True)
        dt_l = jnp.sum(jnp.where(lane == h, dt, 0.0), axis=1, keepdims=True)
        ac_s = jnp.sum(jnp.where(sub == h, ac_t, 0.0), axis=0, keepdims=True)
        dt_s = jnp.sum(jnp.where(sub == h, dt_t, 0.0), axis=0, keepdims=True)
        a_end = jnp.sum(jnp.where(lane1 == h, aend, 0.0), axis=1, keepdims=True)
        decay = jnp.exp(jnp.where(tri, ac_l - ac_s, -1e30))
        w = cb * decay * dt_s
        ys.append(_nn(w, xp) + jnp.exp(ac_l) * cs)
        ws = jnp.exp(a_end - ac_l) * dt_l
        ss.append(_tn(bg * ws, xp) + jnp.exp(a_end) * sp)
    lo = lax.broadcasted_iota(jnp.int32, (l_n, 128), 1) < 64
    lo_s = lax.broadcasted_iota(jnp.int32, (128, 128), 1) < 64
    return jnp.where(lo, ys[0], ys[1]), jnp.where(lo_s, ss[0], ss[1])


def _ssd_post(y, xs, z, dsk, nrm):
    y = (y + dsk * xs) * _silu(z)
    half = D // 2
    ya, yb = y[:, :half], y[:, half:]
    ya = ya * lax.rsqrt(jnp.mean(ya * ya, axis=-1, keepdims=True) + EPS)
    yb = yb * lax.rsqrt(jnp.mean(yb * yb, axis=-1, keepdims=True) + EPS)
    return jnp.concatenate([ya, yb], axis=1) * nrm


def _proj_specs(rows, seg_ids, order):
    return [pl.BlockSpec((rows, SEG), functools.partial(lambda i, s: (order(i), s), s=s)) for s in seg_ids]


def _prev8_specs(rows, seg_ids, order):
    rb = rows // 8
    return [pl.BlockSpec((8, SEG), functools.partial(lambda i, s: (jnp.maximum(order(i) * rb - 1, 0), s), s=s))
            for s in seg_ids]


def _full(shape):
    return pl.BlockSpec(shape, lambda i: (0,) * len(shape))


def _ssd_fwd(proj, cwx, cwb, cbx, cbb, dtb, alog, dsk, nrm):
    t_n = proj.shape[0]
    n_c = t_n // CH
    fwd = lambda i: i

    def body(z_ref, xs_ref, bc_ref, xsp_ref, bcp_ref, cwx_ref, cwb_ref, cbx_ref, cbb_ref, dtb_ref, alog_ref,
             dsk_ref, nrm_ref, ya_ref, sprev_ref, s_ref):
        c = pl.program_id(0)

        @pl.when(c == 0)
        def _():
            s_ref[...] = jnp.zeros_like(s_ref)

        keep = jnp.where(c == 0, 0.0, 1.0)
        xs_pre = _conv_apply(_taps_past(xs_ref[...], xsp_ref[...] * keep), cwx_ref[...], cbx_ref[...])
        bc_pre = _conv_apply(_taps_past(bc_ref[:, :512], bcp_ref[:, :512] * keep), cwb_ref[...], cbb_ref[...])
        prep = _ssd_prep(bc_ref[:, 512:640], dtb_ref[...], alog_ref[...])
        xs = _silu(xs_pre)
        bc = _silu(bc_pre)
        sprev_ref[0] = s_ref[...]
        ys = []
        for j in range(NH // 2):
            g = j // 4
            yp, sn = _ssd_pair(j, xs[:, 128 * j:128 * j + 128], bc[:, 128 * g:128 * g + 128],
                               bc[:, 256 + 128 * g:384 + 128 * g], s_ref[:, 128 * j:128 * j + 128], *prep)
            ys.append(yp)
            s_ref[:, 128 * j:128 * j + 128] = sn
        ya_ref[...] = _ssd_post(jnp.concatenate(ys, axis=1), xs, z_ref[...], dsk_ref[...], nrm_ref[...])

    return pl.pallas_call(
        body, name="ssd_fwd", grid=(n_c,),
        in_specs=_proj_specs(CH, (0, 5, 6), fwd) + _prev8_specs(CH, (5, 6), fwd) + [
            _full((4, D)), _full((4, 512)), _full((1, D)), _full((1, 512)), _full((1, 128)), _full((1, 128)),
            _full((1, D)), _full((1, D))],
        out_specs=[pl.BlockSpec((CH, D), lambda i: (i, 0)), pl.BlockSpec((1, NS, D), lambda i: (i, 0, 0))],
        out_shape=[jax.ShapeDtypeStruct((t_n, D), F32), jax.ShapeDtypeStruct((n_c, NS, D), F32)],
        scratch_shapes=[pltpu.VMEM((NS, D), F32)],
        compiler_params=_cp("arbitrary"),
    )(proj, proj, proj, proj, proj, cwx, cwb, cbx, cbb, dtb, alog, dsk, nrm)


def _ssd_bwd(dya, proj, sprev, cwx, cwb, cbx, cbb, dtb, alog, dsk, nrm):
    t_n = proj.shape[0]
    n_c = t_n // CH
    rev = lambda i: n_c - 1 - i

    def body(dya_ref, z_ref, xs_ref, bc_ref, xsp_ref, bcp_ref, sprev_ref, cwx_ref, cwb_ref, cbx_ref, cbb_ref,
             dtb_ref, alog_ref, dsk_ref, nrm_ref,
             dz_ref, dxs_ref, dbc_ref, dcwx_ref, dcwb_ref, dcbx_ref, dcbb_ref, ddtb_ref, dalog_ref, ddsk_ref,
             dnrm_ref, ds_ref, futx_ref, futb_ref):
        i = pl.program_id(0)
        acc_refs = (dcwx_ref, dcwb_ref, dcbx_ref, dcbb_ref, ddtb_ref, dalog_ref, ddsk_ref, dnrm_ref)

        @pl.when(i == 0)
        def _():
            for r in (ds_ref, futx_ref, futb_ref) + acc_refs:
                r[...] = jnp.zeros_like(r)

        keep = jnp.where(i == n_c - 1, 0.0, 1.0)
        taps_x = _taps_past(xs_ref[...], xsp_ref[...] * keep)
        taps_b = _taps_past(bc_ref[:, :512], bcp_ref[:, :512] * keep)
        xs_pre = _conv_apply(taps_x, cwx_ref[...], cbx_ref[...])
        bc_pre = _conv_apply(taps_b, cwb_ref[...], cbb_ref[...])
        xs = _silu(xs_pre)
        bc = _silu(bc_pre)
        prep, prep_vjp = jax.vjp(_ssd_prep, bc_ref[:, 512:640], dtb_ref[...], alog_ref[...])
        s_in = sprev_ref[0]

        def pair_args(j):
            g = j // 4
            return (xs[:, 128 * j:128 * j + 128], bc[:, 128 * g:128 * g + 128],
                    bc[:, 256 + 128 * g:384 + 128 * g], s_in[:, 128 * j:128 * j + 128]) + tuple(prep)

        y = jnp.concatenate([_ssd_pair(j, *pair_args(j))[0] for j in range(NH // 2)], axis=1)
        _, post_vjp = jax.vjp(_ssd_post, y, xs, z_ref[...], dsk_ref[...], nrm_ref[...])
        dy, dxs_skip, dz, ddsk, dnrm = post_vjp(dya_ref[...])
        dz_ref[...] = dz.astype(dz_ref.dtype)
        ddsk_ref[...] += ddsk
        dnrm_ref[...] += dnrm

        dprep = [jnp.zeros_like(p) for p in prep]
        dxp = []
        dbg = [jnp.zeros((CH, 128), F32), jnp.zeros((CH, 128), F32)]
        dcg = [jnp.zeros((CH, 128), F32), jnp.zeros((CH, 128), F32)]
        for j in range(NH // 2):
            g = j // 4
            _, pair_vjp = jax.vjp(functools.partial(_ssd_pair, j), *pair_args(j))
            cts = pair_vjp((dy[:, 128 * j:128 * j + 128], ds_ref[:, 128 * j:128 * j + 128]))
            dxp.append(cts[0])
            dbg[g] = dbg[g] + cts[1]
            dcg[g] = dcg[g] + cts[2]
            ds_ref[:, 128 * j:128 * j + 128] = cts[3]
            dprep = [a + b for a, b in zip(dprep, cts[4:])]
        ddtraw, ddtb, dalog = prep_vjp(tuple(dprep))
        ddtb_ref[...] += ddtb
        dalog_ref[...] += dalog

        dxs_pre = (dxs_skip + jnp.concatenate(dxp, axis=1)) * _dsilu(xs_pre)
        dbc_pre = jnp.concatenate([dbg[0], dbg[1], dcg[0], dcg[1]], axis=1) * _dsilu(bc_pre)
        dcbx_ref[...] += _rowsum(dxs_pre)
        dcbb_ref[...] += _rowsum(dbc_pre)
        for k in range(4):
            dcwx_ref[k:k + 1, :] += _rowsum(dxs_pre * taps_x[k])
            dcwb_ref[k:k + 1, :] += _rowsum(dbc_pre * taps_b[k])
        fx = _taps_future(dxs_pre, futx_ref[...])
        fb = _taps_future(dbc_pre, futb_ref[...])
        cwx = cwx_ref[...]
        cwb = cwb_ref[...]
        dxs_in = fx[0] * cwx[0:1, :]
        dbc_in = fb[0] * cwb[0:1, :]
        for k in range(1, 4):
            dxs_in = dxs_in + fx[k] * cwx[k:k + 1, :]
            dbc_in = dbc_in + fb[k] * cwb[k:k + 1, :]
        futx_ref[...] = dxs_pre[0:8, :]
        futb_ref[...] = dbc_pre[0:8, :]
        dxs_ref[...] = dxs_in.astype(dxs_ref.dtype)
        dbc_ref[...] = jnp.concatenate([dbc_in, ddtraw, jnp.zeros((CH, SEG - 640), F32)], axis=1).astype(dbc_ref.dtype)

    row_out = lambda: pl.BlockSpec((CH, D), lambda i: (rev(i), 0))
    outs = pl.pallas_call(
        body, name="ssd_bwd", grid=(n_c,),
        in_specs=[pl.BlockSpec((CH, D), lambda i: (rev(i), 0))] + _proj_specs(CH, (0, 5, 6), rev)
        + _prev8_specs(CH, (5, 6), rev) + [pl.BlockSpec((1, NS, D), lambda i: (rev(i), 0, 0)),
                                            _full((4, D)), _full((4, 512)), _full((1, D)), _full((1, 512)),
                                            _full((1, 128)), _full((1, 128)), _full((1, D)), _full((1, D))],
        out_specs=[row_out(), row_out(), row_out(), _full((4, D)), _full((4, 512)), _full((1, D)), _full((1, 512)),
                   _full((1, 128)), _full((1, 128)), _full((1, D)), _full((1, D))],
        out_shape=[jax.ShapeDtypeStruct((t_n, D), MXU)] * 3 + [
            jax.ShapeDtypeStruct(s, F32) for s in ((4, D), (4, 512), (1, D), (1, 512), (1, 128), (1, 128), (1, D), (1, D))],
        scratch_shapes=[pltpu.VMEM((NS, D), F32), pltpu.VMEM((8, D), F32), pltpu.VMEM((8, 512), F32)],
        compiler_params=_cp("arbitrary"),
    )(dya, proj, proj, proj, proj, proj, sprev, cwx, cwb, cbx, cbb, dtb, alog, dsk, nrm)
    return outs


LRU_ROWS = 256
LRU_BLK = 256


def _lru_gates(xr, wa, wx, ba, bx, lam):
    pr = jnp.concatenate([_nn(xr[:, LRU_BLK * b:LRU_BLK * (b + 1)], wa[b]) for b in range(D // LRU_BLK)], axis=1) + ba
    pi = jnp.concatenate([_nn(xr[:, LRU_BLK * b:LRU_BLK * (b + 1)], wx[b]) for b in range(D // LRU_BLK)], axis=1) + bx
    log_a = -LRU_C * jax.nn.sigmoid(pr) * _softplus(-lam)
    a = jnp.exp(log_a)
    mult = jnp.sqrt(1.0 - jnp.exp(2.0 * log_a))
    return a, mult * (jax.nn.sigmoid(pi) * xr)


def _lru_out(h, g):
    return h * jax.nn.gelu(g, approximate=True)


def _lru_fwd(proj, cw, cb, wa, wx, ba, bx, lam):
    t_n = proj.shape[0]
    rows = min(LRU_ROWS, t_n)
    fwd = lambda i: i

    def body(g_ref, x_ref, xp_ref, cw_ref, cb_ref, wa_ref, wx_ref, ba_ref, bx_ref, lam_ref, yb_ref, h_ref,
             a_s, u_s, carry):
        i = pl.program_id(0)

        @pl.when(i == 0)
        def _():
            carry[...] = jnp.zeros_like(carry)

        keep = jnp.where(i == 0, 0.0, 1.0)
        xr = _conv_apply(_taps_past(x_ref[...], xp_ref[...] * keep), cw_ref[...], cb_ref[...])
        a, u = _lru_gates(xr, wa_ref[...], wx_ref[...], ba_ref[...], bx_ref[...], lam_ref[...])
        a_s[...] = a
        u_s[...] = u
        row = lax.broadcasted_iota(jnp.int32, (8, D), 0)

        def blk(b, c):
            s = pl.multiple_of(b * 8, 8)
            av = a_s[pl.ds(s, 8), :]
            uv = u_s[pl.ds(s, 8), :]
            for d in (1, 2, 4):
                m = row >= d
                uv = uv + av * jnp.where(m, pltpu.roll(uv, d, 0), 0.0)
                av = av * jnp.where(m, pltpu.roll(av, d, 0), 1.0)
            hv = uv + av * c
            h_ref[pl.ds(s, 8), :] = hv
            return hv[7:8, :]

        carry[0:1, :] = lax.fori_loop(0, rows // 8, blk, carry[0:1, :])
        yb_ref[...] = _lru_out(h_ref[...], g_ref[...])

    return pl.pallas_call(
        body, name="lru_fwd", grid=(t_n // rows,),
        in_specs=_proj_specs(rows, (1, 2), fwd) + _prev8_specs(rows, (2,), fwd) + [
            _full((4, D)), _full((1, D)), _full((4, LRU_BLK, LRU_BLK)), _full((4, LRU_BLK, LRU_BLK)),
            _full((1, D)), _full((1, D)), _full((1, D))],
        out_specs=[pl.BlockSpec((rows, D), lambda i: (i, 0)), pl.BlockSpec((rows, D), lambda i: (i, 0))],
        out_shape=[jax.ShapeDtypeStruct((t_n, D), F32), jax.ShapeDtypeStruct((t_n, D), F32)],
        scratch_shapes=[pltpu.VMEM((rows, D), F32), pltpu.VMEM((rows, D), F32), pltpu.VMEM((8, D), F32)],
        compiler_params=_cp("arbitrary"),
    )(proj, proj, proj, cw, cb, wa, wx, ba, bx, lam)


def _lru_bwd(dyb, proj, h, cw, cb, wa, wx, ba, bx, lam):
    t_n = proj.shape[0]
    rows = min(LRU_ROWS, t_n)
    n_t = t_n // rows
    rev = lambda i: n_t - 1 - i
    rb = rows // 8

    def body(dyb_ref, g_ref, x_ref, xp_ref, h_ref, hp_ref, cw_ref, cb_ref, wa_ref, wx_ref, ba_ref, bx_ref, lam_ref,
             dg_ref, dx_ref, dcw_ref, dcb_ref, dwa_ref, dwx_ref, dba_ref, dbx_ref, dlam_ref,
             a_s, dh_s, hx_s, da_s, du_s, carry, fut):
        i = pl.program_id(0)
        acc_refs = (dcw_ref, dcb_ref, dwa_ref, dwx_ref, dba_ref, dbx_ref, dlam_ref)

        @pl.when(i == 0)
        def _():
            for r in (carry, fut) + acc_refs:
                r[...] = jnp.zeros_like(r)

        keep = jnp.where(i == n_t - 1, 0.0, 1.0)
        taps = _taps_past(x_ref[...], xp_ref[...] * keep)
        xr = _conv_apply(taps, cw_ref[...], cb_ref[...])
        gate_in = (xr, wa_ref[...], wx_ref[...], ba_ref[...], bx_ref[...], lam_ref[...])
        (a, _), gates_vjp = jax.vjp(_lru_gates, *gate_in)
        _, out_vjp = jax.vjp(_lru_out, h_ref[...], g_ref[...])
        dh, dg = out_vjp(dyb_ref[...])
        dg_ref[...] = dg.astype(dg_ref.dtype)
        a_s[...] = a
        dh_s[...] = dh
        hx_s[0:8, :] = hp_ref[...] * keep
        hx_s[8:, :] = h_ref[...]
        row = lax.broadcasted_iota(jnp.int32, (8, D), 0)

        def blk(b, c):
            s = pl.multiple_of((rb - 1 - b) * 8, 8)
            av = a_s[pl.ds(s, 8), :]
            dhv = dh_s[pl.ds(s, 8), :]
            a0 = av
            kv = av * dhv
            for d in (1, 2, 4):
                m = row <= 7 - d
                kv = kv + av * jnp.where(m, pltpu.roll(kv, 8 - d, 0), 0.0)
                av = av * jnp.where(m, pltpu.roll(av, 8 - d, 0), 1.0)
            kv = kv + av * c
            gv = dhv + jnp.where(row < 7, pltpu.roll(kv, 7, 0), c)
            hb = hx_s[pl.ds(s + 8, 8), :]
            hpv = hx_s[pl.ds(s, 8), :]
            hprev = jnp.where(row >= 1, pltpu.roll(hb, 1, 0), hpv[7:8, :])
            du_s[pl.ds(s, 8), :] = gv
            da_s[pl.ds(s, 8), :] = gv * hprev
            del a0
            return kv[0:1, :]

        carry[0:1, :] = lax.fori_loop(0, rb, blk, carry[0:1, :])
        dxr, dwa, dwx, dba, dbx, dlam = gates_vjp((da_s[...], du_s[...]))
        dwa_ref[...] += dwa
        dwx_ref[...] += dwx
        dba_ref[...] += dba
        dbx_ref[...] += dbx
        dlam_ref[...] += dlam
        dcb_ref[...] += _rowsum(dxr)
        for k in range(4):
            dcw_ref[k:k + 1, :] += _rowsum(dxr * taps[k])
        ft = _taps_future(dxr, fut[...])
        cwv = cw_ref[...]
        dx = ft[0] * cwv[0:1, :]
        for k in range(1, 4):
            dx = dx + ft[k] * cwv[k:k + 1, :]
        fut[...] = dxr[0:8, :]
        dx_ref[...] = dx.astype(dx_ref.dtype)

    row_in = lambda: pl.BlockSpec((rows, D), lambda i: (rev(i), 0))
    prev_h = pl.BlockSpec((8, D), lambda i: (jnp.maximum(rev(i) * rb - 1, 0), 0))
    wspec = lambda: _full((4, LRU_BLK, LRU_BLK))
    return pl.pallas_call(
        body, name="lru_bwd", grid=(n_t,),
        in_specs=[row_in()] + _proj_specs(rows, (1, 2), rev) + _prev8_specs(rows, (2,), rev) + [row_in(), prev_h] + [
            _full((4, D)), _full((1, D)), wspec(), wspec(), _full((1, D)), _full((1, D)), _full((1, D))],
        out_specs=[row_in(), row_in(), _full((4, D)), _full((1, D)), wspec(), wspec(), _full((1, D)), _full((1, D)),
                   _full((1, D))],
        out_shape=[jax.ShapeDtypeStruct((t_n, D), MXU)] * 2 + [
            jax.ShapeDtypeStruct(s, F32) for s in ((4, D), (1, D), (4, LRU_BLK, LRU_BLK), (4, LRU_BLK, LRU_BLK),
                                                   (1, D), (1, D), (1, D))],
        scratch_shapes=[pltpu.VMEM((rows, D), F32), pltpu.VMEM((rows, D), F32), pltpu.VMEM((rows + 8, D), F32),
                        pltpu.VMEM((rows, D), F32), pltpu.VMEM((rows, D), F32), pltpu.VMEM((8, D), F32),
                        pltpu.VMEM((8, D), F32)],
        compiler_params=_cp("arbitrary"),
    )(dyb, proj, proj, proj, h, h, cw, cb, wa, wx, ba, bx, lam)


def _merge_out(ya, yb, proj, x, wout, g1):
    t_n = x.shape[0]
    tm = min(t_n, 512)

    def body(ya_ref, yb_ref, ga_ref, gb_ref, x_ref, w_ref, g_ref, h1_ref, mix_ref, mg_ref):
        merged = jax.nn.sigmoid(ga_ref[...]) * ya_ref[...] + jax.nn.sigmoid(gb_ref[...]) * yb_ref[...]
        mg = merged.astype(MXU)
        mg_ref[...] = mg
        mix = jnp.dot(mg, w_ref[...], preferred_element_type=F32)
        mix_ref[...] = mix
        h1_ref[...] = x_ref[...] + _rms(mix, g_ref[...])

    row = lambda: pl.BlockSpec((tm, D), lambda i: (i, 0))
    return pl.pallas_call(
        body, name="merge_out", grid=(t_n // tm,),
        in_specs=[row(), row()] + _proj_specs(tm, (3, 4), lambda i: i) + [row(), _full((D, D)), _full((1, D))],
        out_specs=[row(), row(), row()],
        out_shape=[jax.ShapeDtypeStruct((t_n, D), F32), jax.ShapeDtypeStruct((t_n, D), F32),
                   jax.ShapeDtypeStruct((t_n, D), MXU)],
        compiler_params=_cp("parallel"),
    )(ya, yb, proj, proj, x, wout, g1)


def _out_bwd(dh1, mix, ya, yb, proj, wout, g1):
    t_n = dh1.shape[0]
    tm = min(t_n, 512)

    def body(dh1_ref, mix_ref, ya_ref, yb_ref, ga_ref, gb_ref, w_ref, g_ref,
             dmix_ref, dya_ref, dyb_ref, dga_ref, dgb_ref, dg1_ref):
        @pl.when(pl.program_id(0) == 0)
        def _():
            dg1_ref[...] = jnp.zeros_like(dg1_ref)

        dmix, dg_rows = _rms_bwd(mix_ref[...], g_ref[...], dh1_ref[...])
        dg1_ref[...] += _rowsum(dg_rows)
        dmix_b = dmix.astype(MXU)
        dmix_ref[...] = dmix_b
        dmg = lax.dot_general(dmix_b, w_ref[...], (((1,), (1,)), ((), ())), preferred_element_type=F32)
        sa = jax.nn.sigmoid(ga_ref[...])
        sb = jax.nn.sigmoid(gb_ref[...])
        dya_ref[...] = dmg * sa
        dyb_ref[...] = dmg * sb
        dga_ref[...] = (dmg * ya_ref[...] * sa * (1.0 - sa)).astype(MXU)
        dgb_ref[...] = (dmg * yb_ref[...] * sb * (1.0 - sb)).astype(MXU)

    row = lambda: pl.BlockSpec((tm, D), lambda i: (i, 0))
    return pl.pallas_call(
        body, name="out_bwd", grid=(t_n // tm,),
        in_specs=[row(), row(), row(), row()] + _proj_specs(tm, (3, 4), lambda i: i) + [_full((D, D)), _full((1, D))],
        out_specs=[row(), row(), row(), row(), row(), _full((1, D))],
        out_shape=[jax.ShapeDtypeStruct((t_n, D), MXU), jax.ShapeDtypeStruct((t_n, D), F32),
                   jax.ShapeDtypeStruct((t_n, D), F32), jax.ShapeDtypeStruct((t_n, D), MXU),
                   jax.ShapeDtypeStruct((t_n, D), MXU), jax.ShapeDtypeStruct((1, D), F32)],
        compiler_params=_cp("arbitrary"),
    )(dh1, mix, ya, yb, proj, proj, wout, g1)


MLP_TF = 512


def _mlp_fwd(h1, g2, wup, wdown, g3, tgt):
    t_n = h1.shape[0]
    tm = min(t_n, 512)
    n_f = FF // MLP_TF

    def body(h1_ref, g2_ref, wu_ref, wd_ref, g3_ref, tgt_ref, hp_ref, v_ref, ff_ref, loss_ref, acc):
        i, j = pl.program_id(0), pl.program_id(1)

        @pl.when((i == 0) & (j == 0))
        def _():
            loss_ref[...] = jnp.zeros_like(loss_ref)

        @pl.when(j == 0)
        def _():
            v_ref[...] = _rms(h1_ref[...], g2_ref[...]).astype(MXU)
            acc[...] = jnp.zeros_like(acc)

        hp = jnp.dot(v_ref[...], wu_ref[...], preferred_element_type=F32)
        hp_ref[...] = hp
        hid = jnp.square(jnp.maximum(hp, 0.0))
        acc[...] += jnp.dot(hid.astype(MXU), wd_ref[...], preferred_element_type=F32)

        @pl.when(j == n_f - 1)
        def _():
            ff = acc[...]
            ff_ref[...] = ff
            err = h1_ref[...] + _rms(ff, g3_ref[...]) - tgt_ref[...]
            loss_ref[...] += 0.5 * jnp.sum(jnp.mean(err * err, axis=-1, keepdims=True), axis=0, keepdims=True)

    row = lambda: pl.BlockSpec((tm, D), lambda i, j: (i, 0))
    vec = lambda: pl.BlockSpec((1, D), lambda i, j: (0, 0))
    return pl.pallas_call(
        body, name="mlp_fwd", grid=(t_n // tm, n_f),
        in_specs=[row(), vec(), pl.BlockSpec((D, MLP_TF), lambda i, j: (0, j)),
                  pl.BlockSpec((MLP_TF, D), lambda i, j: (j, 0)), vec(), row()],
        out_specs=[pl.BlockSpec((tm, MLP_TF), lambda i, j: (i, j)), row(), row(),
                   pl.BlockSpec((1, 1), lambda i, j: (0, 0))],
        out_shape=[jax.ShapeDtypeStruct((t_n, FF), F32), jax.ShapeDtypeStruct((t_n, D), MXU),
                   jax.ShapeDtypeStruct((t_n, D), F32), jax.ShapeDtypeStruct((1, 1), F32)],
        scratch_shapes=[pltpu.VMEM((tm, D), F32)],
        compiler_params=_cp("arbitrary", "arbitrary"),
    )(h1, g2, wup, wdown, g3, tgt)


def _mlp_bwd(h1, ff, tgt, hp, wup, wdown, g2, g3):
    t_n = h1.shape[0]
    tm = min(t_n, 512)
    n_f = FF // MLP_TF

    def body(h1_ref, ff_ref, tgt_ref, hp_ref, wu_ref, wd_ref, g2_ref, g3_ref,
             dh1_ref, dff_ref, dhp_ref, hid_ref, dg2_ref, dg3_ref, acc, dout_s):
        i, j = pl.program_id(0), pl.program_id(1)

        @pl.when((i == 0) & (j == 0))
        def _():
            dg2_ref[...] = jnp.zeros_like(dg2_ref)
            dg3_ref[...] = jnp.zeros_like(dg3_ref)

        @pl.when(j == 0)
        def _():
            ff = ff_ref[...]
            dout = (h1_ref[...] + _rms(ff, g3_ref[...]) - tgt_ref[...]) * (1.0 / D)
            dout_s[...] = dout
            dff, dg_rows = _rms_bwd(ff, g3_ref[...], dout)
            dg3_ref[...] += _rowsum(dg_rows)
            dff_ref[...] = dff.astype(MXU)
            acc[...] = jnp.zeros_like(acc)

        hp = hp_ref[...]
        relu = jnp.maximum(hp, 0.0)
        hid_ref[...] = jnp.square(relu).astype(MXU)
        dhid = lax.dot_general(dff_ref[...], wd_ref[...], (((1,), (1,)), ((), ())), preferred_element_type=F32)
        dhp = (dhid * (2.0 * relu)).astype(MXU)
        dhp_ref[...] = dhp
        acc[...] += lax.dot_general(dhp, wu_ref[...], (((1,), (1,)), ((), ())), preferred_element_type=F32)

        @pl.when(j == n_f - 1)
        def _():
            dv, dg_rows = _rms_bwd(h1_ref[...], g2_ref[...], acc[...])
            dg2_ref[...] += _rowsum(dg_rows)
            dh1_ref[...] = dout_s[...] + dv

    row = lambda: pl.BlockSpec((tm, D), lambda i, j: (i, 0))
    vec = lambda: pl.BlockSpec((1, D), lambda i, j: (0, 0))
    blk = lambda: pl.BlockSpec((tm, MLP_TF), lambda i, j: (i, j))
    return pl.pallas_call(
        body, name="mlp_bwd", grid=(t_n // tm, n_f),
        in_specs=[row(), row(), row(), blk(), pl.BlockSpec((D, MLP_TF), lambda i, j: (0, j)),
                  pl.BlockSpec((MLP_TF, D), lambda i, j: (j, 0)), vec(), vec()],
        out_specs=[row(), row(), blk(), blk(), vec(), vec()],
        out_shape=[jax.ShapeDtypeStruct((t_n, D), F32), jax.ShapeDtypeStruct((t_n, D), MXU),
                   jax.ShapeDtypeStruct((t_n, FF), MXU), jax.ShapeDtypeStruct((t_n, FF), MXU),
                   jax.ShapeDtypeStruct((1, D), F32), jax.ShapeDtypeStruct((1, D), F32)],
        scratch_shapes=[pltpu.VMEM((tm, D), F32), pltpu.VMEM((tm, D), F32)],
        compiler_params=_cp("arbitrary", "arbitrary"),
    )(h1, ff, tgt, hp, wup, wdown, g2, g3)


def _wgrad(a, g, name):
    t_n, k_n = a.shape
    n_n = g.shape[1]
    tt = min(t_n, 1024)
    tk, tn = min(k_n, 1024), min(n_n, 1024)

    def body(a_ref, g_ref, o_ref):
        @pl.when(pl.program_id(2) == 0)
        def _():
            o_ref[...] = jnp.zeros_like(o_ref)

        o_ref[...] += lax.dot_general(a_ref[...], g_ref[...], (((0,), (0,)), ((), ())), preferred_element_type=F32)

    return pl.pallas_call(
        body, name=name, grid=(k_n // tk, n_n // tn, t_n // tt),
        in_specs=[pl.BlockSpec((tt, tk), lambda k, n, t: (t, k)), pl.BlockSpec((tt, tn), lambda k, n, t: (t, n))],
        out_specs=pl.BlockSpec((tk, tn), lambda k, n, t: (k, n)),
        out_shape=jax.ShapeDtypeStruct((k_n, n_n), F32),
        compiler_params=_cp("parallel", "parallel", "arbitrary"),
    )(a, g)


def _inproj_bwd(dproj, w, x, g0, dh1):
    t_n = x.shape[0]
    tm = min(t_n, 1024)
    n_k = NP // SEG

    def body(dp_ref, w_ref, x_ref, g_ref, dh1_ref, dx_ref, dg0_ref, acc):
        i, k = pl.program_id(0), pl.program_id(1)

        @pl.when((i == 0) & (k == 0))
        def _():
            dg0_ref[...] = jnp.zeros_like(dg0_ref)

        @pl.when(k == 0)
        def _():
            acc[...] = jnp.zeros_like(acc)

        acc[...] += lax.dot_general(dp_ref[...], w_ref[...], (((1,), (1,)), ((), ())), preferred_element_type=F32)

        @pl.when(k == n_k - 1)
        def _():
            dx, dg_rows = _rms_bwd(x_ref[...], g_ref[...], acc[...])
            dg0_ref[...] += _rowsum(dg_rows)
            dx_ref[...] = dh1_ref[...] + dx

    row = lambda: pl.BlockSpec((tm, D), lambda i, k: (i, 0))
    vec = lambda: pl.BlockSpec((1, D), lambda i, k: (0, 0))
    return pl.pallas_call(
        body, name="inproj_bwd", grid=(t_n // tm, n_k),
        in_specs=[pl.BlockSpec((tm, SEG), lambda i, k: (i, k)), pl.BlockSpec((D, SEG), lambda i, k: (0, k)),
                  row(), vec(), row()],
        out_specs=[row(), vec()],
        out_shape=[jax.ShapeDtypeStruct((t_n, D), F32), jax.ShapeDtypeStruct((1, D), F32)],
        scratch_shapes=[pltpu.VMEM((tm, D), F32)],
        compiler_params=_cp("arbitrary", "arbitrary"),
    )(dproj, w, x, g0, dh1)


def _blockdiag4(w):
    z = jnp.zeros((4, 4, 64, 4, 64), w.dtype)
    w4 = w.reshape(4, 4, 64, 64)
    for q in range(4):
        z = z.at[:, q, :, q, :].set(w4[:, q])
    return z.reshape(4, 256, 256).astype(MXU)


def _blockdiag4_extract(g):
    g5 = g.reshape(4, 4, 64, 4, 64)
    return jnp.stack([g5[:, q, :, q, :] for q in range(4)], axis=1).reshape(NH, 64, 64)


def _local_step(x, tgt, p):
    f = lambda a: a.astype(F32)
    proj, u = _inproj(x, p["norm_mix_pre"], p["w_in_p"])
    ssm_params = (p["cw_xs"], p["cw_bc"], p["cb_xs"], p["cb_bc"], p["dt_bias"], p["a_log"], p["d_skip_x"],
                  p["ssm_norm"])
    ya, sprev = _ssd_fwd(proj, *ssm_params)
    lru_params = (p["conv_lru_w"], p["conv_lru_b"], p["wa_bd"], p["wx_bd"], p["lru_ba"], p["lru_bx"],
                  p["lru_lambda"])
    yb, h = _lru_fwd(proj, *lru_params)
    h1, mix, merged = _merge_out(ya, yb, proj, x, p["w_out"], p["norm_mix_post"])
    hp, v, ff, loss = _mlp_fwd(h1, p["norm_mlp_pre"], p["w_up"], p["w_down"], p["norm_mlp_post"], tgt)

    dh1, dff, dhp, hid, dg2, dg3 = _mlp_bwd(h1, ff, tgt, hp, p["w_up"], p["w_down"], p["norm_mlp_pre"],
                                            p["norm_mlp_post"])
    d_w_down = _wgrad(hid, dff, "wgrad_down")
    d_w_up = _wgrad(v, dhp, "wgrad_up")
    dmix, dya, dyb, dga, dgb, dg1 = _out_bwd(dh1, mix, ya, yb, proj, p["w_out"], p["norm_mix_post"])
    d_w_out = _wgrad(merged, dmix, "wgrad_out")
    (dz, dxs, dbc, dcwx, dcwb, dcbx, dcbb, ddtb, dalog, ddsk, dnrm) = _ssd_bwd(dya, proj, sprev, *ssm_params)
    (dgl, dxl, dcwl, dcbl, dwa, dwx, dba, dbx, dlam) = _lru_bwd(dyb, proj, h, *lru_params)
    dproj = jnp.concatenate([dz, dgl, dxl, dga, dgb, dxs, dbc], axis=1)
    grad_x, dg0 = _inproj_bwd(dproj, p["w_in_p"], x, p["norm_mix_pre"], dh1)
    d_w_in_p = _wgrad(u, dproj, "wgrad_in")
    grads = dict(
        norm_mix_pre=dg0, w_in_p=d_w_in_p, conv_ssm_w=jnp.concatenate([dcwx, dcwb], axis=1),
        conv_ssm_b=jnp.concatenate([dcbx, dcbb], axis=1), dt_bias=ddtb[:, :NH], a_log=dalog[:, :NH],
        d_skip=f(ddsk).reshape(NH, 64).sum(axis=1)[None, :], ssm_norm=dnrm, conv_lru_w=dcwl, conv_lru_b=dcbl,
        lru_wa=_blockdiag4_extract(dwa), lru_ba=dba, lru_wx=_blockdiag4_extract(dwx), lru_bx=dbx, lru_lambda=dlam,
        w_out=d_w_out, norm_mix_post=dg1, norm_mlp_pre=dg2, w_up=d_w_up, w_down=d_w_down, norm_mlp_post=dg3)
    return loss[0, 0], grad_x, grads


W_IN_COLS = 6672


def _w_in_to_padded(w):
    z, xs, bc, dt = w[:, 0:1024], w[:, 1024:2048], w[:, 2048:2560], w[:, 2560:2576]
    gl, xl, ga, gb = w[:, 2576:3600], w[:, 3600:4624], w[:, 4624:5648], w[:, 5648:6672]
    return jnp.concatenate([z, gl, xl, ga, gb, xs, bc, dt, jnp.zeros((w.shape[0], NP - 6672), w.dtype)], axis=1)


def _w_in_from_padded(wp):
    z, gl, xl, ga, gb = (wp[:, SEG * s:SEG * (s + 1)] for s in range(5))
    xs, bc, dt = wp[:, 5120:6144], wp[:, 6144:6656], wp[:, 6656:6672]
    return jnp.concatenate([z, xs, bc, dt, gl, xl, ga, gb], axis=1)


def _prep_params(full):
    f = lambda a: a.astype(F32)
    pad128 = lambda a: jnp.pad(f(a).reshape(1, -1), ((0, 0), (0, 128 - a.size)))
    cw = f(full["conv_ssm_w"])
    cb = f(full["conv_ssm_b"]).reshape(1, -1)
    return dict(
        norm_mix_pre=f(full["norm_mix_pre"]).reshape(1, D), w_in_p=_w_in_to_padded(full["w_in"].astype(MXU)),
        cw_xs=cw[:, :D], cw_bc=cw[:, D:], cb_xs=cb[:, :D], cb_bc=cb[:, D:],
        dt_bias=pad128(full["dt_bias"]), a_log=pad128(full["a_log"]),
        d_skip_x=jnp.repeat(f(full["d_skip"]).reshape(-1), 64).reshape(1, D), ssm_norm=f(full["ssm_norm"]).reshape(1, D),
        conv_lru_w=f(full["conv_lru_w"]), conv_lru_b=f(full["conv_lru_b"]).reshape(1, D),
        wa_bd=_blockdiag4(full["lru_wa"]), wx_bd=_blockdiag4(full["lru_wx"]),
        lru_ba=f(full["lru_ba"]).reshape(1, D), lru_bx=f(full["lru_bx"]).reshape(1, D),
        lru_lambda=f(full["lru_lambda"]).reshape(1, D), w_out=full["w_out"].astype(MXU),
        norm_mix_post=f(full["norm_mix_post"]).reshape(1, D), norm_mlp_pre=f(full["norm_mlp_pre"]).reshape(1, D),
        w_up=full["w_up"].astype(MXU), w_down=full["w_down"].astype(MXU),
        norm_mlp_post=f(full["norm_mlp_post"]).reshape(1, D))


def _grads_to_reference_layout(g):
    out = dict(g)
    out["w_in"] = _w_in_from_padded(out.pop("w_in_p"))
    return out


MESH_ID = pl.DeviceIdType.MESH
ANY = pl.BlockSpec(memory_space=pl.ANY)


def _my_place():
    x, y, c = lax.axis_index("x"), lax.axis_index("y"), lax.axis_index("c")
    return x, y, c, 4 * x + 2 * y + c


def _peer(x, y, c, k):
    return (x ^ ((k >> 2) & 1), y ^ ((k >> 1) & 1), c ^ (k & 1))


def _all_gather(pack):
    def body(in_ref, out_ref, send_sems, recv_sems, local_sem):
        x, y, c, me = _my_place()
        mine = pltpu.make_async_copy(in_ref, out_ref.at[me], local_sem)
        mine.start()
        copies = []
        for k in range(1, N_DEV):
            cp = pltpu.make_async_remote_copy(
                src_ref=in_ref, dst_ref=out_ref.at[me], send_sem=send_sems.at[k - 1], recv_sem=recv_sems.at[k - 1],
                device_id=_peer(x, y, c, k), device_id_type=MESH_ID)
            cp.start()
            copies.append(cp)
        for cp in copies:
            cp.wait()
        mine.wait()

    return pl.pallas_call(
        body, name="weight_all_gather", in_specs=[ANY], out_specs=ANY,
        out_shape=jax.ShapeDtypeStruct((N_DEV,) + pack.shape, pack.dtype),
        scratch_shapes=[pltpu.SemaphoreType.DMA((N_DEV - 1,)), pltpu.SemaphoreType.DMA((N_DEV - 1,)),
                        pltpu.SemaphoreType.DMA],
    )(pack)


def _grad_exchange(big, small):
    def body(big_ref, small_ref, bout_ref, sout_ref, send_sems, recv_sems, local_sems):
        x, y, c, me = _my_place()
        own_b = pltpu.make_async_copy(big_ref.at[me], bout_ref.at[me], local_sems.at[0])
        own_s = pltpu.make_async_copy(small_ref, sout_ref.at[me], local_sems.at[1])
        own_b.start()
        own_s.start()
        copies = []
        for k in range(1, N_DEV):
            peer = _peer(x, y, c, k)
            cb = pltpu.make_async_remote_copy(
                src_ref=big_ref.at[me ^ k], dst_ref=bout_ref.at[me], send_sem=send_sems.at[2 * k - 2],
                recv_sem=recv_sems.at[2 * k - 2], device_id=peer, device_id_type=MESH_ID)
            cs = pltpu.make_async_remote_copy(
                src_ref=small_ref, dst_ref=sout_ref.at[me], send_sem=send_sems.at[2 * k - 1],
                recv_sem=recv_sems.at[2 * k - 1], device_id=peer, device_id_type=MESH_ID)
            cb.start()
            cs.start()
            copies += [cb, cs]
        for cp in copies:
            cp.wait()
        own_b.wait()
        own_s.wait()

    n_sem = 2 * (N_DEV - 1)
    return pl.pallas_call(
        body, name="grad_exchange", in_specs=[ANY, ANY], out_specs=[ANY, ANY],
        out_shape=[jax.ShapeDtypeStruct(big.shape, big.dtype), jax.ShapeDtypeStruct((N_DEV,) + small.shape, small.dtype)],
        scratch_shapes=[pltpu.SemaphoreType.DMA((n_sem,)), pltpu.SemaphoreType.DMA((n_sem,)),
                        pltpu.SemaphoreType.DMA((2,))],
    )(big, small)


def _slot_sum(parts):
    r_n = parts.shape[1]
    tr = max(t for t in range(16, 6001, 16) if r_n % t == 0)

    def body(p_ref, o_ref):
        acc = p_ref[0].astype(F32)
        for k in range(1, N_DEV):
            acc = acc + p_ref[k].astype(F32)
        o_ref[...] = acc

    return pl.pallas_call(
        body, name="slot_sum_%d" % r_n, grid=(r_n // tr,),
        in_specs=[pl.BlockSpec((N_DEV, tr, 128), lambda i: (0, i, 0))],
        out_specs=pl.BlockSpec((tr, 128), lambda i: (i, 0)),
        out_shape=jax.ShapeDtypeStruct((r_n, 128), F32),
        compiler_params=_cp("parallel"),
    )(parts)


def _adam_math(w, g, m, v):
    m = ADAM_B1 * m + (1.0 - ADAM_B1) * g
    v = ADAM_B2 * v + (1.0 - ADAM_B2) * jnp.square(g)
    m_hat = m / (1.0 - ADAM_B1 ** ADAM_STEP)
    v_hat = v / (1.0 - ADAM_B2 ** ADAM_STEP)
    return -ADAM_LR * (m_hat / (jnp.sqrt(v_hat) + ADAM_EPS) + ADAM_WD * w), m, v


def _adam_big(w, g, m, v, name):
    r_n, c_n = w.shape
    tr = min(r_n, 256)

    def body(w_ref, g_ref, m_ref, v_ref, d_ref, mo_ref, vo_ref):
        d_ref[...], mo_ref[...], vo_ref[...] = _adam_math(w_ref[...], g_ref[...], m_ref[...], v_ref[...])

    blk = lambda: pl.BlockSpec((tr, c_n), lambda i: (i, 0))
    return pl.pallas_call(
        body, name=name, grid=(r_n // tr,), in_specs=[blk(), blk(), blk(), blk()], out_specs=[blk(), blk(), blk()],
        out_shape=[jax.ShapeDtypeStruct(w.shape, F32)] * 3, compiler_params=_cp("parallel"),
    )(w, g, m, v)


def _adam_small(groups, where, wmv):
    n, n_g = len(wmv), len(groups)

    def body(*refs):
        g_refs = refs[:n_g]
        w_refs = refs[n_g:n_g + 3 * n]
        o_refs = refs[n_g + 3 * n:]
        for q in range(n):
            w_ref, m_ref, v_ref = w_refs[3 * q:3 * q + 3]
            r, c = w_ref.shape
            gi, r0 = where[q]
            g = g_refs[gi][0, r0:r0 + r, 0:c]
            for k in range(1, N_DEV):
                g = g + g_refs[gi][k, r0:r0 + r, 0:c]
            d, m, v = _adam_math(w_ref[...], g, m_ref[...], v_ref[...])
            o_refs[4 * q][...] = g
            o_refs[4 * q + 1][...] = d
            o_refs[4 * q + 2][...] = m
            o_refs[4 * q + 3][...] = v

    flat_wmv = [a for t in wmv for a in t]
    vm = pl.BlockSpec(memory_space=pltpu.VMEM)
    outs = pl.pallas_call(
        body, name="adam_small", in_specs=[vm] * (n_g + 3 * n), out_specs=[vm] * (4 * n),
        out_shape=[jax.ShapeDtypeStruct(t[0].shape, F32) for t in wmv for _ in range(4)],
        compiler_params=pltpu.CompilerParams(vmem_limit_bytes=VMEM_LIMIT),
    )(*groups, *flat_wmv)
    return [tuple(outs[4 * q:4 * q + 4]) for q in range(n)]


WEIGHTS = ["norm_mix_pre", "w_in", "conv_ssm_w", "conv_ssm_b", "dt_bias", "a_log", "d_skip", "ssm_norm", "conv_lru_w",
           "conv_lru_b", "lru_wa", "lru_ba", "lru_wx", "lru_bx", "lru_lambda", "w_out", "norm_mix_post", "norm_mlp_pre",
           "w_up", "w_down", "norm_mlp_post"]
BIG = ["w_in", "w_out", "w_up", "w_down"]
BIG_SHARD = {"w_in": (D, W_IN_COLS // N_DEV), "w_out": (D // N_DEV, D), "w_up": (D, FF // N_DEV), "w_down": (FF // N_DEV, D)}
CONV_SSM_COLS, CONV_LRU_COLS = 1536 // N_DEV, D // N_DEV
SMALL = [("norm_mix_pre", (1, D), 0, 0), ("ssm_norm", (1, D), 0, 1), ("conv_lru_b", (1, D), 0, 2),
         ("lru_lambda", (1, D), 0, 3), ("norm_mix_post", (1, D), 0, 4), ("norm_mlp_pre", (1, D), 0, 5),
         ("norm_mlp_post", (1, D), 0, 6), ("conv_ssm_b", (1, 1536), 1, 0), ("dt_bias", (1, NH), 2, 0),
         ("a_log", (1, NH), 2, 1), ("d_skip", (1, NH), 2, 2), ("conv_ssm_w", (4, CONV_SSM_COLS), 3, 0),
         ("conv_lru_w", (4, CONV_LRU_COLS), 4, 0), ("lru_wa", (D, 64), 5, 0), ("lru_wx", (D, 64), 5, D),
         ("lru_ba", (NH, 64), 6, 0), ("lru_bx", (NH, 64), 6, NH)]
SMALL_GROUPS = [(8, D), (1, 1536), (8, 128), (4, 1536), (4, D), (2 * D, 64), (2 * NH, 64)]


def _pad_rows(flat, mult):
    n = flat.shape[0]
    rows = -(-n // (128 * mult)) * mult
    return jnp.pad(flat, (0, rows * 128 - n)).reshape(rows, 128)


def _split3(a):
    hi = a.astype(MXU)
    r1 = a - hi.astype(F32)
    mid = r1.astype(MXU)
    lo = (r1 - mid.astype(F32)).astype(MXU)
    return jnp.stack([hi, mid, lo])


def _weights_pack(a):
    parts = [a[n][0].astype(MXU).reshape(-1) for n in BIG]
    parts += [_split3(a["conv_ssm_w"][0]).reshape(-1), _split3(a["conv_lru_w"][0]).reshape(-1)]
    return _pad_rows(jnp.concatenate(parts), 16)


def _weights_unpack(g):
    flat = g.reshape(N_DEV, -1)
    off = 0
    out = {}
    for n in BIG:
        r, c = BIG_SHARD[n]
        blk = flat[:, off:off + r * c].reshape(N_DEV, r, c)
        off += r * c
        out[n] = blk.reshape(N_DEV * r, c) if n in ("w_out", "w_down") else blk.transpose(1, 0, 2).reshape(r, N_DEV * c)
    for n, c in (("conv_ssm_w", CONV_SSM_COLS), ("conv_lru_w", CONV_LRU_COLS)):
        s = flat[:, off:off + 12 * c].reshape(N_DEV, 3, 4, c).astype(F32)
        off += 12 * c
        out[n] = ((s[:, 0] + s[:, 1]) + s[:, 2]).transpose(1, 0, 2).reshape(4, N_DEV * c)
    return out


def kernel(x, norm_mix_pre, w_in, conv_ssm_w, conv_ssm_b, dt_bias, a_log, d_skip, ssm_norm, conv_lru_w, conv_lru_b, lru_wa, lru_ba, lru_wx, lru_bx, lru_lambda, w_out, norm_mix_post, norm_mlp_pre, w_up, w_down, norm_mlp_post, loss_target, m_norm_mix_pre, m_w_in, m_conv_ssm_w, m_conv_ssm_b, m_dt_bias, m_a_log, m_d_skip, m_ssm_norm, m_conv_lru_w, m_conv_lru_b, m_lru_wa, m_lru_ba, m_lru_wx, m_lru_bx, m_lru_lambda, m_w_out, m_norm_mix_post, m_norm_mlp_pre, m_w_up, m_w_down, m_norm_mlp_post, v_norm_mix_pre, v_w_in, v_conv_ssm_w, v_conv_ssm_b, v_dt_bias, v_a_log, v_d_skip, v_ssm_norm, v_conv_lru_w, v_conv_lru_b, v_lru_wa, v_lru_ba, v_lru_wx, v_lru_bx, v_lru_lambda, v_w_out, v_norm_mix_post, v_norm_mlp_pre, v_w_up, v_w_down, v_norm_mlp_post):
    vals = (norm_mix_pre, w_in, conv_ssm_w, conv_ssm_b, dt_bias, a_log, d_skip, ssm_norm, conv_lru_w, conv_lru_b, lru_wa, lru_ba, lru_wx, lru_bx, lru_lambda, w_out, norm_mix_post, norm_mlp_pre, w_up, w_down, norm_mlp_post)
    m_vals = (m_norm_mix_pre, m_w_in, m_conv_ssm_w, m_conv_ssm_b, m_dt_bias, m_a_log, m_d_skip, m_ssm_norm, m_conv_lru_w, m_conv_lru_b, m_lru_wa, m_lru_ba, m_lru_wx, m_lru_bx, m_lru_lambda, m_w_out, m_norm_mix_post, m_norm_mlp_pre, m_w_up, m_w_down, m_norm_mlp_post)
    v_vals = (v_norm_mix_pre, v_w_in, v_conv_ssm_w, v_conv_ssm_b, v_dt_bias, v_a_log, v_d_skip, v_ssm_norm, v_conv_lru_w, v_conv_lru_b, v_lru_wa, v_lru_ba, v_lru_wx, v_lru_bx, v_lru_lambda, v_w_out, v_norm_mix_post, v_norm_mlp_pre, v_w_up, v_w_down, v_norm_mlp_post)
    w = dict(zip(WEIGHTS, vals))
    m = dict(zip(WEIGHTS, m_vals))
    v = dict(zip(WEIGHTS, v_vals))
    me = 4 * lax.axis_index("x") + 2 * lax.axis_index("y") + lax.axis_index("c")

    gathered = _weights_unpack(_all_gather(_weights_pack(w)))
    full = {n: (gathered[n] if n in gathered else w[n][0]) for n in WEIGHTS}

    loss, grad_x, g = _local_step(x[0], loss_target[0], _prep_params(full))
    loss = lax.psum(loss, ("x", "y", "c"))

    g_w_in = _w_in_from_padded(g["w_in_p"])
    big = jnp.concatenate([
        g_w_in.reshape(D, N_DEV, -1).transpose(1, 0, 2).reshape(N_DEV, -1), g["w_out"].reshape(N_DEV, -1),
        g["w_up"].reshape(D, N_DEV, -1).transpose(1, 0, 2).reshape(N_DEV, -1), g["w_down"].reshape(N_DEV, -1)], axis=1)
    big = big.astype(MXU).reshape(N_DEV, -1, 128)
    zrow = jnp.zeros((1, D), F32)
    pad16 = lambda a: jnp.pad(a, ((0, 0), (0, 128 - NH)))
    small_parts = [
        jnp.concatenate([g["norm_mix_pre"], g["ssm_norm"], g["conv_lru_b"], g["lru_lambda"], g["norm_mix_post"],
                         g["norm_mlp_pre"], g["norm_mlp_post"], zrow], axis=0),
        g["conv_ssm_b"],
        jnp.concatenate([pad16(g["dt_bias"]), pad16(g["a_log"]), pad16(g["d_skip"]), jnp.zeros((5, 128), F32)], axis=0),
        g["conv_ssm_w"], g["conv_lru_w"],
        jnp.concatenate([g["lru_wa"].reshape(D, 64), g["lru_wx"].reshape(D, 64)], axis=0),
        jnp.concatenate([g["lru_ba"].reshape(NH, 64), g["lru_bx"].reshape(NH, 64)], axis=0)]
    small = _pad_rows(jnp.concatenate([s.reshape(-1) for s in small_parts]), 8)
    big_parts, small_all = _grad_exchange(big, small)

    g_big = _slot_sum(big_parts).reshape(-1)
    out_g, out_d, out_m, out_v = {}, {}, {}, {}
    off = 0
    for n in BIG:
        r, c = BIG_SHARD[n]
        gn = g_big[off:off + r * c].reshape(r, c)
        off += r * c
        d_n, m_n, v_n = _adam_big(w[n][0], gn, m[n][0], v[n][0], "adam_" + n)
        out_g[n], out_d[n], out_m[n], out_v[n] = gn[None], d_n[None], m_n[None], v_n[None]
    sflat = small_all.reshape(N_DEV, -1)
    groups = []
    off = 0
    for r, c in SMALL_GROUPS:
        groups.append(sflat[:, off:off + r * c].reshape(N_DEV, r, c))
        off += r * c
    groups[3] = lax.dynamic_slice_in_dim(groups[3], me * CONV_SSM_COLS, CONV_SSM_COLS, axis=2)
    groups[4] = lax.dynamic_slice_in_dim(groups[4], me * CONV_LRU_COLS, CONV_LRU_COLS, axis=2)
    wmv = [(w[n].reshape(s), m[n].reshape(s), v[n].reshape(s)) for n, s, _, _ in SMALL]
    res = _adam_small(groups, [(gi, r0) for _, _, gi, r0 in SMALL], wmv)
    for (n, _, _, _), (g_n, d_n, m_n, v_n) in zip(SMALL, res):
        shape = w[n].shape
        out_g[n], out_d[n], out_m[n], out_v[n] = (g_n.reshape(shape), d_n.reshape(shape), m_n.reshape(shape),
                                                  v_n.reshape(shape))
    return (loss, grad_x[None], *[out_g[n] for n in WEIGHTS], *[out_d[n] for n in WEIGHTS],
            *[out_m[n] for n in WEIGHTS], *[out_v[n] for n in WEIGHTS])
```

```python
import functools

import jax
import jax.numpy as jnp
from jax import lax
from jax.experimental import pallas as pl
from jax.experimental.pallas import tpu as pltpu

F32 = jnp.float32
MXU = jnp.bfloat16
HI = lax.Precision.HIGHEST
EPS = 1e-6

D = 1024
NH = 16
NS = 128
CH = 128
FF = 4096
NP = 7168
SEG = 1024
LRU_C = 8.0
N_DEV = 8

ADAM_LR, ADAM_B1, ADAM_B2, ADAM_EPS, ADAM_WD, ADAM_STEP = 0.001, 0.9, 0.999, 1e-08, 0.01, 10

VMEM_LIMIT = 56 * 1024 * 1024


def _cp(*sem):
    return pltpu.CompilerParams(dimension_semantics=sem, vmem_limit_bytes=VMEM_LIMIT)


def _nn(a, b):
    return jnp.dot(a.astype(MXU), b.astype(MXU), preferred_element_type=F32)


def _nt(a, b):
    return lax.dot_general(a.astype(MXU), b.astype(MXU), (((1,), (1,)), ((), ())), preferred_element_type=F32)


def _tn(a, b):
    return lax.dot_general(a.astype(MXU), b.astype(MXU), (((0,), (0,)), ((), ())), preferred_element_type=F32)


def _silu(x):
    return x * jax.nn.sigmoid(x)


def _dsilu(x):
    s = jax.nn.sigmoid(x)
    return s + x * s * (1.0 - s)


def _softplus(x):
    return jnp.maximum(x, 0.0) + jnp.log(1.0 + jnp.exp(-jnp.abs(x)))


def _rms(x, g):
    r = lax.rsqrt(jnp.mean(x * x, axis=-1, keepdims=True) + EPS)
    return x * r * g


def _rms_bwd(x, g, dy):
    r = lax.rsqrt(jnp.mean(x * x, axis=-1, keepdims=True) + EPS)
    gdy = g * dy
    dx = r * gdy - x * (r * r * r) * jnp.mean(x * gdy, axis=-1, keepdims=True)
    return dx, dy * x * r


def _rowsum(x):
    return jnp.sum(x, axis=0, keepdims=True)


def _taps_past(cur, prev8):
    r_n, c_n = cur.shape
    row = lax.broadcasted_iota(jnp.int32, (r_n, c_n), 0)
    out = []
    for k in range(4):
        s = 3 - k
        if s == 0:
            out.append(cur)
            continue
        head = jnp.concatenate([pltpu.roll(prev8, s, 0), jnp.zeros((r_n - 8, c_n), F32)], axis=0)
        out.append(jnp.where(row < s, head, pltpu.roll(cur, s, 0)))
    return out


def _taps_future(cur, fut8):
    r_n, c_n = cur.shape
    row = lax.broadcasted_iota(jnp.int32, (r_n, c_n), 0)
    out = []
    for k in range(4):
        s = 3 - k
        if s == 0:
            out.append(cur)
            continue
        tail = jnp.concatenate([jnp.zeros((r_n - 8, c_n), F32), pltpu.roll(fut8, 8 - s, 0)], axis=0)
        out.append(jnp.where(row >= r_n - s, tail, pltpu.roll(cur, r_n - s, 0)))
    return out


def _conv_apply(taps, w, b):
    acc = taps[0] * w[0:1, :]
    for k in range(1, 4):
        acc = acc + taps[k] * w[k:k + 1, :]
    return acc + b


def _inproj(x, g0, w):
    t_n = x.shape[0]
    tm = min(t_n, 1024)

    def body(x_ref, g_ref, w_ref, p_ref, u_ref):
        @pl.when(pl.program_id(1) == 0)
        def _():
            u_ref[...] = _rms(x_ref[...], g_ref[...]).astype(MXU)

        p_ref[...] = lax.dot_general(u_ref[...], w_ref[...], (((1,), (1,)), ((), ())), preferred_element_type=F32)

    return pl.pallas_call(
        body, name="inproj", grid=(t_n // tm, NP // SEG),
        in_specs=[pl.BlockSpec((tm, D), lambda i, j: (i, 0)), pl.BlockSpec((1, D), lambda i, j: (0, 0)),
                  pl.BlockSpec((SEG, D), lambda i, j: (j, 0))],
        out_specs=[pl.BlockSpec((tm, SEG), lambda i, j: (i, j)), pl.BlockSpec((tm, D), lambda i, j: (i, 0))],
        out_shape=[jax.ShapeDtypeStruct((t_n, NP), F32), jax.ShapeDtypeStruct((t_n, D), MXU)],
        compiler_params=_cp("parallel", "arbitrary"),
    )(x, g0, w)


def _ssd_prep(dtraw, dtb, alog):
    l_n = dtraw.shape[0]
    r = lax.broadcasted_iota(jnp.int32, (l_n, l_n), 0)
    c = lax.broadcasted_iota(jnp.int32, (l_n, l_n), 1)
    tril = (r >= c).astype(F32)
    triu = (r <= c).astype(F32)
    eye = (r == c).astype(F32)
    dt = _softplus(dtraw + dtb)
    adt = dt * (-jnp.exp(alog))
    ac = jnp.dot(tril, adt, preferred_element_type=F32, precision=HI)
    tn = (((0,), (0,)), ((), ()))
    ac_t = lax.dot_general(adt, triu, tn, preferred_element_type=F32, precision=HI)
    dt_t = lax.dot_general(dt, eye, tn, preferred_element_type=F32, precision=HI)
    return dt, dt_t, ac, ac_t, _rowsum(adt)


def _ssd_pair(j, xp, bg, cg, sp, dt, dt_t, ac, ac_t, aend):
    l_n = xp.shape[0]
    lane = lax.broadcasted_iota(jnp.int32, (l_n, 128), 1)
    sub = lax.broadcasted_iota(jnp.int32, (128, l_n), 0)
    lane1 = lax.broadcasted_iota(jnp.int32, (1, 128), 1)
    tri = lax.broadcasted_iota(jnp.int32, (l_n, l_n), 0) >= lax.broadcasted_iota(jnp.int32, (l_n, l_n), 1)
    cb = _nt(cg, bg)
    cs = _nn(cg, sp)
    ys, ss = [], []
    for e in range(2):
        h = 2 * j + e
        ac_l = jnp.sum(jnp.where(lane == h, ac, 0.0), axis=1, keepdims=True)
        dt_l = jnp.sum(jnp.where(lane == h, dt, 0.0), axis=1, keepdims=True)
        ac_s = jnp.sum(jnp.where(sub == h, ac_t, 0.0), axis=0, keepdims=True)
        dt_s = jnp.sum(jnp.where(sub == h, dt_t, 0.0), axis=0, keepdims=True)
        a_end = jnp.sum(jnp.where(lane1 == h, aend, 0.0), axis=1, keepdims=True)
        decay = jnp.exp(jnp.where(tri, ac_l - ac_s, -1e30))
        w = cb * decay * dt_s
        ys.append(_nn(w, xp) + jnp.exp(ac_l) * cs)
        ws = jnp.exp(a_end - ac_l) * dt_l
        ss.append(_tn(bg * ws, xp) + jnp.exp(a_end) * sp)
    lo = lax.broadcasted_iota(jnp.int32, (l_n, 128), 1) < 64
    lo_s = lax.broadcasted_iota(jnp.int32, (128, 128), 1) < 64
    return jnp.where(lo, ys[0], ys[1]), jnp.where(lo_s, ss[0], ss[1])


def _ssd_post(y, xs, z, dsk, nrm):
    y = (y + dsk * xs) * _silu(z)
    half = D // 2
    ya, yb = y[:, :half], y[:, half:]
    ya = ya * lax.rsqrt(jnp.mean(ya * ya, axis=-1, keepdims=True) + EPS)
    yb = yb * lax.rsqrt(jnp.mean(yb * yb, axis=-1, keepdims=True) + EPS)
    return jnp.concatenate([ya, yb], axis=1) * nrm


def _proj_specs(rows, seg_ids, order):
    return [pl.BlockSpec((rows, SEG), functools.partial(lambda i, s: (order(i), s), s=s)) for s in seg_ids]


def _prev8_specs(rows, seg_ids, order):
    rb = rows // 8
    return [pl.BlockSpec((8, SEG), functools.partial(lambda i, s: (jnp.maximum(order(i) * rb - 1, 0), s), s=s))
            for s in seg_ids]


def _full(shape):
    return pl.BlockSpec(shape, lambda i: (0,) * len(shape))


def _ssd_fwd(proj, cwx, cwb, cbx, cbb, dtb, alog, dsk, nrm):
    t_n = proj.shape[0]
    n_c = t_n // CH
    fwd = lambda i: i

    def body(z_ref, xs_ref, bc_ref, xsp_ref, bcp_ref, cwx_ref, cwb_ref, cbx_ref, cbb_ref, dtb_ref, alog_ref,
             dsk_ref, nrm_ref, ya_ref, sprev_ref, s_ref):
        c = pl.program_id(0)

        @pl.when(c == 0)
        def _():
            s_ref[...] = jnp.zeros_like(s_ref)

        keep = jnp.where(c == 0, 0.0, 1.0)
        xs_pre = _conv_apply(_taps_past(xs_ref[...], xsp_ref[...] * keep), cwx_ref[...], cbx_ref[...])
        bc_pre = _conv_apply(_taps_past(bc_ref[:, :512], bcp_ref[:, :512] * keep), cwb_ref[...], cbb_ref[...])
        prep = _ssd_prep(bc_ref[:, 512:640], dtb_ref[...], alog_ref[...])
        xs = _silu(xs_pre)
        bc = _silu(bc_pre)
        sprev_ref[0] = s_ref[...]
        ys = []
        for j in range(NH // 2):
            g = j // 4
            yp, sn = _ssd_pair(j, xs[:, 128 * j:128 * j + 128], bc[:, 128 * g:128 * g + 128],
                               bc[:, 256 + 128 * g:384 + 128 * g], s_ref[:, 128 * j:128 * j + 128], *prep)
            ys.append(yp)
            s_ref[:, 128 * j:128 * j + 128] = sn
        ya_ref[...] = _ssd_post(jnp.concatenate(ys, axis=1), xs, z_ref[...], dsk_ref[...], nrm_ref[...])

    return pl.pallas_call(
        body, name="ssd_fwd", grid=(n_c,),
        in_specs=_proj_specs(CH, (0, 5, 6), fwd) + _prev8_specs(CH, (5, 6), fwd) + [
            _full((4, D)), _full((4, 512)), _full((1, D)), _full((1, 512)), _full((1, 128)), _full((1, 128)),
            _full((1, D)), _full((1, D))],
        out_specs=[pl.BlockSpec((CH, D), lambda i: (i, 0)), pl.BlockSpec((1, NS, D), lambda i: (i, 0, 0))],
        out_shape=[jax.ShapeDtypeStruct((t_n, D), F32), jax.ShapeDtypeStruct((n_c, NS, D), F32)],
        scratch_shapes=[pltpu.VMEM((NS, D), F32)],
        compiler_params=_cp("arbitrary"),
    )(proj, proj, proj, proj, proj, cwx, cwb, cbx, cbb, dtb, alog, dsk, nrm)


def _ssd_bwd(dya, proj, sprev, cwx, cwb, cbx, cbb, dtb, alog, dsk, nrm):
    t_n = proj.shape[0]
    n_c = t_n // CH
    rev = lambda i: n_c - 1 - i

    def body(dya_ref, z_ref, xs_ref, bc_ref, xsp_ref, bcp_ref, sprev_ref, cwx_ref, cwb_ref, cbx_ref, cbb_ref,
             dtb_ref, alog_ref, dsk_ref, nrm_ref,
             dz_ref, dxs_ref, dbc_ref, dcwx_ref, dcwb_ref, dcbx_ref, dcbb_ref, ddtb_ref, dalog_ref, ddsk_ref,
             dnrm_ref, ds_ref, futx_ref, futb_ref):
        i = pl.program_id(0)
        acc_refs = (dcwx_ref, dcwb_ref, dcbx_ref, dcbb_ref, ddtb_ref, dalog_ref, ddsk_ref, dnrm_ref)

        @pl.when(i == 0)
        def _():
            for r in (ds_ref, futx_ref, futb_ref) + acc_refs:
                r[...] = jnp.zeros_like(r)

        keep = jnp.where(i == n_c - 1, 0.0, 1.0)
        taps_x = _taps_past(xs_ref[...], xsp_ref[...] * keep)
        taps_b = _taps_past(bc_ref[:, :512], bcp_ref[:, :512] * keep)
        xs_pre = _conv_apply(taps_x, cwx_ref[...], cbx_ref[...])
        bc_pre = _conv_apply(taps_b, cwb_ref[...], cbb_ref[...])
        xs = _silu(xs_pre)
        bc = _silu(bc_pre)
        prep, prep_vjp = jax.vjp(_ssd_prep, bc_ref[:, 512:640], dtb_ref[...], alog_ref[...])
        s_in = sprev_ref[0]

        def pair_args(j):
            g = j // 4
            return (xs[:, 128 * j:128 * j + 128], bc[:, 128 * g:128 * g + 128],
                    bc[:, 256 + 128 * g:384 + 128 * g], s_in[:, 128 * j:128 * j + 128]) + tuple(prep)

        y = jnp.concatenate([_ssd_pair(j, *pair_args(j))[0] for j in range(NH // 2)], axis=1)
        _, post_vjp = jax.vjp(_ssd_post, y, xs, z_ref[...], dsk_ref[...], nrm_ref[...])
        dy, dxs_skip, dz, ddsk, dnrm = post_vjp(dya_ref[...])
        dz_ref[...] = dz.astype(dz_ref.dtype)
        ddsk_ref[...] += ddsk
        dnrm_ref[...] += dnrm

        dprep = [jnp.zeros_like(p) for p in prep]
        dxp = []
        dbg = [jnp.zeros((CH, 128), F32), jnp.zeros((CH, 128), F32)]
        dcg = [jnp.zeros((CH, 128), F32), jnp.zeros((CH, 128), F32)]
        for j in range(NH // 2):
            g = j // 4
            _, pair_vjp = jax.vjp(functools.partial(_ssd_pair, j), *pair_args(j))
            cts = pair_vjp((dy[:, 128 * j:128 * j + 128], ds_ref[:, 128 * j:128 * j + 128]))
            dxp.append(cts[0])
            dbg[g] = dbg[g] + cts[1]
            dcg[g] = dcg[g] + cts[2]
            ds_ref[:, 128 * j:128 * j + 128] = cts[3]
            dprep = [a + b for a, b in zip(dprep, cts[4:])]
        ddtraw, ddtb, dalog = prep_vjp(tuple(dprep))
        ddtb_ref[...] += ddtb
        dalog_ref[...] += dalog

        dxs_pre = (dxs_skip + jnp.concatenate(dxp, axis=1)) * _dsilu(xs_pre)
        dbc_pre = jnp.concatenate([dbg[0], dbg[1], dcg[0], dcg[1]], axis=1) * _dsilu(bc_pre)
        dcbx_ref[...] += _rowsum(dxs_pre)
        dcbb_ref[...] += _rowsum(dbc_pre)
        for k in range(4):
            dcwx_ref[k:k + 1, :] += _rowsum(dxs_pre * taps_x[k])
            dcwb_ref[k:k + 1, :] += _rowsum(dbc_pre * taps_b[k])
        fx = _taps_future(dxs_pre, futx_ref[...])
        fb = _taps_future(dbc_pre, futb_ref[...])
        cwx = cwx_ref[...]
        cwb = cwb_ref[...]
        dxs_in = fx[0] * cwx[0:1, :]
        dbc_in = fb[0] * cwb[0:1, :]
        for k in range(1, 4):
            dxs_in = dxs_in + fx[k] * cwx[k:k + 1, :]
            dbc_in = dbc_in + fb[k] * cwb[k:k + 1, :]
        futx_ref[...] = dxs_pre[0:8, :]
        futb_ref[...] = dbc_pre[0:8, :]
        dxs_ref[...] = dxs_in.astype(dxs_ref.dtype)
        dbc_ref[...] = jnp.concatenate([dbc_in, ddtraw, jnp.zeros((CH, SEG - 640), F32)], axis=1).astype(dbc_ref.dtype)

    row_out = lambda: pl.BlockSpec((CH, D), lambda i: (rev(i), 0))
    outs = pl.pallas_call(
        body, name="ssd_bwd", grid=(n_c,),
        in_specs=[pl.BlockSpec((CH, D), lambda i: (rev(i), 0))] + _proj_specs(CH, (0, 5, 6), rev)
        + _prev8_specs(CH, (5, 6), rev) + [pl.BlockSpec((1, NS, D), lambda i: (rev(i), 0, 0)),
                                            _full((4, D)), _full((4, 512)), _full((1, D)), _full((1, 512)),
                                            _full((1, 128)), _full((1, 128)), _full((1, D)), _full((1, D))],
        out_specs=[row_out(), row_out(), row_out(), _full((4, D)), _full((4, 512)), _full((1, D)), _full((1, 512)),
                   _full((1, 128)), _full((1, 128)), _full((1, D)), _full((1, D))],
        out_shape=[jax.ShapeDtypeStruct((t_n, D), MXU)] * 3 + [
            jax.ShapeDtypeStruct(s, F32) for s in ((4, D), (4, 512), (1, D), (1, 512), (1, 128), (1, 128), (1, D), (1, D))],
        scratch_shapes=[pltpu.VMEM((NS, D), F32), pltpu.VMEM((8, D), F32), pltpu.VMEM((8, 512), F32)],
        compiler_params=_cp("arbitrary"),
    )(dya, proj, proj, proj, proj, proj, sprev, cwx, cwb, cbx, cbb, dtb, alog, dsk, nrm)
    return outs


LRU_ROWS = 256
LRU_BLK = 256


def _lru_gates(xr, wa, wx, ba, bx, lam):
    pr = jnp.concatenate([_nn(xr[:, LRU_BLK * b:LRU_BLK * (b + 1)], wa[b]) for b in range(D // LRU_BLK)], axis=1) + ba
    pi = jnp.concatenate([_nn(xr[:, LRU_BLK * b:LRU_BLK * (b + 1)], wx[b]) for b in range(D // LRU_BLK)], axis=1) + bx
    log_a = -LRU_C * jax.nn.sigmoid(pr) * _softplus(-lam)
    a = jnp.exp(log_a)
    mult = jnp.sqrt(1.0 - jnp.exp(2.0 * log_a))
    return a, mult * (jax.nn.sigmoid(pi) * xr)


def _lru_out(h, g):
    return h * jax.nn.gelu(g, approximate=True)


def _lru_fwd(proj, cw, cb, wa, wx, ba, bx, lam):
    t_n = proj.shape[0]
    rows = min(LRU_ROWS, t_n)
    fwd = lambda i: i

    def body(g_ref, x_ref, xp_ref, cw_ref, cb_ref, wa_ref, wx_ref, ba_ref, bx_ref, lam_ref, yb_ref, h_ref,
             a_s, u_s, carry):
        i = pl.program_id(0)

        @pl.when(i == 0)
        def _():
            carry[...] = jnp.zeros_like(carry)

        keep = jnp.where(i == 0, 0.0, 1.0)
        xr = _conv_apply(_taps_past(x_ref[...], xp_ref[...] * keep), cw_ref[...], cb_ref[...])
        a, u = _lru_gates(xr, wa_ref[...], wx_ref[...], ba_ref[...], bx_ref[...], lam_ref[...])
        a_s[...] = a
        u_s[...] = u
        row = lax.broadcasted_iota(jnp.int32, (8, D), 0)

        def blk(b, c):
            s = pl.multiple_of(b * 8, 8)
            av = a_s[pl.ds(s, 8), :]
            uv = u_s[pl.ds(s, 8), :]
            for d in (1, 2, 4):
                m = row >= d
                uv = uv + av * jnp.where(m, pltpu.roll(uv, d, 0), 0.0)
                av = av * jnp.where(m, pltpu.roll(av, d, 0), 1.0)
            hv = uv + av * c
            h_ref[pl.ds(s, 8), :] = hv
            return hv[7:8, :]

        carry[0:1, :] = lax.fori_loop(0, rows // 8, blk, carry[0:1, :])
        yb_ref[...] = _lru_out(h_ref[...], g_ref[...])

    return pl.pallas_call(
        body, name="lru_fwd", grid=(t_n // rows,),
        in_specs=_proj_specs(rows, (1, 2), fwd) + _prev8_specs(rows, (2,), fwd) + [
            _full((4, D)), _full((1, D)), _full((4, LRU_BLK, LRU_BLK)), _full((4, LRU_BLK, LRU_BLK)),
            _full((1, D)), _full((1, D)), _full((1, D))],
        out_specs=[pl.BlockSpec((rows, D), lambda i: (i, 0)), pl.BlockSpec((rows, D), lambda i: (i, 0))],
        out_shape=[jax.ShapeDtypeStruct((t_n, D), F32), jax.ShapeDtypeStruct((t_n, D), F32)],
        scratch_shapes=[pltpu.VMEM((rows, D), F32), pltpu.VMEM((rows, D), F32), pltpu.VMEM((8, D), F32)],
        compiler_params=_cp("arbitrary"),
    )(proj, proj, proj, cw, cb, wa, wx, ba, bx, lam)


def _lru_bwd(dyb, proj, h, cw, cb, wa, wx, ba, bx, lam):
    t_n = proj.shape[0]
    rows = min(LRU_ROWS, t_n)
    n_t = t_n // rows
    rev = lambda i: n_t - 1 - i
    rb = rows // 8

    def body(dyb_ref, g_ref, x_ref, xp_ref, h_ref, hp_ref, cw_ref, cb_ref, wa_ref, wx_ref, ba_ref, bx_ref, lam_ref,
             dg_ref, dx_ref, dcw_ref, dcb_ref, dwa_ref, dwx_ref, dba_ref, dbx_ref, dlam_ref,
             a_s, dh_s, hx_s, da_s, du_s, carry, fut):
        i = pl.program_id(0)
        acc_refs = (dcw_ref, dcb_ref, dwa_ref, dwx_ref, dba_ref, dbx_ref, dlam_ref)

        @pl.when(i == 0)
        def _():
            for r in (carry, fut) + acc_refs:
                r[...] = jnp.zeros_like(r)

        keep = jnp.where(i == n_t - 1, 0.0, 1.0)
        taps = _taps_past(x_ref[...], xp_ref[...] * keep)
        xr = _conv_apply(taps, cw_ref[...], cb_ref[...])
        gate_in = (xr, wa_ref[...], wx_ref[...], ba_ref[...], bx_ref[...], lam_ref[...])
        (a, _), gates_vjp = jax.vjp(_lru_gates, *gate_in)
        _, out_vjp = jax.vjp(_lru_out, h_ref[...], g_ref[...])
        dh, dg = out_vjp(dyb_ref[...])
        dg_ref[...] = dg.astype(dg_ref.dtype)
        a_s[...] = a
        dh_s[...] = dh
        hx_s[0:8, :] = hp_ref[...] * keep
        hx_s[8:, :] = h_ref[...]
        row = lax.broadcasted_iota(jnp.int32, (8, D), 0)

        def blk(b, c):
            s = pl.multiple_of((rb - 1 - b) * 8, 8)
            av = a_s[pl.ds(s, 8), :]
            dhv = dh_s[pl.ds(s, 8), :]
            a0 = av
            kv = av * dhv
            for d in (1, 2, 4):
                m = row <= 7 - d
                kv = kv + av * jnp.where(m, pltpu.roll(kv, 8 - d, 0), 0.0)
                av = av * jnp.where(m, pltpu.roll(av, 8 - d, 0), 1.0)
            kv = kv + av * c
            gv = dhv + jnp.where(row < 7, pltpu.roll(kv, 7, 0), c)
            hb = hx_s[pl.ds(s + 8, 8), :]
            hpv = hx_s[pl.ds(s, 8), :]
            hprev = jnp.where(row >= 1, pltpu.roll(hb, 1, 0), hpv[7:8, :])
            du_s[pl.ds(s, 8), :] = gv
            da_s[pl.ds(s, 8), :] = gv * hprev
            del a0
            return kv[0:1, :]

        carry[0:1, :] = lax.fori_loop(0, rb, blk, carry[0:1, :])
        dxr, dwa, dwx, dba, dbx, dlam = gates_vjp((da_s[...], du_s[...]))
        dwa_ref[...] += dwa
        dwx_ref[...] += dwx
        dba_ref[...] += dba
        dbx_ref[...] += dbx
        dlam_ref[...] += dlam
        dcb_ref[...] += _rowsum(dxr)
        for k in range(4):
            dcw_ref[k:k + 1, :] += _rowsum(dxr * taps[k])
        ft = _taps_future(dxr, fut[...])
        cwv = cw_ref[...]
        dx = ft[0] * cwv[0:1, :]
        for k in range(1, 4):
            dx = dx + ft[k] * cwv[k:k + 1, :]
        fut[...] = dxr[0:8, :]
        dx_ref[...] = dx.astype(dx_ref.dtype)

    row_in = lambda: pl.BlockSpec((rows, D), lambda i: (rev(i), 0))
    prev_h = pl.BlockSpec((8, D), lambda i: (jnp.maximum(rev(i) * rb - 1, 0), 0))
    wspec = lambda: _full((4, LRU_BLK, LRU_BLK))
    return pl.pallas_call(
        body, name="lru_bwd", grid=(n_t,),
        in_specs=[row_in()] + _proj_specs(rows, (1, 2), rev) + _prev8_specs(rows, (2,), rev) + [row_in(), prev_h] + [
            _full((4, D)), _full((1, D)), wspec(), wspec(), _full((1, D)), _full((1, D)), _full((1, D))],
        out_specs=[row_in(), row_in(), _full((4, D)), _full((1, D)), wspec(), wspec(), _full((1, D)), _full((1, D)),
                   _full((1, D))],
        out_shape=[jax.ShapeDtypeStruct((t_n, D), MXU)] * 2 + [
            jax.ShapeDtypeStruct(s, F32) for s in ((4, D), (1, D), (4, LRU_BLK, LRU_BLK), (4, LRU_BLK, LRU_BLK),
                                                   (1, D), (1, D), (1, D))],
        scratch_shapes=[pltpu.VMEM((rows, D), F32), pltpu.VMEM((rows, D), F32), pltpu.VMEM((rows + 8, D), F32),
                        pltpu.VMEM((rows, D), F32), pltpu.VMEM((rows, D), F32), pltpu.VMEM((8, D), F32),
                        pltpu.VMEM((8, D), F32)],
        compiler_params=_cp("arbitrary"),
    )(dyb, proj, proj, proj, h, h, cw, cb, wa, wx, ba, bx, lam)


def _merge_out(ya, yb, proj, x, wout, g1):
    t_n = x.shape[0]
    tm = min(t_n, 512)

    def body(ya_ref, yb_ref, ga_ref, gb_ref, x_ref, w_ref, g_ref, h1_ref, mix_ref, mg_ref):
        merged = jax.nn.sigmoid(ga_ref[...]) * ya_ref[...] + jax.nn.sigmoid(gb_ref[...]) * yb_ref[...]
        mg = merged.astype(MXU)
        mg_ref[...] = mg
        mix = jnp.dot(mg, w_ref[...], preferred_element_type=F32)
        mix_ref[...] = mix
        h1_ref[...] = x_ref[...] + _rms(mix, g_ref[...])

    row = lambda: pl.BlockSpec((tm, D), lambda i: (i, 0))
    return pl.pallas_call(
        body, name="merge_out", grid=(t_n // tm,),
        in_specs=[row(), row()] + _proj_specs(tm, (3, 4), lambda i: i) + [row(), _full((D, D)), _full((1, D))],
        out_specs=[row(), row(), row()],
        out_shape=[jax.ShapeDtypeStruct((t_n, D), F32), jax.ShapeDtypeStruct((t_n, D), F32),
                   jax.ShapeDtypeStruct((t_n, D), MXU)],
        compiler_params=_cp("parallel"),
    )(ya, yb, proj, proj, x, wout, g1)


def _out_bwd(dh1, mix, ya, yb, proj, wout, g1):
    t_n = dh1.shape[0]
    tm = min(t_n, 512)

    def body(dh1_ref, mix_ref, ya_ref, yb_ref, ga_ref, gb_ref, w_ref, g_ref,
             dmix_ref, dya_ref, dyb_ref, dga_ref, dgb_ref, dg1_ref):
        @pl.when(pl.program_id(0) == 0)
        def _():
            dg1_ref[...] = jnp.zeros_like(dg1_ref)

        dmix, dg_rows = _rms_bwd(mix_ref[...], g_ref[...], dh1_ref[...])
        dg1_ref[...] += _rowsum(dg_rows)
        dmix_b = dmix.astype(MXU)
        dmix_ref[...] = dmix_b
        dmg = lax.dot_general(dmix_b, w_ref[...], (((1,), (1,)), ((), ())), preferred_element_type=F32)
        sa = jax.nn.sigmoid(ga_ref[...])
        sb = jax.nn.sigmoid(gb_ref[...])
        dya_ref[...] = dmg * sa
        dyb_ref[...] = dmg * sb
        dga_ref[...] = (dmg * ya_ref[...] * sa * (1.0 - sa)).astype(MXU)
        dgb_ref[...] = (dmg * yb_ref[...] * sb * (1.0 - sb)).astype(MXU)

    row = lambda: pl.BlockSpec((tm, D), lambda i: (i, 0))
    return pl.pallas_call(
        body, name="out_bwd", grid=(t_n // tm,),
        in_specs=[row(), row(), row(), row()] + _proj_specs(tm, (3, 4), lambda i: i) + [_full((D, D)), _full((1, D))],
        out_specs=[row(), row(), row(), row(), row(), _full((1, D))],
        out_shape=[jax.ShapeDtypeStruct((t_n, D), MXU), jax.ShapeDtypeStruct((t_n, D), F32),
                   jax.ShapeDtypeStruct((t_n, D), F32), jax.ShapeDtypeStruct((t_n, D), MXU),
                   jax.ShapeDtypeStruct((t_n, D), MXU), jax.ShapeDtypeStruct((1, D), F32)],
        compiler_params=_cp("arbitrary"),
    )(dh1, mix, ya, yb, proj, proj, wout, g1)


MLP_TF = 512


def _mlp_fwd(h1, g2, wup, wdown, g3, tgt):
    t_n = h1.shape[0]
    tm = min(t_n, 512)
    n_f = FF // MLP_TF

    def body(h1_ref, g2_ref, wu_ref, wd_ref, g3_ref, tgt_ref, hp_ref, v_ref, ff_ref, loss_ref, acc):
        i, j = pl.program_id(0), pl.program_id(1)

        @pl.when((i == 0) & (j == 0))
        def _():
            loss_ref[...] = jnp.zeros_like(loss_ref)

        @pl.when(j == 0)
        def _():
            v_ref[...] = _rms(h1_ref[...], g2_ref[...]).astype(MXU)
            acc[...] = jnp.zeros_like(acc)

        hp = lax.dot_general(v_ref[...], wu_ref[...], (((1,), (1,)), ((), ())), preferred_element_type=F32)
        hp_ref[...] = hp
        hid = jnp.square(jnp.maximum(hp, 0.0))
        acc[...] += jnp.dot(hid.astype(MXU), wd_ref[...], preferred_element_type=F32)

        @pl.when(j == n_f - 1)
        def _():
            ff = acc[...]
            ff_ref[...] = ff
            err = h1_ref[...] + _rms(ff, g3_ref[...]) - tgt_ref[...]
            loss_ref[...] += 0.5 * jnp.sum(jnp.mean(err * err, axis=-1, keepdims=True), axis=0, keepdims=True)

    row = lambda: pl.BlockSpec((tm, D), lambda i, j: (i, 0))
    vec = lambda: pl.BlockSpec((1, D), lambda i, j: (0, 0))
    return pl.pallas_call(
        body, name="mlp_fwd", grid=(t_n // tm, n_f),
        in_specs=[row(), vec(), pl.BlockSpec((MLP_TF, D), lambda i, j: (j, 0)),
                  pl.BlockSpec((MLP_TF, D), lambda i, j: (j, 0)), vec(), row()],
        out_specs=[pl.BlockSpec((tm, MLP_TF), lambda i, j: (i, j)), row(), row(),
                   pl.BlockSpec((1, 1), lambda i, j: (0, 0))],
        out_shape=[jax.ShapeDtypeStruct((t_n, FF), F32), jax.ShapeDtypeStruct((t_n, D), MXU),
                   jax.ShapeDtypeStruct((t_n, D), F32), jax.ShapeDtypeStruct((1, 1), F32)],
        scratch_shapes=[pltpu.VMEM((tm, D), F32)],
        compiler_params=_cp("arbitrary", "arbitrary"),
    )(h1, g2, wup, wdown, g3, tgt)


def _mlp_bwd(h1, ff, tgt, hp, wup, wdown, g2, g3):
    t_n = h1.shape[0]
    tm = min(t_n, 512)
    n_f = FF // MLP_TF

    def body(h1_ref, ff_ref, tgt_ref, hp_ref, wu_ref, wd_ref, g2_ref, g3_ref,
             dh1_ref, dff_ref, dhp_ref, hid_ref, dg2_ref, dg3_ref, acc, dout_s):
        i, j = pl.program_id(0), pl.program_id(1)

        @pl.when((i == 0) & (j == 0))
        def _():
            dg2_ref[...] = jnp.zeros_like(dg2_ref)
            dg3_ref[...] = jnp.zeros_like(dg3_ref)

        @pl.when(j == 0)
        def _():
            ff = ff_ref[...]
            dout = (h1_ref[...] + _rms(ff, g3_ref[...]) - tgt_ref[...]) * (1.0 / D)
            dout_s[...] = dout
            dff, dg_rows = _rms_bwd(ff, g3_ref[...], dout)
            dg3_ref[...] += _rowsum(dg_rows)
            dff_ref[...] = dff.astype(MXU)
            acc[...] = jnp.zeros_like(acc)

        hp = hp_ref[...]
        relu = jnp.maximum(hp, 0.0)
        hid_ref[...] = jnp.square(relu).astype(MXU)
        dhid = lax.dot_general(dff_ref[...], wd_ref[...], (((1,), (1,)), ((), ())), preferred_element_type=F32)
        dhp = (dhid * (2.0 * relu)).astype(MXU)
        dhp_ref[...] = dhp
        acc[...] += jnp.dot(dhp, wu_ref[...], preferred_element_type=F32)

        @pl.when(j == n_f - 1)
        def _():
            dv, dg_rows = _rms_bwd(h1_ref[...], g2_ref[...], acc[...])
            dg2_ref[...] += _rowsum(dg_rows)
            dh1_ref[...] = dout_s[...] + dv

    row = lambda: pl.BlockSpec((tm, D), lambda i, j: (i, 0))
    vec = lambda: pl.BlockSpec((1, D), lambda i, j: (0, 0))
    blk = lambda: pl.BlockSpec((tm, MLP_TF), lambda i, j: (i, j))
    return pl.pallas_call(
        body, name="mlp_bwd", grid=(t_n // tm, n_f),
        in_specs=[row(), row(), row(), blk(), pl.BlockSpec((MLP_TF, D), lambda i, j: (j, 0)),
                  pl.BlockSpec((MLP_TF, D), lambda i, j: (j, 0)), vec(), vec()],
        out_specs=[row(), row(), blk(), blk(), vec(), vec()],
        out_shape=[jax.ShapeDtypeStruct((t_n, D), F32), jax.ShapeDtypeStruct((t_n, D), MXU),
                   jax.ShapeDtypeStruct((t_n, FF), MXU), jax.ShapeDtypeStruct((t_n, FF), MXU),
                   jax.ShapeDtypeStruct((1, D), F32), jax.ShapeDtypeStruct((1, D), F32)],
        scratch_shapes=[pltpu.VMEM((tm, D), F32), pltpu.VMEM((tm, D), F32)],
        compiler_params=_cp("arbitrary", "arbitrary"),
    )(h1, ff, tgt, hp, wup, wdown, g2, g3)


def _wgrad(a, g, name):
    t_n, k_n = a.shape
    n_n = g.shape[1]
    tt = min(t_n, 1024)
    tk, tn = min(k_n, 1024), min(n_n, 1024)

    def body(a_ref, g_ref, o_ref):
        @pl.when(pl.program_id(2) == 0)
        def _():
            o_ref[...] = jnp.zeros_like(o_ref)

        o_ref[...] += lax.dot_general(a_ref[...], g_ref[...], (((0,), (0,)), ((), ())), preferred_element_type=F32)

    return pl.pallas_call(
        body, name=name, grid=(k_n // tk, n_n // tn, t_n // tt),
        in_specs=[pl.BlockSpec((tt, tk), lambda k, n, t: (t, k)), pl.BlockSpec((tt, tn), lambda k, n, t: (t, n))],
        out_specs=pl.BlockSpec((tk, tn), lambda k, n, t: (k, n)),
        out_shape=jax.ShapeDtypeStruct((k_n, n_n), F32),
        compiler_params=_cp("parallel", "parallel", "arbitrary"),
    )(a, g)


def _inproj_bwd(dproj, w, x, g0, dh1):
    t_n = x.shape[0]
    tm = min(t_n, 1024)
    n_k = NP // SEG

    def body(dp_ref, w_ref, x_ref, g_ref, dh1_ref, dx_ref, dg0_ref, acc):
        i, k = pl.program_id(0), pl.program_id(1)

        @pl.when((i == 0) & (k == 0))
        def _():
            dg0_ref[...] = jnp.zeros_like(dg0_ref)

        @pl.when(k == 0)
        def _():
            acc[...] = jnp.zeros_like(acc)

        acc[...] += jnp.dot(dp_ref[...], w_ref[...], preferred_element_type=F32)

        @pl.when(k == n_k - 1)
        def _():
            dx, dg_rows = _rms_bwd(x_ref[...], g_ref[...], acc[...])
            dg0_ref[...] += _rowsum(dg_rows)
            dx_ref[...] = dh1_ref[...] + dx

    row = lambda: pl.BlockSpec((tm, D), lambda i, k: (i, 0))
    vec = lambda: pl.BlockSpec((1, D), lambda i, k: (0, 0))
    return pl.pallas_call(
        body, name="inproj_bwd", grid=(t_n // tm, n_k),
        in_specs=[pl.BlockSpec((tm, SEG), lambda i, k: (i, k)), pl.BlockSpec((SEG, D), lambda i, k: (k, 0)),
                  row(), vec(), row()],
        out_specs=[row(), vec()],
        out_shape=[jax.ShapeDtypeStruct((t_n, D), F32), jax.ShapeDtypeStruct((1, D), F32)],
        scratch_shapes=[pltpu.VMEM((tm, D), F32)],
        compiler_params=_cp("arbitrary", "arbitrary"),
    )(dproj, w, x, g0, dh1)


def _blockdiag4(w):
    z = jnp.zeros((4, 4, 64, 4, 64), w.dtype)
    w4 = w.reshape(4, 4, 64, 64)
    for q in range(4):
        z = z.at[:, q, :, q, :].set(w4[:, q])
    return z.reshape(4, 256, 256).astype(MXU)


def _blockdiag4_extract(g):
    g5 = g.reshape(4, 4, 64, 4, 64)
    return jnp.stack([g5[:, q, :, q, :] for q in range(4)], axis=1).reshape(NH, 64, 64)


def _local_step(x, tgt, p):
    f = lambda a: a.astype(F32)
    proj, u = _inproj(x, p["norm_mix_pre"], p["w_in_pT"])
    ssm_params = (p["cw_xs"], p["cw_bc"], p["cb_xs"], p["cb_bc"], p["dt_bias"], p["a_log"], p["d_skip_x"],
                  p["ssm_norm"])
    ya, sprev = _ssd_fwd(proj, *ssm_params)
    lru_params = (p["conv_lru_w"], p["conv_lru_b"], p["wa_bd"], p["wx_bd"], p["lru_ba"], p["lru_bx"],
                  p["lru_lambda"])
    yb, h = _lru_fwd(proj, *lru_params)
    h1, mix, merged = _merge_out(ya, yb, proj, x, p["w_out"], p["norm_mix_post"])
    hp, v, ff, loss = _mlp_fwd(h1, p["norm_mlp_pre"], p["w_upT"], p["w_down"], p["norm_mlp_post"], tgt)

    dh1, dff, dhp, hid, dg2, dg3 = _mlp_bwd(h1, ff, tgt, hp, p["w_upT"], p["w_down"], p["norm_mlp_pre"],
                                            p["norm_mlp_post"])
    d_w_down = _wgrad(hid, dff, "wgrad_down")
    d_w_up_t = _wgrad(dhp, v, "wgrad_up")
    dmix, dya, dyb, dga, dgb, dg1 = _out_bwd(dh1, mix, ya, yb, proj, p["w_out"], p["norm_mix_post"])
    d_w_out = _wgrad(merged, dmix, "wgrad_out")
    (dz, dxs, dbc, dcwx, dcwb, dcbx, dcbb, ddtb, dalog, ddsk, dnrm) = _ssd_bwd(dya, proj, sprev, *ssm_params)
    (dgl, dxl, dcwl, dcbl, dwa, dwx, dba, dbx, dlam) = _lru_bwd(dyb, proj, h, *lru_params)
    dproj = jnp.concatenate([dz, dgl, dxl, dga, dgb, dxs, dbc], axis=1)
    grad_x, dg0 = _inproj_bwd(dproj, p["w_in_pT"], x, p["norm_mix_pre"], dh1)
    d_w_in_pt = _wgrad(dproj, u, "wgrad_in")
    grads = dict(
        norm_mix_pre=dg0, w_in_pT=d_w_in_pt, conv_ssm_w=jnp.concatenate([dcwx, dcwb], axis=1),
        conv_ssm_b=jnp.concatenate([dcbx, dcbb], axis=1), dt_bias=ddtb[:, :NH], a_log=dalog[:, :NH],
        d_skip=f(ddsk).reshape(NH, 64).sum(axis=1)[None, :], ssm_norm=dnrm, conv_lru_w=dcwl, conv_lru_b=dcbl,
        lru_wa=_blockdiag4_extract(dwa), lru_ba=dba, lru_wx=_blockdiag4_extract(dwx), lru_bx=dbx, lru_lambda=dlam,
        w_out=d_w_out, norm_mix_post=dg1, norm_mlp_pre=dg2, w_upT=d_w_up_t, w_down=d_w_down, norm_mlp_post=dg3)
    return loss[0, 0], grad_x, grads


W_IN_COLS = 6672


def _w_in_t_to_padded(wt):
    z, xs, bc, dt = wt[0:1024], wt[1024:2048], wt[2048:2560], wt[2560:2576]
    gl, xl, ga, gb = wt[2576:3600], wt[3600:4624], wt[4624:5648], wt[5648:6672]
    return jnp.concatenate([z, gl, xl, ga, gb, xs, bc, dt, jnp.zeros((NP - 6672, wt.shape[1]), wt.dtype)], axis=0)


def _w_in_t_from_padded(wp):
    z, gl, xl, ga, gb = (wp[SEG * s:SEG * (s + 1)] for s in range(5))
    xs, bc, dt = wp[5120:6144], wp[6144:6656], wp[6656:6672]
    return jnp.concatenate([z, xs, bc, dt, gl, xl, ga, gb], axis=0)


def _prep_params(full, big):
    f = lambda a: a.astype(F32)
    pad128 = lambda a: jnp.pad(f(a).reshape(1, -1), ((0, 0), (0, 128 - a.size)))
    cw = f(full["conv_ssm_w"])
    cb = f(full["conv_ssm_b"]).reshape(1, -1)
    return dict(
        big, norm_mix_pre=f(full["norm_mix_pre"]).reshape(1, D),
        cw_xs=cw[:, :D], cw_bc=cw[:, D:], cb_xs=cb[:, :D], cb_bc=cb[:, D:],
        dt_bias=pad128(full["dt_bias"]), a_log=pad128(full["a_log"]),
        d_skip_x=jnp.repeat(f(full["d_skip"]).reshape(-1), 64).reshape(1, D), ssm_norm=f(full["ssm_norm"]).reshape(1, D),
        conv_lru_w=f(full["conv_lru_w"]), conv_lru_b=f(full["conv_lru_b"]).reshape(1, D),
        wa_bd=_blockdiag4(full["lru_wa"]), wx_bd=_blockdiag4(full["lru_wx"]),
        lru_ba=f(full["lru_ba"]).reshape(1, D), lru_bx=f(full["lru_bx"]).reshape(1, D),
        lru_lambda=f(full["lru_lambda"]).reshape(1, D),
        norm_mix_post=f(full["norm_mix_post"]).reshape(1, D), norm_mlp_pre=f(full["norm_mlp_pre"]).reshape(1, D),
        norm_mlp_post=f(full["norm_mlp_post"]).reshape(1, D))


MESH_ID = pl.DeviceIdType.MESH
ANY = pl.BlockSpec(memory_space=pl.ANY)


def _my_place():
    x, y, c = lax.axis_index("x"), lax.axis_index("y"), lax.axis_index("c")
    return x, y, c, 4 * x + 2 * y + c


def _peer(x, y, c, k):
    return (x ^ ((k >> 2) & 1), y ^ ((k >> 1) & 1), c ^ (k & 1))


def _all_gather(pack):
    def body(in_ref, out_ref, send_sems, recv_sems, local_sem):
        x, y, c, me = _my_place()
        mine = pltpu.make_async_copy(in_ref, out_ref.at[me], local_sem)
        mine.start()
        copies = []
        for k in range(1, N_DEV):
            cp = pltpu.make_async_remote_copy(
                src_ref=in_ref, dst_ref=out_ref.at[me], send_sem=send_sems.at[k - 1], recv_sem=recv_sems.at[k - 1],
                device_id=_peer(x, y, c, k), device_id_type=MESH_ID)
            cp.start()
            copies.append(cp)
        for cp in copies:
            cp.wait()
        mine.wait()

    return pl.pallas_call(
        body, name="weight_all_gather", in_specs=[ANY], out_specs=ANY,
        out_shape=jax.ShapeDtypeStruct((N_DEV,) + pack.shape, pack.dtype),
        scratch_shapes=[pltpu.SemaphoreType.DMA((N_DEV - 1,)), pltpu.SemaphoreType.DMA((N_DEV - 1,)),
                        pltpu.SemaphoreType.DMA],
    )(pack)


def _grad_exchange(big, small):
    def body(big_ref, small_ref, bout_ref, sout_ref, send_sems, recv_sems, local_sems):
        x, y, c, me = _my_place()
        own_b = pltpu.make_async_copy(big_ref.at[me], bout_ref.at[me], local_sems.at[0])
        own_s = pltpu.make_async_copy(small_ref, sout_ref.at[me], local_sems.at[1])
        own_b.start()
        own_s.start()
        copies = []
        for k in range(1, N_DEV):
            peer = _peer(x, y, c, k)
            cb = pltpu.make_async_remote_copy(
                src_ref=big_ref.at[me ^ k], dst_ref=bout_ref.at[me], send_sem=send_sems.at[2 * k - 2],
                recv_sem=recv_sems.at[2 * k - 2], device_id=peer, device_id_type=MESH_ID)
            cs = pltpu.make_async_remote_copy(
                src_ref=small_ref, dst_ref=sout_ref.at[me], send_sem=send_sems.at[2 * k - 1],
                recv_sem=recv_sems.at[2 * k - 1], device_id=peer, device_id_type=MESH_ID)
            cb.start()
            cs.start()
            copies += [cb, cs]
        for cp in copies:
            cp.wait()
        own_b.wait()
        own_s.wait()

    n_sem = 2 * (N_DEV - 1)
    return pl.pallas_call(
        body, name="grad_exchange", in_specs=[ANY, ANY], out_specs=[ANY, ANY],
        out_shape=[jax.ShapeDtypeStruct(big.shape, big.dtype), jax.ShapeDtypeStruct((N_DEV,) + small.shape, small.dtype)],
        scratch_shapes=[pltpu.SemaphoreType.DMA((n_sem,)), pltpu.SemaphoreType.DMA((n_sem,)),
                        pltpu.SemaphoreType.DMA((2,))],
    )(big, small)


def _slot_sum(parts):
    r_n, c_n = parts.shape[1:]
    tr = max(t for t in range(16, 513, 16) if r_n % t == 0)

    def body(p_ref, o_ref):
        acc = p_ref[0].astype(F32)
        for k in range(1, N_DEV):
            acc = acc + p_ref[k].astype(F32)
        o_ref[...] = acc

    return pl.pallas_call(
        body, name="slot_sum", grid=(r_n // tr,),
        in_specs=[pl.BlockSpec((N_DEV, tr, c_n), lambda i: (0, i, 0))],
        out_specs=pl.BlockSpec((tr, c_n), lambda i: (i, 0)),
        out_shape=jax.ShapeDtypeStruct((r_n, c_n), F32),
        compiler_params=_cp("parallel"),
    )(parts)


def _adam_math(w, g, m, v):
    m = ADAM_B1 * m + (1.0 - ADAM_B1) * g
    v = ADAM_B2 * v + (1.0 - ADAM_B2) * jnp.square(g)
    m_hat = m / (1.0 - ADAM_B1 ** ADAM_STEP)
    v_hat = v / (1.0 - ADAM_B2 ** ADAM_STEP)
    return -ADAM_LR * (m_hat / (jnp.sqrt(v_hat) + ADAM_EPS) + ADAM_WD * w), m, v


def _adam_big(w, g, m, v, name):
    r_n, c_n = w.shape
    tr = min(r_n, 256)

    def body(w_ref, g_ref, m_ref, v_ref, d_ref, mo_ref, vo_ref):
        d_ref[...], mo_ref[...], vo_ref[...] = _adam_math(w_ref[...], g_ref[...], m_ref[...], v_ref[...])

    blk = lambda: pl.BlockSpec((tr, c_n), lambda i: (i, 0))
    return pl.pallas_call(
        body, name=name, grid=(r_n // tr,), in_specs=[blk(), blk(), blk(), blk()], out_specs=[blk(), blk(), blk()],
        out_shape=[jax.ShapeDtypeStruct(w.shape, F32)] * 3, compiler_params=_cp("parallel"),
    )(w, g, m, v)


def _adam_small(groups, where, wmv):
    n, n_g = len(wmv), len(groups)

    def body(*refs):
        g_refs = refs[:n_g]
        w_refs = refs[n_g:n_g + 3 * n]
        o_refs = refs[n_g + 3 * n:]
        for q in range(n):
            w_ref, m_ref, v_ref = w_refs[3 * q:3 * q + 3]
            r, c = w_ref.shape
            gi, r0 = where[q]
            g = g_refs[gi][0, r0:r0 + r, 0:c]
            for k in range(1, N_DEV):
                g = g + g_refs[gi][k, r0:r0 + r, 0:c]
            d, m, v = _adam_math(w_ref[...], g, m_ref[...], v_ref[...])
            o_refs[4 * q][...] = g
            o_refs[4 * q + 1][...] = d
            o_refs[4 * q + 2][...] = m
            o_refs[4 * q + 3][...] = v

    flat_wmv = [a for t in wmv for a in t]
    vm = pl.BlockSpec(memory_space=pltpu.VMEM)
    outs = pl.pallas_call(
        body, name="adam_small", in_specs=[vm] * (n_g + 3 * n), out_specs=[vm] * (4 * n),
        out_shape=[jax.ShapeDtypeStruct(t[0].shape, F32) for t in wmv for _ in range(4)],
        compiler_params=pltpu.CompilerParams(vmem_limit_bytes=VMEM_LIMIT),
    )(*groups, *flat_wmv)
    return [tuple(outs[4 * q:4 * q + 4]) for q in range(n)]


WEIGHTS = ["norm_mix_pre", "w_in", "conv_ssm_w", "conv_ssm_b", "dt_bias", "a_log", "d_skip", "ssm_norm", "conv_lru_w",
           "conv_lru_b", "lru_wa", "lru_ba", "lru_wx", "lru_bx", "lru_lambda", "w_out", "norm_mix_post", "norm_mlp_pre",
           "w_up", "w_down", "norm_mlp_post"]
BIG = ["w_out", "w_up", "w_down", "w_in"]
IN_ROWS = W_IN_COLS // N_DEV
ROW_OUT, ROW_UP, ROW_DOWN, ROW_CONV, ROW_IN, PACK_ROWS = 0, 128, 640, 1152, 1184, 2032
GRAD_ROW_IN, GRAD_ROWS = 1152, 2000
CONV_SSM_COLS, CONV_LRU_COLS = 1536 // N_DEV, D // N_DEV
SMALL = [("norm_mix_pre", (1, D), 0, 0), ("ssm_norm", (1, D), 0, 1), ("conv_lru_b", (1, D), 0, 2),
         ("lru_lambda", (1, D), 0, 3), ("norm_mix_post", (1, D), 0, 4), ("norm_mlp_pre", (1, D), 0, 5),
         ("norm_mlp_post", (1, D), 0, 6), ("conv_ssm_b", (1, 1536), 1, 0), ("dt_bias", (1, NH), 2, 0),
         ("a_log", (1, NH), 2, 1), ("d_skip", (1, NH), 2, 2), ("conv_ssm_w", (4, CONV_SSM_COLS), 3, 0),
         ("conv_lru_w", (4, CONV_LRU_COLS), 4, 0), ("lru_wa", (D, 64), 5, 0), ("lru_wx", (D, 64), 5, D),
         ("lru_ba", (NH, 64), 6, 0), ("lru_bx", (NH, 64), 6, NH)]
SMALL_GROUPS = [(8, D), (1, 1536), (8, 128), (4, 1536), (4, D), (2 * D, 64), (2 * NH, 64)]


def _pad_rows(flat, mult):
    n = flat.shape[0]
    rows = -(-n // (128 * mult)) * mult
    return jnp.pad(flat, (0, rows * 128 - n)).reshape(rows, 128)


def _split3(a):
    hi = a.astype(MXU)
    r1 = a - hi.astype(F32)
    mid = r1.astype(MXU)
    lo = (r1 - mid.astype(F32)).astype(MXU)
    return jnp.stack([hi, mid, lo])


def _weights_pack(a):
    bf = lambda t: t.astype(MXU)
    conv = lambda t, c: jnp.pad(_split3(t).reshape(12, c), ((0, 4), (0, D - c)))
    return jnp.concatenate([
        bf(a["w_out"][0]), bf(a["w_up"][0]).T, bf(a["w_down"][0]), conv(a["conv_ssm_w"][0], CONV_SSM_COLS),
        conv(a["conv_lru_w"][0], CONV_LRU_COLS),
        jnp.pad(bf(a["w_in"][0]).T, ((0, PACK_ROWS - ROW_IN - IN_ROWS), (0, 0)))], axis=0)


def _weights_unpack(g):
    big = dict(w_out=g[:, ROW_OUT:ROW_UP].reshape(D, D), w_upT=g[:, ROW_UP:ROW_DOWN].reshape(FF, D),
               w_down=g[:, ROW_DOWN:ROW_CONV].reshape(FF, D),
               w_in_pT=_w_in_t_to_padded(g[:, ROW_IN:ROW_IN + IN_ROWS].reshape(W_IN_COLS, D)))
    conv = {}
    for n, r0, c in (("conv_ssm_w", ROW_CONV, CONV_SSM_COLS), ("conv_lru_w", ROW_CONV + 16, CONV_LRU_COLS)):
        s = g[:, r0:r0 + 12, :c].astype(F32).reshape(N_DEV, 3, 4, c)
        conv[n] = ((s[:, 0] + s[:, 1]) + s[:, 2]).transpose(1, 0, 2).reshape(4, N_DEV * c)
    return big, conv


def kernel(x, norm_mix_pre, w_in, conv_ssm_w, conv_ssm_b, dt_bias, a_log, d_skip, ssm_norm, conv_lru_w, conv_lru_b, lru_wa, lru_ba, lru_wx, lru_bx, lru_lambda, w_out, norm_mix_post, norm_mlp_pre, w_up, w_down, norm_mlp_post, loss_target, m_norm_mix_pre, m_w_in, m_conv_ssm_w, m_conv_ssm_b, m_dt_bias, m_a_log, m_d_skip, m_ssm_norm, m_conv_lru_w, m_conv_lru_b, m_lru_wa, m_lru_ba, m_lru_wx, m_lru_bx, m_lru_lambda, m_w_out, m_norm_mix_post, m_norm_mlp_pre, m_w_up, m_w_down, m_norm_mlp_post, v_norm_mix_pre, v_w_in, v_conv_ssm_w, v_conv_ssm_b, v_dt_bias, v_a_log, v_d_skip, v_ssm_norm, v_conv_lru_w, v_conv_lru_b, v_lru_wa, v_lru_ba, v_lru_wx, v_lru_bx, v_lru_lambda, v_w_out, v_norm_mix_post, v_norm_mlp_pre, v_w_up, v_w_down, v_norm_mlp_post):
    vals = (norm_mix_pre, w_in, conv_ssm_w, conv_ssm_b, dt_bias, a_log, d_skip, ssm_norm, conv_lru_w, conv_lru_b, lru_wa, lru_ba, lru_wx, lru_bx, lru_lambda, w_out, norm_mix_post, norm_mlp_pre, w_up, w_down, norm_mlp_post)
    m_vals = (m_norm_mix_pre, m_w_in, m_conv_ssm_w, m_conv_ssm_b, m_dt_bias, m_a_log, m_d_skip, m_ssm_norm, m_conv_lru_w, m_conv_lru_b, m_lru_wa, m_lru_ba, m_lru_wx, m_lru_bx, m_lru_lambda, m_w_out, m_norm_mix_post, m_norm_mlp_pre, m_w_up, m_w_down, m_norm_mlp_post)
    v_vals = (v_norm_mix_pre, v_w_in, v_conv_ssm_w, v_conv_ssm_b, v_dt_bias, v_a_log, v_d_skip, v_ssm_norm, v_conv_lru_w, v_conv_lru_b, v_lru_wa, v_lru_ba, v_lru_wx, v_lru_bx, v_lru_lambda, v_w_out, v_norm_mix_post, v_norm_mlp_pre, v_w_up, v_w_down, v_norm_mlp_post)
    w = dict(zip(WEIGHTS, vals))
    m = dict(zip(WEIGHTS, m_vals))
    v = dict(zip(WEIGHTS, v_vals))
    me = 4 * lax.axis_index("x") + 2 * lax.axis_index("y") + lax.axis_index("c")

    big_w, conv_w = _weights_unpack(_all_gather(_weights_pack(w)))
    full = {n: (conv_w[n] if n in conv_w else w[n][0]) for n in WEIGHTS if n not in BIG}

    loss, grad_x, g = _local_step(x[0], loss_target[0], _prep_params(full, big_w))
    loss = lax.psum(loss, ("x", "y", "c"))

    bf = lambda t: t.astype(MXU)
    big = jnp.concatenate([
        bf(g["w_out"]).reshape(N_DEV, -1, D), bf(g["w_upT"]).reshape(N_DEV, -1, D), bf(g["w_down"]).reshape(N_DEV, -1, D),
        jnp.pad(bf(_w_in_t_from_padded(g["w_in_pT"])).reshape(N_DEV, IN_ROWS, D),
                ((0, 0), (0, GRAD_ROWS - GRAD_ROW_IN - IN_ROWS), (0, 0)))], axis=1)
    zrow = jnp.zeros((1, D), F32)
    pad16 = lambda a: jnp.pad(a, ((0, 0), (0, 128 - NH)))
    small_parts = [
        jnp.concatenate([g["norm_mix_pre"], g["ssm_norm"], g["conv_lru_b"], g["lru_lambda"], g["norm_mix_post"],
                         g["norm_mlp_pre"], g["norm_mlp_post"], zrow], axis=0),
        g["conv_ssm_b"],
        jnp.concatenate([pad16(g["dt_bias"]), pad16(g["a_log"]), pad16(g["d_skip"]), jnp.zeros((5, 128), F32)], axis=0),
        g["conv_ssm_w"], g["conv_lru_w"],
        jnp.concatenate([g["lru_wa"].reshape(D, 64), g["lru_wx"].reshape(D, 64)], axis=0),
        jnp.concatenate([g["lru_ba"].reshape(NH, 64), g["lru_bx"].reshape(NH, 64)], axis=0)]
    small = _pad_rows(jnp.concatenate([s.reshape(-1) for s in small_parts]), 8)
    big_parts, small_all = _grad_exchange(big, small)

    g_big = _slot_sum(big_parts)
    g_shard = dict(w_out=g_big[ROW_OUT:ROW_UP], w_up=g_big[ROW_UP:ROW_DOWN].T, w_down=g_big[ROW_DOWN:GRAD_ROW_IN],
                   w_in=g_big[GRAD_ROW_IN:GRAD_ROW_IN + IN_ROWS].T)
    out_g, out_d, out_m, out_v = {}, {}, {}, {}
    for n in BIG:
        gn = g_shard[n]
        d_n, m_n, v_n = _adam_big(w[n][0], gn, m[n][0], v[n][0], "adam_" + n)
        out_g[n], out_d[n], out_m[n], out_v[n] = gn[None], d_n[None], m_n[None], v_n[None]
    sflat = small_all.reshape(N_DEV, -1)
    groups = []
    off = 0
    for r, c in SMALL_GROUPS:
        groups.append(sflat[:, off:off + r * c].reshape(N_DEV, r, c))
        off += r * c
    groups[3] = lax.dynamic_slice_in_dim(groups[3], me * CONV_SSM_COLS, CONV_SSM_COLS, axis=2)
    groups[4] = lax.dynamic_slice_in_dim(groups[4], me * CONV_LRU_COLS, CONV_LRU_COLS, axis=2)
    wmv = [(w[n].reshape(s), m[n].reshape(s), v[n].reshape(s)) for n, s, _, _ in SMALL]
    res = _adam_small(groups, [(gi, r0) for _, _, gi, r0 in SMALL], wmv)
    for (n, _, _, _), (g_n, d_n, m_n, v_n) in zip(SMALL, res):
        shape = w[n].shape
        out_g[n], out_d[n], out_m[n], out_v[n] = (g_n.reshape(shape), d_n.reshape(shape), m_n.reshape(shape),
                                                  v_n.reshape(shape))
    return (loss, grad_x[None], *[out_g[n] for n in WEIGHTS], *[out_d[n] for n in WEIGHTS],
            *[out_m[n] for n in WEIGHTS], *[out_v[n] for n in WEIGHTS])
```

```python
import functools

import jax
import jax.numpy as jnp
from jax import lax
from jax.experimental import pallas as pl
from jax.experimental.pallas import tpu as pltpu

F32 = jnp.float32
MXU = jnp.bfloat16
HI = lax.Precision.HIGHEST
EPS = 1e-6

D = 1024
NH = 16
NS = 128
CH = 128
FF = 4096
NP = 7168
SEG = 1024
LRU_C = 8.0
N_DEV = 8

ADAM_LR, ADAM_B1, ADAM_B2, ADAM_EPS, ADAM_WD, ADAM_STEP = 0.001, 0.9, 0.999, 1e-08, 0.01, 10

VMEM_LIMIT = 56 * 1024 * 1024


def _cp(*sem):
    return pltpu.CompilerParams(dimension_semantics=sem, vmem_limit_bytes=VMEM_LIMIT)


def _nn(a, b):
    return jnp.dot(a.astype(MXU), b.astype(MXU), preferred_element_type=F32)


def _nt(a, b):
    return lax.dot_general(a.astype(MXU), b.astype(MXU), (((1,), (1,)), ((), ())), preferred_element_type=F32)


def _tn(a, b):
    return lax.dot_general(a.astype(MXU), b.astype(MXU), (((0,), (0,)), ((), ())), preferred_element_type=F32)


def _silu(x):
    return x * jax.nn.sigmoid(x)


def _dsilu(x):
    s = jax.nn.sigmoid(x)
    return s + x * s * (1.0 - s)


def _softplus(x):
    return jnp.maximum(x, 0.0) + jnp.log(1.0 + jnp.exp(-jnp.abs(x)))


def _rms(x, g):
    r = lax.rsqrt(jnp.mean(x * x, axis=-1, keepdims=True) + EPS)
    return x * r * g


def _rms_bwd(x, g, dy):
    r = lax.rsqrt(jnp.mean(x * x, axis=-1, keepdims=True) + EPS)
    gdy = g * dy
    dx = r * gdy - x * (r * r * r) * jnp.mean(x * gdy, axis=-1, keepdims=True)
    return dx, dy * x * r


def _rowsum(x):
    return jnp.sum(x, axis=0, keepdims=True)


def _taps_past(cur, prev8):
    r_n, c_n = cur.shape
    row = lax.broadcasted_iota(jnp.int32, (r_n, c_n), 0)
    out = []
    for k in range(4):
        s = 3 - k
        if s == 0:
            out.append(cur)
            continue
        head = jnp.concatenate([pltpu.roll(prev8, s, 0), jnp.zeros((r_n - 8, c_n), F32)], axis=0)
        out.append(jnp.where(row < s, head, pltpu.roll(cur, s, 0)))
    return out


def _taps_future(cur, fut8):
    r_n, c_n = cur.shape
    row = lax.broadcasted_iota(jnp.int32, (r_n, c_n), 0)
    out = []
    for k in range(4):
        s = 3 - k
        if s == 0:
            out.append(cur)
            continue
        tail = jnp.concatenate([jnp.zeros((r_n - 8, c_n), F32), pltpu.roll(fut8, 8 - s, 0)], axis=0)
        out.append(jnp.where(row >= r_n - s, tail, pltpu.roll(cur, r_n - s, 0)))
    return out


def _conv_apply(taps, w, b):
    acc = taps[0] * w[0:1, :]
    for k in range(1, 4):
        acc = acc + taps[k] * w[k:k + 1, :]
    return acc + b


def _inproj(x, g0, w):
    t_n = x.shape[0]
    tm = min(t_n, 1024)

    def body(x_ref, g_ref, w_ref, p_ref, u_ref):
        @pl.when(pl.program_id(1) == 0)
        def _():
            u_ref[...] = _rms(x_ref[...], g_ref[...]).astype(MXU)

        p_ref[...] = lax.dot_general(u_ref[...], w_ref[...], (((1,), (1,)), ((), ())), preferred_element_type=F32)

    return pl.pallas_call(
        body, name="inproj", grid=(t_n // tm, NP // SEG),
        in_specs=[pl.BlockSpec((tm, D), lambda i, j: (i, 0)), pl.BlockSpec((1, D), lambda i, j: (0, 0)),
                  pl.BlockSpec((SEG, D), lambda i, j: (j, 0))],
        out_specs=[pl.BlockSpec((tm, SEG), lambda i, j: (i, j)), pl.BlockSpec((tm, D), lambda i, j: (i, 0))],
        out_shape=[jax.ShapeDtypeStruct((t_n, NP), F32), jax.ShapeDtypeStruct((t_n, D), MXU)],
        compiler_params=_cp("parallel", "arbitrary"),
    )(x, g0, w)


def _ssd_prep(dtraw, dtb, alog):
    l_n = dtraw.shape[0]
    r = lax.broadcasted_iota(jnp.int32, (l_n, l_n), 0)
    c = lax.broadcasted_iota(jnp.int32, (l_n, l_n), 1)
    tril = (r >= c).astype(F32)
    triu = (r <= c).astype(F32)
    eye = (r == c).astype(F32)
    dt = _softplus(dtraw + dtb)
    adt = dt * (-jnp.exp(alog))
    ac = jnp.dot(tril, adt, preferred_element_type=F32, precision=HI)
    tn = (((0,), (0,)), ((), ()))
    ac_t = lax.dot_general(adt, triu, tn, preferred_element_type=F32, precision=HI)
    dt_t = lax.dot_general(dt, eye, tn, preferred_element_type=F32, precision=HI)
    return dt, dt_t, ac, ac_t, _rowsum(adt)


def _ssd_pair(j, xp, bg, cg, sp, dt, dt_t, ac, ac_t, aend):
    l_n = xp.shape[0]
    lane = lax.broadcasted_iota(jnp.int32, (l_n, 128), 1)
    sub = lax.broadcasted_iota(jnp.int32, (128, l_n), 0)
    lane1 = lax.broadcasted_iota(jnp.int32, (1, 128), 1)
    tri = lax.broadcasted_iota(jnp.int32, (l_n, l_n), 0) >= lax.broadcasted_iota(jnp.int32, (l_n, l_n), 1)
    cb = _nt(cg, bg)
    cs = _nn(cg, sp)
    ys, ss = [], []
    for e in range(2):
        h = 2 * j + e
        ac_l = jnp.sum(jnp.where(lane == h, ac, 0.0), axis=1, keepdims=True)
        dt_l = jnp.sum(jnp.where(lane == h, dt, 0.0), axis=1, keepdims=True)
        ac_s = jnp.sum(jnp.where(sub == h, ac_t, 0.0), axis=0, keepdims=True)
        dt_s = jnp.sum(jnp.where(sub == h, dt_t, 0.0), axis=0, keepdims=True)
        a_end = jnp.sum(jnp.where(lane1 == h, aend, 0.0), axis=1, keepdims=True)
        decay = jnp.exp(jnp.where(tri, ac_l - ac_s, -1e30))
        w = cb * decay * dt_s
        ys.append(_nn(w, xp) + jnp.exp(ac_l) * cs)
        ws = jnp.exp(a_end - ac_l) * dt_l
        ss.append(_tn(bg * ws, xp) + jnp.exp(a_end) * sp)
    lo = lax.broadcasted_iota(jnp.int32, (l_n, 128), 1) < 64
    lo_s = lax.broadcasted_iota(jnp.int32, (128, 128), 1) < 64
    return jnp.where(lo, ys[0], ys[1]), jnp.where(lo_s, ss[0], ss[1])


def _ssd_post(y, xs, z, dsk, nrm):
    y = (y + dsk * xs) * _silu(z)
    half = D // 2
    ya, yb = y[:, :half], y[:, half:]
    ya = ya * lax.rsqrt(jnp.mean(ya * ya, axis=-1, keepdims=True) + EPS)
    yb = yb * lax.rsqrt(jnp.mean(yb * yb, axis=-1, keepdims=True) + EPS)
    return jnp.concatenate([ya, yb], axis=1) * nrm


def _proj_specs(rows, seg_ids, order):
    return [pl.BlockSpec((rows, SEG), functools.partial(lambda i, s: (order(i), s), s=s)) for s in seg_ids]


def _prev8_specs(rows, seg_ids, order):
    rb = rows // 8
    return [pl.BlockSpec((8, SEG), functools.partial(lambda i, s: (jnp.maximum(order(i) * rb - 1, 0), s), s=s))
            for s in seg_ids]


def _full(shape):
    return pl.BlockSpec(shape, lambda i: (0,) * len(shape))


def _ssd_fwd(proj, cwx, cwb, cbx, cbb, dtb, alog, dsk, nrm):
    t_n = proj.shape[0]
    n_c = t_n // CH
    fwd = lambda i: i

    def body(z_ref, xs_ref, bc_ref, xsp_ref, bcp_ref, cwx_ref, cwb_ref, cbx_ref, cbb_ref, dtb_ref, alog_ref,
             dsk_ref, nrm_ref, ya_ref, sprev_ref, s_ref):
        c = pl.program_id(0)

        @pl.when(c == 0)
        def _():
            s_ref[...] = jnp.zeros_like(s_ref)

        keep = jnp.where(c == 0, 0.0, 1.0)
        xs_pre = _conv_apply(_taps_past(xs_ref[...], xsp_ref[...] * keep), cwx_ref[...], cbx_ref[...])
        bc_pre = _conv_apply(_taps_past(bc_ref[:, :512], bcp_ref[:, :512] * keep), cwb_ref[...], cbb_ref[...])
        prep = _ssd_prep(bc_ref[:, 512:640], dtb_ref[...], alog_ref[...])
        xs = _silu(xs_pre)
        bc = _silu(bc_pre)
        sprev_ref[0] = s_ref[...]
        ys = []
        for j in range(NH // 2):
            g = j // 4
            yp, sn = _ssd_pair(j, xs[:, 128 * j:128 * j + 128], bc[:, 128 * g:128 * g + 128],
                               bc[:, 256 + 128 * g:384 + 128 * g], s_ref[:, 128 * j:128 * j + 128], *prep)
            ys.append(yp)
            s_ref[:, 128 * j:128 * j + 128] = sn
        ya_ref[...] = _ssd_post(jnp.concatenate(ys, axis=1), xs, z_ref[...], dsk_ref[...], nrm_ref[...])

    return pl.pallas_call(
        body, name="ssd_fwd", grid=(n_c,),
        in_specs=_proj_specs(CH, (0, 5, 6), fwd) + _prev8_specs(CH, (5, 6), fwd) + [
            _full((4, D)), _full((4, 512)), _full((1, D)), _full((1, 512)), _full((1, 128)), _full((1, 128)),
            _full((1, D)), _full((1, D))],
        out_specs=[pl.BlockSpec((CH, D), lambda i: (i, 0)), pl.BlockSpec((1, NS, D), lambda i: (i, 0, 0))],
        out_shape=[jax.ShapeDtypeStruct((t_n, D), F32), jax.ShapeDtypeStruct((n_c, NS, D), F32)],
        scratch_shapes=[pltpu.VMEM((NS, D), F32)],
        compiler_params=_cp("arbitrary"),
    )(proj, proj, proj, proj, proj, cwx, cwb, cbx, cbb, dtb, alog, dsk, nrm)


def _ssd_bwd(dya, proj, sprev, cwx, cwb, cbx, cbb, dtb, alog, dsk, nrm):
    t_n = proj.shape[0]
    n_c = t_n // CH
    rev = lambda i: n_c - 1 - i

    def body(dya_ref, z_ref, xs_ref, bc_ref, xsp_ref, bcp_ref, sprev_ref, cwx_ref, cwb_ref, cbx_ref, cbb_ref,
             dtb_ref, alog_ref, dsk_ref, nrm_ref,
             dz_ref, dxs_ref, dbc_ref, dcwx_ref, dcwb_ref, dcbx_ref, dcbb_ref, ddtb_ref, dalog_ref, ddsk_ref,
             dnrm_ref, ds_ref, futx_ref, futb_ref):
        i = pl.program_id(0)
        acc_refs = (dcwx_ref, dcwb_ref, dcbx_ref, dcbb_ref, ddtb_ref, dalog_ref, ddsk_ref, dnrm_ref)

        @pl.when(i == 0)
        def _():
            for r in (ds_ref, futx_ref, futb_ref) + acc_refs:
                r[...] = jnp.zeros_like(r)

        keep = jnp.where(i == n_c - 1, 0.0, 1.0)
        taps_x = _taps_past(xs_ref[...], xsp_ref[...] * keep)
        taps_b = _taps_past(bc_ref[:, :512], bcp_ref[:, :512] * keep)
        xs_pre = _conv_apply(taps_x, cwx_ref[...], cbx_ref[...])
        bc_pre = _conv_apply(taps_b, cwb_ref[...], cbb_ref[...])
        xs = _silu(xs_pre)
        bc = _silu(bc_pre)
        prep, prep_vjp = jax.vjp(_ssd_prep, bc_ref[:, 512:640], dtb_ref[...], alog_ref[...])
        s_in = sprev_ref[0]

        def pair_args(j):
            g = j // 4
            return (xs[:, 128 * j:128 * j + 128], bc[:, 128 * g:128 * g + 128],
                    bc[:, 256 + 128 * g:384 + 128 * g], s_in[:, 128 * j:128 * j + 128]) + tuple(prep)

        y = jnp.concatenate([_ssd_pair(j, *pair_args(j))[0] for j in range(NH // 2)], axis=1)
        _, post_vjp = jax.vjp(_ssd_post, y, xs, z_ref[...], dsk_ref[...], nrm_ref[...])
        dy, dxs_skip, dz, ddsk, dnrm = post_vjp(dya_ref[...])
        dz_ref[...] = dz.astype(dz_ref.dtype)
        ddsk_ref[...] += ddsk
        dnrm_ref[...] += dnrm

        dprep = [jnp.zeros_like(p) for p in prep]
        dxp = []
        dbg = [jnp.zeros((CH, 128), F32), jnp.zeros((CH, 128), F32)]
        dcg = [jnp.zeros((CH, 128), F32), jnp.zeros((CH, 128), F32)]
        for j in range(NH // 2):
            g = j // 4
            _, pair_vjp = jax.vjp(functools.partial(_ssd_pair, j), *pair_args(j))
            cts = pair_vjp((dy[:, 128 * j:128 * j + 128], ds_ref[:, 128 * j:128 * j + 128]))
            dxp.append(cts[0])
            dbg[g] = dbg[g] + cts[1]
            dcg[g] = dcg[g] + cts[2]
            ds_ref[:, 128 * j:128 * j + 128] = cts[3]
            dprep = [a + b for a, b in zip(dprep, cts[4:])]
        ddtraw, ddtb, dalog = prep_vjp(tuple(dprep))
        ddtb_ref[...] += ddtb
        dalog_ref[...] += dalog

        dxs_pre = (dxs_skip + jnp.concatenate(dxp, axis=1)) * _dsilu(xs_pre)
        dbc_pre = jnp.concatenate([dbg[0], dbg[1], dcg[0], dcg[1]], axis=1) * _dsilu(bc_pre)
        dcbx_ref[...] += _rowsum(dxs_pre)
        dcbb_ref[...] += _rowsum(dbc_pre)
        for k in range(4):
            dcwx_ref[k:k + 1, :] += _rowsum(dxs_pre * taps_x[k])
            dcwb_ref[k:k + 1, :] += _rowsum(dbc_pre * taps_b[k])
        fx = _taps_future(dxs_pre, futx_ref[...])
        fb = _taps_future(dbc_pre, futb_ref[...])
        cwx = cwx_ref[...]
        cwb = cwb_ref[...]
        dxs_in = fx[0] * cwx[0:1, :]
        dbc_in = fb[0] * cwb[0:1, :]
        for k in range(1, 4):
            dxs_in = dxs_in + fx[k] * cwx[k:k + 1, :]
            dbc_in = dbc_in + fb[k] * cwb[k:k + 1, :]
        futx_ref[...] = dxs_pre[0:8, :]
        futb_ref[...] = dbc_pre[0:8, :]
        dxs_ref[...] = dxs_in.astype(dxs_ref.dtype)
        dbc_ref[...] = jnp.concatenate([dbc_in, ddtraw, jnp.zeros((CH, SEG - 640), F32)], axis=1).astype(dbc_ref.dtype)

    row_out = lambda: pl.BlockSpec((CH, D), lambda i: (rev(i), 0))
    outs = pl.pallas_call(
        body, name="ssd_bwd", grid=(n_c,),
        in_specs=[pl.BlockSpec((CH, D), lambda i: (rev(i), 0))] + _proj_specs(CH, (0, 5, 6), rev)
        + _prev8_specs(CH, (5, 6), rev) + [pl.BlockSpec((1, NS, D), lambda i: (rev(i), 0, 0)),
                                            _full((4, D)), _full((4, 512)), _full((1, D)), _full((1, 512)),
                                            _full((1, 128)), _full((1, 128)), _full((1, D)), _full((1, D))],
        out_specs=[row_out(), row_out(), row_out(), _full((4, D)), _full((4, 512)), _full((1, D)), _full((1, 512)),
                   _full((1, 128)), _full((1, 128)), _full((1, D)), _full((1, D))],
        out_shape=[jax.ShapeDtypeStruct((t_n, D), MXU)] * 3 + [
            jax.ShapeDtypeStruct(s, F32) for s in ((4, D), (4, 512), (1, D), (1, 512), (1, 128), (1, 128), (1, D), (1, D))],
        scratch_shapes=[pltpu.VMEM((NS, D), F32), pltpu.VMEM((8, D), F32), pltpu.VMEM((8, 512), F32)],
        compiler_params=_cp("arbitrary"),
    )(dya, proj, proj, proj, proj, proj, sprev, cwx, cwb, cbx, cbb, dtb, alog, dsk, nrm)
    return outs


LRU_ROWS = 256
LRU_BLK = 256


def _lru_gates(xr, wa, wx, ba, bx, lam):
    pr = jnp.concatenate([_nn(xr[:, LRU_BLK * b:LRU_BLK * (b + 1)], wa[b]) for b in range(D // LRU_BLK)], axis=1) + ba
    pi = jnp.concatenate([_nn(xr[:, LRU_BLK * b:LRU_BLK * (b + 1)], wx[b]) for b in range(D // LRU_BLK)], axis=1) + bx
    log_a = -LRU_C * jax.nn.sigmoid(pr) * _softplus(-lam)
    a = jnp.exp(log_a)
    mult = jnp.sqrt(1.0 - jnp.exp(2.0 * log_a))
    return a, mult * (jax.nn.sigmoid(pi) * xr)


def _lru_out(h, g):
    return h * jax.nn.gelu(g, approximate=True)


def _lru_fwd(proj, cw, cb, wa, wx, ba, bx, lam):
    t_n = proj.shape[0]
    rows = min(LRU_ROWS, t_n)
    fwd = lambda i: i

    def body(g_ref, x_ref, xp_ref, cw_ref, cb_ref, wa_ref, wx_ref, ba_ref, bx_ref, lam_ref, yb_ref, h_ref,
             a_s, u_s, carry):
        i = pl.program_id(0)

        @pl.when(i == 0)
        def _():
            carry[...] = jnp.zeros_like(carry)

        keep = jnp.where(i == 0, 0.0, 1.0)
        xr = _conv_apply(_taps_past(x_ref[...], xp_ref[...] * keep), cw_ref[...], cb_ref[...])
        a, u = _lru_gates(xr, wa_ref[...], wx_ref[...], ba_ref[...], bx_ref[...], lam_ref[...])
        a_s[...] = a
        u_s[...] = u
        row = lax.broadcasted_iota(jnp.int32, (8, D), 0)

        def blk(b, c):
            s = pl.multiple_of(b * 8, 8)
            av = a_s[pl.ds(s, 8), :]
            uv = u_s[pl.ds(s, 8), :]
            for d in (1, 2, 4):
                m = row >= d
                uv = uv + av * jnp.where(m, pltpu.roll(uv, d, 0), 0.0)
                av = av * jnp.where(m, pltpu.roll(av, d, 0), 1.0)
            hv = uv + av * c
            h_ref[pl.ds(s, 8), :] = hv
            return hv[7:8, :]

        carry[0:1, :] = lax.fori_loop(0, rows // 8, blk, carry[0:1, :])
        yb_ref[...] = _lru_out(h_ref[...], g_ref[...])

    return pl.pallas_call(
        body, name="lru_fwd", grid=(t_n // rows,),
        in_specs=_proj_specs(rows, (1, 2), fwd) + _prev8_specs(rows, (2,), fwd) + [
            _full((4, D)), _full((1, D)), _full((4, LRU_BLK, LRU_BLK)), _full((4, LRU_BLK, LRU_BLK)),
            _full((1, D)), _full((1, D)), _full((1, D))],
        out_specs=[pl.BlockSpec((rows, D), lambda i: (i, 0)), pl.BlockSpec((rows, D), lambda i: (i, 0))],
        out_shape=[jax.ShapeDtypeStruct((t_n, D), F32), jax.ShapeDtypeStruct((t_n, D), F32)],
        scratch_shapes=[pltpu.VMEM((rows, D), F32), pltpu.VMEM((rows, D), F32), pltpu.VMEM((8, D), F32)],
        compiler_params=_cp("arbitrary"),
    )(proj, proj, proj, cw, cb, wa, wx, ba, bx, lam)


def _lru_bwd(dyb, proj, h, cw, cb, wa, wx, ba, bx, lam):
    t_n = proj.shape[0]
    rows = min(LRU_ROWS, t_n)
    n_t = t_n // rows
    rev = lambda i: n_t - 1 - i
    rb = rows // 8

    def body(dyb_ref, g_ref, x_ref, xp_ref, h_ref, hp_ref, cw_ref, cb_ref, wa_ref, wx_ref, ba_ref, bx_ref, lam_ref,
             dg_ref, dx_ref, dcw_ref, dcb_ref, dwa_ref, dwx_ref, dba_ref, dbx_ref, dlam_ref,
             a_s, dh_s, hx_s, da_s, du_s, carry, fut):
        i = pl.program_id(0)
        acc_refs = (dcw_ref, dcb_ref, dwa_ref, dwx_ref, dba_ref, dbx_ref, dlam_ref)

        @pl.when(i == 0)
        def _():
            for r in (carry, fut) + acc_refs:
                r[...] = jnp.zeros_like(r)

        keep = jnp.where(i == n_t - 1, 0.0, 1.0)
        taps = _taps_past(x_ref[...], xp_ref[...] * keep)
        xr = _conv_apply(taps, cw_ref[...], cb_ref[...])
        gate_in = (xr, wa_ref[...], wx_ref[...], ba_ref[...], bx_ref[...], lam_ref[...])
        (a, _), gates_vjp = jax.vjp(_lru_gates, *gate_in)
        _, out_vjp = jax.vjp(_lru_out, h_ref[...], g_ref[...])
        dh, dg = out_vjp(dyb_ref[...])
        dg_ref[...] = dg.astype(dg_ref.dtype)
        a_s[...] = a
        dh_s[...] = dh
        hx_s[0:8, :] = hp_ref[...] * keep
        hx_s[8:, :] = h_ref[...]
        row = lax.broadcasted_iota(jnp.int32, (8, D), 0)

        def blk(b, c):
            s = pl.multiple_of((rb - 1 - b) * 8, 8)
            av = a_s[pl.ds(s, 8), :]
            dhv = dh_s[pl.ds(s, 8), :]
            a0 = av
            kv = av * dhv
            for d in (1, 2, 4):
                m = row <= 7 - d
                kv = kv + av * jnp.where(m, pltpu.roll(kv, 8 - d, 0), 0.0)
                av = av * jnp.where(m, pltpu.roll(av, 8 - d, 0), 1.0)
            kv = kv + av * c
            gv = dhv + jnp.where(row < 7, pltpu.roll(kv, 7, 0), c)
            hb = hx_s[pl.ds(s + 8, 8), :]
            hpv = hx_s[pl.ds(s, 8), :]
            hprev = jnp.where(row >= 1, pltpu.roll(hb, 1, 0), hpv[7:8, :])
            du_s[pl.ds(s, 8), :] = gv
            da_s[pl.ds(s, 8), :] = gv * hprev
            del a0
            return kv[0:1, :]

        carry[0:1, :] = lax.fori_loop(0, rb, blk, carry[0:1, :])
        dxr, dwa, dwx, dba, dbx, dlam = gates_vjp((da_s[...], du_s[...]))
        dwa_ref[...] += dwa
        dwx_ref[...] += dwx
        dba_ref[...] += dba
        dbx_ref[...] += dbx
        dlam_ref[...] += dlam
        dcb_ref[...] += _rowsum(dxr)
        for k in range(4):
            dcw_ref[k:k + 1, :] += _rowsum(dxr * taps[k])
        ft = _taps_future(dxr, fut[...])
        cwv = cw_ref[...]
        dx = ft[0] * cwv[0:1, :]
        for k in range(1, 4):
            dx = dx + ft[k] * cwv[k:k + 1, :]
        fut[...] = dxr[0:8, :]
        dx_ref[...] = dx.astype(dx_ref.dtype)

    row_in = lambda: pl.BlockSpec((rows, D), lambda i: (rev(i), 0))
    prev_h = pl.BlockSpec((8, D), lambda i: (jnp.maximum(rev(i) * rb - 1, 0), 0))
    wspec = lambda: _full((4, LRU_BLK, LRU_BLK))
    return pl.pallas_call(
        body, name="lru_bwd", grid=(n_t,),
        in_specs=[row_in()] + _proj_specs(rows, (1, 2), rev) + _prev8_specs(rows, (2,), rev) + [row_in(), prev_h] + [
            _full((4, D)), _full((1, D)), wspec(), wspec(), _full((1, D)), _full((1, D)), _full((1, D))],
        out_specs=[row_in(), row_in(), _full((4, D)), _full((1, D)), wspec(), wspec(), _full((1, D)), _full((1, D)),
                   _full((1, D))],
        out_shape=[jax.ShapeDtypeStruct((t_n, D), MXU)] * 2 + [
            jax.ShapeDtypeStruct(s, F32) for s in ((4, D), (1, D), (4, LRU_BLK, LRU_BLK), (4, LRU_BLK, LRU_BLK),
                                                   (1, D), (1, D), (1, D))],
        scratch_shapes=[pltpu.VMEM((rows, D), F32), pltpu.VMEM((rows, D), F32), pltpu.VMEM((rows + 8, D), F32),
                        pltpu.VMEM((rows, D), F32), pltpu.VMEM((rows, D), F32), pltpu.VMEM((8, D), F32),
                        pltpu.VMEM((8, D), F32)],
        compiler_params=_cp("arbitrary"),
    )(dyb, proj, proj, proj, h, h, cw, cb, wa, wx, ba, bx, lam)


def _merge_out(ya, yb, proj, x, wout, g1):
    t_n = x.shape[0]
    tm = min(t_n, 512)

    def body(ya_ref, yb_ref, ga_ref, gb_ref, x_ref, w_ref, g_ref, h1_ref, mix_ref, mg_ref):
        merged = jax.nn.sigmoid(ga_ref[...]) * ya_ref[...] + jax.nn.sigmoid(gb_ref[...]) * yb_ref[...]
        mg = merged.astype(MXU)
        mg_ref[...] = mg
        mix = jnp.dot(mg, w_ref[...], preferred_element_type=F32)
        mix_ref[...] = mix
        h1_ref[...] = x_ref[...] + _rms(mix, g_ref[...])

    row = lambda: pl.BlockSpec((tm, D), lambda i: (i, 0))
    return pl.pallas_call(
        body, name="merge_out", grid=(t_n // tm,),
        in_specs=[row(), row()] + _proj_specs(tm, (3, 4), lambda i: i) + [row(), _full((D, D)), _full((1, D))],
        out_specs=[row(), row(), row()],
        out_shape=[jax.ShapeDtypeStruct((t_n, D), F32), jax.ShapeDtypeStruct((t_n, D), F32),
                   jax.ShapeDtypeStruct((t_n, D), MXU)],
        compiler_params=_cp("parallel"),
    )(ya, yb, proj, proj, x, wout, g1)


def _out_bwd(dh1, mix, ya, yb, proj, wout, g1):
    t_n = dh1.shape[0]
    tm = min(t_n, 512)

    def body(dh1_ref, mix_ref, ya_ref, yb_ref, ga_ref, gb_ref, w_ref, g_ref,
             dmix_ref, dya_ref, dyb_ref, dga_ref, dgb_ref, dg1_ref):
        @pl.when(pl.program_id(0) == 0)
        def _():
            dg1_ref[...] = jnp.zeros_like(dg1_ref)

        dmix, dg_rows = _rms_bwd(mix_ref[...], g_ref[...], dh1_ref[...])
        dg1_ref[...] += _rowsum(dg_rows)
        dmix_b = dmix.astype(MXU)
        dmix_ref[...] = dmix_b
        dmg = lax.dot_general(dmix_b, w_ref[...], (((1,), (1,)), ((), ())), preferred_element_type=F32)
        sa = jax.nn.sigmoid(ga_ref[...])
        sb = jax.nn.sigmoid(gb_ref[...])
        dya_ref[...] = dmg * sa
        dyb_ref[...] = dmg * sb
        dga_ref[...] = (dmg * ya_ref[...] * sa * (1.0 - sa)).astype(MXU)
        dgb_ref[...] = (dmg * yb_ref[...] * sb * (1.0 - sb)).astype(MXU)

    row = lambda: pl.BlockSpec((tm, D), lambda i: (i, 0))
    return pl.pallas_call(
        body, name="out_bwd", grid=(t_n // tm,),
        in_specs=[row(), row(), row(), row()] + _proj_specs(tm, (3, 4), lambda i: i) + [_full((D, D)), _full((1, D))],
        out_specs=[row(), row(), row(), row(), row(), _full((1, D))],
        out_shape=[jax.ShapeDtypeStruct((t_n, D), MXU), jax.ShapeDtypeStruct((t_n, D), F32),
                   jax.ShapeDtypeStruct((t_n, D), F32), jax.ShapeDtypeStruct((t_n, D), MXU),
                   jax.ShapeDtypeStruct((t_n, D), MXU), jax.ShapeDtypeStruct((1, D), F32)],
        compiler_params=_cp("arbitrary"),
    )(dh1, mix, ya, yb, proj, proj, wout, g1)


MLP_TF = 512


def _mlp_fwd(h1, g2, wup, wdown, g3, tgt):
    t_n = h1.shape[0]
    tm = min(t_n, 512)
    n_f = FF // MLP_TF

    def body(h1_ref, g2_ref, wu_ref, wd_ref, g3_ref, tgt_ref, hp_ref, v_ref, ff_ref, loss_ref, acc):
        i, j = pl.program_id(0), pl.program_id(1)

        @pl.when((i == 0) & (j == 0))
        def _():
            loss_ref[...] = jnp.zeros_like(loss_ref)

        @pl.when(j == 0)
        def _():
            v_ref[...] = _rms(h1_ref[...], g2_ref[...]).astype(MXU)
            acc[...] = jnp.zeros_like(acc)

        hp = lax.dot_general(v_ref[...], wu_ref[...], (((1,), (1,)), ((), ())), preferred_element_type=F32)
        hp_ref[...] = hp
        hid = jnp.square(jnp.maximum(hp, 0.0))
        acc[...] += jnp.dot(hid.astype(MXU), wd_ref[...], preferred_element_type=F32)

        @pl.when(j == n_f - 1)
        def _():
            ff = acc[...]
            ff_ref[...] = ff
            err = h1_ref[...] + _rms(ff, g3_ref[...]) - tgt_ref[...]
            loss_ref[...] += 0.5 * jnp.sum(jnp.mean(err * err, axis=-1, keepdims=True), axis=0, keepdims=True)

    row = lambda: pl.BlockSpec((tm, D), lambda i, j: (i, 0))
    vec = lambda: pl.BlockSpec((1, D), lambda i, j: (0, 0))
    return pl.pallas_call(
        body, name="mlp_fwd", grid=(t_n // tm, n_f),
        in_specs=[row(), vec(), pl.BlockSpec((MLP_TF, D), lambda i, j: (j, 0)),
                  pl.BlockSpec((MLP_TF, D), lambda i, j: (j, 0)), vec(), row()],
        out_specs=[pl.BlockSpec((tm, MLP_TF), lambda i, j: (i, j)), row(), row(),
                   pl.BlockSpec((1, 1), lambda i, j: (0, 0))],
        out_shape=[jax.ShapeDtypeStruct((t_n, FF), F32), jax.ShapeDtypeStruct((t_n, D), MXU),
                   jax.ShapeDtypeStruct((t_n, D), F32), jax.ShapeDtypeStruct((1, 1), F32)],
        scratch_shapes=[pltpu.VMEM((tm, D), F32)],
        compiler_params=_cp("arbitrary", "arbitrary"),
    )(h1, g2, wup, wdown, g3, tgt)


def _mlp_bwd(h1, ff, tgt, hp, wup, wdown, g2, g3):
    t_n = h1.shape[0]
    tm = min(t_n, 512)
    n_f = FF // MLP_TF

    def body(h1_ref, ff_ref, tgt_ref, hp_ref, wu_ref, wd_ref, g2_ref, g3_ref,
             dh1_ref, dff_ref, dhp_ref, hid_ref, dg2_ref, dg3_ref, acc, dout_s):
        i, j = pl.program_id(0), pl.program_id(1)

        @pl.when((i == 0) & (j == 0))
        def _():
            dg2_ref[...] = jnp.zeros_like(dg2_ref)
            dg3_ref[...] = jnp.zeros_like(dg3_ref)

        @pl.when(j == 0)
        def _():
            ff = ff_ref[...]
            dout = (h1_ref[...] + _rms(ff, g3_ref[...]) - tgt_ref[...]) * (1.0 / D)
            dout_s[...] = dout
            dff, dg_rows = _rms_bwd(ff, g3_ref[...], dout)
            dg3_ref[...] += _rowsum(dg_rows)
            dff_ref[...] = dff.astype(MXU)
            acc[...] = jnp.zeros_like(acc)

        hp = hp_ref[...]
        relu = jnp.maximum(hp, 0.0)
        hid_ref[...] = jnp.square(relu).astype(MXU)
        dhid = lax.dot_general(dff_ref[...], wd_ref[...], (((1,), (1,)), ((), ())), preferred_element_type=F32)
        dhp = (dhid * (2.0 * relu)).astype(MXU)
        dhp_ref[...] = dhp
        acc[...] += jnp.dot(dhp, wu_ref[...], preferred_element_type=F32)

        @pl.when(j == n_f - 1)
        def _():
            dv, dg_rows = _rms_bwd(h1_ref[...], g2_ref[...], acc[...])
            dg2_ref[...] += _rowsum(dg_rows)
            dh1_ref[...] = dout_s[...] + dv

    row = lambda: pl.BlockSpec((tm, D), lambda i, j: (i, 0))
    vec = lambda: pl.BlockSpec((1, D), lambda i, j: (0, 0))
    blk = lambda: pl.BlockSpec((tm, MLP_TF), lambda i, j: (i, j))
    return pl.pallas_call(
        body, name="mlp_bwd", grid=(t_n // tm, n_f),
        in_specs=[row(), row(), row(), blk(), pl.BlockSpec((MLP_TF, D), lambda i, j: (j, 0)),
                  pl.BlockSpec((MLP_TF, D), lambda i, j: (j, 0)), vec(), vec()],
        out_specs=[row(), row(), blk(), blk(), vec(), vec()],
        out_shape=[jax.ShapeDtypeStruct((t_n, D), F32), jax.ShapeDtypeStruct((t_n, D), MXU),
                   jax.ShapeDtypeStruct((t_n, FF), MXU), jax.ShapeDtypeStruct((t_n, FF), MXU),
                   jax.ShapeDtypeStruct((1, D), F32), jax.ShapeDtypeStruct((1, D), F32)],
        scratch_shapes=[pltpu.VMEM((tm, D), F32), pltpu.VMEM((tm, D), F32)],
        compiler_params=_cp("arbitrary", "arbitrary"),
    )(h1, ff, tgt, hp, wup, wdown, g2, g3)


def _wgrad(a, g, name):
    t_n, k_n = a.shape
    n_n = g.shape[1]
    tt = min(t_n, 1024)
    tk, tn = min(k_n, 1024), min(n_n, 1024)

    def body(a_ref, g_ref, o_ref):
        @pl.when(pl.program_id(2) == 0)
        def _():
            o_ref[...] = jnp.zeros_like(o_ref)

        o_ref[...] += lax.dot_general(a_ref[...], g_ref[...], (((0,), (0,)), ((), ())), preferred_element_type=F32)

    return pl.pallas_call(
        body, name=name, grid=(k_n // tk, n_n // tn, t_n // tt),
        in_specs=[pl.BlockSpec((tt, tk), lambda k, n, t: (t, k)), pl.BlockSpec((tt, tn), lambda k, n, t: (t, n))],
        out_specs=pl.BlockSpec((tk, tn), lambda k, n, t: (k, n)),
        out_shape=jax.ShapeDtypeStruct((k_n, n_n), F32),
        compiler_params=_cp("parallel", "parallel", "arbitrary"),
    )(a, g)


def _inproj_bwd(dproj, w, x, g0, dh1):
    t_n = x.shape[0]
    tm = min(t_n, 1024)
    n_k = NP // SEG

    def body(dp_ref, w_ref, x_ref, g_ref, dh1_ref, dx_ref, dg0_ref, acc):
        i, k = pl.program_id(0), pl.program_id(1)

        @pl.when((i == 0) & (k == 0))
        def _():
            dg0_ref[...] = jnp.zeros_like(dg0_ref)

        @pl.when(k == 0)
        def _():
            acc[...] = jnp.zeros_like(acc)

        acc[...] += jnp.dot(dp_ref[...], w_ref[...], preferred_element_type=F32)

        @pl.when(k == n_k - 1)
        def _():
            dx, dg_rows = _rms_bwd(x_ref[...], g_ref[...], acc[...])
            dg0_ref[...] += _rowsum(dg_rows)
            dx_ref[...] = dh1_ref[...] + dx

    row = lambda: pl.BlockSpec((tm, D), lambda i, k: (i, 0))
    vec = lambda: pl.BlockSpec((1, D), lambda i, k: (0, 0))
    return pl.pallas_call(
        body, name="inproj_bwd", grid=(t_n // tm, n_k),
        in_specs=[pl.BlockSpec((tm, SEG), lambda i, k: (i, k)), pl.BlockSpec((SEG, D), lambda i, k: (k, 0)),
                  row(), vec(), row()],
        out_specs=[row(), vec()],
        out_shape=[jax.ShapeDtypeStruct((t_n, D), F32), jax.ShapeDtypeStruct((1, D), F32)],
        scratch_shapes=[pltpu.VMEM((tm, D), F32)],
        compiler_params=_cp("arbitrary", "arbitrary"),
    )(dproj, w, x, g0, dh1)


def _blockdiag4(w):
    z = jnp.zeros((4, 4, 64, 4, 64), w.dtype)
    w4 = w.reshape(4, 4, 64, 64)
    for q in range(4):
        z = z.at[:, q, :, q, :].set(w4[:, q])
    return z.reshape(4, 256, 256).astype(MXU)


def _blockdiag4_extract(g):
    g5 = g.reshape(4, 4, 64, 4, 64)
    return jnp.stack([g5[:, q, :, q, :] for q in range(4)], axis=1).reshape(NH, 64, 64)


def _local_step(x, tgt, p, late_weights=None, send_mlp_grads=None):
    f = lambda a: a.astype(F32)
    proj, u = _inproj(x, p["norm_mix_pre"], p["w_in_pT"])
    ssm_params = (p["cw_xs"], p["cw_bc"], p["cb_xs"], p["cb_bc"], p["dt_bias"], p["a_log"], p["d_skip_x"],
                  p["ssm_norm"])
    ya, sprev = _ssd_fwd(proj, *ssm_params)
    lru_params = (p["conv_lru_w"], p["conv_lru_b"], p["wa_bd"], p["wx_bd"], p["lru_ba"], p["lru_bx"],
                  p["lru_lambda"])
    yb, h = _lru_fwd(proj, *lru_params)
    if late_weights is not None:
        p = dict(p, **late_weights(yb))
    h1, mix, merged = _merge_out(ya, yb, proj, x, p["w_out"], p["norm_mix_post"])
    hp, v, ff, loss = _mlp_fwd(h1, p["norm_mlp_pre"], p["w_upT"], p["w_down"], p["norm_mlp_post"], tgt)

    dh1, dff, dhp, hid, dg2, dg3 = _mlp_bwd(h1, ff, tgt, hp, p["w_upT"], p["w_down"], p["norm_mlp_pre"],
                                            p["norm_mlp_post"])
    d_w_down = _wgrad(hid, dff, "wgrad_down")
    d_w_up_t = _wgrad(dhp, v, "wgrad_up")
    g1 = p["norm_mix_post"]
    if send_mlp_grads is not None:
        g1 = g1 + send_mlp_grads(d_w_up_t, d_w_down)
    dmix, dya, dyb, dga, dgb, dg1 = _out_bwd(dh1, mix, ya, yb, proj, p["w_out"], g1)
    d_w_out = _wgrad(merged, dmix, "wgrad_out")
    (dz, dxs, dbc, dcwx, dcwb, dcbx, dcbb, ddtb, dalog, ddsk, dnrm) = _ssd_bwd(dya, proj, sprev, *ssm_params)
    (dgl, dxl, dcwl, dcbl, dwa, dwx, dba, dbx, dlam) = _lru_bwd(dyb, proj, h, *lru_params)
    dproj = jnp.concatenate([dz, dgl, dxl, dga, dgb, dxs, dbc], axis=1)
    grad_x, dg0 = _inproj_bwd(dproj, p["w_in_pT"], x, p["norm_mix_pre"], dh1)
    d_w_in_pt = _wgrad(dproj, u, "wgrad_in")
    grads = dict(
        norm_mix_pre=dg0, w_in_pT=d_w_in_pt, conv_ssm_w=jnp.concatenate([dcwx, dcwb], axis=1),
        conv_ssm_b=jnp.concatenate([dcbx, dcbb], axis=1), dt_bias=ddtb[:, :NH], a_log=dalog[:, :NH],
        d_skip=f(ddsk).reshape(NH, 64).sum(axis=1)[None, :], ssm_norm=dnrm, conv_lru_w=dcwl, conv_lru_b=dcbl,
        lru_wa=_blockdiag4_extract(dwa), lru_ba=dba, lru_wx=_blockdiag4_extract(dwx), lru_bx=dbx, lru_lambda=dlam,
        w_out=d_w_out, norm_mix_post=dg1, norm_mlp_pre=dg2, w_upT=d_w_up_t, w_down=d_w_down, norm_mlp_post=dg3)
    return loss[0, 0], grad_x, grads


W_IN_COLS = 6672


def _w_in_t_to_padded(wt):
    z, xs, bc, dt = wt[0:1024], wt[1024:2048], wt[2048:2560], wt[2560:2576]
    gl, xl, ga, gb = wt[2576:3600], wt[3600:4624], wt[4624:5648], wt[5648:6672]
    return jnp.concatenate([z, gl, xl, ga, gb, xs, bc, dt, jnp.zeros((NP - 6672, wt.shape[1]), wt.dtype)], axis=0)


def _w_in_t_from_padded(wp):
    z, gl, xl, ga, gb = (wp[SEG * s:SEG * (s + 1)] for s in range(5))
    xs, bc, dt = wp[5120:6144], wp[6144:6656], wp[6656:6672]
    return jnp.concatenate([z, xs, bc, dt, gl, xl, ga, gb], axis=0)


def _prep_params(full, big):
    f = lambda a: a.astype(F32)
    pad128 = lambda a: jnp.pad(f(a).reshape(1, -1), ((0, 0), (0, 128 - a.size)))
    cw = f(full["conv_ssm_w"])
    cb = f(full["conv_ssm_b"]).reshape(1, -1)
    return dict(
        big, norm_mix_pre=f(full["norm_mix_pre"]).reshape(1, D),
        cw_xs=cw[:, :D], cw_bc=cw[:, D:], cb_xs=cb[:, :D], cb_bc=cb[:, D:],
        dt_bias=pad128(full["dt_bias"]), a_log=pad128(full["a_log"]),
        d_skip_x=jnp.repeat(f(full["d_skip"]).reshape(-1), 64).reshape(1, D), ssm_norm=f(full["ssm_norm"]).reshape(1, D),
        conv_lru_w=f(full["conv_lru_w"]), conv_lru_b=f(full["conv_lru_b"]).reshape(1, D),
        wa_bd=_blockdiag4(full["lru_wa"]), wx_bd=_blockdiag4(full["lru_wx"]),
        lru_ba=f(full["lru_ba"]).reshape(1, D), lru_bx=f(full["lru_bx"]).reshape(1, D),
        lru_lambda=f(full["lru_lambda"]).reshape(1, D),
        norm_mix_post=f(full["norm_mix_post"]).reshape(1, D), norm_mlp_pre=f(full["norm_mlp_pre"]).reshape(1, D),
        norm_mlp_post=f(full["norm_mlp_post"]).reshape(1, D))


MESH_ID = pl.DeviceIdType.MESH
ANY = pl.BlockSpec(memory_space=pl.ANY)


def _my_place():
    x, y, c = lax.axis_index("x"), lax.axis_index("y"), lax.axis_index("c")
    return x, y, c, 4 * x + 2 * y + c


def _peer(x, y, c, k):
    return (x ^ ((k >> 2) & 1), y ^ ((k >> 1) & 1), c ^ (k & 1))


def _all_gather(pack):
    def body(in_ref, out_ref, send_sems, recv_sems, local_sem):
        x, y, c, me = _my_place()
        mine = pltpu.make_async_copy(in_ref, out_ref.at[me], local_sem)
        mine.start()
        copies = []
        for k in range(1, N_DEV):
            cp = pltpu.make_async_remote_copy(
                src_ref=in_ref, dst_ref=out_ref.at[me], send_sem=send_sems.at[k - 1], recv_sem=recv_sems.at[k - 1],
                device_id=_peer(x, y, c, k), device_id_type=MESH_ID)
            cp.start()
            copies.append(cp)
        for cp in copies:
            cp.wait()
        mine.wait()

    return pl.pallas_call(
        body, name="weight_all_gather", in_specs=[ANY], out_specs=ANY,
        out_shape=jax.ShapeDtypeStruct((N_DEV,) + pack.shape, pack.dtype),
        scratch_shapes=[pltpu.SemaphoreType.DMA((N_DEV - 1,)), pltpu.SemaphoreType.DMA((N_DEV - 1,)),
                        pltpu.SemaphoreType.DMA],
    )(pack)


def _grad_exchange(big, small):
    def body(big_ref, small_ref, bout_ref, sout_ref, send_sems, recv_sems, local_sems):
        x, y, c, me = _my_place()
        own_b = pltpu.make_async_copy(big_ref.at[me], bout_ref.at[me], local_sems.at[0])
        own_s = pltpu.make_async_copy(small_ref, sout_ref.at[me], local_sems.at[1])
        own_b.start()
        own_s.start()
        copies = []
        for k in range(1, N_DEV):
            peer = _peer(x, y, c, k)
            cb = pltpu.make_async_remote_copy(
                src_ref=big_ref.at[me ^ k], dst_ref=bout_ref.at[me], send_sem=send_sems.at[2 * k - 2],
                recv_sem=recv_sems.at[2 * k - 2], device_id=peer, device_id_type=MESH_ID)
            cs = pltpu.make_async_remote_copy(
                src_ref=small_ref, dst_ref=sout_ref.at[me], send_sem=send_sems.at[2 * k - 1],
                recv_sem=recv_sems.at[2 * k - 1], device_id=peer, device_id_type=MESH_ID)
            cb.start()
            cs.start()
            copies += [cb, cs]
        for cp in copies:
            cp.wait()
        own_b.wait()
        own_s.wait()

    n_sem = 2 * (N_DEV - 1)
    return pl.pallas_call(
        body, name="grad_exchange", in_specs=[ANY, ANY], out_specs=[ANY, ANY],
        out_shape=[jax.ShapeDtypeStruct(big.shape, big.dtype), jax.ShapeDtypeStruct((N_DEV,) + small.shape, small.dtype)],
        scratch_shapes=[pltpu.SemaphoreType.DMA((n_sem,)), pltpu.SemaphoreType.DMA((n_sem,)),
                        pltpu.SemaphoreType.DMA((2,))],
    )(big, small)


HBM = pl.BlockSpec(memory_space=pltpu.HBM)
SEM = pl.BlockSpec(memory_space=pltpu.SEMAPHORE)
N_PEER = N_DEV - 1


def _peer_copy(k, src_ref, land_ref, sems, scatter):
    x, y, c, me = _my_place()
    return pltpu.make_async_remote_copy(
        src_ref=src_ref.at[me ^ k] if scatter else src_ref, dst_ref=land_ref.at[me], send_sem=sems[k - 1],
        recv_sem=sems[N_PEER + k - 1], device_id=_peer(x, y, c, k), device_id_type=MESH_ID)


def _exchange_start(src, name, scatter):
    rows = src.shape[-2:]

    def body(src_ref, land_ref, *rest):
        sems, token = rest[:2 * N_PEER], rest[2 * N_PEER + 2]
        for k in range(1, N_DEV):
            _peer_copy(k, src_ref, land_ref, sems, scatter).start()
        token[...] = jnp.zeros_like(token)

    land = pltpu.with_memory_space_constraint(lax.empty((N_DEV,) + rows, src.dtype), pltpu.HBM)
    outs = pl.pallas_call(
        body, name=name,
        out_shape=(pltpu.SemaphoreType.DMA(()),) * (2 * N_PEER) + (
            pltpu.HBM(src.shape, src.dtype), pltpu.HBM(land.shape, land.dtype), jax.ShapeDtypeStruct((8, 128), F32)),
        in_specs=(HBM, HBM), out_specs=(SEM,) * (2 * N_PEER) + (HBM, HBM, pl.BlockSpec(memory_space=pltpu.VMEM)),
        input_output_aliases={0: 2 * N_PEER, 1: 2 * N_PEER + 1},
        compiler_params=pltpu.CompilerParams(has_side_effects=pltpu.SideEffectType.DATAFLOW_SIDE_EFFECTING),
    )(pltpu.with_memory_space_constraint(src, pltpu.HBM), land)
    return outs[:2 * N_PEER], outs[2 * N_PEER], outs[2 * N_PEER + 1], outs[2 * N_PEER + 2]


def _exchange_wait(sems, src_thru, land_thru, after, name, scatter):
    def body(src_ref, land_ref, *rest):
        sem_refs = rest[:2 * N_PEER]
        for k in range(1, N_DEV):
            cp = _peer_copy(k, src_ref, land_ref, sem_refs, scatter)
            cp.wait_send()
            cp.wait_recv()

    return pl.pallas_call(
        body, name=name,
        out_shape=(pltpu.HBM(src_thru.shape, src_thru.dtype), pltpu.HBM(land_thru.shape, land_thru.dtype)),
        in_specs=(HBM, HBM) + (SEM,) * (2 * N_PEER) + (pl.BlockSpec(memory_space=pl.ANY),), out_specs=(HBM, HBM),
        input_output_aliases={0: 0, 1: 1},
        compiler_params=pltpu.CompilerParams(has_side_effects=pltpu.SideEffectType.DATAFLOW_SIDE_EFFECTING),
    )(src_thru, land_thru, *sems, after)


def _slot_sum(parts, name):
    r_n, c_n = parts.shape[1:]
    tr = max(t for t in range(16, 513, 16) if r_n % t == 0)

    def body(p_ref, o_ref):
        acc = p_ref[0].astype(F32)
        for k in range(1, N_DEV):
            acc = acc + p_ref[k].astype(F32)
        o_ref[...] = acc

    return pl.pallas_call(
        body, name=name, grid=(r_n // tr,),
        in_specs=[pl.BlockSpec((N_DEV, tr, c_n), lambda i: (0, i, 0))],
        out_specs=pl.BlockSpec((tr, c_n), lambda i: (i, 0)),
        out_shape=jax.ShapeDtypeStruct((r_n, c_n), F32),
        compiler_params=_cp("parallel"),
    )(parts)


def _adam_math(w, g, m, v):
    m = ADAM_B1 * m + (1.0 - ADAM_B1) * g
    v = ADAM_B2 * v + (1.0 - ADAM_B2) * jnp.square(g)
    m_hat = m / (1.0 - ADAM_B1 ** ADAM_STEP)
    v_hat = v / (1.0 - ADAM_B2 ** ADAM_STEP)
    return -ADAM_LR * (m_hat / (jnp.sqrt(v_hat) + ADAM_EPS) + ADAM_WD * w), m, v


def _adam_big(w, g, m, v, name):
    r_n, c_n = w.shape
    tr = min(r_n, 256)

    def body(w_ref, g_ref, m_ref, v_ref, d_ref, mo_ref, vo_ref):
        d_ref[...], mo_ref[...], vo_ref[...] = _adam_math(w_ref[...], g_ref[...], m_ref[...], v_ref[...])

    blk = lambda: pl.BlockSpec((tr, c_n), lambda i: (i, 0))
    return pl.pallas_call(
        body, name=name, grid=(r_n // tr,), in_specs=[blk(), blk(), blk(), blk()], out_specs=[blk(), blk(), blk()],
        out_shape=[jax.ShapeDtypeStruct(w.shape, F32)] * 3, compiler_params=_cp("parallel"),
    )(w, g, m, v)


def _adam_small(groups, where, wmv):
    n, n_g = len(wmv), len(groups)

    def body(*refs):
        g_refs = refs[:n_g]
        w_refs = refs[n_g:n_g + 3 * n]
        o_refs = refs[n_g + 3 * n:]
        for q in range(n):
            w_ref, m_ref, v_ref = w_refs[3 * q:3 * q + 3]
            r, c = w_ref.shape
            gi, r0 = where[q]
            g = g_refs[gi][0, r0:r0 + r, 0:c]
            for k in range(1, N_DEV):
                g = g + g_refs[gi][k, r0:r0 + r, 0:c]
            d, m, v = _adam_math(w_ref[...], g, m_ref[...], v_ref[...])
            o_refs[4 * q][...] = g
            o_refs[4 * q + 1][...] = d
            o_refs[4 * q + 2][...] = m
            o_refs[4 * q + 3][...] = v

    flat_wmv = [a for t in wmv for a in t]
    vm = pl.BlockSpec(memory_space=pltpu.VMEM)
    outs = pl.pallas_call(
        body, name="adam_small", in_specs=[vm] * (n_g + 3 * n), out_specs=[vm] * (4 * n),
        out_shape=[jax.ShapeDtypeStruct(t[0].shape, F32) for t in wmv for _ in range(4)],
        compiler_params=pltpu.CompilerParams(vmem_limit_bytes=VMEM_LIMIT),
    )(*groups, *flat_wmv)
    return [tuple(outs[4 * q:4 * q + 4]) for q in range(n)]


WEIGHTS = ["norm_mix_pre", "w_in", "conv_ssm_w", "conv_ssm_b", "dt_bias", "a_log", "d_skip", "ssm_norm", "conv_lru_w",
           "conv_lru_b", "lru_wa", "lru_ba", "lru_wx", "lru_bx", "lru_lambda", "w_out", "norm_mix_post", "norm_mlp_pre",
           "w_up", "w_down", "norm_mlp_post"]
BIG = ["w_out", "w_up", "w_down", "w_in"]
IN_ROWS = W_IN_COLS // N_DEV
IN_PAD, EARLY_ROWS = 848, 880
ROW_UP, ROW_DOWN, LATE_ROWS = 128, 640, 1152
GRAD_LATE_ROWS = 992
CONV_SSM_COLS, CONV_LRU_COLS = 1536 // N_DEV, D // N_DEV
SMALL = [("norm_mix_pre", (1, D), 0, 0), ("ssm_norm", (1, D), 0, 1), ("conv_lru_b", (1, D), 0, 2),
         ("lru_lambda", (1, D), 0, 3), ("norm_mix_post", (1, D), 0, 4), ("norm_mlp_pre", (1, D), 0, 5),
         ("norm_mlp_post", (1, D), 0, 6), ("conv_ssm_b", (1, 1536), 1, 0), ("dt_bias", (1, NH), 2, 0),
         ("a_log", (1, NH), 2, 1), ("d_skip", (1, NH), 2, 2), ("conv_ssm_w", (4, CONV_SSM_COLS), 3, 0),
         ("conv_lru_w", (4, CONV_LRU_COLS), 4, 0), ("lru_wa", (D, 64), 5, 0), ("lru_wx", (D, 64), 5, D),
         ("lru_ba", (NH, 64), 6, 0), ("lru_bx", (NH, 64), 6, NH)]
SMALL_GROUPS = [(8, D), (1, 1536), (8, 128), (4, 1536), (4, D), (2 * D, 64), (2 * NH, 64)]


def _pad_rows(flat, mult):
    n = flat.shape[0]
    rows = -(-n // (128 * mult)) * mult
    return jnp.pad(flat, (0, rows * 128 - n)).reshape(rows, 128)


def _split3(a):
    hi = a.astype(MXU)
    r1 = a - hi.astype(F32)
    mid = r1.astype(MXU)
    lo = (r1 - mid.astype(F32)).astype(MXU)
    return jnp.stack([hi, mid, lo])


def _early_pack(a):
    bf = lambda t: t.astype(MXU)
    conv = lambda t, c: jnp.pad(_split3(t).reshape(12, c), ((0, 4), (0, D - c)))
    return jnp.concatenate([jnp.pad(bf(a["w_in"][0]).T, ((0, IN_PAD - IN_ROWS), (0, 0))),
                            conv(a["conv_ssm_w"][0], CONV_SSM_COLS), conv(a["conv_lru_w"][0], CONV_LRU_COLS)], axis=0)


def _early_unpack(g):
    w_in_pt = _w_in_t_to_padded(g[:, :IN_ROWS].reshape(W_IN_COLS, D))
    conv = {}
    for n, r0, c in (("conv_ssm_w", IN_PAD, CONV_SSM_COLS), ("conv_lru_w", IN_PAD + 16, CONV_LRU_COLS)):
        s = g[:, r0:r0 + 12, :c].astype(F32).reshape(N_DEV, 3, 4, c)
        conv[n] = ((s[:, 0] + s[:, 1]) + s[:, 2]).transpose(1, 0, 2).reshape(4, N_DEV * c)
    return w_in_pt, conv


def _late_pack(a):
    bf = lambda t: t.astype(MXU)
    return jnp.concatenate([bf(a["w_out"][0]), bf(a["w_up"][0]).T, bf(a["w_down"][0])], axis=0)


def _late_unpack(g):
    return dict(w_out=g[:, :ROW_UP].reshape(D, D), w_upT=g[:, ROW_UP:ROW_DOWN].reshape(FF, D),
                w_down=g[:, ROW_DOWN:].reshape(FF, D))


def _own_slot(land, own):
    me = 4 * lax.axis_index("x") + 2 * lax.axis_index("y") + lax.axis_index("c")
    return lax.dynamic_update_slice_in_dim(land, own[None], me, axis=0)


def kernel(x, norm_mix_pre, w_in, conv_ssm_w, conv_ssm_b, dt_bias, a_log, d_skip, ssm_norm, conv_lru_w, conv_lru_b, lru_wa, lru_ba, lru_wx, lru_bx, lru_lambda, w_out, norm_mix_post, norm_mlp_pre, w_up, w_down, norm_mlp_post, loss_target, m_norm_mix_pre, m_w_in, m_conv_ssm_w, m_conv_ssm_b, m_dt_bias, m_a_log, m_d_skip, m_ssm_norm, m_conv_lru_w, m_conv_lru_b, m_lru_wa, m_lru_ba, m_lru_wx, m_lru_bx, m_lru_lambda, m_w_out, m_norm_mix_post, m_norm_mlp_pre, m_w_up, m_w_down, m_norm_mlp_post, v_norm_mix_pre, v_w_in, v_conv_ssm_w, v_conv_ssm_b, v_dt_bias, v_a_log, v_d_skip, v_ssm_norm, v_conv_lru_w, v_conv_lru_b, v_lru_wa, v_lru_ba, v_lru_wx, v_lru_bx, v_lru_lambda, v_w_out, v_norm_mix_post, v_norm_mlp_pre, v_w_up, v_w_down, v_norm_mlp_post):
    vals = (norm_mix_pre, w_in, conv_ssm_w, conv_ssm_b, dt_bias, a_log, d_skip, ssm_norm, conv_lru_w, conv_lru_b, lru_wa, lru_ba, lru_wx, lru_bx, lru_lambda, w_out, norm_mix_post, norm_mlp_pre, w_up, w_down, norm_mlp_post)
    m_vals = (m_norm_mix_pre, m_w_in, m_conv_ssm_w, m_conv_ssm_b, m_dt_bias, m_a_log, m_d_skip, m_ssm_norm, m_conv_lru_w, m_conv_lru_b, m_lru_wa, m_lru_ba, m_lru_wx, m_lru_bx, m_lru_lambda, m_w_out, m_norm_mix_post, m_norm_mlp_pre, m_w_up, m_w_down, m_norm_mlp_post)
    v_vals = (v_norm_mix_pre, v_w_in, v_conv_ssm_w, v_conv_ssm_b, v_dt_bias, v_a_log, v_d_skip, v_ssm_norm, v_conv_lru_w, v_conv_lru_b, v_lru_wa, v_lru_ba, v_lru_wx, v_lru_bx, v_lru_lambda, v_w_out, v_norm_mix_post, v_norm_mlp_pre, v_w_up, v_w_down, v_norm_mlp_post)
    w = dict(zip(WEIGHTS, vals))
    m = dict(zip(WEIGHTS, m_vals))
    v = dict(zip(WEIGHTS, v_vals))
    me = 4 * lax.axis_index("x") + 2 * lax.axis_index("y") + lax.axis_index("c")

    bf = lambda t: t.astype(MXU)
    late = _late_pack(w)
    early = _all_gather(_early_pack(w))
    late, early = lax.optimization_barrier((late, early))
    late_sems, late_src, late_land, token = _exchange_start(late, "late_weights_start", scatter=False)
    w_in_pt, conv_w = _early_unpack(early)
    full = {n: (conv_w[n] if n in conv_w else w[n][0]) for n in WEIGHTS if n not in BIG}
    full["norm_mix_pre"] = full["norm_mix_pre"] + token[0, 0]

    def late_weights(after):
        src, land = _exchange_wait(late_sems, late_src, late_land, after, "late_weights_wait", scatter=False)
        return _late_unpack(_own_slot(land, src))

    sent = {}

    def send_mlp_grads(d_w_up_t, d_w_down):
        src = jnp.concatenate([bf(d_w_up_t).reshape(N_DEV, -1, D), bf(d_w_down).reshape(N_DEV, -1, D)], axis=1)
        sent["sems"], sent["src"], sent["land"], tok = _exchange_start(src, "mlp_grads_start", scatter=True)
        return tok[0, 0]

    loss, grad_x, g = _local_step(x[0], loss_target[0], _prep_params(full, dict(w_in_pT=w_in_pt)), late_weights,
                                  send_mlp_grads)
    loss = lax.psum(loss, ("x", "y", "c"))

    big = jnp.concatenate([
        bf(g["w_out"]).reshape(N_DEV, -1, D),
        jnp.pad(bf(_w_in_t_from_padded(g["w_in_pT"])).reshape(N_DEV, IN_ROWS, D),
                ((0, 0), (0, GRAD_LATE_ROWS - ROW_UP - IN_ROWS), (0, 0)))], axis=1)
    zrow = jnp.zeros((1, D), F32)
    pad16 = lambda a: jnp.pad(a, ((0, 0), (0, 128 - NH)))
    small_parts = [
        jnp.concatenate([g["norm_mix_pre"], g["ssm_norm"], g["conv_lru_b"], g["lru_lambda"], g["norm_mix_post"],
                         g["norm_mlp_pre"], g["norm_mlp_post"], zrow], axis=0),
        g["conv_ssm_b"],
        jnp.concatenate([pad16(g["dt_bias"]), pad16(g["a_log"]), pad16(g["d_skip"]), jnp.zeros((5, 128), F32)], axis=0),
        g["conv_ssm_w"], g["conv_lru_w"],
        jnp.concatenate([g["lru_wa"].reshape(D, 64), g["lru_wx"].reshape(D, 64)], axis=0),
        jnp.concatenate([g["lru_ba"].reshape(NH, 64), g["lru_bx"].reshape(NH, 64)], axis=0)]
    small = _pad_rows(jnp.concatenate([s.reshape(-1) for s in small_parts]), 8)
    big_parts, small_all = _grad_exchange(big, small)

    mlp_src, mlp_land = _exchange_wait(sent["sems"], sent["src"], sent["land"], big_parts, "mlp_grads_wait", scatter=True)
    g_mlp = _slot_sum(_own_slot(mlp_land, lax.dynamic_index_in_dim(mlp_src, me, keepdims=False)), "slot_sum_mlp")
    g_late = _slot_sum(big_parts, "slot_sum_late")
    g_shard = dict(w_out=g_late[:ROW_UP], w_up=g_mlp[:FF // N_DEV].T, w_down=g_mlp[FF // N_DEV:],
                   w_in=g_late[ROW_UP:ROW_UP + IN_ROWS].T)
    out_g, out_d, out_m, out_v = {}, {}, {}, {}
    for n in BIG:
        gn = g_shard[n]
        d_n, m_n, v_n = _adam_big(w[n][0], gn, m[n][0], v[n][0], "adam_" + n)
        out_g[n], out_d[n], out_m[n], out_v[n] = gn[None], d_n[None], m_n[None], v_n[None]
    sflat = small_all.reshape(N_DEV, -1)
    groups = []
    off = 0
    for r, c in SMALL_GROUPS:
        groups.append(sflat[:, off:off + r * c].reshape(N_DEV, r, c))
        off += r * c
    groups[3] = lax.dynamic_slice_in_dim(groups[3], me * CONV_SSM_COLS, CONV_SSM_COLS, axis=2)
    groups[4] = lax.dynamic_slice_in_dim(groups[4], me * CONV_LRU_COLS, CONV_LRU_COLS, axis=2)
    wmv = [(w[n].reshape(s), m[n].reshape(s), v[n].reshape(s)) for n, s, _, _ in SMALL]
    res = _adam_small(groups, [(gi, r0) for _, _, gi, r0 in SMALL], wmv)
    for (n, _, _, _), (g_n, d_n, m_n, v_n) in zip(SMALL, res):
        shape = w[n].shape
        out_g[n], out_d[n], out_m[n], out_v[n] = (g_n.reshape(shape), d_n.reshape(shape), m_n.reshape(shape),
                                                  v_n.reshape(shape))
    return (loss, grad_x[None], *[out_g[n] for n in WEIGHTS], *[out_d[n] for n in WEIGHTS],
            *[out_m[n] for n in WEIGHTS], *[out_v[n] for n in WEIGHTS])
```

```python
import functools

import jax
import jax.numpy as jnp
from jax import lax
from jax.experimental import pallas as pl
from jax.experimental.pallas import tpu as pltpu

F32 = jnp.float32
MXU = jnp.bfloat16
HI = lax.Precision.HIGHEST
EPS = 1e-6

D = 1024
NH = 16
NS = 128
CH = 128
FF = 4096
NP = 7168
SEG = 1024
LRU_C = 8.0
N_DEV = 8

ADAM_LR, ADAM_B1, ADAM_B2, ADAM_EPS, ADAM_WD, ADAM_STEP = 0.001, 0.9, 0.999, 1e-08, 0.01, 10

VMEM_LIMIT = 56 * 1024 * 1024


def _cp(*sem):
    return pltpu.CompilerParams(dimension_semantics=sem, vmem_limit_bytes=VMEM_LIMIT)


def _nn(a, b):
    return jnp.dot(a.astype(MXU), b.astype(MXU), preferred_element_type=F32)


def _nt(a, b):
    return lax.dot_general(a.astype(MXU), b.astype(MXU), (((1,), (1,)), ((), ())), preferred_element_type=F32)


def _tn(a, b):
    return lax.dot_general(a.astype(MXU), b.astype(MXU), (((0,), (0,)), ((), ())), preferred_element_type=F32)


def _silu(x):
    return x * jax.nn.sigmoid(x)


def _dsilu(x):
    s = jax.nn.sigmoid(x)
    return s + x * s * (1.0 - s)


def _softplus(x):
    return jnp.maximum(x, 0.0) + jnp.log(1.0 + jnp.exp(-jnp.abs(x)))


def _rms(x, g):
    r = lax.rsqrt(jnp.mean(x * x, axis=-1, keepdims=True) + EPS)
    return x * r * g


def _rms_bwd(x, g, dy):
    r = lax.rsqrt(jnp.mean(x * x, axis=-1, keepdims=True) + EPS)
    gdy = g * dy
    dx = r * gdy - x * (r * r * r) * jnp.mean(x * gdy, axis=-1, keepdims=True)
    return dx, dy * x * r


def _rowsum(x):
    return jnp.sum(x, axis=0, keepdims=True)


def _taps_past(cur, prev8):
    r_n, c_n = cur.shape
    row = lax.broadcasted_iota(jnp.int32, (r_n, c_n), 0)
    out = []
    for k in range(4):
        s = 3 - k
        if s == 0:
            out.append(cur)
            continue
        head = jnp.concatenate([pltpu.roll(prev8, s, 0), jnp.zeros((r_n - 8, c_n), F32)], axis=0)
        out.append(jnp.where(row < s, head, pltpu.roll(cur, s, 0)))
    return out


def _taps_future(cur, fut8):
    r_n, c_n = cur.shape
    row = lax.broadcasted_iota(jnp.int32, (r_n, c_n), 0)
    out = []
    for k in range(4):
        s = 3 - k
        if s == 0:
            out.append(cur)
            continue
        tail = jnp.concatenate([jnp.zeros((r_n - 8, c_n), F32), pltpu.roll(fut8, 8 - s, 0)], axis=0)
        out.append(jnp.where(row >= r_n - s, tail, pltpu.roll(cur, r_n - s, 0)))
    return out


def _conv_apply(taps, w, b):
    acc = taps[0] * w[0:1, :]
    for k in range(1, 4):
        acc = acc + taps[k] * w[k:k + 1, :]
    return acc + b


def _inproj(x, g0, w):
    t_n = x.shape[0]
    tm = min(t_n, 1024)

    def body(x_ref, g_ref, w_ref, p_ref, u_ref):
        @pl.when(pl.program_id(1) == 0)
        def _():
            u_ref[...] = _rms(x_ref[...], g_ref[...]).astype(MXU)

        p_ref[...] = lax.dot_general(u_ref[...], w_ref[...], (((1,), (1,)), ((), ())), preferred_element_type=F32)

    return pl.pallas_call(
        body, name="inproj", grid=(t_n // tm, NP // SEG),
        in_specs=[pl.BlockSpec((tm, D), lambda i, j: (i, 0)), pl.BlockSpec((1, D), lambda i, j: (0, 0)),
                  pl.BlockSpec((SEG, D), lambda i, j: (j, 0))],
        out_specs=[pl.BlockSpec((tm, SEG), lambda i, j: (i, j)), pl.BlockSpec((tm, D), lambda i, j: (i, 0))],
        out_shape=[jax.ShapeDtypeStruct((t_n, NP), F32), jax.ShapeDtypeStruct((t_n, D), MXU)],
        compiler_params=_cp("parallel", "arbitrary"),
    )(x, g0, w)


def _ssd_prep(dtraw, dtb, alog):
    l_n = dtraw.shape[0]
    r = lax.broadcasted_iota(jnp.int32, (l_n, l_n), 0)
    c = lax.broadcasted_iota(jnp.int32, (l_n, l_n), 1)
    tril = (r >= c).astype(F32)
    triu = (r <= c).astype(F32)
    eye = (r == c).astype(F32)
    dt = _softplus(dtraw + dtb)
    adt = dt * (-jnp.exp(alog))
    ac = jnp.dot(tril, adt, preferred_element_type=F32, precision=HI)
    tn = (((0,), (0,)), ((), ()))
    ac_t = lax.dot_general(adt, triu, tn, preferred_element_type=F32, precision=HI)
    dt_t = lax.dot_general(dt, eye, tn, preferred_element_type=F32, precision=HI)
    return dt, dt_t, ac, ac_t, _rowsum(adt)


def _ssd_pair(j, xp, bg, cg, sp, dt, dt_t, ac, ac_t, aend):
    l_n = xp.shape[0]
    lane = lax.broadcasted_iota(jnp.int32, (l_n, 128), 1)
    sub = lax.broadcasted_iota(jnp.int32, (128, l_n), 0)
    lane1 = lax.broadcasted_iota(jnp.int32, (1, 128), 1)
    tri = lax.broadcasted_iota(jnp.int32, (l_n, l_n), 0) >= lax.broadcasted_iota(jnp.int32, (l_n, l_n), 1)
    cb = _nt(cg, bg)
    cs = _nn(cg, sp)
    ys, ss = [], []
    for e in range(2):
        h = 2 * j + e
        ac_l = jnp.sum(jnp.where(lane == h, ac, 0.0), axis=1, keepdims=True)
        dt_l = jnp.sum(jnp.where(lane == h, dt, 0.0), axis=1, keepdims=True)
        ac_s = jnp.sum(jnp.where(sub == h, ac_t, 0.0), axis=0, keepdims=True)
        dt_s = jnp.sum(jnp.where(sub == h, dt_t, 0.0), axis=0, keepdims=True)
        a_end = jnp.sum(jnp.where(lane1 == h, aend, 0.0), axis=1, keepdims=True)
        decay = jnp.exp(jnp.where(tri, ac_l - ac_s, -1e30))
        w = cb * decay * dt_s
        ys.append(_nn(w, xp) + jnp.exp(ac_l) * cs)
        ws = jnp.exp(a_end - ac_l) * dt_l
        ss.append(_tn(bg * ws, xp) + jnp.exp(a_end) * sp)
    lo = lax.broadcasted_iota(jnp.int32, (l_n, 128), 1) < 64
    lo_s = lax.broadcasted_iota(jnp.int32, (128, 128), 1) < 64
    return jnp.where(lo, ys[0], ys[1]), jnp.where(lo_s, ss[0], ss[1])


def _ssd_post(y, xs, z, dsk, nrm):
    y = (y + dsk * xs) * _silu(z)
    half = D // 2
    ya, yb = y[:, :half], y[:, half:]
    ya = ya * lax.rsqrt(jnp.mean(ya * ya, axis=-1, keepdims=True) + EPS)
    yb = yb * lax.rsqrt(jnp.mean(yb * yb, axis=-1, keepdims=True) + EPS)
    return jnp.concatenate([ya, yb], axis=1) * nrm


def _proj_specs(rows, seg_ids, order):
    return [pl.BlockSpec((rows, SEG), functools.partial(lambda i, s: (order(i), s), s=s)) for s in seg_ids]


def _prev8_specs(rows, seg_ids, order):
    rb = rows // 8
    return [pl.BlockSpec((8, SEG), functools.partial(lambda i, s: (jnp.maximum(order(i) * rb - 1, 0), s), s=s))
            for s in seg_ids]


def _full(shape):
    return pl.BlockSpec(shape, lambda i: (0,) * len(shape))


def _ssd_fwd(proj, cwx, cwb, cbx, cbb, dtb, alog, dsk, nrm):
    t_n = proj.shape[0]
    n_c = t_n // CH
    fwd = lambda i: i

    def body(z_ref, xs_ref, bc_ref, xsp_ref, bcp_ref, cwx_ref, cwb_ref, cbx_ref, cbb_ref, dtb_ref, alog_ref,
             dsk_ref, nrm_ref, ya_ref, sprev_ref, yraw_ref, s_ref):
        c = pl.program_id(0)

        @pl.when(c == 0)
        def _():
            s_ref[...] = jnp.zeros_like(s_ref)

        keep = jnp.where(c == 0, 0.0, 1.0)
        xs_pre = _conv_apply(_taps_past(xs_ref[...], xsp_ref[...] * keep), cwx_ref[...], cbx_ref[...])
        bc_pre = _conv_apply(_taps_past(bc_ref[:, :512], bcp_ref[:, :512] * keep), cwb_ref[...], cbb_ref[...])
        prep = _ssd_prep(bc_ref[:, 512:640], dtb_ref[...], alog_ref[...])
        xs = _silu(xs_pre)
        bc = _silu(bc_pre)
        sprev_ref[0] = s_ref[...]
        ys = []
        for j in range(NH // 2):
            g = j // 4
            yp, sn = _ssd_pair(j, xs[:, 128 * j:128 * j + 128], bc[:, 128 * g:128 * g + 128],
                               bc[:, 256 + 128 * g:384 + 128 * g], s_ref[:, 128 * j:128 * j + 128], *prep)
            ys.append(yp)
            s_ref[:, 128 * j:128 * j + 128] = sn
        y = jnp.concatenate(ys, axis=1)
        yraw_ref[...] = y
        ya_ref[...] = _ssd_post(y, xs, z_ref[...], dsk_ref[...], nrm_ref[...])

    return pl.pallas_call(
        body, name="ssd_fwd", grid=(n_c,),
        in_specs=_proj_specs(CH, (0, 5, 6), fwd) + _prev8_specs(CH, (5, 6), fwd) + [
            _full((4, D)), _full((4, 512)), _full((1, D)), _full((1, 512)), _full((1, 128)), _full((1, 128)),
            _full((1, D)), _full((1, D))],
        out_specs=[pl.BlockSpec((CH, D), lambda i: (i, 0)), pl.BlockSpec((1, NS, D), lambda i: (i, 0, 0)),
                   pl.BlockSpec((CH, D), lambda i: (i, 0))],
        out_shape=[jax.ShapeDtypeStruct((t_n, D), F32), jax.ShapeDtypeStruct((n_c, NS, D), F32),
                   jax.ShapeDtypeStruct((t_n, D), F32)],
        scratch_shapes=[pltpu.VMEM((NS, D), F32)],
        compiler_params=_cp("arbitrary"),
    )(proj, proj, proj, proj, proj, cwx, cwb, cbx, cbb, dtb, alog, dsk, nrm)


def _ssd_bwd(dya, yraw, proj, sprev, cwx, cwb, cbx, cbb, dtb, alog, dsk, nrm):
    t_n = proj.shape[0]
    n_c = t_n // CH
    rev = lambda i: n_c - 1 - i

    def body(dya_ref, yraw_ref, z_ref, xs_ref, bc_ref, xsp_ref, bcp_ref, sprev_ref, cwx_ref, cwb_ref, cbx_ref, cbb_ref,
             dtb_ref, alog_ref, dsk_ref, nrm_ref,
             dz_ref, dxs_ref, dbc_ref, dcwx_ref, dcwb_ref, dcbx_ref, dcbb_ref, ddtb_ref, dalog_ref, ddsk_ref,
             dnrm_ref, ds_ref, futx_ref, futb_ref):
        i = pl.program_id(0)
        acc_refs = (dcwx_ref, dcwb_ref, dcbx_ref, dcbb_ref, ddtb_ref, dalog_ref, ddsk_ref, dnrm_ref)

        @pl.when(i == 0)
        def _():
            for r in (ds_ref, futx_ref, futb_ref) + acc_refs:
                r[...] = jnp.zeros_like(r)

        keep = jnp.where(i == n_c - 1, 0.0, 1.0)
        taps_x = _taps_past(xs_ref[...], xsp_ref[...] * keep)
        taps_b = _taps_past(bc_ref[:, :512], bcp_ref[:, :512] * keep)
        xs_pre = _conv_apply(taps_x, cwx_ref[...], cbx_ref[...])
        bc_pre = _conv_apply(taps_b, cwb_ref[...], cbb_ref[...])
        xs = _silu(xs_pre)
        bc = _silu(bc_pre)
        prep, prep_vjp = jax.vjp(_ssd_prep, bc_ref[:, 512:640], dtb_ref[...], alog_ref[...])
        s_in = sprev_ref[0]

        def pair_args(j):
            g = j // 4
            return (xs[:, 128 * j:128 * j + 128], bc[:, 128 * g:128 * g + 128],
                    bc[:, 256 + 128 * g:384 + 128 * g], s_in[:, 128 * j:128 * j + 128]) + tuple(prep)

        _, post_vjp = jax.vjp(_ssd_post, yraw_ref[...], xs, z_ref[...], dsk_ref[...], nrm_ref[...])
        dy, dxs_skip, dz, ddsk, dnrm = post_vjp(dya_ref[...])
        dz_ref[...] = dz.astype(dz_ref.dtype)
        ddsk_ref[...] += ddsk
        dnrm_ref[...] += dnrm

        dprep = [jnp.zeros_like(p) for p in prep]
        dxp = []
        dbg = [jnp.zeros((CH, 128), F32), jnp.zeros((CH, 128), F32)]
        dcg = [jnp.zeros((CH, 128), F32), jnp.zeros((CH, 128), F32)]
        for j in range(NH // 2):
            g = j // 4
            _, pair_vjp = jax.vjp(functools.partial(_ssd_pair, j), *pair_args(j))
            cts = pair_vjp((dy[:, 128 * j:128 * j + 128], ds_ref[:, 128 * j:128 * j + 128]))
            dxp.append(cts[0])
            dbg[g] = dbg[g] + cts[1]
            dcg[g] = dcg[g] + cts[2]
            ds_ref[:, 128 * j:128 * j + 128] = cts[3]
            dprep = [a + b for a, b in zip(dprep, cts[4:])]
        ddtraw, ddtb, dalog = prep_vjp(tuple(dprep))
        ddtb_ref[...] += ddtb
        dalog_ref[...] += dalog

        dxs_pre = (dxs_skip + jnp.concatenate(dxp, axis=1)) * _dsilu(xs_pre)
        dbc_pre = jnp.concatenate([dbg[0], dbg[1], dcg[0], dcg[1]], axis=1) * _dsilu(bc_pre)
        dcbx_ref[...] += _rowsum(dxs_pre)
        dcbb_ref[...] += _rowsum(dbc_pre)
        for k in range(4):
            dcwx_ref[k:k + 1, :] += _rowsum(dxs_pre * taps_x[k])
            dcwb_ref[k:k + 1, :] += _rowsum(dbc_pre * taps_b[k])
        fx = _taps_future(dxs_pre, futx_ref[...])
        fb = _taps_future(dbc_pre, futb_ref[...])
        cwx = cwx_ref[...]
        cwb = cwb_ref[...]
        dxs_in = fx[0] * cwx[0:1, :]
        dbc_in = fb[0] * cwb[0:1, :]
        for k in range(1, 4):
            dxs_in = dxs_in + fx[k] * cwx[k:k + 1, :]
            dbc_in = dbc_in + fb[k] * cwb[k:k + 1, :]
        futx_ref[...] = dxs_pre[0:8, :]
        futb_ref[...] = dbc_pre[0:8, :]
        dxs_ref[...] = dxs_in.astype(dxs_ref.dtype)
        dbc_ref[...] = jnp.concatenate([dbc_in, ddtraw, jnp.zeros((CH, SEG - 640), F32)], axis=1).astype(dbc_ref.dtype)

    row_out = lambda: pl.BlockSpec((CH, D), lambda i: (rev(i), 0))
    outs = pl.pallas_call(
        body, name="ssd_bwd", grid=(n_c,),
        in_specs=[row_out(), row_out()] + _proj_specs(CH, (0, 5, 6), rev)
        + _prev8_specs(CH, (5, 6), rev) + [pl.BlockSpec((1, NS, D), lambda i: (rev(i), 0, 0)),
                                            _full((4, D)), _full((4, 512)), _full((1, D)), _full((1, 512)),
                                            _full((1, 128)), _full((1, 128)), _full((1, D)), _full((1, D))],
        out_specs=[row_out(), row_out(), row_out(), _full((4, D)), _full((4, 512)), _full((1, D)), _full((1, 512)),
                   _full((1, 128)), _full((1, 128)), _full((1, D)), _full((1, D))],
        out_shape=[jax.ShapeDtypeStruct((t_n, D), MXU)] * 3 + [
            jax.ShapeDtypeStruct(s, F32) for s in ((4, D), (4, 512), (1, D), (1, 512), (1, 128), (1, 128), (1, D), (1, D))],
        scratch_shapes=[pltpu.VMEM((NS, D), F32), pltpu.VMEM((8, D), F32), pltpu.VMEM((8, 512), F32)],
        compiler_params=_cp("arbitrary"),
    )(dya, yraw, proj, proj, proj, proj, proj, sprev, cwx, cwb, cbx, cbb, dtb, alog, dsk, nrm)
    return outs


LRU_ROWS = 256
LRU_BLK = 256


def _lru_gates(xr, wa, wx, ba, bx, lam):
    pr = jnp.concatenate([_nn(xr[:, LRU_BLK * b:LRU_BLK * (b + 1)], wa[b]) for b in range(D // LRU_BLK)], axis=1) + ba
    pi = jnp.concatenate([_nn(xr[:, LRU_BLK * b:LRU_BLK * (b + 1)], wx[b]) for b in range(D // LRU_BLK)], axis=1) + bx
    log_a = -LRU_C * jax.nn.sigmoid(pr) * _softplus(-lam)
    a = jnp.exp(log_a)
    mult = jnp.sqrt(1.0 - jnp.exp(2.0 * log_a))
    return a, mult * (jax.nn.sigmoid(pi) * xr)


def _lru_out(h, g):
    return h * jax.nn.gelu(g, approximate=True)


def _lru_fwd(proj, cw, cb, wa, wx, ba, bx, lam):
    t_n = proj.shape[0]
    rows = min(LRU_ROWS, t_n)
    fwd = lambda i: i

    def body(g_ref, x_ref, xp_ref, cw_ref, cb_ref, wa_ref, wx_ref, ba_ref, bx_ref, lam_ref, yb_ref, h_ref,
             a_s, u_s, carry):
        i = pl.program_id(0)

        @pl.when(i == 0)
        def _():
            carry[...] = jnp.zeros_like(carry)

        keep = jnp.where(i == 0, 0.0, 1.0)
        xr = _conv_apply(_taps_past(x_ref[...], xp_ref[...] * keep), cw_ref[...], cb_ref[...])
        a, u = _lru_gates(xr, wa_ref[...], wx_ref[...], ba_ref[...], bx_ref[...], lam_ref[...])
        a_s[...] = a
        u_s[...] = u
        row = lax.broadcasted_iota(jnp.int32, (8, D), 0)

        def blk(b, c):
            s = pl.multiple_of(b * 8, 8)
            av = a_s[pl.ds(s, 8), :]
            uv = u_s[pl.ds(s, 8), :]
            for d in (1, 2, 4):
                m = row >= d
                uv = uv + av * jnp.where(m, pltpu.roll(uv, d, 0), 0.0)
                av = av * jnp.where(m, pltpu.roll(av, d, 0), 1.0)
            hv = uv + av * c
            h_ref[pl.ds(s, 8), :] = hv
            return hv[7:8, :]

        carry[0:1, :] = lax.fori_loop(0, rows // 8, blk, carry[0:1, :])
        yb_ref[...] = _lru_out(h_ref[...], g_ref[...])

    return pl.pallas_call(
        body, name="lru_fwd", grid=(t_n // rows,),
        in_specs=_proj_specs(rows, (1, 2), fwd) + _prev8_specs(rows, (2,), fwd) + [
            _full((4, D)), _full((1, D)), _full((4, LRU_BLK, LRU_BLK)), _full((4, LRU_BLK, LRU_BLK)),
            _full((1, D)), _full((1, D)), _full((1, D))],
        out_specs=[pl.BlockSpec((rows, D), lambda i: (i, 0)), pl.BlockSpec((rows, D), lambda i: (i, 0))],
        out_shape=[jax.ShapeDtypeStruct((t_n, D), F32), jax.ShapeDtypeStruct((t_n, D), F32)],
        scratch_shapes=[pltpu.VMEM((rows, D), F32), pltpu.VMEM((rows, D), F32), pltpu.VMEM((8, D), F32)],
        compiler_params=_cp("arbitrary"),
    )(proj, proj, proj, cw, cb, wa, wx, ba, bx, lam)


def _lru_bwd(dyb, proj, h, cw, cb, wa, wx, ba, bx, lam):
    t_n = proj.shape[0]
    rows = min(LRU_ROWS, t_n)
    n_t = t_n // rows
    rev = lambda i: n_t - 1 - i
    rb = rows // 8

    def body(dyb_ref, g_ref, x_ref, xp_ref, h_ref, hp_ref, cw_ref, cb_ref, wa_ref, wx_ref, ba_ref, bx_ref, lam_ref,
             dg_ref, dx_ref, dcw_ref, dcb_ref, dwa_ref, dwx_ref, dba_ref, dbx_ref, dlam_ref,
             a_s, dh_s, hx_s, da_s, du_s, carry, fut):
        i = pl.program_id(0)
        acc_refs = (dcw_ref, dcb_ref, dwa_ref, dwx_ref, dba_ref, dbx_ref, dlam_ref)

        @pl.when(i == 0)
        def _():
            for r in (carry, fut) + acc_refs:
                r[...] = jnp.zeros_like(r)

        keep = jnp.where(i == n_t - 1, 0.0, 1.0)
        taps = _taps_past(x_ref[...], xp_ref[...] * keep)
        xr = _conv_apply(taps, cw_ref[...], cb_ref[...])
        gate_in = (xr, wa_ref[...], wx_ref[...], ba_ref[...], bx_ref[...], lam_ref[...])
        (a, _), gates_vjp = jax.vjp(_lru_gates, *gate_in)
        _, out_vjp = jax.vjp(_lru_out, h_ref[...], g_ref[...])
        dh, dg = out_vjp(dyb_ref[...])
        dg_ref[...] = dg.astype(dg_ref.dtype)
        a_s[...] = a
        dh_s[...] = dh
        hx_s[0:8, :] = hp_ref[...] * keep
        hx_s[8:, :] = h_ref[...]
        row = lax.broadcasted_iota(jnp.int32, (8, D), 0)

        def blk(b, c):
            s = pl.multiple_of((rb - 1 - b) * 8, 8)
            av = a_s[pl.ds(s, 8), :]
            dhv = dh_s[pl.ds(s, 8), :]
            a0 = av
            kv = av * dhv
            for d in (1, 2, 4):
                m = row <= 7 - d
                kv = kv + av * jnp.where(m, pltpu.roll(kv, 8 - d, 0), 0.0)
                av = av * jnp.where(m, pltpu.roll(av, 8 - d, 0), 1.0)
            kv = kv + av * c
            gv = dhv + jnp.where(row < 7, pltpu.roll(kv, 7, 0), c)
            hb = hx_s[pl.ds(s + 8, 8), :]
            hpv = hx_s[pl.ds(s, 8), :]
            hprev = jnp.where(row >= 1, pltpu.roll(hb, 1, 0), hpv[7:8, :])
            du_s[pl.ds(s, 8), :] = gv
            da_s[pl.ds(s, 8), :] = gv * hprev
            del a0
            return kv[0:1, :]

        carry[0:1, :] = lax.fori_loop(0, rb, blk, carry[0:1, :])
        dxr, dwa, dwx, dba, dbx, dlam = gates_vjp((da_s[...], du_s[...]))
        dwa_ref[...] += dwa
        dwx_ref[...] += dwx
        dba_ref[...] += dba
        dbx_ref[...] += dbx
        dlam_ref[...] += dlam
        dcb_ref[...] += _rowsum(dxr)
        for k in range(4):
            dcw_ref[k:k + 1, :] += _rowsum(dxr * taps[k])
        ft = _taps_future(dxr, fut[...])
        cwv = cw_ref[...]
        dx = ft[0] * cwv[0:1, :]
        for k in range(1, 4):
            dx = dx + ft[k] * cwv[k:k + 1, :]
        fut[...] = dxr[0:8, :]
        dx_ref[...] = dx.astype(dx_ref.dtype)

    row_in = lambda: pl.BlockSpec((rows, D), lambda i: (rev(i), 0))
    prev_h = pl.BlockSpec((8, D), lambda i: (jnp.maximum(rev(i) * rb - 1, 0), 0))
    wspec = lambda: _full((4, LRU_BLK, LRU_BLK))
    return pl.pallas_call(
        body, name="lru_bwd", grid=(n_t,),
        in_specs=[row_in()] + _proj_specs(rows, (1, 2), rev) + _prev8_specs(rows, (2,), rev) + [row_in(), prev_h] + [
            _full((4, D)), _full((1, D)), wspec(), wspec(), _full((1, D)), _full((1, D)), _full((1, D))],
        out_specs=[row_in(), row_in(), _full((4, D)), _full((1, D)), wspec(), wspec(), _full((1, D)), _full((1, D)),
                   _full((1, D))],
        out_shape=[jax.ShapeDtypeStruct((t_n, D), MXU)] * 2 + [
            jax.ShapeDtypeStruct(s, F32) for s in ((4, D), (1, D), (4, LRU_BLK, LRU_BLK), (4, LRU_BLK, LRU_BLK),
                                                   (1, D), (1, D), (1, D))],
        scratch_shapes=[pltpu.VMEM((rows, D), F32), pltpu.VMEM((rows, D), F32), pltpu.VMEM((rows + 8, D), F32),
                        pltpu.VMEM((rows, D), F32), pltpu.VMEM((rows, D), F32), pltpu.VMEM((8, D), F32),
                        pltpu.VMEM((8, D), F32)],
        compiler_params=_cp("arbitrary"),
    )(dyb, proj, proj, proj, h, h, cw, cb, wa, wx, ba, bx, lam)


def _merge_out(ya, yb, proj, x, wout, g1):
    t_n = x.shape[0]
    tm = min(t_n, 512)

    def body(ya_ref, yb_ref, ga_ref, gb_ref, x_ref, w_ref, g_ref, h1_ref, mix_ref, mg_ref):
        merged = jax.nn.sigmoid(ga_ref[...]) * ya_ref[...] + jax.nn.sigmoid(gb_ref[...]) * yb_ref[...]
        mg = merged.astype(MXU)
        mg_ref[...] = mg
        mix = jnp.dot(mg, w_ref[...], preferred_element_type=F32)
        mix_ref[...] = mix
        h1_ref[...] = x_ref[...] + _rms(mix, g_ref[...])

    row = lambda: pl.BlockSpec((tm, D), lambda i: (i, 0))
    return pl.pallas_call(
        body, name="merge_out", grid=(t_n // tm,),
        in_specs=[row(), row()] + _proj_specs(tm, (3, 4), lambda i: i) + [row(), _full((D, D)), _full((1, D))],
        out_specs=[row(), row(), row()],
        out_shape=[jax.ShapeDtypeStruct((t_n, D), F32), jax.ShapeDtypeStruct((t_n, D), F32),
                   jax.ShapeDtypeStruct((t_n, D), MXU)],
        compiler_params=_cp("parallel"),
    )(ya, yb, proj, proj, x, wout, g1)


def _out_bwd(dh1, mix, ya, yb, proj, wout, g1):
    t_n = dh1.shape[0]
    tm = min(t_n, 512)

    def body(dh1_ref, mix_ref, ya_ref, yb_ref, ga_ref, gb_ref, w_ref, g_ref,
             dmix_ref, dya_ref, dyb_ref, dga_ref, dgb_ref, dg1_ref):
        @pl.when(pl.program_id(0) == 0)
        def _():
            dg1_ref[...] = jnp.zeros_like(dg1_ref)

        dmix, dg_rows = _rms_bwd(mix_ref[...], g_ref[...], dh1_ref[...])
        dg1_ref[...] += _rowsum(dg_rows)
        dmix_b = dmix.astype(MXU)
        dmix_ref[...] = dmix_b
        dmg = lax.dot_general(dmix_b, w_ref[...], (((1,), (1,)), ((), ())), preferred_element_type=F32)
        sa = jax.nn.sigmoid(ga_ref[...])
        sb = jax.nn.sigmoid(gb_ref[...])
        dya_ref[...] = dmg * sa
        dyb_ref[...] = dmg * sb
        dga_ref[...] = (dmg * ya_ref[...] * sa * (1.0 - sa)).astype(MXU)
        dgb_ref[...] = (dmg * yb_ref[...] * sb * (1.0 - sb)).astype(MXU)

    row = lambda: pl.BlockSpec((tm, D), lambda i: (i, 0))
    return pl.pallas_call(
        body, name="out_bwd", grid=(t_n // tm,),
        in_specs=[row(), row(), row(), row()] + _proj_specs(tm, (3, 4), lambda i: i) + [_full((D, D)), _full((1, D))],
        out_specs=[row(), row(), row(), row(), row(), _full((1, D))],
        out_shape=[jax.ShapeDtypeStruct((t_n, D), MXU), jax.ShapeDtypeStruct((t_n, D), F32),
                   jax.ShapeDtypeStruct((t_n, D), F32), jax.ShapeDtypeStruct((t_n, D), MXU),
                   jax.ShapeDtypeStruct((t_n, D), MXU), jax.ShapeDtypeStruct((1, D), F32)],
        compiler_params=_cp("arbitrary"),
    )(dh1, mix, ya, yb, proj, proj, wout, g1)


MLP_TF = 512


def _mlp_fwd(h1, g2, wup, wdown, g3, tgt):
    t_n = h1.shape[0]
    tm = min(t_n, 512)
    n_f = FF // MLP_TF

    def body(h1_ref, g2_ref, wu_ref, wd_ref, g3_ref, tgt_ref, hp_ref, v_ref, ff_ref, loss_ref, acc):
        i, j = pl.program_id(0), pl.program_id(1)

        @pl.when((i == 0) & (j == 0))
        def _():
            loss_ref[...] = jnp.zeros_like(loss_ref)

        @pl.when(j == 0)
        def _():
            v_ref[...] = _rms(h1_ref[...], g2_ref[...]).astype(MXU)
            acc[...] = jnp.zeros_like(acc)

        hp = lax.dot_general(v_ref[...], wu_ref[...], (((1,), (1,)), ((), ())), preferred_element_type=F32)
        hp_ref[...] = hp
        hid = jnp.square(jnp.maximum(hp, 0.0))
        acc[...] += jnp.dot(hid.astype(MXU), wd_ref[...], preferred_element_type=F32)

        @pl.when(j == n_f - 1)
        def _():
            ff = acc[...]
            ff_ref[...] = ff
            err = h1_ref[...] + _rms(ff, g3_ref[...]) - tgt_ref[...]
            loss_ref[...] += 0.5 * jnp.sum(jnp.mean(err * err, axis=-1, keepdims=True), axis=0, keepdims=True)

    row = lambda: pl.BlockSpec((tm, D), lambda i, j: (i, 0))
    vec = lambda: pl.BlockSpec((1, D), lambda i, j: (0, 0))
    return pl.pallas_call(
        body, name="mlp_fwd", grid=(t_n // tm, n_f),
        in_specs=[row(), vec(), pl.BlockSpec((MLP_TF, D), lambda i, j: (j, 0)),
                  pl.BlockSpec((MLP_TF, D), lambda i, j: (j, 0)), vec(), row()],
        out_specs=[pl.BlockSpec((tm, MLP_TF), lambda i, j: (i, j)), row(), row(),
                   pl.BlockSpec((1, 1), lambda i, j: (0, 0))],
        out_shape=[jax.ShapeDtypeStruct((t_n, FF), F32), jax.ShapeDtypeStruct((t_n, D), MXU),
                   jax.ShapeDtypeStruct((t_n, D), F32), jax.ShapeDtypeStruct((1, 1), F32)],
        scratch_shapes=[pltpu.VMEM((tm, D), F32)],
        compiler_params=_cp("arbitrary", "arbitrary"),
    )(h1, g2, wup, wdown, g3, tgt)


def _mlp_bwd(h1, ff, tgt, hp, wup, wdown, g2, g3):
    t_n = h1.shape[0]
    tm = min(t_n, 512)
    n_f = FF // MLP_TF

    def body(h1_ref, ff_ref, tgt_ref, hp_ref, wu_ref, wd_ref, g2_ref, g3_ref,
             dh1_ref, dff_ref, dhp_ref, hid_ref, dg2_ref, dg3_ref, acc, dout_s):
        i, j = pl.program_id(0), pl.program_id(1)

        @pl.when((i == 0) & (j == 0))
        def _():
            dg2_ref[...] = jnp.zeros_like(dg2_ref)
            dg3_ref[...] = jnp.zeros_like(dg3_ref)

        @pl.when(j == 0)
        def _():
            ff = ff_ref[...]
            dout = (h1_ref[...] + _rms(ff, g3_ref[...]) - tgt_ref[...]) * (1.0 / D)
            dout_s[...] = dout
            dff, dg_rows = _rms_bwd(ff, g3_ref[...], dout)
            dg3_ref[...] += _rowsum(dg_rows)
            dff_ref[...] = dff.astype(MXU)
            acc[...] = jnp.zeros_like(acc)

        hp = hp_ref[...]
        relu = jnp.maximum(hp, 0.0)
        hid_ref[...] = jnp.square(relu).astype(MXU)
        dhid = lax.dot_general(dff_ref[...], wd_ref[...], (((1,), (1,)), ((), ())), preferred_element_type=F32)
        dhp = (dhid * (2.0 * relu)).astype(MXU)
        dhp_ref[...] = dhp
        acc[...] += jnp.dot(dhp, wu_ref[...], preferred_element_type=F32)

        @pl.when(j == n_f - 1)
        def _():
            dv, dg_rows = _rms_bwd(h1_ref[...], g2_ref[...], acc[...])
            dg2_ref[...] += _rowsum(dg_rows)
            dh1_ref[...] = dout_s[...] + dv

    row = lambda: pl.BlockSpec((tm, D), lambda i, j: (i, 0))
    vec = lambda: pl.BlockSpec((1, D), lambda i, j: (0, 0))
    blk = lambda: pl.BlockSpec((tm, MLP_TF), lambda i, j: (i, j))
    return pl.pallas_call(
        body, name="mlp_bwd", grid=(t_n // tm, n_f),
        in_specs=[row(), row(), row(), blk(), pl.BlockSpec((MLP_TF, D), lambda i, j: (j, 0)),
                  pl.BlockSpec((MLP_TF, D), lambda i, j: (j, 0)), vec(), vec()],
        out_specs=[row(), row(), blk(), blk(), vec(), vec()],
        out_shape=[jax.ShapeDtypeStruct((t_n, D), F32), jax.ShapeDtypeStruct((t_n, D), MXU),
                   jax.ShapeDtypeStruct((t_n, FF), MXU), jax.ShapeDtypeStruct((t_n, FF), MXU),
                   jax.ShapeDtypeStruct((1, D), F32), jax.ShapeDtypeStruct((1, D), F32)],
        scratch_shapes=[pltpu.VMEM((tm, D), F32), pltpu.VMEM((tm, D), F32)],
        compiler_params=_cp("arbitrary", "arbitrary"),
    )(h1, ff, tgt, hp, wup, wdown, g2, g3)


def _wgrad(a, g, name):
    t_n, k_n = a.shape
    n_n = g.shape[1]
    tt = min(t_n, 1024)
    tk, tn = min(k_n, 1024), min(n_n, 1024)

    n_t = t_n // tt

    def body(a_ref, g_ref, o_ref, acc):
        t = pl.program_id(2)

        @pl.when(t == 0)
        def _():
            acc[...] = jnp.zeros_like(acc)

        acc[...] += lax.dot_general(a_ref[...], g_ref[...], (((0,), (0,)), ((), ())), preferred_element_type=F32)

        @pl.when(t == n_t - 1)
        def _():
            o_ref[...] = acc[...].astype(o_ref.dtype)

    return pl.pallas_call(
        body, name=name, grid=(k_n // tk, n_n // tn, n_t),
        in_specs=[pl.BlockSpec((tt, tk), lambda k, n, t: (t, k)), pl.BlockSpec((tt, tn), lambda k, n, t: (t, n))],
        out_specs=pl.BlockSpec((tk, tn), lambda k, n, t: (k, n)),
        out_shape=jax.ShapeDtypeStruct((k_n, n_n), MXU),
        scratch_shapes=[pltpu.VMEM((tk, tn), F32)],
        compiler_params=_cp("parallel", "parallel", "arbitrary"),
    )(a, g)


def _inproj_bwd(dproj, w, x, g0, dh1):
    t_n = x.shape[0]
    tm = min(t_n, 1024)
    n_k = NP // SEG

    def body(dp_ref, w_ref, x_ref, g_ref, dh1_ref, dx_ref, dg0_ref, acc):
        i, k = pl.program_id(0), pl.program_id(1)

        @pl.when((i == 0) & (k == 0))
        def _():
            dg0_ref[...] = jnp.zeros_like(dg0_ref)

        @pl.when(k == 0)
        def _():
            acc[...] = jnp.zeros_like(acc)

        acc[...] += jnp.dot(dp_ref[...], w_ref[...], preferred_element_type=F32)

        @pl.when(k == n_k - 1)
        def _():
            dx, dg_rows = _rms_bwd(x_ref[...], g_ref[...], acc[...])
            dg0_ref[...] += _rowsum(dg_rows)
            dx_ref[...] = dh1_ref[...] + dx

    row = lambda: pl.BlockSpec((tm, D), lambda i, k: (i, 0))
    vec = lambda: pl.BlockSpec((1, D), lambda i, k: (0, 0))
    return pl.pallas_call(
        body, name="inproj_bwd", grid=(t_n // tm, n_k),
        in_specs=[pl.BlockSpec((tm, SEG), lambda i, k: (i, k)), pl.BlockSpec((SEG, D), lambda i, k: (k, 0)),
                  row(), vec(), row()],
        out_specs=[row(), vec()],
        out_shape=[jax.ShapeDtypeStruct((t_n, D), F32), jax.ShapeDtypeStruct((1, D), F32)],
        scratch_shapes=[pltpu.VMEM((tm, D), F32)],
        compiler_params=_cp("arbitrary", "arbitrary"),
    )(dproj, w, x, g0, dh1)


def _blockdiag4(w):
    w4 = w.reshape(4, 4, 64, 1, 64).astype(MXU)
    same = (jnp.arange(4)[:, None, None, None] == jnp.arange(4)[None, None, :, None])
    return jnp.where(same[None], w4, jnp.zeros((), MXU)).reshape(4, 256, 256)


def _blockdiag4_extract(g):
    g5 = g.reshape(4, 4, 64, 4, 64)
    return jnp.stack([g5[:, q, :, q, :] for q in range(4)], axis=1).reshape(NH, 64, 64)


def _local_step(x, tgt, p, late_weights=None, send_mlp_grads=None, send_late_grads=None):
    f = lambda a: a.astype(F32)
    proj, u = _inproj(x, p["norm_mix_pre"], p["w_in_pT"])
    ssm_params = (p["cw_xs"], p["cw_bc"], p["cb_xs"], p["cb_bc"], p["dt_bias"], p["a_log"], p["d_skip_x"],
                  p["ssm_norm"])
    ya, sprev, yraw = _ssd_fwd(proj, *ssm_params)
    lru_params = (p["conv_lru_w"], p["conv_lru_b"], p["wa_bd"], p["wx_bd"], p["lru_ba"], p["lru_bx"],
                  p["lru_lambda"])
    yb, h = _lru_fwd(proj, *lru_params)
    if late_weights is not None:
        p = dict(p, **late_weights(yb))
    h1, mix, merged = _merge_out(ya, yb, proj, x, p["w_out"], p["norm_mix_post"])
    hp, v, ff, loss = _mlp_fwd(h1, p["norm_mlp_pre"], p["w_upT"], p["w_down"], p["norm_mlp_post"], tgt)

    dh1, dff, dhp, hid, dg2, dg3 = _mlp_bwd(h1, ff, tgt, hp, p["w_upT"], p["w_down"], p["norm_mlp_pre"],
                                            p["norm_mlp_post"])
    d_w_down = _wgrad(hid, dff, "wgrad_down")
    d_w_up_t = _wgrad(dhp, v, "wgrad_up")
    g1 = p["norm_mix_post"]
    if send_mlp_grads is not None:
        g1 = g1 + send_mlp_grads(d_w_up_t, d_w_down)
    dmix, dya, dyb, dga, dgb, dg1 = _out_bwd(dh1, mix, ya, yb, proj, p["w_out"], g1)
    d_w_out = _wgrad(merged, dmix, "wgrad_out")
    (dz, dxs, dbc, dcwx, dcwb, dcbx, dcbb, ddtb, dalog, ddsk, dnrm) = _ssd_bwd(dya, yraw, proj, sprev, *ssm_params)
    (dgl, dxl, dcwl, dcbl, dwa, dwx, dba, dbx, dlam) = _lru_bwd(dyb, proj, h, *lru_params)
    dproj = jnp.concatenate([dz, dgl, dxl, dga, dgb, dxs, dbc], axis=1)
    d_w_in_pt = _wgrad(dproj, u, "wgrad_in")
    g0 = p["norm_mix_pre"]
    if send_late_grads is not None:
        g0 = g0 + send_late_grads(d_w_out, d_w_in_pt)
    grad_x, dg0 = _inproj_bwd(dproj, p["w_in_pT"], x, g0, dh1)
    grads = dict(
        norm_mix_pre=dg0, w_in_pT=d_w_in_pt, conv_ssm_w=jnp.concatenate([dcwx, dcwb], axis=1),
        conv_ssm_b=jnp.concatenate([dcbx, dcbb], axis=1), dt_bias=ddtb[:, :NH], a_log=dalog[:, :NH],
        d_skip=f(ddsk).reshape(NH, 64).sum(axis=1)[None, :], ssm_norm=dnrm, conv_lru_w=dcwl, conv_lru_b=dcbl,
        lru_wa=_blockdiag4_extract(dwa), lru_ba=dba, lru_wx=_blockdiag4_extract(dwx), lru_bx=dbx, lru_lambda=dlam,
        w_out=d_w_out, norm_mix_post=dg1, norm_mlp_pre=dg2, w_upT=d_w_up_t, w_down=d_w_down, norm_mlp_post=dg3)
    return loss[0, 0], grad_x, grads


W_IN_COLS = 6672


def _w_in_t_to_padded(wt):
    z, xs, bc, dt = wt[0:1024], wt[1024:2048], wt[2048:2560], wt[2560:2576]
    gl, xl, ga, gb = wt[2576:3600], wt[3600:4624], wt[4624:5648], wt[5648:6672]
    return jnp.concatenate([z, gl, xl, ga, gb, xs, bc, dt, jnp.zeros((NP - 6672, wt.shape[1]), wt.dtype)], axis=0)


def _w_in_t_from_padded(wp):
    z, gl, xl, ga, gb = (wp[SEG * s:SEG * (s + 1)] for s in range(5))
    xs, bc, dt = wp[5120:6144], wp[6144:6656], wp[6656:6672]
    return jnp.concatenate([z, xs, bc, dt, gl, xl, ga, gb], axis=0)


def _prep_params(full, big):
    f = lambda a: a.astype(F32)
    pad128 = lambda a: jnp.pad(f(a).reshape(1, -1), ((0, 0), (0, 128 - a.size)))
    cw = f(full["conv_ssm_w"])
    cb = f(full["conv_ssm_b"]).reshape(1, -1)
    return dict(
        big, norm_mix_pre=f(full["norm_mix_pre"]).reshape(1, D),
        cw_xs=cw[:, :D], cw_bc=cw[:, D:], cb_xs=cb[:, :D], cb_bc=cb[:, D:],
        dt_bias=pad128(full["dt_bias"]), a_log=pad128(full["a_log"]),
        d_skip_x=jnp.repeat(f(full["d_skip"]).reshape(-1), 64).reshape(1, D), ssm_norm=f(full["ssm_norm"]).reshape(1, D),
        conv_lru_w=f(full["conv_lru_w"]), conv_lru_b=f(full["conv_lru_b"]).reshape(1, D),
        wa_bd=_blockdiag4(full["lru_wa"]), wx_bd=_blockdiag4(full["lru_wx"]),
        lru_ba=f(full["lru_ba"]).reshape(1, D), lru_bx=f(full["lru_bx"]).reshape(1, D),
        lru_lambda=f(full["lru_lambda"]).reshape(1, D),
        norm_mix_post=f(full["norm_mix_post"]).reshape(1, D), norm_mlp_pre=f(full["norm_mlp_pre"]).reshape(1, D),
        norm_mlp_post=f(full["norm_mlp_post"]).reshape(1, D))


MESH_ID = pl.DeviceIdType.MESH
ANY = pl.BlockSpec(memory_space=pl.ANY)


def _my_place():
    x, y, c = lax.axis_index("x"), lax.axis_index("y"), lax.axis_index("c")
    return x, y, c, 4 * x + 2 * y + c


def _peer(x, y, c, k):
    return (x ^ ((k >> 2) & 1), y ^ ((k >> 1) & 1), c ^ (k & 1))


def _all_gather(pack, name):
    def body(in_ref, out_ref, send_sems, recv_sems, local_sem):
        x, y, c, me = _my_place()
        sibling = (x, y, 1 - c)
        flips = (4, 2, 6)

        def copy(j, block, to, src=None):
            return pltpu.make_async_remote_copy(
                src_ref=out_ref.at[block] if src is None else src, dst_ref=out_ref.at[block],
                send_sem=send_sems.at[j], recv_sem=recv_sems.at[j], device_id=to, device_id_type=MESH_ID)

        mine = pltpu.make_async_copy(in_ref, out_ref.at[me], local_sem)
        mine.start()
        first = [copy(0, me, sibling, src=in_ref)]
        first += [copy(1 + j, me, _peer(x, y, c, k), src=in_ref) for j, k in enumerate(flips)]
        for cp in first:
            cp.start()
        passed = [copy(4 + j, me ^ k, sibling) for j, k in enumerate(flips)]
        for j, k in enumerate(flips):
            copy(1 + j, me ^ k, (x, y, c)).wait_recv()
            passed[j].start()
        copy(0, me ^ 1, (x, y, c)).wait_recv()
        for j, k in enumerate(flips):
            copy(4 + j, me ^ k ^ 1, (x, y, c)).wait_recv()
        for cp in first + passed:
            cp.wait_send()
        mine.wait()

    return pl.pallas_call(
        body, name=name, in_specs=[ANY], out_specs=ANY,
        out_shape=jax.ShapeDtypeStruct((N_DEV,) + pack.shape, pack.dtype),
        scratch_shapes=[pltpu.SemaphoreType.DMA((N_DEV - 1,)), pltpu.SemaphoreType.DMA((N_DEV - 1,)),
                        pltpu.SemaphoreType.DMA],
    )(pack)


HBM = pl.BlockSpec(memory_space=pltpu.HBM)
SEM = pl.BlockSpec(memory_space=pltpu.SEMAPHORE)
N_PEER = N_DEV - 1


def _peer_copy(k, src_ref, land_ref, sems, scatter):
    x, y, c, me = _my_place()
    return pltpu.make_async_remote_copy(
        src_ref=src_ref.at[me ^ k] if scatter else src_ref, dst_ref=land_ref.at[me], send_sem=sems[k - 1],
        recv_sem=sems[N_PEER + k - 1], device_id=_peer(x, y, c, k), device_id_type=MESH_ID)


def _exchange_start(src, name, scatter):
    rows = src.shape[-2:]

    def body(src_ref, land_ref, *rest):
        sems, token = rest[:2 * N_PEER], rest[2 * N_PEER + 2]
        for k in range(1, N_DEV):
            _peer_copy(k, src_ref, land_ref, sems, scatter).start()
        token[...] = jnp.zeros_like(token)

    land = pltpu.with_memory_space_constraint(lax.empty((N_DEV,) + rows, src.dtype), pltpu.HBM)
    outs = pl.pallas_call(
        body, name=name,
        out_shape=(pltpu.SemaphoreType.DMA(()),) * (2 * N_PEER) + (
            pltpu.HBM(src.shape, src.dtype), pltpu.HBM(land.shape, land.dtype), jax.ShapeDtypeStruct((8, 128), F32)),
        in_specs=(HBM, HBM), out_specs=(SEM,) * (2 * N_PEER) + (HBM, HBM, pl.BlockSpec(memory_space=pltpu.VMEM)),
        input_output_aliases={0: 2 * N_PEER, 1: 2 * N_PEER + 1},
        compiler_params=pltpu.CompilerParams(has_side_effects=pltpu.SideEffectType.DATAFLOW_SIDE_EFFECTING),
    )(pltpu.with_memory_space_constraint(src, pltpu.HBM), land)
    return outs[:2 * N_PEER], outs[2 * N_PEER], outs[2 * N_PEER + 1], outs[2 * N_PEER + 2]


def _exchange_wait(sems, src_thru, land_thru, after, name, scatter):
    def body(src_ref, land_ref, *rest):
        sem_refs = rest[:2 * N_PEER]
        for k in range(1, N_DEV):
            cp = _peer_copy(k, src_ref, land_ref, sem_refs, scatter)
            cp.wait_send()
            cp.wait_recv()

    return pl.pallas_call(
        body, name=name,
        out_shape=(pltpu.HBM(src_thru.shape, src_thru.dtype), pltpu.HBM(land_thru.shape, land_thru.dtype)),
        in_specs=(HBM, HBM) + (SEM,) * (2 * N_PEER) + (pl.BlockSpec(memory_space=pl.ANY),), out_specs=(HBM, HBM),
        input_output_aliases={0: 0, 1: 1},
        compiler_params=pltpu.CompilerParams(has_side_effects=pltpu.SideEffectType.DATAFLOW_SIDE_EFFECTING),
    )(src_thru, land_thru, *sems, after)


def _slot_sum(parts, name):
    r_n, c_n = parts.shape[1:]
    tr = max(t for t in range(16, 513, 16) if r_n % t == 0)

    def body(p_ref, o_ref):
        acc = p_ref[0].astype(F32)
        for k in range(1, N_DEV):
            acc = acc + p_ref[k].astype(F32)
        o_ref[...] = acc

    return pl.pallas_call(
        body, name=name, grid=(r_n // tr,),
        in_specs=[pl.BlockSpec((N_DEV, tr, c_n), lambda i: (0, i, 0))],
        out_specs=pl.BlockSpec((tr, c_n), lambda i: (i, 0)),
        out_shape=jax.ShapeDtypeStruct((r_n, c_n), F32),
        compiler_params=_cp("parallel"),
    )(parts)


def _adam_math(w, g, m, v):
    m = ADAM_B1 * m + (1.0 - ADAM_B1) * g
    v = ADAM_B2 * v + (1.0 - ADAM_B2) * jnp.square(g)
    m_hat = m / (1.0 - ADAM_B1 ** ADAM_STEP)
    v_hat = v / (1.0 - ADAM_B2 ** ADAM_STEP)
    return -ADAM_LR * (m_hat / (jnp.sqrt(v_hat) + ADAM_EPS) + ADAM_WD * w), m, v


def _adam_big(w, g, m, v, name):
    r_n, c_n = w.shape
    tr = min(r_n, 256)

    def body(w_ref, g_ref, m_ref, v_ref, d_ref, mo_ref, vo_ref):
        d_ref[...], mo_ref[...], vo_ref[...] = _adam_math(w_ref[...], g_ref[...], m_ref[...], v_ref[...])

    blk = lambda: pl.BlockSpec((tr, c_n), lambda i: (i, 0))
    return pl.pallas_call(
        body, name=name, grid=(r_n // tr,), in_specs=[blk(), blk(), blk(), blk()], out_specs=[blk(), blk(), blk()],
        out_shape=[jax.ShapeDtypeStruct(w.shape, F32)] * 3, compiler_params=_cp("parallel"),
    )(w, g, m, v)


def _adam_small(groups, where, wmv):
    n, n_g = len(wmv), len(groups)

    def body(*refs):
        g_refs = refs[:n_g]
        w_refs = refs[n_g:n_g + 3 * n]
        o_refs = refs[n_g + 3 * n:]
        for q in range(n):
            w_ref, m_ref, v_ref = w_refs[3 * q:3 * q + 3]
            r, c = w_ref.shape
            gi, r0 = where[q]
            g = g_refs[gi][0, r0:r0 + r, 0:c]
            for k in range(1, N_DEV):
                g = g + g_refs[gi][k, r0:r0 + r, 0:c]
            d, m, v = _adam_math(w_ref[...], g, m_ref[...], v_ref[...])
            o_refs[4 * q][...] = g
            o_refs[4 * q + 1][...] = d
            o_refs[4 * q + 2][...] = m
            o_refs[4 * q + 3][...] = v

    flat_wmv = [a for t in wmv for a in t]
    vm = pl.BlockSpec(memory_space=pltpu.VMEM)
    outs = pl.pallas_call(
        body, name="adam_small", in_specs=[vm] * (n_g + 3 * n), out_specs=[vm] * (4 * n),
        out_shape=[jax.ShapeDtypeStruct(t[0].shape, F32) for t in wmv for _ in range(4)],
        compiler_params=pltpu.CompilerParams(vmem_limit_bytes=VMEM_LIMIT),
    )(*groups, *flat_wmv)
    return [tuple(outs[4 * q:4 * q + 4]) for q in range(n)]


WEIGHTS = ["norm_mix_pre", "w_in", "conv_ssm_w", "conv_ssm_b", "dt_bias", "a_log", "d_skip", "ssm_norm", "conv_lru_w",
           "conv_lru_b", "lru_wa", "lru_ba", "lru_wx", "lru_bx", "lru_lambda", "w_out", "norm_mix_post", "norm_mlp_pre",
           "w_up", "w_down", "norm_mlp_post"]
BIG = ["w_out", "w_up", "w_down", "w_in"]
IN_ROWS = W_IN_COLS // N_DEV
IN_PAD, EARLY_ROWS = 848, 880
ROW_UP, ROW_DOWN, LATE_ROWS = 128, 640, 1152
GRAD_LATE_ROWS = 992
CONV_SSM_COLS, CONV_LRU_COLS = 1536 // N_DEV, D // N_DEV
SMALL = [("norm_mix_pre", (1, D), 0, 0), ("ssm_norm", (1, D), 0, 1), ("conv_lru_b", (1, D), 0, 2),
         ("lru_lambda", (1, D), 0, 3), ("norm_mix_post", (1, D), 0, 4), ("norm_mlp_pre", (1, D), 0, 5),
         ("norm_mlp_post", (1, D), 0, 6), ("conv_ssm_b", (1, 1536), 1, 0), ("dt_bias", (1, NH), 2, 0),
         ("a_log", (1, NH), 2, 1), ("d_skip", (1, NH), 2, 2), ("conv_ssm_w", (4, CONV_SSM_COLS), 3, 0),
         ("conv_lru_w", (4, CONV_LRU_COLS), 4, 0), ("lru_wa", (D, 64), 5, 0), ("lru_wx", (D, 64), 5, D),
         ("lru_ba", (NH, 64), 6, 0), ("lru_bx", (NH, 64), 6, NH)]
SMALL_GROUPS = [(8, D), (1, 1536), (8, 128), (4, 1536), (4, D), (2 * D, 64), (2 * NH, 64)]


def _pad_rows(flat, mult):
    n = flat.shape[0]
    rows = -(-n // (128 * mult)) * mult
    return jnp.pad(flat, (0, rows * 128 - n)).reshape(rows, 128)


def _split3(a):
    hi = a.astype(MXU)
    r1 = a - hi.astype(F32)
    mid = r1.astype(MXU)
    lo = (r1 - mid.astype(F32)).astype(MXU)
    return jnp.stack([hi, mid, lo])


def _early_pack(a):
    bf = lambda t: t.astype(MXU)
    conv = lambda t, c: jnp.pad(_split3(t).reshape(12, c), ((0, 4), (0, D - c)))
    return jnp.concatenate([jnp.pad(bf(a["w_in"][0]).T, ((0, IN_PAD - IN_ROWS), (0, 0))),
                            conv(a["conv_ssm_w"][0], CONV_SSM_COLS), conv(a["conv_lru_w"][0], CONV_LRU_COLS)], axis=0)


def _early_unpack(g):
    w_in_pt = _w_in_t_to_padded(g[:, :IN_ROWS].reshape(W_IN_COLS, D))
    conv = {}
    for n, r0, c in (("conv_ssm_w", IN_PAD, CONV_SSM_COLS), ("conv_lru_w", IN_PAD + 16, CONV_LRU_COLS)):
        s = g[:, r0:r0 + 12, :c].astype(F32).reshape(N_DEV, 3, 4, c)
        conv[n] = ((s[:, 0] + s[:, 1]) + s[:, 2]).transpose(1, 0, 2).reshape(4, N_DEV * c)
    return w_in_pt, conv


def _late_pack(a):
    bf = lambda t: t.astype(MXU)
    return jnp.concatenate([bf(a["w_out"][0]), bf(a["w_up"][0]).T, bf(a["w_down"][0])], axis=0)


def _late_unpack(g):
    return dict(w_out=g[:, :ROW_UP].reshape(D, D), w_upT=g[:, ROW_UP:ROW_DOWN].reshape(FF, D),
                w_down=g[:, ROW_DOWN:].reshape(FF, D))


def _own_slot(land, own):
    me = 4 * lax.axis_index("x") + 2 * lax.axis_index("y") + lax.axis_index("c")
    return lax.dynamic_update_slice_in_dim(land, own[None], me, axis=0)


def kernel(x, norm_mix_pre, w_in, conv_ssm_w, conv_ssm_b, dt_bias, a_log, d_skip, ssm_norm, conv_lru_w, conv_lru_b, lru_wa, lru_ba, lru_wx, lru_bx, lru_lambda, w_out, norm_mix_post, norm_mlp_pre, w_up, w_down, norm_mlp_post, loss_target, m_norm_mix_pre, m_w_in, m_conv_ssm_w, m_conv_ssm_b, m_dt_bias, m_a_log, m_d_skip, m_ssm_norm, m_conv_lru_w, m_conv_lru_b, m_lru_wa, m_lru_ba, m_lru_wx, m_lru_bx, m_lru_lambda, m_w_out, m_norm_mix_post, m_norm_mlp_pre, m_w_up, m_w_down, m_norm_mlp_post, v_norm_mix_pre, v_w_in, v_conv_ssm_w, v_conv_ssm_b, v_dt_bias, v_a_log, v_d_skip, v_ssm_norm, v_conv_lru_w, v_conv_lru_b, v_lru_wa, v_lru_ba, v_lru_wx, v_lru_bx, v_lru_lambda, v_w_out, v_norm_mix_post, v_norm_mlp_pre, v_w_up, v_w_down, v_norm_mlp_post):
    vals = (norm_mix_pre, w_in, conv_ssm_w, conv_ssm_b, dt_bias, a_log, d_skip, ssm_norm, conv_lru_w, conv_lru_b, lru_wa, lru_ba, lru_wx, lru_bx, lru_lambda, w_out, norm_mix_post, norm_mlp_pre, w_up, w_down, norm_mlp_post)
    m_vals = (m_norm_mix_pre, m_w_in, m_conv_ssm_w, m_conv_ssm_b, m_dt_bias, m_a_log, m_d_skip, m_ssm_norm, m_conv_lru_w, m_conv_lru_b, m_lru_wa, m_lru_ba, m_lru_wx, m_lru_bx, m_lru_lambda, m_w_out, m_norm_mix_post, m_norm_mlp_pre, m_w_up, m_w_down, m_norm_mlp_post)
    v_vals = (v_norm_mix_pre, v_w_in, v_conv_ssm_w, v_conv_ssm_b, v_dt_bias, v_a_log, v_d_skip, v_ssm_norm, v_conv_lru_w, v_conv_lru_b, v_lru_wa, v_lru_ba, v_lru_wx, v_lru_bx, v_lru_lambda, v_w_out, v_norm_mix_post, v_norm_mlp_pre, v_w_up, v_w_down, v_norm_mlp_post)
    w = dict(zip(WEIGHTS, vals))
    m = dict(zip(WEIGHTS, m_vals))
    v = dict(zip(WEIGHTS, v_vals))
    me = 4 * lax.axis_index("x") + 2 * lax.axis_index("y") + lax.axis_index("c")

    bf = lambda t: t.astype(MXU)
    late = _late_pack(w)
    early = _all_gather(_early_pack(w), "early_weights_all_gather")
    late, early = lax.optimization_barrier((late, early))
    late_sems, late_src, late_land, token = _exchange_start(late, "late_weights_start", scatter=False)
    w_in_pt, conv_w = _early_unpack(early)
    full = {n: (conv_w[n] if n in conv_w else w[n][0]) for n in WEIGHTS if n not in BIG}
    full["norm_mix_pre"] = full["norm_mix_pre"] + token[0, 0]

    def late_weights(after):
        src, land = _exchange_wait(late_sems, late_src, late_land, after, "late_weights_wait", scatter=False)
        return _late_unpack(_own_slot(land, src))

    sent = {}

    def send_mlp_grads(d_w_up_t, d_w_down):
        src = jnp.concatenate([bf(d_w_up_t).reshape(N_DEV, -1, D), bf(d_w_down).reshape(N_DEV, -1, D)], axis=1)
        sent["sems"], sent["src"], sent["land"], tok = _exchange_start(src, "mlp_grads_start", scatter=True)
        return tok[0, 0]

    def send_late_grads(d_w_out, d_w_in_pt):
        src = jnp.concatenate([
            bf(d_w_out).reshape(N_DEV, -1, D),
            jnp.pad(bf(_w_in_t_from_padded(d_w_in_pt)).reshape(N_DEV, IN_ROWS, D),
                    ((0, 0), (0, GRAD_LATE_ROWS - ROW_UP - IN_ROWS), (0, 0)))], axis=1)
        sent["sems2"], sent["src2"], sent["land2"], tok = _exchange_start(src, "late_grads_start", scatter=True)
        return tok[0, 0]

    loss, grad_x, g = _local_step(x[0], loss_target[0], _prep_params(full, dict(w_in_pT=w_in_pt)), late_weights,
                                  send_mlp_grads, send_late_grads)
    loss = lax.psum(loss, ("x", "y", "c"))

    zrow = jnp.zeros((1, D), F32)
    pad16 = lambda a: jnp.pad(a, ((0, 0), (0, 128 - NH)))
    small_parts = [
        jnp.concatenate([g["norm_mix_pre"], g["ssm_norm"], g["conv_lru_b"], g["lru_lambda"], g["norm_mix_post"],
                         g["norm_mlp_pre"], g["norm_mlp_post"], zrow], axis=0),
        g["conv_ssm_b"],
        jnp.concatenate([pad16(g["dt_bias"]), pad16(g["a_log"]), pad16(g["d_skip"]), jnp.zeros((5, 128), F32)], axis=0),
        g["conv_ssm_w"], g["conv_lru_w"],
        jnp.concatenate([g["lru_wa"].reshape(D, 64), g["lru_wx"].reshape(D, 64)], axis=0),
        jnp.concatenate([g["lru_ba"].reshape(NH, 64), g["lru_bx"].reshape(NH, 64)], axis=0)]
    small = _pad_rows(jnp.concatenate([s.reshape(-1) for s in small_parts]), 8)
    small_all = _all_gather(small, "small_grads_all_gather")

    own = lambda src: lax.dynamic_index_in_dim(src, me, keepdims=False)
    mlp_src, mlp_land = _exchange_wait(sent["sems"], sent["src"], sent["land"], small_all, "mlp_grads_wait", scatter=True)
    g_mlp = _slot_sum(_own_slot(mlp_land, own(mlp_src)), "slot_sum_mlp")
    lg_src, lg_land = _exchange_wait(sent["sems2"], sent["src2"], sent["land2"], g_mlp, "late_grads_wait", scatter=True)
    g_late = _slot_sum(_own_slot(lg_land, own(lg_src)), "slot_sum_late")
    g_shard = dict(w_out=g_late[:ROW_UP], w_up=g_mlp[:FF // N_DEV].T, w_down=g_mlp[FF // N_DEV:],
                   w_in=g_late[ROW_UP:ROW_UP + IN_ROWS].T)
    out_g, out_d, out_m, out_v = {}, {}, {}, {}
    for n in BIG:
        gn = g_shard[n]
        d_n, m_n, v_n = _adam_big(w[n][0], gn, m[n][0], v[n][0], "adam_" + n)
        out_g[n], out_d[n], out_m[n], out_v[n] = gn[None], d_n[None], m_n[None], v_n[None]
    sflat = small_all.reshape(N_DEV, -1)
    groups = []
    off = 0
    for r, c in SMALL_GROUPS:
        groups.append(sflat[:, off:off + r * c].reshape(N_DEV, r, c))
        off += r * c
    groups[3] = lax.dynamic_slice_in_dim(groups[3], me * CONV_SSM_COLS, CONV_SSM_COLS, axis=2)
    groups[4] = lax.dynamic_slice_in_dim(groups[4], me * CONV_LRU_COLS, CONV_LRU_COLS, axis=2)
    wmv = [(w[n].reshape(s), m[n].reshape(s), v[n].reshape(s)) for n, s, _, _ in SMALL]
    res = _adam_small(groups, [(gi, r0) for _, _, gi, r0 in SMALL], wmv)
    for (n, _, _, _), (g_n, d_n, m_n, v_n) in zip(SMALL, res):
        shape = w[n].shape
        out_g[n], out_d[n], out_m[n], out_v[n] = (g_n.reshape(shape), d_n.reshape(shape), m_n.reshape(shape),
                                                  v_n.reshape(shape))
    return (loss, grad_x[None], *[out_g[n] for n in WEIGHTS], *[out_d[n] for n in WEIGHTS],
            *[out_m[n] for n in WEIGHTS], *[out_v[n] for n in WEIGHTS])
```

```python
import functools

import jax
import jax.numpy as jnp
from jax import lax
from jax.experimental import pallas as pl
from jax.experimental.pallas import tpu as pltpu

F32 = jnp.float32
MXU = jnp.bfloat16
HI = lax.Precision.HIGHEST
EPS = 1e-6

D = 1024
NH = 16
NS = 128
CH = 128
FF = 4096
NP = 7168
SEG = 1024
LRU_C = 8.0
N_DEV = 8

ADAM_LR, ADAM_B1, ADAM_B2, ADAM_EPS, ADAM_WD, ADAM_STEP = 0.001, 0.9, 0.999, 1e-08, 0.01, 10

VMEM_LIMIT = 56 * 1024 * 1024


def _cp(*sem):
    return pltpu.CompilerParams(dimension_semantics=sem, vmem_limit_bytes=VMEM_LIMIT)


def _nn(a, b):
    return jnp.dot(a.astype(MXU), b.astype(MXU), preferred_element_type=F32)


def _nt(a, b):
    return lax.dot_general(a.astype(MXU), b.astype(MXU), (((1,), (1,)), ((), ())), preferred_element_type=F32)


def _tn(a, b):
    return lax.dot_general(a.astype(MXU), b.astype(MXU), (((0,), (0,)), ((), ())), preferred_element_type=F32)


def _silu(x):
    return x * jax.nn.sigmoid(x)


def _dsilu(x):
    s = jax.nn.sigmoid(x)
    return s + x * s * (1.0 - s)


def _softplus(x):
    return jnp.maximum(x, 0.0) + jnp.log(1.0 + jnp.exp(-jnp.abs(x)))


def _rms(x, g):
    r = lax.rsqrt(jnp.mean(x * x, axis=-1, keepdims=True) + EPS)
    return x * r * g


def _rms_bwd(x, g, dy):
    r = lax.rsqrt(jnp.mean(x * x, axis=-1, keepdims=True) + EPS)
    gdy = g * dy
    dx = r * gdy - x * (r * r * r) * jnp.mean(x * gdy, axis=-1, keepdims=True)
    return dx, dy * x * r


def _rowsum(x):
    return jnp.sum(x, axis=0, keepdims=True)


def _taps_past(cur, prev8):
    r_n, c_n = cur.shape
    row = lax.broadcasted_iota(jnp.int32, (r_n, c_n), 0)
    out = []
    for k in range(4):
        s = 3 - k
        if s == 0:
            out.append(cur)
            continue
        head = jnp.concatenate([pltpu.roll(prev8, s, 0), jnp.zeros((r_n - 8, c_n), F32)], axis=0)
        out.append(jnp.where(row < s, head, pltpu.roll(cur, s, 0)))
    return out


def _taps_future(cur, fut8):
    r_n, c_n = cur.shape
    row = lax.broadcasted_iota(jnp.int32, (r_n, c_n), 0)
    out = []
    for k in range(4):
        s = 3 - k
        if s == 0:
            out.append(cur)
            continue
        tail = jnp.concatenate([jnp.zeros((r_n - 8, c_n), F32), pltpu.roll(fut8, 8 - s, 0)], axis=0)
        out.append(jnp.where(row >= r_n - s, tail, pltpu.roll(cur, r_n - s, 0)))
    return out


def _conv_apply(taps, w, b):
    acc = taps[0] * w[0:1, :]
    for k in range(1, 4):
        acc = acc + taps[k] * w[k:k + 1, :]
    return acc + b


def _inproj(x, g0, w):
    t_n = x.shape[0]
    tm = min(t_n, 1024)

    def body(x_ref, g_ref, w_ref, p_ref, u_ref):
        @pl.when(pl.program_id(1) == 0)
        def _():
            u_ref[...] = _rms(x_ref[...], g_ref[...]).astype(MXU)

        p_ref[...] = lax.dot_general(u_ref[...], w_ref[...], (((1,), (1,)), ((), ())), preferred_element_type=F32)

    return pl.pallas_call(
        body, name="inproj", grid=(t_n // tm, NP // SEG),
        in_specs=[pl.BlockSpec((tm, D), lambda i, j: (i, 0)), pl.BlockSpec((1, D), lambda i, j: (0, 0)),
                  pl.BlockSpec((SEG, D), lambda i, j: (j, 0))],
        out_specs=[pl.BlockSpec((tm, SEG), lambda i, j: (i, j)), pl.BlockSpec((tm, D), lambda i, j: (i, 0))],
        out_shape=[jax.ShapeDtypeStruct((t_n, NP), F32), jax.ShapeDtypeStruct((t_n, D), MXU)],
        compiler_params=_cp("parallel", "arbitrary"),
    )(x, g0, w)


def _ssd_prep(dtraw, dtb, alog):
    l_n = dtraw.shape[0]
    r = lax.broadcasted_iota(jnp.int32, (l_n, l_n), 0)
    c = lax.broadcasted_iota(jnp.int32, (l_n, l_n), 1)
    tril = (r >= c).astype(F32)
    triu = (r <= c).astype(F32)
    eye = (r == c).astype(F32)
    dt = _softplus(dtraw + dtb)
    adt = dt * (-jnp.exp(alog))
    ac = jnp.dot(tril, adt, preferred_element_type=F32, precision=HI)
    tn = (((0,), (0,)), ((), ()))
    ac_t = lax.dot_general(adt, triu, tn, preferred_element_type=F32, precision=HI)
    dt_t = lax.dot_general(dt, eye, tn, preferred_element_type=F32, precision=HI)
    return dt, dt_t, ac, ac_t, _rowsum(adt)


def _ssd_pair(j, xp, bg, cg, sp, dt, dt_t, ac, ac_t, aend):
    l_n = xp.shape[0]
    lane = lax.broadcasted_iota(jnp.int32, (l_n, 128), 1)
    sub = lax.broadcasted_iota(jnp.int32, (128, l_n), 0)
    lane1 = lax.broadcasted_iota(jnp.int32, (1, 128), 1)
    tri = lax.broadcasted_iota(jnp.int32, (l_n, l_n), 0) >= lax.broadcasted_iota(jnp.int32, (l_n, l_n), 1)
    cb = _nt(cg, bg)
    cs = _nn(cg, sp)
    ys, ss = [], []
    for e in range(2):
        h = 2 * j + e
        ac_l = jnp.sum(jnp.where(lane == h, ac, 0.0), axis=1, keepdims=True)
        dt_l = jnp.sum(jnp.where(lane == h, dt, 0.0), axis=1, keepdims=True)
        ac_s = jnp.sum(jnp.where(sub == h, ac_t, 0.0), axis=0, keepdims=True)
        dt_s = jnp.sum(jnp.where(sub == h, dt_t, 0.0), axis=0, keepdims=True)
        a_end = jnp.sum(jnp.where(lane1 == h, aend, 0.0), axis=1, keepdims=True)
        decay = jnp.exp(jnp.where(tri, ac_l - ac_s, -1e30))
        w = cb * decay * dt_s
        ys.append(_nn(w, xp) + jnp.exp(ac_l) * cs)
        ws = jnp.exp(a_end - ac_l) * dt_l
        ss.append(_tn(bg * ws, xp) + jnp.exp(a_end) * sp)
    lo = lax.broadcasted_iota(jnp.int32, (l_n, 128), 1) < 64
    lo_s = lax.broadcasted_iota(jnp.int32, (128, 128), 1) < 64
    return jnp.where(lo, ys[0], ys[1]), jnp.where(lo_s, ss[0], ss[1])


def _ssd_post(y, xs, z, dsk, nrm):
    y = (y + dsk * xs) * _silu(z)
    half = D // 2
    ya, yb = y[:, :half], y[:, half:]
    ya = ya * lax.rsqrt(jnp.mean(ya * ya, axis=-1, keepdims=True) + EPS)
    yb = yb * lax.rsqrt(jnp.mean(yb * yb, axis=-1, keepdims=True) + EPS)
    return jnp.concatenate([ya, yb], axis=1) * nrm


def _proj_specs(rows, seg_ids, order):
    return [pl.BlockSpec((rows, SEG), functools.partial(lambda i, s: (order(i), s), s=s)) for s in seg_ids]


def _prev8_specs(rows, seg_ids, order):
    rb = rows // 8
    return [pl.BlockSpec((8, SEG), functools.partial(lambda i, s: (jnp.maximum(order(i) * rb - 1, 0), s), s=s))
            for s in seg_ids]


def _full(shape):
    return pl.BlockSpec(shape, lambda i: (0,) * len(shape))


def _ssd_fwd(proj, cwx, cwb, cbx, cbb, dtb, alog, dsk, nrm):
    t_n = proj.shape[0]
    n_c = t_n // CH
    fwd = lambda i: i

    def body(z_ref, xs_ref, bc_ref, xsp_ref, bcp_ref, cwx_ref, cwb_ref, cbx_ref, cbb_ref, dtb_ref, alog_ref,
             dsk_ref, nrm_ref, ya_ref, sprev_ref, yraw_ref, s_ref):
        c = pl.program_id(0)

        @pl.when(c == 0)
        def _():
            s_ref[...] = jnp.zeros_like(s_ref)

        keep = jnp.where(c == 0, 0.0, 1.0)
        xs_pre = _conv_apply(_taps_past(xs_ref[...], xsp_ref[...] * keep), cwx_ref[...], cbx_ref[...])
        bc_pre = _conv_apply(_taps_past(bc_ref[:, :512], bcp_ref[:, :512] * keep), cwb_ref[...], cbb_ref[...])
        prep = _ssd_prep(bc_ref[:, 512:640], dtb_ref[...], alog_ref[...])
        xs = _silu(xs_pre)
        bc = _silu(bc_pre)
        sprev_ref[0] = s_ref[...]
        ys = []
        for j in range(NH // 2):
            g = j // 4
            yp, sn = _ssd_pair(j, xs[:, 128 * j:128 * j + 128], bc[:, 128 * g:128 * g + 128],
                               bc[:, 256 + 128 * g:384 + 128 * g], s_ref[:, 128 * j:128 * j + 128], *prep)
            ys.append(yp)
            s_ref[:, 128 * j:128 * j + 128] = sn
        y = jnp.concatenate(ys, axis=1)
        yraw_ref[...] = y
        ya_ref[...] = _ssd_post(y, xs, z_ref[...], dsk_ref[...], nrm_ref[...])

    return pl.pallas_call(
        body, name="ssd_fwd", grid=(n_c,),
        in_specs=_proj_specs(CH, (0, 5, 6), fwd) + _prev8_specs(CH, (5, 6), fwd) + [
            _full((4, D)), _full((4, 512)), _full((1, D)), _full((1, 512)), _full((1, 128)), _full((1, 128)),
            _full((1, D)), _full((1, D))],
        out_specs=[pl.BlockSpec((CH, D), lambda i: (i, 0)), pl.BlockSpec((1, NS, D), lambda i: (i, 0, 0)),
                   pl.BlockSpec((CH, D), lambda i: (i, 0))],
        out_shape=[jax.ShapeDtypeStruct((t_n, D), F32), jax.ShapeDtypeStruct((n_c, NS, D), F32),
                   jax.ShapeDtypeStruct((t_n, D), F32)],
        scratch_shapes=[pltpu.VMEM((NS, D), F32)],
        compiler_params=_cp("arbitrary"),
    )(proj, proj, proj, proj, proj, cwx, cwb, cbx, cbb, dtb, alog, dsk, nrm)


def _ssd_bwd(dya, yraw, proj, sprev, cwx, cwb, cbx, cbb, dtb, alog, dsk, nrm):
    t_n = proj.shape[0]
    n_c = t_n // CH
    rev = lambda i: n_c - 1 - i

    def body(dya_ref, yraw_ref, z_ref, xs_ref, bc_ref, xsp_ref, bcp_ref, sprev_ref, cwx_ref, cwb_ref, cbx_ref, cbb_ref,
             dtb_ref, alog_ref, dsk_ref, nrm_ref,
             dz_ref, dxs_ref, dbc_ref, dcwx_ref, dcwb_ref, dcbx_ref, dcbb_ref, ddtb_ref, dalog_ref, ddsk_ref,
             dnrm_ref, ds_ref, futx_ref, futb_ref):
        i = pl.program_id(0)
        acc_refs = (dcwx_ref, dcwb_ref, dcbx_ref, dcbb_ref, ddtb_ref, dalog_ref, ddsk_ref, dnrm_ref)

        @pl.when(i == 0)
        def _():
            for r in (ds_ref, futx_ref, futb_ref) + acc_refs:
                r[...] = jnp.zeros_like(r)

        keep = jnp.where(i == n_c - 1, 0.0, 1.0)
        taps_x = _taps_past(xs_ref[...], xsp_ref[...] * keep)
        taps_b = _taps_past(bc_ref[:, :512], bcp_ref[:, :512] * keep)
        xs_pre = _conv_apply(taps_x, cwx_ref[...], cbx_ref[...])
        bc_pre = _conv_apply(taps_b, cwb_ref[...], cbb_ref[...])
        xs = _silu(xs_pre)
        bc = _silu(bc_pre)
        prep, prep_vjp = jax.vjp(_ssd_prep, bc_ref[:, 512:640], dtb_ref[...], alog_ref[...])
        s_in = sprev_ref[0]

        def pair_args(j):
            g = j // 4
            return (xs[:, 128 * j:128 * j + 128], bc[:, 128 * g:128 * g + 128],
                    bc[:, 256 + 128 * g:384 + 128 * g], s_in[:, 128 * j:128 * j + 128]) + tuple(prep)

        _, post_vjp = jax.vjp(_ssd_post, yraw_ref[...], xs, z_ref[...], dsk_ref[...], nrm_ref[...])
        dy, dxs_skip, dz, ddsk, dnrm = post_vjp(dya_ref[...])
        dz_ref[...] = dz.astype(dz_ref.dtype)
        ddsk_ref[...] += ddsk
        dnrm_ref[...] += dnrm

        dprep = [jnp.zeros_like(p) for p in prep]
        dxp = []
        dbg = [jnp.zeros((CH, 128), F32), jnp.zeros((CH, 128), F32)]
        dcg = [jnp.zeros((CH, 128), F32), jnp.zeros((CH, 128), F32)]
        for j in range(NH // 2):
            g = j // 4
            _, pair_vjp = jax.vjp(functools.partial(_ssd_pair, j), *pair_args(j))
            cts = pair_vjp((dy[:, 128 * j:128 * j + 128], ds_ref[:, 128 * j:128 * j + 128]))
            dxp.append(cts[0])
            dbg[g] = dbg[g] + cts[1]
            dcg[g] = dcg[g] + cts[2]
            ds_ref[:, 128 * j:128 * j + 128] = cts[3]
            dprep = [a + b for a, b in zip(dprep, cts[4:])]
        ddtraw, ddtb, dalog = prep_vjp(tuple(dprep))
        ddtb_ref[...] += ddtb
        dalog_ref[...] += dalog

        dxs_pre = (dxs_skip + jnp.concatenate(dxp, axis=1)) * _dsilu(xs_pre)
        dbc_pre = jnp.concatenate([dbg[0], dbg[1], dcg[0], dcg[1]], axis=1) * _dsilu(bc_pre)
        dcbx_ref[...] += _rowsum(dxs_pre)
        dcbb_ref[...] += _rowsum(dbc_pre)
        for k in range(4):
            dcwx_ref[k:k + 1, :] += _rowsum(dxs_pre * taps_x[k])
            dcwb_ref[k:k + 1, :] += _rowsum(dbc_pre * taps_b[k])
        fx = _taps_future(dxs_pre, futx_ref[...])
        fb = _taps_future(dbc_pre, futb_ref[...])
        cwx = cwx_ref[...]
        cwb = cwb_ref[...]
        dxs_in = fx[0] * cwx[0:1, :]
        dbc_in = fb[0] * cwb[0:1, :]
        for k in range(1, 4):
            dxs_in = dxs_in + fx[k] * cwx[k:k + 1, :]
            dbc_in = dbc_in + fb[k] * cwb[k:k + 1, :]
        futx_ref[...] = dxs_pre[0:8, :]
        futb_ref[...] = dbc_pre[0:8, :]
        dxs_ref[...] = dxs_in.astype(dxs_ref.dtype)
        dbc_ref[...] = jnp.concatenate([dbc_in, ddtraw, jnp.zeros((CH, SEG - 640), F32)], axis=1).astype(dbc_ref.dtype)

    row_out = lambda: pl.BlockSpec((CH, D), lambda i: (rev(i), 0))
    outs = pl.pallas_call(
        body, name="ssd_bwd", grid=(n_c,),
        in_specs=[row_out(), row_out()] + _proj_specs(CH, (0, 5, 6), rev)
        + _prev8_specs(CH, (5, 6), rev) + [pl.BlockSpec((1, NS, D), lambda i: (rev(i), 0, 0)),
                                            _full((4, D)), _full((4, 512)), _full((1, D)), _full((1, 512)),
                                            _full((1, 128)), _full((1, 128)), _full((1, D)), _full((1, D))],
        out_specs=[row_out(), row_out(), row_out(), _full((4, D)), _full((4, 512)), _full((1, D)), _full((1, 512)),
                   _full((1, 128)), _full((1, 128)), _full((1, D)), _full((1, D))],
        out_shape=[jax.ShapeDtypeStruct((t_n, D), MXU)] * 3 + [
            jax.ShapeDtypeStruct(s, F32) for s in ((4, D), (4, 512), (1, D), (1, 512), (1, 128), (1, 128), (1, D), (1, D))],
        scratch_shapes=[pltpu.VMEM((NS, D), F32), pltpu.VMEM((8, D), F32), pltpu.VMEM((8, 512), F32)],
        compiler_params=_cp("arbitrary"),
    )(dya, yraw, proj, proj, proj, proj, proj, sprev, cwx, cwb, cbx, cbb, dtb, alog, dsk, nrm)
    return outs


LRU_ROWS = 256
LRU_BLK = 256


def _lru_gates(xr, wa, wx, ba, bx, lam):
    pr = jnp.concatenate([_nn(xr[:, LRU_BLK * b:LRU_BLK * (b + 1)], wa[b]) for b in range(D // LRU_BLK)], axis=1) + ba
    pi = jnp.concatenate([_nn(xr[:, LRU_BLK * b:LRU_BLK * (b + 1)], wx[b]) for b in range(D // LRU_BLK)], axis=1) + bx
    log_a = -LRU_C * jax.nn.sigmoid(pr) * _softplus(-lam)
    a = jnp.exp(log_a)
    mult = jnp.sqrt(1.0 - jnp.exp(2.0 * log_a))
    return a, mult * (jax.nn.sigmoid(pi) * xr)


def _lru_out(h, g):
    return h * jax.nn.gelu(g, approximate=True)


def _lru_fwd(proj, cw, cb, wa, wx, ba, bx, lam):
    t_n = proj.shape[0]
    rows = min(LRU_ROWS, t_n)
    fwd = lambda i: i

    def body(g_ref, x_ref, xp_ref, cw_ref, cb_ref, wa_ref, wx_ref, ba_ref, bx_ref, lam_ref, yb_ref, h_ref,
             a_s, u_s, carry):
        i = pl.program_id(0)

        @pl.when(i == 0)
        def _():
            carry[...] = jnp.zeros_like(carry)

        keep = jnp.where(i == 0, 0.0, 1.0)
        xr = _conv_apply(_taps_past(x_ref[...], xp_ref[...] * keep), cw_ref[...], cb_ref[...])
        a, u = _lru_gates(xr, wa_ref[...], wx_ref[...], ba_ref[...], bx_ref[...], lam_ref[...])
        a_s[...] = a
        u_s[...] = u
        row = lax.broadcasted_iota(jnp.int32, (8, D), 0)

        def blk(b, c):
            s = pl.multiple_of(b * 8, 8)
            av = a_s[pl.ds(s, 8), :]
            uv = u_s[pl.ds(s, 8), :]
            for d in (1, 2, 4):
                m = row >= d
                uv = uv + av * jnp.where(m, pltpu.roll(uv, d, 0), 0.0)
                av = av * jnp.where(m, pltpu.roll(av, d, 0), 1.0)
            hv = uv + av * c
            h_ref[pl.ds(s, 8), :] = hv
            return hv[7:8, :]

        carry[0:1, :] = lax.fori_loop(0, rows // 8, blk, carry[0:1, :])
        yb_ref[...] = _lru_out(h_ref[...], g_ref[...])

    return pl.pallas_call(
        body, name="lru_fwd", grid=(t_n // rows,),
        in_specs=_proj_specs(rows, (1, 2), fwd) + _prev8_specs(rows, (2,), fwd) + [
            _full((4, D)), _full((1, D)), _full((4, LRU_BLK, LRU_BLK)), _full((4, LRU_BLK, LRU_BLK)),
            _full((1, D)), _full((1, D)), _full((1, D))],
        out_specs=[pl.BlockSpec((rows, D), lambda i: (i, 0)), pl.BlockSpec((rows, D), lambda i: (i, 0))],
        out_shape=[jax.ShapeDtypeStruct((t_n, D), F32), jax.ShapeDtypeStruct((t_n, D), F32)],
        scratch_shapes=[pltpu.VMEM((rows, D), F32), pltpu.VMEM((rows, D), F32), pltpu.VMEM((8, D), F32)],
        compiler_params=_cp("arbitrary"),
    )(proj, proj, proj, cw, cb, wa, wx, ba, bx, lam)


def _lru_bwd(dyb, proj, h, cw, cb, wa, wx, ba, bx, lam):
    t_n = proj.shape[0]
    rows = min(LRU_ROWS, t_n)
    n_t = t_n // rows
    rev = lambda i: n_t - 1 - i
    rb = rows // 8

    def body(dyb_ref, g_ref, x_ref, xp_ref, h_ref, hp_ref, cw_ref, cb_ref, wa_ref, wx_ref, ba_ref, bx_ref, lam_ref,
             dg_ref, dx_ref, dcw_ref, dcb_ref, dwa_ref, dwx_ref, dba_ref, dbx_ref, dlam_ref,
             a_s, dh_s, hx_s, da_s, du_s, carry, fut):
        i = pl.program_id(0)
        acc_refs = (dcw_ref, dcb_ref, dwa_ref, dwx_ref, dba_ref, dbx_ref, dlam_ref)

        @pl.when(i == 0)
        def _():
            for r in (carry, fut) + acc_refs:
                r[...] = jnp.zeros_like(r)

        keep = jnp.where(i == n_t - 1, 0.0, 1.0)
        taps = _taps_past(x_ref[...], xp_ref[...] * keep)
        xr = _conv_apply(taps, cw_ref[...], cb_ref[...])
        gate_in = (xr, wa_ref[...], wx_ref[...], ba_ref[...], bx_ref[...], lam_ref[...])
        (a, _), gates_vjp = jax.vjp(_lru_gates, *gate_in)
        _, out_vjp = jax.vjp(_lru_out, h_ref[...], g_ref[...])
        dh, dg = out_vjp(dyb_ref[...])
        dg_ref[...] = dg.astype(dg_ref.dtype)
        a_s[...] = a
        dh_s[...] = dh
        hx_s[0:8, :] = hp_ref[...] * keep
        hx_s[8:, :] = h_ref[...]
        row = lax.broadcasted_iota(jnp.int32, (8, D), 0)

        def blk(b, c):
            s = pl.multiple_of((rb - 1 - b) * 8, 8)
            av = a_s[pl.ds(s, 8), :]
            dhv = dh_s[pl.ds(s, 8), :]
            a0 = av
            kv = av * dhv
            for d in (1, 2, 4):
                m = row <= 7 - d
                kv = kv + av * jnp.where(m, pltpu.roll(kv, 8 - d, 0), 0.0)
                av = av * jnp.where(m, pltpu.roll(av, 8 - d, 0), 1.0)
            kv = kv + av * c
            gv = dhv + jnp.where(row < 7, pltpu.roll(kv, 7, 0), c)
            hb = hx_s[pl.ds(s + 8, 8), :]
            hpv = hx_s[pl.ds(s, 8), :]
            hprev = jnp.where(row >= 1, pltpu.roll(hb, 1, 0), hpv[7:8, :])
            du_s[pl.ds(s, 8), :] = gv
            da_s[pl.ds(s, 8), :] = gv * hprev
            del a0
            return kv[0:1, :]

        carry[0:1, :] = lax.fori_loop(0, rb, blk, carry[0:1, :])
        dxr, dwa, dwx, dba, dbx, dlam = gates_vjp((da_s[...], du_s[...]))
        dwa_ref[...] += dwa
        dwx_ref[...] += dwx
        dba_ref[...] += dba
        dbx_ref[...] += dbx
        dlam_ref[...] += dlam
        dcb_ref[...] += _rowsum(dxr)
        for k in range(4):
            dcw_ref[k:k + 1, :] += _rowsum(dxr * taps[k])
        ft = _taps_future(dxr, fut[...])
        cwv = cw_ref[...]
        dx = ft[0] * cwv[0:1, :]
        for k in range(1, 4):
            dx = dx + ft[k] * cwv[k:k + 1, :]
        fut[...] = dxr[0:8, :]
        dx_ref[...] = dx.astype(dx_ref.dtype)

    row_in = lambda: pl.BlockSpec((rows, D), lambda i: (rev(i), 0))
    prev_h = pl.BlockSpec((8, D), lambda i: (jnp.maximum(rev(i) * rb - 1, 0), 0))
    wspec = lambda: _full((4, LRU_BLK, LRU_BLK))
    return pl.pallas_call(
        body, name="lru_bwd", grid=(n_t,),
        in_specs=[row_in()] + _proj_specs(rows, (1, 2), rev) + _prev8_specs(rows, (2,), rev) + [row_in(), prev_h] + [
            _full((4, D)), _full((1, D)), wspec(), wspec(), _full((1, D)), _full((1, D)), _full((1, D))],
        out_specs=[row_in(), row_in(), _full((4, D)), _full((1, D)), wspec(), wspec(), _full((1, D)), _full((1, D)),
                   _full((1, D))],
        out_shape=[jax.ShapeDtypeStruct((t_n, D), MXU)] * 2 + [
            jax.ShapeDtypeStruct(s, F32) for s in ((4, D), (1, D), (4, LRU_BLK, LRU_BLK), (4, LRU_BLK, LRU_BLK),
                                                   (1, D), (1, D), (1, D))],
        scratch_shapes=[pltpu.VMEM((rows, D), F32), pltpu.VMEM((rows, D), F32), pltpu.VMEM((rows + 8, D), F32),
                        pltpu.VMEM((rows, D), F32), pltpu.VMEM((rows, D), F32), pltpu.VMEM((8, D), F32),
                        pltpu.VMEM((8, D), F32)],
        compiler_params=_cp("arbitrary"),
    )(dyb, proj, proj, proj, h, h, cw, cb, wa, wx, ba, bx, lam)


def _merge_out(ya, yb, proj, x, wout, g1):
    t_n = x.shape[0]
    tm = min(t_n, 512)

    def body(ya_ref, yb_ref, ga_ref, gb_ref, x_ref, w_ref, g_ref, h1_ref, mix_ref, mg_ref):
        merged = jax.nn.sigmoid(ga_ref[...]) * ya_ref[...] + jax.nn.sigmoid(gb_ref[...]) * yb_ref[...]
        mg = merged.astype(MXU)
        mg_ref[...] = mg
        mix = jnp.dot(mg, w_ref[...], preferred_element_type=F32)
        mix_ref[...] = mix
        h1_ref[...] = x_ref[...] + _rms(mix, g_ref[...])

    row = lambda: pl.BlockSpec((tm, D), lambda i: (i, 0))
    return pl.pallas_call(
        body, name="merge_out", grid=(t_n // tm,),
        in_specs=[row(), row()] + _proj_specs(tm, (3, 4), lambda i: i) + [row(), _full((D, D)), _full((1, D))],
        out_specs=[row(), row(), row()],
        out_shape=[jax.ShapeDtypeStruct((t_n, D), F32), jax.ShapeDtypeStruct((t_n, D), F32),
                   jax.ShapeDtypeStruct((t_n, D), MXU)],
        compiler_params=_cp("parallel"),
    )(ya, yb, proj, proj, x, wout, g1)


def _out_bwd(dh1, mix, ya, yb, proj, wout, g1):
    t_n = dh1.shape[0]
    tm = min(t_n, 512)

    def body(dh1_ref, mix_ref, ya_ref, yb_ref, ga_ref, gb_ref, w_ref, g_ref,
             dmix_ref, dya_ref, dyb_ref, dga_ref, dgb_ref, dg1_ref):
        @pl.when(pl.program_id(0) == 0)
        def _():
            dg1_ref[...] = jnp.zeros_like(dg1_ref)

        dmix, dg_rows = _rms_bwd(mix_ref[...], g_ref[...], dh1_ref[...])
        dg1_ref[...] += _rowsum(dg_rows)
        dmix_b = dmix.astype(MXU)
        dmix_ref[...] = dmix_b
        dmg = lax.dot_general(dmix_b, w_ref[...], (((1,), (1,)), ((), ())), preferred_element_type=F32)
        sa = jax.nn.sigmoid(ga_ref[...])
        sb = jax.nn.sigmoid(gb_ref[...])
        dya_ref[...] = dmg * sa
        dyb_ref[...] = dmg * sb
        dga_ref[...] = (dmg * ya_ref[...] * sa * (1.0 - sa)).astype(MXU)
        dgb_ref[...] = (dmg * yb_ref[...] * sb * (1.0 - sb)).astype(MXU)

    row = lambda: pl.BlockSpec((tm, D), lambda i: (i, 0))
    return pl.pallas_call(
        body, name="out_bwd", grid=(t_n // tm,),
        in_specs=[row(), row(), row(), row()] + _proj_specs(tm, (3, 4), lambda i: i) + [_full((D, D)), _full((1, D))],
        out_specs=[row(), row(), row(), row(), row(), _full((1, D))],
        out_shape=[jax.ShapeDtypeStruct((t_n, D), MXU), jax.ShapeDtypeStruct((t_n, D), F32),
                   jax.ShapeDtypeStruct((t_n, D), F32), jax.ShapeDtypeStruct((t_n, D), MXU),
                   jax.ShapeDtypeStruct((t_n, D), MXU), jax.ShapeDtypeStruct((1, D), F32)],
        compiler_params=_cp("arbitrary"),
    )(dh1, mix, ya, yb, proj, proj, wout, g1)


MLP_TF = 1024


def _mlp_fwd(h1, g2, wup, wdown, g3, tgt):
    t_n = h1.shape[0]
    tm = min(t_n, 512)
    n_f = FF // MLP_TF

    def body(h1_ref, g2_ref, wu_ref, wd_ref, g3_ref, tgt_ref, hp_ref, v_ref, ff_ref, loss_ref, acc):
        i, j = pl.program_id(0), pl.program_id(1)

        @pl.when((i == 0) & (j == 0))
        def _():
            loss_ref[...] = jnp.zeros_like(loss_ref)

        @pl.when(j == 0)
        def _():
            v_ref[...] = _rms(h1_ref[...], g2_ref[...]).astype(MXU)
            acc[...] = jnp.zeros_like(acc)

        hp = lax.dot_general(v_ref[...], wu_ref[...], (((1,), (1,)), ((), ())), preferred_element_type=F32)
        hp_ref[...] = hp
        hid = jnp.square(jnp.maximum(hp, 0.0))
        acc[...] += jnp.dot(hid.astype(MXU), wd_ref[...], preferred_element_type=F32)

        @pl.when(j == n_f - 1)
        def _():
            ff = acc[...]
            ff_ref[...] = ff
            err = h1_ref[...] + _rms(ff, g3_ref[...]) - tgt_ref[...]
            loss_ref[...] += 0.5 * jnp.sum(jnp.mean(err * err, axis=-1, keepdims=True), axis=0, keepdims=True)

    row = lambda: pl.BlockSpec((tm, D), lambda i, j: (i, 0))
    vec = lambda: pl.BlockSpec((1, D), lambda i, j: (0, 0))
    return pl.pallas_call(
        body, name="mlp_fwd", grid=(t_n // tm, n_f),
        in_specs=[row(), vec(), pl.BlockSpec((MLP_TF, D), lambda i, j: (j, 0)),
                  pl.BlockSpec((MLP_TF, D), lambda i, j: (j, 0)), vec(), row()],
        out_specs=[pl.BlockSpec((tm, MLP_TF), lambda i, j: (i, j)), row(), row(),
                   pl.BlockSpec((1, 1), lambda i, j: (0, 0))],
        out_shape=[jax.ShapeDtypeStruct((t_n, FF), F32), jax.ShapeDtypeStruct((t_n, D), MXU),
                   jax.ShapeDtypeStruct((t_n, D), F32), jax.ShapeDtypeStruct((1, 1), F32)],
        scratch_shapes=[pltpu.VMEM((tm, D), F32)],
        compiler_params=_cp("arbitrary", "arbitrary"),
    )(h1, g2, wup, wdown, g3, tgt)


def _mlp_bwd(h1, ff, tgt, hp, wup, wdown, g2, g3):
    t_n = h1.shape[0]
    tm = min(t_n, 512)
    n_f = FF // MLP_TF

    def body(h1_ref, ff_ref, tgt_ref, hp_ref, wu_ref, wd_ref, g2_ref, g3_ref,
             dh1_ref, dff_ref, dhp_ref, hid_ref, dg2_ref, dg3_ref, acc, dout_s):
        i, j = pl.program_id(0), pl.program_id(1)

        @pl.when((i == 0) & (j == 0))
        def _():
            dg2_ref[...] = jnp.zeros_like(dg2_ref)
            dg3_ref[...] = jnp.zeros_like(dg3_ref)

        @pl.when(j == 0)
        def _():
            ff = ff_ref[...]
            dout = (h1_ref[...] + _rms(ff, g3_ref[...]) - tgt_ref[...]) * (1.0 / D)
            dout_s[...] = dout
            dff, dg_rows = _rms_bwd(ff, g3_ref[...], dout)
            dg3_ref[...] += _rowsum(dg_rows)
            dff_ref[...] = dff.astype(MXU)
            acc[...] = jnp.zeros_like(acc)

        hp = hp_ref[...]
        relu = jnp.maximum(hp, 0.0)
        hid_ref[...] = jnp.square(relu).astype(MXU)
        dhid = lax.dot_general(dff_ref[...], wd_ref[...], (((1,), (1,)), ((), ())), preferred_element_type=F32)
        dhp = (dhid * (2.0 * relu)).astype(MXU)
        dhp_ref[...] = dhp
        acc[...] += jnp.dot(dhp, wu_ref[...], preferred_element_type=F32)

        @pl.when(j == n_f - 1)
        def _():
            dv, dg_rows = _rms_bwd(h1_ref[...], g2_ref[...], acc[...])
            dg2_ref[...] += _rowsum(dg_rows)
            dh1_ref[...] = dout_s[...] + dv

    row = lambda: pl.BlockSpec((tm, D), lambda i, j: (i, 0))
    vec = lambda: pl.BlockSpec((1, D), lambda i, j: (0, 0))
    blk = lambda: pl.BlockSpec((tm, MLP_TF), lambda i, j: (i, j))
    return pl.pallas_call(
        body, name="mlp_bwd", grid=(t_n // tm, n_f),
        in_specs=[row(), row(), row(), blk(), pl.BlockSpec((MLP_TF, D), lambda i, j: (j, 0)),
                  pl.BlockSpec((MLP_TF, D), lambda i, j: (j, 0)), vec(), vec()],
        out_specs=[row(), row(), blk(), blk(), vec(), vec()],
        out_shape=[jax.ShapeDtypeStruct((t_n, D), F32), jax.ShapeDtypeStruct((t_n, D), MXU),
                   jax.ShapeDtypeStruct((t_n, FF), MXU), jax.ShapeDtypeStruct((t_n, FF), MXU),
                   jax.ShapeDtypeStruct((1, D), F32), jax.ShapeDtypeStruct((1, D), F32)],
        scratch_shapes=[pltpu.VMEM((tm, D), F32), pltpu.VMEM((tm, D), F32)],
        compiler_params=_cp("arbitrary", "arbitrary"),
    )(h1, ff, tgt, hp, wup, wdown, g2, g3)


def _wgrad(a, g, name):
    t_n, k_n = a.shape
    n_n = g.shape[1]
    tt = min(t_n, 1024)
    tk, tn = min(k_n, 1024), min(n_n, 1024)

    n_t = t_n // tt

    def body(a_ref, g_ref, o_ref, acc):
        t = pl.program_id(2)

        @pl.when(t == 0)
        def _():
            acc[...] = jnp.zeros_like(acc)

        acc[...] += lax.dot_general(a_ref[...], g_ref[...], (((0,), (0,)), ((), ())), preferred_element_type=F32)

        @pl.when(t == n_t - 1)
        def _():
            o_ref[...] = acc[...].astype(o_ref.dtype)

    return pl.pallas_call(
        body, name=name, grid=(k_n // tk, n_n // tn, n_t),
        in_specs=[pl.BlockSpec((tt, tk), lambda k, n, t: (t, k)), pl.BlockSpec((tt, tn), lambda k, n, t: (t, n))],
        out_specs=pl.BlockSpec((tk, tn), lambda k, n, t: (k, n)),
        out_shape=jax.ShapeDtypeStruct((k_n, n_n), MXU),
        scratch_shapes=[pltpu.VMEM((tk, tn), F32)],
        compiler_params=_cp("parallel", "parallel", "arbitrary"),
    )(a, g)


def _wgrad_segs(segs, g, name):
    t_n, n_n = g.shape
    n_s = len(segs)
    tt = min(t_n, 1024)
    n_t = t_n // tt

    def body(*refs):
        a_refs = refs[:n_s]
        g_ref, o_ref, acc = refs[n_s:]
        s_id, t = pl.program_id(0), pl.program_id(1)

        @pl.when(t == 0)
        def _():
            acc[...] = jnp.zeros_like(acc)

        for s in range(n_s):
            @pl.when(s_id == s)
            def _(s=s):
                acc[...] += lax.dot_general(a_refs[s][...], g_ref[...], (((0,), (0,)), ((), ())),
                                            preferred_element_type=F32)

        @pl.when(t == n_t - 1)
        def _():
            o_ref[...] = acc[...].astype(o_ref.dtype)

    seg_spec = lambda s: pl.BlockSpec((tt, SEG), lambda i, t: (jnp.where(i == s, t, jnp.where(i < s, 0, n_t - 1)), 0))
    return pl.pallas_call(
        body, name=name, grid=(n_s, n_t),
        in_specs=[seg_spec(s) for s in range(n_s)] + [pl.BlockSpec((tt, n_n), lambda i, t: (t, 0))],
        out_specs=pl.BlockSpec((SEG, n_n), lambda i, t: (i, 0)),
        out_shape=jax.ShapeDtypeStruct((n_s * SEG, n_n), MXU),
        scratch_shapes=[pltpu.VMEM((SEG, n_n), F32)],
        compiler_params=_cp("arbitrary", "arbitrary"),
    )(*segs, g)


def _inproj_bwd(dsegs, w, x, g0, dh1):
    t_n = x.shape[0]
    tm = min(t_n, 512)
    n_k = NP // SEG

    def body(*refs):
        dp_refs = refs[:n_k]
        w_ref, x_ref, g_ref, dh1_ref, dx_ref, dg0_ref, acc = refs[n_k:]
        i, k = pl.program_id(0), pl.program_id(1)

        @pl.when((i == 0) & (k == 0))
        def _():
            dg0_ref[...] = jnp.zeros_like(dg0_ref)

        @pl.when(k == 0)
        def _():
            acc[...] = jnp.zeros_like(acc)

        for s in range(n_k):
            @pl.when(k == s)
            def _(s=s):
                acc[...] += jnp.dot(dp_refs[s][...], w_ref[...], preferred_element_type=F32)

        @pl.when(k == n_k - 1)
        def _():
            dx, dg_rows = _rms_bwd(x_ref[...], g_ref[...], acc[...])
            dg0_ref[...] += _rowsum(dg_rows)
            dx_ref[...] = dh1_ref[...] + dx

    row = lambda: pl.BlockSpec((tm, D), lambda i, k: (i, 0))
    vec = lambda: pl.BlockSpec((1, D), lambda i, k: (0, 0))
    return pl.pallas_call(
        body, name="inproj_bwd", grid=(t_n // tm, n_k),
        in_specs=[pl.BlockSpec((tm, SEG), lambda i, k: (i, 0)) for _ in range(n_k)] + [
            pl.BlockSpec((SEG, D), lambda i, k: (k, 0)), row(), vec(), row()],
        out_specs=[row(), vec()],
        out_shape=[jax.ShapeDtypeStruct((t_n, D), F32), jax.ShapeDtypeStruct((1, D), F32)],
        scratch_shapes=[pltpu.VMEM((tm, D), F32)],
        compiler_params=_cp("arbitrary", "arbitrary"),
    )(*dsegs, w, x, g0, dh1)


def _blockdiag4(w):
    w4 = w.reshape(4, 4, 64, 1, 64).astype(MXU)
    same = (jnp.arange(4)[:, None, None, None] == jnp.arange(4)[None, None, :, None])
    return jnp.where(same[None], w4, jnp.zeros((), MXU)).reshape(4, 256, 256)


def _blockdiag4_extract(g):
    g5 = g.reshape(4, 4, 64, 4, 64)
    return jnp.stack([g5[:, q, :, q, :] for q in range(4)], axis=1).reshape(NH, 64, 64)


def _local_step(x, tgt, p, late_weights=None, send_mlp_grads=None, send_late_grads=None):
    f = lambda a: a.astype(F32)
    proj, u = _inproj(x, p["norm_mix_pre"], p["w_in_pT"])
    ssm_params = (p["cw_xs"], p["cw_bc"], p["cb_xs"], p["cb_bc"], p["dt_bias"], p["a_log"], p["d_skip_x"],
                  p["ssm_norm"])
    ya, sprev, yraw = _ssd_fwd(proj, *ssm_params)
    lru_params = (p["conv_lru_w"], p["conv_lru_b"], p["wa_bd"], p["wx_bd"], p["lru_ba"], p["lru_bx"],
                  p["lru_lambda"])
    yb, h = _lru_fwd(proj, *lru_params)
    if late_weights is not None:
        p = dict(p, **late_weights(yb))
    h1, mix, merged = _merge_out(ya, yb, proj, x, p["w_out"], p["norm_mix_post"])
    hp, v, ff, loss = _mlp_fwd(h1, p["norm_mlp_pre"], p["w_upT"], p["w_down"], p["norm_mlp_post"], tgt)

    dh1, dff, dhp, hid, dg2, dg3 = _mlp_bwd(h1, ff, tgt, hp, p["w_upT"], p["w_down"], p["norm_mlp_pre"],
                                            p["norm_mlp_post"])
    d_w_down = _wgrad(hid, dff, "wgrad_down")
    d_w_up_t = _wgrad(dhp, v, "wgrad_up")
    g1 = p["norm_mix_post"]
    if send_mlp_grads is not None:
        g1 = g1 + send_mlp_grads(d_w_up_t, d_w_down)
    dmix, dya, dyb, dga, dgb, dg1 = _out_bwd(dh1, mix, ya, yb, proj, p["w_out"], g1)
    d_w_out = _wgrad(merged, dmix, "wgrad_out")
    (dz, dxs, dbc, dcwx, dcwb, dcbx, dcbb, ddtb, dalog, ddsk, dnrm) = _ssd_bwd(dya, yraw, proj, sprev, *ssm_params)
    (dgl, dxl, dcwl, dcbl, dwa, dwx, dba, dbx, dlam) = _lru_bwd(dyb, proj, h, *lru_params)
    dsegs = [dz, dgl, dxl, dga, dgb, dxs, dbc]
    d_w_in_pt = _wgrad_segs(dsegs, u, "wgrad_in")
    g0 = p["norm_mix_pre"]
    if send_late_grads is not None:
        g0 = g0 + send_late_grads(d_w_out, d_w_in_pt)
    grad_x, dg0 = _inproj_bwd(dsegs, p["w_in_pT"], x, g0, dh1)
    grads = dict(
        norm_mix_pre=dg0, w_in_pT=d_w_in_pt, conv_ssm_w=jnp.concatenate([dcwx, dcwb], axis=1),
        conv_ssm_b=jnp.concatenate([dcbx, dcbb], axis=1), dt_bias=ddtb[:, :NH], a_log=dalog[:, :NH],
        d_skip=f(ddsk).reshape(NH, 64).sum(axis=1)[None, :], ssm_norm=dnrm, conv_lru_w=dcwl, conv_lru_b=dcbl,
        lru_wa=_blockdiag4_extract(dwa), lru_ba=dba, lru_wx=_blockdiag4_extract(dwx), lru_bx=dbx, lru_lambda=dlam,
        w_out=d_w_out, norm_mix_post=dg1, norm_mlp_pre=dg2, w_upT=d_w_up_t, w_down=d_w_down, norm_mlp_post=dg3)
    return loss[0, 0], grad_x, grads


W_IN_COLS = 6672


def _w_in_t_to_padded(wt):
    z, xs, bc, dt = wt[0:1024], wt[1024:2048], wt[2048:2560], wt[2560:2576]
    gl, xl, ga, gb = wt[2576:3600], wt[3600:4624], wt[4624:5648], wt[5648:6672]
    return jnp.concatenate([z, gl, xl, ga, gb, xs, bc, dt, jnp.zeros((NP - 6672, wt.shape[1]), wt.dtype)], axis=0)


def _w_in_t_from_padded(wp):
    z, gl, xl, ga, gb = (wp[SEG * s:SEG * (s + 1)] for s in range(5))
    xs, bc, dt = wp[5120:6144], wp[6144:6656], wp[6656:6672]
    return jnp.concatenate([z, xs, bc, dt, gl, xl, ga, gb], axis=0)


def _prep_params(full, big):
    f = lambda a: a.astype(F32)
    pad128 = lambda a: jnp.pad(f(a).reshape(1, -1), ((0, 0), (0, 128 - a.size)))
    cw = f(full["conv_ssm_w"])
    cb = f(full["conv_ssm_b"]).reshape(1, -1)
    return dict(
        big, norm_mix_pre=f(full["norm_mix_pre"]).reshape(1, D),
        cw_xs=cw[:, :D], cw_bc=cw[:, D:], cb_xs=cb[:, :D], cb_bc=cb[:, D:],
        dt_bias=pad128(full["dt_bias"]), a_log=pad128(full["a_log"]),
        d_skip_x=jnp.repeat(f(full["d_skip"]).reshape(-1), 64).reshape(1, D), ssm_norm=f(full["ssm_norm"]).reshape(1, D),
        conv_lru_w=f(full["conv_lru_w"]), conv_lru_b=f(full["conv_lru_b"]).reshape(1, D),
        wa_bd=_blockdiag4(full["lru_wa"]), wx_bd=_blockdiag4(full["lru_wx"]),
        lru_ba=f(full["lru_ba"]).reshape(1, D), lru_bx=f(full["lru_bx"]).reshape(1, D),
        lru_lambda=f(full["lru_lambda"]).reshape(1, D),
        norm_mix_post=f(full["norm_mix_post"]).reshape(1, D), norm_mlp_pre=f(full["norm_mlp_pre"]).reshape(1, D),
        norm_mlp_post=f(full["norm_mlp_post"]).reshape(1, D))


MESH_ID = pl.DeviceIdType.MESH
ANY = pl.BlockSpec(memory_space=pl.ANY)


def _my_place():
    x, y, c = lax.axis_index("x"), lax.axis_index("y"), lax.axis_index("c")
    return x, y, c, 4 * x + 2 * y + c


def _peer(x, y, c, k):
    return (x ^ ((k >> 2) & 1), y ^ ((k >> 1) & 1), c ^ (k & 1))


def _all_gather(pack, name):
    def body(in_ref, out_ref, send_sems, recv_sems, local_sem):
        x, y, c, me = _my_place()
        sibling = (x, y, 1 - c)
        flips = (4, 2, 6)

        def copy(j, block, to, src=None):
            return pltpu.make_async_remote_copy(
                src_ref=out_ref.at[block] if src is None else src, dst_ref=out_ref.at[block],
                send_sem=send_sems.at[j], recv_sem=recv_sems.at[j], device_id=to, device_id_type=MESH_ID)

        mine = pltpu.make_async_copy(in_ref, out_ref.at[me], local_sem)
        mine.start()
        first = [copy(0, me, sibling, src=in_ref)]
        first += [copy(1 + j, me, _peer(x, y, c, k), src=in_ref) for j, k in enumerate(flips)]
        for cp in first:
            cp.start()
        passed = [copy(4 + j, me ^ k, sibling) for j, k in enumerate(flips)]
        for j, k in enumerate(flips):
            copy(1 + j, me ^ k, (x, y, c)).wait_recv()
            passed[j].start()
        copy(0, me ^ 1, (x, y, c)).wait_recv()
        for j, k in enumerate(flips):
            copy(4 + j, me ^ k ^ 1, (x, y, c)).wait_recv()
        for cp in first + passed:
            cp.wait_send()
        mine.wait()

    return pl.pallas_call(
        body, name=name, in_specs=[ANY], out_specs=ANY,
        out_shape=jax.ShapeDtypeStruct((N_DEV,) + pack.shape, pack.dtype),
        scratch_shapes=[pltpu.SemaphoreType.DMA((N_DEV - 1,)), pltpu.SemaphoreType.DMA((N_DEV - 1,)),
                        pltpu.SemaphoreType.DMA],
    )(pack)


HBM = pl.BlockSpec(memory_space=pltpu.HBM)
SEM = pl.BlockSpec(memory_space=pltpu.SEMAPHORE)
N_PEER = N_DEV - 1


def _peer_copy(k, src_ref, land_ref, sems, scatter):
    x, y, c, me = _my_place()
    return pltpu.make_async_remote_copy(
        src_ref=src_ref.at[me ^ k] if scatter else src_ref, dst_ref=land_ref.at[me], send_sem=sems[k - 1],
        recv_sem=sems[N_PEER + k - 1], device_id=_peer(x, y, c, k), device_id_type=MESH_ID)


def _exchange_start(src, name, scatter):
    rows = src.shape[-2:]

    def body(src_ref, land_ref, *rest):
        sems, token = rest[:2 * N_PEER], rest[2 * N_PEER + 2]
        for k in range(1, N_DEV):
            _peer_copy(k, src_ref, land_ref, sems, scatter).start()
        token[...] = jnp.zeros_like(token)

    land = pltpu.with_memory_space_constraint(lax.empty((N_DEV,) + rows, src.dtype), pltpu.HBM)
    outs = pl.pallas_call(
        body, name=name,
        out_shape=(pltpu.SemaphoreType.DMA(()),) * (2 * N_PEER) + (
            pltpu.HBM(src.shape, src.dtype), pltpu.HBM(land.shape, land.dtype), jax.ShapeDtypeStruct((8, 128), F32)),
        in_specs=(HBM, HBM), out_specs=(SEM,) * (2 * N_PEER) + (HBM, HBM, pl.BlockSpec(memory_space=pltpu.VMEM)),
        input_output_aliases={0: 2 * N_PEER, 1: 2 * N_PEER + 1},
        compiler_params=pltpu.CompilerParams(has_side_effects=pltpu.SideEffectType.DATAFLOW_SIDE_EFFECTING),
    )(pltpu.with_memory_space_constraint(src, pltpu.HBM), land)
    return outs[:2 * N_PEER], outs[2 * N_PEER], outs[2 * N_PEER + 1], outs[2 * N_PEER + 2]


def _exchange_wait(sems, src_thru, land_thru, after, name, scatter):
    def body(src_ref, land_ref, *rest):
        sem_refs = rest[:2 * N_PEER]
        for k in range(1, N_DEV):
            cp = _peer_copy(k, src_ref, land_ref, sem_refs, scatter)
            cp.wait_send()
            cp.wait_recv()

    return pl.pallas_call(
        body, name=name,
        out_shape=(pltpu.HBM(src_thru.shape, src_thru.dtype), pltpu.HBM(land_thru.shape, land_thru.dtype)),
        in_specs=(HBM, HBM) + (SEM,) * (2 * N_PEER) + (pl.BlockSpec(memory_space=pl.ANY),), out_specs=(HBM, HBM),
        input_output_aliases={0: 0, 1: 1},
        compiler_params=pltpu.CompilerParams(has_side_effects=pltpu.SideEffectType.DATAFLOW_SIDE_EFFECTING),
    )(src_thru, land_thru, *sems, after)


def _slot_sum(parts, name):
    r_n, c_n = parts.shape[1:]
    tr = max(t for t in range(16, 513, 16) if r_n % t == 0)

    def body(p_ref, o_ref):
        acc = p_ref[0].astype(F32)
        for k in range(1, N_DEV):
            acc = acc + p_ref[k].astype(F32)
        o_ref[...] = acc

    return pl.pallas_call(
        body, name=name, grid=(r_n // tr,),
        in_specs=[pl.BlockSpec((N_DEV, tr, c_n), lambda i: (0, i, 0))],
        out_specs=pl.BlockSpec((tr, c_n), lambda i: (i, 0)),
        out_shape=jax.ShapeDtypeStruct((r_n, c_n), F32),
        compiler_params=_cp("parallel"),
    )(parts)


def _adam_math(w, g, m, v):
    m = ADAM_B1 * m + (1.0 - ADAM_B1) * g
    v = ADAM_B2 * v + (1.0 - ADAM_B2) * jnp.square(g)
    m_hat = m / (1.0 - ADAM_B1 ** ADAM_STEP)
    v_hat = v / (1.0 - ADAM_B2 ** ADAM_STEP)
    return -ADAM_LR * (m_hat / (jnp.sqrt(v_hat) + ADAM_EPS) + ADAM_WD * w), m, v


def _adam_big(w, g, m, v, name):
    _, r_n, c_n = w.shape
    tr = min(r_n, 256)

    def body(w_ref, g_ref, m_ref, v_ref, d_ref, mo_ref, vo_ref):
        d_ref[...], mo_ref[...], vo_ref[...] = _adam_math(w_ref[...], g_ref[...], m_ref[...], v_ref[...])

    blk = lambda: pl.BlockSpec((1, tr, c_n), lambda i: (0, i, 0))
    return pl.pallas_call(
        body, name=name, grid=(r_n // tr,), in_specs=[blk(), blk(), blk(), blk()], out_specs=[blk(), blk(), blk()],
        out_shape=[jax.ShapeDtypeStruct(w.shape, F32)] * 3, compiler_params=_cp("parallel"),
    )(w, g, m, v)


def _adam_small(groups, where, wmv):
    n, n_g = len(wmv), len(groups)

    def body(*refs):
        g_refs = refs[:n_g]
        w_refs = refs[n_g:n_g + 3 * n]
        o_refs = refs[n_g + 3 * n:]
        for q in range(n):
            w_ref, m_ref, v_ref = w_refs[3 * q:3 * q + 3]
            r, c = w_ref.shape
            gi, r0 = where[q]
            g = g_refs[gi][0, r0:r0 + r, 0:c]
            for k in range(1, N_DEV):
                g = g + g_refs[gi][k, r0:r0 + r, 0:c]
            d, m, v = _adam_math(w_ref[...], g, m_ref[...], v_ref[...])
            o_refs[4 * q][...] = g
            o_refs[4 * q + 1][...] = d
            o_refs[4 * q + 2][...] = m
            o_refs[4 * q + 3][...] = v

    flat_wmv = [a for t in wmv for a in t]
    vm = pl.BlockSpec(memory_space=pltpu.VMEM)
    outs = pl.pallas_call(
        body, name="adam_small", in_specs=[vm] * (n_g + 3 * n), out_specs=[vm] * (4 * n),
        out_shape=[jax.ShapeDtypeStruct(t[0].shape, F32) for t in wmv for _ in range(4)],
        compiler_params=pltpu.CompilerParams(vmem_limit_bytes=VMEM_LIMIT),
    )(*groups, *flat_wmv)
    return [tuple(outs[4 * q:4 * q + 4]) for q in range(n)]


WEIGHTS = ["norm_mix_pre", "w_in", "conv_ssm_w", "conv_ssm_b", "dt_bias", "a_log", "d_skip", "ssm_norm", "conv_lru_w",
           "conv_lru_b", "lru_wa", "lru_ba", "lru_wx", "lru_bx", "lru_lambda", "w_out", "norm_mix_post", "norm_mlp_pre",
           "w_up", "w_down", "norm_mlp_post"]
BIG = ["w_out", "w_up", "w_down", "w_in"]
IN_ROWS = W_IN_COLS // N_DEV
IN_PAD, EARLY_ROWS = 848, 880
ROW_UP, ROW_DOWN, LATE_ROWS = 128, 640, 1152
GRAD_LATE_ROWS = 992
CONV_SSM_COLS, CONV_LRU_COLS = 1536 // N_DEV, D // N_DEV
SMALL = [("norm_mix_pre", (1, D), 0, 0), ("ssm_norm", (1, D), 0, 1), ("conv_lru_b", (1, D), 0, 2),
         ("lru_lambda", (1, D), 0, 3), ("norm_mix_post", (1, D), 0, 4), ("norm_mlp_pre", (1, D), 0, 5),
         ("norm_mlp_post", (1, D), 0, 6), ("conv_ssm_b", (1, 1536), 1, 0), ("dt_bias", (1, NH), 2, 0),
         ("a_log", (1, NH), 2, 1), ("d_skip", (1, NH), 2, 2), ("conv_ssm_w", (4, CONV_SSM_COLS), 3, 0),
         ("conv_lru_w", (4, CONV_LRU_COLS), 4, 0), ("lru_wa", (D, 64), 5, 0), ("lru_wx", (D, 64), 5, D),
         ("lru_ba", (NH, 64), 6, 0), ("lru_bx", (NH, 64), 6, NH)]
SMALL_GROUPS = [(8, D), (1, 1536), (8, 128), (4, 1536), (4, D), (2 * D, 64), (2 * NH, 64)]


def _pad_rows(flat, mult):
    n = flat.shape[0]
    rows = -(-n // (128 * mult)) * mult
    return jnp.pad(flat, (0, rows * 128 - n)).reshape(rows, 128)


def _split3(a):
    hi = a.astype(MXU)
    r1 = a - hi.astype(F32)
    mid = r1.astype(MXU)
    lo = (r1 - mid.astype(F32)).astype(MXU)
    return jnp.stack([hi, mid, lo])


def _early_pack(a):
    bf = lambda t: t.astype(MXU)
    conv = lambda t, c: jnp.pad(_split3(t).reshape(12, c), ((0, 4), (0, D - c)))
    return jnp.concatenate([jnp.pad(bf(a["w_in"][0]).T, ((0, IN_PAD - IN_ROWS), (0, 0))),
                            conv(a["conv_ssm_w"][0], CONV_SSM_COLS), conv(a["conv_lru_w"][0], CONV_LRU_COLS)], axis=0)


def _early_unpack(g):
    w_in_pt = _w_in_t_to_padded(g[:, :IN_ROWS].reshape(W_IN_COLS, D))
    conv = {}
    for n, r0, c in (("conv_ssm_w", IN_PAD, CONV_SSM_COLS), ("conv_lru_w", IN_PAD + 16, CONV_LRU_COLS)):
        s = g[:, r0:r0 + 12, :c].astype(F32).reshape(N_DEV, 3, 4, c)
        conv[n] = ((s[:, 0] + s[:, 1]) + s[:, 2]).transpose(1, 0, 2).reshape(4, N_DEV * c)
    return w_in_pt, conv


def _late_pack(a):
    bf = lambda t: t.astype(MXU)
    return jnp.concatenate([bf(a["w_out"][0]), bf(a["w_up"][0]).T, bf(a["w_down"][0])], axis=0)


def _late_unpack(g):
    return dict(w_out=g[:, :ROW_UP].reshape(D, D), w_upT=g[:, ROW_UP:ROW_DOWN].reshape(FF, D),
                w_down=g[:, ROW_DOWN:].reshape(FF, D))


def _own_slot(land, own):
    me = 4 * lax.axis_index("x") + 2 * lax.axis_index("y") + lax.axis_index("c")
    return lax.dynamic_update_slice_in_dim(land, own[None], me, axis=0)


def kernel(x, norm_mix_pre, w_in, conv_ssm_w, conv_ssm_b, dt_bias, a_log, d_skip, ssm_norm, conv_lru_w, conv_lru_b, lru_wa, lru_ba, lru_wx, lru_bx, lru_lambda, w_out, norm_mix_post, norm_mlp_pre, w_up, w_down, norm_mlp_post, loss_target, m_norm_mix_pre, m_w_in, m_conv_ssm_w, m_conv_ssm_b, m_dt_bias, m_a_log, m_d_skip, m_ssm_norm, m_conv_lru_w, m_conv_lru_b, m_lru_wa, m_lru_ba, m_lru_wx, m_lru_bx, m_lru_lambda, m_w_out, m_norm_mix_post, m_norm_mlp_pre, m_w_up, m_w_down, m_norm_mlp_post, v_norm_mix_pre, v_w_in, v_conv_ssm_w, v_conv_ssm_b, v_dt_bias, v_a_log, v_d_skip, v_ssm_norm, v_conv_lru_w, v_conv_lru_b, v_lru_wa, v_lru_ba, v_lru_wx, v_lru_bx, v_lru_lambda, v_w_out, v_norm_mix_post, v_norm_mlp_pre, v_w_up, v_w_down, v_norm_mlp_post):
    vals = (norm_mix_pre, w_in, conv_ssm_w, conv_ssm_b, dt_bias, a_log, d_skip, ssm_norm, conv_lru_w, conv_lru_b, lru_wa, lru_ba, lru_wx, lru_bx, lru_lambda, w_out, norm_mix_post, norm_mlp_pre, w_up, w_down, norm_mlp_post)
    m_vals = (m_norm_mix_pre, m_w_in, m_conv_ssm_w, m_conv_ssm_b, m_dt_bias, m_a_log, m_d_skip, m_ssm_norm, m_conv_lru_w, m_conv_lru_b, m_lru_wa, m_lru_ba, m_lru_wx, m_lru_bx, m_lru_lambda, m_w_out, m_norm_mix_post, m_norm_mlp_pre, m_w_up, m_w_down, m_norm_mlp_post)
    v_vals = (v_norm_mix_pre, v_w_in, v_conv_ssm_w, v_conv_ssm_b, v_dt_bias, v_a_log, v_d_skip, v_ssm_norm, v_conv_lru_w, v_conv_lru_b, v_lru_wa, v_lru_ba, v_lru_wx, v_lru_bx, v_lru_lambda, v_w_out, v_norm_mix_post, v_norm_mlp_pre, v_w_up, v_w_down, v_norm_mlp_post)
    w = dict(zip(WEIGHTS, vals))
    m = dict(zip(WEIGHTS, m_vals))
    v = dict(zip(WEIGHTS, v_vals))
    me = 4 * lax.axis_index("x") + 2 * lax.axis_index("y") + lax.axis_index("c")

    bf = lambda t: t.astype(MXU)
    late = _late_pack(w)
    early = _all_gather(_early_pack(w), "early_weights_all_gather")
    late, early = lax.optimization_barrier((late, early))
    late_sems, late_src, late_land, token = _exchange_start(late, "late_weights_start", scatter=False)
    w_in_pt, conv_w = _early_unpack(early)
    full = {n: (conv_w[n] if n in conv_w else w[n][0]) for n in WEIGHTS if n not in BIG}
    full["norm_mix_pre"] = full["norm_mix_pre"] + token[0, 0]

    def late_weights(after):
        src, land = _exchange_wait(late_sems, late_src, late_land, after, "late_weights_wait", scatter=False)
        return _late_unpack(_own_slot(land, src))

    sent = {}

    def send_mlp_grads(d_w_up_t, d_w_down):
        src = jnp.concatenate([bf(d_w_up_t).reshape(N_DEV, -1, D), bf(d_w_down).reshape(N_DEV, -1, D)], axis=1)
        sent["sems"], sent["src"], sent["land"], tok = _exchange_start(src, "mlp_grads_start", scatter=True)
        return tok[0, 0]

    def send_late_grads(d_w_out, d_w_in_pt):
        src = jnp.concatenate([
            bf(d_w_out).reshape(N_DEV, -1, D),
            jnp.pad(bf(_w_in_t_from_padded(d_w_in_pt)).reshape(N_DEV, IN_ROWS, D),
                    ((0, 0), (0, GRAD_LATE_ROWS - ROW_UP - IN_ROWS), (0, 0)))], axis=1)
        sent["sems2"], sent["src2"], sent["land2"], tok = _exchange_start(src, "late_grads_start", scatter=True)
        return tok[0, 0]

    loss, grad_x, g = _local_step(x[0], loss_target[0], _prep_params(full, dict(w_in_pT=w_in_pt)), late_weights,
                                  send_mlp_grads, send_late_grads)
    loss = lax.psum(loss, ("x", "y", "c"))

    zrow = jnp.zeros((1, D), F32)
    pad16 = lambda a: jnp.pad(a, ((0, 0), (0, 128 - NH)))
    small_parts = [
        jnp.concatenate([g["norm_mix_pre"], g["ssm_norm"], g["conv_lru_b"], g["lru_lambda"], g["norm_mix_post"],
                         g["norm_mlp_pre"], g["norm_mlp_post"], zrow], axis=0),
        g["conv_ssm_b"],
        jnp.concatenate([pad16(g["dt_bias"]), pad16(g["a_log"]), pad16(g["d_skip"]), jnp.zeros((5, 128), F32)], axis=0),
        g["conv_ssm_w"], g["conv_lru_w"],
        jnp.concatenate([g["lru_wa"].reshape(D, 64), g["lru_wx"].reshape(D, 64)], axis=0),
        jnp.concatenate([g["lru_ba"].reshape(NH, 64), g["lru_bx"].reshape(NH, 64)], axis=0)]
    small = _pad_rows(jnp.concatenate([s.reshape(-1) for s in small_parts]), 8)
    small_all = _all_gather(small, "small_grads_all_gather")

    own = lambda src: lax.dynamic_index_in_dim(src, me, keepdims=False)
    mlp_src, mlp_land = _exchange_wait(sent["sems"], sent["src"], sent["land"], small_all, "mlp_grads_wait", scatter=True)
    g_mlp = _slot_sum(_own_slot(mlp_land, own(mlp_src)), "slot_sum_mlp")
    lg_src, lg_land = _exchange_wait(sent["sems2"], sent["src2"], sent["land2"], g_mlp, "late_grads_wait", scatter=True)
    g_late = _slot_sum(_own_slot(lg_land, own(lg_src)), "slot_sum_late")
    g_shard = dict(w_out=g_late[:ROW_UP], w_up=g_mlp[:FF // N_DEV].T, w_down=g_mlp[FF // N_DEV:],
                   w_in=g_late[ROW_UP:ROW_UP + IN_ROWS].T)
    out_g, out_d, out_m, out_v = {}, {}, {}, {}
    for n in BIG:
        gn = g_shard[n][None]
        out_g[n] = gn
        out_d[n], out_m[n], out_v[n] = _adam_big(w[n], gn, m[n], v[n], "adam_" + n)
    sflat = small_all.reshape(N_DEV, -1)
    groups = []
    off = 0
    for r, c in SMALL_GROUPS:
        groups.append(sflat[:, off:off + r * c].reshape(N_DEV, r, c))
        off += r * c
    groups[3] = lax.dynamic_slice_in_dim(groups[3], me * CONV_SSM_COLS, CONV_SSM_COLS, axis=2)
    groups[4] = lax.dynamic_slice_in_dim(groups[4], me * CONV_LRU_COLS, CONV_LRU_COLS, axis=2)
    wmv = [(w[n].reshape(s), m[n].reshape(s), v[n].reshape(s)) for n, s, _, _ in SMALL]
    res = _adam_small(groups, [(gi, r0) for _, _, gi, r0 in SMALL], wmv)
    for (n, _, _, _), (g_n, d_n, m_n, v_n) in zip(SMALL, res):
        shape = w[n].shape
        out_g[n], out_d[n], out_m[n], out_v[n] = (g_n.reshape(shape), d_n.reshape(shape), m_n.reshape(shape),
                                                  v_n.reshape(shape))
    return (loss, grad_x[None], *[out_g[n] for n in WEIGHTS], *[out_d[n] for n in WEIGHTS],
            *[out_m[n] for n in WEIGHTS], *[out_v[n] for n in WEIGHTS])
```

```python
import functools

import jax
import jax.numpy as jnp
from jax import lax
from jax.experimental import pallas as pl
from jax.experimental.pallas import tpu as pltpu

F32 = jnp.float32
MXU = jnp.bfloat16
HI = lax.Precision.HIGHEST
EPS = 1e-6

D = 1024
NH = 16
NS = 128
CH = 128
FF = 4096
NP = 7168
SEG = 1024
LRU_C = 8.0
N_DEV = 8

ADAM_LR, ADAM_B1, ADAM_B2, ADAM_EPS, ADAM_WD, ADAM_STEP = 0.001, 0.9, 0.999, 1e-08, 0.01, 10

VMEM_LIMIT = 56 * 1024 * 1024


def _cp(*sem):
    return pltpu.CompilerParams(dimension_semantics=sem, vmem_limit_bytes=VMEM_LIMIT)


def _nn(a, b):
    return jnp.dot(a.astype(MXU), b.astype(MXU), preferred_element_type=F32)


def _nt(a, b):
    return lax.dot_general(a.astype(MXU), b.astype(MXU), (((1,), (1,)), ((), ())), preferred_element_type=F32)


def _tn(a, b):
    return lax.dot_general(a.astype(MXU), b.astype(MXU), (((0,), (0,)), ((), ())), preferred_element_type=F32)


def _silu(x):
    return x * jax.nn.sigmoid(x)


def _dsilu(x):
    s = jax.nn.sigmoid(x)
    return s + x * s * (1.0 - s)


def _softplus(x):
    return jnp.maximum(x, 0.0) + jnp.log(1.0 + jnp.exp(-jnp.abs(x)))


def _rms(x, g):
    r = lax.rsqrt(jnp.mean(x * x, axis=-1, keepdims=True) + EPS)
    return x * r * g


def _rms_bwd(x, g, dy):
    r = lax.rsqrt(jnp.mean(x * x, axis=-1, keepdims=True) + EPS)
    gdy = g * dy
    dx = r * gdy - x * (r * r * r) * jnp.mean(x * gdy, axis=-1, keepdims=True)
    return dx, dy * x * r


def _rowsum(x):
    return jnp.sum(x, axis=0, keepdims=True)


def _taps_past(cur, prev8):
    r_n, c_n = cur.shape
    row = lax.broadcasted_iota(jnp.int32, (r_n, c_n), 0)
    out = []
    for k in range(4):
        s = 3 - k
        if s == 0:
            out.append(cur)
            continue
        head = jnp.concatenate([pltpu.roll(prev8, s, 0), jnp.zeros((r_n - 8, c_n), F32)], axis=0)
        out.append(jnp.where(row < s, head, pltpu.roll(cur, s, 0)))
    return out


def _taps_future(cur, fut8):
    r_n, c_n = cur.shape
    row = lax.broadcasted_iota(jnp.int32, (r_n, c_n), 0)
    out = []
    for k in range(4):
        s = 3 - k
        if s == 0:
            out.append(cur)
            continue
        tail = jnp.concatenate([jnp.zeros((r_n - 8, c_n), F32), pltpu.roll(fut8, 8 - s, 0)], axis=0)
        out.append(jnp.where(row >= r_n - s, tail, pltpu.roll(cur, r_n - s, 0)))
    return out


def _conv_apply(taps, w, b):
    acc = taps[0] * w[0:1, :]
    for k in range(1, 4):
        acc = acc + taps[k] * w[k:k + 1, :]
    return acc + b


def _inproj(x, g0, w):
    t_n = x.shape[0]
    tm = min(t_n, 1024)

    def body(x_ref, g_ref, w_ref, p_ref, u_ref):
        @pl.when(pl.program_id(1) == 0)
        def _():
            u_ref[...] = _rms(x_ref[...], g_ref[...]).astype(MXU)

        p_ref[...] = lax.dot_general(u_ref[...], w_ref[...], (((1,), (1,)), ((), ())), preferred_element_type=F32)

    return pl.pallas_call(
        body, name="inproj", grid=(t_n // tm, NP // SEG),
        in_specs=[pl.BlockSpec((tm, D), lambda i, j: (i, 0)), pl.BlockSpec((1, D), lambda i, j: (0, 0)),
                  pl.BlockSpec((SEG, D), lambda i, j: (j, 0))],
        out_specs=[pl.BlockSpec((tm, SEG), lambda i, j: (i, j)), pl.BlockSpec((tm, D), lambda i, j: (i, 0))],
        out_shape=[jax.ShapeDtypeStruct((t_n, NP), F32), jax.ShapeDtypeStruct((t_n, D), MXU)],
        compiler_params=_cp("parallel", "arbitrary"),
    )(x, g0, w)


def _ssd_prep(dtraw, dtb, alog):
    l_n = dtraw.shape[0]
    r = lax.broadcasted_iota(jnp.int32, (l_n, l_n), 0)
    c = lax.broadcasted_iota(jnp.int32, (l_n, l_n), 1)
    tril = (r >= c).astype(F32)
    triu = (r <= c).astype(F32)
    eye = (r == c).astype(F32)
    dt = _softplus(dtraw + dtb)
    adt = dt * (-jnp.exp(alog))
    ac = jnp.dot(tril, adt, preferred_element_type=F32, precision=HI)
    tn = (((0,), (0,)), ((), ()))
    ac_t = lax.dot_general(adt, triu, tn, preferred_element_type=F32, precision=HI)
    dt_t = lax.dot_general(dt, eye, tn, preferred_element_type=F32, precision=HI)
    return dt, dt_t, ac, ac_t, _rowsum(adt)


def _ssd_pair(j, xp, bg, cg, sp, dt, dt_t, ac, ac_t, aend):
    l_n = xp.shape[0]
    lane = lax.broadcasted_iota(jnp.int32, (l_n, 128), 1)
    sub = lax.broadcasted_iota(jnp.int32, (128, l_n), 0)
    lane1 = lax.broadcasted_iota(jnp.int32, (1, 128), 1)
    tri = lax.broadcasted_iota(jnp.int32, (l_n, l_n), 0) >= lax.broadcasted_iota(jnp.int32, (l_n, l_n), 1)
    lo = lax.broadcasted_iota(jnp.int32, (l_n, 128), 1) < 64
    lo_s = lax.broadcasted_iota(jnp.int32, (128, 128), 1) < 64
    cb = _nt(cg, bg)
    cs = _nn(cg, sp)
    x2 = jnp.concatenate([jnp.where(lo, xp, 0.0), jnp.where(lo, 0.0, xp)], axis=0)
    ws_, bs_, eo, ee = [], [], [], []
    for e in range(2):
        h = 2 * j + e
        ac_l = jnp.sum(jnp.where(lane == h, ac, 0.0), axis=1, keepdims=True)
        dt_l = jnp.sum(jnp.where(lane == h, dt, 0.0), axis=1, keepdims=True)
        ac_s = jnp.sum(jnp.where(sub == h, ac_t, 0.0), axis=0, keepdims=True)
        dt_s = jnp.sum(jnp.where(sub == h, dt_t, 0.0), axis=0, keepdims=True)
        a_end = jnp.sum(jnp.where(lane1 == h, aend, 0.0), axis=1, keepdims=True)
        decay = jnp.exp(jnp.where(tri, ac_l - ac_s, -1e30))
        ws_.append(cb * decay * dt_s)
        bs_.append(bg * (jnp.exp(a_end - ac_l) * dt_l))
        eo.append(jnp.exp(ac_l))
        ee.append(jnp.exp(a_end))
    y = _nn(jnp.concatenate(ws_, axis=1), x2) + jnp.where(lo, eo[0], eo[1]) * cs
    s_new = _tn(jnp.concatenate(bs_, axis=0), x2) + jnp.where(lo_s, ee[0], ee[1]) * sp
    return y, s_new


def _ssd_post(y, xs, z, dsk, nrm):
    y = (y + dsk * xs) * _silu(z)
    half = D // 2
    ya, yb = y[:, :half], y[:, half:]
    ya = ya * lax.rsqrt(jnp.mean(ya * ya, axis=-1, keepdims=True) + EPS)
    yb = yb * lax.rsqrt(jnp.mean(yb * yb, axis=-1, keepdims=True) + EPS)
    return jnp.concatenate([ya, yb], axis=1) * nrm


def _proj_specs(rows, seg_ids, order):
    return [pl.BlockSpec((rows, SEG), functools.partial(lambda i, s: (order(i), s), s=s)) for s in seg_ids]


def _prev8_specs(rows, seg_ids, order):
    rb = rows // 8
    return [pl.BlockSpec((8, SEG), functools.partial(lambda i, s: (jnp.maximum(order(i) * rb - 1, 0), s), s=s))
            for s in seg_ids]


def _full(shape):
    return pl.BlockSpec(shape, lambda i: (0,) * len(shape))


def _ssd_fwd(proj, cwx, cwb, cbx, cbb, dtb, alog, dsk, nrm):
    t_n = proj.shape[0]
    n_c = t_n // CH
    fwd = lambda i: i

    def body(z_ref, xs_ref, bc_ref, xsp_ref, bcp_ref, cwx_ref, cwb_ref, cbx_ref, cbb_ref, dtb_ref, alog_ref,
             dsk_ref, nrm_ref, ya_ref, sprev_ref, yraw_ref, s_ref):
        c = pl.program_id(0)

        @pl.when(c == 0)
        def _():
            s_ref[...] = jnp.zeros_like(s_ref)

        keep = jnp.where(c == 0, 0.0, 1.0)
        xs_pre = _conv_apply(_taps_past(xs_ref[...], xsp_ref[...] * keep), cwx_ref[...], cbx_ref[...])
        bc_pre = _conv_apply(_taps_past(bc_ref[:, :512], bcp_ref[:, :512] * keep), cwb_ref[...], cbb_ref[...])
        prep = _ssd_prep(bc_ref[:, 512:640], dtb_ref[...], alog_ref[...])
        xs = _silu(xs_pre)
        bc = _silu(bc_pre)
        sprev_ref[0] = s_ref[...]
        ys = []
        for j in range(NH // 2):
            g = j // 4
            yp, sn = _ssd_pair(j, xs[:, 128 * j:128 * j + 128], bc[:, 128 * g:128 * g + 128],
                               bc[:, 256 + 128 * g:384 + 128 * g], s_ref[:, 128 * j:128 * j + 128], *prep)
            ys.append(yp)
            s_ref[:, 128 * j:128 * j + 128] = sn
        y = jnp.concatenate(ys, axis=1)
        yraw_ref[...] = y
        ya_ref[...] = _ssd_post(y, xs, z_ref[...], dsk_ref[...], nrm_ref[...])

    return pl.pallas_call(
        body, name="ssd_fwd", grid=(n_c,),
        in_specs=_proj_specs(CH, (0, 5, 6), fwd) + _prev8_specs(CH, (5, 6), fwd) + [
            _full((4, D)), _full((4, 512)), _full((1, D)), _full((1, 512)), _full((1, 128)), _full((1, 128)),
            _full((1, D)), _full((1, D))],
        out_specs=[pl.BlockSpec((CH, D), lambda i: (i, 0)), pl.BlockSpec((1, NS, D), lambda i: (i, 0, 0)),
                   pl.BlockSpec((CH, D), lambda i: (i, 0))],
        out_shape=[jax.ShapeDtypeStruct((t_n, D), F32), jax.ShapeDtypeStruct((n_c, NS, D), F32),
                   jax.ShapeDtypeStruct((t_n, D), F32)],
        scratch_shapes=[pltpu.VMEM((NS, D), F32)],
        compiler_params=_cp("arbitrary"),
    )(proj, proj, proj, proj, proj, cwx, cwb, cbx, cbb, dtb, alog, dsk, nrm)


def _ssd_bwd(dya, yraw, proj, sprev, cwx, cwb, cbx, cbb, dtb, alog, dsk, nrm):
    t_n = proj.shape[0]
    n_c = t_n // CH
    rev = lambda i: n_c - 1 - i

    def body(dya_ref, yraw_ref, z_ref, xs_ref, bc_ref, xsp_ref, bcp_ref, sprev_ref, cwx_ref, cwb_ref, cbx_ref, cbb_ref,
             dtb_ref, alog_ref, dsk_ref, nrm_ref,
             dz_ref, dxs_ref, dbc_ref, dcwx_ref, dcwb_ref, dcbx_ref, dcbb_ref, ddtb_ref, dalog_ref, ddsk_ref,
             dnrm_ref, ds_ref, futx_ref, futb_ref):
        i = pl.program_id(0)
        acc_refs = (dcwx_ref, dcwb_ref, dcbx_ref, dcbb_ref, ddtb_ref, dalog_ref, ddsk_ref, dnrm_ref)

        @pl.when(i == 0)
        def _():
            for r in (ds_ref, futx_ref, futb_ref) + acc_refs:
                r[...] = jnp.zeros_like(r)

        keep = jnp.where(i == n_c - 1, 0.0, 1.0)
        taps_x = _taps_past(xs_ref[...], xsp_ref[...] * keep)
        taps_b = _taps_past(bc_ref[:, :512], bcp_ref[:, :512] * keep)
        xs_pre = _conv_apply(taps_x, cwx_ref[...], cbx_ref[...])
        bc_pre = _conv_apply(taps_b, cwb_ref[...], cbb_ref[...])
        xs = _silu(xs_pre)
        bc = _silu(bc_pre)
        prep, prep_vjp = jax.vjp(_ssd_prep, bc_ref[:, 512:640], dtb_ref[...], alog_ref[...])
        s_in = sprev_ref[0]

        def pair_args(j):
            g = j // 4
            return (xs[:, 128 * j:128 * j + 128], bc[:, 128 * g:128 * g + 128],
                    bc[:, 256 + 128 * g:384 + 128 * g], s_in[:, 128 * j:128 * j + 128]) + tuple(prep)

        _, post_vjp = jax.vjp(_ssd_post, yraw_ref[...], xs, z_ref[...], dsk_ref[...], nrm_ref[...])
        dy, dxs_skip, dz, ddsk, dnrm = post_vjp(dya_ref[...])
        dz_ref[...] = dz.astype(dz_ref.dtype)
        ddsk_ref[...] += ddsk
        dnrm_ref[...] += dnrm

        dprep = [jnp.zeros_like(p) for p in prep]
        dxp = []
        dbg = [jnp.zeros((CH, 128), F32), jnp.zeros((CH, 128), F32)]
        dcg = [jnp.zeros((CH, 128), F32), jnp.zeros((CH, 128), F32)]
        for j in range(NH // 2):
            g = j // 4
            _, pair_vjp = jax.vjp(functools.partial(_ssd_pair, j), *pair_args(j))
            cts = pair_vjp((dy[:, 128 * j:128 * j + 128], ds_ref[:, 128 * j:128 * j + 128]))
            dxp.append(cts[0])
            dbg[g] = dbg[g] + cts[1]
            dcg[g] = dcg[g] + cts[2]
            ds_ref[:, 128 * j:128 * j + 128] = cts[3]
            dprep = [a + b for a, b in zip(dprep, cts[4:])]
        ddtraw, ddtb, dalog = prep_vjp(tuple(dprep))
        ddtb_ref[...] += ddtb
        dalog_ref[...] += dalog

        dxs_pre = (dxs_skip + jnp.concatenate(dxp, axis=1)) * _dsilu(xs_pre)
        dbc_pre = jnp.concatenate([dbg[0], dbg[1], dcg[0], dcg[1]], axis=1) * _dsilu(bc_pre)
        dcbx_ref[...] += _rowsum(dxs_pre)
        dcbb_ref[...] += _rowsum(dbc_pre)
        for k in range(4):
            dcwx_ref[k:k + 1, :] += _rowsum(dxs_pre * taps_x[k])
            dcwb_ref[k:k + 1, :] += _rowsum(dbc_pre * taps_b[k])
        fx = _taps_future(dxs_pre, futx_ref[...])
        fb = _taps_future(dbc_pre, futb_ref[...])
        cwx = cwx_ref[...]
        cwb = cwb_ref[...]
        dxs_in = fx[0] * cwx[0:1, :]
        dbc_in = fb[0] * cwb[0:1, :]
        for k in range(1, 4):
            dxs_in = dxs_in + fx[k] * cwx[k:k + 1, :]
            dbc_in = dbc_in + fb[k] * cwb[k:k + 1, :]
        futx_ref[...] = dxs_pre[0:8, :]
        futb_ref[...] = dbc_pre[0:8, :]
        dxs_ref[...] = dxs_in.astype(dxs_ref.dtype)
        dbc_ref[...] = jnp.concatenate([dbc_in, ddtraw, jnp.zeros((CH, SEG - 640), F32)], axis=1).astype(dbc_ref.dtype)

    row_out = lambda: pl.BlockSpec((CH, D), lambda i: (rev(i), 0))
    outs = pl.pallas_call(
        body, name="ssd_bwd", grid=(n_c,),
        in_specs=[row_out(), row_out()] + _proj_specs(CH, (0, 5, 6), rev)
        + _prev8_specs(CH, (5, 6), rev) + [pl.BlockSpec((1, NS, D), lambda i: (rev(i), 0, 0)),
                                            _full((4, D)), _full((4, 512)), _full((1, D)), _full((1, 512)),
                                            _full((1, 128)), _full((1, 128)), _full((1, D)), _full((1, D))],
        out_specs=[row_out(), row_out(), row_out(), _full((4, D)), _full((4, 512)), _full((1, D)), _full((1, 512)),
                   _full((1, 128)), _full((1, 128)), _full((1, D)), _full((1, D))],
        out_shape=[jax.ShapeDtypeStruct((t_n, D), MXU)] * 3 + [
            jax.ShapeDtypeStruct(s, F32) for s in ((4, D), (4, 512), (1, D), (1, 512), (1, 128), (1, 128), (1, D), (1, D))],
        scratch_shapes=[pltpu.VMEM((NS, D), F32), pltpu.VMEM((8, D), F32), pltpu.VMEM((8, 512), F32)],
        compiler_params=_cp("arbitrary"),
    )(dya, yraw, proj, proj, proj, proj, proj, sprev, cwx, cwb, cbx, cbb, dtb, alog, dsk, nrm)
    return outs


LRU_ROWS = 256
LRU_BLK = 256


def _lru_gates(xr, wa, wx, ba, bx, lam):
    pr = jnp.concatenate([_nn(xr[:, LRU_BLK * b:LRU_BLK * (b + 1)], wa[b]) for b in range(D // LRU_BLK)], axis=1) + ba
    pi = jnp.concatenate([_nn(xr[:, LRU_BLK * b:LRU_BLK * (b + 1)], wx[b]) for b in range(D // LRU_BLK)], axis=1) + bx
    log_a = -LRU_C * jax.nn.sigmoid(pr) * _softplus(-lam)
    a = jnp.exp(log_a)
    mult = jnp.sqrt(1.0 - jnp.exp(2.0 * log_a))
    return a, mult * (jax.nn.sigmoid(pi) * xr)


def _lru_out(h, g):
    return h * jax.nn.gelu(g, approximate=True)


def _lru_fwd(proj, cw, cb, wa, wx, ba, bx, lam):
    t_n = proj.shape[0]
    rows = min(LRU_ROWS, t_n)
    fwd = lambda i: i

    def body(g_ref, x_ref, xp_ref, cw_ref, cb_ref, wa_ref, wx_ref, ba_ref, bx_ref, lam_ref, yb_ref, h_ref,
             a_s, u_s, carry):
        i = pl.program_id(0)

        @pl.when(i == 0)
        def _():
            carry[...] = jnp.zeros_like(carry)

        keep = jnp.where(i == 0, 0.0, 1.0)
        xr = _conv_apply(_taps_past(x_ref[...], xp_ref[...] * keep), cw_ref[...], cb_ref[...])
        a, u = _lru_gates(xr, wa_ref[...], wx_ref[...], ba_ref[...], bx_ref[...], lam_ref[...])
        a_s[...] = a
        u_s[...] = u
        row = lax.broadcasted_iota(jnp.int32, (8, D), 0)

        def blk(b, c):
            s = pl.multiple_of(b * 8, 8)
            av = a_s[pl.ds(s, 8), :]
            uv = u_s[pl.ds(s, 8), :]
            for d in (1, 2, 4):
                m = row >= d
                uv = uv + av * jnp.where(m, pltpu.roll(uv, d, 0), 0.0)
                av = av * jnp.where(m, pltpu.roll(av, d, 0), 1.0)
            hv = uv + av * c
            h_ref[pl.ds(s, 8), :] = hv
            return hv[7:8, :]

        carry[0:1, :] = lax.fori_loop(0, rows // 8, blk, carry[0:1, :])
        yb_ref[...] = _lru_out(h_ref[...], g_ref[...])

    return pl.pallas_call(
        body, name="lru_fwd", grid=(t_n // rows,),
        in_specs=_proj_specs(rows, (1, 2), fwd) + _prev8_specs(rows, (2,), fwd) + [
            _full((4, D)), _full((1, D)), _full((4, LRU_BLK, LRU_BLK)), _full((4, LRU_BLK, LRU_BLK)),
            _full((1, D)), _full((1, D)), _full((1, D))],
        out_specs=[pl.BlockSpec((rows, D), lambda i: (i, 0)), pl.BlockSpec((rows, D), lambda i: (i, 0))],
        out_shape=[jax.ShapeDtypeStruct((t_n, D), F32), jax.ShapeDtypeStruct((t_n, D), F32)],
        scratch_shapes=[pltpu.VMEM((rows, D), F32), pltpu.VMEM((rows, D), F32), pltpu.VMEM((8, D), F32)],
        compiler_params=_cp("arbitrary"),
    )(proj, proj, proj, cw, cb, wa, wx, ba, bx, lam)


def _lru_bwd(dyb, proj, h, cw, cb, wa, wx, ba, bx, lam):
    t_n = proj.shape[0]
    rows = min(LRU_ROWS, t_n)
    n_t = t_n // rows
    rev = lambda i: n_t - 1 - i
    rb = rows // 8

    def body(dyb_ref, g_ref, x_ref, xp_ref, h_ref, hp_ref, cw_ref, cb_ref, wa_ref, wx_ref, ba_ref, bx_ref, lam_ref,
             dg_ref, dx_ref, dcw_ref, dcb_ref, dwa_ref, dwx_ref, dba_ref, dbx_ref, dlam_ref,
             a_s, dh_s, hx_s, da_s, du_s, carry, fut):
        i = pl.program_id(0)
        acc_refs = (dcw_ref, dcb_ref, dwa_ref, dwx_ref, dba_ref, dbx_ref, dlam_ref)

        @pl.when(i == 0)
        def _():
            for r in (carry, fut) + acc_refs:
                r[...] = jnp.zeros_like(r)

        keep = jnp.where(i == n_t - 1, 0.0, 1.0)
        taps = _taps_past(x_ref[...], xp_ref[...] * keep)
        xr = _conv_apply(taps, cw_ref[...], cb_ref[...])
        gate_in = (xr, wa_ref[...], wx_ref[...], ba_ref[...], bx_ref[...], lam_ref[...])
        (a, _), gates_vjp = jax.vjp(_lru_gates, *gate_in)
        _, out_vjp = jax.vjp(_lru_out, h_ref[...], g_ref[...])
        dh, dg = out_vjp(dyb_ref[...])
        dg_ref[...] = dg.astype(dg_ref.dtype)
        a_s[...] = a
        dh_s[...] = dh
        hx_s[0:8, :] = hp_ref[...] * keep
        hx_s[8:, :] = h_ref[...]
        row = lax.broadcasted_iota(jnp.int32, (8, D), 0)

        def blk(b, c):
            s = pl.multiple_of((rb - 1 - b) * 8, 8)
            av = a_s[pl.ds(s, 8), :]
            dhv = dh_s[pl.ds(s, 8), :]
            a0 = av
            kv = av * dhv
            for d in (1, 2, 4):
                m = row <= 7 - d
                kv = kv + av * jnp.where(m, pltpu.roll(kv, 8 - d, 0), 0.0)
                av = av * jnp.where(m, pltpu.roll(av, 8 - d, 0), 1.0)
            kv = kv + av * c
            gv = dhv + jnp.where(row < 7, pltpu.roll(kv, 7, 0), c)
            hb = hx_s[pl.ds(s + 8, 8), :]
            hpv = hx_s[pl.ds(s, 8), :]
            hprev = jnp.where(row >= 1, pltpu.roll(hb, 1, 0), hpv[7:8, :])
            du_s[pl.ds(s, 8), :] = gv
            da_s[pl.ds(s, 8), :] = gv * hprev
            del a0
            return kv[0:1, :]

        carry[0:1, :] = lax.fori_loop(0, rb, blk, carry[0:1, :])
        dxr, dwa, dwx, dba, dbx, dlam = gates_vjp((da_s[...], du_s[...]))
        dwa_ref[...] += dwa
        dwx_ref[...] += dwx
        dba_ref[...] += dba
        dbx_ref[...] += dbx
        dlam_ref[...] += dlam
        dcb_ref[...] += _rowsum(dxr)
        for k in range(4):
            dcw_ref[k:k + 1, :] += _rowsum(dxr * taps[k])
        ft = _taps_future(dxr, fut[...])
        cwv = cw_ref[...]
        dx = ft[0] * cwv[0:1, :]
        for k in range(1, 4):
            dx = dx + ft[k] * cwv[k:k + 1, :]
        fut[...] = dxr[0:8, :]
        dx_ref[...] = dx.astype(dx_ref.dtype)

    row_in = lambda: pl.BlockSpec((rows, D), lambda i: (rev(i), 0))
    prev_h = pl.BlockSpec((8, D), lambda i: (jnp.maximum(rev(i) * rb - 1, 0), 0))
    wspec = lambda: _full((4, LRU_BLK, LRU_BLK))
    return pl.pallas_call(
        body, name="lru_bwd", grid=(n_t,),
        in_specs=[row_in()] + _proj_specs(rows, (1, 2), rev) + _prev8_specs(rows, (2,), rev) + [row_in(), prev_h] + [
            _full((4, D)), _full((1, D)), wspec(), wspec(), _full((1, D)), _full((1, D)), _full((1, D))],
        out_specs=[row_in(), row_in(), _full((4, D)), _full((1, D)), wspec(), wspec(), _full((1, D)), _full((1, D)),
                   _full((1, D))],
        out_shape=[jax.ShapeDtypeStruct((t_n, D), MXU)] * 2 + [
            jax.ShapeDtypeStruct(s, F32) for s in ((4, D), (1, D), (4, LRU_BLK, LRU_BLK), (4, LRU_BLK, LRU_BLK),
                                                   (1, D), (1, D), (1, D))],
        scratch_shapes=[pltpu.VMEM((rows, D), F32), pltpu.VMEM((rows, D), F32), pltpu.VMEM((rows + 8, D), F32),
                        pltpu.VMEM((rows, D), F32), pltpu.VMEM((rows, D), F32), pltpu.VMEM((8, D), F32),
                        pltpu.VMEM((8, D), F32)],
        compiler_params=_cp("arbitrary"),
    )(dyb, proj, proj, proj, h, h, cw, cb, wa, wx, ba, bx, lam)


def _merge_out(ya, yb, proj, x, wout, g1):
    t_n = x.shape[0]
    tm = min(t_n, 512)

    def body(ya_ref, yb_ref, ga_ref, gb_ref, x_ref, w_ref, g_ref, h1_ref, mix_ref, mg_ref):
        merged = jax.nn.sigmoid(ga_ref[...]) * ya_ref[...] + jax.nn.sigmoid(gb_ref[...]) * yb_ref[...]
        mg = merged.astype(MXU)
        mg_ref[...] = mg
        mix = jnp.dot(mg, w_ref[...], preferred_element_type=F32)
        mix_ref[...] = mix
        h1_ref[...] = x_ref[...] + _rms(mix, g_ref[...])

    row = lambda: pl.BlockSpec((tm, D), lambda i: (i, 0))
    return pl.pallas_call(
        body, name="merge_out", grid=(t_n // tm,),
        in_specs=[row(), row()] + _proj_specs(tm, (3, 4), lambda i: i) + [row(), _full((D, D)), _full((1, D))],
        out_specs=[row(), row(), row()],
        out_shape=[jax.ShapeDtypeStruct((t_n, D), F32), jax.ShapeDtypeStruct((t_n, D), F32),
                   jax.ShapeDtypeStruct((t_n, D), MXU)],
        compiler_params=_cp("parallel"),
    )(ya, yb, proj, proj, x, wout, g1)


def _out_bwd(dh1, mix, ya, yb, proj, wout, g1):
    t_n = dh1.shape[0]
    tm = min(t_n, 512)

    def body(dh1_ref, mix_ref, ya_ref, yb_ref, ga_ref, gb_ref, w_ref, g_ref,
             dmix_ref, dya_ref, dyb_ref, dga_ref, dgb_ref, dg1_ref):
        @pl.when(pl.program_id(0) == 0)
        def _():
            dg1_ref[...] = jnp.zeros_like(dg1_ref)

        dmix, dg_rows = _rms_bwd(mix_ref[...], g_ref[...], dh1_ref[...])
        dg1_ref[...] += _rowsum(dg_rows)
        dmix_b = dmix.astype(MXU)
        dmix_ref[...] = dmix_b
        dmg = lax.dot_general(dmix_b, w_ref[...], (((1,), (1,)), ((), ())), preferred_element_type=F32)
        sa = jax.nn.sigmoid(ga_ref[...])
        sb = jax.nn.sigmoid(gb_ref[...])
        dya_ref[...] = dmg * sa
        dyb_ref[...] = dmg * sb
        dga_ref[...] = (dmg * ya_ref[...] * sa * (1.0 - sa)).astype(MXU)
        dgb_ref[...] = (dmg * yb_ref[...] * sb * (1.0 - sb)).astype(MXU)

    row = lambda: pl.BlockSpec((tm, D), lambda i: (i, 0))
    return pl.pallas_call(
        body, name="out_bwd", grid=(t_n // tm,),
        in_specs=[row(), row(), row(), row()] + _proj_specs(tm, (3, 4), lambda i: i) + [_full((D, D)), _full((1, D))],
        out_specs=[row(), row(), row(), row(), row(), _full((1, D))],
        out_shape=[jax.ShapeDtypeStruct((t_n, D), MXU), jax.ShapeDtypeStruct((t_n, D), F32),
                   jax.ShapeDtypeStruct((t_n, D), F32), jax.ShapeDtypeStruct((t_n, D), MXU),
                   jax.ShapeDtypeStruct((t_n, D), MXU), jax.ShapeDtypeStruct((1, D), F32)],
        compiler_params=_cp("arbitrary"),
    )(dh1, mix, ya, yb, proj, proj, wout, g1)


MLP_TF = 1024


def _mlp_fwd(h1, g2, wup, wdown, g3, tgt):
    t_n = h1.shape[0]
    tm = min(t_n, 512)
    n_f = FF // MLP_TF

    def body(h1_ref, g2_ref, wu_ref, wd_ref, g3_ref, tgt_ref, hp_ref, v_ref, ff_ref, loss_ref, acc):
        i, j = pl.program_id(0), pl.program_id(1)

        @pl.when((i == 0) & (j == 0))
        def _():
            loss_ref[...] = jnp.zeros_like(loss_ref)

        @pl.when(j == 0)
        def _():
            v_ref[...] = _rms(h1_ref[...], g2_ref[...]).astype(MXU)
            acc[...] = jnp.zeros_like(acc)

        hp = lax.dot_general(v_ref[...], wu_ref[...], (((1,), (1,)), ((), ())), preferred_element_type=F32)
        hp_ref[...] = hp
        hid = jnp.square(jnp.maximum(hp, 0.0))
        acc[...] += jnp.dot(hid.astype(MXU), wd_ref[...], preferred_element_type=F32)

        @pl.when(j == n_f - 1)
        def _():
            ff = acc[...]
            ff_ref[...] = ff
            err = h1_ref[...] + _rms(ff, g3_ref[...]) - tgt_ref[...]
            loss_ref[...] += 0.5 * jnp.sum(jnp.mean(err * err, axis=-1, keepdims=True), axis=0, keepdims=True)

    row = lambda: pl.BlockSpec((tm, D), lambda i, j: (i, 0))
    vec = lambda: pl.BlockSpec((1, D), lambda i, j: (0, 0))
    return pl.pallas_call(
        body, name="mlp_fwd", grid=(t_n // tm, n_f),
        in_specs=[row(), vec(), pl.BlockSpec((MLP_TF, D), lambda i, j: (j, 0)),
                  pl.BlockSpec((MLP_TF, D), lambda i, j: (j, 0)), vec(), row()],
        out_specs=[pl.BlockSpec((tm, MLP_TF), lambda i, j: (i, j)), row(), row(),
                   pl.BlockSpec((1, 1), lambda i, j: (0, 0))],
        out_shape=[jax.ShapeDtypeStruct((t_n, FF), F32), jax.ShapeDtypeStruct((t_n, D), MXU),
                   jax.ShapeDtypeStruct((t_n, D), F32), jax.ShapeDtypeStruct((1, 1), F32)],
        scratch_shapes=[pltpu.VMEM((tm, D), F32)],
        compiler_params=_cp("arbitrary", "arbitrary"),
    )(h1, g2, wup, wdown, g3, tgt)


def _mlp_bwd(h1, ff, tgt, hp, wup, wdown, g2, g3):
    t_n = h1.shape[0]
    tm = min(t_n, 512)
    n_f = FF // MLP_TF

    def body(h1_ref, ff_ref, tgt_ref, hp_ref, wu_ref, wd_ref, g2_ref, g3_ref,
             dh1_ref, dff_ref, dhp_ref, hid_ref, dg2_ref, dg3_ref, acc, dout_s):
        i, j = pl.program_id(0), pl.program_id(1)

        @pl.when((i == 0) & (j == 0))
        def _():
            dg2_ref[...] = jnp.zeros_like(dg2_ref)
            dg3_ref[...] = jnp.zeros_like(dg3_ref)

        @pl.when(j == 0)
        def _():
            ff = ff_ref[...]
            dout = (h1_ref[...] + _rms(ff, g3_ref[...]) - tgt_ref[...]) * (1.0 / D)
            dout_s[...] = dout
            dff, dg_rows = _rms_bwd(ff, g3_ref[...], dout)
            dg3_ref[...] += _rowsum(dg_rows)
            dff_ref[...] = dff.astype(MXU)
            acc[...] = jnp.zeros_like(acc)

        hp = hp_ref[...]
        relu = jnp.maximum(hp, 0.0)
        hid_ref[...] = jnp.square(relu).astype(MXU)
        dhid = lax.dot_general(dff_ref[...], wd_ref[...], (((1,), (1,)), ((), ())), preferred_element_type=F32)
        dhp = (dhid * (2.0 * relu)).astype(MXU)
        dhp_ref[...] = dhp
        acc[...] += jnp.dot(dhp, wu_ref[...], preferred_element_type=F32)

        @pl.when(j == n_f - 1)
        def _():
            dv, dg_rows = _rms_bwd(h1_ref[...], g2_ref[...], acc[...])
            dg2_ref[...] += _rowsum(dg_rows)
            dh1_ref[...] = dout_s[...] + dv

    row = lambda: pl.BlockSpec((tm, D), lambda i, j: (i, 0))
    vec = lambda: pl.BlockSpec((1, D), lambda i, j: (0, 0))
    blk = lambda: pl.BlockSpec((tm, MLP_TF), lambda i, j: (i, j))
    return pl.pallas_call(
        body, name="mlp_bwd", grid=(t_n // tm, n_f),
        in_specs=[row(), row(), row(), blk(), pl.BlockSpec((MLP_TF, D), lambda i, j: (j, 0)),
                  pl.BlockSpec((MLP_TF, D), lambda i, j: (j, 0)), vec(), vec()],
        out_specs=[row(), row(), blk(), blk(), vec(), vec()],
        out_shape=[jax.ShapeDtypeStruct((t_n, D), F32), jax.ShapeDtypeStruct((t_n, D), MXU),
                   jax.ShapeDtypeStruct((t_n, FF), MXU), jax.ShapeDtypeStruct((t_n, FF), MXU),
                   jax.ShapeDtypeStruct((1, D), F32), jax.ShapeDtypeStruct((1, D), F32)],
        scratch_shapes=[pltpu.VMEM((tm, D), F32), pltpu.VMEM((tm, D), F32)],
        compiler_params=_cp("arbitrary", "arbitrary"),
    )(h1, ff, tgt, hp, wup, wdown, g2, g3)


def _wgrad(a, g, name):
    t_n, k_n = a.shape
    n_n = g.shape[1]
    tt = min(t_n, 1024)
    tk, tn = min(k_n, 1024), min(n_n, 1024)

    n_t = t_n // tt

    def body(a_ref, g_ref, o_ref, acc):
        t = pl.program_id(2)

        @pl.when(t == 0)
        def _():
            acc[...] = jnp.zeros_like(acc)

        acc[...] += lax.dot_general(a_ref[...], g_ref[...], (((0,), (0,)), ((), ())), preferred_element_type=F32)

        @pl.when(t == n_t - 1)
        def _():
            o_ref[...] = acc[...].astype(o_ref.dtype)

    return pl.pallas_call(
        body, name=name, grid=(k_n // tk, n_n // tn, n_t),
        in_specs=[pl.BlockSpec((tt, tk), lambda k, n, t: (t, k)), pl.BlockSpec((tt, tn), lambda k, n, t: (t, n))],
        out_specs=pl.BlockSpec((tk, tn), lambda k, n, t: (k, n)),
        out_shape=jax.ShapeDtypeStruct((k_n, n_n), MXU),
        scratch_shapes=[pltpu.VMEM((tk, tn), F32)],
        compiler_params=_cp("parallel", "parallel", "arbitrary"),
    )(a, g)


def _wgrad_segs(segs, g, name):
    t_n, n_n = g.shape
    n_s = len(segs)
    tt = min(t_n, 1024)
    n_t = t_n // tt

    def body(*refs):
        a_refs = refs[:n_s]
        g_ref, o_ref, acc = refs[n_s:]
        s_id, t = pl.program_id(0), pl.program_id(1)

        @pl.when(t == 0)
        def _():
            acc[...] = jnp.zeros_like(acc)

        for s in range(n_s):
            @pl.when(s_id == s)
            def _(s=s):
                acc[...] += lax.dot_general(a_refs[s][...], g_ref[...], (((0,), (0,)), ((), ())),
                                            preferred_element_type=F32)

        @pl.when(t == n_t - 1)
        def _():
            o_ref[...] = acc[...].astype(o_ref.dtype)

    seg_spec = lambda s: pl.BlockSpec((tt, SEG), lambda i, t: (jnp.where(i == s, t, jnp.where(i < s, 0, n_t - 1)), 0))
    return pl.pallas_call(
        body, name=name, grid=(n_s, n_t),
        in_specs=[seg_spec(s) for s in range(n_s)] + [pl.BlockSpec((tt, n_n), lambda i, t: (t, 0))],
        out_specs=pl.BlockSpec((SEG, n_n), lambda i, t: (i, 0)),
        out_shape=jax.ShapeDtypeStruct((n_s * SEG, n_n), MXU),
        scratch_shapes=[pltpu.VMEM((SEG, n_n), F32)],
        compiler_params=_cp("arbitrary", "arbitrary"),
    )(*segs, g)


def _inproj_bwd(dsegs, w, x, g0, dh1):
    t_n = x.shape[0]
    tm = min(t_n, 1024)
    n_k = NP // SEG

    def mm_body(*refs):
        dp_refs = refs[:n_k]
        w_ref, du_ref = refs[n_k:]
        k = pl.program_id(1)

        @pl.when(k == 0)
        def _():
            du_ref[...] = jnp.zeros_like(du_ref)

        for s in range(n_k):
            @pl.when(k == s)
            def _(s=s):
                du_ref[...] += jnp.dot(dp_refs[s][...], w_ref[...], preferred_element_type=F32)

    du = pl.pallas_call(
        mm_body, name="inproj_bwd", grid=(t_n // tm, n_k),
        in_specs=[pl.BlockSpec((tm, SEG), lambda i, k: (i, 0)) for _ in range(n_k)] + [
            pl.BlockSpec((SEG, D), lambda i, k: (k, 0))],
        out_specs=pl.BlockSpec((tm, D), lambda i, k: (i, 0)),
        out_shape=jax.ShapeDtypeStruct((t_n, D), F32),
        compiler_params=_cp("parallel", "arbitrary"),
    )(*dsegs, w)

    tf = min(t_n, 512)

    def fin_body(du_ref, x_ref, g_ref, dh1_ref, dx_ref, dg0_ref):
        @pl.when(pl.program_id(0) == 0)
        def _():
            dg0_ref[...] = jnp.zeros_like(dg0_ref)

        dx, dg_rows = _rms_bwd(x_ref[...], g_ref[...], du_ref[...])
        dg0_ref[...] += _rowsum(dg_rows)
        dx_ref[...] = dh1_ref[...] + dx

    row = lambda: pl.BlockSpec((tf, D), lambda i: (i, 0))
    return pl.pallas_call(
        fin_body, name="grad_x", grid=(t_n // tf,), in_specs=[row(), row(), _full((1, D)), row()],
        out_specs=[row(), _full((1, D))],
        out_shape=[jax.ShapeDtypeStruct((t_n, D), F32), jax.ShapeDtypeStruct((1, D), F32)],
        compiler_params=_cp("arbitrary"),
    )(du, x, g0, dh1)


def _blockdiag4(w):
    w4 = w.reshape(4, 4, 64, 1, 64).astype(MXU)
    same = (jnp.arange(4)[:, None, None, None] == jnp.arange(4)[None, None, :, None])
    return jnp.where(same[None], w4, jnp.zeros((), MXU)).reshape(4, 256, 256)


def _blockdiag4_extract(g):
    g5 = g.reshape(4, 4, 64, 4, 64)
    return jnp.stack([g5[:, q, :, q, :] for q in range(4)], axis=1).reshape(NH, 64, 64)


def _local_step(x, tgt, p, late_weights=None, send_mlp_grads=None, send_late_grads=None):
    f = lambda a: a.astype(F32)
    proj, u = _inproj(x, p["norm_mix_pre"], p["w_in_pT"])
    ssm_params = (p["cw_xs"], p["cw_bc"], p["cb_xs"], p["cb_bc"], p["dt_bias"], p["a_log"], p["d_skip_x"],
                  p["ssm_norm"])
    ya, sprev, yraw = _ssd_fwd(proj, *ssm_params)
    lru_params = (p["conv_lru_w"], p["conv_lru_b"], p["wa_bd"], p["wx_bd"], p["lru_ba"], p["lru_bx"],
                  p["lru_lambda"])
    yb, h = _lru_fwd(proj, *lru_params)
    if late_weights is not None:
        p = dict(p, **late_weights(yb))
    h1, mix, merged = _merge_out(ya, yb, proj, x, p["w_out"], p["norm_mix_post"])
    hp, v, ff, loss = _mlp_fwd(h1, p["norm_mlp_pre"], p["w_upT"], p["w_down"], p["norm_mlp_post"], tgt)

    dh1, dff, dhp, hid, dg2, dg3 = _mlp_bwd(h1, ff, tgt, hp, p["w_upT"], p["w_down"], p["norm_mlp_pre"],
                                            p["norm_mlp_post"])
    d_w_down = _wgrad(hid, dff, "wgrad_down")
    d_w_up_t = _wgrad(dhp, v, "wgrad_up")
    g1 = p["norm_mix_post"]
    if send_mlp_grads is not None:
        g1 = g1 + send_mlp_grads(d_w_up_t, d_w_down)
    dmix, dya, dyb, dga, dgb, dg1 = _out_bwd(dh1, mix, ya, yb, proj, p["w_out"], g1)
    d_w_out = _wgrad(merged, dmix, "wgrad_out")
    (dz, dxs, dbc, dcwx, dcwb, dcbx, dcbb, ddtb, dalog, ddsk, dnrm) = _ssd_bwd(dya, yraw, proj, sprev, *ssm_params)
    (dgl, dxl, dcwl, dcbl, dwa, dwx, dba, dbx, dlam) = _lru_bwd(dyb, proj, h, *lru_params)
    dsegs = [dz, dgl, dxl, dga, dgb, dxs, dbc]
    d_w_in_pt = _wgrad_segs(dsegs, u, "wgrad_in")
    g0 = p["norm_mix_pre"]
    if send_late_grads is not None:
        g0 = g0 + send_late_grads(d_w_out, d_w_in_pt)
    grad_x, dg0 = _inproj_bwd(dsegs, p["w_in_pT"], x, g0, dh1)
    grads = dict(
        norm_mix_pre=dg0, w_in_pT=d_w_in_pt, conv_ssm_w=jnp.concatenate([dcwx, dcwb], axis=1),
        conv_ssm_b=jnp.concatenate([dcbx, dcbb], axis=1), dt_bias=ddtb[:, :NH], a_log=dalog[:, :NH],
        d_skip=f(ddsk).reshape(NH, 64).sum(axis=1)[None, :], ssm_norm=dnrm, conv_lru_w=dcwl, conv_lru_b=dcbl,
        lru_wa=_blockdiag4_extract(dwa), lru_ba=dba, lru_wx=_blockdiag4_extract(dwx), lru_bx=dbx, lru_lambda=dlam,
        w_out=d_w_out, norm_mix_post=dg1, norm_mlp_pre=dg2, w_upT=d_w_up_t, w_down=d_w_down, norm_mlp_post=dg3)
    return loss[0, 0], grad_x, grads


W_IN_COLS = 6672


def _w_in_t_to_padded(wt):
    z, xs, bc, dt = wt[0:1024], wt[1024:2048], wt[2048:2560], wt[2560:2576]
    gl, xl, ga, gb = wt[2576:3600], wt[3600:4624], wt[4624:5648], wt[5648:6672]
    return jnp.concatenate([z, gl, xl, ga, gb, xs, bc, dt, jnp.zeros((NP - 6672, wt.shape[1]), wt.dtype)], axis=0)


def _w_in_t_from_padded(wp):
    z, gl, xl, ga, gb = (wp[SEG * s:SEG * (s + 1)] for s in range(5))
    xs, bc, dt = wp[5120:6144], wp[6144:6656], wp[6656:6672]
    return jnp.concatenate([z, xs, bc, dt, gl, xl, ga, gb], axis=0)


def _prep_params(full, big):
    f = lambda a: a.astype(F32)
    pad128 = lambda a: jnp.pad(f(a).reshape(1, -1), ((0, 0), (0, 128 - a.size)))
    cw = f(full["conv_ssm_w"])
    cb = f(full["conv_ssm_b"]).reshape(1, -1)
    return dict(
        big, norm_mix_pre=f(full["norm_mix_pre"]).reshape(1, D),
        cw_xs=cw[:, :D], cw_bc=cw[:, D:], cb_xs=cb[:, :D], cb_bc=cb[:, D:],
        dt_bias=pad128(full["dt_bias"]), a_log=pad128(full["a_log"]),
        d_skip_x=jnp.repeat(f(full["d_skip"]).reshape(-1), 64).reshape(1, D), ssm_norm=f(full["ssm_norm"]).reshape(1, D),
        conv_lru_w=f(full["conv_lru_w"]), conv_lru_b=f(full["conv_lru_b"]).reshape(1, D),
        wa_bd=_blockdiag4(full["lru_wa"]), wx_bd=_blockdiag4(full["lru_wx"]),
        lru_ba=f(full["lru_ba"]).reshape(1, D), lru_bx=f(full["lru_bx"]).reshape(1, D),
        lru_lambda=f(full["lru_lambda"]).reshape(1, D),
        norm_mix_post=f(full["norm_mix_post"]).reshape(1, D), norm_mlp_pre=f(full["norm_mlp_pre"]).reshape(1, D),
        norm_mlp_post=f(full["norm_mlp_post"]).reshape(1, D))


MESH_ID = pl.DeviceIdType.MESH
ANY = pl.BlockSpec(memory_space=pl.ANY)


def _my_place():
    x, y, c = lax.axis_index("x"), lax.axis_index("y"), lax.axis_index("c")
    return x, y, c, 4 * x + 2 * y + c


def _peer(x, y, c, k):
    return (x ^ ((k >> 2) & 1), y ^ ((k >> 1) & 1), c ^ (k & 1))


def _all_gather(pack, name):
    def body(in_ref, out_ref, send_sems, recv_sems, local_sem):
        x, y, c, me = _my_place()
        sibling = (x, y, 1 - c)
        flips = (4, 2, 6)

        def copy(j, block, to, src=None):
            return pltpu.make_async_remote_copy(
                src_ref=out_ref.at[block] if src is None else src, dst_ref=out_ref.at[block],
                send_sem=send_sems.at[j], recv_sem=recv_sems.at[j], device_id=to, device_id_type=MESH_ID)

        mine = pltpu.make_async_copy(in_ref, out_ref.at[me], local_sem)
        mine.start()
        first = [copy(0, me, sibling, src=in_ref)]
        first += [copy(1 + j, me, _peer(x, y, c, k), src=in_ref) for j, k in enumerate(flips)]
        for cp in first:
            cp.start()
        passed = [copy(4 + j, me ^ k, sibling) for j, k in enumerate(flips)]
        for j, k in enumerate(flips):
            copy(1 + j, me ^ k, (x, y, c)).wait_recv()
            passed[j].start()
        copy(0, me ^ 1, (x, y, c)).wait_recv()
        for j, k in enumerate(flips):
            copy(4 + j, me ^ k ^ 1, (x, y, c)).wait_recv()
        for cp in first + passed:
            cp.wait_send()
        mine.wait()

    return pl.pallas_call(
        body, name=name, in_specs=[ANY], out_specs=ANY,
        out_shape=jax.ShapeDtypeStruct((N_DEV,) + pack.shape, pack.dtype),
        scratch_shapes=[pltpu.SemaphoreType.DMA((N_DEV - 1,)), pltpu.SemaphoreType.DMA((N_DEV - 1,)),
                        pltpu.SemaphoreType.DMA],
    )(pack)


HBM = pl.BlockSpec(memory_space=pltpu.HBM)
SEM = pl.BlockSpec(memory_space=pltpu.SEMAPHORE)
N_PEER = N_DEV - 1


def _peer_copy(k, src_ref, land_ref, sems, scatter):
    x, y, c, me = _my_place()
    return pltpu.make_async_remote_copy(
        src_ref=src_ref.at[me ^ k] if scatter else src_ref, dst_ref=land_ref.at[me], send_sem=sems[k - 1],
        recv_sem=sems[N_PEER + k - 1], device_id=_peer(x, y, c, k), device_id_type=MESH_ID)


def _exchange_start(src, name, scatter):
    rows = src.shape[-2:]

    def body(src_ref, land_ref, *rest):
        sems, token = rest[:2 * N_PEER], rest[2 * N_PEER + 2]
        for k in range(1, N_DEV):
            _peer_copy(k, src_ref, land_ref, sems, scatter).start()
        token[...] = jnp.zeros_like(token)

    land = pltpu.with_memory_space_constraint(lax.empty((N_DEV,) + rows, src.dtype), pltpu.HBM)
    outs = pl.pallas_call(
        body, name=name,
        out_shape=(pltpu.SemaphoreType.DMA(()),) * (2 * N_PEER) + (
            pltpu.HBM(src.shape, src.dtype), pltpu.HBM(land.shape, land.dtype), jax.ShapeDtypeStruct((8, 128), F32)),
        in_specs=(HBM, HBM), out_specs=(SEM,) * (2 * N_PEER) + (HBM, HBM, pl.BlockSpec(memory_space=pltpu.VMEM)),
        input_output_aliases={0: 2 * N_PEER, 1: 2 * N_PEER + 1},
        compiler_params=pltpu.CompilerParams(has_side_effects=pltpu.SideEffectType.DATAFLOW_SIDE_EFFECTING),
    )(pltpu.with_memory_space_constraint(src, pltpu.HBM), land)
    return outs[:2 * N_PEER], outs[2 * N_PEER], outs[2 * N_PEER + 1], outs[2 * N_PEER + 2]


def _exchange_wait(sems, src_thru, land_thru, after, name, scatter):
    def body(src_ref, land_ref, *rest):
        sem_refs = rest[:2 * N_PEER]
        for k in range(1, N_DEV):
            cp = _peer_copy(k, src_ref, land_ref, sem_refs, scatter)
            cp.wait_send()
            cp.wait_recv()

    return pl.pallas_call(
        body, name=name,
        out_shape=(pltpu.HBM(src_thru.shape, src_thru.dtype), pltpu.HBM(land_thru.shape, land_thru.dtype)),
        in_specs=(HBM, HBM) + (SEM,) * (2 * N_PEER) + (pl.BlockSpec(memory_space=pl.ANY),), out_specs=(HBM, HBM),
        input_output_aliases={0: 0, 1: 1},
        compiler_params=pltpu.CompilerParams(has_side_effects=pltpu.SideEffectType.DATAFLOW_SIDE_EFFECTING),
    )(src_thru, land_thru, *sems, after)


def _slot_sum(parts, name):
    r_n, c_n = parts.shape[1:]
    tr = max(t for t in range(16, 513, 16) if r_n % t == 0)

    def body(p_ref, o_ref):
        acc = p_ref[0].astype(F32)
        for k in range(1, N_DEV):
            acc = acc + p_ref[k].astype(F32)
        o_ref[...] = acc

    return pl.pallas_call(
        body, name=name, grid=(r_n // tr,),
        in_specs=[pl.BlockSpec((N_DEV, tr, c_n), lambda i: (0, i, 0))],
        out_specs=pl.BlockSpec((tr, c_n), lambda i: (i, 0)),
        out_shape=jax.ShapeDtypeStruct((r_n, c_n), F32),
        compiler_params=_cp("parallel"),
    )(parts)


def _adam_math(w, g, m, v):
    m = ADAM_B1 * m + (1.0 - ADAM_B1) * g
    v = ADAM_B2 * v + (1.0 - ADAM_B2) * jnp.square(g)
    m_hat = m / (1.0 - ADAM_B1 ** ADAM_STEP)
    v_hat = v / (1.0 - ADAM_B2 ** ADAM_STEP)
    return -ADAM_LR * (m_hat / (jnp.sqrt(v_hat) + ADAM_EPS) + ADAM_WD * w), m, v


def _adam_big(w, g, m, v, name):
    def body(w_ref, g_ref, m_ref, v_ref, d_ref, mo_ref, vo_ref):
        d_ref[...], mo_ref[...], vo_ref[...] = _adam_math(w_ref[...], g_ref[...], m_ref[...], v_ref[...])

    if w.ndim == 3:
        _, r_n, c_n = w.shape
        tr = min(r_n, 256)
        grid = (r_n // tr,)
        blk = lambda: pl.BlockSpec((1, tr, c_n), lambda i: (0, i, 0))
    else:
        r_n, c_n = w.shape
        tc = min(c_n, 256)
        grid = (c_n // tc,)
        blk = lambda: pl.BlockSpec((r_n, tc), lambda i: (0, i))
    return pl.pallas_call(
        body, name=name, grid=grid, in_specs=[blk(), blk(), blk(), blk()], out_specs=[blk(), blk(), blk()],
        out_shape=[jax.ShapeDtypeStruct(w.shape, F32)] * 3, compiler_params=_cp("parallel"),
    )(w, g, m, v)


def _adam_small(groups, where, wmv):
    n, n_g = len(wmv), len(groups)

    def body(*refs):
        g_refs = refs[:n_g]
        w_refs = refs[n_g:n_g + 3 * n]
        o_refs = refs[n_g + 3 * n:]
        for q in range(n):
            w_ref, m_ref, v_ref = w_refs[3 * q:3 * q + 3]
            r, c = w_ref.shape
            gi, r0 = where[q]
            g = g_refs[gi][0, r0:r0 + r, 0:c]
            for k in range(1, N_DEV):
                g = g + g_refs[gi][k, r0:r0 + r, 0:c]
            d, m, v = _adam_math(w_ref[...], g, m_ref[...], v_ref[...])
            o_refs[4 * q][...] = g
            o_refs[4 * q + 1][...] = d
            o_refs[4 * q + 2][...] = m
            o_refs[4 * q + 3][...] = v

    flat_wmv = [a for t in wmv for a in t]
    vm = pl.BlockSpec(memory_space=pltpu.VMEM)
    outs = pl.pallas_call(
        body, name="adam_small", in_specs=[vm] * (n_g + 3 * n), out_specs=[vm] * (4 * n),
        out_shape=[jax.ShapeDtypeStruct(t[0].shape, F32) for t in wmv for _ in range(4)],
        compiler_params=pltpu.CompilerParams(vmem_limit_bytes=VMEM_LIMIT),
    )(*groups, *flat_wmv)
    return [tuple(outs[4 * q:4 * q + 4]) for q in range(n)]


WEIGHTS = ["norm_mix_pre", "w_in", "conv_ssm_w", "conv_ssm_b", "dt_bias", "a_log", "d_skip", "ssm_norm", "conv_lru_w",
           "conv_lru_b", "lru_wa", "lru_ba", "lru_wx", "lru_bx", "lru_lambda", "w_out", "norm_mix_post", "norm_mlp_pre",
           "w_up", "w_down", "norm_mlp_post"]
BIG = ["w_out", "w_up", "w_down", "w_in"]
IN_ROWS = W_IN_COLS // N_DEV
IN_PAD, EARLY_ROWS = 848, 880
ROW_UP, ROW_DOWN, LATE_ROWS = 128, 640, 1152
GRAD_LATE_ROWS = 992
CONV_SSM_COLS, CONV_LRU_COLS = 1536 // N_DEV, D // N_DEV
SMALL = [("norm_mix_pre", (1, D), 0, 0), ("ssm_norm", (1, D), 0, 1), ("conv_lru_b", (1, D), 0, 2),
         ("lru_lambda", (1, D), 0, 3), ("norm_mix_post", (1, D), 0, 4), ("norm_mlp_pre", (1, D), 0, 5),
         ("norm_mlp_post", (1, D), 0, 6), ("conv_ssm_b", (1, 1536), 1, 0), ("dt_bias", (1, NH), 2, 0),
         ("a_log", (1, NH), 2, 1), ("d_skip", (1, NH), 2, 2), ("conv_ssm_w", (4, CONV_SSM_COLS), 3, 0),
         ("conv_lru_w", (4, CONV_LRU_COLS), 4, 0), ("lru_wa", (D, 64), 5, 0), ("lru_wx", (D, 64), 5, D),
         ("lru_ba", (NH, 64), 6, 0), ("lru_bx", (NH, 64), 6, NH)]
SMALL_GROUPS = [(8, D), (1, 1536), (8, 128), (4, 1536), (4, D), (2 * D, 64), (2 * NH, 64)]


def _pad_rows(flat, mult):
    n = flat.shape[0]
    rows = -(-n // (128 * mult)) * mult
    return jnp.pad(flat, (0, rows * 128 - n)).reshape(rows, 128)


def _split3(a):
    hi = a.astype(MXU)
    r1 = a - hi.astype(F32)
    mid = r1.astype(MXU)
    lo = (r1 - mid.astype(F32)).astype(MXU)
    return jnp.stack([hi, mid, lo])


def _early_pack(a):
    bf = lambda t: t.astype(MXU)
    conv = lambda t, c: jnp.pad(_split3(t).reshape(12, c), ((0, 4), (0, D - c)))
    return jnp.concatenate([jnp.pad(bf(a["w_in"][0]).T, ((0, IN_PAD - IN_ROWS), (0, 0))),
                            conv(a["conv_ssm_w"][0], CONV_SSM_COLS), conv(a["conv_lru_w"][0], CONV_LRU_COLS)], axis=0)


def _early_unpack(g):
    w_in_pt = _w_in_t_to_padded(g[:, :IN_ROWS].reshape(W_IN_COLS, D))
    conv = {}
    for n, r0, c in (("conv_ssm_w", IN_PAD, CONV_SSM_COLS), ("conv_lru_w", IN_PAD + 16, CONV_LRU_COLS)):
        s = g[:, r0:r0 + 12, :c].astype(F32).reshape(N_DEV, 3, 4, c)
        conv[n] = ((s[:, 0] + s[:, 1]) + s[:, 2]).transpose(1, 0, 2).reshape(4, N_DEV * c)
    return w_in_pt, conv


def _late_pack(a):
    bf = lambda t: t.astype(MXU)
    return jnp.concatenate([bf(a["w_out"][0]), bf(a["w_up"][0]).T, bf(a["w_down"][0])], axis=0)


def _late_unpack(g):
    return dict(w_out=g[:, :ROW_UP].reshape(D, D), w_upT=g[:, ROW_UP:ROW_DOWN].reshape(FF, D),
                w_down=g[:, ROW_DOWN:].reshape(FF, D))


def _own_slot(land, own):
    me = 4 * lax.axis_index("x") + 2 * lax.axis_index("y") + lax.axis_index("c")
    return lax.dynamic_update_slice_in_dim(land, own[None], me, axis=0)


def kernel(x, norm_mix_pre, w_in, conv_ssm_w, conv_ssm_b, dt_bias, a_log, d_skip, ssm_norm, conv_lru_w, conv_lru_b, lru_wa, lru_ba, lru_wx, lru_bx, lru_lambda, w_out, norm_mix_post, norm_mlp_pre, w_up, w_down, norm_mlp_post, loss_target, m_norm_mix_pre, m_w_in, m_conv_ssm_w, m_conv_ssm_b, m_dt_bias, m_a_log, m_d_skip, m_ssm_norm, m_conv_lru_w, m_conv_lru_b, m_lru_wa, m_lru_ba, m_lru_wx, m_lru_bx, m_lru_lambda, m_w_out, m_norm_mix_post, m_norm_mlp_pre, m_w_up, m_w_down, m_norm_mlp_post, v_norm_mix_pre, v_w_in, v_conv_ssm_w, v_conv_ssm_b, v_dt_bias, v_a_log, v_d_skip, v_ssm_norm, v_conv_lru_w, v_conv_lru_b, v_lru_wa, v_lru_ba, v_lru_wx, v_lru_bx, v_lru_lambda, v_w_out, v_norm_mix_post, v_norm_mlp_pre, v_w_up, v_w_down, v_norm_mlp_post):
    vals = (norm_mix_pre, w_in, conv_ssm_w, conv_ssm_b, dt_bias, a_log, d_skip, ssm_norm, conv_lru_w, conv_lru_b, lru_wa, lru_ba, lru_wx, lru_bx, lru_lambda, w_out, norm_mix_post, norm_mlp_pre, w_up, w_down, norm_mlp_post)
    m_vals = (m_norm_mix_pre, m_w_in, m_conv_ssm_w, m_conv_ssm_b, m_dt_bias, m_a_log, m_d_skip, m_ssm_norm, m_conv_lru_w, m_conv_lru_b, m_lru_wa, m_lru_ba, m_lru_wx, m_lru_bx, m_lru_lambda, m_w_out, m_norm_mix_post, m_norm_mlp_pre, m_w_up, m_w_down, m_norm_mlp_post)
    v_vals = (v_norm_mix_pre, v_w_in, v_conv_ssm_w, v_conv_ssm_b, v_dt_bias, v_a_log, v_d_skip, v_ssm_norm, v_conv_lru_w, v_conv_lru_b, v_lru_wa, v_lru_ba, v_lru_wx, v_lru_bx, v_lru_lambda, v_w_out, v_norm_mix_post, v_norm_mlp_pre, v_w_up, v_w_down, v_norm_mlp_post)
    w = dict(zip(WEIGHTS, vals))
    m = dict(zip(WEIGHTS, m_vals))
    v = dict(zip(WEIGHTS, v_vals))
    me = 4 * lax.axis_index("x") + 2 * lax.axis_index("y") + lax.axis_index("c")

    bf = lambda t: t.astype(MXU)
    late = _late_pack(w)
    early = _all_gather(_early_pack(w), "early_weights_all_gather")
    late, early = lax.optimization_barrier((late, early))
    late_sems, late_src, late_land, token = _exchange_start(late, "late_weights_start", scatter=False)
    w_in_pt, conv_w = _early_unpack(early)
    full = {n: (conv_w[n] if n in conv_w else w[n][0]) for n in WEIGHTS if n not in BIG}
    full["norm_mix_pre"] = full["norm_mix_pre"] + token[0, 0]

    def late_weights(after):
        src, land = _exchange_wait(late_sems, late_src, late_land, after, "late_weights_wait", scatter=False)
        return _late_unpack(_own_slot(land, src))

    sent = {}

    def send_mlp_grads(d_w_up_t, d_w_down):
        src = jnp.concatenate([bf(d_w_up_t).reshape(N_DEV, -1, D), bf(d_w_down).reshape(N_DEV, -1, D)], axis=1)
        sent["sems"], sent["src"], sent["land"], tok = _exchange_start(src, "mlp_grads_start", scatter=True)
        return tok[0, 0]

    def send_late_grads(d_w_out, d_w_in_pt):
        src = jnp.concatenate([
            bf(d_w_out).reshape(N_DEV, -1, D),
            jnp.pad(bf(_w_in_t_from_padded(d_w_in_pt)).reshape(N_DEV, IN_ROWS, D),
                    ((0, 0), (0, GRAD_LATE_ROWS - ROW_UP - IN_ROWS), (0, 0)))], axis=1)
        sent["sems2"], sent["src2"], sent["land2"], tok = _exchange_start(src, "late_grads_start", scatter=True)
        return tok[0, 0]

    loss, grad_x, g = _local_step(x[0], loss_target[0], _prep_params(full, dict(w_in_pT=w_in_pt)), late_weights,
                                  send_mlp_grads, send_late_grads)
    loss = lax.psum(loss, ("x", "y", "c"))

    own = lambda src: lax.dynamic_index_in_dim(src, me, keepdims=False)
    out_g, out_d, out_m, out_v = {}, {}, {}, {}
    mlp_src, mlp_land = _exchange_wait(sent["sems"], sent["src"], sent["land"], grad_x, "mlp_grads_wait", scatter=True)
    g_mlp = _slot_sum(_own_slot(mlp_land, own(mlp_src)), "slot_sum_mlp")
    for n, gn in (("w_up", g_mlp[:FF // N_DEV].T[None]), ("w_down", g_mlp[FF // N_DEV:][None])):
        out_g[n] = gn
        out_d[n], out_m[n], out_v[n] = _adam_big(w[n], gn, m[n], v[n], "adam_" + n)
    zrow = jnp.zeros((1, D), F32)
    pad16 = lambda a: jnp.pad(a, ((0, 0), (0, 128 - NH)))
    small_parts = [
        jnp.concatenate([g["norm_mix_pre"], g["ssm_norm"], g["conv_lru_b"], g["lru_lambda"], g["norm_mix_post"],
                         g["norm_mlp_pre"], g["norm_mlp_post"], zrow], axis=0),
        g["conv_ssm_b"],
        jnp.concatenate([pad16(g["dt_bias"]), pad16(g["a_log"]), pad16(g["d_skip"]), jnp.zeros((5, 128), F32)], axis=0),
        g["conv_ssm_w"], g["conv_lru_w"],
        jnp.concatenate([g["lru_wa"].reshape(D, 64), g["lru_wx"].reshape(D, 64)], axis=0),
        jnp.concatenate([g["lru_ba"].reshape(NH, 64), g["lru_bx"].reshape(NH, 64)], axis=0)]
    small = _pad_rows(jnp.concatenate([s.reshape(-1) for s in small_parts]), 8)
    small, _ = lax.optimization_barrier((small, out_v["w_down"]))
    small_all = _all_gather(small, "small_grads_all_gather")

    lg_src, lg_land = _exchange_wait(sent["sems2"], sent["src2"], sent["land2"], small_all, "late_grads_wait", scatter=True)
    g_late = _slot_sum(_own_slot(lg_land, own(lg_src)), "slot_sum_late")
    gn = g_late[:ROW_UP][None]
    out_g["w_out"] = gn
    out_d["w_out"], out_m["w_out"], out_v["w_out"] = _adam_big(w["w_out"], gn, m["w_out"], v["w_out"], "adam_w_out")
    gt = g_late[ROW_UP:ROW_UP + IN_ROWS]
    dt_, mt_, vt_ = _adam_big(w["w_in"][0].T, gt, m["w_in"][0].T, v["w_in"][0].T, "adam_w_in")
    out_g["w_in"], out_d["w_in"], out_m["w_in"], out_v["w_in"] = gt.T[None], dt_.T[None], mt_.T[None], vt_.T[None]
    sflat = small_all.reshape(N_DEV, -1)
    groups = []
    off = 0
    for r, c in SMALL_GROUPS:
        groups.append(sflat[:, off:off + r * c].reshape(N_DEV, r, c))
        off += r * c
    groups[3] = lax.dynamic_slice_in_dim(groups[3], me * CONV_SSM_COLS, CONV_SSM_COLS, axis=2)
    groups[4] = lax.dynamic_slice_in_dim(groups[4], me * CONV_LRU_COLS, CONV_LRU_COLS, axis=2)
    wmv = [(w[n].reshape(s), m[n].reshape(s), v[n].reshape(s)) for n, s, _, _ in SMALL]
    res = _adam_small(groups, [(gi, r0) for _, _, gi, r0 in SMALL], wmv)
    for (n, _, _, _), (g_n, d_n, m_n, v_n) in zip(SMALL, res):
        shape = w[n].shape
        out_g[n], out_d[n], out_m[n], out_v[n] = (g_n.reshape(shape), d_n.reshape(shape), m_n.reshape(shape),
                                                  v_n.reshape(shape))
    return (loss, grad_x[None], *[out_g[n] for n in WEIGHTS], *[out_d[n] for n in WEIGHTS],
            *[out_m[n] for n in WEIGHTS], *[out_v[n] for n in WEIGHTS])
```

```python
import functools

import jax
import jax.numpy as jnp
from jax import lax
from jax.experimental import pallas as pl
from jax.experimental.pallas import tpu as pltpu

F32 = jnp.float32
MXU = jnp.bfloat16
HI = lax.Precision.HIGHEST
EPS = 1e-6

D = 1024
NH = 16
NS = 128
CH = 128
FF = 4096
NP = 7168
SEG = 1024
LRU_C = 8.0
N_DEV = 8

ADAM_LR, ADAM_B1, ADAM_B2, ADAM_EPS, ADAM_WD, ADAM_STEP = 0.001, 0.9, 0.999, 1e-08, 0.01, 10

VMEM_LIMIT = 56 * 1024 * 1024


def _cp(*sem):
    return pltpu.CompilerParams(dimension_semantics=sem, vmem_limit_bytes=VMEM_LIMIT)


def _nn(a, b):
    return jnp.dot(a.astype(MXU), b.astype(MXU), preferred_element_type=F32)


def _nt(a, b):
    return lax.dot_general(a.astype(MXU), b.astype(MXU), (((1,), (1,)), ((), ())), preferred_element_type=F32)


def _tn(a, b):
    return lax.dot_general(a.astype(MXU), b.astype(MXU), (((0,), (0,)), ((), ())), preferred_element_type=F32)


def _silu(x):
    return x * jax.nn.sigmoid(x)


def _dsilu(x):
    s = jax.nn.sigmoid(x)
    return s + x * s * (1.0 - s)


def _softplus(x):
    return jnp.maximum(x, 0.0) + jnp.log(1.0 + jnp.exp(-jnp.abs(x)))


def _rms(x, g):
    r = lax.rsqrt(jnp.mean(x * x, axis=-1, keepdims=True) + EPS)
    return x * r * g


def _rms_bwd(x, g, dy):
    r = lax.rsqrt(jnp.mean(x * x, axis=-1, keepdims=True) + EPS)
    gdy = g * dy
    dx = r * gdy - x * (r * r * r) * jnp.mean(x * gdy, axis=-1, keepdims=True)
    return dx, dy * x * r


def _rowsum(x):
    return jnp.sum(x, axis=0, keepdims=True)


def _taps_past(cur, prev8):
    r_n, c_n = cur.shape
    row = lax.broadcasted_iota(jnp.int32, (r_n, c_n), 0)
    out = []
    for k in range(4):
        s = 3 - k
        if s == 0:
            out.append(cur)
            continue
        head = jnp.concatenate([pltpu.roll(prev8, s, 0), jnp.zeros((r_n - 8, c_n), F32)], axis=0)
        out.append(jnp.where(row < s, head, pltpu.roll(cur, s, 0)))
    return out


def _taps_future(cur, fut8):
    r_n, c_n = cur.shape
    row = lax.broadcasted_iota(jnp.int32, (r_n, c_n), 0)
    out = []
    for k in range(4):
        s = 3 - k
        if s == 0:
            out.append(cur)
            continue
        tail = jnp.concatenate([jnp.zeros((r_n - 8, c_n), F32), pltpu.roll(fut8, 8 - s, 0)], axis=0)
        out.append(jnp.where(row >= r_n - s, tail, pltpu.roll(cur, r_n - s, 0)))
    return out


def _conv_apply(taps, w, b):
    acc = taps[0] * w[0:1, :]
    for k in range(1, 4):
        acc = acc + taps[k] * w[k:k + 1, :]
    return acc + b


def _inproj(x, g0, w):
    t_n = x.shape[0]
    tm = min(t_n, 1024)

    def body(x_ref, g_ref, w_ref, p_ref, u_ref):
        @pl.when(pl.program_id(1) == 0)
        def _():
            u_ref[...] = _rms(x_ref[...], g_ref[...]).astype(MXU)

        p_ref[...] = lax.dot_general(u_ref[...], w_ref[...], (((1,), (1,)), ((), ())), preferred_element_type=F32)

    return pl.pallas_call(
        body, name="inproj", grid=(t_n // tm, NP // SEG),
        in_specs=[pl.BlockSpec((tm, D), lambda i, j: (i, 0)), pl.BlockSpec((1, D), lambda i, j: (0, 0)),
                  pl.BlockSpec((SEG, D), lambda i, j: (j, 0))],
        out_specs=[pl.BlockSpec((tm, SEG), lambda i, j: (i, j)), pl.BlockSpec((tm, D), lambda i, j: (i, 0))],
        out_shape=[jax.ShapeDtypeStruct((t_n, NP), F32), jax.ShapeDtypeStruct((t_n, D), MXU)],
        compiler_params=_cp("parallel", "arbitrary"),
    )(x, g0, w)


def _ssd_prep(dtraw, dtb, alog):
    l_n = dtraw.shape[0]
    r = lax.broadcasted_iota(jnp.int32, (l_n, l_n), 0)
    c = lax.broadcasted_iota(jnp.int32, (l_n, l_n), 1)
    tril = (r >= c).astype(F32)
    triu = (r <= c).astype(F32)
    eye = (r == c).astype(F32)
    dt = _softplus(dtraw + dtb)
    adt = dt * (-jnp.exp(alog))
    ac = jnp.dot(tril, adt, preferred_element_type=F32, precision=HI)
    tn = (((0,), (0,)), ((), ()))
    ac_t = lax.dot_general(adt, triu, tn, preferred_element_type=F32, precision=HI)
    dt_t = lax.dot_general(dt, eye, tn, preferred_element_type=F32, precision=HI)
    return dt, dt_t, ac, ac_t, _rowsum(adt)


def _ssd_pair(j, xp, bg, cg, sp, dt, dt_t, ac, ac_t, aend):
    l_n = xp.shape[0]
    lane = lax.broadcasted_iota(jnp.int32, (l_n, 128), 1)
    sub = lax.broadcasted_iota(jnp.int32, (128, l_n), 0)
    lane1 = lax.broadcasted_iota(jnp.int32, (1, 128), 1)
    tri = lax.broadcasted_iota(jnp.int32, (l_n, l_n), 0) >= lax.broadcasted_iota(jnp.int32, (l_n, l_n), 1)
    lo = lax.broadcasted_iota(jnp.int32, (l_n, 128), 1) < 64
    lo_s = lax.broadcasted_iota(jnp.int32, (128, 128), 1) < 64
    cb = _nt(cg, bg)
    cs = _nn(cg, sp)
    x2 = jnp.concatenate([jnp.where(lo, xp, 0.0), jnp.where(lo, 0.0, xp)], axis=0)
    ws_, bs_, eo, ee = [], [], [], []
    for e in range(2):
        h = 2 * j + e
        ac_l = jnp.sum(jnp.where(lane == h, ac, 0.0), axis=1, keepdims=True)
        dt_l = jnp.sum(jnp.where(lane == h, dt, 0.0), axis=1, keepdims=True)
        ac_s = jnp.sum(jnp.where(sub == h, ac_t, 0.0), axis=0, keepdims=True)
        dt_s = jnp.sum(jnp.where(sub == h, dt_t, 0.0), axis=0, keepdims=True)
        a_end = jnp.sum(jnp.where(lane1 == h, aend, 0.0), axis=1, keepdims=True)
        decay = jnp.exp(jnp.where(tri, ac_l - ac_s, -1e30))
        ws_.append(cb * decay * dt_s)
        bs_.append(bg * (jnp.exp(a_end - ac_l) * dt_l))
        eo.append(jnp.exp(ac_l))
        ee.append(jnp.exp(a_end))
    y = _nn(jnp.concatenate(ws_, axis=1), x2) + jnp.where(lo, eo[0], eo[1]) * cs
    s_new = _tn(jnp.concatenate(bs_, axis=0), x2) + jnp.where(lo_s, ee[0], ee[1]) * sp
    return y, s_new


def _ssd_post(y, xs, z, dsk, nrm):
    y = (y + dsk * xs) * _silu(z)
    half = D // 2
    ya, yb = y[:, :half], y[:, half:]
    ya = ya * lax.rsqrt(jnp.mean(ya * ya, axis=-1, keepdims=True) + EPS)
    yb = yb * lax.rsqrt(jnp.mean(yb * yb, axis=-1, keepdims=True) + EPS)
    return jnp.concatenate([ya, yb], axis=1) * nrm


def _proj_specs(rows, seg_ids, order):
    return [pl.BlockSpec((rows, SEG), functools.partial(lambda i, s: (order(i), s), s=s)) for s in seg_ids]


def _prev8_specs(rows, seg_ids, order):
    rb = rows // 8
    return [pl.BlockSpec((8, SEG), functools.partial(lambda i, s: (jnp.maximum(order(i) * rb - 1, 0), s), s=s))
            for s in seg_ids]


def _full(shape):
    return pl.BlockSpec(shape, lambda i: (0,) * len(shape))


def _ssd_fwd(proj, cwx, cwb, cbx, cbb, dtb, alog, dsk, nrm):
    t_n = proj.shape[0]
    n_c = t_n // CH
    fwd = lambda i: i

    def body(z_ref, xs_ref, bc_ref, xsp_ref, bcp_ref, cwx_ref, cwb_ref, cbx_ref, cbb_ref, dtb_ref, alog_ref,
             dsk_ref, nrm_ref, ya_ref, sprev_ref, yraw_ref, s_ref):
        c = pl.program_id(0)

        @pl.when(c == 0)
        def _():
            s_ref[...] = jnp.zeros_like(s_ref)

        keep = jnp.where(c == 0, 0.0, 1.0)
        xs_pre = _conv_apply(_taps_past(xs_ref[...], xsp_ref[...] * keep), cwx_ref[...], cbx_ref[...])
        bc_pre = _conv_apply(_taps_past(bc_ref[:, :512], bcp_ref[:, :512] * keep), cwb_ref[...], cbb_ref[...])
        prep = _ssd_prep(bc_ref[:, 512:640], dtb_ref[...], alog_ref[...])
        xs = _silu(xs_pre)
        bc = _silu(bc_pre)
        sprev_ref[0] = s_ref[...]
        ys = []
        for j in range(NH // 2):
            g = j // 4
            yp, sn = _ssd_pair(j, xs[:, 128 * j:128 * j + 128], bc[:, 128 * g:128 * g + 128],
                               bc[:, 256 + 128 * g:384 + 128 * g], s_ref[:, 128 * j:128 * j + 128], *prep)
            ys.append(yp)
            s_ref[:, 128 * j:128 * j + 128] = sn
        y = jnp.concatenate(ys, axis=1)
        yraw_ref[...] = y
        ya_ref[...] = _ssd_post(y, xs, z_ref[...], dsk_ref[...], nrm_ref[...])

    return pl.pallas_call(
        body, name="ssd_fwd", grid=(n_c,),
        in_specs=_proj_specs(CH, (0, 5, 6), fwd) + _prev8_specs(CH, (5, 6), fwd) + [
            _full((4, D)), _full((4, 512)), _full((1, D)), _full((1, 512)), _full((1, 128)), _full((1, 128)),
            _full((1, D)), _full((1, D))],
        out_specs=[pl.BlockSpec((CH, D), lambda i: (i, 0)), pl.BlockSpec((1, NS, D), lambda i: (i, 0, 0)),
                   pl.BlockSpec((CH, D), lambda i: (i, 0))],
        out_shape=[jax.ShapeDtypeStruct((t_n, D), F32), jax.ShapeDtypeStruct((n_c, NS, D), F32),
                   jax.ShapeDtypeStruct((t_n, D), F32)],
        scratch_shapes=[pltpu.VMEM((NS, D), F32)],
        compiler_params=_cp("arbitrary"),
    )(proj, proj, proj, proj, proj, cwx, cwb, cbx, cbb, dtb, alog, dsk, nrm)


def _ssd_bwd(dya, yraw, proj, sprev, cwx, cwb, cbx, cbb, dtb, alog, dsk, nrm):
    t_n = proj.shape[0]
    n_c = t_n // CH
    rev = lambda i: n_c - 1 - i

    def body(dya_ref, yraw_ref, z_ref, xs_ref, bc_ref, xsp_ref, bcp_ref, sprev_ref, cwx_ref, cwb_ref, cbx_ref, cbb_ref,
             dtb_ref, alog_ref, dsk_ref, nrm_ref,
             dz_ref, dxs_ref, dbc_ref, dcwx_ref, dcwb_ref, dcbx_ref, dcbb_ref, ddtb_ref, dalog_ref, ddsk_ref,
             dnrm_ref, ds_ref, futx_ref, futb_ref):
        i = pl.program_id(0)
        acc_refs = (dcwx_ref, dcwb_ref, dcbx_ref, dcbb_ref, ddtb_ref, dalog_ref, ddsk_ref, dnrm_ref)

        @pl.when(i == 0)
        def _():
            for r in (ds_ref, futx_ref, futb_ref) + acc_refs:
                r[...] = jnp.zeros_like(r)

        keep = jnp.where(i == n_c - 1, 0.0, 1.0)
        taps_x = _taps_past(xs_ref[...], xsp_ref[...] * keep)
        taps_b = _taps_past(bc_ref[:, :512], bcp_ref[:, :512] * keep)
        xs_pre = _conv_apply(taps_x, cwx_ref[...], cbx_ref[...])
        bc_pre = _conv_apply(taps_b, cwb_ref[...], cbb_ref[...])
        xs = _silu(xs_pre)
        bc = _silu(bc_pre)
        prep, prep_vjp = jax.vjp(_ssd_prep, bc_ref[:, 512:640], dtb_ref[...], alog_ref[...])
        s_in = sprev_ref[0]

        def pair_args(j):
            g = j // 4
            return (xs[:, 128 * j:128 * j + 128], bc[:, 128 * g:128 * g + 128],
                    bc[:, 256 + 128 * g:384 + 128 * g], s_in[:, 128 * j:128 * j + 128]) + tuple(prep)

        _, post_vjp = jax.vjp(_ssd_post, yraw_ref[...], xs, z_ref[...], dsk_ref[...], nrm_ref[...])
        dy, dxs_skip, dz, ddsk, dnrm = post_vjp(dya_ref[...])
        dz_ref[...] = dz.astype(dz_ref.dtype)
        ddsk_ref[...] += ddsk
        dnrm_ref[...] += dnrm

        dprep = [jnp.zeros_like(p) for p in prep]
        dxp = []
        dbg = [jnp.zeros((CH, 128), F32), jnp.zeros((CH, 128), F32)]
        dcg = [jnp.zeros((CH, 128), F32), jnp.zeros((CH, 128), F32)]
        for j in range(NH // 2):
            g = j // 4
            _, pair_vjp = jax.vjp(functools.partial(_ssd_pair, j), *pair_args(j))
            cts = pair_vjp((dy[:, 128 * j:128 * j + 128], ds_ref[:, 128 * j:128 * j + 128]))
            dxp.append(cts[0])
            dbg[g] = dbg[g] + cts[1]
            dcg[g] = dcg[g] + cts[2]
            ds_ref[:, 128 * j:128 * j + 128] = cts[3]
            dprep = [a + b for a, b in zip(dprep, cts[4:])]
        ddtraw, ddtb, dalog = prep_vjp(tuple(dprep))
        ddtb_ref[...] += ddtb
        dalog_ref[...] += dalog

        dxs_pre = (dxs_skip + jnp.concatenate(dxp, axis=1)) * _dsilu(xs_pre)
        dbc_pre = jnp.concatenate([dbg[0], dbg[1], dcg[0], dcg[1]], axis=1) * _dsilu(bc_pre)
        dcbx_ref[...] += _rowsum(dxs_pre)
        dcbb_ref[...] += _rowsum(dbc_pre)
        for k in range(4):
            dcwx_ref[k:k + 1, :] += _rowsum(dxs_pre * taps_x[k])
            dcwb_ref[k:k + 1, :] += _rowsum(dbc_pre * taps_b[k])
        fx = _taps_future(dxs_pre, futx_ref[...])
        fb = _taps_future(dbc_pre, futb_ref[...])
        cwx = cwx_ref[...]
        cwb = cwb_ref[...]
        dxs_in = fx[0] * cwx[0:1, :]
        dbc_in = fb[0] * cwb[0:1, :]
        for k in range(1, 4):
            dxs_in = dxs_in + fx[k] * cwx[k:k + 1, :]
            dbc_in = dbc_in + fb[k] * cwb[k:k + 1, :]
        futx_ref[...] = dxs_pre[0:8, :]
        futb_ref[...] = dbc_pre[0:8, :]
        dxs_ref[...] = dxs_in.astype(dxs_ref.dtype)
        dbc_ref[...] = jnp.concatenate([dbc_in, ddtraw, jnp.zeros((CH, SEG - 640), F32)], axis=1).astype(dbc_ref.dtype)

    row_out = lambda: pl.BlockSpec((CH, D), lambda i: (rev(i), 0))
    outs = pl.pallas_call(
        body, name="ssd_bwd", grid=(n_c,),
        in_specs=[row_out(), row_out()] + _proj_specs(CH, (0, 5, 6), rev)
        + _prev8_specs(CH, (5, 6), rev) + [pl.BlockSpec((1, NS, D), lambda i: (rev(i), 0, 0)),
                                            _full((4, D)), _full((4, 512)), _full((1, D)), _full((1, 512)),
                                            _full((1, 128)), _full((1, 128)), _full((1, D)), _full((1, D))],
        out_specs=[row_out(), row_out(), row_out(), _full((4, D)), _full((4, 512)), _full((1, D)), _full((1, 512)),
                   _full((1, 128)), _full((1, 128)), _full((1, D)), _full((1, D))],
        out_shape=[jax.ShapeDtypeStruct((t_n, D), MXU)] * 3 + [
            jax.ShapeDtypeStruct(s, F32) for s in ((4, D), (4, 512), (1, D), (1, 512), (1, 128), (1, 128), (1, D), (1, D))],
        scratch_shapes=[pltpu.VMEM((NS, D), F32), pltpu.VMEM((8, D), F32), pltpu.VMEM((8, 512), F32)],
        compiler_params=_cp("arbitrary"),
    )(dya, yraw, proj, proj, proj, proj, proj, sprev, cwx, cwb, cbx, cbb, dtb, alog, dsk, nrm)
    return outs


LRU_ROWS = 256
LRU_BLK = 256


def _lru_gates(xr, wa, wx, ba, bx, lam):
    pr = jnp.concatenate([_nn(xr[:, LRU_BLK * b:LRU_BLK * (b + 1)], wa[b]) for b in range(D // LRU_BLK)], axis=1) + ba
    pi = jnp.concatenate([_nn(xr[:, LRU_BLK * b:LRU_BLK * (b + 1)], wx[b]) for b in range(D // LRU_BLK)], axis=1) + bx
    log_a = -LRU_C * jax.nn.sigmoid(pr) * _softplus(-lam)
    a = jnp.exp(log_a)
    mult = jnp.sqrt(1.0 - jnp.exp(2.0 * log_a))
    return a, mult * (jax.nn.sigmoid(pi) * xr)


def _lru_out(h, g):
    return h * jax.nn.gelu(g, approximate=True)


def _lru_fwd(proj, cw, cb, wa, wx, ba, bx, lam):
    t_n = proj.shape[0]
    rows = min(LRU_ROWS, t_n)
    fwd = lambda i: i

    def body(g_ref, x_ref, xp_ref, cw_ref, cb_ref, wa_ref, wx_ref, ba_ref, bx_ref, lam_ref, yb_ref, h_ref,
             a_s, u_s, carry):
        i = pl.program_id(0)

        @pl.when(i == 0)
        def _():
            carry[...] = jnp.zeros_like(carry)

        keep = jnp.where(i == 0, 0.0, 1.0)
        xr = _conv_apply(_taps_past(x_ref[...], xp_ref[...] * keep), cw_ref[...], cb_ref[...])
        a, u = _lru_gates(xr, wa_ref[...], wx_ref[...], ba_ref[...], bx_ref[...], lam_ref[...])
        a_s[...] = a
        u_s[...] = u
        row = lax.broadcasted_iota(jnp.int32, (8, D), 0)

        def blk(b, c):
            s = pl.multiple_of(b * 8, 8)
            av = a_s[pl.ds(s, 8), :]
            uv = u_s[pl.ds(s, 8), :]
            for d in (1, 2, 4):
                m = row >= d
                uv = uv + av * jnp.where(m, pltpu.roll(uv, d, 0), 0.0)
                av = av * jnp.where(m, pltpu.roll(av, d, 0), 1.0)
            hv = uv + av * c
            h_ref[pl.ds(s, 8), :] = hv
            return hv[7:8, :]

        carry[0:1, :] = lax.fori_loop(0, rows // 8, blk, carry[0:1, :])
        yb_ref[...] = _lru_out(h_ref[...], g_ref[...])

    return pl.pallas_call(
        body, name="lru_fwd", grid=(t_n // rows,),
        in_specs=_proj_specs(rows, (1, 2), fwd) + _prev8_specs(rows, (2,), fwd) + [
            _full((4, D)), _full((1, D)), _full((4, LRU_BLK, LRU_BLK)), _full((4, LRU_BLK, LRU_BLK)),
            _full((1, D)), _full((1, D)), _full((1, D))],
        out_specs=[pl.BlockSpec((rows, D), lambda i: (i, 0)), pl.BlockSpec((rows, D), lambda i: (i, 0))],
        out_shape=[jax.ShapeDtypeStruct((t_n, D), F32), jax.ShapeDtypeStruct((t_n, D), F32)],
        scratch_shapes=[pltpu.VMEM((rows, D), F32), pltpu.VMEM((rows, D), F32), pltpu.VMEM((8, D), F32)],
        compiler_params=_cp("arbitrary"),
    )(proj, proj, proj, cw, cb, wa, wx, ba, bx, lam)


def _lru_bwd(dyb, proj, h, cw, cb, wa, wx, ba, bx, lam):
    t_n = proj.shape[0]
    rows = min(LRU_ROWS, t_n)
    n_t = t_n // rows
    rev = lambda i: n_t - 1 - i
    rb = rows // 8

    def body(dyb_ref, g_ref, x_ref, xp_ref, h_ref, hp_ref, cw_ref, cb_ref, wa_ref, wx_ref, ba_ref, bx_ref, lam_ref,
             dg_ref, dx_ref, dcw_ref, dcb_ref, dwa_ref, dwx_ref, dba_ref, dbx_ref, dlam_ref,
             a_s, dh_s, hx_s, da_s, du_s, carry, fut):
        i = pl.program_id(0)
        acc_refs = (dcw_ref, dcb_ref, dwa_ref, dwx_ref, dba_ref, dbx_ref, dlam_ref)

        @pl.when(i == 0)
        def _():
            for r in (carry, fut) + acc_refs:
                r[...] = jnp.zeros_like(r)

        keep = jnp.where(i == n_t - 1, 0.0, 1.0)
        taps = _taps_past(x_ref[...], xp_ref[...] * keep)
        xr = _conv_apply(taps, cw_ref[...], cb_ref[...])
        gate_in = (xr, wa_ref[...], wx_ref[...], ba_ref[...], bx_ref[...], lam_ref[...])
        (a, _), gates_vjp = jax.vjp(_lru_gates, *gate_in)
        _, out_vjp = jax.vjp(_lru_out, h_ref[...], g_ref[...])
        dh, dg = out_vjp(dyb_ref[...])
        dg_ref[...] = dg.astype(dg_ref.dtype)
        a_s[...] = a
        dh_s[...] = dh
        hx_s[0:8, :] = hp_ref[...] * keep
        hx_s[8:, :] = h_ref[...]
        row = lax.broadcasted_iota(jnp.int32, (8, D), 0)

        def blk(b, c):
            s = pl.multiple_of((rb - 1 - b) * 8, 8)
            av = a_s[pl.ds(s, 8), :]
            dhv = dh_s[pl.ds(s, 8), :]
            a0 = av
            kv = av * dhv
            for d in (1, 2, 4):
                m = row <= 7 - d
                kv = kv + av * jnp.where(m, pltpu.roll(kv, 8 - d, 0), 0.0)
                av = av * jnp.where(m, pltpu.roll(av, 8 - d, 0), 1.0)
            kv = kv + av * c
            gv = dhv + jnp.where(row < 7, pltpu.roll(kv, 7, 0), c)
            hb = hx_s[pl.ds(s + 8, 8), :]
            hpv = hx_s[pl.ds(s, 8), :]
            hprev = jnp.where(row >= 1, pltpu.roll(hb, 1, 0), hpv[7:8, :])
            du_s[pl.ds(s, 8), :] = gv
            da_s[pl.ds(s, 8), :] = gv * hprev
            del a0
            return kv[0:1, :]

        carry[0:1, :] = lax.fori_loop(0, rb, blk, carry[0:1, :])
        dxr, dwa, dwx, dba, dbx, dlam = gates_vjp((da_s[...], du_s[...]))
        dwa_ref[...] += dwa
        dwx_ref[...] += dwx
        dba_ref[...] += dba
        dbx_ref[...] += dbx
        dlam_ref[...] += dlam
        dcb_ref[...] += _rowsum(dxr)
        for k in range(4):
            dcw_ref[k:k + 1, :] += _rowsum(dxr * taps[k])
        ft = _taps_future(dxr, fut[...])
        cwv = cw_ref[...]
        dx = ft[0] * cwv[0:1, :]
        for k in range(1, 4):
            dx = dx + ft[k] * cwv[k:k + 1, :]
        fut[...] = dxr[0:8, :]
        dx_ref[...] = dx.astype(dx_ref.dtype)

    row_in = lambda: pl.BlockSpec((rows, D), lambda i: (rev(i), 0))
    prev_h = pl.BlockSpec((8, D), lambda i: (jnp.maximum(rev(i) * rb - 1, 0), 0))
    wspec = lambda: _full((4, LRU_BLK, LRU_BLK))
    return pl.pallas_call(
        body, name="lru_bwd", grid=(n_t,),
        in_specs=[row_in()] + _proj_specs(rows, (1, 2), rev) + _prev8_specs(rows, (2,), rev) + [row_in(), prev_h] + [
            _full((4, D)), _full((1, D)), wspec(), wspec(), _full((1, D)), _full((1, D)), _full((1, D))],
        out_specs=[row_in(), row_in(), _full((4, D)), _full((1, D)), wspec(), wspec(), _full((1, D)), _full((1, D)),
                   _full((1, D))],
        out_shape=[jax.ShapeDtypeStruct((t_n, D), MXU)] * 2 + [
            jax.ShapeDtypeStruct(s, F32) for s in ((4, D), (1, D), (4, LRU_BLK, LRU_BLK), (4, LRU_BLK, LRU_BLK),
                                                   (1, D), (1, D), (1, D))],
        scratch_shapes=[pltpu.VMEM((rows, D), F32), pltpu.VMEM((rows, D), F32), pltpu.VMEM((rows + 8, D), F32),
                        pltpu.VMEM((rows, D), F32), pltpu.VMEM((rows, D), F32), pltpu.VMEM((8, D), F32),
                        pltpu.VMEM((8, D), F32)],
        compiler_params=_cp("arbitrary"),
    )(dyb, proj, proj, proj, h, h, cw, cb, wa, wx, ba, bx, lam)


def _merge_out(ya, yb, proj, x, wout, g1):
    t_n = x.shape[0]
    tm = min(t_n, 512)

    def body(ya_ref, yb_ref, ga_ref, gb_ref, x_ref, w_ref, g_ref, h1_ref, mix_ref, mg_ref):
        merged = jax.nn.sigmoid(ga_ref[...]) * ya_ref[...] + jax.nn.sigmoid(gb_ref[...]) * yb_ref[...]
        mg = merged.astype(MXU)
        mg_ref[...] = mg
        mix = jnp.dot(mg, w_ref[...], preferred_element_type=F32)
        mix_ref[...] = mix
        h1_ref[...] = x_ref[...] + _rms(mix, g_ref[...])

    row = lambda: pl.BlockSpec((tm, D), lambda i: (i, 0))
    return pl.pallas_call(
        body, name="merge_out", grid=(t_n // tm,),
        in_specs=[row(), row()] + _proj_specs(tm, (3, 4), lambda i: i) + [row(), _full((D, D)), _full((1, D))],
        out_specs=[row(), row(), row()],
        out_shape=[jax.ShapeDtypeStruct((t_n, D), F32), jax.ShapeDtypeStruct((t_n, D), F32),
                   jax.ShapeDtypeStruct((t_n, D), MXU)],
        compiler_params=_cp("parallel"),
    )(ya, yb, proj, proj, x, wout, g1)


def _out_bwd(dh1, mix, ya, yb, proj, wout, g1):
    t_n = dh1.shape[0]
    tm = min(t_n, 512)

    def body(dh1_ref, mix_ref, ya_ref, yb_ref, ga_ref, gb_ref, w_ref, g_ref,
             dmix_ref, dya_ref, dyb_ref, dga_ref, dgb_ref, dg1_ref):
        @pl.when(pl.program_id(0) == 0)
        def _():
            dg1_ref[...] = jnp.zeros_like(dg1_ref)

        dmix, dg_rows = _rms_bwd(mix_ref[...], g_ref[...], dh1_ref[...])
        dg1_ref[...] += _rowsum(dg_rows)
        dmix_b = dmix.astype(MXU)
        dmix_ref[...] = dmix_b
        dmg = lax.dot_general(dmix_b, w_ref[...], (((1,), (1,)), ((), ())), preferred_element_type=F32)
        sa = jax.nn.sigmoid(ga_ref[...])
        sb = jax.nn.sigmoid(gb_ref[...])
        dya_ref[...] = dmg * sa
        dyb_ref[...] = dmg * sb
        dga_ref[...] = (dmg * ya_ref[...] * sa * (1.0 - sa)).astype(MXU)
        dgb_ref[...] = (dmg * yb_ref[...] * sb * (1.0 - sb)).astype(MXU)

    row = lambda: pl.BlockSpec((tm, D), lambda i: (i, 0))
    return pl.pallas_call(
        body, name="out_bwd", grid=(t_n // tm,),
        in_specs=[row(), row(), row(), row()] + _proj_specs(tm, (3, 4), lambda i: i) + [_full((D, D)), _full((1, D))],
        out_specs=[row(), row(), row(), row(), row(), _full((1, D))],
        out_shape=[jax.ShapeDtypeStruct((t_n, D), MXU), jax.ShapeDtypeStruct((t_n, D), F32),
                   jax.ShapeDtypeStruct((t_n, D), F32), jax.ShapeDtypeStruct((t_n, D), MXU),
                   jax.ShapeDtypeStruct((t_n, D), MXU), jax.ShapeDtypeStruct((1, D), F32)],
        compiler_params=_cp("arbitrary"),
    )(dh1, mix, ya, yb, proj, proj, wout, g1)


MLP_TM = 1024
MLP_TF_FWD = 512
MLP_TF_BWD = 1024


def _mlp_fwd(h1, g2, wup, wdown, g3, tgt):
    t_n = h1.shape[0]
    tm = min(t_n, MLP_TM)
    n_f = FF // MLP_TF_FWD

    def body(h1_ref, g2_ref, wu_ref, wd_ref, g3_ref, tgt_ref, hp_ref, v_ref, dout_ref, dff_ref, loss_ref, dg3_ref, acc):
        i, j = pl.program_id(0), pl.program_id(1)

        @pl.when((i == 0) & (j == 0))
        def _():
            loss_ref[...] = jnp.zeros_like(loss_ref)
            dg3_ref[...] = jnp.zeros_like(dg3_ref)

        @pl.when(j == 0)
        def _():
            v_ref[...] = _rms(h1_ref[...], g2_ref[...]).astype(MXU)
            acc[...] = jnp.zeros_like(acc)

        hp = lax.dot_general(v_ref[...], wu_ref[...], (((1,), (1,)), ((), ())), preferred_element_type=F32)
        hp_ref[...] = hp
        hid = jnp.square(jnp.maximum(hp, 0.0))
        acc[...] += jnp.dot(hid.astype(MXU), wd_ref[...], preferred_element_type=F32)

        @pl.when(j == n_f - 1)
        def _():
            ff = acc[...]
            err = h1_ref[...] + _rms(ff, g3_ref[...]) - tgt_ref[...]
            loss_ref[...] += 0.5 * jnp.sum(jnp.mean(err * err, axis=-1, keepdims=True), axis=0, keepdims=True)
            dout = err * (1.0 / D)
            dout_ref[...] = dout
            dff, dg_rows = _rms_bwd(ff, g3_ref[...], dout)
            dg3_ref[...] += _rowsum(dg_rows)
            dff_ref[...] = dff.astype(MXU)

    row = lambda: pl.BlockSpec((tm, D), lambda i, j: (i, 0))
    vec = lambda: pl.BlockSpec((1, D), lambda i, j: (0, 0))
    return pl.pallas_call(
        body, name="mlp_fwd", grid=(t_n // tm, n_f),
        in_specs=[row(), vec(), pl.BlockSpec((MLP_TF_FWD, D), lambda i, j: (j, 0)),
                  pl.BlockSpec((MLP_TF_FWD, D), lambda i, j: (j, 0)), vec(), row()],
        out_specs=[pl.BlockSpec((tm, MLP_TF_FWD), lambda i, j: (i, j)), row(), row(), row(),
                   pl.BlockSpec((1, 1), lambda i, j: (0, 0)), vec()],
        out_shape=[jax.ShapeDtypeStruct((t_n, FF), F32), jax.ShapeDtypeStruct((t_n, D), MXU),
                   jax.ShapeDtypeStruct((t_n, D), F32), jax.ShapeDtypeStruct((t_n, D), MXU),
                   jax.ShapeDtypeStruct((1, 1), F32), jax.ShapeDtypeStruct((1, D), F32)],
        scratch_shapes=[pltpu.VMEM((tm, D), F32)],
        compiler_params=_cp("arbitrary", "arbitrary"),
    )(h1, g2, wup, wdown, g3, tgt)


def _mlp_bwd(dff, dout, hp, h1, wup, wdown, g2):
    t_n = h1.shape[0]
    tm = min(t_n, MLP_TM)
    n_f = FF // MLP_TF_BWD

    def mm_body(dff_ref, hp_ref, wu_ref, wd_ref, dv_ref, dhp_ref, hid_ref):
        @pl.when(pl.program_id(1) == 0)
        def _():
            dv_ref[...] = jnp.zeros_like(dv_ref)

        relu = jnp.maximum(hp_ref[...], 0.0)
        hid_ref[...] = jnp.square(relu).astype(MXU)
        dhid = lax.dot_general(dff_ref[...], wd_ref[...], (((1,), (1,)), ((), ())), preferred_element_type=F32)
        dhp = (dhid * (2.0 * relu)).astype(MXU)
        dhp_ref[...] = dhp
        dv_ref[...] += jnp.dot(dhp, wu_ref[...], preferred_element_type=F32)

    row = lambda: pl.BlockSpec((tm, D), lambda i, j: (i, 0))
    blk = lambda: pl.BlockSpec((tm, MLP_TF_BWD), lambda i, j: (i, j))
    wblk = lambda: pl.BlockSpec((MLP_TF_BWD, D), lambda i, j: (j, 0))
    dv, dhp, hid = pl.pallas_call(
        mm_body, name="mlp_bwd", grid=(t_n // tm, n_f),
        in_specs=[row(), blk(), wblk(), wblk()], out_specs=[row(), blk(), blk()],
        out_shape=[jax.ShapeDtypeStruct((t_n, D), F32), jax.ShapeDtypeStruct((t_n, FF), MXU),
                   jax.ShapeDtypeStruct((t_n, FF), MXU)],
        compiler_params=_cp("parallel", "arbitrary"),
    )(dff, hp, wup, wdown)

    tf = min(t_n, 512)

    def fin_body(dv_ref, h1_ref, g_ref, dout_ref, dh1_ref, dg2_ref):
        @pl.when(pl.program_id(0) == 0)
        def _():
            dg2_ref[...] = jnp.zeros_like(dg2_ref)

        dx, dg_rows = _rms_bwd(h1_ref[...], g_ref[...], dv_ref[...])
        dg2_ref[...] += _rowsum(dg_rows)
        dh1_ref[...] = dout_ref[...] + dx

    frow = lambda: pl.BlockSpec((tf, D), lambda i: (i, 0))
    dh1, dg2 = pl.pallas_call(
        fin_body, name="mlp_bwd_dh1", grid=(t_n // tf,), in_specs=[frow(), frow(), _full((1, D)), frow()],
        out_specs=[frow(), _full((1, D))],
        out_shape=[jax.ShapeDtypeStruct((t_n, D), F32), jax.ShapeDtypeStruct((1, D), F32)],
        compiler_params=_cp("arbitrary"),
    )(dv, h1, g2, dout)
    return dh1, dhp, hid, dg2


def _wgrad(a, g, name):
    t_n, k_n = a.shape
    n_n = g.shape[1]
    tt = min(t_n, 1024)
    tk, tn = min(k_n, 1024), min(n_n, 1024)

    n_t = t_n // tt

    def body(a_ref, g_ref, o_ref, acc):
        t = pl.program_id(2)

        @pl.when(t == 0)
        def _():
            acc[...] = jnp.zeros_like(acc)

        acc[...] += lax.dot_general(a_ref[...], g_ref[...], (((0,), (0,)), ((), ())), preferred_element_type=F32)

        @pl.when(t == n_t - 1)
        def _():
            o_ref[...] = acc[...].astype(o_ref.dtype)

    return pl.pallas_call(
        body, name=name, grid=(k_n // tk, n_n // tn, n_t),
        in_specs=[pl.BlockSpec((tt, tk), lambda k, n, t: (t, k)), pl.BlockSpec((tt, tn), lambda k, n, t: (t, n))],
        out_specs=pl.BlockSpec((tk, tn), lambda k, n, t: (k, n)),
        out_shape=jax.ShapeDtypeStruct((k_n, n_n), MXU),
        scratch_shapes=[pltpu.VMEM((tk, tn), F32)],
        compiler_params=_cp("parallel", "parallel", "arbitrary"),
    )(a, g)


def _wgrad_segs(segs, g, name):
    t_n, n_n = g.shape
    n_s = len(segs)
    tt = min(t_n, 1024)
    n_t = t_n // tt

    def body(*refs):
        a_refs = refs[:n_s]
        g_ref, o_ref, acc = refs[n_s:]
        s_id, t = pl.program_id(0), pl.program_id(1)

        @pl.when(t == 0)
        def _():
            acc[...] = jnp.zeros_like(acc)

        for s in range(n_s):
            @pl.when(s_id == s)
            def _(s=s):
                acc[...] += lax.dot_general(a_refs[s][...], g_ref[...], (((0,), (0,)), ((), ())),
                                            preferred_element_type=F32)

        @pl.when(t == n_t - 1)
        def _():
            o_ref[...] = acc[...].astype(o_ref.dtype)

    seg_spec = lambda s: pl.BlockSpec((tt, SEG), lambda i, t: (jnp.where(i == s, t, jnp.where(i < s, 0, n_t - 1)), 0))
    return pl.pallas_call(
        body, name=name, grid=(n_s, n_t),
        in_specs=[seg_spec(s) for s in range(n_s)] + [pl.BlockSpec((tt, n_n), lambda i, t: (t, 0))],
        out_specs=pl.BlockSpec((SEG, n_n), lambda i, t: (i, 0)),
        out_shape=jax.ShapeDtypeStruct((n_s * SEG, n_n), MXU),
        scratch_shapes=[pltpu.VMEM((SEG, n_n), F32)],
        compiler_params=_cp("arbitrary", "arbitrary"),
    )(*segs, g)


def _inproj_bwd(dsegs, w, x, g0, dh1):
    t_n = x.shape[0]
    tm = min(t_n, 1024)
    n_k = NP // SEG

    def mm_body(*refs):
        dp_refs = refs[:n_k]
        w_ref, du_ref = refs[n_k:]
        k = pl.program_id(1)

        @pl.when(k == 0)
        def _():
            du_ref[...] = jnp.zeros_like(du_ref)

        for s in range(n_k):
            @pl.when(k == s)
            def _(s=s):
                du_ref[...] += jnp.dot(dp_refs[s][...], w_ref[...], preferred_element_type=F32)

    du = pl.pallas_call(
        mm_body, name="inproj_bwd", grid=(t_n // tm, n_k),
        in_specs=[pl.BlockSpec((tm, SEG), functools.partial(
            lambda i, k, s: (jnp.where(k >= s, i, jnp.maximum(i - 1, 0)), 0), s=s)) for s in range(n_k)] + [
            pl.BlockSpec((SEG, D), lambda i, k: (k, 0))],
        out_specs=pl.BlockSpec((tm, D), lambda i, k: (i, 0)),
        out_shape=jax.ShapeDtypeStruct((t_n, D), F32),
        compiler_params=_cp("parallel", "arbitrary"),
    )(*dsegs, w)

    tf = min(t_n, 512)

    def fin_body(du_ref, x_ref, g_ref, dh1_ref, dx_ref, dg0_ref):
        @pl.when(pl.program_id(0) == 0)
        def _():
            dg0_ref[...] = jnp.zeros_like(dg0_ref)

        dx, dg_rows = _rms_bwd(x_ref[...], g_ref[...], du_ref[...])
        dg0_ref[...] += _rowsum(dg_rows)
        dx_ref[...] = dh1_ref[...] + dx

    row = lambda: pl.BlockSpec((tf, D), lambda i: (i, 0))
    return pl.pallas_call(
        fin_body, name="grad_x", grid=(t_n // tf,), in_specs=[row(), row(), _full((1, D)), row()],
        out_specs=[row(), _full((1, D))],
        out_shape=[jax.ShapeDtypeStruct((t_n, D), F32), jax.ShapeDtypeStruct((1, D), F32)],
        compiler_params=_cp("arbitrary"),
    )(du, x, g0, dh1)


def _blockdiag4(w):
    w4 = w.reshape(4, 4, 64, 1, 64).astype(MXU)
    same = (jnp.arange(4)[:, None, None, None] == jnp.arange(4)[None, None, :, None])
    return jnp.where(same[None], w4, jnp.zeros((), MXU)).reshape(4, 256, 256)


def _blockdiag4_extract(g):
    g5 = g.reshape(4, 4, 64, 4, 64)
    return jnp.stack([g5[:, q, :, q, :] for q in range(4)], axis=1).reshape(NH, 64, 64)


def _local_step(x, tgt, p, after_ssd=None, late_weights=None, send_mlp_grads=None, send_late_grads=None):
    f = lambda a: a.astype(F32)
    proj, u = _inproj(x, p["norm_mix_pre"], p["w_in_pT"])
    ssm_params = (p["cw_xs"], p["cw_bc"], p["cb_xs"], p["cb_bc"], p["dt_bias"], p["a_log"], p["d_skip_x"],
                  p["ssm_norm"])
    ya, sprev, yraw = _ssd_fwd(proj, *ssm_params)
    cb_lru = p["conv_lru_b"] if after_ssd is None else p["conv_lru_b"] + after_ssd(ya)
    lru_params = (p["conv_lru_w"], cb_lru, p["wa_bd"], p["wx_bd"], p["lru_ba"], p["lru_bx"], p["lru_lambda"])
    yb, h = _lru_fwd(proj, *lru_params)
    if late_weights is not None:
        p = dict(p, **late_weights(yb))
    h1, mix, merged = _merge_out(ya, yb, proj, x, p["w_out"], p["norm_mix_post"])
    hp, v, dout, dff, loss, dg3 = _mlp_fwd(h1, p["norm_mlp_pre"], p["w_upT"], p["w_down"], p["norm_mlp_post"], tgt)

    dh1, dhp, hid, dg2 = _mlp_bwd(dff, dout, hp, h1, p["w_upT"], p["w_down"], p["norm_mlp_pre"])
    d_w_down = _wgrad(hid, dff, "wgrad_down")
    d_w_up_t = _wgrad(dhp, v, "wgrad_up")
    g1 = p["norm_mix_post"]
    if send_mlp_grads is not None:
        g1 = g1 + send_mlp_grads(d_w_up_t, d_w_down)
    dmix, dya, dyb, dga, dgb, dg1 = _out_bwd(dh1, mix, ya, yb, proj, p["w_out"], g1)
    d_w_out = _wgrad(merged, dmix, "wgrad_out")
    (dz, dxs, dbc, dcwx, dcwb, dcbx, dcbb, ddtb, dalog, ddsk, dnrm) = _ssd_bwd(dya, yraw, proj, sprev, *ssm_params)
    (dgl, dxl, dcwl, dcbl, dwa, dwx, dba, dbx, dlam) = _lru_bwd(dyb, proj, h, *lru_params)
    dsegs = [dz, dgl, dxl, dga, dgb, dxs, dbc]
    d_w_in_pt = _wgrad_segs(dsegs, u, "wgrad_in")
    g0 = p["norm_mix_pre"]
    if send_late_grads is not None:
        g0 = g0 + send_late_grads(d_w_out, d_w_in_pt)
    grad_x, dg0 = _inproj_bwd(dsegs, p["w_in_pT"], x, g0, dh1)
    grads = dict(
        norm_mix_pre=dg0, w_in_pT=d_w_in_pt, conv_ssm_w=jnp.concatenate([dcwx, dcwb], axis=1),
        conv_ssm_b=jnp.concatenate([dcbx, dcbb], axis=1), dt_bias=ddtb[:, :NH], a_log=dalog[:, :NH],
        d_skip=f(ddsk).reshape(NH, 64).sum(axis=1)[None, :], ssm_norm=dnrm, conv_lru_w=dcwl, conv_lru_b=dcbl,
        lru_wa=_blockdiag4_extract(dwa), lru_ba=dba, lru_wx=_blockdiag4_extract(dwx), lru_bx=dbx, lru_lambda=dlam,
        w_out=d_w_out, norm_mix_post=dg1, norm_mlp_pre=dg2, w_upT=d_w_up_t, w_down=d_w_down, norm_mlp_post=dg3)
    return loss[0, 0], grad_x, grads


W_IN_COLS = 6672


def _w_in_t_to_padded(wt):
    z, xs, bc, dt = wt[0:1024], wt[1024:2048], wt[2048:2560], wt[2560:2576]
    gl, xl, ga, gb = wt[2576:3600], wt[3600:4624], wt[4624:5648], wt[5648:6672]
    return jnp.concatenate([z, gl, xl, ga, gb, xs, bc, dt, jnp.zeros((NP - 6672, wt.shape[1]), wt.dtype)], axis=0)


def _w_in_t_from_padded(wp):
    z, gl, xl, ga, gb = (wp[SEG * s:SEG * (s + 1)] for s in range(5))
    xs, bc, dt = wp[5120:6144], wp[6144:6656], wp[6656:6672]
    return jnp.concatenate([z, xs, bc, dt, gl, xl, ga, gb], axis=0)


def _prep_params(full, big):
    f = lambda a: a.astype(F32)
    pad128 = lambda a: jnp.pad(f(a).reshape(1, -1), ((0, 0), (0, 128 - a.size)))
    cw = f(full["conv_ssm_w"])
    cb = f(full["conv_ssm_b"]).reshape(1, -1)
    return dict(
        big, norm_mix_pre=f(full["norm_mix_pre"]).reshape(1, D),
        cw_xs=cw[:, :D], cw_bc=cw[:, D:], cb_xs=cb[:, :D], cb_bc=cb[:, D:],
        dt_bias=pad128(full["dt_bias"]), a_log=pad128(full["a_log"]),
        d_skip_x=jnp.repeat(f(full["d_skip"]).reshape(-1), 64).reshape(1, D), ssm_norm=f(full["ssm_norm"]).reshape(1, D),
        conv_lru_w=f(full["conv_lru_w"]), conv_lru_b=f(full["conv_lru_b"]).reshape(1, D),
        wa_bd=_blockdiag4(full["lru_wa"]), wx_bd=_blockdiag4(full["lru_wx"]),
        lru_ba=f(full["lru_ba"]).reshape(1, D), lru_bx=f(full["lru_bx"]).reshape(1, D),
        lru_lambda=f(full["lru_lambda"]).reshape(1, D),
        norm_mix_post=f(full["norm_mix_post"]).reshape(1, D), norm_mlp_pre=f(full["norm_mlp_pre"]).reshape(1, D),
        norm_mlp_post=f(full["norm_mlp_post"]).reshape(1, D))


MESH_ID = pl.DeviceIdType.MESH
ANY = pl.BlockSpec(memory_space=pl.ANY)


def _my_place():
    x, y, c = lax.axis_index("x"), lax.axis_index("y"), lax.axis_index("c")
    return x, y, c, 4 * x + 2 * y + c


def _peer(x, y, c, k):
    return (x ^ ((k >> 2) & 1), y ^ ((k >> 1) & 1), c ^ (k & 1))


def _all_gather(pack, name):
    def body(in_ref, out_ref, send_sems, recv_sems, local_sem):
        x, y, c, me = _my_place()
        sibling = (x, y, 1 - c)
        flips = (4, 2, 6)

        def copy(j, block, to, src=None):
            return pltpu.make_async_remote_copy(
                src_ref=out_ref.at[block] if src is None else src, dst_ref=out_ref.at[block],
                send_sem=send_sems.at[j], recv_sem=recv_sems.at[j], device_id=to, device_id_type=MESH_ID)

        mine = pltpu.make_async_copy(in_ref, out_ref.at[me], local_sem)
        mine.start()
        first = [copy(0, me, sibling, src=in_ref)]
        first += [copy(1 + j, me, _peer(x, y, c, k), src=in_ref) for j, k in enumerate(flips)]
        for cp in first:
            cp.start()
        passed = [copy(4 + j, me ^ k, sibling) for j, k in enumerate(flips)]
        for j, k in enumerate(flips):
            copy(1 + j, me ^ k, (x, y, c)).wait_recv()
            passed[j].start()
        copy(0, me ^ 1, (x, y, c)).wait_recv()
        for j, k in enumerate(flips):
            copy(4 + j, me ^ k ^ 1, (x, y, c)).wait_recv()
        for cp in first + passed:
            cp.wait_send()
        mine.wait()

    return pl.pallas_call(
        body, name=name, in_specs=[ANY], out_specs=ANY,
        out_shape=jax.ShapeDtypeStruct((N_DEV,) + pack.shape, pack.dtype),
        scratch_shapes=[pltpu.SemaphoreType.DMA((N_DEV - 1,)), pltpu.SemaphoreType.DMA((N_DEV - 1,)),
                        pltpu.SemaphoreType.DMA],
    )(pack)


HBM = pl.BlockSpec(memory_space=pltpu.HBM)
SEM = pl.BlockSpec(memory_space=pltpu.SEMAPHORE)
PLAN_GATHER = tuple((k, "pack", 0) for k in range(1, N_DEV))
PLAN_SCATTER = tuple((k, "slot", 0) for k in range(1, N_DEV))
PLAN_GATHER_ICI = tuple((k, "pack", 0) for k in (2, 4, 6))
PLAN_GATHER_D2D = ((1, "pack", 0),) + tuple((1, s, s) for s in (2, 4, 6))


def _plan_copy(j, plan, src_ref, land_ref, sems):
    k, source, slot = plan[j]
    x, y, c, me = _my_place()
    if source == "pack":
        src = src_ref
    elif source == "slot":
        src = src_ref.at[me ^ k]
    else:
        src = land_ref.at[me ^ source]
    return pltpu.make_async_remote_copy(
        src_ref=src, dst_ref=land_ref.at[me ^ slot], send_sem=sems[j], recv_sem=sems[len(plan) + j],
        device_id=_peer(x, y, c, k), device_id_type=MESH_ID)


def _exchange_start(src, land, plan, name):
    n_c = len(plan)
    if land is None:
        land = pltpu.with_memory_space_constraint(lax.empty((N_DEV,) + src.shape[-2:], src.dtype), pltpu.HBM)

    def body(src_ref, land_ref, *rest):
        sems, token = rest[:2 * n_c], rest[2 * n_c + 2]
        for j in range(n_c):
            _plan_copy(j, plan, src_ref, land_ref, sems).start()
        token[...] = jnp.zeros_like(token)

    outs = pl.pallas_call(
        body, name=name,
        out_shape=(pltpu.SemaphoreType.DMA(()),) * (2 * n_c) + (
            pltpu.HBM(src.shape, src.dtype), pltpu.HBM(land.shape, land.dtype), jax.ShapeDtypeStruct((8, 128), F32)),
        in_specs=(HBM, HBM), out_specs=(SEM,) * (2 * n_c) + (HBM, HBM, pl.BlockSpec(memory_space=pltpu.VMEM)),
        input_output_aliases={0: 2 * n_c, 1: 2 * n_c + 1},
        compiler_params=pltpu.CompilerParams(has_side_effects=pltpu.SideEffectType.DATAFLOW_SIDE_EFFECTING),
    )(pltpu.with_memory_space_constraint(src, pltpu.HBM), land)
    return outs[:2 * n_c], outs[2 * n_c], outs[2 * n_c + 1], outs[2 * n_c + 2]


def _exchange_wait(sems, src_thru, land_thru, after, plan, name):
    n_c = len(plan)

    def body(src_ref, land_ref, *rest):
        for j in range(n_c):
            cp = _plan_copy(j, plan, src_ref, land_ref, rest[:2 * n_c])
            cp.wait_send()
            cp.wait_recv()

    return pl.pallas_call(
        body, name=name,
        out_shape=(pltpu.HBM(src_thru.shape, src_thru.dtype), pltpu.HBM(land_thru.shape, land_thru.dtype)),
        in_specs=(HBM, HBM) + (SEM,) * (2 * n_c) + (pl.BlockSpec(memory_space=pl.ANY),), out_specs=(HBM, HBM),
        input_output_aliases={0: 0, 1: 1},
        compiler_params=pltpu.CompilerParams(has_side_effects=pltpu.SideEffectType.DATAFLOW_SIDE_EFFECTING),
    )(src_thru, land_thru, *sems, after)


def _slot_sum(parts, name):
    r_n, c_n = parts.shape[1:]
    tr = max(t for t in range(16, 513, 16) if r_n % t == 0)

    def body(p_ref, o_ref):
        acc = p_ref[0].astype(F32)
        for k in range(1, N_DEV):
            acc = acc + p_ref[k].astype(F32)
        o_ref[...] = acc

    return pl.pallas_call(
        body, name=name, grid=(r_n // tr,),
        in_specs=[pl.BlockSpec((N_DEV, tr, c_n), lambda i: (0, i, 0))],
        out_specs=pl.BlockSpec((tr, c_n), lambda i: (i, 0)),
        out_shape=jax.ShapeDtypeStruct((r_n, c_n), F32),
        compiler_params=_cp("parallel"),
    )(parts)


def _adam_math(w, g, m, v):
    m = ADAM_B1 * m + (1.0 - ADAM_B1) * g
    v = ADAM_B2 * v + (1.0 - ADAM_B2) * jnp.square(g)
    m_hat = m / (1.0 - ADAM_B1 ** ADAM_STEP)
    v_hat = v / (1.0 - ADAM_B2 ** ADAM_STEP)
    return -ADAM_LR * (m_hat / (jnp.sqrt(v_hat) + ADAM_EPS) + ADAM_WD * w), m, v


def _adam_big(w, g, m, v, name):
    def body(w_ref, g_ref, m_ref, v_ref, d_ref, mo_ref, vo_ref):
        d_ref[...], mo_ref[...], vo_ref[...] = _adam_math(w_ref[...], g_ref[...], m_ref[...], v_ref[...])

    if w.ndim == 3:
        _, r_n, c_n = w.shape
        tr = min(r_n, 256)
        grid = (r_n // tr,)
        blk = lambda: pl.BlockSpec((1, tr, c_n), lambda i: (0, i, 0))
    else:
        r_n, c_n = w.shape
        tc = min(c_n, 256)
        grid = (c_n // tc,)
        blk = lambda: pl.BlockSpec((r_n, tc), lambda i: (0, i))
    return pl.pallas_call(
        body, name=name, grid=grid, in_specs=[blk(), blk(), blk(), blk()], out_specs=[blk(), blk(), blk()],
        out_shape=[jax.ShapeDtypeStruct(w.shape, F32)] * 3, compiler_params=_cp("parallel"),
    )(w, g, m, v)


def _adam_small(groups, where, wmv):
    n, n_g = len(wmv), len(groups)

    def body(*refs):
        g_refs = refs[:n_g]
        w_refs = refs[n_g:n_g + 3 * n]
        o_refs = refs[n_g + 3 * n:]
        for q in range(n):
            w_ref, m_ref, v_ref = w_refs[3 * q:3 * q + 3]
            r, c = w_ref.shape
            gi, r0 = where[q]
            g = g_refs[gi][0, r0:r0 + r, 0:c]
            for k in range(1, N_DEV):
                g = g + g_refs[gi][k, r0:r0 + r, 0:c]
            d, m, v = _adam_math(w_ref[...], g, m_ref[...], v_ref[...])
            o_refs[4 * q][...] = g
            o_refs[4 * q + 1][...] = d
            o_refs[4 * q + 2][...] = m
            o_refs[4 * q + 3][...] = v

    flat_wmv = [a for t in wmv for a in t]
    vm = pl.BlockSpec(memory_space=pltpu.VMEM)
    outs = pl.pallas_call(
        body, name="adam_small", in_specs=[vm] * (n_g + 3 * n), out_specs=[vm] * (4 * n),
        out_shape=[jax.ShapeDtypeStruct(t[0].shape, F32) for t in wmv for _ in range(4)],
        compiler_params=pltpu.CompilerParams(vmem_limit_bytes=VMEM_LIMIT),
    )(*groups, *flat_wmv)
    return [tuple(outs[4 * q:4 * q + 4]) for q in range(n)]


WEIGHTS = ["norm_mix_pre", "w_in", "conv_ssm_w", "conv_ssm_b", "dt_bias", "a_log", "d_skip", "ssm_norm", "conv_lru_w",
           "conv_lru_b", "lru_wa", "lru_ba", "lru_wx", "lru_bx", "lru_lambda", "w_out", "norm_mix_post", "norm_mlp_pre",
           "w_up", "w_down", "norm_mlp_post"]
BIG = ["w_out", "w_up", "w_down", "w_in"]
IN_ROWS = W_IN_COLS // N_DEV
IN_PAD, EARLY_ROWS = 848, 880
ROW_UP, ROW_DOWN, LATE_ROWS = 128, 640, 1152
GRAD_LATE_ROWS = 992
CONV_SSM_COLS, CONV_LRU_COLS = 1536 // N_DEV, D // N_DEV
SMALL = [("norm_mix_pre", (1, D), 0, 0), ("ssm_norm", (1, D), 0, 1), ("conv_lru_b", (1, D), 0, 2),
         ("lru_lambda", (1, D), 0, 3), ("norm_mix_post", (1, D), 0, 4), ("norm_mlp_pre", (1, D), 0, 5),
         ("norm_mlp_post", (1, D), 0, 6), ("conv_ssm_b", (1, 1536), 1, 0), ("dt_bias", (1, NH), 2, 0),
         ("a_log", (1, NH), 2, 1), ("d_skip", (1, NH), 2, 2), ("conv_ssm_w", (4, CONV_SSM_COLS), 3, 0),
         ("conv_lru_w", (4, CONV_LRU_COLS), 4, 0), ("lru_wa", (D, 64), 5, 0), ("lru_wx", (D, 64), 5, D),
         ("lru_ba", (NH, 64), 6, 0), ("lru_bx", (NH, 64), 6, NH)]
SMALL_GROUPS = [(8, D), (1, 1536), (8, 128), (4, 1536), (4, D), (2 * D, 64), (2 * NH, 64)]


def _pad_rows(flat, mult):
    n = flat.shape[0]
    rows = -(-n // (128 * mult)) * mult
    return jnp.pad(flat, (0, rows * 128 - n)).reshape(rows, 128)


def _split3(a):
    hi = a.astype(MXU)
    r1 = a - hi.astype(F32)
    mid = r1.astype(MXU)
    lo = (r1 - mid.astype(F32)).astype(MXU)
    return jnp.stack([hi, mid, lo])


def _early_pack(a):
    bf = lambda t: t.astype(MXU)
    conv = lambda t, c: jnp.pad(_split3(t).reshape(12, c), ((0, 4), (0, D - c)))
    return jnp.concatenate([jnp.pad(bf(a["w_in"][0]).T, ((0, IN_PAD - IN_ROWS), (0, 0))),
                            conv(a["conv_ssm_w"][0], CONV_SSM_COLS), conv(a["conv_lru_w"][0], CONV_LRU_COLS)], axis=0)


def _early_unpack(g):
    w_in_pt = _w_in_t_to_padded(g[:, :IN_ROWS].reshape(W_IN_COLS, D))
    conv = {}
    for n, r0, c in (("conv_ssm_w", IN_PAD, CONV_SSM_COLS), ("conv_lru_w", IN_PAD + 16, CONV_LRU_COLS)):
        s = g[:, r0:r0 + 12, :c].astype(F32).reshape(N_DEV, 3, 4, c)
        conv[n] = ((s[:, 0] + s[:, 1]) + s[:, 2]).transpose(1, 0, 2).reshape(4, N_DEV * c)
    return w_in_pt, conv


def _late_pack(a):
    bf = lambda t: t.astype(MXU)
    return jnp.concatenate([bf(a["w_out"][0]), bf(a["w_up"][0]).T, bf(a["w_down"][0])], axis=0)


def _late_unpack(g):
    return dict(w_out=g[:, :ROW_UP].reshape(D, D), w_upT=g[:, ROW_UP:ROW_DOWN].reshape(FF, D),
                w_down=g[:, ROW_DOWN:].reshape(FF, D))


def _own_slot(land, own):
    me = 4 * lax.axis_index("x") + 2 * lax.axis_index("y") + lax.axis_index("c")
    return lax.dynamic_update_slice_in_dim(land, own[None], me, axis=0)


def kernel(x, norm_mix_pre, w_in, conv_ssm_w, conv_ssm_b, dt_bias, a_log, d_skip, ssm_norm, conv_lru_w, conv_lru_b, lru_wa, lru_ba, lru_wx, lru_bx, lru_lambda, w_out, norm_mix_post, norm_mlp_pre, w_up, w_down, norm_mlp_post, loss_target, m_norm_mix_pre, m_w_in, m_conv_ssm_w, m_conv_ssm_b, m_dt_bias, m_a_log, m_d_skip, m_ssm_norm, m_conv_lru_w, m_conv_lru_b, m_lru_wa, m_lru_ba, m_lru_wx, m_lru_bx, m_lru_lambda, m_w_out, m_norm_mix_post, m_norm_mlp_pre, m_w_up, m_w_down, m_norm_mlp_post, v_norm_mix_pre, v_w_in, v_conv_ssm_w, v_conv_ssm_b, v_dt_bias, v_a_log, v_d_skip, v_ssm_norm, v_conv_lru_w, v_conv_lru_b, v_lru_wa, v_lru_ba, v_lru_wx, v_lru_bx, v_lru_lambda, v_w_out, v_norm_mix_post, v_norm_mlp_pre, v_w_up, v_w_down, v_norm_mlp_post):
    vals = (norm_mix_pre, w_in, conv_ssm_w, conv_ssm_b, dt_bias, a_log, d_skip, ssm_norm, conv_lru_w, conv_lru_b, lru_wa, lru_ba, lru_wx, lru_bx, lru_lambda, w_out, norm_mix_post, norm_mlp_pre, w_up, w_down, norm_mlp_post)
    m_vals = (m_norm_mix_pre, m_w_in, m_conv_ssm_w, m_conv_ssm_b, m_dt_bias, m_a_log, m_d_skip, m_ssm_norm, m_conv_lru_w, m_conv_lru_b, m_lru_wa, m_lru_ba, m_lru_wx, m_lru_bx, m_lru_lambda, m_w_out, m_norm_mix_post, m_norm_mlp_pre, m_w_up, m_w_down, m_norm_mlp_post)
    v_vals = (v_norm_mix_pre, v_w_in, v_conv_ssm_w, v_conv_ssm_b, v_dt_bias, v_a_log, v_d_skip, v_ssm_norm, v_conv_lru_w, v_conv_lru_b, v_lru_wa, v_lru_ba, v_lru_wx, v_lru_bx, v_lru_lambda, v_w_out, v_norm_mix_post, v_norm_mlp_pre, v_w_up, v_w_down, v_norm_mlp_post)
    w = dict(zip(WEIGHTS, vals))
    m = dict(zip(WEIGHTS, m_vals))
    v = dict(zip(WEIGHTS, v_vals))
    me = 4 * lax.axis_index("x") + 2 * lax.axis_index("y") + lax.axis_index("c")

    bf = lambda t: t.astype(MXU)
    late = _late_pack(w)
    early = _all_gather(_early_pack(w), "early_weights_all_gather")
    late, early = lax.optimization_barrier((late, early))
    lw = {}
    lw["sems"], lw["src"], lw["land"], token = _exchange_start(late, None, PLAN_GATHER_ICI, "late_weights_ici_start")
    w_in_pt, conv_w = _early_unpack(early)
    full = {n: (conv_w[n] if n in conv_w else w[n][0]) for n in WEIGHTS if n not in BIG}
    full["norm_mix_pre"] = full["norm_mix_pre"] + token[0, 0]

    def after_ssd(after):
        src, land = _exchange_wait(lw["sems"], lw["src"], lw["land"], after, PLAN_GATHER_ICI, "late_weights_ici_wait")
        lw["sems"], lw["src"], lw["land"], tok = _exchange_start(src, land, PLAN_GATHER_D2D, "late_weights_d2d_start")
        return tok[0, 0]

    def late_weights(after):
        src, land = _exchange_wait(lw["sems"], lw["src"], lw["land"], after, PLAN_GATHER_D2D, "late_weights_d2d_wait")
        return _late_unpack(_own_slot(land, src))

    sent = {}

    def send_mlp_grads(d_w_up_t, d_w_down):
        src = jnp.concatenate([bf(d_w_up_t).reshape(N_DEV, -1, D), bf(d_w_down).reshape(N_DEV, -1, D)], axis=1)
        sent["sems"], sent["src"], sent["land"], tok = _exchange_start(src, None, PLAN_SCATTER, "mlp_grads_start")
        return tok[0, 0]

    def send_late_grads(d_w_out, d_w_in_pt):
        src = jnp.concatenate([
            bf(d_w_out).reshape(N_DEV, -1, D),
            jnp.pad(bf(_w_in_t_from_padded(d_w_in_pt)).reshape(N_DEV, IN_ROWS, D),
                    ((0, 0), (0, GRAD_LATE_ROWS - ROW_UP - IN_ROWS), (0, 0)))], axis=1)
        sent["sems2"], sent["src2"], sent["land2"], tok = _exchange_start(src, None, PLAN_SCATTER, "late_grads_start")
        return tok[0, 0]

    loss, grad_x, g = _local_step(x[0], loss_target[0], _prep_params(full, dict(w_in_pT=w_in_pt)), after_ssd,
                                  late_weights, send_mlp_grads, send_late_grads)
    loss = lax.psum(loss, ("x", "y", "c"))

    own = lambda src: lax.dynamic_index_in_dim(src, me, keepdims=False)
    out_g, out_d, out_m, out_v = {}, {}, {}, {}
    mlp_src, mlp_land = _exchange_wait(sent["sems"], sent["src"], sent["land"], grad_x, PLAN_SCATTER, "mlp_grads_wait")
    g_mlp = _slot_sum(_own_slot(mlp_land, own(mlp_src)), "slot_sum_mlp")
    for n, gn in (("w_up", g_mlp[:FF // N_DEV].T[None]), ("w_down", g_mlp[FF // N_DEV:][None])):
        out_g[n] = gn
        out_d[n], out_m[n], out_v[n] = _adam_big(w[n], gn, m[n], v[n], "adam_" + n)
    zrow = jnp.zeros((1, D), F32)
    pad16 = lambda a: jnp.pad(a, ((0, 0), (0, 128 - NH)))
    small_parts = [
        jnp.concatenate([g["norm_mix_pre"], g["ssm_norm"], g["conv_lru_b"], g["lru_lambda"], g["norm_mix_post"],
                         g["norm_mlp_pre"], g["norm_mlp_post"], zrow], axis=0),
        g["conv_ssm_b"],
        jnp.concatenate([pad16(g["dt_bias"]), pad16(g["a_log"]), pad16(g["d_skip"]), jnp.zeros((5, 128), F32)], axis=0),
        g["conv_ssm_w"], g["conv_lru_w"],
        jnp.concatenate([g["lru_wa"].reshape(D, 64), g["lru_wx"].reshape(D, 64)], axis=0),
        jnp.concatenate([g["lru_ba"].reshape(NH, 64), g["lru_bx"].reshape(NH, 64)], axis=0)]
    small = _pad_rows(jnp.concatenate([s.reshape(-1) for s in small_parts]), 8)
    small, _ = lax.optimization_barrier((small, out_v["w_down"]))
    small_all = _all_gather(small, "small_grads_all_gather")

    lg_src, lg_land = _exchange_wait(sent["sems2"], sent["src2"], sent["land2"], small_all, PLAN_SCATTER, "late_grads_wait")
    g_late = _slot_sum(_own_slot(lg_land, own(lg_src)), "slot_sum_late")
    gn = g_late[:ROW_UP][None]
    out_g["w_out"] = gn
    out_d["w_out"], out_m["w_out"], out_v["w_out"] = _adam_big(w["w_out"], gn, m["w_out"], v["w_out"], "adam_w_out")
    gt = g_late[ROW_UP:ROW_UP + IN_ROWS]
    dt_, mt_, vt_ = _adam_big(w["w_in"][0].T, gt, m["w_in"][0].T, v["w_in"][0].T, "adam_w_in")
    out_g["w_in"], out_d["w_in"], out_m["w_in"], out_v["w_in"] = gt.T[None], dt_.T[None], mt_.T[None], vt_.T[None]
    sflat = small_all.reshape(N_DEV, -1)
    groups = []
    off = 0
    for r, c in SMALL_GROUPS:
        groups.append(sflat[:, off:off + r * c].reshape(N_DEV, r, c))
        off += r * c
    groups[3] = lax.dynamic_slice_in_dim(groups[3], me * CONV_SSM_COLS, CONV_SSM_COLS, axis=2)
    groups[4] = lax.dynamic_slice_in_dim(groups[4], me * CONV_LRU_COLS, CONV_LRU_COLS, axis=2)
    wmv = [(w[n].reshape(s), m[n].reshape(s), v[n].reshape(s)) for n, s, _, _ in SMALL]
    res = _adam_small(groups, [(gi, r0) for _, _, gi, r0 in SMALL], wmv)
    for (n, _, _, _), (g_n, d_n, m_n, v_n) in zip(SMALL, res):
        shape = w[n].shape
        out_g[n], out_d[n], out_m[n], out_v[n] = (g_n.reshape(shape), d_n.reshape(shape), m_n.reshape(shape),
                                                  v_n.reshape(shape))
    return (loss, grad_x[None], *[out_g[n] for n in WEIGHTS], *[out_d[n] for n in WEIGHTS],
            *[out_m[n] for n in WEIGHTS], *[out_v[n] for n in WEIGHTS])
```

```python
import functools

import jax
import jax.numpy as jnp
from jax import lax
from jax.experimental import pallas as pl
from jax.experimental.pallas import tpu as pltpu

F32 = jnp.float32
MXU = jnp.bfloat16
HI = lax.Precision.HIGHEST
EPS = 1e-6

D = 1024
NH = 16
NS = 128
CH = 128
FF = 4096
NP = 7168
SEG = 1024
LRU_C = 8.0
N_DEV = 8

ADAM_LR, ADAM_B1, ADAM_B2, ADAM_EPS, ADAM_WD, ADAM_STEP = 0.001, 0.9, 0.999, 1e-08, 0.01, 10

VMEM_LIMIT = 56 * 1024 * 1024


def _cp(*sem):
    return pltpu.CompilerParams(dimension_semantics=sem, vmem_limit_bytes=VMEM_LIMIT)


def _nn(a, b):
    return jnp.dot(a.astype(MXU), b.astype(MXU), preferred_element_type=F32)


def _nt(a, b):
    return lax.dot_general(a.astype(MXU), b.astype(MXU), (((1,), (1,)), ((), ())), preferred_element_type=F32)


def _tn(a, b):
    return lax.dot_general(a.astype(MXU), b.astype(MXU), (((0,), (0,)), ((), ())), preferred_element_type=F32)


_sigmoid = jax.nn.sigmoid


def _silu(x):
    return x * _sigmoid(x)


def _dsilu(x):
    s = _sigmoid(x)
    return s + x * s * (1.0 - s)


def _softplus(x):
    return jnp.maximum(x, 0.0) + jnp.log(1.0 + jnp.exp(-jnp.abs(x)))


def _rms(x, g):
    r = lax.rsqrt(jnp.mean(x * x, axis=-1, keepdims=True) + EPS)
    return x * r * g


def _rms_bwd(x, g, dy):
    r = lax.rsqrt(jnp.mean(x * x, axis=-1, keepdims=True) + EPS)
    gdy = g * dy
    dx = r * gdy - x * (r * r * r) * jnp.mean(x * gdy, axis=-1, keepdims=True)
    return dx, dy * x * r


def _rowsum(x):
    return jnp.sum(x, axis=0, keepdims=True)


def _taps_past(cur, prev8):
    r_n, c_n = cur.shape
    row = lax.broadcasted_iota(jnp.int32, (r_n, c_n), 0)
    out = []
    for k in range(4):
        s = 3 - k
        if s == 0:
            out.append(cur)
            continue
        head = jnp.concatenate([pltpu.roll(prev8, s, 0), jnp.zeros((r_n - 8, c_n), F32)], axis=0)
        out.append(jnp.where(row < s, head, pltpu.roll(cur, s, 0)))
    return out


def _taps_future(cur, fut8):
    r_n, c_n = cur.shape
    row = lax.broadcasted_iota(jnp.int32, (r_n, c_n), 0)
    out = []
    for k in range(4):
        s = 3 - k
        if s == 0:
            out.append(cur)
            continue
        tail = jnp.concatenate([jnp.zeros((r_n - 8, c_n), F32), pltpu.roll(fut8, 8 - s, 0)], axis=0)
        out.append(jnp.where(row >= r_n - s, tail, pltpu.roll(cur, r_n - s, 0)))
    return out


def _conv_apply(taps, w, b):
    acc = taps[0] * w[0:1, :]
    for k in range(1, 4):
        acc = acc + taps[k] * w[k:k + 1, :]
    return acc + b


def _inproj(x, g0, w):
    t_n = x.shape[0]
    tm = min(t_n, 1024)

    def body(x_ref, g_ref, w_ref, p_ref, u_ref):
        @pl.when(pl.program_id(1) == 0)
        def _():
            u_ref[...] = _rms(x_ref[...], g_ref[...]).astype(MXU)

        p_ref[...] = lax.dot_general(u_ref[...], w_ref[...], (((1,), (1,)), ((), ())), preferred_element_type=F32)

    return pl.pallas_call(
        body, name="inproj", grid=(t_n // tm, NP // SEG),
        in_specs=[pl.BlockSpec((tm, D), lambda i, j: (i, 0)), pl.BlockSpec((1, D), lambda i, j: (0, 0)),
                  pl.BlockSpec((SEG, D), lambda i, j: (j, 0))],
        out_specs=[pl.BlockSpec((tm, SEG), lambda i, j: (i, j)), pl.BlockSpec((tm, D), lambda i, j: (i, 0))],
        out_shape=[jax.ShapeDtypeStruct((t_n, NP), F32), jax.ShapeDtypeStruct((t_n, D), MXU)],
        compiler_params=_cp("parallel", "arbitrary"),
    )(x, g0, w)


def _ssd_prep(dtraw, dtb, alog):
    l_n = dtraw.shape[0]
    r = lax.broadcasted_iota(jnp.int32, (l_n, l_n), 0)
    c = lax.broadcasted_iota(jnp.int32, (l_n, l_n), 1)
    tril = (r >= c).astype(F32)
    triu = (r <= c).astype(F32)
    eye = (r == c).astype(F32)
    dt = _softplus(dtraw + dtb)
    adt = dt * (-jnp.exp(alog))
    ac = jnp.dot(tril, adt, preferred_element_type=F32, precision=HI)
    tn = (((0,), (0,)), ((), ()))
    ac_t = lax.dot_general(adt, triu, tn, preferred_element_type=F32, precision=HI)
    dt_t = lax.dot_general(dt, eye, tn, preferred_element_type=F32, precision=HI)
    return dt, dt_t, ac, ac_t, _rowsum(adt)


def _ssd_pair(j, xp, bg, cg, sp, dt, dt_t, ac, ac_t, aend):
    l_n = xp.shape[0]
    lane = lax.broadcasted_iota(jnp.int32, (l_n, 128), 1)
    sub = lax.broadcasted_iota(jnp.int32, (128, l_n), 0)
    lane1 = lax.broadcasted_iota(jnp.int32, (1, 128), 1)
    tri = lax.broadcasted_iota(jnp.int32, (l_n, l_n), 0) >= lax.broadcasted_iota(jnp.int32, (l_n, l_n), 1)
    lo = lax.broadcasted_iota(jnp.int32, (l_n, 128), 1) < 64
    lo_s = lax.broadcasted_iota(jnp.int32, (128, 128), 1) < 64
    cb = _nt(cg, bg)
    cs = _nn(cg, sp)
    x2 = jnp.concatenate([jnp.where(lo, xp, 0.0), jnp.where(lo, 0.0, xp)], axis=0)
    ws_, bs_, eo, ee = [], [], [], []
    for e in range(2):
        h = 2 * j + e
        ac_l = jnp.sum(jnp.where(lane == h, ac, 0.0), axis=1, keepdims=True)
        dt_l = jnp.sum(jnp.where(lane == h, dt, 0.0), axis=1, keepdims=True)
        a_end = jnp.sum(jnp.where(lane1 == h, aend, 0.0), axis=1, keepdims=True)
        ac_s, dt_s = ac_t[h:h + 1, :], dt_t[h:h + 1, :]
        decay = jnp.exp(jnp.where(tri, ac_l - ac_s, -1e30))
        ws_.append(cb * decay * dt_s)
        bs_.append(bg * (jnp.exp(a_end - ac_l) * dt_l))
        eo.append(jnp.exp(ac_l))
        ee.append(jnp.exp(a_end))
    y = _nn(jnp.concatenate(ws_, axis=1), x2) + jnp.where(lo, eo[0], eo[1]) * cs
    s_new = _tn(jnp.concatenate(bs_, axis=0), x2) + jnp.where(lo_s, ee[0], ee[1]) * sp
    return y, s_new


def _ssd_post(y, xs, z, dsk, nrm):
    y = (y + dsk * xs) * _silu(z)
    half = D // 2
    ya, yb = y[:, :half], y[:, half:]
    ya = ya * lax.rsqrt(jnp.mean(ya * ya, axis=-1, keepdims=True) + EPS)
    yb = yb * lax.rsqrt(jnp.mean(yb * yb, axis=-1, keepdims=True) + EPS)
    return jnp.concatenate([ya, yb], axis=1) * nrm


def _proj_specs(rows, seg_ids, order):
    return [pl.BlockSpec((rows, SEG), functools.partial(lambda i, s: (order(i), s), s=s)) for s in seg_ids]


def _prev8_specs(rows, seg_ids, order):
    rb = rows // 8
    return [pl.BlockSpec((8, SEG), functools.partial(lambda i, s: (jnp.maximum(order(i) * rb - 1, 0), s), s=s))
            for s in seg_ids]


def _full(shape):
    return pl.BlockSpec(shape, lambda i: (0,) * len(shape))


def _ssd_fwd(proj, cwx, cwb, cbx, cbb, dtb, alog, dsk, nrm):
    t_n = proj.shape[0]
    n_c = t_n // CH
    fwd = lambda i: i

    def body(z_ref, xs_ref, bc_ref, xsp_ref, bcp_ref, cwx_ref, cwb_ref, cbx_ref, cbb_ref, dtb_ref, alog_ref,
             dsk_ref, nrm_ref, ya_ref, sprev_ref, yraw_ref, s_ref):
        c = pl.program_id(0)

        @pl.when(c == 0)
        def _():
            s_ref[...] = jnp.zeros_like(s_ref)

        keep = jnp.where(c == 0, 0.0, 1.0)
        xs_pre = _conv_apply(_taps_past(xs_ref[...], xsp_ref[...] * keep), cwx_ref[...], cbx_ref[...])
        bc_pre = _conv_apply(_taps_past(bc_ref[:, :512], bcp_ref[:, :512] * keep), cwb_ref[...], cbb_ref[...])
        prep = _ssd_prep(bc_ref[:, 512:640], dtb_ref[...], alog_ref[...])
        xs = _silu(xs_pre)
        bc = _silu(bc_pre)
        sprev_ref[0] = s_ref[...]
        ys = []
        for j in range(NH // 2):
            g = j // 4
            yp, sn = _ssd_pair(j, xs[:, 128 * j:128 * j + 128], bc[:, 128 * g:128 * g + 128],
                               bc[:, 256 + 128 * g:384 + 128 * g], s_ref[:, 128 * j:128 * j + 128], *prep)
            ys.append(yp)
            s_ref[:, 128 * j:128 * j + 128] = sn
        y = jnp.concatenate(ys, axis=1)
        yraw_ref[...] = y
        ya_ref[...] = _ssd_post(y, xs, z_ref[...], dsk_ref[...], nrm_ref[...])

    return pl.pallas_call(
        body, name="ssd_fwd", grid=(n_c,),
        in_specs=_proj_specs(CH, (0, 5, 6), fwd) + _prev8_specs(CH, (5, 6), fwd) + [
            _full((4, D)), _full((4, 512)), _full((1, D)), _full((1, 512)), _full((1, 128)), _full((1, 128)),
            _full((1, D)), _full((1, D))],
        out_specs=[pl.BlockSpec((CH, D), lambda i: (i, 0)), pl.BlockSpec((1, NS, D), lambda i: (i, 0, 0)),
                   pl.BlockSpec((CH, D), lambda i: (i, 0))],
        out_shape=[jax.ShapeDtypeStruct((t_n, D), F32), jax.ShapeDtypeStruct((n_c, NS, D), F32),
                   jax.ShapeDtypeStruct((t_n, D), F32)],
        scratch_shapes=[pltpu.VMEM((NS, D), F32)],
        compiler_params=_cp("arbitrary"),
    )(proj, proj, proj, proj, proj, cwx, cwb, cbx, cbb, dtb, alog, dsk, nrm)


def _ssd_bwd(dya, yraw, proj, sprev, cwx, cwb, cbx, cbb, dtb, alog, dsk, nrm, mlp_ops):
    t_n = proj.shape[0]
    n_c = t_n // CH
    rev = lambda i: n_c - 1 - i
    fb = FF // n_c

    def body(dya_ref, yraw_ref, z_ref, xs_ref, bc_ref, xsp_ref, bcp_ref, sprev_ref, cwx_ref, cwb_ref, cbx_ref, cbb_ref,
             dtb_ref, alog_ref, dsk_ref, nrm_ref, hid_ref, dff_ref, dhp_ref, v_ref,
             dz_ref, dxs_ref, dbc_ref, dcwx_ref, dcwb_ref, dcbx_ref, dcbb_ref, ddtb_ref, dalog_ref, ddsk_ref,
             dnrm_ref, dwd_ref, dwu_ref, ds_ref, futx_ref, futb_ref):
        i = pl.program_id(0)
        acc_refs = (dcwx_ref, dcwb_ref, dcbx_ref, dcbb_ref, ddtb_ref, dalog_ref, ddsk_ref, dnrm_ref)
        tn = (((0,), (0,)), ((), ()))
        dwd_ref[...] = lax.dot_general(hid_ref[...], dff_ref[...], tn, preferred_element_type=F32).astype(MXU)
        dwu_ref[...] = lax.dot_general(dhp_ref[...], v_ref[...], tn, preferred_element_type=F32).astype(MXU)

        @pl.when(i == 0)
        def _():
            for r in (ds_ref, futx_ref, futb_ref) + acc_refs:
                r[...] = jnp.zeros_like(r)

        keep = jnp.where(i == n_c - 1, 0.0, 1.0)
        taps_x = _taps_past(xs_ref[...], xsp_ref[...] * keep)
        taps_b = _taps_past(bc_ref[:, :512], bcp_ref[:, :512] * keep)
        xs_pre = _conv_apply(taps_x, cwx_ref[...], cbx_ref[...])
        bc_pre = _conv_apply(taps_b, cwb_ref[...], cbb_ref[...])
        xs = _silu(xs_pre)
        bc = _silu(bc_pre)
        prep, prep_vjp = jax.vjp(_ssd_prep, bc_ref[:, 512:640], dtb_ref[...], alog_ref[...])
        s_in = sprev_ref[0]

        def pair_args(j):
            g = j // 4
            return (xs[:, 128 * j:128 * j + 128], bc[:, 128 * g:128 * g + 128],
                    bc[:, 256 + 128 * g:384 + 128 * g], s_in[:, 128 * j:128 * j + 128]) + tuple(prep)

        _, post_vjp = jax.vjp(_ssd_post, yraw_ref[...], xs, z_ref[...], dsk_ref[...], nrm_ref[...])
        dy, dxs_skip, dz, ddsk, dnrm = post_vjp(dya_ref[...])
        dz_ref[...] = dz.astype(dz_ref.dtype)
        ddsk_ref[...] += ddsk
        dnrm_ref[...] += dnrm

        dprep = [jnp.zeros_like(p) for p in prep]
        dxp = []
        dbg = [jnp.zeros((CH, 128), F32), jnp.zeros((CH, 128), F32)]
        dcg = [jnp.zeros((CH, 128), F32), jnp.zeros((CH, 128), F32)]
        for j in range(NH // 2):
            g = j // 4
            _, pair_vjp = jax.vjp(functools.partial(_ssd_pair, j), *pair_args(j))
            cts = pair_vjp((dy[:, 128 * j:128 * j + 128], ds_ref[:, 128 * j:128 * j + 128]))
            dxp.append(cts[0])
            dbg[g] = dbg[g] + cts[1]
            dcg[g] = dcg[g] + cts[2]
            ds_ref[:, 128 * j:128 * j + 128] = cts[3]
            dprep = [a + b for a, b in zip(dprep, cts[4:])]
        ddtraw, ddtb, dalog = prep_vjp(tuple(dprep))
        ddtb_ref[...] += ddtb
        dalog_ref[...] += dalog

        dxs_pre = (dxs_skip + jnp.concatenate(dxp, axis=1)) * _dsilu(xs_pre)
        dbc_pre = jnp.concatenate([dbg[0], dbg[1], dcg[0], dcg[1]], axis=1) * _dsilu(bc_pre)
        dcbx_ref[...] += _rowsum(dxs_pre)
        dcbb_ref[...] += _rowsum(dbc_pre)
        for k in range(4):
            dcwx_ref[k:k + 1, :] += _rowsum(dxs_pre * taps_x[k])
            dcwb_ref[k:k + 1, :] += _rowsum(dbc_pre * taps_b[k])
        fx = _taps_future(dxs_pre, futx_ref[...])
        fb = _taps_future(dbc_pre, futb_ref[...])
        cwx = cwx_ref[...]
        cwb = cwb_ref[...]
        dxs_in = fx[0] * cwx[0:1, :]
        dbc_in = fb[0] * cwb[0:1, :]
        for k in range(1, 4):
            dxs_in = dxs_in + fx[k] * cwx[k:k + 1, :]
            dbc_in = dbc_in + fb[k] * cwb[k:k + 1, :]
        futx_ref[...] = dxs_pre[0:8, :]
        futb_ref[...] = dbc_pre[0:8, :]
        dxs_ref[...] = dxs_in.astype(dxs_ref.dtype)
        dbc_ref[...] = jnp.concatenate([dbc_in, ddtraw, jnp.zeros((CH, SEG - 640), F32)], axis=1).astype(dbc_ref.dtype)

    row_out = lambda: pl.BlockSpec((CH, D), lambda i: (rev(i), 0))
    outs = pl.pallas_call(
        body, name="ssd_bwd", grid=(n_c,),
        in_specs=[row_out(), row_out()] + _proj_specs(CH, (0, 5, 6), rev)
        + _prev8_specs(CH, (5, 6), rev) + [pl.BlockSpec((1, NS, D), lambda i: (rev(i), 0, 0)),
                                            _full((4, D)), _full((4, 512)), _full((1, D)), _full((1, 512)),
                                            _full((1, 128)), _full((1, 128)), _full((1, D)), _full((1, D)),
                                            pl.BlockSpec((t_n, fb), lambda i: (0, i)), _full((t_n, D)),
                                            pl.BlockSpec((t_n, fb), lambda i: (0, i)), _full((t_n, D))],
        out_specs=[row_out(), row_out(), row_out(), _full((4, D)), _full((4, 512)), _full((1, D)), _full((1, 512)),
                   _full((1, 128)), _full((1, 128)), _full((1, D)), _full((1, D)),
                   pl.BlockSpec((fb, D), lambda i: (i, 0)), pl.BlockSpec((fb, D), lambda i: (i, 0))],
        out_shape=[jax.ShapeDtypeStruct((t_n, D), MXU)] * 3 + [
            jax.ShapeDtypeStruct(s, F32) for s in ((4, D), (4, 512), (1, D), (1, 512), (1, 128), (1, 128), (1, D), (1, D))]
        + [jax.ShapeDtypeStruct((FF, D), MXU)] * 2,
        scratch_shapes=[pltpu.VMEM((NS, D), F32), pltpu.VMEM((8, D), F32), pltpu.VMEM((8, 512), F32)],
        compiler_params=_cp("arbitrary"),
    )(dya, yraw, proj, proj, proj, proj, proj, sprev, cwx, cwb, cbx, cbb, dtb, alog, dsk, nrm, *mlp_ops)
    return outs


LRU_ROWS = 256
LRU_BLK = 256


def _lru_gates(xr, wa, wx, ba, bx, lam):
    pr = jnp.concatenate([_nn(xr[:, LRU_BLK * b:LRU_BLK * (b + 1)], wa[b]) for b in range(D // LRU_BLK)], axis=1) + ba
    pi = jnp.concatenate([_nn(xr[:, LRU_BLK * b:LRU_BLK * (b + 1)], wx[b]) for b in range(D // LRU_BLK)], axis=1) + bx
    log_a = -LRU_C * _sigmoid(pr) * _softplus(-lam)
    a = jnp.exp(log_a)
    mult = jnp.sqrt(1.0 - jnp.exp(2.0 * log_a))
    return a, mult * (_sigmoid(pi) * xr)


def _lru_out(h, g):
    return h * jax.nn.gelu(g, approximate=True)


def _lru_fwd(proj, cw, cb, wa, wx, ba, bx, lam):
    t_n = proj.shape[0]
    rows = min(LRU_ROWS, t_n)
    fwd = lambda i: i

    def body(g_ref, x_ref, xp_ref, cw_ref, cb_ref, wa_ref, wx_ref, ba_ref, bx_ref, lam_ref, yb_ref, h_ref,
             a_s, u_s, carry):
        i = pl.program_id(0)

        @pl.when(i == 0)
        def _():
            carry[...] = jnp.zeros_like(carry)

        keep = jnp.where(i == 0, 0.0, 1.0)
        xr = _conv_apply(_taps_past(x_ref[...], xp_ref[...] * keep), cw_ref[...], cb_ref[...])
        a, u = _lru_gates(xr, wa_ref[...], wx_ref[...], ba_ref[...], bx_ref[...], lam_ref[...])
        a_s[...] = a
        u_s[...] = u
        row = lax.broadcasted_iota(jnp.int32, (8, D), 0)

        def blk(b, c):
            s = pl.multiple_of(b * 8, 8)
            av = a_s[pl.ds(s, 8), :]
            uv = u_s[pl.ds(s, 8), :]
            for d in (1, 2, 4):
                m = row >= d
                uv = uv + av * jnp.where(m, pltpu.roll(uv, d, 0), 0.0)
                av = av * jnp.where(m, pltpu.roll(av, d, 0), 1.0)
            hv = uv + av * c
            h_ref[pl.ds(s, 8), :] = hv
            return hv[7:8, :]

        carry[0:1, :] = lax.fori_loop(0, rows // 8, blk, carry[0:1, :])
        yb_ref[...] = _lru_out(h_ref[...], g_ref[...])

    return pl.pallas_call(
        body, name="lru_fwd", grid=(t_n // rows,),
        in_specs=_proj_specs(rows, (1, 2), fwd) + _prev8_specs(rows, (2,), fwd) + [
            _full((4, D)), _full((1, D)), _full((4, LRU_BLK, LRU_BLK)), _full((4, LRU_BLK, LRU_BLK)),
            _full((1, D)), _full((1, D)), _full((1, D))],
        out_specs=[pl.BlockSpec((rows, D), lambda i: (i, 0)), pl.BlockSpec((rows, D), lambda i: (i, 0))],
        out_shape=[jax.ShapeDtypeStruct((t_n, D), F32), jax.ShapeDtypeStruct((t_n, D), F32)],
        scratch_shapes=[pltpu.VMEM((rows, D), F32), pltpu.VMEM((rows, D), F32), pltpu.VMEM((8, D), F32)],
        compiler_params=_cp("arbitrary"),
    )(proj, proj, proj, cw, cb, wa, wx, ba, bx, lam)


def _lru_bwd(dyb, proj, h, cw, cb, wa, wx, ba, bx, lam):
    t_n = proj.shape[0]
    rows = min(LRU_ROWS, t_n)
    n_t = t_n // rows
    rev = lambda i: n_t - 1 - i
    rb = rows // 8

    def body(dyb_ref, g_ref, x_ref, xp_ref, h_ref, hp_ref, cw_ref, cb_ref, wa_ref, wx_ref, ba_ref, bx_ref, lam_ref,
             dg_ref, dx_ref, dcw_ref, dcb_ref, dwa_ref, dwx_ref, dba_ref, dbx_ref, dlam_ref,
             a_s, dh_s, hx_s, da_s, du_s, carry, fut):
        i = pl.program_id(0)
        acc_refs = (dcw_ref, dcb_ref, dwa_ref, dwx_ref, dba_ref, dbx_ref, dlam_ref)

        @pl.when(i == 0)
        def _():
            for r in (carry, fut) + acc_refs:
                r[...] = jnp.zeros_like(r)

        keep = jnp.where(i == n_t - 1, 0.0, 1.0)
        taps = _taps_past(x_ref[...], xp_ref[...] * keep)
        xr = _conv_apply(taps, cw_ref[...], cb_ref[...])
        gate_in = (xr, wa_ref[...], wx_ref[...], ba_ref[...], bx_ref[...], lam_ref[...])
        (a, _), gates_vjp = jax.vjp(_lru_gates, *gate_in)
        _, out_vjp = jax.vjp(_lru_out, h_ref[...], g_ref[...])
        dh, dg = out_vjp(dyb_ref[...])
        dg_ref[...] = dg.astype(dg_ref.dtype)
        a_s[...] = a
        dh_s[...] = dh
        hx_s[0:8, :] = hp_ref[...] * keep
        hx_s[8:, :] = h_ref[...]
        row = lax.broadcasted_iota(jnp.int32, (8, D), 0)

        def blk(b, c):
            s = pl.multiple_of((rb - 1 - b) * 8, 8)
            av = a_s[pl.ds(s, 8), :]
            dhv = dh_s[pl.ds(s, 8), :]
            a0 = av
            kv = av * dhv
            for d in (1, 2, 4):
                m = row <= 7 - d
                kv = kv + av * jnp.where(m, pltpu.roll(kv, 8 - d, 0), 0.0)
                av = av * jnp.where(m, pltpu.roll(av, 8 - d, 0), 1.0)
            kv = kv + av * c
            gv = dhv + jnp.where(row < 7, pltpu.roll(kv, 7, 0), c)
            hb = hx_s[pl.ds(s + 8, 8), :]
            hpv = hx_s[pl.ds(s, 8), :]
            hprev = jnp.where(row >= 1, pltpu.roll(hb, 1, 0), hpv[7:8, :])
            du_s[pl.ds(s, 8), :] = gv
            da_s[pl.ds(s, 8), :] = gv * hprev
            del a0
            return kv[0:1, :]

        carry[0:1, :] = lax.fori_loop(0, rb, blk, carry[0:1, :])
        dxr, dwa, dwx, dba, dbx, dlam = gates_vjp((da_s[...], du_s[...]))
        dwa_ref[...] += dwa
        dwx_ref[...] += dwx
        dba_ref[...] += dba
        dbx_ref[...] += dbx
        dlam_ref[...] += dlam
        dcb_ref[...] += _rowsum(dxr)
        for k in range(4):
            dcw_ref[k:k + 1, :] += _rowsum(dxr * taps[k])
        ft = _taps_future(dxr, fut[...])
        cwv = cw_ref[...]
        dx = ft[0] * cwv[0:1, :]
        for k in range(1, 4):
            dx = dx + ft[k] * cwv[k:k + 1, :]
        fut[...] = dxr[0:8, :]
        dx_ref[...] = dx.astype(dx_ref.dtype)

    row_in = lambda: pl.BlockSpec((rows, D), lambda i: (rev(i), 0))
    prev_h = pl.BlockSpec((8, D), lambda i: (jnp.maximum(rev(i) * rb - 1, 0), 0))
    wspec = lambda: _full((4, LRU_BLK, LRU_BLK))
    return pl.pallas_call(
        body, name="lru_bwd", grid=(n_t,),
        in_specs=[row_in()] + _proj_specs(rows, (1, 2), rev) + _prev8_specs(rows, (2,), rev) + [row_in(), prev_h] + [
            _full((4, D)), _full((1, D)), wspec(), wspec(), _full((1, D)), _full((1, D)), _full((1, D))],
        out_specs=[row_in(), row_in(), _full((4, D)), _full((1, D)), wspec(), wspec(), _full((1, D)), _full((1, D)),
                   _full((1, D))],
        out_shape=[jax.ShapeDtypeStruct((t_n, D), MXU)] * 2 + [
            jax.ShapeDtypeStruct(s, F32) for s in ((4, D), (1, D), (4, LRU_BLK, LRU_BLK), (4, LRU_BLK, LRU_BLK),
                                                   (1, D), (1, D), (1, D))],
        scratch_shapes=[pltpu.VMEM((rows, D), F32), pltpu.VMEM((rows, D), F32), pltpu.VMEM((rows + 8, D), F32),
                        pltpu.VMEM((rows, D), F32), pltpu.VMEM((rows, D), F32), pltpu.VMEM((8, D), F32),
                        pltpu.VMEM((8, D), F32)],
        compiler_params=_cp("arbitrary"),
    )(dyb, proj, proj, proj, h, h, cw, cb, wa, wx, ba, bx, lam)


def _merge_out(ya, yb, proj, x, wout, g1):
    t_n = x.shape[0]
    tm = min(t_n, 512)

    def body(ya_ref, yb_ref, ga_ref, gb_ref, x_ref, w_ref, g_ref, h1_ref, mix_ref, mg_ref):
        merged = _sigmoid(ga_ref[...]) * ya_ref[...] + _sigmoid(gb_ref[...]) * yb_ref[...]
        mg = merged.astype(MXU)
        mg_ref[...] = mg
        mix = jnp.dot(mg, w_ref[...], preferred_element_type=F32)
        mix_ref[...] = mix
        h1_ref[...] = x_ref[...] + _rms(mix, g_ref[...])

    row = lambda: pl.BlockSpec((tm, D), lambda i: (i, 0))
    return pl.pallas_call(
        body, name="merge_out", grid=(t_n // tm,),
        in_specs=[row(), row()] + _proj_specs(tm, (3, 4), lambda i: i) + [row(), _full((D, D)), _full((1, D))],
        out_specs=[row(), row(), row()],
        out_shape=[jax.ShapeDtypeStruct((t_n, D), F32), jax.ShapeDtypeStruct((t_n, D), F32),
                   jax.ShapeDtypeStruct((t_n, D), MXU)],
        compiler_params=_cp("parallel"),
    )(ya, yb, proj, proj, x, wout, g1)


def _out_bwd(dh1, mix, ya, yb, proj, wout, g1):
    t_n = dh1.shape[0]
    tm = min(t_n, 512)

    def body(dh1_ref, mix_ref, ya_ref, yb_ref, ga_ref, gb_ref, w_ref, g_ref,
             dmix_ref, dya_ref, dyb_ref, dga_ref, dgb_ref, dg1_ref):
        @pl.when(pl.program_id(0) == 0)
        def _():
            dg1_ref[...] = jnp.zeros_like(dg1_ref)

        dmix, dg_rows = _rms_bwd(mix_ref[...], g_ref[...], dh1_ref[...])
        dg1_ref[...] += _rowsum(dg_rows)
        dmix_b = dmix.astype(MXU)
        dmix_ref[...] = dmix_b
        dmg = lax.dot_general(dmix_b, w_ref[...], (((1,), (1,)), ((), ())), preferred_element_type=F32)
        sa = _sigmoid(ga_ref[...])
        sb = _sigmoid(gb_ref[...])
        dya_ref[...] = dmg * sa
        dyb_ref[...] = dmg * sb
        dga_ref[...] = (dmg * ya_ref[...] * sa * (1.0 - sa)).astype(MXU)
        dgb_ref[...] = (dmg * yb_ref[...] * sb * (1.0 - sb)).astype(MXU)

    row = lambda: pl.BlockSpec((tm, D), lambda i: (i, 0))
    return pl.pallas_call(
        body, name="out_bwd", grid=(t_n // tm,),
        in_specs=[row(), row(), row(), row()] + _proj_specs(tm, (3, 4), lambda i: i) + [_full((D, D)), _full((1, D))],
        out_specs=[row(), row(), row(), row(), row(), _full((1, D))],
        out_shape=[jax.ShapeDtypeStruct((t_n, D), MXU), jax.ShapeDtypeStruct((t_n, D), F32),
                   jax.ShapeDtypeStruct((t_n, D), F32), jax.ShapeDtypeStruct((t_n, D), MXU),
                   jax.ShapeDtypeStruct((t_n, D), MXU), jax.ShapeDtypeStruct((1, D), F32)],
        compiler_params=_cp("arbitrary"),
    )(dh1, mix, ya, yb, proj, proj, wout, g1)


MLP_TM = 1024
MLP_TF_FWD = 512
MLP_TF_BWD = 1024


def _mlp_fwd(h1, g2, wup, wdown, g3, tgt):
    t_n = h1.shape[0]
    tm = min(t_n, MLP_TM)
    n_f = FF // MLP_TF_FWD

    def body(h1_ref, g2_ref, wu_ref, wd_ref, g3_ref, tgt_ref, hp_ref, v_ref, dout_ref, dff_ref, loss_ref, dg3_ref, acc):
        i, j = pl.program_id(0), pl.program_id(1)

        @pl.when((i == 0) & (j == 0))
        def _():
            loss_ref[...] = jnp.zeros_like(loss_ref)
            dg3_ref[...] = jnp.zeros_like(dg3_ref)

        @pl.when(j == 0)
        def _():
            v_ref[...] = _rms(h1_ref[...], g2_ref[...]).astype(MXU)
            acc[...] = jnp.zeros_like(acc)

        hp = lax.dot_general(v_ref[...], wu_ref[...], (((1,), (1,)), ((), ())), preferred_element_type=F32)
        hp_ref[...] = hp
        hid = jnp.square(jnp.maximum(hp, 0.0))
        acc[...] += jnp.dot(hid.astype(MXU), wd_ref[...], preferred_element_type=F32)

        @pl.when(j == n_f - 1)
        def _():
            ff = acc[...]
            err = h1_ref[...] + _rms(ff, g3_ref[...]) - tgt_ref[...]
            loss_ref[...] += 0.5 * jnp.sum(jnp.mean(err * err, axis=-1, keepdims=True), axis=0, keepdims=True)
            dout = err * (1.0 / D)
            dout_ref[...] = dout
            dff, dg_rows = _rms_bwd(ff, g3_ref[...], dout)
            dg3_ref[...] += _rowsum(dg_rows)
            dff_ref[...] = dff.astype(MXU)

    row = lambda: pl.BlockSpec((tm, D), lambda i, j: (i, 0))
    vec = lambda: pl.BlockSpec((1, D), lambda i, j: (0, 0))
    return pl.pallas_call(
        body, name="mlp_fwd", grid=(t_n // tm, n_f),
        in_specs=[row(), vec(), pl.BlockSpec((MLP_TF_FWD, D), lambda i, j: (j, 0)),
                  pl.BlockSpec((MLP_TF_FWD, D), lambda i, j: (j, 0)), vec(), row()],
        out_specs=[pl.BlockSpec((tm, MLP_TF_FWD), lambda i, j: (i, j)), row(), row(), row(),
                   pl.BlockSpec((1, 1), lambda i, j: (0, 0)), vec()],
        out_shape=[jax.ShapeDtypeStruct((t_n, FF), F32), jax.ShapeDtypeStruct((t_n, D), MXU),
                   jax.ShapeDtypeStruct((t_n, D), F32), jax.ShapeDtypeStruct((t_n, D), MXU),
                   jax.ShapeDtypeStruct((1, 1), F32), jax.ShapeDtypeStruct((1, D), F32)],
        scratch_shapes=[pltpu.VMEM((tm, D), F32)],
        compiler_params=_cp("arbitrary", "arbitrary"),
    )(h1, g2, wup, wdown, g3, tgt)


def _mlp_bwd(dff, dout, hp, h1, wup, wdown, g2):
    t_n = h1.shape[0]
    tm = min(t_n, MLP_TM)
    n_f = FF // MLP_TF_BWD

    def mm_body(dff_ref, hp_ref, wu_ref, wd_ref, dv_ref, dhp_ref, hid_ref):
        @pl.when(pl.program_id(1) == 0)
        def _():
            dv_ref[...] = jnp.zeros_like(dv_ref)

        relu = jnp.maximum(hp_ref[...], 0.0)
        hid_ref[...] = jnp.square(relu).astype(MXU)
        dhid = lax.dot_general(dff_ref[...], wd_ref[...], (((1,), (1,)), ((), ())), preferred_element_type=F32)
        dhp = (dhid * (2.0 * relu)).astype(MXU)
        dhp_ref[...] = dhp
        dv_ref[...] += jnp.dot(dhp, wu_ref[...], preferred_element_type=F32)

    row = lambda: pl.BlockSpec((tm, D), lambda i, j: (i, 0))
    blk = lambda: pl.BlockSpec((tm, MLP_TF_BWD), lambda i, j: (i, j))
    wblk = lambda: pl.BlockSpec((MLP_TF_BWD, D), lambda i, j: (j, 0))
    dv, dhp, hid = pl.pallas_call(
        mm_body, name="mlp_bwd", grid=(t_n // tm, n_f),
        in_specs=[row(), blk(), wblk(), wblk()], out_specs=[row(), blk(), blk()],
        out_shape=[jax.ShapeDtypeStruct((t_n, D), F32), jax.ShapeDtypeStruct((t_n, FF), MXU),
                   jax.ShapeDtypeStruct((t_n, FF), MXU)],
        compiler_params=_cp("parallel", "arbitrary"),
    )(dff, hp, wup, wdown)

    tf = min(t_n, 512)

    def fin_body(dv_ref, h1_ref, g_ref, dout_ref, dh1_ref, dg2_ref):
        @pl.when(pl.program_id(0) == 0)
        def _():
            dg2_ref[...] = jnp.zeros_like(dg2_ref)

        dx, dg_rows = _rms_bwd(h1_ref[...], g_ref[...], dv_ref[...])
        dg2_ref[...] += _rowsum(dg_rows)
        dh1_ref[...] = dout_ref[...] + dx

    frow = lambda: pl.BlockSpec((tf, D), lambda i: (i, 0))
    dh1, dg2 = pl.pallas_call(
        fin_body, name="mlp_bwd_dh1", grid=(t_n // tf,), in_specs=[frow(), frow(), _full((1, D)), frow()],
        out_specs=[frow(), _full((1, D))],
        out_shape=[jax.ShapeDtypeStruct((t_n, D), F32), jax.ShapeDtypeStruct((1, D), F32)],
        compiler_params=_cp("arbitrary"),
    )(dv, h1, g2, dout)
    return dh1, dhp, hid, dg2


def _wgrad(a, g, name):
    t_n, k_n = a.shape
    n_n = g.shape[1]
    tt = min(t_n, 1024)
    tk, tn = min(k_n, 1024), min(n_n, 1024)

    n_t = t_n // tt

    def body(a_ref, g_ref, o_ref, acc):
        t = pl.program_id(2)

        @pl.when(t == 0)
        def _():
            acc[...] = jnp.zeros_like(acc)

        acc[...] += lax.dot_general(a_ref[...], g_ref[...], (((0,), (0,)), ((), ())), preferred_element_type=F32)

        @pl.when(t == n_t - 1)
        def _():
            o_ref[...] = acc[...].astype(o_ref.dtype)

    return pl.pallas_call(
        body, name=name, grid=(k_n // tk, n_n // tn, n_t),
        in_specs=[pl.BlockSpec((tt, tk), lambda k, n, t: (t, k)), pl.BlockSpec((tt, tn), lambda k, n, t: (t, n))],
        out_specs=pl.BlockSpec((tk, tn), lambda k, n, t: (k, n)),
        out_shape=jax.ShapeDtypeStruct((k_n, n_n), MXU),
        scratch_shapes=[pltpu.VMEM((tk, tn), F32)],
        compiler_params=_cp("parallel", "parallel", "arbitrary"),
    )(a, g)


def _wgrad_segs(segs, g, name):
    t_n, n_n = g.shape
    n_s = len(segs)
    tt = min(t_n, 1024)
    n_t = t_n // tt

    def body(*refs):
        a_refs = refs[:n_s]
        g_ref, o_ref, acc = refs[n_s:]
        s_id, t = pl.program_id(0), pl.program_id(1)

        @pl.when(t == 0)
        def _():
            acc[...] = jnp.zeros_like(acc)

        for s in range(n_s):
            @pl.when(s_id == s)
            def _(s=s):
                acc[...] += lax.dot_general(a_refs[s][...], g_ref[...], (((0,), (0,)), ((), ())),
                                            preferred_element_type=F32)

        @pl.when(t == n_t - 1)
        def _():
            o_ref[...] = acc[...].astype(o_ref.dtype)

    seg_spec = lambda s: pl.BlockSpec((tt, SEG), lambda i, t: (jnp.where(i == s, t, jnp.where(i < s, 0, n_t - 1)), 0))
    return pl.pallas_call(
        body, name=name, grid=(n_s, n_t),
        in_specs=[seg_spec(s) for s in range(n_s)] + [pl.BlockSpec((tt, n_n), lambda i, t: (t, 0))],
        out_specs=pl.BlockSpec((SEG, n_n), lambda i, t: (i, 0)),
        out_shape=jax.ShapeDtypeStruct((n_s * SEG, n_n), MXU),
        scratch_shapes=[pltpu.VMEM((SEG, n_n), F32)],
        compiler_params=_cp("arbitrary", "arbitrary"),
    )(*segs, g)


def _inproj_bwd(dsegs, w, x, g0, dh1):
    t_n = x.shape[0]
    tm = min(t_n, 1024)
    n_k = NP // SEG

    def mm_body(*refs):
        dp_refs = refs[:n_k]
        w_ref, du_ref = refs[n_k:]
        k = pl.program_id(1)

        @pl.when(k == 0)
        def _():
            du_ref[...] = jnp.zeros_like(du_ref)

        for s in range(n_k):
            @pl.when(k == s)
            def _(s=s):
                du_ref[...] += jnp.dot(dp_refs[s][...], w_ref[...], preferred_element_type=F32)

    du = pl.pallas_call(
        mm_body, name="inproj_bwd", grid=(t_n // tm, n_k),
        in_specs=[pl.BlockSpec((tm, SEG), functools.partial(
            lambda i, k, s: (jnp.where(k >= s, i, jnp.maximum(i - 1, 0)), 0), s=s)) for s in range(n_k)] + [
            pl.BlockSpec((SEG, D), lambda i, k: (k, 0))],
        out_specs=pl.BlockSpec((tm, D), lambda i, k: (i, 0)),
        out_shape=jax.ShapeDtypeStruct((t_n, D), F32),
        compiler_params=_cp("parallel", "arbitrary"),
    )(*dsegs, w)

    tf = min(t_n, 512)

    def fin_body(du_ref, x_ref, g_ref, dh1_ref, dx_ref, dg0_ref):
        @pl.when(pl.program_id(0) == 0)
        def _():
            dg0_ref[...] = jnp.zeros_like(dg0_ref)

        dx, dg_rows = _rms_bwd(x_ref[...], g_ref[...], du_ref[...])
        dg0_ref[...] += _rowsum(dg_rows)
        dx_ref[...] = dh1_ref[...] + dx

    row = lambda: pl.BlockSpec((tf, D), lambda i: (i, 0))
    return pl.pallas_call(
        fin_body, name="grad_x", grid=(t_n // tf,), in_specs=[row(), row(), _full((1, D)), row()],
        out_specs=[row(), _full((1, D))],
        out_shape=[jax.ShapeDtypeStruct((t_n, D), F32), jax.ShapeDtypeStruct((1, D), F32)],
        compiler_params=_cp("arbitrary"),
    )(du, x, g0, dh1)


def _blockdiag4(w):
    w4 = w.reshape(4, 4, 64, 1, 64).astype(MXU)
    same = (jnp.arange(4)[:, None, None, None] == jnp.arange(4)[None, None, :, None])
    return jnp.where(same[None], w4, jnp.zeros((), MXU)).reshape(4, 256, 256)


def _blockdiag4_extract(g):
    g5 = g.reshape(4, 4, 64, 4, 64)
    return jnp.stack([g5[:, q, :, q, :] for q in range(4)], axis=1).reshape(NH, 64, 64)


def _local_step(x, tgt, p, after_ssd=None, late_weights=None, send_mlp_grads=None, send_late_grads=None):
    f = lambda a: a.astype(F32)
    proj, u = _inproj(x, p["norm_mix_pre"], p["w_in_pT"])
    ssm_params = (p["cw_xs"], p["cw_bc"], p["cb_xs"], p["cb_bc"], p["dt_bias"], p["a_log"], p["d_skip_x"],
                  p["ssm_norm"])
    ya, sprev, yraw = _ssd_fwd(proj, *ssm_params)
    cb_lru = p["conv_lru_b"] if after_ssd is None else p["conv_lru_b"] + after_ssd(ya)
    lru_params = (p["conv_lru_w"], cb_lru, p["wa_bd"], p["wx_bd"], p["lru_ba"], p["lru_bx"], p["lru_lambda"])
    yb, h = _lru_fwd(proj, *lru_params)
    if late_weights is not None:
        p = dict(p, **late_weights(yb))
    h1, mix, merged = _merge_out(ya, yb, proj, x, p["w_out"], p["norm_mix_post"])
    hp, v, dout, dff, loss, dg3 = _mlp_fwd(h1, p["norm_mlp_pre"], p["w_upT"], p["w_down"], p["norm_mlp_post"], tgt)

    dh1, dhp, hid, dg2 = _mlp_bwd(dff, dout, hp, h1, p["w_upT"], p["w_down"], p["norm_mlp_pre"])
    dmix, dya, dyb, dga, dgb, dg1 = _out_bwd(dh1, mix, ya, yb, proj, p["w_out"], p["norm_mix_post"])
    d_w_out = _wgrad(merged, dmix, "wgrad_out")
    (dz, dxs, dbc, dcwx, dcwb, dcbx, dcbb, ddtb, dalog, ddsk, dnrm, d_w_down, d_w_up_t) = _ssd_bwd(
        dya, yraw, proj, sprev, *ssm_params, (hid, dff, dhp, v))
    if send_mlp_grads is not None:
        lru_params = (lru_params[0], lru_params[1] + send_mlp_grads(d_w_up_t, d_w_down, d_w_out)) + lru_params[2:]
    (dgl, dxl, dcwl, dcbl, dwa, dwx, dba, dbx, dlam) = _lru_bwd(dyb, proj, h, *lru_params)
    dsegs = [dz, dgl, dxl, dga, dgb, dxs, dbc]
    d_w_in_pt = _wgrad_segs(dsegs, u, "wgrad_in")
    g0 = p["norm_mix_pre"]
    if send_late_grads is not None:
        g0 = g0 + send_late_grads(d_w_in_pt)
    grad_x, dg0 = _inproj_bwd(dsegs, p["w_in_pT"], x, g0, dh1)
    grads = dict(
        norm_mix_pre=dg0, w_in_pT=d_w_in_pt, conv_ssm_w=jnp.concatenate([dcwx, dcwb], axis=1),
        conv_ssm_b=jnp.concatenate([dcbx, dcbb], axis=1), dt_bias=ddtb[:, :NH], a_log=dalog[:, :NH],
        d_skip=f(ddsk).reshape(NH, 64).sum(axis=1)[None, :], ssm_norm=dnrm, conv_lru_w=dcwl, conv_lru_b=dcbl,
        lru_wa=_blockdiag4_extract(dwa), lru_ba=dba, lru_wx=_blockdiag4_extract(dwx), lru_bx=dbx, lru_lambda=dlam,
        w_out=d_w_out, norm_mix_post=dg1, norm_mlp_pre=dg2, w_upT=d_w_up_t, w_down=d_w_down, norm_mlp_post=dg3)
    return loss[0, 0], grad_x, grads


W_IN_COLS = 6672


def _w_in_t_to_padded(wt):
    z, xs, bc, dt = wt[0:1024], wt[1024:2048], wt[2048:2560], wt[2560:2576]
    gl, xl, ga, gb = wt[2576:3600], wt[3600:4624], wt[4624:5648], wt[5648:6672]
    return jnp.concatenate([z, gl, xl, ga, gb, xs, bc, dt, jnp.zeros((NP - 6672, wt.shape[1]), wt.dtype)], axis=0)


def _w_in_t_from_padded(wp):
    z, gl, xl, ga, gb = (wp[SEG * s:SEG * (s + 1)] for s in range(5))
    xs, bc, dt = wp[5120:6144], wp[6144:6656], wp[6656:6672]
    return jnp.concatenate([z, xs, bc, dt, gl, xl, ga, gb], axis=0)


def _prep_params(full, big):
    f = lambda a: a.astype(F32)
    pad128 = lambda a: jnp.pad(f(a).reshape(1, -1), ((0, 0), (0, 128 - a.size)))
    cw = f(full["conv_ssm_w"])
    cb = f(full["conv_ssm_b"]).reshape(1, -1)
    return dict(
        big, norm_mix_pre=f(full["norm_mix_pre"]).reshape(1, D),
        cw_xs=cw[:, :D], cw_bc=cw[:, D:], cb_xs=cb[:, :D], cb_bc=cb[:, D:],
        dt_bias=pad128(full["dt_bias"]), a_log=pad128(full["a_log"]),
        d_skip_x=jnp.repeat(f(full["d_skip"]).reshape(-1), 64).reshape(1, D), ssm_norm=f(full["ssm_norm"]).reshape(1, D),
        conv_lru_w=f(full["conv_lru_w"]), conv_lru_b=f(full["conv_lru_b"]).reshape(1, D),
        wa_bd=_blockdiag4(full["lru_wa"]), wx_bd=_blockdiag4(full["lru_wx"]),
        lru_ba=f(full["lru_ba"]).reshape(1, D), lru_bx=f(full["lru_bx"]).reshape(1, D),
        lru_lambda=f(full["lru_lambda"]).reshape(1, D),
        norm_mix_post=f(full["norm_mix_post"]).reshape(1, D), norm_mlp_pre=f(full["norm_mlp_pre"]).reshape(1, D),
        norm_mlp_post=f(full["norm_mlp_post"]).reshape(1, D))


MESH_ID = pl.DeviceIdType.MESH
ANY = pl.BlockSpec(memory_space=pl.ANY)


def _my_place():
    x, y, c = lax.axis_index("x"), lax.axis_index("y"), lax.axis_index("c")
    return x, y, c, 4 * x + 2 * y + c


def _peer(x, y, c, k):
    return (x ^ ((k >> 2) & 1), y ^ ((k >> 1) & 1), c ^ (k & 1))


def _all_gather(pack, name):
    def body(in_ref, out_ref, send_sems, recv_sems, local_sem):
        x, y, c, me = _my_place()
        sibling = (x, y, 1 - c)
        flips = (4, 2, 6)

        def copy(j, block, to, src=None):
            return pltpu.make_async_remote_copy(
                src_ref=out_ref.at[block] if src is None else src, dst_ref=out_ref.at[block],
                send_sem=send_sems.at[j], recv_sem=recv_sems.at[j], device_id=to, device_id_type=MESH_ID)

        mine = pltpu.make_async_copy(in_ref, out_ref.at[me], local_sem)
        mine.start()
        first = [copy(0, me, sibling, src=in_ref)]
        first += [copy(1 + j, me, _peer(x, y, c, k), src=in_ref) for j, k in enumerate(flips)]
        for cp in first:
            cp.start()
        passed = [copy(4 + j, me ^ k, sibling) for j, k in enumerate(flips)]
        for j, k in enumerate(flips):
            copy(1 + j, me ^ k, (x, y, c)).wait_recv()
            passed[j].start()
        copy(0, me ^ 1, (x, y, c)).wait_recv()
        for j, k in enumerate(flips):
            copy(4 + j, me ^ k ^ 1, (x, y, c)).wait_recv()
        for cp in first + passed:
            cp.wait_send()
        mine.wait()

    return pl.pallas_call(
        body, name=name, in_specs=[ANY], out_specs=ANY,
        out_shape=jax.ShapeDtypeStruct((N_DEV,) + pack.shape, pack.dtype),
        scratch_shapes=[pltpu.SemaphoreType.DMA((N_DEV - 1,)), pltpu.SemaphoreType.DMA((N_DEV - 1,)),
                        pltpu.SemaphoreType.DMA],
    )(pack)


HBM = pl.BlockSpec(memory_space=pltpu.HBM)
SEM = pl.BlockSpec(memory_space=pltpu.SEMAPHORE)
PLAN_GATHER = tuple((k, "pack", 0) for k in range(1, N_DEV))
PLAN_SCATTER = tuple((k, "slot", 0) for k in range(1, N_DEV))
PLAN_GATHER_ICI = tuple((k, "pack", 0) for k in (2, 4, 6))
PLAN_GATHER_D2D = ((1, "pack", 0),) + tuple((1, s, s) for s in (2, 4, 6))


def _plan_copy(j, plan, src_ref, land_ref, sems):
    k, source, slot = plan[j]
    x, y, c, me = _my_place()
    if source == "pack":
        src = src_ref
    elif source == "slot":
        src = src_ref.at[me ^ k]
    else:
        src = land_ref.at[me ^ source]
    return pltpu.make_async_remote_copy(
        src_ref=src, dst_ref=land_ref.at[me ^ slot], send_sem=sems[j], recv_sem=sems[len(plan) + j],
        device_id=_peer(x, y, c, k), device_id_type=MESH_ID)


def _exchange_start(src, land, plan, name):
    n_c = len(plan)
    if land is None:
        land = pltpu.with_memory_space_constraint(lax.empty((N_DEV,) + src.shape[-2:], src.dtype), pltpu.HBM)

    def body(src_ref, land_ref, *rest):
        sems, token = rest[:2 * n_c], rest[2 * n_c + 2]
        for j in range(n_c):
            _plan_copy(j, plan, src_ref, land_ref, sems).start()
        token[...] = jnp.zeros_like(token)

    outs = pl.pallas_call(
        body, name=name,
        out_shape=(pltpu.SemaphoreType.DMA(()),) * (2 * n_c) + (
            pltpu.HBM(src.shape, src.dtype), pltpu.HBM(land.shape, land.dtype), jax.ShapeDtypeStruct((8, 128), F32)),
        in_specs=(HBM, HBM), out_specs=(SEM,) * (2 * n_c) + (HBM, HBM, pl.BlockSpec(memory_space=pltpu.VMEM)),
        input_output_aliases={0: 2 * n_c, 1: 2 * n_c + 1},
        compiler_params=pltpu.CompilerParams(has_side_effects=pltpu.SideEffectType.DATAFLOW_SIDE_EFFECTING),
    )(pltpu.with_memory_space_constraint(src, pltpu.HBM), land)
    return outs[:2 * n_c], outs[2 * n_c], outs[2 * n_c + 1], outs[2 * n_c + 2]


def _exchange_wait(sems, src_thru, land_thru, after, plan, name):
    n_c = len(plan)

    def body(src_ref, land_ref, *rest):
        for j in range(n_c):
            cp = _plan_copy(j, plan, src_ref, land_ref, rest[:2 * n_c])
            cp.wait_send()
            cp.wait_recv()

    return pl.pallas_call(
        body, name=name,
        out_shape=(pltpu.HBM(src_thru.shape, src_thru.dtype), pltpu.HBM(land_thru.shape, land_thru.dtype)),
        in_specs=(HBM, HBM) + (SEM,) * (2 * n_c) + (pl.BlockSpec(memory_space=pl.ANY),), out_specs=(HBM, HBM),
        input_output_aliases={0: 0, 1: 1},
        compiler_params=pltpu.CompilerParams(has_side_effects=pltpu.SideEffectType.DATAFLOW_SIDE_EFFECTING),
    )(src_thru, land_thru, *sems, after)


def _slot_sum(parts, name):
    r_n, c_n = parts.shape[1:]
    tr = max(t for t in range(16, 513, 16) if r_n % t == 0)

    def body(p_ref, o_ref):
        acc = p_ref[0].astype(F32)
        for k in range(1, N_DEV):
            acc = acc + p_ref[k].astype(F32)
        o_ref[...] = acc

    return pl.pallas_call(
        body, name=name, grid=(r_n // tr,),
        in_specs=[pl.BlockSpec((N_DEV, tr, c_n), lambda i: (0, i, 0))],
        out_specs=pl.BlockSpec((tr, c_n), lambda i: (i, 0)),
        out_shape=jax.ShapeDtypeStruct((r_n, c_n), F32),
        compiler_params=_cp("parallel"),
    )(parts)


def _adam_math(w, g, m, v):
    m = ADAM_B1 * m + (1.0 - ADAM_B1) * g
    v = ADAM_B2 * v + (1.0 - ADAM_B2) * jnp.square(g)
    m_hat = m / (1.0 - ADAM_B1 ** ADAM_STEP)
    v_hat = v / (1.0 - ADAM_B2 ** ADAM_STEP)
    return -ADAM_LR * (m_hat / (jnp.sqrt(v_hat) + ADAM_EPS) + ADAM_WD * w), m, v


def _adam_big(w, g, m, v, name):
    def body(w_ref, g_ref, m_ref, v_ref, d_ref, mo_ref, vo_ref):
        d_ref[...], mo_ref[...], vo_ref[...] = _adam_math(w_ref[...], g_ref[...], m_ref[...], v_ref[...])

    if w.ndim == 3:
        _, r_n, c_n = w.shape
        tr = min(r_n, 256)
        grid = (r_n // tr,)
        blk = lambda: pl.BlockSpec((1, tr, c_n), lambda i: (0, i, 0))
    else:
        r_n, c_n = w.shape
        tc = min(c_n, 256)
        grid = (c_n // tc,)
        blk = lambda: pl.BlockSpec((r_n, tc), lambda i: (0, i))
    return pl.pallas_call(
        body, name=name, grid=grid, in_specs=[blk(), blk(), blk(), blk()], out_specs=[blk(), blk(), blk()],
        out_shape=[jax.ShapeDtypeStruct(w.shape, F32)] * 3, compiler_params=_cp("parallel"),
    )(w, g, m, v)


def _adam_small(groups, where, wmv):
    n, n_g = len(wmv), len(groups)

    def body(*refs):
        g_refs = refs[:n_g]
        w_refs = refs[n_g:n_g + 3 * n]
        o_refs = refs[n_g + 3 * n:]
        for q in range(n):
            w_ref, m_ref, v_ref = w_refs[3 * q:3 * q + 3]
            r, c = w_ref.shape
            gi, r0 = where[q]
            g = g_refs[gi][0, r0:r0 + r, 0:c]
            for k in range(1, N_DEV):
                g = g + g_refs[gi][k, r0:r0 + r, 0:c]
            d, m, v = _adam_math(w_ref[...], g, m_ref[...], v_ref[...])
            o_refs[4 * q][...] = g
            o_refs[4 * q + 1][...] = d
            o_refs[4 * q + 2][...] = m
            o_refs[4 * q + 3][...] = v

    flat_wmv = [a for t in wmv for a in t]
    vm = pl.BlockSpec(memory_space=pltpu.VMEM)
    outs = pl.pallas_call(
        body, name="adam_small", in_specs=[vm] * (n_g + 3 * n), out_specs=[vm] * (4 * n),
        out_shape=[jax.ShapeDtypeStruct(t[0].shape, F32) for t in wmv for _ in range(4)],
        compiler_params=pltpu.CompilerParams(vmem_limit_bytes=VMEM_LIMIT),
    )(*groups, *flat_wmv)
    return [tuple(outs[4 * q:4 * q + 4]) for q in range(n)]


WEIGHTS = ["norm_mix_pre", "w_in", "conv_ssm_w", "conv_ssm_b", "dt_bias", "a_log", "d_skip", "ssm_norm", "conv_lru_w",
           "conv_lru_b", "lru_wa", "lru_ba", "lru_wx", "lru_bx", "lru_lambda", "w_out", "norm_mix_post", "norm_mlp_pre",
           "w_up", "w_down", "norm_mlp_post"]
BIG = ["w_out", "w_up", "w_down", "w_in"]
IN_ROWS = W_IN_COLS // N_DEV
IN_PAD, EARLY_ROWS = 848, 880
ROW_UP, ROW_DOWN, LATE_ROWS = 128, 640, 1152
GRAD_LATE_ROWS = 864
CONV_SSM_COLS, CONV_LRU_COLS = 1536 // N_DEV, D // N_DEV
SMALL = [("norm_mix_pre", (1, D), 0, 0), ("ssm_norm", (1, D), 0, 1), ("conv_lru_b", (1, D), 0, 2),
         ("lru_lambda", (1, D), 0, 3), ("norm_mix_post", (1, D), 0, 4), ("norm_mlp_pre", (1, D), 0, 5),
         ("norm_mlp_post", (1, D), 0, 6), ("conv_ssm_b", (1, 1536), 1, 0), ("dt_bias", (1, NH), 2, 0),
         ("a_log", (1, NH), 2, 1), ("d_skip", (1, NH), 2, 2), ("conv_ssm_w", (4, CONV_SSM_COLS), 3, 0),
         ("conv_lru_w", (4, CONV_LRU_COLS), 4, 0), ("lru_wa", (D, 64), 5, 0), ("lru_wx", (D, 64), 5, D),
         ("lru_ba", (NH, 64), 6, 0), ("lru_bx", (NH, 64), 6, NH)]
SMALL_GROUPS = [(8, D), (1, 1536), (8, 128), (4, 1536), (4, D), (2 * D, 64), (2 * NH, 64)]


def _pad_rows(flat, mult):
    n = flat.shape[0]
    rows = -(-n // (128 * mult)) * mult
    return jnp.pad(flat, (0, rows * 128 - n)).reshape(rows, 128)


def _split3(a):
    hi = a.astype(MXU)
    r1 = a - hi.astype(F32)
    mid = r1.astype(MXU)
    lo = (r1 - mid.astype(F32)).astype(MXU)
    return jnp.stack([hi, mid, lo])


def _early_pack(a):
    bf = lambda t: t.astype(MXU)
    conv = lambda t, c: jnp.pad(_split3(t).reshape(12, c), ((0, 4), (0, D - c)))
    return jnp.concatenate([jnp.pad(bf(a["w_in"][0]).T, ((0, IN_PAD - IN_ROWS), (0, 0))),
                            conv(a["conv_ssm_w"][0], CONV_SSM_COLS), conv(a["conv_lru_w"][0], CONV_LRU_COLS)], axis=0)


def _early_unpack(g):
    w_in_pt = _w_in_t_to_padded(g[:, :IN_ROWS].reshape(W_IN_COLS, D))
    conv = {}
    for n, r0, c in (("conv_ssm_w", IN_PAD, CONV_SSM_COLS), ("conv_lru_w", IN_PAD + 16, CONV_LRU_COLS)):
        s = g[:, r0:r0 + 12, :c].astype(F32).reshape(N_DEV, 3, 4, c)
        conv[n] = ((s[:, 0] + s[:, 1]) + s[:, 2]).transpose(1, 0, 2).reshape(4, N_DEV * c)
    return w_in_pt, conv


def _late_pack(a):
    bf = lambda t: t.astype(MXU)
    return jnp.concatenate([bf(a["w_out"][0]), bf(a["w_up"][0]).T, bf(a["w_down"][0])], axis=0)


def _late_unpack(g):
    return dict(w_out=g[:, :ROW_UP].reshape(D, D), w_upT=g[:, ROW_UP:ROW_DOWN].reshape(FF, D),
                w_down=g[:, ROW_DOWN:].reshape(FF, D))


def _own_slot(land, own):
    me = 4 * lax.axis_index("x") + 2 * lax.axis_index("y") + lax.axis_index("c")
    return lax.dynamic_update_slice_in_dim(land, own[None], me, axis=0)


def kernel(x, norm_mix_pre, w_in, conv_ssm_w, conv_ssm_b, dt_bias, a_log, d_skip, ssm_norm, conv_lru_w, conv_lru_b, lru_wa, lru_ba, lru_wx, lru_bx, lru_lambda, w_out, norm_mix_post, norm_mlp_pre, w_up, w_down, norm_mlp_post, loss_target, m_norm_mix_pre, m_w_in, m_conv_ssm_w, m_conv_ssm_b, m_dt_bias, m_a_log, m_d_skip, m_ssm_norm, m_conv_lru_w, m_conv_lru_b, m_lru_wa, m_lru_ba, m_lru_wx, m_lru_bx, m_lru_lambda, m_w_out, m_norm_mix_post, m_norm_mlp_pre, m_w_up, m_w_down, m_norm_mlp_post, v_norm_mix_pre, v_w_in, v_conv_ssm_w, v_conv_ssm_b, v_dt_bias, v_a_log, v_d_skip, v_ssm_norm, v_conv_lru_w, v_conv_lru_b, v_lru_wa, v_lru_ba, v_lru_wx, v_lru_bx, v_lru_lambda, v_w_out, v_norm_mix_post, v_norm_mlp_pre, v_w_up, v_w_down, v_norm_mlp_post):
    vals = (norm_mix_pre, w_in, conv_ssm_w, conv_ssm_b, dt_bias, a_log, d_skip, ssm_norm, conv_lru_w, conv_lru_b, lru_wa, lru_ba, lru_wx, lru_bx, lru_lambda, w_out, norm_mix_post, norm_mlp_pre, w_up, w_down, norm_mlp_post)
    m_vals = (m_norm_mix_pre, m_w_in, m_conv_ssm_w, m_conv_ssm_b, m_dt_bias, m_a_log, m_d_skip, m_ssm_norm, m_conv_lru_w, m_conv_lru_b, m_lru_wa, m_lru_ba, m_lru_wx, m_lru_bx, m_lru_lambda, m_w_out, m_norm_mix_post, m_norm_mlp_pre, m_w_up, m_w_down, m_norm_mlp_post)
    v_vals = (v_norm_mix_pre, v_w_in, v_conv_ssm_w, v_conv_ssm_b, v_dt_bias, v_a_log, v_d_skip, v_ssm_norm, v_conv_lru_w, v_conv_lru_b, v_lru_wa, v_lru_ba, v_lru_wx, v_lru_bx, v_lru_lambda, v_w_out, v_norm_mix_post, v_norm_mlp_pre, v_w_up, v_w_down, v_norm_mlp_post)
    w = dict(zip(WEIGHTS, vals))
    m = dict(zip(WEIGHTS, m_vals))
    v = dict(zip(WEIGHTS, v_vals))
    me = 4 * lax.axis_index("x") + 2 * lax.axis_index("y") + lax.axis_index("c")

    bf = lambda t: t.astype(MXU)
    late = _late_pack(w)
    early = _all_gather(_early_pack(w), "early_weights_all_gather")
    late, early = lax.optimization_barrier((late, early))
    lw = {}
    lw["sems"], lw["src"], lw["land"], token = _exchange_start(late, None, PLAN_GATHER_ICI, "late_weights_ici_start")
    w_in_pt, conv_w = _early_unpack(early)
    full = {n: (conv_w[n] if n in conv_w else w[n][0]) for n in WEIGHTS if n not in BIG}
    full["norm_mix_pre"] = full["norm_mix_pre"] + token[0, 0]

    def after_ssd(after):
        src, land = _exchange_wait(lw["sems"], lw["src"], lw["land"], after, PLAN_GATHER_ICI, "late_weights_ici_wait")
        lw["sems"], lw["src"], lw["land"], tok = _exchange_start(src, land, PLAN_GATHER_D2D, "late_weights_d2d_start")
        return tok[0, 0]

    def late_weights(after):
        src, land = _exchange_wait(lw["sems"], lw["src"], lw["land"], after, PLAN_GATHER_D2D, "late_weights_d2d_wait")
        return _late_unpack(_own_slot(land, src))

    sent = {}

    def send_mlp_grads(d_w_up_t, d_w_down, d_w_out):
        src = jnp.concatenate([bf(d_w_up_t).reshape(N_DEV, -1, D), bf(d_w_down).reshape(N_DEV, -1, D),
                               bf(d_w_out).reshape(N_DEV, -1, D)], axis=1)
        sent["sems"], sent["src"], sent["land"], tok = _exchange_start(src, None, PLAN_SCATTER, "mlp_grads_start")
        return tok[0, 0]

    def send_late_grads(d_w_in_pt):
        src = jnp.pad(bf(_w_in_t_from_padded(d_w_in_pt)).reshape(N_DEV, IN_ROWS, D),
                      ((0, 0), (0, GRAD_LATE_ROWS - IN_ROWS), (0, 0)))
        sent["sems2"], sent["src2"], sent["land2"], tok = _exchange_start(src, None, PLAN_SCATTER, "late_grads_start")
        return tok[0, 0]

    loss, grad_x, g = _local_step(x[0], loss_target[0], _prep_params(full, dict(w_in_pT=w_in_pt)), after_ssd,
                                  late_weights, send_mlp_grads, send_late_grads)
    loss = lax.psum(loss, ("x", "y", "c"))

    own = lambda src: lax.dynamic_index_in_dim(src, me, keepdims=False)
    out_g, out_d, out_m, out_v = {}, {}, {}, {}
    mlp_src, mlp_land = _exchange_wait(sent["sems"], sent["src"], sent["land"], grad_x, PLAN_SCATTER, "mlp_grads_wait")
    g_mlp = _slot_sum(_own_slot(mlp_land, own(mlp_src)), "slot_sum_mlp")
    fs = FF // N_DEV
    for n, gn in (("w_up", g_mlp[:fs].T[None]), ("w_down", g_mlp[fs:2 * fs][None]), ("w_out", g_mlp[2 * fs:][None])):
        out_g[n] = gn
        out_d[n], out_m[n], out_v[n] = _adam_big(w[n], gn, m[n], v[n], "adam_" + n)
    zrow = jnp.zeros((1, D), F32)
    pad16 = lambda a: jnp.pad(a, ((0, 0), (0, 128 - NH)))
    small_parts = [
        jnp.concatenate([g["norm_mix_pre"], g["ssm_norm"], g["conv_lru_b"], g["lru_lambda"], g["norm_mix_post"],
                         g["norm_mlp_pre"], g["norm_mlp_post"], zrow], axis=0),
        g["conv_ssm_b"],
        jnp.concatenate([pad16(g["dt_bias"]), pad16(g["a_log"]), pad16(g["d_skip"]), jnp.zeros((5, 128), F32)], axis=0),
        g["conv_ssm_w"], g["conv_lru_w"],
        jnp.concatenate([g["lru_wa"].reshape(D, 64), g["lru_wx"].reshape(D, 64)], axis=0),
        jnp.concatenate([g["lru_ba"].reshape(NH, 64), g["lru_bx"].reshape(NH, 64)], axis=0)]
    small = _pad_rows(jnp.concatenate([s.reshape(-1) for s in small_parts]), 8)
    small, _ = lax.optimization_barrier((small, out_v["w_out"]))
    small_all = _all_gather(small, "small_grads_all_gather")

    lg_src, lg_land = _exchange_wait(sent["sems2"], sent["src2"], sent["land2"], small_all, PLAN_SCATTER, "late_grads_wait")
    g_late = _slot_sum(_own_slot(lg_land, own(lg_src)), "slot_sum_late")
    gt = g_late[:IN_ROWS]
    dt_, mt_, vt_ = _adam_big(w["w_in"][0].T, gt, m["w_in"][0].T, v["w_in"][0].T, "adam_w_in")
    out_g["w_in"], out_d["w_in"], out_m["w_in"], out_v["w_in"] = gt.T[None], dt_.T[None], mt_.T[None], vt_.T[None]
    sflat = small_all.reshape(N_DEV, -1)
    groups = []
    off = 0
    for r, c in SMALL_GROUPS:
        groups.append(sflat[:, off:off + r * c].reshape(N_DEV, r, c))
        off += r * c
    groups[3] = lax.dynamic_slice_in_dim(groups[3], me * CONV_SSM_COLS, CONV_SSM_COLS, axis=2)
    groups[4] = lax.dynamic_slice_in_dim(groups[4], me * CONV_LRU_COLS, CONV_LRU_COLS, axis=2)
    wmv = [(w[n].reshape(s), m[n].reshape(s), v[n].reshape(s)) for n, s, _, _ in SMALL]
    res = _adam_small(groups, [(gi, r0) for _, _, gi, r0 in SMALL], wmv)
    for (n, _, _, _), (g_n, d_n, m_n, v_n) in zip(SMALL, res):
        shape = w[n].shape
        out_g[n], out_d[n], out_m[n], out_v[n] = (g_n.reshape(shape), d_n.reshape(shape), m_n.reshape(shape),
                                                  v_n.reshape(shape))
    return (loss, grad_x[None], *[out_g[n] for n in WEIGHTS], *[out_d[n] for n in WEIGHTS],
            *[out_m[n] for n in WEIGHTS], *[out_v[n] for n in WEIGHTS])
```

```python
import functools

import jax
import jax.numpy as jnp
from jax import lax
from jax.experimental import pallas as pl
from jax.experimental.pallas import tpu as pltpu

F32 = jnp.float32
MXU = jnp.bfloat16
HI = lax.Precision.HIGHEST
EPS = 1e-6

D = 1024
NH = 16
NS = 128
CH = 128
FF = 4096
NP = 7168
SEG = 1024
LRU_C = 8.0
N_DEV = 8

ADAM_LR, ADAM_B1, ADAM_B2, ADAM_EPS, ADAM_WD, ADAM_STEP = 0.001, 0.9, 0.999, 1e-08, 0.01, 10

VMEM_LIMIT = 56 * 1024 * 1024


def _cp(*sem):
    return pltpu.CompilerParams(dimension_semantics=sem, vmem_limit_bytes=VMEM_LIMIT)


def _nn(a, b):
    return jnp.dot(a.astype(MXU), b.astype(MXU), preferred_element_type=F32)


def _nt(a, b):
    return lax.dot_general(a.astype(MXU), b.astype(MXU), (((1,), (1,)), ((), ())), preferred_element_type=F32)


def _tn(a, b):
    return lax.dot_general(a.astype(MXU), b.astype(MXU), (((0,), (0,)), ((), ())), preferred_element_type=F32)


_sigmoid = jax.nn.sigmoid


def _silu(x):
    return x * _sigmoid(x)


def _dsilu(x):
    s = _sigmoid(x)
    return s + x * s * (1.0 - s)


def _softplus(x):
    return jnp.maximum(x, 0.0) + jnp.log(1.0 + jnp.exp(-jnp.abs(x)))


def _rms(x, g):
    r = lax.rsqrt(jnp.mean(x * x, axis=-1, keepdims=True) + EPS)
    return x * r * g


def _rms_bwd(x, g, dy):
    r = lax.rsqrt(jnp.mean(x * x, axis=-1, keepdims=True) + EPS)
    gdy = g * dy
    dx = r * gdy - x * (r * r * r) * jnp.mean(x * gdy, axis=-1, keepdims=True)
    return dx, dy * x * r


def _rowsum(x):
    return jnp.sum(x, axis=0, keepdims=True)


def _taps_past(cur, prev8):
    r_n, c_n = cur.shape
    row = lax.broadcasted_iota(jnp.int32, (r_n, c_n), 0)
    out = []
    for k in range(4):
        s = 3 - k
        if s == 0:
            out.append(cur)
            continue
        head = jnp.concatenate([pltpu.roll(prev8, s, 0), jnp.zeros((r_n - 8, c_n), F32)], axis=0)
        out.append(jnp.where(row < s, head, pltpu.roll(cur, s, 0)))
    return out


def _taps_future(cur, fut8):
    r_n, c_n = cur.shape
    row = lax.broadcasted_iota(jnp.int32, (r_n, c_n), 0)
    out = []
    for k in range(4):
        s = 3 - k
        if s == 0:
            out.append(cur)
            continue
        tail = jnp.concatenate([jnp.zeros((r_n - 8, c_n), F32), pltpu.roll(fut8, 8 - s, 0)], axis=0)
        out.append(jnp.where(row >= r_n - s, tail, pltpu.roll(cur, r_n - s, 0)))
    return out


def _conv_apply(taps, w, b):
    acc = taps[0] * w[0:1, :]
    for k in range(1, 4):
        acc = acc + taps[k] * w[k:k + 1, :]
    return acc + b


def _inproj(x, g0, w):
    t_n = x.shape[0]
    tm = min(t_n, 1024)

    def body(x_ref, g_ref, w_ref, p_ref, u_ref):
        @pl.when(pl.program_id(1) == 0)
        def _():
            u_ref[...] = _rms(x_ref[...], g_ref[...]).astype(MXU)

        p_ref[...] = lax.dot_general(u_ref[...], w_ref[...], (((1,), (1,)), ((), ())), preferred_element_type=F32)

    return pl.pallas_call(
        body, name="inproj", grid=(t_n // tm, NP // SEG),
        in_specs=[pl.BlockSpec((tm, D), lambda i, j: (i, 0)), pl.BlockSpec((1, D), lambda i, j: (0, 0)),
                  pl.BlockSpec((SEG, D), lambda i, j: (j, 0))],
        out_specs=[pl.BlockSpec((tm, SEG), lambda i, j: (i, j)), pl.BlockSpec((tm, D), lambda i, j: (i, 0))],
        out_shape=[jax.ShapeDtypeStruct((t_n, NP), F32), jax.ShapeDtypeStruct((t_n, D), MXU)],
        compiler_params=_cp("parallel", "arbitrary"),
    )(x, g0, w)


def _ssd_prep(dtraw, dtb, alog):
    l_n = dtraw.shape[0]
    r = lax.broadcasted_iota(jnp.int32, (l_n, l_n), 0)
    c = lax.broadcasted_iota(jnp.int32, (l_n, l_n), 1)
    tril = (r >= c).astype(F32)
    triu = (r <= c).astype(F32)
    eye = (r == c).astype(F32)
    dt = _softplus(dtraw + dtb)
    adt = dt * (-jnp.exp(alog))
    ac = jnp.dot(tril, adt, preferred_element_type=F32, precision=HI)
    tn = (((0,), (0,)), ((), ()))
    ac_t = lax.dot_general(adt, triu, tn, preferred_element_type=F32, precision=HI)
    dt_t = lax.dot_general(dt, eye, tn, preferred_element_type=F32, precision=HI)
    return dt, dt_t, ac, ac_t, _rowsum(adt)


def _ssd_pair(j, xp, bg, cg, sp, dt, dt_t, ac, ac_t, aend):
    l_n = xp.shape[0]
    lane = lax.broadcasted_iota(jnp.int32, (l_n, 128), 1)
    sub = lax.broadcasted_iota(jnp.int32, (128, l_n), 0)
    lane1 = lax.broadcasted_iota(jnp.int32, (1, 128), 1)
    tri = lax.broadcasted_iota(jnp.int32, (l_n, l_n), 0) >= lax.broadcasted_iota(jnp.int32, (l_n, l_n), 1)
    lo = lax.broadcasted_iota(jnp.int32, (l_n, 128), 1) < 64
    lo_s = lax.broadcasted_iota(jnp.int32, (128, 128), 1) < 64
    cb = _nt(cg, bg)
    cs = _nn(cg, sp)
    x2 = jnp.concatenate([jnp.where(lo, xp, 0.0), jnp.where(lo, 0.0, xp)], axis=0)
    ws_, bs_, eo, ee = [], [], [], []
    for e in range(2):
        h = 2 * j + e
        ac_l = jnp.sum(jnp.where(lane == h, ac, 0.0), axis=1, keepdims=True)
        dt_l = jnp.sum(jnp.where(lane == h, dt, 0.0), axis=1, keepdims=True)
        a_end = jnp.sum(jnp.where(lane1 == h, aend, 0.0), axis=1, keepdims=True)
        ac_s, dt_s = ac_t[h:h + 1, :], dt_t[h:h + 1, :]
        decay = jnp.exp(jnp.where(tri, ac_l - ac_s, -1e30))
        ws_.append(cb * decay * dt_s)
        bs_.append(bg * (jnp.exp(a_end - ac_l) * dt_l))
        eo.append(jnp.exp(ac_l))
        ee.append(jnp.exp(a_end))
    y = _nn(jnp.concatenate(ws_, axis=1), x2) + jnp.where(lo, eo[0], eo[1]) * cs
    s_new = _tn(jnp.concatenate(bs_, axis=0), x2) + jnp.where(lo_s, ee[0], ee[1]) * sp
    return y, s_new


def _ssd_post(y, xs, z, dsk, nrm):
    y = (y + dsk * xs) * _silu(z)
    half = D // 2
    ya, yb = y[:, :half], y[:, half:]
    ya = ya * lax.rsqrt(jnp.mean(ya * ya, axis=-1, keepdims=True) + EPS)
    yb = yb * lax.rsqrt(jnp.mean(yb * yb, axis=-1, keepdims=True) + EPS)
    return jnp.concatenate([ya, yb], axis=1) * nrm


def _proj_specs(rows, seg_ids, order):
    return [pl.BlockSpec((rows, SEG), functools.partial(lambda i, s: (order(i), s), s=s)) for s in seg_ids]


def _prev8_specs(rows, seg_ids, order):
    rb = rows // 8
    return [pl.BlockSpec((8, SEG), functools.partial(lambda i, s: (jnp.maximum(order(i) * rb - 1, 0), s), s=s))
            for s in seg_ids]


def _full(shape):
    return pl.BlockSpec(shape, lambda i: (0,) * len(shape))


def _ssd_fwd(proj, cwx, cwb, cbx, cbb, dtb, alog, dsk, nrm):
    t_n = proj.shape[0]
    n_c = t_n // CH
    fwd = lambda i: i

    def body(z_ref, xs_ref, bc_ref, xsp_ref, bcp_ref, cwx_ref, cwb_ref, cbx_ref, cbb_ref, dtb_ref, alog_ref,
             dsk_ref, nrm_ref, ya_ref, sprev_ref, yraw_ref, s_ref):
        c = pl.program_id(0)

        @pl.when(c == 0)
        def _():
            s_ref[...] = jnp.zeros_like(s_ref)

        keep = jnp.where(c == 0, 0.0, 1.0)
        xs_pre = _conv_apply(_taps_past(xs_ref[...], xsp_ref[...] * keep), cwx_ref[...], cbx_ref[...])
        bc_pre = _conv_apply(_taps_past(bc_ref[:, :512], bcp_ref[:, :512] * keep), cwb_ref[...], cbb_ref[...])
        prep = _ssd_prep(bc_ref[:, 512:640], dtb_ref[...], alog_ref[...])
        xs = _silu(xs_pre)
        bc = _silu(bc_pre)
        sprev_ref[0] = s_ref[...]
        ys = []
        for j in range(NH // 2):
            g = j // 4
            yp, sn = _ssd_pair(j, xs[:, 128 * j:128 * j + 128], bc[:, 128 * g:128 * g + 128],
                               bc[:, 256 + 128 * g:384 + 128 * g], s_ref[:, 128 * j:128 * j + 128], *prep)
            ys.append(yp)
            s_ref[:, 128 * j:128 * j + 128] = sn
        y = jnp.concatenate(ys, axis=1)
        yraw_ref[...] = y
        ya_ref[...] = _ssd_post(y, xs, z_ref[...], dsk_ref[...], nrm_ref[...])

    return pl.pallas_call(
        body, name="ssd_fwd", grid=(n_c,),
        in_specs=_proj_specs(CH, (0, 5, 6), fwd) + _prev8_specs(CH, (5, 6), fwd) + [
            _full((4, D)), _full((4, 512)), _full((1, D)), _full((1, 512)), _full((1, 128)), _full((1, 128)),
            _full((1, D)), _full((1, D))],
        out_specs=[pl.BlockSpec((CH, D), lambda i: (i, 0)), pl.BlockSpec((1, NS, D), lambda i: (i, 0, 0)),
                   pl.BlockSpec((CH, D), lambda i: (i, 0))],
        out_shape=[jax.ShapeDtypeStruct((t_n, D), F32), jax.ShapeDtypeStruct((n_c, NS, D), F32),
                   jax.ShapeDtypeStruct((t_n, D), F32)],
        scratch_shapes=[pltpu.VMEM((NS, D), F32)],
        compiler_params=_cp("arbitrary"),
    )(proj, proj, proj, proj, proj, cwx, cwb, cbx, cbb, dtb, alog, dsk, nrm)


def _ssd_bwd(dya, yraw, proj, sprev, cwx, cwb, cbx, cbb, dtb, alog, dsk, nrm, mlp_ops):
    t_n = proj.shape[0]
    n_c = t_n // CH
    rev = lambda i: n_c - 1 - i
    fb = FF // n_c

    def body(dya_ref, yraw_ref, z_ref, xs_ref, bc_ref, xsp_ref, bcp_ref, sprev_ref, cwx_ref, cwb_ref, cbx_ref, cbb_ref,
             dtb_ref, alog_ref, dsk_ref, nrm_ref, hid_ref, dff_ref, dhp_ref, v_ref,
             dz_ref, dxs_ref, dbc_ref, dcwx_ref, dcwb_ref, dcbx_ref, dcbb_ref, ddtb_ref, dalog_ref, ddsk_ref,
             dnrm_ref, dwd_ref, dwu_ref, ds_ref, futx_ref, futb_ref):
        i = pl.program_id(0)
        acc_refs = (dcwx_ref, dcwb_ref, dcbx_ref, dcbb_ref, ddtb_ref, dalog_ref, ddsk_ref, dnrm_ref)
        tn = (((0,), (0,)), ((), ()))
        dwd_ref[...] = lax.dot_general(hid_ref[...], dff_ref[...], tn, preferred_element_type=F32).astype(MXU)
        dwu_ref[...] = lax.dot_general(dhp_ref[...], v_ref[...], tn, preferred_element_type=F32).astype(MXU)

        @pl.when(i == 0)
        def _():
            for r in (ds_ref, futx_ref, futb_ref) + acc_refs:
                r[...] = jnp.zeros_like(r)

        keep = jnp.where(i == n_c - 1, 0.0, 1.0)
        taps_x = _taps_past(xs_ref[...], xsp_ref[...] * keep)
        taps_b = _taps_past(bc_ref[:, :512], bcp_ref[:, :512] * keep)
        xs_pre = _conv_apply(taps_x, cwx_ref[...], cbx_ref[...])
        bc_pre = _conv_apply(taps_b, cwb_ref[...], cbb_ref[...])
        xs = _silu(xs_pre)
        bc = _silu(bc_pre)
        prep, prep_vjp = jax.vjp(_ssd_prep, bc_ref[:, 512:640], dtb_ref[...], alog_ref[...])
        s_in = sprev_ref[0]

        def pair_args(j):
            g = j // 4
            return (xs[:, 128 * j:128 * j + 128], bc[:, 128 * g:128 * g + 128],
                    bc[:, 256 + 128 * g:384 + 128 * g], s_in[:, 128 * j:128 * j + 128]) + tuple(prep)

        _, post_vjp = jax.vjp(_ssd_post, yraw_ref[...], xs, z_ref[...], dsk_ref[...], nrm_ref[...])
        dy, dxs_skip, dz, ddsk, dnrm = post_vjp(dya_ref[...])
        dz_ref[...] = dz.astype(dz_ref.dtype)
        ddsk_ref[...] += ddsk
        dnrm_ref[...] += dnrm

        dprep = [jnp.zeros_like(p) for p in prep]
        dxp = []
        dbg = [jnp.zeros((CH, 128), F32), jnp.zeros((CH, 128), F32)]
        dcg = [jnp.zeros((CH, 128), F32), jnp.zeros((CH, 128), F32)]
        for j in range(NH // 2):
            g = j // 4
            _, pair_vjp = jax.vjp(functools.partial(_ssd_pair, j), *pair_args(j))
            cts = pair_vjp((dy[:, 128 * j:128 * j + 128], ds_ref[:, 128 * j:128 * j + 128]))
            dxp.append(cts[0])
            dbg[g] = dbg[g] + cts[1]
            dcg[g] = dcg[g] + cts[2]
            ds_ref[:, 128 * j:128 * j + 128] = cts[3]
            dprep = [a + b for a, b in zip(dprep, cts[4:])]
        ddtraw, ddtb, dalog = prep_vjp(tuple(dprep))
        ddtb_ref[...] += ddtb
        dalog_ref[...] += dalog

        dxs_pre = (dxs_skip + jnp.concatenate(dxp, axis=1)) * _dsilu(xs_pre)
        dbc_pre = jnp.concatenate([dbg[0], dbg[1], dcg[0], dcg[1]], axis=1) * _dsilu(bc_pre)
        dcbx_ref[...] += _rowsum(dxs_pre)
        dcbb_ref[...] += _rowsum(dbc_pre)
        for k in range(4):
            dcwx_ref[k:k + 1, :] += _rowsum(dxs_pre * taps_x[k])
            dcwb_ref[k:k + 1, :] += _rowsum(dbc_pre * taps_b[k])
        fx = _taps_future(dxs_pre, futx_ref[...])
        fb = _taps_future(dbc_pre, futb_ref[...])
        cwx = cwx_ref[...]
        cwb = cwb_ref[...]
        dxs_in = fx[0] * cwx[0:1, :]
        dbc_in = fb[0] * cwb[0:1, :]
        for k in range(1, 4):
            dxs_in = dxs_in + fx[k] * cwx[k:k + 1, :]
            dbc_in = dbc_in + fb[k] * cwb[k:k + 1, :]
        futx_ref[...] = dxs_pre[0:8, :]
        futb_ref[...] = dbc_pre[0:8, :]
        dxs_ref[...] = dxs_in.astype(dxs_ref.dtype)
        dbc_ref[...] = jnp.concatenate([dbc_in, ddtraw, jnp.zeros((CH, SEG - 640), F32)], axis=1).astype(dbc_ref.dtype)

    row_out = lambda: pl.BlockSpec((CH, D), lambda i: (rev(i), 0))
    outs = pl.pallas_call(
        body, name="ssd_bwd", grid=(n_c,),
        in_specs=[row_out(), row_out()] + _proj_specs(CH, (0, 5, 6), rev)
        + _prev8_specs(CH, (5, 6), rev) + [pl.BlockSpec((1, NS, D), lambda i: (rev(i), 0, 0)),
                                            _full((4, D)), _full((4, 512)), _full((1, D)), _full((1, 512)),
                                            _full((1, 128)), _full((1, 128)), _full((1, D)), _full((1, D)),
                                            pl.BlockSpec((t_n, fb), lambda i: (0, i)), _full((t_n, D)),
                                            pl.BlockSpec((t_n, fb), lambda i: (0, i)), _full((t_n, D))],
        out_specs=[row_out(), row_out(), row_out(), _full((4, D)), _full((4, 512)), _full((1, D)), _full((1, 512)),
                   _full((1, 128)), _full((1, 128)), _full((1, D)), _full((1, D)),
                   pl.BlockSpec((fb, D), lambda i: (i, 0)), pl.BlockSpec((fb, D), lambda i: (i, 0))],
        out_shape=[jax.ShapeDtypeStruct((t_n, D), MXU)] * 3 + [
            jax.ShapeDtypeStruct(s, F32) for s in ((4, D), (4, 512), (1, D), (1, 512), (1, 128), (1, 128), (1, D), (1, D))]
        + [jax.ShapeDtypeStruct((FF, D), MXU)] * 2,
        scratch_shapes=[pltpu.VMEM((NS, D), F32), pltpu.VMEM((8, D), F32), pltpu.VMEM((8, 512), F32)],
        compiler_params=_cp("arbitrary"),
    )(dya, yraw, proj, proj, proj, proj, proj, sprev, cwx, cwb, cbx, cbb, dtb, alog, dsk, nrm, *mlp_ops)
    return outs


LRU_ROWS = 256
LRU_BLK = 256


def _lru_gates(xr, wa, wx, ba, bx, lam):
    pr = jnp.concatenate([_nn(xr[:, LRU_BLK * b:LRU_BLK * (b + 1)], wa[b]) for b in range(D // LRU_BLK)], axis=1) + ba
    pi = jnp.concatenate([_nn(xr[:, LRU_BLK * b:LRU_BLK * (b + 1)], wx[b]) for b in range(D // LRU_BLK)], axis=1) + bx
    log_a = -LRU_C * _sigmoid(pr) * _softplus(-lam)
    a = jnp.exp(log_a)
    mult = jnp.sqrt(1.0 - jnp.exp(2.0 * log_a))
    return a, mult * (_sigmoid(pi) * xr)


def _lru_out(h, g):
    return h * jax.nn.gelu(g, approximate=True)


def _lru_fwd(proj, cw, cb, wa, wx, ba, bx, lam):
    t_n = proj.shape[0]
    rows = min(LRU_ROWS, t_n)
    fwd = lambda i: i

    def body(g_ref, x_ref, xp_ref, cw_ref, cb_ref, wa_ref, wx_ref, ba_ref, bx_ref, lam_ref, yb_ref, h_ref,
             a_s, u_s, carry):
        i = pl.program_id(0)

        @pl.when(i == 0)
        def _():
            carry[...] = jnp.zeros_like(carry)

        keep = jnp.where(i == 0, 0.0, 1.0)
        xr = _conv_apply(_taps_past(x_ref[...], xp_ref[...] * keep), cw_ref[...], cb_ref[...])
        a, u = _lru_gates(xr, wa_ref[...], wx_ref[...], ba_ref[...], bx_ref[...], lam_ref[...])
        a_s[...] = a
        u_s[...] = u
        row = lax.broadcasted_iota(jnp.int32, (8, D), 0)

        def blk(b, c):
            s = pl.multiple_of(b * 8, 8)
            av = a_s[pl.ds(s, 8), :]
            uv = u_s[pl.ds(s, 8), :]
            for d in (1, 2, 4):
                m = row >= d
                uv = uv + av * jnp.where(m, pltpu.roll(uv, d, 0), 0.0)
                av = av * jnp.where(m, pltpu.roll(av, d, 0), 1.0)
            hv = uv + av * c
            h_ref[pl.ds(s, 8), :] = hv
            return hv[7:8, :]

        carry[0:1, :] = lax.fori_loop(0, rows // 8, blk, carry[0:1, :])
        yb_ref[...] = _lru_out(h_ref[...], g_ref[...])

    return pl.pallas_call(
        body, name="lru_fwd", grid=(t_n // rows,),
        in_specs=_proj_specs(rows, (1, 2), fwd) + _prev8_specs(rows, (2,), fwd) + [
            _full((4, D)), _full((1, D)), _full((4, LRU_BLK, LRU_BLK)), _full((4, LRU_BLK, LRU_BLK)),
            _full((1, D)), _full((1, D)), _full((1, D))],
        out_specs=[pl.BlockSpec((rows, D), lambda i: (i, 0)), pl.BlockSpec((rows, D), lambda i: (i, 0))],
        out_shape=[jax.ShapeDtypeStruct((t_n, D), F32), jax.ShapeDtypeStruct((t_n, D), F32)],
        scratch_shapes=[pltpu.VMEM((rows, D), F32), pltpu.VMEM((rows, D), F32), pltpu.VMEM((8, D), F32)],
        compiler_params=_cp("arbitrary"),
    )(proj, proj, proj, cw, cb, wa, wx, ba, bx, lam)


def _lru_bwd(dyb, proj, h, cw, cb, wa, wx, ba, bx, lam):
    t_n = proj.shape[0]
    rows = min(LRU_ROWS, t_n)
    n_t = t_n // rows
    rev = lambda i: n_t - 1 - i
    rb = rows // 8

    def body(dyb_ref, g_ref, x_ref, xp_ref, h_ref, hp_ref, cw_ref, cb_ref, wa_ref, wx_ref, ba_ref, bx_ref, lam_ref,
             dg_ref, dx_ref, dcw_ref, dcb_ref, dwa_ref, dwx_ref, dba_ref, dbx_ref, dlam_ref,
             a_s, dh_s, hx_s, da_s, du_s, carry, fut):
        i = pl.program_id(0)
        acc_refs = (dcw_ref, dcb_ref, dwa_ref, dwx_ref, dba_ref, dbx_ref, dlam_ref)

        @pl.when(i == 0)
        def _():
            for r in (carry, fut) + acc_refs:
                r[...] = jnp.zeros_like(r)

        keep = jnp.where(i == n_t - 1, 0.0, 1.0)
        taps = _taps_past(x_ref[...], xp_ref[...] * keep)
        xr = _conv_apply(taps, cw_ref[...], cb_ref[...])
        gate_in = (xr, wa_ref[...], wx_ref[...], ba_ref[...], bx_ref[...], lam_ref[...])
        (a, _), gates_vjp = jax.vjp(_lru_gates, *gate_in)
        _, out_vjp = jax.vjp(_lru_out, h_ref[...], g_ref[...])
        dh, dg = out_vjp(dyb_ref[...])
        dg_ref[...] = dg.astype(dg_ref.dtype)
        a_s[...] = a
        dh_s[...] = dh
        hx_s[0:8, :] = hp_ref[...] * keep
        hx_s[8:, :] = h_ref[...]
        row = lax.broadcasted_iota(jnp.int32, (8, D), 0)

        def blk(b, c):
            s = pl.multiple_of((rb - 1 - b) * 8, 8)
            av = a_s[pl.ds(s, 8), :]
            dhv = dh_s[pl.ds(s, 8), :]
            a0 = av
            kv = av * dhv
            for d in (1, 2, 4):
                m = row <= 7 - d
                kv = kv + av * jnp.where(m, pltpu.roll(kv, 8 - d, 0), 0.0)
                av = av * jnp.where(m, pltpu.roll(av, 8 - d, 0), 1.0)
            kv = kv + av * c
            gv = dhv + jnp.where(row < 7, pltpu.roll(kv, 7, 0), c)
            hb = hx_s[pl.ds(s + 8, 8), :]
            hpv = hx_s[pl.ds(s, 8), :]
            hprev = jnp.where(row >= 1, pltpu.roll(hb, 1, 0), hpv[7:8, :])
            du_s[pl.ds(s, 8), :] = gv
            da_s[pl.ds(s, 8), :] = gv * hprev
            del a0
            return kv[0:1, :]

        carry[0:1, :] = lax.fori_loop(0, rb, blk, carry[0:1, :])
        dxr, dwa, dwx, dba, dbx, dlam = gates_vjp((da_s[...], du_s[...]))
        dwa_ref[...] += dwa
        dwx_ref[...] += dwx
        dba_ref[...] += dba
        dbx_ref[...] += dbx
        dlam_ref[...] += dlam
        dcb_ref[...] += _rowsum(dxr)
        for k in range(4):
            dcw_ref[k:k + 1, :] += _rowsum(dxr * taps[k])
        ft = _taps_future(dxr, fut[...])
        cwv = cw_ref[...]
        dx = ft[0] * cwv[0:1, :]
        for k in range(1, 4):
            dx = dx + ft[k] * cwv[k:k + 1, :]
        fut[...] = dxr[0:8, :]
        dx_ref[...] = dx.astype(dx_ref.dtype)

    row_in = lambda: pl.BlockSpec((rows, D), lambda i: (rev(i), 0))
    prev_h = pl.BlockSpec((8, D), lambda i: (jnp.maximum(rev(i) * rb - 1, 0), 0))
    wspec = lambda: _full((4, LRU_BLK, LRU_BLK))
    return pl.pallas_call(
        body, name="lru_bwd", grid=(n_t,),
        in_specs=[row_in()] + _proj_specs(rows, (1, 2), rev) + _prev8_specs(rows, (2,), rev) + [row_in(), prev_h] + [
            _full((4, D)), _full((1, D)), wspec(), wspec(), _full((1, D)), _full((1, D)), _full((1, D))],
        out_specs=[row_in(), row_in(), _full((4, D)), _full((1, D)), wspec(), wspec(), _full((1, D)), _full((1, D)),
                   _full((1, D))],
        out_shape=[jax.ShapeDtypeStruct((t_n, D), MXU)] * 2 + [
            jax.ShapeDtypeStruct(s, F32) for s in ((4, D), (1, D), (4, LRU_BLK, LRU_BLK), (4, LRU_BLK, LRU_BLK),
                                                   (1, D), (1, D), (1, D))],
        scratch_shapes=[pltpu.VMEM((rows, D), F32), pltpu.VMEM((rows, D), F32), pltpu.VMEM((rows + 8, D), F32),
                        pltpu.VMEM((rows, D), F32), pltpu.VMEM((rows, D), F32), pltpu.VMEM((8, D), F32),
                        pltpu.VMEM((8, D), F32)],
        compiler_params=_cp("arbitrary"),
    )(dyb, proj, proj, proj, h, h, cw, cb, wa, wx, ba, bx, lam)


def _merge_out(ya, yb, proj, x, wout, g1):
    t_n = x.shape[0]
    tm = min(t_n, 512)

    def body(ya_ref, yb_ref, ga_ref, gb_ref, x_ref, w_ref, g_ref, h1_ref, mix_ref, mg_ref):
        merged = _sigmoid(ga_ref[...]) * ya_ref[...] + _sigmoid(gb_ref[...]) * yb_ref[...]
        mg = merged.astype(MXU)
        mg_ref[...] = mg
        mix = jnp.dot(mg, w_ref[...], preferred_element_type=F32)
        mix_ref[...] = mix
        h1_ref[...] = x_ref[...] + _rms(mix, g_ref[...])

    row = lambda: pl.BlockSpec((tm, D), lambda i: (i, 0))
    return pl.pallas_call(
        body, name="merge_out", grid=(t_n // tm,),
        in_specs=[row(), row()] + _proj_specs(tm, (3, 4), lambda i: i) + [row(), _full((D, D)), _full((1, D))],
        out_specs=[row(), row(), row()],
        out_shape=[jax.ShapeDtypeStruct((t_n, D), F32), jax.ShapeDtypeStruct((t_n, D), F32),
                   jax.ShapeDtypeStruct((t_n, D), MXU)],
        compiler_params=_cp("parallel"),
    )(ya, yb, proj, proj, x, wout, g1)


def _out_bwd(dh1, mix, ya, yb, proj, wout, g1):
    t_n = dh1.shape[0]
    tm = min(t_n, 512)

    def body(dh1_ref, mix_ref, ya_ref, yb_ref, ga_ref, gb_ref, w_ref, g_ref,
             dmix_ref, dya_ref, dyb_ref, dga_ref, dgb_ref, dg1_ref):
        @pl.when(pl.program_id(0) == 0)
        def _():
            dg1_ref[...] = jnp.zeros_like(dg1_ref)

        dmix, dg_rows = _rms_bwd(mix_ref[...], g_ref[...], dh1_ref[...])
        dg1_ref[...] += _rowsum(dg_rows)
        dmix_b = dmix.astype(MXU)
        dmix_ref[...] = dmix_b
        dmg = lax.dot_general(dmix_b, w_ref[...], (((1,), (1,)), ((), ())), preferred_element_type=F32)
        sa = _sigmoid(ga_ref[...])
        sb = _sigmoid(gb_ref[...])
        dya_ref[...] = dmg * sa
        dyb_ref[...] = dmg * sb
        dga_ref[...] = (dmg * ya_ref[...] * sa * (1.0 - sa)).astype(MXU)
        dgb_ref[...] = (dmg * yb_ref[...] * sb * (1.0 - sb)).astype(MXU)

    row = lambda: pl.BlockSpec((tm, D), lambda i: (i, 0))
    return pl.pallas_call(
        body, name="out_bwd", grid=(t_n // tm,),
        in_specs=[row(), row(), row(), row()] + _proj_specs(tm, (3, 4), lambda i: i) + [_full((D, D)), _full((1, D))],
        out_specs=[row(), row(), row(), row(), row(), _full((1, D))],
        out_shape=[jax.ShapeDtypeStruct((t_n, D), MXU), jax.ShapeDtypeStruct((t_n, D), F32),
                   jax.ShapeDtypeStruct((t_n, D), F32), jax.ShapeDtypeStruct((t_n, D), MXU),
                   jax.ShapeDtypeStruct((t_n, D), MXU), jax.ShapeDtypeStruct((1, D), F32)],
        compiler_params=_cp("arbitrary"),
    )(dh1, mix, ya, yb, proj, proj, wout, g1)


MLP_TM = 1024
MLP_TF_FWD = 512
MLP_TF_BWD = 1024


def _mlp_fwd(h1, g2, wup, wdown, g3, tgt):
    t_n = h1.shape[0]
    tm = min(t_n, MLP_TM)
    n_f = FF // MLP_TF_FWD

    def body(h1_ref, g2_ref, wu_ref, wd_ref, g3_ref, tgt_ref, hp_ref, v_ref, dout_ref, dff_ref, loss_ref, dg3_ref, acc):
        i, j = pl.program_id(0), pl.program_id(1)

        @pl.when((i == 0) & (j == 0))
        def _():
            loss_ref[...] = jnp.zeros_like(loss_ref)
            dg3_ref[...] = jnp.zeros_like(dg3_ref)

        @pl.when(j == 0)
        def _():
            v_ref[...] = _rms(h1_ref[...], g2_ref[...]).astype(MXU)
            acc[...] = jnp.zeros_like(acc)

        hp = lax.dot_general(v_ref[...], wu_ref[...], (((1,), (1,)), ((), ())), preferred_element_type=F32)
        hp_ref[...] = hp
        hid = jnp.square(jnp.maximum(hp, 0.0))
        acc[...] += jnp.dot(hid.astype(MXU), wd_ref[...], preferred_element_type=F32)

        @pl.when(j == n_f - 1)
        def _():
            ff = acc[...]
            err = h1_ref[...] + _rms(ff, g3_ref[...]) - tgt_ref[...]
            loss_ref[...] += 0.5 * jnp.sum(jnp.mean(err * err, axis=-1, keepdims=True), axis=0, keepdims=True)
            dout = err * (1.0 / D)
            dout_ref[...] = dout
            dff, dg_rows = _rms_bwd(ff, g3_ref[...], dout)
            dg3_ref[...] += _rowsum(dg_rows)
            dff_ref[...] = dff.astype(MXU)

    row = lambda: pl.BlockSpec((tm, D), lambda i, j: (i, 0))
    vec = lambda: pl.BlockSpec((1, D), lambda i, j: (0, 0))
    return pl.pallas_call(
        body, name="mlp_fwd", grid=(t_n // tm, n_f),
        in_specs=[row(), vec(), pl.BlockSpec((MLP_TF_FWD, D), lambda i, j: (j, 0)),
                  pl.BlockSpec((MLP_TF_FWD, D), lambda i, j: (j, 0)), vec(), row()],
        out_specs=[pl.BlockSpec((tm, MLP_TF_FWD), lambda i, j: (i, j)), row(), row(), row(),
                   pl.BlockSpec((1, 1), lambda i, j: (0, 0)), vec()],
        out_shape=[jax.ShapeDtypeStruct((t_n, FF), F32), jax.ShapeDtypeStruct((t_n, D), MXU),
                   jax.ShapeDtypeStruct((t_n, D), F32), jax.ShapeDtypeStruct((t_n, D), MXU),
                   jax.ShapeDtypeStruct((1, 1), F32), jax.ShapeDtypeStruct((1, D), F32)],
        scratch_shapes=[pltpu.VMEM((tm, D), F32)],
        compiler_params=_cp("arbitrary", "arbitrary"),
    )(h1, g2, wup, wdown, g3, tgt)


def _mlp_bwd(dff, dout, hp, h1, wup, wdown, g2):
    t_n = h1.shape[0]
    tm = min(t_n, MLP_TM)
    n_f = FF // MLP_TF_BWD

    def mm_body(dff_ref, hp_ref, wu_ref, wd_ref, dv_ref, dhp_ref, hid_ref):
        @pl.when(pl.program_id(1) == 0)
        def _():
            dv_ref[...] = jnp.zeros_like(dv_ref)

        relu = jnp.maximum(hp_ref[...], 0.0)
        hid_ref[...] = jnp.square(relu).astype(MXU)
        dhid = lax.dot_general(dff_ref[...], wd_ref[...], (((1,), (1,)), ((), ())), preferred_element_type=F32)
        dhp = (dhid * (2.0 * relu)).astype(MXU)
        dhp_ref[...] = dhp
        dv_ref[...] += jnp.dot(dhp, wu_ref[...], preferred_element_type=F32)

    row = lambda: pl.BlockSpec((tm, D), lambda i, j: (i, 0))
    blk = lambda: pl.BlockSpec((tm, MLP_TF_BWD), lambda i, j: (i, j))
    wblk = lambda: pl.BlockSpec((MLP_TF_BWD, D), lambda i, j: (j, 0))
    dv, dhp, hid = pl.pallas_call(
        mm_body, name="mlp_bwd", grid=(t_n // tm, n_f),
        in_specs=[row(), blk(), wblk(), wblk()], out_specs=[row(), blk(), blk()],
        out_shape=[jax.ShapeDtypeStruct((t_n, D), F32), jax.ShapeDtypeStruct((t_n, FF), MXU),
                   jax.ShapeDtypeStruct((t_n, FF), MXU)],
        compiler_params=_cp("parallel", "arbitrary"),
    )(dff, hp, wup, wdown)

    tf = min(t_n, 512)

    def fin_body(dv_ref, h1_ref, g_ref, dout_ref, dh1_ref, dg2_ref):
        @pl.when(pl.program_id(0) == 0)
        def _():
            dg2_ref[...] = jnp.zeros_like(dg2_ref)

        dx, dg_rows = _rms_bwd(h1_ref[...], g_ref[...], dv_ref[...])
        dg2_ref[...] += _rowsum(dg_rows)
        dh1_ref[...] = dout_ref[...] + dx

    frow = lambda: pl.BlockSpec((tf, D), lambda i: (i, 0))
    dh1, dg2 = pl.pallas_call(
        fin_body, name="mlp_bwd_dh1", grid=(t_n // tf,), in_specs=[frow(), frow(), _full((1, D)), frow()],
        out_specs=[frow(), _full((1, D))],
        out_shape=[jax.ShapeDtypeStruct((t_n, D), F32), jax.ShapeDtypeStruct((1, D), F32)],
        compiler_params=_cp("arbitrary"),
    )(dv, h1, g2, dout)
    return dh1, dhp, hid, dg2


def _wgrad(a, g, name):
    t_n, k_n = a.shape
    n_n = g.shape[1]
    tt = min(t_n, 1024)
    tk, tn = min(k_n, 1024), min(n_n, 1024)

    n_t = t_n // tt

    def body(a_ref, g_ref, o_ref, acc):
        t = pl.program_id(2)

        @pl.when(t == 0)
        def _():
            acc[...] = jnp.zeros_like(acc)

        acc[...] += lax.dot_general(a_ref[...], g_ref[...], (((0,), (0,)), ((), ())), preferred_element_type=F32)

        @pl.when(t == n_t - 1)
        def _():
            o_ref[...] = acc[...].astype(o_ref.dtype)

    return pl.pallas_call(
        body, name=name, grid=(k_n // tk, n_n // tn, n_t),
        in_specs=[pl.BlockSpec((tt, tk), lambda k, n, t: (t, k)), pl.BlockSpec((tt, tn), lambda k, n, t: (t, n))],
        out_specs=pl.BlockSpec((tk, tn), lambda k, n, t: (k, n)),
        out_shape=jax.ShapeDtypeStruct((k_n, n_n), MXU),
        scratch_shapes=[pltpu.VMEM((tk, tn), F32)],
        compiler_params=_cp("parallel", "parallel", "arbitrary"),
    )(a, g)


def _wgrad_segs(segs, g, name):
    t_n, n_n = g.shape
    n_s = len(segs)
    tt = min(t_n, 1024)
    n_t = t_n // tt

    def body(*refs):
        a_refs = refs[:n_s]
        g_ref, o_ref, acc = refs[n_s:]
        s_id, t = pl.program_id(0), pl.program_id(1)

        @pl.when(t == 0)
        def _():
            acc[...] = jnp.zeros_like(acc)

        for s in range(n_s):
            @pl.when(s_id == s)
            def _(s=s):
                acc[...] += lax.dot_general(a_refs[s][...], g_ref[...], (((0,), (0,)), ((), ())),
                                            preferred_element_type=F32)

        @pl.when(t == n_t - 1)
        def _():
            o_ref[...] = acc[...].astype(o_ref.dtype)

    seg_spec = lambda s: pl.BlockSpec((tt, SEG), lambda i, t: (jnp.where(i == s, t, jnp.where(i < s, 0, n_t - 1)), 0))
    return pl.pallas_call(
        body, name=name, grid=(n_s, n_t),
        in_specs=[seg_spec(s) for s in range(n_s)] + [pl.BlockSpec((tt, n_n), lambda i, t: (t, 0))],
        out_specs=pl.BlockSpec((SEG, n_n), lambda i, t: (i, 0)),
        out_shape=jax.ShapeDtypeStruct((n_s * SEG, n_n), MXU),
        scratch_shapes=[pltpu.VMEM((SEG, n_n), F32)],
        compiler_params=_cp("arbitrary", "arbitrary"),
    )(*segs, g)


def _inproj_bwd(dsegs, w, x, g0, dh1):
    t_n = x.shape[0]
    tm = min(t_n, 1024)
    n_k = NP // SEG

    def mm_body(*refs):
        dp_refs = refs[:n_k]
        w_ref, du_ref = refs[n_k:]
        k = pl.program_id(1)

        @pl.when(k == 0)
        def _():
            du_ref[...] = jnp.zeros_like(du_ref)

        for s in range(n_k):
            @pl.when(k == s)
            def _(s=s):
                du_ref[...] += jnp.dot(dp_refs[s][...], w_ref[...], preferred_element_type=F32)

    du = pl.pallas_call(
        mm_body, name="inproj_bwd", grid=(t_n // tm, n_k),
        in_specs=[pl.BlockSpec((tm, SEG), functools.partial(
            lambda i, k, s: (jnp.where(k >= s, i, jnp.maximum(i - 1, 0)), 0), s=s)) for s in range(n_k)] + [
            pl.BlockSpec((SEG, D), lambda i, k: (k, 0))],
        out_specs=pl.BlockSpec((tm, D), lambda i, k: (i, 0)),
        out_shape=jax.ShapeDtypeStruct((t_n, D), F32),
        compiler_params=_cp("parallel", "arbitrary"),
    )(*dsegs, w)

    tf = min(t_n, 512)

    def fin_body(du_ref, x_ref, g_ref, dh1_ref, dx_ref, dg0_ref):
        @pl.when(pl.program_id(0) == 0)
        def _():
            dg0_ref[...] = jnp.zeros_like(dg0_ref)

        dx, dg_rows = _rms_bwd(x_ref[...], g_ref[...], du_ref[...])
        dg0_ref[...] += _rowsum(dg_rows)
        dx_ref[...] = dh1_ref[...] + dx

    row = lambda: pl.BlockSpec((tf, D), lambda i: (i, 0))
    return pl.pallas_call(
        fin_body, name="grad_x", grid=(t_n // tf,), in_specs=[row(), row(), _full((1, D)), row()],
        out_specs=[row(), _full((1, D))],
        out_shape=[jax.ShapeDtypeStruct((t_n, D), F32), jax.ShapeDtypeStruct((1, D), F32)],
        compiler_params=_cp("arbitrary"),
    )(du, x, g0, dh1)


def _blockdiag4(w):
    w4 = w.reshape(4, 4, 64, 1, 64).astype(MXU)
    same = (jnp.arange(4)[:, None, None, None] == jnp.arange(4)[None, None, :, None])
    return jnp.where(same[None], w4, jnp.zeros((), MXU)).reshape(4, 256, 256)


def _blockdiag4_extract(g):
    g5 = g.reshape(4, 4, 64, 4, 64)
    return jnp.stack([g5[:, q, :, q, :] for q in range(4)], axis=1).reshape(NH, 64, 64)


def _local_step(x, tgt, p, after_ssd=None, late_weights=None, send_mlp_grads=None, send_late_grads=None):
    f = lambda a: a.astype(F32)
    proj, u = _inproj(x, p["norm_mix_pre"], p["w_in_pT"])
    ssm_params = (p["cw_xs"], p["cw_bc"], p["cb_xs"], p["cb_bc"], p["dt_bias"], p["a_log"], p["d_skip_x"],
                  p["ssm_norm"])
    ya, sprev, yraw = _ssd_fwd(proj, *ssm_params)
    cb_lru = p["conv_lru_b"] if after_ssd is None else p["conv_lru_b"] + after_ssd(ya)
    lru_params = (p["conv_lru_w"], cb_lru, p["wa_bd"], p["wx_bd"], p["lru_ba"], p["lru_bx"], p["lru_lambda"])
    yb, h = _lru_fwd(proj, *lru_params)
    if late_weights is not None:
        p = dict(p, **late_weights(yb))
    h1, mix, merged = _merge_out(ya, yb, proj, x, p["w_out"], p["norm_mix_post"])
    hp, v, dout, dff, loss, dg3 = _mlp_fwd(h1, p["norm_mlp_pre"], p["w_upT"], p["w_down"], p["norm_mlp_post"], tgt)

    dh1, dhp, hid, dg2 = _mlp_bwd(dff, dout, hp, h1, p["w_upT"], p["w_down"], p["norm_mlp_pre"])
    dmix, dya, dyb, dga, dgb, dg1 = _out_bwd(dh1, mix, ya, yb, proj, p["w_out"], p["norm_mix_post"])
    d_w_out = _wgrad(merged, dmix, "wgrad_out")
    (dz, dxs, dbc, dcwx, dcwb, dcbx, dcbb, ddtb, dalog, ddsk, dnrm, d_w_down, d_w_up_t) = _ssd_bwd(
        dya, yraw, proj, sprev, *ssm_params, (hid, dff, dhp, v))
    if send_mlp_grads is not None:
        lru_params = (lru_params[0], lru_params[1] + send_mlp_grads(d_w_up_t, d_w_down, d_w_out)) + lru_params[2:]
    (dgl, dxl, dcwl, dcbl, dwa, dwx, dba, dbx, dlam) = _lru_bwd(dyb, proj, h, *lru_params)
    dsegs = [dz, dgl, dxl, dga, dgb, dxs, dbc]
    d_w_in_pt = _wgrad_segs(dsegs, u, "wgrad_in")
    g0 = p["norm_mix_pre"]
    if send_late_grads is not None:
        g0 = g0 + send_late_grads(d_w_in_pt)
    grad_x, dg0 = _inproj_bwd(dsegs, p["w_in_pT"], x, g0, dh1)
    grads = dict(
        norm_mix_pre=dg0, w_in_pT=d_w_in_pt, conv_ssm_w=jnp.concatenate([dcwx, dcwb], axis=1),
        conv_ssm_b=jnp.concatenate([dcbx, dcbb], axis=1), dt_bias=ddtb[:, :NH], a_log=dalog[:, :NH],
        d_skip=f(ddsk).reshape(NH, 64).sum(axis=1)[None, :], ssm_norm=dnrm, conv_lru_w=dcwl, conv_lru_b=dcbl,
        lru_wa=_blockdiag4_extract(dwa), lru_ba=dba, lru_wx=_blockdiag4_extract(dwx), lru_bx=dbx, lru_lambda=dlam,
        w_out=d_w_out, norm_mix_post=dg1, norm_mlp_pre=dg2, w_upT=d_w_up_t, w_down=d_w_down, norm_mlp_post=dg3)
    return loss[0, 0], grad_x, grads


W_IN_COLS = 6672


def _w_in_t_to_padded(wt):
    z, xs, bc, dt = wt[0:1024], wt[1024:2048], wt[2048:2560], wt[2560:2576]
    gl, xl, ga, gb = wt[2576:3600], wt[3600:4624], wt[4624:5648], wt[5648:6672]
    return jnp.concatenate([z, gl, xl, ga, gb, xs, bc, dt, jnp.zeros((NP - 6672, wt.shape[1]), wt.dtype)], axis=0)


def _w_in_t_from_padded(wp):
    z, gl, xl, ga, gb = (wp[SEG * s:SEG * (s + 1)] for s in range(5))
    xs, bc, dt = wp[5120:6144], wp[6144:6656], wp[6656:6672]
    return jnp.concatenate([z, xs, bc, dt, gl, xl, ga, gb], axis=0)


def _prep_params(full, big):
    f = lambda a: a.astype(F32)
    pad128 = lambda a: jnp.pad(f(a).reshape(1, -1), ((0, 0), (0, 128 - a.size)))
    cw = f(full["conv_ssm_w"])
    cb = f(full["conv_ssm_b"]).reshape(1, -1)
    return dict(
        big, norm_mix_pre=f(full["norm_mix_pre"]).reshape(1, D),
        cw_xs=cw[:, :D], cw_bc=cw[:, D:], cb_xs=cb[:, :D], cb_bc=cb[:, D:],
        dt_bias=pad128(full["dt_bias"]), a_log=pad128(full["a_log"]),
        d_skip_x=jnp.repeat(f(full["d_skip"]).reshape(-1), 64).reshape(1, D), ssm_norm=f(full["ssm_norm"]).reshape(1, D),
        conv_lru_w=f(full["conv_lru_w"]), conv_lru_b=f(full["conv_lru_b"]).reshape(1, D),
        wa_bd=_blockdiag4(full["lru_wa"]), wx_bd=_blockdiag4(full["lru_wx"]),
        lru_ba=f(full["lru_ba"]).reshape(1, D), lru_bx=f(full["lru_bx"]).reshape(1, D),
        lru_lambda=f(full["lru_lambda"]).reshape(1, D),
        norm_mix_post=f(full["norm_mix_post"]).reshape(1, D), norm_mlp_pre=f(full["norm_mlp_pre"]).reshape(1, D),
        norm_mlp_post=f(full["norm_mlp_post"]).reshape(1, D))


MESH_ID = pl.DeviceIdType.MESH
ANY = pl.BlockSpec(memory_space=pl.ANY)


def _my_place():
    x, y, c = lax.axis_index("x"), lax.axis_index("y"), lax.axis_index("c")
    return x, y, c, 4 * x + 2 * y + c


def _peer(x, y, c, k):
    return (x ^ ((k >> 2) & 1), y ^ ((k >> 1) & 1), c ^ (k & 1))


def _all_gather(pack, name):
    def body(in_ref, out_ref, send_sems, recv_sems, local_sem):
        x, y, c, me = _my_place()
        sibling = (x, y, 1 - c)
        flips = (4, 2, 6)

        def copy(j, block, to, src=None):
            return pltpu.make_async_remote_copy(
                src_ref=out_ref.at[block] if src is None else src, dst_ref=out_ref.at[block],
                send_sem=send_sems.at[j], recv_sem=recv_sems.at[j], device_id=to, device_id_type=MESH_ID)

        mine = pltpu.make_async_copy(in_ref, out_ref.at[me], local_sem)
        mine.start()
        first = [copy(0, me, sibling, src=in_ref)]
        first += [copy(1 + j, me, _peer(x, y, c, k), src=in_ref) for j, k in enumerate(flips)]
        for cp in first:
            cp.start()
        passed = [copy(4 + j, me ^ k, sibling) for j, k in enumerate(flips)]
        for j, k in enumerate(flips):
            copy(1 + j, me ^ k, (x, y, c)).wait_recv()
            passed[j].start()
        copy(0, me ^ 1, (x, y, c)).wait_recv()
        for j, k in enumerate(flips):
            copy(4 + j, me ^ k ^ 1, (x, y, c)).wait_recv()
        for cp in first + passed:
            cp.wait_send()
        mine.wait()

    return pl.pallas_call(
        body, name=name, in_specs=[ANY], out_specs=ANY,
        out_shape=jax.ShapeDtypeStruct((N_DEV,) + pack.shape, pack.dtype),
        scratch_shapes=[pltpu.SemaphoreType.DMA((N_DEV - 1,)), pltpu.SemaphoreType.DMA((N_DEV - 1,)),
                        pltpu.SemaphoreType.DMA],
    )(pack)


HBM = pl.BlockSpec(memory_space=pltpu.HBM)
SEM = pl.BlockSpec(memory_space=pltpu.SEMAPHORE)
PLAN_GATHER = tuple((k, "pack", 0) for k in range(1, N_DEV))
PLAN_SCATTER = tuple((k, "slot", 0) for k in range(1, N_DEV))
PLAN_GATHER_ICI = tuple((k, "pack", 0) for k in (2, 4, 6))
PLAN_GATHER_D2D = ((1, "pack", 0),) + tuple((1, s, s) for s in (2, 4, 6))


def _plan_copy(j, plan, src_ref, land_ref, sems):
    k, source, slot = plan[j]
    x, y, c, me = _my_place()
    if source == "pack":
        src = src_ref
    elif source == "slot":
        src = src_ref.at[me ^ k]
    else:
        src = land_ref.at[me ^ source]
    return pltpu.make_async_remote_copy(
        src_ref=src, dst_ref=land_ref.at[me ^ slot], send_sem=sems[j], recv_sem=sems[len(plan) + j],
        device_id=_peer(x, y, c, k), device_id_type=MESH_ID)


def _exchange_start(src, land, plan, name):
    n_c = len(plan)
    if land is None:
        land = pltpu.with_memory_space_constraint(lax.empty((N_DEV,) + src.shape[-2:], src.dtype), pltpu.HBM)

    def body(src_ref, land_ref, *rest):
        sems, token = rest[:2 * n_c], rest[2 * n_c + 2]
        for j in range(n_c):
            _plan_copy(j, plan, src_ref, land_ref, sems).start()
        token[...] = jnp.zeros_like(token)

    outs = pl.pallas_call(
        body, name=name,
        out_shape=(pltpu.SemaphoreType.DMA(()),) * (2 * n_c) + (
            pltpu.HBM(src.shape, src.dtype), pltpu.HBM(land.shape, land.dtype), jax.ShapeDtypeStruct((8, 128), F32)),
        in_specs=(HBM, HBM), out_specs=(SEM,) * (2 * n_c) + (HBM, HBM, pl.BlockSpec(memory_space=pltpu.VMEM)),
        input_output_aliases={0: 2 * n_c, 1: 2 * n_c + 1},
        compiler_params=pltpu.CompilerParams(has_side_effects=pltpu.SideEffectType.DATAFLOW_SIDE_EFFECTING),
    )(pltpu.with_memory_space_constraint(src, pltpu.HBM), land)
    return outs[:2 * n_c], outs[2 * n_c], outs[2 * n_c + 1], outs[2 * n_c + 2]


def _exchange_wait(sems, src_thru, land_thru, after, plan, name):
    n_c = len(plan)

    def body(src_ref, land_ref, *rest):
        for j in range(n_c):
            cp = _plan_copy(j, plan, src_ref, land_ref, rest[:2 * n_c])
            cp.wait_send()
            cp.wait_recv()

    return pl.pallas_call(
        body, name=name,
        out_shape=(pltpu.HBM(src_thru.shape, src_thru.dtype), pltpu.HBM(land_thru.shape, land_thru.dtype)),
        in_specs=(HBM, HBM) + (SEM,) * (2 * n_c) + (pl.BlockSpec(memory_space=pl.ANY),), out_specs=(HBM, HBM),
        input_output_aliases={0: 0, 1: 1},
        compiler_params=pltpu.CompilerParams(has_side_effects=pltpu.SideEffectType.DATAFLOW_SIDE_EFFECTING),
    )(src_thru, land_thru, *sems, after)


def _slot_sum(parts, name):
    r_n, c_n = parts.shape[1:]
    tr = max(t for t in range(16, 513, 16) if r_n % t == 0)

    def body(p_ref, o_ref):
        acc = p_ref[0].astype(F32)
        for k in range(1, N_DEV):
            acc = acc + p_ref[k].astype(F32)
        o_ref[...] = acc

    return pl.pallas_call(
        body, name=name, grid=(r_n // tr,),
        in_specs=[pl.BlockSpec((N_DEV, tr, c_n), lambda i: (0, i, 0))],
        out_specs=pl.BlockSpec((tr, c_n), lambda i: (i, 0)),
        out_shape=jax.ShapeDtypeStruct((r_n, c_n), F32),
        compiler_params=_cp("parallel"),
    )(parts)


def _adam_math(w, g, m, v):
    m = ADAM_B1 * m + (1.0 - ADAM_B1) * g
    v = ADAM_B2 * v + (1.0 - ADAM_B2) * jnp.square(g)
    m_hat = m / (1.0 - ADAM_B1 ** ADAM_STEP)
    v_hat = v / (1.0 - ADAM_B2 ** ADAM_STEP)
    return -ADAM_LR * (m_hat / (jnp.sqrt(v_hat) + ADAM_EPS) + ADAM_WD * w), m, v


def _adam_big(w, g, m, v, name):
    def body(w_ref, g_ref, m_ref, v_ref, d_ref, mo_ref, vo_ref):
        d_ref[...], mo_ref[...], vo_ref[...] = _adam_math(w_ref[...], g_ref[...], m_ref[...], v_ref[...])

    if w.ndim == 3:
        _, r_n, c_n = w.shape
        tr = min(r_n, 256)
        grid = (r_n // tr,)
        blk = lambda: pl.BlockSpec((1, tr, c_n), lambda i: (0, i, 0))
    else:
        r_n, c_n = w.shape
        tc = min(c_n, 256)
        grid = (c_n // tc,)
        blk = lambda: pl.BlockSpec((r_n, tc), lambda i: (0, i))
    return pl.pallas_call(
        body, name=name, grid=grid, in_specs=[blk(), blk(), blk(), blk()], out_specs=[blk(), blk(), blk()],
        out_shape=[jax.ShapeDtypeStruct(w.shape, F32)] * 3, compiler_params=_cp("parallel"),
    )(w, g, m, v)


def _adam_small(groups, where, wmv):
    n, n_g = len(wmv), len(groups)

    def body(*refs):
        g_refs = refs[:n_g]
        w_refs = refs[n_g:n_g + 3 * n]
        o_refs = refs[n_g + 3 * n:]
        for q in range(n):
            w_ref, m_ref, v_ref = w_refs[3 * q:3 * q + 3]
            r, c = w_ref.shape
            gi, r0 = where[q]
            g = g_refs[gi][0, r0:r0 + r, 0:c]
            for k in range(1, N_DEV):
                g = g + g_refs[gi][k, r0:r0 + r, 0:c]
            d, m, v = _adam_math(w_ref[...], g, m_ref[...], v_ref[...])
            o_refs[4 * q][...] = g
            o_refs[4 * q + 1][...] = d
            o_refs[4 * q + 2][...] = m
            o_refs[4 * q + 3][...] = v

    flat_wmv = [a for t in wmv for a in t]
    vm = pl.BlockSpec(memory_space=pltpu.VMEM)
    outs = pl.pallas_call(
        body, name="adam_small", in_specs=[vm] * (n_g + 3 * n), out_specs=[vm] * (4 * n),
        out_shape=[jax.ShapeDtypeStruct(t[0].shape, F32) for t in wmv for _ in range(4)],
        compiler_params=pltpu.CompilerParams(vmem_limit_bytes=VMEM_LIMIT),
    )(*groups, *flat_wmv)
    return [tuple(outs[4 * q:4 * q + 4]) for q in range(n)]


WEIGHTS = ["norm_mix_pre", "w_in", "conv_ssm_w", "conv_ssm_b", "dt_bias", "a_log", "d_skip", "ssm_norm", "conv_lru_w",
           "conv_lru_b", "lru_wa", "lru_ba", "lru_wx", "lru_bx", "lru_lambda", "w_out", "norm_mix_post", "norm_mlp_pre",
           "w_up", "w_down", "norm_mlp_post"]
BIG = ["w_out", "w_up", "w_down", "w_in"]
IN_ROWS = W_IN_COLS // N_DEV
IN_PAD, EARLY_ROWS = 848, 880
ROW_UP, ROW_DOWN, LATE_ROWS = 128, 640, 1152
GRAD_LATE_ROWS = 864
CONV_SSM_COLS, CONV_LRU_COLS = 1536 // N_DEV, D // N_DEV
SMALL = [("norm_mix_pre", (1, D), 0, 0), ("ssm_norm", (1, D), 0, 1), ("conv_lru_b", (1, D), 0, 2),
         ("lru_lambda", (1, D), 0, 3), ("norm_mix_post", (1, D), 0, 4), ("norm_mlp_pre", (1, D), 0, 5),
         ("norm_mlp_post", (1, D), 0, 6), ("conv_ssm_b", (1, 1536), 1, 0), ("dt_bias", (1, NH), 2, 0),
         ("a_log", (1, NH), 2, 1), ("d_skip", (1, NH), 2, 2), ("conv_ssm_w", (4, CONV_SSM_COLS), 3, 0),
         ("conv_lru_w", (4, CONV_LRU_COLS), 4, 0), ("lru_wa", (D, 64), 5, 0), ("lru_wx", (D, 64), 5, D),
         ("lru_ba", (NH, 64), 6, 0), ("lru_bx", (NH, 64), 6, NH)]
SMALL_GROUPS = [(8, D), (1, 1536), (8, 128), (4, 1536), (4, D), (2 * D, 64), (2 * NH, 64)]


def _pad_rows(flat, mult):
    n = flat.shape[0]
    rows = -(-n // (128 * mult)) * mult
    return jnp.pad(flat, (0, rows * 128 - n)).reshape(rows, 128)


def _split3(a):
    hi = a.astype(MXU)
    r1 = a - hi.astype(F32)
    mid = r1.astype(MXU)
    lo = (r1 - mid.astype(F32)).astype(MXU)
    return jnp.stack([hi, mid, lo])


def _early_pack(a, me):
    bf = lambda t: t.astype(MXU)
    conv = lambda t, c: jnp.pad(_split3(t).reshape(12, c), ((0, 4), (0, D - c)))
    shifted = lax.dynamic_update_slice(jnp.zeros((IN_PAD, D), MXU), bf(a["w_in"][0]).T, (2 * me, 0))
    return jnp.concatenate([shifted, conv(a["conv_ssm_w"][0], CONV_SSM_COLS), conv(a["conv_lru_w"][0], CONV_LRU_COLS)],
                           axis=0)


TILE = 16
SHARD_TILES = IN_PAD // TILE
SHARD_STEP = (IN_ROWS // TILE)
SEG_TILES = ((0, 64, 0), (64, 128, 320), (128, 160, 384), (160, 161, 416), (161, 225, 64), (225, 289, 128),
             (289, 353, 192), (353, 417, 256))


def _tile_runs(lo, hi):
    runs = []
    for s0, s1, d0 in SEG_TILES:
        a, b = max(lo, s0), min(hi, s1)
        if a < b:
            runs.append((a, b - a, d0 + a - s0))
    return runs


def _assemble_w_in(g):
    whole = []
    for k in range(N_DEV):
        lo = SHARD_STEP * k + (1 if k else 0)
        hi = SHARD_STEP * (k + 1) + (1 if k == N_DEV - 1 else 0)
        whole += [(k, a - SHARD_STEP * k, n, d) for a, n, d in _tile_runs(lo, hi)]
    split = [(k, _tile_runs(SHARD_STEP * k, SHARD_STEP * k + 1)[0][2]) for k in range(1, N_DEV)]
    n_w, n_s = len(whole), len(split)

    def body(g_ref, o_ref, pad_ref, lo_ref, hi_ref, sum_ref, sems):
        rows = lambda t, n=1: pl.ds(TILE * t, TILE * n)
        pad_ref[...] = jnp.zeros_like(pad_ref)
        cps = [pltpu.make_async_copy(g_ref.at[k, rows(t, n)], o_ref.at[rows(d, n)], sems.at[j])
               for j, (k, t, n, d) in enumerate(whole)]
        cps.append(pltpu.make_async_copy(pad_ref, o_ref.at[pl.ds(W_IN_COLS, NP - W_IN_COLS)], sems.at[n_w]))
        loads = []
        for j, (k, _) in enumerate(split):
            loads.append(pltpu.make_async_copy(g_ref.at[k - 1, rows(SHARD_STEP)], lo_ref.at[j], sems.at[n_w + 1 + 2 * j]))
            loads.append(pltpu.make_async_copy(g_ref.at[k, rows(0)], hi_ref.at[j], sems.at[n_w + 2 + 2 * j]))
        for cp in cps + loads:
            cp.start()
        for cp in loads:
            cp.wait()
        sum_ref[...] = lo_ref[...] + hi_ref[...]
        stores = [pltpu.make_async_copy(sum_ref.at[j], o_ref.at[rows(d)], sems.at[n_w + 1 + 2 * n_s + j])
                  for j, (_, d) in enumerate(split)]
        for cp in stores:
            cp.start()
        for cp in cps + stores:
            cp.wait()

    return pl.pallas_call(
        body, name="assemble_w_in", in_specs=[ANY], out_specs=ANY, out_shape=jax.ShapeDtypeStruct((NP, D), g.dtype),
        scratch_shapes=[pltpu.VMEM((NP - W_IN_COLS, D), g.dtype)] + [pltpu.VMEM((n_s, TILE, D), g.dtype)] * 3
        + [pltpu.SemaphoreType.DMA((n_w + 1 + 3 * n_s,))],
    )(g)


def _scatter_w_in_grad(dw):
    runs = [(k, a - SHARD_STEP * k, n, d) for k in range(N_DEV)
            for a, n, d in _tile_runs(SHARD_STEP * k, SHARD_STEP * k + SHARD_TILES)]
    pad = GRAD_LATE_ROWS - IN_PAD

    def body(dw_ref, o_ref, pad_ref, sems):
        rows = lambda t, n: pl.ds(TILE * t, TILE * n)
        pad_ref[...] = jnp.zeros_like(pad_ref)
        cps = [pltpu.make_async_copy(dw_ref.at[rows(d, n)], o_ref.at[k, rows(t, n)], sems.at[j])
               for j, (k, t, n, d) in enumerate(runs)]
        cps += [pltpu.make_async_copy(pad_ref, o_ref.at[k, pl.ds(IN_PAD, pad)], sems.at[len(runs) + k])
                for k in range(N_DEV)]
        for cp in cps:
            cp.start()
        for cp in cps:
            cp.wait()

    return pl.pallas_call(
        body, name="scatter_w_in_grad", in_specs=[ANY], out_specs=ANY,
        out_shape=jax.ShapeDtypeStruct((N_DEV, GRAD_LATE_ROWS, D), dw.dtype),
        scratch_shapes=[pltpu.VMEM((pad, D), dw.dtype), pltpu.SemaphoreType.DMA((len(runs) + N_DEV,))],
    )(dw)


def _early_unpack(g):
    w_in_pt = _assemble_w_in(g)
    conv = {}
    for n, r0, c in (("conv_ssm_w", IN_PAD, CONV_SSM_COLS), ("conv_lru_w", IN_PAD + 16, CONV_LRU_COLS)):
        s = g[:, r0:r0 + 12, :c].astype(F32).reshape(N_DEV, 3, 4, c)
        conv[n] = ((s[:, 0] + s[:, 1]) + s[:, 2]).transpose(1, 0, 2).reshape(4, N_DEV * c)
    return w_in_pt, conv


def _late_pack(a):
    bf = lambda t: t.astype(MXU)
    return jnp.concatenate([bf(a["w_out"][0]), bf(a["w_up"][0]).T, bf(a["w_down"][0])], axis=0)


def _late_unpack(g):
    return dict(w_out=g[:, :ROW_UP].reshape(D, D), w_upT=g[:, ROW_UP:ROW_DOWN].reshape(FF, D),
                w_down=g[:, ROW_DOWN:].reshape(FF, D))


def _own_slot(land, own):
    me = 4 * lax.axis_index("x") + 2 * lax.axis_index("y") + lax.axis_index("c")
    return lax.dynamic_update_slice_in_dim(land, own[None], me, axis=0)


def kernel(x, norm_mix_pre, w_in, conv_ssm_w, conv_ssm_b, dt_bias, a_log, d_skip, ssm_norm, conv_lru_w, conv_lru_b, lru_wa, lru_ba, lru_wx, lru_bx, lru_lambda, w_out, norm_mix_post, norm_mlp_pre, w_up, w_down, norm_mlp_post, loss_target, m_norm_mix_pre, m_w_in, m_conv_ssm_w, m_conv_ssm_b, m_dt_bias, m_a_log, m_d_skip, m_ssm_norm, m_conv_lru_w, m_conv_lru_b, m_lru_wa, m_lru_ba, m_lru_wx, m_lru_bx, m_lru_lambda, m_w_out, m_norm_mix_post, m_norm_mlp_pre, m_w_up, m_w_down, m_norm_mlp_post, v_norm_mix_pre, v_w_in, v_conv_ssm_w, v_conv_ssm_b, v_dt_bias, v_a_log, v_d_skip, v_ssm_norm, v_conv_lru_w, v_conv_lru_b, v_lru_wa, v_lru_ba, v_lru_wx, v_lru_bx, v_lru_lambda, v_w_out, v_norm_mix_post, v_norm_mlp_pre, v_w_up, v_w_down, v_norm_mlp_post):
    vals = (norm_mix_pre, w_in, conv_ssm_w, conv_ssm_b, dt_bias, a_log, d_skip, ssm_norm, conv_lru_w, conv_lru_b, lru_wa, lru_ba, lru_wx, lru_bx, lru_lambda, w_out, norm_mix_post, norm_mlp_pre, w_up, w_down, norm_mlp_post)
    m_vals = (m_norm_mix_pre, m_w_in, m_conv_ssm_w, m_conv_ssm_b, m_dt_bias, m_a_log, m_d_skip, m_ssm_norm, m_conv_lru_w, m_conv_lru_b, m_lru_wa, m_lru_ba, m_lru_wx, m_lru_bx, m_lru_lambda, m_w_out, m_norm_mix_post, m_norm_mlp_pre, m_w_up, m_w_down, m_norm_mlp_post)
    v_vals = (v_norm_mix_pre, v_w_in, v_conv_ssm_w, v_conv_ssm_b, v_dt_bias, v_a_log, v_d_skip, v_ssm_norm, v_conv_lru_w, v_conv_lru_b, v_lru_wa, v_lru_ba, v_lru_wx, v_lru_bx, v_lru_lambda, v_w_out, v_norm_mix_post, v_norm_mlp_pre, v_w_up, v_w_down, v_norm_mlp_post)
    w = dict(zip(WEIGHTS, vals))
    m = dict(zip(WEIGHTS, m_vals))
    v = dict(zip(WEIGHTS, v_vals))
    me = 4 * lax.axis_index("x") + 2 * lax.axis_index("y") + lax.axis_index("c")

    bf = lambda t: t.astype(MXU)
    late = _late_pack(w)
    early = _all_gather(_early_pack(w, me), "early_weights_all_gather")
    late, early = lax.optimization_barrier((late, early))
    lw = {}
    lw["sems"], lw["src"], lw["land"], token = _exchange_start(late, None, PLAN_GATHER_ICI, "late_weights_ici_start")
    w_in_pt, conv_w = _early_unpack(early)
    full = {n: (conv_w[n] if n in conv_w else w[n][0]) for n in WEIGHTS if n not in BIG}
    full["norm_mix_pre"] = full["norm_mix_pre"] + token[0, 0]

    def after_ssd(after):
        src, land = _exchange_wait(lw["sems"], lw["src"], lw["land"], after, PLAN_GATHER_ICI, "late_weights_ici_wait")
        lw["sems"], lw["src"], lw["land"], tok = _exchange_start(src, land, PLAN_GATHER_D2D, "late_weights_d2d_start")
        return tok[0, 0]

    def late_weights(after):
        src, land = _exchange_wait(lw["sems"], lw["src"], lw["land"], after, PLAN_GATHER_D2D, "late_weights_d2d_wait")
        return _late_unpack(_own_slot(land, src))

    sent = {}

    def send_mlp_grads(d_w_up_t, d_w_down, d_w_out):
        src = jnp.concatenate([bf(d_w_up_t).reshape(N_DEV, -1, D), bf(d_w_down).reshape(N_DEV, -1, D),
                               bf(d_w_out).reshape(N_DEV, -1, D)], axis=1)
        sent["sems"], sent["src"], sent["land"], tok = _exchange_start(src, None, PLAN_SCATTER, "mlp_grads_start")
        return tok[0, 0]

    def send_late_grads(d_w_in_pt):
        src = _scatter_w_in_grad(d_w_in_pt)
        sent["sems2"], sent["src2"], sent["land2"], tok = _exchange_start(src, None, PLAN_SCATTER, "late_grads_start")
        return tok[0, 0]

    loss, grad_x, g = _local_step(x[0], loss_target[0], _prep_params(full, dict(w_in_pT=w_in_pt)), after_ssd,
                                  late_weights, send_mlp_grads, send_late_grads)
    loss = lax.psum(loss, ("x", "y", "c"))

    own = lambda src: lax.dynamic_index_in_dim(src, me, keepdims=False)
    out_g, out_d, out_m, out_v = {}, {}, {}, {}
    mlp_src, mlp_land = _exchange_wait(sent["sems"], sent["src"], sent["land"], grad_x, PLAN_SCATTER, "mlp_grads_wait")
    g_mlp = _slot_sum(_own_slot(mlp_land, own(mlp_src)), "slot_sum_mlp")
    fs = FF // N_DEV
    for n, gn in (("w_up", g_mlp[:fs].T[None]), ("w_down", g_mlp[fs:2 * fs][None]), ("w_out", g_mlp[2 * fs:][None])):
        out_g[n] = gn
        out_d[n], out_m[n], out_v[n] = _adam_big(w[n], gn, m[n], v[n], "adam_" + n)
    zrow = jnp.zeros((1, D), F32)
    pad16 = lambda a: jnp.pad(a, ((0, 0), (0, 128 - NH)))
    small_parts = [
        jnp.concatenate([g["norm_mix_pre"], g["ssm_norm"], g["conv_lru_b"], g["lru_lambda"], g["norm_mix_post"],
                         g["norm_mlp_pre"], g["norm_mlp_post"], zrow], axis=0),
        g["conv_ssm_b"],
        jnp.concatenate([pad16(g["dt_bias"]), pad16(g["a_log"]), pad16(g["d_skip"]), jnp.zeros((5, 128), F32)], axis=0),
        g["conv_ssm_w"], g["conv_lru_w"],
        jnp.concatenate([g["lru_wa"].reshape(D, 64), g["lru_wx"].reshape(D, 64)], axis=0),
        jnp.concatenate([g["lru_ba"].reshape(NH, 64), g["lru_bx"].reshape(NH, 64)], axis=0)]
    small = _pad_rows(jnp.concatenate([s.reshape(-1) for s in small_parts]), 8)
    small, _ = lax.optimization_barrier((small, out_v["w_out"]))
    small_all = _all_gather(small, "small_grads_all_gather")

    lg_src, lg_land = _exchange_wait(sent["sems2"], sent["src2"], sent["land2"], small_all, PLAN_SCATTER, "late_grads_wait")
    g_late = _slot_sum(_own_slot(lg_land, own(lg_src)), "slot_sum_late")
    gt = lax.dynamic_slice(g_late, (2 * me, 0), (IN_ROWS, D))
    dt_, mt_, vt_ = _adam_big(w["w_in"][0].T, gt, m["w_in"][0].T, v["w_in"][0].T, "adam_w_in")
    out_g["w_in"], out_d["w_in"], out_m["w_in"], out_v["w_in"] = gt.T[None], dt_.T[None], mt_.T[None], vt_.T[None]
    sflat = small_all.reshape(N_DEV, -1)
    groups = []
    off = 0
    for r, c in SMALL_GROUPS:
        groups.append(sflat[:, off:off + r * c].reshape(N_DEV, r, c))
        off += r * c
    groups[3] = lax.dynamic_slice_in_dim(groups[3], me * CONV_SSM_COLS, CONV_SSM_COLS, axis=2)
    groups[4] = lax.dynamic_slice_in_dim(groups[4], me * CONV_LRU_COLS, CONV_LRU_COLS, axis=2)
    wmv = [(w[n].reshape(s), m[n].reshape(s), v[n].reshape(s)) for n, s, _, _ in SMALL]
    res = _adam_small(groups, [(gi, r0) for _, _, gi, r0 in SMALL], wmv)
    for (n, _, _, _), (g_n, d_n, m_n, v_n) in zip(SMALL, res):
        shape = w[n].shape
        out_g[n], out_d[n], out_m[n], out_v[n] = (g_n.reshape(shape), d_n.reshape(shape), m_n.reshape(shape),
                                                  v_n.reshape(shape))
    return (loss, grad_x[None], *[out_g[n] for n in WEIGHTS], *[out_d[n] for n in WEIGHTS],
            *[out_m[n] for n in WEIGHTS], *[out_v[n] for n in WEIGHTS])
```

```python
import functools

import jax
import jax.numpy as jnp
from jax import lax
from jax.experimental import pallas as pl
from jax.experimental.pallas import tpu as pltpu

F32 = jnp.float32
MXU = jnp.bfloat16
HI = lax.Precision.HIGHEST
EPS = 1e-6

D = 1024
NH = 16
NS = 128
CH = 128
FF = 4096
NP = 7168
SEG = 1024
LRU_C = 8.0
N_DEV = 8

ADAM_LR, ADAM_B1, ADAM_B2, ADAM_EPS, ADAM_WD, ADAM_STEP = 0.001, 0.9, 0.999, 1e-08, 0.01, 10

VMEM_LIMIT = 56 * 1024 * 1024


def _cp(*sem):
    return pltpu.CompilerParams(dimension_semantics=sem, vmem_limit_bytes=VMEM_LIMIT)


def _nn(a, b):
    return jnp.dot(a.astype(MXU), b.astype(MXU), preferred_element_type=F32)


def _nt(a, b):
    return lax.dot_general(a.astype(MXU), b.astype(MXU), (((1,), (1,)), ((), ())), preferred_element_type=F32)


def _tn(a, b):
    return lax.dot_general(a.astype(MXU), b.astype(MXU), (((0,), (0,)), ((), ())), preferred_element_type=F32)


_sigmoid = jax.nn.sigmoid


def _silu(x):
    return x * _sigmoid(x)


def _dsilu(x):
    s = _sigmoid(x)
    return s + x * s * (1.0 - s)


def _softplus(x):
    return jnp.maximum(x, 0.0) + jnp.log(1.0 + jnp.exp(-jnp.abs(x)))


def _rms(x, g):
    r = lax.rsqrt(jnp.mean(x * x, axis=-1, keepdims=True) + EPS)
    return x * r * g


def _rms_bwd(x, g, dy):
    r = lax.rsqrt(jnp.mean(x * x, axis=-1, keepdims=True) + EPS)
    gdy = g * dy
    dx = r * gdy - x * (r * r * r) * jnp.mean(x * gdy, axis=-1, keepdims=True)
    return dx, dy * x * r


def _rowsum(x):
    return jnp.sum(x, axis=0, keepdims=True)


def _taps_past(cur, prev8):
    r_n, c_n = cur.shape
    row = lax.broadcasted_iota(jnp.int32, (r_n, c_n), 0)
    out = []
    for k in range(4):
        s = 3 - k
        if s == 0:
            out.append(cur)
            continue
        head = jnp.concatenate([pltpu.roll(prev8, s, 0), jnp.zeros((r_n - 8, c_n), F32)], axis=0)
        out.append(jnp.where(row < s, head, pltpu.roll(cur, s, 0)))
    return out


def _taps_future(cur, fut8):
    r_n, c_n = cur.shape
    row = lax.broadcasted_iota(jnp.int32, (r_n, c_n), 0)
    out = []
    for k in range(4):
        s = 3 - k
        if s == 0:
            out.append(cur)
            continue
        tail = jnp.concatenate([jnp.zeros((r_n - 8, c_n), F32), pltpu.roll(fut8, 8 - s, 0)], axis=0)
        out.append(jnp.where(row >= r_n - s, tail, pltpu.roll(cur, r_n - s, 0)))
    return out


def _conv_apply(taps, w, b):
    acc = taps[0] * w[0:1, :]
    for k in range(1, 4):
        acc = acc + taps[k] * w[k:k + 1, :]
    return acc + b


def _inproj(x, g0, w):
    t_n = x.shape[0]
    tm = min(t_n, 1024)

    def body(x_ref, g_ref, w_ref, p_ref, u_ref):
        @pl.when(pl.program_id(1) == 0)
        def _():
            u_ref[...] = _rms(x_ref[...], g_ref[...]).astype(MXU)

        p_ref[...] = lax.dot_general(u_ref[...], w_ref[...], (((1,), (1,)), ((), ())), preferred_element_type=F32)

    return pl.pallas_call(
        body, name="inproj", grid=(t_n // tm, NP // SEG),
        in_specs=[pl.BlockSpec((tm, D), lambda i, j: (i, 0)), pl.BlockSpec((1, D), lambda i, j: (0, 0)),
                  pl.BlockSpec((SEG, D), lambda i, j: (j, 0))],
        out_specs=[pl.BlockSpec((tm, SEG), lambda i, j: (i, j)), pl.BlockSpec((tm, D), lambda i, j: (i, 0))],
        out_shape=[jax.ShapeDtypeStruct((t_n, NP), F32), jax.ShapeDtypeStruct((t_n, D), MXU)],
        compiler_params=_cp("parallel", "arbitrary"),
    )(x, g0, w)


def _ssd_prep(dtraw, dtb, alog):
    l_n = dtraw.shape[0]
    r = lax.broadcasted_iota(jnp.int32, (l_n, l_n), 0)
    c = lax.broadcasted_iota(jnp.int32, (l_n, l_n), 1)
    tril = (r >= c).astype(F32)
    triu = (r <= c).astype(F32)
    eye = (r == c).astype(F32)
    dt = _softplus(dtraw + dtb)
    adt = dt * (-jnp.exp(alog))
    ac = jnp.dot(tril, adt, preferred_element_type=F32, precision=HI)
    tn = (((0,), (0,)), ((), ()))
    ac_t = lax.dot_general(adt, triu, tn, preferred_element_type=F32, precision=HI)
    dt_t = lax.dot_general(dt, eye, tn, preferred_element_type=F32, precision=HI)
    return dt, dt_t, ac, ac_t, _rowsum(adt)


def _ssd_pair(j, xp, bg, cg, sp, dt, dt_t, ac, ac_t, aend):
    l_n = xp.shape[0]
    lane = lax.broadcasted_iota(jnp.int32, (l_n, 128), 1)
    sub = lax.broadcasted_iota(jnp.int32, (128, l_n), 0)
    lane1 = lax.broadcasted_iota(jnp.int32, (1, 128), 1)
    tri = lax.broadcasted_iota(jnp.int32, (l_n, l_n), 0) >= lax.broadcasted_iota(jnp.int32, (l_n, l_n), 1)
    lo = lax.broadcasted_iota(jnp.int32, (l_n, 128), 1) < 64
    lo_s = lax.broadcasted_iota(jnp.int32, (128, 128), 1) < 64
    cb = _nt(cg, bg)
    cs = _nn(cg, sp)
    x2 = jnp.concatenate([jnp.where(lo, xp, 0.0), jnp.where(lo, 0.0, xp)], axis=0)
    ws_, bs_, eo, ee = [], [], [], []
    for e in range(2):
        h = 2 * j + e
        ac_l = jnp.sum(jnp.where(lane == h, ac, 0.0), axis=1, keepdims=True)
        dt_l = jnp.sum(jnp.where(lane == h, dt, 0.0), axis=1, keepdims=True)
        a_end = jnp.sum(jnp.where(lane1 == h, aend, 0.0), axis=1, keepdims=True)
        ac_s, dt_s = ac_t[h:h + 1, :], dt_t[h:h + 1, :]
        decay = jnp.exp(jnp.where(tri, ac_l - ac_s, -1e30))
        ws_.append(cb * decay * dt_s)
        bs_.append(bg * (jnp.exp(a_end - ac_l) * dt_l))
        eo.append(jnp.exp(ac_l))
        ee.append(jnp.exp(a_end))
    y = _nn(jnp.concatenate(ws_, axis=1), x2) + jnp.where(lo, eo[0], eo[1]) * cs
    s_new = _tn(jnp.concatenate(bs_, axis=0), x2) + jnp.where(lo_s, ee[0], ee[1]) * sp
    return y, s_new


def _ssd_post(y, xs, z, dsk, nrm):
    y = (y + dsk * xs) * _silu(z)
    half = D // 2
    ya, yb = y[:, :half], y[:, half:]
    ya = ya * lax.rsqrt(jnp.mean(ya * ya, axis=-1, keepdims=True) + EPS)
    yb = yb * lax.rsqrt(jnp.mean(yb * yb, axis=-1, keepdims=True) + EPS)
    return jnp.concatenate([ya, yb], axis=1) * nrm


def _proj_specs(rows, seg_ids, order):
    return [pl.BlockSpec((rows, SEG), functools.partial(lambda i, s: (order(i), s), s=s)) for s in seg_ids]


def _prev8_specs(rows, seg_ids, order):
    rb = rows // 8
    return [pl.BlockSpec((8, SEG), functools.partial(lambda i, s: (jnp.maximum(order(i) * rb - 1, 0), s), s=s))
            for s in seg_ids]


def _full(shape):
    return pl.BlockSpec(shape, lambda i: (0,) * len(shape))


def _ssd_fwd(proj, cwx, cwb, cbx, cbb, dtb, alog, dsk, nrm):
    t_n = proj.shape[0]
    n_c = t_n // CH
    fwd = lambda i: i

    def body(z_ref, xs_ref, bc_ref, xsp_ref, bcp_ref, cwx_ref, cwb_ref, cbx_ref, cbb_ref, dtb_ref, alog_ref,
             dsk_ref, nrm_ref, ya_ref, sprev_ref, yraw_ref, s_ref):
        c = pl.program_id(0)

        @pl.when(c == 0)
        def _():
            s_ref[...] = jnp.zeros_like(s_ref)

        keep = jnp.where(c == 0, 0.0, 1.0)
        xs_pre = _conv_apply(_taps_past(xs_ref[...], xsp_ref[...] * keep), cwx_ref[...], cbx_ref[...])
        bc_pre = _conv_apply(_taps_past(bc_ref[:, :512], bcp_ref[:, :512] * keep), cwb_ref[...], cbb_ref[...])
        prep = _ssd_prep(bc_ref[:, 512:640], dtb_ref[...], alog_ref[...])
        xs = _silu(xs_pre)
        bc = _silu(bc_pre)
        sprev_ref[0] = s_ref[...]
        ys = []
        for j in range(NH // 2):
            g = j // 4
            yp, sn = _ssd_pair(j, xs[:, 128 * j:128 * j + 128], bc[:, 128 * g:128 * g + 128],
                               bc[:, 256 + 128 * g:384 + 128 * g], s_ref[:, 128 * j:128 * j + 128], *prep)
            ys.append(yp)
            s_ref[:, 128 * j:128 * j + 128] = sn
        y = jnp.concatenate(ys, axis=1)
        yraw_ref[...] = y
        ya_ref[...] = _ssd_post(y, xs, z_ref[...], dsk_ref[...], nrm_ref[...])

    return pl.pallas_call(
        body, name="ssd_fwd", grid=(n_c,),
        in_specs=_proj_specs(CH, (0, 5, 6), fwd) + _prev8_specs(CH, (5, 6), fwd) + [
            _full((4, D)), _full((4, 512)), _full((1, D)), _full((1, 512)), _full((1, 128)), _full((1, 128)),
            _full((1, D)), _full((1, D))],
        out_specs=[pl.BlockSpec((CH, D), lambda i: (i, 0)), pl.BlockSpec((1, NS, D), lambda i: (i, 0, 0)),
                   pl.BlockSpec((CH, D), lambda i: (i, 0))],
        out_shape=[jax.ShapeDtypeStruct((t_n, D), F32), jax.ShapeDtypeStruct((n_c, NS, D), F32),
                   jax.ShapeDtypeStruct((t_n, D), F32)],
        scratch_shapes=[pltpu.VMEM((NS, D), F32)],
        compiler_params=_cp("arbitrary"),
    )(proj, proj, proj, proj, proj, cwx, cwb, cbx, cbb, dtb, alog, dsk, nrm)


def _ssd_bwd(dya, yraw, proj, sprev, cwx, cwb, cbx, cbb, dtb, alog, dsk, nrm, mlp_ops):
    t_n = proj.shape[0]
    n_c = t_n // CH
    rev = lambda i: n_c - 1 - i
    fb = FF // n_c

    def body(dya_ref, yraw_ref, z_ref, xs_ref, bc_ref, xsp_ref, bcp_ref, sprev_ref, cwx_ref, cwb_ref, cbx_ref, cbb_ref,
             dtb_ref, alog_ref, dsk_ref, nrm_ref, hid_ref, dff_ref, dhp_ref, v_ref,
             dz_ref, dxs_ref, dbc_ref, dcwx_ref, dcwb_ref, dcbx_ref, dcbb_ref, ddtb_ref, dalog_ref, ddsk_ref,
             dnrm_ref, dwd_ref, dwu_ref, ds_ref, futx_ref, futb_ref):
        i = pl.program_id(0)
        acc_refs = (dcwx_ref, dcwb_ref, dcbx_ref, dcbb_ref, ddtb_ref, dalog_ref, ddsk_ref, dnrm_ref)
        tn = (((0,), (0,)), ((), ()))
        dwd_ref[...] = lax.dot_general(hid_ref[...], dff_ref[...], tn, preferred_element_type=F32).astype(MXU)
        dwu_ref[...] = lax.dot_general(dhp_ref[...], v_ref[...], tn, preferred_element_type=F32).astype(MXU)

        @pl.when(i == 0)
        def _():
            for r in (ds_ref, futx_ref, futb_ref) + acc_refs:
                r[...] = jnp.zeros_like(r)

        keep = jnp.where(i == n_c - 1, 0.0, 1.0)
        taps_x = _taps_past(xs_ref[...], xsp_ref[...] * keep)
        taps_b = _taps_past(bc_ref[:, :512], bcp_ref[:, :512] * keep)
        xs_pre = _conv_apply(taps_x, cwx_ref[...], cbx_ref[...])
        bc_pre = _conv_apply(taps_b, cwb_ref[...], cbb_ref[...])
        xs = _silu(xs_pre)
        bc = _silu(bc_pre)
        prep, prep_vjp = jax.vjp(_ssd_prep, bc_ref[:, 512:640], dtb_ref[...], alog_ref[...])
        s_in = sprev_ref[0]

        def pair_args(j):
            g = j // 4
            return (xs[:, 128 * j:128 * j + 128], bc[:, 128 * g:128 * g + 128],
                    bc[:, 256 + 128 * g:384 + 128 * g], s_in[:, 128 * j:128 * j + 128]) + tuple(prep)

        _, post_vjp = jax.vjp(_ssd_post, yraw_ref[...], xs, z_ref[...], dsk_ref[...], nrm_ref[...])
        dy, dxs_skip, dz, ddsk, dnrm = post_vjp(dya_ref[...])
        dz_ref[...] = dz.astype(dz_ref.dtype)
        ddsk_ref[...] += ddsk
        dnrm_ref[...] += dnrm

        dprep = [jnp.zeros_like(p) for p in prep]
        dxp = []
        dbg = [jnp.zeros((CH, 128), F32), jnp.zeros((CH, 128), F32)]
        dcg = [jnp.zeros((CH, 128), F32), jnp.zeros((CH, 128), F32)]
        for j in range(NH // 2):
            g = j // 4
            _, pair_vjp = jax.vjp(functools.partial(_ssd_pair, j), *pair_args(j))
            cts = pair_vjp((dy[:, 128 * j:128 * j + 128], ds_ref[:, 128 * j:128 * j + 128]))
            dxp.append(cts[0])
            dbg[g] = dbg[g] + cts[1]
            dcg[g] = dcg[g] + cts[2]
            ds_ref[:, 128 * j:128 * j + 128] = cts[3]
            dprep = [a + b for a, b in zip(dprep, cts[4:])]
        ddtraw, ddtb, dalog = prep_vjp(tuple(dprep))
        ddtb_ref[...] += ddtb
        dalog_ref[...] += dalog

        dxs_pre = (dxs_skip + jnp.concatenate(dxp, axis=1)) * _dsilu(xs_pre)
        dbc_pre = jnp.concatenate([dbg[0], dbg[1], dcg[0], dcg[1]], axis=1) * _dsilu(bc_pre)
        dcbx_ref[...] += _rowsum(dxs_pre)
        dcbb_ref[...] += _rowsum(dbc_pre)
        for k in range(4):
            dcwx_ref[k:k + 1, :] += _rowsum(dxs_pre * taps_x[k])
            dcwb_ref[k:k + 1, :] += _rowsum(dbc_pre * taps_b[k])
        fx = _taps_future(dxs_pre, futx_ref[...])
        fb = _taps_future(dbc_pre, futb_ref[...])
        cwx = cwx_ref[...]
        cwb = cwb_ref[...]
        dxs_in = fx[0] * cwx[0:1, :]
        dbc_in = fb[0] * cwb[0:1, :]
        for k in range(1, 4):
            dxs_in = dxs_in + fx[k] * cwx[k:k + 1, :]
            dbc_in = dbc_in + fb[k] * cwb[k:k + 1, :]
        futx_ref[...] = dxs_pre[0:8, :]
        futb_ref[...] = dbc_pre[0:8, :]
        dxs_ref[...] = dxs_in.astype(dxs_ref.dtype)
        dbc_ref[...] = jnp.concatenate([dbc_in, ddtraw, jnp.zeros((CH, SEG - 640), F32)], axis=1).astype(dbc_ref.dtype)

    row_out = lambda: pl.BlockSpec((CH, D), lambda i: (rev(i), 0))
    outs = pl.pallas_call(
        body, name="ssd_bwd", grid=(n_c,),
        in_specs=[row_out(), row_out()] + _proj_specs(CH, (0, 5, 6), rev)
        + _prev8_specs(CH, (5, 6), rev) + [pl.BlockSpec((1, NS, D), lambda i: (rev(i), 0, 0)),
                                            _full((4, D)), _full((4, 512)), _full((1, D)), _full((1, 512)),
                                            _full((1, 128)), _full((1, 128)), _full((1, D)), _full((1, D)),
                                            pl.BlockSpec((t_n, fb), lambda i: (0, i)), _full((t_n, D)),
                                            pl.BlockSpec((t_n, fb), lambda i: (0, i)), _full((t_n, D))],
        out_specs=[row_out(), row_out(), row_out(), _full((4, D)), _full((4, 512)), _full((1, D)), _full((1, 512)),
                   _full((1, 128)), _full((1, 128)), _full((1, D)), _full((1, D)),
                   pl.BlockSpec((fb, D), lambda i: (i, 0)), pl.BlockSpec((fb, D), lambda i: (i, 0))],
        out_shape=[jax.ShapeDtypeStruct((t_n, D), MXU)] * 3 + [
            jax.ShapeDtypeStruct(s, F32) for s in ((4, D), (4, 512), (1, D), (1, 512), (1, 128), (1, 128), (1, D), (1, D))]
        + [jax.ShapeDtypeStruct((FF, D), MXU)] * 2,
        scratch_shapes=[pltpu.VMEM((NS, D), F32), pltpu.VMEM((8, D), F32), pltpu.VMEM((8, 512), F32)],
        compiler_params=_cp("arbitrary"),
    )(dya, yraw, proj, proj, proj, proj, proj, sprev, cwx, cwb, cbx, cbb, dtb, alog, dsk, nrm, *mlp_ops)
    return outs


LRU_ROWS = 256
LRU_BLK = 256


def _lru_gates(xr, wa, wx, ba, bx, lam):
    pr = jnp.concatenate([_nn(xr[:, LRU_BLK * b:LRU_BLK * (b + 1)], wa[b]) for b in range(D // LRU_BLK)], axis=1) + ba
    pi = jnp.concatenate([_nn(xr[:, LRU_BLK * b:LRU_BLK * (b + 1)], wx[b]) for b in range(D // LRU_BLK)], axis=1) + bx
    log_a = -LRU_C * _sigmoid(pr) * _softplus(-lam)
    a = jnp.exp(log_a)
    mult = jnp.sqrt(1.0 - jnp.exp(2.0 * log_a))
    return a, mult * (_sigmoid(pi) * xr)


def _lru_out(h, g):
    return h * jax.nn.gelu(g, approximate=True)


def _lru_fwd(proj, cw, cb, wa, wx, ba, bx, lam):
    t_n = proj.shape[0]
    rows = min(LRU_ROWS, t_n)
    fwd = lambda i: i

    def body(g_ref, x_ref, xp_ref, cw_ref, cb_ref, wa_ref, wx_ref, ba_ref, bx_ref, lam_ref, yb_ref, h_ref,
             a_s, u_s, carry):
        i = pl.program_id(0)

        @pl.when(i == 0)
        def _():
            carry[...] = jnp.zeros_like(carry)

        keep = jnp.where(i == 0, 0.0, 1.0)
        xr = _conv_apply(_taps_past(x_ref[...], xp_ref[...] * keep), cw_ref[...], cb_ref[...])
        a, u = _lru_gates(xr, wa_ref[...], wx_ref[...], ba_ref[...], bx_ref[...], lam_ref[...])
        a_s[...] = a
        u_s[...] = u
        row = lax.broadcasted_iota(jnp.int32, (8, D), 0)

        def blk(b, c):
            s = pl.multiple_of(b * 8, 8)
            av = a_s[pl.ds(s, 8), :]
            uv = u_s[pl.ds(s, 8), :]
            for d in (1, 2, 4):
                m = row >= d
                uv = uv + av * jnp.where(m, pltpu.roll(uv, d, 0), 0.0)
                av = av * jnp.where(m, pltpu.roll(av, d, 0), 1.0)
            hv = uv + av * c
            h_ref[pl.ds(s, 8), :] = hv
            return hv[7:8, :]

        carry[0:1, :] = lax.fori_loop(0, rows // 8, blk, carry[0:1, :])
        yb_ref[...] = _lru_out(h_ref[...], g_ref[...])

    return pl.pallas_call(
        body, name="lru_fwd", grid=(t_n // rows,),
        in_specs=_proj_specs(rows, (1, 2), fwd) + _prev8_specs(rows, (2,), fwd) + [
            _full((4, D)), _full((1, D)), _full((4, LRU_BLK, LRU_BLK)), _full((4, LRU_BLK, LRU_BLK)),
            _full((1, D)), _full((1, D)), _full((1, D))],
        out_specs=[pl.BlockSpec((rows, D), lambda i: (i, 0)), pl.BlockSpec((rows, D), lambda i: (i, 0))],
        out_shape=[jax.ShapeDtypeStruct((t_n, D), F32), jax.ShapeDtypeStruct((t_n, D), F32)],
        scratch_shapes=[pltpu.VMEM((rows, D), F32), pltpu.VMEM((rows, D), F32), pltpu.VMEM((8, D), F32)],
        compiler_params=_cp("arbitrary"),
    )(proj, proj, proj, cw, cb, wa, wx, ba, bx, lam)


def _lru_bwd(dyb, proj, h, cw, cb, wa, wx, ba, bx, lam):
    t_n = proj.shape[0]
    rows = min(LRU_ROWS, t_n)
    n_t = t_n // rows
    rev = lambda i: n_t - 1 - i
    rb = rows // 8

    def body(dyb_ref, g_ref, x_ref, xp_ref, h_ref, hp_ref, cw_ref, cb_ref, wa_ref, wx_ref, ba_ref, bx_ref, lam_ref,
             dg_ref, dx_ref, dcw_ref, dcb_ref, dwa_ref, dwx_ref, dba_ref, dbx_ref, dlam_ref,
             a_s, dh_s, hx_s, da_s, du_s, carry, fut):
        i = pl.program_id(0)
        acc_refs = (dcw_ref, dcb_ref, dwa_ref, dwx_ref, dba_ref, dbx_ref, dlam_ref)

        @pl.when(i == 0)
        def _():
            for r in (carry, fut) + acc_refs:
                r[...] = jnp.zeros_like(r)

        keep = jnp.where(i == n_t - 1, 0.0, 1.0)
        taps = _taps_past(x_ref[...], xp_ref[...] * keep)
        xr = _conv_apply(taps, cw_ref[...], cb_ref[...])
        gate_in = (xr, wa_ref[...], wx_ref[...], ba_ref[...], bx_ref[...], lam_ref[...])
        (a, _), gates_vjp = jax.vjp(_lru_gates, *gate_in)
        _, out_vjp = jax.vjp(_lru_out, h_ref[...], g_ref[...])
        dh, dg = out_vjp(dyb_ref[...])
        dg_ref[...] = dg.astype(dg_ref.dtype)
        a_s[...] = a
        dh_s[...] = dh
        hx_s[0:8, :] = hp_ref[...] * keep
        hx_s[8:, :] = h_ref[...]
        row = lax.broadcasted_iota(jnp.int32, (8, D), 0)

        def blk(b, c):
            s = pl.multiple_of((rb - 1 - b) * 8, 8)
            av = a_s[pl.ds(s, 8), :]
            dhv = dh_s[pl.ds(s, 8), :]
            a0 = av
            kv = av * dhv
            for d in (1, 2, 4):
                m = row <= 7 - d
                kv = kv + av * jnp.where(m, pltpu.roll(kv, 8 - d, 0), 0.0)
                av = av * jnp.where(m, pltpu.roll(av, 8 - d, 0), 1.0)
            kv = kv + av * c
            gv = dhv + jnp.where(row < 7, pltpu.roll(kv, 7, 0), c)
            hb = hx_s[pl.ds(s + 8, 8), :]
            hpv = hx_s[pl.ds(s, 8), :]
            hprev = jnp.where(row >= 1, pltpu.roll(hb, 1, 0), hpv[7:8, :])
            du_s[pl.ds(s, 8), :] = gv
            da_s[pl.ds(s, 8), :] = gv * hprev
            del a0
            return kv[0:1, :]

        carry[0:1, :] = lax.fori_loop(0, rb, blk, carry[0:1, :])
        dxr, dwa, dwx, dba, dbx, dlam = gates_vjp((da_s[...], du_s[...]))
        dwa_ref[...] += dwa
        dwx_ref[...] += dwx
        dba_ref[...] += dba
        dbx_ref[...] += dbx
        dlam_ref[...] += dlam
        dcb_ref[...] += _rowsum(dxr)
        for k in range(4):
            dcw_ref[k:k + 1, :] += _rowsum(dxr * taps[k])
        ft = _taps_future(dxr, fut[...])
        cwv = cw_ref[...]
        dx = ft[0] * cwv[0:1, :]
        for k in range(1, 4):
            dx = dx + ft[k] * cwv[k:k + 1, :]
        fut[...] = dxr[0:8, :]
        dx_ref[...] = dx.astype(dx_ref.dtype)

    row_in = lambda: pl.BlockSpec((rows, D), lambda i: (rev(i), 0))
    prev_h = pl.BlockSpec((8, D), lambda i: (jnp.maximum(rev(i) * rb - 1, 0), 0))
    wspec = lambda: _full((4, LRU_BLK, LRU_BLK))
    return pl.pallas_call(
        body, name="lru_bwd", grid=(n_t,),
        in_specs=[row_in()] + _proj_specs(rows, (1, 2), rev) + _prev8_specs(rows, (2,), rev) + [row_in(), prev_h] + [
            _full((4, D)), _full((1, D)), wspec(), wspec(), _full((1, D)), _full((1, D)), _full((1, D))],
        out_specs=[row_in(), row_in(), _full((4, D)), _full((1, D)), wspec(), wspec(), _full((1, D)), _full((1, D)),
                   _full((1, D))],
        out_shape=[jax.ShapeDtypeStruct((t_n, D), MXU)] * 2 + [
            jax.ShapeDtypeStruct(s, F32) for s in ((4, D), (1, D), (4, LRU_BLK, LRU_BLK), (4, LRU_BLK, LRU_BLK),
                                                   (1, D), (1, D), (1, D))],
        scratch_shapes=[pltpu.VMEM((rows, D), F32), pltpu.VMEM((rows, D), F32), pltpu.VMEM((rows + 8, D), F32),
                        pltpu.VMEM((rows, D), F32), pltpu.VMEM((rows, D), F32), pltpu.VMEM((8, D), F32),
                        pltpu.VMEM((8, D), F32)],
        compiler_params=_cp("arbitrary"),
    )(dyb, proj, proj, proj, h, h, cw, cb, wa, wx, ba, bx, lam)


def _merge_out(ya, yb, proj, x, wout, g1):
    t_n = x.shape[0]
    tm = min(t_n, 512)

    def body(ya_ref, yb_ref, ga_ref, gb_ref, x_ref, w_ref, g_ref, h1_ref, mix_ref, mg_ref):
        merged = _sigmoid(ga_ref[...]) * ya_ref[...] + _sigmoid(gb_ref[...]) * yb_ref[...]
        mg = merged.astype(MXU)
        mg_ref[...] = mg
        mix = jnp.dot(mg, w_ref[...], preferred_element_type=F32)
        mix_ref[...] = mix
        h1_ref[...] = x_ref[...] + _rms(mix, g_ref[...])

    row = lambda: pl.BlockSpec((tm, D), lambda i: (i, 0))
    return pl.pallas_call(
        body, name="merge_out", grid=(t_n // tm,),
        in_specs=[row(), row()] + _proj_specs(tm, (3, 4), lambda i: i) + [row(), _full((D, D)), _full((1, D))],
        out_specs=[row(), row(), row()],
        out_shape=[jax.ShapeDtypeStruct((t_n, D), F32), jax.ShapeDtypeStruct((t_n, D), F32),
                   jax.ShapeDtypeStruct((t_n, D), MXU)],
        compiler_params=_cp("parallel"),
    )(ya, yb, proj, proj, x, wout, g1)


def _out_bwd(dh1, mix, ya, yb, proj, wout, g1):
    t_n = dh1.shape[0]
    tm = min(t_n, 512)

    def body(dh1_ref, mix_ref, ya_ref, yb_ref, ga_ref, gb_ref, w_ref, g_ref,
             dmix_ref, dya_ref, dyb_ref, dga_ref, dgb_ref, dg1_ref):
        @pl.when(pl.program_id(0) == 0)
        def _():
            dg1_ref[...] = jnp.zeros_like(dg1_ref)

        dmix, dg_rows = _rms_bwd(mix_ref[...], g_ref[...], dh1_ref[...])
        dg1_ref[...] += _rowsum(dg_rows)
        dmix_b = dmix.astype(MXU)
        dmix_ref[...] = dmix_b
        dmg = lax.dot_general(dmix_b, w_ref[...], (((1,), (1,)), ((), ())), preferred_element_type=F32)
        sa = _sigmoid(ga_ref[...])
        sb = _sigmoid(gb_ref[...])
        dya_ref[...] = dmg * sa
        dyb_ref[...] = dmg * sb
        dga_ref[...] = (dmg * ya_ref[...] * sa * (1.0 - sa)).astype(MXU)
        dgb_ref[...] = (dmg * yb_ref[...] * sb * (1.0 - sb)).astype(MXU)

    row = lambda: pl.BlockSpec((tm, D), lambda i: (i, 0))
    return pl.pallas_call(
        body, name="out_bwd", grid=(t_n // tm,),
        in_specs=[row(), row(), row(), row()] + _proj_specs(tm, (3, 4), lambda i: i) + [_full((D, D)), _full((1, D))],
        out_specs=[row(), row(), row(), row(), row(), _full((1, D))],
        out_shape=[jax.ShapeDtypeStruct((t_n, D), MXU), jax.ShapeDtypeStruct((t_n, D), F32),
                   jax.ShapeDtypeStruct((t_n, D), F32), jax.ShapeDtypeStruct((t_n, D), MXU),
                   jax.ShapeDtypeStruct((t_n, D), MXU), jax.ShapeDtypeStruct((1, D), F32)],
        compiler_params=_cp("arbitrary"),
    )(dh1, mix, ya, yb, proj, proj, wout, g1)


MLP_TM = 1024
MLP_TF_FWD = 512
MLP_TF_BWD = 1024


def _mlp_fwd(h1, g2, wup, wdown, g3, tgt):
    t_n = h1.shape[0]
    tm = min(t_n, MLP_TM)
    n_f = FF // MLP_TF_FWD

    def body(h1_ref, g2_ref, wu_ref, wd_ref, g3_ref, tgt_ref, hp_ref, v_ref, dout_ref, dff_ref, loss_ref, dg3_ref, acc):
        i, j = pl.program_id(0), pl.program_id(1)

        @pl.when((i == 0) & (j == 0))
        def _():
            loss_ref[...] = jnp.zeros_like(loss_ref)
            dg3_ref[...] = jnp.zeros_like(dg3_ref)

        @pl.when(j == 0)
        def _():
            v_ref[...] = _rms(h1_ref[...], g2_ref[...]).astype(MXU)
            acc[...] = jnp.zeros_like(acc)

        hp = lax.dot_general(v_ref[...], wu_ref[...], (((1,), (1,)), ((), ())), preferred_element_type=F32)
        hp_ref[...] = hp
        hid = jnp.square(jnp.maximum(hp, 0.0))
        acc[...] += jnp.dot(hid.astype(MXU), wd_ref[...], preferred_element_type=F32)

        @pl.when(j == n_f - 1)
        def _():
            ff = acc[...]
            err = h1_ref[...] + _rms(ff, g3_ref[...]) - tgt_ref[...]
            loss_ref[...] += 0.5 * jnp.sum(jnp.mean(err * err, axis=-1, keepdims=True), axis=0, keepdims=True)
            dout = err * (1.0 / D)
            dout_ref[...] = dout
            dff, dg_rows = _rms_bwd(ff, g3_ref[...], dout)
            dg3_ref[...] += _rowsum(dg_rows)
            dff_ref[...] = dff.astype(MXU)

    row = lambda: pl.BlockSpec((tm, D), lambda i, j: (i, 0))
    vec = lambda: pl.BlockSpec((1, D), lambda i, j: (0, 0))
    return pl.pallas_call(
        body, name="mlp_fwd", grid=(t_n // tm, n_f),
        in_specs=[row(), vec(), pl.BlockSpec((MLP_TF_FWD, D), lambda i, j: (j, 0)),
                  pl.BlockSpec((MLP_TF_FWD, D), lambda i, j: (j, 0)), vec(), row()],
        out_specs=[pl.BlockSpec((tm, MLP_TF_FWD), lambda i, j: (i, j)), row(), row(), row(),
                   pl.BlockSpec((1, 1), lambda i, j: (0, 0)), vec()],
        out_shape=[jax.ShapeDtypeStruct((t_n, FF), F32), jax.ShapeDtypeStruct((t_n, D), MXU),
                   jax.ShapeDtypeStruct((t_n, D), F32), jax.ShapeDtypeStruct((t_n, D), MXU),
                   jax.ShapeDtypeStruct((1, 1), F32), jax.ShapeDtypeStruct((1, D), F32)],
        scratch_shapes=[pltpu.VMEM((tm, D), F32)],
        compiler_params=_cp("arbitrary", "arbitrary"),
    )(h1, g2, wup, wdown, g3, tgt)


def _mlp_bwd(dff, dout, hp, h1, wup, wdown, g2):
    t_n = h1.shape[0]
    tm = min(t_n, MLP_TM)
    n_f = FF // MLP_TF_BWD

    def mm_body(dff_ref, hp_ref, wu_ref, wd_ref, dv_ref, dhp_ref, hid_ref):
        @pl.when(pl.program_id(1) == 0)
        def _():
            dv_ref[...] = jnp.zeros_like(dv_ref)

        relu = jnp.maximum(hp_ref[...], 0.0)
        hid_ref[...] = jnp.square(relu).astype(MXU)
        dhid = lax.dot_general(dff_ref[...], wd_ref[...], (((1,), (1,)), ((), ())), preferred_element_type=F32)
        dhp = (dhid * (2.0 * relu)).astype(MXU)
        dhp_ref[...] = dhp
        dv_ref[...] += jnp.dot(dhp, wu_ref[...], preferred_element_type=F32)

    row = lambda: pl.BlockSpec((tm, D), lambda i, j: (i, 0))
    blk = lambda: pl.BlockSpec((tm, MLP_TF_BWD), lambda i, j: (i, j))
    wblk = lambda: pl.BlockSpec((MLP_TF_BWD, D), lambda i, j: (j, 0))
    dv, dhp, hid = pl.pallas_call(
        mm_body, name="mlp_bwd", grid=(t_n // tm, n_f),
        in_specs=[row(), blk(), wblk(), wblk()], out_specs=[row(), blk(), blk()],
        out_shape=[jax.ShapeDtypeStruct((t_n, D), F32), jax.ShapeDtypeStruct((t_n, FF), MXU),
                   jax.ShapeDtypeStruct((t_n, FF), MXU)],
        compiler_params=_cp("parallel", "arbitrary"),
    )(dff, hp, wup, wdown)

    tf = min(t_n, 512)

    def fin_body(dv_ref, h1_ref, g_ref, dout_ref, dh1_ref, dg2_ref):
        @pl.when(pl.program_id(0) == 0)
        def _():
            dg2_ref[...] = jnp.zeros_like(dg2_ref)

        dx, dg_rows = _rms_bwd(h1_ref[...], g_ref[...], dv_ref[...])
        dg2_ref[...] += _rowsum(dg_rows)
        dh1_ref[...] = dout_ref[...] + dx

    frow = lambda: pl.BlockSpec((tf, D), lambda i: (i, 0))
    dh1, dg2 = pl.pallas_call(
        fin_body, name="mlp_bwd_dh1", grid=(t_n // tf,), in_specs=[frow(), frow(), _full((1, D)), frow()],
        out_specs=[frow(), _full((1, D))],
        out_shape=[jax.ShapeDtypeStruct((t_n, D), F32), jax.ShapeDtypeStruct((1, D), F32)],
        compiler_params=_cp("arbitrary"),
    )(dv, h1, g2, dout)
    return dh1, dhp, hid, dg2


def _wgrad(a, g, name):
    t_n, k_n = a.shape
    n_n = g.shape[1]
    tt = min(t_n, 1024)
    tk, tn = min(k_n, 1024), min(n_n, 1024)

    n_t = t_n // tt

    def body(a_ref, g_ref, o_ref, acc):
        t = pl.program_id(2)

        @pl.when(t == 0)
        def _():
            acc[...] = jnp.zeros_like(acc)

        acc[...] += lax.dot_general(a_ref[...], g_ref[...], (((0,), (0,)), ((), ())), preferred_element_type=F32)

        @pl.when(t == n_t - 1)
        def _():
            o_ref[...] = acc[...].astype(o_ref.dtype)

    return pl.pallas_call(
        body, name=name, grid=(k_n // tk, n_n // tn, n_t),
        in_specs=[pl.BlockSpec((tt, tk), lambda k, n, t: (t, k)), pl.BlockSpec((tt, tn), lambda k, n, t: (t, n))],
        out_specs=pl.BlockSpec((tk, tn), lambda k, n, t: (k, n)),
        out_shape=jax.ShapeDtypeStruct((k_n, n_n), MXU),
        scratch_shapes=[pltpu.VMEM((tk, tn), F32)],
        compiler_params=_cp("parallel", "parallel", "arbitrary"),
    )(a, g)


def _wgrad_segs(segs, g, name):
    t_n, n_n = g.shape
    n_s = len(segs)
    tt = min(t_n, 1024)
    n_t = t_n // tt

    def body(*refs):
        a_refs = refs[:n_s]
        g_ref, o_ref, acc = refs[n_s:]
        s_id, t = pl.program_id(0), pl.program_id(1)

        @pl.when(t == 0)
        def _():
            acc[...] = jnp.zeros_like(acc)

        for s in range(n_s):
            @pl.when(s_id == s)
            def _(s=s):
                acc[...] += lax.dot_general(a_refs[s][...], g_ref[...], (((0,), (0,)), ((), ())),
                                            preferred_element_type=F32)

        @pl.when(t == n_t - 1)
        def _():
            o_ref[...] = acc[...].astype(o_ref.dtype)

    seg_spec = lambda s: pl.BlockSpec((tt, SEG), lambda i, t: (jnp.where(i == s, t, jnp.where(i < s, 0, n_t - 1)), 0))
    return pl.pallas_call(
        body, name=name, grid=(n_s, n_t),
        in_specs=[seg_spec(s) for s in range(n_s)] + [pl.BlockSpec((tt, n_n), lambda i, t: (t, 0))],
        out_specs=pl.BlockSpec((SEG, n_n), lambda i, t: (i, 0)),
        out_shape=jax.ShapeDtypeStruct((n_s * SEG, n_n), MXU),
        scratch_shapes=[pltpu.VMEM((SEG, n_n), F32)],
        compiler_params=_cp("arbitrary", "arbitrary"),
    )(*segs, g)


def _inproj_bwd(dsegs, w, x, g0, dh1):
    t_n = x.shape[0]
    tm = min(t_n, 1024)
    n_k = NP // SEG

    def mm_body(*refs):
        dp_refs = refs[:n_k]
        w_ref, du_ref = refs[n_k:]
        k = pl.program_id(1)

        @pl.when(k == 0)
        def _():
            du_ref[...] = jnp.zeros_like(du_ref)

        for s in range(n_k):
            @pl.when(k == s)
            def _(s=s):
                du_ref[...] += jnp.dot(dp_refs[s][...], w_ref[...], preferred_element_type=F32)

    du = pl.pallas_call(
        mm_body, name="inproj_bwd", grid=(t_n // tm, n_k),
        in_specs=[pl.BlockSpec((tm, SEG), functools.partial(
            lambda i, k, s: (jnp.where(k >= s, i, jnp.maximum(i - 1, 0)), 0), s=s)) for s in range(n_k)] + [
            pl.BlockSpec((SEG, D), lambda i, k: (k, 0))],
        out_specs=pl.BlockSpec((tm, D), lambda i, k: (i, 0)),
        out_shape=jax.ShapeDtypeStruct((t_n, D), F32),
        compiler_params=_cp("parallel", "arbitrary"),
    )(*dsegs, w)

    tf = min(t_n, 512)

    def fin_body(du_ref, x_ref, g_ref, dh1_ref, dx_ref, dg0_ref):
        @pl.when(pl.program_id(0) == 0)
        def _():
            dg0_ref[...] = jnp.zeros_like(dg0_ref)

        dx, dg_rows = _rms_bwd(x_ref[...], g_ref[...], du_ref[...])
        dg0_ref[...] += _rowsum(dg_rows)
        dx_ref[...] = dh1_ref[...] + dx

    row = lambda: pl.BlockSpec((tf, D), lambda i: (i, 0))
    return pl.pallas_call(
        fin_body, name="grad_x", grid=(t_n // tf,), in_specs=[row(), row(), _full((1, D)), row()],
        out_specs=[row(), _full((1, D))],
        out_shape=[jax.ShapeDtypeStruct((t_n, D), F32), jax.ShapeDtypeStruct((1, D), F32)],
        compiler_params=_cp("arbitrary"),
    )(du, x, g0, dh1)


def _blockdiag4(w):
    w4 = w.reshape(4, 4, 64, 1, 64).astype(MXU)
    same = (jnp.arange(4)[:, None, None, None] == jnp.arange(4)[None, None, :, None])
    return jnp.where(same[None], w4, jnp.zeros((), MXU)).reshape(4, 256, 256)


def _blockdiag4_extract(g):
    g5 = g.reshape(4, 4, 64, 4, 64)
    return jnp.stack([g5[:, q, :, q, :] for q in range(4)], axis=1).reshape(NH, 64, 64)


def _local_step(x, tgt, p, after_ssd=None, late_weights=None, send_mlp_grads=None, send_late_grads=None):
    f = lambda a: a.astype(F32)
    proj, u = _inproj(x, p["norm_mix_pre"], p["w_in_pT"])
    ssm_params = (p["cw_xs"], p["cw_bc"], p["cb_xs"], p["cb_bc"], p["dt_bias"], p["a_log"], p["d_skip_x"],
                  p["ssm_norm"])
    ya, sprev, yraw = _ssd_fwd(proj, *ssm_params)
    cb_lru = p["conv_lru_b"] if after_ssd is None else p["conv_lru_b"] + after_ssd(ya)
    lru_params = (p["conv_lru_w"], cb_lru, p["wa_bd"], p["wx_bd"], p["lru_ba"], p["lru_bx"], p["lru_lambda"])
    yb, h = _lru_fwd(proj, *lru_params)
    if late_weights is not None:
        p = dict(p, **late_weights(yb))
    h1, mix, merged = _merge_out(ya, yb, proj, x, p["w_out"], p["norm_mix_post"])
    hp, v, dout, dff, loss, dg3 = _mlp_fwd(h1, p["norm_mlp_pre"], p["w_upT"], p["w_down"], p["norm_mlp_post"], tgt)

    dh1, dhp, hid, dg2 = _mlp_bwd(dff, dout, hp, h1, p["w_upT"], p["w_down"], p["norm_mlp_pre"])
    dmix, dya, dyb, dga, dgb, dg1 = _out_bwd(dh1, mix, ya, yb, proj, p["w_out"], p["norm_mix_post"])
    d_w_out = _wgrad(merged, dmix, "wgrad_out")
    (dz, dxs, dbc, dcwx, dcwb, dcbx, dcbb, ddtb, dalog, ddsk, dnrm, d_w_down, d_w_up_t) = _ssd_bwd(
        dya, yraw, proj, sprev, *ssm_params, (hid, dff, dhp, v))
    if send_mlp_grads is not None:
        lru_params = (lru_params[0], lru_params[1] + send_mlp_grads(d_w_up_t, d_w_down, d_w_out)) + lru_params[2:]
    (dgl, dxl, dcwl, dcbl, dwa, dwx, dba, dbx, dlam) = _lru_bwd(dyb, proj, h, *lru_params)
    dsegs = [dz, dgl, dxl, dga, dgb, dxs, dbc]
    d_w_in_pt = _wgrad_segs(dsegs, u, "wgrad_in")
    g0 = p["norm_mix_pre"]
    if send_late_grads is not None:
        g0 = g0 + send_late_grads(d_w_in_pt)
    grad_x, dg0 = _inproj_bwd(dsegs, p["w_in_pT"], x, g0, dh1)
    grads = dict(
        norm_mix_pre=dg0, w_in_pT=d_w_in_pt, conv_ssm_w=jnp.concatenate([dcwx, dcwb], axis=1),
        conv_ssm_b=jnp.concatenate([dcbx, dcbb], axis=1), dt_bias=ddtb[:, :NH], a_log=dalog[:, :NH],
        d_skip=f(ddsk).reshape(NH, 64).sum(axis=1)[None, :], ssm_norm=dnrm, conv_lru_w=dcwl, conv_lru_b=dcbl,
        lru_wa=_blockdiag4_extract(dwa), lru_ba=dba, lru_wx=_blockdiag4_extract(dwx), lru_bx=dbx, lru_lambda=dlam,
        w_out=d_w_out, norm_mix_post=dg1, norm_mlp_pre=dg2, w_upT=d_w_up_t, w_down=d_w_down, norm_mlp_post=dg3)
    return loss[0, 0], grad_x, grads


W_IN_COLS = 6672


def _w_in_t_to_padded(wt):
    z, xs, bc, dt = wt[0:1024], wt[1024:2048], wt[2048:2560], wt[2560:2576]
    gl, xl, ga, gb = wt[2576:3600], wt[3600:4624], wt[4624:5648], wt[5648:6672]
    return jnp.concatenate([z, gl, xl, ga, gb, xs, bc, dt, jnp.zeros((NP - 6672, wt.shape[1]), wt.dtype)], axis=0)


def _w_in_t_from_padded(wp):
    z, gl, xl, ga, gb = (wp[SEG * s:SEG * (s + 1)] for s in range(5))
    xs, bc, dt = wp[5120:6144], wp[6144:6656], wp[6656:6672]
    return jnp.concatenate([z, xs, bc, dt, gl, xl, ga, gb], axis=0)


def _prep_params(full, big):
    f = lambda a: a.astype(F32)
    pad128 = lambda a: jnp.pad(f(a).reshape(1, -1), ((0, 0), (0, 128 - a.size)))
    cw = f(full["conv_ssm_w"])
    cb = f(full["conv_ssm_b"]).reshape(1, -1)
    return dict(
        big, norm_mix_pre=f(full["norm_mix_pre"]).reshape(1, D),
        cw_xs=cw[:, :D], cw_bc=cw[:, D:], cb_xs=cb[:, :D], cb_bc=cb[:, D:],
        dt_bias=pad128(full["dt_bias"]), a_log=pad128(full["a_log"]),
        d_skip_x=jnp.repeat(f(full["d_skip"]).reshape(-1), 64).reshape(1, D), ssm_norm=f(full["ssm_norm"]).reshape(1, D),
        conv_lru_w=f(full["conv_lru_w"]), conv_lru_b=f(full["conv_lru_b"]).reshape(1, D),
        wa_bd=_blockdiag4(full["lru_wa"]), wx_bd=_blockdiag4(full["lru_wx"]),
        lru_ba=f(full["lru_ba"]).reshape(1, D), lru_bx=f(full["lru_bx"]).reshape(1, D),
        lru_lambda=f(full["lru_lambda"]).reshape(1, D),
        norm_mix_post=f(full["norm_mix_post"]).reshape(1, D), norm_mlp_pre=f(full["norm_mlp_pre"]).reshape(1, D),
        norm_mlp_post=f(full["norm_mlp_post"]).reshape(1, D))


MESH_ID = pl.DeviceIdType.MESH
ANY = pl.BlockSpec(memory_space=pl.ANY)


def _my_place():
    x, y, c = lax.axis_index("x"), lax.axis_index("y"), lax.axis_index("c")
    return x, y, c, 4 * x + 2 * y + c


def _peer(x, y, c, k):
    return (x ^ ((k >> 2) & 1), y ^ ((k >> 1) & 1), c ^ (k & 1))


def _all_gather(pack, name):
    def body(in_ref, out_ref, send_sems, recv_sems, local_sem):
        x, y, c, me = _my_place()
        sibling = (x, y, 1 - c)
        flips = (4, 2, 6)

        def copy(j, block, to, src=None):
            return pltpu.make_async_remote_copy(
                src_ref=out_ref.at[block] if src is None else src, dst_ref=out_ref.at[block],
                send_sem=send_sems.at[j], recv_sem=recv_sems.at[j], device_id=to, device_id_type=MESH_ID)

        mine = pltpu.make_async_copy(in_ref, out_ref.at[me], local_sem)
        mine.start()
        first = [copy(0, me, sibling, src=in_ref)]
        first += [copy(1 + j, me, _peer(x, y, c, k), src=in_ref) for j, k in enumerate(flips)]
        for cp in first:
            cp.start()
        passed = [copy(4 + j, me ^ k, sibling) for j, k in enumerate(flips)]
        for j, k in enumerate(flips):
            copy(1 + j, me ^ k, (x, y, c)).wait_recv()
            passed[j].start()
        copy(0, me ^ 1, (x, y, c)).wait_recv()
        for j, k in enumerate(flips):
            copy(4 + j, me ^ k ^ 1, (x, y, c)).wait_recv()
        for cp in first + passed:
            cp.wait_send()
        mine.wait()

    return pl.pallas_call(
        body, name=name, in_specs=[ANY], out_specs=ANY,
        out_shape=jax.ShapeDtypeStruct((N_DEV,) + pack.shape, pack.dtype),
        scratch_shapes=[pltpu.SemaphoreType.DMA((N_DEV - 1,)), pltpu.SemaphoreType.DMA((N_DEV - 1,)),
                        pltpu.SemaphoreType.DMA],
    )(pack)


HBM = pl.BlockSpec(memory_space=pltpu.HBM)
SEM = pl.BlockSpec(memory_space=pltpu.SEMAPHORE)
PLAN_GATHER = tuple((k, "pack", 0) for k in range(1, N_DEV))
PLAN_SCATTER = tuple((k, "slot", 0) for k in range(1, N_DEV))
PLAN_GATHER_ICI = tuple((k, "pack", 0) for k in (2, 4, 6))
PLAN_GATHER_D2D = ((1, "pack", 0),) + tuple((1, s, s) for s in (2, 4, 6))


def _plan_copy(j, plan, src_ref, land_ref, sems):
    k, source, slot = plan[j]
    x, y, c, me = _my_place()
    if source == "pack":
        src = src_ref
    elif source == "slot":
        src = src_ref.at[me ^ k]
    else:
        src = land_ref.at[me ^ source]
    return pltpu.make_async_remote_copy(
        src_ref=src, dst_ref=land_ref.at[me ^ slot], send_sem=sems[j], recv_sem=sems[len(plan) + j],
        device_id=_peer(x, y, c, k), device_id_type=MESH_ID)


def _exchange_start(src, land, plan, name):
    n_c = len(plan)
    if land is None:
        land = pltpu.with_memory_space_constraint(lax.empty((N_DEV,) + src.shape[-2:], src.dtype), pltpu.HBM)

    def body(src_ref, land_ref, *rest):
        sems, token = rest[:2 * n_c], rest[2 * n_c + 2]
        for j in range(n_c):
            _plan_copy(j, plan, src_ref, land_ref, sems).start()
        token[...] = jnp.zeros_like(token)

    outs = pl.pallas_call(
        body, name=name,
        out_shape=(pltpu.SemaphoreType.DMA(()),) * (2 * n_c) + (
            pltpu.HBM(src.shape, src.dtype), pltpu.HBM(land.shape, land.dtype), jax.ShapeDtypeStruct((8, 128), F32)),
        in_specs=(HBM, HBM), out_specs=(SEM,) * (2 * n_c) + (HBM, HBM, pl.BlockSpec(memory_space=pltpu.VMEM)),
        input_output_aliases={0: 2 * n_c, 1: 2 * n_c + 1},
        compiler_params=pltpu.CompilerParams(has_side_effects=pltpu.SideEffectType.DATAFLOW_SIDE_EFFECTING),
    )(pltpu.with_memory_space_constraint(src, pltpu.HBM), land)
    return outs[:2 * n_c], outs[2 * n_c], outs[2 * n_c + 1], outs[2 * n_c + 2]


def _exchange_wait(sems, src_thru, land_thru, after, plan, name):
    n_c = len(plan)

    def body(src_ref, land_ref, *rest):
        for j in range(n_c):
            cp = _plan_copy(j, plan, src_ref, land_ref, rest[:2 * n_c])
            cp.wait_send()
            cp.wait_recv()

    return pl.pallas_call(
        body, name=name,
        out_shape=(pltpu.HBM(src_thru.shape, src_thru.dtype), pltpu.HBM(land_thru.shape, land_thru.dtype)),
        in_specs=(HBM, HBM) + (SEM,) * (2 * n_c) + (pl.BlockSpec(memory_space=pl.ANY),), out_specs=(HBM, HBM),
        input_output_aliases={0: 0, 1: 1},
        compiler_params=pltpu.CompilerParams(has_side_effects=pltpu.SideEffectType.DATAFLOW_SIDE_EFFECTING),
    )(src_thru, land_thru, *sems, after)


def _slot_sum(parts, name):
    r_n, c_n = parts.shape[1:]
    tr = max(t for t in range(16, 513, 16) if r_n % t == 0)

    def body(p_ref, o_ref):
        acc = p_ref[0].astype(F32)
        for k in range(1, N_DEV):
            acc = acc + p_ref[k].astype(F32)
        o_ref[...] = acc

    return pl.pallas_call(
        body, name=name, grid=(r_n // tr,),
        in_specs=[pl.BlockSpec((N_DEV, tr, c_n), lambda i: (0, i, 0))],
        out_specs=pl.BlockSpec((tr, c_n), lambda i: (i, 0)),
        out_shape=jax.ShapeDtypeStruct((r_n, c_n), F32),
        compiler_params=_cp("parallel"),
    )(parts)


def _adam_math(w, g, m, v):
    m = ADAM_B1 * m + (1.0 - ADAM_B1) * g
    v = ADAM_B2 * v + (1.0 - ADAM_B2) * jnp.square(g)
    m_hat = m / (1.0 - ADAM_B1 ** ADAM_STEP)
    v_hat = v / (1.0 - ADAM_B2 ** ADAM_STEP)
    return -ADAM_LR * (m_hat / (jnp.sqrt(v_hat) + ADAM_EPS) + ADAM_WD * w), m, v


def _adam_big(w, g, m, v, name):
    def body(w_ref, g_ref, m_ref, v_ref, d_ref, mo_ref, vo_ref):
        d_ref[...], mo_ref[...], vo_ref[...] = _adam_math(w_ref[...], g_ref[...], m_ref[...], v_ref[...])

    if w.ndim == 3:
        _, r_n, c_n = w.shape
        tr = min(r_n, 256)
        grid = (r_n // tr,)
        blk = lambda: pl.BlockSpec((1, tr, c_n), lambda i: (0, i, 0))
    else:
        r_n, c_n = w.shape
        tc = min(c_n, 256)
        grid = (c_n // tc,)
        blk = lambda: pl.BlockSpec((r_n, tc), lambda i: (0, i))
    return pl.pallas_call(
        body, name=name, grid=grid, in_specs=[blk(), blk(), blk(), blk()], out_specs=[blk(), blk(), blk()],
        out_shape=[jax.ShapeDtypeStruct(w.shape, F32)] * 3, compiler_params=_cp("parallel"),
    )(w, g, m, v)


def _adam_small(groups, where, wmv):
    n, n_g = len(wmv), len(groups)

    def body(*refs):
        g_refs = refs[:n_g]
        w_refs = refs[n_g:n_g + 3 * n]
        o_refs = refs[n_g + 3 * n:]
        for q in range(n):
            w_ref, m_ref, v_ref = w_refs[3 * q:3 * q + 3]
            r, c = w_ref.shape
            gi, r0 = where[q]
            g = g_refs[gi][0, r0:r0 + r, 0:c]
            for k in range(1, N_DEV):
                g = g + g_refs[gi][k, r0:r0 + r, 0:c]
            d, m, v = _adam_math(w_ref[...], g, m_ref[...], v_ref[...])
            o_refs[4 * q][...] = g
            o_refs[4 * q + 1][...] = d
            o_refs[4 * q + 2][...] = m
            o_refs[4 * q + 3][...] = v

    flat_wmv = [a for t in wmv for a in t]
    vm = pl.BlockSpec(memory_space=pltpu.VMEM)
    outs = pl.pallas_call(
        body, name="adam_small", in_specs=[vm] * (n_g + 3 * n), out_specs=[vm] * (4 * n),
        out_shape=[jax.ShapeDtypeStruct(t[0].shape, F32) for t in wmv for _ in range(4)],
        compiler_params=pltpu.CompilerParams(vmem_limit_bytes=VMEM_LIMIT),
    )(*groups, *flat_wmv)
    return [tuple(outs[4 * q:4 * q + 4]) for q in range(n)]


WEIGHTS = ["norm_mix_pre", "w_in", "conv_ssm_w", "conv_ssm_b", "dt_bias", "a_log", "d_skip", "ssm_norm", "conv_lru_w",
           "conv_lru_b", "lru_wa", "lru_ba", "lru_wx", "lru_bx", "lru_lambda", "w_out", "norm_mix_post", "norm_mlp_pre",
           "w_up", "w_down", "norm_mlp_post"]
BIG = ["w_out", "w_up", "w_down", "w_in"]
IN_ROWS = W_IN_COLS // N_DEV
IN_PAD, EARLY_ROWS = 848, 880
ROW_UP, ROW_DOWN, LATE_ROWS = 128, 640, 1152
GRAD_LATE_ROWS = 864
CONV_SSM_COLS, CONV_LRU_COLS = 1536 // N_DEV, D // N_DEV
SMALL = [("norm_mix_pre", (1, D), 0, 0), ("ssm_norm", (1, D), 0, 1), ("conv_lru_b", (1, D), 0, 2),
         ("lru_lambda", (1, D), 0, 3), ("norm_mix_post", (1, D), 0, 4), ("norm_mlp_pre", (1, D), 0, 5),
         ("norm_mlp_post", (1, D), 0, 6), ("conv_ssm_b", (1, 1536), 1, 0), ("dt_bias", (1, NH), 2, 0),
         ("a_log", (1, NH), 2, 1), ("d_skip", (1, NH), 2, 2), ("conv_ssm_w", (4, CONV_SSM_COLS), 3, 0),
         ("conv_lru_w", (4, CONV_LRU_COLS), 4, 0), ("lru_wa", (D, 64), 5, 0), ("lru_wx", (D, 64), 5, D),
         ("lru_ba", (NH, 64), 6, 0), ("lru_bx", (NH, 64), 6, NH)]
SMALL_GROUPS = [(8, D), (1, 1536), (8, 128), (4, 1536), (4, D), (2 * D, 64), (2 * NH, 64)]


def _pad_rows(flat, mult):
    n = flat.shape[0]
    rows = -(-n // (128 * mult)) * mult
    return jnp.pad(flat, (0, rows * 128 - n)).reshape(rows, 128)


def _split3(a):
    hi = a.astype(MXU)
    r1 = a - hi.astype(F32)
    mid = r1.astype(MXU)
    lo = (r1 - mid.astype(F32)).astype(MXU)
    return jnp.stack([hi, mid, lo])


def _early_pack(a, me):
    bf = lambda t: t.astype(MXU)
    conv = lambda t, c: jnp.pad(_split3(t).reshape(12, c), ((0, 4), (0, D - c)))
    shifted = lax.dynamic_update_slice(jnp.zeros((IN_PAD, D), MXU), bf(a["w_in"][0]).T, (2 * me, 0))
    return jnp.concatenate([shifted, conv(a["conv_ssm_w"][0], CONV_SSM_COLS), conv(a["conv_lru_w"][0], CONV_LRU_COLS)],
                           axis=0)


TILE = 16
SHARD_TILES = IN_PAD // TILE
SHARD_STEP = (IN_ROWS // TILE)
SEG_TILES = ((0, 64, 0), (64, 128, 320), (128, 160, 384), (160, 161, 416), (161, 225, 64), (225, 289, 128),
             (289, 353, 192), (353, 417, 256))


def _tile_runs(lo, hi):
    runs = []
    for s0, s1, d0 in SEG_TILES:
        a, b = max(lo, s0), min(hi, s1)
        if a < b:
            runs.append((a, b - a, d0 + a - s0))
    return runs


def _assemble_w_in(g):
    whole = []
    for k in range(N_DEV):
        lo = SHARD_STEP * k + (1 if k else 0)
        hi = SHARD_STEP * (k + 1) + (1 if k == N_DEV - 1 else 0)
        whole += [(k, a - SHARD_STEP * k, n, d) for a, n, d in _tile_runs(lo, hi)]
    split = [(k, _tile_runs(SHARD_STEP * k, SHARD_STEP * k + 1)[0][2]) for k in range(1, N_DEV)]

    def body(g_ref, o_ref):
        rows = lambda t, n=1: pl.ds(TILE * t, TILE * n)
        for k, t, n, d in whole:
            o_ref[rows(d, n), :] = g_ref[k, rows(t, n), :]
        for k, d in split:
            o_ref[rows(d), :] = g_ref[k - 1, rows(SHARD_STEP), :] + g_ref[k, rows(0), :]
        o_ref[pl.ds(W_IN_COLS, NP - W_IN_COLS), :] = jnp.zeros((NP - W_IN_COLS, D), o_ref.dtype)

    vm = pl.BlockSpec(memory_space=pltpu.VMEM)
    return pl.pallas_call(
        body, name="assemble_w_in", in_specs=[vm], out_specs=vm, out_shape=jax.ShapeDtypeStruct((NP, D), g.dtype),
        compiler_params=pltpu.CompilerParams(vmem_limit_bytes=VMEM_LIMIT),
    )(g)


def _scatter_w_in_grad(dw):
    runs = [(k, a - SHARD_STEP * k, n, d) for k in range(N_DEV)
            for a, n, d in _tile_runs(SHARD_STEP * k, SHARD_STEP * k + SHARD_TILES)]
    pad = GRAD_LATE_ROWS - IN_PAD

    def body(dw_ref, o_ref):
        rows = lambda t, n: pl.ds(TILE * t, TILE * n)
        for k, t, n, d in runs:
            o_ref[k, rows(t, n), :] = dw_ref[rows(d, n), :]
        for k in range(N_DEV):
            o_ref[k, pl.ds(IN_PAD, pad), :] = jnp.zeros((pad, D), o_ref.dtype)

    vm = pl.BlockSpec(memory_space=pltpu.VMEM)
    return pl.pallas_call(
        body, name="scatter_w_in_grad", in_specs=[vm], out_specs=vm,
        out_shape=jax.ShapeDtypeStruct((N_DEV, GRAD_LATE_ROWS, D), dw.dtype),
        compiler_params=pltpu.CompilerParams(vmem_limit_bytes=VMEM_LIMIT),
    )(dw)


def _early_unpack(g):
    w_in_pt = _assemble_w_in(g)
    conv = {}
    for n, r0, c in (("conv_ssm_w", IN_PAD, CONV_SSM_COLS), ("conv_lru_w", IN_PAD + 16, CONV_LRU_COLS)):
        s = g[:, r0:r0 + 12, :c].astype(F32).reshape(N_DEV, 3, 4, c)
        conv[n] = ((s[:, 0] + s[:, 1]) + s[:, 2]).transpose(1, 0, 2).reshape(4, N_DEV * c)
    return w_in_pt, conv


def _late_pack(a):
    bf = lambda t: t.astype(MXU)
    return jnp.concatenate([bf(a["w_out"][0]), bf(a["w_up"][0]).T, bf(a["w_down"][0])], axis=0)


def _late_unpack(g):
    return dict(w_out=g[:, :ROW_UP].reshape(D, D), w_upT=g[:, ROW_UP:ROW_DOWN].reshape(FF, D),
                w_down=g[:, ROW_DOWN:].reshape(FF, D))


def _own_slot(land, own):
    me = 4 * lax.axis_index("x") + 2 * lax.axis_index("y") + lax.axis_index("c")
    return lax.dynamic_update_slice_in_dim(land, own[None], me, axis=0)


def kernel(x, norm_mix_pre, w_in, conv_ssm_w, conv_ssm_b, dt_bias, a_log, d_skip, ssm_norm, conv_lru_w, conv_lru_b, lru_wa, lru_ba, lru_wx, lru_bx, lru_lambda, w_out, norm_mix_post, norm_mlp_pre, w_up, w_down, norm_mlp_post, loss_target, m_norm_mix_pre, m_w_in, m_conv_ssm_w, m_conv_ssm_b, m_dt_bias, m_a_log, m_d_skip, m_ssm_norm, m_conv_lru_w, m_conv_lru_b, m_lru_wa, m_lru_ba, m_lru_wx, m_lru_bx, m_lru_lambda, m_w_out, m_norm_mix_post, m_norm_mlp_pre, m_w_up, m_w_down, m_norm_mlp_post, v_norm_mix_pre, v_w_in, v_conv_ssm_w, v_conv_ssm_b, v_dt_bias, v_a_log, v_d_skip, v_ssm_norm, v_conv_lru_w, v_conv_lru_b, v_lru_wa, v_lru_ba, v_lru_wx, v_lru_bx, v_lru_lambda, v_w_out, v_norm_mix_post, v_norm_mlp_pre, v_w_up, v_w_down, v_norm_mlp_post):
    vals = (norm_mix_pre, w_in, conv_ssm_w, conv_ssm_b, dt_bias, a_log, d_skip, ssm_norm, conv_lru_w, conv_lru_b, lru_wa, lru_ba, lru_wx, lru_bx, lru_lambda, w_out, norm_mix_post, norm_mlp_pre, w_up, w_down, norm_mlp_post)
    m_vals = (m_norm_mix_pre, m_w_in, m_conv_ssm_w, m_conv_ssm_b, m_dt_bias, m_a_log, m_d_skip, m_ssm_norm, m_conv_lru_w, m_conv_lru_b, m_lru_wa, m_lru_ba, m_lru_wx, m_lru_bx, m_lru_lambda, m_w_out, m_norm_mix_post, m_norm_mlp_pre, m_w_up, m_w_down, m_norm_mlp_post)
    v_vals = (v_norm_mix_pre, v_w_in, v_conv_ssm_w, v_conv_ssm_b, v_dt_bias, v_a_log, v_d_skip, v_ssm_norm, v_conv_lru_w, v_conv_lru_b, v_lru_wa, v_lru_ba, v_lru_wx, v_lru_bx, v_lru_lambda, v_w_out, v_norm_mix_post, v_norm_mlp_pre, v_w_up, v_w_down, v_norm_mlp_post)
    w = dict(zip(WEIGHTS, vals))
    m = dict(zip(WEIGHTS, m_vals))
    v = dict(zip(WEIGHTS, v_vals))
    me = 4 * lax.axis_index("x") + 2 * lax.axis_index("y") + lax.axis_index("c")

    bf = lambda t: t.astype(MXU)
    late = _late_pack(w)
    early = _all_gather(_early_pack(w, me), "early_weights_all_gather")
    late, early = lax.optimization_barrier((late, early))
    lw = {}
    lw["sems"], lw["src"], lw["land"], token = _exchange_start(late, None, PLAN_GATHER_ICI, "late_weights_ici_start")
    w_in_pt, conv_w = _early_unpack(early)
    full = {n: (conv_w[n] if n in conv_w else w[n][0]) for n in WEIGHTS if n not in BIG}
    full["norm_mix_pre"] = full["norm_mix_pre"] + token[0, 0]

    def after_ssd(after):
        src, land = _exchange_wait(lw["sems"], lw["src"], lw["land"], after, PLAN_GATHER_ICI, "late_weights_ici_wait")
        lw["sems"], lw["src"], lw["land"], tok = _exchange_start(src, land, PLAN_GATHER_D2D, "late_weights_d2d_start")
        return tok[0, 0]

    def late_weights(after):
        src, land = _exchange_wait(lw["sems"], lw["src"], lw["land"], after, PLAN_GATHER_D2D, "late_weights_d2d_wait")
        return _late_unpack(_own_slot(land, src))

    sent = {}

    def send_mlp_grads(d_w_up_t, d_w_down, d_w_out):
        src = jnp.concatenate([bf(d_w_up_t).reshape(N_DEV, -1, D), bf(d_w_down).reshape(N_DEV, -1, D),
                               bf(d_w_out).reshape(N_DEV, -1, D)], axis=1)
        sent["sems"], sent["src"], sent["land"], tok = _exchange_start(src, None, PLAN_SCATTER, "mlp_grads_start")
        return tok[0, 0]

    def send_late_grads(d_w_in_pt):
        src = _scatter_w_in_grad(d_w_in_pt)
        sent["sems2"], sent["src2"], sent["land2"], tok = _exchange_start(src, None, PLAN_SCATTER, "late_grads_start")
        return tok[0, 0]

    loss, grad_x, g = _local_step(x[0], loss_target[0], _prep_params(full, dict(w_in_pT=w_in_pt)), after_ssd,
                                  late_weights, send_mlp_grads, send_late_grads)
    loss = lax.psum(loss, ("x", "y", "c"))

    own = lambda src: lax.dynamic_index_in_dim(src, me, keepdims=False)
    out_g, out_d, out_m, out_v = {}, {}, {}, {}
    mlp_src, mlp_land = _exchange_wait(sent["sems"], sent["src"], sent["land"], grad_x, PLAN_SCATTER, "mlp_grads_wait")
    g_mlp = _slot_sum(_own_slot(mlp_land, own(mlp_src)), "slot_sum_mlp")
    fs = FF // N_DEV
    for n, gn in (("w_up", g_mlp[:fs].T[None]), ("w_down", g_mlp[fs:2 * fs][None]), ("w_out", g_mlp[2 * fs:][None])):
        out_g[n] = gn
        out_d[n], out_m[n], out_v[n] = _adam_big(w[n], gn, m[n], v[n], "adam_" + n)
    zrow = jnp.zeros((1, D), F32)
    pad16 = lambda a: jnp.pad(a, ((0, 0), (0, 128 - NH)))
    small_parts = [
        jnp.concatenate([g["norm_mix_pre"], g["ssm_norm"], g["conv_lru_b"], g["lru_lambda"], g["norm_mix_post"],
                         g["norm_mlp_pre"], g["norm_mlp_post"], zrow], axis=0),
        g["conv_ssm_b"],
        jnp.concatenate([pad16(g["dt_bias"]), pad16(g["a_log"]), pad16(g["d_skip"]), jnp.zeros((5, 128), F32)], axis=0),
        g["conv_ssm_w"], g["conv_lru_w"],
        jnp.concatenate([g["lru_wa"].reshape(D, 64), g["lru_wx"].reshape(D, 64)], axis=0),
        jnp.concatenate([g["lru_ba"].reshape(NH, 64), g["lru_bx"].reshape(NH, 64)], axis=0)]
    small = _pad_rows(jnp.concatenate([s.reshape(-1) for s in small_parts]), 8)
    small, _ = lax.optimization_barrier((small, out_v["w_out"]))
    small_all = _all_gather(small, "small_grads_all_gather")

    lg_src, lg_land = _exchange_wait(sent["sems2"], sent["src2"], sent["land2"], small_all, PLAN_SCATTER, "late_grads_wait")
    g_late = _slot_sum(_own_slot(lg_land, own(lg_src)), "slot_sum_late")
    gt = lax.dynamic_slice(g_late, (2 * me, 0), (IN_ROWS, D))
    dt_, mt_, vt_ = _adam_big(w["w_in"][0].T, gt, m["w_in"][0].T, v["w_in"][0].T, "adam_w_in")
    out_g["w_in"], out_d["w_in"], out_m["w_in"], out_v["w_in"] = gt.T[None], dt_.T[None], mt_.T[None], vt_.T[None]
    sflat = small_all.reshape(N_DEV, -1)
    groups = []
    off = 0
    for r, c in SMALL_GROUPS:
        groups.append(sflat[:, off:off + r * c].reshape(N_DEV, r, c))
        off += r * c
    groups[3] = lax.dynamic_slice_in_dim(groups[3], me * CONV_SSM_COLS, CONV_SSM_COLS, axis=2)
    groups[4] = lax.dynamic_slice_in_dim(groups[4], me * CONV_LRU_COLS, CONV_LRU_COLS, axis=2)
    wmv = [(w[n].reshape(s), m[n].reshape(s), v[n].reshape(s)) for n, s, _, _ in SMALL]
    res = _adam_small(groups, [(gi, r0) for _, _, gi, r0 in SMALL], wmv)
    for (n, _, _, _), (g_n, d_n, m_n, v_n) in zip(SMALL, res):
        shape = w[n].shape
        out_g[n], out_d[n], out_m[n], out_v[n] = (g_n.reshape(shape), d_n.reshape(shape), m_n.reshape(shape),
                                                  v_n.reshape(shape))
    return (loss, grad_x[None], *[out_g[n] for n in WEIGHTS], *[out_d[n] for n in WEIGHTS],
            *[out_m[n] for n in WEIGHTS], *[out_v[n] for n in WEIGHTS])
```

```python
import functools

import jax
import jax.numpy as jnp
from jax import lax
from jax.experimental import pallas as pl
from jax.experimental.pallas import tpu as pltpu

F32 = jnp.float32
MXU = jnp.bfloat16
HI = lax.Precision.HIGHEST
EPS = 1e-6

D = 1024
NH = 16
NS = 128
CH = 128
FF = 4096
NP = 7168
SEG = 1024
LRU_C = 8.0
N_DEV = 8

ADAM_LR, ADAM_B1, ADAM_B2, ADAM_EPS, ADAM_WD, ADAM_STEP = 0.001, 0.9, 0.999, 1e-08, 0.01, 10

VMEM_LIMIT = 56 * 1024 * 1024


def _cp(*sem):
    return pltpu.CompilerParams(dimension_semantics=sem, vmem_limit_bytes=VMEM_LIMIT)


def _nn(a, b):
    return jnp.dot(a.astype(MXU), b.astype(MXU), preferred_element_type=F32)


def _nt(a, b):
    return lax.dot_general(a.astype(MXU), b.astype(MXU), (((1,), (1,)), ((), ())), preferred_element_type=F32)


def _tn(a, b):
    return lax.dot_general(a.astype(MXU), b.astype(MXU), (((0,), (0,)), ((), ())), preferred_element_type=F32)


_sigmoid = jax.nn.sigmoid


def _silu(x):
    return x * _sigmoid(x)


def _dsilu(x):
    s = _sigmoid(x)
    return s + x * s * (1.0 - s)


def _softplus(x):
    return jnp.maximum(x, 0.0) + jnp.log(1.0 + jnp.exp(-jnp.abs(x)))


def _rms(x, g):
    r = lax.rsqrt(jnp.mean(x * x, axis=-1, keepdims=True) + EPS)
    return x * r * g


def _rms_bwd(x, g, dy):
    r = lax.rsqrt(jnp.mean(x * x, axis=-1, keepdims=True) + EPS)
    gdy = g * dy
    dx = r * gdy - x * (r * r * r) * jnp.mean(x * gdy, axis=-1, keepdims=True)
    return dx, dy * x * r


def _rowsum(x):
    return jnp.sum(x, axis=0, keepdims=True)


def _taps_past(cur, prev8):
    r_n, c_n = cur.shape
    row = lax.broadcasted_iota(jnp.int32, (r_n, c_n), 0)
    out = []
    for k in range(4):
        s = 3 - k
        if s == 0:
            out.append(cur)
            continue
        head = jnp.concatenate([pltpu.roll(prev8, s, 0), jnp.zeros((r_n - 8, c_n), F32)], axis=0)
        out.append(jnp.where(row < s, head, pltpu.roll(cur, s, 0)))
    return out


def _taps_future(cur, fut8):
    r_n, c_n = cur.shape
    row = lax.broadcasted_iota(jnp.int32, (r_n, c_n), 0)
    out = []
    for k in range(4):
        s = 3 - k
        if s == 0:
            out.append(cur)
            continue
        tail = jnp.concatenate([jnp.zeros((r_n - 8, c_n), F32), pltpu.roll(fut8, 8 - s, 0)], axis=0)
        out.append(jnp.where(row >= r_n - s, tail, pltpu.roll(cur, r_n - s, 0)))
    return out


def _conv_apply(taps, w, b):
    acc = taps[0] * w[0:1, :]
    for k in range(1, 4):
        acc = acc + taps[k] * w[k:k + 1, :]
    return acc + b


def _inproj(x, g0, w):
    t_n = x.shape[0]
    tm = min(t_n, 1024)

    def body(x_ref, g_ref, w_ref, p_ref, u_ref):
        @pl.when(pl.program_id(1) == 0)
        def _():
            u_ref[...] = _rms(x_ref[...], g_ref[...]).astype(MXU)

        p_ref[...] = lax.dot_general(u_ref[...], w_ref[...], (((1,), (1,)), ((), ())), preferred_element_type=F32)

    return pl.pallas_call(
        body, name="inproj", grid=(t_n // tm, NP // SEG),
        in_specs=[pl.BlockSpec((tm, D), lambda i, j: (i, 0)), pl.BlockSpec((1, D), lambda i, j: (0, 0)),
                  pl.BlockSpec((SEG, D), lambda i, j: (j, 0))],
        out_specs=[pl.BlockSpec((tm, SEG), lambda i, j: (i, j)), pl.BlockSpec((tm, D), lambda i, j: (i, 0))],
        out_shape=[jax.ShapeDtypeStruct((t_n, NP), F32), jax.ShapeDtypeStruct((t_n, D), MXU)],
        compiler_params=_cp("parallel", "arbitrary"),
    )(x, g0, w)


def _ssd_prep(dtraw, dtb, alog):
    l_n = dtraw.shape[0]
    r = lax.broadcasted_iota(jnp.int32, (l_n, l_n), 0)
    c = lax.broadcasted_iota(jnp.int32, (l_n, l_n), 1)
    tril = (r >= c).astype(F32)
    triu = (r <= c).astype(F32)
    eye = (r == c).astype(F32)
    dt = _softplus(dtraw + dtb)
    adt = dt * (-jnp.exp(alog))
    ac = jnp.dot(tril, adt, preferred_element_type=F32, precision=HI)
    tn = (((0,), (0,)), ((), ()))
    ac_t = lax.dot_general(adt, triu, tn, preferred_element_type=F32, precision=HI)
    dt_t = lax.dot_general(dt, eye, tn, preferred_element_type=F32, precision=HI)
    return dt, dt_t, ac, ac_t, _rowsum(adt)


def _ssd_pair(j, xp, bg, cg, sp, dt, dt_t, ac, ac_t, aend):
    l_n = xp.shape[0]
    lane = lax.broadcasted_iota(jnp.int32, (l_n, 128), 1)
    sub = lax.broadcasted_iota(jnp.int32, (128, l_n), 0)
    lane1 = lax.broadcasted_iota(jnp.int32, (1, 128), 1)
    tri = lax.broadcasted_iota(jnp.int32, (l_n, l_n), 0) >= lax.broadcasted_iota(jnp.int32, (l_n, l_n), 1)
    lo = lax.broadcasted_iota(jnp.int32, (l_n, 128), 1) < 64
    lo_s = lax.broadcasted_iota(jnp.int32, (128, 128), 1) < 64
    cb = _nt(cg, bg)
    cs = _nn(cg, sp)
    x2 = jnp.concatenate([jnp.where(lo, xp, 0.0), jnp.where(lo, 0.0, xp)], axis=0)
    ws_, bs_, eo, ee = [], [], [], []
    for e in range(2):
        h = 2 * j + e
        ac_l = jnp.sum(jnp.where(lane == h, ac, 0.0), axis=1, keepdims=True)
        dt_l = jnp.sum(jnp.where(lane == h, dt, 0.0), axis=1, keepdims=True)
        a_end = jnp.sum(jnp.where(lane1 == h, aend, 0.0), axis=1, keepdims=True)
        ac_s, dt_s = ac_t[h:h + 1, :], dt_t[h:h + 1, :]
        decay = jnp.exp(jnp.where(tri, ac_l - ac_s, -1e30))
        ws_.append(cb * decay * dt_s)
        bs_.append(bg * (jnp.exp(a_end - ac_l) * dt_l))
        eo.append(jnp.exp(ac_l))
        ee.append(jnp.exp(a_end))
    y = _nn(jnp.concatenate(ws_, axis=1), x2) + jnp.where(lo, eo[0], eo[1]) * cs
    s_new = _tn(jnp.concatenate(bs_, axis=0), x2) + jnp.where(lo_s, ee[0], ee[1]) * sp
    return y, s_new


def _ssd_post(y, xs, z, dsk, nrm):
    y = (y + dsk * xs) * _silu(z)
    half = D // 2
    ya, yb = y[:, :half], y[:, half:]
    ya = ya * lax.rsqrt(jnp.mean(ya * ya, axis=-1, keepdims=True) + EPS)
    yb = yb * lax.rsqrt(jnp.mean(yb * yb, axis=-1, keepdims=True) + EPS)
    return jnp.concatenate([ya, yb], axis=1) * nrm


def _proj_specs(rows, seg_ids, order):
    return [pl.BlockSpec((rows, SEG), functools.partial(lambda i, s: (order(i), s), s=s)) for s in seg_ids]


def _prev8_specs(rows, seg_ids, order):
    rb = rows // 8
    return [pl.BlockSpec((8, SEG), functools.partial(lambda i, s: (jnp.maximum(order(i) * rb - 1, 0), s), s=s))
            for s in seg_ids]


def _full(shape):
    return pl.BlockSpec(shape, lambda i: (0,) * len(shape))


def _ssd_fwd(proj, cwx, cwb, cbx, cbb, dtb, alog, dsk, nrm):
    t_n = proj.shape[0]
    n_c = t_n // CH
    fwd = lambda i: i

    def body(z_ref, xs_ref, bc_ref, xsp_ref, bcp_ref, cwx_ref, cwb_ref, cbx_ref, cbb_ref, dtb_ref, alog_ref,
             dsk_ref, nrm_ref, ya_ref, sprev_ref, yraw_ref, s_ref):
        c = pl.program_id(0)

        @pl.when(c == 0)
        def _():
            s_ref[...] = jnp.zeros_like(s_ref)

        keep = jnp.where(c == 0, 0.0, 1.0)
        xs_pre = _conv_apply(_taps_past(xs_ref[...], xsp_ref[...] * keep), cwx_ref[...], cbx_ref[...])
        bc_pre = _conv_apply(_taps_past(bc_ref[:, :512], bcp_ref[:, :512] * keep), cwb_ref[...], cbb_ref[...])
        prep = _ssd_prep(bc_ref[:, 512:640], dtb_ref[...], alog_ref[...])
        xs = _silu(xs_pre)
        bc = _silu(bc_pre)
        sprev_ref[0] = s_ref[...]
        ys = []
        for j in range(NH // 2):
            g = j // 4
            yp, sn = _ssd_pair(j, xs[:, 128 * j:128 * j + 128], bc[:, 128 * g:128 * g + 128],
                               bc[:, 256 + 128 * g:384 + 128 * g], s_ref[:, 128 * j:128 * j + 128], *prep)
            ys.append(yp)
            s_ref[:, 128 * j:128 * j + 128] = sn
        y = jnp.concatenate(ys, axis=1)
        yraw_ref[...] = y
        ya_ref[...] = _ssd_post(y, xs, z_ref[...], dsk_ref[...], nrm_ref[...]).astype(ya_ref.dtype)

    return pl.pallas_call(
        body, name="ssd_fwd", grid=(n_c,),
        in_specs=_proj_specs(CH, (0, 5, 6), fwd) + _prev8_specs(CH, (5, 6), fwd) + [
            _full((4, D)), _full((4, 512)), _full((1, D)), _full((1, 512)), _full((1, 128)), _full((1, 128)),
            _full((1, D)), _full((1, D))],
        out_specs=[pl.BlockSpec((CH, D), lambda i: (i, 0)), pl.BlockSpec((1, NS, D), lambda i: (i, 0, 0)),
                   pl.BlockSpec((CH, D), lambda i: (i, 0))],
        out_shape=[jax.ShapeDtypeStruct((t_n, D), MXU), jax.ShapeDtypeStruct((n_c, NS, D), F32),
                   jax.ShapeDtypeStruct((t_n, D), F32)],
        scratch_shapes=[pltpu.VMEM((NS, D), F32)],
        compiler_params=_cp("arbitrary"),
    )(proj, proj, proj, proj, proj, cwx, cwb, cbx, cbb, dtb, alog, dsk, nrm)


def _ssd_bwd(dya, yraw, proj, sprev, cwx, cwb, cbx, cbb, dtb, alog, dsk, nrm, mlp_ops):
    t_n = proj.shape[0]
    n_c = t_n // CH
    rev = lambda i: n_c - 1 - i
    fb = FF // n_c

    def body(dya_ref, yraw_ref, z_ref, xs_ref, bc_ref, xsp_ref, bcp_ref, sprev_ref, cwx_ref, cwb_ref, cbx_ref, cbb_ref,
             dtb_ref, alog_ref, dsk_ref, nrm_ref, hid_ref, dff_ref, dhp_ref, v_ref,
             dz_ref, dxs_ref, dbc_ref, dcwx_ref, dcwb_ref, dcbx_ref, dcbb_ref, ddtb_ref, dalog_ref, ddsk_ref,
             dnrm_ref, dwd_ref, dwu_ref, ds_ref, futx_ref, futb_ref):
        i = pl.program_id(0)
        acc_refs = (dcwx_ref, dcwb_ref, dcbx_ref, dcbb_ref, ddtb_ref, dalog_ref, ddsk_ref, dnrm_ref)
        tn = (((0,), (0,)), ((), ()))
        dwd_ref[...] = lax.dot_general(hid_ref[...], dff_ref[...], tn, preferred_element_type=F32).astype(MXU)
        dwu_ref[...] = lax.dot_general(dhp_ref[...], v_ref[...], tn, preferred_element_type=F32).astype(MXU)

        @pl.when(i == 0)
        def _():
            for r in (ds_ref, futx_ref, futb_ref) + acc_refs:
                r[...] = jnp.zeros_like(r)

        keep = jnp.where(i == n_c - 1, 0.0, 1.0)
        taps_x = _taps_past(xs_ref[...], xsp_ref[...] * keep)
        taps_b = _taps_past(bc_ref[:, :512], bcp_ref[:, :512] * keep)
        xs_pre = _conv_apply(taps_x, cwx_ref[...], cbx_ref[...])
        bc_pre = _conv_apply(taps_b, cwb_ref[...], cbb_ref[...])
        xs = _silu(xs_pre)
        bc = _silu(bc_pre)
        prep, prep_vjp = jax.vjp(_ssd_prep, bc_ref[:, 512:640], dtb_ref[...], alog_ref[...])
        s_in = sprev_ref[0]

        def pair_args(j):
            g = j // 4
            return (xs[:, 128 * j:128 * j + 128], bc[:, 128 * g:128 * g + 128],
                    bc[:, 256 + 128 * g:384 + 128 * g], s_in[:, 128 * j:128 * j + 128]) + tuple(prep)

        _, post_vjp = jax.vjp(_ssd_post, yraw_ref[...], xs, z_ref[...], dsk_ref[...], nrm_ref[...])
        dy, dxs_skip, dz, ddsk, dnrm = post_vjp(dya_ref[...])
        dz_ref[...] = dz.astype(dz_ref.dtype)
        ddsk_ref[...] += ddsk
        dnrm_ref[...] += dnrm

        dprep = [jnp.zeros_like(p) for p in prep]
        dxp = []
        dbg = [jnp.zeros((CH, 128), F32), jnp.zeros((CH, 128), F32)]
        dcg = [jnp.zeros((CH, 128), F32), jnp.zeros((CH, 128), F32)]
        for j in range(NH // 2):
            g = j // 4
            _, pair_vjp = jax.vjp(functools.partial(_ssd_pair, j), *pair_args(j))
            cts = pair_vjp((dy[:, 128 * j:128 * j + 128], ds_ref[:, 128 * j:128 * j + 128]))
            dxp.append(cts[0])
            dbg[g] = dbg[g] + cts[1]
            dcg[g] = dcg[g] + cts[2]
            ds_ref[:, 128 * j:128 * j + 128] = cts[3]
            dprep = [a + b for a, b in zip(dprep, cts[4:])]
        ddtraw, ddtb, dalog = prep_vjp(tuple(dprep))
        ddtb_ref[...] += ddtb
        dalog_ref[...] += dalog

        dxs_pre = (dxs_skip + jnp.concatenate(dxp, axis=1)) * _dsilu(xs_pre)
        dbc_pre = jnp.concatenate([dbg[0], dbg[1], dcg[0], dcg[1]], axis=1) * _dsilu(bc_pre)
        dcbx_ref[...] += _rowsum(dxs_pre)
        dcbb_ref[...] += _rowsum(dbc_pre)
        for k in range(4):
            dcwx_ref[k:k + 1, :] += _rowsum(dxs_pre * taps_x[k])
            dcwb_ref[k:k + 1, :] += _rowsum(dbc_pre * taps_b[k])
        fx = _taps_future(dxs_pre, futx_ref[...])
        fb = _taps_future(dbc_pre, futb_ref[...])
        cwx = cwx_ref[...]
        cwb = cwb_ref[...]
        dxs_in = fx[0] * cwx[0:1, :]
        dbc_in = fb[0] * cwb[0:1, :]
        for k in range(1, 4):
            dxs_in = dxs_in + fx[k] * cwx[k:k + 1, :]
            dbc_in = dbc_in + fb[k] * cwb[k:k + 1, :]
        futx_ref[...] = dxs_pre[0:8, :]
        futb_ref[...] = dbc_pre[0:8, :]
        dxs_ref[...] = dxs_in.astype(dxs_ref.dtype)
        dbc_ref[...] = jnp.concatenate([dbc_in, ddtraw, jnp.zeros((CH, SEG - 640), F32)], axis=1).astype(dbc_ref.dtype)

    row_out = lambda: pl.BlockSpec((CH, D), lambda i: (rev(i), 0))
    outs = pl.pallas_call(
        body, name="ssd_bwd", grid=(n_c,),
        in_specs=[row_out(), row_out()] + _proj_specs(CH, (0, 5, 6), rev)
        + _prev8_specs(CH, (5, 6), rev) + [pl.BlockSpec((1, NS, D), lambda i: (rev(i), 0, 0)),
                                            _full((4, D)), _full((4, 512)), _full((1, D)), _full((1, 512)),
                                            _full((1, 128)), _full((1, 128)), _full((1, D)), _full((1, D)),
                                            pl.BlockSpec((t_n, fb), lambda i: (0, i)), _full((t_n, D)),
                                            pl.BlockSpec((t_n, fb), lambda i: (0, i)), _full((t_n, D))],
        out_specs=[row_out(), row_out(), row_out(), _full((4, D)), _full((4, 512)), _full((1, D)), _full((1, 512)),
                   _full((1, 128)), _full((1, 128)), _full((1, D)), _full((1, D)),
                   pl.BlockSpec((fb, D), lambda i: (i, 0)), pl.BlockSpec((fb, D), lambda i: (i, 0))],
        out_shape=[jax.ShapeDtypeStruct((t_n, D), MXU)] * 3 + [
            jax.ShapeDtypeStruct(s, F32) for s in ((4, D), (4, 512), (1, D), (1, 512), (1, 128), (1, 128), (1, D), (1, D))]
        + [jax.ShapeDtypeStruct((FF, D), MXU)] * 2,
        scratch_shapes=[pltpu.VMEM((NS, D), F32), pltpu.VMEM((8, D), F32), pltpu.VMEM((8, 512), F32)],
        compiler_params=_cp("arbitrary"),
    )(dya, yraw, proj, proj, proj, proj, proj, sprev, cwx, cwb, cbx, cbb, dtb, alog, dsk, nrm, *mlp_ops)
    return outs


LRU_ROWS = 256
LRU_BLK = 256


def _lru_gates(xr, wa, wx, ba, bx, lam):
    pr = jnp.concatenate([_nn(xr[:, LRU_BLK * b:LRU_BLK * (b + 1)], wa[b]) for b in range(D // LRU_BLK)], axis=1) + ba
    pi = jnp.concatenate([_nn(xr[:, LRU_BLK * b:LRU_BLK * (b + 1)], wx[b]) for b in range(D // LRU_BLK)], axis=1) + bx
    log_a = -LRU_C * _sigmoid(pr) * _softplus(-lam)
    a = jnp.exp(log_a)
    mult = jnp.sqrt(1.0 - jnp.exp(2.0 * log_a))
    return a, mult * (_sigmoid(pi) * xr)


def _lru_out(h, g):
    return h * jax.nn.gelu(g, approximate=True)


def _lru_fwd(proj, cw, cb, wa, wx, ba, bx, lam):
    t_n = proj.shape[0]
    rows = min(LRU_ROWS, t_n)
    fwd = lambda i: i

    def body(g_ref, x_ref, xp_ref, cw_ref, cb_ref, wa_ref, wx_ref, ba_ref, bx_ref, lam_ref, yb_ref, h_ref,
             a_s, u_s, carry):
        i = pl.program_id(0)

        @pl.when(i == 0)
        def _():
            carry[...] = jnp.zeros_like(carry)

        keep = jnp.where(i == 0, 0.0, 1.0)
        xr = _conv_apply(_taps_past(x_ref[...], xp_ref[...] * keep), cw_ref[...], cb_ref[...])
        a, u = _lru_gates(xr, wa_ref[...], wx_ref[...], ba_ref[...], bx_ref[...], lam_ref[...])
        a_s[...] = a
        u_s[...] = u
        row = lax.broadcasted_iota(jnp.int32, (8, D), 0)

        def blk(b, c):
            s = pl.multiple_of(b * 8, 8)
            av = a_s[pl.ds(s, 8), :]
            uv = u_s[pl.ds(s, 8), :]
            for d in (1, 2, 4):
                m = row >= d
                uv = uv + av * jnp.where(m, pltpu.roll(uv, d, 0), 0.0)
                av = av * jnp.where(m, pltpu.roll(av, d, 0), 1.0)
            hv = uv + av * c
            h_ref[pl.ds(s, 8), :] = hv
            return hv[7:8, :]

        carry[0:1, :] = lax.fori_loop(0, rows // 8, blk, carry[0:1, :])
        yb_ref[...] = _lru_out(h_ref[...], g_ref[...]).astype(yb_ref.dtype)

    return pl.pallas_call(
        body, name="lru_fwd", grid=(t_n // rows,),
        in_specs=_proj_specs(rows, (1, 2), fwd) + _prev8_specs(rows, (2,), fwd) + [
            _full((4, D)), _full((1, D)), _full((4, LRU_BLK, LRU_BLK)), _full((4, LRU_BLK, LRU_BLK)),
            _full((1, D)), _full((1, D)), _full((1, D))],
        out_specs=[pl.BlockSpec((rows, D), lambda i: (i, 0)), pl.BlockSpec((rows, D), lambda i: (i, 0))],
        out_shape=[jax.ShapeDtypeStruct((t_n, D), MXU), jax.ShapeDtypeStruct((t_n, D), F32)],
        scratch_shapes=[pltpu.VMEM((rows, D), F32), pltpu.VMEM((rows, D), F32), pltpu.VMEM((8, D), F32)],
        compiler_params=_cp("arbitrary"),
    )(proj, proj, proj, cw, cb, wa, wx, ba, bx, lam)


def _lru_bwd(dyb, proj, h, cw, cb, wa, wx, ba, bx, lam):
    t_n = proj.shape[0]
    rows = min(LRU_ROWS, t_n)
    n_t = t_n // rows
    rev = lambda i: n_t - 1 - i
    rb = rows // 8

    def body(dyb_ref, g_ref, x_ref, xp_ref, h_ref, hp_ref, cw_ref, cb_ref, wa_ref, wx_ref, ba_ref, bx_ref, lam_ref,
             dg_ref, dx_ref, dcw_ref, dcb_ref, dwa_ref, dwx_ref, dba_ref, dbx_ref, dlam_ref,
             a_s, dh_s, hx_s, da_s, du_s, carry, fut):
        i = pl.program_id(0)
        acc_refs = (dcw_ref, dcb_ref, dwa_ref, dwx_ref, dba_ref, dbx_ref, dlam_ref)

        @pl.when(i == 0)
        def _():
            for r in (carry, fut) + acc_refs:
                r[...] = jnp.zeros_like(r)

        keep = jnp.where(i == n_t - 1, 0.0, 1.0)
        taps = _taps_past(x_ref[...], xp_ref[...] * keep)
        xr = _conv_apply(taps, cw_ref[...], cb_ref[...])
        gate_in = (xr, wa_ref[...], wx_ref[...], ba_ref[...], bx_ref[...], lam_ref[...])
        (a, _), gates_vjp = jax.vjp(_lru_gates, *gate_in)
        _, out_vjp = jax.vjp(_lru_out, h_ref[...], g_ref[...])
        dh, dg = out_vjp(dyb_ref[...])
        dg_ref[...] = dg.astype(dg_ref.dtype)
        a_s[...] = a
        dh_s[...] = dh
        hx_s[0:8, :] = hp_ref[...] * keep
        hx_s[8:, :] = h_ref[...]
        row = lax.broadcasted_iota(jnp.int32, (8, D), 0)

        def blk(b, c):
            s = pl.multiple_of((rb - 1 - b) * 8, 8)
            av = a_s[pl.ds(s, 8), :]
            dhv = dh_s[pl.ds(s, 8), :]
            a0 = av
            kv = av * dhv
            for d in (1, 2, 4):
                m = row <= 7 - d
                kv = kv + av * jnp.where(m, pltpu.roll(kv, 8 - d, 0), 0.0)
                av = av * jnp.where(m, pltpu.roll(av, 8 - d, 0), 1.0)
            kv = kv + av * c
            gv = dhv + jnp.where(row < 7, pltpu.roll(kv, 7, 0), c)
            hb = hx_s[pl.ds(s + 8, 8), :]
            hpv = hx_s[pl.ds(s, 8), :]
            hprev = jnp.where(row >= 1, pltpu.roll(hb, 1, 0), hpv[7:8, :])
            du_s[pl.ds(s, 8), :] = gv
            da_s[pl.ds(s, 8), :] = gv * hprev
            del a0
            return kv[0:1, :]

        carry[0:1, :] = lax.fori_loop(0, rb, blk, carry[0:1, :])
        dxr, dwa, dwx, dba, dbx, dlam = gates_vjp((da_s[...], du_s[...]))
        dwa_ref[...] += dwa
        dwx_ref[...] += dwx
        dba_ref[...] += dba
        dbx_ref[...] += dbx
        dlam_ref[...] += dlam
        dcb_ref[...] += _rowsum(dxr)
        for k in range(4):
            dcw_ref[k:k + 1, :] += _rowsum(dxr * taps[k])
        ft = _taps_future(dxr, fut[...])
        cwv = cw_ref[...]
        dx = ft[0] * cwv[0:1, :]
        for k in range(1, 4):
            dx = dx + ft[k] * cwv[k:k + 1, :]
        fut[...] = dxr[0:8, :]
        dx_ref[...] = dx.astype(dx_ref.dtype)

    row_in = lambda: pl.BlockSpec((rows, D), lambda i: (rev(i), 0))
    prev_h = pl.BlockSpec((8, D), lambda i: (jnp.maximum(rev(i) * rb - 1, 0), 0))
    wspec = lambda: _full((4, LRU_BLK, LRU_BLK))
    return pl.pallas_call(
        body, name="lru_bwd", grid=(n_t,),
        in_specs=[row_in()] + _proj_specs(rows, (1, 2), rev) + _prev8_specs(rows, (2,), rev) + [row_in(), prev_h] + [
            _full((4, D)), _full((1, D)), wspec(), wspec(), _full((1, D)), _full((1, D)), _full((1, D))],
        out_specs=[row_in(), row_in(), _full((4, D)), _full((1, D)), wspec(), wspec(), _full((1, D)), _full((1, D)),
                   _full((1, D))],
        out_shape=[jax.ShapeDtypeStruct((t_n, D), MXU)] * 2 + [
            jax.ShapeDtypeStruct(s, F32) for s in ((4, D), (1, D), (4, LRU_BLK, LRU_BLK), (4, LRU_BLK, LRU_BLK),
                                                   (1, D), (1, D), (1, D))],
        scratch_shapes=[pltpu.VMEM((rows, D), F32), pltpu.VMEM((rows, D), F32), pltpu.VMEM((rows + 8, D), F32),
                        pltpu.VMEM((rows, D), F32), pltpu.VMEM((rows, D), F32), pltpu.VMEM((8, D), F32),
                        pltpu.VMEM((8, D), F32)],
        compiler_params=_cp("arbitrary"),
    )(dyb, proj, proj, proj, h, h, cw, cb, wa, wx, ba, bx, lam)


def _merge_out(ya, yb, proj, x, wout, g1):
    t_n = x.shape[0]
    tm = min(t_n, 512)

    def body(ya_ref, yb_ref, ga_ref, gb_ref, x_ref, w_ref, g_ref, h1_ref, mix_ref, mg_ref):
        merged = (_sigmoid(ga_ref[...]) * ya_ref[...].astype(F32)
                  + _sigmoid(gb_ref[...]) * yb_ref[...].astype(F32))
        mg = merged.astype(MXU)
        mg_ref[...] = mg
        mix = jnp.dot(mg, w_ref[...], preferred_element_type=F32)
        mix_ref[...] = mix
        h1_ref[...] = x_ref[...] + _rms(mix, g_ref[...])

    row = lambda: pl.BlockSpec((tm, D), lambda i: (i, 0))
    return pl.pallas_call(
        body, name="merge_out", grid=(t_n // tm,),
        in_specs=[row(), row()] + _proj_specs(tm, (3, 4), lambda i: i) + [row(), _full((D, D)), _full((1, D))],
        out_specs=[row(), row(), row()],
        out_shape=[jax.ShapeDtypeStruct((t_n, D), F32), jax.ShapeDtypeStruct((t_n, D), F32),
                   jax.ShapeDtypeStruct((t_n, D), MXU)],
        compiler_params=_cp("parallel"),
    )(ya, yb, proj, proj, x, wout, g1)


def _out_bwd(dv, h1, g2, dout, mix, ya, yb, proj, wout, g1):
    t_n = dv.shape[0]
    tm = min(t_n, 256)

    def body(dv_ref, h1_ref, g2_ref, dout_ref, mix_ref, ya_ref, yb_ref, ga_ref, gb_ref, w_ref, g_ref,
             dh1_ref, dmix_ref, dya_ref, dyb_ref, dga_ref, dgb_ref, dg2_ref, dg1_ref):
        @pl.when(pl.program_id(0) == 0)
        def _():
            dg1_ref[...] = jnp.zeros_like(dg1_ref)
            dg2_ref[...] = jnp.zeros_like(dg2_ref)

        dx, dg_rows = _rms_bwd(h1_ref[...], g2_ref[...], dv_ref[...])
        dg2_ref[...] += _rowsum(dg_rows)
        dh1 = dout_ref[...] + dx
        dh1_ref[...] = dh1
        dmix, dg_rows = _rms_bwd(mix_ref[...], g_ref[...], dh1)
        dg1_ref[...] += _rowsum(dg_rows)
        dmix_b = dmix.astype(MXU)
        dmix_ref[...] = dmix_b
        dmg = lax.dot_general(dmix_b, w_ref[...], (((1,), (1,)), ((), ())), preferred_element_type=F32)
        sa = _sigmoid(ga_ref[...])
        sb = _sigmoid(gb_ref[...])
        dya_ref[...] = dmg * sa
        dyb_ref[...] = dmg * sb
        dga_ref[...] = (dmg * ya_ref[...].astype(F32) * sa * (1.0 - sa)).astype(MXU)
        dgb_ref[...] = (dmg * yb_ref[...].astype(F32) * sb * (1.0 - sb)).astype(MXU)

    row = lambda: pl.BlockSpec((tm, D), lambda i: (i, 0))
    vec = lambda: _full((1, D))
    return pl.pallas_call(
        body, name="out_bwd", grid=(t_n // tm,),
        in_specs=[row(), row(), vec(), row(), row(), row(), row()] + _proj_specs(tm, (3, 4), lambda i: i)
        + [_full((D, D)), vec()],
        out_specs=[row(), row(), row(), row(), row(), row(), vec(), vec()],
        out_shape=[jax.ShapeDtypeStruct((t_n, D), F32), jax.ShapeDtypeStruct((t_n, D), MXU),
                   jax.ShapeDtypeStruct((t_n, D), F32), jax.ShapeDtypeStruct((t_n, D), F32),
                   jax.ShapeDtypeStruct((t_n, D), MXU), jax.ShapeDtypeStruct((t_n, D), MXU),
                   jax.ShapeDtypeStruct((1, D), F32), jax.ShapeDtypeStruct((1, D), F32)],
        compiler_params=_cp("arbitrary"),
    )(dv, h1, g2, dout, mix, ya, yb, proj, proj, wout, g1)


MLP_TM = 1024
MLP_TF_FWD = 512
MLP_TF_BWD = 1024


def _mlp_fwd(h1, g2, wup, wdown, g3, tgt):
    t_n = h1.shape[0]
    tm = min(t_n, MLP_TM)
    n_f = FF // MLP_TF_FWD

    def body(h1_ref, g2_ref, wu_ref, wd_ref, g3_ref, tgt_ref, hp_ref, v_ref, dout_ref, dff_ref, loss_ref, dg3_ref, acc):
        i, j = pl.program_id(0), pl.program_id(1)

        @pl.when((i == 0) & (j == 0))
        def _():
            loss_ref[...] = jnp.zeros_like(loss_ref)
            dg3_ref[...] = jnp.zeros_like(dg3_ref)

        @pl.when(j == 0)
        def _():
            v_ref[...] = _rms(h1_ref[...], g2_ref[...]).astype(MXU)
            acc[...] = jnp.zeros_like(acc)

        hp = lax.dot_general(v_ref[...], wu_ref[...], (((1,), (1,)), ((), ())), preferred_element_type=F32)
        hp_ref[...] = hp.astype(MXU)
        hid = jnp.square(jnp.maximum(hp, 0.0))
        acc[...] += jnp.dot(hid.astype(MXU), wd_ref[...], preferred_element_type=F32)

        @pl.when(j == n_f - 1)
        def _():
            ff = acc[...]
            err = h1_ref[...] + _rms(ff, g3_ref[...]) - tgt_ref[...]
            loss_ref[...] += 0.5 * jnp.sum(jnp.mean(err * err, axis=-1, keepdims=True), axis=0, keepdims=True)
            dout = err * (1.0 / D)
            dout_ref[...] = dout
            dff, dg_rows = _rms_bwd(ff, g3_ref[...], dout)
            dg3_ref[...] += _rowsum(dg_rows)
            dff_ref[...] = dff.astype(MXU)

    row = lambda: pl.BlockSpec((tm, D), lambda i, j: (i, 0))
    vec = lambda: pl.BlockSpec((1, D), lambda i, j: (0, 0))
    return pl.pallas_call(
        body, name="mlp_fwd", grid=(t_n // tm, n_f),
        in_specs=[row(), vec(), pl.BlockSpec((MLP_TF_FWD, D), lambda i, j: (j, 0)),
                  pl.BlockSpec((MLP_TF_FWD, D), lambda i, j: (j, 0)), vec(), row()],
        out_specs=[pl.BlockSpec((tm, MLP_TF_FWD), lambda i, j: (i, j)), row(), row(), row(),
                   pl.BlockSpec((1, 1), lambda i, j: (0, 0)), vec()],
        out_shape=[jax.ShapeDtypeStruct((t_n, FF), MXU), jax.ShapeDtypeStruct((t_n, D), MXU),
                   jax.ShapeDtypeStruct((t_n, D), F32), jax.ShapeDtypeStruct((t_n, D), MXU),
                   jax.ShapeDtypeStruct((1, 1), F32), jax.ShapeDtypeStruct((1, D), F32)],
        scratch_shapes=[pltpu.VMEM((tm, D), F32)],
        compiler_params=_cp("arbitrary", "arbitrary"),
    )(h1, g2, wup, wdown, g3, tgt)


def _mlp_bwd(dff, hp, wup, wdown):
    t_n = dff.shape[0]
    tm = min(t_n, MLP_TM)
    n_f = FF // MLP_TF_BWD

    def mm_body(dff_ref, hp_ref, wu_ref, wd_ref, dv_ref, dhp_ref, hid_ref):
        @pl.when(pl.program_id(1) == 0)
        def _():
            dv_ref[...] = jnp.zeros_like(dv_ref)

        relu = jnp.maximum(hp_ref[...].astype(F32), 0.0)
        hid_ref[...] = jnp.square(relu).astype(MXU)
        dhid = lax.dot_general(dff_ref[...], wd_ref[...], (((1,), (1,)), ((), ())), preferred_element_type=F32)
        dhp = (dhid * (2.0 * relu)).astype(MXU)
        dhp_ref[...] = dhp
        dv_ref[...] += jnp.dot(dhp, wu_ref[...], preferred_element_type=F32)

    row = lambda: pl.BlockSpec((tm, D), lambda i, j: (i, 0))
    blk = lambda: pl.BlockSpec((tm, MLP_TF_BWD), lambda i, j: (i, j))
    wblk = lambda: pl.BlockSpec((MLP_TF_BWD, D), lambda i, j: (j, 0))
    return pl.pallas_call(
        mm_body, name="mlp_bwd", grid=(t_n // tm, n_f),
        in_specs=[row(), blk(), wblk(), wblk()], out_specs=[row(), blk(), blk()],
        out_shape=[jax.ShapeDtypeStruct((t_n, D), F32), jax.ShapeDtypeStruct((t_n, FF), MXU),
                   jax.ShapeDtypeStruct((t_n, FF), MXU)],
        compiler_params=_cp("parallel", "arbitrary"),
    )(dff, hp, wup, wdown)


def _wgrad(a, g, name):
    t_n, k_n = a.shape
    n_n = g.shape[1]
    tt = min(t_n, 1024)
    tk, tn = min(k_n, 1024), min(n_n, 1024)

    n_t = t_n // tt

    def body(a_ref, g_ref, o_ref, acc):
        t = pl.program_id(2)

        @pl.when(t == 0)
        def _():
            acc[...] = jnp.zeros_like(acc)

        acc[...] += lax.dot_general(a_ref[...], g_ref[...], (((0,), (0,)), ((), ())), preferred_element_type=F32)

        @pl.when(t == n_t - 1)
        def _():
            o_ref[...] = acc[...].astype(o_ref.dtype)

    return pl.pallas_call(
        body, name=name, grid=(k_n // tk, n_n // tn, n_t),
        in_specs=[pl.BlockSpec((tt, tk), lambda k, n, t: (t, k)), pl.BlockSpec((tt, tn), lambda k, n, t: (t, n))],
        out_specs=pl.BlockSpec((tk, tn), lambda k, n, t: (k, n)),
        out_shape=jax.ShapeDtypeStruct((k_n, n_n), MXU),
        scratch_shapes=[pltpu.VMEM((tk, tn), F32)],
        compiler_params=_cp("parallel", "parallel", "arbitrary"),
    )(a, g)


def _wgrad_segs(segs, g, name):
    t_n, n_n = g.shape
    n_s = len(segs)
    tt = min(t_n, 1024)
    n_t = t_n // tt

    def body(*refs):
        a_refs = refs[:n_s]
        g_ref, o_ref, acc = refs[n_s:]
        s_id, t = pl.program_id(0), pl.program_id(1)

        @pl.when(t == 0)
        def _():
            acc[...] = jnp.zeros_like(acc)

        for s in range(n_s):
            @pl.when(s_id == s)
            def _(s=s):
                acc[...] += lax.dot_general(a_refs[s][...], g_ref[...], (((0,), (0,)), ((), ())),
                                            preferred_element_type=F32)

        @pl.when(t == n_t - 1)
        def _():
            o_ref[...] = acc[...].astype(o_ref.dtype)

    seg_spec = lambda s: pl.BlockSpec((tt, SEG), lambda i, t: (jnp.where(i == s, t, jnp.where(i < s, 0, n_t - 1)), 0))
    return pl.pallas_call(
        body, name=name, grid=(n_s, n_t),
        in_specs=[seg_spec(s) for s in range(n_s)] + [pl.BlockSpec((tt, n_n), lambda i, t: (t, 0))],
        out_specs=pl.BlockSpec((SEG, n_n), lambda i, t: (i, 0)),
        out_shape=jax.ShapeDtypeStruct((n_s * SEG, n_n), MXU),
        scratch_shapes=[pltpu.VMEM((SEG, n_n), F32)],
        compiler_params=_cp("arbitrary", "arbitrary"),
    )(*segs, g)


def _inproj_bwd(dsegs, w, x, g0, dh1):
    t_n = x.shape[0]
    tm = min(t_n, 1024)
    n_k = NP // SEG

    def mm_body(*refs):
        dp_refs = refs[:n_k]
        w_ref, du_ref = refs[n_k:]
        k = pl.program_id(1)

        @pl.when(k == 0)
        def _():
            du_ref[...] = jnp.zeros_like(du_ref)

        for s in range(n_k):
            @pl.when(k == s)
            def _(s=s):
                du_ref[...] += jnp.dot(dp_refs[s][...], w_ref[...], preferred_element_type=F32)

    du = pl.pallas_call(
        mm_body, name="inproj_bwd", grid=(t_n // tm, n_k),
        in_specs=[pl.BlockSpec((tm, SEG), functools.partial(
            lambda i, k, s: (jnp.where(k >= s, i, jnp.maximum(i - 1, 0)), 0), s=s)) for s in range(n_k)] + [
            pl.BlockSpec((SEG, D), lambda i, k: (k, 0))],
        out_specs=pl.BlockSpec((tm, D), lambda i, k: (i, 0)),
        out_shape=jax.ShapeDtypeStruct((t_n, D), F32),
        compiler_params=_cp("parallel", "arbitrary"),
    )(*dsegs, w)

    tf = min(t_n, 512)

    def fin_body(du_ref, x_ref, g_ref, dh1_ref, dx_ref, dg0_ref):
        @pl.when(pl.program_id(0) == 0)
        def _():
            dg0_ref[...] = jnp.zeros_like(dg0_ref)

        dx, dg_rows = _rms_bwd(x_ref[...], g_ref[...], du_ref[...])
        dg0_ref[...] += _rowsum(dg_rows)
        dx_ref[...] = dh1_ref[...] + dx

    row = lambda: pl.BlockSpec((tf, D), lambda i: (i, 0))
    return pl.pallas_call(
        fin_body, name="grad_x", grid=(t_n // tf,), in_specs=[row(), row(), _full((1, D)), row()],
        out_specs=[row(), _full((1, D))],
        out_shape=[jax.ShapeDtypeStruct((t_n, D), F32), jax.ShapeDtypeStruct((1, D), F32)],
        compiler_params=_cp("arbitrary"),
    )(du, x, g0, dh1)


def _blockdiag4(w):
    w4 = w.reshape(4, 4, 64, 1, 64).astype(MXU)
    same = (jnp.arange(4)[:, None, None, None] == jnp.arange(4)[None, None, :, None])
    return jnp.where(same[None], w4, jnp.zeros((), MXU)).reshape(4, 256, 256)


def _blockdiag4_extract(g):
    g5 = g.reshape(4, 4, 64, 4, 64)
    return jnp.stack([g5[:, q, :, q, :] for q in range(4)], axis=1).reshape(NH, 64, 64)


def _local_step(x, tgt, p, after_ssd=None, late_weights=None, send_mlp_grads=None, send_late_grads=None):
    f = lambda a: a.astype(F32)
    proj, u = _inproj(x, p["norm_mix_pre"], p["w_in_pT"])
    ssm_params = (p["cw_xs"], p["cw_bc"], p["cb_xs"], p["cb_bc"], p["dt_bias"], p["a_log"], p["d_skip_x"],
                  p["ssm_norm"])
    ya, sprev, yraw = _ssd_fwd(proj, *ssm_params)
    cb_lru = p["conv_lru_b"] if after_ssd is None else p["conv_lru_b"] + after_ssd(ya)
    lru_params = (p["conv_lru_w"], cb_lru, p["wa_bd"], p["wx_bd"], p["lru_ba"], p["lru_bx"], p["lru_lambda"])
    yb, h = _lru_fwd(proj, *lru_params)
    if late_weights is not None:
        p = dict(p, **late_weights(yb))
    h1, mix, merged = _merge_out(ya, yb, proj, x, p["w_out"], p["norm_mix_post"])
    hp, v, dout, dff, loss, dg3 = _mlp_fwd(h1, p["norm_mlp_pre"], p["w_upT"], p["w_down"], p["norm_mlp_post"], tgt)

    dv, dhp, hid = _mlp_bwd(dff, hp, p["w_upT"], p["w_down"])
    dh1, dmix, dya, dyb, dga, dgb, dg2, dg1 = _out_bwd(dv, h1, p["norm_mlp_pre"], dout, mix, ya, yb, proj, p["w_out"],
                                                     p["norm_mix_post"])
    d_w_out = _wgrad(merged, dmix, "wgrad_out")
    (dz, dxs, dbc, dcwx, dcwb, dcbx, dcbb, ddtb, dalog, ddsk, dnrm, d_w_down, d_w_up_t) = _ssd_bwd(
        dya, yraw, proj, sprev, *ssm_params, (hid, dff, dhp, v))
    if send_mlp_grads is not None:
        lru_params = (lru_params[0], lru_params[1] + send_mlp_grads(d_w_up_t, d_w_down, d_w_out)) + lru_params[2:]
    (dgl, dxl, dcwl, dcbl, dwa, dwx, dba, dbx, dlam) = _lru_bwd(dyb, proj, h, *lru_params)
    dsegs = [dz, dgl, dxl, dga, dgb, dxs, dbc]
    d_w_in_pt = _wgrad_segs(dsegs, u, "wgrad_in")
    g0 = p["norm_mix_pre"]
    if send_late_grads is not None:
        g0 = g0 + send_late_grads(d_w_in_pt)
    grad_x, dg0 = _inproj_bwd(dsegs, p["w_in_pT"], x, g0, dh1)
    grads = dict(
        norm_mix_pre=dg0, w_in_pT=d_w_in_pt, conv_ssm_w=jnp.concatenate([dcwx, dcwb], axis=1),
        conv_ssm_b=jnp.concatenate([dcbx, dcbb], axis=1), dt_bias=ddtb[:, :NH], a_log=dalog[:, :NH],
        d_skip=f(ddsk).reshape(NH, 64).sum(axis=1)[None, :], ssm_norm=dnrm, conv_lru_w=dcwl, conv_lru_b=dcbl,
        lru_wa=_blockdiag4_extract(dwa), lru_ba=dba, lru_wx=_blockdiag4_extract(dwx), lru_bx=dbx, lru_lambda=dlam,
        w_out=d_w_out, norm_mix_post=dg1, norm_mlp_pre=dg2, w_upT=d_w_up_t, w_down=d_w_down, norm_mlp_post=dg3)
    return loss[0, 0], grad_x, grads


W_IN_COLS = 6672


def _w_in_t_to_padded(wt):
    z, xs, bc, dt = wt[0:1024], wt[1024:2048], wt[2048:2560], wt[2560:2576]
    gl, xl, ga, gb = wt[2576:3600], wt[3600:4624], wt[4624:5648], wt[5648:6672]
    return jnp.concatenate([z, gl, xl, ga, gb, xs, bc, dt, jnp.zeros((NP - 6672, wt.shape[1]), wt.dtype)], axis=0)


def _w_in_t_from_padded(wp):
    z, gl, xl, ga, gb = (wp[SEG * s:SEG * (s + 1)] for s in range(5))
    xs, bc, dt = wp[5120:6144], wp[6144:6656], wp[6656:6672]
    return jnp.concatenate([z, xs, bc, dt, gl, xl, ga, gb], axis=0)


def _prep_params(full, big):
    f = lambda a: a.astype(F32)
    pad128 = lambda a: jnp.pad(f(a).reshape(1, -1), ((0, 0), (0, 128 - a.size)))
    cw = f(full["conv_ssm_w"])
    cb = f(full["conv_ssm_b"]).reshape(1, -1)
    return dict(
        big, norm_mix_pre=f(full["norm_mix_pre"]).reshape(1, D),
        cw_xs=cw[:, :D], cw_bc=cw[:, D:], cb_xs=cb[:, :D], cb_bc=cb[:, D:],
        dt_bias=pad128(full["dt_bias"]), a_log=pad128(full["a_log"]),
        d_skip_x=jnp.repeat(f(full["d_skip"]).reshape(-1), 64).reshape(1, D), ssm_norm=f(full["ssm_norm"]).reshape(1, D),
        conv_lru_w=f(full["conv_lru_w"]), conv_lru_b=f(full["conv_lru_b"]).reshape(1, D),
        wa_bd=_blockdiag4(full["lru_wa"]), wx_bd=_blockdiag4(full["lru_wx"]),
        lru_ba=f(full["lru_ba"]).reshape(1, D), lru_bx=f(full["lru_bx"]).reshape(1, D),
        lru_lambda=f(full["lru_lambda"]).reshape(1, D),
        norm_mix_post=f(full["norm_mix_post"]).reshape(1, D), norm_mlp_pre=f(full["norm_mlp_pre"]).reshape(1, D),
        norm_mlp_post=f(full["norm_mlp_post"]).reshape(1, D))


MESH_ID = pl.DeviceIdType.MESH
ANY = pl.BlockSpec(memory_space=pl.ANY)


def _my_place():
    x, y, c = lax.axis_index("x"), lax.axis_index("y"), lax.axis_index("c")
    return x, y, c, 4 * x + 2 * y + c


def _peer(x, y, c, k):
    return (x ^ ((k >> 2) & 1), y ^ ((k >> 1) & 1), c ^ (k & 1))


def _all_gather(pack, name):
    def body(in_ref, out_ref, send_sems, recv_sems, local_sem):
        x, y, c, me = _my_place()
        sibling = (x, y, 1 - c)
        flips = (4, 2, 6)

        def copy(j, block, to, src=None):
            return pltpu.make_async_remote_copy(
                src_ref=out_ref.at[block] if src is None else src, dst_ref=out_ref.at[block],
                send_sem=send_sems.at[j], recv_sem=recv_sems.at[j], device_id=to, device_id_type=MESH_ID)

        mine = pltpu.make_async_copy(in_ref, out_ref.at[me], local_sem)
        mine.start()
        first = [copy(0, me, sibling, src=in_ref)]
        first += [copy(1 + j, me, _peer(x, y, c, k), src=in_ref) for j, k in enumerate(flips)]
        for cp in first:
            cp.start()
        passed = [copy(4 + j, me ^ k, sibling) for j, k in enumerate(flips)]
        for j, k in enumerate(flips):
            copy(1 + j, me ^ k, (x, y, c)).wait_recv()
            passed[j].start()
        copy(0, me ^ 1, (x, y, c)).wait_recv()
        for j, k in enumerate(flips):
            copy(4 + j, me ^ k ^ 1, (x, y, c)).wait_recv()
        for cp in first + passed:
            cp.wait_send()
        mine.wait()

    return pl.pallas_call(
        body, name=name, in_specs=[ANY], out_specs=ANY,
        out_shape=jax.ShapeDtypeStruct((N_DEV,) + pack.shape, pack.dtype),
        scratch_shapes=[pltpu.SemaphoreType.DMA((N_DEV - 1,)), pltpu.SemaphoreType.DMA((N_DEV - 1,)),
                        pltpu.SemaphoreType.DMA],
    )(pack)


HBM = pl.BlockSpec(memory_space=pltpu.HBM)
SEM = pl.BlockSpec(memory_space=pltpu.SEMAPHORE)
PLAN_GATHER = tuple((k, "pack", 0) for k in range(1, N_DEV))
PLAN_SCATTER = tuple((k, "slot", 0) for k in range(1, N_DEV))
PLAN_GATHER_ICI = tuple((k, "pack", 0) for k in (2, 4, 6))
PLAN_GATHER_D2D = ((1, "pack", 0),) + tuple((1, s, s) for s in (2, 4, 6))


def _plan_copy(j, plan, src_ref, land_ref, sems):
    k, source, slot = plan[j]
    x, y, c, me = _my_place()
    if source == "pack":
        src = src_ref
    elif source == "slot":
        src = src_ref.at[me ^ k]
    else:
        src = land_ref.at[me ^ source]
    return pltpu.make_async_remote_copy(
        src_ref=src, dst_ref=land_ref.at[me ^ slot], send_sem=sems[j], recv_sem=sems[len(plan) + j],
        device_id=_peer(x, y, c, k), device_id_type=MESH_ID)


def _exchange_start(src, land, plan, name):
    n_c = len(plan)
    if land is None:
        land = pltpu.with_memory_space_constraint(lax.empty((N_DEV,) + src.shape[-2:], src.dtype), pltpu.HBM)

    def body(src_ref, land_ref, *rest):
        sems, token = rest[:2 * n_c], rest[2 * n_c + 2]
        for j in range(n_c):
            _plan_copy(j, plan, src_ref, land_ref, sems).start()
        token[...] = jnp.zeros_like(token)

    outs = pl.pallas_call(
        body, name=name,
        out_shape=(pltpu.SemaphoreType.DMA(()),) * (2 * n_c) + (
            pltpu.HBM(src.shape, src.dtype), pltpu.HBM(land.shape, land.dtype), jax.ShapeDtypeStruct((8, 128), F32)),
        in_specs=(HBM, HBM), out_specs=(SEM,) * (2 * n_c) + (HBM, HBM, pl.BlockSpec(memory_space=pltpu.VMEM)),
        input_output_aliases={0: 2 * n_c, 1: 2 * n_c + 1},
        compiler_params=pltpu.CompilerParams(has_side_effects=pltpu.SideEffectType.DATAFLOW_SIDE_EFFECTING),
    )(pltpu.with_memory_space_constraint(src, pltpu.HBM), land)
    return outs[:2 * n_c], outs[2 * n_c], outs[2 * n_c + 1], outs[2 * n_c + 2]


def _exchange_wait(sems, src_thru, land_thru, after, plan, name):
    n_c = len(plan)

    def body(src_ref, land_ref, *rest):
        for j in range(n_c):
            cp = _plan_copy(j, plan, src_ref, land_ref, rest[:2 * n_c])
            cp.wait_send()
            cp.wait_recv()

    return pl.pallas_call(
        body, name=name,
        out_shape=(pltpu.HBM(src_thru.shape, src_thru.dtype), pltpu.HBM(land_thru.shape, land_thru.dtype)),
        in_specs=(HBM, HBM) + (SEM,) * (2 * n_c) + (pl.BlockSpec(memory_space=pl.ANY),), out_specs=(HBM, HBM),
        input_output_aliases={0: 0, 1: 1},
        compiler_params=pltpu.CompilerParams(has_side_effects=pltpu.SideEffectType.DATAFLOW_SIDE_EFFECTING),
    )(src_thru, land_thru, *sems, after)


def _slot_sum(parts, name):
    r_n, c_n = parts.shape[1:]
    tr = max(t for t in range(16, 513, 16) if r_n % t == 0)

    def body(p_ref, o_ref):
        acc = p_ref[0].astype(F32)
        for k in range(1, N_DEV):
            acc = acc + p_ref[k].astype(F32)
        o_ref[...] = acc

    return pl.pallas_call(
        body, name=name, grid=(r_n // tr,),
        in_specs=[pl.BlockSpec((N_DEV, tr, c_n), lambda i: (0, i, 0))],
        out_specs=pl.BlockSpec((tr, c_n), lambda i: (i, 0)),
        out_shape=jax.ShapeDtypeStruct((r_n, c_n), F32),
        compiler_params=_cp("parallel"),
    )(parts)


def _adam_math(w, g, m, v):
    m = ADAM_B1 * m + (1.0 - ADAM_B1) * g
    v = ADAM_B2 * v + (1.0 - ADAM_B2) * jnp.square(g)
    m_hat = m / (1.0 - ADAM_B1 ** ADAM_STEP)
    v_hat = v / (1.0 - ADAM_B2 ** ADAM_STEP)
    return -ADAM_LR * (m_hat / (jnp.sqrt(v_hat) + ADAM_EPS) + ADAM_WD * w), m, v


def _adam_big(w, g, m, v, name):
    def body(w_ref, g_ref, m_ref, v_ref, d_ref, mo_ref, vo_ref):
        d_ref[...], mo_ref[...], vo_ref[...] = _adam_math(w_ref[...], g_ref[...], m_ref[...], v_ref[...])

    if w.ndim == 3:
        _, r_n, c_n = w.shape
        tr = min(r_n, 256)
        grid = (r_n // tr,)
        blk = lambda: pl.BlockSpec((1, tr, c_n), lambda i: (0, i, 0))
    else:
        r_n, c_n = w.shape
        tc = min(c_n, 256)
        grid = (c_n // tc,)
        blk = lambda: pl.BlockSpec((r_n, tc), lambda i: (0, i))
    return pl.pallas_call(
        body, name=name, grid=grid, in_specs=[blk(), blk(), blk(), blk()], out_specs=[blk(), blk(), blk()],
        out_shape=[jax.ShapeDtypeStruct(w.shape, F32)] * 3, compiler_params=_cp("parallel"),
    )(w, g, m, v)


def _adam_small(groups, where, wmv):
    n, n_g = len(wmv), len(groups)

    def body(*refs):
        g_refs = refs[:n_g]
        w_refs = refs[n_g:n_g + 3 * n]
        o_refs = refs[n_g + 3 * n:]
        for q in range(n):
            w_ref, m_ref, v_ref = w_refs[3 * q:3 * q + 3]
            r, c = w_ref.shape
            gi, r0 = where[q]
            g = g_refs[gi][0, r0:r0 + r, 0:c]
            for k in range(1, N_DEV):
                g = g + g_refs[gi][k, r0:r0 + r, 0:c]
            d, m, v = _adam_math(w_ref[...], g, m_ref[...], v_ref[...])
            o_refs[4 * q][...] = g
            o_refs[4 * q + 1][...] = d
            o_refs[4 * q + 2][...] = m
            o_refs[4 * q + 3][...] = v

    flat_wmv = [a for t in wmv for a in t]
    vm = pl.BlockSpec(memory_space=pltpu.VMEM)
    outs = pl.pallas_call(
        body, name="adam_small", in_specs=[vm] * (n_g + 3 * n), out_specs=[vm] * (4 * n),
        out_shape=[jax.ShapeDtypeStruct(t[0].shape, F32) for t in wmv for _ in range(4)],
        compiler_params=pltpu.CompilerParams(vmem_limit_bytes=VMEM_LIMIT),
    )(*groups, *flat_wmv)
    return [tuple(outs[4 * q:4 * q + 4]) for q in range(n)]


WEIGHTS = ["norm_mix_pre", "w_in", "conv_ssm_w", "conv_ssm_b", "dt_bias", "a_log", "d_skip", "ssm_norm", "conv_lru_w",
           "conv_lru_b", "lru_wa", "lru_ba", "lru_wx", "lru_bx", "lru_lambda", "w_out", "norm_mix_post", "norm_mlp_pre",
           "w_up", "w_down", "norm_mlp_post"]
BIG = ["w_out", "w_up", "w_down", "w_in"]
IN_ROWS = W_IN_COLS // N_DEV
IN_PAD, EARLY_ROWS = 848, 880
ROW_UP, ROW_DOWN, LATE_ROWS = 128, 640, 1152
GRAD_LATE_ROWS = 864
CONV_SSM_COLS, CONV_LRU_COLS = 1536 // N_DEV, D // N_DEV
SMALL = [("norm_mix_pre", (1, D), 0, 0), ("ssm_norm", (1, D), 0, 1), ("conv_lru_b", (1, D), 0, 2),
         ("lru_lambda", (1, D), 0, 3), ("norm_mix_post", (1, D), 0, 4), ("norm_mlp_pre", (1, D), 0, 5),
         ("norm_mlp_post", (1, D), 0, 6), ("conv_ssm_b", (1, 1536), 1, 0), ("dt_bias", (1, NH), 2, 0),
         ("a_log", (1, NH), 2, 1), ("d_skip", (1, NH), 2, 2), ("conv_ssm_w", (4, CONV_SSM_COLS), 3, 0),
         ("conv_lru_w", (4, CONV_LRU_COLS), 4, 0), ("lru_wa", (D, 64), 5, 0), ("lru_wx", (D, 64), 5, D),
         ("lru_ba", (NH, 64), 6, 0), ("lru_bx", (NH, 64), 6, NH)]
SMALL_GROUPS = [(8, D), (1, 1536), (8, 128), (4, 1536), (4, D), (2 * D, 64), (2 * NH, 64)]


def _pad_rows(flat, mult):
    n = flat.shape[0]
    rows = -(-n // (128 * mult)) * mult
    return jnp.pad(flat, (0, rows * 128 - n)).reshape(rows, 128)


def _split3(a):
    hi = a.astype(MXU)
    r1 = a - hi.astype(F32)
    mid = r1.astype(MXU)
    lo = (r1 - mid.astype(F32)).astype(MXU)
    return jnp.stack([hi, mid, lo])


def _early_pack(a, me):
    bf = lambda t: t.astype(MXU)
    conv = lambda t, c: jnp.pad(_split3(t).reshape(12, c), ((0, 4), (0, D - c)))
    shifted = lax.dynamic_update_slice(jnp.zeros((IN_PAD, D), MXU), bf(a["w_in"][0]).T, (2 * me, 0))
    return jnp.concatenate([shifted, conv(a["conv_ssm_w"][0], CONV_SSM_COLS), conv(a["conv_lru_w"][0], CONV_LRU_COLS)],
                           axis=0)


TILE = 16
SHARD_TILES = IN_PAD // TILE
SHARD_STEP = (IN_ROWS // TILE)
SEG_TILES = ((0, 64, 0), (64, 128, 320), (128, 160, 384), (160, 161, 416), (161, 225, 64), (225, 289, 128),
             (289, 353, 192), (353, 417, 256))


def _tile_runs(lo, hi):
    runs = []
    for s0, s1, d0 in SEG_TILES:
        a, b = max(lo, s0), min(hi, s1)
        if a < b:
            runs.append((a, b - a, d0 + a - s0))
    return runs


def _assemble_w_in(g):
    whole = []
    for k in range(N_DEV):
        lo = SHARD_STEP * k + (1 if k else 0)
        hi = SHARD_STEP * (k + 1) + (1 if k == N_DEV - 1 else 0)
        whole += [(k, a - SHARD_STEP * k, n, d) for a, n, d in _tile_runs(lo, hi)]
    split = [(k, _tile_runs(SHARD_STEP * k, SHARD_STEP * k + 1)[0][2]) for k in range(1, N_DEV)]

    def body(g_ref, o_ref):
        rows = lambda t, n=1: pl.ds(TILE * t, TILE * n)
        for k, t, n, d in whole:
            o_ref[rows(d, n), :] = g_ref[k, rows(t, n), :]
        for k, d in split:
            o_ref[rows(d), :] = g_ref[k - 1, rows(SHARD_STEP), :] + g_ref[k, rows(0), :]
        o_ref[pl.ds(W_IN_COLS, NP - W_IN_COLS), :] = jnp.zeros((NP - W_IN_COLS, D), o_ref.dtype)

    vm = pl.BlockSpec(memory_space=pltpu.VMEM)
    return pl.pallas_call(
        body, name="assemble_w_in", in_specs=[vm], out_specs=vm, out_shape=jax.ShapeDtypeStruct((NP, D), g.dtype),
        compiler_params=pltpu.CompilerParams(vmem_limit_bytes=VMEM_LIMIT),
    )(g)


def _scatter_w_in_grad(dw):
    runs = [(k, a - SHARD_STEP * k, n, d) for k in range(N_DEV)
            for a, n, d in _tile_runs(SHARD_STEP * k, SHARD_STEP * k + SHARD_TILES)]
    pad = GRAD_LATE_ROWS - IN_PAD

    def body(dw_ref, o_ref):
        rows = lambda t, n: pl.ds(TILE * t, TILE * n)
        for k, t, n, d in runs:
            o_ref[k, rows(t, n), :] = dw_ref[rows(d, n), :]
        for k in range(N_DEV):
            o_ref[k, pl.ds(IN_PAD, pad), :] = jnp.zeros((pad, D), o_ref.dtype)

    vm = pl.BlockSpec(memory_space=pltpu.VMEM)
    return pl.pallas_call(
        body, name="scatter_w_in_grad", in_specs=[vm], out_specs=vm,
        out_shape=jax.ShapeDtypeStruct((N_DEV, GRAD_LATE_ROWS, D), dw.dtype),
        compiler_params=pltpu.CompilerParams(vmem_limit_bytes=VMEM_LIMIT),
    )(dw)


def _early_unpack(g):
    w_in_pt = _assemble_w_in(g)
    conv = {}
    for n, r0, c in (("conv_ssm_w", IN_PAD, CONV_SSM_COLS), ("conv_lru_w", IN_PAD + 16, CONV_LRU_COLS)):
        s = g[:, r0:r0 + 12, :c].astype(F32).reshape(N_DEV, 3, 4, c)
        conv[n] = ((s[:, 0] + s[:, 1]) + s[:, 2]).transpose(1, 0, 2).reshape(4, N_DEV * c)
    return w_in_pt, conv


def _late_pack(a):
    bf = lambda t: t.astype(MXU)
    return jnp.concatenate([bf(a["w_out"][0]), bf(a["w_up"][0]).T, bf(a["w_down"][0])], axis=0)


def _late_unpack(g):
    return dict(w_out=g[:, :ROW_UP].reshape(D, D), w_upT=g[:, ROW_UP:ROW_DOWN].reshape(FF, D),
                w_down=g[:, ROW_DOWN:].reshape(FF, D))


def _own_slot(land, own):
    me = 4 * lax.axis_index("x") + 2 * lax.axis_index("y") + lax.axis_index("c")
    return lax.dynamic_update_slice_in_dim(land, own[None], me, axis=0)


def kernel(x, norm_mix_pre, w_in, conv_ssm_w, conv_ssm_b, dt_bias, a_log, d_skip, ssm_norm, conv_lru_w, conv_lru_b, lru_wa, lru_ba, lru_wx, lru_bx, lru_lambda, w_out, norm_mix_post, norm_mlp_pre, w_up, w_down, norm_mlp_post, loss_target, m_norm_mix_pre, m_w_in, m_conv_ssm_w, m_conv_ssm_b, m_dt_bias, m_a_log, m_d_skip, m_ssm_norm, m_conv_lru_w, m_conv_lru_b, m_lru_wa, m_lru_ba, m_lru_wx, m_lru_bx, m_lru_lambda, m_w_out, m_norm_mix_post, m_norm_mlp_pre, m_w_up, m_w_down, m_norm_mlp_post, v_norm_mix_pre, v_w_in, v_conv_ssm_w, v_conv_ssm_b, v_dt_bias, v_a_log, v_d_skip, v_ssm_norm, v_conv_lru_w, v_conv_lru_b, v_lru_wa, v_lru_ba, v_lru_wx, v_lru_bx, v_lru_lambda, v_w_out, v_norm_mix_post, v_norm_mlp_pre, v_w_up, v_w_down, v_norm_mlp_post):
    vals = (norm_mix_pre, w_in, conv_ssm_w, conv_ssm_b, dt_bias, a_log, d_skip, ssm_norm, conv_lru_w, conv_lru_b, lru_wa, lru_ba, lru_wx, lru_bx, lru_lambda, w_out, norm_mix_post, norm_mlp_pre, w_up, w_down, norm_mlp_post)
    m_vals = (m_norm_mix_pre, m_w_in, m_conv_ssm_w, m_conv_ssm_b, m_dt_bias, m_a_log, m_d_skip, m_ssm_norm, m_conv_lru_w, m_conv_lru_b, m_lru_wa, m_lru_ba, m_lru_wx, m_lru_bx, m_lru_lambda, m_w_out, m_norm_mix_post, m_norm_mlp_pre, m_w_up, m_w_down, m_norm_mlp_post)
    v_vals = (v_norm_mix_pre, v_w_in, v_conv_ssm_w, v_conv_ssm_b, v_dt_bias, v_a_log, v_d_skip, v_ssm_norm, v_conv_lru_w, v_conv_lru_b, v_lru_wa, v_lru_ba, v_lru_wx, v_lru_bx, v_lru_lambda, v_w_out, v_norm_mix_post, v_norm_mlp_pre, v_w_up, v_w_down, v_norm_mlp_post)
    w = dict(zip(WEIGHTS, vals))
    m = dict(zip(WEIGHTS, m_vals))
    v = dict(zip(WEIGHTS, v_vals))
    me = 4 * lax.axis_index("x") + 2 * lax.axis_index("y") + lax.axis_index("c")

    bf = lambda t: t.astype(MXU)
    late = _late_pack(w)
    early = _all_gather(_early_pack(w, me), "early_weights_all_gather")
    late, early = lax.optimization_barrier((late, early))
    lw = {}
    lw["sems"], lw["src"], lw["land"], token = _exchange_start(late, None, PLAN_GATHER_ICI, "late_weights_ici_start")
    w_in_pt, conv_w = _early_unpack(early)
    full = {n: (conv_w[n] if n in conv_w else w[n][0]) for n in WEIGHTS if n not in BIG}
    full["norm_mix_pre"] = full["norm_mix_pre"] + token[0, 0]

    def after_ssd(after):
        src, land = _exchange_wait(lw["sems"], lw["src"], lw["land"], after, PLAN_GATHER_ICI, "late_weights_ici_wait")
        lw["sems"], lw["src"], lw["land"], tok = _exchange_start(src, land, PLAN_GATHER_D2D, "late_weights_d2d_start")
        return tok[0, 0]

    def late_weights(after):
        src, land = _exchange_wait(lw["sems"], lw["src"], lw["land"], after, PLAN_GATHER_D2D, "late_weights_d2d_wait")
        return _late_unpack(_own_slot(land, src))

    sent = {}

    def send_mlp_grads(d_w_up_t, d_w_down, d_w_out):
        src = jnp.concatenate([bf(d_w_up_t).reshape(N_DEV, -1, D), bf(d_w_down).reshape(N_DEV, -1, D),
                               bf(d_w_out).reshape(N_DEV, -1, D)], axis=1)
        sent["sems"], sent["src"], sent["land"], tok = _exchange_start(src, None, PLAN_SCATTER, "mlp_grads_start")
        return tok[0, 0]

    def send_late_grads(d_w_in_pt):
        src = _scatter_w_in_grad(d_w_in_pt)
        sent["sems2"], sent["src2"], sent["land2"], tok = _exchange_start(src, None, PLAN_SCATTER, "late_grads_start")
        return tok[0, 0]

    loss, grad_x, g = _local_step(x[0], loss_target[0], _prep_params(full, dict(w_in_pT=w_in_pt)), after_ssd,
                                  late_weights, send_mlp_grads, send_late_grads)
    loss = lax.psum(loss, ("x", "y", "c"))

    own = lambda src: lax.dynamic_index_in_dim(src, me, keepdims=False)
    out_g, out_d, out_m, out_v = {}, {}, {}, {}
    mlp_src, mlp_land = _exchange_wait(sent["sems"], sent["src"], sent["land"], grad_x, PLAN_SCATTER, "mlp_grads_wait")
    g_mlp = _slot_sum(_own_slot(mlp_land, own(mlp_src)), "slot_sum_mlp")
    fs = FF // N_DEV
    for n, gn in (("w_up", g_mlp[:fs].T[None]), ("w_down", g_mlp[fs:2 * fs][None]), ("w_out", g_mlp[2 * fs:][None])):
        out_g[n] = gn
        out_d[n], out_m[n], out_v[n] = _adam_big(w[n], gn, m[n], v[n], "adam_" + n)
    zrow = jnp.zeros((1, D), F32)
    pad16 = lambda a: jnp.pad(a, ((0, 0), (0, 128 - NH)))
    small_parts = [
        jnp.concatenate([g["norm_mix_pre"], g["ssm_norm"], g["conv_lru_b"], g["lru_lambda"], g["norm_mix_post"],
                         g["norm_mlp_pre"], g["norm_mlp_post"], zrow], axis=0),
        g["conv_ssm_b"],
        jnp.concatenate([pad16(g["dt_bias"]), pad16(g["a_log"]), pad16(g["d_skip"]), jnp.zeros((5, 128), F32)], axis=0),
        g["conv_ssm_w"], g["conv_lru_w"],
        jnp.concatenate([g["lru_wa"].reshape(D, 64), g["lru_wx"].reshape(D, 64)], axis=0),
        jnp.concatenate([g["lru_ba"].reshape(NH, 64), g["lru_bx"].reshape(NH, 64)], axis=0)]
    small = _pad_rows(jnp.concatenate([s.reshape(-1) for s in small_parts]), 8)
    small, _ = lax.optimization_barrier((small, out_v["w_out"]))
    small_all = _all_gather(small, "small_grads_all_gather")

    lg_src, lg_land = _exchange_wait(sent["sems2"], sent["src2"], sent["land2"], small_all, PLAN_SCATTER, "late_grads_wait")
    g_late = _slot_sum(_own_slot(lg_land, own(lg_src)), "slot_sum_late")
    gt = lax.dynamic_slice(g_late, (2 * me, 0), (IN_ROWS, D))
    dt_, mt_, vt_ = _adam_big(w["w_in"][0].T, gt, m["w_in"][0].T, v["w_in"][0].T, "adam_w_in")
    out_g["w_in"], out_d["w_in"], out_m["w_in"], out_v["w_in"] = gt.T[None], dt_.T[None], mt_.T[None], vt_.T[None]
    sflat = small_all.reshape(N_DEV, -1)
    groups = []
    off = 0
    for r, c in SMALL_GROUPS:
        groups.append(sflat[:, off:off + r * c].reshape(N_DEV, r, c))
        off += r * c
    groups[3] = lax.dynamic_slice_in_dim(groups[3], me * CONV_SSM_COLS, CONV_SSM_COLS, axis=2)
    groups[4] = lax.dynamic_slice_in_dim(groups[4], me * CONV_LRU_COLS, CONV_LRU_COLS, axis=2)
    wmv = [(w[n].reshape(s), m[n].reshape(s), v[n].reshape(s)) for n, s, _, _ in SMALL]
    res = _adam_small(groups, [(gi, r0) for _, _, gi, r0 in SMALL], wmv)
    for (n, _, _, _), (g_n, d_n, m_n, v_n) in zip(SMALL, res):
        shape = w[n].shape
        out_g[n], out_d[n], out_m[n], out_v[n] = (g_n.reshape(shape), d_n.reshape(shape), m_n.reshape(shape),
                                                  v_n.reshape(shape))
    return (loss, grad_x[None], *[out_g[n] for n in WEIGHTS], *[out_d[n] for n in WEIGHTS],
            *[out_m[n] for n in WEIGHTS], *[out_v[n] for n in WEIGHTS])
```

```python
import functools

import jax
import jax.numpy as jnp
from jax import lax
from jax.experimental import pallas as pl
from jax.experimental.pallas import tpu as pltpu

F32 = jnp.float32
MXU = jnp.bfloat16
HI = lax.Precision.HIGHEST
EPS = 1e-6

D = 1024
NH = 16
NS = 128
CH = 128
FF = 4096
NP = 7168
SEG = 1024
LRU_C = 8.0
N_DEV = 8

ADAM_LR, ADAM_B1, ADAM_B2, ADAM_EPS, ADAM_WD, ADAM_STEP = 0.001, 0.9, 0.999, 1e-08, 0.01, 10

VMEM_LIMIT = 56 * 1024 * 1024


def _cp(*sem):
    return pltpu.CompilerParams(dimension_semantics=sem, vmem_limit_bytes=VMEM_LIMIT)


def _nn(a, b):
    return jnp.dot(a.astype(MXU), b.astype(MXU), preferred_element_type=F32)


def _nt(a, b):
    return lax.dot_general(a.astype(MXU), b.astype(MXU), (((1,), (1,)), ((), ())), preferred_element_type=F32)


def _tn(a, b):
    return lax.dot_general(a.astype(MXU), b.astype(MXU), (((0,), (0,)), ((), ())), preferred_element_type=F32)


_sigmoid = jax.nn.sigmoid


def _silu(x):
    return x * _sigmoid(x)


def _dsilu(x):
    s = _sigmoid(x)
    return s + x * s * (1.0 - s)


def _softplus(x):
    return jnp.maximum(x, 0.0) + jnp.log(1.0 + jnp.exp(-jnp.abs(x)))


def _rms(x, g):
    r = lax.rsqrt(jnp.mean(x * x, axis=-1, keepdims=True) + EPS)
    return x * r * g


def _rms_bwd(x, g, dy):
    r = lax.rsqrt(jnp.mean(x * x, axis=-1, keepdims=True) + EPS)
    gdy = g * dy
    dx = r * gdy - x * (r * r * r) * jnp.mean(x * gdy, axis=-1, keepdims=True)
    return dx, dy * x * r


def _rowsum(x):
    return jnp.sum(x, axis=0, keepdims=True)


def _taps_past(cur, prev8):
    r_n, c_n = cur.shape
    row = lax.broadcasted_iota(jnp.int32, (r_n, c_n), 0)
    out = []
    for k in range(4):
        s = 3 - k
        if s == 0:
            out.append(cur)
            continue
        head = jnp.concatenate([pltpu.roll(prev8, s, 0), jnp.zeros((r_n - 8, c_n), F32)], axis=0)
        out.append(jnp.where(row < s, head, pltpu.roll(cur, s, 0)))
    return out


def _taps_future(cur, fut8):
    r_n, c_n = cur.shape
    row = lax.broadcasted_iota(jnp.int32, (r_n, c_n), 0)
    out = []
    for k in range(4):
        s = 3 - k
        if s == 0:
            out.append(cur)
            continue
        tail = jnp.concatenate([jnp.zeros((r_n - 8, c_n), F32), pltpu.roll(fut8, 8 - s, 0)], axis=0)
        out.append(jnp.where(row >= r_n - s, tail, pltpu.roll(cur, r_n - s, 0)))
    return out


def _conv_apply(taps, w, b):
    acc = taps[0] * w[0:1, :]
    for k in range(1, 4):
        acc = acc + taps[k] * w[k:k + 1, :]
    return acc + b


def _inproj(x, g0, w):
    t_n = x.shape[0]
    tm = min(t_n, 1024)

    n_j = NP // SEG

    def body(x_ref, g_ref, w_ref, pb_ref, p6_ref, u_ref):
        j = pl.program_id(1)

        @pl.when(j == 0)
        def _():
            u_ref[...] = _rms(x_ref[...], g_ref[...]).astype(MXU)

        p = lax.dot_general(u_ref[...], w_ref[...], (((1,), (1,)), ((), ())), preferred_element_type=F32)

        @pl.when(j < n_j - 1)
        def _():
            pb_ref[...] = p.astype(MXU)

        @pl.when(j == n_j - 1)
        def _():
            p6_ref[...] = p

    pb, p6, u = pl.pallas_call(
        body, name="inproj", grid=(t_n // tm, n_j),
        in_specs=[pl.BlockSpec((tm, D), lambda i, j: (i, 0)), pl.BlockSpec((1, D), lambda i, j: (0, 0)),
                  pl.BlockSpec((SEG, D), lambda i, j: (j, 0))],
        out_specs=[pl.BlockSpec((tm, SEG), lambda i, j: (i, jnp.minimum(j, n_j - 2))),
                   pl.BlockSpec((tm, SEG), lambda i, j: (i, 0)), pl.BlockSpec((tm, D), lambda i, j: (i, 0))],
        out_shape=[jax.ShapeDtypeStruct((t_n, NP - SEG), MXU), jax.ShapeDtypeStruct((t_n, SEG), F32),
                   jax.ShapeDtypeStruct((t_n, D), MXU)],
        compiler_params=_cp("parallel", "arbitrary"),
    )(x, g0, w)
    return (pb, p6), u


def _ssd_prep(dtraw, dtb, alog):
    l_n = dtraw.shape[0]
    r = lax.broadcasted_iota(jnp.int32, (l_n, l_n), 0)
    c = lax.broadcasted_iota(jnp.int32, (l_n, l_n), 1)
    tril = (r >= c).astype(F32)
    triu = (r <= c).astype(F32)
    eye = (r == c).astype(F32)
    dt = _softplus(dtraw + dtb)
    adt = dt * (-jnp.exp(alog))
    ac = jnp.dot(tril, adt, preferred_element_type=F32, precision=HI)
    tn = (((0,), (0,)), ((), ()))
    ac_t = lax.dot_general(adt, triu, tn, preferred_element_type=F32, precision=HI)
    dt_t = lax.dot_general(dt, eye, tn, preferred_element_type=F32, precision=HI)
    return dt, dt_t, ac, ac_t, _rowsum(adt)


def _ssd_pair(j, xp, bg, cg, sp, dt, dt_t, ac, ac_t, aend):
    l_n = xp.shape[0]
    lane = lax.broadcasted_iota(jnp.int32, (l_n, 128), 1)
    sub = lax.broadcasted_iota(jnp.int32, (128, l_n), 0)
    lane1 = lax.broadcasted_iota(jnp.int32, (1, 128), 1)
    tri = lax.broadcasted_iota(jnp.int32, (l_n, l_n), 0) >= lax.broadcasted_iota(jnp.int32, (l_n, l_n), 1)
    lo = lax.broadcasted_iota(jnp.int32, (l_n, 128), 1) < 64
    lo_s = lax.broadcasted_iota(jnp.int32, (128, 128), 1) < 64
    cb = _nt(cg, bg)
    cs = _nn(cg, sp)
    x2 = jnp.concatenate([jnp.where(lo, xp, 0.0), jnp.where(lo, 0.0, xp)], axis=0)
    ws_, bs_, eo, ee = [], [], [], []
    for e in range(2):
        h = 2 * j + e
        ac_l = jnp.sum(jnp.where(lane == h, ac, 0.0), axis=1, keepdims=True)
        dt_l = jnp.sum(jnp.where(lane == h, dt, 0.0), axis=1, keepdims=True)
        a_end = jnp.sum(jnp.where(lane1 == h, aend, 0.0), axis=1, keepdims=True)
        ac_s, dt_s = ac_t[h:h + 1, :], dt_t[h:h + 1, :]
        decay = jnp.exp(jnp.where(tri, ac_l - ac_s, -1e30))
        ws_.append(cb * decay * dt_s)
        bs_.append(bg * (jnp.exp(a_end - ac_l) * dt_l))
        eo.append(jnp.exp(ac_l))
        ee.append(jnp.exp(a_end))
    y = _nn(jnp.concatenate(ws_, axis=1), x2) + jnp.where(lo, eo[0], eo[1]) * cs
    s_new = _tn(jnp.concatenate(bs_, axis=0), x2) + jnp.where(lo_s, ee[0], ee[1]) * sp
    return y, s_new


def _ssd_post(y, xs, z, dsk, nrm):
    y = (y + dsk * xs) * _silu(z)
    half = D // 2
    ya, yb = y[:, :half], y[:, half:]
    ya = ya * lax.rsqrt(jnp.mean(ya * ya, axis=-1, keepdims=True) + EPS)
    yb = yb * lax.rsqrt(jnp.mean(yb * yb, axis=-1, keepdims=True) + EPS)
    return jnp.concatenate([ya, yb], axis=1) * nrm


LAST_SEG = NP // SEG - 1


def _proj_ops(proj, seg_ids):
    return [proj[1] if s == LAST_SEG else proj[0] for s in seg_ids]


def _proj_specs(rows, seg_ids, order):
    return [pl.BlockSpec((rows, SEG), functools.partial(lambda i, c: (order(i), c), c=0 if s == LAST_SEG else s))
            for s in seg_ids]


def _prev_specs(rows, seg_ids, order):
    specs = []
    for s in seg_ids:
        n, c = (8, 0) if s == LAST_SEG else (16, s)
        specs.append(pl.BlockSpec((n, SEG), functools.partial(
            lambda i, n, c: (jnp.maximum(order(i) * (rows // n) - 1, 0), c), n=n, c=c)))
    return specs


def _prev8(ref):
    return ref[...] if ref.shape[0] == 8 else ref[8:16, :].astype(F32)


def _full(shape):
    return pl.BlockSpec(shape, lambda i: (0,) * len(shape))


def _ssd_fwd(proj, cwx, cwb, cbx, cbb, dtb, alog, dsk, nrm):
    t_n = proj[0].shape[0]
    n_c = t_n // CH
    fwd = lambda i: i

    def body(z_ref, xs_ref, bc_ref, xsp_ref, bcp_ref, cwx_ref, cwb_ref, cbx_ref, cbb_ref, dtb_ref, alog_ref,
             dsk_ref, nrm_ref, ya_ref, sprev_ref, yraw_ref, s_ref):
        c = pl.program_id(0)

        @pl.when(c == 0)
        def _():
            s_ref[...] = jnp.zeros_like(s_ref)

        keep = jnp.where(c == 0, 0.0, 1.0)
        xs_pre = _conv_apply(_taps_past(xs_ref[...].astype(F32), _prev8(xsp_ref) * keep), cwx_ref[...], cbx_ref[...])
        bc_pre = _conv_apply(_taps_past(bc_ref[:, :512], bcp_ref[:, :512] * keep), cwb_ref[...], cbb_ref[...])
        prep = _ssd_prep(bc_ref[:, 512:640], dtb_ref[...], alog_ref[...])
        xs = _silu(xs_pre)
        bc = _silu(bc_pre)
        sprev_ref[0] = s_ref[...]
        ys = []
        for j in range(NH // 2):
            g = j // 4
            yp, sn = _ssd_pair(j, xs[:, 128 * j:128 * j + 128], bc[:, 128 * g:128 * g + 128],
                               bc[:, 256 + 128 * g:384 + 128 * g], s_ref[:, 128 * j:128 * j + 128], *prep)
            ys.append(yp)
            s_ref[:, 128 * j:128 * j + 128] = sn
        y = jnp.concatenate(ys, axis=1)
        yraw_ref[...] = y
        ya_ref[...] = _ssd_post(y, xs, z_ref[...].astype(F32), dsk_ref[...], nrm_ref[...]).astype(ya_ref.dtype)

    return pl.pallas_call(
        body, name="ssd_fwd", grid=(n_c,),
        in_specs=_proj_specs(CH, (0, 5, 6), fwd) + _prev_specs(CH, (5, 6), fwd) + [
            _full((4, D)), _full((4, 512)), _full((1, D)), _full((1, 512)), _full((1, 128)), _full((1, 128)),
            _full((1, D)), _full((1, D))],
        out_specs=[pl.BlockSpec((CH, D), lambda i: (i, 0)), pl.BlockSpec((1, NS, D), lambda i: (i, 0, 0)),
                   pl.BlockSpec((CH, D), lambda i: (i, 0))],
        out_shape=[jax.ShapeDtypeStruct((t_n, D), MXU), jax.ShapeDtypeStruct((n_c, NS, D), F32),
                   jax.ShapeDtypeStruct((t_n, D), F32)],
        scratch_shapes=[pltpu.VMEM((NS, D), F32)],
        compiler_params=_cp("arbitrary"),
    )(*_proj_ops(proj, (0, 5, 6, 5, 6)), cwx, cwb, cbx, cbb, dtb, alog, dsk, nrm)


def _ssd_bwd(dya, yraw, proj, sprev, cwx, cwb, cbx, cbb, dtb, alog, dsk, nrm, mlp_ops):
    t_n = proj[0].shape[0]
    n_c = t_n // CH
    rev = lambda i: n_c - 1 - i
    fb = FF // n_c

    def body(dya_ref, yraw_ref, z_ref, xs_ref, bc_ref, xsp_ref, bcp_ref, sprev_ref, cwx_ref, cwb_ref, cbx_ref, cbb_ref,
             dtb_ref, alog_ref, dsk_ref, nrm_ref, hid_ref, dff_ref, dhp_ref, v_ref,
             dz_ref, dxs_ref, dbc_ref, dcwx_ref, dcwb_ref, dcbx_ref, dcbb_ref, ddtb_ref, dalog_ref, ddsk_ref,
             dnrm_ref, dwd_ref, dwu_ref, ds_ref, futx_ref, futb_ref):
        i = pl.program_id(0)
        acc_refs = (dcwx_ref, dcwb_ref, dcbx_ref, dcbb_ref, ddtb_ref, dalog_ref, ddsk_ref, dnrm_ref)
        tn = (((0,), (0,)), ((), ()))
        dwd_ref[...] = lax.dot_general(hid_ref[...], dff_ref[...], tn, preferred_element_type=F32).astype(MXU)
        dwu_ref[...] = lax.dot_general(dhp_ref[...], v_ref[...], tn, preferred_element_type=F32).astype(MXU)

        @pl.when(i == 0)
        def _():
            for r in (ds_ref, futx_ref, futb_ref) + acc_refs:
                r[...] = jnp.zeros_like(r)

        keep = jnp.where(i == n_c - 1, 0.0, 1.0)
        taps_x = _taps_past(xs_ref[...].astype(F32), _prev8(xsp_ref) * keep)
        taps_b = _taps_past(bc_ref[:, :512], bcp_ref[:, :512] * keep)
        xs_pre = _conv_apply(taps_x, cwx_ref[...], cbx_ref[...])
        bc_pre = _conv_apply(taps_b, cwb_ref[...], cbb_ref[...])
        xs = _silu(xs_pre)
        bc = _silu(bc_pre)
        prep, prep_vjp = jax.vjp(_ssd_prep, bc_ref[:, 512:640], dtb_ref[...], alog_ref[...])
        s_in = sprev_ref[0]

        def pair_args(j):
            g = j // 4
            return (xs[:, 128 * j:128 * j + 128], bc[:, 128 * g:128 * g + 128],
                    bc[:, 256 + 128 * g:384 + 128 * g], s_in[:, 128 * j:128 * j + 128]) + tuple(prep)

        _, post_vjp = jax.vjp(_ssd_post, yraw_ref[...], xs, z_ref[...].astype(F32), dsk_ref[...], nrm_ref[...])
        dy, dxs_skip, dz, ddsk, dnrm = post_vjp(dya_ref[...])
        dz_ref[...] = dz.astype(dz_ref.dtype)
        ddsk_ref[...] += ddsk
        dnrm_ref[...] += dnrm

        dprep = [jnp.zeros_like(p) for p in prep]
        dxp = []
        dbg = [jnp.zeros((CH, 128), F32), jnp.zeros((CH, 128), F32)]
        dcg = [jnp.zeros((CH, 128), F32), jnp.zeros((CH, 128), F32)]
        for j in range(NH // 2):
            g = j // 4
            _, pair_vjp = jax.vjp(functools.partial(_ssd_pair, j), *pair_args(j))
            cts = pair_vjp((dy[:, 128 * j:128 * j + 128], ds_ref[:, 128 * j:128 * j + 128]))
            dxp.append(cts[0])
            dbg[g] = dbg[g] + cts[1]
            dcg[g] = dcg[g] + cts[2]
            ds_ref[:, 128 * j:128 * j + 128] = cts[3]
            dprep = [a + b for a, b in zip(dprep, cts[4:])]
        ddtraw, ddtb, dalog = prep_vjp(tuple(dprep))
        ddtb_ref[...] += ddtb
        dalog_ref[...] += dalog

        dxs_pre = (dxs_skip + jnp.concatenate(dxp, axis=1)) * _dsilu(xs_pre)
        dbc_pre = jnp.concatenate([dbg[0], dbg[1], dcg[0], dcg[1]], axis=1) * _dsilu(bc_pre)
        dcbx_ref[...] += _rowsum(dxs_pre)
        dcbb_ref[...] += _rowsum(dbc_pre)
        for k in range(4):
            dcwx_ref[k:k + 1, :] += _rowsum(dxs_pre * taps_x[k])
            dcwb_ref[k:k + 1, :] += _rowsum(dbc_pre * taps_b[k])
        fx = _taps_future(dxs_pre, futx_ref[...])
        fb = _taps_future(dbc_pre, futb_ref[...])
        cwx = cwx_ref[...]
        cwb = cwb_ref[...]
        dxs_in = fx[0] * cwx[0:1, :]
        dbc_in = fb[0] * cwb[0:1, :]
        for k in range(1, 4):
            dxs_in = dxs_in + fx[k] * cwx[k:k + 1, :]
            dbc_in = dbc_in + fb[k] * cwb[k:k + 1, :]
        futx_ref[...] = dxs_pre[0:8, :]
        futb_ref[...] = dbc_pre[0:8, :]
        dxs_ref[...] = dxs_in.astype(dxs_ref.dtype)
        dbc_ref[...] = jnp.concatenate([dbc_in, ddtraw, jnp.zeros((CH, SEG - 640), F32)], axis=1).astype(dbc_ref.dtype)

    row_out = lambda: pl.BlockSpec((CH, D), lambda i: (rev(i), 0))
    outs = pl.pallas_call(
        body, name="ssd_bwd", grid=(n_c,),
        in_specs=[row_out(), row_out()] + _proj_specs(CH, (0, 5, 6), rev)
        + _prev_specs(CH, (5, 6), rev) + [pl.BlockSpec((1, NS, D), lambda i: (rev(i), 0, 0)),
                                            _full((4, D)), _full((4, 512)), _full((1, D)), _full((1, 512)),
                                            _full((1, 128)), _full((1, 128)), _full((1, D)), _full((1, D)),
                                            pl.BlockSpec((t_n, fb), lambda i: (0, i)), _full((t_n, D)),
                                            pl.BlockSpec((t_n, fb), lambda i: (0, i)), _full((t_n, D))],
        out_specs=[row_out(), row_out(), row_out(), _full((4, D)), _full((4, 512)), _full((1, D)), _full((1, 512)),
                   _full((1, 128)), _full((1, 128)), _full((1, D)), _full((1, D)),
                   pl.BlockSpec((fb, D), lambda i: (i, 0)), pl.BlockSpec((fb, D), lambda i: (i, 0))],
        out_shape=[jax.ShapeDtypeStruct((t_n, D), MXU)] * 3 + [
            jax.ShapeDtypeStruct(s, F32) for s in ((4, D), (4, 512), (1, D), (1, 512), (1, 128), (1, 128), (1, D), (1, D))]
        + [jax.ShapeDtypeStruct((FF, D), MXU)] * 2,
        scratch_shapes=[pltpu.VMEM((NS, D), F32), pltpu.VMEM((8, D), F32), pltpu.VMEM((8, 512), F32)],
        compiler_params=_cp("arbitrary"),
    )(dya, yraw, *_proj_ops(proj, (0, 5, 6, 5, 6)), sprev, cwx, cwb, cbx, cbb, dtb, alog, dsk, nrm, *mlp_ops)
    return outs


LRU_ROWS = 256
LRU_BLK = 256


def _lru_gates(xr, wa, wx, ba, bx, lam):
    pr = jnp.concatenate([_nn(xr[:, LRU_BLK * b:LRU_BLK * (b + 1)], wa[b]) for b in range(D // LRU_BLK)], axis=1) + ba
    pi = jnp.concatenate([_nn(xr[:, LRU_BLK * b:LRU_BLK * (b + 1)], wx[b]) for b in range(D // LRU_BLK)], axis=1) + bx
    log_a = -LRU_C * _sigmoid(pr) * _softplus(-lam)
    a = jnp.exp(log_a)
    mult = jnp.sqrt(1.0 - jnp.exp(2.0 * log_a))
    return a, mult * (_sigmoid(pi) * xr)


def _lru_out(h, g):
    return h * jax.nn.gelu(g, approximate=True)


def _lru_fwd(proj, cw, cb, wa, wx, ba, bx, lam):
    t_n = proj[0].shape[0]
    rows = min(LRU_ROWS, t_n)
    fwd = lambda i: i

    def body(g_ref, x_ref, xp_ref, cw_ref, cb_ref, wa_ref, wx_ref, ba_ref, bx_ref, lam_ref, yb_ref, h_ref,
             a_s, u_s, carry):
        i = pl.program_id(0)

        @pl.when(i == 0)
        def _():
            carry[...] = jnp.zeros_like(carry)

        keep = jnp.where(i == 0, 0.0, 1.0)
        xr = _conv_apply(_taps_past(x_ref[...].astype(F32), _prev8(xp_ref) * keep), cw_ref[...], cb_ref[...])
        a, u = _lru_gates(xr, wa_ref[...], wx_ref[...], ba_ref[...], bx_ref[...], lam_ref[...])
        a_s[...] = a
        u_s[...] = u
        row = lax.broadcasted_iota(jnp.int32, (8, D), 0)

        def blk(b, c):
            s = pl.multiple_of(b * 8, 8)
            av = a_s[pl.ds(s, 8), :]
            uv = u_s[pl.ds(s, 8), :]
            for d in (1, 2, 4):
                m = row >= d
                uv = uv + av * jnp.where(m, pltpu.roll(uv, d, 0), 0.0)
                av = av * jnp.where(m, pltpu.roll(av, d, 0), 1.0)
            hv = uv + av * c
            h_ref[pl.ds(s, 8), :] = hv
            return hv[7:8, :]

        carry[0:1, :] = lax.fori_loop(0, rows // 8, blk, carry[0:1, :])
        yb_ref[...] = _lru_out(h_ref[...], g_ref[...].astype(F32)).astype(yb_ref.dtype)

    return pl.pallas_call(
        body, name="lru_fwd", grid=(t_n // rows,),
        in_specs=_proj_specs(rows, (1, 2), fwd) + _prev_specs(rows, (2,), fwd) + [
            _full((4, D)), _full((1, D)), _full((4, LRU_BLK, LRU_BLK)), _full((4, LRU_BLK, LRU_BLK)),
            _full((1, D)), _full((1, D)), _full((1, D))],
        out_specs=[pl.BlockSpec((rows, D), lambda i: (i, 0)), pl.BlockSpec((rows, D), lambda i: (i, 0))],
        out_shape=[jax.ShapeDtypeStruct((t_n, D), MXU), jax.ShapeDtypeStruct((t_n, D), F32)],
        scratch_shapes=[pltpu.VMEM((rows, D), F32), pltpu.VMEM((rows, D), F32), pltpu.VMEM((8, D), F32)],
        compiler_params=_cp("arbitrary"),
    )(*_proj_ops(proj, (1, 2, 2)), cw, cb, wa, wx, ba, bx, lam)


def _lru_bwd(dyb, proj, h, cw, cb, wa, wx, ba, bx, lam):
    t_n = proj[0].shape[0]
    rows = min(LRU_ROWS, t_n)
    n_t = t_n // rows
    rev = lambda i: n_t - 1 - i
    rb = rows // 8

    def body(dyb_ref, g_ref, x_ref, xp_ref, h_ref, hp_ref, cw_ref, cb_ref, wa_ref, wx_ref, ba_ref, bx_ref, lam_ref,
             dg_ref, dx_ref, dcw_ref, dcb_ref, dwa_ref, dwx_ref, dba_ref, dbx_ref, dlam_ref,
             a_s, dh_s, hx_s, da_s, du_s, carry, fut):
        i = pl.program_id(0)
        acc_refs = (dcw_ref, dcb_ref, dwa_ref, dwx_ref, dba_ref, dbx_ref, dlam_ref)

        @pl.when(i == 0)
        def _():
            for r in (carry, fut) + acc_refs:
                r[...] = jnp.zeros_like(r)

        keep = jnp.where(i == n_t - 1, 0.0, 1.0)
        taps = _taps_past(x_ref[...].astype(F32), _prev8(xp_ref) * keep)
        xr = _conv_apply(taps, cw_ref[...], cb_ref[...])
        gate_in = (xr, wa_ref[...], wx_ref[...], ba_ref[...], bx_ref[...], lam_ref[...])
        (a, _), gates_vjp = jax.vjp(_lru_gates, *gate_in)
        _, out_vjp = jax.vjp(_lru_out, h_ref[...], g_ref[...].astype(F32))
        dh, dg = out_vjp(dyb_ref[...])
        dg_ref[...] = dg.astype(dg_ref.dtype)
        a_s[...] = a
        dh_s[...] = dh
        hx_s[0:8, :] = hp_ref[...] * keep
        hx_s[8:, :] = h_ref[...]
        row = lax.broadcasted_iota(jnp.int32, (8, D), 0)

        def blk(b, c):
            s = pl.multiple_of((rb - 1 - b) * 8, 8)
            av = a_s[pl.ds(s, 8), :]
            dhv = dh_s[pl.ds(s, 8), :]
            a0 = av
            kv = av * dhv
            for d in (1, 2, 4):
                m = row <= 7 - d
                kv = kv + av * jnp.where(m, pltpu.roll(kv, 8 - d, 0), 0.0)
                av = av * jnp.where(m, pltpu.roll(av, 8 - d, 0), 1.0)
            kv = kv + av * c
            gv = dhv + jnp.where(row < 7, pltpu.roll(kv, 7, 0), c)
            hb = hx_s[pl.ds(s + 8, 8), :]
            hpv = hx_s[pl.ds(s, 8), :]
            hprev = jnp.where(row >= 1, pltpu.roll(hb, 1, 0), hpv[7:8, :])
            du_s[pl.ds(s, 8), :] = gv
            da_s[pl.ds(s, 8), :] = gv * hprev
            del a0
            return kv[0:1, :]

        carry[0:1, :] = lax.fori_loop(0, rb, blk, carry[0:1, :])
        dxr, dwa, dwx, dba, dbx, dlam = gates_vjp((da_s[...], du_s[...]))
        dwa_ref[...] += dwa
        dwx_ref[...] += dwx
        dba_ref[...] += dba
        dbx_ref[...] += dbx
        dlam_ref[...] += dlam
        dcb_ref[...] += _rowsum(dxr)
        for k in range(4):
            dcw_ref[k:k + 1, :] += _rowsum(dxr * taps[k])
        ft = _taps_future(dxr, fut[...])
        cwv = cw_ref[...]
        dx = ft[0] * cwv[0:1, :]
        for k in range(1, 4):
            dx = dx + ft[k] * cwv[k:k + 1, :]
        fut[...] = dxr[0:8, :]
        dx_ref[...] = dx.astype(dx_ref.dtype)

    row_in = lambda: pl.BlockSpec((rows, D), lambda i: (rev(i), 0))
    prev_h = pl.BlockSpec((8, D), lambda i: (jnp.maximum(rev(i) * rb - 1, 0), 0))
    wspec = lambda: _full((4, LRU_BLK, LRU_BLK))
    return pl.pallas_call(
        body, name="lru_bwd", grid=(n_t,),
        in_specs=[row_in()] + _proj_specs(rows, (1, 2), rev) + _prev_specs(rows, (2,), rev) + [row_in(), prev_h] + [
            _full((4, D)), _full((1, D)), wspec(), wspec(), _full((1, D)), _full((1, D)), _full((1, D))],
        out_specs=[row_in(), row_in(), _full((4, D)), _full((1, D)), wspec(), wspec(), _full((1, D)), _full((1, D)),
                   _full((1, D))],
        out_shape=[jax.ShapeDtypeStruct((t_n, D), MXU)] * 2 + [
            jax.ShapeDtypeStruct(s, F32) for s in ((4, D), (1, D), (4, LRU_BLK, LRU_BLK), (4, LRU_BLK, LRU_BLK),
                                                   (1, D), (1, D), (1, D))],
        scratch_shapes=[pltpu.VMEM((rows, D), F32), pltpu.VMEM((rows, D), F32), pltpu.VMEM((rows + 8, D), F32),
                        pltpu.VMEM((rows, D), F32), pltpu.VMEM((rows, D), F32), pltpu.VMEM((8, D), F32),
                        pltpu.VMEM((8, D), F32)],
        compiler_params=_cp("arbitrary"),
    )(dyb, *_proj_ops(proj, (1, 2, 2)), h, h, cw, cb, wa, wx, ba, bx, lam)


def _merge_out(ya, yb, proj, x, wout, g1):
    t_n = x.shape[0]
    tm = min(t_n, 512)

    def body(ya_ref, yb_ref, ga_ref, gb_ref, x_ref, w_ref, g_ref, h1_ref, mix_ref, mg_ref):
        merged = (_sigmoid(ga_ref[...].astype(F32)) * ya_ref[...].astype(F32)
                  + _sigmoid(gb_ref[...].astype(F32)) * yb_ref[...].astype(F32))
        mg = merged.astype(MXU)
        mg_ref[...] = mg
        mix = jnp.dot(mg, w_ref[...], preferred_element_type=F32)
        mix_ref[...] = mix
        h1_ref[...] = x_ref[...] + _rms(mix, g_ref[...])

    row = lambda: pl.BlockSpec((tm, D), lambda i: (i, 0))
    return pl.pallas_call(
        body, name="merge_out", grid=(t_n // tm,),
        in_specs=[row(), row()] + _proj_specs(tm, (3, 4), lambda i: i) + [row(), _full((D, D)), _full((1, D))],
        out_specs=[row(), row(), row()],
        out_shape=[jax.ShapeDtypeStruct((t_n, D), F32), jax.ShapeDtypeStruct((t_n, D), F32),
                   jax.ShapeDtypeStruct((t_n, D), MXU)],
        compiler_params=_cp("parallel"),
    )(ya, yb, *_proj_ops(proj, (3, 4)), x, wout, g1)


def _out_bwd(dv, h1, g2, dout, mix, ya, yb, proj, wout, g1):
    t_n = dv.shape[0]
    tm = min(t_n, 256)

    def body(dv_ref, h1_ref, g2_ref, dout_ref, mix_ref, ya_ref, yb_ref, ga_ref, gb_ref, w_ref, g_ref,
             dh1_ref, dmix_ref, dya_ref, dyb_ref, dga_ref, dgb_ref, dg2_ref, dg1_ref):
        @pl.when(pl.program_id(0) == 0)
        def _():
            dg1_ref[...] = jnp.zeros_like(dg1_ref)
            dg2_ref[...] = jnp.zeros_like(dg2_ref)

        dx, dg_rows = _rms_bwd(h1_ref[...], g2_ref[...], dv_ref[...])
        dg2_ref[...] += _rowsum(dg_rows)
        dh1 = dout_ref[...] + dx
        dh1_ref[...] = dh1
        dmix, dg_rows = _rms_bwd(mix_ref[...], g_ref[...], dh1)
        dg1_ref[...] += _rowsum(dg_rows)
        dmix_b = dmix.astype(MXU)
        dmix_ref[...] = dmix_b
        dmg = lax.dot_general(dmix_b, w_ref[...], (((1,), (1,)), ((), ())), preferred_element_type=F32)
        sa = _sigmoid(ga_ref[...].astype(F32))
        sb = _sigmoid(gb_ref[...].astype(F32))
        dya_ref[...] = dmg * sa
        dyb_ref[...] = dmg * sb
        dga_ref[...] = (dmg * ya_ref[...].astype(F32) * sa * (1.0 - sa)).astype(MXU)
        dgb_ref[...] = (dmg * yb_ref[...].astype(F32) * sb * (1.0 - sb)).astype(MXU)

    row = lambda: pl.BlockSpec((tm, D), lambda i: (i, 0))
    vec = lambda: _full((1, D))
    return pl.pallas_call(
        body, name="out_bwd", grid=(t_n // tm,),
        in_specs=[row(), row(), vec(), row(), row(), row(), row()] + _proj_specs(tm, (3, 4), lambda i: i)
        + [_full((D, D)), vec()],
        out_specs=[row(), row(), row(), row(), row(), row(), vec(), vec()],
        out_shape=[jax.ShapeDtypeStruct((t_n, D), F32), jax.ShapeDtypeStruct((t_n, D), MXU),
                   jax.ShapeDtypeStruct((t_n, D), F32), jax.ShapeDtypeStruct((t_n, D), F32),
                   jax.ShapeDtypeStruct((t_n, D), MXU), jax.ShapeDtypeStruct((t_n, D), MXU),
                   jax.ShapeDtypeStruct((1, D), F32), jax.ShapeDtypeStruct((1, D), F32)],
        compiler_params=_cp("arbitrary"),
    )(dv, h1, g2, dout, mix, ya, yb, *_proj_ops(proj, (3, 4)), wout, g1)


MLP_TM = 1024
MLP_TF_FWD = 512
MLP_TF_BWD = 1024


def _mlp_fwd(h1, g2, wup, wdown, g3, tgt):
    t_n = h1.shape[0]
    tm = min(t_n, MLP_TM)
    n_f = FF // MLP_TF_FWD

    def body(h1_ref, g2_ref, wu_ref, wd_ref, g3_ref, tgt_ref, hp_ref, v_ref, dout_ref, dff_ref, loss_ref, dg3_ref, acc):
        i, j = pl.program_id(0), pl.program_id(1)

        @pl.when((i == 0) & (j == 0))
        def _():
            loss_ref[...] = jnp.zeros_like(loss_ref)
            dg3_ref[...] = jnp.zeros_like(dg3_ref)

        @pl.when(j == 0)
        def _():
            v_ref[...] = _rms(h1_ref[...], g2_ref[...]).astype(MXU)
            acc[...] = jnp.zeros_like(acc)

        hp = lax.dot_general(v_ref[...], wu_ref[...], (((1,), (1,)), ((), ())), preferred_element_type=F32)
        hp_ref[...] = hp.astype(MXU)
        hid = jnp.square(jnp.maximum(hp, 0.0))
        acc[...] += jnp.dot(hid.astype(MXU), wd_ref[...], preferred_element_type=F32)

        @pl.when(j == n_f - 1)
        def _():
            ff = acc[...]
            err = h1_ref[...] + _rms(ff, g3_ref[...]) - tgt_ref[...]
            loss_ref[...] += 0.5 * jnp.sum(jnp.mean(err * err, axis=-1, keepdims=True), axis=0, keepdims=True)
            dout = err * (1.0 / D)
            dout_ref[...] = dout
            dff, dg_rows = _rms_bwd(ff, g3_ref[...], dout)
            dg3_ref[...] += _rowsum(dg_rows)
            dff_ref[...] = dff.astype(MXU)

    row = lambda: pl.BlockSpec((tm, D), lambda i, j: (i, 0))
    vec = lambda: pl.BlockSpec((1, D), lambda i, j: (0, 0))
    return pl.pallas_call(
        body, name="mlp_fwd", grid=(t_n // tm, n_f),
        in_specs=[row(), vec(), pl.BlockSpec((MLP_TF_FWD, D), lambda i, j: (j, 0)),
                  pl.BlockSpec((MLP_TF_FWD, D), lambda i, j: (j, 0)), vec(), row()],
        out_specs=[pl.BlockSpec((tm, MLP_TF_FWD), lambda i, j: (i, j)), row(), row(), row(),
                   pl.BlockSpec((1, 1), lambda i, j: (0, 0)), vec()],
        out_shape=[jax.ShapeDtypeStruct((t_n, FF), MXU), jax.ShapeDtypeStruct((t_n, D), MXU),
                   jax.ShapeDtypeStruct((t_n, D), F32), jax.ShapeDtypeStruct((t_n, D), MXU),
                   jax.ShapeDtypeStruct((1, 1), F32), jax.ShapeDtypeStruct((1, D), F32)],
        scratch_shapes=[pltpu.VMEM((tm, D), F32)],
        compiler_params=_cp("arbitrary", "arbitrary"),
    )(h1, g2, wup, wdown, g3, tgt)


def _mlp_bwd(dff, hp, wup, wdown):
    t_n = dff.shape[0]
    tm = min(t_n, MLP_TM)
    n_f = FF // MLP_TF_BWD

    def mm_body(dff_ref, hp_ref, wu_ref, wd_ref, dv_ref, dhp_ref, hid_ref):
        @pl.when(pl.program_id(1) == 0)
        def _():
            dv_ref[...] = jnp.zeros_like(dv_ref)

        relu = jnp.maximum(hp_ref[...].astype(F32), 0.0)
        hid_ref[...] = jnp.square(relu).astype(MXU)
        dhid = lax.dot_general(dff_ref[...], wd_ref[...], (((1,), (1,)), ((), ())), preferred_element_type=F32)
        dhp = (dhid * (2.0 * relu)).astype(MXU)
        dhp_ref[...] = dhp
        dv_ref[...] += jnp.dot(dhp, wu_ref[...], preferred_element_type=F32)

    row = lambda: pl.BlockSpec((tm, D), lambda i, j: (i, 0))
    blk = lambda: pl.BlockSpec((tm, MLP_TF_BWD), lambda i, j: (i, j))
    wblk = lambda: pl.BlockSpec((MLP_TF_BWD, D), lambda i, j: (j, 0))
    return pl.pallas_call(
        mm_body, name="mlp_bwd", grid=(t_n // tm, n_f),
        in_specs=[row(), blk(), wblk(), wblk()], out_specs=[row(), blk(), blk()],
        out_shape=[jax.ShapeDtypeStruct((t_n, D), F32), jax.ShapeDtypeStruct((t_n, FF), MXU),
                   jax.ShapeDtypeStruct((t_n, FF), MXU)],
        compiler_params=_cp("parallel", "arbitrary"),
    )(dff, hp, wup, wdown)


def _wgrad(a, g, name):
    t_n, k_n = a.shape
    n_n = g.shape[1]
    tt = min(t_n, 1024)
    tk, tn = min(k_n, 1024), min(n_n, 1024)

    n_t = t_n // tt

    def body(a_ref, g_ref, o_ref, acc):
        t = pl.program_id(2)

        @pl.when(t == 0)
        def _():
            acc[...] = jnp.zeros_like(acc)

        acc[...] += lax.dot_general(a_ref[...], g_ref[...], (((0,), (0,)), ((), ())), preferred_element_type=F32)

        @pl.when(t == n_t - 1)
        def _():
            o_ref[...] = acc[...].astype(o_ref.dtype)

    return pl.pallas_call(
        body, name=name, grid=(k_n // tk, n_n // tn, n_t),
        in_specs=[pl.BlockSpec((tt, tk), lambda k, n, t: (t, k)), pl.BlockSpec((tt, tn), lambda k, n, t: (t, n))],
        out_specs=pl.BlockSpec((tk, tn), lambda k, n, t: (k, n)),
        out_shape=jax.ShapeDtypeStruct((k_n, n_n), MXU),
        scratch_shapes=[pltpu.VMEM((tk, tn), F32)],
        compiler_params=_cp("parallel", "parallel", "arbitrary"),
    )(a, g)


def _wgrad_segs(segs, g, name):
    t_n, n_n = g.shape
    n_s = len(segs)
    tt = min(t_n, 1024)
    n_t = t_n // tt

    def body(*refs):
        a_refs = refs[:n_s]
        g_ref, o_ref, acc = refs[n_s:]
        s_id, t = pl.program_id(0), pl.program_id(1)

        @pl.when(t == 0)
        def _():
            acc[...] = jnp.zeros_like(acc)

        for s in range(n_s):
            @pl.when(s_id == s)
            def _(s=s):
                acc[...] += lax.dot_general(a_refs[s][...], g_ref[...], (((0,), (0,)), ((), ())),
                                            preferred_element_type=F32)

        @pl.when(t == n_t - 1)
        def _():
            o_ref[...] = acc[...].astype(o_ref.dtype)

    seg_spec = lambda s: pl.BlockSpec((tt, SEG), lambda i, t: (jnp.where(i == s, t, jnp.where(i < s, 0, n_t - 1)), 0))
    return pl.pallas_call(
        body, name=name, grid=(n_s, n_t),
        in_specs=[seg_spec(s) for s in range(n_s)] + [pl.BlockSpec((tt, n_n), lambda i, t: (t, 0))],
        out_specs=pl.BlockSpec((SEG, n_n), lambda i, t: (i, 0)),
        out_shape=jax.ShapeDtypeStruct((n_s * SEG, n_n), MXU),
        scratch_shapes=[pltpu.VMEM((SEG, n_n), F32)],
        compiler_params=_cp("arbitrary", "arbitrary"),
    )(*segs, g)


def _inproj_bwd(dsegs, w, x, g0, dh1):
    t_n = x.shape[0]
    tm = min(t_n, 1024)
    n_k = NP // SEG

    def mm_body(*refs):
        dp_refs = refs[:n_k]
        w_ref, du_ref = refs[n_k:]
        k = pl.program_id(1)

        @pl.when(k == 0)
        def _():
            du_ref[...] = jnp.zeros_like(du_ref)

        for s in range(n_k):
            @pl.when(k == s)
            def _(s=s):
                du_ref[...] += jnp.dot(dp_refs[s][...], w_ref[...], preferred_element_type=F32)

    du = pl.pallas_call(
        mm_body, name="inproj_bwd", grid=(t_n // tm, n_k),
        in_specs=[pl.BlockSpec((tm, SEG), functools.partial(
            lambda i, k, s: (jnp.where(k >= s, i, jnp.maximum(i - 1, 0)), 0), s=s)) for s in range(n_k)] + [
            pl.BlockSpec((SEG, D), lambda i, k: (k, 0))],
        out_specs=pl.BlockSpec((tm, D), lambda i, k: (i, 0)),
        out_shape=jax.ShapeDtypeStruct((t_n, D), F32),
        compiler_params=_cp("parallel", "arbitrary"),
    )(*dsegs, w)

    tf = min(t_n, 512)

    def fin_body(du_ref, x_ref, g_ref, dh1_ref, dx_ref, dg0_ref):
        @pl.when(pl.program_id(0) == 0)
        def _():
            dg0_ref[...] = jnp.zeros_like(dg0_ref)

        dx, dg_rows = _rms_bwd(x_ref[...], g_ref[...], du_ref[...])
        dg0_ref[...] += _rowsum(dg_rows)
        dx_ref[...] = dh1_ref[...] + dx

    row = lambda: pl.BlockSpec((tf, D), lambda i: (i, 0))
    return pl.pallas_call(
        fin_body, name="grad_x", grid=(t_n // tf,), in_specs=[row(), row(), _full((1, D)), row()],
        out_specs=[row(), _full((1, D))],
        out_shape=[jax.ShapeDtypeStruct((t_n, D), F32), jax.ShapeDtypeStruct((1, D), F32)],
        compiler_params=_cp("arbitrary"),
    )(du, x, g0, dh1)


def _blockdiag4(w):
    w4 = w.reshape(4, 4, 64, 1, 64).astype(MXU)
    same = (jnp.arange(4)[:, None, None, None] == jnp.arange(4)[None, None, :, None])
    return jnp.where(same[None], w4, jnp.zeros((), MXU)).reshape(4, 256, 256)


def _blockdiag4_extract(g):
    g5 = g.reshape(4, 4, 64, 4, 64)
    return jnp.stack([g5[:, q, :, q, :] for q in range(4)], axis=1).reshape(NH, 64, 64)


def _local_step(x, tgt, p, after_ssd=None, late_weights=None, send_mlp_grads=None, send_late_grads=None):
    f = lambda a: a.astype(F32)
    proj, u = _inproj(x, p["norm_mix_pre"], p["w_in_pT"])
    ssm_params = (p["cw_xs"], p["cw_bc"], p["cb_xs"], p["cb_bc"], p["dt_bias"], p["a_log"], p["d_skip_x"],
                  p["ssm_norm"])
    ya, sprev, yraw = _ssd_fwd(proj, *ssm_params)
    cb_lru = p["conv_lru_b"] if after_ssd is None else p["conv_lru_b"] + after_ssd(ya)
    lru_params = (p["conv_lru_w"], cb_lru, p["wa_bd"], p["wx_bd"], p["lru_ba"], p["lru_bx"], p["lru_lambda"])
    yb, h = _lru_fwd(proj, *lru_params)
    if late_weights is not None:
        p = dict(p, **late_weights(yb))
    h1, mix, merged = _merge_out(ya, yb, proj, x, p["w_out"], p["norm_mix_post"])
    hp, v, dout, dff, loss, dg3 = _mlp_fwd(h1, p["norm_mlp_pre"], p["w_upT"], p["w_down"], p["norm_mlp_post"], tgt)

    dv, dhp, hid = _mlp_bwd(dff, hp, p["w_upT"], p["w_down"])
    dh1, dmix, dya, dyb, dga, dgb, dg2, dg1 = _out_bwd(dv, h1, p["norm_mlp_pre"], dout, mix, ya, yb, proj, p["w_out"],
                                                     p["norm_mix_post"])
    d_w_out = _wgrad(merged, dmix, "wgrad_out")
    (dz, dxs, dbc, dcwx, dcwb, dcbx, dcbb, ddtb, dalog, ddsk, dnrm, d_w_down, d_w_up_t) = _ssd_bwd(
        dya, yraw, proj, sprev, *ssm_params, (hid, dff, dhp, v))
    if send_mlp_grads is not None:
        lru_params = (lru_params[0], lru_params[1] + send_mlp_grads(d_w_up_t, d_w_down, d_w_out)) + lru_params[2:]
    (dgl, dxl, dcwl, dcbl, dwa, dwx, dba, dbx, dlam) = _lru_bwd(dyb, proj, h, *lru_params)
    dsegs = [dz, dgl, dxl, dga, dgb, dxs, dbc]
    d_w_in_pt = _wgrad_segs(dsegs, u, "wgrad_in")
    g0 = p["norm_mix_pre"]
    if send_late_grads is not None:
        g0 = g0 + send_late_grads(d_w_in_pt)
    grad_x, dg0 = _inproj_bwd(dsegs, p["w_in_pT"], x, g0, dh1)
    grads = dict(
        norm_mix_pre=dg0, w_in_pT=d_w_in_pt, conv_ssm_w=jnp.concatenate([dcwx, dcwb], axis=1),
        conv_ssm_b=jnp.concatenate([dcbx, dcbb], axis=1), dt_bias=ddtb[:, :NH], a_log=dalog[:, :NH],
        d_skip=f(ddsk).reshape(NH, 64).sum(axis=1)[None, :], ssm_norm=dnrm, conv_lru_w=dcwl, conv_lru_b=dcbl,
        lru_wa=_blockdiag4_extract(dwa), lru_ba=dba, lru_wx=_blockdiag4_extract(dwx), lru_bx=dbx, lru_lambda=dlam,
        w_out=d_w_out, norm_mix_post=dg1, norm_mlp_pre=dg2, w_upT=d_w_up_t, w_down=d_w_down, norm_mlp_post=dg3)
    return loss[0, 0], grad_x, grads


W_IN_COLS = 6672


def _w_in_t_to_padded(wt):
    z, xs, bc, dt = wt[0:1024], wt[1024:2048], wt[2048:2560], wt[2560:2576]
    gl, xl, ga, gb = wt[2576:3600], wt[3600:4624], wt[4624:5648], wt[5648:6672]
    return jnp.concatenate([z, gl, xl, ga, gb, xs, bc, dt, jnp.zeros((NP - 6672, wt.shape[1]), wt.dtype)], axis=0)


def _w_in_t_from_padded(wp):
    z, gl, xl, ga, gb = (wp[SEG * s:SEG * (s + 1)] for s in range(5))
    xs, bc, dt = wp[5120:6144], wp[6144:6656], wp[6656:6672]
    return jnp.concatenate([z, xs, bc, dt, gl, xl, ga, gb], axis=0)


def _prep_params(full, big):
    f = lambda a: a.astype(F32)
    pad128 = lambda a: jnp.pad(f(a).reshape(1, -1), ((0, 0), (0, 128 - a.size)))
    cw = f(full["conv_ssm_w"])
    cb = f(full["conv_ssm_b"]).reshape(1, -1)
    return dict(
        big, norm_mix_pre=f(full["norm_mix_pre"]).reshape(1, D),
        cw_xs=cw[:, :D], cw_bc=cw[:, D:], cb_xs=cb[:, :D], cb_bc=cb[:, D:],
        dt_bias=pad128(full["dt_bias"]), a_log=pad128(full["a_log"]),
        d_skip_x=jnp.repeat(f(full["d_skip"]).reshape(-1), 64).reshape(1, D), ssm_norm=f(full["ssm_norm"]).reshape(1, D),
        conv_lru_w=f(full["conv_lru_w"]), conv_lru_b=f(full["conv_lru_b"]).reshape(1, D),
        wa_bd=_blockdiag4(full["lru_wa"]), wx_bd=_blockdiag4(full["lru_wx"]),
        lru_ba=f(full["lru_ba"]).reshape(1, D), lru_bx=f(full["lru_bx"]).reshape(1, D),
        lru_lambda=f(full["lru_lambda"]).reshape(1, D),
        norm_mix_post=f(full["norm_mix_post"]).reshape(1, D), norm_mlp_pre=f(full["norm_mlp_pre"]).reshape(1, D),
        norm_mlp_post=f(full["norm_mlp_post"]).reshape(1, D))


MESH_ID = pl.DeviceIdType.MESH
ANY = pl.BlockSpec(memory_space=pl.ANY)


def _my_place():
    x, y, c = lax.axis_index("x"), lax.axis_index("y"), lax.axis_index("c")
    return x, y, c, 4 * x + 2 * y + c


def _peer(x, y, c, k):
    return (x ^ ((k >> 2) & 1), y ^ ((k >> 1) & 1), c ^ (k & 1))


def _all_gather(pack, name):
    def body(in_ref, out_ref, send_sems, recv_sems, local_sem):
        x, y, c, me = _my_place()
        sibling = (x, y, 1 - c)
        flips = (4, 2, 6)

        def copy(j, block, to, src=None):
            return pltpu.make_async_remote_copy(
                src_ref=out_ref.at[block] if src is None else src, dst_ref=out_ref.at[block],
                send_sem=send_sems.at[j], recv_sem=recv_sems.at[j], device_id=to, device_id_type=MESH_ID)

        mine = pltpu.make_async_copy(in_ref, out_ref.at[me], local_sem)
        mine.start()
        first = [copy(0, me, sibling, src=in_ref)]
        first += [copy(1 + j, me, _peer(x, y, c, k), src=in_ref) for j, k in enumerate(flips)]
        for cp in first:
            cp.start()
        passed = [copy(4 + j, me ^ k, sibling) for j, k in enumerate(flips)]
        for j, k in enumerate(flips):
            copy(1 + j, me ^ k, (x, y, c)).wait_recv()
            passed[j].start()
        copy(0, me ^ 1, (x, y, c)).wait_recv()
        for j, k in enumerate(flips):
            copy(4 + j, me ^ k ^ 1, (x, y, c)).wait_recv()
        for cp in first + passed:
            cp.wait_send()
        mine.wait()

    return pl.pallas_call(
        body, name=name, in_specs=[ANY], out_specs=ANY,
        out_shape=jax.ShapeDtypeStruct((N_DEV,) + pack.shape, pack.dtype),
        scratch_shapes=[pltpu.SemaphoreType.DMA((N_DEV - 1,)), pltpu.SemaphoreType.DMA((N_DEV - 1,)),
                        pltpu.SemaphoreType.DMA],
    )(pack)


HBM = pl.BlockSpec(memory_space=pltpu.HBM)
SEM = pl.BlockSpec(memory_space=pltpu.SEMAPHORE)
PLAN_GATHER = tuple((k, "pack", 0) for k in range(1, N_DEV))
PLAN_SCATTER = tuple((k, "slot", 0) for k in range(1, N_DEV))
PLAN_GATHER_ICI = tuple((k, "pack", 0) for k in (2, 4, 6))
PLAN_GATHER_D2D = ((1, "pack", 0),) + tuple((1, s, s) for s in (2, 4, 6))


def _plan_copy(j, plan, src_ref, land_ref, sems):
    k, source, slot = plan[j]
    x, y, c, me = _my_place()
    if source == "pack":
        src = src_ref
    elif source == "slot":
        src = src_ref.at[me ^ k]
    else:
        src = land_ref.at[me ^ source]
    return pltpu.make_async_remote_copy(
        src_ref=src, dst_ref=land_ref.at[me ^ slot], send_sem=sems[j], recv_sem=sems[len(plan) + j],
        device_id=_peer(x, y, c, k), device_id_type=MESH_ID)


def _exchange_start(src, land, plan, name):
    n_c = len(plan)
    if land is None:
        land = pltpu.with_memory_space_constraint(lax.empty((N_DEV,) + src.shape[-2:], src.dtype), pltpu.HBM)

    def body(src_ref, land_ref, *rest):
        sems, token = rest[:2 * n_c], rest[2 * n_c + 2]
        for j in range(n_c):
            _plan_copy(j, plan, src_ref, land_ref, sems).start()
        token[...] = jnp.zeros_like(token)

    outs = pl.pallas_call(
        body, name=name,
        out_shape=(pltpu.SemaphoreType.DMA(()),) * (2 * n_c) + (
            pltpu.HBM(src.shape, src.dtype), pltpu.HBM(land.shape, land.dtype), jax.ShapeDtypeStruct((8, 128), F32)),
        in_specs=(HBM, HBM), out_specs=(SEM,) * (2 * n_c) + (HBM, HBM, pl.BlockSpec(memory_space=pltpu.VMEM)),
        input_output_aliases={0: 2 * n_c, 1: 2 * n_c + 1},
        compiler_params=pltpu.CompilerParams(has_side_effects=pltpu.SideEffectType.DATAFLOW_SIDE_EFFECTING),
    )(pltpu.with_memory_space_constraint(src, pltpu.HBM), land)
    return outs[:2 * n_c], outs[2 * n_c], outs[2 * n_c + 1], outs[2 * n_c + 2]


def _exchange_wait(sems, src_thru, land_thru, after, plan, name):
    n_c = len(plan)

    def body(src_ref, land_ref, *rest):
        for j in range(n_c):
            cp = _plan_copy(j, plan, src_ref, land_ref, rest[:2 * n_c])
            cp.wait_send()
            cp.wait_recv()

    return pl.pallas_call(
        body, name=name,
        out_shape=(pltpu.HBM(src_thru.shape, src_thru.dtype), pltpu.HBM(land_thru.shape, land_thru.dtype)),
        in_specs=(HBM, HBM) + (SEM,) * (2 * n_c) + (pl.BlockSpec(memory_space=pl.ANY),), out_specs=(HBM, HBM),
        input_output_aliases={0: 0, 1: 1},
        compiler_params=pltpu.CompilerParams(has_side_effects=pltpu.SideEffectType.DATAFLOW_SIDE_EFFECTING),
    )(src_thru, land_thru, *sems, after)


def _slot_sum(parts, name):
    r_n, c_n = parts.shape[1:]
    tr = max(t for t in range(16, 513, 16) if r_n % t == 0)

    def body(p_ref, o_ref):
        acc = p_ref[0].astype(F32)
        for k in range(1, N_DEV):
            acc = acc + p_ref[k].astype(F32)
        o_ref[...] = acc

    return pl.pallas_call(
        body, name=name, grid=(r_n // tr,),
        in_specs=[pl.BlockSpec((N_DEV, tr, c_n), lambda i: (0, i, 0))],
        out_specs=pl.BlockSpec((tr, c_n), lambda i: (i, 0)),
        out_shape=jax.ShapeDtypeStruct((r_n, c_n), F32),
        compiler_params=_cp("parallel"),
    )(parts)


def _adam_math(w, g, m, v):
    m = ADAM_B1 * m + (1.0 - ADAM_B1) * g
    v = ADAM_B2 * v + (1.0 - ADAM_B2) * jnp.square(g)
    m_hat = m / (1.0 - ADAM_B1 ** ADAM_STEP)
    v_hat = v / (1.0 - ADAM_B2 ** ADAM_STEP)
    return -ADAM_LR * (m_hat / (jnp.sqrt(v_hat) + ADAM_EPS) + ADAM_WD * w), m, v


def _adam_big(w, g, m, v, name):
    def body(w_ref, g_ref, m_ref, v_ref, d_ref, mo_ref, vo_ref):
        d_ref[...], mo_ref[...], vo_ref[...] = _adam_math(w_ref[...], g_ref[...], m_ref[...], v_ref[...])

    if w.ndim == 3:
        _, r_n, c_n = w.shape
        tr = min(r_n, 256)
        grid = (r_n // tr,)
        blk = lambda: pl.BlockSpec((1, tr, c_n), lambda i: (0, i, 0))
    else:
        r_n, c_n = w.shape
        tc = min(c_n, 256)
        grid = (c_n // tc,)
        blk = lambda: pl.BlockSpec((r_n, tc), lambda i: (0, i))
    return pl.pallas_call(
        body, name=name, grid=grid, in_specs=[blk(), blk(), blk(), blk()], out_specs=[blk(), blk(), blk()],
        out_shape=[jax.ShapeDtypeStruct(w.shape, F32)] * 3, compiler_params=_cp("parallel"),
    )(w, g, m, v)


def _adam_small(groups, where, wmv):
    n, n_g = len(wmv), len(groups)

    def body(*refs):
        g_refs = refs[:n_g]
        w_refs = refs[n_g:n_g + 3 * n]
        o_refs = refs[n_g + 3 * n:]
        for q in range(n):
            w_ref, m_ref, v_ref = w_refs[3 * q:3 * q + 3]
            r, c = w_ref.shape
            gi, r0 = where[q]
            g = g_refs[gi][0, r0:r0 + r, 0:c]
            for k in range(1, N_DEV):
                g = g + g_refs[gi][k, r0:r0 + r, 0:c]
            d, m, v = _adam_math(w_ref[...], g, m_ref[...], v_ref[...])
            o_refs[4 * q][...] = g
            o_refs[4 * q + 1][...] = d
            o_refs[4 * q + 2][...] = m
            o_refs[4 * q + 3][...] = v

    flat_wmv = [a for t in wmv for a in t]
    vm = pl.BlockSpec(memory_space=pltpu.VMEM)
    outs = pl.pallas_call(
        body, name="adam_small", in_specs=[vm] * (n_g + 3 * n), out_specs=[vm] * (4 * n),
        out_shape=[jax.ShapeDtypeStruct(t[0].shape, F32) for t in wmv for _ in range(4)],
        compiler_params=pltpu.CompilerParams(vmem_limit_bytes=VMEM_LIMIT),
    )(*groups, *flat_wmv)
    return [tuple(outs[4 * q:4 * q + 4]) for q in range(n)]


WEIGHTS = ["norm_mix_pre", "w_in", "conv_ssm_w", "conv_ssm_b", "dt_bias", "a_log", "d_skip", "ssm_norm", "conv_lru_w",
           "conv_lru_b", "lru_wa", "lru_ba", "lru_wx", "lru_bx", "lru_lambda", "w_out", "norm_mix_post", "norm_mlp_pre",
           "w_up", "w_down", "norm_mlp_post"]
BIG = ["w_out", "w_up", "w_down", "w_in"]
IN_ROWS = W_IN_COLS // N_DEV
IN_PAD, EARLY_ROWS = 848, 880
ROW_UP, ROW_DOWN, LATE_ROWS = 128, 640, 1152
GRAD_LATE_ROWS = 864
CONV_SSM_COLS, CONV_LRU_COLS = 1536 // N_DEV, D // N_DEV
SMALL = [("norm_mix_pre", (1, D), 0, 0), ("ssm_norm", (1, D), 0, 1), ("conv_lru_b", (1, D), 0, 2),
         ("lru_lambda", (1, D), 0, 3), ("norm_mix_post", (1, D), 0, 4), ("norm_mlp_pre", (1, D), 0, 5),
         ("norm_mlp_post", (1, D), 0, 6), ("conv_ssm_b", (1, 1536), 1, 0), ("dt_bias", (1, NH), 2, 0),
         ("a_log", (1, NH), 2, 1), ("d_skip", (1, NH), 2, 2), ("conv_ssm_w", (4, CONV_SSM_COLS), 3, 0),
         ("conv_lru_w", (4, CONV_LRU_COLS), 4, 0), ("lru_wa", (D, 64), 5, 0), ("lru_wx", (D, 64), 5, D),
         ("lru_ba", (NH, 64), 6, 0), ("lru_bx", (NH, 64), 6, NH)]
SMALL_GROUPS = [(8, D), (1, 1536), (8, 128), (4, 1536), (4, D), (2 * D, 64), (2 * NH, 64)]


def _pad_rows(flat, mult):
    n = flat.shape[0]
    rows = -(-n // (128 * mult)) * mult
    return jnp.pad(flat, (0, rows * 128 - n)).reshape(rows, 128)


def _split3(a):
    hi = a.astype(MXU)
    r1 = a - hi.astype(F32)
    mid = r1.astype(MXU)
    lo = (r1 - mid.astype(F32)).astype(MXU)
    return jnp.stack([hi, mid, lo])


def _early_pack(a, me):
    bf = lambda t: t.astype(MXU)
    conv = lambda t, c: jnp.pad(_split3(t).reshape(12, c), ((0, 4), (0, D - c)))
    shifted = lax.dynamic_update_slice(jnp.zeros((IN_PAD, D), MXU), bf(a["w_in"][0]).T, (2 * me, 0))
    return jnp.concatenate([shifted, conv(a["conv_ssm_w"][0], CONV_SSM_COLS), conv(a["conv_lru_w"][0], CONV_LRU_COLS)],
                           axis=0)


TILE = 16
SHARD_TILES = IN_PAD // TILE
SHARD_STEP = (IN_ROWS // TILE)
SEG_TILES = ((0, 64, 0), (64, 128, 320), (128, 160, 384), (160, 161, 416), (161, 225, 64), (225, 289, 128),
             (289, 353, 192), (353, 417, 256))


def _tile_runs(lo, hi):
    runs = []
    for s0, s1, d0 in SEG_TILES:
        a, b = max(lo, s0), min(hi, s1)
        if a < b:
            runs.append((a, b - a, d0 + a - s0))
    return runs


def _assemble_w_in(g):
    whole = []
    for k in range(N_DEV):
        lo = SHARD_STEP * k + (1 if k else 0)
        hi = SHARD_STEP * (k + 1) + (1 if k == N_DEV - 1 else 0)
        whole += [(k, a - SHARD_STEP * k, n, d) for a, n, d in _tile_runs(lo, hi)]
    split = [(k, _tile_runs(SHARD_STEP * k, SHARD_STEP * k + 1)[0][2]) for k in range(1, N_DEV)]

    def body(g_ref, o_ref):
        rows = lambda t, n=1: pl.ds(TILE * t, TILE * n)
        for k, t, n, d in whole:
            o_ref[rows(d, n), :] = g_ref[k, rows(t, n), :]
        for k, d in split:
            o_ref[rows(d), :] = g_ref[k - 1, rows(SHARD_STEP), :] + g_ref[k, rows(0), :]
        o_ref[pl.ds(W_IN_COLS, NP - W_IN_COLS), :] = jnp.zeros((NP - W_IN_COLS, D), o_ref.dtype)

    vm = pl.BlockSpec(memory_space=pltpu.VMEM)
    return pl.pallas_call(
        body, name="assemble_w_in", in_specs=[vm], out_specs=vm, out_shape=jax.ShapeDtypeStruct((NP, D), g.dtype),
        compiler_params=pltpu.CompilerParams(vmem_limit_bytes=VMEM_LIMIT),
    )(g)


def _scatter_w_in_grad(dw):
    runs = [(k, a - SHARD_STEP * k, n, d) for k in range(N_DEV)
            for a, n, d in _tile_runs(SHARD_STEP * k, SHARD_STEP * k + SHARD_TILES)]
    pad = GRAD_LATE_ROWS - IN_PAD

    def body(dw_ref, o_ref):
        rows = lambda t, n: pl.ds(TILE * t, TILE * n)
        for k, t, n, d in runs:
            o_ref[k, rows(t, n), :] = dw_ref[rows(d, n), :]
        for k in range(N_DEV):
            o_ref[k, pl.ds(IN_PAD, pad), :] = jnp.zeros((pad, D), o_ref.dtype)

    vm = pl.BlockSpec(memory_space=pltpu.VMEM)
    return pl.pallas_call(
        body, name="scatter_w_in_grad", in_specs=[vm], out_specs=vm,
        out_shape=jax.ShapeDtypeStruct((N_DEV, GRAD_LATE_ROWS, D), dw.dtype),
        compiler_params=pltpu.CompilerParams(vmem_limit_bytes=VMEM_LIMIT),
    )(dw)


def _early_unpack(g):
    w_in_pt = _assemble_w_in(g)
    conv = {}
    for n, r0, c in (("conv_ssm_w", IN_PAD, CONV_SSM_COLS), ("conv_lru_w", IN_PAD + 16, CONV_LRU_COLS)):
        s = g[:, r0:r0 + 12, :c].astype(F32).reshape(N_DEV, 3, 4, c)
        conv[n] = ((s[:, 0] + s[:, 1]) + s[:, 2]).transpose(1, 0, 2).reshape(4, N_DEV * c)
    return w_in_pt, conv


def _late_pack(a):
    bf = lambda t: t.astype(MXU)
    return jnp.concatenate([bf(a["w_out"][0]), bf(a["w_up"][0]).T, bf(a["w_down"][0])], axis=0)


def _late_unpack(g):
    return dict(w_out=g[:, :ROW_UP].reshape(D, D), w_upT=g[:, ROW_UP:ROW_DOWN].reshape(FF, D),
                w_down=g[:, ROW_DOWN:].reshape(FF, D))


def _own_slot(land, own):
    me = 4 * lax.axis_index("x") + 2 * lax.axis_index("y") + lax.axis_index("c")
    return lax.dynamic_update_slice_in_dim(land, own[None], me, axis=0)


def kernel(x, norm_mix_pre, w_in, conv_ssm_w, conv_ssm_b, dt_bias, a_log, d_skip, ssm_norm, conv_lru_w, conv_lru_b, lru_wa, lru_ba, lru_wx, lru_bx, lru_lambda, w_out, norm_mix_post, norm_mlp_pre, w_up, w_down, norm_mlp_post, loss_target, m_norm_mix_pre, m_w_in, m_conv_ssm_w, m_conv_ssm_b, m_dt_bias, m_a_log, m_d_skip, m_ssm_norm, m_conv_lru_w, m_conv_lru_b, m_lru_wa, m_lru_ba, m_lru_wx, m_lru_bx, m_lru_lambda, m_w_out, m_norm_mix_post, m_norm_mlp_pre, m_w_up, m_w_down, m_norm_mlp_post, v_norm_mix_pre, v_w_in, v_conv_ssm_w, v_conv_ssm_b, v_dt_bias, v_a_log, v_d_skip, v_ssm_norm, v_conv_lru_w, v_conv_lru_b, v_lru_wa, v_lru_ba, v_lru_wx, v_lru_bx, v_lru_lambda, v_w_out, v_norm_mix_post, v_norm_mlp_pre, v_w_up, v_w_down, v_norm_mlp_post):
    vals = (norm_mix_pre, w_in, conv_ssm_w, conv_ssm_b, dt_bias, a_log, d_skip, ssm_norm, conv_lru_w, conv_lru_b, lru_wa, lru_ba, lru_wx, lru_bx, lru_lambda, w_out, norm_mix_post, norm_mlp_pre, w_up, w_down, norm_mlp_post)
    m_vals = (m_norm_mix_pre, m_w_in, m_conv_ssm_w, m_conv_ssm_b, m_dt_bias, m_a_log, m_d_skip, m_ssm_norm, m_conv_lru_w, m_conv_lru_b, m_lru_wa, m_lru_ba, m_lru_wx, m_lru_bx, m_lru_lambda, m_w_out, m_norm_mix_post, m_norm_mlp_pre, m_w_up, m_w_down, m_norm_mlp_post)
    v_vals = (v_norm_mix_pre, v_w_in, v_conv_ssm_w, v_conv_ssm_b, v_dt_bias, v_a_log, v_d_skip, v_ssm_norm, v_conv_lru_w, v_conv_lru_b, v_lru_wa, v_lru_ba, v_lru_wx, v_lru_bx, v_lru_lambda, v_w_out, v_norm_mix_post, v_norm_mlp_pre, v_w_up, v_w_down, v_norm_mlp_post)
    w = dict(zip(WEIGHTS, vals))
    m = dict(zip(WEIGHTS, m_vals))
    v = dict(zip(WEIGHTS, v_vals))
    me = 4 * lax.axis_index("x") + 2 * lax.axis_index("y") + lax.axis_index("c")

    bf = lambda t: t.astype(MXU)
    late = _late_pack(w)
    early = _all_gather(_early_pack(w, me), "early_weights_all_gather")
    late, early = lax.optimization_barrier((late, early))
    lw = {}
    lw["sems"], lw["src"], lw["land"], token = _exchange_start(late, None, PLAN_GATHER_ICI, "late_weights_ici_start")
    w_in_pt, conv_w = _early_unpack(early)
    full = {n: (conv_w[n] if n in conv_w else w[n][0]) for n in WEIGHTS if n not in BIG}
    full["norm_mix_pre"] = full["norm_mix_pre"] + token[0, 0]

    def after_ssd(after):
        src, land = _exchange_wait(lw["sems"], lw["src"], lw["land"], after, PLAN_GATHER_ICI, "late_weights_ici_wait")
        lw["sems"], lw["src"], lw["land"], tok = _exchange_start(src, land, PLAN_GATHER_D2D, "late_weights_d2d_start")
        return tok[0, 0]

    def late_weights(after):
        src, land = _exchange_wait(lw["sems"], lw["src"], lw["land"], after, PLAN_GATHER_D2D, "late_weights_d2d_wait")
        return _late_unpack(_own_slot(land, src))

    sent = {}

    def send_mlp_grads(d_w_up_t, d_w_down, d_w_out):
        src = jnp.concatenate([bf(d_w_up_t).reshape(N_DEV, -1, D), bf(d_w_down).reshape(N_DEV, -1, D),
                               bf(d_w_out).reshape(N_DEV, -1, D)], axis=1)
        sent["sems"], sent["src"], sent["land"], tok = _exchange_start(src, None, PLAN_SCATTER, "mlp_grads_start")
        return tok[0, 0]

    def send_late_grads(d_w_in_pt):
        src = _scatter_w_in_grad(d_w_in_pt)
        sent["sems2"], sent["src2"], sent["land2"], tok = _exchange_start(src, None, PLAN_SCATTER, "late_grads_start")
        return tok[0, 0]

    loss, grad_x, g = _local_step(x[0], loss_target[0], _prep_params(full, dict(w_in_pT=w_in_pt)), after_ssd,
                                  late_weights, send_mlp_grads, send_late_grads)
    loss = lax.psum(loss, ("x", "y", "c"))

    own = lambda src: lax.dynamic_index_in_dim(src, me, keepdims=False)
    out_g, out_d, out_m, out_v = {}, {}, {}, {}
    mlp_src, mlp_land = _exchange_wait(sent["sems"], sent["src"], sent["land"], grad_x, PLAN_SCATTER, "mlp_grads_wait")
    g_mlp = _slot_sum(_own_slot(mlp_land, own(mlp_src)), "slot_sum_mlp")
    fs = FF // N_DEV
    for n, gn in (("w_up", g_mlp[:fs].T[None]), ("w_down", g_mlp[fs:2 * fs][None]), ("w_out", g_mlp[2 * fs:][None])):
        out_g[n] = gn
        out_d[n], out_m[n], out_v[n] = _adam_big(w[n], gn, m[n], v[n], "adam_" + n)
    zrow = jnp.zeros((1, D), F32)
    pad16 = lambda a: jnp.pad(a, ((0, 0), (0, 128 - NH)))
    small_parts = [
        jnp.concatenate([g["norm_mix_pre"], g["ssm_norm"], g["conv_lru_b"], g["lru_lambda"], g["norm_mix_post"],
                         g["norm_mlp_pre"], g["norm_mlp_post"], zrow], axis=0),
        g["conv_ssm_b"],
        jnp.concatenate([pad16(g["dt_bias"]), pad16(g["a_log"]), pad16(g["d_skip"]), jnp.zeros((5, 128), F32)], axis=0),
        g["conv_ssm_w"], g["conv_lru_w"],
        jnp.concatenate([g["lru_wa"].reshape(D, 64), g["lru_wx"].reshape(D, 64)], axis=0),
        jnp.concatenate([g["lru_ba"].reshape(NH, 64), g["lru_bx"].reshape(NH, 64)], axis=0)]
    small = _pad_rows(jnp.concatenate([s.reshape(-1) for s in small_parts]), 8)
    small, _ = lax.optimization_barrier((small, out_v["w_out"]))
    small_all = _all_gather(small, "small_grads_all_gather")

    lg_src, lg_land = _exchange_wait(sent["sems2"], sent["src2"], sent["land2"], small_all, PLAN_SCATTER, "late_grads_wait")
    g_late = _slot_sum(_own_slot(lg_land, own(lg_src)), "slot_sum_late")
    gt = lax.dynamic_slice(g_late, (2 * me, 0), (IN_ROWS, D))
    dt_, mt_, vt_ = _adam_big(w["w_in"][0].T, gt, m["w_in"][0].T, v["w_in"][0].T, "adam_w_in")
    out_g["w_in"], out_d["w_in"], out_m["w_in"], out_v["w_in"] = gt.T[None], dt_.T[None], mt_.T[None], vt_.T[None]
    sflat = small_all.reshape(N_DEV, -1)
    groups = []
    off = 0
    for r, c in SMALL_GROUPS:
        groups.append(sflat[:, off:off + r * c].reshape(N_DEV, r, c))
        off += r * c
    groups[3] = lax.dynamic_slice_in_dim(groups[3], me * CONV_SSM_COLS, CONV_SSM_COLS, axis=2)
    groups[4] = lax.dynamic_slice_in_dim(groups[4], me * CONV_LRU_COLS, CONV_LRU_COLS, axis=2)
    wmv = [(w[n].reshape(s), m[n].reshape(s), v[n].reshape(s)) for n, s, _, _ in SMALL]
    res = _adam_small(groups, [(gi, r0) for _, _, gi, r0 in SMALL], wmv)
    for (n, _, _, _), (g_n, d_n, m_n, v_n) in zip(SMALL, res):
        shape = w[n].shape
        out_g[n], out_d[n], out_m[n], out_v[n] = (g_n.reshape(shape), d_n.reshape(shape), m_n.reshape(shape),
                                                  v_n.reshape(shape))
    return (loss, grad_x[None], *[out_g[n] for n in WEIGHTS], *[out_d[n] for n in WEIGHTS],
            *[out_m[n] for n in WEIGHTS], *[out_v[n] for n in WEIGHTS])
```

```python
import functools

import jax
import jax.numpy as jnp
from jax import lax
from jax.experimental import pallas as pl
from jax.experimental.pallas import tpu as pltpu

F32 = jnp.float32
MXU = jnp.bfloat16
HI = lax.Precision.HIGHEST
EPS = 1e-6

D = 1024
NH = 16
NS = 128
CH = 128
FF = 4096
NP = 7168
SEG = 1024
LRU_C = 8.0
N_DEV = 8

ADAM_LR, ADAM_B1, ADAM_B2, ADAM_EPS, ADAM_WD, ADAM_STEP = 0.001, 0.9, 0.999, 1e-08, 0.01, 10

VMEM_LIMIT = 56 * 1024 * 1024


def _cp(*sem):
    return pltpu.CompilerParams(dimension_semantics=sem, vmem_limit_bytes=VMEM_LIMIT)


def _nn(a, b):
    return jnp.dot(a.astype(MXU), b.astype(MXU), preferred_element_type=F32)


def _nt(a, b):
    return lax.dot_general(a.astype(MXU), b.astype(MXU), (((1,), (1,)), ((), ())), preferred_element_type=F32)


def _tn(a, b):
    return lax.dot_general(a.astype(MXU), b.astype(MXU), (((0,), (0,)), ((), ())), preferred_element_type=F32)


_sigmoid = jax.nn.sigmoid


def _silu(x):
    return x * _sigmoid(x)


def _dsilu(x):
    s = _sigmoid(x)
    return s + x * s * (1.0 - s)


def _softplus(x):
    return jnp.maximum(x, 0.0) + jnp.log(1.0 + jnp.exp(-jnp.abs(x)))


def _rms(x, g):
    r = lax.rsqrt(jnp.mean(x * x, axis=-1, keepdims=True) + EPS)
    return x * r * g


def _rms_bwd(x, g, dy):
    r = lax.rsqrt(jnp.mean(x * x, axis=-1, keepdims=True) + EPS)
    gdy = g * dy
    dx = r * gdy - x * (r * r * r) * jnp.mean(x * gdy, axis=-1, keepdims=True)
    return dx, dy * x * r


def _rowsum(x):
    return jnp.sum(x, axis=0, keepdims=True)


def _taps_past(cur, prev8):
    r_n, c_n = cur.shape
    row = lax.broadcasted_iota(jnp.int32, (r_n, c_n), 0)
    out = []
    for k in range(4):
        s = 3 - k
        if s == 0:
            out.append(cur)
            continue
        head = jnp.concatenate([pltpu.roll(prev8, s, 0), jnp.zeros((r_n - 8, c_n), F32)], axis=0)
        out.append(jnp.where(row < s, head, pltpu.roll(cur, s, 0)))
    return out


def _taps_future(cur, fut8):
    r_n, c_n = cur.shape
    row = lax.broadcasted_iota(jnp.int32, (r_n, c_n), 0)
    out = []
    for k in range(4):
        s = 3 - k
        if s == 0:
            out.append(cur)
            continue
        tail = jnp.concatenate([jnp.zeros((r_n - 8, c_n), F32), pltpu.roll(fut8, 8 - s, 0)], axis=0)
        out.append(jnp.where(row >= r_n - s, tail, pltpu.roll(cur, r_n - s, 0)))
    return out


def _conv_apply(taps, w, b):
    acc = taps[0] * w[0:1, :]
    for k in range(1, 4):
        acc = acc + taps[k] * w[k:k + 1, :]
    return acc + b


def _inproj(x, g0, w):
    t_n = x.shape[0]
    tm = min(t_n, 1024)

    n_j = NP // SEG

    def body(x_ref, g_ref, w_ref, pb_ref, p6_ref, u_ref):
        j = pl.program_id(1)

        @pl.when(j == 0)
        def _():
            u_ref[...] = _rms(x_ref[...], g_ref[...]).astype(MXU)

        p = lax.dot_general(u_ref[...], w_ref[...], (((1,), (1,)), ((), ())), preferred_element_type=F32)

        @pl.when(j < n_j - 1)
        def _():
            pb_ref[...] = p.astype(MXU)

        @pl.when(j == n_j - 1)
        def _():
            p6_ref[...] = p

    pb, p6, u = pl.pallas_call(
        body, name="inproj", grid=(t_n // tm, n_j),
        in_specs=[pl.BlockSpec((tm, D), lambda i, j: (i, 0)), pl.BlockSpec((1, D), lambda i, j: (0, 0)),
                  pl.BlockSpec((SEG, D), lambda i, j: (j, 0))],
        out_specs=[pl.BlockSpec((tm, SEG), lambda i, j: (i, jnp.minimum(j, n_j - 2))),
                   pl.BlockSpec((tm, SEG), lambda i, j: (i, 0)), pl.BlockSpec((tm, D), lambda i, j: (i, 0))],
        out_shape=[jax.ShapeDtypeStruct((t_n, NP - SEG), MXU), jax.ShapeDtypeStruct((t_n, SEG), F32),
                   jax.ShapeDtypeStruct((t_n, D), MXU)],
        compiler_params=_cp("parallel", "arbitrary"),
    )(x, g0, w)
    return (pb, p6), u


def _ssd_prep(dtraw, dtb, alog):
    l_n = dtraw.shape[0]
    r = lax.broadcasted_iota(jnp.int32, (l_n, l_n), 0)
    c = lax.broadcasted_iota(jnp.int32, (l_n, l_n), 1)
    tril = (r >= c).astype(F32)
    triu = (r <= c).astype(F32)
    eye = (r == c).astype(F32)
    dt = _softplus(dtraw + dtb)
    adt = dt * (-jnp.exp(alog))
    ac = jnp.dot(tril, adt, preferred_element_type=F32, precision=HI)
    tn = (((0,), (0,)), ((), ()))
    ac_t = lax.dot_general(adt, triu, tn, preferred_element_type=F32, precision=HI)
    dt_t = lax.dot_general(dt, eye, tn, preferred_element_type=F32, precision=HI)
    return dt, dt_t, ac, ac_t, _rowsum(adt)


def _ssd_pair(j, xp, bg, cg, sp, dt, dt_t, ac, ac_t, aend):
    l_n = xp.shape[0]
    lane = lax.broadcasted_iota(jnp.int32, (l_n, 128), 1)
    sub = lax.broadcasted_iota(jnp.int32, (128, l_n), 0)
    lane1 = lax.broadcasted_iota(jnp.int32, (1, 128), 1)
    tri = lax.broadcasted_iota(jnp.int32, (l_n, l_n), 0) >= lax.broadcasted_iota(jnp.int32, (l_n, l_n), 1)
    lo = lax.broadcasted_iota(jnp.int32, (l_n, 128), 1) < 64
    lo_s = lax.broadcasted_iota(jnp.int32, (128, 128), 1) < 64
    cb = _nt(cg, bg)
    cs = _nn(cg, sp)
    x2 = jnp.concatenate([jnp.where(lo, xp, 0.0), jnp.where(lo, 0.0, xp)], axis=0)
    ws_, bs_, eo, ee = [], [], [], []
    for e in range(2):
        h = 2 * j + e
        ac_l = jnp.sum(jnp.where(lane == h, ac, 0.0), axis=1, keepdims=True)
        dt_l = jnp.sum(jnp.where(lane == h, dt, 0.0), axis=1, keepdims=True)
        a_end = jnp.sum(jnp.where(lane1 == h, aend, 0.0), axis=1, keepdims=True)
        ac_s, dt_s = ac_t[h:h + 1, :], dt_t[h:h + 1, :]
        decay = jnp.exp(jnp.where(tri, ac_l - ac_s, -1e30))
        ws_.append(cb * decay * dt_s)
        bs_.append(bg * (jnp.exp(a_end - ac_l) * dt_l))
        eo.append(jnp.exp(ac_l))
        ee.append(jnp.exp(a_end))
    y = _nn(jnp.concatenate(ws_, axis=1), x2) + jnp.where(lo, eo[0], eo[1]) * cs
    s_new = _tn(jnp.concatenate(bs_, axis=0), x2) + jnp.where(lo_s, ee[0], ee[1]) * sp
    return y, s_new


def _ssd_post(y, xs, z, dsk, nrm):
    y = (y + dsk * xs) * _silu(z)
    half = D // 2
    ya, yb = y[:, :half], y[:, half:]
    ya = ya * lax.rsqrt(jnp.mean(ya * ya, axis=-1, keepdims=True) + EPS)
    yb = yb * lax.rsqrt(jnp.mean(yb * yb, axis=-1, keepdims=True) + EPS)
    return jnp.concatenate([ya, yb], axis=1) * nrm


LAST_SEG = NP // SEG - 1


def _proj_ops(proj, seg_ids):
    return [proj[1] if s == LAST_SEG else proj[0] for s in seg_ids]


def _proj_specs(rows, seg_ids, order):
    return [pl.BlockSpec((rows, SEG), functools.partial(lambda i, c: (order(i), c), c=0 if s == LAST_SEG else s))
            for s in seg_ids]


def _prev_specs(rows, seg_ids, order):
    specs = []
    for s in seg_ids:
        n, c = (8, 0) if s == LAST_SEG else (16, s)
        specs.append(pl.BlockSpec((n, SEG), functools.partial(
            lambda i, n, c: (jnp.maximum(order(i) * (rows // n) - 1, 0), c), n=n, c=c)))
    return specs


def _prev8(ref):
    return ref[...] if ref.shape[0] == 8 else ref[8:16, :].astype(F32)


def _full(shape):
    return pl.BlockSpec(shape, lambda i: (0,) * len(shape))


def _ssd_fwd(proj, cwx, cwb, cbx, cbb, dtb, alog, dsk, nrm):
    t_n = proj[0].shape[0]
    n_c = t_n // CH
    fwd = lambda i: i

    def body(z_ref, xs_ref, bc_ref, xsp_ref, bcp_ref, cwx_ref, cwb_ref, cbx_ref, cbb_ref, dtb_ref, alog_ref,
             dsk_ref, nrm_ref, ya_ref, sprev_ref, yraw_ref, xspre_ref, bcpre_ref, s_ref):
        c = pl.program_id(0)

        @pl.when(c == 0)
        def _():
            s_ref[...] = jnp.zeros_like(s_ref)

        keep = jnp.where(c == 0, 0.0, 1.0)
        xs_pre = _conv_apply(_taps_past(xs_ref[...].astype(F32), _prev8(xsp_ref) * keep), cwx_ref[...], cbx_ref[...])
        bc_pre = _conv_apply(_taps_past(bc_ref[:, :512], bcp_ref[:, :512] * keep), cwb_ref[...], cbb_ref[...])
        xspre_ref[...] = xs_pre
        bcpre_ref[...] = bc_pre
        prep = _ssd_prep(bc_ref[:, 512:640], dtb_ref[...], alog_ref[...])
        xs = _silu(xs_pre)
        bc = _silu(bc_pre)
        sprev_ref[0] = s_ref[...]
        ys = []
        for j in range(NH // 2):
            g = j // 4
            yp, sn = _ssd_pair(j, xs[:, 128 * j:128 * j + 128], bc[:, 128 * g:128 * g + 128],
                               bc[:, 256 + 128 * g:384 + 128 * g], s_ref[:, 128 * j:128 * j + 128], *prep)
            ys.append(yp)
            s_ref[:, 128 * j:128 * j + 128] = sn
        y = jnp.concatenate(ys, axis=1)
        yraw_ref[...] = y
        ya_ref[...] = _ssd_post(y, xs, z_ref[...].astype(F32), dsk_ref[...], nrm_ref[...]).astype(ya_ref.dtype)

    return pl.pallas_call(
        body, name="ssd_fwd", grid=(n_c,),
        in_specs=_proj_specs(CH, (0, 5, 6), fwd) + _prev_specs(CH, (5, 6), fwd) + [
            _full((4, D)), _full((4, 512)), _full((1, D)), _full((1, 512)), _full((1, 128)), _full((1, 128)),
            _full((1, D)), _full((1, D))],
        out_specs=[pl.BlockSpec((CH, D), lambda i: (i, 0)), pl.BlockSpec((1, NS, D), lambda i: (i, 0, 0)),
                   pl.BlockSpec((CH, D), lambda i: (i, 0)), pl.BlockSpec((CH, D), lambda i: (i, 0)),
                   pl.BlockSpec((CH, 512), lambda i: (i, 0))],
        out_shape=[jax.ShapeDtypeStruct((t_n, D), MXU), jax.ShapeDtypeStruct((n_c, NS, D), F32),
                   jax.ShapeDtypeStruct((t_n, D), F32), jax.ShapeDtypeStruct((t_n, D), F32),
                   jax.ShapeDtypeStruct((t_n, 512), F32)],
        scratch_shapes=[pltpu.VMEM((NS, D), F32)],
        compiler_params=_cp("arbitrary"),
    )(*_proj_ops(proj, (0, 5, 6, 5, 6)), cwx, cwb, cbx, cbb, dtb, alog, dsk, nrm)


def _ssd_bwd(dya, saved, proj, sprev, cwx, cwb, dtb, alog, dsk, nrm, mlp_ops):
    t_n = proj[0].shape[0]
    n_c = t_n // CH
    rev = lambda i: n_c - 1 - i
    fb = FF // n_c

    def body(dya_ref, yraw_ref, xspre_ref, bcpre_ref, z_ref, xs_ref, bc_ref, sprev_ref, cwx_ref, cwb_ref,
             dtb_ref, alog_ref, dsk_ref, nrm_ref, hid_ref, dff_ref, dhp_ref, v_ref,
             dz_ref, dxs_ref, dbc_ref, dcwx_ref, dcwb_ref, dcbx_ref, dcbb_ref, ddtb_ref, dalog_ref, ddsk_ref,
             dnrm_ref, dwd_ref, dwu_ref, ds_ref, futx_ref, futb_ref):
        i = pl.program_id(0)
        acc_refs = (dcwx_ref, dcwb_ref, dcbx_ref, dcbb_ref, ddtb_ref, dalog_ref, ddsk_ref, dnrm_ref)
        tn = (((0,), (0,)), ((), ()))
        dwd_ref[...] = lax.dot_general(hid_ref[...], dff_ref[...], tn, preferred_element_type=F32).astype(MXU)
        dwu_ref[...] = lax.dot_general(dhp_ref[...], v_ref[...], tn, preferred_element_type=F32).astype(MXU)

        @pl.when(i == 0)
        def _():
            for r in (ds_ref, futx_ref, futb_ref) + acc_refs:
                r[...] = jnp.zeros_like(r)

        xs_pre = xspre_ref[...]
        bc_pre = bcpre_ref[...]
        xs = _silu(xs_pre)
        bc = _silu(bc_pre)
        prep, prep_vjp = jax.vjp(_ssd_prep, bc_ref[:, 512:640], dtb_ref[...], alog_ref[...])
        s_in = sprev_ref[0]

        def pair_args(j):
            g = j // 4
            return (xs[:, 128 * j:128 * j + 128], bc[:, 128 * g:128 * g + 128],
                    bc[:, 256 + 128 * g:384 + 128 * g], s_in[:, 128 * j:128 * j + 128]) + tuple(prep)

        _, post_vjp = jax.vjp(_ssd_post, yraw_ref[...], xs, z_ref[...].astype(F32), dsk_ref[...], nrm_ref[...])
        dy, dxs_skip, dz, ddsk, dnrm = post_vjp(dya_ref[...])
        dz_ref[...] = dz.astype(dz_ref.dtype)
        ddsk_ref[...] += ddsk
        dnrm_ref[...] += dnrm

        dprep = [jnp.zeros_like(p) for p in prep]
        dxp = []
        dbg = [jnp.zeros((CH, 128), F32), jnp.zeros((CH, 128), F32)]
        dcg = [jnp.zeros((CH, 128), F32), jnp.zeros((CH, 128), F32)]
        for j in range(NH // 2):
            g = j // 4
            _, pair_vjp = jax.vjp(functools.partial(_ssd_pair, j), *pair_args(j))
            cts = pair_vjp((dy[:, 128 * j:128 * j + 128], ds_ref[:, 128 * j:128 * j + 128]))
            dxp.append(cts[0])
            dbg[g] = dbg[g] + cts[1]
            dcg[g] = dcg[g] + cts[2]
            ds_ref[:, 128 * j:128 * j + 128] = cts[3]
            dprep = [a + b for a, b in zip(dprep, cts[4:])]
        ddtraw, ddtb, dalog = prep_vjp(tuple(dprep))
        ddtb_ref[...] += ddtb
        dalog_ref[...] += dalog

        dxs_pre = (dxs_skip + jnp.concatenate(dxp, axis=1)) * _dsilu(xs_pre)
        dbc_pre = jnp.concatenate([dbg[0], dbg[1], dcg[0], dcg[1]], axis=1) * _dsilu(bc_pre)
        dcbx_ref[...] += _rowsum(dxs_pre)
        dcbb_ref[...] += _rowsum(dbc_pre)
        fx = _taps_future(dxs_pre, futx_ref[...])
        fbc = _taps_future(dbc_pre, futb_ref[...])
        xs_in = xs_ref[...].astype(F32)
        bc_in = bc_ref[:, :512]
        for k in range(4):
            dcwx_ref[k:k + 1, :] += _rowsum(fx[k] * xs_in)
            dcwb_ref[k:k + 1, :] += _rowsum(fbc[k] * bc_in)
        cwx = cwx_ref[...]
        cwb = cwb_ref[...]
        dxs_in = fx[0] * cwx[0:1, :]
        dbc_in = fbc[0] * cwb[0:1, :]
        for k in range(1, 4):
            dxs_in = dxs_in + fx[k] * cwx[k:k + 1, :]
            dbc_in = dbc_in + fbc[k] * cwb[k:k + 1, :]
        futx_ref[...] = dxs_pre[0:8, :]
        futb_ref[...] = dbc_pre[0:8, :]
        dxs_ref[...] = dxs_in.astype(dxs_ref.dtype)
        dbc_ref[...] = jnp.concatenate([dbc_in, ddtraw, jnp.zeros((CH, SEG - 640), F32)], axis=1).astype(dbc_ref.dtype)

    row_out = lambda: pl.BlockSpec((CH, D), lambda i: (rev(i), 0))
    outs = pl.pallas_call(
        body, name="ssd_bwd", grid=(n_c,),
        in_specs=[row_out(), row_out(), row_out(), pl.BlockSpec((CH, 512), lambda i: (rev(i), 0))]
        + _proj_specs(CH, (0, 5, 6), rev) + [pl.BlockSpec((1, NS, D), lambda i: (rev(i), 0, 0)),
                                             _full((4, D)), _full((4, 512)),
                                             _full((1, 128)), _full((1, 128)), _full((1, D)), _full((1, D)),
                                             pl.BlockSpec((t_n, fb), lambda i: (0, i)), _full((t_n, D)),
                                             pl.BlockSpec((t_n, fb), lambda i: (0, i)), _full((t_n, D))],
        out_specs=[row_out(), row_out(), row_out(), _full((4, D)), _full((4, 512)), _full((1, D)), _full((1, 512)),
                   _full((1, 128)), _full((1, 128)), _full((1, D)), _full((1, D)),
                   pl.BlockSpec((fb, D), lambda i: (i, 0)), pl.BlockSpec((fb, D), lambda i: (i, 0))],
        out_shape=[jax.ShapeDtypeStruct((t_n, D), MXU)] * 3 + [
            jax.ShapeDtypeStruct(s, F32) for s in ((4, D), (4, 512), (1, D), (1, 512), (1, 128), (1, 128), (1, D), (1, D))]
        + [jax.ShapeDtypeStruct((FF, D), MXU)] * 2,
        scratch_shapes=[pltpu.VMEM((NS, D), F32), pltpu.VMEM((8, D), F32), pltpu.VMEM((8, 512), F32)],
        compiler_params=_cp("arbitrary"),
    )(dya, *saved, *_proj_ops(proj, (0, 5, 6)), sprev, cwx, cwb, dtb, alog, dsk, nrm, *mlp_ops)
    return outs


LRU_ROWS = 256
LRU_BLK = 256


def _lru_gates(xr, wa, wx, ba, bx, lam):
    pr = jnp.concatenate([_nn(xr[:, LRU_BLK * b:LRU_BLK * (b + 1)], wa[b]) for b in range(D // LRU_BLK)], axis=1) + ba
    pi = jnp.concatenate([_nn(xr[:, LRU_BLK * b:LRU_BLK * (b + 1)], wx[b]) for b in range(D // LRU_BLK)], axis=1) + bx
    log_a = -LRU_C * _sigmoid(pr) * _softplus(-lam)
    a = jnp.exp(log_a)
    mult = jnp.sqrt(1.0 - jnp.exp(2.0 * log_a))
    return a, mult * (_sigmoid(pi) * xr)


def _lru_out(h, g):
    return h * jax.nn.gelu(g, approximate=True)


def _lru_fwd(proj, cw, cb, wa, wx, ba, bx, lam):
    t_n = proj[0].shape[0]
    rows = min(LRU_ROWS, t_n)
    fwd = lambda i: i

    def body(g_ref, x_ref, xp_ref, cw_ref, cb_ref, wa_ref, wx_ref, ba_ref, bx_ref, lam_ref, yb_ref, h_ref, xr_ref,
             a_s, u_s, carry):
        i = pl.program_id(0)

        @pl.when(i == 0)
        def _():
            carry[...] = jnp.zeros_like(carry)

        keep = jnp.where(i == 0, 0.0, 1.0)
        xr = _conv_apply(_taps_past(x_ref[...].astype(F32), _prev8(xp_ref) * keep), cw_ref[...], cb_ref[...])
        xr_ref[...] = xr
        a, u = _lru_gates(xr, wa_ref[...], wx_ref[...], ba_ref[...], bx_ref[...], lam_ref[...])
        a_s[...] = a
        u_s[...] = u
        row = lax.broadcasted_iota(jnp.int32, (8, D), 0)

        def blk(b, c):
            s = pl.multiple_of(b * 8, 8)
            av = a_s[pl.ds(s, 8), :]
            uv = u_s[pl.ds(s, 8), :]
            for d in (1, 2, 4):
                m = row >= d
                uv = uv + av * jnp.where(m, pltpu.roll(uv, d, 0), 0.0)
                av = av * jnp.where(m, pltpu.roll(av, d, 0), 1.0)
            hv = uv + av * c
            h_ref[pl.ds(s, 8), :] = hv
            return hv[7:8, :]

        carry[0:1, :] = lax.fori_loop(0, rows // 8, blk, carry[0:1, :])
        yb_ref[...] = _lru_out(h_ref[...], g_ref[...].astype(F32)).astype(yb_ref.dtype)

    return pl.pallas_call(
        body, name="lru_fwd", grid=(t_n // rows,),
        in_specs=_proj_specs(rows, (1, 2), fwd) + _prev_specs(rows, (2,), fwd) + [
            _full((4, D)), _full((1, D)), _full((4, LRU_BLK, LRU_BLK)), _full((4, LRU_BLK, LRU_BLK)),
            _full((1, D)), _full((1, D)), _full((1, D))],
        out_specs=[pl.BlockSpec((rows, D), lambda i: (i, 0))] * 3,
        out_shape=[jax.ShapeDtypeStruct((t_n, D), MXU), jax.ShapeDtypeStruct((t_n, D), F32),
                   jax.ShapeDtypeStruct((t_n, D), F32)],
        scratch_shapes=[pltpu.VMEM((rows, D), F32), pltpu.VMEM((rows, D), F32), pltpu.VMEM((8, D), F32)],
        compiler_params=_cp("arbitrary"),
    )(*_proj_ops(proj, (1, 2, 2)), cw, cb, wa, wx, ba, bx, lam)


def _lru_bwd(dyb, proj, h, xr_saved, cw, wa, wx, ba, bx, lam):
    t_n = proj[0].shape[0]
    rows = min(LRU_ROWS, t_n)
    n_t = t_n // rows
    rev = lambda i: n_t - 1 - i
    rb = rows // 8

    def body(dyb_ref, g_ref, x_ref, h_ref, hp_ref, xr_ref, cw_ref, wa_ref, wx_ref, ba_ref, bx_ref, lam_ref,
             dg_ref, dx_ref, dcw_ref, dcb_ref, dwa_ref, dwx_ref, dba_ref, dbx_ref, dlam_ref,
             a_s, dh_s, hx_s, da_s, du_s, carry, fut):
        i = pl.program_id(0)
        acc_refs = (dcw_ref, dcb_ref, dwa_ref, dwx_ref, dba_ref, dbx_ref, dlam_ref)

        @pl.when(i == 0)
        def _():
            for r in (carry, fut) + acc_refs:
                r[...] = jnp.zeros_like(r)

        keep = jnp.where(i == n_t - 1, 0.0, 1.0)
        gate_in = (xr_ref[...], wa_ref[...], wx_ref[...], ba_ref[...], bx_ref[...], lam_ref[...])
        (a, _), gates_vjp = jax.vjp(_lru_gates, *gate_in)
        _, out_vjp = jax.vjp(_lru_out, h_ref[...], g_ref[...].astype(F32))
        dh, dg = out_vjp(dyb_ref[...])
        dg_ref[...] = dg.astype(dg_ref.dtype)
        a_s[...] = a
        dh_s[...] = dh
        hx_s[0:8, :] = hp_ref[...] * keep
        hx_s[8:, :] = h_ref[...]
        row = lax.broadcasted_iota(jnp.int32, (8, D), 0)

        def blk(b, c):
            s = pl.multiple_of((rb - 1 - b) * 8, 8)
            av = a_s[pl.ds(s, 8), :]
            dhv = dh_s[pl.ds(s, 8), :]
            kv = av * dhv
            for d in (1, 2, 4):
                m = row <= 7 - d
                kv = kv + av * jnp.where(m, pltpu.roll(kv, 8 - d, 0), 0.0)
                av = av * jnp.where(m, pltpu.roll(av, 8 - d, 0), 1.0)
            kv = kv + av * c
            gv = dhv + jnp.where(row < 7, pltpu.roll(kv, 7, 0), c)
            hb = hx_s[pl.ds(s + 8, 8), :]
            hpv = hx_s[pl.ds(s, 8), :]
            hprev = jnp.where(row >= 1, pltpu.roll(hb, 1, 0), hpv[7:8, :])
            du_s[pl.ds(s, 8), :] = gv
            da_s[pl.ds(s, 8), :] = gv * hprev
            return kv[0:1, :]

        carry[0:1, :] = lax.fori_loop(0, rb, blk, carry[0:1, :])
        dxr, dwa, dwx, dba, dbx, dlam = gates_vjp((da_s[...], du_s[...]))
        dwa_ref[...] += dwa
        dwx_ref[...] += dwx
        dba_ref[...] += dba
        dbx_ref[...] += dbx
        dlam_ref[...] += dlam
        dcb_ref[...] += _rowsum(dxr)
        ft = _taps_future(dxr, fut[...])
        x_in = x_ref[...].astype(F32)
        for k in range(4):
            dcw_ref[k:k + 1, :] += _rowsum(ft[k] * x_in)
        cwv = cw_ref[...]
        dx = ft[0] * cwv[0:1, :]
        for k in range(1, 4):
            dx = dx + ft[k] * cwv[k:k + 1, :]
        fut[...] = dxr[0:8, :]
        dx_ref[...] = dx.astype(dx_ref.dtype)

    row_in = lambda: pl.BlockSpec((rows, D), lambda i: (rev(i), 0))
    prev_h = pl.BlockSpec((8, D), lambda i: (jnp.maximum(rev(i) * rb - 1, 0), 0))
    wspec = lambda: _full((4, LRU_BLK, LRU_BLK))
    return pl.pallas_call(
        body, name="lru_bwd", grid=(n_t,),
        in_specs=[row_in()] + _proj_specs(rows, (1, 2), rev) + [row_in(), prev_h, row_in()] + [
            _full((4, D)), wspec(), wspec(), _full((1, D)), _full((1, D)), _full((1, D))],
        out_specs=[row_in(), row_in(), _full((4, D)), _full((1, D)), wspec(), wspec(), _full((1, D)), _full((1, D)),
                   _full((1, D))],
        out_shape=[jax.ShapeDtypeStruct((t_n, D), MXU)] * 2 + [
            jax.ShapeDtypeStruct(s, F32) for s in ((4, D), (1, D), (4, LRU_BLK, LRU_BLK), (4, LRU_BLK, LRU_BLK),
                                                   (1, D), (1, D), (1, D))],
        scratch_shapes=[pltpu.VMEM((rows, D), F32), pltpu.VMEM((rows, D), F32), pltpu.VMEM((rows + 8, D), F32),
                        pltpu.VMEM((rows, D), F32), pltpu.VMEM((rows, D), F32), pltpu.VMEM((8, D), F32),
                        pltpu.VMEM((8, D), F32)],
        compiler_params=_cp("arbitrary"),
    )(dyb, *_proj_ops(proj, (1, 2)), h, h, xr_saved, cw, wa, wx, ba, bx, lam)


def _merge_out(ya, yb, proj, x, wout, g1):
    t_n = x.shape[0]
    tm = min(t_n, 512)

    def body(ya_ref, yb_ref, ga_ref, gb_ref, x_ref, w_ref, g_ref, h1_ref, mix_ref, mg_ref):
        merged = (_sigmoid(ga_ref[...].astype(F32)) * ya_ref[...].astype(F32)
                  + _sigmoid(gb_ref[...].astype(F32)) * yb_ref[...].astype(F32))
        mg = merged.astype(MXU)
        mg_ref[...] = mg
        mix = jnp.dot(mg, w_ref[...], preferred_element_type=F32)
        mix_ref[...] = mix
        h1_ref[...] = x_ref[...] + _rms(mix, g_ref[...])

    row = lambda: pl.BlockSpec((tm, D), lambda i: (i, 0))
    return pl.pallas_call(
        body, name="merge_out", grid=(t_n // tm,),
        in_specs=[row(), row()] + _proj_specs(tm, (3, 4), lambda i: i) + [row(), _full((D, D)), _full((1, D))],
        out_specs=[row(), row(), row()],
        out_shape=[jax.ShapeDtypeStruct((t_n, D), F32), jax.ShapeDtypeStruct((t_n, D), F32),
                   jax.ShapeDtypeStruct((t_n, D), MXU)],
        compiler_params=_cp("parallel"),
    )(ya, yb, *_proj_ops(proj, (3, 4)), x, wout, g1)


def _out_bwd(dv, h1, g2, dout, mix, ya, yb, proj, wout, g1):
    t_n = dv.shape[0]
    tm = min(t_n, 256)

    def body(dv_ref, h1_ref, g2_ref, dout_ref, mix_ref, ya_ref, yb_ref, ga_ref, gb_ref, w_ref, g_ref,
             dh1_ref, dmix_ref, dya_ref, dyb_ref, dga_ref, dgb_ref, dg2_ref, dg1_ref):
        @pl.when(pl.program_id(0) == 0)
        def _():
            dg1_ref[...] = jnp.zeros_like(dg1_ref)
            dg2_ref[...] = jnp.zeros_like(dg2_ref)

        dx, dg_rows = _rms_bwd(h1_ref[...], g2_ref[...], dv_ref[...])
        dg2_ref[...] += _rowsum(dg_rows)
        dh1 = dout_ref[...] + dx
        dh1_ref[...] = dh1
        dmix, dg_rows = _rms_bwd(mix_ref[...], g_ref[...], dh1)
        dg1_ref[...] += _rowsum(dg_rows)
        dmix_b = dmix.astype(MXU)
        dmix_ref[...] = dmix_b
        dmg = lax.dot_general(dmix_b, w_ref[...], (((1,), (1,)), ((), ())), preferred_element_type=F32)
        sa = _sigmoid(ga_ref[...].astype(F32))
        sb = _sigmoid(gb_ref[...].astype(F32))
        dya_ref[...] = dmg * sa
        dyb_ref[...] = dmg * sb
        dga_ref[...] = (dmg * ya_ref[...].astype(F32) * sa * (1.0 - sa)).astype(MXU)
        dgb_ref[...] = (dmg * yb_ref[...].astype(F32) * sb * (1.0 - sb)).astype(MXU)

    row = lambda: pl.BlockSpec((tm, D), lambda i: (i, 0))
    vec = lambda: _full((1, D))
    return pl.pallas_call(
        body, name="out_bwd", grid=(t_n // tm,),
        in_specs=[row(), row(), vec(), row(), row(), row(), row()] + _proj_specs(tm, (3, 4), lambda i: i)
        + [_full((D, D)), vec()],
        out_specs=[row(), row(), row(), row(), row(), row(), vec(), vec()],
        out_shape=[jax.ShapeDtypeStruct((t_n, D), F32), jax.ShapeDtypeStruct((t_n, D), MXU),
                   jax.ShapeDtypeStruct((t_n, D), F32), jax.ShapeDtypeStruct((t_n, D), F32),
                   jax.ShapeDtypeStruct((t_n, D), MXU), jax.ShapeDtypeStruct((t_n, D), MXU),
                   jax.ShapeDtypeStruct((1, D), F32), jax.ShapeDtypeStruct((1, D), F32)],
        compiler_params=_cp("arbitrary"),
    )(dv, h1, g2, dout, mix, ya, yb, *_proj_ops(proj, (3, 4)), wout, g1)


MLP_TM = 1024
MLP_TF_FWD = 1024
MLP_TF_BWD = 1024


def _mlp_fwd(h1, g2, wup, wdown, g3, tgt):
    t_n = h1.shape[0]
    tm = min(t_n, MLP_TM)
    n_f = FF // MLP_TF_FWD

    def body(h1_ref, g2_ref, wu_ref, wd_ref, g3_ref, tgt_ref, hp_ref, v_ref, dout_ref, dff_ref, loss_ref, dg3_ref, acc):
        i, j = pl.program_id(0), pl.program_id(1)

        @pl.when((i == 0) & (j == 0))
        def _():
            loss_ref[...] = jnp.zeros_like(loss_ref)
            dg3_ref[...] = jnp.zeros_like(dg3_ref)

        @pl.when(j == 0)
        def _():
            v_ref[...] = _rms(h1_ref[...], g2_ref[...]).astype(MXU)
            acc[...] = jnp.zeros_like(acc)

        hp = lax.dot_general(v_ref[...], wu_ref[...], (((1,), (1,)), ((), ())), preferred_element_type=F32)
        hp_ref[...] = hp.astype(MXU)
        hid = jnp.square(jnp.maximum(hp, 0.0))
        acc[...] += jnp.dot(hid.astype(MXU), wd_ref[...], preferred_element_type=F32)

        @pl.when(j == n_f - 1)
        def _():
            ff = acc[...]
            err = h1_ref[...] + _rms(ff, g3_ref[...]) - tgt_ref[...]
            loss_ref[...] += 0.5 * jnp.sum(jnp.mean(err * err, axis=-1, keepdims=True), axis=0, keepdims=True)
            dout = err * (1.0 / D)
            dout_ref[...] = dout
            dff, dg_rows = _rms_bwd(ff, g3_ref[...], dout)
            dg3_ref[...] += _rowsum(dg_rows)
            dff_ref[...] = dff.astype(MXU)

    row = lambda: pl.BlockSpec((tm, D), lambda i, j: (i, 0))
    vec = lambda: pl.BlockSpec((1, D), lambda i, j: (0, 0))
    return pl.pallas_call(
        body, name="mlp_fwd", grid=(t_n // tm, n_f),
        in_specs=[row(), vec(), pl.BlockSpec((MLP_TF_FWD, D), lambda i, j: (j, 0)),
                  pl.BlockSpec((MLP_TF_FWD, D), lambda i, j: (j, 0)), vec(), row()],
        out_specs=[pl.BlockSpec((tm, MLP_TF_FWD), lambda i, j: (i, j)), row(), row(), row(),
                   pl.BlockSpec((1, 1), lambda i, j: (0, 0)), vec()],
        out_shape=[jax.ShapeDtypeStruct((t_n, FF), MXU), jax.ShapeDtypeStruct((t_n, D), MXU),
                   jax.ShapeDtypeStruct((t_n, D), F32), jax.ShapeDtypeStruct((t_n, D), MXU),
                   jax.ShapeDtypeStruct((1, 1), F32), jax.ShapeDtypeStruct((1, D), F32)],
        scratch_shapes=[pltpu.VMEM((tm, D), F32)],
        compiler_params=_cp("arbitrary", "arbitrary"),
    )(h1, g2, wup, wdown, g3, tgt)


def _mlp_bwd(dff, hp, wup, wdown):
    t_n = dff.shape[0]
    tm = min(t_n, MLP_TM)
    n_f = FF // MLP_TF_BWD

    def mm_body(dff_ref, hp_ref, wu_ref, wd_ref, dv_ref, dhp_ref, hid_ref):
        @pl.when(pl.program_id(1) == 0)
        def _():
            dv_ref[...] = jnp.zeros_like(dv_ref)

        relu = jnp.maximum(hp_ref[...].astype(F32), 0.0)
        hid_ref[...] = jnp.square(relu).astype(MXU)
        dhid = lax.dot_general(dff_ref[...], wd_ref[...], (((1,), (1,)), ((), ())), preferred_element_type=F32)
        dhp = (dhid * (2.0 * relu)).astype(MXU)
        dhp_ref[...] = dhp
        dv_ref[...] += jnp.dot(dhp, wu_ref[...], preferred_element_type=F32)

    row = lambda: pl.BlockSpec((tm, D), lambda i, j: (i, 0))
    blk = lambda: pl.BlockSpec((tm, MLP_TF_BWD), lambda i, j: (i, j))
    wblk = lambda: pl.BlockSpec((MLP_TF_BWD, D), lambda i, j: (j, 0))
    return pl.pallas_call(
        mm_body, name="mlp_bwd", grid=(t_n // tm, n_f),
        in_specs=[row(), blk(), wblk(), wblk()], out_specs=[row(), blk(), blk()],
        out_shape=[jax.ShapeDtypeStruct((t_n, D), F32), jax.ShapeDtypeStruct((t_n, FF), MXU),
                   jax.ShapeDtypeStruct((t_n, FF), MXU)],
        compiler_params=_cp("parallel", "arbitrary"),
    )(dff, hp, wup, wdown)


def _wgrad(a, g, name):
    t_n, k_n = a.shape
    n_n = g.shape[1]
    tt = min(t_n, 1024)
    tk, tn = min(k_n, 1024), min(n_n, 1024)

    n_t = t_n // tt

    def body(a_ref, g_ref, o_ref, acc):
        t = pl.program_id(2)

        @pl.when(t == 0)
        def _():
            acc[...] = jnp.zeros_like(acc)

        acc[...] += lax.dot_general(a_ref[...], g_ref[...], (((0,), (0,)), ((), ())), preferred_element_type=F32)

        @pl.when(t == n_t - 1)
        def _():
            o_ref[...] = acc[...].astype(o_ref.dtype)

    return pl.pallas_call(
        body, name=name, grid=(k_n // tk, n_n // tn, n_t),
        in_specs=[pl.BlockSpec((tt, tk), lambda k, n, t: (t, k)), pl.BlockSpec((tt, tn), lambda k, n, t: (t, n))],
        out_specs=pl.BlockSpec((tk, tn), lambda k, n, t: (k, n)),
        out_shape=jax.ShapeDtypeStruct((k_n, n_n), MXU),
        scratch_shapes=[pltpu.VMEM((tk, tn), F32)],
        compiler_params=_cp("parallel", "parallel", "arbitrary"),
    )(a, g)


def _wgrad_segs(segs, g, name):
    t_n, n_n = g.shape
    n_s = len(segs)
    tt = min(t_n, 1024)
    n_t = t_n // tt

    def body(*refs):
        a_refs = refs[:n_s]
        g_ref, o_ref, acc = refs[n_s:]
        s_id, t = pl.program_id(0), pl.program_id(1)

        @pl.when(t == 0)
        def _():
            acc[...] = jnp.zeros_like(acc)

        for s in range(n_s):
            @pl.when(s_id == s)
            def _(s=s):
                acc[...] += lax.dot_general(a_refs[s][...], g_ref[...], (((0,), (0,)), ((), ())),
                                            preferred_element_type=F32)

        @pl.when(t == n_t - 1)
        def _():
            o_ref[...] = acc[...].astype(o_ref.dtype)

    seg_spec = lambda s: pl.BlockSpec((tt, SEG), lambda i, t: (jnp.where(i == s, t, jnp.where(i < s, 0, n_t - 1)), 0))
    return pl.pallas_call(
        body, name=name, grid=(n_s, n_t),
        in_specs=[seg_spec(s) for s in range(n_s)] + [pl.BlockSpec((tt, n_n), lambda i, t: (t, 0))],
        out_specs=pl.BlockSpec((SEG, n_n), lambda i, t: (i, 0)),
        out_shape=jax.ShapeDtypeStruct((n_s * SEG, n_n), MXU),
        scratch_shapes=[pltpu.VMEM((SEG, n_n), F32)],
        compiler_params=_cp("arbitrary", "arbitrary"),
    )(*segs, g)


def _inproj_bwd(dsegs, w, x, g0, dh1):
    t_n = x.shape[0]
    tm = min(t_n, 1024)
    n_k = NP // SEG

    def mm_body(*refs):
        dp_refs = refs[:n_k]
        w_ref, du_ref = refs[n_k:]
        k = pl.program_id(1)

        @pl.when(k == 0)
        def _():
            du_ref[...] = jnp.zeros_like(du_ref)

        for s in range(n_k):
            @pl.when(k == s)
            def _(s=s):
                du_ref[...] += jnp.dot(dp_refs[s][...], w_ref[...], preferred_element_type=F32)

    du = pl.pallas_call(
        mm_body, name="inproj_bwd", grid=(t_n // tm, n_k),
        in_specs=[pl.BlockSpec((tm, SEG), functools.partial(
            lambda i, k, s: (jnp.where(k >= s, i, jnp.maximum(i - 1, 0)), 0), s=s)) for s in range(n_k)] + [
            pl.BlockSpec((SEG, D), lambda i, k: (k, 0))],
        out_specs=pl.BlockSpec((tm, D), lambda i, k: (i, 0)),
        out_shape=jax.ShapeDtypeStruct((t_n, D), F32),
        compiler_params=_cp("parallel", "arbitrary"),
    )(*dsegs, w)

    tf = min(t_n, 512)

    def fin_body(du_ref, x_ref, g_ref, dh1_ref, dx_ref, dg0_ref):
        @pl.when(pl.program_id(0) == 0)
        def _():
            dg0_ref[...] = jnp.zeros_like(dg0_ref)

        dx, dg_rows = _rms_bwd(x_ref[...], g_ref[...], du_ref[...])
        dg0_ref[...] += _rowsum(dg_rows)
        dx_ref[...] = dh1_ref[...] + dx

    row = lambda: pl.BlockSpec((tf, D), lambda i: (i, 0))
    return pl.pallas_call(
        fin_body, name="grad_x", grid=(t_n // tf,), in_specs=[row(), row(), _full((1, D)), row()],
        out_specs=[row(), _full((1, D))],
        out_shape=[jax.ShapeDtypeStruct((t_n, D), F32), jax.ShapeDtypeStruct((1, D), F32)],
        compiler_params=_cp("arbitrary"),
    )(du, x, g0, dh1)


def _blockdiag4(w):
    w4 = w.reshape(4, 4, 64, 1, 64).astype(MXU)
    same = (jnp.arange(4)[:, None, None, None] == jnp.arange(4)[None, None, :, None])
    return jnp.where(same[None], w4, jnp.zeros((), MXU)).reshape(4, 256, 256)


def _blockdiag4_extract(g):
    g5 = g.reshape(4, 4, 64, 4, 64)
    return jnp.stack([g5[:, q, :, q, :] for q in range(4)], axis=1).reshape(NH, 64, 64)


def _local_step(x, tgt, p, after_ssd=None, late_weights=None, send_mlp_grads=None, send_late_grads=None):
    f = lambda a: a.astype(F32)
    proj, u = _inproj(x, p["norm_mix_pre"], p["w_in_pT"])
    ssm_params = (p["cw_xs"], p["cw_bc"], p["cb_xs"], p["cb_bc"], p["dt_bias"], p["a_log"], p["d_skip_x"],
                  p["ssm_norm"])
    ya, sprev, *ssd_saved = _ssd_fwd(proj, *ssm_params)
    cb_lru = p["conv_lru_b"] if after_ssd is None else p["conv_lru_b"] + after_ssd(ya)
    lru_params = (p["conv_lru_w"], cb_lru, p["wa_bd"], p["wx_bd"], p["lru_ba"], p["lru_bx"], p["lru_lambda"])
    yb, h, xr = _lru_fwd(proj, *lru_params)
    if late_weights is not None:
        p = dict(p, **late_weights(yb))
    h1, mix, merged = _merge_out(ya, yb, proj, x, p["w_out"], p["norm_mix_post"])
    hp, v, dout, dff, loss, dg3 = _mlp_fwd(h1, p["norm_mlp_pre"], p["w_upT"], p["w_down"], p["norm_mlp_post"], tgt)

    dv, dhp, hid = _mlp_bwd(dff, hp, p["w_upT"], p["w_down"])
    dh1, dmix, dya, dyb, dga, dgb, dg2, dg1 = _out_bwd(dv, h1, p["norm_mlp_pre"], dout, mix, ya, yb, proj, p["w_out"],
                                                     p["norm_mix_post"])
    d_w_out = _wgrad(merged, dmix, "wgrad_out")
    (dz, dxs, dbc, dcwx, dcwb, dcbx, dcbb, ddtb, dalog, ddsk, dnrm, d_w_down, d_w_up_t) = _ssd_bwd(
        dya, ssd_saved, proj, sprev, p["cw_xs"], p["cw_bc"], *ssm_params[4:], (hid, dff, dhp, v))
    ba_lru = p["lru_ba"] if send_mlp_grads is None else p["lru_ba"] + send_mlp_grads(d_w_up_t, d_w_down, d_w_out)
    (dgl, dxl, dcwl, dcbl, dwa, dwx, dba, dbx, dlam) = _lru_bwd(
        dyb, proj, h, xr, p["conv_lru_w"], p["wa_bd"], p["wx_bd"], ba_lru, p["lru_bx"], p["lru_lambda"])
    dsegs = [dz, dgl, dxl, dga, dgb, dxs, dbc]
    d_w_in_pt = _wgrad_segs(dsegs, u, "wgrad_in")
    g0 = p["norm_mix_pre"]
    if send_late_grads is not None:
        g0 = g0 + send_late_grads(d_w_in_pt)
    grad_x, dg0 = _inproj_bwd(dsegs, p["w_in_pT"], x, g0, dh1)
    grads = dict(
        norm_mix_pre=dg0, w_in_pT=d_w_in_pt, conv_ssm_w=jnp.concatenate([dcwx, dcwb], axis=1),
        conv_ssm_b=jnp.concatenate([dcbx, dcbb], axis=1), dt_bias=ddtb[:, :NH], a_log=dalog[:, :NH],
        d_skip=f(ddsk).reshape(NH, 64).sum(axis=1)[None, :], ssm_norm=dnrm, conv_lru_w=dcwl, conv_lru_b=dcbl,
        lru_wa=_blockdiag4_extract(dwa), lru_ba=dba, lru_wx=_blockdiag4_extract(dwx), lru_bx=dbx, lru_lambda=dlam,
        w_out=d_w_out, norm_mix_post=dg1, norm_mlp_pre=dg2, w_upT=d_w_up_t, w_down=d_w_down, norm_mlp_post=dg3)
    return loss[0, 0], grad_x, grads


W_IN_COLS = 6672


def _w_in_t_to_padded(wt):
    z, xs, bc, dt = wt[0:1024], wt[1024:2048], wt[2048:2560], wt[2560:2576]
    gl, xl, ga, gb = wt[2576:3600], wt[3600:4624], wt[4624:5648], wt[5648:6672]
    return jnp.concatenate([z, gl, xl, ga, gb, xs, bc, dt, jnp.zeros((NP - 6672, wt.shape[1]), wt.dtype)], axis=0)


def _w_in_t_from_padded(wp):
    z, gl, xl, ga, gb = (wp[SEG * s:SEG * (s + 1)] for s in range(5))
    xs, bc, dt = wp[5120:6144], wp[6144:6656], wp[6656:6672]
    return jnp.concatenate([z, xs, bc, dt, gl, xl, ga, gb], axis=0)


def _prep_params(full, big):
    f = lambda a: a.astype(F32)
    pad128 = lambda a: jnp.pad(f(a).reshape(1, -1), ((0, 0), (0, 128 - a.size)))
    cw = f(full["conv_ssm_w"])
    cb = f(full["conv_ssm_b"]).reshape(1, -1)
    return dict(
        big, norm_mix_pre=f(full["norm_mix_pre"]).reshape(1, D),
        cw_xs=cw[:, :D], cw_bc=cw[:, D:], cb_xs=cb[:, :D], cb_bc=cb[:, D:],
        dt_bias=pad128(full["dt_bias"]), a_log=pad128(full["a_log"]),
        d_skip_x=jnp.repeat(f(full["d_skip"]).reshape(-1), 64).reshape(1, D), ssm_norm=f(full["ssm_norm"]).reshape(1, D),
        conv_lru_w=f(full["conv_lru_w"]), conv_lru_b=f(full["conv_lru_b"]).reshape(1, D),
        wa_bd=_blockdiag4(full["lru_wa"]), wx_bd=_blockdiag4(full["lru_wx"]),
        lru_ba=f(full["lru_ba"]).reshape(1, D), lru_bx=f(full["lru_bx"]).reshape(1, D),
        lru_lambda=f(full["lru_lambda"]).reshape(1, D),
        norm_mix_post=f(full["norm_mix_post"]).reshape(1, D), norm_mlp_pre=f(full["norm_mlp_pre"]).reshape(1, D),
        norm_mlp_post=f(full["norm_mlp_post"]).reshape(1, D))


MESH_ID = pl.DeviceIdType.MESH
ANY = pl.BlockSpec(memory_space=pl.ANY)


def _my_place():
    x, y, c = lax.axis_index("x"), lax.axis_index("y"), lax.axis_index("c")
    return x, y, c, 4 * x + 2 * y + c


def _peer(x, y, c, k):
    return (x ^ ((k >> 2) & 1), y ^ ((k >> 1) & 1), c ^ (k & 1))


def _all_gather(pack, name):
    def body(in_ref, out_ref, send_sems, recv_sems, local_sem):
        x, y, c, me = _my_place()
        sibling = (x, y, 1 - c)
        flips = (4, 2, 6)

        def copy(j, block, to, src=None):
            return pltpu.make_async_remote_copy(
                src_ref=out_ref.at[block] if src is None else src, dst_ref=out_ref.at[block],
                send_sem=send_sems.at[j], recv_sem=recv_sems.at[j], device_id=to, device_id_type=MESH_ID)

        mine = pltpu.make_async_copy(in_ref, out_ref.at[me], local_sem)
        mine.start()
        first = [copy(0, me, sibling, src=in_ref)]
        first += [copy(1 + j, me, _peer(x, y, c, k), src=in_ref) for j, k in enumerate(flips)]
        for cp in first:
            cp.start()
        passed = [copy(4 + j, me ^ k, sibling) for j, k in enumerate(flips)]
        for j, k in enumerate(flips):
            copy(1 + j, me ^ k, (x, y, c)).wait_recv()
            passed[j].start()
        copy(0, me ^ 1, (x, y, c)).wait_recv()
        for j, k in enumerate(flips):
            copy(4 + j, me ^ k ^ 1, (x, y, c)).wait_recv()
        for cp in first + passed:
            cp.wait_send()
        mine.wait()

    return pl.pallas_call(
        body, name=name, in_specs=[ANY], out_specs=ANY,
        out_shape=jax.ShapeDtypeStruct((N_DEV,) + pack.shape, pack.dtype),
        scratch_shapes=[pltpu.SemaphoreType.DMA((N_DEV - 1,)), pltpu.SemaphoreType.DMA((N_DEV - 1,)),
                        pltpu.SemaphoreType.DMA],
    )(pack)


HBM = pl.BlockSpec(memory_space=pltpu.HBM)
SEM = pl.BlockSpec(memory_space=pltpu.SEMAPHORE)
PLAN_GATHER = tuple((k, "pack", 0) for k in range(1, N_DEV))
PLAN_SCATTER = tuple((k, "slot", 0) for k in range(1, N_DEV))
PLAN_GATHER_ICI = tuple((k, "pack", 0) for k in (2, 4, 6))
PLAN_GATHER_D2D = ((1, "pack", 0),) + tuple((1, s, s) for s in (2, 4, 6))


def _plan_copy(j, plan, src_ref, land_ref, sems):
    k, source, slot = plan[j]
    x, y, c, me = _my_place()
    if source == "pack":
        src = src_ref
    elif source == "slot":
        src = src_ref.at[me ^ k]
    else:
        src = land_ref.at[me ^ source]
    return pltpu.make_async_remote_copy(
        src_ref=src, dst_ref=land_ref.at[me ^ slot], send_sem=sems[j], recv_sem=sems[len(plan) + j],
        device_id=_peer(x, y, c, k), device_id_type=MESH_ID)


def _exchange_start(src, land, plan, name):
    n_c = len(plan)
    if land is None:
        land = pltpu.with_memory_space_constraint(lax.empty((N_DEV,) + src.shape[-2:], src.dtype), pltpu.HBM)

    def body(src_ref, land_ref, *rest):
        sems, token = rest[:2 * n_c], rest[2 * n_c + 2]
        for j in range(n_c):
            _plan_copy(j, plan, src_ref, land_ref, sems).start()
        token[...] = jnp.zeros_like(token)

    outs = pl.pallas_call(
        body, name=name,
        out_shape=(pltpu.SemaphoreType.DMA(()),) * (2 * n_c) + (
            pltpu.HBM(src.shape, src.dtype), pltpu.HBM(land.shape, land.dtype), jax.ShapeDtypeStruct((8, 128), F32)),
        in_specs=(HBM, HBM), out_specs=(SEM,) * (2 * n_c) + (HBM, HBM, pl.BlockSpec(memory_space=pltpu.VMEM)),
        input_output_aliases={0: 2 * n_c, 1: 2 * n_c + 1},
        compiler_params=pltpu.CompilerParams(has_side_effects=pltpu.SideEffectType.DATAFLOW_SIDE_EFFECTING),
    )(pltpu.with_memory_space_constraint(src, pltpu.HBM), land)
    return outs[:2 * n_c], outs[2 * n_c], outs[2 * n_c + 1], outs[2 * n_c + 2]


def _exchange_wait(sems, src_thru, land_thru, after, plan, name):
    n_c = len(plan)

    def body(src_ref, land_ref, *rest):
        for j in range(n_c):
            cp = _plan_copy(j, plan, src_ref, land_ref, rest[:2 * n_c])
            cp.wait_send()
            cp.wait_recv()

    return pl.pallas_call(
        body, name=name,
        out_shape=(pltpu.HBM(src_thru.shape, src_thru.dtype), pltpu.HBM(land_thru.shape, land_thru.dtype)),
        in_specs=(HBM, HBM) + (SEM,) * (2 * n_c) + (pl.BlockSpec(memory_space=pl.ANY),), out_specs=(HBM, HBM),
        input_output_aliases={0: 0, 1: 1},
        compiler_params=pltpu.CompilerParams(has_side_effects=pltpu.SideEffectType.DATAFLOW_SIDE_EFFECTING),
    )(src_thru, land_thru, *sems, after)


def _slot_sum(parts, name):
    r_n, c_n = parts.shape[1:]
    tr = max(t for t in range(16, 513, 16) if r_n % t == 0)

    def body(p_ref, o_ref):
        acc = p_ref[0].astype(F32)
        for k in range(1, N_DEV):
            acc = acc + p_ref[k].astype(F32)
        o_ref[...] = acc

    return pl.pallas_call(
        body, name=name, grid=(r_n // tr,),
        in_specs=[pl.BlockSpec((N_DEV, tr, c_n), lambda i: (0, i, 0))],
        out_specs=pl.BlockSpec((tr, c_n), lambda i: (i, 0)),
        out_shape=jax.ShapeDtypeStruct((r_n, c_n), F32),
        compiler_params=_cp("parallel"),
    )(parts)


def _adam_math(w, g, m, v):
    m = ADAM_B1 * m + (1.0 - ADAM_B1) * g
    v = ADAM_B2 * v + (1.0 - ADAM_B2) * jnp.square(g)
    m_hat = m / (1.0 - ADAM_B1 ** ADAM_STEP)
    v_hat = v / (1.0 - ADAM_B2 ** ADAM_STEP)
    return -ADAM_LR * (m_hat / (jnp.sqrt(v_hat) + ADAM_EPS) + ADAM_WD * w), m, v


def _adam_big(w, g, m, v, name):
    def body(w_ref, g_ref, m_ref, v_ref, d_ref, mo_ref, vo_ref):
        d_ref[...], mo_ref[...], vo_ref[...] = _adam_math(w_ref[...], g_ref[...], m_ref[...], v_ref[...])

    if w.ndim == 3:
        _, r_n, c_n = w.shape
        tr = min(r_n, 256)
        grid = (r_n // tr,)
        blk = lambda: pl.BlockSpec((1, tr, c_n), lambda i: (0, i, 0))
    else:
        r_n, c_n = w.shape
        tc = min(c_n, 256)
        grid = (c_n // tc,)
        blk = lambda: pl.BlockSpec((r_n, tc), lambda i: (0, i))
    return pl.pallas_call(
        body, name=name, grid=grid, in_specs=[blk(), blk(), blk(), blk()], out_specs=[blk(), blk(), blk()],
        out_shape=[jax.ShapeDtypeStruct(w.shape, F32)] * 3, compiler_params=_cp("parallel"),
    )(w, g, m, v)


def _adam_small(groups, where, wmv):
    n, n_g = len(wmv), len(groups)

    def body(*refs):
        g_refs = refs[:n_g]
        w_refs = refs[n_g:n_g + 3 * n]
        o_refs = refs[n_g + 3 * n:]
        for q in range(n):
            w_ref, m_ref, v_ref = w_refs[3 * q:3 * q + 3]
            r, c = w_ref.shape
            gi, r0 = where[q]
            g = g_refs[gi][0, r0:r0 + r, 0:c]
            for k in range(1, N_DEV):
                g = g + g_refs[gi][k, r0:r0 + r, 0:c]
            d, m, v = _adam_math(w_ref[...], g, m_ref[...], v_ref[...])
            o_refs[4 * q][...] = g
            o_refs[4 * q + 1][...] = d
            o_refs[4 * q + 2][...] = m
            o_refs[4 * q + 3][...] = v

    flat_wmv = [a for t in wmv for a in t]
    vm = pl.BlockSpec(memory_space=pltpu.VMEM)
    outs = pl.pallas_call(
        body, name="adam_small", in_specs=[vm] * (n_g + 3 * n), out_specs=[vm] * (4 * n),
        out_shape=[jax.ShapeDtypeStruct(t[0].shape, F32) for t in wmv for _ in range(4)],
        compiler_params=pltpu.CompilerParams(vmem_limit_bytes=VMEM_LIMIT),
    )(*groups, *flat_wmv)
    return [tuple(outs[4 * q:4 * q + 4]) for q in range(n)]


WEIGHTS = ["norm_mix_pre", "w_in", "conv_ssm_w", "conv_ssm_b", "dt_bias", "a_log", "d_skip", "ssm_norm", "conv_lru_w",
           "conv_lru_b", "lru_wa", "lru_ba", "lru_wx", "lru_bx", "lru_lambda", "w_out", "norm_mix_post", "norm_mlp_pre",
           "w_up", "w_down", "norm_mlp_post"]
BIG = ["w_out", "w_up", "w_down", "w_in"]
IN_ROWS = W_IN_COLS // N_DEV
IN_PAD, EARLY_ROWS = 848, 880
ROW_UP, ROW_DOWN, LATE_ROWS = 128, 640, 1152
GRAD_LATE_ROWS = 864
CONV_SSM_COLS, CONV_LRU_COLS = 1536 // N_DEV, D // N_DEV
SMALL = [("norm_mix_pre", (1, D), 0, 0), ("ssm_norm", (1, D), 0, 1), ("conv_lru_b", (1, D), 0, 2),
         ("lru_lambda", (1, D), 0, 3), ("norm_mix_post", (1, D), 0, 4), ("norm_mlp_pre", (1, D), 0, 5),
         ("norm_mlp_post", (1, D), 0, 6), ("conv_ssm_b", (1, 1536), 1, 0), ("dt_bias", (1, NH), 2, 0),
         ("a_log", (1, NH), 2, 1), ("d_skip", (1, NH), 2, 2), ("conv_ssm_w", (4, CONV_SSM_COLS), 3, 0),
         ("conv_lru_w", (4, CONV_LRU_COLS), 4, 0), ("lru_wa", (D, 64), 5, 0), ("lru_wx", (D, 64), 5, D),
         ("lru_ba", (NH, 64), 6, 0), ("lru_bx", (NH, 64), 6, NH)]
SMALL_GROUPS = [(8, D), (1, 1536), (8, 128), (4, 1536), (4, D), (2 * D, 64), (2 * NH, 64)]


def _pad_rows(flat, mult):
    n = flat.shape[0]
    rows = -(-n // (128 * mult)) * mult
    return jnp.pad(flat, (0, rows * 128 - n)).reshape(rows, 128)


def _split3(a):
    hi = a.astype(MXU)
    r1 = a - hi.astype(F32)
    mid = r1.astype(MXU)
    lo = (r1 - mid.astype(F32)).astype(MXU)
    return jnp.stack([hi, mid, lo])


def _early_pack(a, me):
    bf = lambda t: t.astype(MXU)
    conv = lambda t, c: jnp.pad(_split3(t).reshape(12, c), ((0, 4), (0, D - c)))
    shifted = lax.dynamic_update_slice(jnp.zeros((IN_PAD, D), MXU), bf(a["w_in"][0]).T, (2 * me, 0))
    return jnp.concatenate([shifted, conv(a["conv_ssm_w"][0], CONV_SSM_COLS), conv(a["conv_lru_w"][0], CONV_LRU_COLS)],
                           axis=0)


TILE = 16
SHARD_TILES = IN_PAD // TILE
SHARD_STEP = (IN_ROWS // TILE)
SEG_TILES = ((0, 64, 0), (64, 128, 320), (128, 160, 384), (160, 161, 416), (161, 225, 64), (225, 289, 128),
             (289, 353, 192), (353, 417, 256))


def _tile_runs(lo, hi):
    runs = []
    for s0, s1, d0 in SEG_TILES:
        a, b = max(lo, s0), min(hi, s1)
        if a < b:
            runs.append((a, b - a, d0 + a - s0))
    return runs


def _assemble_w_in(g):
    whole = []
    for k in range(N_DEV):
        lo = SHARD_STEP * k + (1 if k else 0)
        hi = SHARD_STEP * (k + 1) + (1 if k == N_DEV - 1 else 0)
        whole += [(k, a - SHARD_STEP * k, n, d) for a, n, d in _tile_runs(lo, hi)]
    split = [(k, _tile_runs(SHARD_STEP * k, SHARD_STEP * k + 1)[0][2]) for k in range(1, N_DEV)]

    def body(g_ref, o_ref):
        rows = lambda t, n=1: pl.ds(TILE * t, TILE * n)
        for k, t, n, d in whole:
            o_ref[rows(d, n), :] = g_ref[k, rows(t, n), :]
        for k, d in split:
            o_ref[rows(d), :] = g_ref[k - 1, rows(SHARD_STEP), :] + g_ref[k, rows(0), :]
        o_ref[pl.ds(W_IN_COLS, NP - W_IN_COLS), :] = jnp.zeros((NP - W_IN_COLS, D), o_ref.dtype)

    vm = pl.BlockSpec(memory_space=pltpu.VMEM)
    return pl.pallas_call(
        body, name="assemble_w_in", in_specs=[vm], out_specs=vm, out_shape=jax.ShapeDtypeStruct((NP, D), g.dtype),
        compiler_params=pltpu.CompilerParams(vmem_limit_bytes=VMEM_LIMIT),
    )(g)


def _scatter_w_in_grad(dw):
    runs = [(k, a - SHARD_STEP * k, n, d) for k in range(N_DEV)
            for a, n, d in _tile_runs(SHARD_STEP * k, SHARD_STEP * k + SHARD_TILES)]
    pad = GRAD_LATE_ROWS - IN_PAD

    def body(dw_ref, o_ref):
        rows = lambda t, n: pl.ds(TILE * t, TILE * n)
        for k, t, n, d in runs:
            o_ref[k, rows(t, n), :] = dw_ref[rows(d, n), :]
        for k in range(N_DEV):
            o_ref[k, pl.ds(IN_PAD, pad), :] = jnp.zeros((pad, D), o_ref.dtype)

    vm = pl.BlockSpec(memory_space=pltpu.VMEM)
    return pl.pallas_call(
        body, name="scatter_w_in_grad", in_specs=[vm], out_specs=vm,
        out_shape=jax.ShapeDtypeStruct((N_DEV, GRAD_LATE_ROWS, D), dw.dtype),
        compiler_params=pltpu.CompilerParams(vmem_limit_bytes=VMEM_LIMIT),
    )(dw)


def _early_unpack(g):
    w_in_pt = _assemble_w_in(g)
    conv = {}
    for n, r0, c in (("conv_ssm_w", IN_PAD, CONV_SSM_COLS), ("conv_lru_w", IN_PAD + 16, CONV_LRU_COLS)):
        s = g[:, r0:r0 + 12, :c].astype(F32).reshape(N_DEV, 3, 4, c)
        conv[n] = ((s[:, 0] + s[:, 1]) + s[:, 2]).transpose(1, 0, 2).reshape(4, N_DEV * c)
    return w_in_pt, conv


def _late_pack(a):
    bf = lambda t: t.astype(MXU)
    return jnp.concatenate([bf(a["w_out"][0]), bf(a["w_up"][0]).T, bf(a["w_down"][0])], axis=0)


def _late_unpack(g):
    return dict(w_out=g[:, :ROW_UP].reshape(D, D), w_upT=g[:, ROW_UP:ROW_DOWN].reshape(FF, D),
                w_down=g[:, ROW_DOWN:].reshape(FF, D))


def _own_slot(land, own):
    me = 4 * lax.axis_index("x") + 2 * lax.axis_index("y") + lax.axis_index("c")
    return lax.dynamic_update_slice_in_dim(land, own[None], me, axis=0)


def kernel(x, norm_mix_pre, w_in, conv_ssm_w, conv_ssm_b, dt_bias, a_log, d_skip, ssm_norm, conv_lru_w, conv_lru_b, lru_wa, lru_ba, lru_wx, lru_bx, lru_lambda, w_out, norm_mix_post, norm_mlp_pre, w_up, w_down, norm_mlp_post, loss_target, m_norm_mix_pre, m_w_in, m_conv_ssm_w, m_conv_ssm_b, m_dt_bias, m_a_log, m_d_skip, m_ssm_norm, m_conv_lru_w, m_conv_lru_b, m_lru_wa, m_lru_ba, m_lru_wx, m_lru_bx, m_lru_lambda, m_w_out, m_norm_mix_post, m_norm_mlp_pre, m_w_up, m_w_down, m_norm_mlp_post, v_norm_mix_pre, v_w_in, v_conv_ssm_w, v_conv_ssm_b, v_dt_bias, v_a_log, v_d_skip, v_ssm_norm, v_conv_lru_w, v_conv_lru_b, v_lru_wa, v_lru_ba, v_lru_wx, v_lru_bx, v_lru_lambda, v_w_out, v_norm_mix_post, v_norm_mlp_pre, v_w_up, v_w_down, v_norm_mlp_post):
    vals = (norm_mix_pre, w_in, conv_ssm_w, conv_ssm_b, dt_bias, a_log, d_skip, ssm_norm, conv_lru_w, conv_lru_b, lru_wa, lru_ba, lru_wx, lru_bx, lru_lambda, w_out, norm_mix_post, norm_mlp_pre, w_up, w_down, norm_mlp_post)
    m_vals = (m_norm_mix_pre, m_w_in, m_conv_ssm_w, m_conv_ssm_b, m_dt_bias, m_a_log, m_d_skip, m_ssm_norm, m_conv_lru_w, m_conv_lru_b, m_lru_wa, m_lru_ba, m_lru_wx, m_lru_bx, m_lru_lambda, m_w_out, m_norm_mix_post, m_norm_mlp_pre, m_w_up, m_w_down, m_norm_mlp_post)
    v_vals = (v_norm_mix_pre, v_w_in, v_conv_ssm_w, v_conv_ssm_b, v_dt_bias, v_a_log, v_d_skip, v_ssm_norm, v_conv_lru_w, v_conv_lru_b, v_lru_wa, v_lru_ba, v_lru_wx, v_lru_bx, v_lru_lambda, v_w_out, v_norm_mix_post, v_norm_mlp_pre, v_w_up, v_w_down, v_norm_mlp_post)
    w = dict(zip(WEIGHTS, vals))
    m = dict(zip(WEIGHTS, m_vals))
    v = dict(zip(WEIGHTS, v_vals))
    me = 4 * lax.axis_index("x") + 2 * lax.axis_index("y") + lax.axis_index("c")

    bf = lambda t: t.astype(MXU)
    late = _late_pack(w)
    early = _all_gather(_early_pack(w, me), "early_weights_all_gather")
    late, early = lax.optimization_barrier((late, early))
    lw = {}
    lw["sems"], lw["src"], lw["land"], token = _exchange_start(late, None, PLAN_GATHER_ICI, "late_weights_ici_start")
    w_in_pt, conv_w = _early_unpack(early)
    full = {n: (conv_w[n] if n in conv_w else w[n][0]) for n in WEIGHTS if n not in BIG}
    full["norm_mix_pre"] = full["norm_mix_pre"] + token[0, 0]

    def after_ssd(after):
        src, land = _exchange_wait(lw["sems"], lw["src"], lw["land"], after, PLAN_GATHER_ICI, "late_weights_ici_wait")
        lw["sems"], lw["src"], lw["land"], tok = _exchange_start(src, land, PLAN_GATHER_D2D, "late_weights_d2d_start")
        return tok[0, 0]

    def late_weights(after):
        src, land = _exchange_wait(lw["sems"], lw["src"], lw["land"], after, PLAN_GATHER_D2D, "late_weights_d2d_wait")
        return _late_unpack(_own_slot(land, src))

    sent = {}

    def send_mlp_grads(d_w_up_t, d_w_down, d_w_out):
        src = jnp.concatenate([bf(d_w_up_t).reshape(N_DEV, -1, D), bf(d_w_down).reshape(N_DEV, -1, D),
                               bf(d_w_out).reshape(N_DEV, -1, D)], axis=1)
        sent["sems"], sent["src"], sent["land"], tok = _exchange_start(src, None, PLAN_SCATTER, "mlp_grads_start")
        return tok[0, 0]

    def send_late_grads(d_w_in_pt):
        src = _scatter_w_in_grad(d_w_in_pt)
        sent["sems2"], sent["src2"], sent["land2"], tok = _exchange_start(src, None, PLAN_SCATTER, "late_grads_start")
        return tok[0, 0]

    loss, grad_x, g = _local_step(x[0], loss_target[0], _prep_params(full, dict(w_in_pT=w_in_pt)), after_ssd,
                                  late_weights, send_mlp_grads, send_late_grads)
    loss = lax.psum(loss, ("x", "y", "c"))

    own = lambda src: lax.dynamic_index_in_dim(src, me, keepdims=False)
    out_g, out_d, out_m, out_v = {}, {}, {}, {}
    mlp_src, mlp_land = _exchange_wait(sent["sems"], sent["src"], sent["land"], grad_x, PLAN_SCATTER, "mlp_grads_wait")
    g_mlp = _slot_sum(_own_slot(mlp_land, own(mlp_src)), "slot_sum_mlp")
    fs = FF // N_DEV
    for n, gn in (("w_up", g_mlp[:fs].T[None]), ("w_down", g_mlp[fs:2 * fs][None]), ("w_out", g_mlp[2 * fs:][None])):
        out_g[n] = gn
        out_d[n], out_m[n], out_v[n] = _adam_big(w[n], gn, m[n], v[n], "adam_" + n)
    zrow = jnp.zeros((1, D), F32)
    pad16 = lambda a: jnp.pad(a, ((0, 0), (0, 128 - NH)))
    small_parts = [
        jnp.concatenate([g["norm_mix_pre"], g["ssm_norm"], g["conv_lru_b"], g["lru_lambda"], g["norm_mix_post"],
                         g["norm_mlp_pre"], g["norm_mlp_post"], zrow], axis=0),
        g["conv_ssm_b"],
        jnp.concatenate([pad16(g["dt_bias"]), pad16(g["a_log"]), pad16(g["d_skip"]), jnp.zeros((5, 128), F32)], axis=0),
        g["conv_ssm_w"], g["conv_lru_w"],
        jnp.concatenate([g["lru_wa"].reshape(D, 64), g["lru_wx"].reshape(D, 64)], axis=0),
        jnp.concatenate([g["lru_ba"].reshape(NH, 64), g["lru_bx"].reshape(NH, 64)], axis=0)]
    small = _pad_rows(jnp.concatenate([s.reshape(-1) for s in small_parts]), 8)
    small, _ = lax.optimization_barrier((small, out_v["w_out"]))
    small_all = _all_gather(small, "small_grads_all_gather")

    lg_src, lg_land = _exchange_wait(sent["sems2"], sent["src2"], sent["land2"], small_all, PLAN_SCATTER, "late_grads_wait")
    g_late = _slot_sum(_own_slot(lg_land, own(lg_src)), "slot_sum_late")
    gt = lax.dynamic_slice(g_late, (2 * me, 0), (IN_ROWS, D))
    dt_, mt_, vt_ = _adam_big(w["w_in"][0].T, gt, m["w_in"][0].T, v["w_in"][0].T, "adam_w_in")
    out_g["w_in"], out_d["w_in"], out_m["w_in"], out_v["w_in"] = gt.T[None], dt_.T[None], mt_.T[None], vt_.T[None]
    sflat = small_all.reshape(N_DEV, -1)
    groups = []
    off = 0
    for r, c in SMALL_GROUPS:
        groups.append(sflat[:, off:off + r * c].reshape(N_DEV, r, c))
        off += r * c
    groups[3] = lax.dynamic_slice_in_dim(groups[3], me * CONV_SSM_COLS, CONV_SSM_COLS, axis=2)
    groups[4] = lax.dynamic_slice_in_dim(groups[4], me * CONV_LRU_COLS, CONV_LRU_COLS, axis=2)
    wmv = [(w[n].reshape(s), m[n].reshape(s), v[n].reshape(s)) for n, s, _, _ in SMALL]
    res = _adam_small(groups, [(gi, r0) for _, _, gi, r0 in SMALL], wmv)
    for (n, _, _, _), (g_n, d_n, m_n, v_n) in zip(SMALL, res):
        shape = w[n].shape
        out_g[n], out_d[n], out_m[n], out_v[n] = (g_n.reshape(shape), d_n.reshape(shape), m_n.reshape(shape),
                                                  v_n.reshape(shape))
    return (loss, grad_x[None], *[out_g[n] for n in WEIGHTS], *[out_d[n] for n in WEIGHTS],
            *[out_m[n] for n in WEIGHTS], *[out_v[n] for n in WEIGHTS])
```

```python
import functools

import jax
import jax.numpy as jnp
from jax import lax
from jax.experimental import pallas as pl
from jax.experimental.pallas import tpu as pltpu

F32 = jnp.float32
MXU = jnp.bfloat16
HI = lax.Precision.HIGHEST
EPS = 1e-6

D = 1024
NH = 16
NS = 128
CH = 128
FF = 4096
NP = 7168
SEG = 1024
LRU_C = 8.0
N_DEV = 8

ADAM_LR, ADAM_B1, ADAM_B2, ADAM_EPS, ADAM_WD, ADAM_STEP = 0.001, 0.9, 0.999, 1e-08, 0.01, 10

VMEM_LIMIT = 56 * 1024 * 1024


def _cp(*sem):
    return pltpu.CompilerParams(dimension_semantics=sem, vmem_limit_bytes=VMEM_LIMIT)


def _nn(a, b):
    return jnp.dot(a.astype(MXU), b.astype(MXU), preferred_element_type=F32)


def _nt(a, b):
    return lax.dot_general(a.astype(MXU), b.astype(MXU), (((1,), (1,)), ((), ())), preferred_element_type=F32)


def _tn(a, b):
    return lax.dot_general(a.astype(MXU), b.astype(MXU), (((0,), (0,)), ((), ())), preferred_element_type=F32)


_sigmoid = jax.nn.sigmoid


def _silu(x):
    return x * _sigmoid(x)


def _dsilu(x):
    s = _sigmoid(x)
    return s + x * s * (1.0 - s)


def _softplus(x):
    return jnp.maximum(x, 0.0) + jnp.log(1.0 + jnp.exp(-jnp.abs(x)))


def _rms(x, g):
    r = lax.rsqrt(jnp.mean(x * x, axis=-1, keepdims=True) + EPS)
    return x * r * g


def _rms_bwd(x, g, dy):
    r = lax.rsqrt(jnp.mean(x * x, axis=-1, keepdims=True) + EPS)
    gdy = g * dy
    dx = r * gdy - x * (r * r * r) * jnp.mean(x * gdy, axis=-1, keepdims=True)
    return dx, dy * x * r


def _rowsum(x):
    return jnp.sum(x, axis=0, keepdims=True)


def _taps_past(cur, prev8):
    r_n, c_n = cur.shape
    row = lax.broadcasted_iota(jnp.int32, (r_n, c_n), 0)
    out = []
    for k in range(4):
        s = 3 - k
        if s == 0:
            out.append(cur)
            continue
        head = jnp.concatenate([pltpu.roll(prev8, s, 0), jnp.zeros((r_n - 8, c_n), F32)], axis=0)
        out.append(jnp.where(row < s, head, pltpu.roll(cur, s, 0)))
    return out


def _taps_future(cur, fut8):
    r_n, c_n = cur.shape
    row = lax.broadcasted_iota(jnp.int32, (r_n, c_n), 0)
    out = []
    for k in range(4):
        s = 3 - k
        if s == 0:
            out.append(cur)
            continue
        tail = jnp.concatenate([jnp.zeros((r_n - 8, c_n), F32), pltpu.roll(fut8, 8 - s, 0)], axis=0)
        out.append(jnp.where(row >= r_n - s, tail, pltpu.roll(cur, r_n - s, 0)))
    return out


def _conv_apply(taps, w, b):
    acc = taps[0] * w[0:1, :]
    for k in range(1, 4):
        acc = acc + taps[k] * w[k:k + 1, :]
    return acc + b


def _inproj(x, g0, w):
    t_n = x.shape[0]
    tm = min(t_n, 1024)

    n_j = NP // SEG

    def body(x_ref, g_ref, w_ref, pb_ref, p6_ref, u_ref):
        j = pl.program_id(1)

        @pl.when(j == 0)
        def _():
            u_ref[...] = _rms(x_ref[...], g_ref[...]).astype(MXU)

        p = lax.dot_general(u_ref[...], w_ref[...], (((1,), (1,)), ((), ())), preferred_element_type=F32)

        @pl.when(j < n_j - 1)
        def _():
            pb_ref[...] = p.astype(MXU)

        @pl.when(j == n_j - 1)
        def _():
            p6_ref[...] = p

    pb, p6, u = pl.pallas_call(
        body, name="inproj", grid=(t_n // tm, n_j),
        in_specs=[pl.BlockSpec((tm, D), lambda i, j: (i, 0)), pl.BlockSpec((1, D), lambda i, j: (0, 0)),
                  pl.BlockSpec((SEG, D), lambda i, j: (j, 0))],
        out_specs=[pl.BlockSpec((tm, SEG), lambda i, j: (i, jnp.minimum(j, n_j - 2))),
                   pl.BlockSpec((tm, SEG), lambda i, j: (i, 0)), pl.BlockSpec((tm, D), lambda i, j: (i, 0))],
        out_shape=[jax.ShapeDtypeStruct((t_n, NP - SEG), MXU), jax.ShapeDtypeStruct((t_n, SEG), F32),
                   jax.ShapeDtypeStruct((t_n, D), MXU)],
        compiler_params=_cp("parallel", "arbitrary"),
    )(x, g0, w)
    return (pb, p6), u


def _ssd_prep(dtraw, dtb, alog):
    l_n = dtraw.shape[0]
    r = lax.broadcasted_iota(jnp.int32, (l_n, l_n), 0)
    c = lax.broadcasted_iota(jnp.int32, (l_n, l_n), 1)
    tril = (r >= c).astype(F32)
    triu = (r <= c).astype(F32)
    eye = (r == c).astype(F32)
    dt = _softplus(dtraw + dtb)
    adt = dt * (-jnp.exp(alog))
    ac = jnp.dot(tril, adt, preferred_element_type=F32, precision=HI)
    tn = (((0,), (0,)), ((), ()))
    ac_t = lax.dot_general(adt, triu, tn, preferred_element_type=F32, precision=HI)
    dt_t = lax.dot_general(dt, eye, tn, preferred_element_type=F32, precision=HI)
    return dt, dt_t, ac, ac_t, _rowsum(adt)


def _ssd_pair(j, xp, bg, cg, sp, dt, dt_t, ac, ac_t, aend):
    l_n = xp.shape[0]
    lane = lax.broadcasted_iota(jnp.int32, (l_n, 128), 1)
    sub = lax.broadcasted_iota(jnp.int32, (128, l_n), 0)
    lane1 = lax.broadcasted_iota(jnp.int32, (1, 128), 1)
    tri = lax.broadcasted_iota(jnp.int32, (l_n, l_n), 0) >= lax.broadcasted_iota(jnp.int32, (l_n, l_n), 1)
    lo = lax.broadcasted_iota(jnp.int32, (l_n, 128), 1) < 64
    lo_s = lax.broadcasted_iota(jnp.int32, (128, 128), 1) < 64
    cb = _nt(cg, bg)
    cs = _nn(cg, sp)
    x2 = jnp.concatenate([jnp.where(lo, xp, 0.0), jnp.where(lo, 0.0, xp)], axis=0)
    ws_, bs_, eo, ee = [], [], [], []
    for e in range(2):
        h = 2 * j + e
        ac_l = jnp.sum(jnp.where(lane == h, ac, 0.0), axis=1, keepdims=True)
        dt_l = jnp.sum(jnp.where(lane == h, dt, 0.0), axis=1, keepdims=True)
        a_end = jnp.sum(jnp.where(lane1 == h, aend, 0.0), axis=1, keepdims=True)
        ac_s, dt_s = ac_t[h:h + 1, :], dt_t[h:h + 1, :]
        decay = jnp.exp(jnp.where(tri, ac_l - ac_s, -1e30))
        ws_.append(cb * decay * dt_s)
        bs_.append(bg * (jnp.exp(a_end - ac_l) * dt_l))
        eo.append(jnp.exp(ac_l))
        ee.append(jnp.exp(a_end))
    y = _nn(jnp.concatenate(ws_, axis=1), x2) + jnp.where(lo, eo[0], eo[1]) * cs
    s_new = _tn(jnp.concatenate(bs_, axis=0), x2) + jnp.where(lo_s, ee[0], ee[1]) * sp
    return y, s_new


def _ssd_post(y, xs, z, dsk, nrm):
    y = (y + dsk * xs) * _silu(z)
    half = D // 2
    ya, yb = y[:, :half], y[:, half:]
    ya = ya * lax.rsqrt(jnp.mean(ya * ya, axis=-1, keepdims=True) + EPS)
    yb = yb * lax.rsqrt(jnp.mean(yb * yb, axis=-1, keepdims=True) + EPS)
    return jnp.concatenate([ya, yb], axis=1) * nrm


LAST_SEG = NP // SEG - 1


def _proj_ops(proj, seg_ids):
    return [proj[1] if s == LAST_SEG else proj[0] for s in seg_ids]


def _proj_specs(rows, seg_ids, order):
    return [pl.BlockSpec((rows, SEG), functools.partial(lambda i, c: (order(i), c), c=0 if s == LAST_SEG else s))
            for s in seg_ids]


def _prev_specs(rows, seg_ids, order):
    specs = []
    for s in seg_ids:
        n, c = (8, 0) if s == LAST_SEG else (16, s)
        specs.append(pl.BlockSpec((n, SEG), functools.partial(
            lambda i, n, c: (jnp.maximum(order(i) * (rows // n) - 1, 0), c), n=n, c=c)))
    return specs


def _prev8(ref):
    return ref[...] if ref.shape[0] == 8 else ref[8:16, :].astype(F32)


def _full(shape):
    return pl.BlockSpec(shape, lambda i: (0,) * len(shape))


def _ssd_fwd(proj, cwx, cwb, cbx, cbb, dtb, alog, dsk, nrm):
    t_n = proj[0].shape[0]
    n_c = t_n // CH
    fwd = lambda i: i

    def body(z_ref, xs_ref, bc_ref, xsp_ref, bcp_ref, cwx_ref, cwb_ref, cbx_ref, cbb_ref, dtb_ref, alog_ref,
             dsk_ref, nrm_ref, ya_ref, sprev_ref, yraw_ref, xspre_ref, bcpre_ref, s_ref):
        c = pl.program_id(0)

        @pl.when(c == 0)
        def _():
            s_ref[...] = jnp.zeros_like(s_ref)

        keep = jnp.where(c == 0, 0.0, 1.0)
        xs_pre = _conv_apply(_taps_past(xs_ref[...].astype(F32), _prev8(xsp_ref) * keep), cwx_ref[...], cbx_ref[...])
        bc_pre = _conv_apply(_taps_past(bc_ref[:, :512], bcp_ref[:, :512] * keep), cwb_ref[...], cbb_ref[...])
        xspre_ref[...] = xs_pre
        bcpre_ref[...] = bc_pre
        prep = _ssd_prep(bc_ref[:, 512:640], dtb_ref[...], alog_ref[...])
        xs = _silu(xs_pre)
        bc = _silu(bc_pre)
        sprev_ref[0] = s_ref[...]
        ys = []
        for j in range(NH // 2):
            g = j // 4
            yp, sn = _ssd_pair(j, xs[:, 128 * j:128 * j + 128], bc[:, 128 * g:128 * g + 128],
                               bc[:, 256 + 128 * g:384 + 128 * g], s_ref[:, 128 * j:128 * j + 128], *prep)
            ys.append(yp)
            s_ref[:, 128 * j:128 * j + 128] = sn
        y = jnp.concatenate(ys, axis=1)
        yraw_ref[...] = y
        ya_ref[...] = _ssd_post(y, xs, z_ref[...].astype(F32), dsk_ref[...], nrm_ref[...]).astype(ya_ref.dtype)

    return pl.pallas_call(
        body, name="ssd_fwd", grid=(n_c,),
        in_specs=_proj_specs(CH, (0, 5, 6), fwd) + _prev_specs(CH, (5, 6), fwd) + [
            _full((4, D)), _full((4, 512)), _full((1, D)), _full((1, 512)), _full((1, 128)), _full((1, 128)),
            _full((1, D)), _full((1, D))],
        out_specs=[pl.BlockSpec((CH, D), lambda i: (i, 0)), pl.BlockSpec((1, NS, D), lambda i: (i, 0, 0)),
                   pl.BlockSpec((CH, D), lambda i: (i, 0)), pl.BlockSpec((CH, D), lambda i: (i, 0)),
                   pl.BlockSpec((CH, 512), lambda i: (i, 0))],
        out_shape=[jax.ShapeDtypeStruct((t_n, D), MXU), jax.ShapeDtypeStruct((n_c, NS, D), F32),
                   jax.ShapeDtypeStruct((t_n, D), F32), jax.ShapeDtypeStruct((t_n, D), F32),
                   jax.ShapeDtypeStruct((t_n, 512), F32)],
        scratch_shapes=[pltpu.VMEM((NS, D), F32)],
        compiler_params=_cp("arbitrary"),
    )(*_proj_ops(proj, (0, 5, 6, 5, 6)), cwx, cwb, cbx, cbb, dtb, alog, dsk, nrm)


def _ssd_bwd(dya, saved, proj, sprev, cwx, cwb, dtb, alog, dsk, nrm, mlp_ops):
    t_n = proj[0].shape[0]
    n_c = t_n // CH
    rev = lambda i: n_c - 1 - i
    fb = FF // n_c

    def body(dya_ref, yraw_ref, xspre_ref, bcpre_ref, z_ref, xs_ref, bc_ref, sprev_ref, cwx_ref, cwb_ref,
             dtb_ref, alog_ref, dsk_ref, nrm_ref, hid_ref, dff_ref, dhp_ref, v_ref,
             dz_ref, dxs_ref, dbc_ref, dcwx_ref, dcwb_ref, dcbx_ref, dcbb_ref, ddtb_ref, dalog_ref, ddsk_ref,
             dnrm_ref, dwd_ref, dwu_ref, ds_ref, futx_ref, futb_ref):
        i = pl.program_id(0)
        acc_refs = (dcwx_ref, dcwb_ref, dcbx_ref, dcbb_ref, ddtb_ref, dalog_ref, ddsk_ref, dnrm_ref)
        tn = (((0,), (0,)), ((), ()))
        dwd_ref[...] = lax.dot_general(hid_ref[...], dff_ref[...], tn, preferred_element_type=F32).astype(MXU)
        dwu_ref[...] = lax.dot_general(dhp_ref[...], v_ref[...], tn, preferred_element_type=F32).astype(MXU)

        @pl.when(i == 0)
        def _():
            for r in (ds_ref, futx_ref, futb_ref) + acc_refs:
                r[...] = jnp.zeros_like(r)

        xs_pre = xspre_ref[...]
        bc_pre = bcpre_ref[...]
        xs = _silu(xs_pre)
        bc = _silu(bc_pre)
        prep, prep_vjp = jax.vjp(_ssd_prep, bc_ref[:, 512:640], dtb_ref[...], alog_ref[...])
        s_in = sprev_ref[0]

        def pair_args(j):
            g = j // 4
            return (xs[:, 128 * j:128 * j + 128], bc[:, 128 * g:128 * g + 128],
                    bc[:, 256 + 128 * g:384 + 128 * g], s_in[:, 128 * j:128 * j + 128]) + tuple(prep)

        _, post_vjp = jax.vjp(_ssd_post, yraw_ref[...], xs, z_ref[...].astype(F32), dsk_ref[...], nrm_ref[...])
        dy, dxs_skip, dz, ddsk, dnrm = post_vjp(dya_ref[...])
        dz_ref[...] = dz.astype(dz_ref.dtype)
        ddsk_ref[...] += ddsk
        dnrm_ref[...] += dnrm

        dprep = [jnp.zeros_like(p) for p in prep]
        dxp = []
        dbg = [jnp.zeros((CH, 128), F32), jnp.zeros((CH, 128), F32)]
        dcg = [jnp.zeros((CH, 128), F32), jnp.zeros((CH, 128), F32)]
        for j in range(NH // 2):
            g = j // 4
            _, pair_vjp = jax.vjp(functools.partial(_ssd_pair, j), *pair_args(j))
            cts = pair_vjp((dy[:, 128 * j:128 * j + 128], ds_ref[:, 128 * j:128 * j + 128]))
            dxp.append(cts[0])
            dbg[g] = dbg[g] + cts[1]
            dcg[g] = dcg[g] + cts[2]
            ds_ref[:, 128 * j:128 * j + 128] = cts[3]
            dprep = [a + b for a, b in zip(dprep, cts[4:])]
        ddtraw, ddtb, dalog = prep_vjp(tuple(dprep))
        ddtb_ref[...] += ddtb
        dalog_ref[...] += dalog

        dxs_pre = (dxs_skip + jnp.concatenate(dxp, axis=1)) * _dsilu(xs_pre)
        dbc_pre = jnp.concatenate([dbg[0], dbg[1], dcg[0], dcg[1]], axis=1) * _dsilu(bc_pre)
        dcbx_ref[...] += _rowsum(dxs_pre)
        dcbb_ref[...] += _rowsum(dbc_pre)
        fx = _taps_future(dxs_pre, futx_ref[...])
        fbc = _taps_future(dbc_pre, futb_ref[...])
        xs_in = xs_ref[...].astype(F32)
        bc_in = bc_ref[:, :512]
        for k in range(4):
            dcwx_ref[k:k + 1, :] += _rowsum(fx[k] * xs_in)
            dcwb_ref[k:k + 1, :] += _rowsum(fbc[k] * bc_in)
        cwx = cwx_ref[...]
        cwb = cwb_ref[...]
        dxs_in = fx[0] * cwx[0:1, :]
        dbc_in = fbc[0] * cwb[0:1, :]
        for k in range(1, 4):
            dxs_in = dxs_in + fx[k] * cwx[k:k + 1, :]
            dbc_in = dbc_in + fbc[k] * cwb[k:k + 1, :]
        futx_ref[...] = dxs_pre[0:8, :]
        futb_ref[...] = dbc_pre[0:8, :]
        dxs_ref[...] = dxs_in.astype(dxs_ref.dtype)
        dbc_ref[...] = jnp.concatenate([dbc_in, ddtraw, jnp.zeros((CH, SEG - 640), F32)], axis=1).astype(dbc_ref.dtype)

    row_out = lambda: pl.BlockSpec((CH, D), lambda i: (rev(i), 0))
    outs = pl.pallas_call(
        body, name="ssd_bwd", grid=(n_c,),
        in_specs=[row_out(), row_out(), row_out(), pl.BlockSpec((CH, 512), lambda i: (rev(i), 0))]
        + _proj_specs(CH, (0, 5, 6), rev) + [pl.BlockSpec((1, NS, D), lambda i: (rev(i), 0, 0)),
                                             _full((4, D)), _full((4, 512)),
                                             _full((1, 128)), _full((1, 128)), _full((1, D)), _full((1, D)),
                                             pl.BlockSpec((t_n, fb), lambda i: (0, i)), _full((t_n, D)),
                                             pl.BlockSpec((t_n, fb), lambda i: (0, i)), _full((t_n, D))],
        out_specs=[row_out(), row_out(), row_out(), _full((4, D)), _full((4, 512)), _full((1, D)), _full((1, 512)),
                   _full((1, 128)), _full((1, 128)), _full((1, D)), _full((1, D)),
                   pl.BlockSpec((fb, D), lambda i: (i, 0)), pl.BlockSpec((fb, D), lambda i: (i, 0))],
        out_shape=[jax.ShapeDtypeStruct((t_n, D), MXU)] * 3 + [
            jax.ShapeDtypeStruct(s, F32) for s in ((4, D), (4, 512), (1, D), (1, 512), (1, 128), (1, 128), (1, D), (1, D))]
        + [jax.ShapeDtypeStruct((FF, D), MXU)] * 2,
        scratch_shapes=[pltpu.VMEM((NS, D), F32), pltpu.VMEM((8, D), F32), pltpu.VMEM((8, 512), F32)],
        compiler_params=_cp("arbitrary"),
    )(dya, *saved, *_proj_ops(proj, (0, 5, 6)), sprev, cwx, cwb, dtb, alog, dsk, nrm, *mlp_ops)
    return outs


LRU_ROWS = 256
LRU_BLK = 256


def _lru_gates(xr, wa, wx, ba, bx, lam):
    pr = jnp.concatenate([_nn(xr[:, LRU_BLK * b:LRU_BLK * (b + 1)], wa[b]) for b in range(D // LRU_BLK)], axis=1) + ba
    pi = jnp.concatenate([_nn(xr[:, LRU_BLK * b:LRU_BLK * (b + 1)], wx[b]) for b in range(D // LRU_BLK)], axis=1) + bx
    log_a = -LRU_C * _sigmoid(pr) * _softplus(-lam)
    a = jnp.exp(log_a)
    mult = jnp.sqrt(1.0 - jnp.exp(2.0 * log_a))
    return a, mult * (_sigmoid(pi) * xr)


def _lru_out(h, g):
    return h * jax.nn.gelu(g, approximate=True)


def _lru_fwd(proj, cw, cb, wa, wx, ba, bx, lam):
    t_n = proj[0].shape[0]
    rows = min(LRU_ROWS, t_n)
    fwd = lambda i: i

    def body(g_ref, x_ref, xp_ref, cw_ref, cb_ref, wa_ref, wx_ref, ba_ref, bx_ref, lam_ref, yb_ref, h_ref, xr_ref,
             a_s, u_s, carry):
        i = pl.program_id(0)

        @pl.when(i == 0)
        def _():
            carry[...] = jnp.zeros_like(carry)

        keep = jnp.where(i == 0, 0.0, 1.0)
        xr = _conv_apply(_taps_past(x_ref[...].astype(F32), _prev8(xp_ref) * keep), cw_ref[...], cb_ref[...])
        xr_ref[...] = xr
        a, u = _lru_gates(xr, wa_ref[...], wx_ref[...], ba_ref[...], bx_ref[...], lam_ref[...])
        a_s[...] = a
        u_s[...] = u
        row = lax.broadcasted_iota(jnp.int32, (8, D), 0)

        def blk(b, c):
            s = pl.multiple_of(b * 8, 8)
            av = a_s[pl.ds(s, 8), :]
            uv = u_s[pl.ds(s, 8), :]
            for d in (1, 2, 4):
                m = row >= d
                uv = uv + av * jnp.where(m, pltpu.roll(uv, d, 0), 0.0)
                av = av * jnp.where(m, pltpu.roll(av, d, 0), 1.0)
            hv = uv + av * c
            h_ref[pl.ds(s, 8), :] = hv
            return hv[7:8, :]

        carry[0:1, :] = lax.fori_loop(0, rows // 8, blk, carry[0:1, :])
        yb_ref[...] = _lru_out(h_ref[...], g_ref[...].astype(F32)).astype(yb_ref.dtype)

    return pl.pallas_call(
        body, name="lru_fwd", grid=(t_n // rows,),
        in_specs=_proj_specs(rows, (1, 2), fwd) + _prev_specs(rows, (2,), fwd) + [
            _full((4, D)), _full((1, D)), _full((4, LRU_BLK, LRU_BLK)), _full((4, LRU_BLK, LRU_BLK)),
            _full((1, D)), _full((1, D)), _full((1, D))],
        out_specs=[pl.BlockSpec((rows, D), lambda i: (i, 0))] * 3,
        out_shape=[jax.ShapeDtypeStruct((t_n, D), MXU), jax.ShapeDtypeStruct((t_n, D), F32),
                   jax.ShapeDtypeStruct((t_n, D), F32)],
        scratch_shapes=[pltpu.VMEM((rows, D), F32), pltpu.VMEM((rows, D), F32), pltpu.VMEM((8, D), F32)],
        compiler_params=_cp("arbitrary"),
    )(*_proj_ops(proj, (1, 2, 2)), cw, cb, wa, wx, ba, bx, lam)


def _lru_bwd(dyb, proj, h, xr_saved, cw, wa, wx, ba, bx, lam):
    t_n = proj[0].shape[0]
    rows = min(LRU_ROWS, t_n)
    n_t = t_n // rows
    rev = lambda i: n_t - 1 - i
    rb = rows // 8

    def body(dyb_ref, g_ref, x_ref, h_ref, hp_ref, xr_ref, cw_ref, wa_ref, wx_ref, ba_ref, bx_ref, lam_ref,
             dg_ref, dx_ref, dcw_ref, dcb_ref, dwa_ref, dwx_ref, dba_ref, dbx_ref, dlam_ref,
             a_s, dh_s, hx_s, da_s, du_s, carry, fut):
        i = pl.program_id(0)
        acc_refs = (dcw_ref, dcb_ref, dwa_ref, dwx_ref, dba_ref, dbx_ref, dlam_ref)

        @pl.when(i == 0)
        def _():
            for r in (carry, fut) + acc_refs:
                r[...] = jnp.zeros_like(r)

        keep = jnp.where(i == n_t - 1, 0.0, 1.0)
        gate_in = (xr_ref[...], wa_ref[...], wx_ref[...], ba_ref[...], bx_ref[...], lam_ref[...])
        (a, _), gates_vjp = jax.vjp(_lru_gates, *gate_in)
        _, out_vjp = jax.vjp(_lru_out, h_ref[...], g_ref[...].astype(F32))
        dh, dg = out_vjp(dyb_ref[...])
        dg_ref[...] = dg.astype(dg_ref.dtype)
        a_s[...] = a
        dh_s[...] = dh
        hx_s[0:8, :] = hp_ref[...] * keep
        hx_s[8:, :] = h_ref[...]
        row = lax.broadcasted_iota(jnp.int32, (8, D), 0)

        def blk(b, c):
            s = pl.multiple_of((rb - 1 - b) * 8, 8)
            av = a_s[pl.ds(s, 8), :]
            dhv = dh_s[pl.ds(s, 8), :]
            kv = av * dhv
            for d in (1, 2, 4):
                m = row <= 7 - d
                kv = kv + av * jnp.where(m, pltpu.roll(kv, 8 - d, 0), 0.0)
                av = av * jnp.where(m, pltpu.roll(av, 8 - d, 0), 1.0)
            kv = kv + av * c
            gv = dhv + jnp.where(row < 7, pltpu.roll(kv, 7, 0), c)
            hb = hx_s[pl.ds(s + 8, 8), :]
            hpv = hx_s[pl.ds(s, 8), :]
            hprev = jnp.where(row >= 1, pltpu.roll(hb, 1, 0), hpv[7:8, :])
            du_s[pl.ds(s, 8), :] = gv
            da_s[pl.ds(s, 8), :] = gv * hprev
            return kv[0:1, :]

        carry[0:1, :] = lax.fori_loop(0, rb, blk, carry[0:1, :])
        dxr, dwa, dwx, dba, dbx, dlam = gates_vjp((da_s[...], du_s[...]))
        dwa_ref[...] += dwa
        dwx_ref[...] += dwx
        dba_ref[...] += dba
        dbx_ref[...] += dbx
        dlam_ref[...] += dlam
        dcb_ref[...] += _rowsum(dxr)
        ft = _taps_future(dxr, fut[...])
        x_in = x_ref[...].astype(F32)
        for k in range(4):
            dcw_ref[k:k + 1, :] += _rowsum(ft[k] * x_in)
        cwv = cw_ref[...]
        dx = ft[0] * cwv[0:1, :]
        for k in range(1, 4):
            dx = dx + ft[k] * cwv[k:k + 1, :]
        fut[...] = dxr[0:8, :]
        dx_ref[...] = dx.astype(dx_ref.dtype)

    row_in = lambda: pl.BlockSpec((rows, D), lambda i: (rev(i), 0))
    prev_h = pl.BlockSpec((8, D), lambda i: (jnp.maximum(rev(i) * rb - 1, 0), 0))
    wspec = lambda: _full((4, LRU_BLK, LRU_BLK))
    return pl.pallas_call(
        body, name="lru_bwd", grid=(n_t,),
        in_specs=[row_in()] + _proj_specs(rows, (1, 2), rev) + [row_in(), prev_h, row_in()] + [
            _full((4, D)), wspec(), wspec(), _full((1, D)), _full((1, D)), _full((1, D))],
        out_specs=[row_in(), row_in(), _full((4, D)), _full((1, D)), wspec(), wspec(), _full((1, D)), _full((1, D)),
                   _full((1, D))],
        out_shape=[jax.ShapeDtypeStruct((t_n, D), MXU)] * 2 + [
            jax.ShapeDtypeStruct(s, F32) for s in ((4, D), (1, D), (4, LRU_BLK, LRU_BLK), (4, LRU_BLK, LRU_BLK),
                                                   (1, D), (1, D), (1, D))],
        scratch_shapes=[pltpu.VMEM((rows, D), F32), pltpu.VMEM((rows, D), F32), pltpu.VMEM((rows + 8, D), F32),
                        pltpu.VMEM((rows, D), F32), pltpu.VMEM((rows, D), F32), pltpu.VMEM((8, D), F32),
                        pltpu.VMEM((8, D), F32)],
        compiler_params=_cp("arbitrary"),
    )(dyb, *_proj_ops(proj, (1, 2)), h, h, xr_saved, cw, wa, wx, ba, bx, lam)


def _merge_out(ya, yb, proj, x, wout, g1):
    t_n = x.shape[0]
    tm = min(t_n, 512)

    def body(ya_ref, yb_ref, ga_ref, gb_ref, x_ref, w_ref, g_ref, h1_ref, mix_ref, mg_ref):
        merged = (_sigmoid(ga_ref[...].astype(F32)) * ya_ref[...].astype(F32)
                  + _sigmoid(gb_ref[...].astype(F32)) * yb_ref[...].astype(F32))
        mg = merged.astype(MXU)
        mg_ref[...] = mg
        mix = jnp.dot(mg, w_ref[...], preferred_element_type=F32)
        mix_ref[...] = mix
        h1_ref[...] = x_ref[...] + _rms(mix, g_ref[...])

    row = lambda: pl.BlockSpec((tm, D), lambda i: (i, 0))
    return pl.pallas_call(
        body, name="merge_out", grid=(t_n // tm,),
        in_specs=[row(), row()] + _proj_specs(tm, (3, 4), lambda i: i) + [row(), _full((D, D)), _full((1, D))],
        out_specs=[row(), row(), row()],
        out_shape=[jax.ShapeDtypeStruct((t_n, D), F32), jax.ShapeDtypeStruct((t_n, D), F32),
                   jax.ShapeDtypeStruct((t_n, D), MXU)],
        compiler_params=_cp("parallel"),
    )(ya, yb, *_proj_ops(proj, (3, 4)), x, wout, g1)


def _out_bwd(dv, h1, g2, dout, mix, ya, yb, proj, wout, g1):
    t_n = dv.shape[0]
    tm = min(t_n, 256)

    def body(dv_ref, h1_ref, g2_ref, dout_ref, mix_ref, ya_ref, yb_ref, ga_ref, gb_ref, w_ref, g_ref,
             dh1_ref, dmix_ref, dya_ref, dyb_ref, dga_ref, dgb_ref, dg2_ref, dg1_ref):
        @pl.when(pl.program_id(0) == 0)
        def _():
            dg1_ref[...] = jnp.zeros_like(dg1_ref)
            dg2_ref[...] = jnp.zeros_like(dg2_ref)

        dx, dg_rows = _rms_bwd(h1_ref[...], g2_ref[...], dv_ref[...])
        dg2_ref[...] += _rowsum(dg_rows)
        dh1 = dout_ref[...] + dx
        dh1_ref[...] = dh1
        dmix, dg_rows = _rms_bwd(mix_ref[...], g_ref[...], dh1)
        dg1_ref[...] += _rowsum(dg_rows)
        dmix_b = dmix.astype(MXU)
        dmix_ref[...] = dmix_b
        dmg = lax.dot_general(dmix_b, w_ref[...], (((1,), (1,)), ((), ())), preferred_element_type=F32)
        sa = _sigmoid(ga_ref[...].astype(F32))
        sb = _sigmoid(gb_ref[...].astype(F32))
        dya_ref[...] = dmg * sa
        dyb_ref[...] = dmg * sb
        dga_ref[...] = (dmg * ya_ref[...].astype(F32) * sa * (1.0 - sa)).astype(MXU)
        dgb_ref[...] = (dmg * yb_ref[...].astype(F32) * sb * (1.0 - sb)).astype(MXU)

    row = lambda: pl.BlockSpec((tm, D), lambda i: (i, 0))
    vec = lambda: _full((1, D))
    return pl.pallas_call(
        body, name="out_bwd", grid=(t_n // tm,),
        in_specs=[row(), row(), vec(), row(), row(), row(), row()] + _proj_specs(tm, (3, 4), lambda i: i)
        + [_full((D, D)), vec()],
        out_specs=[row(), row(), row(), row(), row(), row(), vec(), vec()],
        out_shape=[jax.ShapeDtypeStruct((t_n, D), F32), jax.ShapeDtypeStruct((t_n, D), MXU),
                   jax.ShapeDtypeStruct((t_n, D), F32), jax.ShapeDtypeStruct((t_n, D), F32),
                   jax.ShapeDtypeStruct((t_n, D), MXU), jax.ShapeDtypeStruct((t_n, D), MXU),
                   jax.ShapeDtypeStruct((1, D), F32), jax.ShapeDtypeStruct((1, D), F32)],
        compiler_params=_cp("arbitrary"),
    )(dv, h1, g2, dout, mix, ya, yb, *_proj_ops(proj, (3, 4)), wout, g1)


MLP_TM = 1024
MLP_TF_FWD = 1024
MLP_TF_BWD = 1024


def _mlp_fwd(h1, g2, wup, wdown, g3, tgt):
    t_n = h1.shape[0]
    tm = min(t_n, MLP_TM)
    n_f = FF // MLP_TF_FWD

    def body(h1_ref, g2_ref, wu_ref, wd_ref, g3_ref, tgt_ref, hp_ref, v_ref, dout_ref, dff_ref, loss_ref, dg3_ref, acc):
        i, j = pl.program_id(0), pl.program_id(1)

        @pl.when((i == 0) & (j == 0))
        def _():
            loss_ref[...] = jnp.zeros_like(loss_ref)
            dg3_ref[...] = jnp.zeros_like(dg3_ref)

        @pl.when(j == 0)
        def _():
            v_ref[...] = _rms(h1_ref[...], g2_ref[...]).astype(MXU)
            acc[...] = jnp.zeros_like(acc)

        hp = lax.dot_general(v_ref[...], wu_ref[...], (((1,), (1,)), ((), ())), preferred_element_type=F32)
        hp_ref[...] = hp.astype(MXU)
        hid = jnp.square(jnp.maximum(hp, 0.0))
        acc[...] += jnp.dot(hid.astype(MXU), wd_ref[...], preferred_element_type=F32)

        @pl.when(j == n_f - 1)
        def _():
            ff = acc[...]
            err = h1_ref[...] + _rms(ff, g3_ref[...]) - tgt_ref[...]
            loss_ref[...] += 0.5 * jnp.sum(jnp.mean(err * err, axis=-1, keepdims=True), axis=0, keepdims=True)
            dout = err * (1.0 / D)
            dout_ref[...] = dout
            dff, dg_rows = _rms_bwd(ff, g3_ref[...], dout)
            dg3_ref[...] += _rowsum(dg_rows)
            dff_ref[...] = dff.astype(MXU)

    row = lambda: pl.BlockSpec((tm, D), lambda i, j: (i, 0))
    vec = lambda: pl.BlockSpec((1, D), lambda i, j: (0, 0))
    return pl.pallas_call(
        body, name="mlp_fwd", grid=(t_n // tm, n_f),
        in_specs=[row(), vec(), pl.BlockSpec((MLP_TF_FWD, D), lambda i, j: (j, 0)),
                  pl.BlockSpec((MLP_TF_FWD, D), lambda i, j: (j, 0)), vec(), row()],
        out_specs=[pl.BlockSpec((tm, MLP_TF_FWD), lambda i, j: (i, j)), row(), row(), row(),
                   pl.BlockSpec((1, 1), lambda i, j: (0, 0)), vec()],
        out_shape=[jax.ShapeDtypeStruct((t_n, FF), MXU), jax.ShapeDtypeStruct((t_n, D), MXU),
                   jax.ShapeDtypeStruct((t_n, D), F32), jax.ShapeDtypeStruct((t_n, D), MXU),
                   jax.ShapeDtypeStruct((1, 1), F32), jax.ShapeDtypeStruct((1, D), F32)],
        scratch_shapes=[pltpu.VMEM((tm, D), F32)],
        compiler_params=_cp("arbitrary", "arbitrary"),
    )(h1, g2, wup, wdown, g3, tgt)


def _mlp_bwd(dff, hp, wup, wdown):
    t_n = dff.shape[0]
    tm = min(t_n, MLP_TM)
    n_f = FF // MLP_TF_BWD

    def mm_body(dff_ref, hp_ref, wu_ref, wd_ref, dv_ref, dhp_ref, hid_ref):
        @pl.when(pl.program_id(1) == 0)
        def _():
            dv_ref[...] = jnp.zeros_like(dv_ref)

        relu = jnp.maximum(hp_ref[...].astype(F32), 0.0)
        hid_ref[...] = jnp.square(relu).astype(MXU)
        dhid = lax.dot_general(dff_ref[...], wd_ref[...], (((1,), (1,)), ((), ())), preferred_element_type=F32)
        dhp = (dhid * (2.0 * relu)).astype(MXU)
        dhp_ref[...] = dhp
        dv_ref[...] += jnp.dot(dhp, wu_ref[...], preferred_element_type=F32)

    row = lambda: pl.BlockSpec((tm, D), lambda i, j: (i, 0))
    blk = lambda: pl.BlockSpec((tm, MLP_TF_BWD), lambda i, j: (i, j))
    wblk = lambda: pl.BlockSpec((MLP_TF_BWD, D), lambda i, j: (j, 0))
    return pl.pallas_call(
        mm_body, name="mlp_bwd", grid=(t_n // tm, n_f),
        in_specs=[row(), blk(), wblk(), wblk()], out_specs=[row(), blk(), blk()],
        out_shape=[jax.ShapeDtypeStruct((t_n, D), F32), jax.ShapeDtypeStruct((t_n, FF), MXU),
                   jax.ShapeDtypeStruct((t_n, FF), MXU)],
        compiler_params=_cp("parallel", "arbitrary"),
    )(dff, hp, wup, wdown)


def _wgrad(a, g, name):
    t_n, k_n = a.shape
    n_n = g.shape[1]
    tt = min(t_n, 1024)
    tk, tn = min(k_n, 1024), min(n_n, 1024)

    n_t = t_n // tt

    def body(a_ref, g_ref, o_ref, acc):
        t = pl.program_id(2)

        @pl.when(t == 0)
        def _():
            acc[...] = jnp.zeros_like(acc)

        acc[...] += lax.dot_general(a_ref[...], g_ref[...], (((0,), (0,)), ((), ())), preferred_element_type=F32)

        @pl.when(t == n_t - 1)
        def _():
            o_ref[...] = acc[...].astype(o_ref.dtype)

    return pl.pallas_call(
        body, name=name, grid=(k_n // tk, n_n // tn, n_t),
        in_specs=[pl.BlockSpec((tt, tk), lambda k, n, t: (t, k)), pl.BlockSpec((tt, tn), lambda k, n, t: (t, n))],
        out_specs=pl.BlockSpec((tk, tn), lambda k, n, t: (k, n)),
        out_shape=jax.ShapeDtypeStruct((k_n, n_n), MXU),
        scratch_shapes=[pltpu.VMEM((tk, tn), F32)],
        compiler_params=_cp("parallel", "parallel", "arbitrary"),
    )(a, g)


def _wgrad_segs(segs, g, name):
    t_n, n_n = g.shape
    n_s = len(segs)
    tt = min(t_n, 1024)
    n_t = t_n // tt

    def body(*refs):
        a_refs = refs[:n_s]
        g_ref, o_ref, acc = refs[n_s:]
        s_id, t = pl.program_id(0), pl.program_id(1)

        @pl.when(t == 0)
        def _():
            acc[...] = jnp.zeros_like(acc)

        for s in range(n_s):
            @pl.when(s_id == s)
            def _(s=s):
                acc[...] += lax.dot_general(a_refs[s][...], g_ref[...], (((0,), (0,)), ((), ())),
                                            preferred_element_type=F32)

        @pl.when(t == n_t - 1)
        def _():
            o_ref[...] = acc[...].astype(o_ref.dtype)

    seg_spec = lambda s: pl.BlockSpec((tt, SEG), lambda i, t: (jnp.where(i == s, t, jnp.where(i < s, 0, n_t - 1)), 0))
    return pl.pallas_call(
        body, name=name, grid=(n_s, n_t),
        in_specs=[seg_spec(s) for s in range(n_s)] + [pl.BlockSpec((tt, n_n), lambda i, t: (t, 0))],
        out_specs=pl.BlockSpec((SEG, n_n), lambda i, t: (i, 0)),
        out_shape=jax.ShapeDtypeStruct((n_s * SEG, n_n), MXU),
        scratch_shapes=[pltpu.VMEM((SEG, n_n), F32)],
        compiler_params=_cp("arbitrary", "arbitrary"),
    )(*segs, g)


def _inproj_bwd(dsegs, w, x, g0, dh1):
    t_n = x.shape[0]
    tm = min(t_n, 1024)
    n_k = NP // SEG

    def mm_body(*refs):
        dp_refs = refs[:n_k]
        w_ref, du_ref = refs[n_k:]
        k = pl.program_id(1)

        @pl.when(k == 0)
        def _():
            du_ref[...] = jnp.zeros_like(du_ref)

        for s in range(n_k):
            @pl.when(k == s)
            def _(s=s):
                du_ref[...] += jnp.dot(dp_refs[s][...], w_ref[...], preferred_element_type=F32)

    du = pl.pallas_call(
        mm_body, name="inproj_bwd", grid=(t_n // tm, n_k),
        in_specs=[pl.BlockSpec((tm, SEG), functools.partial(
            lambda i, k, s: (jnp.where(k >= s, i, jnp.maximum(i - 1, 0)), 0), s=s)) for s in range(n_k)] + [
            pl.BlockSpec((SEG, D), lambda i, k: (k, 0))],
        out_specs=pl.BlockSpec((tm, D), lambda i, k: (i, 0)),
        out_shape=jax.ShapeDtypeStruct((t_n, D), F32),
        compiler_params=_cp("parallel", "arbitrary"),
    )(*dsegs, w)

    tf = min(t_n, 512)

    def fin_body(du_ref, x_ref, g_ref, dh1_ref, dx_ref, dg0_ref):
        @pl.when(pl.program_id(0) == 0)
        def _():
            dg0_ref[...] = jnp.zeros_like(dg0_ref)

        dx, dg_rows = _rms_bwd(x_ref[...], g_ref[...], du_ref[...])
        dg0_ref[...] += _rowsum(dg_rows)
        dx_ref[...] = dh1_ref[...] + dx

    row = lambda: pl.BlockSpec((tf, D), lambda i: (i, 0))
    return pl.pallas_call(
        fin_body, name="grad_x", grid=(t_n // tf,), in_specs=[row(), row(), _full((1, D)), row()],
        out_specs=[row(), _full((1, D))],
        out_shape=[jax.ShapeDtypeStruct((t_n, D), F32), jax.ShapeDtypeStruct((1, D), F32)],
        compiler_params=_cp("arbitrary"),
    )(du, x, g0, dh1)


def _blockdiag4(w):
    w4 = w.reshape(4, 4, 64, 1, 64).astype(MXU)
    same = (jnp.arange(4)[:, None, None, None] == jnp.arange(4)[None, None, :, None])
    return jnp.where(same[None], w4, jnp.zeros((), MXU)).reshape(4, 256, 256)


def _blockdiag4_extract(g):
    g5 = g.reshape(4, 4, 64, 4, 64)
    return jnp.stack([g5[:, q, :, q, :] for q in range(4)], axis=1).reshape(NH, 64, 64)


def _local_step(x, tgt, p, after_ssd=None, late_weights=None, send_mlp_grads=None, send_late_grads=None):
    f = lambda a: a.astype(F32)
    proj, u = _inproj(x, p["norm_mix_pre"], p["w_in_pT"])
    ssm_params = (p["cw_xs"], p["cw_bc"], p["cb_xs"], p["cb_bc"], p["dt_bias"], p["a_log"], p["d_skip_x"],
                  p["ssm_norm"])
    ya, sprev, *ssd_saved = _ssd_fwd(proj, *ssm_params)
    cb_lru = p["conv_lru_b"] if after_ssd is None else p["conv_lru_b"] + after_ssd(ya)
    lru_params = (p["conv_lru_w"], cb_lru, p["wa_bd"], p["wx_bd"], p["lru_ba"], p["lru_bx"], p["lru_lambda"])
    yb, h, xr = _lru_fwd(proj, *lru_params)
    if late_weights is not None:
        p = dict(p, **late_weights(yb))
    h1, mix, merged = _merge_out(ya, yb, proj, x, p["w_out"], p["norm_mix_post"])
    hp, v, dout, dff, loss, dg3 = _mlp_fwd(h1, p["norm_mlp_pre"], p["w_upT"], p["w_down"], p["norm_mlp_post"], tgt)

    dv, dhp, hid = _mlp_bwd(dff, hp, p["w_upT"], p["w_down"])
    dh1, dmix, dya, dyb, dga, dgb, dg2, dg1 = _out_bwd(dv, h1, p["norm_mlp_pre"], dout, mix, ya, yb, proj, p["w_out"],
                                                     p["norm_mix_post"])
    d_w_out = _wgrad(merged, dmix, "wgrad_out")
    (dz, dxs, dbc, dcwx, dcwb, dcbx, dcbb, ddtb, dalog, ddsk, dnrm, d_w_down, d_w_up_t) = _ssd_bwd(
        dya, ssd_saved, proj, sprev, p["cw_xs"], p["cw_bc"], *ssm_params[4:], (hid, dff, dhp, v))
    ba_lru = p["lru_ba"] if send_mlp_grads is None else p["lru_ba"] + send_mlp_grads(d_w_up_t, d_w_down, d_w_out)
    (dgl, dxl, dcwl, dcbl, dwa, dwx, dba, dbx, dlam) = _lru_bwd(
        dyb, proj, h, xr, p["conv_lru_w"], p["wa_bd"], p["wx_bd"], ba_lru, p["lru_bx"], p["lru_lambda"])
    dsegs = [dz, dgl, dxl, dga, dgb, dxs, dbc]
    d_w_in_pt = _wgrad_segs(dsegs, u, "wgrad_in")
    g0 = p["norm_mix_pre"]
    if send_late_grads is not None:
        g0 = g0 + send_late_grads(d_w_in_pt)
    grad_x, dg0 = _inproj_bwd(dsegs, p["w_in_pT"], x, g0, dh1)
    grads = dict(
        norm_mix_pre=dg0, w_in_pT=d_w_in_pt, conv_ssm_w=jnp.concatenate([dcwx, dcwb], axis=1),
        conv_ssm_b=jnp.concatenate([dcbx, dcbb], axis=1), dt_bias=ddtb[:, :NH], a_log=dalog[:, :NH],
        d_skip=f(ddsk).reshape(NH, 64).sum(axis=1)[None, :], ssm_norm=dnrm, conv_lru_w=dcwl, conv_lru_b=dcbl,
        lru_wa=_blockdiag4_extract(dwa), lru_ba=dba, lru_wx=_blockdiag4_extract(dwx), lru_bx=dbx, lru_lambda=dlam,
        w_out=d_w_out, norm_mix_post=dg1, norm_mlp_pre=dg2, w_upT=d_w_up_t, w_down=d_w_down, norm_mlp_post=dg3)
    return loss[0, 0], grad_x, grads


W_IN_COLS = 6672


def _w_in_t_to_padded(wt):
    z, xs, bc, dt = wt[0:1024], wt[1024:2048], wt[2048:2560], wt[2560:2576]
    gl, xl, ga, gb = wt[2576:3600], wt[3600:4624], wt[4624:5648], wt[5648:6672]
    return jnp.concatenate([z, gl, xl, ga, gb, xs, bc, dt, jnp.zeros((NP - 6672, wt.shape[1]), wt.dtype)], axis=0)


def _w_in_t_from_padded(wp):
    z, gl, xl, ga, gb = (wp[SEG * s:SEG * (s + 1)] for s in range(5))
    xs, bc, dt = wp[5120:6144], wp[6144:6656], wp[6656:6672]
    return jnp.concatenate([z, xs, bc, dt, gl, xl, ga, gb], axis=0)


def _prep_params(full, big):
    f = lambda a: a.astype(F32)
    pad128 = lambda a: jnp.pad(f(a).reshape(1, -1), ((0, 0), (0, 128 - a.size)))
    cw = f(full["conv_ssm_w"])
    cb = f(full["conv_ssm_b"]).reshape(1, -1)
    return dict(
        big, norm_mix_pre=f(full["norm_mix_pre"]).reshape(1, D),
        cw_xs=cw[:, :D], cw_bc=cw[:, D:], cb_xs=cb[:, :D], cb_bc=cb[:, D:],
        dt_bias=pad128(full["dt_bias"]), a_log=pad128(full["a_log"]),
        d_skip_x=jnp.repeat(f(full["d_skip"]).reshape(-1), 64).reshape(1, D), ssm_norm=f(full["ssm_norm"]).reshape(1, D),
        conv_lru_w=f(full["conv_lru_w"]), conv_lru_b=f(full["conv_lru_b"]).reshape(1, D),
        wa_bd=_blockdiag4(full["lru_wa"]), wx_bd=_blockdiag4(full["lru_wx"]),
        lru_ba=f(full["lru_ba"]).reshape(1, D), lru_bx=f(full["lru_bx"]).reshape(1, D),
        lru_lambda=f(full["lru_lambda"]).reshape(1, D),
        norm_mix_post=f(full["norm_mix_post"]).reshape(1, D), norm_mlp_pre=f(full["norm_mlp_pre"]).reshape(1, D),
        norm_mlp_post=f(full["norm_mlp_post"]).reshape(1, D))


MESH_ID = pl.DeviceIdType.MESH
ANY = pl.BlockSpec(memory_space=pl.ANY)


def _my_place():
    x, y, c = lax.axis_index("x"), lax.axis_index("y"), lax.axis_index("c")
    return x, y, c, 4 * x + 2 * y + c


def _peer(x, y, c, k):
    return (x ^ ((k >> 2) & 1), y ^ ((k >> 1) & 1), c ^ (k & 1))


def _all_gather(pack, name):
    r_n = pack.shape[0]
    half = -(-r_n // 32) * 16
    n_cp = 9

    def body(in_ref, out_ref, send_sems, recv_sems, local_sem):
        x, y, c, me = _my_place()
        here, sibling, x_nbr, y_nbr = (x, y, c), (x, y, 1 - c), (1 - x, y, c), (x, 1 - y, c)
        part = {"all": pl.ds(0, r_n), "lo": pl.ds(0, half), "hi": pl.ds(half, r_n - half)}

        def copy(j, block, rows, to, src=None):
            return pltpu.make_async_remote_copy(
                src_ref=out_ref.at[block, part[rows]] if src is None else src, dst_ref=out_ref.at[block, part[rows]],
                send_sem=send_sems.at[j], recv_sem=recv_sems.at[j], device_id=to, device_id_type=MESH_ID)

        mine = pltpu.make_async_copy(in_ref, out_ref.at[me], local_sem)
        mine.start()
        first = [copy(0, me, "all", sibling, src=in_ref), copy(1, me, "all", x_nbr, src=in_ref),
                 copy(2, me, "all", y_nbr, src=in_ref)]
        for cp in first:
            cp.start()
        relay = [[(3, me ^ 4, "lo", y_nbr), (5, me ^ 4, "all", sibling)],
                 [(4, me ^ 2, "hi", x_nbr), (6, me ^ 2, "all", sibling)],
                 [(7, me ^ 6, "lo", sibling)],
                 [(8, me ^ 6, "hi", sibling)]]
        landed = [(1, me ^ 4, "all"), (2, me ^ 2, "all"), (3, me ^ 6, "lo"), (4, me ^ 6, "hi")]
        passed = []
        for (j, block, rows), nxt in zip(landed, relay):
            copy(j, block, rows, here).wait_recv()
            for args in nxt:
                cp = copy(*args)
                cp.start()
                passed.append(cp)
        for j, block, rows in ((0, me ^ 1, "all"), (5, me ^ 5, "all"), (6, me ^ 3, "all"), (7, me ^ 7, "lo"),
                               (8, me ^ 7, "hi")):
            copy(j, block, rows, here).wait_recv()
        for cp in first + passed:
            cp.wait_send()
        mine.wait()

    return pl.pallas_call(
        body, name=name, in_specs=[ANY], out_specs=ANY,
        out_shape=jax.ShapeDtypeStruct((N_DEV,) + pack.shape, pack.dtype),
        scratch_shapes=[pltpu.SemaphoreType.DMA((n_cp,)), pltpu.SemaphoreType.DMA((n_cp,)), pltpu.SemaphoreType.DMA],
    )(pack)


HBM = pl.BlockSpec(memory_space=pltpu.HBM)
SEM = pl.BlockSpec(memory_space=pltpu.SEMAPHORE)
PLAN_GATHER = tuple((k, "pack", 0) for k in range(1, N_DEV))
PLAN_SCATTER = tuple((k, "slot", 0) for k in range(1, N_DEV))
PLAN_GATHER_ICI = tuple((k, "pack", 0) for k in (2, 4, 6))
PLAN_GATHER_D2D = ((1, "pack", 0),) + tuple((1, s, s) for s in (2, 4, 6))


def _plan_copy(j, plan, src_ref, land_ref, sems):
    k, source, slot = plan[j]
    x, y, c, me = _my_place()
    if source == "pack":
        src = src_ref
    elif source == "slot":
        src = src_ref.at[me ^ k]
    else:
        src = land_ref.at[me ^ source]
    return pltpu.make_async_remote_copy(
        src_ref=src, dst_ref=land_ref.at[me ^ slot], send_sem=sems[j], recv_sem=sems[len(plan) + j],
        device_id=_peer(x, y, c, k), device_id_type=MESH_ID)


def _exchange_start(src, land, plan, name):
    n_c = len(plan)
    if land is None:
        land = pltpu.with_memory_space_constraint(lax.empty((N_DEV,) + src.shape[-2:], src.dtype), pltpu.HBM)

    def body(src_ref, land_ref, *rest):
        sems, token = rest[:2 * n_c], rest[2 * n_c + 2]
        for j in range(n_c):
            _plan_copy(j, plan, src_ref, land_ref, sems).start()
        token[...] = jnp.zeros_like(token)

    outs = pl.pallas_call(
        body, name=name,
        out_shape=(pltpu.SemaphoreType.DMA(()),) * (2 * n_c) + (
            pltpu.HBM(src.shape, src.dtype), pltpu.HBM(land.shape, land.dtype), jax.ShapeDtypeStruct((8, 128), F32)),
        in_specs=(HBM, HBM), out_specs=(SEM,) * (2 * n_c) + (HBM, HBM, pl.BlockSpec(memory_space=pltpu.VMEM)),
        input_output_aliases={0: 2 * n_c, 1: 2 * n_c + 1},
        compiler_params=pltpu.CompilerParams(has_side_effects=pltpu.SideEffectType.DATAFLOW_SIDE_EFFECTING),
    )(pltpu.with_memory_space_constraint(src, pltpu.HBM), land)
    return outs[:2 * n_c], outs[2 * n_c], outs[2 * n_c + 1], outs[2 * n_c + 2]


def _exchange_wait(sems, src_thru, land_thru, after, plan, name):
    n_c = len(plan)

    def body(src_ref, land_ref, *rest):
        for j in range(n_c):
            cp = _plan_copy(j, plan, src_ref, land_ref, rest[:2 * n_c])
            cp.wait_send()
            cp.wait_recv()

    return pl.pallas_call(
        body, name=name,
        out_shape=(pltpu.HBM(src_thru.shape, src_thru.dtype), pltpu.HBM(land_thru.shape, land_thru.dtype)),
        in_specs=(HBM, HBM) + (SEM,) * (2 * n_c) + (pl.BlockSpec(memory_space=pl.ANY),), out_specs=(HBM, HBM),
        input_output_aliases={0: 0, 1: 1},
        compiler_params=pltpu.CompilerParams(has_side_effects=pltpu.SideEffectType.DATAFLOW_SIDE_EFFECTING),
    )(src_thru, land_thru, *sems, after)


def _slot_sum(parts, name):
    r_n, c_n = parts.shape[1:]
    tr = max(t for t in range(16, 513, 16) if r_n % t == 0)

    def body(p_ref, o_ref):
        acc = p_ref[0].astype(F32)
        for k in range(1, N_DEV):
            acc = acc + p_ref[k].astype(F32)
        o_ref[...] = acc

    return pl.pallas_call(
        body, name=name, grid=(r_n // tr,),
        in_specs=[pl.BlockSpec((N_DEV, tr, c_n), lambda i: (0, i, 0))],
        out_specs=pl.BlockSpec((tr, c_n), lambda i: (i, 0)),
        out_shape=jax.ShapeDtypeStruct((r_n, c_n), F32),
        compiler_params=_cp("parallel"),
    )(parts)


def _adam_math(w, g, m, v):
    m = ADAM_B1 * m + (1.0 - ADAM_B1) * g
    v = ADAM_B2 * v + (1.0 - ADAM_B2) * jnp.square(g)
    m_hat = m / (1.0 - ADAM_B1 ** ADAM_STEP)
    v_hat = v / (1.0 - ADAM_B2 ** ADAM_STEP)
    return -ADAM_LR * (m_hat / (jnp.sqrt(v_hat) + ADAM_EPS) + ADAM_WD * w), m, v


def _adam_big(w, g, m, v, name):
    def body(w_ref, g_ref, m_ref, v_ref, d_ref, mo_ref, vo_ref):
        d_ref[...], mo_ref[...], vo_ref[...] = _adam_math(w_ref[...], g_ref[...], m_ref[...], v_ref[...])

    if w.ndim == 3:
        _, r_n, c_n = w.shape
        tr = min(r_n, 256)
        grid = (r_n // tr,)
        blk = lambda: pl.BlockSpec((1, tr, c_n), lambda i: (0, i, 0))
    else:
        r_n, c_n = w.shape
        tc = min(c_n, 256)
        grid = (c_n // tc,)
        blk = lambda: pl.BlockSpec((r_n, tc), lambda i: (0, i))
    return pl.pallas_call(
        body, name=name, grid=grid, in_specs=[blk(), blk(), blk(), blk()], out_specs=[blk(), blk(), blk()],
        out_shape=[jax.ShapeDtypeStruct(w.shape, F32)] * 3, compiler_params=_cp("parallel"),
    )(w, g, m, v)


def _adam_small(groups, where, wmv):
    n, n_g = len(wmv), len(groups)

    def body(*refs):
        g_refs = refs[:n_g]
        w_refs = refs[n_g:n_g + 3 * n]
        o_refs = refs[n_g + 3 * n:]
        for q in range(n):
            w_ref, m_ref, v_ref = w_refs[3 * q:3 * q + 3]
            r, c = w_ref.shape
            gi, r0 = where[q]
            g = g_refs[gi][0, r0:r0 + r, 0:c]
            for k in range(1, N_DEV):
                g = g + g_refs[gi][k, r0:r0 + r, 0:c]
            d, m, v = _adam_math(w_ref[...], g, m_ref[...], v_ref[...])
            o_refs[4 * q][...] = g
            o_refs[4 * q + 1][...] = d
            o_refs[4 * q + 2][...] = m
            o_refs[4 * q + 3][...] = v

    flat_wmv = [a for t in wmv for a in t]
    vm = pl.BlockSpec(memory_space=pltpu.VMEM)
    outs = pl.pallas_call(
        body, name="adam_small", in_specs=[vm] * (n_g + 3 * n), out_specs=[vm] * (4 * n),
        out_shape=[jax.ShapeDtypeStruct(t[0].shape, F32) for t in wmv for _ in range(4)],
        compiler_params=pltpu.CompilerParams(vmem_limit_bytes=VMEM_LIMIT),
    )(*groups, *flat_wmv)
    return [tuple(outs[4 * q:4 * q + 4]) for q in range(n)]


WEIGHTS = ["norm_mix_pre", "w_in", "conv_ssm_w", "conv_ssm_b", "dt_bias", "a_log", "d_skip", "ssm_norm", "conv_lru_w",
           "conv_lru_b", "lru_wa", "lru_ba", "lru_wx", "lru_bx", "lru_lambda", "w_out", "norm_mix_post", "norm_mlp_pre",
           "w_up", "w_down", "norm_mlp_post"]
BIG = ["w_out", "w_up", "w_down", "w_in"]
IN_ROWS = W_IN_COLS // N_DEV
IN_PAD, EARLY_ROWS = 848, 880
ROW_UP, ROW_DOWN, LATE_ROWS = 128, 640, 1152
GRAD_LATE_ROWS = 864
CONV_SSM_COLS, CONV_LRU_COLS = 1536 // N_DEV, D // N_DEV
SMALL = [("norm_mix_pre", (1, D), 0, 0), ("ssm_norm", (1, D), 0, 1), ("conv_lru_b", (1, D), 0, 2),
         ("lru_lambda", (1, D), 0, 3), ("norm_mix_post", (1, D), 0, 4), ("norm_mlp_pre", (1, D), 0, 5),
         ("norm_mlp_post", (1, D), 0, 6), ("conv_ssm_b", (1, 1536), 1, 0), ("dt_bias", (1, NH), 2, 0),
         ("a_log", (1, NH), 2, 1), ("d_skip", (1, NH), 2, 2), ("conv_ssm_w", (4, CONV_SSM_COLS), 3, 0),
         ("conv_lru_w", (4, CONV_LRU_COLS), 4, 0), ("lru_wa", (D, 64), 5, 0), ("lru_wx", (D, 64), 5, D),
         ("lru_ba", (NH, 64), 6, 0), ("lru_bx", (NH, 64), 6, NH)]
SMALL_GROUPS = [(8, D), (1, 1536), (8, 128), (4, 1536), (4, D), (2 * D, 64), (2 * NH, 64)]


def _pad_rows(flat, mult):
    n = flat.shape[0]
    rows = -(-n // (128 * mult)) * mult
    return jnp.pad(flat, (0, rows * 128 - n)).reshape(rows, 128)


def _split3(a):
    hi = a.astype(MXU)
    r1 = a - hi.astype(F32)
    mid = r1.astype(MXU)
    lo = (r1 - mid.astype(F32)).astype(MXU)
    return jnp.stack([hi, mid, lo])


def _early_pack(a, me):
    bf = lambda t: t.astype(MXU)
    conv = lambda t, c: jnp.pad(_split3(t).reshape(12, c), ((0, 4), (0, D - c)))
    shifted = lax.dynamic_update_slice(jnp.zeros((IN_PAD, D), MXU), bf(a["w_in"][0]).T, (2 * me, 0))
    return jnp.concatenate([shifted, conv(a["conv_ssm_w"][0], CONV_SSM_COLS), conv(a["conv_lru_w"][0], CONV_LRU_COLS)],
                           axis=0)


TILE = 16
SHARD_TILES = IN_PAD // TILE
SHARD_STEP = (IN_ROWS // TILE)
SEG_TILES = ((0, 64, 0), (64, 128, 320), (128, 160, 384), (160, 161, 416), (161, 225, 64), (225, 289, 128),
             (289, 353, 192), (353, 417, 256))


def _tile_runs(lo, hi):
    runs = []
    for s0, s1, d0 in SEG_TILES:
        a, b = max(lo, s0), min(hi, s1)
        if a < b:
            runs.append((a, b - a, d0 + a - s0))
    return runs


def _assemble_w_in(g):
    whole = []
    for k in range(N_DEV):
        lo = SHARD_STEP * k + (1 if k else 0)
        hi = SHARD_STEP * (k + 1) + (1 if k == N_DEV - 1 else 0)
        whole += [(k, a - SHARD_STEP * k, n, d) for a, n, d in _tile_runs(lo, hi)]
    split = [(k, _tile_runs(SHARD_STEP * k, SHARD_STEP * k + 1)[0][2]) for k in range(1, N_DEV)]

    def body(g_ref, o_ref):
        rows = lambda t, n=1: pl.ds(TILE * t, TILE * n)
        for k, t, n, d in whole:
            o_ref[rows(d, n), :] = g_ref[k, rows(t, n), :]
        for k, d in split:
            o_ref[rows(d), :] = g_ref[k - 1, rows(SHARD_STEP), :] + g_ref[k, rows(0), :]
        o_ref[pl.ds(W_IN_COLS, NP - W_IN_COLS), :] = jnp.zeros((NP - W_IN_COLS, D), o_ref.dtype)

    vm = pl.BlockSpec(memory_space=pltpu.VMEM)
    return pl.pallas_call(
        body, name="assemble_w_in", in_specs=[vm], out_specs=vm, out_shape=jax.ShapeDtypeStruct((NP, D), g.dtype),
        compiler_params=pltpu.CompilerParams(vmem_limit_bytes=VMEM_LIMIT),
    )(g)


def _scatter_w_in_grad(dw):
    runs = [(k, a - SHARD_STEP * k, n, d) for k in range(N_DEV)
            for a, n, d in _tile_runs(SHARD_STEP * k, SHARD_STEP * k + SHARD_TILES)]
    pad = GRAD_LATE_ROWS - IN_PAD

    def body(dw_ref, o_ref):
        rows = lambda t, n: pl.ds(TILE * t, TILE * n)
        for k, t, n, d in runs:
            o_ref[k, rows(t, n), :] = dw_ref[rows(d, n), :]
        for k in range(N_DEV):
            o_ref[k, pl.ds(IN_PAD, pad), :] = jnp.zeros((pad, D), o_ref.dtype)

    vm = pl.BlockSpec(memory_space=pltpu.VMEM)
    return pl.pallas_call(
        body, name="scatter_w_in_grad", in_specs=[vm], out_specs=vm,
        out_shape=jax.ShapeDtypeStruct((N_DEV, GRAD_LATE_ROWS, D), dw.dtype),
        compiler_params=pltpu.CompilerParams(vmem_limit_bytes=VMEM_LIMIT),
    )(dw)


def _early_unpack(g):
    w_in_pt = _assemble_w_in(g)
    conv = {}
    for n, r0, c in (("conv_ssm_w", IN_PAD, CONV_SSM_COLS), ("conv_lru_w", IN_PAD + 16, CONV_LRU_COLS)):
        s = g[:, r0:r0 + 12, :c].astype(F32).reshape(N_DEV, 3, 4, c)
        conv[n] = ((s[:, 0] + s[:, 1]) + s[:, 2]).transpose(1, 0, 2).reshape(4, N_DEV * c)
    return w_in_pt, conv


def _late_pack(a):
    bf = lambda t: t.astype(MXU)
    return jnp.concatenate([bf(a["w_out"][0]), bf(a["w_up"][0]).T, bf(a["w_down"][0])], axis=0)


def _late_unpack(g):
    return dict(w_out=g[:, :ROW_UP].reshape(D, D), w_upT=g[:, ROW_UP:ROW_DOWN].reshape(FF, D),
                w_down=g[:, ROW_DOWN:].reshape(FF, D))


def _own_slot(land, own):
    me = 4 * lax.axis_index("x") + 2 * lax.axis_index("y") + lax.axis_index("c")
    return lax.dynamic_update_slice_in_dim(land, own[None], me, axis=0)


def kernel(x, norm_mix_pre, w_in, conv_ssm_w, conv_ssm_b, dt_bias, a_log, d_skip, ssm_norm, conv_lru_w, conv_lru_b, lru_wa, lru_ba, lru_wx, lru_bx, lru_lambda, w_out, norm_mix_post, norm_mlp_pre, w_up, w_down, norm_mlp_post, loss_target, m_norm_mix_pre, m_w_in, m_conv_ssm_w, m_conv_ssm_b, m_dt_bias, m_a_log, m_d_skip, m_ssm_norm, m_conv_lru_w, m_conv_lru_b, m_lru_wa, m_lru_ba, m_lru_wx, m_lru_bx, m_lru_lambda, m_w_out, m_norm_mix_post, m_norm_mlp_pre, m_w_up, m_w_down, m_norm_mlp_post, v_norm_mix_pre, v_w_in, v_conv_ssm_w, v_conv_ssm_b, v_dt_bias, v_a_log, v_d_skip, v_ssm_norm, v_conv_lru_w, v_conv_lru_b, v_lru_wa, v_lru_ba, v_lru_wx, v_lru_bx, v_lru_lambda, v_w_out, v_norm_mix_post, v_norm_mlp_pre, v_w_up, v_w_down, v_norm_mlp_post):
    vals = (norm_mix_pre, w_in, conv_ssm_w, conv_ssm_b, dt_bias, a_log, d_skip, ssm_norm, conv_lru_w, conv_lru_b, lru_wa, lru_ba, lru_wx, lru_bx, lru_lambda, w_out, norm_mix_post, norm_mlp_pre, w_up, w_down, norm_mlp_post)
    m_vals = (m_norm_mix_pre, m_w_in, m_conv_ssm_w, m_conv_ssm_b, m_dt_bias, m_a_log, m_d_skip, m_ssm_norm, m_conv_lru_w, m_conv_lru_b, m_lru_wa, m_lru_ba, m_lru_wx, m_lru_bx, m_lru_lambda, m_w_out, m_norm_mix_post, m_norm_mlp_pre, m_w_up, m_w_down, m_norm_mlp_post)
    v_vals = (v_norm_mix_pre, v_w_in, v_conv_ssm_w, v_conv_ssm_b, v_dt_bias, v_a_log, v_d_skip, v_ssm_norm, v_conv_lru_w, v_conv_lru_b, v_lru_wa, v_lru_ba, v_lru_wx, v_lru_bx, v_lru_lambda, v_w_out, v_norm_mix_post, v_norm_mlp_pre, v_w_up, v_w_down, v_norm_mlp_post)
    w = dict(zip(WEIGHTS, vals))
    m = dict(zip(WEIGHTS, m_vals))
    v = dict(zip(WEIGHTS, v_vals))
    me = 4 * lax.axis_index("x") + 2 * lax.axis_index("y") + lax.axis_index("c")

    bf = lambda t: t.astype(MXU)
    late = _late_pack(w)
    early = _all_gather(_early_pack(w, me), "early_weights_all_gather")
    late, early = lax.optimization_barrier((late, early))
    lw = {}
    lw["sems"], lw["src"], lw["land"], token = _exchange_start(late, None, PLAN_GATHER_ICI, "late_weights_ici_start")
    w_in_pt, conv_w = _early_unpack(early)
    full = {n: (conv_w[n] if n in conv_w else w[n][0]) for n in WEIGHTS if n not in BIG}
    full["norm_mix_pre"] = full["norm_mix_pre"] + token[0, 0]

    def after_ssd(after):
        src, land = _exchange_wait(lw["sems"], lw["src"], lw["land"], after, PLAN_GATHER_ICI, "late_weights_ici_wait")
        lw["sems"], lw["src"], lw["land"], tok = _exchange_start(src, land, PLAN_GATHER_D2D, "late_weights_d2d_start")
        return tok[0, 0]

    def late_weights(after):
        src, land = _exchange_wait(lw["sems"], lw["src"], lw["land"], after, PLAN_GATHER_D2D, "late_weights_d2d_wait")
        return _late_unpack(_own_slot(land, src))

    sent = {}

    def send_mlp_grads(d_w_up_t, d_w_down, d_w_out):
        src = jnp.concatenate([bf(d_w_up_t).reshape(N_DEV, -1, D), bf(d_w_down).reshape(N_DEV, -1, D),
                               bf(d_w_out).reshape(N_DEV, -1, D)], axis=1)
        sent["sems"], sent["src"], sent["land"], tok = _exchange_start(src, None, PLAN_SCATTER, "mlp_grads_start")
        return tok[0, 0]

    def send_late_grads(d_w_in_pt):
        src = _scatter_w_in_grad(d_w_in_pt)
        sent["sems2"], sent["src2"], sent["land2"], tok = _exchange_start(src, None, PLAN_SCATTER, "late_grads_start")
        return tok[0, 0]

    loss, grad_x, g = _local_step(x[0], loss_target[0], _prep_params(full, dict(w_in_pT=w_in_pt)), after_ssd,
                                  late_weights, send_mlp_grads, send_late_grads)
    loss = lax.psum(loss, ("x", "y", "c"))

    own = lambda src: lax.dynamic_index_in_dim(src, me, keepdims=False)
    out_g, out_d, out_m, out_v = {}, {}, {}, {}
    mlp_src, mlp_land = _exchange_wait(sent["sems"], sent["src"], sent["land"], grad_x, PLAN_SCATTER, "mlp_grads_wait")
    g_mlp = _slot_sum(_own_slot(mlp_land, own(mlp_src)), "slot_sum_mlp")
    fs = FF // N_DEV
    for n, gn in (("w_up", g_mlp[:fs].T[None]), ("w_down", g_mlp[fs:2 * fs][None]), ("w_out", g_mlp[2 * fs:][None])):
        out_g[n] = gn
        out_d[n], out_m[n], out_v[n] = _adam_big(w[n], gn, m[n], v[n], "adam_" + n)
    zrow = jnp.zeros((1, D), F32)
    pad16 = lambda a: jnp.pad(a, ((0, 0), (0, 128 - NH)))
    small_parts = [
        jnp.concatenate([g["norm_mix_pre"], g["ssm_norm"], g["conv_lru_b"], g["lru_lambda"], g["norm_mix_post"],
                         g["norm_mlp_pre"], g["norm_mlp_post"], zrow], axis=0),
        g["conv_ssm_b"],
        jnp.concatenate([pad16(g["dt_bias"]), pad16(g["a_log"]), pad16(g["d_skip"]), jnp.zeros((5, 128), F32)], axis=0),
        g["conv_ssm_w"], g["conv_lru_w"],
        jnp.concatenate([g["lru_wa"].reshape(D, 64), g["lru_wx"].reshape(D, 64)], axis=0),
        jnp.concatenate([g["lru_ba"].reshape(NH, 64), g["lru_bx"].reshape(NH, 64)], axis=0)]
    small = _pad_rows(jnp.concatenate([s.reshape(-1) for s in small_parts]), 8)
    small, _ = lax.optimization_barrier((small, out_v["w_out"]))
    small_all = _all_gather(small, "small_grads_all_gather")

    lg_src, lg_land = _exchange_wait(sent["sems2"], sent["src2"], sent["land2"], small_all, PLAN_SCATTER, "late_grads_wait")
    g_late = _slot_sum(_own_slot(lg_land, own(lg_src)), "slot_sum_late")
    gt = lax.dynamic_slice(g_late, (2 * me, 0), (IN_ROWS, D))
    dt_, mt_, vt_ = _adam_big(w["w_in"][0].T, gt, m["w_in"][0].T, v["w_in"][0].T, "adam_w_in")
    out_g["w_in"], out_d["w_in"], out_m["w_in"], out_v["w_in"] = gt.T[None], dt_.T[None], mt_.T[None], vt_.T[None]
    sflat = small_all.reshape(N_DEV, -1)
    groups = []
    off = 0
    for r, c in SMALL_GROUPS:
        groups.append(sflat[:, off:off + r * c].reshape(N_DEV, r, c))
        off += r * c
    groups[3] = lax.dynamic_slice_in_dim(groups[3], me * CONV_SSM_COLS, CONV_SSM_COLS, axis=2)
    groups[4] = lax.dynamic_slice_in_dim(groups[4], me * CONV_LRU_COLS, CONV_LRU_COLS, axis=2)
    wmv = [(w[n].reshape(s), m[n].reshape(s), v[n].reshape(s)) for n, s, _, _ in SMALL]
    res = _adam_small(groups, [(gi, r0) for _, _, gi, r0 in SMALL], wmv)
    for (n, _, _, _), (g_n, d_n, m_n, v_n) in zip(SMALL, res):
        shape = w[n].shape
        out_g[n], out_d[n], out_m[n], out_v[n] = (g_n.reshape(shape), d_n.reshape(shape), m_n.reshape(shape),
                                                  v_n.reshape(shape))
    return (loss, grad_x[None], *[out_g[n] for n in WEIGHTS], *[out_d[n] for n in WEIGHTS],
            *[out_m[n] for n in WEIGHTS], *[out_v[n] for n in WEIGHTS])
```

```python
import functools

import jax
import jax.numpy as jnp
from jax import lax
from jax.experimental import pallas as pl
from jax.experimental.pallas import tpu as pltpu

F32 = jnp.float32
MXU = jnp.bfloat16
HI = lax.Precision.HIGHEST
EPS = 1e-6

D = 1024
NH = 16
NS = 128
CH = 128
FF = 4096
NP = 7168
SEG = 1024
LRU_C = 8.0
N_DEV = 8

ADAM_LR, ADAM_B1, ADAM_B2, ADAM_EPS, ADAM_WD, ADAM_STEP = 0.001, 0.9, 0.999, 1e-08, 0.01, 10

VMEM_LIMIT = 56 * 1024 * 1024


def _cp(*sem):
    return pltpu.CompilerParams(dimension_semantics=sem, vmem_limit_bytes=VMEM_LIMIT)


def _nn(a, b):
    return jnp.dot(a.astype(MXU), b.astype(MXU), preferred_element_type=F32)


def _nt(a, b):
    return lax.dot_general(a.astype(MXU), b.astype(MXU), (((1,), (1,)), ((), ())), preferred_element_type=F32)


def _tn(a, b):
    return lax.dot_general(a.astype(MXU), b.astype(MXU), (((0,), (0,)), ((), ())), preferred_element_type=F32)


_sigmoid = jax.nn.sigmoid


def _silu(x):
    return x * _sigmoid(x)


def _dsilu(x):
    s = _sigmoid(x)
    return s + x * s * (1.0 - s)


def _softplus(x):
    return jnp.maximum(x, 0.0) + jnp.log(1.0 + jnp.exp(-jnp.abs(x)))


def _rms(x, g):
    r = lax.rsqrt(jnp.mean(x * x, axis=-1, keepdims=True) + EPS)
    return x * r * g


def _rms_bwd(x, g, dy):
    r = lax.rsqrt(jnp.mean(x * x, axis=-1, keepdims=True) + EPS)
    gdy = g * dy
    dx = r * gdy - x * (r * r * r) * jnp.mean(x * gdy, axis=-1, keepdims=True)
    return dx, dy * x * r


def _rowsum(x):
    return jnp.sum(x, axis=0, keepdims=True)


def _taps_past(cur, prev8):
    r_n, c_n = cur.shape
    row = lax.broadcasted_iota(jnp.int32, (r_n, c_n), 0)
    out = []
    for k in range(4):
        s = 3 - k
        if s == 0:
            out.append(cur)
            continue
        head = jnp.concatenate([pltpu.roll(prev8, s, 0), jnp.zeros((r_n - 8, c_n), F32)], axis=0)
        out.append(jnp.where(row < s, head, pltpu.roll(cur, s, 0)))
    return out


def _taps_future(cur, fut8):
    r_n, c_n = cur.shape
    row = lax.broadcasted_iota(jnp.int32, (r_n, c_n), 0)
    out = []
    for k in range(4):
        s = 3 - k
        if s == 0:
            out.append(cur)
            continue
        tail = jnp.concatenate([jnp.zeros((r_n - 8, c_n), F32), pltpu.roll(fut8, 8 - s, 0)], axis=0)
        out.append(jnp.where(row >= r_n - s, tail, pltpu.roll(cur, r_n - s, 0)))
    return out


def _conv_apply(taps, w, b):
    acc = taps[0] * w[0:1, :]
    for k in range(1, 4):
        acc = acc + taps[k] * w[k:k + 1, :]
    return acc + b


def _inproj(x, g0, w):
    t_n = x.shape[0]
    tm = min(t_n, 1024)

    n_j = NP // SEG

    def body(x_ref, g_ref, w_ref, pb_ref, p6_ref, u_ref):
        j = pl.program_id(1)

        @pl.when(j == 0)
        def _():
            u_ref[...] = _rms(x_ref[...], g_ref[...]).astype(MXU)

        p = lax.dot_general(u_ref[...], w_ref[...], (((1,), (1,)), ((), ())), preferred_element_type=F32)

        @pl.when(j < n_j - 1)
        def _():
            pb_ref[...] = p.astype(MXU)

        @pl.when(j == n_j - 1)
        def _():
            p6_ref[...] = p

    pb, p6, u = pl.pallas_call(
        body, name="inproj", grid=(t_n // tm, n_j),
        in_specs=[pl.BlockSpec((tm, D), lambda i, j: (i, 0)), pl.BlockSpec((1, D), lambda i, j: (0, 0)),
                  pl.BlockSpec((SEG, D), lambda i, j: (j, 0))],
        out_specs=[pl.BlockSpec((tm, SEG), lambda i, j: (i, jnp.minimum(j, n_j - 2))),
                   pl.BlockSpec((tm, SEG), lambda i, j: (i, 0)), pl.BlockSpec((tm, D), lambda i, j: (i, 0))],
        out_shape=[jax.ShapeDtypeStruct((t_n, NP - SEG), MXU), jax.ShapeDtypeStruct((t_n, SEG), F32),
                   jax.ShapeDtypeStruct((t_n, D), MXU)],
        compiler_params=_cp("parallel", "arbitrary"),
    )(x, g0, w)
    return (pb, p6), u


def _ssd_prep(dtraw, dtb, alog):
    l_n = dtraw.shape[0]
    r = lax.broadcasted_iota(jnp.int32, (l_n, l_n), 0)
    c = lax.broadcasted_iota(jnp.int32, (l_n, l_n), 1)
    tril = (r >= c).astype(F32)
    triu = (r <= c).astype(F32)
    eye = (r == c).astype(F32)
    dt = _softplus(dtraw + dtb)
    adt = dt * (-jnp.exp(alog))
    ac = jnp.dot(tril, adt, preferred_element_type=F32, precision=HI)
    tn = (((0,), (0,)), ((), ()))
    ac_t = lax.dot_general(adt, triu, tn, preferred_element_type=F32, precision=HI)
    dt_t = lax.dot_general(dt, eye, tn, preferred_element_type=F32, precision=HI)
    return dt, dt_t, ac, ac_t, _rowsum(adt)


def _ssd_pair(j, xp, bg, cg, sp, dt, dt_t, ac, ac_t, aend):
    l_n = xp.shape[0]
    lane = lax.broadcasted_iota(jnp.int32, (l_n, 128), 1)
    sub = lax.broadcasted_iota(jnp.int32, (128, l_n), 0)
    lane1 = lax.broadcasted_iota(jnp.int32, (1, 128), 1)
    tri = lax.broadcasted_iota(jnp.int32, (l_n, l_n), 0) >= lax.broadcasted_iota(jnp.int32, (l_n, l_n), 1)
    lo = lax.broadcasted_iota(jnp.int32, (l_n, 128), 1) < 64
    lo_s = lax.broadcasted_iota(jnp.int32, (128, 128), 1) < 64
    cb = _nt(cg, bg)
    cs = _nn(cg, sp)
    x2 = jnp.concatenate([jnp.where(lo, xp, 0.0), jnp.where(lo, 0.0, xp)], axis=0)
    ws_, bs_, eo, ee = [], [], [], []
    for e in range(2):
        h = 2 * j + e
        ac_l = jnp.sum(jnp.where(lane == h, ac, 0.0), axis=1, keepdims=True)
        dt_l = jnp.sum(jnp.where(lane == h, dt, 0.0), axis=1, keepdims=True)
        a_end = jnp.sum(jnp.where(lane1 == h, aend, 0.0), axis=1, keepdims=True)
        ac_s, dt_s = ac_t[h:h + 1, :], dt_t[h:h + 1, :]
        decay = jnp.exp(jnp.where(tri, ac_l - ac_s, -1e30))
        ws_.append(cb * decay * dt_s)
        bs_.append(bg * (jnp.exp(a_end - ac_l) * dt_l))
        eo.append(jnp.exp(ac_l))
        ee.append(jnp.exp(a_end))
    y = _nn(jnp.concatenate(ws_, axis=1), x2) + jnp.where(lo, eo[0], eo[1]) * cs
    s_new = _tn(jnp.concatenate(bs_, axis=0), x2) + jnp.where(lo_s, ee[0], ee[1]) * sp
    return y, s_new


def _ssd_post(y, xs, z, dsk, nrm):
    y = (y + dsk * xs) * _silu(z)
    half = D // 2
    ya, yb = y[:, :half], y[:, half:]
    ya = ya * lax.rsqrt(jnp.mean(ya * ya, axis=-1, keepdims=True) + EPS)
    yb = yb * lax.rsqrt(jnp.mean(yb * yb, axis=-1, keepdims=True) + EPS)
    return jnp.concatenate([ya, yb], axis=1) * nrm


LAST_SEG = NP // SEG - 1


def _proj_ops(proj, seg_ids):
    return [proj[1] if s == LAST_SEG else proj[0] for s in seg_ids]


def _proj_specs(rows, seg_ids, order):
    return [pl.BlockSpec((rows, SEG), functools.partial(lambda i, c: (order(i), c), c=0 if s == LAST_SEG else s))
            for s in seg_ids]


def _prev_specs(rows, seg_ids, order):
    specs = []
    for s in seg_ids:
        n, c = (8, 0) if s == LAST_SEG else (16, s)
        specs.append(pl.BlockSpec((n, SEG), functools.partial(
            lambda i, n, c: (jnp.maximum(order(i) * (rows // n) - 1, 0), c), n=n, c=c)))
    return specs


def _prev8(ref):
    return ref[...] if ref.shape[0] == 8 else ref[8:16, :].astype(F32)


def _full(shape):
    return pl.BlockSpec(shape, lambda i: (0,) * len(shape))


def _ssd_fwd(proj, cwx, cwb, cbx, cbb, dtb, alog, dsk, nrm):
    t_n = proj[0].shape[0]
    n_c = t_n // CH
    fwd = lambda i: i

    def body(z_ref, xs_ref, bc_ref, xsp_ref, bcp_ref, cwx_ref, cwb_ref, cbx_ref, cbb_ref, dtb_ref, alog_ref,
             dsk_ref, nrm_ref, ya_ref, sprev_ref, yraw_ref, xspre_ref, bcpre_ref, s_ref):
        c = pl.program_id(0)

        @pl.when(c == 0)
        def _():
            s_ref[...] = jnp.zeros_like(s_ref)

        keep = jnp.where(c == 0, 0.0, 1.0)
        xs_pre = _conv_apply(_taps_past(xs_ref[...].astype(F32), _prev8(xsp_ref) * keep), cwx_ref[...], cbx_ref[...])
        bc_pre = _conv_apply(_taps_past(bc_ref[:, :512], bcp_ref[:, :512] * keep), cwb_ref[...], cbb_ref[...])
        xspre_ref[...] = xs_pre
        bcpre_ref[...] = bc_pre
        prep = _ssd_prep(bc_ref[:, 512:640], dtb_ref[...], alog_ref[...])
        xs = _silu(xs_pre)
        bc = _silu(bc_pre)
        sprev_ref[0] = s_ref[...]
        ys = []
        for j in range(NH // 2):
            g = j // 4
            yp, sn = _ssd_pair(j, xs[:, 128 * j:128 * j + 128], bc[:, 128 * g:128 * g + 128],
                               bc[:, 256 + 128 * g:384 + 128 * g], s_ref[:, 128 * j:128 * j + 128], *prep)
            ys.append(yp)
            s_ref[:, 128 * j:128 * j + 128] = sn
        y = jnp.concatenate(ys, axis=1)
        yraw_ref[...] = y
        ya_ref[...] = _ssd_post(y, xs, z_ref[...].astype(F32), dsk_ref[...], nrm_ref[...]).astype(ya_ref.dtype)

    return pl.pallas_call(
        body, name="ssd_fwd", grid=(n_c,),
        in_specs=_proj_specs(CH, (0, 5, 6), fwd) + _prev_specs(CH, (5, 6), fwd) + [
            _full((4, D)), _full((4, 512)), _full((1, D)), _full((1, 512)), _full((1, 128)), _full((1, 128)),
            _full((1, D)), _full((1, D))],
        out_specs=[pl.BlockSpec((CH, D), lambda i: (i, 0)), pl.BlockSpec((1, NS, D), lambda i: (i, 0, 0)),
                   pl.BlockSpec((CH, D), lambda i: (i, 0)), pl.BlockSpec((CH, D), lambda i: (i, 0)),
                   pl.BlockSpec((CH, 512), lambda i: (i, 0))],
        out_shape=[jax.ShapeDtypeStruct((t_n, D), MXU), jax.ShapeDtypeStruct((n_c, NS, D), F32),
                   jax.ShapeDtypeStruct((t_n, D), F32), jax.ShapeDtypeStruct((t_n, D), F32),
                   jax.ShapeDtypeStruct((t_n, 512), F32)],
        scratch_shapes=[pltpu.VMEM((NS, D), F32)],
        compiler_params=_cp("arbitrary"),
    )(*_proj_ops(proj, (0, 5, 6, 5, 6)), cwx, cwb, cbx, cbb, dtb, alog, dsk, nrm)


def _ssd_bwd(dya, saved, proj, sprev, cwx, cwb, dtb, alog, dsk, nrm, mlp_ops):
    t_n = proj[0].shape[0]
    n_c = t_n // CH
    rev = lambda i: n_c - 1 - i
    fb = FF // n_c

    def body(dya_ref, yraw_ref, xspre_ref, bcpre_ref, z_ref, xs_ref, bc_ref, sprev_ref, cwx_ref, cwb_ref,
             dtb_ref, alog_ref, dsk_ref, nrm_ref, hid_ref, dff_ref, dhp_ref, v_ref,
             dz_ref, dxs_ref, dbc_ref, dcwx_ref, dcwb_ref, dcbx_ref, dcbb_ref, ddtb_ref, dalog_ref, ddsk_ref,
             dnrm_ref, dwd_ref, dwu_ref, ds_ref, futx_ref, futb_ref):
        i = pl.program_id(0)
        acc_refs = (dcwx_ref, dcwb_ref, dcbx_ref, dcbb_ref, ddtb_ref, dalog_ref, ddsk_ref, dnrm_ref)
        tn = (((0,), (0,)), ((), ()))
        kt = t_n // (NH // 2)
        dwd_acc = jnp.zeros((fb, D), F32)
        dwu_acc = jnp.zeros((fb, D), F32)

        @pl.when(i == 0)
        def _():
            for r in (ds_ref, futx_ref, futb_ref) + acc_refs:
                r[...] = jnp.zeros_like(r)

        xs_pre = xspre_ref[...]
        bc_pre = bcpre_ref[...]
        xs = _silu(xs_pre)
        bc = _silu(bc_pre)
        prep, prep_vjp = jax.vjp(_ssd_prep, bc_ref[:, 512:640], dtb_ref[...], alog_ref[...])
        s_in = sprev_ref[0]

        def pair_args(j):
            g = j // 4
            return (xs[:, 128 * j:128 * j + 128], bc[:, 128 * g:128 * g + 128],
                    bc[:, 256 + 128 * g:384 + 128 * g], s_in[:, 128 * j:128 * j + 128]) + tuple(prep)

        _, post_vjp = jax.vjp(_ssd_post, yraw_ref[...], xs, z_ref[...].astype(F32), dsk_ref[...], nrm_ref[...])
        dy, dxs_skip, dz, ddsk, dnrm = post_vjp(dya_ref[...])
        dz_ref[...] = dz.astype(dz_ref.dtype)
        ddsk_ref[...] += ddsk
        dnrm_ref[...] += dnrm

        dprep = [jnp.zeros_like(p) for p in prep]
        dxp = []
        dbg = [jnp.zeros((CH, 128), F32), jnp.zeros((CH, 128), F32)]
        dcg = [jnp.zeros((CH, 128), F32), jnp.zeros((CH, 128), F32)]
        for j in range(NH // 2):
            g = j // 4
            _, pair_vjp = jax.vjp(functools.partial(_ssd_pair, j), *pair_args(j))
            cts = pair_vjp((dy[:, 128 * j:128 * j + 128], ds_ref[:, 128 * j:128 * j + 128]))
            dxp.append(cts[0])
            dbg[g] = dbg[g] + cts[1]
            dcg[g] = dcg[g] + cts[2]
            ds_ref[:, 128 * j:128 * j + 128] = cts[3]
            dprep = [a + b for a, b in zip(dprep, cts[4:])]
            rows = pl.ds(kt * j, kt)
            dwd_acc = dwd_acc + lax.dot_general(hid_ref[rows, :], dff_ref[rows, :], tn, preferred_element_type=F32)
            dwu_acc = dwu_acc + lax.dot_general(dhp_ref[rows, :], v_ref[rows, :], tn, preferred_element_type=F32)
        dwd_ref[...] = dwd_acc.astype(MXU)
        dwu_ref[...] = dwu_acc.astype(MXU)
        ddtraw, ddtb, dalog = prep_vjp(tuple(dprep))
        ddtb_ref[...] += ddtb
        dalog_ref[...] += dalog

        dxs_pre = (dxs_skip + jnp.concatenate(dxp, axis=1)) * _dsilu(xs_pre)
        dbc_pre = jnp.concatenate([dbg[0], dbg[1], dcg[0], dcg[1]], axis=1) * _dsilu(bc_pre)
        dcbx_ref[...] += _rowsum(dxs_pre)
        dcbb_ref[...] += _rowsum(dbc_pre)
        fx = _taps_future(dxs_pre, futx_ref[...])
        fbc = _taps_future(dbc_pre, futb_ref[...])
        xs_in = xs_ref[...].astype(F32)
        bc_in = bc_ref[:, :512]
        for k in range(4):
            dcwx_ref[k:k + 1, :] += _rowsum(fx[k] * xs_in)
            dcwb_ref[k:k + 1, :] += _rowsum(fbc[k] * bc_in)
        cwx = cwx_ref[...]
        cwb = cwb_ref[...]
        dxs_in = fx[0] * cwx[0:1, :]
        dbc_in = fbc[0] * cwb[0:1, :]
        for k in range(1, 4):
            dxs_in = dxs_in + fx[k] * cwx[k:k + 1, :]
            dbc_in = dbc_in + fbc[k] * cwb[k:k + 1, :]
        futx_ref[...] = dxs_pre[0:8, :]
        futb_ref[...] = dbc_pre[0:8, :]
        dxs_ref[...] = dxs_in.astype(dxs_ref.dtype)
        dbc_ref[...] = jnp.concatenate([dbc_in, ddtraw, jnp.zeros((CH, SEG - 640), F32)], axis=1).astype(dbc_ref.dtype)

    row_out = lambda: pl.BlockSpec((CH, D), lambda i: (rev(i), 0))
    outs = pl.pallas_call(
        body, name="ssd_bwd", grid=(n_c,),
        in_specs=[row_out(), row_out(), row_out(), pl.BlockSpec((CH, 512), lambda i: (rev(i), 0))]
        + _proj_specs(CH, (0, 5, 6), rev) + [pl.BlockSpec((1, NS, D), lambda i: (rev(i), 0, 0)),
                                             _full((4, D)), _full((4, 512)),
                                             _full((1, 128)), _full((1, 128)), _full((1, D)), _full((1, D)),
                                             pl.BlockSpec((t_n, fb), lambda i: (0, i)), _full((t_n, D)),
                                             pl.BlockSpec((t_n, fb), lambda i: (0, i)), _full((t_n, D))],
        out_specs=[row_out(), row_out(), row_out(), _full((4, D)), _full((4, 512)), _full((1, D)), _full((1, 512)),
                   _full((1, 128)), _full((1, 128)), _full((1, D)), _full((1, D)),
                   pl.BlockSpec((fb, D), lambda i: (i, 0)), pl.BlockSpec((fb, D), lambda i: (i, 0))],
        out_shape=[jax.ShapeDtypeStruct((t_n, D), MXU)] * 3 + [
            jax.ShapeDtypeStruct(s, F32) for s in ((4, D), (4, 512), (1, D), (1, 512), (1, 128), (1, 128), (1, D), (1, D))]
        + [jax.ShapeDtypeStruct((FF, D), MXU)] * 2,
        scratch_shapes=[pltpu.VMEM((NS, D), F32), pltpu.VMEM((8, D), F32), pltpu.VMEM((8, 512), F32)],
        compiler_params=_cp("arbitrary"),
    )(dya, *saved, *_proj_ops(proj, (0, 5, 6)), sprev, cwx, cwb, dtb, alog, dsk, nrm, *mlp_ops)
    return outs


LRU_ROWS = 256
LRU_BLK = 256


def _lru_gates(xr, wa, wx, ba, bx, lam):
    pr = jnp.concatenate([_nn(xr[:, LRU_BLK * b:LRU_BLK * (b + 1)], wa[b]) for b in range(D // LRU_BLK)], axis=1) + ba
    pi = jnp.concatenate([_nn(xr[:, LRU_BLK * b:LRU_BLK * (b + 1)], wx[b]) for b in range(D // LRU_BLK)], axis=1) + bx
    log_a = -LRU_C * _sigmoid(pr) * _softplus(-lam)
    a = jnp.exp(log_a)
    mult = jnp.sqrt(1.0 - jnp.exp(2.0 * log_a))
    return a, mult * (_sigmoid(pi) * xr)


def _lru_out(h, g):
    return h * jax.nn.gelu(g, approximate=True)


def _lru_fwd(proj, cw, cb, wa, wx, ba, bx, lam):
    t_n = proj[0].shape[0]
    rows = min(LRU_ROWS, t_n)
    fwd = lambda i: i

    def body(g_ref, x_ref, xp_ref, cw_ref, cb_ref, wa_ref, wx_ref, ba_ref, bx_ref, lam_ref, yb_ref, h_ref, xr_ref,
             a_s, u_s, carry):
        i = pl.program_id(0)

        @pl.when(i == 0)
        def _():
            carry[...] = jnp.zeros_like(carry)

        keep = jnp.where(i == 0, 0.0, 1.0)
        xr = _conv_apply(_taps_past(x_ref[...].astype(F32), _prev8(xp_ref) * keep), cw_ref[...], cb_ref[...])
        xr_ref[...] = xr
        a, u = _lru_gates(xr, wa_ref[...], wx_ref[...], ba_ref[...], bx_ref[...], lam_ref[...])
        a_s[...] = a
        u_s[...] = u
        row = lax.broadcasted_iota(jnp.int32, (8, D), 0)

        def blk(b, c):
            s = pl.multiple_of(b * 8, 8)
            av = a_s[pl.ds(s, 8), :]
            uv = u_s[pl.ds(s, 8), :]
            for d in (1, 2, 4):
                m = row >= d
                uv = uv + av * jnp.where(m, pltpu.roll(uv, d, 0), 0.0)
                av = av * jnp.where(m, pltpu.roll(av, d, 0), 1.0)
            hv = uv + av * c
            h_ref[pl.ds(s, 8), :] = hv
            return hv[7:8, :]

        carry[0:1, :] = lax.fori_loop(0, rows // 8, blk, carry[0:1, :])
        yb_ref[...] = _lru_out(h_ref[...], g_ref[...].astype(F32)).astype(yb_ref.dtype)

    return pl.pallas_call(
        body, name="lru_fwd", grid=(t_n // rows,),
        in_specs=_proj_specs(rows, (1, 2), fwd) + _prev_specs(rows, (2,), fwd) + [
            _full((4, D)), _full((1, D)), _full((4, LRU_BLK, LRU_BLK)), _full((4, LRU_BLK, LRU_BLK)),
            _full((1, D)), _full((1, D)), _full((1, D))],
        out_specs=[pl.BlockSpec((rows, D), lambda i: (i, 0))] * 3,
        out_shape=[jax.ShapeDtypeStruct((t_n, D), MXU), jax.ShapeDtypeStruct((t_n, D), F32),
                   jax.ShapeDtypeStruct((t_n, D), F32)],
        scratch_shapes=[pltpu.VMEM((rows, D), F32), pltpu.VMEM((rows, D), F32), pltpu.VMEM((8, D), F32)],
        compiler_params=_cp("arbitrary"),
    )(*_proj_ops(proj, (1, 2, 2)), cw, cb, wa, wx, ba, bx, lam)


def _lru_bwd(dyb, proj, h, xr_saved, cw, wa, wx, ba, bx, lam):
    t_n = proj[0].shape[0]
    rows = min(LRU_ROWS, t_n)
    n_t = t_n // rows
    rev = lambda i: n_t - 1 - i
    rb = rows // 8

    def body(dyb_ref, g_ref, x_ref, h_ref, hp_ref, xr_ref, cw_ref, wa_ref, wx_ref, ba_ref, bx_ref, lam_ref,
             dg_ref, dx_ref, dcw_ref, dcb_ref, dwa_ref, dwx_ref, dba_ref, dbx_ref, dlam_ref,
             a_s, dh_s, hx_s, da_s, du_s, carry, fut):
        i = pl.program_id(0)
        acc_refs = (dcw_ref, dcb_ref, dwa_ref, dwx_ref, dba_ref, dbx_ref, dlam_ref)

        @pl.when(i == 0)
        def _():
            for r in (carry, fut) + acc_refs:
                r[...] = jnp.zeros_like(r)

        keep = jnp.where(i == n_t - 1, 0.0, 1.0)
        gate_in = (xr_ref[...], wa_ref[...], wx_ref[...], ba_ref[...], bx_ref[...], lam_ref[...])
        (a, _), gates_vjp = jax.vjp(_lru_gates, *gate_in)
        _, out_vjp = jax.vjp(_lru_out, h_ref[...], g_ref[...].astype(F32))
        dh, dg = out_vjp(dyb_ref[...])
        dg_ref[...] = dg.astype(dg_ref.dtype)
        a_s[...] = a
        dh_s[...] = dh
        hx_s[0:8, :] = hp_ref[...] * keep
        hx_s[8:, :] = h_ref[...]
        row = lax.broadcasted_iota(jnp.int32, (8, D), 0)

        def blk(b, c):
            s = pl.multiple_of((rb - 1 - b) * 8, 8)
            av = a_s[pl.ds(s, 8), :]
            dhv = dh_s[pl.ds(s, 8), :]
            kv = av * dhv
            for d in (1, 2, 4):
                m = row <= 7 - d
                kv = kv + av * jnp.where(m, pltpu.roll(kv, 8 - d, 0), 0.0)
                av = av * jnp.where(m, pltpu.roll(av, 8 - d, 0), 1.0)
            kv = kv + av * c
            gv = dhv + jnp.where(row < 7, pltpu.roll(kv, 7, 0), c)
            hb = hx_s[pl.ds(s + 8, 8), :]
            hpv = hx_s[pl.ds(s, 8), :]
            hprev = jnp.where(row >= 1, pltpu.roll(hb, 1, 0), hpv[7:8, :])
            du_s[pl.ds(s, 8), :] = gv
            da_s[pl.ds(s, 8), :] = gv * hprev
            return kv[0:1, :]

        carry[0:1, :] = lax.fori_loop(0, rb, blk, carry[0:1, :])
        dxr, dwa, dwx, dba, dbx, dlam = gates_vjp((da_s[...], du_s[...]))
        dwa_ref[...] += dwa
        dwx_ref[...] += dwx
        dba_ref[...] += dba
        dbx_ref[...] += dbx
        dlam_ref[...] += dlam
        dcb_ref[...] += _rowsum(dxr)
        ft = _taps_future(dxr, fut[...])
        x_in = x_ref[...].astype(F32)
        for k in range(4):
            dcw_ref[k:k + 1, :] += _rowsum(ft[k] * x_in)
        cwv = cw_ref[...]
        dx = ft[0] * cwv[0:1, :]
        for k in range(1, 4):
            dx = dx + ft[k] * cwv[k:k + 1, :]
        fut[...] = dxr[0:8, :]
        dx_ref[...] = dx.astype(dx_ref.dtype)

    row_in = lambda: pl.BlockSpec((rows, D), lambda i: (rev(i), 0))
    prev_h = pl.BlockSpec((8, D), lambda i: (jnp.maximum(rev(i) * rb - 1, 0), 0))
    wspec = lambda: _full((4, LRU_BLK, LRU_BLK))
    return pl.pallas_call(
        body, name="lru_bwd", grid=(n_t,),
        in_specs=[row_in()] + _proj_specs(rows, (1, 2), rev) + [row_in(), prev_h, row_in()] + [
            _full((4, D)), wspec(), wspec(), _full((1, D)), _full((1, D)), _full((1, D))],
        out_specs=[row_in(), row_in(), _full((4, D)), _full((1, D)), wspec(), wspec(), _full((1, D)), _full((1, D)),
                   _full((1, D))],
        out_shape=[jax.ShapeDtypeStruct((t_n, D), MXU)] * 2 + [
            jax.ShapeDtypeStruct(s, F32) for s in ((4, D), (1, D), (4, LRU_BLK, LRU_BLK), (4, LRU_BLK, LRU_BLK),
                                                   (1, D), (1, D), (1, D))],
        scratch_shapes=[pltpu.VMEM((rows, D), F32), pltpu.VMEM((rows, D), F32), pltpu.VMEM((rows + 8, D), F32),
                        pltpu.VMEM((rows, D), F32), pltpu.VMEM((rows, D), F32), pltpu.VMEM((8, D), F32),
                        pltpu.VMEM((8, D), F32)],
        compiler_params=_cp("arbitrary"),
    )(dyb, *_proj_ops(proj, (1, 2)), h, h, xr_saved, cw, wa, wx, ba, bx, lam)


def _merge_out(ya, yb, proj, x, wout, g1):
    t_n = x.shape[0]
    tm = min(t_n, 512)

    def body(ya_ref, yb_ref, ga_ref, gb_ref, x_ref, w_ref, g_ref, h1_ref, mix_ref, mg_ref):
        merged = (_sigmoid(ga_ref[...].astype(F32)) * ya_ref[...].astype(F32)
                  + _sigmoid(gb_ref[...].astype(F32)) * yb_ref[...].astype(F32))
        mg = merged.astype(MXU)
        mg_ref[...] = mg
        mix = jnp.dot(mg, w_ref[...], preferred_element_type=F32)
        mix_ref[...] = mix
        h1_ref[...] = x_ref[...] + _rms(mix, g_ref[...])

    row = lambda: pl.BlockSpec((tm, D), lambda i: (i, 0))
    return pl.pallas_call(
        body, name="merge_out", grid=(t_n // tm,),
        in_specs=[row(), row()] + _proj_specs(tm, (3, 4), lambda i: i) + [row(), _full((D, D)), _full((1, D))],
        out_specs=[row(), row(), row()],
        out_shape=[jax.ShapeDtypeStruct((t_n, D), F32), jax.ShapeDtypeStruct((t_n, D), F32),
                   jax.ShapeDtypeStruct((t_n, D), MXU)],
        compiler_params=_cp("parallel"),
    )(ya, yb, *_proj_ops(proj, (3, 4)), x, wout, g1)


def _out_bwd(dv, h1, g2, dout, mix, ya, yb, proj, wout, g1):
    t_n = dv.shape[0]
    tm = min(t_n, 256)

    def body(dv_ref, h1_ref, g2_ref, dout_ref, mix_ref, ya_ref, yb_ref, ga_ref, gb_ref, w_ref, g_ref,
             dh1_ref, dmix_ref, dya_ref, dyb_ref, dga_ref, dgb_ref, dg2_ref, dg1_ref):
        @pl.when(pl.program_id(0) == 0)
        def _():
            dg1_ref[...] = jnp.zeros_like(dg1_ref)
            dg2_ref[...] = jnp.zeros_like(dg2_ref)

        dx, dg_rows = _rms_bwd(h1_ref[...], g2_ref[...], dv_ref[...])
        dg2_ref[...] += _rowsum(dg_rows)
        dh1 = dout_ref[...] + dx
        dh1_ref[...] = dh1
        dmix, dg_rows = _rms_bwd(mix_ref[...], g_ref[...], dh1)
        dg1_ref[...] += _rowsum(dg_rows)
        dmix_b = dmix.astype(MXU)
        dmix_ref[...] = dmix_b
        dmg = lax.dot_general(dmix_b, w_ref[...], (((1,), (1,)), ((), ())), preferred_element_type=F32)
        sa = _sigmoid(ga_ref[...].astype(F32))
        sb = _sigmoid(gb_ref[...].astype(F32))
        dya_ref[...] = dmg * sa
        dyb_ref[...] = dmg * sb
        dga_ref[...] = (dmg * ya_ref[...].astype(F32) * sa * (1.0 - sa)).astype(MXU)
        dgb_ref[...] = (dmg * yb_ref[...].astype(F32) * sb * (1.0 - sb)).astype(MXU)

    row = lambda: pl.BlockSpec((tm, D), lambda i: (i, 0))
    vec = lambda: _full((1, D))
    return pl.pallas_call(
        body, name="out_bwd", grid=(t_n // tm,),
        in_specs=[row(), row(), vec(), row(), row(), row(), row()] + _proj_specs(tm, (3, 4), lambda i: i)
        + [_full((D, D)), vec()],
        out_specs=[row(), row(), row(), row(), row(), row(), vec(), vec()],
        out_shape=[jax.ShapeDtypeStruct((t_n, D), F32), jax.ShapeDtypeStruct((t_n, D), MXU),
                   jax.ShapeDtypeStruct((t_n, D), F32), jax.ShapeDtypeStruct((t_n, D), F32),
                   jax.ShapeDtypeStruct((t_n, D), MXU), jax.ShapeDtypeStruct((t_n, D), MXU),
                   jax.ShapeDtypeStruct((1, D), F32), jax.ShapeDtypeStruct((1, D), F32)],
        compiler_params=_cp("arbitrary"),
    )(dv, h1, g2, dout, mix, ya, yb, *_proj_ops(proj, (3, 4)), wout, g1)


MLP_TM = 1024
MLP_TF_FWD = 1024
MLP_TF_BWD = 1024


def _mlp_fwd(h1, g2, wup, wdown, g3, tgt):
    t_n = h1.shape[0]
    tm = min(t_n, MLP_TM)
    n_f = FF // MLP_TF_FWD

    def body(h1_ref, g2_ref, wu_ref, wd_ref, g3_ref, tgt_ref, hp_ref, v_ref, dout_ref, dff_ref, loss_ref, dg3_ref, acc):
        i, j = pl.program_id(0), pl.program_id(1)

        @pl.when((i == 0) & (j == 0))
        def _():
            loss_ref[...] = jnp.zeros_like(loss_ref)
            dg3_ref[...] = jnp.zeros_like(dg3_ref)

        @pl.when(j == 0)
        def _():
            v_ref[...] = _rms(h1_ref[...], g2_ref[...]).astype(MXU)
            acc[...] = jnp.zeros_like(acc)

        hp = lax.dot_general(v_ref[...], wu_ref[...], (((1,), (1,)), ((), ())), preferred_element_type=F32)
        hp_ref[...] = hp.astype(MXU)
        hid = jnp.square(jnp.maximum(hp, 0.0))
        acc[...] += jnp.dot(hid.astype(MXU), wd_ref[...], preferred_element_type=F32)

        @pl.when(j == n_f - 1)
        def _():
            ff = acc[...]
            err = h1_ref[...] + _rms(ff, g3_ref[...]) - tgt_ref[...]
            loss_ref[...] += 0.5 * jnp.sum(jnp.mean(err * err, axis=-1, keepdims=True), axis=0, keepdims=True)
            dout = err * (1.0 / D)
            dout_ref[...] = dout
            dff, dg_rows = _rms_bwd(ff, g3_ref[...], dout)
            dg3_ref[...] += _rowsum(dg_rows)
            dff_ref[...] = dff.astype(MXU)

    row = lambda: pl.BlockSpec((tm, D), lambda i, j: (i, 0))
    vec = lambda: pl.BlockSpec((1, D), lambda i, j: (0, 0))
    return pl.pallas_call(
        body, name="mlp_fwd", grid=(t_n // tm, n_f),
        in_specs=[row(), vec(), pl.BlockSpec((MLP_TF_FWD, D), lambda i, j: (j, 0)),
                  pl.BlockSpec((MLP_TF_FWD, D), lambda i, j: (j, 0)), vec(), row()],
        out_specs=[pl.BlockSpec((tm, MLP_TF_FWD), lambda i, j: (i, j)), row(), row(), row(),
                   pl.BlockSpec((1, 1), lambda i, j: (0, 0)), vec()],
        out_shape=[jax.ShapeDtypeStruct((t_n, FF), MXU), jax.ShapeDtypeStruct((t_n, D), MXU),
                   jax.ShapeDtypeStruct((t_n, D), F32), jax.ShapeDtypeStruct((t_n, D), MXU),
                   jax.ShapeDtypeStruct((1, 1), F32), jax.ShapeDtypeStruct((1, D), F32)],
        scratch_shapes=[pltpu.VMEM((tm, D), F32)],
        compiler_params=_cp("arbitrary", "arbitrary"),
    )(h1, g2, wup, wdown, g3, tgt)


def _mlp_bwd(dff, hp, wup, wdown):
    t_n = dff.shape[0]
    tm = min(t_n, MLP_TM)
    n_f = FF // MLP_TF_BWD

    def mm_body(dff_ref, hp_ref, wu_ref, wd_ref, dv_ref, dhp_ref, hid_ref):
        @pl.when(pl.program_id(1) == 0)
        def _():
            dv_ref[...] = jnp.zeros_like(dv_ref)

        relu = jnp.maximum(hp_ref[...].astype(F32), 0.0)
        hid_ref[...] = jnp.square(relu).astype(MXU)
        dhid = lax.dot_general(dff_ref[...], wd_ref[...], (((1,), (1,)), ((), ())), preferred_element_type=F32)
        dhp = (dhid * (2.0 * relu)).astype(MXU)
        dhp_ref[...] = dhp
        dv_ref[...] += jnp.dot(dhp, wu_ref[...], preferred_element_type=F32)

    row = lambda: pl.BlockSpec((tm, D), lambda i, j: (i, 0))
    blk = lambda: pl.BlockSpec((tm, MLP_TF_BWD), lambda i, j: (i, j))
    wblk = lambda: pl.BlockSpec((MLP_TF_BWD, D), lambda i, j: (j, 0))
    return pl.pallas_call(
        mm_body, name="mlp_bwd", grid=(t_n // tm, n_f),
        in_specs=[row(), blk(), wblk(), wblk()], out_specs=[row(), blk(), blk()],
        out_shape=[jax.ShapeDtypeStruct((t_n, D), F32), jax.ShapeDtypeStruct((t_n, FF), MXU),
                   jax.ShapeDtypeStruct((t_n, FF), MXU)],
        compiler_params=_cp("parallel", "arbitrary"),
    )(dff, hp, wup, wdown)


def _wgrad(a, g, name):
    t_n, k_n = a.shape
    n_n = g.shape[1]
    tt = min(t_n, 1024)
    tk, tn = min(k_n, 1024), min(n_n, 1024)

    n_t = t_n // tt

    def body(a_ref, g_ref, o_ref, acc):
        t = pl.program_id(2)

        @pl.when(t == 0)
        def _():
            acc[...] = jnp.zeros_like(acc)

        acc[...] += lax.dot_general(a_ref[...], g_ref[...], (((0,), (0,)), ((), ())), preferred_element_type=F32)

        @pl.when(t == n_t - 1)
        def _():
            o_ref[...] = acc[...].astype(o_ref.dtype)

    return pl.pallas_call(
        body, name=name, grid=(k_n // tk, n_n // tn, n_t),
        in_specs=[pl.BlockSpec((tt, tk), lambda k, n, t: (t, k)), pl.BlockSpec((tt, tn), lambda k, n, t: (t, n))],
        out_specs=pl.BlockSpec((tk, tn), lambda k, n, t: (k, n)),
        out_shape=jax.ShapeDtypeStruct((k_n, n_n), MXU),
        scratch_shapes=[pltpu.VMEM((tk, tn), F32)],
        compiler_params=_cp("parallel", "parallel", "arbitrary"),
    )(a, g)


def _wgrad_segs(segs, g, name):
    t_n, n_n = g.shape
    n_s = len(segs)
    tt = min(t_n, 1024)
    n_t = t_n // tt

    def body(*refs):
        a_refs = refs[:n_s]
        g_ref, o_ref, acc = refs[n_s:]
        s_id, t = pl.program_id(0), pl.program_id(1)

        @pl.when(t == 0)
        def _():
            acc[...] = jnp.zeros_like(acc)

        for s in range(n_s):
            @pl.when(s_id == s)
            def _(s=s):
                acc[...] += lax.dot_general(a_refs[s][...], g_ref[...], (((0,), (0,)), ((), ())),
                                            preferred_element_type=F32)

        @pl.when(t == n_t - 1)
        def _():
            o_ref[...] = acc[...].astype(o_ref.dtype)

    seg_spec = lambda s: pl.BlockSpec((tt, SEG), lambda i, t: (jnp.where(i == s, t, jnp.where(i < s, 0, n_t - 1)), 0))
    return pl.pallas_call(
        body, name=name, grid=(n_s, n_t),
        in_specs=[seg_spec(s) for s in range(n_s)] + [pl.BlockSpec((tt, n_n), lambda i, t: (t, 0))],
        out_specs=pl.BlockSpec((SEG, n_n), lambda i, t: (i, 0)),
        out_shape=jax.ShapeDtypeStruct((n_s * SEG, n_n), MXU),
        scratch_shapes=[pltpu.VMEM((SEG, n_n), F32)],
        compiler_params=_cp("arbitrary", "arbitrary"),
    )(*segs, g)


def _inproj_bwd(dsegs, w, x, g0, dh1):
    t_n = x.shape[0]
    tm = min(t_n, 1024)
    n_k = NP // SEG

    def mm_body(*refs):
        dp_refs = refs[:n_k]
        w_ref, du_ref = refs[n_k:]
        k = pl.program_id(1)

        @pl.when(k == 0)
        def _():
            du_ref[...] = jnp.zeros_like(du_ref)

        for s in range(n_k):
            @pl.when(k == s)
            def _(s=s):
                du_ref[...] += jnp.dot(dp_refs[s][...], w_ref[...], preferred_element_type=F32)

    du = pl.pallas_call(
        mm_body, name="inproj_bwd", grid=(t_n // tm, n_k),
        in_specs=[pl.BlockSpec((tm, SEG), functools.partial(
            lambda i, k, s: (jnp.where(k >= s, i, jnp.maximum(i - 1, 0)), 0), s=s)) for s in range(n_k)] + [
            pl.BlockSpec((SEG, D), lambda i, k: (k, 0))],
        out_specs=pl.BlockSpec((tm, D), lambda i, k: (i, 0)),
        out_shape=jax.ShapeDtypeStruct((t_n, D), F32),
        compiler_params=_cp("parallel", "arbitrary"),
    )(*dsegs, w)

    tf = min(t_n, 512)

    def fin_body(du_ref, x_ref, g_ref, dh1_ref, dx_ref, dg0_ref):
        @pl.when(pl.program_id(0) == 0)
        def _():
            dg0_ref[...] = jnp.zeros_like(dg0_ref)

        dx, dg_rows = _rms_bwd(x_ref[...], g_ref[...], du_ref[...])
        dg0_ref[...] += _rowsum(dg_rows)
        dx_ref[...] = dh1_ref[...] + dx

    row = lambda: pl.BlockSpec((tf, D), lambda i: (i, 0))
    return pl.pallas_call(
        fin_body, name="grad_x", grid=(t_n // tf,), in_specs=[row(), row(), _full((1, D)), row()],
        out_specs=[row(), _full((1, D))],
        out_shape=[jax.ShapeDtypeStruct((t_n, D), F32), jax.ShapeDtypeStruct((1, D), F32)],
        compiler_params=_cp("arbitrary"),
    )(du, x, g0, dh1)


def _blockdiag4(w):
    w4 = w.reshape(4, 4, 64, 1, 64).astype(MXU)
    same = (jnp.arange(4)[:, None, None, None] == jnp.arange(4)[None, None, :, None])
    return jnp.where(same[None], w4, jnp.zeros((), MXU)).reshape(4, 256, 256)


def _blockdiag4_extract(g):
    g5 = g.reshape(4, 4, 64, 4, 64)
    return jnp.stack([g5[:, q, :, q, :] for q in range(4)], axis=1).reshape(NH, 64, 64)


def _local_step(x, tgt, p, after_ssd=None, late_weights=None, send_mlp_grads=None, send_late_grads=None):
    f = lambda a: a.astype(F32)
    proj, u = _inproj(x, p["norm_mix_pre"], p["w_in_pT"])
    ssm_params = (p["cw_xs"], p["cw_bc"], p["cb_xs"], p["cb_bc"], p["dt_bias"], p["a_log"], p["d_skip_x"],
                  p["ssm_norm"])
    ya, sprev, *ssd_saved = _ssd_fwd(proj, *ssm_params)
    cb_lru = p["conv_lru_b"] if after_ssd is None else p["conv_lru_b"] + after_ssd(ya)
    lru_params = (p["conv_lru_w"], cb_lru, p["wa_bd"], p["wx_bd"], p["lru_ba"], p["lru_bx"], p["lru_lambda"])
    yb, h, xr = _lru_fwd(proj, *lru_params)
    if late_weights is not None:
        p = dict(p, **late_weights(yb))
    h1, mix, merged = _merge_out(ya, yb, proj, x, p["w_out"], p["norm_mix_post"])
    hp, v, dout, dff, loss, dg3 = _mlp_fwd(h1, p["norm_mlp_pre"], p["w_upT"], p["w_down"], p["norm_mlp_post"], tgt)

    dv, dhp, hid = _mlp_bwd(dff, hp, p["w_upT"], p["w_down"])
    dh1, dmix, dya, dyb, dga, dgb, dg2, dg1 = _out_bwd(dv, h1, p["norm_mlp_pre"], dout, mix, ya, yb, proj, p["w_out"],
                                                     p["norm_mix_post"])
    d_w_out = _wgrad(merged, dmix, "wgrad_out")
    (dz, dxs, dbc, dcwx, dcwb, dcbx, dcbb, ddtb, dalog, ddsk, dnrm, d_w_down, d_w_up_t) = _ssd_bwd(
        dya, ssd_saved, proj, sprev, p["cw_xs"], p["cw_bc"], *ssm_params[4:], (hid, dff, dhp, v))
    ba_lru = p["lru_ba"] if send_mlp_grads is None else p["lru_ba"] + send_mlp_grads(d_w_up_t, d_w_down, d_w_out)
    (dgl, dxl, dcwl, dcbl, dwa, dwx, dba, dbx, dlam) = _lru_bwd(
        dyb, proj, h, xr, p["conv_lru_w"], p["wa_bd"], p["wx_bd"], ba_lru, p["lru_bx"], p["lru_lambda"])
    dsegs = [dz, dgl, dxl, dga, dgb, dxs, dbc]
    d_w_in_pt = _wgrad_segs(dsegs, u, "wgrad_in")
    g0 = p["norm_mix_pre"]
    if send_late_grads is not None:
        g0 = g0 + send_late_grads(d_w_in_pt)
    grad_x, dg0 = _inproj_bwd(dsegs, p["w_in_pT"], x, g0, dh1)
    grads = dict(
        norm_mix_pre=dg0, w_in_pT=d_w_in_pt, conv_ssm_w=jnp.concatenate([dcwx, dcwb], axis=1),
        conv_ssm_b=jnp.concatenate([dcbx, dcbb], axis=1), dt_bias=ddtb[:, :NH], a_log=dalog[:, :NH],
        d_skip=f(ddsk).reshape(NH, 64).sum(axis=1)[None, :], ssm_norm=dnrm, conv_lru_w=dcwl, conv_lru_b=dcbl,
        lru_wa=_blockdiag4_extract(dwa), lru_ba=dba, lru_wx=_blockdiag4_extract(dwx), lru_bx=dbx, lru_lambda=dlam,
        w_out=d_w_out, norm_mix_post=dg1, norm_mlp_pre=dg2, w_upT=d_w_up_t, w_down=d_w_down, norm_mlp_post=dg3)
    return loss[0, 0], grad_x, grads


W_IN_COLS = 6672


def _w_in_t_to_padded(wt):
    z, xs, bc, dt = wt[0:1024], wt[1024:2048], wt[2048:2560], wt[2560:2576]
    gl, xl, ga, gb = wt[2576:3600], wt[3600:4624], wt[4624:5648], wt[5648:6672]
    return jnp.concatenate([z, gl, xl, ga, gb, xs, bc, dt, jnp.zeros((NP - 6672, wt.shape[1]), wt.dtype)], axis=0)


def _w_in_t_from_padded(wp):
    z, gl, xl, ga, gb = (wp[SEG * s:SEG * (s + 1)] for s in range(5))
    xs, bc, dt = wp[5120:6144], wp[6144:6656], wp[6656:6672]
    return jnp.concatenate([z, xs, bc, dt, gl, xl, ga, gb], axis=0)


def _prep_params(full, big):
    f = lambda a: a.astype(F32)
    pad128 = lambda a: jnp.pad(f(a).reshape(1, -1), ((0, 0), (0, 128 - a.size)))
    cw = f(full["conv_ssm_w"])
    cb = f(full["conv_ssm_b"]).reshape(1, -1)
    return dict(
        big, norm_mix_pre=f(full["norm_mix_pre"]).reshape(1, D),
        cw_xs=cw[:, :D], cw_bc=cw[:, D:], cb_xs=cb[:, :D], cb_bc=cb[:, D:],
        dt_bias=pad128(full["dt_bias"]), a_log=pad128(full["a_log"]),
        d_skip_x=jnp.repeat(f(full["d_skip"]).reshape(-1), 64).reshape(1, D), ssm_norm=f(full["ssm_norm"]).reshape(1, D),
        conv_lru_w=f(full["conv_lru_w"]), conv_lru_b=f(full["conv_lru_b"]).reshape(1, D),
        wa_bd=_blockdiag4(full["lru_wa"]), wx_bd=_blockdiag4(full["lru_wx"]),
        lru_ba=f(full["lru_ba"]).reshape(1, D), lru_bx=f(full["lru_bx"]).reshape(1, D),
        lru_lambda=f(full["lru_lambda"]).reshape(1, D),
        norm_mix_post=f(full["norm_mix_post"]).reshape(1, D), norm_mlp_pre=f(full["norm_mlp_pre"]).reshape(1, D),
        norm_mlp_post=f(full["norm_mlp_post"]).reshape(1, D))


MESH_ID = pl.DeviceIdType.MESH
ANY = pl.BlockSpec(memory_space=pl.ANY)


def _my_place():
    x, y, c = lax.axis_index("x"), lax.axis_index("y"), lax.axis_index("c")
    return x, y, c, 4 * x + 2 * y + c


def _peer(x, y, c, k):
    return (x ^ ((k >> 2) & 1), y ^ ((k >> 1) & 1), c ^ (k & 1))


def _all_gather(pack, name):
    r_n = pack.shape[0]
    half = -(-r_n // 32) * 16
    n_cp = 9

    def body(in_ref, out_ref, send_sems, recv_sems, local_sem):
        x, y, c, me = _my_place()
        here, sibling, x_nbr, y_nbr = (x, y, c), (x, y, 1 - c), (1 - x, y, c), (x, 1 - y, c)
        part = {"all": pl.ds(0, r_n), "lo": pl.ds(0, half), "hi": pl.ds(half, r_n - half)}

        def copy(j, block, rows, to, src=None):
            return pltpu.make_async_remote_copy(
                src_ref=out_ref.at[block, part[rows]] if src is None else src, dst_ref=out_ref.at[block, part[rows]],
                send_sem=send_sems.at[j], recv_sem=recv_sems.at[j], device_id=to, device_id_type=MESH_ID)

        mine = pltpu.make_async_copy(in_ref, out_ref.at[me], local_sem)
        mine.start()
        first = [copy(0, me, "all", sibling, src=in_ref), copy(1, me, "all", x_nbr, src=in_ref),
                 copy(2, me, "all", y_nbr, src=in_ref)]
        for cp in first:
            cp.start()
        relay = [[(3, me ^ 4, "lo", y_nbr), (5, me ^ 4, "all", sibling)],
                 [(4, me ^ 2, "hi", x_nbr), (6, me ^ 2, "all", sibling)],
                 [(7, me ^ 6, "lo", sibling)],
                 [(8, me ^ 6, "hi", sibling)]]
        landed = [(1, me ^ 4, "all"), (2, me ^ 2, "all"), (3, me ^ 6, "lo"), (4, me ^ 6, "hi")]
        passed = []
        for (j, block, rows), nxt in zip(landed, relay):
            copy(j, block, rows, here).wait_recv()
            for args in nxt:
                cp = copy(*args)
                cp.start()
                passed.append(cp)
        for j, block, rows in ((0, me ^ 1, "all"), (5, me ^ 5, "all"), (6, me ^ 3, "all"), (7, me ^ 7, "lo"),
                               (8, me ^ 7, "hi")):
            copy(j, block, rows, here).wait_recv()
        for cp in first + passed:
            cp.wait_send()
        mine.wait()

    return pl.pallas_call(
        body, name=name, in_specs=[ANY], out_specs=ANY,
        out_shape=jax.ShapeDtypeStruct((N_DEV,) + pack.shape, pack.dtype),
        scratch_shapes=[pltpu.SemaphoreType.DMA((n_cp,)), pltpu.SemaphoreType.DMA((n_cp,)), pltpu.SemaphoreType.DMA],
    )(pack)


HBM = pl.BlockSpec(memory_space=pltpu.HBM)
SEM = pl.BlockSpec(memory_space=pltpu.SEMAPHORE)
PLAN_GATHER = tuple((k, "pack", 0) for k in range(1, N_DEV))
PLAN_SCATTER = tuple((k, "slot", 0) for k in range(1, N_DEV))
PLAN_GATHER_ICI = tuple((k, "pack", 0) for k in (2, 4, 6))
PLAN_GATHER_D2D = ((1, "pack", 0),) + tuple((1, s, s) for s in (2, 4, 6))


def _plan_copy(j, plan, src_ref, land_ref, sems):
    k, source, slot = plan[j]
    x, y, c, me = _my_place()
    if source == "pack":
        src = src_ref
    elif source == "slot":
        src = src_ref.at[me ^ k]
    else:
        src = land_ref.at[me ^ source]
    return pltpu.make_async_remote_copy(
        src_ref=src, dst_ref=land_ref.at[me ^ slot], send_sem=sems[j], recv_sem=sems[len(plan) + j],
        device_id=_peer(x, y, c, k), device_id_type=MESH_ID)


def _exchange_start(src, land, plan, name):
    n_c = len(plan)
    if land is None:
        land = pltpu.with_memory_space_constraint(lax.empty((N_DEV,) + src.shape[-2:], src.dtype), pltpu.HBM)

    def body(src_ref, land_ref, *rest):
        sems, token = rest[:2 * n_c], rest[2 * n_c + 2]
        for j in range(n_c):
            _plan_copy(j, plan, src_ref, land_ref, sems).start()
        token[...] = jnp.zeros_like(token)

    outs = pl.pallas_call(
        body, name=name,
        out_shape=(pltpu.SemaphoreType.DMA(()),) * (2 * n_c) + (
            pltpu.HBM(src.shape, src.dtype), pltpu.HBM(land.shape, land.dtype), jax.ShapeDtypeStruct((8, 128), F32)),
        in_specs=(HBM, HBM), out_specs=(SEM,) * (2 * n_c) + (HBM, HBM, pl.BlockSpec(memory_space=pltpu.VMEM)),
        input_output_aliases={0: 2 * n_c, 1: 2 * n_c + 1},
        compiler_params=pltpu.CompilerParams(has_side_effects=pltpu.SideEffectType.DATAFLOW_SIDE_EFFECTING),
    )(pltpu.with_memory_space_constraint(src, pltpu.HBM), land)
    return outs[:2 * n_c], outs[2 * n_c], outs[2 * n_c + 1], outs[2 * n_c + 2]


def _exchange_wait(sems, src_thru, land_thru, after, plan, name):
    n_c = len(plan)

    def body(src_ref, land_ref, *rest):
        for j in range(n_c):
            cp = _plan_copy(j, plan, src_ref, land_ref, rest[:2 * n_c])
            cp.wait_send()
            cp.wait_recv()

    return pl.pallas_call(
        body, name=name,
        out_shape=(pltpu.HBM(src_thru.shape, src_thru.dtype), pltpu.HBM(land_thru.shape, land_thru.dtype)),
        in_specs=(HBM, HBM) + (SEM,) * (2 * n_c) + (pl.BlockSpec(memory_space=pl.ANY),), out_specs=(HBM, HBM),
        input_output_aliases={0: 0, 1: 1},
        compiler_params=pltpu.CompilerParams(has_side_effects=pltpu.SideEffectType.DATAFLOW_SIDE_EFFECTING),
    )(src_thru, land_thru, *sems, after)


def _slot_sum(parts, name):
    r_n, c_n = parts.shape[1:]
    tr = max(t for t in range(16, 513, 16) if r_n % t == 0)

    def body(p_ref, o_ref):
        acc = p_ref[0].astype(F32)
        for k in range(1, N_DEV):
            acc = acc + p_ref[k].astype(F32)
        o_ref[...] = acc

    return pl.pallas_call(
        body, name=name, grid=(r_n // tr,),
        in_specs=[pl.BlockSpec((N_DEV, tr, c_n), lambda i: (0, i, 0))],
        out_specs=pl.BlockSpec((tr, c_n), lambda i: (i, 0)),
        out_shape=jax.ShapeDtypeStruct((r_n, c_n), F32),
        compiler_params=_cp("parallel"),
    )(parts)


def _adam_math(w, g, m, v):
    m = ADAM_B1 * m + (1.0 - ADAM_B1) * g
    v = ADAM_B2 * v + (1.0 - ADAM_B2) * jnp.square(g)
    m_hat = m / (1.0 - ADAM_B1 ** ADAM_STEP)
    v_hat = v / (1.0 - ADAM_B2 ** ADAM_STEP)
    return -ADAM_LR * (m_hat / (jnp.sqrt(v_hat) + ADAM_EPS) + ADAM_WD * w), m, v


def _adam_big(w, g, m, v, name):
    def body(w_ref, g_ref, m_ref, v_ref, d_ref, mo_ref, vo_ref):
        d_ref[...], mo_ref[...], vo_ref[...] = _adam_math(w_ref[...], g_ref[...], m_ref[...], v_ref[...])

    if w.ndim == 3:
        _, r_n, c_n = w.shape
        tr = min(r_n, 256)
        grid = (r_n // tr,)
        blk = lambda: pl.BlockSpec((1, tr, c_n), lambda i: (0, i, 0))
    else:
        r_n, c_n = w.shape
        tc = min(c_n, 256)
        grid = (c_n // tc,)
        blk = lambda: pl.BlockSpec((r_n, tc), lambda i: (0, i))
    return pl.pallas_call(
        body, name=name, grid=grid, in_specs=[blk(), blk(), blk(), blk()], out_specs=[blk(), blk(), blk()],
        out_shape=[jax.ShapeDtypeStruct(w.shape, F32)] * 3, compiler_params=_cp("parallel"),
    )(w, g, m, v)


def _adam_small(groups, where, wmv):
    n, n_g = len(wmv), len(groups)

    def body(*refs):
        g_refs = refs[:n_g]
        w_refs = refs[n_g:n_g + 3 * n]
        o_refs = refs[n_g + 3 * n:]
        for q in range(n):
            w_ref, m_ref, v_ref = w_refs[3 * q:3 * q + 3]
            r, c = w_ref.shape
            gi, r0 = where[q]
            g = g_refs[gi][0, r0:r0 + r, 0:c]
            for k in range(1, N_DEV):
                g = g + g_refs[gi][k, r0:r0 + r, 0:c]
            d, m, v = _adam_math(w_ref[...], g, m_ref[...], v_ref[...])
            o_refs[4 * q][...] = g
            o_refs[4 * q + 1][...] = d
            o_refs[4 * q + 2][...] = m
            o_refs[4 * q + 3][...] = v

    flat_wmv = [a for t in wmv for a in t]
    vm = pl.BlockSpec(memory_space=pltpu.VMEM)
    outs = pl.pallas_call(
        body, name="adam_small", in_specs=[vm] * (n_g + 3 * n), out_specs=[vm] * (4 * n),
        out_shape=[jax.ShapeDtypeStruct(t[0].shape, F32) for t in wmv for _ in range(4)],
        compiler_params=pltpu.CompilerParams(vmem_limit_bytes=VMEM_LIMIT),
    )(*groups, *flat_wmv)
    return [tuple(outs[4 * q:4 * q + 4]) for q in range(n)]


WEIGHTS = ["norm_mix_pre", "w_in", "conv_ssm_w", "conv_ssm_b", "dt_bias", "a_log", "d_skip", "ssm_norm", "conv_lru_w",
           "conv_lru_b", "lru_wa", "lru_ba", "lru_wx", "lru_bx", "lru_lambda", "w_out", "norm_mix_post", "norm_mlp_pre",
           "w_up", "w_down", "norm_mlp_post"]
BIG = ["w_out", "w_up", "w_down", "w_in"]
IN_ROWS = W_IN_COLS // N_DEV
IN_PAD, EARLY_ROWS = 848, 880
ROW_UP, ROW_DOWN, LATE_ROWS = 128, 640, 1152
GRAD_LATE_ROWS = 864
CONV_SSM_COLS, CONV_LRU_COLS = 1536 // N_DEV, D // N_DEV
SMALL = [("norm_mix_pre", (1, D), 0, 0), ("ssm_norm", (1, D), 0, 1), ("conv_lru_b", (1, D), 0, 2),
         ("lru_lambda", (1, D), 0, 3), ("norm_mix_post", (1, D), 0, 4), ("norm_mlp_pre", (1, D), 0, 5),
         ("norm_mlp_post", (1, D), 0, 6), ("conv_ssm_b", (1, 1536), 1, 0), ("dt_bias", (1, NH), 2, 0),
         ("a_log", (1, NH), 2, 1), ("d_skip", (1, NH), 2, 2), ("conv_ssm_w", (4, CONV_SSM_COLS), 3, 0),
         ("conv_lru_w", (4, CONV_LRU_COLS), 4, 0), ("lru_wa", (D, 64), 5, 0), ("lru_wx", (D, 64), 5, D),
         ("lru_ba", (NH, 64), 6, 0), ("lru_bx", (NH, 64), 6, NH)]
SMALL_GROUPS = [(8, D), (1, 1536), (8, 128), (4, 1536), (4, D), (2 * D, 64), (2 * NH, 64)]


def _pad_rows(flat, mult):
    n = flat.shape[0]
    rows = -(-n // (128 * mult)) * mult
    return jnp.pad(flat, (0, rows * 128 - n)).reshape(rows, 128)


def _split3(a):
    hi = a.astype(MXU)
    r1 = a - hi.astype(F32)
    mid = r1.astype(MXU)
    lo = (r1 - mid.astype(F32)).astype(MXU)
    return jnp.stack([hi, mid, lo])


def _early_pack(a, me):
    bf = lambda t: t.astype(MXU)
    conv = lambda t, c: jnp.pad(_split3(t).reshape(12, c), ((0, 4), (0, D - c)))
    shifted = lax.dynamic_update_slice(jnp.zeros((IN_PAD, D), MXU), bf(a["w_in"][0]).T, (2 * me, 0))
    return jnp.concatenate([shifted, conv(a["conv_ssm_w"][0], CONV_SSM_COLS), conv(a["conv_lru_w"][0], CONV_LRU_COLS)],
                           axis=0)


TILE = 16
SHARD_TILES = IN_PAD // TILE
SHARD_STEP = (IN_ROWS // TILE)
SEG_TILES = ((0, 64, 0), (64, 128, 320), (128, 160, 384), (160, 161, 416), (161, 225, 64), (225, 289, 128),
             (289, 353, 192), (353, 417, 256))


def _tile_runs(lo, hi):
    runs = []
    for s0, s1, d0 in SEG_TILES:
        a, b = max(lo, s0), min(hi, s1)
        if a < b:
            runs.append((a, b - a, d0 + a - s0))
    return runs


def _assemble_w_in(g):
    whole = []
    for k in range(N_DEV):
        lo = SHARD_STEP * k + (1 if k else 0)
        hi = SHARD_STEP * (k + 1) + (1 if k == N_DEV - 1 else 0)
        whole += [(k, a - SHARD_STEP * k, n, d) for a, n, d in _tile_runs(lo, hi)]
    split = [(k, _tile_runs(SHARD_STEP * k, SHARD_STEP * k + 1)[0][2]) for k in range(1, N_DEV)]

    def body(g_ref, o_ref):
        rows = lambda t, n=1: pl.ds(TILE * t, TILE * n)
        for k, t, n, d in whole:
            o_ref[rows(d, n), :] = g_ref[k, rows(t, n), :]
        for k, d in split:
            o_ref[rows(d), :] = g_ref[k - 1, rows(SHARD_STEP), :] + g_ref[k, rows(0), :]
        o_ref[pl.ds(W_IN_COLS, NP - W_IN_COLS), :] = jnp.zeros((NP - W_IN_COLS, D), o_ref.dtype)

    vm = pl.BlockSpec(memory_space=pltpu.VMEM)
    return pl.pallas_call(
        body, name="assemble_w_in", in_specs=[vm], out_specs=vm, out_shape=jax.ShapeDtypeStruct((NP, D), g.dtype),
        compiler_params=pltpu.CompilerParams(vmem_limit_bytes=VMEM_LIMIT),
    )(g)


def _scatter_w_in_grad(dw):
    runs = [(k, a - SHARD_STEP * k, n, d) for k in range(N_DEV)
            for a, n, d in _tile_runs(SHARD_STEP * k, SHARD_STEP * k + SHARD_TILES)]
    pad = GRAD_LATE_ROWS - IN_PAD

    def body(dw_ref, o_ref):
        rows = lambda t, n: pl.ds(TILE * t, TILE * n)
        for k, t, n, d in runs:
            o_ref[k, rows(t, n), :] = dw_ref[rows(d, n), :]
        for k in range(N_DEV):
            o_ref[k, pl.ds(IN_PAD, pad), :] = jnp.zeros((pad, D), o_ref.dtype)

    vm = pl.BlockSpec(memory_space=pltpu.VMEM)
    return pl.pallas_call(
        body, name="scatter_w_in_grad", in_specs=[vm], out_specs=vm,
        out_shape=jax.ShapeDtypeStruct((N_DEV, GRAD_LATE_ROWS, D), dw.dtype),
        compiler_params=pltpu.CompilerParams(vmem_limit_bytes=VMEM_LIMIT),
    )(dw)


def _early_unpack(g):
    w_in_pt = _assemble_w_in(g)
    conv = {}
    for n, r0, c in (("conv_ssm_w", IN_PAD, CONV_SSM_COLS), ("conv_lru_w", IN_PAD + 16, CONV_LRU_COLS)):
        s = g[:, r0:r0 + 12, :c].astype(F32).reshape(N_DEV, 3, 4, c)
        conv[n] = ((s[:, 0] + s[:, 1]) + s[:, 2]).transpose(1, 0, 2).reshape(4, N_DEV * c)
    return w_in_pt, conv


def _late_pack(a):
    bf = lambda t: t.astype(MXU)
    return jnp.concatenate([bf(a["w_out"][0]), bf(a["w_up"][0]).T, bf(a["w_down"][0])], axis=0)


def _late_unpack(g):
    return dict(w_out=g[:, :ROW_UP].reshape(D, D), w_upT=g[:, ROW_UP:ROW_DOWN].reshape(FF, D),
                w_down=g[:, ROW_DOWN:].reshape(FF, D))


def _own_slot(land, own):
    me = 4 * lax.axis_index("x") + 2 * lax.axis_index("y") + lax.axis_index("c")
    return lax.dynamic_update_slice_in_dim(land, own[None], me, axis=0)


def kernel(x, norm_mix_pre, w_in, conv_ssm_w, conv_ssm_b, dt_bias, a_log, d_skip, ssm_norm, conv_lru_w, conv_lru_b, lru_wa, lru_ba, lru_wx, lru_bx, lru_lambda, w_out, norm_mix_post, norm_mlp_pre, w_up, w_down, norm_mlp_post, loss_target, m_norm_mix_pre, m_w_in, m_conv_ssm_w, m_conv_ssm_b, m_dt_bias, m_a_log, m_d_skip, m_ssm_norm, m_conv_lru_w, m_conv_lru_b, m_lru_wa, m_lru_ba, m_lru_wx, m_lru_bx, m_lru_lambda, m_w_out, m_norm_mix_post, m_norm_mlp_pre, m_w_up, m_w_down, m_norm_mlp_post, v_norm_mix_pre, v_w_in, v_conv_ssm_w, v_conv_ssm_b, v_dt_bias, v_a_log, v_d_skip, v_ssm_norm, v_conv_lru_w, v_conv_lru_b, v_lru_wa, v_lru_ba, v_lru_wx, v_lru_bx, v_lru_lambda, v_w_out, v_norm_mix_post, v_norm_mlp_pre, v_w_up, v_w_down, v_norm_mlp_post):
    vals = (norm_mix_pre, w_in, conv_ssm_w, conv_ssm_b, dt_bias, a_log, d_skip, ssm_norm, conv_lru_w, conv_lru_b, lru_wa, lru_ba, lru_wx, lru_bx, lru_lambda, w_out, norm_mix_post, norm_mlp_pre, w_up, w_down, norm_mlp_post)
    m_vals = (m_norm_mix_pre, m_w_in, m_conv_ssm_w, m_conv_ssm_b, m_dt_bias, m_a_log, m_d_skip, m_ssm_norm, m_conv_lru_w, m_conv_lru_b, m_lru_wa, m_lru_ba, m_lru_wx, m_lru_bx, m_lru_lambda, m_w_out, m_norm_mix_post, m_norm_mlp_pre, m_w_up, m_w_down, m_norm_mlp_post)
    v_vals = (v_norm_mix_pre, v_w_in, v_conv_ssm_w, v_conv_ssm_b, v_dt_bias, v_a_log, v_d_skip, v_ssm_norm, v_conv_lru_w, v_conv_lru_b, v_lru_wa, v_lru_ba, v_lru_wx, v_lru_bx, v_lru_lambda, v_w_out, v_norm_mix_post, v_norm_mlp_pre, v_w_up, v_w_down, v_norm_mlp_post)
    w = dict(zip(WEIGHTS, vals))
    m = dict(zip(WEIGHTS, m_vals))
    v = dict(zip(WEIGHTS, v_vals))
    me = 4 * lax.axis_index("x") + 2 * lax.axis_index("y") + lax.axis_index("c")

    bf = lambda t: t.astype(MXU)
    late = _late_pack(w)
    early = _all_gather(_early_pack(w, me), "early_weights_all_gather")
    late, early = lax.optimization_barrier((late, early))
    lw = {}
    lw["sems"], lw["src"], lw["land"], token = _exchange_start(late, None, PLAN_GATHER_ICI, "late_weights_ici_start")
    w_in_pt, conv_w = _early_unpack(early)
    full = {n: (conv_w[n] if n in conv_w else w[n][0]) for n in WEIGHTS if n not in BIG}
    full["norm_mix_pre"] = full["norm_mix_pre"] + token[0, 0]

    def after_ssd(after):
        src, land = _exchange_wait(lw["sems"], lw["src"], lw["land"], after, PLAN_GATHER_ICI, "late_weights_ici_wait")
        lw["sems"], lw["src"], lw["land"], tok = _exchange_start(src, land, PLAN_GATHER_D2D, "late_weights_d2d_start")
        return tok[0, 0]

    def late_weights(after):
        src, land = _exchange_wait(lw["sems"], lw["src"], lw["land"], after, PLAN_GATHER_D2D, "late_weights_d2d_wait")
        return _late_unpack(_own_slot(land, src))

    sent = {}

    def send_mlp_grads(d_w_up_t, d_w_down, d_w_out):
        src = jnp.concatenate([bf(d_w_up_t).reshape(N_DEV, -1, D), bf(d_w_down).reshape(N_DEV, -1, D),
                               bf(d_w_out).reshape(N_DEV, -1, D)], axis=1)
        sent["sems"], sent["src"], sent["land"], tok = _exchange_start(src, None, PLAN_SCATTER, "mlp_grads_start")
        return tok[0, 0]

    def send_late_grads(d_w_in_pt):
        src = _scatter_w_in_grad(d_w_in_pt)
        sent["sems2"], sent["src2"], sent["land2"], tok = _exchange_start(src, None, PLAN_SCATTER, "late_grads_start")
        return tok[0, 0]

    loss, grad_x, g = _local_step(x[0], loss_target[0], _prep_params(full, dict(w_in_pT=w_in_pt)), after_ssd,
                                  late_weights, send_mlp_grads, send_late_grads)
    loss = lax.psum(loss, ("x", "y", "c"))

    own = lambda src: lax.dynamic_index_in_dim(src, me, keepdims=False)
    out_g, out_d, out_m, out_v = {}, {}, {}, {}
    mlp_src, mlp_land = _exchange_wait(sent["sems"], sent["src"], sent["land"], grad_x, PLAN_SCATTER, "mlp_grads_wait")
    g_mlp = _slot_sum(_own_slot(mlp_land, own(mlp_src)), "slot_sum_mlp")
    fs = FF // N_DEV
    for n, gn in (("w_up", g_mlp[:fs].T[None]), ("w_down", g_mlp[fs:2 * fs][None]), ("w_out", g_mlp[2 * fs:][None])):
        out_g[n] = gn
        out_d[n], out_m[n], out_v[n] = _adam_big(w[n], gn, m[n], v[n], "adam_" + n)
    zrow = jnp.zeros((1, D), F32)
    pad16 = lambda a: jnp.pad(a, ((0, 0), (0, 128 - NH)))
    small_parts = [
        jnp.concatenate([g["norm_mix_pre"], g["ssm_norm"], g["conv_lru_b"], g["lru_lambda"], g["norm_mix_post"],
                         g["norm_mlp_pre"], g["norm_mlp_post"], zrow], axis=0),
        g["conv_ssm_b"],
        jnp.concatenate([pad16(g["dt_bias"]), pad16(g["a_log"]), pad16(g["d_skip"]), jnp.zeros((5, 128), F32)], axis=0),
        g["conv_ssm_w"], g["conv_lru_w"],
        jnp.concatenate([g["lru_wa"].reshape(D, 64), g["lru_wx"].reshape(D, 64)], axis=0),
        jnp.concatenate([g["lru_ba"].reshape(NH, 64), g["lru_bx"].reshape(NH, 64)], axis=0)]
    small = _pad_rows(jnp.concatenate([s.reshape(-1) for s in small_parts]), 8)
    small, _ = lax.optimization_barrier((small, out_v["w_out"]))
    small_all = _all_gather(small, "small_grads_all_gather")

    lg_src, lg_land = _exchange_wait(sent["sems2"], sent["src2"], sent["land2"], small_all, PLAN_SCATTER, "late_grads_wait")
    g_late = _slot_sum(_own_slot(lg_land, own(lg_src)), "slot_sum_late")
    gt = lax.dynamic_slice(g_late, (2 * me, 0), (IN_ROWS, D))
    dt_, mt_, vt_ = _adam_big(w["w_in"][0].T, gt, m["w_in"][0].T, v["w_in"][0].T, "adam_w_in")
    out_g["w_in"], out_d["w_in"], out_m["w_in"], out_v["w_in"] = gt.T[None], dt_.T[None], mt_.T[None], vt_.T[None]
    sflat = small_all.reshape(N_DEV, -1)
    groups = []
    off = 0
    for r, c in SMALL_GROUPS:
        groups.append(sflat[:, off:off + r * c].reshape(N_DEV, r, c))
        off += r * c
    groups[3] = lax.dynamic_slice_in_dim(groups[3], me * CONV_SSM_COLS, CONV_SSM_COLS, axis=2)
    groups[4] = lax.dynamic_slice_in_dim(groups[4], me * CONV_LRU_COLS, CONV_LRU_COLS, axis=2)
    wmv = [(w[n].reshape(s), m[n].reshape(s), v[n].reshape(s)) for n, s, _, _ in SMALL]
    res = _adam_small(groups, [(gi, r0) for _, _, gi, r0 in SMALL], wmv)
    for (n, _, _, _), (g_n, d_n, m_n, v_n) in zip(SMALL, res):
        shape = w[n].shape
        out_g[n], out_d[n], out_m[n], out_v[n] = (g_n.reshape(shape), d_n.reshape(shape), m_n.reshape(shape),
                                                  v_n.reshape(shape))
    return (loss, grad_x[None], *[out_g[n] for n in WEIGHTS], *[out_d[n] for n in WEIGHTS],
            *[out_m[n] for n in WEIGHTS], *[out_v[n] for n in WEIGHTS])
```

```python
import functools

import jax
import jax.numpy as jnp
from jax import lax
from jax.experimental import pallas as pl
from jax.experimental.pallas import tpu as pltpu

F32 = jnp.float32
MXU = jnp.bfloat16
HI = lax.Precision.HIGHEST
EPS = 1e-6

D = 1024
NH = 16
NS = 128
CH = 128
FF = 4096
NP = 7168
SEG = 1024
LRU_C = 8.0
N_DEV = 8

ADAM_LR, ADAM_B1, ADAM_B2, ADAM_EPS, ADAM_WD, ADAM_STEP = 0.001, 0.9, 0.999, 1e-08, 0.01, 10

VMEM_LIMIT = 56 * 1024 * 1024


def _cp(*sem):
    return pltpu.CompilerParams(dimension_semantics=sem, vmem_limit_bytes=VMEM_LIMIT)


def _nn(a, b):
    return jnp.dot(a.astype(MXU), b.astype(MXU), preferred_element_type=F32)


def _nt(a, b):
    return lax.dot_general(a.astype(MXU), b.astype(MXU), (((1,), (1,)), ((), ())), preferred_element_type=F32)


def _tn(a, b):
    return lax.dot_general(a.astype(MXU), b.astype(MXU), (((0,), (0,)), ((), ())), preferred_element_type=F32)


_sigmoid = jax.nn.sigmoid


def _silu(x):
    return x * _sigmoid(x)


def _dsilu(x):
    s = _sigmoid(x)
    return s + x * s * (1.0 - s)


def _softplus(x):
    return jnp.maximum(x, 0.0) + jnp.log(1.0 + jnp.exp(-jnp.abs(x)))


def _rms(x, g):
    r = lax.rsqrt(jnp.mean(x * x, axis=-1, keepdims=True) + EPS)
    return x * r * g


def _rms_bwd(x, g, dy):
    r = lax.rsqrt(jnp.mean(x * x, axis=-1, keepdims=True) + EPS)
    gdy = g * dy
    dx = r * gdy - x * (r * r * r) * jnp.mean(x * gdy, axis=-1, keepdims=True)
    return dx, dy * x * r


def _rowsum(x):
    return jnp.sum(x, axis=0, keepdims=True)


def _taps_past(cur, prev8):
    r_n, c_n = cur.shape
    row = lax.broadcasted_iota(jnp.int32, (r_n, c_n), 0)
    out = []
    for k in range(4):
        s = 3 - k
        if s == 0:
            out.append(cur)
            continue
        head = jnp.concatenate([pltpu.roll(prev8, s, 0), jnp.zeros((r_n - 8, c_n), F32)], axis=0)
        out.append(jnp.where(row < s, head, pltpu.roll(cur, s, 0)))
    return out


def _taps_future(cur, fut8):
    r_n, c_n = cur.shape
    row = lax.broadcasted_iota(jnp.int32, (r_n, c_n), 0)
    out = []
    for k in range(4):
        s = 3 - k
        if s == 0:
            out.append(cur)
            continue
        tail = jnp.concatenate([jnp.zeros((r_n - 8, c_n), F32), pltpu.roll(fut8, 8 - s, 0)], axis=0)
        out.append(jnp.where(row >= r_n - s, tail, pltpu.roll(cur, r_n - s, 0)))
    return out


def _conv_apply(taps, w, b):
    acc = taps[0] * w[0:1, :]
    for k in range(1, 4):
        acc = acc + taps[k] * w[k:k + 1, :]
    return acc + b


def _inproj(x, g0, w):
    t_n = x.shape[0]
    tm = min(t_n, 1024)

    n_j = NP // SEG

    def body(x_ref, g_ref, w_ref, pb_ref, p6_ref, u_ref):
        j = pl.program_id(1)

        @pl.when(j == 0)
        def _():
            u_ref[...] = _rms(x_ref[...], g_ref[...]).astype(MXU)

        p = lax.dot_general(u_ref[...], w_ref[...], (((1,), (1,)), ((), ())), preferred_element_type=F32)

        @pl.when(j < n_j - 1)
        def _():
            pb_ref[...] = p.astype(MXU)

        @pl.when(j == n_j - 1)
        def _():
            p6_ref[...] = p

    pb, p6, u = pl.pallas_call(
        body, name="inproj", grid=(t_n // tm, n_j),
        in_specs=[pl.BlockSpec((tm, D), lambda i, j: (i, 0)), pl.BlockSpec((1, D), lambda i, j: (0, 0)),
                  pl.BlockSpec((SEG, D), lambda i, j: (j, 0))],
        out_specs=[pl.BlockSpec((tm, SEG), lambda i, j: (i, jnp.minimum(j, n_j - 2))),
                   pl.BlockSpec((tm, SEG), lambda i, j: (i, 0)), pl.BlockSpec((tm, D), lambda i, j: (i, 0))],
        out_shape=[jax.ShapeDtypeStruct((t_n, NP - SEG), MXU), jax.ShapeDtypeStruct((t_n, SEG), F32),
                   jax.ShapeDtypeStruct((t_n, D), MXU)],
        compiler_params=_cp("parallel", "arbitrary"),
    )(x, g0, w)
    return (pb, p6), u


def _ssd_prep(dtraw, dtb, alog):
    l_n = dtraw.shape[0]
    r = lax.broadcasted_iota(jnp.int32, (l_n, l_n), 0)
    c = lax.broadcasted_iota(jnp.int32, (l_n, l_n), 1)
    tril = (r >= c).astype(F32)
    triu = (r <= c).astype(F32)
    eye = (r == c).astype(F32)
    dt = _softplus(dtraw + dtb)
    adt = dt * (-jnp.exp(alog))
    ac = jnp.dot(tril, adt, preferred_element_type=F32, precision=HI)
    tn = (((0,), (0,)), ((), ()))
    ac_t = lax.dot_general(adt, triu, tn, preferred_element_type=F32, precision=HI)
    dt_t = lax.dot_general(dt, eye, tn, preferred_element_type=F32, precision=HI)
    return dt, dt_t, ac, ac_t, _rowsum(adt)


def _ssd_pair(j, xp, bg, cg, sp, dt, dt_t, ac, ac_t, aend):
    l_n = xp.shape[0]
    lane = lax.broadcasted_iota(jnp.int32, (l_n, 128), 1)
    sub = lax.broadcasted_iota(jnp.int32, (128, l_n), 0)
    lane1 = lax.broadcasted_iota(jnp.int32, (1, 128), 1)
    tri = lax.broadcasted_iota(jnp.int32, (l_n, l_n), 0) >= lax.broadcasted_iota(jnp.int32, (l_n, l_n), 1)
    lo = lax.broadcasted_iota(jnp.int32, (l_n, 128), 1) < 64
    lo_s = lax.broadcasted_iota(jnp.int32, (128, 128), 1) < 64
    cb = _nt(cg, bg)
    cs = _nn(cg, sp)
    x2 = jnp.concatenate([jnp.where(lo, xp, 0.0), jnp.where(lo, 0.0, xp)], axis=0)
    ws_, bs_, eo, ee = [], [], [], []
    for e in range(2):
        h = 2 * j + e
        ac_l = jnp.sum(jnp.where(lane == h, ac, 0.0), axis=1, keepdims=True)
        dt_l = jnp.sum(jnp.where(lane == h, dt, 0.0), axis=1, keepdims=True)
        a_end = jnp.sum(jnp.where(lane1 == h, aend, 0.0), axis=1, keepdims=True)
        ac_s, dt_s = ac_t[h:h + 1, :], dt_t[h:h + 1, :]
        decay = jnp.exp(jnp.where(tri, ac_l - ac_s, -1e30))
        ws_.append(cb * decay * dt_s)
        bs_.append(bg * (jnp.exp(a_end - ac_l) * dt_l))
        eo.append(jnp.exp(ac_l))
        ee.append(jnp.exp(a_end))
    y = _nn(jnp.concatenate(ws_, axis=1), x2) + jnp.where(lo, eo[0], eo[1]) * cs
    s_new = _tn(jnp.concatenate(bs_, axis=0), x2) + jnp.where(lo_s, ee[0], ee[1]) * sp
    return y, s_new


def _ssd_post(y, xs, z, dsk, nrm):
    y = (y + dsk * xs) * _silu(z)
    half = D // 2
    ya, yb = y[:, :half], y[:, half:]
    ya = ya * lax.rsqrt(jnp.mean(ya * ya, axis=-1, keepdims=True) + EPS)
    yb = yb * lax.rsqrt(jnp.mean(yb * yb, axis=-1, keepdims=True) + EPS)
    return jnp.concatenate([ya, yb], axis=1) * nrm


LAST_SEG = NP // SEG - 1


def _proj_ops(proj, seg_ids):
    return [proj[1] if s == LAST_SEG else proj[0] for s in seg_ids]


def _proj_specs(rows, seg_ids, order):
    return [pl.BlockSpec((rows, SEG), functools.partial(lambda i, c: (order(i), c), c=0 if s == LAST_SEG else s))
            for s in seg_ids]


def _prev_specs(rows, seg_ids, order):
    specs = []
    for s in seg_ids:
        n, c = (8, 0) if s == LAST_SEG else (16, s)
        specs.append(pl.BlockSpec((n, SEG), functools.partial(
            lambda i, n, c: (jnp.maximum(order(i) * (rows // n) - 1, 0), c), n=n, c=c)))
    return specs


def _prev8(ref):
    return ref[...] if ref.shape[0] == 8 else ref[8:16, :].astype(F32)


def _full(shape):
    return pl.BlockSpec(shape, lambda i: (0,) * len(shape))


def _ssd_fwd(proj, cwx, cwb, cbx, cbb, dtb, alog, dsk, nrm):
    t_n = proj[0].shape[0]
    n_c = t_n // CH
    fwd = lambda i: i

    def body(z_ref, xs_ref, bc_ref, xsp_ref, bcp_ref, cwx_ref, cwb_ref, cbx_ref, cbb_ref, dtb_ref, alog_ref,
             dsk_ref, nrm_ref, ya_ref, sprev_ref, yraw_ref, xspre_ref, bcpre_ref, s_ref):
        c = pl.program_id(0)

        @pl.when(c == 0)
        def _():
            s_ref[...] = jnp.zeros_like(s_ref)

        keep = jnp.where(c == 0, 0.0, 1.0)
        xs_pre = _conv_apply(_taps_past(xs_ref[...].astype(F32), _prev8(xsp_ref) * keep), cwx_ref[...], cbx_ref[...])
        bc_pre = _conv_apply(_taps_past(bc_ref[:, :512], bcp_ref[:, :512] * keep), cwb_ref[...], cbb_ref[...])
        xspre_ref[...] = xs_pre
        bcpre_ref[...] = bc_pre
        prep = _ssd_prep(bc_ref[:, 512:640], dtb_ref[...], alog_ref[...])
        xs = _silu(xs_pre)
        bc = _silu(bc_pre)
        sprev_ref[0] = s_ref[...]
        ys = []
        for j in range(NH // 2):
            g = j // 4
            yp, sn = _ssd_pair(j, xs[:, 128 * j:128 * j + 128], bc[:, 128 * g:128 * g + 128],
                               bc[:, 256 + 128 * g:384 + 128 * g], s_ref[:, 128 * j:128 * j + 128], *prep)
            ys.append(yp)
            s_ref[:, 128 * j:128 * j + 128] = sn
        y = jnp.concatenate(ys, axis=1)
        yraw_ref[...] = y
        ya_ref[...] = _ssd_post(y, xs, z_ref[...].astype(F32), dsk_ref[...], nrm_ref[...]).astype(ya_ref.dtype)

    return pl.pallas_call(
        body, name="ssd_fwd", grid=(n_c,),
        in_specs=_proj_specs(CH, (0, 5, 6), fwd) + _prev_specs(CH, (5, 6), fwd) + [
            _full((4, D)), _full((4, 512)), _full((1, D)), _full((1, 512)), _full((1, 128)), _full((1, 128)),
            _full((1, D)), _full((1, D))],
        out_specs=[pl.BlockSpec((CH, D), lambda i: (i, 0)), pl.BlockSpec((1, NS, D), lambda i: (i, 0, 0)),
                   pl.BlockSpec((CH, D), lambda i: (i, 0)), pl.BlockSpec((CH, D), lambda i: (i, 0)),
                   pl.BlockSpec((CH, 512), lambda i: (i, 0))],
        out_shape=[jax.ShapeDtypeStruct((t_n, D), MXU), jax.ShapeDtypeStruct((n_c, NS, D), F32),
                   jax.ShapeDtypeStruct((t_n, D), F32), jax.ShapeDtypeStruct((t_n, D), F32),
                   jax.ShapeDtypeStruct((t_n, 512), F32)],
        scratch_shapes=[pltpu.VMEM((NS, D), F32)],
        compiler_params=_cp("arbitrary"),
    )(*_proj_ops(proj, (0, 5, 6, 5, 6)), cwx, cwb, cbx, cbb, dtb, alog, dsk, nrm)


def _ssd_bwd(dya, saved, proj, sprev, cwx, cwb, dtb, alog, dsk, nrm, mlp_ops):
    t_n = proj[0].shape[0]
    n_c = t_n // CH
    rev = lambda i: n_c - 1 - i
    fb = FF // n_c

    def body(dya_ref, yraw_ref, xspre_ref, bcpre_ref, z_ref, xs_ref, bc_ref, sprev_ref, cwx_ref, cwb_ref,
             dtb_ref, alog_ref, dsk_ref, nrm_ref, hid_ref, dff_ref, dhp_ref, v_ref,
             dz_ref, dxs_ref, dbc_ref, dcwx_ref, dcwb_ref, dcbx_ref, dcbb_ref, ddtb_ref, dalog_ref, ddsk_ref,
             dnrm_ref, dwd_ref, dwu_ref, ds_ref, futx_ref, futb_ref):
        i = pl.program_id(0)
        acc_refs = (dcwx_ref, dcwb_ref, dcbx_ref, dcbb_ref, ddtb_ref, dalog_ref, ddsk_ref, dnrm_ref)
        tn = (((0,), (0,)), ((), ()))
        kt = t_n // (NH // 2)
        dwd_acc = jnp.zeros((fb, D), F32)
        dwu_acc = jnp.zeros((fb, D), F32)

        @pl.when(i == 0)
        def _():
            for r in (ds_ref, futx_ref, futb_ref) + acc_refs:
                r[...] = jnp.zeros_like(r)

        xs_pre = xspre_ref[...]
        bc_pre = bcpre_ref[...]
        xs = _silu(xs_pre)
        bc = _silu(bc_pre)
        prep, prep_vjp = jax.vjp(_ssd_prep, bc_ref[:, 512:640], dtb_ref[...], alog_ref[...])
        s_in = sprev_ref[0]

        def pair_args(j):
            g = j // 4
            return (xs[:, 128 * j:128 * j + 128], bc[:, 128 * g:128 * g + 128],
                    bc[:, 256 + 128 * g:384 + 128 * g], s_in[:, 128 * j:128 * j + 128]) + tuple(prep)

        _, post_vjp = jax.vjp(_ssd_post, yraw_ref[...], xs, z_ref[...].astype(F32), dsk_ref[...], nrm_ref[...])
        dy, dxs_skip, dz, ddsk, dnrm = post_vjp(dya_ref[...])
        dz_ref[...] = dz.astype(dz_ref.dtype)
        ddsk_ref[...] += ddsk
        dnrm_ref[...] += dnrm

        dprep = [jnp.zeros_like(p) for p in prep]
        dxp = []
        dbg = [jnp.zeros((CH, 128), F32), jnp.zeros((CH, 128), F32)]
        dcg = [jnp.zeros((CH, 128), F32), jnp.zeros((CH, 128), F32)]
        for j in range(NH // 2):
            g = j // 4
            _, pair_vjp = jax.vjp(functools.partial(_ssd_pair, j), *pair_args(j))
            cts = pair_vjp((dy[:, 128 * j:128 * j + 128], ds_ref[:, 128 * j:128 * j + 128]))
            dxp.append(cts[0])
            dbg[g] = dbg[g] + cts[1]
            dcg[g] = dcg[g] + cts[2]
            ds_ref[:, 128 * j:128 * j + 128] = cts[3]
            dprep = [a + b for a, b in zip(dprep, cts[4:])]
            rows = pl.ds(kt * j, kt)
            dwd_acc = dwd_acc + lax.dot_general(hid_ref[rows, :], dff_ref[rows, :], tn, preferred_element_type=F32)
            dwu_acc = dwu_acc + lax.dot_general(dhp_ref[rows, :], v_ref[rows, :], tn, preferred_element_type=F32)
        dwd_ref[...] = dwd_acc.astype(MXU)
        dwu_ref[...] = dwu_acc.astype(MXU)
        ddtraw, ddtb, dalog = prep_vjp(tuple(dprep))
        ddtb_ref[...] += ddtb
        dalog_ref[...] += dalog

        dxs_pre = (dxs_skip + jnp.concatenate(dxp, axis=1)) * _dsilu(xs_pre)
        dbc_pre = jnp.concatenate([dbg[0], dbg[1], dcg[0], dcg[1]], axis=1) * _dsilu(bc_pre)
        dcbx_ref[...] += _rowsum(dxs_pre)
        dcbb_ref[...] += _rowsum(dbc_pre)
        fx = _taps_future(dxs_pre, futx_ref[...])
        fbc = _taps_future(dbc_pre, futb_ref[...])
        xs_in = xs_ref[...].astype(F32)
        bc_in = bc_ref[:, :512]
        for k in range(4):
            dcwx_ref[k:k + 1, :] += _rowsum(fx[k] * xs_in)
            dcwb_ref[k:k + 1, :] += _rowsum(fbc[k] * bc_in)
        cwx = cwx_ref[...]
        cwb = cwb_ref[...]
        dxs_in = fx[0] * cwx[0:1, :]
        dbc_in = fbc[0] * cwb[0:1, :]
        for k in range(1, 4):
            dxs_in = dxs_in + fx[k] * cwx[k:k + 1, :]
            dbc_in = dbc_in + fbc[k] * cwb[k:k + 1, :]
        futx_ref[...] = dxs_pre[0:8, :]
        futb_ref[...] = dbc_pre[0:8, :]
        dxs_ref[...] = dxs_in.astype(dxs_ref.dtype)
        dbc_ref[...] = jnp.concatenate([dbc_in, ddtraw, jnp.zeros((CH, SEG - 640), F32)], axis=1).astype(dbc_ref.dtype)

    row_out = lambda: pl.BlockSpec((CH, D), lambda i: (rev(i), 0))
    outs = pl.pallas_call(
        body, name="ssd_bwd", grid=(n_c,),
        in_specs=[row_out(), row_out(), row_out(), pl.BlockSpec((CH, 512), lambda i: (rev(i), 0))]
        + _proj_specs(CH, (0, 5, 6), rev) + [pl.BlockSpec((1, NS, D), lambda i: (rev(i), 0, 0)),
                                             _full((4, D)), _full((4, 512)),
                                             _full((1, 128)), _full((1, 128)), _full((1, D)), _full((1, D)),
                                             pl.BlockSpec((t_n, fb), lambda i: (0, i)), _full((t_n, D)),
                                             pl.BlockSpec((t_n, fb), lambda i: (0, i)), _full((t_n, D))],
        out_specs=[row_out(), row_out(), row_out(), _full((4, D)), _full((4, 512)), _full((1, D)), _full((1, 512)),
                   _full((1, 128)), _full((1, 128)), _full((1, D)), _full((1, D)),
                   pl.BlockSpec((fb, D), lambda i: (i, 0)), pl.BlockSpec((fb, D), lambda i: (i, 0))],
        out_shape=[jax.ShapeDtypeStruct((t_n, D), MXU)] * 3 + [
            jax.ShapeDtypeStruct(s, F32) for s in ((4, D), (4, 512), (1, D), (1, 512), (1, 128), (1, 128), (1, D), (1, D))]
        + [jax.ShapeDtypeStruct((FF, D), MXU)] * 2,
        scratch_shapes=[pltpu.VMEM((NS, D), F32), pltpu.VMEM((8, D), F32), pltpu.VMEM((8, 512), F32)],
        compiler_params=_cp("arbitrary"),
    )(dya, *saved, *_proj_ops(proj, (0, 5, 6)), sprev, cwx, cwb, dtb, alog, dsk, nrm, *mlp_ops)
    return outs


LRU_ROWS = 256
LRU_BLK = 256


def _lru_gates(xr, wa, wx, ba, bx, lam):
    pr = jnp.concatenate([_nn(xr[:, LRU_BLK * b:LRU_BLK * (b + 1)], wa[b]) for b in range(D // LRU_BLK)], axis=1) + ba
    pi = jnp.concatenate([_nn(xr[:, LRU_BLK * b:LRU_BLK * (b + 1)], wx[b]) for b in range(D // LRU_BLK)], axis=1) + bx
    log_a = -LRU_C * _sigmoid(pr) * _softplus(-lam)
    a = jnp.exp(log_a)
    mult = jnp.sqrt(1.0 - jnp.exp(2.0 * log_a))
    return a, mult * (_sigmoid(pi) * xr)


def _lru_out(h, g):
    return h * jax.nn.gelu(g, approximate=True)


def _lru_fwd(proj, cw, cb, wa, wx, ba, bx, lam):
    t_n = proj[0].shape[0]
    rows = min(LRU_ROWS, t_n)
    fwd = lambda i: i

    def body(g_ref, x_ref, xp_ref, cw_ref, cb_ref, wa_ref, wx_ref, ba_ref, bx_ref, lam_ref, yb_ref, h_ref, xr_ref,
             a_s, u_s, carry):
        i = pl.program_id(0)

        @pl.when(i == 0)
        def _():
            carry[...] = jnp.zeros_like(carry)

        keep = jnp.where(i == 0, 0.0, 1.0)
        xr = _conv_apply(_taps_past(x_ref[...].astype(F32), _prev8(xp_ref) * keep), cw_ref[...], cb_ref[...])
        xr_ref[...] = xr
        a, u = _lru_gates(xr, wa_ref[...], wx_ref[...], ba_ref[...], bx_ref[...], lam_ref[...])
        a_s[...] = a
        u_s[...] = u
        row = lax.broadcasted_iota(jnp.int32, (8, D), 0)

        def blk(b, c):
            s = pl.multiple_of(b * 8, 8)
            av = a_s[pl.ds(s, 8), :]
            uv = u_s[pl.ds(s, 8), :]
            for d in (1, 2, 4):
                m = row >= d
                uv = uv + av * jnp.where(m, pltpu.roll(uv, d, 0), 0.0)
                av = av * jnp.where(m, pltpu.roll(av, d, 0), 1.0)
            hv = uv + av * c
            h_ref[pl.ds(s, 8), :] = hv
            return hv[7:8, :]

        carry[0:1, :] = lax.fori_loop(0, rows // 8, blk, carry[0:1, :])
        yb_ref[...] = _lru_out(h_ref[...], g_ref[...].astype(F32)).astype(yb_ref.dtype)

    return pl.pallas_call(
        body, name="lru_fwd", grid=(t_n // rows,),
        in_specs=_proj_specs(rows, (1, 2), fwd) + _prev_specs(rows, (2,), fwd) + [
            _full((4, D)), _full((1, D)), _full((4, LRU_BLK, LRU_BLK)), _full((4, LRU_BLK, LRU_BLK)),
            _full((1, D)), _full((1, D)), _full((1, D))],
        out_specs=[pl.BlockSpec((rows, D), lambda i: (i, 0))] * 3,
        out_shape=[jax.ShapeDtypeStruct((t_n, D), MXU), jax.ShapeDtypeStruct((t_n, D), F32),
                   jax.ShapeDtypeStruct((t_n, D), F32)],
        scratch_shapes=[pltpu.VMEM((rows, D), F32), pltpu.VMEM((rows, D), F32), pltpu.VMEM((8, D), F32)],
        compiler_params=_cp("arbitrary"),
    )(*_proj_ops(proj, (1, 2, 2)), cw, cb, wa, wx, ba, bx, lam)


def _lru_bwd(dyb, proj, h, xr_saved, cw, wa, wx, ba, bx, lam):
    t_n = proj[0].shape[0]
    rows = min(LRU_ROWS, t_n)
    n_t = t_n // rows
    rev = lambda i: n_t - 1 - i
    rb = rows // 8

    def body(dyb_ref, g_ref, x_ref, h_ref, hp_ref, xr_ref, cw_ref, wa_ref, wx_ref, ba_ref, bx_ref, lam_ref,
             dg_ref, dx_ref, dcw_ref, dcb_ref, dwa_ref, dwx_ref, dba_ref, dbx_ref, dlam_ref,
             a_s, dh_s, hx_s, da_s, du_s, carry, fut):
        i = pl.program_id(0)
        acc_refs = (dcw_ref, dcb_ref, dwa_ref, dwx_ref, dba_ref, dbx_ref, dlam_ref)

        @pl.when(i == 0)
        def _():
            for r in (carry, fut) + acc_refs:
                r[...] = jnp.zeros_like(r)

        keep = jnp.where(i == n_t - 1, 0.0, 1.0)
        gate_in = (xr_ref[...], wa_ref[...], wx_ref[...], ba_ref[...], bx_ref[...], lam_ref[...])
        (a, _), gates_vjp = jax.vjp(_lru_gates, *gate_in)
        _, out_vjp = jax.vjp(_lru_out, h_ref[...], g_ref[...].astype(F32))
        dh, dg = out_vjp(dyb_ref[...])
        dg_ref[...] = dg.astype(dg_ref.dtype)
        a_s[...] = a
        dh_s[...] = dh
        hx_s[0:8, :] = hp_ref[...] * keep
        hx_s[8:, :] = h_ref[...]
        row = lax.broadcasted_iota(jnp.int32, (8, D), 0)

        def blk(b, c):
            s = pl.multiple_of((rb - 1 - b) * 8, 8)
            av = a_s[pl.ds(s, 8), :]
            dhv = dh_s[pl.ds(s, 8), :]
            kv = av * dhv
            for d in (1, 2, 4):
                m = row <= 7 - d
                kv = kv + av * jnp.where(m, pltpu.roll(kv, 8 - d, 0), 0.0)
                av = av * jnp.where(m, pltpu.roll(av, 8 - d, 0), 1.0)
            kv = kv + av * c
            gv = dhv + jnp.where(row < 7, pltpu.roll(kv, 7, 0), c)
            hb = hx_s[pl.ds(s + 8, 8), :]
            hpv = hx_s[pl.ds(s, 8), :]
            hprev = jnp.where(row >= 1, pltpu.roll(hb, 1, 0), hpv[7:8, :])
            du_s[pl.ds(s, 8), :] = gv
            da_s[pl.ds(s, 8), :] = gv * hprev
            return kv[0:1, :]

        carry[0:1, :] = lax.fori_loop(0, rb, blk, carry[0:1, :])
        dxr, dwa, dwx, dba, dbx, dlam = gates_vjp((da_s[...], du_s[...]))
        dwa_ref[...] += dwa
        dwx_ref[...] += dwx
        dba_ref[...] += dba
        dbx_ref[...] += dbx
        dlam_ref[...] += dlam
        dcb_ref[...] += _rowsum(dxr)
        ft = _taps_future(dxr, fut[...])
        x_in = x_ref[...].astype(F32)
        for k in range(4):
            dcw_ref[k:k + 1, :] += _rowsum(ft[k] * x_in)
        cwv = cw_ref[...]
        dx = ft[0] * cwv[0:1, :]
        for k in range(1, 4):
            dx = dx + ft[k] * cwv[k:k + 1, :]
        fut[...] = dxr[0:8, :]
        dx_ref[...] = dx.astype(dx_ref.dtype)

    row_in = lambda: pl.BlockSpec((rows, D), lambda i: (rev(i), 0))
    prev_h = pl.BlockSpec((8, D), lambda i: (jnp.maximum(rev(i) * rb - 1, 0), 0))
    wspec = lambda: _full((4, LRU_BLK, LRU_BLK))
    return pl.pallas_call(
        body, name="lru_bwd", grid=(n_t,),
        in_specs=[row_in()] + _proj_specs(rows, (1, 2), rev) + [row_in(), prev_h, row_in()] + [
            _full((4, D)), wspec(), wspec(), _full((1, D)), _full((1, D)), _full((1, D))],
        out_specs=[row_in(), row_in(), _full((4, D)), _full((1, D)), wspec(), wspec(), _full((1, D)), _full((1, D)),
                   _full((1, D))],
        out_shape=[jax.ShapeDtypeStruct((t_n, D), MXU)] * 2 + [
            jax.ShapeDtypeStruct(s, F32) for s in ((4, D), (1, D), (4, LRU_BLK, LRU_BLK), (4, LRU_BLK, LRU_BLK),
                                                   (1, D), (1, D), (1, D))],
        scratch_shapes=[pltpu.VMEM((rows, D), F32), pltpu.VMEM((rows, D), F32), pltpu.VMEM((rows + 8, D), F32),
                        pltpu.VMEM((rows, D), F32), pltpu.VMEM((rows, D), F32), pltpu.VMEM((8, D), F32),
                        pltpu.VMEM((8, D), F32)],
        compiler_params=_cp("arbitrary"),
    )(dyb, *_proj_ops(proj, (1, 2)), h, h, xr_saved, cw, wa, wx, ba, bx, lam)


def _merge_out(ya, yb, proj, x, wout, g1):
    t_n = x.shape[0]
    tm = min(t_n, 512)

    def body(ya_ref, yb_ref, ga_ref, gb_ref, x_ref, w_ref, g_ref, h1_ref, mix_ref, mg_ref):
        merged = (_sigmoid(ga_ref[...].astype(F32)) * ya_ref[...].astype(F32)
                  + _sigmoid(gb_ref[...].astype(F32)) * yb_ref[...].astype(F32))
        mg = merged.astype(MXU)
        mg_ref[...] = mg
        mix = jnp.dot(mg, w_ref[...], preferred_element_type=F32)
        mix_ref[...] = mix
        h1_ref[...] = x_ref[...] + _rms(mix, g_ref[...])

    row = lambda: pl.BlockSpec((tm, D), lambda i: (i, 0))
    return pl.pallas_call(
        body, name="merge_out", grid=(t_n // tm,),
        in_specs=[row(), row()] + _proj_specs(tm, (3, 4), lambda i: i) + [row(), _full((D, D)), _full((1, D))],
        out_specs=[row(), row(), row()],
        out_shape=[jax.ShapeDtypeStruct((t_n, D), F32), jax.ShapeDtypeStruct((t_n, D), F32),
                   jax.ShapeDtypeStruct((t_n, D), MXU)],
        compiler_params=_cp("parallel"),
    )(ya, yb, *_proj_ops(proj, (3, 4)), x, wout, g1)


def _out_bwd(dv, h1, g2, dout, mix, ya, yb, proj, wout, g1, w_in_pt):
    t_n = dv.shape[0]
    tm = min(t_n, 256)

    def body(dv_ref, h1_ref, g2_ref, dout_ref, mix_ref, ya_ref, yb_ref, ga_ref, gb_ref, w_ref, g_ref, wa_ref, wb_ref,
             dh1_ref, dmix_ref, dya_ref, dyb_ref, dga_ref, dgb_ref, dg2_ref, dg1_ref, dug_ref):
        @pl.when(pl.program_id(0) == 0)
        def _():
            dg1_ref[...] = jnp.zeros_like(dg1_ref)
            dg2_ref[...] = jnp.zeros_like(dg2_ref)

        dx, dg_rows = _rms_bwd(h1_ref[...], g2_ref[...], dv_ref[...])
        dg2_ref[...] += _rowsum(dg_rows)
        dh1 = dout_ref[...] + dx
        dh1_ref[...] = dh1
        dmix, dg_rows = _rms_bwd(mix_ref[...], g_ref[...], dh1)
        dg1_ref[...] += _rowsum(dg_rows)
        dmix_b = dmix.astype(MXU)
        dmix_ref[...] = dmix_b
        dmg = lax.dot_general(dmix_b, w_ref[...], (((1,), (1,)), ((), ())), preferred_element_type=F32)
        sa = _sigmoid(ga_ref[...].astype(F32))
        sb = _sigmoid(gb_ref[...].astype(F32))
        dya_ref[...] = dmg * sa
        dyb_ref[...] = dmg * sb
        dga = (dmg * ya_ref[...].astype(F32) * sa * (1.0 - sa)).astype(MXU)
        dga_ref[...] = dga
        dug = jnp.dot(dga, wa_ref[...], preferred_element_type=F32)
        dgb = (dmg * yb_ref[...].astype(F32) * sb * (1.0 - sb)).astype(MXU)
        dgb_ref[...] = dgb
        dug_ref[...] = dug + jnp.dot(dgb, wb_ref[...], preferred_element_type=F32)

    row = lambda: pl.BlockSpec((tm, D), lambda i: (i, 0))
    vec = lambda: _full((1, D))
    seg = lambda s: pl.BlockSpec((SEG, D), lambda i: (s, 0))
    return pl.pallas_call(
        body, name="out_bwd", grid=(t_n // tm,),
        in_specs=[row(), row(), vec(), row(), row(), row(), row()] + _proj_specs(tm, (3, 4), lambda i: i)
        + [_full((D, D)), vec(), seg(3), seg(4)],
        out_specs=[row(), row(), row(), row(), row(), row(), vec(), vec(), row()],
        out_shape=[jax.ShapeDtypeStruct((t_n, D), F32), jax.ShapeDtypeStruct((t_n, D), MXU),
                   jax.ShapeDtypeStruct((t_n, D), F32), jax.ShapeDtypeStruct((t_n, D), F32),
                   jax.ShapeDtypeStruct((t_n, D), MXU), jax.ShapeDtypeStruct((t_n, D), MXU),
                   jax.ShapeDtypeStruct((1, D), F32), jax.ShapeDtypeStruct((1, D), F32),
                   jax.ShapeDtypeStruct((t_n, D), F32)],
        compiler_params=_cp("arbitrary"),
    )(dv, h1, g2, dout, mix, ya, yb, *_proj_ops(proj, (3, 4)), wout, g1, w_in_pt, w_in_pt)


MLP_TM = 1024
MLP_TF_FWD = 1024
MLP_TF_BWD = 1024


def _mlp_fwd(h1, g2, wup, wdown, g3, tgt):
    t_n = h1.shape[0]
    tm = min(t_n, MLP_TM)
    n_f = FF // MLP_TF_FWD

    def body(h1_ref, g2_ref, wu_ref, wd_ref, g3_ref, tgt_ref, hp_ref, v_ref, dout_ref, dff_ref, loss_ref, dg3_ref, acc):
        i, j = pl.program_id(0), pl.program_id(1)

        @pl.when((i == 0) & (j == 0))
        def _():
            loss_ref[...] = jnp.zeros_like(loss_ref)
            dg3_ref[...] = jnp.zeros_like(dg3_ref)

        @pl.when(j == 0)
        def _():
            v_ref[...] = _rms(h1_ref[...], g2_ref[...]).astype(MXU)
            acc[...] = jnp.zeros_like(acc)

        hp = lax.dot_general(v_ref[...], wu_ref[...], (((1,), (1,)), ((), ())), preferred_element_type=F32)
        hp_ref[...] = hp.astype(MXU)
        hid = jnp.square(jnp.maximum(hp, 0.0))
        acc[...] += jnp.dot(hid.astype(MXU), wd_ref[...], preferred_element_type=F32)

        @pl.when(j == n_f - 1)
        def _():
            ff = acc[...]
            err = h1_ref[...] + _rms(ff, g3_ref[...]) - tgt_ref[...]
            loss_ref[...] += 0.5 * jnp.sum(jnp.mean(err * err, axis=-1, keepdims=True), axis=0, keepdims=True)
            dout = err * (1.0 / D)
            dout_ref[...] = dout
            dff, dg_rows = _rms_bwd(ff, g3_ref[...], dout)
            dg3_ref[...] += _rowsum(dg_rows)
            dff_ref[...] = dff.astype(MXU)

    row = lambda: pl.BlockSpec((tm, D), lambda i, j: (i, 0))
    vec = lambda: pl.BlockSpec((1, D), lambda i, j: (0, 0))
    return pl.pallas_call(
        body, name="mlp_fwd", grid=(t_n // tm, n_f),
        in_specs=[row(), vec(), pl.BlockSpec((MLP_TF_FWD, D), lambda i, j: (j, 0)),
                  pl.BlockSpec((MLP_TF_FWD, D), lambda i, j: (j, 0)), vec(), row()],
        out_specs=[pl.BlockSpec((tm, MLP_TF_FWD), lambda i, j: (i, j)), row(), row(), row(),
                   pl.BlockSpec((1, 1), lambda i, j: (0, 0)), vec()],
        out_shape=[jax.ShapeDtypeStruct((t_n, FF), MXU), jax.ShapeDtypeStruct((t_n, D), MXU),
                   jax.ShapeDtypeStruct((t_n, D), F32), jax.ShapeDtypeStruct((t_n, D), MXU),
                   jax.ShapeDtypeStruct((1, 1), F32), jax.ShapeDtypeStruct((1, D), F32)],
        scratch_shapes=[pltpu.VMEM((tm, D), F32)],
        compiler_params=_cp("arbitrary", "arbitrary"),
    )(h1, g2, wup, wdown, g3, tgt)


def _mlp_bwd(dff, hp, wup, wdown):
    t_n = dff.shape[0]
    tm = min(t_n, MLP_TM)
    n_f = FF // MLP_TF_BWD

    def mm_body(dff_ref, hp_ref, wu_ref, wd_ref, dv_ref, dhp_ref, hid_ref):
        @pl.when(pl.program_id(1) == 0)
        def _():
            dv_ref[...] = jnp.zeros_like(dv_ref)

        relu = jnp.maximum(hp_ref[...].astype(F32), 0.0)
        hid_ref[...] = jnp.square(relu).astype(MXU)
        dhid = lax.dot_general(dff_ref[...], wd_ref[...], (((1,), (1,)), ((), ())), preferred_element_type=F32)
        dhp = (dhid * (2.0 * relu)).astype(MXU)
        dhp_ref[...] = dhp
        dv_ref[...] += jnp.dot(dhp, wu_ref[...], preferred_element_type=F32)

    row = lambda: pl.BlockSpec((tm, D), lambda i, j: (i, 0))
    blk = lambda: pl.BlockSpec((tm, MLP_TF_BWD), lambda i, j: (i, j))
    wblk = lambda: pl.BlockSpec((MLP_TF_BWD, D), lambda i, j: (j, 0))
    return pl.pallas_call(
        mm_body, name="mlp_bwd", grid=(t_n // tm, n_f),
        in_specs=[row(), blk(), wblk(), wblk()], out_specs=[row(), blk(), blk()],
        out_shape=[jax.ShapeDtypeStruct((t_n, D), F32), jax.ShapeDtypeStruct((t_n, FF), MXU),
                   jax.ShapeDtypeStruct((t_n, FF), MXU)],
        compiler_params=_cp("parallel", "arbitrary"),
    )(dff, hp, wup, wdown)


def _wgrad(a, g, name):
    t_n, k_n = a.shape
    n_n = g.shape[1]
    tt = min(t_n, 1024)
    tk, tn = min(k_n, 1024), min(n_n, 1024)

    n_t = t_n // tt

    def body(a_ref, g_ref, o_ref, acc):
        t = pl.program_id(2)

        @pl.when(t == 0)
        def _():
            acc[...] = jnp.zeros_like(acc)

        acc[...] += lax.dot_general(a_ref[...], g_ref[...], (((0,), (0,)), ((), ())), preferred_element_type=F32)

        @pl.when(t == n_t - 1)
        def _():
            o_ref[...] = acc[...].astype(o_ref.dtype)

    return pl.pallas_call(
        body, name=name, grid=(k_n // tk, n_n // tn, n_t),
        in_specs=[pl.BlockSpec((tt, tk), lambda k, n, t: (t, k)), pl.BlockSpec((tt, tn), lambda k, n, t: (t, n))],
        out_specs=pl.BlockSpec((tk, tn), lambda k, n, t: (k, n)),
        out_shape=jax.ShapeDtypeStruct((k_n, n_n), MXU),
        scratch_shapes=[pltpu.VMEM((tk, tn), F32)],
        compiler_params=_cp("parallel", "parallel", "arbitrary"),
    )(a, g)


def _wgrad_segs(segs, g, name):
    t_n, n_n = g.shape
    n_s = len(segs)
    tt = min(t_n, 1024)
    n_t = t_n // tt

    def body(*refs):
        a_refs = refs[:n_s]
        g_ref, o_ref, acc = refs[n_s:]
        s_id, t = pl.program_id(0), pl.program_id(1)

        @pl.when(t == 0)
        def _():
            acc[...] = jnp.zeros_like(acc)

        for s in range(n_s):
            @pl.when(s_id == s)
            def _(s=s):
                acc[...] += lax.dot_general(a_refs[s][...], g_ref[...], (((0,), (0,)), ((), ())),
                                            preferred_element_type=F32)

        @pl.when(t == n_t - 1)
        def _():
            o_ref[...] = acc[...].astype(o_ref.dtype)

    seg_spec = lambda s: pl.BlockSpec((tt, SEG), lambda i, t: (jnp.where(i == s, t, jnp.where(i < s, 0, n_t - 1)), 0))
    return pl.pallas_call(
        body, name=name, grid=(n_s, n_t),
        in_specs=[seg_spec(s) for s in range(n_s)] + [pl.BlockSpec((tt, n_n), lambda i, t: (t, 0))],
        out_specs=pl.BlockSpec((SEG, n_n), lambda i, t: (i, 0)),
        out_shape=jax.ShapeDtypeStruct((n_s * SEG, n_n), MXU),
        scratch_shapes=[pltpu.VMEM((SEG, n_n), F32)],
        compiler_params=_cp("arbitrary", "arbitrary"),
    )(*segs, g)


def _inproj_bwd(dsegs, seg_ids, du_part, w, x, g0, dh1):
    t_n = x.shape[0]
    tm = min(t_n, 512)
    n_k = len(dsegs)

    def body(*refs):
        dp_refs = refs[:n_k]
        w_refs = refs[n_k:2 * n_k]
        part_ref, x_ref, g_ref, dh1_ref, dx_ref, dg0_ref = refs[2 * n_k:]

        @pl.when(pl.program_id(0) == 0)
        def _():
            dg0_ref[...] = jnp.zeros_like(dg0_ref)

        du = part_ref[...]
        for s in range(n_k):
            du = du + jnp.dot(dp_refs[s][...], w_refs[s][...], preferred_element_type=F32)
        dx, dg_rows = _rms_bwd(x_ref[...], g_ref[...], du)
        dg0_ref[...] += _rowsum(dg_rows)
        dx_ref[...] = dh1_ref[...] + dx

    row = lambda: pl.BlockSpec((tm, D), lambda i: (i, 0))
    return pl.pallas_call(
        body, name="inproj_bwd", grid=(t_n // tm,),
        in_specs=[row() for _ in range(n_k)]
        + [pl.BlockSpec((SEG, D), functools.partial(lambda i, s: (s, 0), s=s)) for s in seg_ids]
        + [row(), row(), _full((1, D)), row()],
        out_specs=[row(), _full((1, D))],
        out_shape=[jax.ShapeDtypeStruct((t_n, D), F32), jax.ShapeDtypeStruct((1, D), F32)],
        compiler_params=_cp("arbitrary"),
    )(*dsegs, *([w] * n_k), du_part, x, g0, dh1)


def _blockdiag4(w):
    w4 = w.reshape(4, 4, 64, 1, 64).astype(MXU)
    same = (jnp.arange(4)[:, None, None, None] == jnp.arange(4)[None, None, :, None])
    return jnp.where(same[None], w4, jnp.zeros((), MXU)).reshape(4, 256, 256)


def _blockdiag4_extract(g):
    g5 = g.reshape(4, 4, 64, 4, 64)
    return jnp.stack([g5[:, q, :, q, :] for q in range(4)], axis=1).reshape(NH, 64, 64)


def _local_step(x, tgt, p, after_ssd=None, late_weights=None, send_mlp_grads=None, send_late_grads=None):
    f = lambda a: a.astype(F32)
    proj, u = _inproj(x, p["norm_mix_pre"], p["w_in_pT"])
    ssm_params = (p["cw_xs"], p["cw_bc"], p["cb_xs"], p["cb_bc"], p["dt_bias"], p["a_log"], p["d_skip_x"],
                  p["ssm_norm"])
    ya, sprev, *ssd_saved = _ssd_fwd(proj, *ssm_params)
    cb_lru = p["conv_lru_b"] if after_ssd is None else p["conv_lru_b"] + after_ssd(ya)
    lru_params = (p["conv_lru_w"], cb_lru, p["wa_bd"], p["wx_bd"], p["lru_ba"], p["lru_bx"], p["lru_lambda"])
    yb, h, xr = _lru_fwd(proj, *lru_params)
    if late_weights is not None:
        p = dict(p, **late_weights(yb))
    h1, mix, merged = _merge_out(ya, yb, proj, x, p["w_out"], p["norm_mix_post"])
    hp, v, dout, dff, loss, dg3 = _mlp_fwd(h1, p["norm_mlp_pre"], p["w_upT"], p["w_down"], p["norm_mlp_post"], tgt)

    dv, dhp, hid = _mlp_bwd(dff, hp, p["w_upT"], p["w_down"])
    dh1, dmix, dya, dyb, dga, dgb, dg2, dg1, du_gates = _out_bwd(
        dv, h1, p["norm_mlp_pre"], dout, mix, ya, yb, proj, p["w_out"], p["norm_mix_post"], p["w_in_pT"])
    d_w_out = _wgrad(merged, dmix, "wgrad_out")
    (dz, dxs, dbc, dcwx, dcwb, dcbx, dcbb, ddtb, dalog, ddsk, dnrm, d_w_down, d_w_up_t) = _ssd_bwd(
        dya, ssd_saved, proj, sprev, p["cw_xs"], p["cw_bc"], *ssm_params[4:], (hid, dff, dhp, v))
    ba_lru = p["lru_ba"] if send_mlp_grads is None else p["lru_ba"] + send_mlp_grads(d_w_up_t, d_w_down, d_w_out)
    (dgl, dxl, dcwl, dcbl, dwa, dwx, dba, dbx, dlam) = _lru_bwd(
        dyb, proj, h, xr, p["conv_lru_w"], p["wa_bd"], p["wx_bd"], ba_lru, p["lru_bx"], p["lru_lambda"])
    dsegs = [dz, dgl, dxl, dga, dgb, dxs, dbc]
    d_w_in_pt = _wgrad_segs(dsegs, u, "wgrad_in")
    g0 = p["norm_mix_pre"]
    if send_late_grads is not None:
        g0 = g0 + send_late_grads(d_w_in_pt)
    grad_x, dg0 = _inproj_bwd([dz, dgl, dxl, dxs, dbc], (0, 1, 2, 5, 6), du_gates, p["w_in_pT"], x, g0, dh1)
    grads = dict(
        norm_mix_pre=dg0, w_in_pT=d_w_in_pt, conv_ssm_w=jnp.concatenate([dcwx, dcwb], axis=1),
        conv_ssm_b=jnp.concatenate([dcbx, dcbb], axis=1), dt_bias=ddtb[:, :NH], a_log=dalog[:, :NH],
        d_skip=f(ddsk).reshape(NH, 64).sum(axis=1)[None, :], ssm_norm=dnrm, conv_lru_w=dcwl, conv_lru_b=dcbl,
        lru_wa=_blockdiag4_extract(dwa), lru_ba=dba, lru_wx=_blockdiag4_extract(dwx), lru_bx=dbx, lru_lambda=dlam,
        w_out=d_w_out, norm_mix_post=dg1, norm_mlp_pre=dg2, w_upT=d_w_up_t, w_down=d_w_down, norm_mlp_post=dg3)
    return loss[0, 0], grad_x, grads


W_IN_COLS = 6672


def _w_in_t_to_padded(wt):
    z, xs, bc, dt = wt[0:1024], wt[1024:2048], wt[2048:2560], wt[2560:2576]
    gl, xl, ga, gb = wt[2576:3600], wt[3600:4624], wt[4624:5648], wt[5648:6672]
    return jnp.concatenate([z, gl, xl, ga, gb, xs, bc, dt, jnp.zeros((NP - 6672, wt.shape[1]), wt.dtype)], axis=0)


def _w_in_t_from_padded(wp):
    z, gl, xl, ga, gb = (wp[SEG * s:SEG * (s + 1)] for s in range(5))
    xs, bc, dt = wp[5120:6144], wp[6144:6656], wp[6656:6672]
    return jnp.concatenate([z, xs, bc, dt, gl, xl, ga, gb], axis=0)


def _prep_params(full, big):
    f = lambda a: a.astype(F32)
    pad128 = lambda a: jnp.pad(f(a).reshape(1, -1), ((0, 0), (0, 128 - a.size)))
    cw = f(full["conv_ssm_w"])
    cb = f(full["conv_ssm_b"]).reshape(1, -1)
    return dict(
        big, norm_mix_pre=f(full["norm_mix_pre"]).reshape(1, D),
        cw_xs=cw[:, :D], cw_bc=cw[:, D:], cb_xs=cb[:, :D], cb_bc=cb[:, D:],
        dt_bias=pad128(full["dt_bias"]), a_log=pad128(full["a_log"]),
        d_skip_x=jnp.repeat(f(full["d_skip"]).reshape(-1), 64).reshape(1, D), ssm_norm=f(full["ssm_norm"]).reshape(1, D),
        conv_lru_w=f(full["conv_lru_w"]), conv_lru_b=f(full["conv_lru_b"]).reshape(1, D),
        wa_bd=_blockdiag4(full["lru_wa"]), wx_bd=_blockdiag4(full["lru_wx"]),
        lru_ba=f(full["lru_ba"]).reshape(1, D), lru_bx=f(full["lru_bx"]).reshape(1, D),
        lru_lambda=f(full["lru_lambda"]).reshape(1, D),
        norm_mix_post=f(full["norm_mix_post"]).reshape(1, D), norm_mlp_pre=f(full["norm_mlp_pre"]).reshape(1, D),
        norm_mlp_post=f(full["norm_mlp_post"]).reshape(1, D))


MESH_ID = pl.DeviceIdType.MESH
ANY = pl.BlockSpec(memory_space=pl.ANY)


def _my_place():
    x, y, c = lax.axis_index("x"), lax.axis_index("y"), lax.axis_index("c")
    return x, y, c, 4 * x + 2 * y + c


def _peer(x, y, c, k):
    return (x ^ ((k >> 2) & 1), y ^ ((k >> 1) & 1), c ^ (k & 1))


def _all_gather(pack, name):
    r_n = pack.shape[0]
    half = -(-r_n // 32) * 16
    n_cp = 9

    def body(in_ref, out_ref, send_sems, recv_sems, local_sem):
        x, y, c, me = _my_place()
        here, sibling, x_nbr, y_nbr = (x, y, c), (x, y, 1 - c), (1 - x, y, c), (x, 1 - y, c)
        part = {"all": pl.ds(0, r_n), "lo": pl.ds(0, half), "hi": pl.ds(half, r_n - half)}

        def copy(j, block, rows, to, src=None):
            return pltpu.make_async_remote_copy(
                src_ref=out_ref.at[block, part[rows]] if src is None else src, dst_ref=out_ref.at[block, part[rows]],
                send_sem=send_sems.at[j], recv_sem=recv_sems.at[j], device_id=to, device_id_type=MESH_ID)

        mine = pltpu.make_async_copy(in_ref, out_ref.at[me], local_sem)
        mine.start()
        first = [copy(0, me, "all", sibling, src=in_ref), copy(1, me, "all", x_nbr, src=in_ref),
                 copy(2, me, "all", y_nbr, src=in_ref)]
        for cp in first:
            cp.start()
        relay = [[(3, me ^ 4, "lo", y_nbr), (5, me ^ 4, "all", sibling)],
                 [(4, me ^ 2, "hi", x_nbr), (6, me ^ 2, "all", sibling)],
                 [(7, me ^ 6, "lo", sibling)],
                 [(8, me ^ 6, "hi", sibling)]]
        landed = [(1, me ^ 4, "all"), (2, me ^ 2, "all"), (3, me ^ 6, "lo"), (4, me ^ 6, "hi")]
        passed = []
        for (j, block, rows), nxt in zip(landed, relay):
            copy(j, block, rows, here).wait_recv()
            for args in nxt:
                cp = copy(*args)
                cp.start()
                passed.append(cp)
        for j, block, rows in ((0, me ^ 1, "all"), (5, me ^ 5, "all"), (6, me ^ 3, "all"), (7, me ^ 7, "lo"),
                               (8, me ^ 7, "hi")):
            copy(j, block, rows, here).wait_recv()
        for cp in first + passed:
            cp.wait_send()
        mine.wait()

    return pl.pallas_call(
        body, name=name, in_specs=[ANY], out_specs=ANY,
        out_shape=jax.ShapeDtypeStruct((N_DEV,) + pack.shape, pack.dtype),
        scratch_shapes=[pltpu.SemaphoreType.DMA((n_cp,)), pltpu.SemaphoreType.DMA((n_cp,)), pltpu.SemaphoreType.DMA],
    )(pack)


HBM = pl.BlockSpec(memory_space=pltpu.HBM)
SEM = pl.BlockSpec(memory_space=pltpu.SEMAPHORE)
PLAN_GATHER = tuple((k, "pack", 0) for k in range(1, N_DEV))
PLAN_SCATTER = tuple((k, "slot", 0) for k in range(1, N_DEV))
PLAN_GATHER_ICI = tuple((k, "pack", 0) for k in (2, 4, 6))
PLAN_GATHER_D2D = ((1, "pack", 0),) + tuple((1, s, s) for s in (2, 4, 6))


def _plan_copy(j, plan, src_ref, land_ref, sems):
    k, source, slot = plan[j]
    x, y, c, me = _my_place()
    if source == "pack":
        src = src_ref
    elif source == "slot":
        src = src_ref.at[me ^ k]
    else:
        src = land_ref.at[me ^ source]
    return pltpu.make_async_remote_copy(
        src_ref=src, dst_ref=land_ref.at[me ^ slot], send_sem=sems[j], recv_sem=sems[len(plan) + j],
        device_id=_peer(x, y, c, k), device_id_type=MESH_ID)


def _exchange_start(src, land, plan, name):
    n_c = len(plan)
    if land is None:
        land = pltpu.with_memory_space_constraint(lax.empty((N_DEV,) + src.shape[-2:], src.dtype), pltpu.HBM)

    def body(src_ref, land_ref, *rest):
        sems, token = rest[:2 * n_c], rest[2 * n_c + 2]
        for j in range(n_c):
            _plan_copy(j, plan, src_ref, land_ref, sems).start()
        token[...] = jnp.zeros_like(token)

    outs = pl.pallas_call(
        body, name=name,
        out_shape=(pltpu.SemaphoreType.DMA(()),) * (2 * n_c) + (
            pltpu.HBM(src.shape, src.dtype), pltpu.HBM(land.shape, land.dtype), jax.ShapeDtypeStruct((8, 128), F32)),
        in_specs=(HBM, HBM), out_specs=(SEM,) * (2 * n_c) + (HBM, HBM, pl.BlockSpec(memory_space=pltpu.VMEM)),
        input_output_aliases={0: 2 * n_c, 1: 2 * n_c + 1},
        compiler_params=pltpu.CompilerParams(has_side_effects=pltpu.SideEffectType.DATAFLOW_SIDE_EFFECTING),
    )(pltpu.with_memory_space_constraint(src, pltpu.HBM), land)
    return outs[:2 * n_c], outs[2 * n_c], outs[2 * n_c + 1], outs[2 * n_c + 2]


def _exchange_wait(sems, src_thru, land_thru, after, plan, name):
    n_c = len(plan)

    def body(src_ref, land_ref, *rest):
        for j in range(n_c):
            cp = _plan_copy(j, plan, src_ref, land_ref, rest[:2 * n_c])
            cp.wait_send()
            cp.wait_recv()

    return pl.pallas_call(
        body, name=name,
        out_shape=(pltpu.HBM(src_thru.shape, src_thru.dtype), pltpu.HBM(land_thru.shape, land_thru.dtype)),
        in_specs=(HBM, HBM) + (SEM,) * (2 * n_c) + (pl.BlockSpec(memory_space=pl.ANY),), out_specs=(HBM, HBM),
        input_output_aliases={0: 0, 1: 1},
        compiler_params=pltpu.CompilerParams(has_side_effects=pltpu.SideEffectType.DATAFLOW_SIDE_EFFECTING),
    )(src_thru, land_thru, *sems, after)


def _slot_sum(parts, name):
    r_n, c_n = parts.shape[1:]
    tr = max(t for t in range(16, 513, 16) if r_n % t == 0)

    def body(p_ref, o_ref):
        acc = p_ref[0].astype(F32)
        for k in range(1, N_DEV):
            acc = acc + p_ref[k].astype(F32)
        o_ref[...] = acc

    return pl.pallas_call(
        body, name=name, grid=(r_n // tr,),
        in_specs=[pl.BlockSpec((N_DEV, tr, c_n), lambda i: (0, i, 0))],
        out_specs=pl.BlockSpec((tr, c_n), lambda i: (i, 0)),
        out_shape=jax.ShapeDtypeStruct((r_n, c_n), F32),
        compiler_params=_cp("parallel"),
    )(parts)


def _adam_math(w, g, m, v):
    m = ADAM_B1 * m + (1.0 - ADAM_B1) * g
    v = ADAM_B2 * v + (1.0 - ADAM_B2) * jnp.square(g)
    m_hat = m / (1.0 - ADAM_B1 ** ADAM_STEP)
    v_hat = v / (1.0 - ADAM_B2 ** ADAM_STEP)
    return -ADAM_LR * (m_hat / (jnp.sqrt(v_hat) + ADAM_EPS) + ADAM_WD * w), m, v


def _adam_big(w, g, m, v, name):
    def body(w_ref, g_ref, m_ref, v_ref, d_ref, mo_ref, vo_ref):
        d_ref[...], mo_ref[...], vo_ref[...] = _adam_math(w_ref[...], g_ref[...], m_ref[...], v_ref[...])

    if w.ndim == 3:
        _, r_n, c_n = w.shape
        tr = min(r_n, 256)
        grid = (r_n // tr,)
        blk = lambda: pl.BlockSpec((1, tr, c_n), lambda i: (0, i, 0))
    else:
        r_n, c_n = w.shape
        tc = min(c_n, 256)
        grid = (c_n // tc,)
        blk = lambda: pl.BlockSpec((r_n, tc), lambda i: (0, i))
    return pl.pallas_call(
        body, name=name, grid=grid, in_specs=[blk(), blk(), blk(), blk()], out_specs=[blk(), blk(), blk()],
        out_shape=[jax.ShapeDtypeStruct(w.shape, F32)] * 3, compiler_params=_cp("parallel"),
    )(w, g, m, v)


def _adam_small(groups, where, wmv):
    n, n_g = len(wmv), len(groups)

    def body(*refs):
        g_refs = refs[:n_g]
        w_refs = refs[n_g:n_g + 3 * n]
        o_refs = refs[n_g + 3 * n:]
        for q in range(n):
            w_ref, m_ref, v_ref = w_refs[3 * q:3 * q + 3]
            r, c = w_ref.shape
            gi, r0 = where[q]
            g = g_refs[gi][0, r0:r0 + r, 0:c]
            for k in range(1, N_DEV):
                g = g + g_refs[gi][k, r0:r0 + r, 0:c]
            d, m, v = _adam_math(w_ref[...], g, m_ref[...], v_ref[...])
            o_refs[4 * q][...] = g
            o_refs[4 * q + 1][...] = d
            o_refs[4 * q + 2][...] = m
            o_refs[4 * q + 3][...] = v

    flat_wmv = [a for t in wmv for a in t]
    vm = pl.BlockSpec(memory_space=pltpu.VMEM)
    outs = pl.pallas_call(
        body, name="adam_small", in_specs=[vm] * (n_g + 3 * n), out_specs=[vm] * (4 * n),
        out_shape=[jax.ShapeDtypeStruct(t[0].shape, F32) for t in wmv for _ in range(4)],
        compiler_params=pltpu.CompilerParams(vmem_limit_bytes=VMEM_LIMIT),
    )(*groups, *flat_wmv)
    return [tuple(outs[4 * q:4 * q + 4]) for q in range(n)]


WEIGHTS = ["norm_mix_pre", "w_in", "conv_ssm_w", "conv_ssm_b", "dt_bias", "a_log", "d_skip", "ssm_norm", "conv_lru_w",
           "conv_lru_b", "lru_wa", "lru_ba", "lru_wx", "lru_bx", "lru_lambda", "w_out", "norm_mix_post", "norm_mlp_pre",
           "w_up", "w_down", "norm_mlp_post"]
BIG = ["w_out", "w_up", "w_down", "w_in"]
IN_ROWS = W_IN_COLS // N_DEV
IN_PAD, EARLY_ROWS = 848, 880
ROW_UP, ROW_DOWN, LATE_ROWS = 128, 640, 1152
GRAD_LATE_ROWS = 864
CONV_SSM_COLS, CONV_LRU_COLS = 1536 // N_DEV, D // N_DEV
SMALL = [("norm_mix_pre", (1, D), 0, 0), ("ssm_norm", (1, D), 0, 1), ("conv_lru_b", (1, D), 0, 2),
         ("lru_lambda", (1, D), 0, 3), ("norm_mix_post", (1, D), 0, 4), ("norm_mlp_pre", (1, D), 0, 5),
         ("norm_mlp_post", (1, D), 0, 6), ("conv_ssm_b", (1, 1536), 1, 0), ("dt_bias", (1, NH), 2, 0),
         ("a_log", (1, NH), 2, 1), ("d_skip", (1, NH), 2, 2), ("conv_ssm_w", (4, CONV_SSM_COLS), 3, 0),
         ("conv_lru_w", (4, CONV_LRU_COLS), 4, 0), ("lru_wa", (D, 64), 5, 0), ("lru_wx", (D, 64), 5, D),
         ("lru_ba", (NH, 64), 6, 0), ("lru_bx", (NH, 64), 6, NH)]
SMALL_GROUPS = [(8, D), (1, 1536), (8, 128), (4, 1536), (4, D), (2 * D, 64), (2 * NH, 64)]


def _pad_rows(flat, mult):
    n = flat.shape[0]
    rows = -(-n // (128 * mult)) * mult
    return jnp.pad(flat, (0, rows * 128 - n)).reshape(rows, 128)


def _split3(a):
    hi = a.astype(MXU)
    r1 = a - hi.astype(F32)
    mid = r1.astype(MXU)
    lo = (r1 - mid.astype(F32)).astype(MXU)
    return jnp.stack([hi, mid, lo])


def _early_pack(a, me):
    bf = lambda t: t.astype(MXU)
    conv = lambda t, c: jnp.pad(_split3(t).reshape(12, c), ((0, 4), (0, D - c)))
    shifted = lax.dynamic_update_slice(jnp.zeros((IN_PAD, D), MXU), bf(a["w_in"][0]).T, (2 * me, 0))
    return jnp.concatenate([shifted, conv(a["conv_ssm_w"][0], CONV_SSM_COLS), conv(a["conv_lru_w"][0], CONV_LRU_COLS)],
                           axis=0)


TILE = 16
SHARD_TILES = IN_PAD // TILE
SHARD_STEP = (IN_ROWS // TILE)
SEG_TILES = ((0, 64, 0), (64, 128, 320), (128, 160, 384), (160, 161, 416), (161, 225, 64), (225, 289, 128),
             (289, 353, 192), (353, 417, 256))


def _tile_runs(lo, hi):
    runs = []
    for s0, s1, d0 in SEG_TILES:
        a, b = max(lo, s0), min(hi, s1)
        if a < b:
            runs.append((a, b - a, d0 + a - s0))
    return runs


def _assemble_w_in(g):
    whole = []
    for k in range(N_DEV):
        lo = SHARD_STEP * k + (1 if k else 0)
        hi = SHARD_STEP * (k + 1) + (1 if k == N_DEV - 1 else 0)
        whole += [(k, a - SHARD_STEP * k, n, d) for a, n, d in _tile_runs(lo, hi)]
    split = [(k, _tile_runs(SHARD_STEP * k, SHARD_STEP * k + 1)[0][2]) for k in range(1, N_DEV)]

    def body(g_ref, o_ref):
        rows = lambda t, n=1: pl.ds(TILE * t, TILE * n)
        for k, t, n, d in whole:
            o_ref[rows(d, n), :] = g_ref[k, rows(t, n), :]
        for k, d in split:
            o_ref[rows(d), :] = g_ref[k - 1, rows(SHARD_STEP), :] + g_ref[k, rows(0), :]
        o_ref[pl.ds(W_IN_COLS, NP - W_IN_COLS), :] = jnp.zeros((NP - W_IN_COLS, D), o_ref.dtype)

    vm = pl.BlockSpec(memory_space=pltpu.VMEM)
    return pl.pallas_call(
        body, name="assemble_w_in", in_specs=[vm], out_specs=vm, out_shape=jax.ShapeDtypeStruct((NP, D), g.dtype),
        compiler_params=pltpu.CompilerParams(vmem_limit_bytes=VMEM_LIMIT),
    )(g)


def _scatter_w_in_grad(dw):
    runs = [(k, a - SHARD_STEP * k, n, d) for k in range(N_DEV)
            for a, n, d in _tile_runs(SHARD_STEP * k, SHARD_STEP * k + SHARD_TILES)]
    pad = GRAD_LATE_ROWS - IN_PAD

    def body(dw_ref, o_ref):
        rows = lambda t, n: pl.ds(TILE * t, TILE * n)
        for k, t, n, d in runs:
            o_ref[k, rows(t, n), :] = dw_ref[rows(d, n), :]
        for k in range(N_DEV):
            o_ref[k, pl.ds(IN_PAD, pad), :] = jnp.zeros((pad, D), o_ref.dtype)

    vm = pl.BlockSpec(memory_space=pltpu.VMEM)
    return pl.pallas_call(
        body, name="scatter_w_in_grad", in_specs=[vm], out_specs=vm,
        out_shape=jax.ShapeDtypeStruct((N_DEV, GRAD_LATE_ROWS, D), dw.dtype),
        compiler_params=pltpu.CompilerParams(vmem_limit_bytes=VMEM_LIMIT),
    )(dw)


def _early_unpack(g):
    w_in_pt = _assemble_w_in(g)
    conv = {}
    for n, r0, c in (("conv_ssm_w", IN_PAD, CONV_SSM_COLS), ("conv_lru_w", IN_PAD + 16, CONV_LRU_COLS)):
        s = g[:, r0:r0 + 12, :c].astype(F32).reshape(N_DEV, 3, 4, c)
        conv[n] = ((s[:, 0] + s[:, 1]) + s[:, 2]).transpose(1, 0, 2).reshape(4, N_DEV * c)
    return w_in_pt, conv


def _late_pack(a):
    bf = lambda t: t.astype(MXU)
    return jnp.concatenate([bf(a["w_out"][0]), bf(a["w_up"][0]).T, bf(a["w_down"][0])], axis=0)


def _late_unpack(g):
    return dict(w_out=g[:, :ROW_UP].reshape(D, D), w_upT=g[:, ROW_UP:ROW_DOWN].reshape(FF, D),
                w_down=g[:, ROW_DOWN:].reshape(FF, D))


def _own_slot(land, own):
    me = 4 * lax.axis_index("x") + 2 * lax.axis_index("y") + lax.axis_index("c")
    return lax.dynamic_update_slice_in_dim(land, own[None], me, axis=0)


def kernel(x, norm_mix_pre, w_in, conv_ssm_w, conv_ssm_b, dt_bias, a_log, d_skip, ssm_norm, conv_lru_w, conv_lru_b, lru_wa, lru_ba, lru_wx, lru_bx, lru_lambda, w_out, norm_mix_post, norm_mlp_pre, w_up, w_down, norm_mlp_post, loss_target, m_norm_mix_pre, m_w_in, m_conv_ssm_w, m_conv_ssm_b, m_dt_bias, m_a_log, m_d_skip, m_ssm_norm, m_conv_lru_w, m_conv_lru_b, m_lru_wa, m_lru_ba, m_lru_wx, m_lru_bx, m_lru_lambda, m_w_out, m_norm_mix_post, m_norm_mlp_pre, m_w_up, m_w_down, m_norm_mlp_post, v_norm_mix_pre, v_w_in, v_conv_ssm_w, v_conv_ssm_b, v_dt_bias, v_a_log, v_d_skip, v_ssm_norm, v_conv_lru_w, v_conv_lru_b, v_lru_wa, v_lru_ba, v_lru_wx, v_lru_bx, v_lru_lambda, v_w_out, v_norm_mix_post, v_norm_mlp_pre, v_w_up, v_w_down, v_norm_mlp_post):
    vals = (norm_mix_pre, w_in, conv_ssm_w, conv_ssm_b, dt_bias, a_log, d_skip, ssm_norm, conv_lru_w, conv_lru_b, lru_wa, lru_ba, lru_wx, lru_bx, lru_lambda, w_out, norm_mix_post, norm_mlp_pre, w_up, w_down, norm_mlp_post)
    m_vals = (m_norm_mix_pre, m_w_in, m_conv_ssm_w, m_conv_ssm_b, m_dt_bias, m_a_log, m_d_skip, m_ssm_norm, m_conv_lru_w, m_conv_lru_b, m_lru_wa, m_lru_ba, m_lru_wx, m_lru_bx, m_lru_lambda, m_w_out, m_norm_mix_post, m_norm_mlp_pre, m_w_up, m_w_down, m_norm_mlp_post)
    v_vals = (v_norm_mix_pre, v_w_in, v_conv_ssm_w, v_conv_ssm_b, v_dt_bias, v_a_log, v_d_skip, v_ssm_norm, v_conv_lru_w, v_conv_lru_b, v_lru_wa, v_lru_ba, v_lru_wx, v_lru_bx, v_lru_lambda, v_w_out, v_norm_mix_post, v_norm_mlp_pre, v_w_up, v_w_down, v_norm_mlp_post)
    w = dict(zip(WEIGHTS, vals))
    m = dict(zip(WEIGHTS, m_vals))
    v = dict(zip(WEIGHTS, v_vals))
    me = 4 * lax.axis_index("x") + 2 * lax.axis_index("y") + lax.axis_index("c")

    bf = lambda t: t.astype(MXU)
    late = _late_pack(w)
    early = _all_gather(_early_pack(w, me), "early_weights_all_gather")
    late, early = lax.optimization_barrier((late, early))
    lw = {}
    lw["sems"], lw["src"], lw["land"], token = _exchange_start(late, None, PLAN_GATHER_ICI, "late_weights_ici_start")
    w_in_pt, conv_w = _early_unpack(early)
    full = {n: (conv_w[n] if n in conv_w else w[n][0]) for n in WEIGHTS if n not in BIG}
    full["norm_mix_pre"] = full["norm_mix_pre"] + token[0, 0]

    def after_ssd(after):
        src, land = _exchange_wait(lw["sems"], lw["src"], lw["land"], after, PLAN_GATHER_ICI, "late_weights_ici_wait")
        lw["sems"], lw["src"], lw["land"], tok = _exchange_start(src, land, PLAN_GATHER_D2D, "late_weights_d2d_start")
        return tok[0, 0]

    def late_weights(after):
        src, land = _exchange_wait(lw["sems"], lw["src"], lw["land"], after, PLAN_GATHER_D2D, "late_weights_d2d_wait")
        return _late_unpack(_own_slot(land, src))

    sent = {}

    def send_mlp_grads(d_w_up_t, d_w_down, d_w_out):
        src = jnp.concatenate([bf(d_w_up_t).reshape(N_DEV, -1, D), bf(d_w_down).reshape(N_DEV, -1, D),
                               bf(d_w_out).reshape(N_DEV, -1, D)], axis=1)
        sent["sems"], sent["src"], sent["land"], tok = _exchange_start(src, None, PLAN_SCATTER, "mlp_grads_start")
        return tok[0, 0]

    def send_late_grads(d_w_in_pt):
        src = _scatter_w_in_grad(d_w_in_pt)
        sent["sems2"], sent["src2"], sent["land2"], tok = _exchange_start(src, None, PLAN_SCATTER, "late_grads_start")
        return tok[0, 0]

    loss, grad_x, g = _local_step(x[0], loss_target[0], _prep_params(full, dict(w_in_pT=w_in_pt)), after_ssd,
                                  late_weights, send_mlp_grads, send_late_grads)
    loss = lax.psum(loss, ("x", "y", "c"))

    own = lambda src: lax.dynamic_index_in_dim(src, me, keepdims=False)
    out_g, out_d, out_m, out_v = {}, {}, {}, {}
    mlp_src, mlp_land = _exchange_wait(sent["sems"], sent["src"], sent["land"], grad_x, PLAN_SCATTER, "mlp_grads_wait")
    g_mlp = _slot_sum(_own_slot(mlp_land, own(mlp_src)), "slot_sum_mlp")
    fs = FF // N_DEV
    for n, gn in (("w_up", g_mlp[:fs].T[None]), ("w_down", g_mlp[fs:2 * fs][None]), ("w_out", g_mlp[2 * fs:][None])):
        out_g[n] = gn
        out_d[n], out_m[n], out_v[n] = _adam_big(w[n], gn, m[n], v[n], "adam_" + n)
    zrow = jnp.zeros((1, D), F32)
    pad16 = lambda a: jnp.pad(a, ((0, 0), (0, 128 - NH)))
    small_parts = [
        jnp.concatenate([g["norm_mix_pre"], g["ssm_norm"], g["conv_lru_b"], g["lru_lambda"], g["norm_mix_post"],
                         g["norm_mlp_pre"], g["norm_mlp_post"], zrow], axis=0),
        g["conv_ssm_b"],
        jnp.concatenate([pad16(g["dt_bias"]), pad16(g["a_log"]), pad16(g["d_skip"]), jnp.zeros((5, 128), F32)], axis=0),
        g["conv_ssm_w"], g["conv_lru_w"],
        jnp.concatenate([g["lru_wa"].reshape(D, 64), g["lru_wx"].reshape(D, 64)], axis=0),
        jnp.concatenate([g["lru_ba"].reshape(NH, 64), g["lru_bx"].reshape(NH, 64)], axis=0)]
    small = _pad_rows(jnp.concatenate([s.reshape(-1) for s in small_parts]), 8)
    small, _ = lax.optimization_barrier((small, out_v["w_out"]))
    small_all = _all_gather(small, "small_grads_all_gather")

    lg_src, lg_land = _exchange_wait(sent["sems2"], sent["src2"], sent["land2"], small_all, PLAN_SCATTER, "late_grads_wait")
    g_late = _slot_sum(_own_slot(lg_land, own(lg_src)), "slot_sum_late")
    gt = lax.dynamic_slice(g_late, (2 * me, 0), (IN_ROWS, D))
    dt_, mt_, vt_ = _adam_big(w["w_in"][0].T, gt, m["w_in"][0].T, v["w_in"][0].T, "adam_w_in")
    out_g["w_in"], out_d["w_in"], out_m["w_in"], out_v["w_in"] = gt.T[None], dt_.T[None], mt_.T[None], vt_.T[None]
    sflat = small_all.reshape(N_DEV, -1)
    groups = []
    off = 0
    for r, c in SMALL_GROUPS:
        groups.append(sflat[:, off:off + r * c].reshape(N_DEV, r, c))
        off += r * c
    groups[3] = lax.dynamic_slice_in_dim(groups[3], me * CONV_SSM_COLS, CONV_SSM_COLS, axis=2)
    groups[4] = lax.dynamic_slice_in_dim(groups[4], me * CONV_LRU_COLS, CONV_LRU_COLS, axis=2)
    wmv = [(w[n].reshape(s), m[n].reshape(s), v[n].reshape(s)) for n, s, _, _ in SMALL]
    res = _adam_small(groups, [(gi, r0) for _, _, gi, r0 in SMALL], wmv)
    for (n, _, _, _), (g_n, d_n, m_n, v_n) in zip(SMALL, res):
        shape = w[n].shape
        out_g[n], out_d[n], out_m[n], out_v[n] = (g_n.reshape(shape), d_n.reshape(shape), m_n.reshape(shape),
                                                  v_n.reshape(shape))
    return (loss, grad_x[None], *[out_g[n] for n in WEIGHTS], *[out_d[n] for n in WEIGHTS],
            *[out_m[n] for n in WEIGHTS], *[out_v[n] for n in WEIGHTS])
```

```python
import functools

import jax
import jax.numpy as jnp
from jax import lax
from jax.experimental import pallas as pl
from jax.experimental.pallas import tpu as pltpu

F32 = jnp.float32
MXU = jnp.bfloat16
HI = lax.Precision.HIGHEST
EPS = 1e-6

D = 1024
NH = 16
NS = 128
CH = 128
FF = 4096
NP = 7168
SEG = 1024
LRU_C = 8.0
N_DEV = 8

ADAM_LR, ADAM_B1, ADAM_B2, ADAM_EPS, ADAM_WD, ADAM_STEP = 0.001, 0.9, 0.999, 1e-08, 0.01, 10

VMEM_LIMIT = 56 * 1024 * 1024


def _cp(*sem):
    return pltpu.CompilerParams(dimension_semantics=sem, vmem_limit_bytes=VMEM_LIMIT)


def _nn(a, b):
    return jnp.dot(a.astype(MXU), b.astype(MXU), preferred_element_type=F32)


def _nt(a, b):
    return lax.dot_general(a.astype(MXU), b.astype(MXU), (((1,), (1,)), ((), ())), preferred_element_type=F32)


def _tn(a, b):
    return lax.dot_general(a.astype(MXU), b.astype(MXU), (((0,), (0,)), ((), ())), preferred_element_type=F32)


_sigmoid = jax.nn.sigmoid


def _silu(x):
    return x * _sigmoid(x)


def _dsilu(x):
    s = _sigmoid(x)
    return s + x * s * (1.0 - s)


def _softplus(x):
    return jnp.maximum(x, 0.0) + jnp.log(1.0 + jnp.exp(-jnp.abs(x)))


def _rms(x, g):
    r = lax.rsqrt(jnp.mean(x * x, axis=-1, keepdims=True) + EPS)
    return x * r * g


def _rms_bwd(x, g, dy):
    r = lax.rsqrt(jnp.mean(x * x, axis=-1, keepdims=True) + EPS)
    gdy = g * dy
    dx = r * gdy - x * (r * r * r) * jnp.mean(x * gdy, axis=-1, keepdims=True)
    return dx, dy * x * r


def _rowsum(x):
    return jnp.sum(x, axis=0, keepdims=True)


def _taps_past(cur, prev8):
    r_n, c_n = cur.shape
    row = lax.broadcasted_iota(jnp.int32, (r_n, c_n), 0)
    out = []
    for k in range(4):
        s = 3 - k
        if s == 0:
            out.append(cur)
            continue
        head = jnp.concatenate([pltpu.roll(prev8, s, 0), jnp.zeros((r_n - 8, c_n), F32)], axis=0)
        out.append(jnp.where(row < s, head, pltpu.roll(cur, s, 0)))
    return out


def _taps_future(cur, fut8):
    r_n, c_n = cur.shape
    row = lax.broadcasted_iota(jnp.int32, (r_n, c_n), 0)
    out = []
    for k in range(4):
        s = 3 - k
        if s == 0:
            out.append(cur)
            continue
        tail = jnp.concatenate([jnp.zeros((r_n - 8, c_n), F32), pltpu.roll(fut8, 8 - s, 0)], axis=0)
        out.append(jnp.where(row >= r_n - s, tail, pltpu.roll(cur, r_n - s, 0)))
    return out


def _conv_apply(taps, w, b):
    acc = taps[0] * w[0:1, :]
    for k in range(1, 4):
        acc = acc + taps[k] * w[k:k + 1, :]
    return acc + b


def _inproj(x, g0, w):
    t_n = x.shape[0]
    tm = min(t_n, 1024)

    n_j = NP // SEG

    def body(x_ref, g_ref, w_ref, pb_ref, p6_ref, u_ref):
        j = pl.program_id(1)

        @pl.when(j == 0)
        def _():
            u_ref[...] = _rms(x_ref[...], g_ref[...]).astype(MXU)

        p = lax.dot_general(u_ref[...], w_ref[...], (((1,), (1,)), ((), ())), preferred_element_type=F32)

        @pl.when(j < n_j - 1)
        def _():
            pb_ref[...] = p.astype(MXU)

        @pl.when(j == n_j - 1)
        def _():
            p6_ref[...] = p

    pb, p6, u = pl.pallas_call(
        body, name="inproj", grid=(t_n // tm, n_j),
        in_specs=[pl.BlockSpec((tm, D), lambda i, j: (i, 0)), pl.BlockSpec((1, D), lambda i, j: (0, 0)),
                  pl.BlockSpec((SEG, D), lambda i, j: (j, 0))],
        out_specs=[pl.BlockSpec((tm, SEG), lambda i, j: (i, jnp.minimum(j, n_j - 2))),
                   pl.BlockSpec((tm, SEG), lambda i, j: (i, 0)), pl.BlockSpec((tm, D), lambda i, j: (i, 0))],
        out_shape=[jax.ShapeDtypeStruct((t_n, NP - SEG), MXU), jax.ShapeDtypeStruct((t_n, SEG), F32),
                   jax.ShapeDtypeStruct((t_n, D), MXU)],
        compiler_params=_cp("parallel", "arbitrary"),
    )(x, g0, w)
    return (pb, p6), u


def _ssd_prep(dtraw, dtb, alog):
    l_n = dtraw.shape[0]
    r = lax.broadcasted_iota(jnp.int32, (l_n, l_n), 0)
    c = lax.broadcasted_iota(jnp.int32, (l_n, l_n), 1)
    tril = (r >= c).astype(F32)
    triu = (r <= c).astype(F32)
    eye = (r == c).astype(F32)
    dt = _softplus(dtraw + dtb)
    adt = dt * (-jnp.exp(alog))
    ac = jnp.dot(tril, adt, preferred_element_type=F32, precision=HI)
    tn = (((0,), (0,)), ((), ()))
    ac_t = lax.dot_general(adt, triu, tn, preferred_element_type=F32, precision=HI)
    dt_t = lax.dot_general(dt, eye, tn, preferred_element_type=F32, precision=HI)
    return dt, dt_t, ac, ac_t, _rowsum(adt)


def _ssd_pair(j, xp, bg, cg, sp, dt, dt_t, ac, ac_t, aend):
    l_n = xp.shape[0]
    lane = lax.broadcasted_iota(jnp.int32, (l_n, 128), 1)
    sub = lax.broadcasted_iota(jnp.int32, (128, l_n), 0)
    lane1 = lax.broadcasted_iota(jnp.int32, (1, 128), 1)
    tri = lax.broadcasted_iota(jnp.int32, (l_n, l_n), 0) >= lax.broadcasted_iota(jnp.int32, (l_n, l_n), 1)
    lo = lax.broadcasted_iota(jnp.int32, (l_n, 128), 1) < 64
    lo_s = lax.broadcasted_iota(jnp.int32, (128, 128), 1) < 64
    cb = _nt(cg, bg)
    cs = _nn(cg, sp)
    x2 = jnp.concatenate([jnp.where(lo, xp, 0.0), jnp.where(lo, 0.0, xp)], axis=0)
    ws_, bs_, eo, ee = [], [], [], []
    for e in range(2):
        h = 2 * j + e
        ac_l = jnp.sum(jnp.where(lane == h, ac, 0.0), axis=1, keepdims=True)
        dt_l = jnp.sum(jnp.where(lane == h, dt, 0.0), axis=1, keepdims=True)
        a_end = jnp.sum(jnp.where(lane1 == h, aend, 0.0), axis=1, keepdims=True)
        ac_s, dt_s = ac_t[h:h + 1, :], dt_t[h:h + 1, :]
        decay = jnp.exp(jnp.where(tri, ac_l - ac_s, -1e30))
        ws_.append(cb * decay * dt_s)
        bs_.append(bg * (jnp.exp(a_end - ac_l) * dt_l))
        eo.append(jnp.exp(ac_l))
        ee.append(jnp.exp(a_end))
    y = _nn(jnp.concatenate(ws_, axis=1), x2) + jnp.where(lo, eo[0], eo[1]) * cs
    s_new = _tn(jnp.concatenate(bs_, axis=0), x2) + jnp.where(lo_s, ee[0], ee[1]) * sp
    return y, s_new


def _ssd_post(y, xs, z, dsk, nrm):
    y = (y + dsk * xs) * _silu(z)
    half = D // 2
    ya, yb = y[:, :half], y[:, half:]
    ya = ya * lax.rsqrt(jnp.mean(ya * ya, axis=-1, keepdims=True) + EPS)
    yb = yb * lax.rsqrt(jnp.mean(yb * yb, axis=-1, keepdims=True) + EPS)
    return jnp.concatenate([ya, yb], axis=1) * nrm


LAST_SEG = NP // SEG - 1


def _proj_ops(proj, seg_ids):
    return [proj[1] if s == LAST_SEG else proj[0] for s in seg_ids]


def _proj_specs(rows, seg_ids, order):
    return [pl.BlockSpec((rows, SEG), functools.partial(lambda i, c: (order(i), c), c=0 if s == LAST_SEG else s))
            for s in seg_ids]


def _prev_specs(rows, seg_ids, order):
    specs = []
    for s in seg_ids:
        n, c = (8, 0) if s == LAST_SEG else (16, s)
        specs.append(pl.BlockSpec((n, SEG), functools.partial(
            lambda i, n, c: (jnp.maximum(order(i) * (rows // n) - 1, 0), c), n=n, c=c)))
    return specs


def _prev8(ref):
    return ref[...] if ref.shape[0] == 8 else ref[8:16, :].astype(F32)


def _full(shape):
    return pl.BlockSpec(shape, lambda i: (0,) * len(shape))


def _ssd_fwd(proj, cwx, cwb, cbx, cbb, dtb, alog, dsk, nrm):
    t_n = proj[0].shape[0]
    n_c = t_n // CH
    fwd = lambda i: i

    def body(z_ref, xs_ref, bc_ref, xsp_ref, bcp_ref, cwx_ref, cwb_ref, cbx_ref, cbb_ref, dtb_ref, alog_ref,
             dsk_ref, nrm_ref, ya_ref, sprev_ref, yraw_ref, xspre_ref, bcpre_ref, s_ref):
        c = pl.program_id(0)

        @pl.when(c == 0)
        def _():
            s_ref[...] = jnp.zeros_like(s_ref)

        keep = jnp.where(c == 0, 0.0, 1.0)
        xs_pre = _conv_apply(_taps_past(xs_ref[...].astype(F32), _prev8(xsp_ref) * keep), cwx_ref[...], cbx_ref[...])
        bc_pre = _conv_apply(_taps_past(bc_ref[:, :512], bcp_ref[:, :512] * keep), cwb_ref[...], cbb_ref[...])
        xspre_ref[...] = xs_pre
        bcpre_ref[...] = bc_pre
        prep = _ssd_prep(bc_ref[:, 512:640], dtb_ref[...], alog_ref[...])
        xs = _silu(xs_pre)
        bc = _silu(bc_pre)
        sprev_ref[0] = s_ref[...]
        ys = []
        for j in range(NH // 2):
            g = j // 4
            yp, sn = _ssd_pair(j, xs[:, 128 * j:128 * j + 128], bc[:, 128 * g:128 * g + 128],
                               bc[:, 256 + 128 * g:384 + 128 * g], s_ref[:, 128 * j:128 * j + 128], *prep)
            ys.append(yp)
            s_ref[:, 128 * j:128 * j + 128] = sn
        y = jnp.concatenate(ys, axis=1)
        yraw_ref[...] = y
        ya_ref[...] = _ssd_post(y, xs, z_ref[...].astype(F32), dsk_ref[...], nrm_ref[...]).astype(ya_ref.dtype)

    return pl.pallas_call(
        body, name="ssd_fwd", grid=(n_c,),
        in_specs=_proj_specs(CH, (0, 5, 6), fwd) + _prev_specs(CH, (5, 6), fwd) + [
            _full((4, D)), _full((4, 512)), _full((1, D)), _full((1, 512)), _full((1, 128)), _full((1, 128)),
            _full((1, D)), _full((1, D))],
        out_specs=[pl.BlockSpec((CH, D), lambda i: (i, 0)), pl.BlockSpec((1, NS, D), lambda i: (i, 0, 0)),
                   pl.BlockSpec((CH, D), lambda i: (i, 0)), pl.BlockSpec((CH, D), lambda i: (i, 0)),
                   pl.BlockSpec((CH, 512), lambda i: (i, 0))],
        out_shape=[jax.ShapeDtypeStruct((t_n, D), MXU), jax.ShapeDtypeStruct((n_c, NS, D), F32),
                   jax.ShapeDtypeStruct((t_n, D), F32), jax.ShapeDtypeStruct((t_n, D), F32),
                   jax.ShapeDtypeStruct((t_n, 512), F32)],
        scratch_shapes=[pltpu.VMEM((NS, D), F32)],
        compiler_params=_cp("arbitrary"),
    )(*_proj_ops(proj, (0, 5, 6, 5, 6)), cwx, cwb, cbx, cbb, dtb, alog, dsk, nrm)


def _ssd_bwd(dya, saved, proj, sprev, cwx, cwb, dtb, alog, dsk, nrm, mlp_ops):
    t_n = proj[0].shape[0]
    n_c = t_n // CH
    rev = lambda i: n_c - 1 - i
    fb = FF // n_c

    def body(dya_ref, yraw_ref, xspre_ref, bcpre_ref, z_ref, xs_ref, bc_ref, sprev_ref, cwx_ref, cwb_ref,
             dtb_ref, alog_ref, dsk_ref, nrm_ref, hid_ref, dff_ref, dhp_ref, v_ref,
             dz_ref, dxs_ref, dbc_ref, dcwx_ref, dcwb_ref, dcbx_ref, dcbb_ref, ddtb_ref, dalog_ref, ddsk_ref,
             dnrm_ref, dwd_ref, dwu_ref, ds_ref, futx_ref, futb_ref):
        i = pl.program_id(0)
        acc_refs = (dcwx_ref, dcwb_ref, dcbx_ref, dcbb_ref, ddtb_ref, dalog_ref, ddsk_ref, dnrm_ref)
        tn = (((0,), (0,)), ((), ()))
        kt = t_n // (NH // 2)
        dwd_acc = jnp.zeros((fb, D), F32)
        dwu_acc = jnp.zeros((fb, D), F32)

        @pl.when(i == 0)
        def _():
            for r in (ds_ref, futx_ref, futb_ref) + acc_refs:
                r[...] = jnp.zeros_like(r)

        xs_pre = xspre_ref[...]
        bc_pre = bcpre_ref[...]
        xs = _silu(xs_pre)
        bc = _silu(bc_pre)
        prep, prep_vjp = jax.vjp(_ssd_prep, bc_ref[:, 512:640], dtb_ref[...], alog_ref[...])
        s_in = sprev_ref[0]

        def pair_args(j):
            g = j // 4
            return (xs[:, 128 * j:128 * j + 128], bc[:, 128 * g:128 * g + 128],
                    bc[:, 256 + 128 * g:384 + 128 * g], s_in[:, 128 * j:128 * j + 128]) + tuple(prep)

        _, post_vjp = jax.vjp(_ssd_post, yraw_ref[...], xs, z_ref[...].astype(F32), dsk_ref[...], nrm_ref[...])
        dy, dxs_skip, dz, ddsk, dnrm = post_vjp(dya_ref[...])
        dz_ref[...] = dz.astype(dz_ref.dtype)
        ddsk_ref[...] += ddsk
        dnrm_ref[...] += dnrm

        dprep = [jnp.zeros_like(p) for p in prep]
        dxp = []
        dbg = [jnp.zeros((CH, 128), F32), jnp.zeros((CH, 128), F32)]
        dcg = [jnp.zeros((CH, 128), F32), jnp.zeros((CH, 128), F32)]
        for j in range(NH // 2):
            g = j // 4
            _, pair_vjp = jax.vjp(functools.partial(_ssd_pair, j), *pair_args(j))
            cts = pair_vjp((dy[:, 128 * j:128 * j + 128], ds_ref[:, 128 * j:128 * j + 128]))
            dxp.append(cts[0])
            dbg[g] = dbg[g] + cts[1]
            dcg[g] = dcg[g] + cts[2]
            ds_ref[:, 128 * j:128 * j + 128] = cts[3]
            dprep = [a + b for a, b in zip(dprep, cts[4:])]
            rows = pl.ds(kt * j, kt)
            dwd_acc = dwd_acc + lax.dot_general(hid_ref[rows, :], dff_ref[rows, :], tn, preferred_element_type=F32)
            dwu_acc = dwu_acc + lax.dot_general(dhp_ref[rows, :], v_ref[rows, :], tn, preferred_element_type=F32)
        dwd_ref[...] = dwd_acc.astype(MXU)
        dwu_ref[...] = dwu_acc.astype(MXU)
        ddtraw, ddtb, dalog = prep_vjp(tuple(dprep))
        ddtb_ref[...] += ddtb
        dalog_ref[...] += dalog

        dxs_pre = (dxs_skip + jnp.concatenate(dxp, axis=1)) * _dsilu(xs_pre)
        dbc_pre = jnp.concatenate([dbg[0], dbg[1], dcg[0], dcg[1]], axis=1) * _dsilu(bc_pre)
        dcbx_ref[...] += _rowsum(dxs_pre)
        dcbb_ref[...] += _rowsum(dbc_pre)
        fx = _taps_future(dxs_pre, futx_ref[...])
        fbc = _taps_future(dbc_pre, futb_ref[...])
        xs_in = xs_ref[...].astype(F32)
        bc_in = bc_ref[:, :512]
        for k in range(4):
            dcwx_ref[k:k + 1, :] += _rowsum(fx[k] * xs_in)
            dcwb_ref[k:k + 1, :] += _rowsum(fbc[k] * bc_in)
        cwx = cwx_ref[...]
        cwb = cwb_ref[...]
        dxs_in = fx[0] * cwx[0:1, :]
        dbc_in = fbc[0] * cwb[0:1, :]
        for k in range(1, 4):
            dxs_in = dxs_in + fx[k] * cwx[k:k + 1, :]
            dbc_in = dbc_in + fbc[k] * cwb[k:k + 1, :]
        futx_ref[...] = dxs_pre[0:8, :]
        futb_ref[...] = dbc_pre[0:8, :]
        dxs_ref[...] = dxs_in.astype(dxs_ref.dtype)
        dbc_ref[...] = jnp.concatenate([dbc_in, ddtraw, jnp.zeros((CH, SEG - 640), F32)], axis=1).astype(dbc_ref.dtype)

    row_out = lambda: pl.BlockSpec((CH, D), lambda i: (rev(i), 0))
    outs = pl.pallas_call(
        body, name="ssd_bwd", grid=(n_c,),
        in_specs=[row_out(), row_out(), row_out(), pl.BlockSpec((CH, 512), lambda i: (rev(i), 0))]
        + _proj_specs(CH, (0, 5, 6), rev) + [pl.BlockSpec((1, NS, D), lambda i: (rev(i), 0, 0)),
                                             _full((4, D)), _full((4, 512)),
                                             _full((1, 128)), _full((1, 128)), _full((1, D)), _full((1, D)),
                                             pl.BlockSpec((t_n, fb), lambda i: (0, i)), _full((t_n, D)),
                                             pl.BlockSpec((t_n, fb), lambda i: (0, i)), _full((t_n, D))],
        out_specs=[row_out(), row_out(), row_out(), _full((4, D)), _full((4, 512)), _full((1, D)), _full((1, 512)),
                   _full((1, 128)), _full((1, 128)), _full((1, D)), _full((1, D)),
                   pl.BlockSpec((fb, D), lambda i: (i, 0)), pl.BlockSpec((fb, D), lambda i: (i, 0))],
        out_shape=[jax.ShapeDtypeStruct((t_n, D), MXU)] * 3 + [
            jax.ShapeDtypeStruct(s, F32) for s in ((4, D), (4, 512), (1, D), (1, 512), (1, 128), (1, 128), (1, D), (1, D))]
        + [jax.ShapeDtypeStruct((FF, D), MXU)] * 2,
        scratch_shapes=[pltpu.VMEM((NS, D), F32), pltpu.VMEM((8, D), F32), pltpu.VMEM((8, 512), F32)],
        compiler_params=_cp("arbitrary"),
    )(dya, *saved, *_proj_ops(proj, (0, 5, 6)), sprev, cwx, cwb, dtb, alog, dsk, nrm, *mlp_ops)
    return outs


LRU_ROWS = 256
LRU_BLK = 256


def _lru_gates(xr, wa, wx, ba, bx, lam):
    pr = jnp.concatenate([_nn(xr[:, LRU_BLK * b:LRU_BLK * (b + 1)], wa[b]) for b in range(D // LRU_BLK)], axis=1) + ba
    pi = jnp.concatenate([_nn(xr[:, LRU_BLK * b:LRU_BLK * (b + 1)], wx[b]) for b in range(D // LRU_BLK)], axis=1) + bx
    log_a = -LRU_C * _sigmoid(pr) * _softplus(-lam)
    a = jnp.exp(log_a)
    mult = jnp.sqrt(1.0 - jnp.exp(2.0 * log_a))
    return a, mult * (_sigmoid(pi) * xr)


def _lru_out(h, g):
    return h * jax.nn.gelu(g, approximate=True)


def _lru_fwd(proj, cw, cb, wa, wx, ba, bx, lam):
    t_n = proj[0].shape[0]
    rows = min(LRU_ROWS, t_n)
    fwd = lambda i: i

    def body(g_ref, x_ref, xp_ref, cw_ref, cb_ref, wa_ref, wx_ref, ba_ref, bx_ref, lam_ref, yb_ref, h_ref, xr_ref,
             a_s, u_s, carry):
        i = pl.program_id(0)

        @pl.when(i == 0)
        def _():
            carry[...] = jnp.zeros_like(carry)

        keep = jnp.where(i == 0, 0.0, 1.0)
        xr = _conv_apply(_taps_past(x_ref[...].astype(F32), _prev8(xp_ref) * keep), cw_ref[...], cb_ref[...])
        xr_ref[...] = xr
        a, u = _lru_gates(xr, wa_ref[...], wx_ref[...], ba_ref[...], bx_ref[...], lam_ref[...])
        a_s[...] = a
        u_s[...] = u
        row = lax.broadcasted_iota(jnp.int32, (8, D), 0)

        def blk(b, c):
            s = pl.multiple_of(b * 8, 8)
            av = a_s[pl.ds(s, 8), :]
            uv = u_s[pl.ds(s, 8), :]
            for d in (1, 2, 4):
                m = row >= d
                uv = uv + av * jnp.where(m, pltpu.roll(uv, d, 0), 0.0)
                av = av * jnp.where(m, pltpu.roll(av, d, 0), 1.0)
            hv = uv + av * c
            h_ref[pl.ds(s, 8), :] = hv
            return hv[7:8, :]

        carry[0:1, :] = lax.fori_loop(0, rows // 8, blk, carry[0:1, :])
        yb_ref[...] = _lru_out(h_ref[...], g_ref[...].astype(F32)).astype(yb_ref.dtype)

    return pl.pallas_call(
        body, name="lru_fwd", grid=(t_n // rows,),
        in_specs=_proj_specs(rows, (1, 2), fwd) + _prev_specs(rows, (2,), fwd) + [
            _full((4, D)), _full((1, D)), _full((4, LRU_BLK, LRU_BLK)), _full((4, LRU_BLK, LRU_BLK)),
            _full((1, D)), _full((1, D)), _full((1, D))],
        out_specs=[pl.BlockSpec((rows, D), lambda i: (i, 0))] * 3,
        out_shape=[jax.ShapeDtypeStruct((t_n, D), MXU), jax.ShapeDtypeStruct((t_n, D), F32),
                   jax.ShapeDtypeStruct((t_n, D), F32)],
        scratch_shapes=[pltpu.VMEM((rows, D), F32), pltpu.VMEM((rows, D), F32), pltpu.VMEM((8, D), F32)],
        compiler_params=_cp("arbitrary"),
    )(*_proj_ops(proj, (1, 2, 2)), cw, cb, wa, wx, ba, bx, lam)


def _lru_bwd(dyb, proj, h, xr_saved, cw, wa, wx, ba, bx, lam):
    t_n = proj[0].shape[0]
    rows = min(LRU_ROWS, t_n)
    n_t = t_n // rows
    rev = lambda i: n_t - 1 - i
    rb = rows // 8

    def body(dyb_ref, g_ref, x_ref, h_ref, hp_ref, xr_ref, cw_ref, wa_ref, wx_ref, ba_ref, bx_ref, lam_ref,
             dg_ref, dx_ref, dcw_ref, dcb_ref, dwa_ref, dwx_ref, dba_ref, dbx_ref, dlam_ref,
             a_s, dh_s, hx_s, da_s, du_s, carry, fut):
        i = pl.program_id(0)
        acc_refs = (dcw_ref, dcb_ref, dwa_ref, dwx_ref, dba_ref, dbx_ref, dlam_ref)

        @pl.when(i == 0)
        def _():
            for r in (carry, fut) + acc_refs:
                r[...] = jnp.zeros_like(r)

        keep = jnp.where(i == n_t - 1, 0.0, 1.0)
        gate_in = (xr_ref[...], wa_ref[...], wx_ref[...], ba_ref[...], bx_ref[...], lam_ref[...])
        (a, _), gates_vjp = jax.vjp(_lru_gates, *gate_in)
        _, out_vjp = jax.vjp(_lru_out, h_ref[...], g_ref[...].astype(F32))
        dh, dg = out_vjp(dyb_ref[...])
        dg_ref[...] = dg.astype(dg_ref.dtype)
        a_s[...] = a
        dh_s[...] = dh
        hx_s[0:8, :] = hp_ref[...] * keep
        hx_s[8:, :] = h_ref[...]
        row = lax.broadcasted_iota(jnp.int32, (8, D), 0)

        def blk(b, c):
            s = pl.multiple_of((rb - 1 - b) * 8, 8)
            av = a_s[pl.ds(s, 8), :]
            dhv = dh_s[pl.ds(s, 8), :]
            kv = av * dhv
            for d in (1, 2, 4):
                m = row <= 7 - d
                kv = kv + av * jnp.where(m, pltpu.roll(kv, 8 - d, 0), 0.0)
                av = av * jnp.where(m, pltpu.roll(av, 8 - d, 0), 1.0)
            kv = kv + av * c
            gv = dhv + jnp.where(row < 7, pltpu.roll(kv, 7, 0), c)
            hb = hx_s[pl.ds(s + 8, 8), :]
            hpv = hx_s[pl.ds(s, 8), :]
            hprev = jnp.where(row >= 1, pltpu.roll(hb, 1, 0), hpv[7:8, :])
            du_s[pl.ds(s, 8), :] = gv
            da_s[pl.ds(s, 8), :] = gv * hprev
            return kv[0:1, :]

        carry[0:1, :] = lax.fori_loop(0, rb, blk, carry[0:1, :])
        dxr, dwa, dwx, dba, dbx, dlam = gates_vjp((da_s[...], du_s[...]))
        dwa_ref[...] += dwa
        dwx_ref[...] += dwx
        dba_ref[...] += dba
        dbx_ref[...] += dbx
        dlam_ref[...] += dlam
        dcb_ref[...] += _rowsum(dxr)
        ft = _taps_future(dxr, fut[...])
        x_in = x_ref[...].astype(F32)
        for k in range(4):
            dcw_ref[k:k + 1, :] += _rowsum(ft[k] * x_in)
        cwv = cw_ref[...]
        dx = ft[0] * cwv[0:1, :]
        for k in range(1, 4):
            dx = dx + ft[k] * cwv[k:k + 1, :]
        fut[...] = dxr[0:8, :]
        dx_ref[...] = dx.astype(dx_ref.dtype)

    row_in = lambda: pl.BlockSpec((rows, D), lambda i: (rev(i), 0))
    prev_h = pl.BlockSpec((8, D), lambda i: (jnp.maximum(rev(i) * rb - 1, 0), 0))
    wspec = lambda: _full((4, LRU_BLK, LRU_BLK))
    return pl.pallas_call(
        body, name="lru_bwd", grid=(n_t,),
        in_specs=[row_in()] + _proj_specs(rows, (1, 2), rev) + [row_in(), prev_h, row_in()] + [
            _full((4, D)), wspec(), wspec(), _full((1, D)), _full((1, D)), _full((1, D))],
        out_specs=[row_in(), row_in(), _full((4, D)), _full((1, D)), wspec(), wspec(), _full((1, D)), _full((1, D)),
                   _full((1, D))],
        out_shape=[jax.ShapeDtypeStruct((t_n, D), MXU)] * 2 + [
            jax.ShapeDtypeStruct(s, F32) for s in ((4, D), (1, D), (4, LRU_BLK, LRU_BLK), (4, LRU_BLK, LRU_BLK),
                                                   (1, D), (1, D), (1, D))],
        scratch_shapes=[pltpu.VMEM((rows, D), F32), pltpu.VMEM((rows, D), F32), pltpu.VMEM((rows + 8, D), F32),
                        pltpu.VMEM((rows, D), F32), pltpu.VMEM((rows, D), F32), pltpu.VMEM((8, D), F32),
                        pltpu.VMEM((8, D), F32)],
        compiler_params=_cp("arbitrary"),
    )(dyb, *_proj_ops(proj, (1, 2)), h, h, xr_saved, cw, wa, wx, ba, bx, lam)


def _merge_out(ya, yb, proj, x, wout, g1):
    t_n = x.shape[0]
    tm = min(t_n, 512)

    def body(ya_ref, yb_ref, ga_ref, gb_ref, x_ref, w_ref, g_ref, h1_ref, mix_ref, mg_ref):
        merged = (_sigmoid(ga_ref[...].astype(F32)) * ya_ref[...].astype(F32)
                  + _sigmoid(gb_ref[...].astype(F32)) * yb_ref[...].astype(F32))
        mg = merged.astype(MXU)
        mg_ref[...] = mg
        mix = jnp.dot(mg, w_ref[...], preferred_element_type=F32)
        mix_ref[...] = mix
        h1_ref[...] = x_ref[...] + _rms(mix, g_ref[...])

    row = lambda: pl.BlockSpec((tm, D), lambda i: (i, 0))
    return pl.pallas_call(
        body, name="merge_out", grid=(t_n // tm,),
        in_specs=[row(), row()] + _proj_specs(tm, (3, 4), lambda i: i) + [row(), _full((D, D)), _full((1, D))],
        out_specs=[row(), row(), row()],
        out_shape=[jax.ShapeDtypeStruct((t_n, D), F32), jax.ShapeDtypeStruct((t_n, D), F32),
                   jax.ShapeDtypeStruct((t_n, D), MXU)],
        compiler_params=_cp("parallel"),
    )(ya, yb, *_proj_ops(proj, (3, 4)), x, wout, g1)


def _out_bwd(dv, h1, g2, dout, mix, ya, yb, proj, wout, g1, w_in_pt):
    t_n = dv.shape[0]
    tm = min(t_n, 256)

    def body(dv_ref, h1_ref, g2_ref, dout_ref, mix_ref, ya_ref, yb_ref, ga_ref, gb_ref, w_ref, g_ref, wa_ref, wb_ref,
             dh1_ref, dmix_ref, dya_ref, dyb_ref, dga_ref, dgb_ref, dg2_ref, dg1_ref, dug_ref):
        @pl.when(pl.program_id(0) == 0)
        def _():
            dg1_ref[...] = jnp.zeros_like(dg1_ref)
            dg2_ref[...] = jnp.zeros_like(dg2_ref)

        dx, dg_rows = _rms_bwd(h1_ref[...], g2_ref[...], dv_ref[...])
        dg2_ref[...] += _rowsum(dg_rows)
        dh1 = dout_ref[...] + dx
        dh1_ref[...] = dh1
        dmix, dg_rows = _rms_bwd(mix_ref[...], g_ref[...], dh1)
        dg1_ref[...] += _rowsum(dg_rows)
        dmix_b = dmix.astype(MXU)
        dmix_ref[...] = dmix_b
        dmg = lax.dot_general(dmix_b, w_ref[...], (((1,), (1,)), ((), ())), preferred_element_type=F32)
        sa = _sigmoid(ga_ref[...].astype(F32))
        sb = _sigmoid(gb_ref[...].astype(F32))
        dya_ref[...] = dmg * sa
        dyb_ref[...] = dmg * sb
        dga = (dmg * ya_ref[...].astype(F32) * sa * (1.0 - sa)).astype(MXU)
        dga_ref[...] = dga
        dug = jnp.dot(dga, wa_ref[...], preferred_element_type=F32)
        dgb = (dmg * yb_ref[...].astype(F32) * sb * (1.0 - sb)).astype(MXU)
        dgb_ref[...] = dgb
        dug_ref[...] = dug + jnp.dot(dgb, wb_ref[...], preferred_element_type=F32)

    row = lambda: pl.BlockSpec((tm, D), lambda i: (i, 0))
    vec = lambda: _full((1, D))
    seg = lambda s: pl.BlockSpec((SEG, D), lambda i: (s, 0))
    return pl.pallas_call(
        body, name="out_bwd", grid=(t_n // tm,),
        in_specs=[row(), row(), vec(), row(), row(), row(), row()] + _proj_specs(tm, (3, 4), lambda i: i)
        + [_full((D, D)), vec(), seg(3), seg(4)],
        out_specs=[row(), row(), row(), row(), row(), row(), vec(), vec(), row()],
        out_shape=[jax.ShapeDtypeStruct((t_n, D), F32), jax.ShapeDtypeStruct((t_n, D), MXU),
                   jax.ShapeDtypeStruct((t_n, D), F32), jax.ShapeDtypeStruct((t_n, D), F32),
                   jax.ShapeDtypeStruct((t_n, D), MXU), jax.ShapeDtypeStruct((t_n, D), MXU),
                   jax.ShapeDtypeStruct((1, D), F32), jax.ShapeDtypeStruct((1, D), F32),
                   jax.ShapeDtypeStruct((t_n, D), F32)],
        compiler_params=_cp("arbitrary"),
    )(dv, h1, g2, dout, mix, ya, yb, *_proj_ops(proj, (3, 4)), wout, g1, w_in_pt, w_in_pt)


MLP_TM = 1024
MLP_TF_FWD = 1024
MLP_TF_BWD = 1024


def _mlp_fwd(h1, g2, wup, wdown, g3, tgt):
    t_n = h1.shape[0]
    tm = min(t_n, MLP_TM)
    n_f = FF // MLP_TF_FWD

    def body(h1_ref, g2_ref, wu_ref, wd_ref, g3_ref, tgt_ref, hp_ref, v_ref, dout_ref, dff_ref, loss_ref, dg3_ref, acc):
        i, j = pl.program_id(0), pl.program_id(1)

        @pl.when((i == 0) & (j == 0))
        def _():
            loss_ref[...] = jnp.zeros_like(loss_ref)
            dg3_ref[...] = jnp.zeros_like(dg3_ref)

        @pl.when(j == 0)
        def _():
            v_ref[...] = _rms(h1_ref[...], g2_ref[...]).astype(MXU)
            acc[...] = jnp.zeros_like(acc)

        hp = lax.dot_general(v_ref[...], wu_ref[...], (((1,), (1,)), ((), ())), preferred_element_type=F32)
        hp_ref[...] = hp.astype(MXU)
        hid = jnp.square(jnp.maximum(hp, 0.0))
        acc[...] += jnp.dot(hid.astype(MXU), wd_ref[...], preferred_element_type=F32)

        @pl.when(j == n_f - 1)
        def _():
            ff = acc[...]
            err = h1_ref[...] + _rms(ff, g3_ref[...]) - tgt_ref[...]
            loss_ref[...] += 0.5 * jnp.sum(jnp.mean(err * err, axis=-1, keepdims=True), axis=0, keepdims=True)
            dout = err * (1.0 / D)
            dout_ref[...] = dout
            dff, dg_rows = _rms_bwd(ff, g3_ref[...], dout)
            dg3_ref[...] += _rowsum(dg_rows)
            dff_ref[...] = dff.astype(MXU)

    row = lambda: pl.BlockSpec((tm, D), lambda i, j: (i, 0))
    vec = lambda: pl.BlockSpec((1, D), lambda i, j: (0, 0))
    return pl.pallas_call(
        body, name="mlp_fwd", grid=(t_n // tm, n_f),
        in_specs=[row(), vec(), pl.BlockSpec((MLP_TF_FWD, D), lambda i, j: (j, 0)),
                  pl.BlockSpec((MLP_TF_FWD, D), lambda i, j: (j, 0)), vec(), row()],
        out_specs=[pl.BlockSpec((tm, MLP_TF_FWD), lambda i, j: (i, j)), row(), row(), row(),
                   pl.BlockSpec((1, 1), lambda i, j: (0, 0)), vec()],
        out_shape=[jax.ShapeDtypeStruct((t_n, FF), MXU), jax.ShapeDtypeStruct((t_n, D), MXU),
                   jax.ShapeDtypeStruct((t_n, D), F32), jax.ShapeDtypeStruct((t_n, D), MXU),
                   jax.ShapeDtypeStruct((1, 1), F32), jax.ShapeDtypeStruct((1, D), F32)],
        scratch_shapes=[pltpu.VMEM((tm, D), F32)],
        compiler_params=_cp("arbitrary", "arbitrary"),
    )(h1, g2, wup, wdown, g3, tgt)


def _mlp_bwd(dff, hp, wup, wdown):
    t_n = dff.shape[0]
    tm = min(t_n, MLP_TM)
    n_f = FF // MLP_TF_BWD

    def mm_body(dff_ref, hp_ref, wu_ref, wd_ref, dv_ref, dhp_ref, hid_ref):
        @pl.when(pl.program_id(1) == 0)
        def _():
            dv_ref[...] = jnp.zeros_like(dv_ref)

        relu = jnp.maximum(hp_ref[...].astype(F32), 0.0)
        hid_ref[...] = jnp.square(relu).astype(MXU)
        dhid = lax.dot_general(dff_ref[...], wd_ref[...], (((1,), (1,)), ((), ())), preferred_element_type=F32)
        dhp = (dhid * (2.0 * relu)).astype(MXU)
        dhp_ref[...] = dhp
        dv_ref[...] += jnp.dot(dhp, wu_ref[...], preferred_element_type=F32)

    row = lambda: pl.BlockSpec((tm, D), lambda i, j: (i, 0))
    blk = lambda: pl.BlockSpec((tm, MLP_TF_BWD), lambda i, j: (i, j))
    wblk = lambda: pl.BlockSpec((MLP_TF_BWD, D), lambda i, j: (j, 0))
    return pl.pallas_call(
        mm_body, name="mlp_bwd", grid=(t_n // tm, n_f),
        in_specs=[row(), blk(), wblk(), wblk()], out_specs=[row(), blk(), blk()],
        out_shape=[jax.ShapeDtypeStruct((t_n, D), F32), jax.ShapeDtypeStruct((t_n, FF), MXU),
                   jax.ShapeDtypeStruct((t_n, FF), MXU)],
        compiler_params=_cp("parallel", "arbitrary"),
    )(dff, hp, wup, wdown)


def _wgrad(a, g, name):
    t_n, k_n = a.shape
    n_n = g.shape[1]
    tt = min(t_n, 1024)
    tk, tn = min(k_n, 1024), min(n_n, 1024)

    n_t = t_n // tt

    def body(a_ref, g_ref, o_ref, acc):
        t = pl.program_id(2)

        @pl.when(t == 0)
        def _():
            acc[...] = jnp.zeros_like(acc)

        acc[...] += lax.dot_general(a_ref[...], g_ref[...], (((0,), (0,)), ((), ())), preferred_element_type=F32)

        @pl.when(t == n_t - 1)
        def _():
            o_ref[...] = acc[...].astype(o_ref.dtype)

    return pl.pallas_call(
        body, name=name, grid=(k_n // tk, n_n // tn, n_t),
        in_specs=[pl.BlockSpec((tt, tk), lambda k, n, t: (t, k)), pl.BlockSpec((tt, tn), lambda k, n, t: (t, n))],
        out_specs=pl.BlockSpec((tk, tn), lambda k, n, t: (k, n)),
        out_shape=jax.ShapeDtypeStruct((k_n, n_n), MXU),
        scratch_shapes=[pltpu.VMEM((tk, tn), F32)],
        compiler_params=_cp("parallel", "parallel", "arbitrary"),
    )(a, g)


def _wgrad_segs(segs, g, name):
    t_n, n_n = g.shape
    n_s = len(segs)
    tt = min(t_n, 1024)
    n_t = t_n // tt

    def body(*refs):
        a_refs = refs[:n_s]
        g_ref, o_ref, acc = refs[n_s:]
        s_id, t = pl.program_id(0), pl.program_id(1)

        @pl.when(t == 0)
        def _():
            acc[...] = jnp.zeros_like(acc)

        for s in range(n_s):
            @pl.when(s_id == s)
            def _(s=s):
                acc[...] += lax.dot_general(a_refs[s][...], g_ref[...], (((0,), (0,)), ((), ())),
                                            preferred_element_type=F32)

        @pl.when(t == n_t - 1)
        def _():
            o_ref[...] = acc[...].astype(o_ref.dtype)

    seg_spec = lambda s: pl.BlockSpec((tt, SEG), lambda i, t: (jnp.where(i == s, t, jnp.where(i < s, 0, n_t - 1)), 0))
    return pl.pallas_call(
        body, name=name, grid=(n_s, n_t),
        in_specs=[seg_spec(s) for s in range(n_s)] + [pl.BlockSpec((tt, n_n), lambda i, t: (t, 0))],
        out_specs=pl.BlockSpec((SEG, n_n), lambda i, t: (i, 0)),
        out_shape=jax.ShapeDtypeStruct((n_s * SEG, n_n), MXU),
        scratch_shapes=[pltpu.VMEM((SEG, n_n), F32)],
        compiler_params=_cp("arbitrary", "arbitrary"),
    )(*segs, g)


def _inproj_bwd(dsegs, seg_ids, du_part, w, x, g0, dh1):
    t_n = x.shape[0]
    tm = min(t_n, 512)
    n_k = len(dsegs)

    def body(*refs):
        dp_refs = refs[:n_k]
        w_refs = refs[n_k:2 * n_k]
        part_ref, x_ref, g_ref, dh1_ref, dx_ref, dg0_ref = refs[2 * n_k:]

        @pl.when(pl.program_id(0) == 0)
        def _():
            dg0_ref[...] = jnp.zeros_like(dg0_ref)

        du = part_ref[...]
        for s in range(n_k):
            du = du + jnp.dot(dp_refs[s][...], w_refs[s][...], preferred_element_type=F32)
        dx, dg_rows = _rms_bwd(x_ref[...], g_ref[...], du)
        dg0_ref[...] += _rowsum(dg_rows)
        dx_ref[...] = dh1_ref[...] + dx

    row = lambda: pl.BlockSpec((tm, D), lambda i: (i, 0))
    return pl.pallas_call(
        body, name="inproj_bwd", grid=(t_n // tm,),
        in_specs=[row() for _ in range(n_k)]
        + [pl.BlockSpec((SEG, D), functools.partial(lambda i, s: (s, 0), s=s)) for s in seg_ids]
        + [row(), row(), _full((1, D)), row()],
        out_specs=[row(), _full((1, D))],
        out_shape=[jax.ShapeDtypeStruct((t_n, D), F32), jax.ShapeDtypeStruct((1, D), F32)],
        compiler_params=_cp("arbitrary"),
    )(*dsegs, *([w] * n_k), du_part, x, g0, dh1)


def _blockdiag4(w):
    w4 = w.reshape(4, 4, 64, 1, 64).astype(MXU)
    same = (jnp.arange(4)[:, None, None, None] == jnp.arange(4)[None, None, :, None])
    return jnp.where(same[None], w4, jnp.zeros((), MXU)).reshape(4, 256, 256)


def _blockdiag4_extract(g):
    g5 = g.reshape(4, 4, 64, 4, 64)
    return jnp.stack([g5[:, q, :, q, :] for q in range(4)], axis=1).reshape(NH, 64, 64)


def _local_step(x, tgt, p, after_ssd=None, late_weights=None, send_mlp_grads=None, send_late_grads=None):
    f = lambda a: a.astype(F32)
    proj, u = _inproj(x, p["norm_mix_pre"], p["w_in_pT"])
    ssm_params = (p["cw_xs"], p["cw_bc"], p["cb_xs"], p["cb_bc"], p["dt_bias"], p["a_log"], p["d_skip_x"],
                  p["ssm_norm"])
    ya, sprev, *ssd_saved = _ssd_fwd(proj, *ssm_params)
    cb_lru = p["conv_lru_b"] if after_ssd is None else p["conv_lru_b"] + after_ssd(ya)
    lru_params = (p["conv_lru_w"], cb_lru, p["wa_bd"], p["wx_bd"], p["lru_ba"], p["lru_bx"], p["lru_lambda"])
    yb, h, xr = _lru_fwd(proj, *lru_params)
    if late_weights is not None:
        p = dict(p, **late_weights(yb))
    h1, mix, merged = _merge_out(ya, yb, proj, x, p["w_out"], p["norm_mix_post"])
    hp, v, dout, dff, loss, dg3 = _mlp_fwd(h1, p["norm_mlp_pre"], p["w_upT"], p["w_down"], p["norm_mlp_post"], tgt)

    dv, dhp, hid = _mlp_bwd(dff, hp, p["w_upT"], p["w_down"])
    dh1, dmix, dya, dyb, dga, dgb, dg2, dg1, du_gates = _out_bwd(
        dv, h1, p["norm_mlp_pre"], dout, mix, ya, yb, proj, p["w_out"], p["norm_mix_post"], p["w_in_pT"])
    d_w_out = _wgrad(merged, dmix, "wgrad_out")
    (dz, dxs, dbc, dcwx, dcwb, dcbx, dcbb, ddtb, dalog, ddsk, dnrm, d_w_down, d_w_up_t) = _ssd_bwd(
        dya, ssd_saved, proj, sprev, p["cw_xs"], p["cw_bc"], *ssm_params[4:], (hid, dff, dhp, v))
    ba_lru = p["lru_ba"] if send_mlp_grads is None else p["lru_ba"] + send_mlp_grads(d_w_up_t, d_w_down, d_w_out)
    (dgl, dxl, dcwl, dcbl, dwa, dwx, dba, dbx, dlam) = _lru_bwd(
        dyb, proj, h, xr, p["conv_lru_w"], p["wa_bd"], p["wx_bd"], ba_lru, p["lru_bx"], p["lru_lambda"])
    dsegs = [dz, dgl, dxl, dga, dgb, dxs, dbc]
    d_w_in_pt = _wgrad_segs(dsegs, u, "wgrad_in")
    g0 = p["norm_mix_pre"]
    if send_late_grads is not None:
        g0 = g0 + send_late_grads(d_w_in_pt)
    grad_x, dg0 = _inproj_bwd([dz, dgl, dxl, dxs, dbc], (0, 1, 2, 5, 6), du_gates, p["w_in_pT"], x, g0, dh1)
    grads = dict(
        norm_mix_pre=dg0, w_in_pT=d_w_in_pt, conv_ssm_w=jnp.concatenate([dcwx, dcwb], axis=1),
        conv_ssm_b=jnp.concatenate([dcbx, dcbb], axis=1), dt_bias=ddtb[:, :NH], a_log=dalog[:, :NH],
        d_skip=f(ddsk).reshape(NH, 64).sum(axis=1)[None, :], ssm_norm=dnrm, conv_lru_w=dcwl, conv_lru_b=dcbl,
        lru_wa=_blockdiag4_extract(dwa), lru_ba=dba, lru_wx=_blockdiag4_extract(dwx), lru_bx=dbx, lru_lambda=dlam,
        w_out=d_w_out, norm_mix_post=dg1, norm_mlp_pre=dg2, w_upT=d_w_up_t, w_down=d_w_down, norm_mlp_post=dg3)
    return loss[0, 0], grad_x, grads


W_IN_COLS = 6672


def _w_in_t_to_padded(wt):
    z, xs, bc, dt = wt[0:1024], wt[1024:2048], wt[2048:2560], wt[2560:2576]
    gl, xl, ga, gb = wt[2576:3600], wt[3600:4624], wt[4624:5648], wt[5648:6672]
    return jnp.concatenate([z, gl, xl, ga, gb, xs, bc, dt, jnp.zeros((NP - 6672, wt.shape[1]), wt.dtype)], axis=0)


def _w_in_t_from_padded(wp):
    z, gl, xl, ga, gb = (wp[SEG * s:SEG * (s + 1)] for s in range(5))
    xs, bc, dt = wp[5120:6144], wp[6144:6656], wp[6656:6672]
    return jnp.concatenate([z, xs, bc, dt, gl, xl, ga, gb], axis=0)


def _prep_params(full, big):
    f = lambda a: a.astype(F32)
    pad128 = lambda a: jnp.pad(f(a).reshape(1, -1), ((0, 0), (0, 128 - a.size)))
    cw = f(full["conv_ssm_w"])
    cb = f(full["conv_ssm_b"]).reshape(1, -1)
    return dict(
        big, norm_mix_pre=f(full["norm_mix_pre"]).reshape(1, D),
        cw_xs=cw[:, :D], cw_bc=cw[:, D:], cb_xs=cb[:, :D], cb_bc=cb[:, D:],
        dt_bias=pad128(full["dt_bias"]), a_log=pad128(full["a_log"]),
        d_skip_x=jnp.repeat(f(full["d_skip"]).reshape(-1), 64).reshape(1, D), ssm_norm=f(full["ssm_norm"]).reshape(1, D),
        conv_lru_w=f(full["conv_lru_w"]), conv_lru_b=f(full["conv_lru_b"]).reshape(1, D),
        wa_bd=_blockdiag4(full["lru_wa"]), wx_bd=_blockdiag4(full["lru_wx"]),
        lru_ba=f(full["lru_ba"]).reshape(1, D), lru_bx=f(full["lru_bx"]).reshape(1, D),
        lru_lambda=f(full["lru_lambda"]).reshape(1, D),
        norm_mix_post=f(full["norm_mix_post"]).reshape(1, D), norm_mlp_pre=f(full["norm_mlp_pre"]).reshape(1, D),
        norm_mlp_post=f(full["norm_mlp_post"]).reshape(1, D))


MESH_ID = pl.DeviceIdType.MESH
ANY = pl.BlockSpec(memory_space=pl.ANY)


def _my_place():
    x, y, c = lax.axis_index("x"), lax.axis_index("y"), lax.axis_index("c")
    return x, y, c, 4 * x + 2 * y + c


def _peer(x, y, c, k):
    return (x ^ ((k >> 2) & 1), y ^ ((k >> 1) & 1), c ^ (k & 1))


def _all_gather(pack, name):
    r_n = pack.shape[0]
    half = -(-r_n // 32) * 16
    n_cp = 9

    def body(in_ref, out_ref, send_sems, recv_sems, local_sem):
        x, y, c, me = _my_place()
        here, sibling, x_nbr, y_nbr = (x, y, c), (x, y, 1 - c), (1 - x, y, c), (x, 1 - y, c)
        part = {"all": pl.ds(0, r_n), "lo": pl.ds(0, half), "hi": pl.ds(half, r_n - half)}

        def copy(j, block, rows, to, src=None):
            return pltpu.make_async_remote_copy(
                src_ref=out_ref.at[block, part[rows]] if src is None else src, dst_ref=out_ref.at[block, part[rows]],
                send_sem=send_sems.at[j], recv_sem=recv_sems.at[j], device_id=to, device_id_type=MESH_ID)

        mine = pltpu.make_async_copy(in_ref, out_ref.at[me], local_sem)
        mine.start()
        first = [copy(0, me, "all", sibling, src=in_ref), copy(1, me, "all", x_nbr, src=in_ref),
                 copy(2, me, "all", y_nbr, src=in_ref)]
        for cp in first:
            cp.start()
        relay = [[(3, me ^ 4, "lo", y_nbr), (5, me ^ 4, "all", sibling)],
                 [(4, me ^ 2, "hi", x_nbr), (6, me ^ 2, "all", sibling)],
                 [(7, me ^ 6, "lo", sibling)],
                 [(8, me ^ 6, "hi", sibling)]]
        landed = [(1, me ^ 4, "all"), (2, me ^ 2, "all"), (3, me ^ 6, "lo"), (4, me ^ 6, "hi")]
        passed = []
        for (j, block, rows), nxt in zip(landed, relay):
            copy(j, block, rows, here).wait_recv()
            for args in nxt:
                cp = copy(*args)
                cp.start()
                passed.append(cp)
        for j, block, rows in ((0, me ^ 1, "all"), (5, me ^ 5, "all"), (6, me ^ 3, "all"), (7, me ^ 7, "lo"),
                               (8, me ^ 7, "hi")):
            copy(j, block, rows, here).wait_recv()
        for cp in first + passed:
            cp.wait_send()
        mine.wait()

    return pl.pallas_call(
        body, name=name, in_specs=[ANY], out_specs=ANY,
        out_shape=jax.ShapeDtypeStruct((N_DEV,) + pack.shape, pack.dtype),
        scratch_shapes=[pltpu.SemaphoreType.DMA((n_cp,)), pltpu.SemaphoreType.DMA((n_cp,)), pltpu.SemaphoreType.DMA],
    )(pack)


HBM = pl.BlockSpec(memory_space=pltpu.HBM)
SEM = pl.BlockSpec(memory_space=pltpu.SEMAPHORE)
PLAN_GATHER = tuple((k, "pack", 0) for k in range(1, N_DEV))
PLAN_SCATTER = tuple((k, "slot", 0) for k in range(1, N_DEV))
PLAN_GATHER_ICI = tuple((k, "pack", 0) for k in (2, 4, 6))
PLAN_GATHER_D2D = ((1, "pack", 0),) + tuple((1, s, s) for s in (2, 4, 6))
PLAN_SCATTER_CHIPS = tuple((k, "chip", "chip") for k in (2, 4, 6))


def _plan_copy(j, plan, src_ref, land_ref, sems):
    k, source, slot = plan[j]
    x, y, c, me = _my_place()
    if source == "pack":
        src = src_ref
    elif source == "slot":
        src = src_ref.at[me ^ k]
    elif source == "chip":
        src = src_ref.at[(me ^ k) >> 1]
    else:
        src = land_ref.at[me ^ source]
    dst = land_ref.at[me >> 1] if slot == "chip" else land_ref.at[me ^ slot]
    return pltpu.make_async_remote_copy(
        src_ref=src, dst_ref=dst, send_sem=sems[j], recv_sem=sems[len(plan) + j],
        device_id=_peer(x, y, c, k), device_id_type=MESH_ID)


def _exchange_start(src, land, plan, name):
    n_c = len(plan)
    if land is None:
        slots = src.shape[0] if src.ndim == 3 else N_DEV
        land = pltpu.with_memory_space_constraint(lax.empty((slots,) + src.shape[-2:], src.dtype), pltpu.HBM)

    def body(src_ref, land_ref, *rest):
        sems, token = rest[:2 * n_c], rest[2 * n_c + 2]
        for j in range(n_c):
            _plan_copy(j, plan, src_ref, land_ref, sems).start()
        token[...] = jnp.zeros_like(token)

    outs = pl.pallas_call(
        body, name=name,
        out_shape=(pltpu.SemaphoreType.DMA(()),) * (2 * n_c) + (
            pltpu.HBM(src.shape, src.dtype), pltpu.HBM(land.shape, land.dtype), jax.ShapeDtypeStruct((8, 128), F32)),
        in_specs=(HBM, HBM), out_specs=(SEM,) * (2 * n_c) + (HBM, HBM, pl.BlockSpec(memory_space=pltpu.VMEM)),
        input_output_aliases={0: 2 * n_c, 1: 2 * n_c + 1},
        compiler_params=pltpu.CompilerParams(has_side_effects=pltpu.SideEffectType.DATAFLOW_SIDE_EFFECTING),
    )(pltpu.with_memory_space_constraint(src, pltpu.HBM), land)
    return outs[:2 * n_c], outs[2 * n_c], outs[2 * n_c + 1], outs[2 * n_c + 2]


def _exchange_wait(sems, src_thru, land_thru, after, plan, name):
    n_c = len(plan)

    def body(src_ref, land_ref, *rest):
        for j in range(n_c):
            cp = _plan_copy(j, plan, src_ref, land_ref, rest[:2 * n_c])
            cp.wait_send()
            cp.wait_recv()

    return pl.pallas_call(
        body, name=name,
        out_shape=(pltpu.HBM(src_thru.shape, src_thru.dtype), pltpu.HBM(land_thru.shape, land_thru.dtype)),
        in_specs=(HBM, HBM) + (SEM,) * (2 * n_c) + (pl.BlockSpec(memory_space=pl.ANY),), out_specs=(HBM, HBM),
        input_output_aliases={0: 0, 1: 1},
        compiler_params=pltpu.CompilerParams(has_side_effects=pltpu.SideEffectType.DATAFLOW_SIDE_EFFECTING),
    )(src_thru, land_thru, *sems, after)


def _pair_exchange(src, name):
    n_s = N_DEV // 2

    def body(src_ref, land_ref, send_sems, recv_sems):
        x, y, c, _ = _my_place()
        cps = [pltpu.make_async_remote_copy(
            src_ref=src_ref.at[2 * i + 1 - c], dst_ref=land_ref.at[i], send_sem=send_sems.at[i],
            recv_sem=recv_sems.at[i], device_id=(x, y, 1 - c), device_id_type=MESH_ID) for i in range(n_s)]
        for cp in cps:
            cp.start()
        for cp in cps:
            cp.wait_recv()
        for cp in cps:
            cp.wait_send()

    return pl.pallas_call(
        body, name=name, in_specs=[ANY], out_specs=ANY,
        out_shape=jax.ShapeDtypeStruct((n_s,) + src.shape[1:], src.dtype),
        scratch_shapes=[pltpu.SemaphoreType.DMA((n_s,)), pltpu.SemaphoreType.DMA((n_s,))],
    )(src)


def _pair_sum(src, land, name):
    n_s, r_n, c_n = land.shape
    tr = max(t for t in range(16, 513, 16) if r_n % t == 0)

    def body(c_ref, a_ref, b_ref, o_ref):
        o_ref[...] = (a_ref[...].astype(F32) + b_ref[...].astype(F32)).astype(o_ref.dtype)

    blk = lambda: pl.BlockSpec((1, tr, c_n), lambda i, j, c_ref: (i, j, 0))
    return pl.pallas_call(
        body, name=name,
        grid_spec=pltpu.PrefetchScalarGridSpec(
            num_scalar_prefetch=1, grid=(n_s, r_n // tr),
            in_specs=[pl.BlockSpec((1, tr, c_n), lambda i, j, c_ref: (2 * i + c_ref[0], j, 0)), blk()],
            out_specs=blk()),
        out_shape=jax.ShapeDtypeStruct(land.shape, land.dtype),
        compiler_params=_cp("parallel", "parallel"),
    )(lax.axis_index("c").astype(jnp.int32).reshape(1), src, land)


def _slot_sum(parts, name):
    n_s, r_n, c_n = parts.shape
    tr = max(t for t in range(16, 513, 16) if r_n % t == 0)

    def body(p_ref, o_ref):
        acc = p_ref[0].astype(F32)
        for k in range(1, n_s):
            acc = acc + p_ref[k].astype(F32)
        o_ref[...] = acc

    return pl.pallas_call(
        body, name=name, grid=(r_n // tr,),
        in_specs=[pl.BlockSpec((n_s, tr, c_n), lambda i: (0, i, 0))],
        out_specs=pl.BlockSpec((tr, c_n), lambda i: (i, 0)),
        out_shape=jax.ShapeDtypeStruct((r_n, c_n), F32),
        compiler_params=_cp("parallel"),
    )(parts)


def _adam_math(w, g, m, v):
    m = ADAM_B1 * m + (1.0 - ADAM_B1) * g
    v = ADAM_B2 * v + (1.0 - ADAM_B2) * jnp.square(g)
    m_hat = m / (1.0 - ADAM_B1 ** ADAM_STEP)
    v_hat = v / (1.0 - ADAM_B2 ** ADAM_STEP)
    return -ADAM_LR * (m_hat / (jnp.sqrt(v_hat) + ADAM_EPS) + ADAM_WD * w), m, v


def _adam_big(w, g, m, v, name):
    def body(w_ref, g_ref, m_ref, v_ref, d_ref, mo_ref, vo_ref):
        d_ref[...], mo_ref[...], vo_ref[...] = _adam_math(w_ref[...], g_ref[...], m_ref[...], v_ref[...])

    if w.ndim == 3:
        _, r_n, c_n = w.shape
        tr = min(r_n, 256)
        grid = (r_n // tr,)
        blk = lambda: pl.BlockSpec((1, tr, c_n), lambda i: (0, i, 0))
    else:
        r_n, c_n = w.shape
        tc = min(c_n, 256)
        grid = (c_n // tc,)
        blk = lambda: pl.BlockSpec((r_n, tc), lambda i: (0, i))
    return pl.pallas_call(
        body, name=name, grid=grid, in_specs=[blk(), blk(), blk(), blk()], out_specs=[blk(), blk(), blk()],
        out_shape=[jax.ShapeDtypeStruct(w.shape, F32)] * 3, compiler_params=_cp("parallel"),
    )(w, g, m, v)


def _adam_small(groups, where, wmv):
    n, n_g = len(wmv), len(groups)

    def body(*refs):
        g_refs = refs[:n_g]
        w_refs = refs[n_g:n_g + 3 * n]
        o_refs = refs[n_g + 3 * n:]
        for q in range(n):
            w_ref, m_ref, v_ref = w_refs[3 * q:3 * q + 3]
            r, c = w_ref.shape
            gi, r0 = where[q]
            g = g_refs[gi][0, r0:r0 + r, 0:c]
            for k in range(1, N_DEV):
                g = g + g_refs[gi][k, r0:r0 + r, 0:c]
            d, m, v = _adam_math(w_ref[...], g, m_ref[...], v_ref[...])
            o_refs[4 * q][...] = g
            o_refs[4 * q + 1][...] = d
            o_refs[4 * q + 2][...] = m
            o_refs[4 * q + 3][...] = v

    flat_wmv = [a for t in wmv for a in t]
    vm = pl.BlockSpec(memory_space=pltpu.VMEM)
    outs = pl.pallas_call(
        body, name="adam_small", in_specs=[vm] * (n_g + 3 * n), out_specs=[vm] * (4 * n),
        out_shape=[jax.ShapeDtypeStruct(t[0].shape, F32) for t in wmv for _ in range(4)],
        compiler_params=pltpu.CompilerParams(vmem_limit_bytes=VMEM_LIMIT),
    )(*groups, *flat_wmv)
    return [tuple(outs[4 * q:4 * q + 4]) for q in range(n)]


WEIGHTS = ["norm_mix_pre", "w_in", "conv_ssm_w", "conv_ssm_b", "dt_bias", "a_log", "d_skip", "ssm_norm", "conv_lru_w",
           "conv_lru_b", "lru_wa", "lru_ba", "lru_wx", "lru_bx", "lru_lambda", "w_out", "norm_mix_post", "norm_mlp_pre",
           "w_up", "w_down", "norm_mlp_post"]
BIG = ["w_out", "w_up", "w_down", "w_in"]
IN_ROWS = W_IN_COLS // N_DEV
IN_PAD, EARLY_ROWS = 848, 880
ROW_UP, ROW_DOWN, LATE_ROWS = 128, 640, 1152
GRAD_LATE_ROWS = 864
CONV_SSM_COLS, CONV_LRU_COLS = 1536 // N_DEV, D // N_DEV
SMALL = [("norm_mix_pre", (1, D), 0, 0), ("ssm_norm", (1, D), 0, 1), ("conv_lru_b", (1, D), 0, 2),
         ("lru_lambda", (1, D), 0, 3), ("norm_mix_post", (1, D), 0, 4), ("norm_mlp_pre", (1, D), 0, 5),
         ("norm_mlp_post", (1, D), 0, 6), ("conv_ssm_b", (1, 1536), 1, 0), ("dt_bias", (1, NH), 2, 0),
         ("a_log", (1, NH), 2, 1), ("d_skip", (1, NH), 2, 2), ("conv_ssm_w", (4, CONV_SSM_COLS), 3, 0),
         ("conv_lru_w", (4, CONV_LRU_COLS), 4, 0), ("lru_wa", (D, 64), 5, 0), ("lru_wx", (D, 64), 5, D),
         ("lru_ba", (NH, 64), 6, 0), ("lru_bx", (NH, 64), 6, NH)]
SMALL_GROUPS = [(8, D), (1, 1536), (8, 128), (4, 1536), (4, D), (2 * D, 64), (2 * NH, 64)]


def _pad_rows(flat, mult):
    n = flat.shape[0]
    rows = -(-n // (128 * mult)) * mult
    return jnp.pad(flat, (0, rows * 128 - n)).reshape(rows, 128)


def _split3(a):
    hi = a.astype(MXU)
    r1 = a - hi.astype(F32)
    mid = r1.astype(MXU)
    lo = (r1 - mid.astype(F32)).astype(MXU)
    return jnp.stack([hi, mid, lo])


def _early_pack(a, me):
    bf = lambda t: t.astype(MXU)
    conv = lambda t, c: jnp.pad(_split3(t).reshape(12, c), ((0, 4), (0, D - c)))
    shifted = lax.dynamic_update_slice(jnp.zeros((IN_PAD, D), MXU), bf(a["w_in"][0]).T, (2 * me, 0))
    return jnp.concatenate([shifted, conv(a["conv_ssm_w"][0], CONV_SSM_COLS), conv(a["conv_lru_w"][0], CONV_LRU_COLS)],
                           axis=0)


TILE = 16
SHARD_TILES = IN_PAD // TILE
SHARD_STEP = (IN_ROWS // TILE)
SEG_TILES = ((0, 64, 0), (64, 128, 320), (128, 160, 384), (160, 161, 416), (161, 225, 64), (225, 289, 128),
             (289, 353, 192), (353, 417, 256))


def _tile_runs(lo, hi):
    runs = []
    for s0, s1, d0 in SEG_TILES:
        a, b = max(lo, s0), min(hi, s1)
        if a < b:
            runs.append((a, b - a, d0 + a - s0))
    return runs


def _assemble_w_in(g):
    whole = []
    for k in range(N_DEV):
        lo = SHARD_STEP * k + (1 if k else 0)
        hi = SHARD_STEP * (k + 1) + (1 if k == N_DEV - 1 else 0)
        whole += [(k, a - SHARD_STEP * k, n, d) for a, n, d in _tile_runs(lo, hi)]
    split = [(k, _tile_runs(SHARD_STEP * k, SHARD_STEP * k + 1)[0][2]) for k in range(1, N_DEV)]

    def body(g_ref, o_ref):
        rows = lambda t, n=1: pl.ds(TILE * t, TILE * n)
        for k, t, n, d in whole:
            o_ref[rows(d, n), :] = g_ref[k, rows(t, n), :]
        for k, d in split:
            o_ref[rows(d), :] = g_ref[k - 1, rows(SHARD_STEP), :] + g_ref[k, rows(0), :]
        o_ref[pl.ds(W_IN_COLS, NP - W_IN_COLS), :] = jnp.zeros((NP - W_IN_COLS, D), o_ref.dtype)

    vm = pl.BlockSpec(memory_space=pltpu.VMEM)
    return pl.pallas_call(
        body, name="assemble_w_in", in_specs=[vm], out_specs=vm, out_shape=jax.ShapeDtypeStruct((NP, D), g.dtype),
        compiler_params=pltpu.CompilerParams(vmem_limit_bytes=VMEM_LIMIT),
    )(g)


def _scatter_w_in_grad(dw):
    runs = [(k, a - SHARD_STEP * k, n, d) for k in range(N_DEV)
            for a, n, d in _tile_runs(SHARD_STEP * k, SHARD_STEP * k + SHARD_TILES)]
    pad = GRAD_LATE_ROWS - IN_PAD

    def body(dw_ref, o_ref):
        rows = lambda t, n: pl.ds(TILE * t, TILE * n)
        for k, t, n, d in runs:
            o_ref[k, rows(t, n), :] = dw_ref[rows(d, n), :]
        for k in range(N_DEV):
            o_ref[k, pl.ds(IN_PAD, pad), :] = jnp.zeros((pad, D), o_ref.dtype)

    vm = pl.BlockSpec(memory_space=pltpu.VMEM)
    return pl.pallas_call(
        body, name="scatter_w_in_grad", in_specs=[vm], out_specs=vm,
        out_shape=jax.ShapeDtypeStruct((N_DEV, GRAD_LATE_ROWS, D), dw.dtype),
        compiler_params=pltpu.CompilerParams(vmem_limit_bytes=VMEM_LIMIT),
    )(dw)


def _early_unpack(g):
    w_in_pt = _assemble_w_in(g)
    conv = {}
    for n, r0, c in (("conv_ssm_w", IN_PAD, CONV_SSM_COLS), ("conv_lru_w", IN_PAD + 16, CONV_LRU_COLS)):
        s = g[:, r0:r0 + 12, :c].astype(F32).reshape(N_DEV, 3, 4, c)
        conv[n] = ((s[:, 0] + s[:, 1]) + s[:, 2]).transpose(1, 0, 2).reshape(4, N_DEV * c)
    return w_in_pt, conv


def _late_pack(a):
    bf = lambda t: t.astype(MXU)
    return jnp.concatenate([bf(a["w_out"][0]), bf(a["w_up"][0]).T, bf(a["w_down"][0])], axis=0)


def _late_unpack(g):
    return dict(w_out=g[:, :ROW_UP].reshape(D, D), w_upT=g[:, ROW_UP:ROW_DOWN].reshape(FF, D),
                w_down=g[:, ROW_DOWN:].reshape(FF, D))


def _own_slot(land, own):
    me = 4 * lax.axis_index("x") + 2 * lax.axis_index("y") + lax.axis_index("c")
    return lax.dynamic_update_slice_in_dim(land, own[None], me, axis=0)


def kernel(x, norm_mix_pre, w_in, conv_ssm_w, conv_ssm_b, dt_bias, a_log, d_skip, ssm_norm, conv_lru_w, conv_lru_b, lru_wa, lru_ba, lru_wx, lru_bx, lru_lambda, w_out, norm_mix_post, norm_mlp_pre, w_up, w_down, norm_mlp_post, loss_target, m_norm_mix_pre, m_w_in, m_conv_ssm_w, m_conv_ssm_b, m_dt_bias, m_a_log, m_d_skip, m_ssm_norm, m_conv_lru_w, m_conv_lru_b, m_lru_wa, m_lru_ba, m_lru_wx, m_lru_bx, m_lru_lambda, m_w_out, m_norm_mix_post, m_norm_mlp_pre, m_w_up, m_w_down, m_norm_mlp_post, v_norm_mix_pre, v_w_in, v_conv_ssm_w, v_conv_ssm_b, v_dt_bias, v_a_log, v_d_skip, v_ssm_norm, v_conv_lru_w, v_conv_lru_b, v_lru_wa, v_lru_ba, v_lru_wx, v_lru_bx, v_lru_lambda, v_w_out, v_norm_mix_post, v_norm_mlp_pre, v_w_up, v_w_down, v_norm_mlp_post):
    vals = (norm_mix_pre, w_in, conv_ssm_w, conv_ssm_b, dt_bias, a_log, d_skip, ssm_norm, conv_lru_w, conv_lru_b, lru_wa, lru_ba, lru_wx, lru_bx, lru_lambda, w_out, norm_mix_post, norm_mlp_pre, w_up, w_down, norm_mlp_post)
    m_vals = (m_norm_mix_pre, m_w_in, m_conv_ssm_w, m_conv_ssm_b, m_dt_bias, m_a_log, m_d_skip, m_ssm_norm, m_conv_lru_w, m_conv_lru_b, m_lru_wa, m_lru_ba, m_lru_wx, m_lru_bx, m_lru_lambda, m_w_out, m_norm_mix_post, m_norm_mlp_pre, m_w_up, m_w_down, m_norm_mlp_post)
    v_vals = (v_norm_mix_pre, v_w_in, v_conv_ssm_w, v_conv_ssm_b, v_dt_bias, v_a_log, v_d_skip, v_ssm_norm, v_conv_lru_w, v_conv_lru_b, v_lru_wa, v_lru_ba, v_lru_wx, v_lru_bx, v_lru_lambda, v_w_out, v_norm_mix_post, v_norm_mlp_pre, v_w_up, v_w_down, v_norm_mlp_post)
    w = dict(zip(WEIGHTS, vals))
    m = dict(zip(WEIGHTS, m_vals))
    v = dict(zip(WEIGHTS, v_vals))
    me = 4 * lax.axis_index("x") + 2 * lax.axis_index("y") + lax.axis_index("c")

    bf = lambda t: t.astype(MXU)
    late = _late_pack(w)
    early = _all_gather(_early_pack(w, me), "early_weights_all_gather")
    late, early = lax.optimization_barrier((late, early))
    lw = {}
    lw["sems"], lw["src"], lw["land"], token = _exchange_start(late, None, PLAN_GATHER_ICI, "late_weights_ici_start")
    w_in_pt, conv_w = _early_unpack(early)
    full = {n: (conv_w[n] if n in conv_w else w[n][0]) for n in WEIGHTS if n not in BIG}
    full["norm_mix_pre"] = full["norm_mix_pre"] + token[0, 0]

    def after_ssd(after):
        src, land = _exchange_wait(lw["sems"], lw["src"], lw["land"], after, PLAN_GATHER_ICI, "late_weights_ici_wait")
        lw["sems"], lw["src"], lw["land"], tok = _exchange_start(src, land, PLAN_GATHER_D2D, "late_weights_d2d_start")
        return tok[0, 0]

    def late_weights(after):
        src, land = _exchange_wait(lw["sems"], lw["src"], lw["land"], after, PLAN_GATHER_D2D, "late_weights_d2d_wait")
        return _late_unpack(_own_slot(land, src))

    sent = {}

    def send_mlp_grads(d_w_up_t, d_w_down, d_w_out):
        src = jnp.concatenate([bf(d_w_up_t).reshape(N_DEV, -1, D), bf(d_w_down).reshape(N_DEV, -1, D),
                               bf(d_w_out).reshape(N_DEV, -1, D)], axis=1)
        sent["sems"], sent["src"], sent["land"], tok = _exchange_start(src, None, PLAN_SCATTER, "mlp_grads_start")
        return tok[0, 0]

    def send_late_grads(d_w_in_pt):
        src = _scatter_w_in_grad(d_w_in_pt)
        chip = _pair_sum(src, _pair_exchange(src, "late_grads_pair_exchange"), "late_grads_pair_sum")
        sent["sems2"], sent["src2"], sent["land2"], tok = _exchange_start(chip, None, PLAN_SCATTER_CHIPS,
                                                                          "late_grads_start")
        return tok[0, 0]

    loss, grad_x, g = _local_step(x[0], loss_target[0], _prep_params(full, dict(w_in_pT=w_in_pt)), after_ssd,
                                  late_weights, send_mlp_grads, send_late_grads)
    loss = lax.psum(loss, ("x", "y", "c"))

    own = lambda src: lax.dynamic_index_in_dim(src, me, keepdims=False)
    out_g, out_d, out_m, out_v = {}, {}, {}, {}
    mlp_src, mlp_land = _exchange_wait(sent["sems"], sent["src"], sent["land"], grad_x, PLAN_SCATTER, "mlp_grads_wait")
    g_mlp = _slot_sum(_own_slot(mlp_land, own(mlp_src)), "slot_sum_mlp")
    fs = FF // N_DEV
    for n, gn in (("w_up", g_mlp[:fs].T[None]), ("w_down", g_mlp[fs:2 * fs][None]), ("w_out", g_mlp[2 * fs:][None])):
        out_g[n] = gn
        out_d[n], out_m[n], out_v[n] = _adam_big(w[n], gn, m[n], v[n], "adam_" + n)
    zrow = jnp.zeros((1, D), F32)
    pad16 = lambda a: jnp.pad(a, ((0, 0), (0, 128 - NH)))
    small_parts = [
        jnp.concatenate([g["norm_mix_pre"], g["ssm_norm"], g["conv_lru_b"], g["lru_lambda"], g["norm_mix_post"],
                         g["norm_mlp_pre"], g["norm_mlp_post"], zrow], axis=0),
        g["conv_ssm_b"],
        jnp.concatenate([pad16(g["dt_bias"]), pad16(g["a_log"]), pad16(g["d_skip"]), jnp.zeros((5, 128), F32)], axis=0),
        g["conv_ssm_w"], g["conv_lru_w"],
        jnp.concatenate([g["lru_wa"].reshape(D, 64), g["lru_wx"].reshape(D, 64)], axis=0),
        jnp.concatenate([g["lru_ba"].reshape(NH, 64), g["lru_bx"].reshape(NH, 64)], axis=0)]
    small = _pad_rows(jnp.concatenate([s.reshape(-1) for s in small_parts]), 8)
    small, _ = lax.optimization_barrier((small, out_v["w_out"]))
    small_all = _all_gather(small, "small_grads_all_gather")

    lg_src, lg_land = _exchange_wait(sent["sems2"], sent["src2"], sent["land2"], small_all, PLAN_SCATTER_CHIPS,
                                     "late_grads_wait")
    my_chip = me >> 1
    lg_own = lax.dynamic_index_in_dim(lg_src, my_chip, keepdims=True)
    g_late = _slot_sum(lax.dynamic_update_slice_in_dim(lg_land, lg_own, my_chip, axis=0), "slot_sum_late")
    gt = lax.dynamic_slice(g_late, (2 * me, 0), (IN_ROWS, D))
    dt_, mt_, vt_ = _adam_big(w["w_in"][0].T, gt, m["w_in"][0].T, v["w_in"][0].T, "adam_w_in")
    out_g["w_in"], out_d["w_in"], out_m["w_in"], out_v["w_in"] = gt.T[None], dt_.T[None], mt_.T[None], vt_.T[None]
    sflat = small_all.reshape(N_DEV, -1)
    groups = []
    off = 0
    for r, c in SMALL_GROUPS:
        groups.append(sflat[:, off:off + r * c].reshape(N_DEV, r, c))
        off += r * c
    groups[3] = lax.dynamic_slice_in_dim(groups[3], me * CONV_SSM_COLS, CONV_SSM_COLS, axis=2)
    groups[4] = lax.dynamic_slice_in_dim(groups[4], me * CONV_LRU_COLS, CONV_LRU_COLS, axis=2)
    wmv = [(w[n].reshape(s), m[n].reshape(s), v[n].reshape(s)) for n, s, _, _ in SMALL]
    res = _adam_small(groups, [(gi, r0) for _, _, gi, r0 in SMALL], wmv)
    for (n, _, _, _), (g_n, d_n, m_n, v_n) in zip(SMALL, res):
        shape = w[n].shape
        out_g[n], out_d[n], out_m[n], out_v[n] = (g_n.reshape(shape), d_n.reshape(shape), m_n.reshape(shape),
                                                  v_n.reshape(shape))
    return (loss, grad_x[None], *[out_g[n] for n in WEIGHTS], *[out_d[n] for n in WEIGHTS],
            *[out_m[n] for n in WEIGHTS], *[out_v[n] for n in WEIGHTS])
```

```python
import functools

import jax
import jax.numpy as jnp
from jax import lax
from jax.experimental import pallas as pl
from jax.experimental.pallas import tpu as pltpu

F32 = jnp.float32
MXU = jnp.bfloat16
HI = lax.Precision.HIGHEST
EPS = 1e-6

D = 1024
NH = 16
NS = 128
CH = 128
FF = 4096
NP = 7168
SEG = 1024
LRU_C = 8.0
N_DEV = 8

ADAM_LR, ADAM_B1, ADAM_B2, ADAM_EPS, ADAM_WD, ADAM_STEP = 0.001, 0.9, 0.999, 1e-08, 0.01, 10

VMEM_LIMIT = 56 * 1024 * 1024


def _cp(*sem):
    return pltpu.CompilerParams(dimension_semantics=sem, vmem_limit_bytes=VMEM_LIMIT)


def _nn(a, b):
    return jnp.dot(a.astype(MXU), b.astype(MXU), preferred_element_type=F32)


def _nt(a, b):
    return lax.dot_general(a.astype(MXU), b.astype(MXU), (((1,), (1,)), ((), ())), preferred_element_type=F32)


def _tn(a, b):
    return lax.dot_general(a.astype(MXU), b.astype(MXU), (((0,), (0,)), ((), ())), preferred_element_type=F32)


_sigmoid = jax.nn.sigmoid


def _silu(x):
    return x * _sigmoid(x)


def _dsilu(x):
    s = _sigmoid(x)
    return s + x * s * (1.0 - s)


def _softplus(x):
    return jnp.maximum(x, 0.0) + jnp.log(1.0 + jnp.exp(-jnp.abs(x)))


def _rms(x, g):
    r = lax.rsqrt(jnp.mean(x * x, axis=-1, keepdims=True) + EPS)
    return x * r * g


def _rms_bwd(x, g, dy):
    r = lax.rsqrt(jnp.mean(x * x, axis=-1, keepdims=True) + EPS)
    gdy = g * dy
    dx = r * gdy - x * (r * r * r) * jnp.mean(x * gdy, axis=-1, keepdims=True)
    return dx, dy * x * r


def _rowsum(x):
    return jnp.sum(x, axis=0, keepdims=True)


def _taps_past(cur, prev8):
    r_n, c_n = cur.shape
    row = lax.broadcasted_iota(jnp.int32, (r_n, c_n), 0)
    out = []
    for k in range(4):
        s = 3 - k
        if s == 0:
            out.append(cur)
            continue
        head = jnp.concatenate([pltpu.roll(prev8, s, 0), jnp.zeros((r_n - 8, c_n), F32)], axis=0)
        out.append(jnp.where(row < s, head, pltpu.roll(cur, s, 0)))
    return out


def _taps_future(cur, fut8):
    r_n, c_n = cur.shape
    row = lax.broadcasted_iota(jnp.int32, (r_n, c_n), 0)
    out = []
    for k in range(4):
        s = 3 - k
        if s == 0:
            out.append(cur)
            continue
        tail = jnp.concatenate([jnp.zeros((r_n - 8, c_n), F32), pltpu.roll(fut8, 8 - s, 0)], axis=0)
        out.append(jnp.where(row >= r_n - s, tail, pltpu.roll(cur, r_n - s, 0)))
    return out


def _conv_apply(taps, w, b):
    acc = taps[0] * w[0:1, :]
    for k in range(1, 4):
        acc = acc + taps[k] * w[k:k + 1, :]
    return acc + b


def _inproj(x, g0, w):
    t_n = x.shape[0]
    tm = min(t_n, 1024)

    n_j = NP // SEG

    def body(x_ref, g_ref, w_ref, pb_ref, p6_ref, u_ref):
        j = pl.program_id(1)

        @pl.when(j == 0)
        def _():
            u_ref[...] = _rms(x_ref[...], g_ref[...]).astype(MXU)

        p = lax.dot_general(u_ref[...], w_ref[...], (((1,), (1,)), ((), ())), preferred_element_type=F32)

        @pl.when(j < n_j - 1)
        def _():
            pb_ref[...] = p.astype(MXU)

        @pl.when(j == n_j - 1)
        def _():
            p6_ref[...] = p

    pb, p6, u = pl.pallas_call(
        body, name="inproj", grid=(t_n // tm, n_j),
        in_specs=[pl.BlockSpec((tm, D), lambda i, j: (i, 0)), pl.BlockSpec((1, D), lambda i, j: (0, 0)),
                  pl.BlockSpec((SEG, D), lambda i, j: (j, 0))],
        out_specs=[pl.BlockSpec((tm, SEG), lambda i, j: (i, jnp.minimum(j, n_j - 2))),
                   pl.BlockSpec((tm, SEG), lambda i, j: (i, 0)), pl.BlockSpec((tm, D), lambda i, j: (i, 0))],
        out_shape=[jax.ShapeDtypeStruct((t_n, NP - SEG), MXU), jax.ShapeDtypeStruct((t_n, SEG), F32),
                   jax.ShapeDtypeStruct((t_n, D), MXU)],
        compiler_params=_cp("parallel", "arbitrary"),
    )(x, g0, w)
    return (pb, p6), u


def _ssd_prep(dtraw, dtb, alog):
    l_n = dtraw.shape[0]
    r = lax.broadcasted_iota(jnp.int32, (l_n, l_n), 0)
    c = lax.broadcasted_iota(jnp.int32, (l_n, l_n), 1)
    tril = (r >= c).astype(F32)
    triu = (r <= c).astype(F32)
    eye = (r == c).astype(F32)
    dt = _softplus(dtraw + dtb)
    adt = dt * (-jnp.exp(alog))
    ac = jnp.dot(tril, adt, preferred_element_type=F32, precision=HI)
    tn = (((0,), (0,)), ((), ()))
    ac_t = lax.dot_general(adt, triu, tn, preferred_element_type=F32, precision=HI)
    dt_t = lax.dot_general(dt, eye, tn, preferred_element_type=F32, precision=HI)
    return dt, dt_t, ac, ac_t, _rowsum(adt)


def _ssd_pair(j, xp, bg, cg, sp, dt, dt_t, ac, ac_t, aend):
    l_n = xp.shape[0]
    lane = lax.broadcasted_iota(jnp.int32, (l_n, 128), 1)
    sub = lax.broadcasted_iota(jnp.int32, (128, l_n), 0)
    lane1 = lax.broadcasted_iota(jnp.int32, (1, 128), 1)
    tri = lax.broadcasted_iota(jnp.int32, (l_n, l_n), 0) >= lax.broadcasted_iota(jnp.int32, (l_n, l_n), 1)
    lo = lax.broadcasted_iota(jnp.int32, (l_n, 128), 1) < 64
    lo_s = lax.broadcasted_iota(jnp.int32, (128, 128), 1) < 64
    cb = _nt(cg, bg)
    cs = _nn(cg, sp)
    x2 = jnp.concatenate([jnp.where(lo, xp, 0.0), jnp.where(lo, 0.0, xp)], axis=0)
    ws_, bs_, eo, ee = [], [], [], []
    for e in range(2):
        h = 2 * j + e
        ac_l = jnp.sum(jnp.where(lane == h, ac, 0.0), axis=1, keepdims=True)
        dt_l = jnp.sum(jnp.where(lane == h, dt, 0.0), axis=1, keepdims=True)
        a_end = jnp.sum(jnp.where(lane1 == h, aend, 0.0), axis=1, keepdims=True)
        ac_s, dt_s = ac_t[h:h + 1, :], dt_t[h:h + 1, :]
        decay = jnp.exp(jnp.where(tri, ac_l - ac_s, -1e30))
        ws_.append(cb * decay * dt_s)
        bs_.append(bg * (jnp.exp(a_end - ac_l) * dt_l))
        eo.append(jnp.exp(ac_l))
        ee.append(jnp.exp(a_end))
    y = _nn(jnp.concatenate(ws_, axis=1), x2) + jnp.where(lo, eo[0], eo[1]) * cs
    s_new = _tn(jnp.concatenate(bs_, axis=0), x2) + jnp.where(lo_s, ee[0], ee[1]) * sp
    return y, s_new


def _ssd_post(y, xs, z, dsk, nrm):
    y = (y + dsk * xs) * _silu(z)
    half = D // 2
    ya, yb = y[:, :half], y[:, half:]
    ya = ya * lax.rsqrt(jnp.mean(ya * ya, axis=-1, keepdims=True) + EPS)
    yb = yb * lax.rsqrt(jnp.mean(yb * yb, axis=-1, keepdims=True) + EPS)
    return jnp.concatenate([ya, yb], axis=1) * nrm


LAST_SEG = NP // SEG - 1


def _proj_ops(proj, seg_ids):
    return [proj[1] if s == LAST_SEG else proj[0] for s in seg_ids]


def _proj_specs(rows, seg_ids, order):
    return [pl.BlockSpec((rows, SEG), functools.partial(lambda i, c: (order(i), c), c=0 if s == LAST_SEG else s))
            for s in seg_ids]


def _prev_specs(rows, seg_ids, order):
    specs = []
    for s in seg_ids:
        n, c = (8, 0) if s == LAST_SEG else (16, s)
        specs.append(pl.BlockSpec((n, SEG), functools.partial(
            lambda i, n, c: (jnp.maximum(order(i) * (rows // n) - 1, 0), c), n=n, c=c)))
    return specs


def _prev8(ref):
    return ref[...] if ref.shape[0] == 8 else ref[8:16, :].astype(F32)


def _full(shape):
    return pl.BlockSpec(shape, lambda i: (0,) * len(shape))


def _slotted(a):
    return a if isinstance(a, tuple) else (a.reshape(N_DEV, a.shape[0] // N_DEV, a.shape[1]), 0)


def _slot_spec(w, rows, tile, index):
    per = rows // N_DEV
    b = w[1]
    return pl.BlockSpec((tile // per, per, D), lambda *ids: (index(*ids), b, 0))


def _ssd_fwd(proj, cwx, cwb, cbx, cbb, dtb, alog, dsk, nrm):
    t_n = proj[0].shape[0]
    n_c = t_n // CH
    fwd = lambda i: i

    def body(z_ref, xs_ref, bc_ref, xsp_ref, bcp_ref, cwx_ref, cwb_ref, cbx_ref, cbb_ref, dtb_ref, alog_ref,
             dsk_ref, nrm_ref, ya_ref, sprev_ref, yraw_ref, xspre_ref, bcpre_ref, s_ref):
        c = pl.program_id(0)

        @pl.when(c == 0)
        def _():
            s_ref[...] = jnp.zeros_like(s_ref)

        keep = jnp.where(c == 0, 0.0, 1.0)
        xs_pre = _conv_apply(_taps_past(xs_ref[...].astype(F32), _prev8(xsp_ref) * keep), cwx_ref[...], cbx_ref[...])
        bc_pre = _conv_apply(_taps_past(bc_ref[:, :512], bcp_ref[:, :512] * keep), cwb_ref[...], cbb_ref[...])
        xspre_ref[...] = xs_pre
        bcpre_ref[...] = bc_pre
        prep = _ssd_prep(bc_ref[:, 512:640], dtb_ref[...], alog_ref[...])
        xs = _silu(xs_pre)
        bc = _silu(bc_pre)
        sprev_ref[0] = s_ref[...]
        ys = []
        for j in range(NH // 2):
            g = j // 4
            yp, sn = _ssd_pair(j, xs[:, 128 * j:128 * j + 128], bc[:, 128 * g:128 * g + 128],
                               bc[:, 256 + 128 * g:384 + 128 * g], s_ref[:, 128 * j:128 * j + 128], *prep)
            ys.append(yp)
            s_ref[:, 128 * j:128 * j + 128] = sn
        y = jnp.concatenate(ys, axis=1)
        yraw_ref[...] = y
        ya_ref[...] = _ssd_post(y, xs, z_ref[...].astype(F32), dsk_ref[...], nrm_ref[...]).astype(ya_ref.dtype)

    return pl.pallas_call(
        body, name="ssd_fwd", grid=(n_c,),
        in_specs=_proj_specs(CH, (0, 5, 6), fwd) + _prev_specs(CH, (5, 6), fwd) + [
            _full((4, D)), _full((4, 512)), _full((1, D)), _full((1, 512)), _full((1, 128)), _full((1, 128)),
            _full((1, D)), _full((1, D))],
        out_specs=[pl.BlockSpec((CH, D), lambda i: (i, 0)), pl.BlockSpec((1, NS, D), lambda i: (i, 0, 0)),
                   pl.BlockSpec((CH, D), lambda i: (i, 0)), pl.BlockSpec((CH, D), lambda i: (i, 0)),
                   pl.BlockSpec((CH, 512), lambda i: (i, 0))],
        out_shape=[jax.ShapeDtypeStruct((t_n, D), MXU), jax.ShapeDtypeStruct((n_c, NS, D), F32),
                   jax.ShapeDtypeStruct((t_n, D), F32), jax.ShapeDtypeStruct((t_n, D), F32),
                   jax.ShapeDtypeStruct((t_n, 512), F32)],
        scratch_shapes=[pltpu.VMEM((NS, D), F32)],
        compiler_params=_cp("arbitrary"),
    )(*_proj_ops(proj, (0, 5, 6, 5, 6)), cwx, cwb, cbx, cbb, dtb, alog, dsk, nrm)


def _ssd_bwd(dya, saved, proj, sprev, cwx, cwb, dtb, alog, dsk, nrm, mlp_ops):
    t_n = proj[0].shape[0]
    n_c = t_n // CH
    rev = lambda i: n_c - 1 - i
    fb = FF // n_c

    def body(dya_ref, yraw_ref, xspre_ref, bcpre_ref, z_ref, xs_ref, bc_ref, sprev_ref, cwx_ref, cwb_ref,
             dtb_ref, alog_ref, dsk_ref, nrm_ref, hid_ref, dff_ref, dhp_ref, v_ref,
             dz_ref, dxs_ref, dbc_ref, dcwx_ref, dcwb_ref, dcbx_ref, dcbb_ref, ddtb_ref, dalog_ref, ddsk_ref,
             dnrm_ref, dwd_ref, dwu_ref, ds_ref, futx_ref, futb_ref):
        i = pl.program_id(0)
        acc_refs = (dcwx_ref, dcwb_ref, dcbx_ref, dcbb_ref, ddtb_ref, dalog_ref, ddsk_ref, dnrm_ref)
        tn = (((0,), (0,)), ((), ()))
        kt = t_n // (NH // 2)
        dwd_acc = jnp.zeros((fb, D), F32)
        dwu_acc = jnp.zeros((fb, D), F32)

        @pl.when(i == 0)
        def _():
            for r in (ds_ref, futx_ref, futb_ref) + acc_refs:
                r[...] = jnp.zeros_like(r)

        xs_pre = xspre_ref[...]
        bc_pre = bcpre_ref[...]
        xs = _silu(xs_pre)
        bc = _silu(bc_pre)
        prep, prep_vjp = jax.vjp(_ssd_prep, bc_ref[:, 512:640], dtb_ref[...], alog_ref[...])
        s_in = sprev_ref[0]

        def pair_args(j):
            g = j // 4
            return (xs[:, 128 * j:128 * j + 128], bc[:, 128 * g:128 * g + 128],
                    bc[:, 256 + 128 * g:384 + 128 * g], s_in[:, 128 * j:128 * j + 128]) + tuple(prep)

        _, post_vjp = jax.vjp(_ssd_post, yraw_ref[...], xs, z_ref[...].astype(F32), dsk_ref[...], nrm_ref[...])
        dy, dxs_skip, dz, ddsk, dnrm = post_vjp(dya_ref[...])
        dz_ref[...] = dz.astype(dz_ref.dtype)
        ddsk_ref[...] += ddsk
        dnrm_ref[...] += dnrm

        dprep = [jnp.zeros_like(p) for p in prep]
        dxp = []
        dbg = [jnp.zeros((CH, 128), F32), jnp.zeros((CH, 128), F32)]
        dcg = [jnp.zeros((CH, 128), F32), jnp.zeros((CH, 128), F32)]
        for j in range(NH // 2):
            g = j // 4
            _, pair_vjp = jax.vjp(functools.partial(_ssd_pair, j), *pair_args(j))
            cts = pair_vjp((dy[:, 128 * j:128 * j + 128], ds_ref[:, 128 * j:128 * j + 128]))
            dxp.append(cts[0])
            dbg[g] = dbg[g] + cts[1]
            dcg[g] = dcg[g] + cts[2]
            ds_ref[:, 128 * j:128 * j + 128] = cts[3]
            dprep = [a + b for a, b in zip(dprep, cts[4:])]
            rows = pl.ds(kt * j, kt)
            dwd_acc = dwd_acc + lax.dot_general(hid_ref[rows, :], dff_ref[rows, :], tn, preferred_element_type=F32)
            dwu_acc = dwu_acc + lax.dot_general(dhp_ref[rows, :], v_ref[rows, :], tn, preferred_element_type=F32)
        dwd_ref[...] = dwd_acc.astype(MXU)
        dwu_ref[...] = dwu_acc.astype(MXU)
        ddtraw, ddtb, dalog = prep_vjp(tuple(dprep))
        ddtb_ref[...] += ddtb
        dalog_ref[...] += dalog

        dxs_pre = (dxs_skip + jnp.concatenate(dxp, axis=1)) * _dsilu(xs_pre)
        dbc_pre = jnp.concatenate([dbg[0], dbg[1], dcg[0], dcg[1]], axis=1) * _dsilu(bc_pre)
        dcbx_ref[...] += _rowsum(dxs_pre)
        dcbb_ref[...] += _rowsum(dbc_pre)
        fx = _taps_future(dxs_pre, futx_ref[...])
        fbc = _taps_future(dbc_pre, futb_ref[...])
        xs_in = xs_ref[...].astype(F32)
        bc_in = bc_ref[:, :512]
        for k in range(4):
            dcwx_ref[k:k + 1, :] += _rowsum(fx[k] * xs_in)
            dcwb_ref[k:k + 1, :] += _rowsum(fbc[k] * bc_in)
        cwx = cwx_ref[...]
        cwb = cwb_ref[...]
        dxs_in = fx[0] * cwx[0:1, :]
        dbc_in = fbc[0] * cwb[0:1, :]
        for k in range(1, 4):
            dxs_in = dxs_in + fx[k] * cwx[k:k + 1, :]
            dbc_in = dbc_in + fbc[k] * cwb[k:k + 1, :]
        futx_ref[...] = dxs_pre[0:8, :]
        futb_ref[...] = dbc_pre[0:8, :]
        dxs_ref[...] = dxs_in.astype(dxs_ref.dtype)
        dbc_ref[...] = jnp.concatenate([dbc_in, ddtraw, jnp.zeros((CH, SEG - 640), F32)], axis=1).astype(dbc_ref.dtype)

    row_out = lambda: pl.BlockSpec((CH, D), lambda i: (rev(i), 0))
    outs = pl.pallas_call(
        body, name="ssd_bwd", grid=(n_c,),
        in_specs=[row_out(), row_out(), row_out(), pl.BlockSpec((CH, 512), lambda i: (rev(i), 0))]
        + _proj_specs(CH, (0, 5, 6), rev) + [pl.BlockSpec((1, NS, D), lambda i: (rev(i), 0, 0)),
                                             _full((4, D)), _full((4, 512)),
                                             _full((1, 128)), _full((1, 128)), _full((1, D)), _full((1, D)),
                                             pl.BlockSpec((t_n, fb), lambda i: (0, i)), _full((t_n, D)),
                                             pl.BlockSpec((t_n, fb), lambda i: (0, i)), _full((t_n, D))],
        out_specs=[row_out(), row_out(), row_out(), _full((4, D)), _full((4, 512)), _full((1, D)), _full((1, 512)),
                   _full((1, 128)), _full((1, 128)), _full((1, D)), _full((1, D)),
                   pl.BlockSpec((fb, D), lambda i: (i, 0)), pl.BlockSpec((fb, D), lambda i: (i, 0))],
        out_shape=[jax.ShapeDtypeStruct((t_n, D), MXU)] * 3 + [
            jax.ShapeDtypeStruct(s, F32) for s in ((4, D), (4, 512), (1, D), (1, 512), (1, 128), (1, 128), (1, D), (1, D))]
        + [jax.ShapeDtypeStruct((FF, D), MXU)] * 2,
        scratch_shapes=[pltpu.VMEM((NS, D), F32), pltpu.VMEM((8, D), F32), pltpu.VMEM((8, 512), F32)],
        compiler_params=_cp("arbitrary"),
    )(dya, *saved, *_proj_ops(proj, (0, 5, 6)), sprev, cwx, cwb, dtb, alog, dsk, nrm, *mlp_ops)
    return outs


LRU_ROWS = 256
LRU_BLK = 256


def _lru_gates(xr, wa, wx, ba, bx, lam):
    pr = jnp.concatenate([_nn(xr[:, LRU_BLK * b:LRU_BLK * (b + 1)], wa[b]) for b in range(D // LRU_BLK)], axis=1) + ba
    pi = jnp.concatenate([_nn(xr[:, LRU_BLK * b:LRU_BLK * (b + 1)], wx[b]) for b in range(D // LRU_BLK)], axis=1) + bx
    log_a = -LRU_C * _sigmoid(pr) * _softplus(-lam)
    a = jnp.exp(log_a)
    mult = jnp.sqrt(1.0 - jnp.exp(2.0 * log_a))
    return a, mult * (_sigmoid(pi) * xr)


def _lru_out(h, g):
    return h * jax.nn.gelu(g, approximate=True)


def _lru_fwd(proj, cw, cb, wa, wx, ba, bx, lam):
    t_n = proj[0].shape[0]
    rows = min(LRU_ROWS, t_n)
    fwd = lambda i: i

    def body(g_ref, x_ref, xp_ref, cw_ref, cb_ref, wa_ref, wx_ref, ba_ref, bx_ref, lam_ref, yb_ref, h_ref, xr_ref,
             a_s, u_s, carry):
        i = pl.program_id(0)

        @pl.when(i == 0)
        def _():
            carry[...] = jnp.zeros_like(carry)

        keep = jnp.where(i == 0, 0.0, 1.0)
        xr = _conv_apply(_taps_past(x_ref[...].astype(F32), _prev8(xp_ref) * keep), cw_ref[...], cb_ref[...])
        xr_ref[...] = xr
        a, u = _lru_gates(xr, wa_ref[...], wx_ref[...], ba_ref[...], bx_ref[...], lam_ref[...])
        a_s[...] = a
        u_s[...] = u
        row = lax.broadcasted_iota(jnp.int32, (8, D), 0)

        def blk(b, c):
            s = pl.multiple_of(b * 8, 8)
            av = a_s[pl.ds(s, 8), :]
            uv = u_s[pl.ds(s, 8), :]
            for d in (1, 2, 4):
                m = row >= d
                uv = uv + av * jnp.where(m, pltpu.roll(uv, d, 0), 0.0)
                av = av * jnp.where(m, pltpu.roll(av, d, 0), 1.0)
            hv = uv + av * c
            h_ref[pl.ds(s, 8), :] = hv
            return hv[7:8, :]

        carry[0:1, :] = lax.fori_loop(0, rows // 8, blk, carry[0:1, :])
        yb_ref[...] = _lru_out(h_ref[...], g_ref[...].astype(F32)).astype(yb_ref.dtype)

    return pl.pallas_call(
        body, name="lru_fwd", grid=(t_n // rows,),
        in_specs=_proj_specs(rows, (1, 2), fwd) + _prev_specs(rows, (2,), fwd) + [
            _full((4, D)), _full((1, D)), _full((4, LRU_BLK, LRU_BLK)), _full((4, LRU_BLK, LRU_BLK)),
            _full((1, D)), _full((1, D)), _full((1, D))],
        out_specs=[pl.BlockSpec((rows, D), lambda i: (i, 0))] * 3,
        out_shape=[jax.ShapeDtypeStruct((t_n, D), MXU), jax.ShapeDtypeStruct((t_n, D), F32),
                   jax.ShapeDtypeStruct((t_n, D), F32)],
        scratch_shapes=[pltpu.VMEM((rows, D), F32), pltpu.VMEM((rows, D), F32), pltpu.VMEM((8, D), F32)],
        compiler_params=_cp("arbitrary"),
    )(*_proj_ops(proj, (1, 2, 2)), cw, cb, wa, wx, ba, bx, lam)


def _lru_bwd(dyb, proj, h, xr_saved, cw, wa, wx, ba, bx, lam):
    t_n = proj[0].shape[0]
    rows = min(LRU_ROWS, t_n)
    n_t = t_n // rows
    rev = lambda i: n_t - 1 - i
    rb = rows // 8

    def body(dyb_ref, g_ref, x_ref, h_ref, hp_ref, xr_ref, cw_ref, wa_ref, wx_ref, ba_ref, bx_ref, lam_ref,
             dg_ref, dx_ref, dcw_ref, dcb_ref, dwa_ref, dwx_ref, dba_ref, dbx_ref, dlam_ref,
             a_s, dh_s, hx_s, da_s, du_s, carry, fut):
        i = pl.program_id(0)
        acc_refs = (dcw_ref, dcb_ref, dwa_ref, dwx_ref, dba_ref, dbx_ref, dlam_ref)

        @pl.when(i == 0)
        def _():
            for r in (carry, fut) + acc_refs:
                r[...] = jnp.zeros_like(r)

        keep = jnp.where(i == n_t - 1, 0.0, 1.0)
        gate_in = (xr_ref[...], wa_ref[...], wx_ref[...], ba_ref[...], bx_ref[...], lam_ref[...])
        (a, _), gates_vjp = jax.vjp(_lru_gates, *gate_in)
        _, out_vjp = jax.vjp(_lru_out, h_ref[...], g_ref[...].astype(F32))
        dh, dg = out_vjp(dyb_ref[...])
        dg_ref[...] = dg.astype(dg_ref.dtype)
        a_s[...] = a
        dh_s[...] = dh
        hx_s[0:8, :] = hp_ref[...] * keep
        hx_s[8:, :] = h_ref[...]
        row = lax.broadcasted_iota(jnp.int32, (8, D), 0)

        def blk(b, c):
            s = pl.multiple_of((rb - 1 - b) * 8, 8)
            av = a_s[pl.ds(s, 8), :]
            dhv = dh_s[pl.ds(s, 8), :]
            kv = av * dhv
            for d in (1, 2, 4):
                m = row <= 7 - d
                kv = kv + av * jnp.where(m, pltpu.roll(kv, 8 - d, 0), 0.0)
                av = av * jnp.where(m, pltpu.roll(av, 8 - d, 0), 1.0)
            kv = kv + av * c
            gv = dhv + jnp.where(row < 7, pltpu.roll(kv, 7, 0), c)
            hb = hx_s[pl.ds(s + 8, 8), :]
            hpv = hx_s[pl.ds(s, 8), :]
            hprev = jnp.where(row >= 1, pltpu.roll(hb, 1, 0), hpv[7:8, :])
            du_s[pl.ds(s, 8), :] = gv
            da_s[pl.ds(s, 8), :] = gv * hprev
            return kv[0:1, :]

        carry[0:1, :] = lax.fori_loop(0, rb, blk, carry[0:1, :])
        dxr, dwa, dwx, dba, dbx, dlam = gates_vjp((da_s[...], du_s[...]))
        dwa_ref[...] += dwa
        dwx_ref[...] += dwx
        dba_ref[...] += dba
        dbx_ref[...] += dbx
        dlam_ref[...] += dlam
        dcb_ref[...] += _rowsum(dxr)
        ft = _taps_future(dxr, fut[...])
        x_in = x_ref[...].astype(F32)
        for k in range(4):
            dcw_ref[k:k + 1, :] += _rowsum(ft[k] * x_in)
        cwv = cw_ref[...]
        dx = ft[0] * cwv[0:1, :]
        for k in range(1, 4):
            dx = dx + ft[k] * cwv[k:k + 1, :]
        fut[...] = dxr[0:8, :]
        dx_ref[...] = dx.astype(dx_ref.dtype)

    row_in = lambda: pl.BlockSpec((rows, D), lambda i: (rev(i), 0))
    prev_h = pl.BlockSpec((8, D), lambda i: (jnp.maximum(rev(i) * rb - 1, 0), 0))
    wspec = lambda: _full((4, LRU_BLK, LRU_BLK))
    return pl.pallas_call(
        body, name="lru_bwd", grid=(n_t,),
        in_specs=[row_in()] + _proj_specs(rows, (1, 2), rev) + [row_in(), prev_h, row_in()] + [
            _full((4, D)), wspec(), wspec(), _full((1, D)), _full((1, D)), _full((1, D))],
        out_specs=[row_in(), row_in(), _full((4, D)), _full((1, D)), wspec(), wspec(), _full((1, D)), _full((1, D)),
                   _full((1, D))],
        out_shape=[jax.ShapeDtypeStruct((t_n, D), MXU)] * 2 + [
            jax.ShapeDtypeStruct(s, F32) for s in ((4, D), (1, D), (4, LRU_BLK, LRU_BLK), (4, LRU_BLK, LRU_BLK),
                                                   (1, D), (1, D), (1, D))],
        scratch_shapes=[pltpu.VMEM((rows, D), F32), pltpu.VMEM((rows, D), F32), pltpu.VMEM((rows + 8, D), F32),
                        pltpu.VMEM((rows, D), F32), pltpu.VMEM((rows, D), F32), pltpu.VMEM((8, D), F32),
                        pltpu.VMEM((8, D), F32)],
        compiler_params=_cp("arbitrary"),
    )(dyb, *_proj_ops(proj, (1, 2)), h, h, xr_saved, cw, wa, wx, ba, bx, lam)


def _merge_out(ya, yb, proj, x, wout, g1):
    t_n = x.shape[0]
    tm = min(t_n, 512)

    def body(ya_ref, yb_ref, ga_ref, gb_ref, x_ref, w_ref, g_ref, h1_ref, mix_ref, mg_ref):
        merged = (_sigmoid(ga_ref[...].astype(F32)) * ya_ref[...].astype(F32)
                  + _sigmoid(gb_ref[...].astype(F32)) * yb_ref[...].astype(F32))
        mg = merged.astype(MXU)
        mg_ref[...] = mg
        mix = jnp.dot(mg, w_ref[...].reshape(D, D), preferred_element_type=F32)
        mix_ref[...] = mix
        h1_ref[...] = x_ref[...] + _rms(mix, g_ref[...])

    row = lambda: pl.BlockSpec((tm, D), lambda i: (i, 0))
    return pl.pallas_call(
        body, name="merge_out", grid=(t_n // tm,),
        in_specs=[row(), row()] + _proj_specs(tm, (3, 4), lambda i: i)
        + [row(), _slot_spec(wout, D, D, lambda i: 0), _full((1, D))],
        out_specs=[row(), row(), row()],
        out_shape=[jax.ShapeDtypeStruct((t_n, D), F32), jax.ShapeDtypeStruct((t_n, D), F32),
                   jax.ShapeDtypeStruct((t_n, D), MXU)],
        compiler_params=_cp("parallel"),
    )(ya, yb, *_proj_ops(proj, (3, 4)), x, wout[0], g1)


def _out_bwd(dv, h1, g2, dout, mix, ya, yb, proj, wout, g1, w_in_pt):
    t_n = dv.shape[0]
    tm = min(t_n, 256)

    def body(dv_ref, h1_ref, g2_ref, dout_ref, mix_ref, ya_ref, yb_ref, ga_ref, gb_ref, w_ref, g_ref, wa_ref, wb_ref,
             dh1_ref, dmix_ref, dya_ref, dyb_ref, dga_ref, dgb_ref, dg2_ref, dg1_ref, dug_ref):
        @pl.when(pl.program_id(0) == 0)
        def _():
            dg1_ref[...] = jnp.zeros_like(dg1_ref)
            dg2_ref[...] = jnp.zeros_like(dg2_ref)

        dx, dg_rows = _rms_bwd(h1_ref[...], g2_ref[...], dv_ref[...])
        dg2_ref[...] += _rowsum(dg_rows)
        dh1 = dout_ref[...] + dx
        dh1_ref[...] = dh1
        dmix, dg_rows = _rms_bwd(mix_ref[...], g_ref[...], dh1)
        dg1_ref[...] += _rowsum(dg_rows)
        dmix_b = dmix.astype(MXU)
        dmix_ref[...] = dmix_b
        dmg = lax.dot_general(dmix_b, w_ref[...].reshape(D, D), (((1,), (1,)), ((), ())), preferred_element_type=F32)
        sa = _sigmoid(ga_ref[...].astype(F32))
        sb = _sigmoid(gb_ref[...].astype(F32))
        dya_ref[...] = dmg * sa
        dyb_ref[...] = dmg * sb
        dga = (dmg * ya_ref[...].astype(F32) * sa * (1.0 - sa)).astype(MXU)
        dga_ref[...] = dga
        dug = jnp.dot(dga, wa_ref[...], preferred_element_type=F32)
        dgb = (dmg * yb_ref[...].astype(F32) * sb * (1.0 - sb)).astype(MXU)
        dgb_ref[...] = dgb
        dug_ref[...] = dug + jnp.dot(dgb, wb_ref[...], preferred_element_type=F32)

    row = lambda: pl.BlockSpec((tm, D), lambda i: (i, 0))
    vec = lambda: _full((1, D))
    seg = lambda s: pl.BlockSpec((SEG, D), lambda i: (s, 0))
    return pl.pallas_call(
        body, name="out_bwd", grid=(t_n // tm,),
        in_specs=[row(), row(), vec(), row(), row(), row(), row()] + _proj_specs(tm, (3, 4), lambda i: i)
        + [_slot_spec(wout, D, D, lambda i: 0), vec(), seg(3), seg(4)],
        out_specs=[row(), row(), row(), row(), row(), row(), vec(), vec(), row()],
        out_shape=[jax.ShapeDtypeStruct((t_n, D), F32), jax.ShapeDtypeStruct((t_n, D), MXU),
                   jax.ShapeDtypeStruct((t_n, D), F32), jax.ShapeDtypeStruct((t_n, D), F32),
                   jax.ShapeDtypeStruct((t_n, D), MXU), jax.ShapeDtypeStruct((t_n, D), MXU),
                   jax.ShapeDtypeStruct((1, D), F32), jax.ShapeDtypeStruct((1, D), F32),
                   jax.ShapeDtypeStruct((t_n, D), F32)],
        compiler_params=_cp("arbitrary"),
    )(dv, h1, g2, dout, mix, ya, yb, *_proj_ops(proj, (3, 4)), wout[0], g1, w_in_pt, w_in_pt)


MLP_TM = 1024
MLP_TF_FWD = 1024
MLP_TF_BWD = 1024


def _mlp_fwd(h1, g2, wup, wdown, g3, tgt):
    t_n = h1.shape[0]
    tm = min(t_n, MLP_TM)
    n_f = FF // MLP_TF_FWD

    def body(h1_ref, g2_ref, wu_ref, wd_ref, g3_ref, tgt_ref, hp_ref, v_ref, dout_ref, dff_ref, loss_ref, dg3_ref, acc):
        i, j = pl.program_id(0), pl.program_id(1)

        @pl.when((i == 0) & (j == 0))
        def _():
            loss_ref[...] = jnp.zeros_like(loss_ref)
            dg3_ref[...] = jnp.zeros_like(dg3_ref)

        @pl.when(j == 0)
        def _():
            v_ref[...] = _rms(h1_ref[...], g2_ref[...]).astype(MXU)
            acc[...] = jnp.zeros_like(acc)

        hp = lax.dot_general(v_ref[...], wu_ref[...].reshape(MLP_TF_FWD, D), (((1,), (1,)), ((), ())),
                             preferred_element_type=F32)
        hp_ref[...] = hp.astype(MXU)
        hid = jnp.square(jnp.maximum(hp, 0.0))
        acc[...] += jnp.dot(hid.astype(MXU), wd_ref[...].reshape(MLP_TF_FWD, D), preferred_element_type=F32)

        @pl.when(j == n_f - 1)
        def _():
            ff = acc[...]
            err = h1_ref[...] + _rms(ff, g3_ref[...]) - tgt_ref[...]
            loss_ref[...] += 0.5 * jnp.sum(jnp.mean(err * err, axis=-1, keepdims=True), axis=0, keepdims=True)
            dout = err * (1.0 / D)
            dout_ref[...] = dout
            dff, dg_rows = _rms_bwd(ff, g3_ref[...], dout)
            dg3_ref[...] += _rowsum(dg_rows)
            dff_ref[...] = dff.astype(MXU)

    row = lambda: pl.BlockSpec((tm, D), lambda i, j: (i, 0))
    vec = lambda: pl.BlockSpec((1, D), lambda i, j: (0, 0))
    return pl.pallas_call(
        body, name="mlp_fwd", grid=(t_n // tm, n_f),
        in_specs=[row(), vec(), _slot_spec(wup, FF, MLP_TF_FWD, lambda i, j: j),
                  _slot_spec(wdown, FF, MLP_TF_FWD, lambda i, j: j), vec(), row()],
        out_specs=[pl.BlockSpec((tm, MLP_TF_FWD), lambda i, j: (i, j)), row(), row(), row(),
                   pl.BlockSpec((1, 1), lambda i, j: (0, 0)), vec()],
        out_shape=[jax.ShapeDtypeStruct((t_n, FF), MXU), jax.ShapeDtypeStruct((t_n, D), MXU),
                   jax.ShapeDtypeStruct((t_n, D), F32), jax.ShapeDtypeStruct((t_n, D), MXU),
                   jax.ShapeDtypeStruct((1, 1), F32), jax.ShapeDtypeStruct((1, D), F32)],
        scratch_shapes=[pltpu.VMEM((tm, D), F32)],
        compiler_params=_cp("arbitrary", "arbitrary"),
    )(h1, g2, wup[0], wdown[0], g3, tgt)


def _mlp_bwd(dff, hp, wup, wdown):
    t_n = dff.shape[0]
    tm = min(t_n, MLP_TM)
    n_f = FF // MLP_TF_BWD

    def mm_body(dff_ref, hp_ref, wu_ref, wd_ref, dv_ref, dhp_ref, hid_ref):
        @pl.when(pl.program_id(1) == 0)
        def _():
            dv_ref[...] = jnp.zeros_like(dv_ref)

        relu = jnp.maximum(hp_ref[...].astype(F32), 0.0)
        hid_ref[...] = jnp.square(relu).astype(MXU)
        dhid = lax.dot_general(dff_ref[...], wd_ref[...].reshape(MLP_TF_BWD, D), (((1,), (1,)), ((), ())),
                               preferred_element_type=F32)
        dhp = (dhid * (2.0 * relu)).astype(MXU)
        dhp_ref[...] = dhp
        dv_ref[...] += jnp.dot(dhp, wu_ref[...].reshape(MLP_TF_BWD, D), preferred_element_type=F32)

    row = lambda: pl.BlockSpec((tm, D), lambda i, j: (i, 0))
    blk = lambda: pl.BlockSpec((tm, MLP_TF_BWD), lambda i, j: (i, j))
    wblk = lambda w: _slot_spec(w, FF, MLP_TF_BWD, lambda i, j: j)
    return pl.pallas_call(
        mm_body, name="mlp_bwd", grid=(t_n // tm, n_f),
        in_specs=[row(), blk(), wblk(wup), wblk(wdown)], out_specs=[row(), blk(), blk()],
        out_shape=[jax.ShapeDtypeStruct((t_n, D), F32), jax.ShapeDtypeStruct((t_n, FF), MXU),
                   jax.ShapeDtypeStruct((t_n, FF), MXU)],
        compiler_params=_cp("parallel", "arbitrary"),
    )(dff, hp, wup[0], wdown[0])


def _wgrad(a, g, name):
    t_n, k_n = a.shape
    n_n = g.shape[1]
    tt = min(t_n, 1024)
    tk, tn = min(k_n, 1024), min(n_n, 1024)

    n_t = t_n // tt

    def body(a_ref, g_ref, o_ref, acc):
        t = pl.program_id(2)

        @pl.when(t == 0)
        def _():
            acc[...] = jnp.zeros_like(acc)

        acc[...] += lax.dot_general(a_ref[...], g_ref[...], (((0,), (0,)), ((), ())), preferred_element_type=F32)

        @pl.when(t == n_t - 1)
        def _():
            o_ref[...] = acc[...].astype(o_ref.dtype)

    return pl.pallas_call(
        body, name=name, grid=(k_n // tk, n_n // tn, n_t),
        in_specs=[pl.BlockSpec((tt, tk), lambda k, n, t: (t, k)), pl.BlockSpec((tt, tn), lambda k, n, t: (t, n))],
        out_specs=pl.BlockSpec((tk, tn), lambda k, n, t: (k, n)),
        out_shape=jax.ShapeDtypeStruct((k_n, n_n), MXU),
        scratch_shapes=[pltpu.VMEM((tk, tn), F32)],
        compiler_params=_cp("parallel", "parallel", "arbitrary"),
    )(a, g)


def _wgrad_segs(segs, g, name):
    t_n, n_n = g.shape
    n_s = len(segs)
    tt = min(t_n, 1024)
    n_t = t_n // tt

    def body(*refs):
        a_refs = refs[:n_s]
        g_ref, o_ref, acc = refs[n_s:]
        s_id, t = pl.program_id(0), pl.program_id(1)

        @pl.when(t == 0)
        def _():
            acc[...] = jnp.zeros_like(acc)

        for s in range(n_s):
            @pl.when(s_id == s)
            def _(s=s):
                acc[...] += lax.dot_general(a_refs[s][...], g_ref[...], (((0,), (0,)), ((), ())),
                                            preferred_element_type=F32)

        @pl.when(t == n_t - 1)
        def _():
            o_ref[...] = acc[...].astype(o_ref.dtype)

    seg_spec = lambda s: pl.BlockSpec((tt, SEG), lambda i, t: (jnp.where(i == s, t, jnp.where(i < s, 0, n_t - 1)), 0))
    return pl.pallas_call(
        body, name=name, grid=(n_s, n_t),
        in_specs=[seg_spec(s) for s in range(n_s)] + [pl.BlockSpec((tt, n_n), lambda i, t: (t, 0))],
        out_specs=pl.BlockSpec((SEG, n_n), lambda i, t: (i, 0)),
        out_shape=jax.ShapeDtypeStruct((n_s * SEG, n_n), MXU),
        scratch_shapes=[pltpu.VMEM((SEG, n_n), F32)],
        compiler_params=_cp("arbitrary", "arbitrary"),
    )(*segs, g)


def _inproj_bwd(dsegs, seg_ids, du_part, w, x, g0, dh1):
    t_n = x.shape[0]
    tm = min(t_n, 512)
    n_k = len(dsegs)

    def body(*refs):
        dp_refs = refs[:n_k]
        w_refs = refs[n_k:2 * n_k]
        part_ref, x_ref, g_ref, dh1_ref, dx_ref, dg0_ref = refs[2 * n_k:]

        @pl.when(pl.program_id(0) == 0)
        def _():
            dg0_ref[...] = jnp.zeros_like(dg0_ref)

        du = part_ref[...]
        for s in range(n_k):
            du = du + jnp.dot(dp_refs[s][...], w_refs[s][...], preferred_element_type=F32)
        dx, dg_rows = _rms_bwd(x_ref[...], g_ref[...], du)
        dg0_ref[...] += _rowsum(dg_rows)
        dx_ref[...] = dh1_ref[...] + dx

    row = lambda: pl.BlockSpec((tm, D), lambda i: (i, 0))
    return pl.pallas_call(
        body, name="inproj_bwd", grid=(t_n // tm,),
        in_specs=[row() for _ in range(n_k)]
        + [pl.BlockSpec((SEG, D), functools.partial(lambda i, s: (s, 0), s=s)) for s in seg_ids]
        + [row(), row(), _full((1, D)), row()],
        out_specs=[row(), _full((1, D))],
        out_shape=[jax.ShapeDtypeStruct((t_n, D), F32), jax.ShapeDtypeStruct((1, D), F32)],
        compiler_params=_cp("arbitrary"),
    )(*dsegs, *([w] * n_k), du_part, x, g0, dh1)


def _blockdiag4(w):
    w4 = w.reshape(4, 4, 64, 1, 64).astype(MXU)
    same = (jnp.arange(4)[:, None, None, None] == jnp.arange(4)[None, None, :, None])
    return jnp.where(same[None], w4, jnp.zeros((), MXU)).reshape(4, 256, 256)


def _blockdiag4_extract(g):
    g5 = g.reshape(4, 4, 64, 4, 64)
    return jnp.stack([g5[:, q, :, q, :] for q in range(4)], axis=1).reshape(NH, 64, 64)


def _local_step(x, tgt, p, after_ssd=None, late_weights=None, send_mlp_grads=None, send_late_grads=None):
    f = lambda a: a.astype(F32)
    proj, u = _inproj(x, p["norm_mix_pre"], p["w_in_pT"])
    ssm_params = (p["cw_xs"], p["cw_bc"], p["cb_xs"], p["cb_bc"], p["dt_bias"], p["a_log"], p["d_skip_x"],
                  p["ssm_norm"])
    ya, sprev, *ssd_saved = _ssd_fwd(proj, *ssm_params)
    cb_lru = p["conv_lru_b"] if after_ssd is None else p["conv_lru_b"] + after_ssd(ya)
    lru_params = (p["conv_lru_w"], cb_lru, p["wa_bd"], p["wx_bd"], p["lru_ba"], p["lru_bx"], p["lru_lambda"])
    yb, h, xr = _lru_fwd(proj, *lru_params)
    if late_weights is not None:
        p = dict(p, **late_weights(yb))
    p = dict(p, **{n: _slotted(p[n]) for n in ("w_out", "w_upT", "w_down")})
    h1, mix, merged = _merge_out(ya, yb, proj, x, p["w_out"], p["norm_mix_post"])
    hp, v, dout, dff, loss, dg3 = _mlp_fwd(h1, p["norm_mlp_pre"], p["w_upT"], p["w_down"], p["norm_mlp_post"], tgt)

    dv, dhp, hid = _mlp_bwd(dff, hp, p["w_upT"], p["w_down"])
    dh1, dmix, dya, dyb, dga, dgb, dg2, dg1, du_gates = _out_bwd(
        dv, h1, p["norm_mlp_pre"], dout, mix, ya, yb, proj, p["w_out"], p["norm_mix_post"], p["w_in_pT"])
    d_w_out = _wgrad(merged, dmix, "wgrad_out")
    (dz, dxs, dbc, dcwx, dcwb, dcbx, dcbb, ddtb, dalog, ddsk, dnrm, d_w_down, d_w_up_t) = _ssd_bwd(
        dya, ssd_saved, proj, sprev, p["cw_xs"], p["cw_bc"], *ssm_params[4:], (hid, dff, dhp, v))
    ba_lru = p["lru_ba"] if send_mlp_grads is None else p["lru_ba"] + send_mlp_grads(d_w_up_t, d_w_down, d_w_out)
    (dgl, dxl, dcwl, dcbl, dwa, dwx, dba, dbx, dlam) = _lru_bwd(
        dyb, proj, h, xr, p["conv_lru_w"], p["wa_bd"], p["wx_bd"], ba_lru, p["lru_bx"], p["lru_lambda"])
    dsegs = [dz, dgl, dxl, dga, dgb, dxs, dbc]
    d_w_in_pt = _wgrad_segs(dsegs, u, "wgrad_in")
    g0 = p["norm_mix_pre"]
    if send_late_grads is not None:
        g0 = g0 + send_late_grads(d_w_in_pt)
    grad_x, dg0 = _inproj_bwd([dz, dgl, dxl, dxs, dbc], (0, 1, 2, 5, 6), du_gates, p["w_in_pT"], x, g0, dh1)
    grads = dict(
        norm_mix_pre=dg0, w_in_pT=d_w_in_pt, conv_ssm_w=jnp.concatenate([dcwx, dcwb], axis=1),
        conv_ssm_b=jnp.concatenate([dcbx, dcbb], axis=1), dt_bias=ddtb[:, :NH], a_log=dalog[:, :NH],
        d_skip=f(ddsk).reshape(NH, 64).sum(axis=1)[None, :], ssm_norm=dnrm, conv_lru_w=dcwl, conv_lru_b=dcbl,
        lru_wa=_blockdiag4_extract(dwa), lru_ba=dba, lru_wx=_blockdiag4_extract(dwx), lru_bx=dbx, lru_lambda=dlam,
        w_out=d_w_out, norm_mix_post=dg1, norm_mlp_pre=dg2, w_upT=d_w_up_t, w_down=d_w_down, norm_mlp_post=dg3)
    return loss[0, 0], grad_x, grads


W_IN_COLS = 6672


def _w_in_t_to_padded(wt):
    z, xs, bc, dt = wt[0:1024], wt[1024:2048], wt[2048:2560], wt[2560:2576]
    gl, xl, ga, gb = wt[2576:3600], wt[3600:4624], wt[4624:5648], wt[5648:6672]
    return jnp.concatenate([z, gl, xl, ga, gb, xs, bc, dt, jnp.zeros((NP - 6672, wt.shape[1]), wt.dtype)], axis=0)


def _w_in_t_from_padded(wp):
    z, gl, xl, ga, gb = (wp[SEG * s:SEG * (s + 1)] for s in range(5))
    xs, bc, dt = wp[5120:6144], wp[6144:6656], wp[6656:6672]
    return jnp.concatenate([z, xs, bc, dt, gl, xl, ga, gb], axis=0)


def _prep_params(full, big):
    f = lambda a: a.astype(F32)
    pad128 = lambda a: jnp.pad(f(a).reshape(1, -1), ((0, 0), (0, 128 - a.size)))
    cw = f(full["conv_ssm_w"])
    cb = f(full["conv_ssm_b"]).reshape(1, -1)
    return dict(
        big, norm_mix_pre=f(full["norm_mix_pre"]).reshape(1, D),
        cw_xs=cw[:, :D], cw_bc=cw[:, D:], cb_xs=cb[:, :D], cb_bc=cb[:, D:],
        dt_bias=pad128(full["dt_bias"]), a_log=pad128(full["a_log"]),
        d_skip_x=jnp.repeat(f(full["d_skip"]).reshape(-1), 64).reshape(1, D), ssm_norm=f(full["ssm_norm"]).reshape(1, D),
        conv_lru_w=f(full["conv_lru_w"]), conv_lru_b=f(full["conv_lru_b"]).reshape(1, D),
        wa_bd=_blockdiag4(full["lru_wa"]), wx_bd=_blockdiag4(full["lru_wx"]),
        lru_ba=f(full["lru_ba"]).reshape(1, D), lru_bx=f(full["lru_bx"]).reshape(1, D),
        lru_lambda=f(full["lru_lambda"]).reshape(1, D),
        norm_mix_post=f(full["norm_mix_post"]).reshape(1, D), norm_mlp_pre=f(full["norm_mlp_pre"]).reshape(1, D),
        norm_mlp_post=f(full["norm_mlp_post"]).reshape(1, D))


MESH_ID = pl.DeviceIdType.MESH
ANY = pl.BlockSpec(memory_space=pl.ANY)


def _my_place():
    x, y, c = lax.axis_index("x"), lax.axis_index("y"), lax.axis_index("c")
    return x, y, c, 4 * x + 2 * y + c


def _peer(x, y, c, k):
    return (x ^ ((k >> 2) & 1), y ^ ((k >> 1) & 1), c ^ (k & 1))


def _all_gather(pack, name):
    r_n = pack.shape[0]
    half = -(-r_n // 32) * 16
    n_cp = 9

    def body(in_ref, out_ref, send_sems, recv_sems, local_sem):
        x, y, c, me = _my_place()
        here, sibling, x_nbr, y_nbr = (x, y, c), (x, y, 1 - c), (1 - x, y, c), (x, 1 - y, c)
        part = {"all": pl.ds(0, r_n), "lo": pl.ds(0, half), "hi": pl.ds(half, r_n - half)}

        def copy(j, block, rows, to, src=None):
            return pltpu.make_async_remote_copy(
                src_ref=out_ref.at[block, part[rows]] if src is None else src, dst_ref=out_ref.at[block, part[rows]],
                send_sem=send_sems.at[j], recv_sem=recv_sems.at[j], device_id=to, device_id_type=MESH_ID)

        mine = pltpu.make_async_copy(in_ref, out_ref.at[me], local_sem)
        mine.start()
        first = [copy(0, me, "all", sibling, src=in_ref), copy(1, me, "all", x_nbr, src=in_ref),
                 copy(2, me, "all", y_nbr, src=in_ref)]
        for cp in first:
            cp.start()
        relay = [[(3, me ^ 4, "lo", y_nbr), (5, me ^ 4, "all", sibling)],
                 [(4, me ^ 2, "hi", x_nbr), (6, me ^ 2, "all", sibling)],
                 [(7, me ^ 6, "lo", sibling)],
                 [(8, me ^ 6, "hi", sibling)]]
        landed = [(1, me ^ 4, "all"), (2, me ^ 2, "all"), (3, me ^ 6, "lo"), (4, me ^ 6, "hi")]
        passed = []
        for (j, block, rows), nxt in zip(landed, relay):
            copy(j, block, rows, here).wait_recv()
            for args in nxt:
                cp = copy(*args)
                cp.start()
                passed.append(cp)
        for j, block, rows in ((0, me ^ 1, "all"), (5, me ^ 5, "all"), (6, me ^ 3, "all"), (7, me ^ 7, "lo"),
                               (8, me ^ 7, "hi")):
            copy(j, block, rows, here).wait_recv()
        for cp in first + passed:
            cp.wait_send()
        mine.wait()

    return pl.pallas_call(
        body, name=name, in_specs=[ANY], out_specs=ANY,
        out_shape=jax.ShapeDtypeStruct((N_DEV,) + pack.shape, pack.dtype),
        scratch_shapes=[pltpu.SemaphoreType.DMA((n_cp,)), pltpu.SemaphoreType.DMA((n_cp,)), pltpu.SemaphoreType.DMA],
    )(pack)


HBM = pl.BlockSpec(memory_space=pltpu.HBM)
SEM = pl.BlockSpec(memory_space=pltpu.SEMAPHORE)
PLAN_GATHER = tuple((k, "pack", 0) for k in range(1, N_DEV))
PLAN_SCATTER = tuple((k, "slot", 0) for k in range(1, N_DEV))
PLAN_GATHER_ICI = tuple((k, "pack", 0) for k in (2, 4, 6))
PLAN_GATHER_D2D = ((1, "pack", 0),) + tuple((1, s, s) for s in (2, 4, 6))
PLAN_SCATTER_CHIPS = tuple((k, "chip", "chip") for k in (2, 4, 6))


def _plan_copy(j, plan, src_ref, land_ref, sems):
    k, source, slot = plan[j]
    x, y, c, me = _my_place()
    if source == "pack":
        src = src_ref
    elif source == "slot":
        src = src_ref.at[me ^ k]
    elif source == "chip":
        src = src_ref.at[(me ^ k) >> 1]
    else:
        src = land_ref.at[me ^ source]
    dst = land_ref.at[me >> 1] if slot == "chip" else land_ref.at[me ^ slot]
    return pltpu.make_async_remote_copy(
        src_ref=src, dst_ref=dst, send_sem=sems[j], recv_sem=sems[len(plan) + j],
        device_id=_peer(x, y, c, k), device_id_type=MESH_ID)


def _exchange_start(src, land, plan, name):
    n_c = len(plan)
    if land is None:
        slots = src.shape[0] if src.ndim == 3 else N_DEV
        land = pltpu.with_memory_space_constraint(lax.empty((slots,) + src.shape[-2:], src.dtype), pltpu.HBM)

    def body(src_ref, land_ref, *rest):
        sems, token = rest[:2 * n_c], rest[2 * n_c + 2]
        for j in range(n_c):
            _plan_copy(j, plan, src_ref, land_ref, sems).start()
        token[...] = jnp.zeros_like(token)

    outs = pl.pallas_call(
        body, name=name,
        out_shape=(pltpu.SemaphoreType.DMA(()),) * (2 * n_c) + (
            pltpu.HBM(src.shape, src.dtype), pltpu.HBM(land.shape, land.dtype), jax.ShapeDtypeStruct((8, 128), F32)),
        in_specs=(HBM, HBM), out_specs=(SEM,) * (2 * n_c) + (HBM, HBM, pl.BlockSpec(memory_space=pltpu.VMEM)),
        input_output_aliases={0: 2 * n_c, 1: 2 * n_c + 1},
        compiler_params=pltpu.CompilerParams(has_side_effects=pltpu.SideEffectType.DATAFLOW_SIDE_EFFECTING),
    )(pltpu.with_memory_space_constraint(src, pltpu.HBM), land)
    return outs[:2 * n_c], outs[2 * n_c], outs[2 * n_c + 1], outs[2 * n_c + 2]


def _exchange_wait(sems, src_thru, land_thru, after, plan, name):
    n_c = len(plan)

    def body(src_ref, land_ref, *rest):
        for j in range(n_c):
            cp = _plan_copy(j, plan, src_ref, land_ref, rest[:2 * n_c])
            cp.wait_send()
            cp.wait_recv()

    return pl.pallas_call(
        body, name=name,
        out_shape=(pltpu.HBM(src_thru.shape, src_thru.dtype), pltpu.HBM(land_thru.shape, land_thru.dtype)),
        in_specs=(HBM, HBM) + (SEM,) * (2 * n_c) + (pl.BlockSpec(memory_space=pl.ANY),), out_specs=(HBM, HBM),
        input_output_aliases={0: 0, 1: 1},
        compiler_params=pltpu.CompilerParams(has_side_effects=pltpu.SideEffectType.DATAFLOW_SIDE_EFFECTING),
    )(src_thru, land_thru, *sems, after)


def _pair_exchange(src, name):
    n_s = N_DEV // 2

    def body(src_ref, land_ref, send_sems, recv_sems):
        x, y, c, _ = _my_place()
        cps = [pltpu.make_async_remote_copy(
            src_ref=src_ref.at[2 * i + 1 - c], dst_ref=land_ref.at[i], send_sem=send_sems.at[i],
            recv_sem=recv_sems.at[i], device_id=(x, y, 1 - c), device_id_type=MESH_ID) for i in range(n_s)]
        for cp in cps:
            cp.start()
        for cp in cps:
            cp.wait_recv()
        for cp in cps:
            cp.wait_send()

    return pl.pallas_call(
        body, name=name, in_specs=[ANY], out_specs=ANY,
        out_shape=jax.ShapeDtypeStruct((n_s,) + src.shape[1:], src.dtype),
        scratch_shapes=[pltpu.SemaphoreType.DMA((n_s,)), pltpu.SemaphoreType.DMA((n_s,))],
    )(src)


def _pair_sum(src, land, name):
    n_s, r_n, c_n = land.shape
    tr = max(t for t in range(16, 513, 16) if r_n % t == 0)

    def body(c_ref, a_ref, b_ref, o_ref):
        o_ref[...] = (a_ref[...].astype(F32) + b_ref[...].astype(F32)).astype(o_ref.dtype)

    blk = lambda: pl.BlockSpec((1, tr, c_n), lambda i, j, c_ref: (i, j, 0))
    return pl.pallas_call(
        body, name=name,
        grid_spec=pltpu.PrefetchScalarGridSpec(
            num_scalar_prefetch=1, grid=(n_s, r_n // tr),
            in_specs=[pl.BlockSpec((1, tr, c_n), lambda i, j, c_ref: (2 * i + c_ref[0], j, 0)), blk()],
            out_specs=blk()),
        out_shape=jax.ShapeDtypeStruct(land.shape, land.dtype),
        compiler_params=_cp("parallel", "parallel"),
    )(lax.axis_index("c").astype(jnp.int32).reshape(1), src, land)


def _slot_sum(parts, name):
    n_s, r_n, c_n = parts.shape
    tr = max(t for t in range(16, 513, 16) if r_n % t == 0)

    def body(p_ref, o_ref):
        acc = p_ref[0].astype(F32)
        for k in range(1, n_s):
            acc = acc + p_ref[k].astype(F32)
        o_ref[...] = acc

    return pl.pallas_call(
        body, name=name, grid=(r_n // tr,),
        in_specs=[pl.BlockSpec((n_s, tr, c_n), lambda i: (0, i, 0))],
        out_specs=pl.BlockSpec((tr, c_n), lambda i: (i, 0)),
        out_shape=jax.ShapeDtypeStruct((r_n, c_n), F32),
        compiler_params=_cp("parallel"),
    )(parts)


def _adam_math(w, g, m, v):
    m = ADAM_B1 * m + (1.0 - ADAM_B1) * g
    v = ADAM_B2 * v + (1.0 - ADAM_B2) * jnp.square(g)
    m_hat = m / (1.0 - ADAM_B1 ** ADAM_STEP)
    v_hat = v / (1.0 - ADAM_B2 ** ADAM_STEP)
    return -ADAM_LR * (m_hat / (jnp.sqrt(v_hat) + ADAM_EPS) + ADAM_WD * w), m, v


def _adam_big(w, g, m, v, name):
    def body(w_ref, g_ref, m_ref, v_ref, d_ref, mo_ref, vo_ref):
        d_ref[...], mo_ref[...], vo_ref[...] = _adam_math(w_ref[...], g_ref[...], m_ref[...], v_ref[...])

    if w.ndim == 3:
        _, r_n, c_n = w.shape
        tr = min(r_n, 256)
        grid = (r_n // tr,)
        blk = lambda: pl.BlockSpec((1, tr, c_n), lambda i: (0, i, 0))
    else:
        r_n, c_n = w.shape
        tc = min(c_n, 256)
        grid = (c_n // tc,)
        blk = lambda: pl.BlockSpec((r_n, tc), lambda i: (0, i))
    return pl.pallas_call(
        body, name=name, grid=grid, in_specs=[blk(), blk(), blk(), blk()], out_specs=[blk(), blk(), blk()],
        out_shape=[jax.ShapeDtypeStruct(w.shape, F32)] * 3, compiler_params=_cp("parallel"),
    )(w, g, m, v)


def _adam_rows_linear(w, g, m, v, name):
    r_n, n_b, _ = w.shape
    step = max(t for t in range(1, 257) if r_n % t == 0)

    def body(w_ref, g_ref, m_ref, v_ref, go_ref, d_ref, mo_ref, vo_ref):
        for k in range(n_b):
            go_ref[:, k, :] = g_ref[:, 128 * k:128 * (k + 1)]
        for r0 in range(0, r_n, step):
            rows = pl.ds(r0, step)
            d_ref[rows], mo_ref[rows], vo_ref[rows] = _adam_math(w_ref[rows], go_ref[rows], m_ref[rows], v_ref[rows])

    vm = pl.BlockSpec(memory_space=pltpu.VMEM)
    return pl.pallas_call(
        body, name=name, in_specs=[vm] * 4, out_specs=[vm] * 4, out_shape=[jax.ShapeDtypeStruct(w.shape, F32)] * 4,
        compiler_params=pltpu.CompilerParams(vmem_limit_bytes=VMEM_LIMIT),
    )(w, g, m, v)


def _adam_small(groups, where, wmv, total):
    n, n_g = len(wmv), len(groups)
    t_g, t_r = total

    def body(*refs):
        g_refs = refs[:n_g]
        w_refs = refs[n_g:n_g + 3 * n]
        o_refs = refs[n_g + 3 * n:]
        for q in range(n):
            w_ref, m_ref, v_ref = w_refs[3 * q:3 * q + 3]
            r, c = w_ref.shape
            gi, r0 = where[q]
            g = g_refs[gi][0, r0:r0 + r, 0:c]
            for k in range(1, N_DEV):
                g = g + g_refs[gi][k, r0:r0 + r, 0:c]
            d, m, v = _adam_math(w_ref[...], g, m_ref[...], v_ref[...])
            o_refs[4 * q][...] = g
            o_refs[4 * q + 1][...] = d
            o_refs[4 * q + 2][...] = m
            o_refs[4 * q + 3][...] = v
        tot = g_refs[t_g][0, t_r:t_r + 1, :]
        for k in range(1, N_DEV):
            tot = tot + g_refs[t_g][k, t_r:t_r + 1, :]
        o_refs[4 * n][...] = tot

    flat_wmv = [a for t in wmv for a in t]
    vm = pl.BlockSpec(memory_space=pltpu.VMEM)
    outs = pl.pallas_call(
        body, name="adam_small", in_specs=[vm] * (n_g + 3 * n), out_specs=[vm] * (4 * n + 1),
        out_shape=[jax.ShapeDtypeStruct(t[0].shape, F32) for t in wmv for _ in range(4)]
        + [jax.ShapeDtypeStruct((1, groups[t_g].shape[2]), F32)],
        compiler_params=pltpu.CompilerParams(vmem_limit_bytes=VMEM_LIMIT),
    )(*groups, *flat_wmv)
    return [tuple(outs[4 * q:4 * q + 4]) for q in range(n)], outs[4 * n]


WEIGHTS = ["norm_mix_pre", "w_in", "conv_ssm_w", "conv_ssm_b", "dt_bias", "a_log", "d_skip", "ssm_norm", "conv_lru_w",
           "conv_lru_b", "lru_wa", "lru_ba", "lru_wx", "lru_bx", "lru_lambda", "w_out", "norm_mix_post", "norm_mlp_pre",
           "w_up", "w_down", "norm_mlp_post"]
BIG = ["w_out", "w_up", "w_down", "w_in"]
IN_ROWS = W_IN_COLS // N_DEV
IN_PAD, EARLY_ROWS = 848, 880
LATE_ROWS = 1152
GRAD_LATE_ROWS = 864
CONV_SSM_COLS, CONV_LRU_COLS = 1536 // N_DEV, D // N_DEV
SMALL = [("norm_mix_pre", (1, D), 0, 0), ("ssm_norm", (1, D), 0, 1), ("conv_lru_b", (1, D), 0, 2),
         ("lru_lambda", (1, D), 0, 3), ("norm_mix_post", (1, D), 0, 4), ("norm_mlp_pre", (1, D), 0, 5),
         ("norm_mlp_post", (1, D), 0, 6), ("conv_ssm_b", (1, 1536), 1, 0), ("dt_bias", (1, NH), 2, 0),
         ("a_log", (1, NH), 2, 1), ("d_skip", (1, NH), 2, 2), ("conv_ssm_w", (4, CONV_SSM_COLS), 3, 0),
         ("conv_lru_w", (4, CONV_LRU_COLS), 4, 0), ("lru_wa", (D, 64), 5, 0), ("lru_wx", (D, 64), 5, D),
         ("lru_ba", (NH, 64), 6, 0), ("lru_bx", (NH, 64), 6, NH)]
SMALL_GROUPS = [(8, D), (1, 1536), (8, 128), (4, 1536), (4, D), (2 * D, 64), (2 * NH, 64)]


def _pad_rows(flat, mult):
    n = flat.shape[0]
    rows = -(-n // (128 * mult)) * mult
    return jnp.pad(flat, (0, rows * 128 - n)).reshape(rows, 128)


def _split3(a):
    hi = a.astype(MXU)
    r1 = a - hi.astype(F32)
    mid = r1.astype(MXU)
    lo = (r1 - mid.astype(F32)).astype(MXU)
    return jnp.stack([hi, mid, lo])


def _early_pack(a, me):
    bf = lambda t: t.astype(MXU)
    conv = lambda t, c: jnp.pad(_split3(t).reshape(12, c), ((0, 4), (0, D - c)))
    shifted = lax.dynamic_update_slice(jnp.zeros((IN_PAD, D), MXU), bf(a["w_in"][0]).T, (2 * me, 0))
    return jnp.concatenate([shifted, conv(a["conv_ssm_w"][0], CONV_SSM_COLS), conv(a["conv_lru_w"][0], CONV_LRU_COLS)],
                           axis=0)


TILE = 16
SHARD_TILES = IN_PAD // TILE
SHARD_STEP = (IN_ROWS // TILE)
SEG_TILES = ((0, 64, 0), (64, 128, 320), (128, 160, 384), (160, 161, 416), (161, 225, 64), (225, 289, 128),
             (289, 353, 192), (353, 417, 256))


def _tile_runs(lo, hi):
    runs = []
    for s0, s1, d0 in SEG_TILES:
        a, b = max(lo, s0), min(hi, s1)
        if a < b:
            runs.append((a, b - a, d0 + a - s0))
    return runs


def _assemble_w_in(g):
    whole = []
    for k in range(N_DEV):
        lo = SHARD_STEP * k + (1 if k else 0)
        hi = SHARD_STEP * (k + 1) + (1 if k == N_DEV - 1 else 0)
        whole += [(k, a - SHARD_STEP * k, n, d) for a, n, d in _tile_runs(lo, hi)]
    split = [(k, _tile_runs(SHARD_STEP * k, SHARD_STEP * k + 1)[0][2]) for k in range(1, N_DEV)]

    def body(g_ref, o_ref):
        rows = lambda t, n=1: pl.ds(TILE * t, TILE * n)
        for k, t, n, d in whole:
            o_ref[rows(d, n), :] = g_ref[k, rows(t, n), :]
        for k, d in split:
            o_ref[rows(d), :] = g_ref[k - 1, rows(SHARD_STEP), :] + g_ref[k, rows(0), :]
        o_ref[pl.ds(W_IN_COLS, NP - W_IN_COLS), :] = jnp.zeros((NP - W_IN_COLS, D), o_ref.dtype)

    vm = pl.BlockSpec(memory_space=pltpu.VMEM)
    return pl.pallas_call(
        body, name="assemble_w_in", in_specs=[vm], out_specs=vm, out_shape=jax.ShapeDtypeStruct((NP, D), g.dtype),
        compiler_params=pltpu.CompilerParams(vmem_limit_bytes=VMEM_LIMIT),
    )(g)


def _scatter_w_in_grad(dw):
    runs = [(k, a - SHARD_STEP * k, n, d) for k in range(N_DEV)
            for a, n, d in _tile_runs(SHARD_STEP * k, SHARD_STEP * k + SHARD_TILES)]
    pad = GRAD_LATE_ROWS - IN_PAD

    def body(dw_ref, o_ref):
        rows = lambda t, n: pl.ds(TILE * t, TILE * n)
        for k, t, n, d in runs:
            o_ref[k, rows(t, n), :] = dw_ref[rows(d, n), :]
        for k in range(N_DEV):
            o_ref[k, pl.ds(IN_PAD, pad), :] = jnp.zeros((pad, D), o_ref.dtype)

    vm = pl.BlockSpec(memory_space=pltpu.VMEM)
    return pl.pallas_call(
        body, name="scatter_w_in_grad", in_specs=[vm], out_specs=vm,
        out_shape=jax.ShapeDtypeStruct((N_DEV, GRAD_LATE_ROWS, D), dw.dtype),
        compiler_params=pltpu.CompilerParams(vmem_limit_bytes=VMEM_LIMIT),
    )(dw)


def _early_unpack(g):
    w_in_pt = _assemble_w_in(g)
    conv = {}
    for n, r0, c in (("conv_ssm_w", IN_PAD, CONV_SSM_COLS), ("conv_lru_w", IN_PAD + 16, CONV_LRU_COLS)):
        s = g[:, r0:r0 + 12, :c].astype(F32).reshape(N_DEV, 3, 4, c)
        conv[n] = ((s[:, 0] + s[:, 1]) + s[:, 2]).transpose(1, 0, 2).reshape(4, N_DEV * c)
    return w_in_pt, conv


def _late_pack(a):
    bf = lambda t: t.astype(MXU)
    return jnp.concatenate([bf(a["w_up"][0]).T, bf(a["w_down"][0]), bf(a["w_out"][0])], axis=0)


def _late_unpack(g):
    return dict(w_upT=(g, 0), w_down=(g, 1), w_out=(g, 2 * FF // D))


def _own_slot(land, own):
    me = 4 * lax.axis_index("x") + 2 * lax.axis_index("y") + lax.axis_index("c")
    return lax.dynamic_update_slice_in_dim(land, own[None], me, axis=0)


def kernel(x, norm_mix_pre, w_in, conv_ssm_w, conv_ssm_b, dt_bias, a_log, d_skip, ssm_norm, conv_lru_w, conv_lru_b, lru_wa, lru_ba, lru_wx, lru_bx, lru_lambda, w_out, norm_mix_post, norm_mlp_pre, w_up, w_down, norm_mlp_post, loss_target, m_norm_mix_pre, m_w_in, m_conv_ssm_w, m_conv_ssm_b, m_dt_bias, m_a_log, m_d_skip, m_ssm_norm, m_conv_lru_w, m_conv_lru_b, m_lru_wa, m_lru_ba, m_lru_wx, m_lru_bx, m_lru_lambda, m_w_out, m_norm_mix_post, m_norm_mlp_pre, m_w_up, m_w_down, m_norm_mlp_post, v_norm_mix_pre, v_w_in, v_conv_ssm_w, v_conv_ssm_b, v_dt_bias, v_a_log, v_d_skip, v_ssm_norm, v_conv_lru_w, v_conv_lru_b, v_lru_wa, v_lru_ba, v_lru_wx, v_lru_bx, v_lru_lambda, v_w_out, v_norm_mix_post, v_norm_mlp_pre, v_w_up, v_w_down, v_norm_mlp_post):
    vals = (norm_mix_pre, w_in, conv_ssm_w, conv_ssm_b, dt_bias, a_log, d_skip, ssm_norm, conv_lru_w, conv_lru_b, lru_wa, lru_ba, lru_wx, lru_bx, lru_lambda, w_out, norm_mix_post, norm_mlp_pre, w_up, w_down, norm_mlp_post)
    m_vals = (m_norm_mix_pre, m_w_in, m_conv_ssm_w, m_conv_ssm_b, m_dt_bias, m_a_log, m_d_skip, m_ssm_norm, m_conv_lru_w, m_conv_lru_b, m_lru_wa, m_lru_ba, m_lru_wx, m_lru_bx, m_lru_lambda, m_w_out, m_norm_mix_post, m_norm_mlp_pre, m_w_up, m_w_down, m_norm_mlp_post)
    v_vals = (v_norm_mix_pre, v_w_in, v_conv_ssm_w, v_conv_ssm_b, v_dt_bias, v_a_log, v_d_skip, v_ssm_norm, v_conv_lru_w, v_conv_lru_b, v_lru_wa, v_lru_ba, v_lru_wx, v_lru_bx, v_lru_lambda, v_w_out, v_norm_mix_post, v_norm_mlp_pre, v_w_up, v_w_down, v_norm_mlp_post)
    w = dict(zip(WEIGHTS, vals))
    m = dict(zip(WEIGHTS, m_vals))
    v = dict(zip(WEIGHTS, v_vals))
    me = 4 * lax.axis_index("x") + 2 * lax.axis_index("y") + lax.axis_index("c")

    bf = lambda t: t.astype(MXU)
    late = _late_pack(w)
    early = _all_gather(_early_pack(w, me), "early_weights_all_gather")
    late, early = lax.optimization_barrier((late, early))
    lw = {}
    lw["sems"], lw["src"], lw["land"], token = _exchange_start(late, None, PLAN_GATHER_ICI, "late_weights_ici_start")
    w_in_pt, conv_w = _early_unpack(early)
    full = {n: (conv_w[n] if n in conv_w else w[n][0]) for n in WEIGHTS if n not in BIG}
    full["norm_mix_pre"] = full["norm_mix_pre"] + token[0, 0]

    def after_ssd(after):
        src, land = _exchange_wait(lw["sems"], lw["src"], lw["land"], after, PLAN_GATHER_ICI, "late_weights_ici_wait")
        lw["sems"], lw["src"], lw["land"], tok = _exchange_start(src, land, PLAN_GATHER_D2D, "late_weights_d2d_start")
        return tok[0, 0]

    def late_weights(after):
        src, land = _exchange_wait(lw["sems"], lw["src"], lw["land"], after, PLAN_GATHER_D2D, "late_weights_d2d_wait")
        return _late_unpack(_own_slot(land, src))

    sent = {}

    def send_mlp_grads(d_w_up_t, d_w_down, d_w_out):
        src = jnp.concatenate([bf(d_w_up_t).reshape(N_DEV, -1, D), bf(d_w_down).reshape(N_DEV, -1, D),
                               bf(d_w_out).reshape(N_DEV, -1, D)], axis=1)
        sent["sems"], sent["src"], sent["land"], tok = _exchange_start(src, None, PLAN_SCATTER, "mlp_grads_start")
        return tok[0, 0]

    def send_late_grads(d_w_in_pt):
        src = _scatter_w_in_grad(d_w_in_pt)
        chip = _pair_sum(src, _pair_exchange(src, "late_grads_pair_exchange"), "late_grads_pair_sum")
        sent["sems2"], sent["src2"], sent["land2"], tok = _exchange_start(chip, None, PLAN_SCATTER_CHIPS,
                                                                          "late_grads_start")
        return tok[0, 0]

    loss, grad_x, g = _local_step(x[0], loss_target[0], _prep_params(full, dict(w_in_pT=w_in_pt)), after_ssd,
                                  late_weights, send_mlp_grads, send_late_grads)

    own = lambda src: lax.dynamic_index_in_dim(src, me, keepdims=False)
    out_g, out_d, out_m, out_v = {}, {}, {}, {}
    mlp_src, mlp_land = _exchange_wait(sent["sems"], sent["src"], sent["land"], grad_x, PLAN_SCATTER, "mlp_grads_wait")
    g_mlp = _slot_sum(_own_slot(mlp_land, own(mlp_src)), "slot_sum_mlp")
    fs = FF // N_DEV
    for n, gn in (("w_up", g_mlp[:fs].T[None]), ("w_down", g_mlp[fs:2 * fs][None]), ("w_out", g_mlp[2 * fs:][None])):
        out_g[n] = gn
        out_d[n], out_m[n], out_v[n] = _adam_big(w[n], gn, m[n], v[n], "adam_" + n)
    zrow = jnp.zeros((1, D), F32)
    pad16 = lambda a: jnp.pad(a, ((0, 0), (0, 128 - NH)))
    small_parts = [
        jnp.concatenate([g["norm_mix_pre"], g["ssm_norm"], g["conv_lru_b"], g["lru_lambda"], g["norm_mix_post"],
                         g["norm_mlp_pre"], g["norm_mlp_post"], zrow], axis=0),
        g["conv_ssm_b"],
        jnp.concatenate([pad16(g["dt_bias"]), pad16(g["a_log"]), pad16(g["d_skip"]), jnp.full((1, 128), loss, F32),
                         jnp.zeros((4, 128), F32)], axis=0),
        g["conv_ssm_w"], g["conv_lru_w"],
        jnp.concatenate([g["lru_wa"].reshape(D, 64), g["lru_wx"].reshape(D, 64)], axis=0),
        jnp.concatenate([g["lru_ba"].reshape(NH, 64), g["lru_bx"].reshape(NH, 64)], axis=0)]
    small = _pad_rows(jnp.concatenate([s.reshape(-1) for s in small_parts]), 8)
    small, _ = lax.optimization_barrier((small, out_v["w_out"]))
    small_all = _all_gather(small, "small_grads_all_gather")

    lg_src, lg_land = _exchange_wait(sent["sems2"], sent["src2"], sent["land2"], small_all, PLAN_SCATTER_CHIPS,
                                     "late_grads_wait")
    my_chip = me >> 1
    lg_own = lax.dynamic_index_in_dim(lg_src, my_chip, keepdims=True)
    g_late = _slot_sum(lax.dynamic_update_slice_in_dim(lg_land, lg_own, my_chip, axis=0), "slot_sum_late")
    gt = lax.dynamic_slice(g_late, (2 * me, 0), (IN_ROWS, D))
    lin = lambda a: a[0].T.reshape(IN_ROWS, D // 128, 128)
    for out, val in zip((out_g, out_d, out_m, out_v),
                        _adam_rows_linear(lin(w["w_in"]), gt, lin(m["w_in"]), lin(v["w_in"]), "adam_w_in")):
        out["w_in"] = val.reshape(IN_ROWS, D).T[None]
    sflat = small_all.reshape(N_DEV, -1)
    groups = []
    off = 0
    for r, c in SMALL_GROUPS:
        groups.append(sflat[:, off:off + r * c].reshape(N_DEV, r, c))
        off += r * c
    groups[3] = lax.dynamic_slice_in_dim(groups[3], me * CONV_SSM_COLS, CONV_SSM_COLS, axis=2)
    groups[4] = lax.dynamic_slice_in_dim(groups[4], me * CONV_LRU_COLS, CONV_LRU_COLS, axis=2)
    wmv = [(w[n].reshape(s), m[n].reshape(s), v[n].reshape(s)) for n, s, _, _ in SMALL]
    res, loss_row = _adam_small(groups, [(gi, r0) for _, _, gi, r0 in SMALL], wmv, (2, 3))
    loss = loss_row[0, 0]
    for (n, _, _, _), (g_n, d_n, m_n, v_n) in zip(SMALL, res):
        shape = w[n].shape
        out_g[n], out_d[n], out_m[n], out_v[n] = (g_n.reshape(shape), d_n.reshape(shape), m_n.reshape(shape),
                                                  v_n.reshape(shape))
    return (loss, grad_x[None], *[out_g[n] for n in WEIGHTS], *[out_d[n] for n in WEIGHTS],
            *[out_m[n] for n in WEIGHTS], *[out_v[n] for n in WEIGHTS])
```

```python
import functools

import jax
import jax.numpy as jnp
from jax import lax
from jax.experimental import pallas as pl
from jax.experimental.pallas import tpu as pltpu

F32 = jnp.float32
MXU = jnp.bfloat16
HI = lax.Precision.HIGHEST
EPS = 1e-6

D = 1024
NH = 16
NS = 128
CH = 128
FF = 4096
NP = 7168
SEG = 1024
LRU_C = 8.0
N_DEV = 8

ADAM_LR, ADAM_B1, ADAM_B2, ADAM_EPS, ADAM_WD, ADAM_STEP = 0.001, 0.9, 0.999, 1e-08, 0.01, 10

VMEM_LIMIT = 56 * 1024 * 1024


def _cp(*sem):
    return pltpu.CompilerParams(dimension_semantics=sem, vmem_limit_bytes=VMEM_LIMIT)


def _nn(a, b):
    return jnp.dot(a.astype(MXU), b.astype(MXU), preferred_element_type=F32)


def _nt(a, b):
    return lax.dot_general(a.astype(MXU), b.astype(MXU), (((1,), (1,)), ((), ())), preferred_element_type=F32)


def _tn(a, b):
    return lax.dot_general(a.astype(MXU), b.astype(MXU), (((0,), (0,)), ((), ())), preferred_element_type=F32)


_sigmoid = jax.nn.sigmoid


def _silu(x):
    return x * _sigmoid(x)


def _dsilu(x):
    s = _sigmoid(x)
    return s + x * s * (1.0 - s)


def _softplus(x):
    return jnp.maximum(x, 0.0) + jnp.log(1.0 + jnp.exp(-jnp.abs(x)))


def _rms(x, g):
    r = lax.rsqrt(jnp.mean(x * x, axis=-1, keepdims=True) + EPS)
    return x * r * g


def _rms_bwd(x, g, dy):
    r = lax.rsqrt(jnp.mean(x * x, axis=-1, keepdims=True) + EPS)
    gdy = g * dy
    dx = r * gdy - x * (r * r * r) * jnp.mean(x * gdy, axis=-1, keepdims=True)
    return dx, dy * x * r


def _rowsum(x):
    return jnp.sum(x, axis=0, keepdims=True)


def _taps_past(cur, prev8):
    r_n, c_n = cur.shape
    row = lax.broadcasted_iota(jnp.int32, (r_n, c_n), 0)
    out = []
    for k in range(4):
        s = 3 - k
        if s == 0:
            out.append(cur)
            continue
        head = jnp.concatenate([pltpu.roll(prev8, s, 0), jnp.zeros((r_n - 8, c_n), F32)], axis=0)
        out.append(jnp.where(row < s, head, pltpu.roll(cur, s, 0)))
    return out


def _taps_future(cur, fut8):
    r_n, c_n = cur.shape
    row = lax.broadcasted_iota(jnp.int32, (r_n, c_n), 0)
    out = []
    for k in range(4):
        s = 3 - k
        if s == 0:
            out.append(cur)
            continue
        tail = jnp.concatenate([jnp.zeros((r_n - 8, c_n), F32), pltpu.roll(fut8, 8 - s, 0)], axis=0)
        out.append(jnp.where(row >= r_n - s, tail, pltpu.roll(cur, r_n - s, 0)))
    return out


def _conv_apply(taps, w, b):
    acc = taps[0] * w[0:1, :]
    for k in range(1, 4):
        acc = acc + taps[k] * w[k:k + 1, :]
    return acc + b


def _inproj(x, g0, w):
    t_n = x.shape[0]
    tm = min(t_n, 1024)

    n_j = NP // SEG

    def body(x_ref, g_ref, w_ref, pb_ref, p6_ref, u_ref):
        j = pl.program_id(1)

        @pl.when(j == 0)
        def _():
            u_ref[...] = _rms(x_ref[...], g_ref[...]).astype(MXU)

        p = lax.dot_general(u_ref[...], w_ref[...], (((1,), (1,)), ((), ())), preferred_element_type=F32)

        @pl.when(j < n_j - 1)
        def _():
            pb_ref[...] = p.astype(MXU)

        @pl.when(j == n_j - 1)
        def _():
            p6_ref[...] = p

    pb, p6, u = pl.pallas_call(
        body, name="inproj", grid=(t_n // tm, n_j),
        in_specs=[pl.BlockSpec((tm, D), lambda i, j: (i, 0)), pl.BlockSpec((1, D), lambda i, j: (0, 0)),
                  pl.BlockSpec((SEG, D), lambda i, j: (j, 0))],
        out_specs=[pl.BlockSpec((tm, SEG), lambda i, j: (i, jnp.minimum(j, n_j - 2))),
                   pl.BlockSpec((tm, SEG), lambda i, j: (i, 0)), pl.BlockSpec((tm, D), lambda i, j: (i, 0))],
        out_shape=[jax.ShapeDtypeStruct((t_n, NP - SEG), MXU), jax.ShapeDtypeStruct((t_n, SEG), F32),
                   jax.ShapeDtypeStruct((t_n, D), MXU)],
        compiler_params=_cp("parallel", "arbitrary"),
    )(x, g0, w)
    return (pb, p6), u


def _ssd_prep(dtraw, dtb, alog):
    l_n = dtraw.shape[0]
    r = lax.broadcasted_iota(jnp.int32, (l_n, l_n), 0)
    c = lax.broadcasted_iota(jnp.int32, (l_n, l_n), 1)
    tril = (r >= c).astype(F32)
    triu = (r <= c).astype(F32)
    eye = (r == c).astype(F32)
    dt = _softplus(dtraw + dtb)
    adt = dt * (-jnp.exp(alog))
    ac = jnp.dot(tril, adt, preferred_element_type=F32, precision=HI)
    tn = (((0,), (0,)), ((), ()))
    ac_t = lax.dot_general(adt, triu, tn, preferred_element_type=F32, precision=HI)
    dt_t = lax.dot_general(dt, eye, tn, preferred_element_type=F32, precision=HI)
    return dt, dt_t, ac, ac_t, _rowsum(adt)


def _ssd_pair(j, xp, bg, cg, sp, dt, dt_t, ac, ac_t, aend):
    l_n = xp.shape[0]
    lane = lax.broadcasted_iota(jnp.int32, (l_n, 128), 1)
    sub = lax.broadcasted_iota(jnp.int32, (128, l_n), 0)
    lane1 = lax.broadcasted_iota(jnp.int32, (1, 128), 1)
    tri = lax.broadcasted_iota(jnp.int32, (l_n, l_n), 0) >= lax.broadcasted_iota(jnp.int32, (l_n, l_n), 1)
    lo = lax.broadcasted_iota(jnp.int32, (l_n, 128), 1) < 64
    lo_s = lax.broadcasted_iota(jnp.int32, (128, 128), 1) < 64
    cb = _nt(cg, bg)
    cs = _nn(cg, sp)
    x2 = jnp.concatenate([jnp.where(lo, xp, 0.0), jnp.where(lo, 0.0, xp)], axis=0)
    ws_, bs_, eo, ee = [], [], [], []
    for e in range(2):
        h = 2 * j + e
        ac_l = jnp.sum(jnp.where(lane == h, ac, 0.0), axis=1, keepdims=True)
        dt_l = jnp.sum(jnp.where(lane == h, dt, 0.0), axis=1, keepdims=True)
        a_end = jnp.sum(jnp.where(lane1 == h, aend, 0.0), axis=1, keepdims=True)
        ac_s, dt_s = ac_t[h:h + 1, :], dt_t[h:h + 1, :]
        decay = jnp.exp(jnp.where(tri, ac_l - ac_s, -1e30))
        ws_.append(cb * decay * dt_s)
        bs_.append(bg * (jnp.exp(a_end - ac_l) * dt_l))
        eo.append(jnp.exp(ac_l))
        ee.append(jnp.exp(a_end))
    y = _nn(jnp.concatenate(ws_, axis=1), x2) + jnp.where(lo, eo[0], eo[1]) * cs
    s_new = _tn(jnp.concatenate(bs_, axis=0), x2) + jnp.where(lo_s, ee[0], ee[1]) * sp
    return y, s_new


def _ssd_post(y, xs, z, dsk, nrm):
    y = (y + dsk * xs) * _silu(z)
    half = D // 2
    ya, yb = y[:, :half], y[:, half:]
    ya = ya * lax.rsqrt(jnp.mean(ya * ya, axis=-1, keepdims=True) + EPS)
    yb = yb * lax.rsqrt(jnp.mean(yb * yb, axis=-1, keepdims=True) + EPS)
    return jnp.concatenate([ya, yb], axis=1) * nrm


LAST_SEG = NP // SEG - 1


def _proj_ops(proj, seg_ids):
    return [proj[1] if s == LAST_SEG else proj[0] for s in seg_ids]


def _proj_specs(rows, seg_ids, order):
    return [pl.BlockSpec((rows, SEG), functools.partial(lambda i, c: (order(i), c), c=0 if s == LAST_SEG else s))
            for s in seg_ids]


def _prev_specs(rows, seg_ids, order):
    specs = []
    for s in seg_ids:
        n, c = (8, 0) if s == LAST_SEG else (16, s)
        specs.append(pl.BlockSpec((n, SEG), functools.partial(
            lambda i, n, c: (jnp.maximum(order(i) * (rows // n) - 1, 0), c), n=n, c=c)))
    return specs


def _prev8(ref):
    return ref[...] if ref.shape[0] == 8 else ref[8:16, :].astype(F32)


def _full(shape):
    return pl.BlockSpec(shape, lambda i: (0,) * len(shape))


def _slotted(a):
    return a if isinstance(a, tuple) else (a.reshape(N_DEV, a.shape[0] // N_DEV, a.shape[1]), 0)


def _slot_spec(w, rows, tile, index):
    per = rows // N_DEV
    b = w[1]
    return pl.BlockSpec((tile // per, per, D), lambda *ids: (index(*ids), b, 0))


def _ssd_fwd(proj, cwx, cwb, cbx, cbb, dtb, alog, dsk, nrm):
    t_n = proj[0].shape[0]
    n_c = t_n // CH
    fwd = lambda i: i

    def body(z_ref, xs_ref, bc_ref, xsp_ref, bcp_ref, cwx_ref, cwb_ref, cbx_ref, cbb_ref, dtb_ref, alog_ref,
             dsk_ref, nrm_ref, ya_ref, sprev_ref, yraw_ref, xspre_ref, bcpre_ref, s_ref):
        c = pl.program_id(0)

        @pl.when(c == 0)
        def _():
            s_ref[...] = jnp.zeros_like(s_ref)

        keep = jnp.where(c == 0, 0.0, 1.0)
        xs_pre = _conv_apply(_taps_past(xs_ref[...].astype(F32), _prev8(xsp_ref) * keep), cwx_ref[...], cbx_ref[...])
        bc_pre = _conv_apply(_taps_past(bc_ref[:, :512], bcp_ref[:, :512] * keep), cwb_ref[...], cbb_ref[...])
        xspre_ref[...] = xs_pre
        bcpre_ref[...] = bc_pre
        prep = _ssd_prep(bc_ref[:, 512:640], dtb_ref[...], alog_ref[...])
        xs = _silu(xs_pre)
        bc = _silu(bc_pre)
        sprev_ref[0] = s_ref[...]
        ys = []
        for j in range(NH // 2):
            g = j // 4
            yp, sn = _ssd_pair(j, xs[:, 128 * j:128 * j + 128], bc[:, 128 * g:128 * g + 128],
                               bc[:, 256 + 128 * g:384 + 128 * g], s_ref[:, 128 * j:128 * j + 128], *prep)
            ys.append(yp)
            s_ref[:, 128 * j:128 * j + 128] = sn
        y = jnp.concatenate(ys, axis=1)
        yraw_ref[...] = y
        ya_ref[...] = _ssd_post(y, xs, z_ref[...].astype(F32), dsk_ref[...], nrm_ref[...]).astype(ya_ref.dtype)

    return pl.pallas_call(
        body, name="ssd_fwd", grid=(n_c,),
        in_specs=_proj_specs(CH, (0, 5, 6), fwd) + _prev_specs(CH, (5, 6), fwd) + [
            _full((4, D)), _full((4, 512)), _full((1, D)), _full((1, 512)), _full((1, 128)), _full((1, 128)),
            _full((1, D)), _full((1, D))],
        out_specs=[pl.BlockSpec((CH, D), lambda i: (i, 0)), pl.BlockSpec((1, NS, D), lambda i: (i, 0, 0)),
                   pl.BlockSpec((CH, D), lambda i: (i, 0)), pl.BlockSpec((CH, D), lambda i: (i, 0)),
                   pl.BlockSpec((CH, 512), lambda i: (i, 0))],
        out_shape=[jax.ShapeDtypeStruct((t_n, D), MXU), jax.ShapeDtypeStruct((n_c, NS, D), F32),
                   jax.ShapeDtypeStruct((t_n, D), F32), jax.ShapeDtypeStruct((t_n, D), F32),
                   jax.ShapeDtypeStruct((t_n, 512), F32)],
        scratch_shapes=[pltpu.VMEM((NS, D), F32)],
        compiler_params=_cp("arbitrary"),
    )(*_proj_ops(proj, (0, 5, 6, 5, 6)), cwx, cwb, cbx, cbb, dtb, alog, dsk, nrm)


def _ssd_bwd(dya, saved, proj, sprev, cwx, cwb, dtb, alog, dsk, nrm, mlp_ops):
    t_n = proj[0].shape[0]
    n_c = t_n // CH
    rev = lambda i: n_c - 1 - i
    fb = FF // n_c

    def body(dya_ref, yraw_ref, xspre_ref, bcpre_ref, z_ref, xs_ref, bc_ref, sprev_ref, cwx_ref, cwb_ref,
             dtb_ref, alog_ref, dsk_ref, nrm_ref, hid_ref, dff_ref, dhp_ref, v_ref,
             dz_ref, dxs_ref, dbc_ref, dcwx_ref, dcwb_ref, dcbx_ref, dcbb_ref, ddtb_ref, dalog_ref, ddsk_ref,
             dnrm_ref, dwd_ref, dwu_ref, ds_ref, futx_ref, futb_ref):
        i = pl.program_id(0)
        acc_refs = (dcwx_ref, dcwb_ref, dcbx_ref, dcbb_ref, ddtb_ref, dalog_ref, ddsk_ref, dnrm_ref)
        tn = (((0,), (0,)), ((), ()))
        kt = t_n // (NH // 2)
        dwd_acc = jnp.zeros((fb, D), F32)
        dwu_acc = jnp.zeros((fb, D), F32)

        @pl.when(i == 0)
        def _():
            for r in (ds_ref, futx_ref, futb_ref) + acc_refs:
                r[...] = jnp.zeros_like(r)

        xs_pre = xspre_ref[...]
        bc_pre = bcpre_ref[...]
        xs = _silu(xs_pre)
        bc = _silu(bc_pre)
        prep, prep_vjp = jax.vjp(_ssd_prep, bc_ref[:, 512:640], dtb_ref[...], alog_ref[...])
        s_in = sprev_ref[0]

        def pair_args(j):
            g = j // 4
            return (xs[:, 128 * j:128 * j + 128], bc[:, 128 * g:128 * g + 128],
                    bc[:, 256 + 128 * g:384 + 128 * g], s_in[:, 128 * j:128 * j + 128]) + tuple(prep)

        _, post_vjp = jax.vjp(_ssd_post, yraw_ref[...], xs, z_ref[...].astype(F32), dsk_ref[...], nrm_ref[...])
        dy, dxs_skip, dz, ddsk, dnrm = post_vjp(dya_ref[...])
        dz_ref[...] = dz.astype(dz_ref.dtype)
        ddsk_ref[...] += ddsk
        dnrm_ref[...] += dnrm

        dprep = [jnp.zeros_like(p) for p in prep]
        dxp = []
        dbg = [jnp.zeros((CH, 128), F32), jnp.zeros((CH, 128), F32)]
        dcg = [jnp.zeros((CH, 128), F32), jnp.zeros((CH, 128), F32)]
        for j in range(NH // 2):
            g = j // 4
            _, pair_vjp = jax.vjp(functools.partial(_ssd_pair, j), *pair_args(j))
            cts = pair_vjp((dy[:, 128 * j:128 * j + 128], ds_ref[:, 128 * j:128 * j + 128]))
            dxp.append(cts[0])
            dbg[g] = dbg[g] + cts[1]
            dcg[g] = dcg[g] + cts[2]
            ds_ref[:, 128 * j:128 * j + 128] = cts[3]
            dprep = [a + b for a, b in zip(dprep, cts[4:])]
            rows = pl.ds(kt * j, kt)
            dwd_acc = dwd_acc + lax.dot_general(hid_ref[rows, :], dff_ref[rows, :], tn, preferred_element_type=F32)
            dwu_acc = dwu_acc + lax.dot_general(dhp_ref[rows, :], v_ref[rows, :], tn, preferred_element_type=F32)
        dwd_ref[...] = dwd_acc.astype(MXU)
        dwu_ref[...] = dwu_acc.astype(MXU)
        ddtraw, ddtb, dalog = prep_vjp(tuple(dprep))
        ddtb_ref[...] += ddtb
        dalog_ref[...] += dalog

        dxs_pre = (dxs_skip + jnp.concatenate(dxp, axis=1)) * _dsilu(xs_pre)
        dbc_pre = jnp.concatenate([dbg[0], dbg[1], dcg[0], dcg[1]], axis=1) * _dsilu(bc_pre)
        dcbx_ref[...] += _rowsum(dxs_pre)
        dcbb_ref[...] += _rowsum(dbc_pre)
        fx = _taps_future(dxs_pre, futx_ref[...])
        fbc = _taps_future(dbc_pre, futb_ref[...])
        xs_in = xs_ref[...].astype(F32)
        bc_in = bc_ref[:, :512]
        for k in range(4):
            dcwx_ref[k:k + 1, :] += _rowsum(fx[k] * xs_in)
            dcwb_ref[k:k + 1, :] += _rowsum(fbc[k] * bc_in)
        cwx = cwx_ref[...]
        cwb = cwb_ref[...]
        dxs_in = fx[0] * cwx[0:1, :]
        dbc_in = fbc[0] * cwb[0:1, :]
        for k in range(1, 4):
            dxs_in = dxs_in + fx[k] * cwx[k:k + 1, :]
            dbc_in = dbc_in + fbc[k] * cwb[k:k + 1, :]
        futx_ref[...] = dxs_pre[0:8, :]
        futb_ref[...] = dbc_pre[0:8, :]
        dxs_ref[...] = dxs_in.astype(dxs_ref.dtype)
        dbc_ref[...] = jnp.concatenate([dbc_in, ddtraw, jnp.zeros((CH, SEG - 640), F32)], axis=1).astype(dbc_ref.dtype)

    row_out = lambda: pl.BlockSpec((CH, D), lambda i: (rev(i), 0))
    outs = pl.pallas_call(
        body, name="ssd_bwd", grid=(n_c,),
        in_specs=[row_out(), row_out(), row_out(), pl.BlockSpec((CH, 512), lambda i: (rev(i), 0))]
        + _proj_specs(CH, (0, 5, 6), rev) + [pl.BlockSpec((1, NS, D), lambda i: (rev(i), 0, 0)),
                                             _full((4, D)), _full((4, 512)),
                                             _full((1, 128)), _full((1, 128)), _full((1, D)), _full((1, D)),
                                             pl.BlockSpec((t_n, fb), lambda i: (0, i)), _full((t_n, D)),
                                             pl.BlockSpec((t_n, fb), lambda i: (0, i)), _full((t_n, D))],
        out_specs=[row_out(), row_out(), row_out(), _full((4, D)), _full((4, 512)), _full((1, D)), _full((1, 512)),
                   _full((1, 128)), _full((1, 128)), _full((1, D)), _full((1, D)),
                   pl.BlockSpec((fb, D), lambda i: (i, 0)), pl.BlockSpec((fb, D), lambda i: (i, 0))],
        out_shape=[jax.ShapeDtypeStruct((t_n, D), MXU)] * 3 + [
            jax.ShapeDtypeStruct(s, F32) for s in ((4, D), (4, 512), (1, D), (1, 512), (1, 128), (1, 128), (1, D), (1, D))]
        + [jax.ShapeDtypeStruct((FF, D), MXU)] * 2,
        scratch_shapes=[pltpu.VMEM((NS, D), F32), pltpu.VMEM((8, D), F32), pltpu.VMEM((8, 512), F32)],
        compiler_params=_cp("arbitrary"),
    )(dya, *saved, *_proj_ops(proj, (0, 5, 6)), sprev, cwx, cwb, dtb, alog, dsk, nrm, *mlp_ops)
    return outs


LRU_ROWS = 256
LRU_BLK = 256


def _lru_gates(xr, wa, wx, ba, bx, lam):
    pr = jnp.concatenate([_nn(xr[:, LRU_BLK * b:LRU_BLK * (b + 1)], wa[b]) for b in range(D // LRU_BLK)], axis=1) + ba
    pi = jnp.concatenate([_nn(xr[:, LRU_BLK * b:LRU_BLK * (b + 1)], wx[b]) for b in range(D // LRU_BLK)], axis=1) + bx
    log_a = -LRU_C * _sigmoid(pr) * _softplus(-lam)
    a = jnp.exp(log_a)
    mult = jnp.sqrt(1.0 - jnp.exp(2.0 * log_a))
    return a, mult * (_sigmoid(pi) * xr)


def _lru_out(h, g):
    return h * jax.nn.gelu(g, approximate=True)


def _lru_fwd(proj, cw, cb, wa, wx, ba, bx, lam):
    t_n = proj[0].shape[0]
    rows = min(LRU_ROWS, t_n)
    fwd = lambda i: i

    def body(g_ref, x_ref, xp_ref, cw_ref, cb_ref, wa_ref, wx_ref, ba_ref, bx_ref, lam_ref, yb_ref, h_ref, xr_ref,
             a_s, u_s, carry):
        i = pl.program_id(0)

        @pl.when(i == 0)
        def _():
            carry[...] = jnp.zeros_like(carry)

        keep = jnp.where(i == 0, 0.0, 1.0)
        xr = _conv_apply(_taps_past(x_ref[...].astype(F32), _prev8(xp_ref) * keep), cw_ref[...], cb_ref[...])
        xr_ref[...] = xr
        a, u = _lru_gates(xr, wa_ref[...], wx_ref[...], ba_ref[...], bx_ref[...], lam_ref[...])
        a_s[...] = a
        u_s[...] = u
        row = lax.broadcasted_iota(jnp.int32, (8, D), 0)

        def blk(b, c):
            s = pl.multiple_of(b * 8, 8)
            av = a_s[pl.ds(s, 8), :]
            uv = u_s[pl.ds(s, 8), :]
            for d in (1, 2, 4):
                m = row >= d
                uv = uv + av * jnp.where(m, pltpu.roll(uv, d, 0), 0.0)
                av = av * jnp.where(m, pltpu.roll(av, d, 0), 1.0)
            hv = uv + av * c
            h_ref[pl.ds(s, 8), :] = hv
            return hv[7:8, :]

        carry[0:1, :] = lax.fori_loop(0, rows // 8, blk, carry[0:1, :])
        yb_ref[...] = _lru_out(h_ref[...], g_ref[...].astype(F32)).astype(yb_ref.dtype)

    return pl.pallas_call(
        body, name="lru_fwd", grid=(t_n // rows,),
        in_specs=_proj_specs(rows, (1, 2), fwd) + _prev_specs(rows, (2,), fwd) + [
            _full((4, D)), _full((1, D)), _full((4, LRU_BLK, LRU_BLK)), _full((4, LRU_BLK, LRU_BLK)),
            _full((1, D)), _full((1, D)), _full((1, D))],
        out_specs=[pl.BlockSpec((rows, D), lambda i: (i, 0))] * 3,
        out_shape=[jax.ShapeDtypeStruct((t_n, D), MXU), jax.ShapeDtypeStruct((t_n, D), F32),
                   jax.ShapeDtypeStruct((t_n, D), F32)],
        scratch_shapes=[pltpu.VMEM((rows, D), F32), pltpu.VMEM((rows, D), F32), pltpu.VMEM((8, D), F32)],
        compiler_params=_cp("arbitrary"),
    )(*_proj_ops(proj, (1, 2, 2)), cw, cb, wa, wx, ba, bx, lam)


def _lru_bwd(dyb, proj, h, xr_saved, cw, wa, wx, ba, bx, lam):
    t_n = proj[0].shape[0]
    rows = min(LRU_ROWS, t_n)
    n_t = t_n // rows
    rev = lambda i: n_t - 1 - i
    rb = rows // 8

    def body(dyb_ref, g_ref, x_ref, h_ref, hp_ref, xr_ref, cw_ref, wa_ref, wx_ref, ba_ref, bx_ref, lam_ref,
             dg_ref, dx_ref, dcw_ref, dcb_ref, dwa_ref, dwx_ref, dba_ref, dbx_ref, dlam_ref,
             a_s, dh_s, hx_s, da_s, du_s, carry, fut):
        i = pl.program_id(0)
        acc_refs = (dcw_ref, dcb_ref, dwa_ref, dwx_ref, dba_ref, dbx_ref, dlam_ref)

        @pl.when(i == 0)
        def _():
            for r in (carry, fut) + acc_refs:
                r[...] = jnp.zeros_like(r)

        keep = jnp.where(i == n_t - 1, 0.0, 1.0)
        gate_in = (xr_ref[...], wa_ref[...], wx_ref[...], ba_ref[...], bx_ref[...], lam_ref[...])
        (a, _), gates_vjp = jax.vjp(_lru_gates, *gate_in)
        _, out_vjp = jax.vjp(_lru_out, h_ref[...], g_ref[...].astype(F32))
        dh, dg = out_vjp(dyb_ref[...])
        dg_ref[...] = dg.astype(dg_ref.dtype)
        a_s[...] = a
        dh_s[...] = dh
        hx_s[0:8, :] = hp_ref[...] * keep
        hx_s[8:, :] = h_ref[...]
        row = lax.broadcasted_iota(jnp.int32, (8, D), 0)

        def blk(b, c):
            s = pl.multiple_of((rb - 1 - b) * 8, 8)
            av = a_s[pl.ds(s, 8), :]
            dhv = dh_s[pl.ds(s, 8), :]
            kv = av * dhv
            for d in (1, 2, 4):
                m = row <= 7 - d
                kv = kv + av * jnp.where(m, pltpu.roll(kv, 8 - d, 0), 0.0)
                av = av * jnp.where(m, pltpu.roll(av, 8 - d, 0), 1.0)
            kv = kv + av * c
            gv = dhv + jnp.where(row < 7, pltpu.roll(kv, 7, 0), c)
            hb = hx_s[pl.ds(s + 8, 8), :]
            hpv = hx_s[pl.ds(s, 8), :]
            hprev = jnp.where(row >= 1, pltpu.roll(hb, 1, 0), hpv[7:8, :])
            du_s[pl.ds(s, 8), :] = gv
            da_s[pl.ds(s, 8), :] = gv * hprev
            return kv[0:1, :]

        carry[0:1, :] = lax.fori_loop(0, rb, blk, carry[0:1, :])
        dxr, dwa, dwx, dba, dbx, dlam = gates_vjp((da_s[...], du_s[...]))
        dwa_ref[...] += dwa
        dwx_ref[...] += dwx
        dba_ref[...] += dba
        dbx_ref[...] += dbx
        dlam_ref[...] += dlam
        dcb_ref[...] += _rowsum(dxr)
        ft = _taps_future(dxr, fut[...])
        x_in = x_ref[...].astype(F32)
        for k in range(4):
            dcw_ref[k:k + 1, :] += _rowsum(ft[k] * x_in)
        cwv = cw_ref[...]
        dx = ft[0] * cwv[0:1, :]
        for k in range(1, 4):
            dx = dx + ft[k] * cwv[k:k + 1, :]
        fut[...] = dxr[0:8, :]
        dx_ref[...] = dx.astype(dx_ref.dtype)

    row_in = lambda: pl.BlockSpec((rows, D), lambda i: (rev(i), 0))
    prev_h = pl.BlockSpec((8, D), lambda i: (jnp.maximum(rev(i) * rb - 1, 0), 0))
    wspec = lambda: _full((4, LRU_BLK, LRU_BLK))
    return pl.pallas_call(
        body, name="lru_bwd", grid=(n_t,),
        in_specs=[row_in()] + _proj_specs(rows, (1, 2), rev) + [row_in(), prev_h, row_in()] + [
            _full((4, D)), wspec(), wspec(), _full((1, D)), _full((1, D)), _full((1, D))],
        out_specs=[row_in(), row_in(), _full((4, D)), _full((1, D)), wspec(), wspec(), _full((1, D)), _full((1, D)),
                   _full((1, D))],
        out_shape=[jax.ShapeDtypeStruct((t_n, D), MXU)] * 2 + [
            jax.ShapeDtypeStruct(s, F32) for s in ((4, D), (1, D), (4, LRU_BLK, LRU_BLK), (4, LRU_BLK, LRU_BLK),
                                                   (1, D), (1, D), (1, D))],
        scratch_shapes=[pltpu.VMEM((rows, D), F32), pltpu.VMEM((rows, D), F32), pltpu.VMEM((rows + 8, D), F32),
                        pltpu.VMEM((rows, D), F32), pltpu.VMEM((rows, D), F32), pltpu.VMEM((8, D), F32),
                        pltpu.VMEM((8, D), F32)],
        compiler_params=_cp("arbitrary"),
    )(dyb, *_proj_ops(proj, (1, 2)), h, h, xr_saved, cw, wa, wx, ba, bx, lam)


def _merge_out(ya, yb, proj, x, wout, g1):
    t_n = x.shape[0]
    tm = min(t_n, 512)

    def body(ya_ref, yb_ref, ga_ref, gb_ref, x_ref, w_ref, g_ref, h1_ref, mix_ref, mg_ref):
        merged = (_sigmoid(ga_ref[...].astype(F32)) * ya_ref[...].astype(F32)
                  + _sigmoid(gb_ref[...].astype(F32)) * yb_ref[...].astype(F32))
        mg = merged.astype(MXU)
        mg_ref[...] = mg
        mix = jnp.dot(mg, w_ref[...].reshape(D, D), preferred_element_type=F32)
        mix_ref[...] = mix
        h1_ref[...] = x_ref[...] + _rms(mix, g_ref[...])

    row = lambda: pl.BlockSpec((tm, D), lambda i: (i, 0))
    return pl.pallas_call(
        body, name="merge_out", grid=(t_n // tm,),
        in_specs=[row(), row()] + _proj_specs(tm, (3, 4), lambda i: i)
        + [row(), _slot_spec(wout, D, D, lambda i: 0), _full((1, D))],
        out_specs=[row(), row(), row()],
        out_shape=[jax.ShapeDtypeStruct((t_n, D), F32), jax.ShapeDtypeStruct((t_n, D), F32),
                   jax.ShapeDtypeStruct((t_n, D), MXU)],
        compiler_params=_cp("parallel"),
    )(ya, yb, *_proj_ops(proj, (3, 4)), x, wout[0], g1)


def _out_bwd(dv, h1, g2, dout, mix, ya, yb, proj, wout, g1, w_in_pt):
    t_n = dv.shape[0]
    tm = min(t_n, 256)

    def body(dv_ref, h1_ref, g2_ref, dout_ref, mix_ref, ya_ref, yb_ref, ga_ref, gb_ref, w_ref, g_ref, wa_ref, wb_ref,
             dh1_ref, dmix_ref, dya_ref, dyb_ref, dga_ref, dgb_ref, dg2_ref, dg1_ref, dug_ref):
        @pl.when(pl.program_id(0) == 0)
        def _():
            dg1_ref[...] = jnp.zeros_like(dg1_ref)
            dg2_ref[...] = jnp.zeros_like(dg2_ref)

        dx, dg_rows = _rms_bwd(h1_ref[...], g2_ref[...], dv_ref[...])
        dg2_ref[...] += _rowsum(dg_rows)
        dh1 = dout_ref[...] + dx
        dh1_ref[...] = dh1
        dmix, dg_rows = _rms_bwd(mix_ref[...], g_ref[...], dh1)
        dg1_ref[...] += _rowsum(dg_rows)
        dmix_b = dmix.astype(MXU)
        dmix_ref[...] = dmix_b
        dmg = lax.dot_general(dmix_b, w_ref[...].reshape(D, D), (((1,), (1,)), ((), ())), preferred_element_type=F32)
        sa = _sigmoid(ga_ref[...].astype(F32))
        sb = _sigmoid(gb_ref[...].astype(F32))
        dya_ref[...] = dmg * sa
        dyb_ref[...] = dmg * sb
        dga = (dmg * ya_ref[...].astype(F32) * sa * (1.0 - sa)).astype(MXU)
        dga_ref[...] = dga
        dug = jnp.dot(dga, wa_ref[...], preferred_element_type=F32)
        dgb = (dmg * yb_ref[...].astype(F32) * sb * (1.0 - sb)).astype(MXU)
        dgb_ref[...] = dgb
        dug_ref[...] = dug + jnp.dot(dgb, wb_ref[...], preferred_element_type=F32)

    row = lambda: pl.BlockSpec((tm, D), lambda i: (i, 0))
    vec = lambda: _full((1, D))
    seg = lambda s: pl.BlockSpec((SEG, D), lambda i: (s, 0))
    return pl.pallas_call(
        body, name="out_bwd", grid=(t_n // tm,),
        in_specs=[row(), row(), vec(), row(), row(), row(), row()] + _proj_specs(tm, (3, 4), lambda i: i)
        + [_slot_spec(wout, D, D, lambda i: 0), vec(), seg(3), seg(4)],
        out_specs=[row(), row(), row(), row(), row(), row(), vec(), vec(), row()],
        out_shape=[jax.ShapeDtypeStruct((t_n, D), F32), jax.ShapeDtypeStruct((t_n, D), MXU),
                   jax.ShapeDtypeStruct((t_n, D), F32), jax.ShapeDtypeStruct((t_n, D), F32),
                   jax.ShapeDtypeStruct((t_n, D), MXU), jax.ShapeDtypeStruct((t_n, D), MXU),
                   jax.ShapeDtypeStruct((1, D), F32), jax.ShapeDtypeStruct((1, D), F32),
                   jax.ShapeDtypeStruct((t_n, D), F32)],
        compiler_params=_cp("arbitrary"),
    )(dv, h1, g2, dout, mix, ya, yb, *_proj_ops(proj, (3, 4)), wout[0], g1, w_in_pt, w_in_pt)


MLP_TM = 1024
MLP_TF_FWD = 1024
MLP_TF_BWD = 1024


def _mlp_fwd(h1, g2, wup, wdown, g3, tgt):
    t_n = h1.shape[0]
    tm = min(t_n, MLP_TM)
    n_f = FF // MLP_TF_FWD

    def body(h1_ref, g2_ref, wu_ref, wd_ref, g3_ref, tgt_ref, hp_ref, v_ref, dout_ref, dff_ref, loss_ref, dg3_ref, acc):
        i, j = pl.program_id(0), pl.program_id(1)

        @pl.when((i == 0) & (j == 0))
        def _():
            loss_ref[...] = jnp.zeros_like(loss_ref)
            dg3_ref[...] = jnp.zeros_like(dg3_ref)

        @pl.when(j == 0)
        def _():
            v_ref[...] = _rms(h1_ref[...], g2_ref[...]).astype(MXU)
            acc[...] = jnp.zeros_like(acc)

        hp = lax.dot_general(v_ref[...], wu_ref[...].reshape(MLP_TF_FWD, D), (((1,), (1,)), ((), ())),
                             preferred_element_type=F32)
        hp_ref[...] = hp.astype(MXU)
        hid = jnp.square(jnp.maximum(hp, 0.0))
        acc[...] += jnp.dot(hid.astype(MXU), wd_ref[...].reshape(MLP_TF_FWD, D), preferred_element_type=F32)

        @pl.when(j == n_f - 1)
        def _():
            ff = acc[...]
            err = h1_ref[...] + _rms(ff, g3_ref[...]) - tgt_ref[...]
            loss_ref[...] += 0.5 * jnp.sum(jnp.mean(err * err, axis=-1, keepdims=True), axis=0, keepdims=True)
            dout = err * (1.0 / D)
            dout_ref[...] = dout
            dff, dg_rows = _rms_bwd(ff, g3_ref[...], dout)
            dg3_ref[...] += _rowsum(dg_rows)
            dff_ref[...] = dff.astype(MXU)

    row = lambda: pl.BlockSpec((tm, D), lambda i, j: (i, 0))
    vec = lambda: pl.BlockSpec((1, D), lambda i, j: (0, 0))
    return pl.pallas_call(
        body, name="mlp_fwd", grid=(t_n // tm, n_f),
        in_specs=[row(), vec(), _slot_spec(wup, FF, MLP_TF_FWD, lambda i, j: j),
                  _slot_spec(wdown, FF, MLP_TF_FWD, lambda i, j: j), vec(), row()],
        out_specs=[pl.BlockSpec((tm, MLP_TF_FWD), lambda i, j: (i, j)), row(), row(), row(),
                   pl.BlockSpec((1, 1), lambda i, j: (0, 0)), vec()],
        out_shape=[jax.ShapeDtypeStruct((t_n, FF), MXU), jax.ShapeDtypeStruct((t_n, D), MXU),
                   jax.ShapeDtypeStruct((t_n, D), F32), jax.ShapeDtypeStruct((t_n, D), MXU),
                   jax.ShapeDtypeStruct((1, 1), F32), jax.ShapeDtypeStruct((1, D), F32)],
        scratch_shapes=[pltpu.VMEM((tm, D), F32)],
        compiler_params=_cp("arbitrary", "arbitrary"),
    )(h1, g2, wup[0], wdown[0], g3, tgt)


def _mlp_bwd(dff, hp, wup, wdown):
    t_n = dff.shape[0]
    tm = min(t_n, MLP_TM)
    n_f = FF // MLP_TF_BWD

    def mm_body(dff_ref, hp_ref, wu_ref, wd_ref, dv_ref, dhp_ref, hid_ref):
        @pl.when(pl.program_id(1) == 0)
        def _():
            dv_ref[...] = jnp.zeros_like(dv_ref)

        relu = jnp.maximum(hp_ref[...].astype(F32), 0.0)
        hid_ref[...] = jnp.square(relu).astype(MXU)
        dhid = lax.dot_general(dff_ref[...], wd_ref[...].reshape(MLP_TF_BWD, D), (((1,), (1,)), ((), ())),
                               preferred_element_type=F32)
        dhp = (dhid * (2.0 * relu)).astype(MXU)
        dhp_ref[...] = dhp
        dv_ref[...] += jnp.dot(dhp, wu_ref[...].reshape(MLP_TF_BWD, D), preferred_element_type=F32)

    row = lambda: pl.BlockSpec((tm, D), lambda i, j: (i, 0))
    blk = lambda: pl.BlockSpec((tm, MLP_TF_BWD), lambda i, j: (i, j))
    wblk = lambda w: _slot_spec(w, FF, MLP_TF_BWD, lambda i, j: j)
    return pl.pallas_call(
        mm_body, name="mlp_bwd", grid=(t_n // tm, n_f),
        in_specs=[row(), blk(), wblk(wup), wblk(wdown)], out_specs=[row(), blk(), blk()],
        out_shape=[jax.ShapeDtypeStruct((t_n, D), F32), jax.ShapeDtypeStruct((t_n, FF), MXU),
                   jax.ShapeDtypeStruct((t_n, FF), MXU)],
        compiler_params=_cp("parallel", "arbitrary"),
    )(dff, hp, wup[0], wdown[0])


def _wgrad(a, g, name):
    t_n, k_n = a.shape
    n_n = g.shape[1]
    tt = min(t_n, 1024)
    tk, tn = min(k_n, 1024), min(n_n, 1024)

    n_t = t_n // tt

    def body(a_ref, g_ref, o_ref, acc):
        t = pl.program_id(2)

        @pl.when(t == 0)
        def _():
            acc[...] = jnp.zeros_like(acc)

        acc[...] += lax.dot_general(a_ref[...], g_ref[...], (((0,), (0,)), ((), ())), preferred_element_type=F32)

        @pl.when(t == n_t - 1)
        def _():
            o_ref[...] = acc[...].astype(o_ref.dtype)

    return pl.pallas_call(
        body, name=name, grid=(k_n // tk, n_n // tn, n_t),
        in_specs=[pl.BlockSpec((tt, tk), lambda k, n, t: (t, k)), pl.BlockSpec((tt, tn), lambda k, n, t: (t, n))],
        out_specs=pl.BlockSpec((tk, tn), lambda k, n, t: (k, n)),
        out_shape=jax.ShapeDtypeStruct((k_n, n_n), MXU),
        scratch_shapes=[pltpu.VMEM((tk, tn), F32)],
        compiler_params=_cp("parallel", "parallel", "arbitrary"),
    )(a, g)


def _wgrad_segs(segs, g, name):
    t_n, n_n = g.shape
    n_s = len(segs)
    tt = min(t_n, 1024)
    n_t = t_n // tt

    def body(*refs):
        a_refs = refs[:n_s]
        g_ref, o_ref, acc = refs[n_s:]
        s_id, t = pl.program_id(0), pl.program_id(1)

        @pl.when(t == 0)
        def _():
            acc[...] = jnp.zeros_like(acc)

        for s in range(n_s):
            @pl.when(s_id == s)
            def _(s=s):
                acc[...] += lax.dot_general(a_refs[s][...], g_ref[...], (((0,), (0,)), ((), ())),
                                            preferred_element_type=F32)

        @pl.when(t == n_t - 1)
        def _():
            o_ref[...] = acc[...].astype(o_ref.dtype)

    seg_spec = lambda s: pl.BlockSpec((tt, SEG), lambda i, t: (jnp.where(i == s, t, jnp.where(i < s, 0, n_t - 1)), 0))
    return pl.pallas_call(
        body, name=name, grid=(n_s, n_t),
        in_specs=[seg_spec(s) for s in range(n_s)] + [pl.BlockSpec((tt, n_n), lambda i, t: (t, 0))],
        out_specs=pl.BlockSpec((SEG, n_n), lambda i, t: (i, 0)),
        out_shape=jax.ShapeDtypeStruct((n_s * SEG, n_n), MXU),
        scratch_shapes=[pltpu.VMEM((SEG, n_n), F32)],
        compiler_params=_cp("arbitrary", "arbitrary"),
    )(*segs, g)


def _inproj_bwd(dsegs, seg_ids, du_part, w, x, g0, dh1):
    t_n = x.shape[0]
    tm = min(t_n, 512)
    n_k = len(dsegs)

    def body(*refs):
        dp_refs = refs[:n_k]
        w_refs = refs[n_k:2 * n_k]
        part_ref, x_ref, g_ref, dh1_ref, dx_ref, dg0_ref = refs[2 * n_k:]

        @pl.when(pl.program_id(0) == 0)
        def _():
            dg0_ref[...] = jnp.zeros_like(dg0_ref)

        du = part_ref[...]
        for s in range(n_k):
            du = du + jnp.dot(dp_refs[s][...], w_refs[s][...], preferred_element_type=F32)
        dx, dg_rows = _rms_bwd(x_ref[...], g_ref[...], du)
        dg0_ref[...] += _rowsum(dg_rows)
        dx_ref[...] = dh1_ref[...] + dx

    row = lambda: pl.BlockSpec((tm, D), lambda i: (i, 0))
    return pl.pallas_call(
        body, name="inproj_bwd", grid=(t_n // tm,),
        in_specs=[row() for _ in range(n_k)]
        + [pl.BlockSpec((SEG, D), functools.partial(lambda i, s: (s, 0), s=s)) for s in seg_ids]
        + [row(), row(), _full((1, D)), row()],
        out_specs=[row(), _full((1, D))],
        out_shape=[jax.ShapeDtypeStruct((t_n, D), F32), jax.ShapeDtypeStruct((1, D), F32)],
        compiler_params=_cp("arbitrary"),
    )(*dsegs, *([w] * n_k), du_part, x, g0, dh1)


def _blockdiag4(w):
    w4 = w.reshape(4, 4, 64, 1, 64).astype(MXU)
    same = (jnp.arange(4)[:, None, None, None] == jnp.arange(4)[None, None, :, None])
    return jnp.where(same[None], w4, jnp.zeros((), MXU)).reshape(4, 256, 256)


def _blockdiag4_extract(g):
    g5 = g.reshape(4, 4, 64, 4, 64)
    return jnp.stack([g5[:, q, :, q, :] for q in range(4)], axis=1).reshape(NH, 64, 64)


def _local_step(x, tgt, p, after_ssd=None, late_weights=None, send_mlp_grads=None, send_late_grads=None):
    f = lambda a: a.astype(F32)
    proj, u = _inproj(x, p["norm_mix_pre"], p["w_in_pT"])
    ssm_params = (p["cw_xs"], p["cw_bc"], p["cb_xs"], p["cb_bc"], p["dt_bias"], p["a_log"], p["d_skip_x"],
                  p["ssm_norm"])
    ya, sprev, *ssd_saved = _ssd_fwd(proj, *ssm_params)
    cb_lru = p["conv_lru_b"] if after_ssd is None else p["conv_lru_b"] + after_ssd(ya)
    lru_params = (p["conv_lru_w"], cb_lru, p["wa_bd"], p["wx_bd"], p["lru_ba"], p["lru_bx"], p["lru_lambda"])
    yb, h, xr = _lru_fwd(proj, *lru_params)
    if late_weights is not None:
        p = dict(p, **late_weights(yb))
    p = dict(p, **{n: _slotted(p[n]) for n in ("w_out", "w_upT", "w_down")})
    h1, mix, merged = _merge_out(ya, yb, proj, x, p["w_out"], p["norm_mix_post"])
    hp, v, dout, dff, loss, dg3 = _mlp_fwd(h1, p["norm_mlp_pre"], p["w_upT"], p["w_down"], p["norm_mlp_post"], tgt)

    dv, dhp, hid = _mlp_bwd(dff, hp, p["w_upT"], p["w_down"])
    dh1, dmix, dya, dyb, dga, dgb, dg2, dg1, du_gates = _out_bwd(
        dv, h1, p["norm_mlp_pre"], dout, mix, ya, yb, proj, p["w_out"], p["norm_mix_post"], p["w_in_pT"])
    d_w_out = _wgrad(merged, dmix, "wgrad_out")
    (dz, dxs, dbc, dcwx, dcwb, dcbx, dcbb, ddtb, dalog, ddsk, dnrm, d_w_down, d_w_up_t) = _ssd_bwd(
        dya, ssd_saved, proj, sprev, p["cw_xs"], p["cw_bc"], *ssm_params[4:], (hid, dff, dhp, v))
    ba_lru = p["lru_ba"] if send_mlp_grads is None else p["lru_ba"] + send_mlp_grads(d_w_up_t, d_w_down, d_w_out)
    (dgl, dxl, dcwl, dcbl, dwa, dwx, dba, dbx, dlam) = _lru_bwd(
        dyb, proj, h, xr, p["conv_lru_w"], p["wa_bd"], p["wx_bd"], ba_lru, p["lru_bx"], p["lru_lambda"])
    dsegs = [dz, dgl, dxl, dga, dgb, dxs, dbc]
    d_w_in_pt = _wgrad_segs(dsegs, u, "wgrad_in")
    g0 = p["norm_mix_pre"]
    if send_late_grads is not None:
        g0 = g0 + send_late_grads(d_w_in_pt)
    grad_x, dg0 = _inproj_bwd([dz, dgl, dxl, dxs, dbc], (0, 1, 2, 5, 6), du_gates, p["w_in_pT"], x, g0, dh1)
    grads = dict(
        norm_mix_pre=dg0, w_in_pT=d_w_in_pt, conv_ssm_w=jnp.concatenate([dcwx, dcwb], axis=1),
        conv_ssm_b=jnp.concatenate([dcbx, dcbb], axis=1), dt_bias=ddtb[:, :NH], a_log=dalog[:, :NH],
        d_skip=f(ddsk).reshape(NH, 64).sum(axis=1)[None, :], ssm_norm=dnrm, conv_lru_w=dcwl, conv_lru_b=dcbl,
        lru_wa=_blockdiag4_extract(dwa), lru_ba=dba, lru_wx=_blockdiag4_extract(dwx), lru_bx=dbx, lru_lambda=dlam,
        w_out=d_w_out, norm_mix_post=dg1, norm_mlp_pre=dg2, w_upT=d_w_up_t, w_down=d_w_down, norm_mlp_post=dg3)
    return loss[0, 0], grad_x, grads


W_IN_COLS = 6672


def _w_in_t_to_padded(wt):
    z, xs, bc, dt = wt[0:1024], wt[1024:2048], wt[2048:2560], wt[2560:2576]
    gl, xl, ga, gb = wt[2576:3600], wt[3600:4624], wt[4624:5648], wt[5648:6672]
    return jnp.concatenate([z, gl, xl, ga, gb, xs, bc, dt, jnp.zeros((NP - 6672, wt.shape[1]), wt.dtype)], axis=0)


def _w_in_t_from_padded(wp):
    z, gl, xl, ga, gb = (wp[SEG * s:SEG * (s + 1)] for s in range(5))
    xs, bc, dt = wp[5120:6144], wp[6144:6656], wp[6656:6672]
    return jnp.concatenate([z, xs, bc, dt, gl, xl, ga, gb], axis=0)


def _prep_params(full, big):
    f = lambda a: a.astype(F32)
    pad128 = lambda a: jnp.pad(f(a).reshape(1, -1), ((0, 0), (0, 128 - a.size)))
    cw = f(full["conv_ssm_w"])
    cb = f(full["conv_ssm_b"]).reshape(1, -1)
    return dict(
        big, norm_mix_pre=f(full["norm_mix_pre"]).reshape(1, D),
        cw_xs=cw[:, :D], cw_bc=cw[:, D:], cb_xs=cb[:, :D], cb_bc=cb[:, D:],
        dt_bias=pad128(full["dt_bias"]), a_log=pad128(full["a_log"]),
        d_skip_x=jnp.repeat(f(full["d_skip"]).reshape(-1), 64).reshape(1, D), ssm_norm=f(full["ssm_norm"]).reshape(1, D),
        conv_lru_w=f(full["conv_lru_w"]), conv_lru_b=f(full["conv_lru_b"]).reshape(1, D),
        wa_bd=_blockdiag4(full["lru_wa"]), wx_bd=_blockdiag4(full["lru_wx"]),
        lru_ba=f(full["lru_ba"]).reshape(1, D), lru_bx=f(full["lru_bx"]).reshape(1, D),
        lru_lambda=f(full["lru_lambda"]).reshape(1, D),
        norm_mix_post=f(full["norm_mix_post"]).reshape(1, D), norm_mlp_pre=f(full["norm_mlp_pre"]).reshape(1, D),
        norm_mlp_post=f(full["norm_mlp_post"]).reshape(1, D))


MESH_ID = pl.DeviceIdType.MESH
ANY = pl.BlockSpec(memory_space=pl.ANY)


def _my_place():
    x, y, c = lax.axis_index("x"), lax.axis_index("y"), lax.axis_index("c")
    return x, y, c, 4 * x + 2 * y + c


def _peer(x, y, c, k):
    return (x ^ ((k >> 2) & 1), y ^ ((k >> 1) & 1), c ^ (k & 1))


def _all_gather(pack, name):
    r_n = pack.shape[0]
    half = -(-r_n // 32) * 16
    n_cp = 9

    def body(in_ref, out_ref, send_sems, recv_sems, local_sem):
        x, y, c, me = _my_place()
        here, sibling, x_nbr, y_nbr = (x, y, c), (x, y, 1 - c), (1 - x, y, c), (x, 1 - y, c)
        part = {"all": pl.ds(0, r_n), "lo": pl.ds(0, half), "hi": pl.ds(half, r_n - half)}

        def copy(j, block, rows, to, src=None):
            return pltpu.make_async_remote_copy(
                src_ref=out_ref.at[block, part[rows]] if src is None else src, dst_ref=out_ref.at[block, part[rows]],
                send_sem=send_sems.at[j], recv_sem=recv_sems.at[j], device_id=to, device_id_type=MESH_ID)

        mine = pltpu.make_async_copy(in_ref, out_ref.at[me], local_sem)
        mine.start()
        first = [copy(0, me, "all", sibling, src=in_ref), copy(1, me, "all", x_nbr, src=in_ref),
                 copy(2, me, "all", y_nbr, src=in_ref)]
        for cp in first:
            cp.start()
        relay = [[(3, me ^ 4, "lo", y_nbr), (5, me ^ 4, "all", sibling)],
                 [(4, me ^ 2, "hi", x_nbr), (6, me ^ 2, "all", sibling)],
                 [(7, me ^ 6, "lo", sibling)],
                 [(8, me ^ 6, "hi", sibling)]]
        landed = [(1, me ^ 4, "all"), (2, me ^ 2, "all"), (3, me ^ 6, "lo"), (4, me ^ 6, "hi")]
        passed = []
        for (j, block, rows), nxt in zip(landed, relay):
            copy(j, block, rows, here).wait_recv()
            for args in nxt:
                cp = copy(*args)
                cp.start()
                passed.append(cp)
        for j, block, rows in ((0, me ^ 1, "all"), (5, me ^ 5, "all"), (6, me ^ 3, "all"), (7, me ^ 7, "lo"),
                               (8, me ^ 7, "hi")):
            copy(j, block, rows, here).wait_recv()
        for cp in first + passed:
            cp.wait_send()
        mine.wait()

    return pl.pallas_call(
        body, name=name, in_specs=[ANY], out_specs=ANY,
        out_shape=jax.ShapeDtypeStruct((N_DEV,) + pack.shape, pack.dtype),
        scratch_shapes=[pltpu.SemaphoreType.DMA((n_cp,)), pltpu.SemaphoreType.DMA((n_cp,)), pltpu.SemaphoreType.DMA],
    )(pack)


HBM = pl.BlockSpec(memory_space=pltpu.HBM)
SEM = pl.BlockSpec(memory_space=pltpu.SEMAPHORE)
PLAN_GATHER = tuple((k, "pack", 0) for k in range(1, N_DEV))
PLAN_SCATTER = tuple((k, "slot", 0) for k in range(1, N_DEV))
PLAN_GATHER_ICI = tuple((k, "pack", 0) for k in (2, 4, 6))
PLAN_GATHER_D2D = ((1, "pack", 0),) + tuple((1, s, s) for s in (2, 4, 6))
PLAN_SCATTER_CHIPS = tuple((k, "chip", "chip") for k in (2, 4, 6))


def _plan_copy(j, plan, src_ref, land_ref, sems):
    k, source, slot = plan[j]
    x, y, c, me = _my_place()
    if source == "pack":
        src = src_ref
    elif source == "slot":
        src = src_ref.at[me ^ k]
    elif source == "chip":
        src = src_ref.at[(me ^ k) >> 1]
    else:
        src = land_ref.at[me ^ source]
    dst = land_ref.at[me >> 1] if slot == "chip" else land_ref.at[me ^ slot]
    return pltpu.make_async_remote_copy(
        src_ref=src, dst_ref=dst, send_sem=sems[j], recv_sem=sems[len(plan) + j],
        device_id=_peer(x, y, c, k), device_id_type=MESH_ID)


def _exchange_start(src, land, plan, name):
    n_c = len(plan)
    if land is None:
        slots = src.shape[0] if src.ndim == 3 else N_DEV
        land = pltpu.with_memory_space_constraint(lax.empty((slots,) + src.shape[-2:], src.dtype), pltpu.HBM)

    def body(src_ref, land_ref, *rest):
        sems, token = rest[:2 * n_c], rest[2 * n_c + 2]
        for j in range(n_c):
            _plan_copy(j, plan, src_ref, land_ref, sems).start()
        token[...] = jnp.zeros_like(token)

    outs = pl.pallas_call(
        body, name=name,
        out_shape=(pltpu.SemaphoreType.DMA(()),) * (2 * n_c) + (
            pltpu.HBM(src.shape, src.dtype), pltpu.HBM(land.shape, land.dtype), jax.ShapeDtypeStruct((8, 128), F32)),
        in_specs=(HBM, HBM), out_specs=(SEM,) * (2 * n_c) + (HBM, HBM, pl.BlockSpec(memory_space=pltpu.VMEM)),
        input_output_aliases={0: 2 * n_c, 1: 2 * n_c + 1},
        compiler_params=pltpu.CompilerParams(has_side_effects=pltpu.SideEffectType.DATAFLOW_SIDE_EFFECTING),
    )(pltpu.with_memory_space_constraint(src, pltpu.HBM), land)
    return outs[:2 * n_c], outs[2 * n_c], outs[2 * n_c + 1], outs[2 * n_c + 2]


def _exchange_wait(sems, src_thru, land_thru, after, plan, name):
    n_c = len(plan)

    def body(src_ref, land_ref, *rest):
        for j in range(n_c):
            cp = _plan_copy(j, plan, src_ref, land_ref, rest[:2 * n_c])
            cp.wait_send()
            cp.wait_recv()

    return pl.pallas_call(
        body, name=name,
        out_shape=(pltpu.HBM(src_thru.shape, src_thru.dtype), pltpu.HBM(land_thru.shape, land_thru.dtype)),
        in_specs=(HBM, HBM) + (SEM,) * (2 * n_c) + (pl.BlockSpec(memory_space=pl.ANY),), out_specs=(HBM, HBM),
        input_output_aliases={0: 0, 1: 1},
        compiler_params=pltpu.CompilerParams(has_side_effects=pltpu.SideEffectType.DATAFLOW_SIDE_EFFECTING),
    )(src_thru, land_thru, *sems, after)


def _pair_exchange(src, name):
    n_s = N_DEV // 2

    def body(src_ref, land_ref, send_sems, recv_sems):
        x, y, c, _ = _my_place()
        cps = [pltpu.make_async_remote_copy(
            src_ref=src_ref.at[2 * i + 1 - c], dst_ref=land_ref.at[i], send_sem=send_sems.at[i],
            recv_sem=recv_sems.at[i], device_id=(x, y, 1 - c), device_id_type=MESH_ID) for i in range(n_s)]
        for cp in cps:
            cp.start()
        for cp in cps:
            cp.wait_recv()
        for cp in cps:
            cp.wait_send()

    return pl.pallas_call(
        body, name=name, in_specs=[ANY], out_specs=ANY,
        out_shape=jax.ShapeDtypeStruct((n_s,) + src.shape[1:], src.dtype),
        scratch_shapes=[pltpu.SemaphoreType.DMA((n_s,)), pltpu.SemaphoreType.DMA((n_s,))],
    )(src)


def _pair_sum(src, land, name):
    n_s, r_n, c_n = land.shape
    tr = max(t for t in range(16, 513, 16) if r_n % t == 0)

    def body(c_ref, a_ref, b_ref, o_ref):
        o_ref[...] = (a_ref[...].astype(F32) + b_ref[...].astype(F32)).astype(o_ref.dtype)

    blk = lambda: pl.BlockSpec((1, tr, c_n), lambda i, j, c_ref: (i, j, 0))
    return pl.pallas_call(
        body, name=name,
        grid_spec=pltpu.PrefetchScalarGridSpec(
            num_scalar_prefetch=1, grid=(n_s, r_n // tr),
            in_specs=[pl.BlockSpec((1, tr, c_n), lambda i, j, c_ref: (2 * i + c_ref[0], j, 0)), blk()],
            out_specs=blk()),
        out_shape=jax.ShapeDtypeStruct(land.shape, land.dtype),
        compiler_params=_cp("parallel", "parallel"),
    )(lax.axis_index("c").astype(jnp.int32).reshape(1), src, land)


def _slot_sum(parts, name):
    n_s, r_n, c_n = parts.shape
    tr = max(t for t in range(16, 513, 16) if r_n % t == 0)

    def body(p_ref, o_ref):
        acc = p_ref[0].astype(F32)
        for k in range(1, n_s):
            acc = acc + p_ref[k].astype(F32)
        o_ref[...] = acc

    return pl.pallas_call(
        body, name=name, grid=(r_n // tr,),
        in_specs=[pl.BlockSpec((n_s, tr, c_n), lambda i: (0, i, 0))],
        out_specs=pl.BlockSpec((tr, c_n), lambda i: (i, 0)),
        out_shape=jax.ShapeDtypeStruct((r_n, c_n), F32),
        compiler_params=_cp("parallel"),
    )(parts)


def _adam_math(w, g, m, v):
    m = ADAM_B1 * m + (1.0 - ADAM_B1) * g
    v = ADAM_B2 * v + (1.0 - ADAM_B2) * jnp.square(g)
    m_hat = m / (1.0 - ADAM_B1 ** ADAM_STEP)
    v_hat = v / (1.0 - ADAM_B2 ** ADAM_STEP)
    return -ADAM_LR * (m_hat / (jnp.sqrt(v_hat) + ADAM_EPS) + ADAM_WD * w), m, v


def _adam_big(w, g, m, v, name):
    def body(w_ref, g_ref, m_ref, v_ref, d_ref, mo_ref, vo_ref):
        d_ref[...], mo_ref[...], vo_ref[...] = _adam_math(w_ref[...], g_ref[...], m_ref[...], v_ref[...])

    if w.ndim == 3:
        _, r_n, c_n = w.shape
        tr = min(r_n, 256)
        grid = (r_n // tr,)
        blk = lambda: pl.BlockSpec((1, tr, c_n), lambda i: (0, i, 0))
    else:
        r_n, c_n = w.shape
        tc = min(c_n, 256)
        grid = (c_n // tc,)
        blk = lambda: pl.BlockSpec((r_n, tc), lambda i: (0, i))
    return pl.pallas_call(
        body, name=name, grid=grid, in_specs=[blk(), blk(), blk(), blk()], out_specs=[blk(), blk(), blk()],
        out_shape=[jax.ShapeDtypeStruct(w.shape, F32)] * 3, compiler_params=_cp("parallel"),
    )(w, g, m, v)


def _adam_small(groups, where, wmv, total):
    n, n_g = len(wmv), len(groups)
    t_g, t_r = total

    def body(*refs):
        g_refs = refs[:n_g]
        w_refs = refs[n_g:n_g + 3 * n]
        o_refs = refs[n_g + 3 * n:]
        for q in range(n):
            w_ref, m_ref, v_ref = w_refs[3 * q:3 * q + 3]
            r, c = w_ref.shape
            gi, r0 = where[q]
            g = g_refs[gi][r0:r0 + r, 0:c]
            d, m, v = _adam_math(w_ref[...], g, m_ref[...], v_ref[...])
            o_refs[4 * q][...] = g
            o_refs[4 * q + 1][...] = d
            o_refs[4 * q + 2][...] = m
            o_refs[4 * q + 3][...] = v
        o_refs[4 * n][...] = g_refs[t_g][t_r:t_r + 1, :]

    flat_wmv = [a for t in wmv for a in t]
    vm = pl.BlockSpec(memory_space=pltpu.VMEM)
    outs = pl.pallas_call(
        body, name="adam_small", in_specs=[vm] * (n_g + 3 * n), out_specs=[vm] * (4 * n + 1),
        out_shape=[jax.ShapeDtypeStruct(t[0].shape, F32) for t in wmv for _ in range(4)]
        + [jax.ShapeDtypeStruct((1, groups[t_g].shape[1]), F32)],
        compiler_params=pltpu.CompilerParams(vmem_limit_bytes=VMEM_LIMIT),
    )(*groups, *flat_wmv)
    return [tuple(outs[4 * q:4 * q + 4]) for q in range(n)], outs[4 * n]


WEIGHTS = ["norm_mix_pre", "w_in", "conv_ssm_w", "conv_ssm_b", "dt_bias", "a_log", "d_skip", "ssm_norm", "conv_lru_w",
           "conv_lru_b", "lru_wa", "lru_ba", "lru_wx", "lru_bx", "lru_lambda", "w_out", "norm_mix_post", "norm_mlp_pre",
           "w_up", "w_down", "norm_mlp_post"]
BIG = ["w_out", "w_up", "w_down", "w_in"]
IN_ROWS = W_IN_COLS // N_DEV
IN_PAD, EARLY_ROWS = 848, 880
LATE_ROWS = 1152
GRAD_LATE_ROWS = 864
CONV_SSM_COLS, CONV_LRU_COLS = 1536 // N_DEV, D // N_DEV
SMALL = [("norm_mix_pre", (1, D), 0, 0), ("ssm_norm", (1, D), 0, 1), ("conv_lru_b", (1, D), 0, 2),
         ("lru_lambda", (1, D), 0, 3), ("norm_mix_post", (1, D), 0, 4), ("norm_mlp_pre", (1, D), 0, 5),
         ("norm_mlp_post", (1, D), 0, 6), ("conv_ssm_b", (1, 1536), 1, 0), ("dt_bias", (1, NH), 2, 0),
         ("a_log", (1, NH), 2, 1), ("d_skip", (1, NH), 2, 2), ("conv_ssm_w", (4, CONV_SSM_COLS), 3, 0),
         ("conv_lru_w", (4, CONV_LRU_COLS), 4, 0), ("lru_wa", (D, 64), 5, 0), ("lru_wx", (D, 64), 5, D),
         ("lru_ba", (NH, 64), 6, 0), ("lru_bx", (NH, 64), 6, NH)]
SMALL_GROUPS = [(8, D), (1, 1536), (8, 128), (4, 1536), (4, D), (2 * D, 64), (2 * NH, 64)]


def _pad_rows(flat, mult):
    n = flat.shape[0]
    rows = -(-n // (128 * mult)) * mult
    return jnp.pad(flat, (0, rows * 128 - n)).reshape(rows, 128)


def _split3(a):
    hi = a.astype(MXU)
    r1 = a - hi.astype(F32)
    mid = r1.astype(MXU)
    lo = (r1 - mid.astype(F32)).astype(MXU)
    return jnp.stack([hi, mid, lo])


def _early_pack(a, me):
    bf = lambda t: t.astype(MXU)
    conv = lambda t, c: jnp.pad(_split3(t).reshape(12, c), ((0, 4), (0, D - c)))
    shifted = lax.dynamic_update_slice(jnp.zeros((IN_PAD, D), MXU), bf(a["w_in"][0]).T, (2 * me, 0))
    return jnp.concatenate([shifted, conv(a["conv_ssm_w"][0], CONV_SSM_COLS), conv(a["conv_lru_w"][0], CONV_LRU_COLS)],
                           axis=0)


TILE = 16
SHARD_TILES = IN_PAD // TILE
SHARD_STEP = (IN_ROWS // TILE)
SEG_TILES = ((0, 64, 0), (64, 128, 320), (128, 160, 384), (160, 161, 416), (161, 225, 64), (225, 289, 128),
             (289, 353, 192), (353, 417, 256))


def _tile_runs(lo, hi):
    runs = []
    for s0, s1, d0 in SEG_TILES:
        a, b = max(lo, s0), min(hi, s1)
        if a < b:
            runs.append((a, b - a, d0 + a - s0))
    return runs


def _assemble_w_in(g):
    whole = []
    for k in range(N_DEV):
        lo = SHARD_STEP * k + (1 if k else 0)
        hi = SHARD_STEP * (k + 1) + (1 if k == N_DEV - 1 else 0)
        whole += [(k, a - SHARD_STEP * k, n, d) for a, n, d in _tile_runs(lo, hi)]
    split = [(k, _tile_runs(SHARD_STEP * k, SHARD_STEP * k + 1)[0][2]) for k in range(1, N_DEV)]

    def body(g_ref, o_ref):
        rows = lambda t, n=1: pl.ds(TILE * t, TILE * n)
        for k, t, n, d in whole:
            o_ref[rows(d, n), :] = g_ref[k, rows(t, n), :]
        for k, d in split:
            o_ref[rows(d), :] = g_ref[k - 1, rows(SHARD_STEP), :] + g_ref[k, rows(0), :]
        o_ref[pl.ds(W_IN_COLS, NP - W_IN_COLS), :] = jnp.zeros((NP - W_IN_COLS, D), o_ref.dtype)

    vm = pl.BlockSpec(memory_space=pltpu.VMEM)
    return pl.pallas_call(
        body, name="assemble_w_in", in_specs=[vm], out_specs=vm, out_shape=jax.ShapeDtypeStruct((NP, D), g.dtype),
        compiler_params=pltpu.CompilerParams(vmem_limit_bytes=VMEM_LIMIT),
    )(g)


def _scatter_w_in_grad(dw):
    runs = [(k, a - SHARD_STEP * k, n, d) for k in range(N_DEV)
            for a, n, d in _tile_runs(SHARD_STEP * k, SHARD_STEP * k + SHARD_TILES)]
    pad = GRAD_LATE_ROWS - IN_PAD

    def body(dw_ref, o_ref):
        rows = lambda t, n: pl.ds(TILE * t, TILE * n)
        for k, t, n, d in runs:
            o_ref[k, rows(t, n), :] = dw_ref[rows(d, n), :]
        for k in range(N_DEV):
            o_ref[k, pl.ds(IN_PAD, pad), :] = jnp.zeros((pad, D), o_ref.dtype)

    vm = pl.BlockSpec(memory_space=pltpu.VMEM)
    return pl.pallas_call(
        body, name="scatter_w_in_grad", in_specs=[vm], out_specs=vm,
        out_shape=jax.ShapeDtypeStruct((N_DEV, GRAD_LATE_ROWS, D), dw.dtype),
        compiler_params=pltpu.CompilerParams(vmem_limit_bytes=VMEM_LIMIT),
    )(dw)


def _early_unpack(g):
    w_in_pt = _assemble_w_in(g)
    conv = {}
    for n, r0, c in (("conv_ssm_w", IN_PAD, CONV_SSM_COLS), ("conv_lru_w", IN_PAD + 16, CONV_LRU_COLS)):
        s = g[:, r0:r0 + 12, :c].astype(F32).reshape(N_DEV, 3, 4, c)
        conv[n] = ((s[:, 0] + s[:, 1]) + s[:, 2]).transpose(1, 0, 2).reshape(4, N_DEV * c)
    return w_in_pt, conv


def _late_pack(a):
    bf = lambda t: t.astype(MXU)
    return jnp.concatenate([bf(a["w_up"][0]).T, bf(a["w_down"][0]), bf(a["w_out"][0])], axis=0)


def _late_unpack(g):
    return dict(w_upT=(g, 0), w_down=(g, 1), w_out=(g, 2 * FF // D))


def _own_slot(land, own):
    me = 4 * lax.axis_index("x") + 2 * lax.axis_index("y") + lax.axis_index("c")
    return lax.dynamic_update_slice_in_dim(land, own[None], me, axis=0)


def kernel(x, norm_mix_pre, w_in, conv_ssm_w, conv_ssm_b, dt_bias, a_log, d_skip, ssm_norm, conv_lru_w, conv_lru_b, lru_wa, lru_ba, lru_wx, lru_bx, lru_lambda, w_out, norm_mix_post, norm_mlp_pre, w_up, w_down, norm_mlp_post, loss_target, m_norm_mix_pre, m_w_in, m_conv_ssm_w, m_conv_ssm_b, m_dt_bias, m_a_log, m_d_skip, m_ssm_norm, m_conv_lru_w, m_conv_lru_b, m_lru_wa, m_lru_ba, m_lru_wx, m_lru_bx, m_lru_lambda, m_w_out, m_norm_mix_post, m_norm_mlp_pre, m_w_up, m_w_down, m_norm_mlp_post, v_norm_mix_pre, v_w_in, v_conv_ssm_w, v_conv_ssm_b, v_dt_bias, v_a_log, v_d_skip, v_ssm_norm, v_conv_lru_w, v_conv_lru_b, v_lru_wa, v_lru_ba, v_lru_wx, v_lru_bx, v_lru_lambda, v_w_out, v_norm_mix_post, v_norm_mlp_pre, v_w_up, v_w_down, v_norm_mlp_post):
    vals = (norm_mix_pre, w_in, conv_ssm_w, conv_ssm_b, dt_bias, a_log, d_skip, ssm_norm, conv_lru_w, conv_lru_b, lru_wa, lru_ba, lru_wx, lru_bx, lru_lambda, w_out, norm_mix_post, norm_mlp_pre, w_up, w_down, norm_mlp_post)
    m_vals = (m_norm_mix_pre, m_w_in, m_conv_ssm_w, m_conv_ssm_b, m_dt_bias, m_a_log, m_d_skip, m_ssm_norm, m_conv_lru_w, m_conv_lru_b, m_lru_wa, m_lru_ba, m_lru_wx, m_lru_bx, m_lru_lambda, m_w_out, m_norm_mix_post, m_norm_mlp_pre, m_w_up, m_w_down, m_norm_mlp_post)
    v_vals = (v_norm_mix_pre, v_w_in, v_conv_ssm_w, v_conv_ssm_b, v_dt_bias, v_a_log, v_d_skip, v_ssm_norm, v_conv_lru_w, v_conv_lru_b, v_lru_wa, v_lru_ba, v_lru_wx, v_lru_bx, v_lru_lambda, v_w_out, v_norm_mix_post, v_norm_mlp_pre, v_w_up, v_w_down, v_norm_mlp_post)
    w = dict(zip(WEIGHTS, vals))
    m = dict(zip(WEIGHTS, m_vals))
    v = dict(zip(WEIGHTS, v_vals))
    me = 4 * lax.axis_index("x") + 2 * lax.axis_index("y") + lax.axis_index("c")

    bf = lambda t: t.astype(MXU)
    late = _late_pack(w)
    early = _all_gather(_early_pack(w, me), "early_weights_all_gather")
    late, early = lax.optimization_barrier((late, early))
    lw = {}
    lw["sems"], lw["src"], lw["land"], token = _exchange_start(late, None, PLAN_GATHER_ICI, "late_weights_ici_start")
    w_in_pt, conv_w = _early_unpack(early)
    full = {n: (conv_w[n] if n in conv_w else w[n][0]) for n in WEIGHTS if n not in BIG}
    full["norm_mix_pre"] = full["norm_mix_pre"] + token[0, 0]

    def after_ssd(after):
        src, land = _exchange_wait(lw["sems"], lw["src"], lw["land"], after, PLAN_GATHER_ICI, "late_weights_ici_wait")
        lw["sems"], lw["src"], lw["land"], tok = _exchange_start(src, land, PLAN_GATHER_D2D, "late_weights_d2d_start")
        return tok[0, 0]

    def late_weights(after):
        src, land = _exchange_wait(lw["sems"], lw["src"], lw["land"], after, PLAN_GATHER_D2D, "late_weights_d2d_wait")
        return _late_unpack(_own_slot(land, src))

    sent = {}

    def send_mlp_grads(d_w_up_t, d_w_down, d_w_out):
        src = jnp.concatenate([bf(d_w_up_t).reshape(N_DEV, -1, D), bf(d_w_down).reshape(N_DEV, -1, D),
                               bf(d_w_out).reshape(N_DEV, -1, D)], axis=1)
        sent["sems"], sent["src"], sent["land"], tok = _exchange_start(src, None, PLAN_SCATTER, "mlp_grads_start")
        return tok[0, 0]

    def send_late_grads(d_w_in_pt):
        src = _scatter_w_in_grad(d_w_in_pt)
        chip = _pair_sum(src, _pair_exchange(src, "late_grads_pair_exchange"), "late_grads_pair_sum")
        sent["sems2"], sent["src2"], sent["land2"], tok = _exchange_start(chip, None, PLAN_SCATTER_CHIPS,
                                                                          "late_grads_start")
        return tok[0, 0]

    loss, grad_x, g = _local_step(x[0], loss_target[0], _prep_params(full, dict(w_in_pT=w_in_pt)), after_ssd,
                                  late_weights, send_mlp_grads, send_late_grads)

    own = lambda src: lax.dynamic_index_in_dim(src, me, keepdims=False)
    out_g, out_d, out_m, out_v = {}, {}, {}, {}
    mlp_src, mlp_land = _exchange_wait(sent["sems"], sent["src"], sent["land"], grad_x, PLAN_SCATTER, "mlp_grads_wait")
    g_mlp = _slot_sum(_own_slot(mlp_land, own(mlp_src)), "slot_sum_mlp")
    fs = FF // N_DEV
    for n, gn in (("w_up", g_mlp[:fs].T[None]), ("w_down", g_mlp[fs:2 * fs][None]), ("w_out", g_mlp[2 * fs:][None])):
        out_g[n] = gn
        out_d[n], out_m[n], out_v[n] = _adam_big(w[n], gn, m[n], v[n], "adam_" + n)
    zrow = jnp.zeros((1, D), F32)
    pad16 = lambda a: jnp.pad(a, ((0, 0), (0, 128 - NH)))
    small_parts = [
        jnp.concatenate([g["norm_mix_pre"], g["ssm_norm"], g["conv_lru_b"], g["lru_lambda"], g["norm_mix_post"],
                         g["norm_mlp_pre"], g["norm_mlp_post"], zrow], axis=0),
        g["conv_ssm_b"],
        jnp.concatenate([pad16(g["dt_bias"]), pad16(g["a_log"]), pad16(g["d_skip"]), jnp.full((1, 128), loss, F32),
                         jnp.zeros((4, 128), F32)], axis=0),
        g["conv_ssm_w"], g["conv_lru_w"],
        jnp.concatenate([g["lru_wa"].reshape(D, 64), g["lru_wx"].reshape(D, 64)], axis=0),
        jnp.concatenate([g["lru_ba"].reshape(NH, 64), g["lru_bx"].reshape(NH, 64)], axis=0)]
    small = _pad_rows(jnp.concatenate([s.reshape(-1) for s in small_parts]), 64)
    small, _ = lax.optimization_barrier((small, out_v["w_out"]))
    small_all = _all_gather(small, "small_grads_all_gather")

    lg_src, lg_land = _exchange_wait(sent["sems2"], sent["src2"], sent["land2"], small_all, PLAN_SCATTER_CHIPS,
                                     "late_grads_wait")
    my_chip = me >> 1
    lg_own = lax.dynamic_index_in_dim(lg_src, my_chip, keepdims=True)
    g_late = _slot_sum(lax.dynamic_update_slice_in_dim(lg_land, lg_own, my_chip, axis=0), "slot_sum_late")
    gt = lax.dynamic_slice(g_late, (2 * me, 0), (IN_ROWS, D))
    dt_, mt_, vt_ = _adam_big(w["w_in"][0].T, gt, m["w_in"][0].T, v["w_in"][0].T, "adam_w_in")
    out_g["w_in"], out_d["w_in"], out_m["w_in"], out_v["w_in"] = gt.T[None], dt_.T[None], mt_.T[None], vt_.T[None]
    sflat = _slot_sum(small_all, "slot_sum_small").reshape(-1)
    groups = []
    off = 0
    for r, c in SMALL_GROUPS:
        groups.append(sflat[off:off + r * c].reshape(r, c))
        off += r * c
    groups[3] = lax.dynamic_slice_in_dim(groups[3], me * CONV_SSM_COLS, CONV_SSM_COLS, axis=1)
    groups[4] = lax.dynamic_slice_in_dim(groups[4], me * CONV_LRU_COLS, CONV_LRU_COLS, axis=1)
    wmv = [(w[n].reshape(s), m[n].reshape(s), v[n].reshape(s)) for n, s, _, _ in SMALL]
    res, loss_row = _adam_small(groups, [(gi, r0) for _, _, gi, r0 in SMALL], wmv, (2, 3))
    loss = loss_row[0, 0]
    for (n, _, _, _), (g_n, d_n, m_n, v_n) in zip(SMALL, res):
        shape = w[n].shape
        out_g[n], out_d[n], out_m[n], out_v[n] = (g_n.reshape(shape), d_n.reshape(shape), m_n.reshape(shape),
                                                  v_n.reshape(shape))
    return (loss, grad_x[None], *[out_g[n] for n in WEIGHTS], *[out_d[n] for n in WEIGHTS],
            *[out_m[n] for n in WEIGHTS], *[out_v[n] for n in WEIGHTS])
```

```python
import functools

import jax
import jax.numpy as jnp
from jax import lax
from jax.experimental import pallas as pl
from jax.experimental.pallas import tpu as pltpu

F32 = jnp.float32
MXU = jnp.bfloat16
HI = lax.Precision.HIGHEST
EPS = 1e-6

D = 1024
NH = 16
NS = 128
CH = 128
FF = 4096
NP = 7168
SEG = 1024
LRU_C = 8.0
N_DEV = 8

ADAM_LR, ADAM_B1, ADAM_B2, ADAM_EPS, ADAM_WD, ADAM_STEP = 0.001, 0.9, 0.999, 1e-08, 0.01, 10

VMEM_LIMIT = 56 * 1024 * 1024


def _cp(*sem):
    return pltpu.CompilerParams(dimension_semantics=sem, vmem_limit_bytes=VMEM_LIMIT)


def _nn(a, b):
    return jnp.dot(a.astype(MXU), b.astype(MXU), preferred_element_type=F32)


def _nt(a, b):
    return lax.dot_general(a.astype(MXU), b.astype(MXU), (((1,), (1,)), ((), ())), preferred_element_type=F32)


def _tn(a, b):
    return lax.dot_general(a.astype(MXU), b.astype(MXU), (((0,), (0,)), ((), ())), preferred_element_type=F32)


_sigmoid = jax.nn.sigmoid


def _silu(x):
    return x * _sigmoid(x)


def _dsilu(x):
    s = _sigmoid(x)
    return s + x * s * (1.0 - s)


def _softplus(x):
    return jnp.maximum(x, 0.0) + jnp.log(1.0 + jnp.exp(-jnp.abs(x)))


def _rms(x, g):
    r = lax.rsqrt(jnp.mean(x * x, axis=-1, keepdims=True) + EPS)
    return x * r * g


def _rms_bwd(x, g, dy):
    r = lax.rsqrt(jnp.mean(x * x, axis=-1, keepdims=True) + EPS)
    gdy = g * dy
    dx = r * gdy - x * (r * r * r) * jnp.mean(x * gdy, axis=-1, keepdims=True)
    return dx, dy * x * r


def _rowsum(x):
    return jnp.sum(x, axis=0, keepdims=True)


def _taps_past(cur, prev8):
    r_n, c_n = cur.shape
    row = lax.broadcasted_iota(jnp.int32, (r_n, c_n), 0)
    out = []
    for k in range(4):
        s = 3 - k
        if s == 0:
            out.append(cur)
            continue
        head = jnp.concatenate([pltpu.roll(prev8, s, 0), jnp.zeros((r_n - 8, c_n), F32)], axis=0)
        out.append(jnp.where(row < s, head, pltpu.roll(cur, s, 0)))
    return out


def _taps_future(cur, fut8):
    r_n, c_n = cur.shape
    row = lax.broadcasted_iota(jnp.int32, (r_n, c_n), 0)
    out = []
    for k in range(4):
        s = 3 - k
        if s == 0:
            out.append(cur)
            continue
        tail = jnp.concatenate([jnp.zeros((r_n - 8, c_n), F32), pltpu.roll(fut8, 8 - s, 0)], axis=0)
        out.append(jnp.where(row >= r_n - s, tail, pltpu.roll(cur, r_n - s, 0)))
    return out


def _conv_apply(taps, w, b):
    acc = taps[0] * w[0:1, :]
    for k in range(1, 4):
        acc = acc + taps[k] * w[k:k + 1, :]
    return acc + b


def _inproj(x, g0, w):
    t_n = x.shape[0]
    tm = min(t_n, 1024)

    n_j = NP // SEG

    def body(x_ref, g_ref, w_ref, pb_ref, p6_ref, u_ref):
        j = pl.program_id(1)

        @pl.when(j == 0)
        def _():
            u_ref[...] = _rms(x_ref[...], g_ref[...]).astype(MXU)

        p = lax.dot_general(u_ref[...], w_ref[...], (((1,), (1,)), ((), ())), preferred_element_type=F32)

        @pl.when(j < n_j - 1)
        def _():
            pb_ref[...] = p.astype(MXU)

        @pl.when(j == n_j - 1)
        def _():
            p6_ref[...] = p

    pb, p6, u = pl.pallas_call(
        body, name="inproj", grid=(t_n // tm, n_j),
        in_specs=[pl.BlockSpec((tm, D), lambda i, j: (i, 0)), pl.BlockSpec((1, D), lambda i, j: (0, 0)),
                  pl.BlockSpec((SEG, D), lambda i, j: (j, 0))],
        out_specs=[pl.BlockSpec((tm, SEG), lambda i, j: (i, jnp.minimum(j, n_j - 2))),
                   pl.BlockSpec((tm, SEG), lambda i, j: (i, 0)), pl.BlockSpec((tm, D), lambda i, j: (i, 0))],
        out_shape=[jax.ShapeDtypeStruct((t_n, NP - SEG), MXU), jax.ShapeDtypeStruct((t_n, SEG), F32),
                   jax.ShapeDtypeStruct((t_n, D), MXU)],
        compiler_params=_cp("parallel", "arbitrary"),
    )(x, g0, w)
    return (pb, p6), u


def _ssd_prep(dtraw, dtb, alog):
    l_n = dtraw.shape[0]
    r = lax.broadcasted_iota(jnp.int32, (l_n, l_n), 0)
    c = lax.broadcasted_iota(jnp.int32, (l_n, l_n), 1)
    tril = (r >= c).astype(F32)
    triu = (r <= c).astype(F32)
    eye = (r == c).astype(F32)
    dt = _softplus(dtraw + dtb)
    adt = dt * (-jnp.exp(alog))
    ac = jnp.dot(tril, adt, preferred_element_type=F32, precision=HI)
    tn = (((0,), (0,)), ((), ()))
    ac_t = lax.dot_general(adt, triu, tn, preferred_element_type=F32, precision=HI)
    dt_t = lax.dot_general(dt, eye, tn, preferred_element_type=F32, precision=HI)
    return dt, dt_t, ac, ac_t, _rowsum(adt)


def _ssd_pair(j, xp, bg, cg, sp, dt, dt_t, ac, ac_t, aend):
    l_n = xp.shape[0]
    lane = lax.broadcasted_iota(jnp.int32, (l_n, 128), 1)
    sub = lax.broadcasted_iota(jnp.int32, (128, l_n), 0)
    lane1 = lax.broadcasted_iota(jnp.int32, (1, 128), 1)
    tri = lax.broadcasted_iota(jnp.int32, (l_n, l_n), 0) >= lax.broadcasted_iota(jnp.int32, (l_n, l_n), 1)
    lo = lax.broadcasted_iota(jnp.int32, (l_n, 128), 1) < 64
    lo_s = lax.broadcasted_iota(jnp.int32, (128, 128), 1) < 64
    cb = _nt(cg, bg)
    cs = _nn(cg, sp)
    x2 = jnp.concatenate([jnp.where(lo, xp, 0.0), jnp.where(lo, 0.0, xp)], axis=0)
    ws_, bs_, eo, ee = [], [], [], []
    for e in range(2):
        h = 2 * j + e
        ac_l = jnp.sum(jnp.where(lane == h, ac, 0.0), axis=1, keepdims=True)
        dt_l = jnp.sum(jnp.where(lane == h, dt, 0.0), axis=1, keepdims=True)
        a_end = jnp.sum(jnp.where(lane1 == h, aend, 0.0), axis=1, keepdims=True)
        ac_s, dt_s = ac_t[h:h + 1, :], dt_t[h:h + 1, :]
        decay = jnp.exp(jnp.where(tri, ac_l - ac_s, -1e30))
        ws_.append(cb * decay * dt_s)
        bs_.append(bg * (jnp.exp(a_end - ac_l) * dt_l))
        eo.append(jnp.exp(ac_l))
        ee.append(jnp.exp(a_end))
    y = _nn(jnp.concatenate(ws_, axis=1), x2) + jnp.where(lo, eo[0], eo[1]) * cs
    s_new = _tn(jnp.concatenate(bs_, axis=0), x2) + jnp.where(lo_s, ee[0], ee[1]) * sp
    return y, s_new


def _ssd_post(y, xs, z, dsk, nrm):
    y = (y + dsk * xs) * _silu(z)
    half = D // 2
    ya, yb = y[:, :half], y[:, half:]
    ya = ya * lax.rsqrt(jnp.mean(ya * ya, axis=-1, keepdims=True) + EPS)
    yb = yb * lax.rsqrt(jnp.mean(yb * yb, axis=-1, keepdims=True) + EPS)
    return jnp.concatenate([ya, yb], axis=1) * nrm


LAST_SEG = NP // SEG - 1


def _proj_ops(proj, seg_ids):
    return [proj[1] if s == LAST_SEG else proj[0] for s in seg_ids]


def _proj_specs(rows, seg_ids, order):
    return [pl.BlockSpec((rows, SEG), functools.partial(lambda i, c: (order(i), c), c=0 if s == LAST_SEG else s))
            for s in seg_ids]


def _prev_specs(rows, seg_ids, order):
    specs = []
    for s in seg_ids:
        n, c = (8, 0) if s == LAST_SEG else (16, s)
        specs.append(pl.BlockSpec((n, SEG), functools.partial(
            lambda i, n, c: (jnp.maximum(order(i) * (rows // n) - 1, 0), c), n=n, c=c)))
    return specs


def _prev8(ref):
    return ref[...] if ref.shape[0] == 8 else ref[8:16, :].astype(F32)


def _full(shape):
    return pl.BlockSpec(shape, lambda i: (0,) * len(shape))


def _slotted(a):
    return a if isinstance(a, tuple) else (a.reshape(N_DEV, a.shape[0] // N_DEV, a.shape[1]), 0)


def _slot_spec(w, rows, tile, index):
    per = rows // N_DEV
    b = w[1]
    return pl.BlockSpec((tile // per, per, D), lambda *ids: (index(*ids), b, 0))


def _ssd_fwd(proj, cwx, cwb, cbx, cbb, dtb, alog, dsk, nrm):
    t_n = proj[0].shape[0]
    n_c = t_n // CH
    fwd = lambda i: i

    def body(z_ref, xs_ref, bc_ref, xsp_ref, bcp_ref, cwx_ref, cwb_ref, cbx_ref, cbb_ref, dtb_ref, alog_ref,
             dsk_ref, nrm_ref, ya_ref, sprev_ref, yraw_ref, xspre_ref, bcpre_ref, s_ref):
        c = pl.program_id(0)

        @pl.when(c == 0)
        def _():
            s_ref[...] = jnp.zeros_like(s_ref)

        keep = jnp.where(c == 0, 0.0, 1.0)
        xs_pre = _conv_apply(_taps_past(xs_ref[...].astype(F32), _prev8(xsp_ref) * keep), cwx_ref[...], cbx_ref[...])
        bc_pre = _conv_apply(_taps_past(bc_ref[:, :512], bcp_ref[:, :512] * keep), cwb_ref[...], cbb_ref[...])
        xspre_ref[...] = xs_pre
        bcpre_ref[...] = bc_pre
        prep = _ssd_prep(bc_ref[:, 512:640], dtb_ref[...], alog_ref[...])
        xs = _silu(xs_pre)
        bc = _silu(bc_pre)
        sprev_ref[0] = s_ref[...]
        ys = []
        for j in range(NH // 2):
            g = j // 4
            yp, sn = _ssd_pair(j, xs[:, 128 * j:128 * j + 128], bc[:, 128 * g:128 * g + 128],
                               bc[:, 256 + 128 * g:384 + 128 * g], s_ref[:, 128 * j:128 * j + 128], *prep)
            ys.append(yp)
            s_ref[:, 128 * j:128 * j + 128] = sn
        y = jnp.concatenate(ys, axis=1)
        yraw_ref[...] = y
        ya_ref[...] = _ssd_post(y, xs, z_ref[...].astype(F32), dsk_ref[...], nrm_ref[...]).astype(ya_ref.dtype)

    return pl.pallas_call(
        body, name="ssd_fwd", grid=(n_c,),
        in_specs=_proj_specs(CH, (0, 5, 6), fwd) + _prev_specs(CH, (5, 6), fwd) + [
            _full((4, D)), _full((4, 512)), _full((1, D)), _full((1, 512)), _full((1, 128)), _full((1, 128)),
            _full((1, D)), _full((1, D))],
        out_specs=[pl.BlockSpec((CH, D), lambda i: (i, 0)), pl.BlockSpec((1, NS, D), lambda i: (i, 0, 0)),
                   pl.BlockSpec((CH, D), lambda i: (i, 0)), pl.BlockSpec((CH, D), lambda i: (i, 0)),
                   pl.BlockSpec((CH, 512), lambda i: (i, 0))],
        out_shape=[jax.ShapeDtypeStruct((t_n, D), MXU), jax.ShapeDtypeStruct((n_c, NS, D), F32),
                   jax.ShapeDtypeStruct((t_n, D), F32), jax.ShapeDtypeStruct((t_n, D), F32),
                   jax.ShapeDtypeStruct((t_n, 512), F32)],
        scratch_shapes=[pltpu.VMEM((NS, D), F32)],
        compiler_params=_cp("arbitrary"),
    )(*_proj_ops(proj, (0, 5, 6, 5, 6)), cwx, cwb, cbx, cbb, dtb, alog, dsk, nrm)


def _ssd_bwd(dya, saved, proj, sprev, cwx, cwb, dtb, alog, dsk, nrm, mlp_ops):
    t_n = proj[0].shape[0]
    n_c = t_n // CH
    rev = lambda i: n_c - 1 - i
    fb = FF // n_c

    def body(dya_ref, yraw_ref, xspre_ref, bcpre_ref, z_ref, xs_ref, bc_ref, sprev_ref, cwx_ref, cwb_ref,
             dtb_ref, alog_ref, dsk_ref, nrm_ref, hid_ref, dff_ref, dhp_ref, v_ref,
             dz_ref, dxs_ref, dbc_ref, dcwx_ref, dcwb_ref, dcbx_ref, dcbb_ref, ddtb_ref, dalog_ref, ddsk_ref,
             dnrm_ref, dwd_ref, dwu_ref, ds_ref, futx_ref, futb_ref):
        i = pl.program_id(0)
        acc_refs = (dcwx_ref, dcwb_ref, dcbx_ref, dcbb_ref, ddtb_ref, dalog_ref, ddsk_ref, dnrm_ref)
        tn = (((0,), (0,)), ((), ()))
        kt = t_n // (NH // 2)
        dwd_acc = jnp.zeros((fb, D), F32)
        dwu_acc = jnp.zeros((fb, D), F32)

        @pl.when(i == 0)
        def _():
            for r in (ds_ref, futx_ref, futb_ref) + acc_refs:
                r[...] = jnp.zeros_like(r)

        xs_pre = xspre_ref[...]
        bc_pre = bcpre_ref[...]
        xs = _silu(xs_pre)
        bc = _silu(bc_pre)
        prep, prep_vjp = jax.vjp(_ssd_prep, bc_ref[:, 512:640], dtb_ref[...], alog_ref[...])
        s_in = sprev_ref[0]

        def pair_args(j):
            g = j // 4
            return (xs[:, 128 * j:128 * j + 128], bc[:, 128 * g:128 * g + 128],
                    bc[:, 256 + 128 * g:384 + 128 * g], s_in[:, 128 * j:128 * j + 128]) + tuple(prep)

        _, post_vjp = jax.vjp(_ssd_post, yraw_ref[...], xs, z_ref[...].astype(F32), dsk_ref[...], nrm_ref[...])
        dy, dxs_skip, dz, ddsk, dnrm = post_vjp(dya_ref[...])
        dz_ref[...] = dz.astype(dz_ref.dtype)
        ddsk_ref[...] += ddsk
        dnrm_ref[...] += dnrm

        dprep = [jnp.zeros_like(p) for p in prep]
        dxp = []
        dbg = [jnp.zeros((CH, 128), F32), jnp.zeros((CH, 128), F32)]
        dcg = [jnp.zeros((CH, 128), F32), jnp.zeros((CH, 128), F32)]
        for j in range(NH // 2):
            g = j // 4
            _, pair_vjp = jax.vjp(functools.partial(_ssd_pair, j), *pair_args(j))
            cts = pair_vjp((dy[:, 128 * j:128 * j + 128], ds_ref[:, 128 * j:128 * j + 128]))
            dxp.append(cts[0])
            dbg[g] = dbg[g] + cts[1]
            dcg[g] = dcg[g] + cts[2]
            ds_ref[:, 128 * j:128 * j + 128] = cts[3]
            dprep = [a + b for a, b in zip(dprep, cts[4:])]
            rows = pl.ds(kt * j, kt)
            dwd_acc = dwd_acc + lax.dot_general(hid_ref[rows, :], dff_ref[rows, :], tn, preferred_element_type=F32)
            dwu_acc = dwu_acc + lax.dot_general(dhp_ref[rows, :], v_ref[rows, :], tn, preferred_element_type=F32)
        dwd_ref[...] = dwd_acc.astype(MXU)
        dwu_ref[...] = dwu_acc.astype(MXU)
        ddtraw, ddtb, dalog = prep_vjp(tuple(dprep))
        ddtb_ref[...] += ddtb
        dalog_ref[...] += dalog

        dxs_pre = (dxs_skip + jnp.concatenate(dxp, axis=1)) * _dsilu(xs_pre)
        dbc_pre = jnp.concatenate([dbg[0], dbg[1], dcg[0], dcg[1]], axis=1) * _dsilu(bc_pre)
        dcbx_ref[...] += _rowsum(dxs_pre)
        dcbb_ref[...] += _rowsum(dbc_pre)
        fx = _taps_future(dxs_pre, futx_ref[...])
        fbc = _taps_future(dbc_pre, futb_ref[...])
        xs_in = xs_ref[...].astype(F32)
        bc_in = bc_ref[:, :512]
        for k in range(4):
            dcwx_ref[k:k + 1, :] += _rowsum(fx[k] * xs_in)
            dcwb_ref[k:k + 1, :] += _rowsum(fbc[k] * bc_in)
        cwx = cwx_ref[...]
        cwb = cwb_ref[...]
        dxs_in = fx[0] * cwx[0:1, :]
        dbc_in = fbc[0] * cwb[0:1, :]
        for k in range(1, 4):
            dxs_in = dxs_in + fx[k] * cwx[k:k + 1, :]
            dbc_in = dbc_in + fbc[k] * cwb[k:k + 1, :]
        futx_ref[...] = dxs_pre[0:8, :]
        futb_ref[...] = dbc_pre[0:8, :]
        dxs_ref[...] = dxs_in.astype(dxs_ref.dtype)
        dbc_ref[...] = jnp.concatenate([dbc_in, ddtraw, jnp.zeros((CH, SEG - 640), F32)], axis=1).astype(dbc_ref.dtype)

    row_out = lambda: pl.BlockSpec((CH, D), lambda i: (rev(i), 0))
    outs = pl.pallas_call(
        body, name="ssd_bwd", grid=(n_c,),
        in_specs=[row_out(), row_out(), row_out(), pl.BlockSpec((CH, 512), lambda i: (rev(i), 0))]
        + _proj_specs(CH, (0, 5, 6), rev) + [pl.BlockSpec((1, NS, D), lambda i: (rev(i), 0, 0)),
                                             _full((4, D)), _full((4, 512)),
                                             _full((1, 128)), _full((1, 128)), _full((1, D)), _full((1, D)),
                                             pl.BlockSpec((t_n, fb), lambda i: (0, i)), _full((t_n, D)),
                                             pl.BlockSpec((t_n, fb), lambda i: (0, i)), _full((t_n, D))],
        out_specs=[row_out(), row_out(), row_out(), _full((4, D)), _full((4, 512)), _full((1, D)), _full((1, 512)),
                   _full((1, 128)), _full((1, 128)), _full((1, D)), _full((1, D)),
                   pl.BlockSpec((fb, D), lambda i: (i, 0)), pl.BlockSpec((fb, D), lambda i: (i, 0))],
        out_shape=[jax.ShapeDtypeStruct((t_n, D), MXU)] * 3 + [
            jax.ShapeDtypeStruct(s, F32) for s in ((4, D), (4, 512), (1, D), (1, 512), (1, 128), (1, 128), (1, D), (1, D))]
        + [jax.ShapeDtypeStruct((FF, D), MXU)] * 2,
        scratch_shapes=[pltpu.VMEM((NS, D), F32), pltpu.VMEM((8, D), F32), pltpu.VMEM((8, 512), F32)],
        compiler_params=_cp("arbitrary"),
    )(dya, *saved, *_proj_ops(proj, (0, 5, 6)), sprev, cwx, cwb, dtb, alog, dsk, nrm, *mlp_ops)
    return outs


LRU_ROWS = 256
LRU_BLK = 256


def _lru_gates(xr, wa, wx, ba, bx, lam):
    pr = jnp.concatenate([_nn(xr[:, LRU_BLK * b:LRU_BLK * (b + 1)], wa[b]) for b in range(D // LRU_BLK)], axis=1) + ba
    pi = jnp.concatenate([_nn(xr[:, LRU_BLK * b:LRU_BLK * (b + 1)], wx[b]) for b in range(D // LRU_BLK)], axis=1) + bx
    log_a = -LRU_C * _sigmoid(pr) * _softplus(-lam)
    a = jnp.exp(log_a)
    mult = jnp.sqrt(1.0 - jnp.exp(2.0 * log_a))
    return a, mult * (_sigmoid(pi) * xr)


def _lru_out(h, g):
    return h * jax.nn.gelu(g, approximate=True)


def _lru_fwd(proj, cw, cb, wa, wx, ba, bx, lam):
    t_n = proj[0].shape[0]
    rows = min(LRU_ROWS, t_n)
    fwd = lambda i: i

    def body(g_ref, x_ref, xp_ref, cw_ref, cb_ref, wa_ref, wx_ref, ba_ref, bx_ref, lam_ref, yb_ref, h_ref, xr_ref,
             a_s, u_s, carry):
        i = pl.program_id(0)

        @pl.when(i == 0)
        def _():
            carry[...] = jnp.zeros_like(carry)

        keep = jnp.where(i == 0, 0.0, 1.0)
        xr = _conv_apply(_taps_past(x_ref[...].astype(F32), _prev8(xp_ref) * keep), cw_ref[...], cb_ref[...])
        xr_ref[...] = xr
        a, u = _lru_gates(xr, wa_ref[...], wx_ref[...], ba_ref[...], bx_ref[...], lam_ref[...])
        a_s[...] = a
        u_s[...] = u
        row = lax.broadcasted_iota(jnp.int32, (8, D), 0)

        def blk(b, c):
            s = pl.multiple_of(b * 8, 8)
            av = a_s[pl.ds(s, 8), :]
            uv = u_s[pl.ds(s, 8), :]
            for d in (1, 2, 4):
                m = row >= d
                uv = uv + av * jnp.where(m, pltpu.roll(uv, d, 0), 0.0)
                av = av * jnp.where(m, pltpu.roll(av, d, 0), 1.0)
            hv = uv + av * c
            h_ref[pl.ds(s, 8), :] = hv
            return hv[7:8, :]

        carry[0:1, :] = lax.fori_loop(0, rows // 8, blk, carry[0:1, :])
        yb_ref[...] = _lru_out(h_ref[...], g_ref[...].astype(F32)).astype(yb_ref.dtype)

    return pl.pallas_call(
        body, name="lru_fwd", grid=(t_n // rows,),
        in_specs=_proj_specs(rows, (1, 2), fwd) + _prev_specs(rows, (2,), fwd) + [
            _full((4, D)), _full((1, D)), _full((4, LRU_BLK, LRU_BLK)), _full((4, LRU_BLK, LRU_BLK)),
            _full((1, D)), _full((1, D)), _full((1, D))],
        out_specs=[pl.BlockSpec((rows, D), lambda i: (i, 0))] * 3,
        out_shape=[jax.ShapeDtypeStruct((t_n, D), MXU), jax.ShapeDtypeStruct((t_n, D), F32),
                   jax.ShapeDtypeStruct((t_n, D), F32)],
        scratch_shapes=[pltpu.VMEM((rows, D), F32), pltpu.VMEM((rows, D), F32), pltpu.VMEM((8, D), F32)],
        compiler_params=_cp("arbitrary"),
    )(*_proj_ops(proj, (1, 2, 2)), cw, cb, wa, wx, ba, bx, lam)


def _lru_bwd(dyb, proj, h, xr_saved, cw, wa, wx, ba, bx, lam):
    t_n = proj[0].shape[0]
    rows = min(LRU_ROWS, t_n)
    n_t = t_n // rows
    rev = lambda i: n_t - 1 - i
    rb = rows // 8

    def body(dyb_ref, g_ref, x_ref, h_ref, hp_ref, xr_ref, cw_ref, wa_ref, wx_ref, ba_ref, bx_ref, lam_ref,
             dg_ref, dx_ref, dcw_ref, dcb_ref, dwa_ref, dwx_ref, dba_ref, dbx_ref, dlam_ref,
             a_s, dh_s, hx_s, da_s, du_s, carry, fut):
        i = pl.program_id(0)
        acc_refs = (dcw_ref, dcb_ref, dwa_ref, dwx_ref, dba_ref, dbx_ref, dlam_ref)

        @pl.when(i == 0)
        def _():
            for r in (carry, fut) + acc_refs:
                r[...] = jnp.zeros_like(r)

        keep = jnp.where(i == n_t - 1, 0.0, 1.0)
        gate_in = (xr_ref[...], wa_ref[...], wx_ref[...], ba_ref[...], bx_ref[...], lam_ref[...])
        (a, _), gates_vjp = jax.vjp(_lru_gates, *gate_in)
        _, out_vjp = jax.vjp(_lru_out, h_ref[...], g_ref[...].astype(F32))
        dh, dg = out_vjp(dyb_ref[...])
        dg_ref[...] = dg.astype(dg_ref.dtype)
        a_s[...] = a
        dh_s[...] = dh
        hx_s[0:8, :] = hp_ref[...] * keep
        hx_s[8:, :] = h_ref[...]
        row = lax.broadcasted_iota(jnp.int32, (8, D), 0)

        def blk(b, c):
            s = pl.multiple_of((rb - 1 - b) * 8, 8)
            av = a_s[pl.ds(s, 8), :]
            dhv = dh_s[pl.ds(s, 8), :]
            kv = av * dhv
            for d in (1, 2, 4):
                m = row <= 7 - d
                kv = kv + av * jnp.where(m, pltpu.roll(kv, 8 - d, 0), 0.0)
                av = av * jnp.where(m, pltpu.roll(av, 8 - d, 0), 1.0)
            kv = kv + av * c
            gv = dhv + jnp.where(row < 7, pltpu.roll(kv, 7, 0), c)
            hb = hx_s[pl.ds(s + 8, 8), :]
            hpv = hx_s[pl.ds(s, 8), :]
            hprev = jnp.where(row >= 1, pltpu.roll(hb, 1, 0), hpv[7:8, :])
            du_s[pl.ds(s, 8), :] = gv
            da_s[pl.ds(s, 8), :] = gv * hprev
            return kv[0:1, :]

        carry[0:1, :] = lax.fori_loop(0, rb, blk, carry[0:1, :])
        dxr, dwa, dwx, dba, dbx, dlam = gates_vjp((da_s[...], du_s[...]))
        dwa_ref[...] += dwa
        dwx_ref[...] += dwx
        dba_ref[...] += dba
        dbx_ref[...] += dbx
        dlam_ref[...] += dlam
        dcb_ref[...] += _rowsum(dxr)
        ft = _taps_future(dxr, fut[...])
        x_in = x_ref[...].astype(F32)
        for k in range(4):
            dcw_ref[k:k + 1, :] += _rowsum(ft[k] * x_in)
        cwv = cw_ref[...]
        dx = ft[0] * cwv[0:1, :]
        for k in range(1, 4):
            dx = dx + ft[k] * cwv[k:k + 1, :]
        fut[...] = dxr[0:8, :]
        dx_ref[...] = dx.astype(dx_ref.dtype)

    row_in = lambda: pl.BlockSpec((rows, D), lambda i: (rev(i), 0))
    prev_h = pl.BlockSpec((8, D), lambda i: (jnp.maximum(rev(i) * rb - 1, 0), 0))
    wspec = lambda: _full((4, LRU_BLK, LRU_BLK))
    return pl.pallas_call(
        body, name="lru_bwd", grid=(n_t,),
        in_specs=[row_in()] + _proj_specs(rows, (1, 2), rev) + [row_in(), prev_h, row_in()] + [
            _full((4, D)), wspec(), wspec(), _full((1, D)), _full((1, D)), _full((1, D))],
        out_specs=[row_in(), row_in(), _full((4, D)), _full((1, D)), wspec(), wspec(), _full((1, D)), _full((1, D)),
                   _full((1, D))],
        out_shape=[jax.ShapeDtypeStruct((t_n, D), MXU)] * 2 + [
            jax.ShapeDtypeStruct(s, F32) for s in ((4, D), (1, D), (4, LRU_BLK, LRU_BLK), (4, LRU_BLK, LRU_BLK),
                                                   (1, D), (1, D), (1, D))],
        scratch_shapes=[pltpu.VMEM((rows, D), F32), pltpu.VMEM((rows, D), F32), pltpu.VMEM((rows + 8, D), F32),
                        pltpu.VMEM((rows, D), F32), pltpu.VMEM((rows, D), F32), pltpu.VMEM((8, D), F32),
                        pltpu.VMEM((8, D), F32)],
        compiler_params=_cp("arbitrary"),
    )(dyb, *_proj_ops(proj, (1, 2)), h, h, xr_saved, cw, wa, wx, ba, bx, lam)


def _merge_out(ya, yb, proj, x, wout, g1):
    t_n = x.shape[0]
    tm = min(t_n, 512)

    def body(ya_ref, yb_ref, ga_ref, gb_ref, x_ref, w_ref, g_ref, h1_ref, mix_ref, mg_ref):
        merged = (_sigmoid(ga_ref[...].astype(F32)) * ya_ref[...].astype(F32)
                  + _sigmoid(gb_ref[...].astype(F32)) * yb_ref[...].astype(F32))
        mg = merged.astype(MXU)
        mg_ref[...] = mg
        mix = jnp.dot(mg, w_ref[...].reshape(D, D), preferred_element_type=F32)
        mix_ref[...] = mix
        h1_ref[...] = x_ref[...] + _rms(mix, g_ref[...])

    row = lambda: pl.BlockSpec((tm, D), lambda i: (i, 0))
    return pl.pallas_call(
        body, name="merge_out", grid=(t_n // tm,),
        in_specs=[row(), row()] + _proj_specs(tm, (3, 4), lambda i: i)
        + [row(), _slot_spec(wout, D, D, lambda i: 0), _full((1, D))],
        out_specs=[row(), row(), row()],
        out_shape=[jax.ShapeDtypeStruct((t_n, D), F32), jax.ShapeDtypeStruct((t_n, D), F32),
                   jax.ShapeDtypeStruct((t_n, D), MXU)],
        compiler_params=_cp("parallel"),
    )(ya, yb, *_proj_ops(proj, (3, 4)), x, wout[0], g1)


def _out_bwd(dv, h1, g2, dout, mix, ya, yb, proj, wout, g1, w_in_pt):
    t_n = dv.shape[0]
    tm = min(t_n, 256)

    def body(dv_ref, h1_ref, g2_ref, dout_ref, mix_ref, ya_ref, yb_ref, ga_ref, gb_ref, w_ref, g_ref, wa_ref, wb_ref,
             dh1_ref, dmix_ref, dya_ref, dyb_ref, dga_ref, dgb_ref, dg2_ref, dg1_ref, dug_ref):
        @pl.when(pl.program_id(0) == 0)
        def _():
            dg1_ref[...] = jnp.zeros_like(dg1_ref)
            dg2_ref[...] = jnp.zeros_like(dg2_ref)

        dx, dg_rows = _rms_bwd(h1_ref[...], g2_ref[...], dv_ref[...])
        dg2_ref[...] += _rowsum(dg_rows)
        dh1 = dout_ref[...] + dx
        dh1_ref[...] = dh1
        dmix, dg_rows = _rms_bwd(mix_ref[...], g_ref[...], dh1)
        dg1_ref[...] += _rowsum(dg_rows)
        dmix_b = dmix.astype(MXU)
        dmix_ref[...] = dmix_b
        dmg = lax.dot_general(dmix_b, w_ref[...].reshape(D, D), (((1,), (1,)), ((), ())), preferred_element_type=F32)
        sa = _sigmoid(ga_ref[...].astype(F32))
        sb = _sigmoid(gb_ref[...].astype(F32))
        dya_ref[...] = dmg * sa
        dyb_ref[...] = dmg * sb
        dga = (dmg * ya_ref[...].astype(F32) * sa * (1.0 - sa)).astype(MXU)
        dga_ref[...] = dga
        dug = jnp.dot(dga, wa_ref[...], preferred_element_type=F32)
        dgb = (dmg * yb_ref[...].astype(F32) * sb * (1.0 - sb)).astype(MXU)
        dgb_ref[...] = dgb
        dug_ref[...] = dug + jnp.dot(dgb, wb_ref[...], preferred_element_type=F32)

    row = lambda: pl.BlockSpec((tm, D), lambda i: (i, 0))
    vec = lambda: _full((1, D))
    seg = lambda s: pl.BlockSpec((SEG, D), lambda i: (s, 0))
    return pl.pallas_call(
        body, name="out_bwd", grid=(t_n // tm,),
        in_specs=[row(), row(), vec(), row(), row(), row(), row()] + _proj_specs(tm, (3, 4), lambda i: i)
        + [_slot_spec(wout, D, D, lambda i: 0), vec(), seg(3), seg(4)],
        out_specs=[row(), row(), row(), row(), row(), row(), vec(), vec(), row()],
        out_shape=[jax.ShapeDtypeStruct((t_n, D), F32), jax.ShapeDtypeStruct((t_n, D), MXU),
                   jax.ShapeDtypeStruct((t_n, D), F32), jax.ShapeDtypeStruct((t_n, D), F32),
                   jax.ShapeDtypeStruct((t_n, D), MXU), jax.ShapeDtypeStruct((t_n, D), MXU),
                   jax.ShapeDtypeStruct((1, D), F32), jax.ShapeDtypeStruct((1, D), F32),
                   jax.ShapeDtypeStruct((t_n, D), F32)],
        compiler_params=_cp("arbitrary"),
    )(dv, h1, g2, dout, mix, ya, yb, *_proj_ops(proj, (3, 4)), wout[0], g1, w_in_pt, w_in_pt)


MLP_TM = 1024
MLP_TF_FWD = 1024
MLP_TF_BWD = 1024


def _mlp_fwd(h1, g2, wup, wdown, g3, tgt):
    t_n = h1.shape[0]
    tm = min(t_n, MLP_TM)
    n_f = FF // MLP_TF_FWD

    def body(h1_ref, g2_ref, wu_ref, wd_ref, g3_ref, tgt_ref, hp_ref, v_ref, dout_ref, dff_ref, loss_ref, dg3_ref, acc):
        i, j = pl.program_id(0), pl.program_id(1)

        @pl.when((i == 0) & (j == 0))
        def _():
            loss_ref[...] = jnp.zeros_like(loss_ref)
            dg3_ref[...] = jnp.zeros_like(dg3_ref)

        @pl.when(j == 0)
        def _():
            v_ref[...] = _rms(h1_ref[...], g2_ref[...]).astype(MXU)
            acc[...] = jnp.zeros_like(acc)

        hp = lax.dot_general(v_ref[...], wu_ref[...].reshape(MLP_TF_FWD, D), (((1,), (1,)), ((), ())),
                             preferred_element_type=F32)
        hp_ref[...] = hp.astype(MXU)
        hid = jnp.square(jnp.maximum(hp, 0.0))
        acc[...] += jnp.dot(hid.astype(MXU), wd_ref[...].reshape(MLP_TF_FWD, D), preferred_element_type=F32)

        @pl.when(j == n_f - 1)
        def _():
            ff = acc[...]
            err = h1_ref[...] + _rms(ff, g3_ref[...]) - tgt_ref[...]
            loss_ref[...] += 0.5 * jnp.sum(jnp.mean(err * err, axis=-1, keepdims=True), axis=0, keepdims=True)
            dout = err * (1.0 / D)
            dout_ref[...] = dout
            dff, dg_rows = _rms_bwd(ff, g3_ref[...], dout)
            dg3_ref[...] += _rowsum(dg_rows)
            dff_ref[...] = dff.astype(MXU)

    row = lambda: pl.BlockSpec((tm, D), lambda i, j: (i, 0))
    vec = lambda: pl.BlockSpec((1, D), lambda i, j: (0, 0))
    return pl.pallas_call(
        body, name="mlp_fwd", grid=(t_n // tm, n_f),
        in_specs=[row(), vec(), _slot_spec(wup, FF, MLP_TF_FWD, lambda i, j: j),
                  _slot_spec(wdown, FF, MLP_TF_FWD, lambda i, j: j), vec(), row()],
        out_specs=[pl.BlockSpec((tm, MLP_TF_FWD), lambda i, j: (i, j)), row(), row(), row(),
                   pl.BlockSpec((1, 1), lambda i, j: (0, 0)), vec()],
        out_shape=[jax.ShapeDtypeStruct((t_n, FF), MXU), jax.ShapeDtypeStruct((t_n, D), MXU),
                   jax.ShapeDtypeStruct((t_n, D), F32), jax.ShapeDtypeStruct((t_n, D), MXU),
                   jax.ShapeDtypeStruct((1, 1), F32), jax.ShapeDtypeStruct((1, D), F32)],
        scratch_shapes=[pltpu.VMEM((tm, D), F32)],
        compiler_params=_cp("arbitrary", "arbitrary"),
    )(h1, g2, wup[0], wdown[0], g3, tgt)


def _mlp_bwd(dff, hp, wup, wdown):
    t_n = dff.shape[0]
    tm = min(t_n, MLP_TM)
    n_f = FF // MLP_TF_BWD

    def mm_body(dff_ref, hp_ref, wu_ref, wd_ref, dv_ref, dhp_ref, hid_ref):
        @pl.when(pl.program_id(1) == 0)
        def _():
            dv_ref[...] = jnp.zeros_like(dv_ref)

        relu = jnp.maximum(hp_ref[...].astype(F32), 0.0)
        hid_ref[...] = jnp.square(relu).astype(MXU)
        dhid = lax.dot_general(dff_ref[...], wd_ref[...].reshape(MLP_TF_BWD, D), (((1,), (1,)), ((), ())),
                               preferred_element_type=F32)
        dhp = (dhid * (2.0 * relu)).astype(MXU)
        dhp_ref[...] = dhp
        dv_ref[...] += jnp.dot(dhp, wu_ref[...].reshape(MLP_TF_BWD, D), preferred_element_type=F32)

    row = lambda: pl.BlockSpec((tm, D), lambda i, j: (i, 0))
    blk = lambda: pl.BlockSpec((tm, MLP_TF_BWD), lambda i, j: (i, j))
    wblk = lambda w: _slot_spec(w, FF, MLP_TF_BWD, lambda i, j: j)
    return pl.pallas_call(
        mm_body, name="mlp_bwd", grid=(t_n // tm, n_f),
        in_specs=[row(), blk(), wblk(wup), wblk(wdown)], out_specs=[row(), blk(), blk()],
        out_shape=[jax.ShapeDtypeStruct((t_n, D), F32), jax.ShapeDtypeStruct((t_n, FF), MXU),
                   jax.ShapeDtypeStruct((t_n, FF), MXU)],
        compiler_params=_cp("parallel", "arbitrary"),
    )(dff, hp, wup[0], wdown[0])


def _wgrad(a, g, name):
    t_n, k_n = a.shape
    n_n = g.shape[1]
    tt = min(t_n, 1024)
    tk, tn = min(k_n, 1024), min(n_n, 1024)

    n_t = t_n // tt

    def body(a_ref, g_ref, o_ref, acc):
        t = pl.program_id(2)

        @pl.when(t == 0)
        def _():
            acc[...] = jnp.zeros_like(acc)

        acc[...] += lax.dot_general(a_ref[...], g_ref[...], (((0,), (0,)), ((), ())), preferred_element_type=F32)

        @pl.when(t == n_t - 1)
        def _():
            o_ref[...] = acc[...].astype(o_ref.dtype)

    return pl.pallas_call(
        body, name=name, grid=(k_n // tk, n_n // tn, n_t),
        in_specs=[pl.BlockSpec((tt, tk), lambda k, n, t: (t, k)), pl.BlockSpec((tt, tn), lambda k, n, t: (t, n))],
        out_specs=pl.BlockSpec((tk, tn), lambda k, n, t: (k, n)),
        out_shape=jax.ShapeDtypeStruct((k_n, n_n), MXU),
        scratch_shapes=[pltpu.VMEM((tk, tn), F32)],
        compiler_params=_cp("parallel", "parallel", "arbitrary"),
    )(a, g)


def _wgrad_segs(segs, g, name):
    t_n, n_n = g.shape
    n_s = len(segs)
    tt = min(t_n, 1024)
    n_t = t_n // tt

    def body(*refs):
        a_refs = refs[:n_s]
        g_ref, o_ref, acc = refs[n_s:]
        s_id, t = pl.program_id(0), pl.program_id(1)

        @pl.when(t == 0)
        def _():
            acc[...] = jnp.zeros_like(acc)

        for s in range(n_s):
            @pl.when(s_id == s)
            def _(s=s):
                acc[...] += lax.dot_general(a_refs[s][...], g_ref[...], (((0,), (0,)), ((), ())),
                                            preferred_element_type=F32)

        @pl.when(t == n_t - 1)
        def _():
            o_ref[...] = acc[...].astype(o_ref.dtype)

    seg_spec = lambda s: pl.BlockSpec((tt, SEG), lambda i, t: (jnp.where(i == s, t, jnp.where(i < s, 0, n_t - 1)), 0))
    return pl.pallas_call(
        body, name=name, grid=(n_s, n_t),
        in_specs=[seg_spec(s) for s in range(n_s)] + [pl.BlockSpec((tt, n_n), lambda i, t: (t, 0))],
        out_specs=pl.BlockSpec((SEG, n_n), lambda i, t: (i, 0)),
        out_shape=jax.ShapeDtypeStruct((n_s * SEG, n_n), MXU),
        scratch_shapes=[pltpu.VMEM((SEG, n_n), F32)],
        compiler_params=_cp("arbitrary", "arbitrary"),
    )(*segs, g)


def _inproj_bwd(dsegs, seg_ids, du_part, w, x, g0, dh1):
    t_n = x.shape[0]
    tm = min(t_n, 512)
    n_k = len(dsegs)

    def body(*refs):
        dp_refs = refs[:n_k]
        w_refs = refs[n_k:2 * n_k]
        part_ref, x_ref, g_ref, dh1_ref, dx_ref, dg0_ref = refs[2 * n_k:]

        @pl.when(pl.program_id(0) == 0)
        def _():
            dg0_ref[...] = jnp.zeros_like(dg0_ref)

        du = part_ref[...]
        for s in range(n_k):
            du = du + jnp.dot(dp_refs[s][...], w_refs[s][...], preferred_element_type=F32)
        dx, dg_rows = _rms_bwd(x_ref[...], g_ref[...], du)
        dg0_ref[...] += _rowsum(dg_rows)
        dx_ref[...] = dh1_ref[...] + dx

    row = lambda: pl.BlockSpec((tm, D), lambda i: (i, 0))
    return pl.pallas_call(
        body, name="inproj_bwd", grid=(t_n // tm,),
        in_specs=[row() for _ in range(n_k)]
        + [pl.BlockSpec((SEG, D), functools.partial(lambda i, s: (s, 0), s=s)) for s in seg_ids]
        + [row(), row(), _full((1, D)), row()],
        out_specs=[row(), _full((1, D))],
        out_shape=[jax.ShapeDtypeStruct((t_n, D), F32), jax.ShapeDtypeStruct((1, D), F32)],
        compiler_params=_cp("arbitrary"),
    )(*dsegs, *([w] * n_k), du_part, x, g0, dh1)


def _blockdiag4(w):
    w4 = w.reshape(4, 4, 64, 1, 64).astype(MXU)
    same = (jnp.arange(4)[:, None, None, None] == jnp.arange(4)[None, None, :, None])
    return jnp.where(same[None], w4, jnp.zeros((), MXU)).reshape(4, 256, 256)


def _blockdiag4_extract(g):
    g5 = g.reshape(4, 4, 64, 4, 64)
    return jnp.stack([g5[:, q, :, q, :] for q in range(4)], axis=1).reshape(NH, 64, 64)


def _local_step(x, tgt, p, after_ssd=None, late_weights=None, send_mlp_grads=None, send_late_grads=None):
    f = lambda a: a.astype(F32)
    proj, u = _inproj(x, p["norm_mix_pre"], p["w_in_pT"])
    ssm_params = (p["cw_xs"], p["cw_bc"], p["cb_xs"], p["cb_bc"], p["dt_bias"], p["a_log"], p["d_skip_x"],
                  p["ssm_norm"])
    ya, sprev, *ssd_saved = _ssd_fwd(proj, *ssm_params)
    cb_lru = p["conv_lru_b"] if after_ssd is None else p["conv_lru_b"] + after_ssd(ya)
    lru_params = (p["conv_lru_w"], cb_lru, p["wa_bd"], p["wx_bd"], p["lru_ba"], p["lru_bx"], p["lru_lambda"])
    yb, h, xr = _lru_fwd(proj, *lru_params)
    if late_weights is not None:
        p = dict(p, **late_weights(yb))
    p = dict(p, **{n: _slotted(p[n]) for n in ("w_out", "w_upT", "w_down")})
    h1, mix, merged = _merge_out(ya, yb, proj, x, p["w_out"], p["norm_mix_post"])
    hp, v, dout, dff, loss, dg3 = _mlp_fwd(h1, p["norm_mlp_pre"], p["w_upT"], p["w_down"], p["norm_mlp_post"], tgt)

    dv, dhp, hid = _mlp_bwd(dff, hp, p["w_upT"], p["w_down"])
    dh1, dmix, dya, dyb, dga, dgb, dg2, dg1, du_gates = _out_bwd(
        dv, h1, p["norm_mlp_pre"], dout, mix, ya, yb, proj, p["w_out"], p["norm_mix_post"], p["w_in_pT"])
    d_w_out = _wgrad(merged, dmix, "wgrad_out")
    (dz, dxs, dbc, dcwx, dcwb, dcbx, dcbb, ddtb, dalog, ddsk, dnrm, d_w_down, d_w_up_t) = _ssd_bwd(
        dya, ssd_saved, proj, sprev, p["cw_xs"], p["cw_bc"], *ssm_params[4:], (hid, dff, dhp, v))
    ba_lru = p["lru_ba"] if send_mlp_grads is None else p["lru_ba"] + send_mlp_grads(d_w_up_t, d_w_down, d_w_out)
    (dgl, dxl, dcwl, dcbl, dwa, dwx, dba, dbx, dlam) = _lru_bwd(
        dyb, proj, h, xr, p["conv_lru_w"], p["wa_bd"], p["wx_bd"], ba_lru, p["lru_bx"], p["lru_lambda"])
    dsegs = [dz, dgl, dxl, dga, dgb, dxs, dbc]
    d_w_in_pt = _wgrad_segs(dsegs, u, "wgrad_in")
    g0 = p["norm_mix_pre"]
    if send_late_grads is not None:
        g0 = g0 + send_late_grads(d_w_in_pt)
    grad_x, dg0 = _inproj_bwd([dz, dgl, dxl, dxs, dbc], (0, 1, 2, 5, 6), du_gates, p["w_in_pT"], x, g0, dh1)
    grads = dict(
        norm_mix_pre=dg0, w_in_pT=d_w_in_pt, conv_ssm_w=jnp.concatenate([dcwx, dcwb], axis=1),
        conv_ssm_b=jnp.concatenate([dcbx, dcbb], axis=1), dt_bias=ddtb[:, :NH], a_log=dalog[:, :NH],
        d_skip=f(ddsk).reshape(NH, 64).sum(axis=1)[None, :], ssm_norm=dnrm, conv_lru_w=dcwl, conv_lru_b=dcbl,
        lru_wa=_blockdiag4_extract(dwa), lru_ba=dba, lru_wx=_blockdiag4_extract(dwx), lru_bx=dbx, lru_lambda=dlam,
        w_out=d_w_out, norm_mix_post=dg1, norm_mlp_pre=dg2, w_upT=d_w_up_t, w_down=d_w_down, norm_mlp_post=dg3)
    return loss[0, 0], grad_x, grads


W_IN_COLS = 6672


def _w_in_t_to_padded(wt):
    z, xs, bc, dt = wt[0:1024], wt[1024:2048], wt[2048:2560], wt[2560:2576]
    gl, xl, ga, gb = wt[2576:3600], wt[3600:4624], wt[4624:5648], wt[5648:6672]
    return jnp.concatenate([z, gl, xl, ga, gb, xs, bc, dt, jnp.zeros((NP - 6672, wt.shape[1]), wt.dtype)], axis=0)


def _w_in_t_from_padded(wp):
    z, gl, xl, ga, gb = (wp[SEG * s:SEG * (s + 1)] for s in range(5))
    xs, bc, dt = wp[5120:6144], wp[6144:6656], wp[6656:6672]
    return jnp.concatenate([z, xs, bc, dt, gl, xl, ga, gb], axis=0)


def _prep_params(full, big):
    f = lambda a: a.astype(F32)
    pad128 = lambda a: jnp.pad(f(a).reshape(1, -1), ((0, 0), (0, 128 - a.size)))
    cw = f(full["conv_ssm_w"])
    cb = f(full["conv_ssm_b"]).reshape(1, -1)
    return dict(
        big, norm_mix_pre=f(full["norm_mix_pre"]).reshape(1, D),
        cw_xs=cw[:, :D], cw_bc=cw[:, D:], cb_xs=cb[:, :D], cb_bc=cb[:, D:],
        dt_bias=pad128(full["dt_bias"]), a_log=pad128(full["a_log"]),
        d_skip_x=jnp.repeat(f(full["d_skip"]).reshape(-1), 64).reshape(1, D), ssm_norm=f(full["ssm_norm"]).reshape(1, D),
        conv_lru_w=f(full["conv_lru_w"]), conv_lru_b=f(full["conv_lru_b"]).reshape(1, D),
        wa_bd=_blockdiag4(full["lru_wa"]), wx_bd=_blockdiag4(full["lru_wx"]),
        lru_ba=f(full["lru_ba"]).reshape(1, D), lru_bx=f(full["lru_bx"]).reshape(1, D),
        lru_lambda=f(full["lru_lambda"]).reshape(1, D),
        norm_mix_post=f(full["norm_mix_post"]).reshape(1, D), norm_mlp_pre=f(full["norm_mlp_pre"]).reshape(1, D),
        norm_mlp_post=f(full["norm_mlp_post"]).reshape(1, D))


MESH_ID = pl.DeviceIdType.MESH
ANY = pl.BlockSpec(memory_space=pl.ANY)


def _my_place():
    x, y, c = lax.axis_index("x"), lax.axis_index("y"), lax.axis_index("c")
    return x, y, c, 4 * x + 2 * y + c


def _peer(x, y, c, k):
    return (x ^ ((k >> 2) & 1), y ^ ((k >> 1) & 1), c ^ (k & 1))


def _all_gather(pack, name):
    r_n = pack.shape[0]
    half = -(-r_n // 32) * 16
    n_cp = 9

    def body(in_ref, out_ref, send_sems, recv_sems, local_sem):
        x, y, c, me = _my_place()
        here, sibling, x_nbr, y_nbr = (x, y, c), (x, y, 1 - c), (1 - x, y, c), (x, 1 - y, c)
        part = {"all": pl.ds(0, r_n), "lo": pl.ds(0, half), "hi": pl.ds(half, r_n - half)}

        def copy(j, block, rows, to, src=None):
            return pltpu.make_async_remote_copy(
                src_ref=out_ref.at[block, part[rows]] if src is None else src, dst_ref=out_ref.at[block, part[rows]],
                send_sem=send_sems.at[j], recv_sem=recv_sems.at[j], device_id=to, device_id_type=MESH_ID)

        mine = pltpu.make_async_copy(in_ref, out_ref.at[me], local_sem)
        mine.start()
        first = [copy(0, me, "all", sibling, src=in_ref), copy(1, me, "all", x_nbr, src=in_ref),
                 copy(2, me, "all", y_nbr, src=in_ref)]
        for cp in first:
            cp.start()
        relay = [[(3, me ^ 4, "lo", y_nbr), (5, me ^ 4, "all", sibling)],
                 [(4, me ^ 2, "hi", x_nbr), (6, me ^ 2, "all", sibling)],
                 [(7, me ^ 6, "lo", sibling)],
                 [(8, me ^ 6, "hi", sibling)]]
        landed = [(1, me ^ 4, "all"), (2, me ^ 2, "all"), (3, me ^ 6, "lo"), (4, me ^ 6, "hi")]
        passed = []
        for (j, block, rows), nxt in zip(landed, relay):
            copy(j, block, rows, here).wait_recv()
            for args in nxt:
                cp = copy(*args)
                cp.start()
                passed.append(cp)
        for j, block, rows in ((0, me ^ 1, "all"), (5, me ^ 5, "all"), (6, me ^ 3, "all"), (7, me ^ 7, "lo"),
                               (8, me ^ 7, "hi")):
            copy(j, block, rows, here).wait_recv()
        for cp in first + passed:
            cp.wait_send()
        mine.wait()

    return pl.pallas_call(
        body, name=name, in_specs=[ANY], out_specs=ANY,
        out_shape=jax.ShapeDtypeStruct((N_DEV,) + pack.shape, pack.dtype),
        scratch_shapes=[pltpu.SemaphoreType.DMA((n_cp,)), pltpu.SemaphoreType.DMA((n_cp,)), pltpu.SemaphoreType.DMA],
    )(pack)


HBM = pl.BlockSpec(memory_space=pltpu.HBM)
SEM = pl.BlockSpec(memory_space=pltpu.SEMAPHORE)
PLAN_GATHER = tuple((k, "pack", 0) for k in range(1, N_DEV))
PLAN_SCATTER = tuple((k, "slot", 0) for k in range(1, N_DEV))
PLAN_GATHER_ICI = tuple((k, "pack", 0) for k in (2, 4, 6))
PLAN_GATHER_D2D = ((1, "pack", 0),) + tuple((1, s, s) for s in (2, 4, 6))
PLAN_SCATTER_CHIPS = tuple((k, "chip", "chip") for k in (2, 4, 6))


def _plan_copy(j, plan, src_ref, land_ref, sems):
    k, source, slot = plan[j]
    x, y, c, me = _my_place()
    if source == "pack":
        src = src_ref
    elif source == "slot":
        src = src_ref.at[me ^ k]
    elif source == "chip":
        src = src_ref.at[(me ^ k) >> 1]
    else:
        src = land_ref.at[me ^ source]
    dst = land_ref.at[me >> 1] if slot == "chip" else land_ref.at[me ^ slot]
    return pltpu.make_async_remote_copy(
        src_ref=src, dst_ref=dst, send_sem=sems[j], recv_sem=sems[len(plan) + j],
        device_id=_peer(x, y, c, k), device_id_type=MESH_ID)


def _exchange_start(src, land, plan, name):
    n_c = len(plan)
    if land is None:
        slots = src.shape[0] if src.ndim == 3 else N_DEV
        land = pltpu.with_memory_space_constraint(lax.empty((slots,) + src.shape[-2:], src.dtype), pltpu.HBM)

    def body(src_ref, land_ref, *rest):
        sems, token = rest[:2 * n_c], rest[2 * n_c + 2]
        for j in range(n_c):
            _plan_copy(j, plan, src_ref, land_ref, sems).start()
        token[...] = jnp.zeros_like(token)

    outs = pl.pallas_call(
        body, name=name,
        out_shape=(pltpu.SemaphoreType.DMA(()),) * (2 * n_c) + (
            pltpu.HBM(src.shape, src.dtype), pltpu.HBM(land.shape, land.dtype), jax.ShapeDtypeStruct((8, 128), F32)),
        in_specs=(HBM, HBM), out_specs=(SEM,) * (2 * n_c) + (HBM, HBM, pl.BlockSpec(memory_space=pltpu.VMEM)),
        input_output_aliases={0: 2 * n_c, 1: 2 * n_c + 1},
        compiler_params=pltpu.CompilerParams(has_side_effects=pltpu.SideEffectType.DATAFLOW_SIDE_EFFECTING),
    )(pltpu.with_memory_space_constraint(src, pltpu.HBM), land)
    return outs[:2 * n_c], outs[2 * n_c], outs[2 * n_c + 1], outs[2 * n_c + 2]


def _exchange_wait(sems, src_thru, land_thru, after, plan, name):
    n_c = len(plan)

    def body(src_ref, land_ref, *rest):
        for j in range(n_c):
            cp = _plan_copy(j, plan, src_ref, land_ref, rest[:2 * n_c])
            cp.wait_send()
            cp.wait_recv()

    return pl.pallas_call(
        body, name=name,
        out_shape=(pltpu.HBM(src_thru.shape, src_thru.dtype), pltpu.HBM(land_thru.shape, land_thru.dtype)),
        in_specs=(HBM, HBM) + (SEM,) * (2 * n_c) + (pl.BlockSpec(memory_space=pl.ANY),), out_specs=(HBM, HBM),
        input_output_aliases={0: 0, 1: 1},
        compiler_params=pltpu.CompilerParams(has_side_effects=pltpu.SideEffectType.DATAFLOW_SIDE_EFFECTING),
    )(src_thru, land_thru, *sems, after)


def _parts_copies(src_refs, land_ref, sems):
    x, y, c, me = _my_place()
    n_p = len(src_refs)
    n_c = (N_DEV - 1) * n_p
    cps = []
    for k in range(1, N_DEV):
        row0 = 0
        for p, src_ref in enumerate(src_refs):
            j = (k - 1) * n_p + p
            rows = src_ref.shape[1]
            cps.append(pltpu.make_async_remote_copy(
                src_ref=src_ref.at[me ^ k], dst_ref=land_ref.at[me, pl.ds(row0, rows)], send_sem=sems[j],
                recv_sem=sems[n_c + j], device_id=_peer(x, y, c, k), device_id_type=MESH_ID))
            row0 += rows
    return cps


def _scatter_parts_start(srcs, name):
    n_p = len(srcs)
    n_c = (N_DEV - 1) * n_p
    hbm = lambda a: pltpu.with_memory_space_constraint(a, pltpu.HBM)
    land = hbm(lax.empty((N_DEV, sum(s.shape[1] for s in srcs), srcs[0].shape[2]), srcs[0].dtype))

    def body(*refs):
        sems, token = refs[n_p + 1:n_p + 1 + 2 * n_c], refs[-1]
        for cp in _parts_copies(refs[:n_p], refs[n_p], sems):
            cp.start()
        token[...] = jnp.zeros_like(token)

    thru = [pltpu.HBM(a.shape, a.dtype) for a in list(srcs) + [land]]
    outs = pl.pallas_call(
        body, name=name,
        out_shape=(pltpu.SemaphoreType.DMA(()),) * (2 * n_c) + tuple(thru) + (jax.ShapeDtypeStruct((8, 128), F32),),
        in_specs=(HBM,) * (n_p + 1),
        out_specs=(SEM,) * (2 * n_c) + (HBM,) * (n_p + 1) + (pl.BlockSpec(memory_space=pltpu.VMEM),),
        input_output_aliases={i: 2 * n_c + i for i in range(n_p + 1)},
        compiler_params=pltpu.CompilerParams(has_side_effects=pltpu.SideEffectType.DATAFLOW_SIDE_EFFECTING),
    )(*[hbm(s) for s in srcs], land)
    return outs[:2 * n_c], outs[2 * n_c:2 * n_c + n_p], outs[2 * n_c + n_p], outs[2 * n_c + n_p + 1]


def _scatter_parts_wait(sems, srcs_thru, land_thru, after, name):
    n_p = len(srcs_thru)

    def body(*refs):
        for cp in _parts_copies(refs[:n_p], refs[n_p], refs[n_p + 1:n_p + 1 + len(sems)]):
            cp.wait_send()
            cp.wait_recv()

    thru = [pltpu.HBM(a.shape, a.dtype) for a in list(srcs_thru) + [land_thru]]
    outs = pl.pallas_call(
        body, name=name, out_shape=tuple(thru),
        in_specs=(HBM,) * (n_p + 1) + (SEM,) * len(sems) + (pl.BlockSpec(memory_space=pl.ANY),),
        out_specs=(HBM,) * (n_p + 1), input_output_aliases={i: i for i in range(n_p + 1)},
        compiler_params=pltpu.CompilerParams(has_side_effects=pltpu.SideEffectType.DATAFLOW_SIDE_EFFECTING),
    )(*srcs_thru, land_thru, *sems, after)
    return outs[:n_p], outs[n_p]


def _pair_exchange(src, name):
    n_s = N_DEV // 2

    def body(src_ref, land_ref, send_sems, recv_sems):
        x, y, c, _ = _my_place()
        cps = [pltpu.make_async_remote_copy(
            src_ref=src_ref.at[2 * i + 1 - c], dst_ref=land_ref.at[i], send_sem=send_sems.at[i],
            recv_sem=recv_sems.at[i], device_id=(x, y, 1 - c), device_id_type=MESH_ID) for i in range(n_s)]
        for cp in cps:
            cp.start()
        for cp in cps:
            cp.wait_recv()
        for cp in cps:
            cp.wait_send()

    return pl.pallas_call(
        body, name=name, in_specs=[ANY], out_specs=ANY,
        out_shape=jax.ShapeDtypeStruct((n_s,) + src.shape[1:], src.dtype),
        scratch_shapes=[pltpu.SemaphoreType.DMA((n_s,)), pltpu.SemaphoreType.DMA((n_s,))],
    )(src)


def _pair_sum(src, land, name):
    n_s, r_n, c_n = land.shape
    tr = max(t for t in range(16, 513, 16) if r_n % t == 0)

    def body(c_ref, a_ref, b_ref, o_ref):
        o_ref[...] = (a_ref[...].astype(F32) + b_ref[...].astype(F32)).astype(o_ref.dtype)

    blk = lambda: pl.BlockSpec((1, tr, c_n), lambda i, j, c_ref: (i, j, 0))
    return pl.pallas_call(
        body, name=name,
        grid_spec=pltpu.PrefetchScalarGridSpec(
            num_scalar_prefetch=1, grid=(n_s, r_n // tr),
            in_specs=[pl.BlockSpec((1, tr, c_n), lambda i, j, c_ref: (2 * i + c_ref[0], j, 0)), blk()],
            out_specs=blk()),
        out_shape=jax.ShapeDtypeStruct(land.shape, land.dtype),
        compiler_params=_cp("parallel", "parallel"),
    )(lax.axis_index("c").astype(jnp.int32).reshape(1), src, land)


def _slot_sum(parts, name):
    n_s, r_n, c_n = parts.shape
    tr = max(t for t in range(16, 513, 16) if r_n % t == 0)

    def body(p_ref, o_ref):
        acc = p_ref[0].astype(F32)
        for k in range(1, n_s):
            acc = acc + p_ref[k].astype(F32)
        o_ref[...] = acc

    return pl.pallas_call(
        body, name=name, grid=(r_n // tr,),
        in_specs=[pl.BlockSpec((n_s, tr, c_n), lambda i: (0, i, 0))],
        out_specs=pl.BlockSpec((tr, c_n), lambda i: (i, 0)),
        out_shape=jax.ShapeDtypeStruct((r_n, c_n), F32),
        compiler_params=_cp("parallel"),
    )(parts)


def _adam_math(w, g, m, v):
    m = ADAM_B1 * m + (1.0 - ADAM_B1) * g
    v = ADAM_B2 * v + (1.0 - ADAM_B2) * jnp.square(g)
    m_hat = m / (1.0 - ADAM_B1 ** ADAM_STEP)
    v_hat = v / (1.0 - ADAM_B2 ** ADAM_STEP)
    return -ADAM_LR * (m_hat / (jnp.sqrt(v_hat) + ADAM_EPS) + ADAM_WD * w), m, v


def _adam_big(w, g, m, v, name):
    def body(w_ref, g_ref, m_ref, v_ref, d_ref, mo_ref, vo_ref):
        d_ref[...], mo_ref[...], vo_ref[...] = _adam_math(w_ref[...], g_ref[...], m_ref[...], v_ref[...])

    if w.ndim == 3:
        _, r_n, c_n = w.shape
        tr = min(r_n, 256)
        grid = (r_n // tr,)
        blk = lambda: pl.BlockSpec((1, tr, c_n), lambda i: (0, i, 0))
    else:
        r_n, c_n = w.shape
        tc = min(c_n, 256)
        grid = (c_n // tc,)
        blk = lambda: pl.BlockSpec((r_n, tc), lambda i: (0, i))
    return pl.pallas_call(
        body, name=name, grid=grid, in_specs=[blk(), blk(), blk(), blk()], out_specs=[blk(), blk(), blk()],
        out_shape=[jax.ShapeDtypeStruct(w.shape, F32)] * 3, compiler_params=_cp("parallel"),
    )(w, g, m, v)


def _adam_small(groups, where, wmv, total):
    n, n_g = len(wmv), len(groups)
    t_g, t_r = total

    def body(*refs):
        g_refs = refs[:n_g]
        w_refs = refs[n_g:n_g + 3 * n]
        o_refs = refs[n_g + 3 * n:]
        for q in range(n):
            w_ref, m_ref, v_ref = w_refs[3 * q:3 * q + 3]
            r, c = w_ref.shape
            gi, r0 = where[q]
            g = g_refs[gi][r0:r0 + r, 0:c]
            d, m, v = _adam_math(w_ref[...], g, m_ref[...], v_ref[...])
            o_refs[4 * q][...] = g
            o_refs[4 * q + 1][...] = d
            o_refs[4 * q + 2][...] = m
            o_refs[4 * q + 3][...] = v
        o_refs[4 * n][...] = g_refs[t_g][t_r:t_r + 1, :]

    flat_wmv = [a for t in wmv for a in t]
    vm = pl.BlockSpec(memory_space=pltpu.VMEM)
    outs = pl.pallas_call(
        body, name="adam_small", in_specs=[vm] * (n_g + 3 * n), out_specs=[vm] * (4 * n + 1),
        out_shape=[jax.ShapeDtypeStruct(t[0].shape, F32) for t in wmv for _ in range(4)]
        + [jax.ShapeDtypeStruct((1, groups[t_g].shape[1]), F32)],
        compiler_params=pltpu.CompilerParams(vmem_limit_bytes=VMEM_LIMIT),
    )(*groups, *flat_wmv)
    return [tuple(outs[4 * q:4 * q + 4]) for q in range(n)], outs[4 * n]


WEIGHTS = ["norm_mix_pre", "w_in", "conv_ssm_w", "conv_ssm_b", "dt_bias", "a_log", "d_skip", "ssm_norm", "conv_lru_w",
           "conv_lru_b", "lru_wa", "lru_ba", "lru_wx", "lru_bx", "lru_lambda", "w_out", "norm_mix_post", "norm_mlp_pre",
           "w_up", "w_down", "norm_mlp_post"]
BIG = ["w_out", "w_up", "w_down", "w_in"]
IN_ROWS = W_IN_COLS // N_DEV
IN_PAD, EARLY_ROWS = 848, 880
LATE_ROWS = 1152
GRAD_LATE_ROWS = 864
CONV_SSM_COLS, CONV_LRU_COLS = 1536 // N_DEV, D // N_DEV
SMALL = [("norm_mix_pre", (1, D), 0, 0), ("ssm_norm", (1, D), 0, 1), ("conv_lru_b", (1, D), 0, 2),
         ("lru_lambda", (1, D), 0, 3), ("norm_mix_post", (1, D), 0, 4), ("norm_mlp_pre", (1, D), 0, 5),
         ("norm_mlp_post", (1, D), 0, 6), ("conv_ssm_b", (1, 1536), 1, 0), ("dt_bias", (1, NH), 2, 0),
         ("a_log", (1, NH), 2, 1), ("d_skip", (1, NH), 2, 2), ("conv_ssm_w", (4, CONV_SSM_COLS), 3, 0),
         ("conv_lru_w", (4, CONV_LRU_COLS), 4, 0), ("lru_wa", (D, 64), 5, 0), ("lru_wx", (D, 64), 5, D),
         ("lru_ba", (NH, 64), 6, 0), ("lru_bx", (NH, 64), 6, NH)]
SMALL_GROUPS = [(8, D), (1, 1536), (8, 128), (4, 1536), (4, D), (2 * D, 64), (2 * NH, 64)]


def _pad_rows(flat, mult):
    n = flat.shape[0]
    rows = -(-n // (128 * mult)) * mult
    return jnp.pad(flat, (0, rows * 128 - n)).reshape(rows, 128)


def _split3(a):
    hi = a.astype(MXU)
    r1 = a - hi.astype(F32)
    mid = r1.astype(MXU)
    lo = (r1 - mid.astype(F32)).astype(MXU)
    return jnp.stack([hi, mid, lo])


def _early_pack(a, me):
    bf = lambda t: t.astype(MXU)
    conv = lambda t, c: jnp.pad(_split3(t).reshape(12, c), ((0, 4), (0, D - c)))
    shifted = lax.dynamic_update_slice(jnp.zeros((IN_PAD, D), MXU), bf(a["w_in"][0]).T, (2 * me, 0))
    return jnp.concatenate([shifted, conv(a["conv_ssm_w"][0], CONV_SSM_COLS), conv(a["conv_lru_w"][0], CONV_LRU_COLS)],
                           axis=0)


TILE = 16
SHARD_TILES = IN_PAD // TILE
SHARD_STEP = (IN_ROWS // TILE)
SEG_TILES = ((0, 64, 0), (64, 128, 320), (128, 160, 384), (160, 161, 416), (161, 225, 64), (225, 289, 128),
             (289, 353, 192), (353, 417, 256))


def _tile_runs(lo, hi):
    runs = []
    for s0, s1, d0 in SEG_TILES:
        a, b = max(lo, s0), min(hi, s1)
        if a < b:
            runs.append((a, b - a, d0 + a - s0))
    return runs


def _assemble_w_in(g):
    whole = []
    for k in range(N_DEV):
        lo = SHARD_STEP * k + (1 if k else 0)
        hi = SHARD_STEP * (k + 1) + (1 if k == N_DEV - 1 else 0)
        whole += [(k, a - SHARD_STEP * k, n, d) for a, n, d in _tile_runs(lo, hi)]
    split = [(k, _tile_runs(SHARD_STEP * k, SHARD_STEP * k + 1)[0][2]) for k in range(1, N_DEV)]

    def body(g_ref, o_ref):
        rows = lambda t, n=1: pl.ds(TILE * t, TILE * n)
        for k, t, n, d in whole:
            o_ref[rows(d, n), :] = g_ref[k, rows(t, n), :]
        for k, d in split:
            o_ref[rows(d), :] = g_ref[k - 1, rows(SHARD_STEP), :] + g_ref[k, rows(0), :]
        o_ref[pl.ds(W_IN_COLS, NP - W_IN_COLS), :] = jnp.zeros((NP - W_IN_COLS, D), o_ref.dtype)

    vm = pl.BlockSpec(memory_space=pltpu.VMEM)
    return pl.pallas_call(
        body, name="assemble_w_in", in_specs=[vm], out_specs=vm, out_shape=jax.ShapeDtypeStruct((NP, D), g.dtype),
        compiler_params=pltpu.CompilerParams(vmem_limit_bytes=VMEM_LIMIT),
    )(g)


def _scatter_w_in_grad(dw):
    runs = [(k, a - SHARD_STEP * k, n, d) for k in range(N_DEV)
            for a, n, d in _tile_runs(SHARD_STEP * k, SHARD_STEP * k + SHARD_TILES)]
    pad = GRAD_LATE_ROWS - IN_PAD

    def body(dw_ref, o_ref):
        rows = lambda t, n: pl.ds(TILE * t, TILE * n)
        for k, t, n, d in runs:
            o_ref[k, rows(t, n), :] = dw_ref[rows(d, n), :]
        for k in range(N_DEV):
            o_ref[k, pl.ds(IN_PAD, pad), :] = jnp.zeros((pad, D), o_ref.dtype)

    vm = pl.BlockSpec(memory_space=pltpu.VMEM)
    return pl.pallas_call(
        body, name="scatter_w_in_grad", in_specs=[vm], out_specs=vm,
        out_shape=jax.ShapeDtypeStruct((N_DEV, GRAD_LATE_ROWS, D), dw.dtype),
        compiler_params=pltpu.CompilerParams(vmem_limit_bytes=VMEM_LIMIT),
    )(dw)


def _early_unpack(g):
    w_in_pt = _assemble_w_in(g)
    conv = {}
    for n, r0, c in (("conv_ssm_w", IN_PAD, CONV_SSM_COLS), ("conv_lru_w", IN_PAD + 16, CONV_LRU_COLS)):
        s = g[:, r0:r0 + 12, :c].astype(F32).reshape(N_DEV, 3, 4, c)
        conv[n] = ((s[:, 0] + s[:, 1]) + s[:, 2]).transpose(1, 0, 2).reshape(4, N_DEV * c)
    return w_in_pt, conv


def _late_pack(a):
    bf = lambda t: t.astype(MXU)
    return jnp.concatenate([bf(a["w_up"][0]).T, bf(a["w_down"][0]), bf(a["w_out"][0])], axis=0)


def _late_unpack(g):
    return dict(w_upT=(g, 0), w_down=(g, 1), w_out=(g, 2 * FF // D))


def _own_slot(land, own):
    me = 4 * lax.axis_index("x") + 2 * lax.axis_index("y") + lax.axis_index("c")
    return lax.dynamic_update_slice_in_dim(land, own[None], me, axis=0)


def kernel(x, norm_mix_pre, w_in, conv_ssm_w, conv_ssm_b, dt_bias, a_log, d_skip, ssm_norm, conv_lru_w, conv_lru_b, lru_wa, lru_ba, lru_wx, lru_bx, lru_lambda, w_out, norm_mix_post, norm_mlp_pre, w_up, w_down, norm_mlp_post, loss_target, m_norm_mix_pre, m_w_in, m_conv_ssm_w, m_conv_ssm_b, m_dt_bias, m_a_log, m_d_skip, m_ssm_norm, m_conv_lru_w, m_conv_lru_b, m_lru_wa, m_lru_ba, m_lru_wx, m_lru_bx, m_lru_lambda, m_w_out, m_norm_mix_post, m_norm_mlp_pre, m_w_up, m_w_down, m_norm_mlp_post, v_norm_mix_pre, v_w_in, v_conv_ssm_w, v_conv_ssm_b, v_dt_bias, v_a_log, v_d_skip, v_ssm_norm, v_conv_lru_w, v_conv_lru_b, v_lru_wa, v_lru_ba, v_lru_wx, v_lru_bx, v_lru_lambda, v_w_out, v_norm_mix_post, v_norm_mlp_pre, v_w_up, v_w_down, v_norm_mlp_post):
    vals = (norm_mix_pre, w_in, conv_ssm_w, conv_ssm_b, dt_bias, a_log, d_skip, ssm_norm, conv_lru_w, conv_lru_b, lru_wa, lru_ba, lru_wx, lru_bx, lru_lambda, w_out, norm_mix_post, norm_mlp_pre, w_up, w_down, norm_mlp_post)
    m_vals = (m_norm_mix_pre, m_w_in, m_conv_ssm_w, m_conv_ssm_b, m_dt_bias, m_a_log, m_d_skip, m_ssm_norm, m_conv_lru_w, m_conv_lru_b, m_lru_wa, m_lru_ba, m_lru_wx, m_lru_bx, m_lru_lambda, m_w_out, m_norm_mix_post, m_norm_mlp_pre, m_w_up, m_w_down, m_norm_mlp_post)
    v_vals = (v_norm_mix_pre, v_w_in, v_conv_ssm_w, v_conv_ssm_b, v_dt_bias, v_a_log, v_d_skip, v_ssm_norm, v_conv_lru_w, v_conv_lru_b, v_lru_wa, v_lru_ba, v_lru_wx, v_lru_bx, v_lru_lambda, v_w_out, v_norm_mix_post, v_norm_mlp_pre, v_w_up, v_w_down, v_norm_mlp_post)
    w = dict(zip(WEIGHTS, vals))
    m = dict(zip(WEIGHTS, m_vals))
    v = dict(zip(WEIGHTS, v_vals))
    me = 4 * lax.axis_index("x") + 2 * lax.axis_index("y") + lax.axis_index("c")

    bf = lambda t: t.astype(MXU)
    late = _late_pack(w)
    early = _all_gather(_early_pack(w, me), "early_weights_all_gather")
    late, early = lax.optimization_barrier((late, early))
    lw = {}
    lw["sems"], lw["src"], lw["land"], token = _exchange_start(late, None, PLAN_GATHER_ICI, "late_weights_ici_start")
    w_in_pt, conv_w = _early_unpack(early)
    full = {n: (conv_w[n] if n in conv_w else w[n][0]) for n in WEIGHTS if n not in BIG}
    full["norm_mix_pre"] = full["norm_mix_pre"] + token[0, 0]

    def after_ssd(after):
        src, land = _exchange_wait(lw["sems"], lw["src"], lw["land"], after, PLAN_GATHER_ICI, "late_weights_ici_wait")
        lw["sems"], lw["src"], lw["land"], tok = _exchange_start(src, land, PLAN_GATHER_D2D, "late_weights_d2d_start")
        return tok[0, 0]

    def late_weights(after):
        src, land = _exchange_wait(lw["sems"], lw["src"], lw["land"], after, PLAN_GATHER_D2D, "late_weights_d2d_wait")
        return _late_unpack(_own_slot(land, src))

    sent = {}

    def send_mlp_grads(d_w_up_t, d_w_down, d_w_out):
        srcs = [bf(d).reshape(N_DEV, -1, D) for d in (d_w_up_t, d_w_down, d_w_out)]
        sent["sems"], sent["srcs"], sent["land"], tok = _scatter_parts_start(srcs, "mlp_grads_start")
        return tok[0, 0]

    def send_late_grads(d_w_in_pt):
        src = _scatter_w_in_grad(d_w_in_pt)
        chip = _pair_sum(src, _pair_exchange(src, "late_grads_pair_exchange"), "late_grads_pair_sum")
        sent["sems2"], sent["src2"], sent["land2"], tok = _exchange_start(chip, None, PLAN_SCATTER_CHIPS,
                                                                          "late_grads_start")
        return tok[0, 0]

    loss, grad_x, g = _local_step(x[0], loss_target[0], _prep_params(full, dict(w_in_pT=w_in_pt)), after_ssd,
                                  late_weights, send_mlp_grads, send_late_grads)

    own = lambda src: lax.dynamic_index_in_dim(src, me, keepdims=False)
    out_g, out_d, out_m, out_v = {}, {}, {}, {}
    mlp_srcs, mlp_land = _scatter_parts_wait(sent["sems"], sent["srcs"], sent["land"], grad_x, "mlp_grads_wait")
    g_mlp = _slot_sum(_own_slot(mlp_land, jnp.concatenate([own(s) for s in mlp_srcs], axis=0)), "slot_sum_mlp")
    fs = FF // N_DEV
    for n, gn in (("w_up", g_mlp[:fs].T[None]), ("w_down", g_mlp[fs:2 * fs][None]), ("w_out", g_mlp[2 * fs:][None])):
        out_g[n] = gn
        out_d[n], out_m[n], out_v[n] = _adam_big(w[n], gn, m[n], v[n], "adam_" + n)
    zrow = jnp.zeros((1, D), F32)
    pad16 = lambda a: jnp.pad(a, ((0, 0), (0, 128 - NH)))
    small_parts = [
        jnp.concatenate([g["norm_mix_pre"], g["ssm_norm"], g["conv_lru_b"], g["lru_lambda"], g["norm_mix_post"],
                         g["norm_mlp_pre"], g["norm_mlp_post"], zrow], axis=0),
        g["conv_ssm_b"],
        jnp.concatenate([pad16(g["dt_bias"]), pad16(g["a_log"]), pad16(g["d_skip"]), jnp.full((1, 128), loss, F32),
                         jnp.zeros((4, 128), F32)], axis=0),
        g["conv_ssm_w"], g["conv_lru_w"],
        jnp.concatenate([g["lru_wa"].reshape(D, 64), g["lru_wx"].reshape(D, 64)], axis=0),
        jnp.concatenate([g["lru_ba"].reshape(NH, 64), g["lru_bx"].reshape(NH, 64)], axis=0)]
    small = _pad_rows(jnp.concatenate([s.reshape(-1) for s in small_parts]), 64)
    small, _ = lax.optimization_barrier((small, out_v["w_out"]))
    small_all = _all_gather(small, "small_grads_all_gather")

    lg_src, lg_land = _exchange_wait(sent["sems2"], sent["src2"], sent["land2"], small_all, PLAN_SCATTER_CHIPS,
                                     "late_grads_wait")
    my_chip = me >> 1
    lg_own = lax.dynamic_index_in_dim(lg_src, my_chip, keepdims=True)
    g_late = _slot_sum(lax.dynamic_update_slice_in_dim(lg_land, lg_own, my_chip, axis=0), "slot_sum_late")
    gt = lax.dynamic_slice(g_late, (2 * me, 0), (IN_ROWS, D))
    dt_, mt_, vt_ = _adam_big(w["w_in"][0].T, gt, m["w_in"][0].T, v["w_in"][0].T, "adam_w_in")
    out_g["w_in"], out_d["w_in"], out_m["w_in"], out_v["w_in"] = gt.T[None], dt_.T[None], mt_.T[None], vt_.T[None]
    sflat = _slot_sum(small_all, "slot_sum_small").reshape(-1)
    groups = []
    off = 0
    for r, c in SMALL_GROUPS:
        groups.append(sflat[off:off + r * c].reshape(r, c))
        off += r * c
    groups[3] = lax.dynamic_slice_in_dim(groups[3], me * CONV_SSM_COLS, CONV_SSM_COLS, axis=1)
    groups[4] = lax.dynamic_slice_in_dim(groups[4], me * CONV_LRU_COLS, CONV_LRU_COLS, axis=1)
    wmv = [(w[n].reshape(s), m[n].reshape(s), v[n].reshape(s)) for n, s, _, _ in SMALL]
    res, loss_row = _adam_small(groups, [(gi, r0) for _, _, gi, r0 in SMALL], wmv, (2, 3))
    loss = loss_row[0, 0]
    for (n, _, _, _), (g_n, d_n, m_n, v_n) in zip(SMALL, res):
        shape = w[n].shape
        out_g[n], out_d[n], out_m[n], out_v[n] = (g_n.reshape(shape), d_n.reshape(shape), m_n.reshape(shape),
                                                  v_n.reshape(shape))
    return (loss, grad_x[None], *[out_g[n] for n in WEIGHTS], *[out_d[n] for n in WEIGHTS],
            *[out_m[n] for n in WEIGHTS], *[out_v[n] for n in WEIGHTS])
```

```python
import functools

import jax
import jax.numpy as jnp
from jax import lax
from jax.experimental import pallas as pl
from jax.experimental.pallas import tpu as pltpu

F32 = jnp.float32
MXU = jnp.bfloat16
HI = lax.Precision.HIGHEST
EPS = 1e-6

D = 1024
NH = 16
NS = 128
CH = 128
FF = 4096
NP = 7168
SEG = 1024
LRU_C = 8.0
N_DEV = 8

ADAM_LR, ADAM_B1, ADAM_B2, ADAM_EPS, ADAM_WD, ADAM_STEP = 0.001, 0.9, 0.999, 1e-08, 0.01, 10

VMEM_LIMIT = 56 * 1024 * 1024


def _cp(*sem):
    return pltpu.CompilerParams(dimension_semantics=sem, vmem_limit_bytes=VMEM_LIMIT)


def _nn(a, b):
    return jnp.dot(a.astype(MXU), b.astype(MXU), preferred_element_type=F32)


def _nt(a, b):
    return lax.dot_general(a.astype(MXU), b.astype(MXU), (((1,), (1,)), ((), ())), preferred_element_type=F32)


def _tn(a, b):
    return lax.dot_general(a.astype(MXU), b.astype(MXU), (((0,), (0,)), ((), ())), preferred_element_type=F32)


_sigmoid = jax.nn.sigmoid


def _silu(x):
    return x * _sigmoid(x)


def _dsilu(x):
    s = _sigmoid(x)
    return s + x * s * (1.0 - s)


def _softplus(x):
    return jnp.maximum(x, 0.0) + jnp.log(1.0 + jnp.exp(-jnp.abs(x)))


def _rms(x, g):
    r = lax.rsqrt(jnp.mean(x * x, axis=-1, keepdims=True) + EPS)
    return x * r * g


def _rms_bwd(x, g, dy):
    r = lax.rsqrt(jnp.mean(x * x, axis=-1, keepdims=True) + EPS)
    gdy = g * dy
    dx = r * gdy - x * (r * r * r) * jnp.mean(x * gdy, axis=-1, keepdims=True)
    return dx, dy * x * r


def _rowsum(x):
    return jnp.sum(x, axis=0, keepdims=True)


def _taps_past(cur, prev8):
    r_n, c_n = cur.shape
    row = lax.broadcasted_iota(jnp.int32, (r_n, c_n), 0)
    out = []
    for k in range(4):
        s = 3 - k
        if s == 0:
            out.append(cur)
            continue
        head = jnp.concatenate([pltpu.roll(prev8, s, 0), jnp.zeros((r_n - 8, c_n), F32)], axis=0)
        out.append(jnp.where(row < s, head, pltpu.roll(cur, s, 0)))
    return out


def _taps_future(cur, fut8):
    r_n, c_n = cur.shape
    row = lax.broadcasted_iota(jnp.int32, (r_n, c_n), 0)
    out = []
    for k in range(4):
        s = 3 - k
        if s == 0:
            out.append(cur)
            continue
        tail = jnp.concatenate([jnp.zeros((r_n - 8, c_n), F32), pltpu.roll(fut8, 8 - s, 0)], axis=0)
        out.append(jnp.where(row >= r_n - s, tail, pltpu.roll(cur, r_n - s, 0)))
    return out


def _conv_apply(taps, w, b):
    acc = taps[0] * w[0:1, :]
    for k in range(1, 4):
        acc = acc + taps[k] * w[k:k + 1, :]
    return acc + b


def _inproj(x, g0, w):
    t_n = x.shape[0]
    tm = min(t_n, 1024)

    n_j = NP // SEG

    def body(x_ref, g_ref, w_ref, pb_ref, p6_ref, u_ref):
        j = pl.program_id(1)

        @pl.when(j == 0)
        def _():
            u_ref[...] = _rms(x_ref[...], g_ref[...]).astype(MXU)

        p = lax.dot_general(u_ref[...], w_ref[...], (((1,), (1,)), ((), ())), preferred_element_type=F32)

        @pl.when(j < n_j - 1)
        def _():
            pb_ref[...] = p.astype(MXU)

        @pl.when(j == n_j - 1)
        def _():
            p6_ref[...] = p

    pb, p6, u = pl.pallas_call(
        body, name="inproj", grid=(t_n // tm, n_j),
        in_specs=[pl.BlockSpec((tm, D), lambda i, j: (i, 0)), pl.BlockSpec((1, D), lambda i, j: (0, 0)),
                  pl.BlockSpec((SEG, D), lambda i, j: (j, 0))],
        out_specs=[pl.BlockSpec((tm, SEG), lambda i, j: (i, jnp.minimum(j, n_j - 2))),
                   pl.BlockSpec((tm, SEG), lambda i, j: (i, 0)), pl.BlockSpec((tm, D), lambda i, j: (i, 0))],
        out_shape=[jax.ShapeDtypeStruct((t_n, NP - SEG), MXU), jax.ShapeDtypeStruct((t_n, SEG), F32),
                   jax.ShapeDtypeStruct((t_n, D), MXU)],
        compiler_params=_cp("parallel", "arbitrary"),
    )(x, g0, w)
    return (pb, p6), u


def _ssd_prep(dtraw, dtb, alog):
    l_n = dtraw.shape[0]
    r = lax.broadcasted_iota(jnp.int32, (l_n, l_n), 0)
    c = lax.broadcasted_iota(jnp.int32, (l_n, l_n), 1)
    tril = (r >= c).astype(F32)
    triu = (r <= c).astype(F32)
    eye = (r == c).astype(F32)
    dt = _softplus(dtraw + dtb)
    adt = dt * (-jnp.exp(alog))
    ac = jnp.dot(tril, adt, preferred_element_type=F32, precision=HI)
    tn = (((0,), (0,)), ((), ()))
    ac_t = lax.dot_general(adt, triu, tn, preferred_element_type=F32, precision=HI)
    dt_t = lax.dot_general(dt, eye, tn, preferred_element_type=F32, precision=HI)
    return dt, dt_t, ac, ac_t, _rowsum(adt)


def _ssd_pair(j, xp, bg, cg, sp, dt, dt_t, ac, ac_t, aend):
    l_n = xp.shape[0]
    lane = lax.broadcasted_iota(jnp.int32, (l_n, 128), 1)
    sub = lax.broadcasted_iota(jnp.int32, (128, l_n), 0)
    lane1 = lax.broadcasted_iota(jnp.int32, (1, 128), 1)
    tri = lax.broadcasted_iota(jnp.int32, (l_n, l_n), 0) >= lax.broadcasted_iota(jnp.int32, (l_n, l_n), 1)
    lo = lax.broadcasted_iota(jnp.int32, (l_n, 128), 1) < 64
    lo_s = lax.broadcasted_iota(jnp.int32, (128, 128), 1) < 64
    cb = _nt(cg, bg)
    cs = _nn(cg, sp)
    x2 = jnp.concatenate([jnp.where(lo, xp, 0.0), jnp.where(lo, 0.0, xp)], axis=0)
    ws_, bs_, eo, ee = [], [], [], []
    for e in range(2):
        h = 2 * j + e
        ac_l = jnp.sum(jnp.where(lane == h, ac, 0.0), axis=1, keepdims=True)
        dt_l = jnp.sum(jnp.where(lane == h, dt, 0.0), axis=1, keepdims=True)
        a_end = jnp.sum(jnp.where(lane1 == h, aend, 0.0), axis=1, keepdims=True)
        ac_s, dt_s = ac_t[h:h + 1, :], dt_t[h:h + 1, :]
        decay = jnp.exp(jnp.where(tri, ac_l - ac_s, -1e30))
        ws_.append(cb * decay * dt_s)
        bs_.append(bg * (jnp.exp(a_end - ac_l) * dt_l))
        eo.append(jnp.exp(ac_l))
        ee.append(jnp.exp(a_end))
    y = _nn(jnp.concatenate(ws_, axis=1), x2) + jnp.where(lo, eo[0], eo[1]) * cs
    s_new = _tn(jnp.concatenate(bs_, axis=0), x2) + jnp.where(lo_s, ee[0], ee[1]) * sp
    return y, s_new


def _ssd_post(y, xs, z, dsk, nrm):
    y = (y + dsk * xs) * _silu(z)
    half = D // 2
    ya, yb = y[:, :half], y[:, half:]
    ya = ya * lax.rsqrt(jnp.mean(ya * ya, axis=-1, keepdims=True) + EPS)
    yb = yb * lax.rsqrt(jnp.mean(yb * yb, axis=-1, keepdims=True) + EPS)
    return jnp.concatenate([ya, yb], axis=1) * nrm


LAST_SEG = NP // SEG - 1


def _proj_ops(proj, seg_ids):
    return [proj[1] if s == LAST_SEG else proj[0] for s in seg_ids]


def _proj_specs(rows, seg_ids, order):
    return [pl.BlockSpec((rows, SEG), functools.partial(lambda i, c: (order(i), c), c=0 if s == LAST_SEG else s))
            for s in seg_ids]


def _prev_specs(rows, seg_ids, order):
    specs = []
    for s in seg_ids:
        n, c = (8, 0) if s == LAST_SEG else (16, s)
        specs.append(pl.BlockSpec((n, SEG), functools.partial(
            lambda i, n, c: (jnp.maximum(order(i) * (rows // n) - 1, 0), c), n=n, c=c)))
    return specs


def _prev8(ref):
    return ref[...] if ref.shape[0] == 8 else ref[8:16, :].astype(F32)


def _full(shape):
    return pl.BlockSpec(shape, lambda i: (0,) * len(shape))


def _slotted(a):
    return a if isinstance(a, tuple) else (a.reshape(N_DEV, a.shape[0] // N_DEV, a.shape[1]), 0)


def _slot_spec(w, rows, tile, index):
    per = rows // N_DEV
    b = w[1]
    return pl.BlockSpec((tile // per, per, D), lambda *ids: (index(*ids), b, 0))


def _ssd_fwd(proj, cwx, cwb, cbx, cbb, dtb, alog, dsk, nrm):
    t_n = proj[0].shape[0]
    n_c = t_n // CH
    fwd = lambda i: i

    def body(z_ref, xs_ref, bc_ref, xsp_ref, bcp_ref, cwx_ref, cwb_ref, cbx_ref, cbb_ref, dtb_ref, alog_ref,
             dsk_ref, nrm_ref, ya_ref, sprev_ref, yraw_ref, xspre_ref, bcpre_ref, s_ref):
        c = pl.program_id(0)

        @pl.when(c == 0)
        def _():
            s_ref[...] = jnp.zeros_like(s_ref)

        keep = jnp.where(c == 0, 0.0, 1.0)
        xs_pre = _conv_apply(_taps_past(xs_ref[...].astype(F32), _prev8(xsp_ref) * keep), cwx_ref[...], cbx_ref[...])
        bc_pre = _conv_apply(_taps_past(bc_ref[:, :512], bcp_ref[:, :512] * keep), cwb_ref[...], cbb_ref[...])
        xspre_ref[...] = xs_pre
        bcpre_ref[...] = bc_pre
        prep = _ssd_prep(bc_ref[:, 512:640], dtb_ref[...], alog_ref[...])
        xs = _silu(xs_pre)
        bc = _silu(bc_pre)
        sprev_ref[0] = s_ref[...]
        ys = []
        for j in range(NH // 2):
            g = j // 4
            yp, sn = _ssd_pair(j, xs[:, 128 * j:128 * j + 128], bc[:, 128 * g:128 * g + 128],
                               bc[:, 256 + 128 * g:384 + 128 * g], s_ref[:, 128 * j:128 * j + 128], *prep)
            ys.append(yp)
            s_ref[:, 128 * j:128 * j + 128] = sn
        y = jnp.concatenate(ys, axis=1)
        yraw_ref[...] = y
        ya_ref[...] = _ssd_post(y, xs, z_ref[...].astype(F32), dsk_ref[...], nrm_ref[...]).astype(ya_ref.dtype)

    return pl.pallas_call(
        body, name="ssd_fwd", grid=(n_c,),
        in_specs=_proj_specs(CH, (0, 5, 6), fwd) + _prev_specs(CH, (5, 6), fwd) + [
            _full((4, D)), _full((4, 512)), _full((1, D)), _full((1, 512)), _full((1, 128)), _full((1, 128)),
            _full((1, D)), _full((1, D))],
        out_specs=[pl.BlockSpec((CH, D), lambda i: (i, 0)), pl.BlockSpec((1, NS, D), lambda i: (i, 0, 0)),
                   pl.BlockSpec((CH, D), lambda i: (i, 0)), pl.BlockSpec((CH, D), lambda i: (i, 0)),
                   pl.BlockSpec((CH, 512), lambda i: (i, 0))],
        out_shape=[jax.ShapeDtypeStruct((t_n, D), MXU), jax.ShapeDtypeStruct((n_c, NS, D), F32),
                   jax.ShapeDtypeStruct((t_n, D), F32), jax.ShapeDtypeStruct((t_n, D), F32),
                   jax.ShapeDtypeStruct((t_n, 512), F32)],
        scratch_shapes=[pltpu.VMEM((NS, D), F32)],
        compiler_params=_cp("arbitrary"),
    )(*_proj_ops(proj, (0, 5, 6, 5, 6)), cwx, cwb, cbx, cbb, dtb, alog, dsk, nrm)


def _ssd_bwd(dya, saved, proj, sprev, cwx, cwb, dtb, alog, dsk, nrm, mlp_ops):
    t_n = proj[0].shape[0]
    n_c = t_n // CH
    rev = lambda i: n_c - 1 - i
    fb = FF // n_c

    def body(dya_ref, yraw_ref, xspre_ref, bcpre_ref, z_ref, xs_ref, bc_ref, sprev_ref, cwx_ref, cwb_ref,
             dtb_ref, alog_ref, dsk_ref, nrm_ref, hid_ref, dff_ref, dhp_ref, v_ref,
             dz_ref, dxs_ref, dbc_ref, dcwx_ref, dcwb_ref, dcbx_ref, dcbb_ref, ddtb_ref, dalog_ref, ddsk_ref,
             dnrm_ref, dwd_ref, dwu_ref, ds_ref, futx_ref, futb_ref):
        i = pl.program_id(0)
        acc_refs = (dcwx_ref, dcwb_ref, dcbx_ref, dcbb_ref, ddtb_ref, dalog_ref, ddsk_ref, dnrm_ref)
        tn = (((0,), (0,)), ((), ()))
        kt = t_n // (NH // 2)
        dwd_acc = jnp.zeros((fb, D), F32)
        dwu_acc = jnp.zeros((fb, D), F32)

        @pl.when(i == 0)
        def _():
            for r in (ds_ref, futx_ref, futb_ref) + acc_refs:
                r[...] = jnp.zeros_like(r)

        xs_pre = xspre_ref[...]
        bc_pre = bcpre_ref[...]
        xs = _silu(xs_pre)
        bc = _silu(bc_pre)
        prep, prep_vjp = jax.vjp(_ssd_prep, bc_ref[:, 512:640], dtb_ref[...], alog_ref[...])
        s_in = sprev_ref[0]

        def pair_args(j):
            g = j // 4
            return (xs[:, 128 * j:128 * j + 128], bc[:, 128 * g:128 * g + 128],
                    bc[:, 256 + 128 * g:384 + 128 * g], s_in[:, 128 * j:128 * j + 128]) + tuple(prep)

        _, post_vjp = jax.vjp(_ssd_post, yraw_ref[...], xs, z_ref[...].astype(F32), dsk_ref[...], nrm_ref[...])
        dy, dxs_skip, dz, ddsk, dnrm = post_vjp(dya_ref[...])
        dz_ref[...] = dz.astype(dz_ref.dtype)
        ddsk_ref[...] += ddsk
        dnrm_ref[...] += dnrm

        dprep = [jnp.zeros_like(p) for p in prep]
        dxp = []
        dbg = [jnp.zeros((CH, 128), F32), jnp.zeros((CH, 128), F32)]
        dcg = [jnp.zeros((CH, 128), F32), jnp.zeros((CH, 128), F32)]
        for j in range(NH // 2):
            g = j // 4
            _, pair_vjp = jax.vjp(functools.partial(_ssd_pair, j), *pair_args(j))
            cts = pair_vjp((dy[:, 128 * j:128 * j + 128], ds_ref[:, 128 * j:128 * j + 128]))
            dxp.append(cts[0])
            dbg[g] = dbg[g] + cts[1]
            dcg[g] = dcg[g] + cts[2]
            ds_ref[:, 128 * j:128 * j + 128] = cts[3]
            dprep = [a + b for a, b in zip(dprep, cts[4:])]
            rows = pl.ds(kt * j, kt)
            dwd_acc = dwd_acc + lax.dot_general(hid_ref[rows, :], dff_ref[rows, :], tn, preferred_element_type=F32)
            dwu_acc = dwu_acc + lax.dot_general(dhp_ref[rows, :], v_ref[rows, :], tn, preferred_element_type=F32)
        dwd_ref[...] = dwd_acc.astype(MXU)
        dwu_ref[...] = dwu_acc.astype(MXU)
        ddtraw, ddtb, dalog = prep_vjp(tuple(dprep))
        ddtb_ref[...] += ddtb
        dalog_ref[...] += dalog

        dxs_pre = (dxs_skip + jnp.concatenate(dxp, axis=1)) * _dsilu(xs_pre)
        dbc_pre = jnp.concatenate([dbg[0], dbg[1], dcg[0], dcg[1]], axis=1) * _dsilu(bc_pre)
        dcbx_ref[...] += _rowsum(dxs_pre)
        dcbb_ref[...] += _rowsum(dbc_pre)
        fx = _taps_future(dxs_pre, futx_ref[...])
        fbc = _taps_future(dbc_pre, futb_ref[...])
        xs_in = xs_ref[...].astype(F32)
        bc_in = bc_ref[:, :512]
        for k in range(4):
            dcwx_ref[k:k + 1, :] += _rowsum(fx[k] * xs_in)
            dcwb_ref[k:k + 1, :] += _rowsum(fbc[k] * bc_in)
        cwx = cwx_ref[...]
        cwb = cwb_ref[...]
        dxs_in = fx[0] * cwx[0:1, :]
        dbc_in = fbc[0] * cwb[0:1, :]
        for k in range(1, 4):
            dxs_in = dxs_in + fx[k] * cwx[k:k + 1, :]
            dbc_in = dbc_in + fbc[k] * cwb[k:k + 1, :]
        futx_ref[...] = dxs_pre[0:8, :]
        futb_ref[...] = dbc_pre[0:8, :]
        dxs_ref[...] = dxs_in.astype(dxs_ref.dtype)
        dbc_ref[...] = jnp.concatenate([dbc_in, ddtraw, jnp.zeros((CH, SEG - 640), F32)], axis=1).astype(dbc_ref.dtype)

    row_out = lambda: pl.BlockSpec((CH, D), lambda i: (rev(i), 0))
    outs = pl.pallas_call(
        body, name="ssd_bwd", grid=(n_c,),
        in_specs=[row_out(), row_out(), row_out(), pl.BlockSpec((CH, 512), lambda i: (rev(i), 0))]
        + _proj_specs(CH, (0, 5, 6), rev) + [pl.BlockSpec((1, NS, D), lambda i: (rev(i), 0, 0)),
                                             _full((4, D)), _full((4, 512)),
                                             _full((1, 128)), _full((1, 128)), _full((1, D)), _full((1, D)),
                                             pl.BlockSpec((t_n, fb), lambda i: (0, i)), _full((t_n, D)),
                                             pl.BlockSpec((t_n, fb), lambda i: (0, i)), _full((t_n, D))],
        out_specs=[row_out(), row_out(), row_out(), _full((4, D)), _full((4, 512)), _full((1, D)), _full((1, 512)),
                   _full((1, 128)), _full((1, 128)), _full((1, D)), _full((1, D)),
                   pl.BlockSpec((fb, D), lambda i: (i, 0)), pl.BlockSpec((fb, D), lambda i: (i, 0))],
        out_shape=[jax.ShapeDtypeStruct((t_n, D), MXU)] * 3 + [
            jax.ShapeDtypeStruct(s, F32) for s in ((4, D), (4, 512), (1, D), (1, 512), (1, 128), (1, 128), (1, D), (1, D))]
        + [jax.ShapeDtypeStruct((FF, D), MXU)] * 2,
        scratch_shapes=[pltpu.VMEM((NS, D), F32), pltpu.VMEM((8, D), F32), pltpu.VMEM((8, 512), F32)],
        compiler_params=_cp("arbitrary"),
    )(dya, *saved, *_proj_ops(proj, (0, 5, 6)), sprev, cwx, cwb, dtb, alog, dsk, nrm, *mlp_ops)
    return outs


LRU_ROWS = 256
LRU_BLK = 256


def _lru_gates(xr, wa, wx, ba, bx, lam):
    pr = jnp.concatenate([_nn(xr[:, LRU_BLK * b:LRU_BLK * (b + 1)], wa[b]) for b in range(D // LRU_BLK)], axis=1) + ba
    pi = jnp.concatenate([_nn(xr[:, LRU_BLK * b:LRU_BLK * (b + 1)], wx[b]) for b in range(D // LRU_BLK)], axis=1) + bx
    log_a = -LRU_C * _sigmoid(pr) * _softplus(-lam)
    a = jnp.exp(log_a)
    mult = jnp.sqrt(1.0 - jnp.exp(2.0 * log_a))
    return a, mult * (_sigmoid(pi) * xr)


def _lru_out(h, g):
    return h * jax.nn.gelu(g, approximate=True)


def _lru_fwd(proj, cw, cb, wa, wx, ba, bx, lam):
    t_n = proj[0].shape[0]
    rows = min(LRU_ROWS, t_n)
    fwd = lambda i: i

    def body(g_ref, x_ref, xp_ref, cw_ref, cb_ref, wa_ref, wx_ref, ba_ref, bx_ref, lam_ref, yb_ref, h_ref, xr_ref,
             a_s, u_s, carry):
        i = pl.program_id(0)

        @pl.when(i == 0)
        def _():
            carry[...] = jnp.zeros_like(carry)

        keep = jnp.where(i == 0, 0.0, 1.0)
        xr = _conv_apply(_taps_past(x_ref[...].astype(F32), _prev8(xp_ref) * keep), cw_ref[...], cb_ref[...])
        xr_ref[...] = xr
        a, u = _lru_gates(xr, wa_ref[...], wx_ref[...], ba_ref[...], bx_ref[...], lam_ref[...])
        a_s[...] = a
        u_s[...] = u
        row = lax.broadcasted_iota(jnp.int32, (8, D), 0)

        def blk(b, c):
            s = pl.multiple_of(b * 8, 8)
            av = a_s[pl.ds(s, 8), :]
            uv = u_s[pl.ds(s, 8), :]
            for d in (1, 2, 4):
                m = row >= d
                uv = uv + av * jnp.where(m, pltpu.roll(uv, d, 0), 0.0)
                av = av * jnp.where(m, pltpu.roll(av, d, 0), 1.0)
            hv = uv + av * c
            h_ref[pl.ds(s, 8), :] = hv
            return hv[7:8, :]

        carry[0:1, :] = lax.fori_loop(0, rows // 8, blk, carry[0:1, :])
        yb_ref[...] = _lru_out(h_ref[...], g_ref[...].astype(F32)).astype(yb_ref.dtype)

    return pl.pallas_call(
        body, name="lru_fwd", grid=(t_n // rows,),
        in_specs=_proj_specs(rows, (1, 2), fwd) + _prev_specs(rows, (2,), fwd) + [
            _full((4, D)), _full((1, D)), _full((4, LRU_BLK, LRU_BLK)), _full((4, LRU_BLK, LRU_BLK)),
            _full((1, D)), _full((1, D)), _full((1, D))],
        out_specs=[pl.BlockSpec((rows, D), lambda i: (i, 0))] * 3,
        out_shape=[jax.ShapeDtypeStruct((t_n, D), MXU), jax.ShapeDtypeStruct((t_n, D), F32),
                   jax.ShapeDtypeStruct((t_n, D), F32)],
        scratch_shapes=[pltpu.VMEM((rows, D), F32), pltpu.VMEM((rows, D), F32), pltpu.VMEM((8, D), F32)],
        compiler_params=_cp("arbitrary"),
    )(*_proj_ops(proj, (1, 2, 2)), cw, cb, wa, wx, ba, bx, lam)


def _lru_bwd(dyb, proj, h, xr_saved, cw, wa, wx, ba, bx, lam):
    t_n = proj[0].shape[0]
    rows = min(LRU_ROWS, t_n)
    n_t = t_n // rows
    rev = lambda i: n_t - 1 - i
    rb = rows // 8

    def body(dyb_ref, g_ref, x_ref, h_ref, hp_ref, xr_ref, cw_ref, wa_ref, wx_ref, ba_ref, bx_ref, lam_ref,
             dg_ref, dx_ref, dcw_ref, dcb_ref, dwa_ref, dwx_ref, dba_ref, dbx_ref, dlam_ref,
             a_s, dh_s, hx_s, da_s, du_s, carry, fut):
        i = pl.program_id(0)
        acc_refs = (dcw_ref, dcb_ref, dwa_ref, dwx_ref, dba_ref, dbx_ref, dlam_ref)

        @pl.when(i == 0)
        def _():
            for r in (carry, fut) + acc_refs:
                r[...] = jnp.zeros_like(r)

        keep = jnp.where(i == n_t - 1, 0.0, 1.0)
        gate_in = (xr_ref[...], wa_ref[...], wx_ref[...], ba_ref[...], bx_ref[...], lam_ref[...])
        (a, _), gates_vjp = jax.vjp(_lru_gates, *gate_in)
        _, out_vjp = jax.vjp(_lru_out, h_ref[...], g_ref[...].astype(F32))
        dh, dg = out_vjp(dyb_ref[...])
        dg_ref[...] = dg.astype(dg_ref.dtype)
        a_s[...] = a
        dh_s[...] = dh
        hx_s[0:8, :] = hp_ref[...] * keep
        hx_s[8:, :] = h_ref[...]
        row = lax.broadcasted_iota(jnp.int32, (8, D), 0)

        def blk(b, c):
            s = pl.multiple_of((rb - 1 - b) * 8, 8)
            av = a_s[pl.ds(s, 8), :]
            dhv = dh_s[pl.ds(s, 8), :]
            kv = av * dhv
            for d in (1, 2, 4):
                m = row <= 7 - d
                kv = kv + av * jnp.where(m, pltpu.roll(kv, 8 - d, 0), 0.0)
                av = av * jnp.where(m, pltpu.roll(av, 8 - d, 0), 1.0)
            kv = kv + av * c
            gv = dhv + jnp.where(row < 7, pltpu.roll(kv, 7, 0), c)
            hb = hx_s[pl.ds(s + 8, 8), :]
            hpv = hx_s[pl.ds(s, 8), :]
            hprev = jnp.where(row >= 1, pltpu.roll(hb, 1, 0), hpv[7:8, :])
            du_s[pl.ds(s, 8), :] = gv
            da_s[pl.ds(s, 8), :] = gv * hprev
            return kv[0:1, :]

        carry[0:1, :] = lax.fori_loop(0, rb, blk, carry[0:1, :])
        dxr, dwa, dwx, dba, dbx, dlam = gates_vjp((da_s[...], du_s[...]))
        dwa_ref[...] += dwa
        dwx_ref[...] += dwx
        dba_ref[...] += dba
        dbx_ref[...] += dbx
        dlam_ref[...] += dlam
        dcb_ref[...] += _rowsum(dxr)
        ft = _taps_future(dxr, fut[...])
        x_in = x_ref[...].astype(F32)
        for k in range(4):
            dcw_ref[k:k + 1, :] += _rowsum(ft[k] * x_in)
        cwv = cw_ref[...]
        dx = ft[0] * cwv[0:1, :]
        for k in range(1, 4):
            dx = dx + ft[k] * cwv[k:k + 1, :]
        fut[...] = dxr[0:8, :]
        dx_ref[...] = dx.astype(dx_ref.dtype)

    row_in = lambda: pl.BlockSpec((rows, D), lambda i: (rev(i), 0))
    prev_h = pl.BlockSpec((8, D), lambda i: (jnp.maximum(rev(i) * rb - 1, 0), 0))
    wspec = lambda: _full((4, LRU_BLK, LRU_BLK))
    return pl.pallas_call(
        body, name="lru_bwd", grid=(n_t,),
        in_specs=[row_in()] + _proj_specs(rows, (1, 2), rev) + [row_in(), prev_h, row_in()] + [
            _full((4, D)), wspec(), wspec(), _full((1, D)), _full((1, D)), _full((1, D))],
        out_specs=[row_in(), row_in(), _full((4, D)), _full((1, D)), wspec(), wspec(), _full((1, D)), _full((1, D)),
                   _full((1, D))],
        out_shape=[jax.ShapeDtypeStruct((t_n, D), MXU)] * 2 + [
            jax.ShapeDtypeStruct(s, F32) for s in ((4, D), (1, D), (4, LRU_BLK, LRU_BLK), (4, LRU_BLK, LRU_BLK),
                                                   (1, D), (1, D), (1, D))],
        scratch_shapes=[pltpu.VMEM((rows, D), F32), pltpu.VMEM((rows, D), F32), pltpu.VMEM((rows + 8, D), F32),
                        pltpu.VMEM((rows, D), F32), pltpu.VMEM((rows, D), F32), pltpu.VMEM((8, D), F32),
                        pltpu.VMEM((8, D), F32)],
        compiler_params=_cp("arbitrary"),
    )(dyb, *_proj_ops(proj, (1, 2)), h, h, xr_saved, cw, wa, wx, ba, bx, lam)


def _merge_out(ya, yb, proj, x, wout, g1):
    t_n = x.shape[0]
    tm = min(t_n, 512)

    def body(ya_ref, yb_ref, ga_ref, gb_ref, x_ref, w_ref, g_ref, h1_ref, mix_ref, mg_ref):
        merged = (_sigmoid(ga_ref[...].astype(F32)) * ya_ref[...].astype(F32)
                  + _sigmoid(gb_ref[...].astype(F32)) * yb_ref[...].astype(F32))
        mg = merged.astype(MXU)
        mg_ref[...] = mg
        mix = jnp.dot(mg, w_ref[...].reshape(D, D), preferred_element_type=F32)
        mix_ref[...] = mix
        h1_ref[...] = x_ref[...] + _rms(mix, g_ref[...])

    row = lambda: pl.BlockSpec((tm, D), lambda i: (i, 0))
    return pl.pallas_call(
        body, name="merge_out", grid=(t_n // tm,),
        in_specs=[row(), row()] + _proj_specs(tm, (3, 4), lambda i: i)
        + [row(), _slot_spec(wout, D, D, lambda i: 0), _full((1, D))],
        out_specs=[row(), row(), row()],
        out_shape=[jax.ShapeDtypeStruct((t_n, D), F32), jax.ShapeDtypeStruct((t_n, D), F32),
                   jax.ShapeDtypeStruct((t_n, D), MXU)],
        compiler_params=_cp("parallel"),
    )(ya, yb, *_proj_ops(proj, (3, 4)), x, wout[0], g1)


def _out_bwd(dv, h1, g2, dout, mix, ya, yb, proj, wout, g1, w_in_pt):
    t_n = dv.shape[0]
    tm = min(t_n, 256)

    def body(dv_ref, h1_ref, g2_ref, dout_ref, mix_ref, ya_ref, yb_ref, ga_ref, gb_ref, w_ref, g_ref, wa_ref, wb_ref,
             dh1_ref, dmix_ref, dya_ref, dyb_ref, dga_ref, dgb_ref, dg2_ref, dg1_ref, dug_ref):
        @pl.when(pl.program_id(0) == 0)
        def _():
            dg1_ref[...] = jnp.zeros_like(dg1_ref)
            dg2_ref[...] = jnp.zeros_like(dg2_ref)

        dx, dg_rows = _rms_bwd(h1_ref[...], g2_ref[...], dv_ref[...])
        dg2_ref[...] += _rowsum(dg_rows)
        dh1 = dout_ref[...] + dx
        dh1_ref[...] = dh1
        dmix, dg_rows = _rms_bwd(mix_ref[...], g_ref[...], dh1)
        dg1_ref[...] += _rowsum(dg_rows)
        dmix_b = dmix.astype(MXU)
        dmix_ref[...] = dmix_b
        dmg = lax.dot_general(dmix_b, w_ref[...].reshape(D, D), (((1,), (1,)), ((), ())), preferred_element_type=F32)
        sa = _sigmoid(ga_ref[...].astype(F32))
        sb = _sigmoid(gb_ref[...].astype(F32))
        dya_ref[...] = dmg * sa
        dyb_ref[...] = dmg * sb
        dga = (dmg * ya_ref[...].astype(F32) * sa * (1.0 - sa)).astype(MXU)
        dga_ref[...] = dga
        dug = jnp.dot(dga, wa_ref[...], preferred_element_type=F32)
        dgb = (dmg * yb_ref[...].astype(F32) * sb * (1.0 - sb)).astype(MXU)
        dgb_ref[...] = dgb
        dug_ref[...] = dug + jnp.dot(dgb, wb_ref[...], preferred_element_type=F32)

    row = lambda: pl.BlockSpec((tm, D), lambda i: (i, 0))
    vec = lambda: _full((1, D))
    seg = lambda s: pl.BlockSpec((SEG, D), lambda i: (s, 0))
    return pl.pallas_call(
        body, name="out_bwd", grid=(t_n // tm,),
        in_specs=[row(), row(), vec(), row(), row(), row(), row()] + _proj_specs(tm, (3, 4), lambda i: i)
        + [_slot_spec(wout, D, D, lambda i: 0), vec(), seg(3), seg(4)],
        out_specs=[row(), row(), row(), row(), row(), row(), vec(), vec(), row()],
        out_shape=[jax.ShapeDtypeStruct((t_n, D), F32), jax.ShapeDtypeStruct((t_n, D), MXU),
                   jax.ShapeDtypeStruct((t_n, D), F32), jax.ShapeDtypeStruct((t_n, D), F32),
                   jax.ShapeDtypeStruct((t_n, D), MXU), jax.ShapeDtypeStruct((t_n, D), MXU),
                   jax.ShapeDtypeStruct((1, D), F32), jax.ShapeDtypeStruct((1, D), F32),
                   jax.ShapeDtypeStruct((t_n, D), F32)],
        compiler_params=_cp("arbitrary"),
    )(dv, h1, g2, dout, mix, ya, yb, *_proj_ops(proj, (3, 4)), wout[0], g1, w_in_pt, w_in_pt)


MLP_TM = 1024
MLP_TF_FWD = 1024
MLP_TF_BWD = 1024


def _mlp_fwd(h1, g2, wup, wdown, g3, tgt):
    t_n = h1.shape[0]
    tm = min(t_n, MLP_TM)
    n_f = FF // MLP_TF_FWD

    def body(h1_ref, g2_ref, wu_ref, wd_ref, g3_ref, tgt_ref, hp_ref, v_ref, dout_ref, dff_ref, loss_ref, dg3_ref, acc):
        i, j = pl.program_id(0), pl.program_id(1)

        @pl.when((i == 0) & (j == 0))
        def _():
            loss_ref[...] = jnp.zeros_like(loss_ref)
            dg3_ref[...] = jnp.zeros_like(dg3_ref)

        @pl.when(j == 0)
        def _():
            v_ref[...] = _rms(h1_ref[...], g2_ref[...]).astype(MXU)
            acc[...] = jnp.zeros_like(acc)

        hp = lax.dot_general(v_ref[...], wu_ref[...].reshape(MLP_TF_FWD, D), (((1,), (1,)), ((), ())),
                             preferred_element_type=F32)
        hp_ref[...] = hp.astype(MXU)
        hid = jnp.square(jnp.maximum(hp, 0.0))
        acc[...] += jnp.dot(hid.astype(MXU), wd_ref[...].reshape(MLP_TF_FWD, D), preferred_element_type=F32)

        @pl.when(j == n_f - 1)
        def _():
            ff = acc[...]
            err = h1_ref[...] + _rms(ff, g3_ref[...]) - tgt_ref[...]
            loss_ref[...] += 0.5 * jnp.sum(jnp.mean(err * err, axis=-1, keepdims=True), axis=0, keepdims=True)
            dout = err * (1.0 / D)
            dout_ref[...] = dout
            dff, dg_rows = _rms_bwd(ff, g3_ref[...], dout)
            dg3_ref[...] += _rowsum(dg_rows)
            dff_ref[...] = dff.astype(MXU)

    row = lambda: pl.BlockSpec((tm, D), lambda i, j: (i, 0))
    vec = lambda: pl.BlockSpec((1, D), lambda i, j: (0, 0))
    return pl.pallas_call(
        body, name="mlp_fwd", grid=(t_n // tm, n_f),
        in_specs=[row(), vec(), _slot_spec(wup, FF, MLP_TF_FWD, lambda i, j: j),
                  _slot_spec(wdown, FF, MLP_TF_FWD, lambda i, j: j), vec(), row()],
        out_specs=[pl.BlockSpec((tm, MLP_TF_FWD), lambda i, j: (i, j)), row(), row(), row(),
                   pl.BlockSpec((1, 1), lambda i, j: (0, 0)), vec()],
        out_shape=[jax.ShapeDtypeStruct((t_n, FF), MXU), jax.ShapeDtypeStruct((t_n, D), MXU),
                   jax.ShapeDtypeStruct((t_n, D), F32), jax.ShapeDtypeStruct((t_n, D), MXU),
                   jax.ShapeDtypeStruct((1, 1), F32), jax.ShapeDtypeStruct((1, D), F32)],
        scratch_shapes=[pltpu.VMEM((tm, D), F32)],
        compiler_params=_cp("arbitrary", "arbitrary"),
    )(h1, g2, wup[0], wdown[0], g3, tgt)


def _mlp_bwd(dff, hp, wup, wdown):
    t_n = dff.shape[0]
    tm = min(t_n, MLP_TM)
    n_f = FF // MLP_TF_BWD

    def mm_body(dff_ref, hp_ref, wu_ref, wd_ref, dv_ref, dhp_ref, hid_ref):
        @pl.when(pl.program_id(1) == 0)
        def _():
            dv_ref[...] = jnp.zeros_like(dv_ref)

        relu = jnp.maximum(hp_ref[...].astype(F32), 0.0)
        hid_ref[...] = jnp.square(relu).astype(MXU)
        dhid = lax.dot_general(dff_ref[...], wd_ref[...].reshape(MLP_TF_BWD, D), (((1,), (1,)), ((), ())),
                               preferred_element_type=F32)
        dhp = (dhid * (2.0 * relu)).astype(MXU)
        dhp_ref[...] = dhp
        dv_ref[...] += jnp.dot(dhp, wu_ref[...].reshape(MLP_TF_BWD, D), preferred_element_type=F32)

    row = lambda: pl.BlockSpec((tm, D), lambda i, j: (i, 0))
    blk = lambda: pl.BlockSpec((tm, MLP_TF_BWD), lambda i, j: (i, j))
    wblk = lambda w: _slot_spec(w, FF, MLP_TF_BWD, lambda i, j: j)
    return pl.pallas_call(
        mm_body, name="mlp_bwd", grid=(t_n // tm, n_f),
        in_specs=[row(), blk(), wblk(wup), wblk(wdown)], out_specs=[row(), blk(), blk()],
        out_shape=[jax.ShapeDtypeStruct((t_n, D), F32), jax.ShapeDtypeStruct((t_n, FF), MXU),
                   jax.ShapeDtypeStruct((t_n, FF), MXU)],
        compiler_params=_cp("parallel", "arbitrary"),
    )(dff, hp, wup[0], wdown[0])


def _wgrad(a, g, name):
    t_n, k_n = a.shape
    n_n = g.shape[1]
    tt = min(t_n, 1024)
    tk, tn = min(k_n, 1024), min(n_n, 1024)

    n_t = t_n // tt

    def body(a_ref, g_ref, o_ref, acc):
        t = pl.program_id(2)

        @pl.when(t == 0)
        def _():
            acc[...] = jnp.zeros_like(acc)

        acc[...] += lax.dot_general(a_ref[...], g_ref[...], (((0,), (0,)), ((), ())), preferred_element_type=F32)

        @pl.when(t == n_t - 1)
        def _():
            o_ref[...] = acc[...].astype(o_ref.dtype)

    return pl.pallas_call(
        body, name=name, grid=(k_n // tk, n_n // tn, n_t),
        in_specs=[pl.BlockSpec((tt, tk), lambda k, n, t: (t, k)), pl.BlockSpec((tt, tn), lambda k, n, t: (t, n))],
        out_specs=pl.BlockSpec((tk, tn), lambda k, n, t: (k, n)),
        out_shape=jax.ShapeDtypeStruct((k_n, n_n), MXU),
        scratch_shapes=[pltpu.VMEM((tk, tn), F32)],
        compiler_params=_cp("parallel", "parallel", "arbitrary"),
    )(a, g)


def _wgrad_segs(segs, g, name):
    t_n, n_n = g.shape
    n_s = len(segs)
    tt = min(t_n, 1024)
    n_t = t_n // tt

    def body(*refs):
        a_refs = refs[:n_s]
        g_ref, o_ref, acc = refs[n_s:]
        s_id, t = pl.program_id(0), pl.program_id(1)

        @pl.when(t == 0)
        def _():
            acc[...] = jnp.zeros_like(acc)

        for s in range(n_s):
            @pl.when(s_id == s)
            def _(s=s):
                acc[...] += lax.dot_general(a_refs[s][...], g_ref[...], (((0,), (0,)), ((), ())),
                                            preferred_element_type=F32)

        @pl.when(t == n_t - 1)
        def _():
            o_ref[...] = acc[...].astype(o_ref.dtype)

    seg_spec = lambda s: pl.BlockSpec((tt, SEG), lambda i, t: (jnp.where(i == s, t, jnp.where(i < s, 0, n_t - 1)), 0))
    return pl.pallas_call(
        body, name=name, grid=(n_s, n_t),
        in_specs=[seg_spec(s) for s in range(n_s)] + [pl.BlockSpec((tt, n_n), lambda i, t: (t, 0))],
        out_specs=pl.BlockSpec((SEG, n_n), lambda i, t: (i, 0)),
        out_shape=jax.ShapeDtypeStruct((n_s * SEG, n_n), MXU),
        scratch_shapes=[pltpu.VMEM((SEG, n_n), F32)],
        compiler_params=_cp("arbitrary", "arbitrary"),
    )(*segs, g)


def _inproj_bwd(dsegs, seg_ids, du_part, w, x, g0, dh1):
    t_n = x.shape[0]
    tm = min(t_n, 512)
    n_k = len(dsegs)

    def body(*refs):
        dp_refs = refs[:n_k]
        w_refs = refs[n_k:2 * n_k]
        part_ref, x_ref, g_ref, dh1_ref, dx_ref, dg0_ref = refs[2 * n_k:]

        @pl.when(pl.program_id(0) == 0)
        def _():
            dg0_ref[...] = jnp.zeros_like(dg0_ref)

        du = part_ref[...]
        for s in range(n_k):
            du = du + jnp.dot(dp_refs[s][...], w_refs[s][...], preferred_element_type=F32)
        dx, dg_rows = _rms_bwd(x_ref[...], g_ref[...], du)
        dg0_ref[...] += _rowsum(dg_rows)
        dx_ref[...] = dh1_ref[...] + dx

    row = lambda: pl.BlockSpec((tm, D), lambda i: (i, 0))
    return pl.pallas_call(
        body, name="inproj_bwd", grid=(t_n // tm,),
        in_specs=[row() for _ in range(n_k)]
        + [pl.BlockSpec((SEG, D), functools.partial(lambda i, s: (s, 0), s=s)) for s in seg_ids]
        + [row(), row(), _full((1, D)), row()],
        out_specs=[row(), _full((1, D))],
        out_shape=[jax.ShapeDtypeStruct((t_n, D), F32), jax.ShapeDtypeStruct((1, D), F32)],
        compiler_params=_cp("arbitrary"),
    )(*dsegs, *([w] * n_k), du_part, x, g0, dh1)


def _blockdiag4(w):
    w4 = w.reshape(4, 4, 64, 1, 64).astype(MXU)
    same = (jnp.arange(4)[:, None, None, None] == jnp.arange(4)[None, None, :, None])
    return jnp.where(same[None], w4, jnp.zeros((), MXU)).reshape(4, 256, 256)


def _blockdiag4_extract(g):
    g5 = g.reshape(4, 4, 64, 4, 64)
    return jnp.stack([g5[:, q, :, q, :] for q in range(4)], axis=1).reshape(NH, 64, 64)


def _local_step(x, tgt, p, after_ssd=None, late_weights=None, send_mlp_grads=None, send_late_grads=None):
    f = lambda a: a.astype(F32)
    proj, u = _inproj(x, p["norm_mix_pre"], p["w_in_pT"])
    ssm_params = (p["cw_xs"], p["cw_bc"], p["cb_xs"], p["cb_bc"], p["dt_bias"], p["a_log"], p["d_skip_x"],
                  p["ssm_norm"])
    ya, sprev, *ssd_saved = _ssd_fwd(proj, *ssm_params)
    cb_lru = p["conv_lru_b"] if after_ssd is None else p["conv_lru_b"] + after_ssd(ya)
    lru_params = (p["conv_lru_w"], cb_lru, p["wa_bd"], p["wx_bd"], p["lru_ba"], p["lru_bx"], p["lru_lambda"])
    yb, h, xr = _lru_fwd(proj, *lru_params)
    if late_weights is not None:
        p = dict(p, **late_weights(yb))
    p = dict(p, **{n: _slotted(p[n]) for n in ("w_out", "w_upT", "w_down")})
    h1, mix, merged = _merge_out(ya, yb, proj, x, p["w_out"], p["norm_mix_post"])
    hp, v, dout, dff, loss, dg3 = _mlp_fwd(h1, p["norm_mlp_pre"], p["w_upT"], p["w_down"], p["norm_mlp_post"], tgt)

    dv, dhp, hid = _mlp_bwd(dff, hp, p["w_upT"], p["w_down"])
    dh1, dmix, dya, dyb, dga, dgb, dg2, dg1, du_gates = _out_bwd(
        dv, h1, p["norm_mlp_pre"], dout, mix, ya, yb, proj, p["w_out"], p["norm_mix_post"], p["w_in_pT"])
    d_w_out = _wgrad(merged, dmix, "wgrad_out")
    (dz, dxs, dbc, dcwx, dcwb, dcbx, dcbb, ddtb, dalog, ddsk, dnrm, d_w_down, d_w_up_t) = _ssd_bwd(
        dya, ssd_saved, proj, sprev, p["cw_xs"], p["cw_bc"], *ssm_params[4:], (hid, dff, dhp, v))
    ba_lru = p["lru_ba"] if send_mlp_grads is None else p["lru_ba"] + send_mlp_grads(d_w_up_t, d_w_down, d_w_out)
    (dgl, dxl, dcwl, dcbl, dwa, dwx, dba, dbx, dlam) = _lru_bwd(
        dyb, proj, h, xr, p["conv_lru_w"], p["wa_bd"], p["wx_bd"], ba_lru, p["lru_bx"], p["lru_lambda"])
    dsegs = [dz, dgl, dxl, dga, dgb, dxs, dbc]
    d_w_in_pt = _wgrad_segs(dsegs, u, "wgrad_in")
    g0 = p["norm_mix_pre"]
    if send_late_grads is not None:
        g0 = g0 + send_late_grads(d_w_in_pt)
    grad_x, dg0 = _inproj_bwd([dz, dgl, dxl, dxs, dbc], (0, 1, 2, 5, 6), du_gates, p["w_in_pT"], x, g0, dh1)
    grads = dict(
        norm_mix_pre=dg0, w_in_pT=d_w_in_pt, conv_ssm_w=jnp.concatenate([dcwx, dcwb], axis=1),
        conv_ssm_b=jnp.concatenate([dcbx, dcbb], axis=1), dt_bias=ddtb[:, :NH], a_log=dalog[:, :NH],
        d_skip=f(ddsk).reshape(NH, 64).sum(axis=1)[None, :], ssm_norm=dnrm, conv_lru_w=dcwl, conv_lru_b=dcbl,
        lru_wa=_blockdiag4_extract(dwa), lru_ba=dba, lru_wx=_blockdiag4_extract(dwx), lru_bx=dbx, lru_lambda=dlam,
        w_out=d_w_out, norm_mix_post=dg1, norm_mlp_pre=dg2, w_upT=d_w_up_t, w_down=d_w_down, norm_mlp_post=dg3)
    return loss[0, 0], grad_x, grads


W_IN_COLS = 6672


def _w_in_t_to_padded(wt):
    z, xs, bc, dt = wt[0:1024], wt[1024:2048], wt[2048:2560], wt[2560:2576]
    gl, xl, ga, gb = wt[2576:3600], wt[3600:4624], wt[4624:5648], wt[5648:6672]
    return jnp.concatenate([z, gl, xl, ga, gb, xs, bc, dt, jnp.zeros((NP - 6672, wt.shape[1]), wt.dtype)], axis=0)


def _w_in_t_from_padded(wp):
    z, gl, xl, ga, gb = (wp[SEG * s:SEG * (s + 1)] for s in range(5))
    xs, bc, dt = wp[5120:6144], wp[6144:6656], wp[6656:6672]
    return jnp.concatenate([z, xs, bc, dt, gl, xl, ga, gb], axis=0)


def _prep_params(full, big):
    f = lambda a: a.astype(F32)
    pad128 = lambda a: jnp.pad(f(a).reshape(1, -1), ((0, 0), (0, 128 - a.size)))
    cw = f(full["conv_ssm_w"])
    cb = f(full["conv_ssm_b"]).reshape(1, -1)
    return dict(
        big, norm_mix_pre=f(full["norm_mix_pre"]).reshape(1, D),
        cw_xs=cw[:, :D], cw_bc=cw[:, D:], cb_xs=cb[:, :D], cb_bc=cb[:, D:],
        dt_bias=pad128(full["dt_bias"]), a_log=pad128(full["a_log"]),
        d_skip_x=jnp.repeat(f(full["d_skip"]).reshape(-1), 64).reshape(1, D), ssm_norm=f(full["ssm_norm"]).reshape(1, D),
        conv_lru_w=f(full["conv_lru_w"]), conv_lru_b=f(full["conv_lru_b"]).reshape(1, D),
        wa_bd=_blockdiag4(full["lru_wa"]), wx_bd=_blockdiag4(full["lru_wx"]),
        lru_ba=f(full["lru_ba"]).reshape(1, D), lru_bx=f(full["lru_bx"]).reshape(1, D),
        lru_lambda=f(full["lru_lambda"]).reshape(1, D),
        norm_mix_post=f(full["norm_mix_post"]).reshape(1, D), norm_mlp_pre=f(full["norm_mlp_pre"]).reshape(1, D),
        norm_mlp_post=f(full["norm_mlp_post"]).reshape(1, D))


MESH_ID = pl.DeviceIdType.MESH
ANY = pl.BlockSpec(memory_space=pl.ANY)


def _my_place():
    x, y, c = lax.axis_index("x"), lax.axis_index("y"), lax.axis_index("c")
    return x, y, c, 4 * x + 2 * y + c


def _peer(x, y, c, k):
    return (x ^ ((k >> 2) & 1), y ^ ((k >> 1) & 1), c ^ (k & 1))


def _all_gather(pack, name):
    r_n = pack.shape[0]
    half = -(-r_n // 32) * 16
    n_cp = 9

    def body(in_ref, out_ref, send_sems, recv_sems, local_sem):
        x, y, c, me = _my_place()
        here, sibling, x_nbr, y_nbr = (x, y, c), (x, y, 1 - c), (1 - x, y, c), (x, 1 - y, c)
        part = {"all": pl.ds(0, r_n), "lo": pl.ds(0, half), "hi": pl.ds(half, r_n - half)}

        def copy(j, block, rows, to, src=None):
            return pltpu.make_async_remote_copy(
                src_ref=out_ref.at[block, part[rows]] if src is None else src, dst_ref=out_ref.at[block, part[rows]],
                send_sem=send_sems.at[j], recv_sem=recv_sems.at[j], device_id=to, device_id_type=MESH_ID)

        mine = pltpu.make_async_copy(in_ref, out_ref.at[me], local_sem)
        mine.start()
        first = [copy(0, me, "all", sibling, src=in_ref), copy(1, me, "all", x_nbr, src=in_ref),
                 copy(2, me, "all", y_nbr, src=in_ref)]
        for cp in first:
            cp.start()
        relay = [[(3, me ^ 4, "lo", y_nbr), (5, me ^ 4, "all", sibling)],
                 [(4, me ^ 2, "hi", x_nbr), (6, me ^ 2, "all", sibling)],
                 [(7, me ^ 6, "lo", sibling)],
                 [(8, me ^ 6, "hi", sibling)]]
        landed = [(1, me ^ 4, "all"), (2, me ^ 2, "all"), (3, me ^ 6, "lo"), (4, me ^ 6, "hi")]
        passed = []
        for (j, block, rows), nxt in zip(landed, relay):
            copy(j, block, rows, here).wait_recv()
            for args in nxt:
                cp = copy(*args)
                cp.start()
                passed.append(cp)
        for j, block, rows in ((0, me ^ 1, "all"), (5, me ^ 5, "all"), (6, me ^ 3, "all"), (7, me ^ 7, "lo"),
                               (8, me ^ 7, "hi")):
            copy(j, block, rows, here).wait_recv()
        for cp in first + passed:
            cp.wait_send()
        mine.wait()

    return pl.pallas_call(
        body, name=name, in_specs=[ANY], out_specs=ANY,
        out_shape=jax.ShapeDtypeStruct((N_DEV,) + pack.shape, pack.dtype),
        scratch_shapes=[pltpu.SemaphoreType.DMA((n_cp,)), pltpu.SemaphoreType.DMA((n_cp,)), pltpu.SemaphoreType.DMA],
    )(pack)


HBM = pl.BlockSpec(memory_space=pltpu.HBM)
SEM = pl.BlockSpec(memory_space=pltpu.SEMAPHORE)
PLAN_GATHER = tuple((k, "pack", 0) for k in range(1, N_DEV))
PLAN_SCATTER = tuple((k, "slot", 0) for k in range(1, N_DEV))
PLAN_GATHER_ICI = tuple((k, "pack", 0) for k in (2, 4, 6))
PLAN_GATHER_D2D = ((1, "pack", 0),) + tuple((1, s, s) for s in (2, 4, 6))
PLAN_SCATTER_CHIPS = tuple((k, "chip", "chip") for k in (2, 4, 6))


def _plan_copy(j, plan, src_ref, land_ref, sems):
    k, source, slot = plan[j]
    x, y, c, me = _my_place()
    if source == "pack":
        src = src_ref
    elif source == "slot":
        src = src_ref.at[me ^ k]
    elif source == "chip":
        src = src_ref.at[(me ^ k) >> 1]
    else:
        src = land_ref.at[me ^ source]
    dst = land_ref.at[me >> 1] if slot == "chip" else land_ref.at[me ^ slot]
    return pltpu.make_async_remote_copy(
        src_ref=src, dst_ref=dst, send_sem=sems[j], recv_sem=sems[len(plan) + j],
        device_id=_peer(x, y, c, k), device_id_type=MESH_ID)


def _exchange_start(src, land, plan, name):
    n_c = len(plan)
    if land is None:
        slots = src.shape[0] if src.ndim == 3 else N_DEV
        land = pltpu.with_memory_space_constraint(lax.empty((slots,) + src.shape[-2:], src.dtype), pltpu.HBM)

    def body(src_ref, land_ref, *rest):
        sems, token = rest[:2 * n_c], rest[2 * n_c + 2]
        for j in range(n_c):
            _plan_copy(j, plan, src_ref, land_ref, sems).start()
        token[...] = jnp.zeros_like(token)

    outs = pl.pallas_call(
        body, name=name,
        out_shape=(pltpu.SemaphoreType.DMA(()),) * (2 * n_c) + (
            pltpu.HBM(src.shape, src.dtype), pltpu.HBM(land.shape, land.dtype), jax.ShapeDtypeStruct((8, 128), F32)),
        in_specs=(HBM, HBM), out_specs=(SEM,) * (2 * n_c) + (HBM, HBM, pl.BlockSpec(memory_space=pltpu.VMEM)),
        input_output_aliases={0: 2 * n_c, 1: 2 * n_c + 1},
        compiler_params=pltpu.CompilerParams(has_side_effects=pltpu.SideEffectType.DATAFLOW_SIDE_EFFECTING),
    )(pltpu.with_memory_space_constraint(src, pltpu.HBM), land)
    return outs[:2 * n_c], outs[2 * n_c], outs[2 * n_c + 1], outs[2 * n_c + 2]


def _exchange_wait(sems, src_thru, land_thru, after, plan, name):
    n_c = len(plan)

    def body(src_ref, land_ref, *rest):
        for j in range(n_c):
            cp = _plan_copy(j, plan, src_ref, land_ref, rest[:2 * n_c])
            cp.wait_send()
            cp.wait_recv()

    return pl.pallas_call(
        body, name=name,
        out_shape=(pltpu.HBM(src_thru.shape, src_thru.dtype), pltpu.HBM(land_thru.shape, land_thru.dtype)),
        in_specs=(HBM, HBM) + (SEM,) * (2 * n_c) + (pl.BlockSpec(memory_space=pl.ANY),), out_specs=(HBM, HBM),
        input_output_aliases={0: 0, 1: 1},
        compiler_params=pltpu.CompilerParams(has_side_effects=pltpu.SideEffectType.DATAFLOW_SIDE_EFFECTING),
    )(src_thru, land_thru, *sems, after)


def _parts_copies(src_refs, land_ref, sems):
    x, y, c, me = _my_place()
    n_p = len(src_refs)
    n_c = (N_DEV - 1) * n_p
    remote, local = [], []
    row0 = 0
    for p, src_ref in enumerate(src_refs):
        rows = pl.ds(row0, src_ref.shape[1])
        row0 += src_ref.shape[1]
        for k in range(1, N_DEV):
            j = (k - 1) * n_p + p
            remote.append(pltpu.make_async_remote_copy(
                src_ref=src_ref.at[me ^ k], dst_ref=land_ref.at[me, rows], send_sem=sems[j], recv_sem=sems[n_c + j],
                device_id=_peer(x, y, c, k), device_id_type=MESH_ID))
        local.append(pltpu.make_async_copy(src_ref.at[me], land_ref.at[me, rows], sems[2 * n_c + p]))
    return remote, local


def _scatter_parts_start(srcs, name):
    n_p = len(srcs)
    n_s = (2 * (N_DEV - 1) + 1) * n_p
    hbm = lambda a: pltpu.with_memory_space_constraint(a, pltpu.HBM)
    land = hbm(lax.empty((N_DEV, sum(s.shape[1] for s in srcs), srcs[0].shape[2]), srcs[0].dtype))

    def body(*refs):
        sems, token = refs[n_p + 1:n_p + 1 + n_s], refs[-1]
        remote, local = _parts_copies(refs[:n_p], refs[n_p], sems)
        for cp in remote + local:
            cp.start()
        token[...] = jnp.zeros_like(token)

    thru = [pltpu.HBM(a.shape, a.dtype) for a in list(srcs) + [land]]
    outs = pl.pallas_call(
        body, name=name,
        out_shape=(pltpu.SemaphoreType.DMA(()),) * n_s + tuple(thru) + (jax.ShapeDtypeStruct((8, 128), F32),),
        in_specs=(HBM,) * (n_p + 1),
        out_specs=(SEM,) * n_s + (HBM,) * (n_p + 1) + (pl.BlockSpec(memory_space=pltpu.VMEM),),
        input_output_aliases={i: n_s + i for i in range(n_p + 1)},
        compiler_params=pltpu.CompilerParams(has_side_effects=pltpu.SideEffectType.DATAFLOW_SIDE_EFFECTING),
    )(*[hbm(s) for s in srcs], land)
    return outs[:n_s], outs[n_s:n_s + n_p], outs[n_s + n_p], outs[n_s + n_p + 1]


def _scatter_parts_wait(sems, srcs_thru, land_thru, after, name):
    n_p = len(srcs_thru)

    def body(*refs):
        remote, local = _parts_copies(refs[:n_p], refs[n_p], refs[n_p + 1:n_p + 1 + len(sems)])
        for cp in remote:
            cp.wait_send()
            cp.wait_recv()
        for cp in local:
            cp.wait()

    thru = [pltpu.HBM(a.shape, a.dtype) for a in list(srcs_thru) + [land_thru]]
    outs = pl.pallas_call(
        body, name=name, out_shape=tuple(thru),
        in_specs=(HBM,) * (n_p + 1) + (SEM,) * len(sems) + (pl.BlockSpec(memory_space=pl.ANY),),
        out_specs=(HBM,) * (n_p + 1), input_output_aliases={i: i for i in range(n_p + 1)},
        compiler_params=pltpu.CompilerParams(has_side_effects=pltpu.SideEffectType.DATAFLOW_SIDE_EFFECTING),
    )(*srcs_thru, land_thru, *sems, after)
    return outs[:n_p], outs[n_p]


def _pair_exchange(src, name):
    n_s = N_DEV // 2

    def body(src_ref, land_ref, send_sems, recv_sems):
        x, y, c, _ = _my_place()
        cps = [pltpu.make_async_remote_copy(
            src_ref=src_ref.at[2 * i + 1 - c], dst_ref=land_ref.at[i], send_sem=send_sems.at[i],
            recv_sem=recv_sems.at[i], device_id=(x, y, 1 - c), device_id_type=MESH_ID) for i in range(n_s)]
        for cp in cps:
            cp.start()
        for cp in cps:
            cp.wait_recv()
        for cp in cps:
            cp.wait_send()

    return pl.pallas_call(
        body, name=name, in_specs=[ANY], out_specs=ANY,
        out_shape=jax.ShapeDtypeStruct((n_s,) + src.shape[1:], src.dtype),
        scratch_shapes=[pltpu.SemaphoreType.DMA((n_s,)), pltpu.SemaphoreType.DMA((n_s,))],
    )(src)


def _pair_sum(src, land, name):
    n_s, r_n, c_n = land.shape
    tr = max(t for t in range(16, 513, 16) if r_n % t == 0)

    def body(c_ref, a_ref, b_ref, o_ref):
        o_ref[...] = (a_ref[...].astype(F32) + b_ref[...].astype(F32)).astype(o_ref.dtype)

    blk = lambda: pl.BlockSpec((1, tr, c_n), lambda i, j, c_ref: (i, j, 0))
    return pl.pallas_call(
        body, name=name,
        grid_spec=pltpu.PrefetchScalarGridSpec(
            num_scalar_prefetch=1, grid=(n_s, r_n // tr),
            in_specs=[pl.BlockSpec((1, tr, c_n), lambda i, j, c_ref: (2 * i + c_ref[0], j, 0)), blk()],
            out_specs=blk()),
        out_shape=jax.ShapeDtypeStruct(land.shape, land.dtype),
        compiler_params=_cp("parallel", "parallel"),
    )(lax.axis_index("c").astype(jnp.int32).reshape(1), src, land)


def _slot_sum(parts, name):
    n_s, r_n, c_n = parts.shape
    tr = max(t for t in range(16, 513, 16) if r_n % t == 0)

    def body(p_ref, o_ref):
        acc = p_ref[0].astype(F32)
        for k in range(1, n_s):
            acc = acc + p_ref[k].astype(F32)
        o_ref[...] = acc

    return pl.pallas_call(
        body, name=name, grid=(r_n // tr,),
        in_specs=[pl.BlockSpec((n_s, tr, c_n), lambda i: (0, i, 0))],
        out_specs=pl.BlockSpec((tr, c_n), lambda i: (i, 0)),
        out_shape=jax.ShapeDtypeStruct((r_n, c_n), F32),
        compiler_params=_cp("parallel"),
    )(parts)


def _adam_math(w, g, m, v):
    m = ADAM_B1 * m + (1.0 - ADAM_B1) * g
    v = ADAM_B2 * v + (1.0 - ADAM_B2) * jnp.square(g)
    m_hat = m / (1.0 - ADAM_B1 ** ADAM_STEP)
    v_hat = v / (1.0 - ADAM_B2 ** ADAM_STEP)
    return -ADAM_LR * (m_hat / (jnp.sqrt(v_hat) + ADAM_EPS) + ADAM_WD * w), m, v


def _adam_big(w, g, m, v, name):
    def body(w_ref, g_ref, m_ref, v_ref, d_ref, mo_ref, vo_ref):
        d_ref[...], mo_ref[...], vo_ref[...] = _adam_math(w_ref[...], g_ref[...], m_ref[...], v_ref[...])

    if w.ndim == 3:
        _, r_n, c_n = w.shape
        tr = min(r_n, 256)
        grid = (r_n // tr,)
        blk = lambda: pl.BlockSpec((1, tr, c_n), lambda i: (0, i, 0))
    else:
        r_n, c_n = w.shape
        tc = min(c_n, 256)
        grid = (c_n // tc,)
        blk = lambda: pl.BlockSpec((r_n, tc), lambda i: (0, i))
    return pl.pallas_call(
        body, name=name, grid=grid, in_specs=[blk(), blk(), blk(), blk()], out_specs=[blk(), blk(), blk()],
        out_shape=[jax.ShapeDtypeStruct(w.shape, F32)] * 3, compiler_params=_cp("parallel"),
    )(w, g, m, v)


def _adam_small(groups, where, wmv, total):
    n, n_g = len(wmv), len(groups)
    t_g, t_r = total

    def body(*refs):
        g_refs = refs[:n_g]
        w_refs = refs[n_g:n_g + 3 * n]
        o_refs = refs[n_g + 3 * n:]
        for q in range(n):
            w_ref, m_ref, v_ref = w_refs[3 * q:3 * q + 3]
            r, c = w_ref.shape
            gi, r0 = where[q]
            g = g_refs[gi][r0:r0 + r, 0:c]
            d, m, v = _adam_math(w_ref[...], g, m_ref[...], v_ref[...])
            o_refs[4 * q][...] = g
            o_refs[4 * q + 1][...] = d
            o_refs[4 * q + 2][...] = m
            o_refs[4 * q + 3][...] = v
        o_refs[4 * n][...] = g_refs[t_g][t_r:t_r + 1, :]

    flat_wmv = [a for t in wmv for a in t]
    vm = pl.BlockSpec(memory_space=pltpu.VMEM)
    outs = pl.pallas_call(
        body, name="adam_small", in_specs=[vm] * (n_g + 3 * n), out_specs=[vm] * (4 * n + 1),
        out_shape=[jax.ShapeDtypeStruct(t[0].shape, F32) for t in wmv for _ in range(4)]
        + [jax.ShapeDtypeStruct((1, groups[t_g].shape[1]), F32)],
        compiler_params=pltpu.CompilerParams(vmem_limit_bytes=VMEM_LIMIT),
    )(*groups, *flat_wmv)
    return [tuple(outs[4 * q:4 * q + 4]) for q in range(n)], outs[4 * n]


WEIGHTS = ["norm_mix_pre", "w_in", "conv_ssm_w", "conv_ssm_b", "dt_bias", "a_log", "d_skip", "ssm_norm", "conv_lru_w",
           "conv_lru_b", "lru_wa", "lru_ba", "lru_wx", "lru_bx", "lru_lambda", "w_out", "norm_mix_post", "norm_mlp_pre",
           "w_up", "w_down", "norm_mlp_post"]
BIG = ["w_out", "w_up", "w_down", "w_in"]
IN_ROWS = W_IN_COLS // N_DEV
IN_PAD, EARLY_ROWS = 848, 880
LATE_ROWS = 1152
GRAD_LATE_ROWS = 864
CONV_SSM_COLS, CONV_LRU_COLS = 1536 // N_DEV, D // N_DEV
SMALL = [("norm_mix_pre", (1, D), 0, 0), ("ssm_norm", (1, D), 0, 1), ("conv_lru_b", (1, D), 0, 2),
         ("lru_lambda", (1, D), 0, 3), ("norm_mix_post", (1, D), 0, 4), ("norm_mlp_pre", (1, D), 0, 5),
         ("norm_mlp_post", (1, D), 0, 6), ("conv_ssm_b", (1, 1536), 1, 0), ("dt_bias", (1, NH), 2, 0),
         ("a_log", (1, NH), 2, 1), ("d_skip", (1, NH), 2, 2), ("conv_ssm_w", (4, CONV_SSM_COLS), 3, 0),
         ("conv_lru_w", (4, CONV_LRU_COLS), 4, 0), ("lru_wa", (D, 64), 5, 0), ("lru_wx", (D, 64), 5, D),
         ("lru_ba", (NH, 64), 6, 0), ("lru_bx", (NH, 64), 6, NH)]
SMALL_GROUPS = [(8, D), (1, 1536), (8, 128), (4, 1536), (4, D), (2 * D, 64), (2 * NH, 64)]


def _pad_rows(flat, mult):
    n = flat.shape[0]
    rows = -(-n // (128 * mult)) * mult
    return jnp.pad(flat, (0, rows * 128 - n)).reshape(rows, 128)


def _split3(a):
    hi = a.astype(MXU)
    r1 = a - hi.astype(F32)
    mid = r1.astype(MXU)
    lo = (r1 - mid.astype(F32)).astype(MXU)
    return jnp.stack([hi, mid, lo])


def _early_pack(a, me):
    bf = lambda t: t.astype(MXU)
    conv = lambda t, c: jnp.pad(_split3(t).reshape(12, c), ((0, 4), (0, D - c)))
    shifted = lax.dynamic_update_slice(jnp.zeros((IN_PAD, D), MXU), bf(a["w_in"][0]).T, (2 * me, 0))
    return jnp.concatenate([shifted, conv(a["conv_ssm_w"][0], CONV_SSM_COLS), conv(a["conv_lru_w"][0], CONV_LRU_COLS)],
                           axis=0)


TILE = 16
SHARD_TILES = IN_PAD // TILE
SHARD_STEP = (IN_ROWS // TILE)
SEG_TILES = ((0, 64, 0), (64, 128, 320), (128, 160, 384), (160, 161, 416), (161, 225, 64), (225, 289, 128),
             (289, 353, 192), (353, 417, 256))


def _tile_runs(lo, hi):
    runs = []
    for s0, s1, d0 in SEG_TILES:
        a, b = max(lo, s0), min(hi, s1)
        if a < b:
            runs.append((a, b - a, d0 + a - s0))
    return runs


def _assemble_w_in(g):
    whole = []
    for k in range(N_DEV):
        lo = SHARD_STEP * k + (1 if k else 0)
        hi = SHARD_STEP * (k + 1) + (1 if k == N_DEV - 1 else 0)
        whole += [(k, a - SHARD_STEP * k, n, d) for a, n, d in _tile_runs(lo, hi)]
    split = [(k, _tile_runs(SHARD_STEP * k, SHARD_STEP * k + 1)[0][2]) for k in range(1, N_DEV)]

    def body(g_ref, o_ref):
        rows = lambda t, n=1: pl.ds(TILE * t, TILE * n)
        for k, t, n, d in whole:
            o_ref[rows(d, n), :] = g_ref[k, rows(t, n), :]
        for k, d in split:
            o_ref[rows(d), :] = g_ref[k - 1, rows(SHARD_STEP), :] + g_ref[k, rows(0), :]
        o_ref[pl.ds(W_IN_COLS, NP - W_IN_COLS), :] = jnp.zeros((NP - W_IN_COLS, D), o_ref.dtype)

    vm = pl.BlockSpec(memory_space=pltpu.VMEM)
    return pl.pallas_call(
        body, name="assemble_w_in", in_specs=[vm], out_specs=vm, out_shape=jax.ShapeDtypeStruct((NP, D), g.dtype),
        compiler_params=pltpu.CompilerParams(vmem_limit_bytes=VMEM_LIMIT),
    )(g)


def _scatter_w_in_grad(dw):
    runs = [(k, a - SHARD_STEP * k, n, d) for k in range(N_DEV)
            for a, n, d in _tile_runs(SHARD_STEP * k, SHARD_STEP * k + SHARD_TILES)]
    pad = GRAD_LATE_ROWS - IN_PAD

    def body(dw_ref, o_ref):
        rows = lambda t, n: pl.ds(TILE * t, TILE * n)
        for k, t, n, d in runs:
            o_ref[k, rows(t, n), :] = dw_ref[rows(d, n), :]
        for k in range(N_DEV):
            o_ref[k, pl.ds(IN_PAD, pad), :] = jnp.zeros((pad, D), o_ref.dtype)

    vm = pl.BlockSpec(memory_space=pltpu.VMEM)
    return pl.pallas_call(
        body, name="scatter_w_in_grad", in_specs=[vm], out_specs=vm,
        out_shape=jax.ShapeDtypeStruct((N_DEV, GRAD_LATE_ROWS, D), dw.dtype),
        compiler_params=pltpu.CompilerParams(vmem_limit_bytes=VMEM_LIMIT),
    )(dw)


def _early_unpack(g):
    w_in_pt = _assemble_w_in(g)
    conv = {}
    for n, r0, c in (("conv_ssm_w", IN_PAD, CONV_SSM_COLS), ("conv_lru_w", IN_PAD + 16, CONV_LRU_COLS)):
        s = g[:, r0:r0 + 12, :c].astype(F32).reshape(N_DEV, 3, 4, c)
        conv[n] = ((s[:, 0] + s[:, 1]) + s[:, 2]).transpose(1, 0, 2).reshape(4, N_DEV * c)
    return w_in_pt, conv


def _late_pack(a):
    bf = lambda t: t.astype(MXU)
    return jnp.concatenate([bf(a["w_up"][0]).T, bf(a["w_down"][0]), bf(a["w_out"][0])], axis=0)


def _late_unpack(g):
    return dict(w_upT=(g, 0), w_down=(g, 1), w_out=(g, 2 * FF // D))


def _own_slot(land, own):
    me = 4 * lax.axis_index("x") + 2 * lax.axis_index("y") + lax.axis_index("c")
    return lax.dynamic_update_slice_in_dim(land, own[None], me, axis=0)


def kernel(x, norm_mix_pre, w_in, conv_ssm_w, conv_ssm_b, dt_bias, a_log, d_skip, ssm_norm, conv_lru_w, conv_lru_b, lru_wa, lru_ba, lru_wx, lru_bx, lru_lambda, w_out, norm_mix_post, norm_mlp_pre, w_up, w_down, norm_mlp_post, loss_target, m_norm_mix_pre, m_w_in, m_conv_ssm_w, m_conv_ssm_b, m_dt_bias, m_a_log, m_d_skip, m_ssm_norm, m_conv_lru_w, m_conv_lru_b, m_lru_wa, m_lru_ba, m_lru_wx, m_lru_bx, m_lru_lambda, m_w_out, m_norm_mix_post, m_norm_mlp_pre, m_w_up, m_w_down, m_norm_mlp_post, v_norm_mix_pre, v_w_in, v_conv_ssm_w, v_conv_ssm_b, v_dt_bias, v_a_log, v_d_skip, v_ssm_norm, v_conv_lru_w, v_conv_lru_b, v_lru_wa, v_lru_ba, v_lru_wx, v_lru_bx, v_lru_lambda, v_w_out, v_norm_mix_post, v_norm_mlp_pre, v_w_up, v_w_down, v_norm_mlp_post):
    vals = (norm_mix_pre, w_in, conv_ssm_w, conv_ssm_b, dt_bias, a_log, d_skip, ssm_norm, conv_lru_w, conv_lru_b, lru_wa, lru_ba, lru_wx, lru_bx, lru_lambda, w_out, norm_mix_post, norm_mlp_pre, w_up, w_down, norm_mlp_post)
    m_vals = (m_norm_mix_pre, m_w_in, m_conv_ssm_w, m_conv_ssm_b, m_dt_bias, m_a_log, m_d_skip, m_ssm_norm, m_conv_lru_w, m_conv_lru_b, m_lru_wa, m_lru_ba, m_lru_wx, m_lru_bx, m_lru_lambda, m_w_out, m_norm_mix_post, m_norm_mlp_pre, m_w_up, m_w_down, m_norm_mlp_post)
    v_vals = (v_norm_mix_pre, v_w_in, v_conv_ssm_w, v_conv_ssm_b, v_dt_bias, v_a_log, v_d_skip, v_ssm_norm, v_conv_lru_w, v_conv_lru_b, v_lru_wa, v_lru_ba, v_lru_wx, v_lru_bx, v_lru_lambda, v_w_out, v_norm_mix_post, v_norm_mlp_pre, v_w_up, v_w_down, v_norm_mlp_post)
    w = dict(zip(WEIGHTS, vals))
    m = dict(zip(WEIGHTS, m_vals))
    v = dict(zip(WEIGHTS, v_vals))
    me = 4 * lax.axis_index("x") + 2 * lax.axis_index("y") + lax.axis_index("c")

    bf = lambda t: t.astype(MXU)
    late = _late_pack(w)
    early = _all_gather(_early_pack(w, me), "early_weights_all_gather")
    late, early = lax.optimization_barrier((late, early))
    lw = {}
    lw["sems"], lw["src"], lw["land"], token = _exchange_start(late, None, PLAN_GATHER_ICI, "late_weights_ici_start")
    w_in_pt, conv_w = _early_unpack(early)
    full = {n: (conv_w[n] if n in conv_w else w[n][0]) for n in WEIGHTS if n not in BIG}
    full["norm_mix_pre"] = full["norm_mix_pre"] + token[0, 0]

    def after_ssd(after):
        src, land = _exchange_wait(lw["sems"], lw["src"], lw["land"], after, PLAN_GATHER_ICI, "late_weights_ici_wait")
        lw["sems"], lw["src"], lw["land"], tok = _exchange_start(src, land, PLAN_GATHER_D2D, "late_weights_d2d_start")
        return tok[0, 0]

    def late_weights(after):
        src, land = _exchange_wait(lw["sems"], lw["src"], lw["land"], after, PLAN_GATHER_D2D, "late_weights_d2d_wait")
        return _late_unpack(_own_slot(land, src))

    sent = {}

    def send_mlp_grads(d_w_up_t, d_w_down, d_w_out):
        srcs = [bf(d).reshape(N_DEV, -1, D) for d in (d_w_up_t, d_w_down, d_w_out)]
        sent["sems"], sent["srcs"], sent["land"], tok = _scatter_parts_start(srcs, "mlp_grads_start")
        return tok[0, 0]

    def send_late_grads(d_w_in_pt):
        src = _scatter_w_in_grad(d_w_in_pt)
        chip = _pair_sum(src, _pair_exchange(src, "late_grads_pair_exchange"), "late_grads_pair_sum")
        sent["sems2"], sent["src2"], sent["land2"], tok = _exchange_start(chip, None, PLAN_SCATTER_CHIPS,
                                                                          "late_grads_start")
        return tok[0, 0]

    loss, grad_x, g = _local_step(x[0], loss_target[0], _prep_params(full, dict(w_in_pT=w_in_pt)), after_ssd,
                                  late_weights, send_mlp_grads, send_late_grads)

    out_g, out_d, out_m, out_v = {}, {}, {}, {}
    _, mlp_land = _scatter_parts_wait(sent["sems"], sent["srcs"], sent["land"], grad_x, "mlp_grads_wait")
    g_mlp = _slot_sum(mlp_land, "slot_sum_mlp")
    fs = FF // N_DEV
    for n, gn in (("w_up", g_mlp[:fs].T[None]), ("w_down", g_mlp[fs:2 * fs][None]), ("w_out", g_mlp[2 * fs:][None])):
        out_g[n] = gn
        out_d[n], out_m[n], out_v[n] = _adam_big(w[n], gn, m[n], v[n], "adam_" + n)
    zrow = jnp.zeros((1, D), F32)
    pad16 = lambda a: jnp.pad(a, ((0, 0), (0, 128 - NH)))
    small_parts = [
        jnp.concatenate([g["norm_mix_pre"], g["ssm_norm"], g["conv_lru_b"], g["lru_lambda"], g["norm_mix_post"],
                         g["norm_mlp_pre"], g["norm_mlp_post"], zrow], axis=0),
        g["conv_ssm_b"],
        jnp.concatenate([pad16(g["dt_bias"]), pad16(g["a_log"]), pad16(g["d_skip"]), jnp.full((1, 128), loss, F32),
                         jnp.zeros((4, 128), F32)], axis=0),
        g["conv_ssm_w"], g["conv_lru_w"],
        jnp.concatenate([g["lru_wa"].reshape(D, 64), g["lru_wx"].reshape(D, 64)], axis=0),
        jnp.concatenate([g["lru_ba"].reshape(NH, 64), g["lru_bx"].reshape(NH, 64)], axis=0)]
    small = _pad_rows(jnp.concatenate([s.reshape(-1) for s in small_parts]), 64)
    small, _ = lax.optimization_barrier((small, out_v["w_out"]))
    small_all = _all_gather(small, "small_grads_all_gather")

    lg_src, lg_land = _exchange_wait(sent["sems2"], sent["src2"], sent["land2"], small_all, PLAN_SCATTER_CHIPS,
                                     "late_grads_wait")
    my_chip = me >> 1
    lg_own = lax.dynamic_index_in_dim(lg_src, my_chip, keepdims=True)
    g_late = _slot_sum(lax.dynamic_update_slice_in_dim(lg_land, lg_own, my_chip, axis=0), "slot_sum_late")
    gt = lax.dynamic_slice(g_late, (2 * me, 0), (IN_ROWS, D))
    dt_, mt_, vt_ = _adam_big(w["w_in"][0].T, gt, m["w_in"][0].T, v["w_in"][0].T, "adam_w_in")
    out_g["w_in"], out_d["w_in"], out_m["w_in"], out_v["w_in"] = gt.T[None], dt_.T[None], mt_.T[None], vt_.T[None]
    sflat = _slot_sum(small_all, "slot_sum_small").reshape(-1)
    groups = []
    off = 0
    for r, c in SMALL_GROUPS:
        groups.append(sflat[off:off + r * c].reshape(r, c))
        off += r * c
    groups[3] = lax.dynamic_slice_in_dim(groups[3], me * CONV_SSM_COLS, CONV_SSM_COLS, axis=1)
    groups[4] = lax.dynamic_slice_in_dim(groups[4], me * CONV_LRU_COLS, CONV_LRU_COLS, axis=1)
    wmv = [(w[n].reshape(s), m[n].reshape(s), v[n].reshape(s)) for n, s, _, _ in SMALL]
    res, loss_row = _adam_small(groups, [(gi, r0) for _, _, gi, r0 in SMALL], wmv, (2, 3))
    loss = loss_row[0, 0]
    for (n, _, _, _), (g_n, d_n, m_n, v_n) in zip(SMALL, res):
        shape = w[n].shape
        out_g[n], out_d[n], out_m[n], out_v[n] = (g_n.reshape(shape), d_n.reshape(shape), m_n.reshape(shape),
                                                  v_n.reshape(shape))
    return (loss, grad_x[None], *[out_g[n] for n in WEIGHTS], *[out_d[n] for n in WEIGHTS],
            *[out_m[n] for n in WEIGHTS], *[out_v[n] for n in WEIGHTS])
```

```python
import functools

import jax
import jax.numpy as jnp
from jax import lax
from jax.experimental import pallas as pl
from jax.experimental.pallas import tpu as pltpu

F32 = jnp.float32
MXU = jnp.bfloat16
HI = lax.Precision.HIGHEST
EPS = 1e-6

D = 1024
NH = 16
NS = 128
CH = 128
FF = 4096
NP = 7168
SEG = 1024
LRU_C = 8.0
N_DEV = 8

ADAM_LR, ADAM_B1, ADAM_B2, ADAM_EPS, ADAM_WD, ADAM_STEP = 0.001, 0.9, 0.999, 1e-08, 0.01, 10

VMEM_LIMIT = 56 * 1024 * 1024


def _cp(*sem):
    return pltpu.CompilerParams(dimension_semantics=sem, vmem_limit_bytes=VMEM_LIMIT)


def _nn(a, b):
    return jnp.dot(a.astype(MXU), b.astype(MXU), preferred_element_type=F32)


def _nt(a, b):
    return lax.dot_general(a.astype(MXU), b.astype(MXU), (((1,), (1,)), ((), ())), preferred_element_type=F32)


def _tn(a, b):
    return lax.dot_general(a.astype(MXU), b.astype(MXU), (((0,), (0,)), ((), ())), preferred_element_type=F32)


_sigmoid = jax.nn.sigmoid


def _silu(x):
    return x * _sigmoid(x)


def _dsilu(x):
    s = _sigmoid(x)
    return s + x * s * (1.0 - s)


def _softplus(x):
    return jnp.maximum(x, 0.0) + jnp.log(1.0 + jnp.exp(-jnp.abs(x)))


def _rms(x, g):
    r = lax.rsqrt(jnp.mean(x * x, axis=-1, keepdims=True) + EPS)
    return x * r * g


def _rms_bwd(x, g, dy):
    r = lax.rsqrt(jnp.mean(x * x, axis=-1, keepdims=True) + EPS)
    gdy = g * dy
    dx = r * gdy - x * (r * r * r) * jnp.mean(x * gdy, axis=-1, keepdims=True)
    return dx, dy * x * r


def _rowsum(x):
    return jnp.sum(x, axis=0, keepdims=True)


def _taps_past(cur, prev8):
    r_n, c_n = cur.shape
    row = lax.broadcasted_iota(jnp.int32, (r_n, c_n), 0)
    out = []
    for k in range(4):
        s = 3 - k
        if s == 0:
            out.append(cur)
            continue
        head = jnp.concatenate([pltpu.roll(prev8, s, 0), jnp.zeros((r_n - 8, c_n), F32)], axis=0)
        out.append(jnp.where(row < s, head, pltpu.roll(cur, s, 0)))
    return out


def _taps_future(cur, fut8):
    r_n, c_n = cur.shape
    row = lax.broadcasted_iota(jnp.int32, (r_n, c_n), 0)
    out = []
    for k in range(4):
        s = 3 - k
        if s == 0:
            out.append(cur)
            continue
        tail = jnp.concatenate([jnp.zeros((r_n - 8, c_n), F32), pltpu.roll(fut8, 8 - s, 0)], axis=0)
        out.append(jnp.where(row >= r_n - s, tail, pltpu.roll(cur, r_n - s, 0)))
    return out


def _conv_apply(taps, w, b):
    acc = taps[0] * w[0:1, :]
    for k in range(1, 4):
        acc = acc + taps[k] * w[k:k + 1, :]
    return acc + b


def _inproj(x, g0, w):
    t_n = x.shape[0]
    tm = min(t_n, 1024)

    n_j = NP // SEG

    def body(x_ref, g_ref, w_ref, pb_ref, p6_ref, u_ref):
        j = pl.program_id(1)

        @pl.when(j == 0)
        def _():
            u_ref[...] = _rms(x_ref[...], g_ref[...]).astype(MXU)

        p = lax.dot_general(u_ref[...], w_ref[...], (((1,), (1,)), ((), ())), preferred_element_type=F32)

        @pl.when(j < n_j - 1)
        def _():
            pb_ref[...] = p.astype(MXU)

        @pl.when(j == n_j - 1)
        def _():
            p6_ref[...] = p

    pb, p6, u = pl.pallas_call(
        body, name="inproj", grid=(t_n // tm, n_j),
        in_specs=[pl.BlockSpec((tm, D), lambda i, j: (i, 0)), pl.BlockSpec((1, D), lambda i, j: (0, 0)),
                  pl.BlockSpec((SEG, D), lambda i, j: (j, 0))],
        out_specs=[pl.BlockSpec((tm, SEG), lambda i, j: (i, jnp.minimum(j, n_j - 2))),
                   pl.BlockSpec((tm, SEG), lambda i, j: (i, 0)), pl.BlockSpec((tm, D), lambda i, j: (i, 0))],
        out_shape=[jax.ShapeDtypeStruct((t_n, NP - SEG), MXU), jax.ShapeDtypeStruct((t_n, SEG), F32),
                   jax.ShapeDtypeStruct((t_n, D), MXU)],
        compiler_params=_cp("parallel", "arbitrary"),
    )(x, g0, w)
    return (pb, p6), u


def _ssd_prep(dtraw, dtb, alog):
    l_n = dtraw.shape[0]
    r = lax.broadcasted_iota(jnp.int32, (l_n, l_n), 0)
    c = lax.broadcasted_iota(jnp.int32, (l_n, l_n), 1)
    tril = (r >= c).astype(F32)
    triu = (r <= c).astype(F32)
    eye = (r == c).astype(F32)
    dt = _softplus(dtraw + dtb)
    adt = dt * (-jnp.exp(alog))
    ac = jnp.dot(tril, adt, preferred_element_type=F32, precision=HI)
    tn = (((0,), (0,)), ((), ()))
    ac_t = lax.dot_general(adt, triu, tn, preferred_element_type=F32, precision=HI)
    dt_t = lax.dot_general(dt, eye, tn, preferred_element_type=F32, precision=HI)
    return dt, dt_t, ac, ac_t, _rowsum(adt)


def _ssd_pair(j, xp, bg, cg, sp, dt, dt_t, ac, ac_t, aend):
    l_n = xp.shape[0]
    lane = lax.broadcasted_iota(jnp.int32, (l_n, 128), 1)
    sub = lax.broadcasted_iota(jnp.int32, (128, l_n), 0)
    lane1 = lax.broadcasted_iota(jnp.int32, (1, 128), 1)
    tri = lax.broadcasted_iota(jnp.int32, (l_n, l_n), 0) >= lax.broadcasted_iota(jnp.int32, (l_n, l_n), 1)
    lo = lax.broadcasted_iota(jnp.int32, (l_n, 128), 1) < 64
    lo_s = lax.broadcasted_iota(jnp.int32, (128, 128), 1) < 64
    cb = _nt(cg, bg)
    cs = _nn(cg, sp)
    x2 = jnp.concatenate([jnp.where(lo, xp, 0.0), jnp.where(lo, 0.0, xp)], axis=0)
    ws_, bs_, eo, ee = [], [], [], []
    for e in range(2):
        h = 2 * j + e
        ac_l = jnp.sum(jnp.where(lane == h, ac, 0.0), axis=1, keepdims=True)
        dt_l = jnp.sum(jnp.where(lane == h, dt, 0.0), axis=1, keepdims=True)
        a_end = jnp.sum(jnp.where(lane1 == h, aend, 0.0), axis=1, keepdims=True)
        ac_s, dt_s = ac_t[h:h + 1, :], dt_t[h:h + 1, :]
        decay = jnp.exp(jnp.where(tri, ac_l - ac_s, -1e30))
        ws_.append(cb * decay * dt_s)
        bs_.append(bg * (jnp.exp(a_end - ac_l) * dt_l))
        eo.append(jnp.exp(ac_l))
        ee.append(jnp.exp(a_end))
    y = _nn(jnp.concatenate(ws_, axis=1), x2) + jnp.where(lo, eo[0], eo[1]) * cs
    s_new = _tn(jnp.concatenate(bs_, axis=0), x2) + jnp.where(lo_s, ee[0], ee[1]) * sp
    return y, s_new


def _ssd_post(y, xs, z, dsk, nrm):
    y = (y + dsk * xs) * _silu(z)
    half = D // 2
    ya, yb = y[:, :half], y[:, half:]
    ya = ya * lax.rsqrt(jnp.mean(ya * ya, axis=-1, keepdims=True) + EPS)
    yb = yb * lax.rsqrt(jnp.mean(yb * yb, axis=-1, keepdims=True) + EPS)
    return jnp.concatenate([ya, yb], axis=1) * nrm


LAST_SEG = NP // SEG - 1


def _proj_ops(proj, seg_ids):
    return [proj[1] if s == LAST_SEG else proj[0] for s in seg_ids]


def _proj_specs(rows, seg_ids, order):
    return [pl.BlockSpec((rows, SEG), functools.partial(lambda i, c: (order(i), c), c=0 if s == LAST_SEG else s))
            for s in seg_ids]


def _prev_specs(rows, seg_ids, order):
    specs = []
    for s in seg_ids:
        n, c = (8, 0) if s == LAST_SEG else (16, s)
        specs.append(pl.BlockSpec((n, SEG), functools.partial(
            lambda i, n, c: (jnp.maximum(order(i) * (rows // n) - 1, 0), c), n=n, c=c)))
    return specs


def _prev8(ref):
    return ref[...] if ref.shape[0] == 8 else ref[8:16, :].astype(F32)


def _full(shape):
    return pl.BlockSpec(shape, lambda i: (0,) * len(shape))


def _slotted(a):
    return a if isinstance(a, tuple) else (a.reshape(N_DEV, a.shape[0] // N_DEV, a.shape[1]), 0)


def _slot_spec(w, rows, tile, index):
    per = rows // N_DEV
    b = w[1]
    return pl.BlockSpec((tile // per, per, D), lambda *ids: (index(*ids), b, 0))


def _ssd_fwd(proj, cwx, cwb, cbx, cbb, dtb, alog, dsk, nrm):
    t_n = proj[0].shape[0]
    n_c = t_n // CH
    fwd = lambda i: i

    def body(z_ref, xs_ref, bc_ref, xsp_ref, bcp_ref, cwx_ref, cwb_ref, cbx_ref, cbb_ref, dtb_ref, alog_ref,
             dsk_ref, nrm_ref, ya_ref, sprev_ref, yraw_ref, xspre_ref, bcpre_ref, s_ref):
        c = pl.program_id(0)

        @pl.when(c == 0)
        def _():
            s_ref[...] = jnp.zeros_like(s_ref)

        keep = jnp.where(c == 0, 0.0, 1.0)
        xs_pre = _conv_apply(_taps_past(xs_ref[...].astype(F32), _prev8(xsp_ref) * keep), cwx_ref[...], cbx_ref[...])
        bc_pre = _conv_apply(_taps_past(bc_ref[:, :512], bcp_ref[:, :512] * keep), cwb_ref[...], cbb_ref[...])
        xspre_ref[...] = xs_pre
        bcpre_ref[...] = bc_pre
        prep = _ssd_prep(bc_ref[:, 512:640], dtb_ref[...], alog_ref[...])
        xs = _silu(xs_pre)
        bc = _silu(bc_pre)
        sprev_ref[0] = s_ref[...]
        ys = []
        for j in range(NH // 2):
            g = j // 4
            yp, sn = _ssd_pair(j, xs[:, 128 * j:128 * j + 128], bc[:, 128 * g:128 * g + 128],
                               bc[:, 256 + 128 * g:384 + 128 * g], s_ref[:, 128 * j:128 * j + 128], *prep)
            ys.append(yp)
            s_ref[:, 128 * j:128 * j + 128] = sn
        y = jnp.concatenate(ys, axis=1)
        yraw_ref[...] = y
        ya_ref[...] = _ssd_post(y, xs, z_ref[...].astype(F32), dsk_ref[...], nrm_ref[...]).astype(ya_ref.dtype)

    return pl.pallas_call(
        body, name="ssd_fwd", grid=(n_c,),
        in_specs=_proj_specs(CH, (0, 5, 6), fwd) + _prev_specs(CH, (5, 6), fwd) + [
            _full((4, D)), _full((4, 512)), _full((1, D)), _full((1, 512)), _full((1, 128)), _full((1, 128)),
            _full((1, D)), _full((1, D))],
        out_specs=[pl.BlockSpec((CH, D), lambda i: (i, 0)), pl.BlockSpec((1, NS, D), lambda i: (i, 0, 0)),
                   pl.BlockSpec((CH, D), lambda i: (i, 0)), pl.BlockSpec((CH, D), lambda i: (i, 0)),
                   pl.BlockSpec((CH, 512), lambda i: (i, 0))],
        out_shape=[jax.ShapeDtypeStruct((t_n, D), MXU), jax.ShapeDtypeStruct((n_c, NS, D), F32),
                   jax.ShapeDtypeStruct((t_n, D), F32), jax.ShapeDtypeStruct((t_n, D), F32),
                   jax.ShapeDtypeStruct((t_n, 512), F32)],
        scratch_shapes=[pltpu.VMEM((NS, D), F32)],
        compiler_params=_cp("arbitrary"),
    )(*_proj_ops(proj, (0, 5, 6, 5, 6)), cwx, cwb, cbx, cbb, dtb, alog, dsk, nrm)


def _ssd_bwd(dya, saved, proj, sprev, cwx, cwb, dtb, alog, dsk, nrm, mlp_ops):
    t_n = proj[0].shape[0]
    n_c = t_n // CH
    rev = lambda i: n_c - 1 - i
    fb = FF // n_c

    def body(dya_ref, yraw_ref, xspre_ref, bcpre_ref, z_ref, xs_ref, bc_ref, sprev_ref, cwx_ref, cwb_ref,
             dtb_ref, alog_ref, dsk_ref, nrm_ref, hid_ref, dff_ref, dhp_ref, v_ref,
             dz_ref, dxs_ref, dbc_ref, dcwx_ref, dcwb_ref, dcbx_ref, dcbb_ref, ddtb_ref, dalog_ref, ddsk_ref,
             dnrm_ref, dwd_ref, dwu_ref, ds_ref, futx_ref, futb_ref):
        i = pl.program_id(0)
        acc_refs = (dcwx_ref, dcwb_ref, dcbx_ref, dcbb_ref, ddtb_ref, dalog_ref, ddsk_ref, dnrm_ref)
        tn = (((0,), (0,)), ((), ()))
        kt = t_n // (NH // 2)
        dwd_acc = jnp.zeros((fb, D), F32)
        dwu_acc = jnp.zeros((fb, D), F32)

        @pl.when(i == 0)
        def _():
            for r in (ds_ref, futx_ref, futb_ref) + acc_refs:
                r[...] = jnp.zeros_like(r)

        xs_pre = xspre_ref[...]
        bc_pre = bcpre_ref[...]
        xs = _silu(xs_pre)
        bc = _silu(bc_pre)
        prep, prep_vjp = jax.vjp(_ssd_prep, bc_ref[:, 512:640], dtb_ref[...], alog_ref[...])
        s_in = sprev_ref[0]

        def pair_args(j):
            g = j // 4
            return (xs[:, 128 * j:128 * j + 128], bc[:, 128 * g:128 * g + 128],
                    bc[:, 256 + 128 * g:384 + 128 * g], s_in[:, 128 * j:128 * j + 128]) + tuple(prep)

        _, post_vjp = jax.vjp(_ssd_post, yraw_ref[...], xs, z_ref[...].astype(F32), dsk_ref[...], nrm_ref[...])
        dy, dxs_skip, dz, ddsk, dnrm = post_vjp(dya_ref[...])
        dz_ref[...] = dz.astype(dz_ref.dtype)
        ddsk_ref[...] += ddsk
        dnrm_ref[...] += dnrm

        dprep = [jnp.zeros_like(p) for p in prep]
        dxp = []
        dbg = [jnp.zeros((CH, 128), F32), jnp.zeros((CH, 128), F32)]
        dcg = [jnp.zeros((CH, 128), F32), jnp.zeros((CH, 128), F32)]
        for j in range(NH // 2):
            g = j // 4
            _, pair_vjp = jax.vjp(functools.partial(_ssd_pair, j), *pair_args(j))
            cts = pair_vjp((dy[:, 128 * j:128 * j + 128], ds_ref[:, 128 * j:128 * j + 128]))
            dxp.append(cts[0])
            dbg[g] = dbg[g] + cts[1]
            dcg[g] = dcg[g] + cts[2]
            ds_ref[:, 128 * j:128 * j + 128] = cts[3]
            dprep = [a + b for a, b in zip(dprep, cts[4:])]
            rows = pl.ds(kt * j, kt)
            dwd_acc = dwd_acc + lax.dot_general(hid_ref[rows, :], dff_ref[rows, :], tn, preferred_element_type=F32)
            dwu_acc = dwu_acc + lax.dot_general(dhp_ref[rows, :], v_ref[rows, :], tn, preferred_element_type=F32)
        dwd_ref[...] = dwd_acc.astype(MXU)
        dwu_ref[...] = dwu_acc.astype(MXU)
        ddtraw, ddtb, dalog = prep_vjp(tuple(dprep))
        ddtb_ref[...] += ddtb
        dalog_ref[...] += dalog

        dxs_pre = (dxs_skip + jnp.concatenate(dxp, axis=1)) * _dsilu(xs_pre)
        dbc_pre = jnp.concatenate([dbg[0], dbg[1], dcg[0], dcg[1]], axis=1) * _dsilu(bc_pre)
        dcbx_ref[...] += _rowsum(dxs_pre)
        dcbb_ref[...] += _rowsum(dbc_pre)
        fx = _taps_future(dxs_pre, futx_ref[...])
        fbc = _taps_future(dbc_pre, futb_ref[...])
        xs_in = xs_ref[...].astype(F32)
        bc_in = bc_ref[:, :512]
        for k in range(4):
            dcwx_ref[k:k + 1, :] += _rowsum(fx[k] * xs_in)
            dcwb_ref[k:k + 1, :] += _rowsum(fbc[k] * bc_in)
        cwx = cwx_ref[...]
        cwb = cwb_ref[...]
        dxs_in = fx[0] * cwx[0:1, :]
        dbc_in = fbc[0] * cwb[0:1, :]
        for k in range(1, 4):
            dxs_in = dxs_in + fx[k] * cwx[k:k + 1, :]
            dbc_in = dbc_in + fbc[k] * cwb[k:k + 1, :]
        futx_ref[...] = dxs_pre[0:8, :]
        futb_ref[...] = dbc_pre[0:8, :]
        dxs_ref[...] = dxs_in.astype(dxs_ref.dtype)
        dbc_ref[...] = jnp.concatenate([dbc_in, ddtraw, jnp.zeros((CH, SEG - 640), F32)], axis=1).astype(dbc_ref.dtype)

    row_out = lambda: pl.BlockSpec((CH, D), lambda i: (rev(i), 0))
    outs = pl.pallas_call(
        body, name="ssd_bwd", grid=(n_c,),
        in_specs=[row_out(), row_out(), row_out(), pl.BlockSpec((CH, 512), lambda i: (rev(i), 0))]
        + _proj_specs(CH, (0, 5, 6), rev) + [pl.BlockSpec((1, NS, D), lambda i: (rev(i), 0, 0)),
                                             _full((4, D)), _full((4, 512)),
                                             _full((1, 128)), _full((1, 128)), _full((1, D)), _full((1, D)),
                                             pl.BlockSpec((t_n, fb), lambda i: (0, i)), _full((t_n, D)),
                                             pl.BlockSpec((t_n, fb), lambda i: (0, i)), _full((t_n, D))],
        out_specs=[row_out(), row_out(), row_out(), _full((4, D)), _full((4, 512)), _full((1, D)), _full((1, 512)),
                   _full((1, 128)), _full((1, 128)), _full((1, D)), _full((1, D)),
                   pl.BlockSpec((fb, D), lambda i: (i, 0)), pl.BlockSpec((fb, D), lambda i: (i, 0))],
        out_shape=[jax.ShapeDtypeStruct((t_n, D), MXU)] * 3 + [
            jax.ShapeDtypeStruct(s, F32) for s in ((4, D), (4, 512), (1, D), (1, 512), (1, 128), (1, 128), (1, D), (1, D))]
        + [jax.ShapeDtypeStruct((FF, D), MXU)] * 2,
        scratch_shapes=[pltpu.VMEM((NS, D), F32), pltpu.VMEM((8, D), F32), pltpu.VMEM((8, 512), F32)],
        compiler_params=_cp("arbitrary"),
    )(dya, *saved, *_proj_ops(proj, (0, 5, 6)), sprev, cwx, cwb, dtb, alog, dsk, nrm, *mlp_ops)
    return outs


LRU_ROWS = 256
LRU_BLK = 256


def _lru_gates(xr, wa, wx, ba, bx, lam):
    pr = jnp.concatenate([_nn(xr[:, LRU_BLK * b:LRU_BLK * (b + 1)], wa[b]) for b in range(D // LRU_BLK)], axis=1) + ba
    pi = jnp.concatenate([_nn(xr[:, LRU_BLK * b:LRU_BLK * (b + 1)], wx[b]) for b in range(D // LRU_BLK)], axis=1) + bx
    log_a = -LRU_C * _sigmoid(pr) * _softplus(-lam)
    a = jnp.exp(log_a)
    mult = jnp.sqrt(1.0 - jnp.exp(2.0 * log_a))
    return a, mult * (_sigmoid(pi) * xr)


def _lru_out(h, g):
    return h * jax.nn.gelu(g, approximate=True)


def _lru_fwd(proj, cw, cb, wa, wx, ba, bx, lam):
    t_n = proj[0].shape[0]
    rows = min(LRU_ROWS, t_n)
    fwd = lambda i: i

    def body(g_ref, x_ref, xp_ref, cw_ref, cb_ref, wa_ref, wx_ref, ba_ref, bx_ref, lam_ref, yb_ref, h_ref, xr_ref,
             a_s, u_s, carry):
        i = pl.program_id(0)

        @pl.when(i == 0)
        def _():
            carry[...] = jnp.zeros_like(carry)

        keep = jnp.where(i == 0, 0.0, 1.0)
        xr = _conv_apply(_taps_past(x_ref[...].astype(F32), _prev8(xp_ref) * keep), cw_ref[...], cb_ref[...])
        xr_ref[...] = xr
        a, u = _lru_gates(xr, wa_ref[...], wx_ref[...], ba_ref[...], bx_ref[...], lam_ref[...])
        a_s[...] = a
        u_s[...] = u
        row = lax.broadcasted_iota(jnp.int32, (8, D), 0)

        def blk(b, c):
            s = pl.multiple_of(b * 8, 8)
            av = a_s[pl.ds(s, 8), :]
            uv = u_s[pl.ds(s, 8), :]
            for d in (1, 2, 4):
                m = row >= d
                uv = uv + av * jnp.where(m, pltpu.roll(uv, d, 0), 0.0)
                av = av * jnp.where(m, pltpu.roll(av, d, 0), 1.0)
            hv = uv + av * c
            h_ref[pl.ds(s, 8), :] = hv
            return hv[7:8, :]

        carry[0:1, :] = lax.fori_loop(0, rows // 8, blk, carry[0:1, :])
        yb_ref[...] = _lru_out(h_ref[...], g_ref[...].astype(F32)).astype(yb_ref.dtype)

    return pl.pallas_call(
        body, name="lru_fwd", grid=(t_n // rows,),
        in_specs=_proj_specs(rows, (1, 2), fwd) + _prev_specs(rows, (2,), fwd) + [
            _full((4, D)), _full((1, D)), _full((4, LRU_BLK, LRU_BLK)), _full((4, LRU_BLK, LRU_BLK)),
            _full((1, D)), _full((1, D)), _full((1, D))],
        out_specs=[pl.BlockSpec((rows, D), lambda i: (i, 0))] * 3,
        out_shape=[jax.ShapeDtypeStruct((t_n, D), MXU), jax.ShapeDtypeStruct((t_n, D), F32),
                   jax.ShapeDtypeStruct((t_n, D), F32)],
        scratch_shapes=[pltpu.VMEM((rows, D), F32), pltpu.VMEM((rows, D), F32), pltpu.VMEM((8, D), F32)],
        compiler_params=_cp("arbitrary"),
    )(*_proj_ops(proj, (1, 2, 2)), cw, cb, wa, wx, ba, bx, lam)


def _lru_bwd(dyb, proj, h, xr_saved, cw, wa, wx, ba, bx, lam):
    t_n = proj[0].shape[0]
    rows = min(LRU_ROWS, t_n)
    n_t = t_n // rows
    rev = lambda i: n_t - 1 - i
    rb = rows // 8

    def body(dyb_ref, g_ref, x_ref, h_ref, hp_ref, xr_ref, cw_ref, wa_ref, wx_ref, ba_ref, bx_ref, lam_ref,
             dg_ref, dx_ref, dcw_ref, dcb_ref, dwa_ref, dwx_ref, dba_ref, dbx_ref, dlam_ref,
             a_s, dh_s, hx_s, da_s, du_s, carry, fut):
        i = pl.program_id(0)
        acc_refs = (dcw_ref, dcb_ref, dwa_ref, dwx_ref, dba_ref, dbx_ref, dlam_ref)

        @pl.when(i == 0)
        def _():
            for r in (carry, fut) + acc_refs:
                r[...] = jnp.zeros_like(r)

        keep = jnp.where(i == n_t - 1, 0.0, 1.0)
        gate_in = (xr_ref[...], wa_ref[...], wx_ref[...], ba_ref[...], bx_ref[...], lam_ref[...])
        (a, _), gates_vjp = jax.vjp(_lru_gates, *gate_in)
        _, out_vjp = jax.vjp(_lru_out, h_ref[...], g_ref[...].astype(F32))
        dh, dg = out_vjp(dyb_ref[...])
        dg_ref[...] = dg.astype(dg_ref.dtype)
        a_s[...] = a
        dh_s[...] = dh
        hx_s[0:8, :] = hp_ref[...] * keep
        hx_s[8:, :] = h_ref[...]
        row = lax.broadcasted_iota(jnp.int32, (8, D), 0)

        def blk(b, c):
            s = pl.multiple_of((rb - 1 - b) * 8, 8)
            av = a_s[pl.ds(s, 8), :]
            dhv = dh_s[pl.ds(s, 8), :]
            kv = av * dhv
            for d in (1, 2, 4):
                m = row <= 7 - d
                kv = kv + av * jnp.where(m, pltpu.roll(kv, 8 - d, 0), 0.0)
                av = av * jnp.where(m, pltpu.roll(av, 8 - d, 0), 1.0)
            kv = kv + av * c
            gv = dhv + jnp.where(row < 7, pltpu.roll(kv, 7, 0), c)
            hb = hx_s[pl.ds(s + 8, 8), :]
            hpv = hx_s[pl.ds(s, 8), :]
            hprev = jnp.where(row >= 1, pltpu.roll(hb, 1, 0), hpv[7:8, :])
            du_s[pl.ds(s, 8), :] = gv
            da_s[pl.ds(s, 8), :] = gv * hprev
            return kv[0:1, :]

        carry[0:1, :] = lax.fori_loop(0, rb, blk, carry[0:1, :])
        dxr, dwa, dwx, dba, dbx, dlam = gates_vjp((da_s[...], du_s[...]))
        dwa_ref[...] += dwa
        dwx_ref[...] += dwx
        dba_ref[...] += dba
        dbx_ref[...] += dbx
        dlam_ref[...] += dlam
        dcb_ref[...] += _rowsum(dxr)
        ft = _taps_future(dxr, fut[...])
        x_in = x_ref[...].astype(F32)
        for k in range(4):
            dcw_ref[k:k + 1, :] += _rowsum(ft[k] * x_in)
        cwv = cw_ref[...]
        dx = ft[0] * cwv[0:1, :]
        for k in range(1, 4):
            dx = dx + ft[k] * cwv[k:k + 1, :]
        fut[...] = dxr[0:8, :]
        dx_ref[...] = dx.astype(dx_ref.dtype)

    row_in = lambda: pl.BlockSpec((rows, D), lambda i: (rev(i), 0))
    prev_h = pl.BlockSpec((8, D), lambda i: (jnp.maximum(rev(i) * rb - 1, 0), 0))
    wspec = lambda: _full((4, LRU_BLK, LRU_BLK))
    return pl.pallas_call(
        body, name="lru_bwd", grid=(n_t,),
        in_specs=[row_in()] + _proj_specs(rows, (1, 2), rev) + [row_in(), prev_h, row_in()] + [
            _full((4, D)), wspec(), wspec(), _full((1, D)), _full((1, D)), _full((1, D))],
        out_specs=[row_in(), row_in(), _full((4, D)), _full((1, D)), wspec(), wspec(), _full((1, D)), _full((1, D)),
                   _full((1, D))],
        out_shape=[jax.ShapeDtypeStruct((t_n, D), MXU)] * 2 + [
            jax.ShapeDtypeStruct(s, F32) for s in ((4, D), (1, D), (4, LRU_BLK, LRU_BLK), (4, LRU_BLK, LRU_BLK),
                                                   (1, D), (1, D), (1, D))],
        scratch_shapes=[pltpu.VMEM((rows, D), F32), pltpu.VMEM((rows, D), F32), pltpu.VMEM((rows + 8, D), F32),
                        pltpu.VMEM((rows, D), F32), pltpu.VMEM((rows, D), F32), pltpu.VMEM((8, D), F32),
                        pltpu.VMEM((8, D), F32)],
        compiler_params=_cp("arbitrary"),
    )(dyb, *_proj_ops(proj, (1, 2)), h, h, xr_saved, cw, wa, wx, ba, bx, lam)


def _merge_out(ya, yb, proj, x, wout, g1):
    t_n = x.shape[0]
    tm = min(t_n, 512)

    def body(ya_ref, yb_ref, ga_ref, gb_ref, x_ref, w_ref, g_ref, h1_ref, mix_ref, mg_ref):
        merged = (_sigmoid(ga_ref[...].astype(F32)) * ya_ref[...].astype(F32)
                  + _sigmoid(gb_ref[...].astype(F32)) * yb_ref[...].astype(F32))
        mg = merged.astype(MXU)
        mg_ref[...] = mg
        mix = jnp.dot(mg, w_ref[...].reshape(D, D), preferred_element_type=F32)
        mix_ref[...] = mix
        h1_ref[...] = x_ref[...] + _rms(mix, g_ref[...])

    row = lambda: pl.BlockSpec((tm, D), lambda i: (i, 0))
    return pl.pallas_call(
        body, name="merge_out", grid=(t_n // tm,),
        in_specs=[row(), row()] + _proj_specs(tm, (3, 4), lambda i: i)
        + [row(), _slot_spec(wout, D, D, lambda i: 0), _full((1, D))],
        out_specs=[row(), row(), row()],
        out_shape=[jax.ShapeDtypeStruct((t_n, D), F32), jax.ShapeDtypeStruct((t_n, D), F32),
                   jax.ShapeDtypeStruct((t_n, D), MXU)],
        compiler_params=_cp("parallel"),
    )(ya, yb, *_proj_ops(proj, (3, 4)), x, wout[0], g1)


def _out_bwd(dv, h1, g2, dout, mix, ya, yb, proj, wout, g1, w_in_pt):
    t_n = dv.shape[0]
    tm = min(t_n, 256)

    def body(dv_ref, h1_ref, g2_ref, dout_ref, mix_ref, ya_ref, yb_ref, ga_ref, gb_ref, w_ref, g_ref, wa_ref, wb_ref,
             dh1_ref, dmix_ref, dya_ref, dyb_ref, dga_ref, dgb_ref, dg2_ref, dg1_ref, dug_ref):
        @pl.when(pl.program_id(0) == 0)
        def _():
            dg1_ref[...] = jnp.zeros_like(dg1_ref)
            dg2_ref[...] = jnp.zeros_like(dg2_ref)

        dx, dg_rows = _rms_bwd(h1_ref[...], g2_ref[...], dv_ref[...])
        dg2_ref[...] += _rowsum(dg_rows)
        dh1 = dout_ref[...] + dx
        dh1_ref[...] = dh1
        dmix, dg_rows = _rms_bwd(mix_ref[...], g_ref[...], dh1)
        dg1_ref[...] += _rowsum(dg_rows)
        dmix_b = dmix.astype(MXU)
        dmix_ref[...] = dmix_b
        dmg = lax.dot_general(dmix_b, w_ref[...].reshape(D, D), (((1,), (1,)), ((), ())), preferred_element_type=F32)
        sa = _sigmoid(ga_ref[...].astype(F32))
        sb = _sigmoid(gb_ref[...].astype(F32))
        dya_ref[...] = dmg * sa
        dyb_ref[...] = dmg * sb
        dga = (dmg * ya_ref[...].astype(F32) * sa * (1.0 - sa)).astype(MXU)
        dga_ref[...] = dga
        dug = jnp.dot(dga, wa_ref[...], preferred_element_type=F32)
        dgb = (dmg * yb_ref[...].astype(F32) * sb * (1.0 - sb)).astype(MXU)
        dgb_ref[...] = dgb
        dug_ref[...] = dug + jnp.dot(dgb, wb_ref[...], preferred_element_type=F32)

    row = lambda: pl.BlockSpec((tm, D), lambda i: (i, 0))
    vec = lambda: _full((1, D))
    seg = lambda s: pl.BlockSpec((SEG, D), lambda i: (s, 0))
    return pl.pallas_call(
        body, name="out_bwd", grid=(t_n // tm,),
        in_specs=[row(), row(), vec(), row(), row(), row(), row()] + _proj_specs(tm, (3, 4), lambda i: i)
        + [_slot_spec(wout, D, D, lambda i: 0), vec(), seg(3), seg(4)],
        out_specs=[row(), row(), row(), row(), row(), row(), vec(), vec(), row()],
        out_shape=[jax.ShapeDtypeStruct((t_n, D), F32), jax.ShapeDtypeStruct((t_n, D), MXU),
                   jax.ShapeDtypeStruct((t_n, D), F32), jax.ShapeDtypeStruct((t_n, D), F32),
                   jax.ShapeDtypeStruct((t_n, D), MXU), jax.ShapeDtypeStruct((t_n, D), MXU),
                   jax.ShapeDtypeStruct((1, D), F32), jax.ShapeDtypeStruct((1, D), F32),
                   jax.ShapeDtypeStruct((t_n, D), F32)],
        compiler_params=_cp("arbitrary"),
    )(dv, h1, g2, dout, mix, ya, yb, *_proj_ops(proj, (3, 4)), wout[0], g1, w_in_pt, w_in_pt)


MLP_TM = 1024
MLP_TF_FWD = 1024
MLP_TF_BWD = 1024


def _mlp_fwd(h1, g2, wup, wdown, g3, tgt):
    t_n = h1.shape[0]
    tm = min(t_n, MLP_TM)
    n_f = FF // MLP_TF_FWD

    def body(h1_ref, g2_ref, wu_ref, wd_ref, g3_ref, tgt_ref, hp_ref, v_ref, dout_ref, dff_ref, loss_ref, dg3_ref, acc):
        i, j = pl.program_id(0), pl.program_id(1)

        @pl.when((i == 0) & (j == 0))
        def _():
            loss_ref[...] = jnp.zeros_like(loss_ref)
            dg3_ref[...] = jnp.zeros_like(dg3_ref)

        @pl.when(j == 0)
        def _():
            v_ref[...] = _rms(h1_ref[...], g2_ref[...]).astype(MXU)
            acc[...] = jnp.zeros_like(acc)

        hp = lax.dot_general(v_ref[...], wu_ref[...].reshape(MLP_TF_FWD, D), (((1,), (1,)), ((), ())),
                             preferred_element_type=F32)
        hp_ref[...] = hp.astype(MXU)
        hid = jnp.square(jnp.maximum(hp, 0.0))
        acc[...] += jnp.dot(hid.astype(MXU), wd_ref[...].reshape(MLP_TF_FWD, D), preferred_element_type=F32)

        @pl.when(j == n_f - 1)
        def _():
            ff = acc[...]
            err = h1_ref[...] + _rms(ff, g3_ref[...]) - tgt_ref[...]
            loss_ref[...] += 0.5 * jnp.sum(jnp.mean(err * err, axis=-1, keepdims=True), axis=0, keepdims=True)
            dout = err * (1.0 / D)
            dout_ref[...] = dout
            dff, dg_rows = _rms_bwd(ff, g3_ref[...], dout)
            dg3_ref[...] += _rowsum(dg_rows)
            dff_ref[...] = dff.astype(MXU)

    row = lambda: pl.BlockSpec((tm, D), lambda i, j: (i, 0))
    vec = lambda: pl.BlockSpec((1, D), lambda i, j: (0, 0))
    return pl.pallas_call(
        body, name="mlp_fwd", grid=(t_n // tm, n_f),
        in_specs=[row(), vec(), _slot_spec(wup, FF, MLP_TF_FWD, lambda i, j: j),
                  _slot_spec(wdown, FF, MLP_TF_FWD, lambda i, j: j), vec(), row()],
        out_specs=[pl.BlockSpec((tm, MLP_TF_FWD), lambda i, j: (i, j)), row(), row(), row(),
                   pl.BlockSpec((1, 1), lambda i, j: (0, 0)), vec()],
        out_shape=[jax.ShapeDtypeStruct((t_n, FF), MXU), jax.ShapeDtypeStruct((t_n, D), MXU),
                   jax.ShapeDtypeStruct((t_n, D), F32), jax.ShapeDtypeStruct((t_n, D), MXU),
                   jax.ShapeDtypeStruct((1, 1), F32), jax.ShapeDtypeStruct((1, D), F32)],
        scratch_shapes=[pltpu.VMEM((tm, D), F32)],
        compiler_params=_cp("arbitrary", "arbitrary"),
    )(h1, g2, wup[0], wdown[0], g3, tgt)


def _mlp_bwd(dff, hp, wup, wdown):
    t_n = dff.shape[0]
    tm = min(t_n, MLP_TM)
    n_f = FF // MLP_TF_BWD

    def mm_body(dff_ref, hp_ref, wu_ref, wd_ref, dv_ref, dhp_ref, hid_ref):
        @pl.when(pl.program_id(1) == 0)
        def _():
            dv_ref[...] = jnp.zeros_like(dv_ref)

        relu = jnp.maximum(hp_ref[...].astype(F32), 0.0)
        hid_ref[...] = jnp.square(relu).astype(MXU)
        dhid = lax.dot_general(dff_ref[...], wd_ref[...].reshape(MLP_TF_BWD, D), (((1,), (1,)), ((), ())),
                               preferred_element_type=F32)
        dhp = (dhid * (2.0 * relu)).astype(MXU)
        dhp_ref[...] = dhp
        dv_ref[...] += jnp.dot(dhp, wu_ref[...].reshape(MLP_TF_BWD, D), preferred_element_type=F32)

    row = lambda: pl.BlockSpec((tm, D), lambda i, j: (i, 0))
    blk = lambda: pl.BlockSpec((tm, MLP_TF_BWD), lambda i, j: (i, j))
    wblk = lambda w: _slot_spec(w, FF, MLP_TF_BWD, lambda i, j: j)
    return pl.pallas_call(
        mm_body, name="mlp_bwd", grid=(t_n // tm, n_f),
        in_specs=[row(), blk(), wblk(wup), wblk(wdown)], out_specs=[row(), blk(), blk()],
        out_shape=[jax.ShapeDtypeStruct((t_n, D), F32), jax.ShapeDtypeStruct((t_n, FF), MXU),
                   jax.ShapeDtypeStruct((t_n, FF), MXU)],
        compiler_params=_cp("parallel", "arbitrary"),
    )(dff, hp, wup[0], wdown[0])


def _wgrad(a, g, name):
    t_n, k_n = a.shape
    n_n = g.shape[1]
    tt = min(t_n, 1024)
    tk, tn = min(k_n, 1024), min(n_n, 1024)

    n_t = t_n // tt

    def body(a_ref, g_ref, o_ref, acc):
        t = pl.program_id(2)

        @pl.when(t == 0)
        def _():
            acc[...] = jnp.zeros_like(acc)

        acc[...] += lax.dot_general(a_ref[...], g_ref[...], (((0,), (0,)), ((), ())), preferred_element_type=F32)

        @pl.when(t == n_t - 1)
        def _():
            o_ref[...] = acc[...].astype(o_ref.dtype)

    return pl.pallas_call(
        body, name=name, grid=(k_n // tk, n_n // tn, n_t),
        in_specs=[pl.BlockSpec((tt, tk), lambda k, n, t: (t, k)), pl.BlockSpec((tt, tn), lambda k, n, t: (t, n))],
        out_specs=pl.BlockSpec((tk, tn), lambda k, n, t: (k, n)),
        out_shape=jax.ShapeDtypeStruct((k_n, n_n), MXU),
        scratch_shapes=[pltpu.VMEM((tk, tn), F32)],
        compiler_params=_cp("parallel", "parallel", "arbitrary"),
    )(a, g)


def _wgrad_segs(segs, g, name):
    t_n, n_n = g.shape
    n_s = len(segs)
    tt = min(t_n, 1024)
    n_t = t_n // tt

    def body(*refs):
        a_refs = refs[:n_s]
        g_ref, o_ref, acc = refs[n_s:]
        s_id, t = pl.program_id(0), pl.program_id(1)

        @pl.when(t == 0)
        def _():
            acc[...] = jnp.zeros_like(acc)

        for s in range(n_s):
            @pl.when(s_id == s)
            def _(s=s):
                acc[...] += lax.dot_general(a_refs[s][...], g_ref[...], (((0,), (0,)), ((), ())),
                                            preferred_element_type=F32)

        @pl.when(t == n_t - 1)
        def _():
            o_ref[...] = acc[...].astype(o_ref.dtype)

    seg_spec = lambda s: pl.BlockSpec((tt, SEG), lambda i, t: (jnp.where(i == s, t, jnp.where(i < s, 0, n_t - 1)), 0))
    return pl.pallas_call(
        body, name=name, grid=(n_s, n_t),
        in_specs=[seg_spec(s) for s in range(n_s)] + [pl.BlockSpec((tt, n_n), lambda i, t: (t, 0))],
        out_specs=pl.BlockSpec((SEG, n_n), lambda i, t: (i, 0)),
        out_shape=jax.ShapeDtypeStruct((n_s * SEG, n_n), MXU),
        scratch_shapes=[pltpu.VMEM((SEG, n_n), F32)],
        compiler_params=_cp("arbitrary", "arbitrary"),
    )(*segs, g)


def _inproj_bwd(dsegs, seg_ids, du_part, w, x, g0, dh1):
    t_n = x.shape[0]
    tm = min(t_n, 512)
    n_k = len(dsegs)

    def body(*refs):
        dp_refs = refs[:n_k]
        w_refs = refs[n_k:2 * n_k]
        part_ref, x_ref, g_ref, dh1_ref, dx_ref, dg0_ref = refs[2 * n_k:]

        @pl.when(pl.program_id(0) == 0)
        def _():
            dg0_ref[...] = jnp.zeros_like(dg0_ref)

        du = part_ref[...]
        for s in range(n_k):
            du = du + jnp.dot(dp_refs[s][...], w_refs[s][...], preferred_element_type=F32)
        dx, dg_rows = _rms_bwd(x_ref[...], g_ref[...], du)
        dg0_ref[...] += _rowsum(dg_rows)
        dx_ref[...] = dh1_ref[...] + dx

    row = lambda: pl.BlockSpec((tm, D), lambda i: (i, 0))
    return pl.pallas_call(
        body, name="inproj_bwd", grid=(t_n // tm,),
        in_specs=[row() for _ in range(n_k)]
        + [pl.BlockSpec((SEG, D), functools.partial(lambda i, s: (s, 0), s=s)) for s in seg_ids]
        + [row(), row(), _full((1, D)), row()],
        out_specs=[row(), _full((1, D))],
        out_shape=[jax.ShapeDtypeStruct((t_n, D), F32), jax.ShapeDtypeStruct((1, D), F32)],
        compiler_params=_cp("arbitrary"),
    )(*dsegs, *([w] * n_k), du_part, x, g0, dh1)


def _blockdiag4(w):
    w4 = w.reshape(4, 4, 64, 1, 64).astype(MXU)
    same = (jnp.arange(4)[:, None, None, None] == jnp.arange(4)[None, None, :, None])
    return jnp.where(same[None], w4, jnp.zeros((), MXU)).reshape(4, 256, 256)


def _blockdiag4_extract(g):
    g5 = g.reshape(4, 4, 64, 4, 64)
    return jnp.stack([g5[:, q, :, q, :] for q in range(4)], axis=1).reshape(NH, 64, 64)


def _local_step(x, tgt, p, after_ssd=None, late_weights=None, send_mlp_grads=None, send_late_grads=None):
    f = lambda a: a.astype(F32)
    proj, u = _inproj(x, p["norm_mix_pre"], p["w_in_pT"])
    ssm_params = (p["cw_xs"], p["cw_bc"], p["cb_xs"], p["cb_bc"], p["dt_bias"], p["a_log"], p["d_skip_x"],
                  p["ssm_norm"])
    ya, sprev, *ssd_saved = _ssd_fwd(proj, *ssm_params)
    cb_lru = p["conv_lru_b"] if after_ssd is None else p["conv_lru_b"] + after_ssd(ya)
    lru_params = (p["conv_lru_w"], cb_lru, p["wa_bd"], p["wx_bd"], p["lru_ba"], p["lru_bx"], p["lru_lambda"])
    yb, h, xr = _lru_fwd(proj, *lru_params)
    if late_weights is not None:
        p = dict(p, **late_weights(yb))
    p = dict(p, **{n: _slotted(p[n]) for n in ("w_out", "w_upT", "w_down")})
    h1, mix, merged = _merge_out(ya, yb, proj, x, p["w_out"], p["norm_mix_post"])
    hp, v, dout, dff, loss, dg3 = _mlp_fwd(h1, p["norm_mlp_pre"], p["w_upT"], p["w_down"], p["norm_mlp_post"], tgt)

    dv, dhp, hid = _mlp_bwd(dff, hp, p["w_upT"], p["w_down"])
    dh1, dmix, dya, dyb, dga, dgb, dg2, dg1, du_gates = _out_bwd(
        dv, h1, p["norm_mlp_pre"], dout, mix, ya, yb, proj, p["w_out"], p["norm_mix_post"], p["w_in_pT"])
    d_w_out = _wgrad(merged, dmix, "wgrad_out")
    (dz, dxs, dbc, dcwx, dcwb, dcbx, dcbb, ddtb, dalog, ddsk, dnrm, d_w_down, d_w_up_t) = _ssd_bwd(
        dya, ssd_saved, proj, sprev, p["cw_xs"], p["cw_bc"], *ssm_params[4:], (hid, dff, dhp, v))
    ba_lru = p["lru_ba"] if send_mlp_grads is None else p["lru_ba"] + send_mlp_grads(d_w_up_t, d_w_down, d_w_out)
    (dgl, dxl, dcwl, dcbl, dwa, dwx, dba, dbx, dlam) = _lru_bwd(
        dyb, proj, h, xr, p["conv_lru_w"], p["wa_bd"], p["wx_bd"], ba_lru, p["lru_bx"], p["lru_lambda"])
    dsegs = [dz, dgl, dxl, dga, dgb, dxs, dbc]
    d_w_in_pt = _wgrad_segs(dsegs, u, "wgrad_in")
    g0 = p["norm_mix_pre"]
    if send_late_grads is not None:
        g0 = g0 + send_late_grads(d_w_in_pt)
    grad_x, dg0 = _inproj_bwd([dz, dgl, dxl, dxs, dbc], (0, 1, 2, 5, 6), du_gates, p["w_in_pT"], x, g0, dh1)
    grads = dict(
        norm_mix_pre=dg0, w_in_pT=d_w_in_pt, conv_ssm_w=jnp.concatenate([dcwx, dcwb], axis=1),
        conv_ssm_b=jnp.concatenate([dcbx, dcbb], axis=1), dt_bias=ddtb[:, :NH], a_log=dalog[:, :NH],
        d_skip=f(ddsk).reshape(NH, 64).sum(axis=1)[None, :], ssm_norm=dnrm, conv_lru_w=dcwl, conv_lru_b=dcbl,
        lru_wa=_blockdiag4_extract(dwa), lru_ba=dba, lru_wx=_blockdiag4_extract(dwx), lru_bx=dbx, lru_lambda=dlam,
        w_out=d_w_out, norm_mix_post=dg1, norm_mlp_pre=dg2, w_upT=d_w_up_t, w_down=d_w_down, norm_mlp_post=dg3)
    return loss[0, 0], grad_x, grads


W_IN_COLS = 6672


def _w_in_t_to_padded(wt):
    z, xs, bc, dt = wt[0:1024], wt[1024:2048], wt[2048:2560], wt[2560:2576]
    gl, xl, ga, gb = wt[2576:3600], wt[3600:4624], wt[4624:5648], wt[5648:6672]
    return jnp.concatenate([z, gl, xl, ga, gb, xs, bc, dt, jnp.zeros((NP - 6672, wt.shape[1]), wt.dtype)], axis=0)


def _w_in_t_from_padded(wp):
    z, gl, xl, ga, gb = (wp[SEG * s:SEG * (s + 1)] for s in range(5))
    xs, bc, dt = wp[5120:6144], wp[6144:6656], wp[6656:6672]
    return jnp.concatenate([z, xs, bc, dt, gl, xl, ga, gb], axis=0)


def _prep_params(full, big):
    f = lambda a: a.astype(F32)
    pad128 = lambda a: jnp.pad(f(a).reshape(1, -1), ((0, 0), (0, 128 - a.size)))
    cw = f(full["conv_ssm_w"])
    cb = f(full["conv_ssm_b"]).reshape(1, -1)
    return dict(
        big, norm_mix_pre=f(full["norm_mix_pre"]).reshape(1, D),
        cw_xs=cw[:, :D], cw_bc=cw[:, D:], cb_xs=cb[:, :D], cb_bc=cb[:, D:],
        dt_bias=pad128(full["dt_bias"]), a_log=pad128(full["a_log"]),
        d_skip_x=jnp.repeat(f(full["d_skip"]).reshape(-1), 64).reshape(1, D), ssm_norm=f(full["ssm_norm"]).reshape(1, D),
        conv_lru_w=f(full["conv_lru_w"]), conv_lru_b=f(full["conv_lru_b"]).reshape(1, D),
        wa_bd=_blockdiag4(full["lru_wa"]), wx_bd=_blockdiag4(full["lru_wx"]),
        lru_ba=f(full["lru_ba"]).reshape(1, D), lru_bx=f(full["lru_bx"]).reshape(1, D),
        lru_lambda=f(full["lru_lambda"]).reshape(1, D),
        norm_mix_post=f(full["norm_mix_post"]).reshape(1, D), norm_mlp_pre=f(full["norm_mlp_pre"]).reshape(1, D),
        norm_mlp_post=f(full["norm_mlp_post"]).reshape(1, D))


MESH_ID = pl.DeviceIdType.MESH
ANY = pl.BlockSpec(memory_space=pl.ANY)


def _my_place():
    x, y, c = lax.axis_index("x"), lax.axis_index("y"), lax.axis_index("c")
    return x, y, c, 4 * x + 2 * y + c


def _peer(x, y, c, k):
    return (x ^ ((k >> 2) & 1), y ^ ((k >> 1) & 1), c ^ (k & 1))


def _all_gather(pack, name):
    r_n = pack.shape[0]
    half = -(-r_n // 32) * 16
    n_cp = 9

    def body(in_ref, out_ref, send_sems, recv_sems, local_sem):
        x, y, c, me = _my_place()
        here, sibling, x_nbr, y_nbr = (x, y, c), (x, y, 1 - c), (1 - x, y, c), (x, 1 - y, c)
        part = {"all": pl.ds(0, r_n), "lo": pl.ds(0, half), "hi": pl.ds(half, r_n - half)}

        def copy(j, block, rows, to, src=None):
            return pltpu.make_async_remote_copy(
                src_ref=out_ref.at[block, part[rows]] if src is None else src, dst_ref=out_ref.at[block, part[rows]],
                send_sem=send_sems.at[j], recv_sem=recv_sems.at[j], device_id=to, device_id_type=MESH_ID)

        mine = pltpu.make_async_copy(in_ref, out_ref.at[me], local_sem)
        mine.start()
        first = [copy(0, me, "all", sibling, src=in_ref), copy(1, me, "all", x_nbr, src=in_ref),
                 copy(2, me, "all", y_nbr, src=in_ref)]
        for cp in first:
            cp.start()
        relay = [[(3, me ^ 4, "lo", y_nbr), (5, me ^ 4, "all", sibling)],
                 [(4, me ^ 2, "hi", x_nbr), (6, me ^ 2, "all", sibling)],
                 [(7, me ^ 6, "lo", sibling)],
                 [(8, me ^ 6, "hi", sibling)]]
        landed = [(1, me ^ 4, "all"), (2, me ^ 2, "all"), (3, me ^ 6, "lo"), (4, me ^ 6, "hi")]
        passed = []
        for (j, block, rows), nxt in zip(landed, relay):
            copy(j, block, rows, here).wait_recv()
            for args in nxt:
                cp = copy(*args)
                cp.start()
                passed.append(cp)
        for j, block, rows in ((0, me ^ 1, "all"), (5, me ^ 5, "all"), (6, me ^ 3, "all"), (7, me ^ 7, "lo"),
                               (8, me ^ 7, "hi")):
            copy(j, block, rows, here).wait_recv()
        for cp in first + passed:
            cp.wait_send()
        mine.wait()

    return pl.pallas_call(
        body, name=name, in_specs=[ANY], out_specs=ANY,
        out_shape=jax.ShapeDtypeStruct((N_DEV,) + pack.shape, pack.dtype),
        scratch_shapes=[pltpu.SemaphoreType.DMA((n_cp,)), pltpu.SemaphoreType.DMA((n_cp,)), pltpu.SemaphoreType.DMA],
    )(pack)


HBM = pl.BlockSpec(memory_space=pltpu.HBM)
SEM = pl.BlockSpec(memory_space=pltpu.SEMAPHORE)
PLAN_GATHER_ICI = tuple((k, "pack", 0) for k in (2, 4, 6, 0))
PLAN_GATHER_D2D = ((1, "pack", 0),) + tuple((1, s, s) for s in (2, 4, 6))
PLAN_SCATTER_CHIPS = tuple((k, "chip", "chip") for k in (2, 4, 6, 0))


def _plan_copy(j, plan, src_ref, land_ref, sems):
    k, source, slot = plan[j]
    x, y, c, me = _my_place()
    if source == "pack":
        src = src_ref
    elif source == "chip":
        src = src_ref.at[(me ^ k) >> 1]
    else:
        src = land_ref.at[me ^ source]
    dst = land_ref.at[me >> 1] if slot == "chip" else land_ref.at[me ^ slot]
    if k == 0:
        return pltpu.make_async_copy(src, dst, sems[j])
    return pltpu.make_async_remote_copy(
        src_ref=src, dst_ref=dst, send_sem=sems[j], recv_sem=sems[len(plan) + j],
        device_id=_peer(x, y, c, k), device_id_type=MESH_ID)


def _exchange_start(src, land, plan, name):
    n_c = len(plan)
    if land is None:
        slots = src.shape[0] if src.ndim == 3 else N_DEV
        land = pltpu.with_memory_space_constraint(lax.empty((slots,) + src.shape[-2:], src.dtype), pltpu.HBM)

    def body(src_ref, land_ref, *rest):
        sems, token = rest[:2 * n_c], rest[2 * n_c + 2]
        for j in range(n_c):
            _plan_copy(j, plan, src_ref, land_ref, sems).start()
        token[...] = jnp.zeros_like(token)

    outs = pl.pallas_call(
        body, name=name,
        out_shape=(pltpu.SemaphoreType.DMA(()),) * (2 * n_c) + (
            pltpu.HBM(src.shape, src.dtype), pltpu.HBM(land.shape, land.dtype), jax.ShapeDtypeStruct((8, 128), F32)),
        in_specs=(HBM, HBM), out_specs=(SEM,) * (2 * n_c) + (HBM, HBM, pl.BlockSpec(memory_space=pltpu.VMEM)),
        input_output_aliases={0: 2 * n_c, 1: 2 * n_c + 1},
        compiler_params=pltpu.CompilerParams(has_side_effects=pltpu.SideEffectType.DATAFLOW_SIDE_EFFECTING),
    )(pltpu.with_memory_space_constraint(src, pltpu.HBM), land)
    return outs[:2 * n_c], outs[2 * n_c], outs[2 * n_c + 1], outs[2 * n_c + 2]


def _exchange_wait(sems, src_thru, land_thru, after, plan, name):
    n_c = len(plan)

    def body(src_ref, land_ref, *rest):
        for j in range(n_c):
            cp = _plan_copy(j, plan, src_ref, land_ref, rest[:2 * n_c])
            if plan[j][0] == 0:
                cp.wait()
            else:
                cp.wait_send()
                cp.wait_recv()

    return pl.pallas_call(
        body, name=name,
        out_shape=(pltpu.HBM(src_thru.shape, src_thru.dtype), pltpu.HBM(land_thru.shape, land_thru.dtype)),
        in_specs=(HBM, HBM) + (SEM,) * (2 * n_c) + (pl.BlockSpec(memory_space=pl.ANY),), out_specs=(HBM, HBM),
        input_output_aliases={0: 0, 1: 1},
        compiler_params=pltpu.CompilerParams(has_side_effects=pltpu.SideEffectType.DATAFLOW_SIDE_EFFECTING),
    )(src_thru, land_thru, *sems, after)


def _parts_copies(src_refs, land_ref, sems):
    x, y, c, me = _my_place()
    n_p = len(src_refs)
    n_c = (N_DEV - 1) * n_p
    remote, local = [], []
    row0 = 0
    for p, src_ref in enumerate(src_refs):
        rows = pl.ds(row0, src_ref.shape[1])
        row0 += src_ref.shape[1]
        for k in range(1, N_DEV):
            j = (k - 1) * n_p + p
            remote.append(pltpu.make_async_remote_copy(
                src_ref=src_ref.at[me ^ k], dst_ref=land_ref.at[me, rows], send_sem=sems[j], recv_sem=sems[n_c + j],
                device_id=_peer(x, y, c, k), device_id_type=MESH_ID))
        local.append(pltpu.make_async_copy(src_ref.at[me], land_ref.at[me, rows], sems[2 * n_c + p]))
    return remote, local


def _scatter_parts_start(srcs, name):
    n_p = len(srcs)
    n_s = (2 * (N_DEV - 1) + 1) * n_p
    hbm = lambda a: pltpu.with_memory_space_constraint(a, pltpu.HBM)
    land = hbm(lax.empty((N_DEV, sum(s.shape[1] for s in srcs), srcs[0].shape[2]), srcs[0].dtype))

    def body(*refs):
        sems, token = refs[n_p + 1:n_p + 1 + n_s], refs[-1]
        remote, local = _parts_copies(refs[:n_p], refs[n_p], sems)
        for cp in remote + local:
            cp.start()
        token[...] = jnp.zeros_like(token)

    thru = [pltpu.HBM(a.shape, a.dtype) for a in list(srcs) + [land]]
    outs = pl.pallas_call(
        body, name=name,
        out_shape=(pltpu.SemaphoreType.DMA(()),) * n_s + tuple(thru) + (jax.ShapeDtypeStruct((8, 128), F32),),
        in_specs=(HBM,) * (n_p + 1),
        out_specs=(SEM,) * n_s + (HBM,) * (n_p + 1) + (pl.BlockSpec(memory_space=pltpu.VMEM),),
        input_output_aliases={i: n_s + i for i in range(n_p + 1)},
        compiler_params=pltpu.CompilerParams(has_side_effects=pltpu.SideEffectType.DATAFLOW_SIDE_EFFECTING),
    )(*[hbm(s) for s in srcs], land)
    return outs[:n_s], outs[n_s:n_s + n_p], outs[n_s + n_p], outs[n_s + n_p + 1]


def _scatter_parts_wait(sems, srcs_thru, land_thru, after, name):
    n_p = len(srcs_thru)

    def body(*refs):
        remote, local = _parts_copies(refs[:n_p], refs[n_p], refs[n_p + 1:n_p + 1 + len(sems)])
        for cp in remote:
            cp.wait_send()
            cp.wait_recv()
        for cp in local:
            cp.wait()

    thru = [pltpu.HBM(a.shape, a.dtype) for a in list(srcs_thru) + [land_thru]]
    outs = pl.pallas_call(
        body, name=name, out_shape=tuple(thru),
        in_specs=(HBM,) * (n_p + 1) + (SEM,) * len(sems) + (pl.BlockSpec(memory_space=pl.ANY),),
        out_specs=(HBM,) * (n_p + 1), input_output_aliases={i: i for i in range(n_p + 1)},
        compiler_params=pltpu.CompilerParams(has_side_effects=pltpu.SideEffectType.DATAFLOW_SIDE_EFFECTING),
    )(*srcs_thru, land_thru, *sems, after)
    return outs[:n_p], outs[n_p]


def _pair_exchange(src, name):
    n_s = N_DEV // 2

    def body(src_ref, land_ref, send_sems, recv_sems):
        x, y, c, _ = _my_place()
        cps = [pltpu.make_async_remote_copy(
            src_ref=src_ref.at[2 * i + 1 - c], dst_ref=land_ref.at[i], send_sem=send_sems.at[i],
            recv_sem=recv_sems.at[i], device_id=(x, y, 1 - c), device_id_type=MESH_ID) for i in range(n_s)]
        for cp in cps:
            cp.start()
        for cp in cps:
            cp.wait_recv()
        for cp in cps:
            cp.wait_send()

    return pl.pallas_call(
        body, name=name, in_specs=[ANY], out_specs=ANY,
        out_shape=jax.ShapeDtypeStruct((n_s,) + src.shape[1:], src.dtype),
        scratch_shapes=[pltpu.SemaphoreType.DMA((n_s,)), pltpu.SemaphoreType.DMA((n_s,))],
    )(src)


def _pair_sum(src, land, name):
    n_s, r_n, c_n = land.shape
    tr = max(t for t in range(16, 513, 16) if r_n % t == 0)

    def body(c_ref, a_ref, b_ref, o_ref):
        o_ref[...] = (a_ref[...].astype(F32) + b_ref[...].astype(F32)).astype(o_ref.dtype)

    blk = lambda: pl.BlockSpec((1, tr, c_n), lambda i, j, c_ref: (i, j, 0))
    return pl.pallas_call(
        body, name=name,
        grid_spec=pltpu.PrefetchScalarGridSpec(
            num_scalar_prefetch=1, grid=(n_s, r_n // tr),
            in_specs=[pl.BlockSpec((1, tr, c_n), lambda i, j, c_ref: (2 * i + c_ref[0], j, 0)), blk()],
            out_specs=blk()),
        out_shape=jax.ShapeDtypeStruct(land.shape, land.dtype),
        compiler_params=_cp("parallel", "parallel"),
    )(lax.axis_index("c").astype(jnp.int32).reshape(1), src, land)


def _slot_sum(parts, name):
    n_s, r_n, c_n = parts.shape
    tr = max(t for t in range(16, 513, 16) if r_n % t == 0)

    def body(p_ref, o_ref):
        acc = p_ref[0].astype(F32)
        for k in range(1, n_s):
            acc = acc + p_ref[k].astype(F32)
        o_ref[...] = acc

    return pl.pallas_call(
        body, name=name, grid=(r_n // tr,),
        in_specs=[pl.BlockSpec((n_s, tr, c_n), lambda i: (0, i, 0))],
        out_specs=pl.BlockSpec((tr, c_n), lambda i: (i, 0)),
        out_shape=jax.ShapeDtypeStruct((r_n, c_n), F32),
        compiler_params=_cp("parallel"),
    )(parts)


def _adam_math(w, g, m, v):
    m = ADAM_B1 * m + (1.0 - ADAM_B1) * g
    v = ADAM_B2 * v + (1.0 - ADAM_B2) * jnp.square(g)
    m_hat = m / (1.0 - ADAM_B1 ** ADAM_STEP)
    v_hat = v / (1.0 - ADAM_B2 ** ADAM_STEP)
    return -ADAM_LR * (m_hat / (jnp.sqrt(v_hat) + ADAM_EPS) + ADAM_WD * w), m, v


def _adam_big(w, g, m, v, name):
    def body(w_ref, g_ref, m_ref, v_ref, d_ref, mo_ref, vo_ref):
        d_ref[...], mo_ref[...], vo_ref[...] = _adam_math(w_ref[...], g_ref[...], m_ref[...], v_ref[...])

    if w.ndim == 3:
        _, r_n, c_n = w.shape
        tr = min(r_n, 256)
        grid = (r_n // tr,)
        blk = lambda: pl.BlockSpec((1, tr, c_n), lambda i: (0, i, 0))
    else:
        r_n, c_n = w.shape
        tc = min(c_n, 256)
        grid = (c_n // tc,)
        blk = lambda: pl.BlockSpec((r_n, tc), lambda i: (0, i))
    return pl.pallas_call(
        body, name=name, grid=grid, in_specs=[blk(), blk(), blk(), blk()], out_specs=[blk(), blk(), blk()],
        out_shape=[jax.ShapeDtypeStruct(w.shape, F32)] * 3, compiler_params=_cp("parallel"),
    )(w, g, m, v)


def _adam_small(groups, where, wmv, total):
    n, n_g = len(wmv), len(groups)
    t_g, t_r = total

    def body(*refs):
        g_refs = refs[:n_g]
        w_refs = refs[n_g:n_g + 3 * n]
        o_refs = refs[n_g + 3 * n:]
        for q in range(n):
            w_ref, m_ref, v_ref = w_refs[3 * q:3 * q + 3]
            r, c = w_ref.shape
            gi, r0 = where[q]
            g = g_refs[gi][r0:r0 + r, 0:c]
            d, m, v = _adam_math(w_ref[...], g, m_ref[...], v_ref[...])
            o_refs[4 * q][...] = g
            o_refs[4 * q + 1][...] = d
            o_refs[4 * q + 2][...] = m
            o_refs[4 * q + 3][...] = v
        o_refs[4 * n][...] = g_refs[t_g][t_r:t_r + 1, :]

    flat_wmv = [a for t in wmv for a in t]
    vm = pl.BlockSpec(memory_space=pltpu.VMEM)
    outs = pl.pallas_call(
        body, name="adam_small", in_specs=[vm] * (n_g + 3 * n), out_specs=[vm] * (4 * n + 1),
        out_shape=[jax.ShapeDtypeStruct(t[0].shape, F32) for t in wmv for _ in range(4)]
        + [jax.ShapeDtypeStruct((1, groups[t_g].shape[1]), F32)],
        compiler_params=pltpu.CompilerParams(vmem_limit_bytes=VMEM_LIMIT),
    )(*groups, *flat_wmv)
    return [tuple(outs[4 * q:4 * q + 4]) for q in range(n)], outs[4 * n]


WEIGHTS = ["norm_mix_pre", "w_in", "conv_ssm_w", "conv_ssm_b", "dt_bias", "a_log", "d_skip", "ssm_norm", "conv_lru_w",
           "conv_lru_b", "lru_wa", "lru_ba", "lru_wx", "lru_bx", "lru_lambda", "w_out", "norm_mix_post", "norm_mlp_pre",
           "w_up", "w_down", "norm_mlp_post"]
BIG = ["w_out", "w_up", "w_down", "w_in"]
IN_ROWS = W_IN_COLS // N_DEV
IN_PAD, EARLY_ROWS = 848, 880
LATE_ROWS = 1152
GRAD_LATE_ROWS = 864
CONV_SSM_COLS, CONV_LRU_COLS = 1536 // N_DEV, D // N_DEV
SMALL = [("norm_mix_pre", (1, D), 0, 0), ("ssm_norm", (1, D), 0, 1), ("conv_lru_b", (1, D), 0, 2),
         ("lru_lambda", (1, D), 0, 3), ("norm_mix_post", (1, D), 0, 4), ("norm_mlp_pre", (1, D), 0, 5),
         ("norm_mlp_post", (1, D), 0, 6), ("conv_ssm_b", (1, 1536), 1, 0), ("dt_bias", (1, NH), 2, 0),
         ("a_log", (1, NH), 2, 1), ("d_skip", (1, NH), 2, 2), ("conv_ssm_w", (4, CONV_SSM_COLS), 3, 0),
         ("conv_lru_w", (4, CONV_LRU_COLS), 4, 0), ("lru_wa", (D, 64), 5, 0), ("lru_wx", (D, 64), 5, D),
         ("lru_ba", (NH, 64), 6, 0), ("lru_bx", (NH, 64), 6, NH)]
SMALL_GROUPS = [(8, D), (1, 1536), (8, 128), (4, 1536), (4, D), (2 * D, 64), (2 * NH, 64)]


def _pad_rows(flat, mult):
    n = flat.shape[0]
    rows = -(-n // (128 * mult)) * mult
    return jnp.pad(flat, (0, rows * 128 - n)).reshape(rows, 128)


def _split3(a):
    hi = a.astype(MXU)
    r1 = a - hi.astype(F32)
    mid = r1.astype(MXU)
    lo = (r1 - mid.astype(F32)).astype(MXU)
    return jnp.stack([hi, mid, lo])


def _early_pack(a, me):
    bf = lambda t: t.astype(MXU)
    conv = lambda t, c: jnp.pad(_split3(t).reshape(12, c), ((0, 4), (0, D - c)))
    shifted = lax.dynamic_update_slice(jnp.zeros((IN_PAD, D), MXU), bf(a["w_in"][0]).T, (2 * me, 0))
    return jnp.concatenate([shifted, conv(a["conv_ssm_w"][0], CONV_SSM_COLS), conv(a["conv_lru_w"][0], CONV_LRU_COLS)],
                           axis=0)


TILE = 16
SHARD_TILES = IN_PAD // TILE
SHARD_STEP = (IN_ROWS // TILE)
SEG_TILES = ((0, 64, 0), (64, 128, 320), (128, 160, 384), (160, 161, 416), (161, 225, 64), (225, 289, 128),
             (289, 353, 192), (353, 417, 256))


def _tile_runs(lo, hi):
    runs = []
    for s0, s1, d0 in SEG_TILES:
        a, b = max(lo, s0), min(hi, s1)
        if a < b:
            runs.append((a, b - a, d0 + a - s0))
    return runs


def _assemble_w_in(g):
    whole = []
    for k in range(N_DEV):
        lo = SHARD_STEP * k + (1 if k else 0)
        hi = SHARD_STEP * (k + 1) + (1 if k == N_DEV - 1 else 0)
        whole += [(k, a - SHARD_STEP * k, n, d) for a, n, d in _tile_runs(lo, hi)]
    split = [(k, _tile_runs(SHARD_STEP * k, SHARD_STEP * k + 1)[0][2]) for k in range(1, N_DEV)]

    def body(g_ref, o_ref):
        rows = lambda t, n=1: pl.ds(TILE * t, TILE * n)
        for k, t, n, d in whole:
            o_ref[rows(d, n), :] = g_ref[k, rows(t, n), :]
        for k, d in split:
            o_ref[rows(d), :] = g_ref[k - 1, rows(SHARD_STEP), :] + g_ref[k, rows(0), :]
        o_ref[pl.ds(W_IN_COLS, NP - W_IN_COLS), :] = jnp.zeros((NP - W_IN_COLS, D), o_ref.dtype)

    vm = pl.BlockSpec(memory_space=pltpu.VMEM)
    return pl.pallas_call(
        body, name="assemble_w_in", in_specs=[vm], out_specs=vm, out_shape=jax.ShapeDtypeStruct((NP, D), g.dtype),
        compiler_params=pltpu.CompilerParams(vmem_limit_bytes=VMEM_LIMIT),
    )(g)


def _scatter_w_in_grad(dw):
    runs = [(k, a - SHARD_STEP * k, n, d) for k in range(N_DEV)
            for a, n, d in _tile_runs(SHARD_STEP * k, SHARD_STEP * k + SHARD_TILES)]
    pad = GRAD_LATE_ROWS - IN_PAD

    def body(dw_ref, o_ref):
        rows = lambda t, n: pl.ds(TILE * t, TILE * n)
        for k, t, n, d in runs:
            o_ref[k, rows(t, n), :] = dw_ref[rows(d, n), :]
        for k in range(N_DEV):
            o_ref[k, pl.ds(IN_PAD, pad), :] = jnp.zeros((pad, D), o_ref.dtype)

    vm = pl.BlockSpec(memory_space=pltpu.VMEM)
    return pl.pallas_call(
        body, name="scatter_w_in_grad", in_specs=[vm], out_specs=vm,
        out_shape=jax.ShapeDtypeStruct((N_DEV, GRAD_LATE_ROWS, D), dw.dtype),
        compiler_params=pltpu.CompilerParams(vmem_limit_bytes=VMEM_LIMIT),
    )(dw)


def _early_unpack(g):
    w_in_pt = _assemble_w_in(g)
    conv = {}
    for n, r0, c in (("conv_ssm_w", IN_PAD, CONV_SSM_COLS), ("conv_lru_w", IN_PAD + 16, CONV_LRU_COLS)):
        s = g[:, r0:r0 + 12, :c].astype(F32).reshape(N_DEV, 3, 4, c)
        conv[n] = ((s[:, 0] + s[:, 1]) + s[:, 2]).transpose(1, 0, 2).reshape(4, N_DEV * c)
    return w_in_pt, conv


def _late_pack(a):
    bf = lambda t: t.astype(MXU)
    return jnp.concatenate([bf(a["w_up"][0]).T, bf(a["w_down"][0]), bf(a["w_out"][0])], axis=0)


def _late_unpack(g):
    return dict(w_upT=(g, 0), w_down=(g, 1), w_out=(g, 2 * FF // D))


def kernel(x, norm_mix_pre, w_in, conv_ssm_w, conv_ssm_b, dt_bias, a_log, d_skip, ssm_norm, conv_lru_w, conv_lru_b, lru_wa, lru_ba, lru_wx, lru_bx, lru_lambda, w_out, norm_mix_post, norm_mlp_pre, w_up, w_down, norm_mlp_post, loss_target, m_norm_mix_pre, m_w_in, m_conv_ssm_w, m_conv_ssm_b, m_dt_bias, m_a_log, m_d_skip, m_ssm_norm, m_conv_lru_w, m_conv_lru_b, m_lru_wa, m_lru_ba, m_lru_wx, m_lru_bx, m_lru_lambda, m_w_out, m_norm_mix_post, m_norm_mlp_pre, m_w_up, m_w_down, m_norm_mlp_post, v_norm_mix_pre, v_w_in, v_conv_ssm_w, v_conv_ssm_b, v_dt_bias, v_a_log, v_d_skip, v_ssm_norm, v_conv_lru_w, v_conv_lru_b, v_lru_wa, v_lru_ba, v_lru_wx, v_lru_bx, v_lru_lambda, v_w_out, v_norm_mix_post, v_norm_mlp_pre, v_w_up, v_w_down, v_norm_mlp_post):
    vals = (norm_mix_pre, w_in, conv_ssm_w, conv_ssm_b, dt_bias, a_log, d_skip, ssm_norm, conv_lru_w, conv_lru_b, lru_wa, lru_ba, lru_wx, lru_bx, lru_lambda, w_out, norm_mix_post, norm_mlp_pre, w_up, w_down, norm_mlp_post)
    m_vals = (m_norm_mix_pre, m_w_in, m_conv_ssm_w, m_conv_ssm_b, m_dt_bias, m_a_log, m_d_skip, m_ssm_norm, m_conv_lru_w, m_conv_lru_b, m_lru_wa, m_lru_ba, m_lru_wx, m_lru_bx, m_lru_lambda, m_w_out, m_norm_mix_post, m_norm_mlp_pre, m_w_up, m_w_down, m_norm_mlp_post)
    v_vals = (v_norm_mix_pre, v_w_in, v_conv_ssm_w, v_conv_ssm_b, v_dt_bias, v_a_log, v_d_skip, v_ssm_norm, v_conv_lru_w, v_conv_lru_b, v_lru_wa, v_lru_ba, v_lru_wx, v_lru_bx, v_lru_lambda, v_w_out, v_norm_mix_post, v_norm_mlp_pre, v_w_up, v_w_down, v_norm_mlp_post)
    w = dict(zip(WEIGHTS, vals))
    m = dict(zip(WEIGHTS, m_vals))
    v = dict(zip(WEIGHTS, v_vals))
    me = 4 * lax.axis_index("x") + 2 * lax.axis_index("y") + lax.axis_index("c")

    bf = lambda t: t.astype(MXU)
    late = _late_pack(w)
    early = _all_gather(_early_pack(w, me), "early_weights_all_gather")
    late, early = lax.optimization_barrier((late, early))
    lw = {}
    lw["sems"], lw["src"], lw["land"], token = _exchange_start(late, None, PLAN_GATHER_ICI, "late_weights_ici_start")
    w_in_pt, conv_w = _early_unpack(early)
    full = {n: (conv_w[n] if n in conv_w else w[n][0]) for n in WEIGHTS if n not in BIG}
    full["norm_mix_pre"] = full["norm_mix_pre"] + token[0, 0]

    def after_ssd(after):
        src, land = _exchange_wait(lw["sems"], lw["src"], lw["land"], after, PLAN_GATHER_ICI, "late_weights_ici_wait")
        lw["sems"], lw["src"], lw["land"], tok = _exchange_start(src, land, PLAN_GATHER_D2D, "late_weights_d2d_start")
        return tok[0, 0]

    def late_weights(after):
        src, land = _exchange_wait(lw["sems"], lw["src"], lw["land"], after, PLAN_GATHER_D2D, "late_weights_d2d_wait")
        return _late_unpack(land)

    sent = {}

    def send_mlp_grads(d_w_up_t, d_w_down, d_w_out):
        srcs = [bf(d).reshape(N_DEV, -1, D) for d in (d_w_up_t, d_w_down, d_w_out)]
        sent["sems"], sent["srcs"], sent["land"], tok = _scatter_parts_start(srcs, "mlp_grads_start")
        return tok[0, 0]

    def send_late_grads(d_w_in_pt):
        src = _scatter_w_in_grad(d_w_in_pt)
        chip = _pair_sum(src, _pair_exchange(src, "late_grads_pair_exchange"), "late_grads_pair_sum")
        sent["sems2"], sent["src2"], sent["land2"], tok = _exchange_start(chip, None, PLAN_SCATTER_CHIPS,
                                                                          "late_grads_start")
        return tok[0, 0]

    loss, grad_x, g = _local_step(x[0], loss_target[0], _prep_params(full, dict(w_in_pT=w_in_pt)), after_ssd,
                                  late_weights, send_mlp_grads, send_late_grads)

    out_g, out_d, out_m, out_v = {}, {}, {}, {}
    _, mlp_land = _scatter_parts_wait(sent["sems"], sent["srcs"], sent["land"], grad_x, "mlp_grads_wait")
    g_mlp = _slot_sum(mlp_land, "slot_sum_mlp")
    fs = FF // N_DEV
    for n, gn in (("w_up", g_mlp[:fs].T[None]), ("w_down", g_mlp[fs:2 * fs][None]), ("w_out", g_mlp[2 * fs:][None])):
        out_g[n] = gn
        out_d[n], out_m[n], out_v[n] = _adam_big(w[n], gn, m[n], v[n], "adam_" + n)
    zrow = jnp.zeros((1, D), F32)
    pad16 = lambda a: jnp.pad(a, ((0, 0), (0, 128 - NH)))
    small_parts = [
        jnp.concatenate([g["norm_mix_pre"], g["ssm_norm"], g["conv_lru_b"], g["lru_lambda"], g["norm_mix_post"],
                         g["norm_mlp_pre"], g["norm_mlp_post"], zrow], axis=0),
        g["conv_ssm_b"],
        jnp.concatenate([pad16(g["dt_bias"]), pad16(g["a_log"]), pad16(g["d_skip"]), jnp.full((1, 128), loss, F32),
                         jnp.zeros((4, 128), F32)], axis=0),
        g["conv_ssm_w"], g["conv_lru_w"],
        jnp.concatenate([g["lru_wa"].reshape(D, 64), g["lru_wx"].reshape(D, 64)], axis=0),
        jnp.concatenate([g["lru_ba"].reshape(NH, 64), g["lru_bx"].reshape(NH, 64)], axis=0)]
    small = _pad_rows(jnp.concatenate([s.reshape(-1) for s in small_parts]), 64)
    small, _ = lax.optimization_barrier((small, out_v["w_out"]))
    small_all = _all_gather(small, "small_grads_all_gather")

    _, lg_land = _exchange_wait(sent["sems2"], sent["src2"], sent["land2"], small_all, PLAN_SCATTER_CHIPS,
                                     "late_grads_wait")
    g_late = _slot_sum(lg_land, "slot_sum_late")
    gt = lax.dynamic_slice(g_late, (2 * me, 0), (IN_ROWS, D))
    dt_, mt_, vt_ = _adam_big(w["w_in"][0].T, gt, m["w_in"][0].T, v["w_in"][0].T, "adam_w_in")
    out_g["w_in"], out_d["w_in"], out_m["w_in"], out_v["w_in"] = gt.T[None], dt_.T[None], mt_.T[None], vt_.T[None]
    sflat = _slot_sum(small_all, "slot_sum_small").reshape(-1)
    groups = []
    off = 0
    for r, c in SMALL_GROUPS:
        groups.append(sflat[off:off + r * c].reshape(r, c))
        off += r * c
    groups[3] = lax.dynamic_slice_in_dim(groups[3], me * CONV_SSM_COLS, CONV_SSM_COLS, axis=1)
    groups[4] = lax.dynamic_slice_in_dim(groups[4], me * CONV_LRU_COLS, CONV_LRU_COLS, axis=1)
    wmv = [(w[n].reshape(s), m[n].reshape(s), v[n].reshape(s)) for n, s, _, _ in SMALL]
    res, loss_row = _adam_small(groups, [(gi, r0) for _, _, gi, r0 in SMALL], wmv, (2, 3))
    loss = loss_row[0, 0]
    for (n, _, _, _), (g_n, d_n, m_n, v_n) in zip(SMALL, res):
        shape = w[n].shape
        out_g[n], out_d[n], out_m[n], out_v[n] = (g_n.reshape(shape), d_n.reshape(shape), m_n.reshape(shape),
                                                  v_n.reshape(shape))
    return (loss, grad_x[None], *[out_g[n] for n in WEIGHTS], *[out_d[n] for n in WEIGHTS],
            *[out_m[n] for n in WEIGHTS], *[out_v[n] for n in WEIGHTS])
```

```python
import functools

import jax
import jax.numpy as jnp
from jax import lax
from jax.experimental import pallas as pl
from jax.experimental.pallas import tpu as pltpu

F32 = jnp.float32
MXU = jnp.bfloat16
HI = lax.Precision.HIGHEST
EPS = 1e-6

D = 1024
NH = 16
NS = 128
CH = 128
FF = 4096
NP = 7168
SEG = 1024
LRU_C = 8.0
N_DEV = 8

ADAM_LR, ADAM_B1, ADAM_B2, ADAM_EPS, ADAM_WD, ADAM_STEP = 0.001, 0.9, 0.999, 1e-08, 0.01, 10

VMEM_LIMIT = 56 * 1024 * 1024


def _cp(*sem):
    return pltpu.CompilerParams(dimension_semantics=sem, vmem_limit_bytes=VMEM_LIMIT)


def _nn(a, b):
    return jnp.dot(a.astype(MXU), b.astype(MXU), preferred_element_type=F32)


def _nt(a, b):
    return lax.dot_general(a.astype(MXU), b.astype(MXU), (((1,), (1,)), ((), ())), preferred_element_type=F32)


def _tn(a, b):
    return lax.dot_general(a.astype(MXU), b.astype(MXU), (((0,), (0,)), ((), ())), preferred_element_type=F32)


_sigmoid = jax.nn.sigmoid


def _silu(x):
    return x * _sigmoid(x)


def _dsilu(x):
    s = _sigmoid(x)
    return s + x * s * (1.0 - s)


def _softplus(x):
    return jnp.maximum(x, 0.0) + jnp.log(1.0 + jnp.exp(-jnp.abs(x)))


def _rms(x, g):
    r = lax.rsqrt(jnp.mean(x * x, axis=-1, keepdims=True) + EPS)
    return x * r * g


def _rms_bwd(x, g, dy):
    r = lax.rsqrt(jnp.mean(x * x, axis=-1, keepdims=True) + EPS)
    gdy = g * dy
    dx = r * gdy - x * (r * r * r) * jnp.mean(x * gdy, axis=-1, keepdims=True)
    return dx, dy * x * r


def _rowsum(x):
    return jnp.sum(x, axis=0, keepdims=True)


def _taps_past(cur, prev8):
    r_n, c_n = cur.shape
    row = lax.broadcasted_iota(jnp.int32, (r_n, c_n), 0)
    out = []
    for k in range(4):
        s = 3 - k
        if s == 0:
            out.append(cur)
            continue
        head = jnp.concatenate([pltpu.roll(prev8, s, 0), jnp.zeros((r_n - 8, c_n), F32)], axis=0)
        out.append(jnp.where(row < s, head, pltpu.roll(cur, s, 0)))
    return out


def _taps_future(cur, fut8):
    r_n, c_n = cur.shape
    row = lax.broadcasted_iota(jnp.int32, (r_n, c_n), 0)
    out = []
    for k in range(4):
        s = 3 - k
        if s == 0:
            out.append(cur)
            continue
        tail = jnp.concatenate([jnp.zeros((r_n - 8, c_n), F32), pltpu.roll(fut8, 8 - s, 0)], axis=0)
        out.append(jnp.where(row >= r_n - s, tail, pltpu.roll(cur, r_n - s, 0)))
    return out


def _conv_apply(taps, w, b):
    acc = taps[0] * w[0:1, :]
    for k in range(1, 4):
        acc = acc + taps[k] * w[k:k + 1, :]
    return acc + b


def _inproj(x, g0, w):
    t_n = x.shape[0]
    tm = min(t_n, 1024)

    n_j = NP // SEG

    def body(x_ref, g_ref, w_ref, pb_ref, p6_ref, u_ref):
        j = pl.program_id(1)

        @pl.when(j == 0)
        def _():
            u_ref[...] = _rms(x_ref[...], g_ref[...]).astype(MXU)

        p = lax.dot_general(u_ref[...], w_ref[...], (((1,), (1,)), ((), ())), preferred_element_type=F32)

        @pl.when(j < n_j - 1)
        def _():
            pb_ref[...] = p.astype(MXU)

        @pl.when(j == n_j - 1)
        def _():
            p6_ref[...] = p

    pb, p6, u = pl.pallas_call(
        body, name="inproj", grid=(t_n // tm, n_j),
        in_specs=[pl.BlockSpec((tm, D), lambda i, j: (i, 0)), pl.BlockSpec((1, D), lambda i, j: (0, 0)),
                  pl.BlockSpec((SEG, D), lambda i, j: (j, 0))],
        out_specs=[pl.BlockSpec((tm, SEG), lambda i, j: (i, jnp.minimum(j, n_j - 2))),
                   pl.BlockSpec((tm, SEG), lambda i, j: (i, 0)), pl.BlockSpec((tm, D), lambda i, j: (i, 0))],
        out_shape=[jax.ShapeDtypeStruct((t_n, NP - SEG), MXU), jax.ShapeDtypeStruct((t_n, SEG), F32),
                   jax.ShapeDtypeStruct((t_n, D), MXU)],
        compiler_params=_cp("parallel", "arbitrary"),
    )(x, g0, w)
    return (pb, p6), u


def _ssd_prep(dtraw, dtb, alog):
    l_n = dtraw.shape[0]
    r = lax.broadcasted_iota(jnp.int32, (l_n, l_n), 0)
    c = lax.broadcasted_iota(jnp.int32, (l_n, l_n), 1)
    tril = (r >= c).astype(F32)
    triu = (r <= c).astype(F32)
    eye = (r == c).astype(F32)
    dt = _softplus(dtraw + dtb)
    adt = dt * (-jnp.exp(alog))
    ac = jnp.dot(tril, adt, preferred_element_type=F32, precision=HI)
    tn = (((0,), (0,)), ((), ()))
    ac_t = lax.dot_general(adt, triu, tn, preferred_element_type=F32, precision=HI)
    dt_t = lax.dot_general(dt, eye, tn, preferred_element_type=F32, precision=HI)
    return dt, dt_t, ac, ac_t, _rowsum(adt)


def _ssd_pair(j, xp, bg, cg, sp, dt, dt_t, ac, ac_t, aend):
    l_n = xp.shape[0]
    lane = lax.broadcasted_iota(jnp.int32, (l_n, 128), 1)
    sub = lax.broadcasted_iota(jnp.int32, (128, l_n), 0)
    lane1 = lax.broadcasted_iota(jnp.int32, (1, 128), 1)
    tri = lax.broadcasted_iota(jnp.int32, (l_n, l_n), 0) >= lax.broadcasted_iota(jnp.int32, (l_n, l_n), 1)
    lo = lax.broadcasted_iota(jnp.int32, (l_n, 128), 1) < 64
    lo_s = lax.broadcasted_iota(jnp.int32, (128, 128), 1) < 64
    cb = _nt(cg, bg)
    cs = _nn(cg, sp)
    x2 = jnp.concatenate([jnp.where(lo, xp, 0.0), jnp.where(lo, 0.0, xp)], axis=0)
    ws_, bs_, eo, ee = [], [], [], []
    for e in range(2):
        h = 2 * j + e
        ac_l = jnp.sum(jnp.where(lane == h, ac, 0.0), axis=1, keepdims=True)
        dt_l = jnp.sum(jnp.where(lane == h, dt, 0.0), axis=1, keepdims=True)
        a_end = jnp.sum(jnp.where(lane1 == h, aend, 0.0), axis=1, keepdims=True)
        ac_s, dt_s = ac_t[h:h + 1, :], dt_t[h:h + 1, :]
        decay = jnp.exp(jnp.where(tri, ac_l - ac_s, -1e30))
        ws_.append(cb * decay * dt_s)
        bs_.append(bg * (jnp.exp(a_end - ac_l) * dt_l))
        eo.append(jnp.exp(ac_l))
        ee.append(jnp.exp(a_end))
    y = _nn(jnp.concatenate(ws_, axis=1), x2) + jnp.where(lo, eo[0], eo[1]) * cs
    s_new = _tn(jnp.concatenate(bs_, axis=0), x2) + jnp.where(lo_s, ee[0], ee[1]) * sp
    return y, s_new


def _ssd_post(y, xs, z, dsk, nrm):
    y = (y + dsk * xs) * _silu(z)
    half = D // 2
    ya, yb = y[:, :half], y[:, half:]
    ya = ya * lax.rsqrt(jnp.mean(ya * ya, axis=-1, keepdims=True) + EPS)
    yb = yb * lax.rsqrt(jnp.mean(yb * yb, axis=-1, keepdims=True) + EPS)
    return jnp.concatenate([ya, yb], axis=1) * nrm


LAST_SEG = NP // SEG - 1


def _proj_ops(proj, seg_ids):
    return [proj[1] if s == LAST_SEG else proj[0] for s in seg_ids]


def _proj_specs(rows, seg_ids, order):
    return [pl.BlockSpec((rows, SEG), functools.partial(lambda i, c: (order(i), c), c=0 if s == LAST_SEG else s))
            for s in seg_ids]


def _prev_specs(rows, seg_ids, order):
    specs = []
    for s in seg_ids:
        n, c = (8, 0) if s == LAST_SEG else (16, s)
        specs.append(pl.BlockSpec((n, SEG), functools.partial(
            lambda i, n, c: (jnp.maximum(order(i) * (rows // n) - 1, 0), c), n=n, c=c)))
    return specs


def _prev8(ref):
    return ref[...] if ref.shape[0] == 8 else ref[8:16, :].astype(F32)


def _full(shape):
    return pl.BlockSpec(shape, lambda i: (0,) * len(shape))


def _slotted(a):
    return a if isinstance(a, tuple) else (a.reshape(N_DEV, a.shape[0] // N_DEV, a.shape[1]), 0)


def _slot_spec(w, rows, tile, index):
    per = rows // N_DEV
    b = w[1]
    return pl.BlockSpec((tile // per, per, D), lambda *ids: (index(*ids), b, 0))


def _ssd_fwd(proj, cwx, cwb, cbx, cbb, dtb, alog, dsk, nrm):
    t_n = proj[0].shape[0]
    n_c = t_n // CH
    fwd = lambda i: i

    def body(z_ref, xs_ref, bc_ref, xsp_ref, bcp_ref, cwx_ref, cwb_ref, cbx_ref, cbb_ref, dtb_ref, alog_ref,
             dsk_ref, nrm_ref, ya_ref, sprev_ref, yraw_ref, xspre_ref, bcpre_ref, s_ref):
        c = pl.program_id(0)

        @pl.when(c == 0)
        def _():
            s_ref[...] = jnp.zeros_like(s_ref)

        keep = jnp.where(c == 0, 0.0, 1.0)
        xs_pre = _conv_apply(_taps_past(xs_ref[...].astype(F32), _prev8(xsp_ref) * keep), cwx_ref[...], cbx_ref[...])
        bc_pre = _conv_apply(_taps_past(bc_ref[:, :512], bcp_ref[:, :512] * keep), cwb_ref[...], cbb_ref[...])
        xspre_ref[...] = xs_pre
        bcpre_ref[...] = bc_pre
        prep = _ssd_prep(bc_ref[:, 512:640], dtb_ref[...], alog_ref[...])
        xs = _silu(xs_pre)
        bc = _silu(bc_pre)
        sprev_ref[0] = s_ref[...]
        ys = []
        for j in range(NH // 2):
            g = j // 4
            yp, sn = _ssd_pair(j, xs[:, 128 * j:128 * j + 128], bc[:, 128 * g:128 * g + 128],
                               bc[:, 256 + 128 * g:384 + 128 * g], s_ref[:, 128 * j:128 * j + 128], *prep)
            ys.append(yp)
            s_ref[:, 128 * j:128 * j + 128] = sn
        y = jnp.concatenate(ys, axis=1)
        yraw_ref[...] = y
        ya_ref[...] = _ssd_post(y, xs, z_ref[...].astype(F32), dsk_ref[...], nrm_ref[...]).astype(ya_ref.dtype)

    return pl.pallas_call(
        body, name="ssd_fwd", grid=(n_c,),
        in_specs=_proj_specs(CH, (0, 5, 6), fwd) + _prev_specs(CH, (5, 6), fwd) + [
            _full((4, D)), _full((4, 512)), _full((1, D)), _full((1, 512)), _full((1, 128)), _full((1, 128)),
            _full((1, D)), _full((1, D))],
        out_specs=[pl.BlockSpec((CH, D), lambda i: (i, 0)), pl.BlockSpec((1, NS, D), lambda i: (i, 0, 0)),
                   pl.BlockSpec((CH, D), lambda i: (i, 0)), pl.BlockSpec((CH, D), lambda i: (i, 0)),
                   pl.BlockSpec((CH, 512), lambda i: (i, 0))],
        out_shape=[jax.ShapeDtypeStruct((t_n, D), MXU), jax.ShapeDtypeStruct((n_c, NS, D), F32),
                   jax.ShapeDtypeStruct((t_n, D), F32), jax.ShapeDtypeStruct((t_n, D), F32),
                   jax.ShapeDtypeStruct((t_n, 512), F32)],
        scratch_shapes=[pltpu.VMEM((NS, D), F32)],
        compiler_params=_cp("arbitrary"),
    )(*_proj_ops(proj, (0, 5, 6, 5, 6)), cwx, cwb, cbx, cbb, dtb, alog, dsk, nrm)


def _ssd_bwd(dya, saved, proj, sprev, cwx, cwb, dtb, alog, dsk, nrm, mlp_ops):
    t_n = proj[0].shape[0]
    n_c = t_n // CH
    rev = lambda i: n_c - 1 - i
    fb = FF // n_c

    def body(dya_ref, yraw_ref, xspre_ref, bcpre_ref, z_ref, xs_ref, bc_ref, sprev_ref, cwx_ref, cwb_ref,
             dtb_ref, alog_ref, dsk_ref, nrm_ref, hid_ref, dff_ref, dhp_ref, v_ref,
             dz_ref, dxs_ref, dbc_ref, dcwx_ref, dcwb_ref, dcbx_ref, dcbb_ref, ddtb_ref, dalog_ref, ddsk_ref,
             dnrm_ref, dwd_ref, dwu_ref, ds_ref, futx_ref, futb_ref):
        i = pl.program_id(0)
        acc_refs = (dcwx_ref, dcwb_ref, dcbx_ref, dcbb_ref, ddtb_ref, dalog_ref, ddsk_ref, dnrm_ref)
        tn = (((0,), (0,)), ((), ()))
        kt = t_n // (NH // 2)
        dwd_acc = jnp.zeros((fb, D), F32)
        dwu_acc = jnp.zeros((fb, D), F32)

        @pl.when(i == 0)
        def _():
            for r in (ds_ref, futx_ref, futb_ref) + acc_refs:
                r[...] = jnp.zeros_like(r)

        xs_pre = xspre_ref[...]
        bc_pre = bcpre_ref[...]
        xs = _silu(xs_pre)
        bc = _silu(bc_pre)
        prep, prep_vjp = jax.vjp(_ssd_prep, bc_ref[:, 512:640], dtb_ref[...], alog_ref[...])
        s_in = sprev_ref[0]

        def pair_args(j):
            g = j // 4
            return (xs[:, 128 * j:128 * j + 128], bc[:, 128 * g:128 * g + 128],
                    bc[:, 256 + 128 * g:384 + 128 * g], s_in[:, 128 * j:128 * j + 128]) + tuple(prep)

        _, post_vjp = jax.vjp(_ssd_post, yraw_ref[...], xs, z_ref[...].astype(F32), dsk_ref[...], nrm_ref[...])
        dy, dxs_skip, dz, ddsk, dnrm = post_vjp(dya_ref[...])
        dz_ref[...] = dz.astype(dz_ref.dtype)
        ddsk_ref[...] += ddsk
        dnrm_ref[...] += dnrm

        dprep = [jnp.zeros_like(p) for p in prep]
        dxp = []
        dbg = [jnp.zeros((CH, 128), F32), jnp.zeros((CH, 128), F32)]
        dcg = [jnp.zeros((CH, 128), F32), jnp.zeros((CH, 128), F32)]
        for j in range(NH // 2):
            g = j // 4
            _, pair_vjp = jax.vjp(functools.partial(_ssd_pair, j), *pair_args(j))
            cts = pair_vjp((dy[:, 128 * j:128 * j + 128], ds_ref[:, 128 * j:128 * j + 128]))
            dxp.append(cts[0])
            dbg[g] = dbg[g] + cts[1]
            dcg[g] = dcg[g] + cts[2]
            ds_ref[:, 128 * j:128 * j + 128] = cts[3]
            dprep = [a + b for a, b in zip(dprep, cts[4:])]
            rows = pl.ds(kt * j, kt)
            dwd_acc = dwd_acc + lax.dot_general(hid_ref[rows, :], dff_ref[rows, :], tn, preferred_element_type=F32)
            dwu_acc = dwu_acc + lax.dot_general(dhp_ref[rows, :], v_ref[rows, :], tn, preferred_element_type=F32)
        dwd_ref[...] = dwd_acc.astype(MXU)
        dwu_ref[...] = dwu_acc.astype(MXU)
        ddtraw, ddtb, dalog = prep_vjp(tuple(dprep))
        ddtb_ref[...] += ddtb
        dalog_ref[...] += dalog

        dxs_pre = (dxs_skip + jnp.concatenate(dxp, axis=1)) * _dsilu(xs_pre)
        dbc_pre = jnp.concatenate([dbg[0], dbg[1], dcg[0], dcg[1]], axis=1) * _dsilu(bc_pre)
        dcbx_ref[...] += _rowsum(dxs_pre)
        dcbb_ref[...] += _rowsum(dbc_pre)
        fx = _taps_future(dxs_pre, futx_ref[...])
        fbc = _taps_future(dbc_pre, futb_ref[...])
        xs_in = xs_ref[...].astype(F32)
        bc_in = bc_ref[:, :512]
        for k in range(4):
            dcwx_ref[k:k + 1, :] += _rowsum(fx[k] * xs_in)
            dcwb_ref[k:k + 1, :] += _rowsum(fbc[k] * bc_in)
        cwx = cwx_ref[...]
        cwb = cwb_ref[...]
        dxs_in = fx[0] * cwx[0:1, :]
        dbc_in = fbc[0] * cwb[0:1, :]
        for k in range(1, 4):
            dxs_in = dxs_in + fx[k] * cwx[k:k + 1, :]
            dbc_in = dbc_in + fbc[k] * cwb[k:k + 1, :]
        futx_ref[...] = dxs_pre[0:8, :]
        futb_ref[...] = dbc_pre[0:8, :]
        dxs_ref[...] = dxs_in.astype(dxs_ref.dtype)
        dbc_ref[...] = jnp.concatenate([dbc_in, ddtraw, jnp.zeros((CH, SEG - 640), F32)], axis=1).astype(dbc_ref.dtype)

    row_out = lambda: pl.BlockSpec((CH, D), lambda i: (rev(i), 0))
    outs = pl.pallas_call(
        body, name="ssd_bwd", grid=(n_c,),
        in_specs=[row_out(), row_out(), row_out(), pl.BlockSpec((CH, 512), lambda i: (rev(i), 0))]
        + _proj_specs(CH, (0, 5, 6), rev) + [pl.BlockSpec((1, NS, D), lambda i: (rev(i), 0, 0)),
                                             _full((4, D)), _full((4, 512)),
                                             _full((1, 128)), _full((1, 128)), _full((1, D)), _full((1, D)),
                                             pl.BlockSpec((t_n, fb), lambda i: (0, i)), _full((t_n, D)),
                                             pl.BlockSpec((t_n, fb), lambda i: (0, i)), _full((t_n, D))],
        out_specs=[row_out(), row_out(), row_out(), _full((4, D)), _full((4, 512)), _full((1, D)), _full((1, 512)),
                   _full((1, 128)), _full((1, 128)), _full((1, D)), _full((1, D)),
                   pl.BlockSpec((fb, D), lambda i: (i, 0)), pl.BlockSpec((fb, D), lambda i: (i, 0))],
        out_shape=[jax.ShapeDtypeStruct((t_n, D), MXU)] * 3 + [
            jax.ShapeDtypeStruct(s, F32) for s in ((4, D), (4, 512), (1, D), (1, 512), (1, 128), (1, 128), (1, D), (1, D))]
        + [jax.ShapeDtypeStruct((FF, D), MXU)] * 2,
        scratch_shapes=[pltpu.VMEM((NS, D), F32), pltpu.VMEM((8, D), F32), pltpu.VMEM((8, 512), F32)],
        compiler_params=_cp("arbitrary"),
    )(dya, *saved, *_proj_ops(proj, (0, 5, 6)), sprev, cwx, cwb, dtb, alog, dsk, nrm, *mlp_ops)
    return outs


LRU_ROWS = 256
LRU_BLK = 256


def _lru_gates_block(xr, wa, wx, ba, bx, lam):
    pr = _nn(xr, wa) + ba
    pi = _nn(xr, wx) + bx
    log_a = -LRU_C * _sigmoid(pr) * _softplus(-lam)
    a = jnp.exp(log_a)
    mult = jnp.sqrt(1.0 - jnp.exp(2.0 * log_a))
    return a, mult * (_sigmoid(pi) * xr)


def _lru_out(h, g):
    return h * jax.nn.gelu(g, approximate=True)


def _lru_fwd(proj, cw, cb, wa, wx, ba, bx, lam):
    t_n = proj[0].shape[0]
    rows = min(LRU_ROWS, t_n)
    fwd = lambda i: i

    def body(g_ref, x_ref, xp_ref, cw_ref, cb_ref, wa_ref, wx_ref, ba_ref, bx_ref, lam_ref, yb_ref, h_ref, xr_ref,
             a_s, u_s, carry):
        i = pl.program_id(0)

        @pl.when(i == 0)
        def _():
            carry[...] = jnp.zeros_like(carry)

        keep = jnp.where(i == 0, 0.0, 1.0)
        prev = _prev8(xp_ref) * keep
        cols = [pl.ds(LRU_BLK * q, LRU_BLK) for q in range(D // LRU_BLK)]
        for q, cs in enumerate(cols):
            xr = _conv_apply(_taps_past(x_ref[:, cs].astype(F32), prev[:, LRU_BLK * q:LRU_BLK * (q + 1)]),
                             cw_ref[:, cs], cb_ref[:, cs])
            xr_ref[:, cs] = xr
            a, u = _lru_gates_block(xr, wa_ref[q], wx_ref[q], ba_ref[:, cs], bx_ref[:, cs], lam_ref[:, cs])
            a_s[:, cs] = a
            u_s[:, cs] = u
        row = lax.broadcasted_iota(jnp.int32, (8, D), 0)

        def blk(b, c):
            s = pl.multiple_of(b * 8, 8)
            av = a_s[pl.ds(s, 8), :]
            uv = u_s[pl.ds(s, 8), :]
            for d in (1, 2, 4):
                m = row >= d
                uv = uv + av * jnp.where(m, pltpu.roll(uv, d, 0), 0.0)
                av = av * jnp.where(m, pltpu.roll(av, d, 0), 1.0)
            hv = uv + av * c
            h_ref[pl.ds(s, 8), :] = hv
            return hv[7:8, :]

        carry[0:1, :] = lax.fori_loop(0, rows // 8, blk, carry[0:1, :])
        for cs in cols:
            yb_ref[:, cs] = _lru_out(h_ref[:, cs], g_ref[:, cs].astype(F32)).astype(yb_ref.dtype)

    return pl.pallas_call(
        body, name="lru_fwd", grid=(t_n // rows,),
        in_specs=_proj_specs(rows, (1, 2), fwd) + _prev_specs(rows, (2,), fwd) + [
            _full((4, D)), _full((1, D)), _full((4, LRU_BLK, LRU_BLK)), _full((4, LRU_BLK, LRU_BLK)),
            _full((1, D)), _full((1, D)), _full((1, D))],
        out_specs=[pl.BlockSpec((rows, D), lambda i: (i, 0))] * 3,
        out_shape=[jax.ShapeDtypeStruct((t_n, D), MXU), jax.ShapeDtypeStruct((t_n, D), F32),
                   jax.ShapeDtypeStruct((t_n, D), F32)],
        scratch_shapes=[pltpu.VMEM((rows, D), F32), pltpu.VMEM((rows, D), F32), pltpu.VMEM((8, D), F32)],
        compiler_params=_cp("arbitrary"),
    )(*_proj_ops(proj, (1, 2, 2)), cw, cb, wa, wx, ba, bx, lam)


def _lru_bwd(dyb, proj, h, xr_saved, cw, wa, wx, ba, bx, lam):
    t_n = proj[0].shape[0]
    rows = min(LRU_ROWS, t_n)
    n_t = t_n // rows
    rev = lambda i: n_t - 1 - i
    rb = rows // 8

    def body(dyb_ref, g_ref, x_ref, h_ref, hp_ref, xr_ref, cw_ref, wa_ref, wx_ref, ba_ref, bx_ref, lam_ref,
             dg_ref, dx_ref, dcw_ref, dcb_ref, dwa_ref, dwx_ref, dba_ref, dbx_ref, dlam_ref,
             a_s, dh_s, hx_s, da_s, du_s, carry, fut):
        i = pl.program_id(0)
        acc_refs = (dcw_ref, dcb_ref, dwa_ref, dwx_ref, dba_ref, dbx_ref, dlam_ref)

        @pl.when(i == 0)
        def _():
            for r in (carry, fut) + acc_refs:
                r[...] = jnp.zeros_like(r)

        keep = jnp.where(i == n_t - 1, 0.0, 1.0)
        cols = [pl.ds(LRU_BLK * q, LRU_BLK) for q in range(D // LRU_BLK)]
        gates_vjps = []
        for q, cs in enumerate(cols):
            (a, _), gates_vjp = jax.vjp(_lru_gates_block, xr_ref[:, cs], wa_ref[q], wx_ref[q], ba_ref[:, cs],
                                        bx_ref[:, cs], lam_ref[:, cs])
            gates_vjps.append(gates_vjp)
            _, out_vjp = jax.vjp(_lru_out, h_ref[:, cs], g_ref[:, cs].astype(F32))
            dh, dg = out_vjp(dyb_ref[:, cs])
            dg_ref[:, cs] = dg.astype(dg_ref.dtype)
            a_s[:, cs] = a
            dh_s[:, cs] = dh
        hx_s[0:8, :] = hp_ref[...] * keep
        hx_s[8:, :] = h_ref[...]
        row = lax.broadcasted_iota(jnp.int32, (8, D), 0)

        def blk(b, c):
            s = pl.multiple_of((rb - 1 - b) * 8, 8)
            av = a_s[pl.ds(s, 8), :]
            dhv = dh_s[pl.ds(s, 8), :]
            kv = av * dhv
            for d in (1, 2, 4):
                m = row <= 7 - d
                kv = kv + av * jnp.where(m, pltpu.roll(kv, 8 - d, 0), 0.0)
                av = av * jnp.where(m, pltpu.roll(av, 8 - d, 0), 1.0)
            kv = kv + av * c
            gv = dhv + jnp.where(row < 7, pltpu.roll(kv, 7, 0), c)
            hb = hx_s[pl.ds(s + 8, 8), :]
            hpv = hx_s[pl.ds(s, 8), :]
            hprev = jnp.where(row >= 1, pltpu.roll(hb, 1, 0), hpv[7:8, :])
            du_s[pl.ds(s, 8), :] = gv
            da_s[pl.ds(s, 8), :] = gv * hprev
            return kv[0:1, :]

        carry[0:1, :] = lax.fori_loop(0, rb, blk, carry[0:1, :])
        for q, cs in enumerate(cols):
            dxr, dwa, dwx, dba, dbx, dlam = gates_vjps[q]((da_s[:, cs], du_s[:, cs]))
            dwa_ref[q] += dwa
            dwx_ref[q] += dwx
            dba_ref[:, cs] += dba
            dbx_ref[:, cs] += dbx
            dlam_ref[:, cs] += dlam
            dcb_ref[:, cs] += _rowsum(dxr)
            ft = _taps_future(dxr, fut[:, cs])
            x_in = x_ref[:, cs].astype(F32)
            for k in range(4):
                dcw_ref[k:k + 1, cs] += _rowsum(ft[k] * x_in)
            cwv = cw_ref[:, cs]
            dx = ft[0] * cwv[0:1, :]
            for k in range(1, 4):
                dx = dx + ft[k] * cwv[k:k + 1, :]
            fut[:, cs] = dxr[0:8, :]
            dx_ref[:, cs] = dx.astype(dx_ref.dtype)

    row_in = lambda: pl.BlockSpec((rows, D), lambda i: (rev(i), 0))
    prev_h = pl.BlockSpec((8, D), lambda i: (jnp.maximum(rev(i) * rb - 1, 0), 0))
    wspec = lambda: _full((4, LRU_BLK, LRU_BLK))
    return pl.pallas_call(
        body, name="lru_bwd", grid=(n_t,),
        in_specs=[row_in()] + _proj_specs(rows, (1, 2), rev) + [row_in(), prev_h, row_in()] + [
            _full((4, D)), wspec(), wspec(), _full((1, D)), _full((1, D)), _full((1, D))],
        out_specs=[row_in(), row_in(), _full((4, D)), _full((1, D)), wspec(), wspec(), _full((1, D)), _full((1, D)),
                   _full((1, D))],
        out_shape=[jax.ShapeDtypeStruct((t_n, D), MXU)] * 2 + [
            jax.ShapeDtypeStruct(s, F32) for s in ((4, D), (1, D), (4, LRU_BLK, LRU_BLK), (4, LRU_BLK, LRU_BLK),
                                                   (1, D), (1, D), (1, D))],
        scratch_shapes=[pltpu.VMEM((rows, D), F32), pltpu.VMEM((rows, D), F32), pltpu.VMEM((rows + 8, D), F32),
                        pltpu.VMEM((rows, D), F32), pltpu.VMEM((rows, D), F32), pltpu.VMEM((8, D), F32),
                        pltpu.VMEM((8, D), F32)],
        compiler_params=_cp("arbitrary"),
    )(dyb, *_proj_ops(proj, (1, 2)), h, h, xr_saved, cw, wa, wx, ba, bx, lam)


def _merge_out(ya, yb, proj, x, wout, g1):
    t_n = x.shape[0]
    tm = min(t_n, 512)

    def body(ya_ref, yb_ref, ga_ref, gb_ref, x_ref, w_ref, g_ref, h1_ref, mix_ref, mg_ref):
        merged = (_sigmoid(ga_ref[...].astype(F32)) * ya_ref[...].astype(F32)
                  + _sigmoid(gb_ref[...].astype(F32)) * yb_ref[...].astype(F32))
        mg = merged.astype(MXU)
        mg_ref[...] = mg
        mix = jnp.dot(mg, w_ref[...].reshape(D, D), preferred_element_type=F32)
        mix_ref[...] = mix
        h1_ref[...] = x_ref[...] + _rms(mix, g_ref[...])

    row = lambda: pl.BlockSpec((tm, D), lambda i: (i, 0))
    return pl.pallas_call(
        body, name="merge_out", grid=(t_n // tm,),
        in_specs=[row(), row()] + _proj_specs(tm, (3, 4), lambda i: i)
        + [row(), _slot_spec(wout, D, D, lambda i: 0), _full((1, D))],
        out_specs=[row(), row(), row()],
        out_shape=[jax.ShapeDtypeStruct((t_n, D), F32), jax.ShapeDtypeStruct((t_n, D), F32),
                   jax.ShapeDtypeStruct((t_n, D), MXU)],
        compiler_params=_cp("parallel"),
    )(ya, yb, *_proj_ops(proj, (3, 4)), x, wout[0], g1)


def _out_bwd(dv, h1, g2, dout, mix, ya, yb, proj, wout, g1, w_in_pt):
    t_n = dv.shape[0]
    tm = min(t_n, 256)

    def body(dv_ref, h1_ref, g2_ref, dout_ref, mix_ref, ya_ref, yb_ref, ga_ref, gb_ref, w_ref, g_ref, wa_ref, wb_ref,
             dh1_ref, dmix_ref, dya_ref, dyb_ref, dga_ref, dgb_ref, dg2_ref, dg1_ref, dug_ref):
        @pl.when(pl.program_id(0) == 0)
        def _():
            dg1_ref[...] = jnp.zeros_like(dg1_ref)
            dg2_ref[...] = jnp.zeros_like(dg2_ref)

        dx, dg_rows = _rms_bwd(h1_ref[...], g2_ref[...], dv_ref[...])
        dg2_ref[...] += _rowsum(dg_rows)
        dh1 = dout_ref[...] + dx
        dh1_ref[...] = dh1
        dmix, dg_rows = _rms_bwd(mix_ref[...], g_ref[...], dh1)
        dg1_ref[...] += _rowsum(dg_rows)
        dmix_b = dmix.astype(MXU)
        dmix_ref[...] = dmix_b
        dmg = lax.dot_general(dmix_b, w_ref[...].reshape(D, D), (((1,), (1,)), ((), ())), preferred_element_type=F32)
        sa = _sigmoid(ga_ref[...].astype(F32))
        sb = _sigmoid(gb_ref[...].astype(F32))
        dya_ref[...] = dmg * sa
        dyb_ref[...] = dmg * sb
        dga = (dmg * ya_ref[...].astype(F32) * sa * (1.0 - sa)).astype(MXU)
        dga_ref[...] = dga
        dug = jnp.dot(dga, wa_ref[...], preferred_element_type=F32)
        dgb = (dmg * yb_ref[...].astype(F32) * sb * (1.0 - sb)).astype(MXU)
        dgb_ref[...] = dgb
        dug_ref[...] = dug + jnp.dot(dgb, wb_ref[...], preferred_element_type=F32)

    row = lambda: pl.BlockSpec((tm, D), lambda i: (i, 0))
    vec = lambda: _full((1, D))
    seg = lambda s: pl.BlockSpec((SEG, D), lambda i: (s, 0))
    return pl.pallas_call(
        body, name="out_bwd", grid=(t_n // tm,),
        in_specs=[row(), row(), vec(), row(), row(), row(), row()] + _proj_specs(tm, (3, 4), lambda i: i)
        + [_slot_spec(wout, D, D, lambda i: 0), vec(), seg(3), seg(4)],
        out_specs=[row(), row(), row(), row(), row(), row(), vec(), vec(), row()],
        out_shape=[jax.ShapeDtypeStruct((t_n, D), F32), jax.ShapeDtypeStruct((t_n, D), MXU),
                   jax.ShapeDtypeStruct((t_n, D), F32), jax.ShapeDtypeStruct((t_n, D), F32),
                   jax.ShapeDtypeStruct((t_n, D), MXU), jax.ShapeDtypeStruct((t_n, D), MXU),
                   jax.ShapeDtypeStruct((1, D), F32), jax.ShapeDtypeStruct((1, D), F32),
                   jax.ShapeDtypeStruct((t_n, D), F32)],
        compiler_params=_cp("arbitrary"),
    )(dv, h1, g2, dout, mix, ya, yb, *_proj_ops(proj, (3, 4)), wout[0], g1, w_in_pt, w_in_pt)


MLP_TM = 1024
MLP_TF_FWD = 1024
MLP_TF_BWD = 1024


def _mlp_fwd(h1, g2, wup, wdown, g3, tgt):
    t_n = h1.shape[0]
    tm = min(t_n, MLP_TM)
    n_f = FF // MLP_TF_FWD

    def body(h1_ref, g2_ref, wu_ref, wd_ref, g3_ref, tgt_ref, hp_ref, v_ref, dout_ref, dff_ref, loss_ref, dg3_ref, acc):
        i, j = pl.program_id(0), pl.program_id(1)

        @pl.when((i == 0) & (j == 0))
        def _():
            loss_ref[...] = jnp.zeros_like(loss_ref)
            dg3_ref[...] = jnp.zeros_like(dg3_ref)

        @pl.when(j == 0)
        def _():
            v_ref[...] = _rms(h1_ref[...], g2_ref[...]).astype(MXU)
            acc[...] = jnp.zeros_like(acc)

        hp = lax.dot_general(v_ref[...], wu_ref[...].reshape(MLP_TF_FWD, D), (((1,), (1,)), ((), ())),
                             preferred_element_type=F32)
        hp_ref[...] = hp.astype(MXU)
        hid = jnp.square(jnp.maximum(hp, 0.0))
        acc[...] += jnp.dot(hid.astype(MXU), wd_ref[...].reshape(MLP_TF_FWD, D), preferred_element_type=F32)

        @pl.when(j == n_f - 1)
        def _():
            ff = acc[...]
            err = h1_ref[...] + _rms(ff, g3_ref[...]) - tgt_ref[...]
            loss_ref[...] += 0.5 * jnp.sum(jnp.mean(err * err, axis=-1, keepdims=True), axis=0, keepdims=True)
            dout = err * (1.0 / D)
            dout_ref[...] = dout
            dff, dg_rows = _rms_bwd(ff, g3_ref[...], dout)
            dg3_ref[...] += _rowsum(dg_rows)
            dff_ref[...] = dff.astype(MXU)

    row = lambda: pl.BlockSpec((tm, D), lambda i, j: (i, 0))
    vec = lambda: pl.BlockSpec((1, D), lambda i, j: (0, 0))
    return pl.pallas_call(
        body, name="mlp_fwd", grid=(t_n // tm, n_f),
        in_specs=[row(), vec(), _slot_spec(wup, FF, MLP_TF_FWD, lambda i, j: j),
                  _slot_spec(wdown, FF, MLP_TF_FWD, lambda i, j: j), vec(), row()],
        out_specs=[pl.BlockSpec((tm, MLP_TF_FWD), lambda i, j: (i, j)), row(), row(), row(),
                   pl.BlockSpec((1, 1), lambda i, j: (0, 0)), vec()],
        out_shape=[jax.ShapeDtypeStruct((t_n, FF), MXU), jax.ShapeDtypeStruct((t_n, D), MXU),
                   jax.ShapeDtypeStruct((t_n, D), F32), jax.ShapeDtypeStruct((t_n, D), MXU),
                   jax.ShapeDtypeStruct((1, 1), F32), jax.ShapeDtypeStruct((1, D), F32)],
        scratch_shapes=[pltpu.VMEM((tm, D), F32)],
        compiler_params=_cp("arbitrary", "arbitrary"),
    )(h1, g2, wup[0], wdown[0], g3, tgt)


def _mlp_bwd(dff, hp, wup, wdown):
    t_n = dff.shape[0]
    tm = min(t_n, MLP_TM)
    n_f = FF // MLP_TF_BWD

    def mm_body(dff_ref, hp_ref, wu_ref, wd_ref, dv_ref, dhp_ref, hid_ref):
        @pl.when(pl.program_id(1) == 0)
        def _():
            dv_ref[...] = jnp.zeros_like(dv_ref)

        relu = jnp.maximum(hp_ref[...].astype(F32), 0.0)
        hid_ref[...] = jnp.square(relu).astype(MXU)
        dhid = lax.dot_general(dff_ref[...], wd_ref[...].reshape(MLP_TF_BWD, D), (((1,), (1,)), ((), ())),
                               preferred_element_type=F32)
        dhp = (dhid * (2.0 * relu)).astype(MXU)
        dhp_ref[...] = dhp
        dv_ref[...] += jnp.dot(dhp, wu_ref[...].reshape(MLP_TF_BWD, D), preferred_element_type=F32)

    row = lambda: pl.BlockSpec((tm, D), lambda i, j: (i, 0))
    blk = lambda: pl.BlockSpec((tm, MLP_TF_BWD), lambda i, j: (i, j))
    wblk = lambda w: _slot_spec(w, FF, MLP_TF_BWD, lambda i, j: j)
    return pl.pallas_call(
        mm_body, name="mlp_bwd", grid=(t_n // tm, n_f),
        in_specs=[row(), blk(), wblk(wup), wblk(wdown)], out_specs=[row(), blk(), blk()],
        out_shape=[jax.ShapeDtypeStruct((t_n, D), F32), jax.ShapeDtypeStruct((t_n, FF), MXU),
                   jax.ShapeDtypeStruct((t_n, FF), MXU)],
        compiler_params=_cp("parallel", "arbitrary"),
    )(dff, hp, wup[0], wdown[0])


def _wgrad(a, g, name):
    t_n, k_n = a.shape
    n_n = g.shape[1]
    tt = min(t_n, 1024)
    tk, tn = min(k_n, 1024), min(n_n, 1024)

    n_t = t_n // tt

    def body(a_ref, g_ref, o_ref, acc):
        t = pl.program_id(2)

        @pl.when(t == 0)
        def _():
            acc[...] = jnp.zeros_like(acc)

        acc[...] += lax.dot_general(a_ref[...], g_ref[...], (((0,), (0,)), ((), ())), preferred_element_type=F32)

        @pl.when(t == n_t - 1)
        def _():
            o_ref[...] = acc[...].astype(o_ref.dtype)

    return pl.pallas_call(
        body, name=name, grid=(k_n // tk, n_n // tn, n_t),
        in_specs=[pl.BlockSpec((tt, tk), lambda k, n, t: (t, k)), pl.BlockSpec((tt, tn), lambda k, n, t: (t, n))],
        out_specs=pl.BlockSpec((tk, tn), lambda k, n, t: (k, n)),
        out_shape=jax.ShapeDtypeStruct((k_n, n_n), MXU),
        scratch_shapes=[pltpu.VMEM((tk, tn), F32)],
        compiler_params=_cp("parallel", "parallel", "arbitrary"),
    )(a, g)


def _wgrad_segs(segs, g, name):
    t_n, n_n = g.shape
    n_s = len(segs)
    tt = min(t_n, 1024)
    n_t = t_n // tt

    def body(*refs):
        a_refs = refs[:n_s]
        g_ref, o_ref, acc = refs[n_s:]
        s_id, t = pl.program_id(0), pl.program_id(1)

        @pl.when(t == 0)
        def _():
            acc[...] = jnp.zeros_like(acc)

        for s in range(n_s):
            @pl.when(s_id == s)
            def _(s=s):
                acc[...] += lax.dot_general(a_refs[s][...], g_ref[...], (((0,), (0,)), ((), ())),
                                            preferred_element_type=F32)

        @pl.when(t == n_t - 1)
        def _():
            o_ref[...] = acc[...].astype(o_ref.dtype)

    seg_spec = lambda s: pl.BlockSpec((tt, SEG), lambda i, t: (jnp.where(i == s, t, jnp.where(i < s, 0, n_t - 1)), 0))
    return pl.pallas_call(
        body, name=name, grid=(n_s, n_t),
        in_specs=[seg_spec(s) for s in range(n_s)] + [pl.BlockSpec((tt, n_n), lambda i, t: (t, 0))],
        out_specs=pl.BlockSpec((SEG, n_n), lambda i, t: (i, 0)),
        out_shape=jax.ShapeDtypeStruct((n_s * SEG, n_n), MXU),
        scratch_shapes=[pltpu.VMEM((SEG, n_n), F32)],
        compiler_params=_cp("arbitrary", "arbitrary"),
    )(*segs, g)


def _inproj_bwd(dsegs, seg_ids, du_part, w, x, g0, dh1):
    t_n = x.shape[0]
    tm = min(t_n, 512)
    n_k = len(dsegs)

    def body(*refs):
        dp_refs = refs[:n_k]
        w_refs = refs[n_k:2 * n_k]
        part_ref, x_ref, g_ref, dh1_ref, dx_ref, dg0_ref = refs[2 * n_k:]

        @pl.when(pl.program_id(0) == 0)
        def _():
            dg0_ref[...] = jnp.zeros_like(dg0_ref)

        du = part_ref[...]
        for s in range(n_k):
            du = du + jnp.dot(dp_refs[s][...], w_refs[s][...], preferred_element_type=F32)
        dx, dg_rows = _rms_bwd(x_ref[...], g_ref[...], du)
        dg0_ref[...] += _rowsum(dg_rows)
        dx_ref[...] = dh1_ref[...] + dx

    row = lambda: pl.BlockSpec((tm, D), lambda i: (i, 0))
    return pl.pallas_call(
        body, name="inproj_bwd", grid=(t_n // tm,),
        in_specs=[row() for _ in range(n_k)]
        + [pl.BlockSpec((SEG, D), functools.partial(lambda i, s: (s, 0), s=s)) for s in seg_ids]
        + [row(), row(), _full((1, D)), row()],
        out_specs=[row(), _full((1, D))],
        out_shape=[jax.ShapeDtypeStruct((t_n, D), F32), jax.ShapeDtypeStruct((1, D), F32)],
        compiler_params=_cp("arbitrary"),
    )(*dsegs, *([w] * n_k), du_part, x, g0, dh1)


def _blockdiag4(w):
    w4 = w.reshape(4, 4, 64, 1, 64).astype(MXU)
    same = (jnp.arange(4)[:, None, None, None] == jnp.arange(4)[None, None, :, None])
    return jnp.where(same[None], w4, jnp.zeros((), MXU)).reshape(4, 256, 256)


def _blockdiag4_extract(g):
    g5 = g.reshape(4, 4, 64, 4, 64)
    return jnp.stack([g5[:, q, :, q, :] for q in range(4)], axis=1).reshape(NH, 64, 64)


def _local_step(x, tgt, p, after_ssd=None, late_weights=None, send_mlp_grads=None, send_late_grads=None):
    f = lambda a: a.astype(F32)
    proj, u = _inproj(x, p["norm_mix_pre"], p["w_in_pT"])
    ssm_params = (p["cw_xs"], p["cw_bc"], p["cb_xs"], p["cb_bc"], p["dt_bias"], p["a_log"], p["d_skip_x"],
                  p["ssm_norm"])
    ya, sprev, *ssd_saved = _ssd_fwd(proj, *ssm_params)
    cb_lru = p["conv_lru_b"] if after_ssd is None else p["conv_lru_b"] + after_ssd(ya)
    lru_params = (p["conv_lru_w"], cb_lru, p["wa_bd"], p["wx_bd"], p["lru_ba"], p["lru_bx"], p["lru_lambda"])
    yb, h, xr = _lru_fwd(proj, *lru_params)
    if late_weights is not None:
        p = dict(p, **late_weights(yb))
    p = dict(p, **{n: _slotted(p[n]) for n in ("w_out", "w_upT", "w_down")})
    h1, mix, merged = _merge_out(ya, yb, proj, x, p["w_out"], p["norm_mix_post"])
    hp, v, dout, dff, loss, dg3 = _mlp_fwd(h1, p["norm_mlp_pre"], p["w_upT"], p["w_down"], p["norm_mlp_post"], tgt)

    dv, dhp, hid = _mlp_bwd(dff, hp, p["w_upT"], p["w_down"])
    dh1, dmix, dya, dyb, dga, dgb, dg2, dg1, du_gates = _out_bwd(
        dv, h1, p["norm_mlp_pre"], dout, mix, ya, yb, proj, p["w_out"], p["norm_mix_post"], p["w_in_pT"])
    d_w_out = _wgrad(merged, dmix, "wgrad_out")
    (dz, dxs, dbc, dcwx, dcwb, dcbx, dcbb, ddtb, dalog, ddsk, dnrm, d_w_down, d_w_up_t) = _ssd_bwd(
        dya, ssd_saved, proj, sprev, p["cw_xs"], p["cw_bc"], *ssm_params[4:], (hid, dff, dhp, v))
    ba_lru = p["lru_ba"] if send_mlp_grads is None else p["lru_ba"] + send_mlp_grads(d_w_up_t, d_w_down, d_w_out)
    (dgl, dxl, dcwl, dcbl, dwa, dwx, dba, dbx, dlam) = _lru_bwd(
        dyb, proj, h, xr, p["conv_lru_w"], p["wa_bd"], p["wx_bd"], ba_lru, p["lru_bx"], p["lru_lambda"])
    dsegs = [dz, dgl, dxl, dga, dgb, dxs, dbc]
    d_w_in_pt = _wgrad_segs(dsegs, u, "wgrad_in")
    g0 = p["norm_mix_pre"]
    if send_late_grads is not None:
        g0 = g0 + send_late_grads(d_w_in_pt)
    grad_x, dg0 = _inproj_bwd([dz, dgl, dxl, dxs, dbc], (0, 1, 2, 5, 6), du_gates, p["w_in_pT"], x, g0, dh1)
    grads = dict(
        norm_mix_pre=dg0, w_in_pT=d_w_in_pt, conv_ssm_w=jnp.concatenate([dcwx, dcwb], axis=1),
        conv_ssm_b=jnp.concatenate([dcbx, dcbb], axis=1), dt_bias=ddtb[:, :NH], a_log=dalog[:, :NH],
        d_skip=f(ddsk).reshape(NH, 64).sum(axis=1)[None, :], ssm_norm=dnrm, conv_lru_w=dcwl, conv_lru_b=dcbl,
        lru_wa=_blockdiag4_extract(dwa), lru_ba=dba, lru_wx=_blockdiag4_extract(dwx), lru_bx=dbx, lru_lambda=dlam,
        w_out=d_w_out, norm_mix_post=dg1, norm_mlp_pre=dg2, w_upT=d_w_up_t, w_down=d_w_down, norm_mlp_post=dg3)
    return loss[0, 0], grad_x, grads


W_IN_COLS = 6672


def _w_in_t_to_padded(wt):
    z, xs, bc, dt = wt[0:1024], wt[1024:2048], wt[2048:2560], wt[2560:2576]
    gl, xl, ga, gb = wt[2576:3600], wt[3600:4624], wt[4624:5648], wt[5648:6672]
    return jnp.concatenate([z, gl, xl, ga, gb, xs, bc, dt, jnp.zeros((NP - 6672, wt.shape[1]), wt.dtype)], axis=0)


def _w_in_t_from_padded(wp):
    z, gl, xl, ga, gb = (wp[SEG * s:SEG * (s + 1)] for s in range(5))
    xs, bc, dt = wp[5120:6144], wp[6144:6656], wp[6656:6672]
    return jnp.concatenate([z, xs, bc, dt, gl, xl, ga, gb], axis=0)


def _prep_params(full, big):
    f = lambda a: a.astype(F32)
    pad128 = lambda a: jnp.pad(f(a).reshape(1, -1), ((0, 0), (0, 128 - a.size)))
    cw = f(full["conv_ssm_w"])
    cb = f(full["conv_ssm_b"]).reshape(1, -1)
    return dict(
        big, norm_mix_pre=f(full["norm_mix_pre"]).reshape(1, D),
        cw_xs=cw[:, :D], cw_bc=cw[:, D:], cb_xs=cb[:, :D], cb_bc=cb[:, D:],
        dt_bias=pad128(full["dt_bias"]), a_log=pad128(full["a_log"]),
        d_skip_x=jnp.repeat(f(full["d_skip"]).reshape(-1), 64).reshape(1, D), ssm_norm=f(full["ssm_norm"]).reshape(1, D),
        conv_lru_w=f(full["conv_lru_w"]), conv_lru_b=f(full["conv_lru_b"]).reshape(1, D),
        wa_bd=_blockdiag4(full["lru_wa"]), wx_bd=_blockdiag4(full["lru_wx"]),
        lru_ba=f(full["lru_ba"]).reshape(1, D), lru_bx=f(full["lru_bx"]).reshape(1, D),
        lru_lambda=f(full["lru_lambda"]).reshape(1, D),
        norm_mix_post=f(full["norm_mix_post"]).reshape(1, D), norm_mlp_pre=f(full["norm_mlp_pre"]).reshape(1, D),
        norm_mlp_post=f(full["norm_mlp_post"]).reshape(1, D))


MESH_ID = pl.DeviceIdType.MESH
ANY = pl.BlockSpec(memory_space=pl.ANY)


def _my_place():
    x, y, c = lax.axis_index("x"), lax.axis_index("y"), lax.axis_index("c")
    return x, y, c, 4 * x + 2 * y + c


def _peer(x, y, c, k):
    return (x ^ ((k >> 2) & 1), y ^ ((k >> 1) & 1), c ^ (k & 1))


def _all_gather(pack, name):
    r_n = pack.shape[0]
    half = -(-r_n // 32) * 16
    n_cp = 9

    def body(in_ref, out_ref, send_sems, recv_sems, local_sem):
        x, y, c, me = _my_place()
        here, sibling, x_nbr, y_nbr = (x, y, c), (x, y, 1 - c), (1 - x, y, c), (x, 1 - y, c)
        part = {"all": pl.ds(0, r_n), "lo": pl.ds(0, half), "hi": pl.ds(half, r_n - half)}

        def copy(j, block, rows, to, src=None):
            return pltpu.make_async_remote_copy(
                src_ref=out_ref.at[block, part[rows]] if src is None else src, dst_ref=out_ref.at[block, part[rows]],
                send_sem=send_sems.at[j], recv_sem=recv_sems.at[j], device_id=to, device_id_type=MESH_ID)

        mine = pltpu.make_async_copy(in_ref, out_ref.at[me], local_sem)
        mine.start()
        first = [copy(0, me, "all", sibling, src=in_ref), copy(1, me, "all", x_nbr, src=in_ref),
                 copy(2, me, "all", y_nbr, src=in_ref)]
        for cp in first:
            cp.start()
        relay = [[(3, me ^ 4, "lo", y_nbr), (5, me ^ 4, "all", sibling)],
                 [(4, me ^ 2, "hi", x_nbr), (6, me ^ 2, "all", sibling)],
                 [(7, me ^ 6, "lo", sibling)],
                 [(8, me ^ 6, "hi", sibling)]]
        landed = [(1, me ^ 4, "all"), (2, me ^ 2, "all"), (3, me ^ 6, "lo"), (4, me ^ 6, "hi")]
        passed = []
        for (j, block, rows), nxt in zip(landed, relay):
            copy(j, block, rows, here).wait_recv()
            for args in nxt:
                cp = copy(*args)
                cp.start()
                passed.append(cp)
        for j, block, rows in ((0, me ^ 1, "all"), (5, me ^ 5, "all"), (6, me ^ 3, "all"), (7, me ^ 7, "lo"),
                               (8, me ^ 7, "hi")):
            copy(j, block, rows, here).wait_recv()
        for cp in first + passed:
            cp.wait_send()
        mine.wait()

    return pl.pallas_call(
        body, name=name, in_specs=[ANY], out_specs=ANY,
        out_shape=jax.ShapeDtypeStruct((N_DEV,) + pack.shape, pack.dtype),
        scratch_shapes=[pltpu.SemaphoreType.DMA((n_cp,)), pltpu.SemaphoreType.DMA((n_cp,)), pltpu.SemaphoreType.DMA],
    )(pack)


HBM = pl.BlockSpec(memory_space=pltpu.HBM)
SEM = pl.BlockSpec(memory_space=pltpu.SEMAPHORE)
PLAN_GATHER_ICI = tuple((k, "pack", 0) for k in (2, 4, 6, 0))
PLAN_GATHER_D2D = ((1, "pack", 0),) + tuple((1, s, s) for s in (2, 4, 6))
PLAN_SCATTER_CHIPS = tuple((k, "chip", "chip") for k in (2, 4, 6, 0))


def _plan_copy(j, plan, src_ref, land_ref, sems):
    k, source, slot = plan[j]
    x, y, c, me = _my_place()
    if source == "pack":
        src = src_ref
    elif source == "chip":
        src = src_ref.at[(me ^ k) >> 1]
    else:
        src = land_ref.at[me ^ source]
    dst = land_ref.at[me >> 1] if slot == "chip" else land_ref.at[me ^ slot]
    if k == 0:
        return pltpu.make_async_copy(src, dst, sems[j])
    return pltpu.make_async_remote_copy(
        src_ref=src, dst_ref=dst, send_sem=sems[j], recv_sem=sems[len(plan) + j],
        device_id=_peer(x, y, c, k), device_id_type=MESH_ID)


def _exchange_start(src, land, plan, name):
    n_c = len(plan)
    if land is None:
        slots = src.shape[0] if src.ndim == 3 else N_DEV
        land = pltpu.with_memory_space_constraint(lax.empty((slots,) + src.shape[-2:], src.dtype), pltpu.HBM)

    def body(src_ref, land_ref, *rest):
        sems, token = rest[:2 * n_c], rest[2 * n_c + 2]
        for j in range(n_c):
            _plan_copy(j, plan, src_ref, land_ref, sems).start()
        token[...] = jnp.zeros_like(token)

    outs = pl.pallas_call(
        body, name=name,
        out_shape=(pltpu.SemaphoreType.DMA(()),) * (2 * n_c) + (
            pltpu.HBM(src.shape, src.dtype), pltpu.HBM(land.shape, land.dtype), jax.ShapeDtypeStruct((8, 128), F32)),
        in_specs=(HBM, HBM), out_specs=(SEM,) * (2 * n_c) + (HBM, HBM, pl.BlockSpec(memory_space=pltpu.VMEM)),
        input_output_aliases={0: 2 * n_c, 1: 2 * n_c + 1},
        compiler_params=pltpu.CompilerParams(has_side_effects=pltpu.SideEffectType.DATAFLOW_SIDE_EFFECTING),
    )(pltpu.with_memory_space_constraint(src, pltpu.HBM), land)
    return outs[:2 * n_c], outs[2 * n_c], outs[2 * n_c + 1], outs[2 * n_c + 2]


def _exchange_wait(sems, src_thru, land_thru, after, plan, name):
    n_c = len(plan)

    def body(src_ref, land_ref, *rest):
        for j in range(n_c):
            cp = _plan_copy(j, plan, src_ref, land_ref, rest[:2 * n_c])
            if plan[j][0] == 0:
                cp.wait()
            else:
                cp.wait_send()
                cp.wait_recv()

    return pl.pallas_call(
        body, name=name,
        out_shape=(pltpu.HBM(src_thru.shape, src_thru.dtype), pltpu.HBM(land_thru.shape, land_thru.dtype)),
        in_specs=(HBM, HBM) + (SEM,) * (2 * n_c) + (pl.BlockSpec(memory_space=pl.ANY),), out_specs=(HBM, HBM),
        input_output_aliases={0: 0, 1: 1},
        compiler_params=pltpu.CompilerParams(has_side_effects=pltpu.SideEffectType.DATAFLOW_SIDE_EFFECTING),
    )(src_thru, land_thru, *sems, after)


def _parts_copies(src_refs, land_ref, sems):
    x, y, c, me = _my_place()
    n_p = len(src_refs)
    n_c = (N_DEV - 1) * n_p
    remote, local = [], []
    row0 = 0
    for p, src_ref in enumerate(src_refs):
        rows = pl.ds(row0, src_ref.shape[1])
        row0 += src_ref.shape[1]
        for k in range(1, N_DEV):
            j = (k - 1) * n_p + p
            remote.append(pltpu.make_async_remote_copy(
                src_ref=src_ref.at[me ^ k], dst_ref=land_ref.at[me, rows], send_sem=sems[j], recv_sem=sems[n_c + j],
                device_id=_peer(x, y, c, k), device_id_type=MESH_ID))
        local.append(pltpu.make_async_copy(src_ref.at[me], land_ref.at[me, rows], sems[2 * n_c + p]))
    return remote, local


def _scatter_parts_start(srcs, name):
    n_p = len(srcs)
    n_s = (2 * (N_DEV - 1) + 1) * n_p
    hbm = lambda a: pltpu.with_memory_space_constraint(a, pltpu.HBM)
    land = hbm(lax.empty((N_DEV, sum(s.shape[1] for s in srcs), srcs[0].shape[2]), srcs[0].dtype))

    def body(*refs):
        sems, token = refs[n_p + 1:n_p + 1 + n_s], refs[-1]
        remote, local = _parts_copies(refs[:n_p], refs[n_p], sems)
        for cp in remote + local:
            cp.start()
        token[...] = jnp.zeros_like(token)

    thru = [pltpu.HBM(a.shape, a.dtype) for a in list(srcs) + [land]]
    outs = pl.pallas_call(
        body, name=name,
        out_shape=(pltpu.SemaphoreType.DMA(()),) * n_s + tuple(thru) + (jax.ShapeDtypeStruct((8, 128), F32),),
        in_specs=(HBM,) * (n_p + 1),
        out_specs=(SEM,) * n_s + (HBM,) * (n_p + 1) + (pl.BlockSpec(memory_space=pltpu.VMEM),),
        input_output_aliases={i: n_s + i for i in range(n_p + 1)},
        compiler_params=pltpu.CompilerParams(has_side_effects=pltpu.SideEffectType.DATAFLOW_SIDE_EFFECTING),
    )(*[hbm(s) for s in srcs], land)
    return outs[:n_s], outs[n_s:n_s + n_p], outs[n_s + n_p], outs[n_s + n_p + 1]


def _scatter_parts_wait(sems, srcs_thru, land_thru, after, name):
    n_p = len(srcs_thru)

    def body(*refs):
        remote, local = _parts_copies(refs[:n_p], refs[n_p], refs[n_p + 1:n_p + 1 + len(sems)])
        for cp in remote:
            cp.wait_send()
            cp.wait_recv()
        for cp in local:
            cp.wait()

    thru = [pltpu.HBM(a.shape, a.dtype) for a in list(srcs_thru) + [land_thru]]
    outs = pl.pallas_call(
        body, name=name, out_shape=tuple(thru),
        in_specs=(HBM,) * (n_p + 1) + (SEM,) * len(sems) + (pl.BlockSpec(memory_space=pl.ANY),),
        out_specs=(HBM,) * (n_p + 1), input_output_aliases={i: i for i in range(n_p + 1)},
        compiler_params=pltpu.CompilerParams(has_side_effects=pltpu.SideEffectType.DATAFLOW_SIDE_EFFECTING),
    )(*srcs_thru, land_thru, *sems, after)
    return outs[:n_p], outs[n_p]


def _pair_exchange(src, name):
    n_s = N_DEV // 2

    def body(src_ref, land_ref, send_sems, recv_sems):
        x, y, c, _ = _my_place()
        cps = [pltpu.make_async_remote_copy(
            src_ref=src_ref.at[2 * i + 1 - c], dst_ref=land_ref.at[i], send_sem=send_sems.at[i],
            recv_sem=recv_sems.at[i], device_id=(x, y, 1 - c), device_id_type=MESH_ID) for i in range(n_s)]
        for cp in cps:
            cp.start()
        for cp in cps:
            cp.wait_recv()
        for cp in cps:
            cp.wait_send()

    return pl.pallas_call(
        body, name=name, in_specs=[ANY], out_specs=ANY,
        out_shape=jax.ShapeDtypeStruct((n_s,) + src.shape[1:], src.dtype),
        scratch_shapes=[pltpu.SemaphoreType.DMA((n_s,)), pltpu.SemaphoreType.DMA((n_s,))],
    )(src)


def _pair_sum(src, land, name):
    n_s, r_n, c_n = land.shape
    tr = max(t for t in range(16, 513, 16) if r_n % t == 0)

    def body(c_ref, a_ref, b_ref, o_ref):
        o_ref[...] = (a_ref[...].astype(F32) + b_ref[...].astype(F32)).astype(o_ref.dtype)

    blk = lambda: pl.BlockSpec((1, tr, c_n), lambda i, j, c_ref: (i, j, 0))
    return pl.pallas_call(
        body, name=name,
        grid_spec=pltpu.PrefetchScalarGridSpec(
            num_scalar_prefetch=1, grid=(n_s, r_n // tr),
            in_specs=[pl.BlockSpec((1, tr, c_n), lambda i, j, c_ref: (2 * i + c_ref[0], j, 0)), blk()],
            out_specs=blk()),
        out_shape=jax.ShapeDtypeStruct(land.shape, land.dtype),
        compiler_params=_cp("parallel", "parallel"),
    )(lax.axis_index("c").astype(jnp.int32).reshape(1), src, land)


def _slot_sum(parts, name):
    n_s, r_n, c_n = parts.shape
    tr = max(t for t in range(16, 513, 16) if r_n % t == 0)

    def body(p_ref, o_ref):
        acc = p_ref[0].astype(F32)
        for k in range(1, n_s):
            acc = acc + p_ref[k].astype(F32)
        o_ref[...] = acc

    return pl.pallas_call(
        body, name=name, grid=(r_n // tr,),
        in_specs=[pl.BlockSpec((n_s, tr, c_n), lambda i: (0, i, 0))],
        out_specs=pl.BlockSpec((tr, c_n), lambda i: (i, 0)),
        out_shape=jax.ShapeDtypeStruct((r_n, c_n), F32),
        compiler_params=_cp("parallel"),
    )(parts)


def _adam_math(w, g, m, v):
    m = ADAM_B1 * m + (1.0 - ADAM_B1) * g
    v = ADAM_B2 * v + (1.0 - ADAM_B2) * jnp.square(g)
    m_hat = m / (1.0 - ADAM_B1 ** ADAM_STEP)
    v_hat = v / (1.0 - ADAM_B2 ** ADAM_STEP)
    return -ADAM_LR * (m_hat / (jnp.sqrt(v_hat) + ADAM_EPS) + ADAM_WD * w), m, v


def _adam_big(w, g, m, v, name):
    def body(w_ref, g_ref, m_ref, v_ref, d_ref, mo_ref, vo_ref):
        d_ref[...], mo_ref[...], vo_ref[...] = _adam_math(w_ref[...], g_ref[...], m_ref[...], v_ref[...])

    if w.ndim == 3:
        _, r_n, c_n = w.shape
        tr = min(r_n, 256)
        grid = (r_n // tr,)
        blk = lambda: pl.BlockSpec((1, tr, c_n), lambda i: (0, i, 0))
    else:
        r_n, c_n = w.shape
        tc = min(c_n, 256)
        grid = (c_n // tc,)
        blk = lambda: pl.BlockSpec((r_n, tc), lambda i: (0, i))
    return pl.pallas_call(
        body, name=name, grid=grid, in_specs=[blk(), blk(), blk(), blk()], out_specs=[blk(), blk(), blk()],
        out_shape=[jax.ShapeDtypeStruct(w.shape, F32)] * 3, compiler_params=_cp("parallel"),
    )(w, g, m, v)


def _adam_small(groups, where, wmv, total):
    n, n_g = len(wmv), len(groups)
    t_g, t_r = total

    def body(*refs):
        g_refs = refs[:n_g]
        w_refs = refs[n_g:n_g + 3 * n]
        o_refs = refs[n_g + 3 * n:]
        for q in range(n):
            w_ref, m_ref, v_ref = w_refs[3 * q:3 * q + 3]
            r, c = w_ref.shape
            gi, r0 = where[q]
            g = g_refs[gi][r0:r0 + r, 0:c]
            d, m, v = _adam_math(w_ref[...], g, m_ref[...], v_ref[...])
            o_refs[4 * q][...] = g
            o_refs[4 * q + 1][...] = d
            o_refs[4 * q + 2][...] = m
            o_refs[4 * q + 3][...] = v
        o_refs[4 * n][...] = g_refs[t_g][t_r:t_r + 1, :]

    flat_wmv = [a for t in wmv for a in t]
    vm = pl.BlockSpec(memory_space=pltpu.VMEM)
    outs = pl.pallas_call(
        body, name="adam_small", in_specs=[vm] * (n_g + 3 * n), out_specs=[vm] * (4 * n + 1),
        out_shape=[jax.ShapeDtypeStruct(t[0].shape, F32) for t in wmv for _ in range(4)]
        + [jax.ShapeDtypeStruct((1, groups[t_g].shape[1]), F32)],
        compiler_params=pltpu.CompilerParams(vmem_limit_bytes=VMEM_LIMIT),
    )(*groups, *flat_wmv)
    return [tuple(outs[4 * q:4 * q + 4]) for q in range(n)], outs[4 * n]


WEIGHTS = ["norm_mix_pre", "w_in", "conv_ssm_w", "conv_ssm_b", "dt_bias", "a_log", "d_skip", "ssm_norm", "conv_lru_w",
           "conv_lru_b", "lru_wa", "lru_ba", "lru_wx", "lru_bx", "lru_lambda", "w_out", "norm_mix_post", "norm_mlp_pre",
           "w_up", "w_down", "norm_mlp_post"]
BIG = ["w_out", "w_up", "w_down", "w_in"]
IN_ROWS = W_IN_COLS // N_DEV
IN_PAD, EARLY_ROWS = 848, 880
LATE_ROWS = 1152
GRAD_LATE_ROWS = 864
CONV_SSM_COLS, CONV_LRU_COLS = 1536 // N_DEV, D // N_DEV
SMALL = [("norm_mix_pre", (1, D), 0, 0), ("ssm_norm", (1, D), 0, 1), ("conv_lru_b", (1, D), 0, 2),
         ("lru_lambda", (1, D), 0, 3), ("norm_mix_post", (1, D), 0, 4), ("norm_mlp_pre", (1, D), 0, 5),
         ("norm_mlp_post", (1, D), 0, 6), ("conv_ssm_b", (1, 1536), 1, 0), ("dt_bias", (1, NH), 2, 0),
         ("a_log", (1, NH), 2, 1), ("d_skip", (1, NH), 2, 2), ("conv_ssm_w", (4, CONV_SSM_COLS), 3, 0),
         ("conv_lru_w", (4, CONV_LRU_COLS), 4, 0), ("lru_wa", (D, 64), 5, 0), ("lru_wx", (D, 64), 5, D),
         ("lru_ba", (NH, 64), 6, 0), ("lru_bx", (NH, 64), 6, NH)]
SMALL_GROUPS = [(8, D), (1, 1536), (8, 128), (4, 1536), (4, D), (2 * D, 64), (2 * NH, 64)]


def _pad_rows(flat, mult):
    n = flat.shape[0]
    rows = -(-n // (128 * mult)) * mult
    return jnp.pad(flat, (0, rows * 128 - n)).reshape(rows, 128)


def _split3(a):
    hi = a.astype(MXU)
    r1 = a - hi.astype(F32)
    mid = r1.astype(MXU)
    lo = (r1 - mid.astype(F32)).astype(MXU)
    return jnp.stack([hi, mid, lo])


def _early_pack(a, me):
    bf = lambda t: t.astype(MXU)
    conv = lambda t, c: jnp.pad(_split3(t).reshape(12, c), ((0, 4), (0, D - c)))
    shifted = lax.dynamic_update_slice(jnp.zeros((IN_PAD, D), MXU), bf(a["w_in"][0]).T, (2 * me, 0))
    return jnp.concatenate([shifted, conv(a["conv_ssm_w"][0], CONV_SSM_COLS), conv(a["conv_lru_w"][0], CONV_LRU_COLS)],
                           axis=0)


TILE = 16
SHARD_TILES = IN_PAD // TILE
SHARD_STEP = (IN_ROWS // TILE)
SEG_TILES = ((0, 64, 0), (64, 128, 320), (128, 160, 384), (160, 161, 416), (161, 225, 64), (225, 289, 128),
             (289, 353, 192), (353, 417, 256))


def _tile_runs(lo, hi):
    runs = []
    for s0, s1, d0 in SEG_TILES:
        a, b = max(lo, s0), min(hi, s1)
        if a < b:
            runs.append((a, b - a, d0 + a - s0))
    return runs


def _assemble_w_in(g):
    whole = []
    for k in range(N_DEV):
        lo = SHARD_STEP * k + (1 if k else 0)
        hi = SHARD_STEP * (k + 1) + (1 if k == N_DEV - 1 else 0)
        whole += [(k, a - SHARD_STEP * k, n, d) for a, n, d in _tile_runs(lo, hi)]
    split = [(k, _tile_runs(SHARD_STEP * k, SHARD_STEP * k + 1)[0][2]) for k in range(1, N_DEV)]

    def body(g_ref, o_ref):
        rows = lambda t, n=1: pl.ds(TILE * t, TILE * n)
        for k, t, n, d in whole:
            o_ref[rows(d, n), :] = g_ref[k, rows(t, n), :]
        for k, d in split:
            o_ref[rows(d), :] = g_ref[k - 1, rows(SHARD_STEP), :] + g_ref[k, rows(0), :]
        o_ref[pl.ds(W_IN_COLS, NP - W_IN_COLS), :] = jnp.zeros((NP - W_IN_COLS, D), o_ref.dtype)

    vm = pl.BlockSpec(memory_space=pltpu.VMEM)
    return pl.pallas_call(
        body, name="assemble_w_in", in_specs=[vm], out_specs=vm, out_shape=jax.ShapeDtypeStruct((NP, D), g.dtype),
        compiler_params=pltpu.CompilerParams(vmem_limit_bytes=VMEM_LIMIT),
    )(g)


def _scatter_w_in_grad(dw):
    runs = [(k, a - SHARD_STEP * k, n, d) for k in range(N_DEV)
            for a, n, d in _tile_runs(SHARD_STEP * k, SHARD_STEP * k + SHARD_TILES)]
    pad = GRAD_LATE_ROWS - IN_PAD

    def body(dw_ref, o_ref):
        rows = lambda t, n: pl.ds(TILE * t, TILE * n)
        for k, t, n, d in runs:
            o_ref[k, rows(t, n), :] = dw_ref[rows(d, n), :]
        for k in range(N_DEV):
            o_ref[k, pl.ds(IN_PAD, pad), :] = jnp.zeros((pad, D), o_ref.dtype)

    vm = pl.BlockSpec(memory_space=pltpu.VMEM)
    return pl.pallas_call(
        body, name="scatter_w_in_grad", in_specs=[vm], out_specs=vm,
        out_shape=jax.ShapeDtypeStruct((N_DEV, GRAD_LATE_ROWS, D), dw.dtype),
        compiler_params=pltpu.CompilerParams(vmem_limit_bytes=VMEM_LIMIT),
    )(dw)


def _early_unpack(g):
    w_in_pt = _assemble_w_in(g)
    conv = {}
    for n, r0, c in (("conv_ssm_w", IN_PAD, CONV_SSM_COLS), ("conv_lru_w", IN_PAD + 16, CONV_LRU_COLS)):
        s = g[:, r0:r0 + 12, :c].astype(F32).reshape(N_DEV, 3, 4, c)
        conv[n] = ((s[:, 0] + s[:, 1]) + s[:, 2]).transpose(1, 0, 2).reshape(4, N_DEV * c)
    return w_in_pt, conv


def _late_pack(a):
    bf = lambda t: t.astype(MXU)
    return jnp.concatenate([bf(a["w_up"][0]).T, bf(a["w_down"][0]), bf(a["w_out"][0])], axis=0)


def _late_unpack(g):
    return dict(w_upT=(g, 0), w_down=(g, 1), w_out=(g, 2 * FF // D))


def kernel(x, norm_mix_pre, w_in, conv_ssm_w, conv_ssm_b, dt_bias, a_log, d_skip, ssm_norm, conv_lru_w, conv_lru_b, lru_wa, lru_ba, lru_wx, lru_bx, lru_lambda, w_out, norm_mix_post, norm_mlp_pre, w_up, w_down, norm_mlp_post, loss_target, m_norm_mix_pre, m_w_in, m_conv_ssm_w, m_conv_ssm_b, m_dt_bias, m_a_log, m_d_skip, m_ssm_norm, m_conv_lru_w, m_conv_lru_b, m_lru_wa, m_lru_ba, m_lru_wx, m_lru_bx, m_lru_lambda, m_w_out, m_norm_mix_post, m_norm_mlp_pre, m_w_up, m_w_down, m_norm_mlp_post, v_norm_mix_pre, v_w_in, v_conv_ssm_w, v_conv_ssm_b, v_dt_bias, v_a_log, v_d_skip, v_ssm_norm, v_conv_lru_w, v_conv_lru_b, v_lru_wa, v_lru_ba, v_lru_wx, v_lru_bx, v_lru_lambda, v_w_out, v_norm_mix_post, v_norm_mlp_pre, v_w_up, v_w_down, v_norm_mlp_post):
    vals = (norm_mix_pre, w_in, conv_ssm_w, conv_ssm_b, dt_bias, a_log, d_skip, ssm_norm, conv_lru_w, conv_lru_b, lru_wa, lru_ba, lru_wx, lru_bx, lru_lambda, w_out, norm_mix_post, norm_mlp_pre, w_up, w_down, norm_mlp_post)
    m_vals = (m_norm_mix_pre, m_w_in, m_conv_ssm_w, m_conv_ssm_b, m_dt_bias, m_a_log, m_d_skip, m_ssm_norm, m_conv_lru_w, m_conv_lru_b, m_lru_wa, m_lru_ba, m_lru_wx, m_lru_bx, m_lru_lambda, m_w_out, m_norm_mix_post, m_norm_mlp_pre, m_w_up, m_w_down, m_norm_mlp_post)
    v_vals = (v_norm_mix_pre, v_w_in, v_conv_ssm_w, v_conv_ssm_b, v_dt_bias, v_a_log, v_d_skip, v_ssm_norm, v_conv_lru_w, v_conv_lru_b, v_lru_wa, v_lru_ba, v_lru_wx, v_lru_bx, v_lru_lambda, v_w_out, v_norm_mix_post, v_norm_mlp_pre, v_w_up, v_w_down, v_norm_mlp_post)
    w = dict(zip(WEIGHTS, vals))
    m = dict(zip(WEIGHTS, m_vals))
    v = dict(zip(WEIGHTS, v_vals))
    me = 4 * lax.axis_index("x") + 2 * lax.axis_index("y") + lax.axis_index("c")

    bf = lambda t: t.astype(MXU)
    late = _late_pack(w)
    early = _all_gather(_early_pack(w, me), "early_weights_all_gather")
    late, early = lax.optimization_barrier((late, early))
    lw = {}
    lw["sems"], lw["src"], lw["land"], token = _exchange_start(late, None, PLAN_GATHER_ICI, "late_weights_ici_start")
    w_in_pt, conv_w = _early_unpack(early)
    full = {n: (conv_w[n] if n in conv_w else w[n][0]) for n in WEIGHTS if n not in BIG}
    full["norm_mix_pre"] = full["norm_mix_pre"] + token[0, 0]

    def after_ssd(after):
        src, land = _exchange_wait(lw["sems"], lw["src"], lw["land"], after, PLAN_GATHER_ICI, "late_weights_ici_wait")
        lw["sems"], lw["src"], lw["land"], tok = _exchange_start(src, land, PLAN_GATHER_D2D, "late_weights_d2d_start")
        return tok[0, 0]

    def late_weights(after):
        src, land = _exchange_wait(lw["sems"], lw["src"], lw["land"], after, PLAN_GATHER_D2D, "late_weights_d2d_wait")
        return _late_unpack(land)

    sent = {}

    def send_mlp_grads(d_w_up_t, d_w_down, d_w_out):
        srcs = [bf(d).reshape(N_DEV, -1, D) for d in (d_w_up_t, d_w_down, d_w_out)]
        sent["sems"], sent["srcs"], sent["land"], tok = _scatter_parts_start(srcs, "mlp_grads_start")
        return tok[0, 0]

    def send_late_grads(d_w_in_pt):
        src = _scatter_w_in_grad(d_w_in_pt)
        chip = _pair_sum(src, _pair_exchange(src, "late_grads_pair_exchange"), "late_grads_pair_sum")
        sent["sems2"], sent["src2"], sent["land2"], tok = _exchange_start(chip, None, PLAN_SCATTER_CHIPS,
                                                                          "late_grads_start")
        return tok[0, 0]

    loss, grad_x, g = _local_step(x[0], loss_target[0], _prep_params(full, dict(w_in_pT=w_in_pt)), after_ssd,
                                  late_weights, send_mlp_grads, send_late_grads)

    out_g, out_d, out_m, out_v = {}, {}, {}, {}
    _, mlp_land = _scatter_parts_wait(sent["sems"], sent["srcs"], sent["land"], grad_x, "mlp_grads_wait")
    g_mlp = _slot_sum(mlp_land, "slot_sum_mlp")
    fs = FF // N_DEV
    for n, gn in (("w_up", g_mlp[:fs].T[None]), ("w_down", g_mlp[fs:2 * fs][None]), ("w_out", g_mlp[2 * fs:][None])):
        out_g[n] = gn
        out_d[n], out_m[n], out_v[n] = _adam_big(w[n], gn, m[n], v[n], "adam_" + n)
    zrow = jnp.zeros((1, D), F32)
    pad16 = lambda a: jnp.pad(a, ((0, 0), (0, 128 - NH)))
    small_parts = [
        jnp.concatenate([g["norm_mix_pre"], g["ssm_norm"], g["conv_lru_b"], g["lru_lambda"], g["norm_mix_post"],
                         g["norm_mlp_pre"], g["norm_mlp_post"], zrow], axis=0),
        g["conv_ssm_b"],
        jnp.concatenate([pad16(g["dt_bias"]), pad16(g["a_log"]), pad16(g["d_skip"]), jnp.full((1, 128), loss, F32),
                         jnp.zeros((4, 128), F32)], axis=0),
        g["conv_ssm_w"], g["conv_lru_w"],
        jnp.concatenate([g["lru_wa"].reshape(D, 64), g["lru_wx"].reshape(D, 64)], axis=0),
        jnp.concatenate([g["lru_ba"].reshape(NH, 64), g["lru_bx"].reshape(NH, 64)], axis=0)]
    small = _pad_rows(jnp.concatenate([s.reshape(-1) for s in small_parts]), 64)
    small, _ = lax.optimization_barrier((small, out_v["w_out"]))
    small_all = _all_gather(small, "small_grads_all_gather")

    _, lg_land = _exchange_wait(sent["sems2"], sent["src2"], sent["land2"], small_all, PLAN_SCATTER_CHIPS,
                                     "late_grads_wait")
    g_late = _slot_sum(lg_land, "slot_sum_late")
    gt = lax.dynamic_slice(g_late, (2 * me, 0), (IN_ROWS, D))
    dt_, mt_, vt_ = _adam_big(w["w_in"][0].T, gt, m["w_in"][0].T, v["w_in"][0].T, "adam_w_in")
    out_g["w_in"], out_d["w_in"], out_m["w_in"], out_v["w_in"] = gt.T[None], dt_.T[None], mt_.T[None], vt_.T[None]
    sflat = _slot_sum(small_all, "slot_sum_small").reshape(-1)
    groups = []
    off = 0
    for r, c in SMALL_GROUPS:
        groups.append(sflat[off:off + r * c].reshape(r, c))
        off += r * c
    groups[3] = lax.dynamic_slice_in_dim(groups[3], me * CONV_SSM_COLS, CONV_SSM_COLS, axis=1)
    groups[4] = lax.dynamic_slice_in_dim(groups[4], me * CONV_LRU_COLS, CONV_LRU_COLS, axis=1)
    wmv = [(w[n].reshape(s), m[n].reshape(s), v[n].reshape(s)) for n, s, _, _ in SMALL]
    res, loss_row = _adam_small(groups, [(gi, r0) for _, _, gi, r0 in SMALL], wmv, (2, 3))
    loss = loss_row[0, 0]
    for (n, _, _, _), (g_n, d_n, m_n, v_n) in zip(SMALL, res):
        shape = w[n].shape
        out_g[n], out_d[n], out_m[n], out_v[n] = (g_n.reshape(shape), d_n.reshape(shape), m_n.reshape(shape),
                                                  v_n.reshape(shape))
    return (loss, grad_x[None], *[out_g[n] for n in WEIGHTS], *[out_d[n] for n in WEIGHTS],
            *[out_m[n] for n in WEIGHTS], *[out_v[n] for n in WEIGHTS])
```

```python
import functools

import jax
import jax.numpy as jnp
from jax import lax
from jax.experimental import pallas as pl
from jax.experimental.pallas import tpu as pltpu

F32 = jnp.float32
MXU = jnp.bfloat16
HI = lax.Precision.HIGHEST
EPS = 1e-6

D = 1024
NH = 16
NS = 128
CH = 128
FF = 4096
NP = 7168
SEG = 1024
LRU_C = 8.0
N_DEV = 8

ADAM_LR, ADAM_B1, ADAM_B2, ADAM_EPS, ADAM_WD, ADAM_STEP = 0.001, 0.9, 0.999, 1e-08, 0.01, 10

VMEM_LIMIT = 56 * 1024 * 1024


def _cp(*sem):
    return pltpu.CompilerParams(dimension_semantics=sem, vmem_limit_bytes=VMEM_LIMIT)


def _nn(a, b):
    return jnp.dot(a.astype(MXU), b.astype(MXU), preferred_element_type=F32)


def _nt(a, b):
    return lax.dot_general(a.astype(MXU), b.astype(MXU), (((1,), (1,)), ((), ())), preferred_element_type=F32)


def _tn(a, b):
    return lax.dot_general(a.astype(MXU), b.astype(MXU), (((0,), (0,)), ((), ())), preferred_element_type=F32)


_sigmoid = jax.nn.sigmoid


def _silu(x):
    return x * _sigmoid(x)


def _dsilu(x):
    s = _sigmoid(x)
    return s + x * s * (1.0 - s)


def _softplus(x):
    return jnp.maximum(x, 0.0) + jnp.log(1.0 + jnp.exp(-jnp.abs(x)))


def _rms(x, g):
    r = lax.rsqrt(jnp.mean(x * x, axis=-1, keepdims=True) + EPS)
    return x * r * g


def _rms_bwd(x, g, dy):
    r = lax.rsqrt(jnp.mean(x * x, axis=-1, keepdims=True) + EPS)
    gdy = g * dy
    dx = r * gdy - x * (r * r * r) * jnp.mean(x * gdy, axis=-1, keepdims=True)
    return dx, dy * x * r


def _rowsum(x):
    return jnp.sum(x, axis=0, keepdims=True)


def _taps_past(cur, prev8):
    r_n, c_n = cur.shape
    row = lax.broadcasted_iota(jnp.int32, (r_n, c_n), 0)
    out = []
    for k in range(4):
        s = 3 - k
        if s == 0:
            out.append(cur)
            continue
        head = jnp.concatenate([pltpu.roll(prev8, s, 0), jnp.zeros((r_n - 8, c_n), F32)], axis=0)
        out.append(jnp.where(row < s, head, pltpu.roll(cur, s, 0)))
    return out


def _taps_future(cur, fut8):
    r_n, c_n = cur.shape
    row = lax.broadcasted_iota(jnp.int32, (r_n, c_n), 0)
    out = []
    for k in range(4):
        s = 3 - k
        if s == 0:
            out.append(cur)
            continue
        tail = jnp.concatenate([jnp.zeros((r_n - 8, c_n), F32), pltpu.roll(fut8, 8 - s, 0)], axis=0)
        out.append(jnp.where(row >= r_n - s, tail, pltpu.roll(cur, r_n - s, 0)))
    return out


def _conv_apply(taps, w, b):
    acc = taps[0] * w[0:1, :]
    for k in range(1, 4):
        acc = acc + taps[k] * w[k:k + 1, :]
    return acc + b


def _inproj(x, g0, w):
    t_n = x.shape[0]
    tm = min(t_n, 1024)

    n_j = NP // SEG

    def body(x_ref, g_ref, w_ref, pb_ref, p6_ref, u_ref):
        j = pl.program_id(1)

        @pl.when(j == 0)
        def _():
            u_ref[...] = _rms(x_ref[...], g_ref[...]).astype(MXU)

        p = lax.dot_general(u_ref[...], w_ref[...], (((1,), (1,)), ((), ())), preferred_element_type=F32)

        @pl.when(j < n_j - 1)
        def _():
            pb_ref[...] = p.astype(MXU)

        @pl.when(j == n_j - 1)
        def _():
            p6_ref[...] = p

    pb, p6, u = pl.pallas_call(
        body, name="inproj", grid=(t_n // tm, n_j),
        in_specs=[pl.BlockSpec((tm, D), lambda i, j: (i, 0)), pl.BlockSpec((1, D), lambda i, j: (0, 0)),
                  pl.BlockSpec((SEG, D), lambda i, j: (j, 0))],
        out_specs=[pl.BlockSpec((tm, SEG), lambda i, j: (i, jnp.minimum(j, n_j - 2))),
                   pl.BlockSpec((tm, SEG), lambda i, j: (i, 0)), pl.BlockSpec((tm, D), lambda i, j: (i, 0))],
        out_shape=[jax.ShapeDtypeStruct((t_n, NP - SEG), MXU), jax.ShapeDtypeStruct((t_n, SEG), F32),
                   jax.ShapeDtypeStruct((t_n, D), MXU)],
        compiler_params=_cp("parallel", "arbitrary"),
    )(x, g0, w)
    return (pb, p6), u


def _ssd_prep(dtraw, dtb, alog):
    l_n = dtraw.shape[0]
    r = lax.broadcasted_iota(jnp.int32, (l_n, l_n), 0)
    c = lax.broadcasted_iota(jnp.int32, (l_n, l_n), 1)
    tril = (r >= c).astype(F32)
    triu = (r <= c).astype(F32)
    eye = (r == c).astype(F32)
    dt = _softplus(dtraw + dtb)
    adt = dt * (-jnp.exp(alog))
    ac = jnp.dot(tril, adt, preferred_element_type=F32, precision=HI)
    tn = (((0,), (0,)), ((), ()))
    ac_t = lax.dot_general(adt, triu, tn, preferred_element_type=F32, precision=HI)
    dt_t = lax.dot_general(dt, eye, tn, preferred_element_type=F32, precision=HI)
    return dt, dt_t, ac, ac_t, _rowsum(adt)


def _ssd_pair(j, xp, bg, cg, sp, dt, dt_t, ac, ac_t, aend):
    l_n = xp.shape[0]
    lane = lax.broadcasted_iota(jnp.int32, (l_n, 128), 1)
    sub = lax.broadcasted_iota(jnp.int32, (128, l_n), 0)
    lane1 = lax.broadcasted_iota(jnp.int32, (1, 128), 1)
    tri = lax.broadcasted_iota(jnp.int32, (l_n, l_n), 0) >= lax.broadcasted_iota(jnp.int32, (l_n, l_n), 1)
    lo = lax.broadcasted_iota(jnp.int32, (l_n, 128), 1) < 64
    lo_s = lax.broadcasted_iota(jnp.int32, (128, 128), 1) < 64
    cb = _nt(cg, bg)
    cs = _nn(cg, sp)
    x2 = jnp.concatenate([jnp.where(lo, xp, 0.0), jnp.where(lo, 0.0, xp)], axis=0)
    ws_, bs_, eo, ee = [], [], [], []
    for e in range(2):
        h = 2 * j + e
        ac_l = jnp.sum(jnp.where(lane == h, ac, 0.0), axis=1, keepdims=True)
        dt_l = jnp.sum(jnp.where(lane == h, dt, 0.0), axis=1, keepdims=True)
        a_end = jnp.sum(jnp.where(lane1 == h, aend, 0.0), axis=1, keepdims=True)
        ac_s, dt_s = ac_t[h:h + 1, :], dt_t[h:h + 1, :]
        decay = jnp.exp(jnp.where(tri, ac_l - ac_s, -1e30))
        ws_.append(cb * decay * dt_s)
        bs_.append(bg * (jnp.exp(a_end - ac_l) * dt_l))
        eo.append(jnp.exp(ac_l))
        ee.append(jnp.exp(a_end))
    y = _nn(jnp.concatenate(ws_, axis=1), x2) + jnp.where(lo, eo[0], eo[1]) * cs
    s_new = _tn(jnp.concatenate(bs_, axis=0), x2) + jnp.where(lo_s, ee[0], ee[1]) * sp
    return y, s_new


def _ssd_post(y, xs, z, dsk, nrm):
    y = (y + dsk * xs) * _silu(z)
    half = D // 2
    ya, yb = y[:, :half], y[:, half:]
    ya = ya * lax.rsqrt(jnp.mean(ya * ya, axis=-1, keepdims=True) + EPS)
    yb = yb * lax.rsqrt(jnp.mean(yb * yb, axis=-1, keepdims=True) + EPS)
    return jnp.concatenate([ya, yb], axis=1) * nrm


LAST_SEG = NP // SEG - 1


def _proj_ops(proj, seg_ids):
    return [proj[1] if s == LAST_SEG else proj[0] for s in seg_ids]


def _proj_specs(rows, seg_ids, order):
    return [pl.BlockSpec((rows, SEG), functools.partial(lambda i, c: (order(i), c), c=0 if s == LAST_SEG else s))
            for s in seg_ids]


def _prev_specs(rows, seg_ids, order):
    specs = []
    for s in seg_ids:
        n, c = (8, 0) if s == LAST_SEG else (16, s)
        specs.append(pl.BlockSpec((n, SEG), functools.partial(
            lambda i, n, c: (jnp.maximum(order(i) * (rows // n) - 1, 0), c), n=n, c=c)))
    return specs


def _prev8(ref):
    return ref[...] if ref.shape[0] == 8 else ref[8:16, :].astype(F32)


def _full(shape):
    return pl.BlockSpec(shape, lambda i: (0,) * len(shape))


def _slotted(a):
    return a if isinstance(a, tuple) else (a.reshape(N_DEV, a.shape[0] // N_DEV, a.shape[1]), 0)


def _slot_spec(w, rows, tile, index):
    per = rows // N_DEV
    b = w[1]
    return pl.BlockSpec((tile // per, per, D), lambda *ids: (index(*ids), b, 0))


def _ssd_fwd(proj, cwx, cwb, cbx, cbb, dtb, alog, dsk, nrm):
    t_n = proj[0].shape[0]
    n_c = t_n // CH
    fwd = lambda i: i

    def body(z_ref, xs_ref, bc_ref, xsp_ref, bcp_ref, cwx_ref, cwb_ref, cbx_ref, cbb_ref, dtb_ref, alog_ref,
             dsk_ref, nrm_ref, ya_ref, sprev_ref, yraw_ref, xspre_ref, bcpre_ref, s_ref):
        c = pl.program_id(0)

        @pl.when(c == 0)
        def _():
            s_ref[...] = jnp.zeros_like(s_ref)

        keep = jnp.where(c == 0, 0.0, 1.0)
        xs_pre = _conv_apply(_taps_past(xs_ref[...].astype(F32), _prev8(xsp_ref) * keep), cwx_ref[...], cbx_ref[...])
        bc_pre = _conv_apply(_taps_past(bc_ref[:, :512], bcp_ref[:, :512] * keep), cwb_ref[...], cbb_ref[...])
        xspre_ref[...] = xs_pre
        bcpre_ref[...] = bc_pre
        prep = _ssd_prep(bc_ref[:, 512:640], dtb_ref[...], alog_ref[...])
        xs = _silu(xs_pre)
        bc = _silu(bc_pre)
        sprev_ref[0] = s_ref[...]
        ys = []
        for j in range(NH // 2):
            g = j // 4
            yp, sn = _ssd_pair(j, xs[:, 128 * j:128 * j + 128], bc[:, 128 * g:128 * g + 128],
                               bc[:, 256 + 128 * g:384 + 128 * g], s_ref[:, 128 * j:128 * j + 128], *prep)
            ys.append(yp)
            s_ref[:, 128 * j:128 * j + 128] = sn
        y = jnp.concatenate(ys, axis=1)
        yraw_ref[...] = y
        ya_ref[...] = _ssd_post(y, xs, z_ref[...].astype(F32), dsk_ref[...], nrm_ref[...]).astype(ya_ref.dtype)

    return pl.pallas_call(
        body, name="ssd_fwd", grid=(n_c,),
        in_specs=_proj_specs(CH, (0, 5, 6), fwd) + _prev_specs(CH, (5, 6), fwd) + [
            _full((4, D)), _full((4, 512)), _full((1, D)), _full((1, 512)), _full((1, 128)), _full((1, 128)),
            _full((1, D)), _full((1, D))],
        out_specs=[pl.BlockSpec((CH, D), lambda i: (i, 0)), pl.BlockSpec((1, NS, D), lambda i: (i, 0, 0)),
                   pl.BlockSpec((CH, D), lambda i: (i, 0)), pl.BlockSpec((CH, D), lambda i: (i, 0)),
                   pl.BlockSpec((CH, 512), lambda i: (i, 0))],
        out_shape=[jax.ShapeDtypeStruct((t_n, D), MXU), jax.ShapeDtypeStruct((n_c, NS, D), F32),
                   jax.ShapeDtypeStruct((t_n, D), F32), jax.ShapeDtypeStruct((t_n, D), F32),
                   jax.ShapeDtypeStruct((t_n, 512), F32)],
        scratch_shapes=[pltpu.VMEM((NS, D), F32)],
        compiler_params=_cp("arbitrary"),
    )(*_proj_ops(proj, (0, 5, 6, 5, 6)), cwx, cwb, cbx, cbb, dtb, alog, dsk, nrm)


def _ssd_bwd(dya, saved, proj, sprev, cwx, cwb, dtb, alog, dsk, nrm, mlp_ops):
    t_n = proj[0].shape[0]
    n_c = t_n // CH
    rev = lambda i: n_c - 1 - i
    fb = FF // n_c

    def body(dya_ref, yraw_ref, xspre_ref, bcpre_ref, z_ref, xs_ref, bc_ref, sprev_ref, cwx_ref, cwb_ref,
             dtb_ref, alog_ref, dsk_ref, nrm_ref, hid_ref, dff_ref, dhp_ref, v_ref,
             dz_ref, dxs_ref, dbc_ref, dcwx_ref, dcwb_ref, dcbx_ref, dcbb_ref, ddtb_ref, dalog_ref, ddsk_ref,
             dnrm_ref, dwd_ref, dwu_ref, ds_ref, futx_ref, futb_ref):
        i = pl.program_id(0)
        acc_refs = (dcwx_ref, dcwb_ref, dcbx_ref, dcbb_ref, ddtb_ref, dalog_ref, ddsk_ref, dnrm_ref)
        tn = (((0,), (0,)), ((), ()))
        kt = t_n // (NH // 2)
        dwd_acc = jnp.zeros((fb, D), F32)
        dwu_acc = jnp.zeros((fb, D), F32)

        @pl.when(i == 0)
        def _():
            for r in (ds_ref, futx_ref, futb_ref) + acc_refs:
                r[...] = jnp.zeros_like(r)

        xs_pre = xspre_ref[...]
        bc_pre = bcpre_ref[...]
        xs = _silu(xs_pre)
        bc = _silu(bc_pre)
        prep, prep_vjp = jax.vjp(_ssd_prep, bc_ref[:, 512:640], dtb_ref[...], alog_ref[...])
        s_in = sprev_ref[0]

        def pair_args(j):
            g = j // 4
            return (xs[:, 128 * j:128 * j + 128], bc[:, 128 * g:128 * g + 128],
                    bc[:, 256 + 128 * g:384 + 128 * g], s_in[:, 128 * j:128 * j + 128]) + tuple(prep)

        _, post_vjp = jax.vjp(_ssd_post, yraw_ref[...], xs, z_ref[...].astype(F32), dsk_ref[...], nrm_ref[...])
        dy, dxs_skip, dz, ddsk, dnrm = post_vjp(dya_ref[...])
        dz_ref[...] = dz.astype(dz_ref.dtype)
        ddsk_ref[...] += ddsk
        dnrm_ref[...] += dnrm

        dprep = [jnp.zeros_like(p) for p in prep]
        dxp = []
        dbg = [jnp.zeros((CH, 128), F32), jnp.zeros((CH, 128), F32)]
        dcg = [jnp.zeros((CH, 128), F32), jnp.zeros((CH, 128), F32)]
        for j in range(NH // 2):
            g = j // 4
            _, pair_vjp = jax.vjp(functools.partial(_ssd_pair, j), *pair_args(j))
            cts = pair_vjp((dy[:, 128 * j:128 * j + 128], ds_ref[:, 128 * j:128 * j + 128]))
            dxp.append(cts[0])
            dbg[g] = dbg[g] + cts[1]
            dcg[g] = dcg[g] + cts[2]
            ds_ref[:, 128 * j:128 * j + 128] = cts[3]
            dprep = [a + b for a, b in zip(dprep, cts[4:])]
            rows = pl.ds(kt * j, kt)
            dwd_acc = dwd_acc + lax.dot_general(hid_ref[rows, :], dff_ref[rows, :], tn, preferred_element_type=F32)
            dwu_acc = dwu_acc + lax.dot_general(dhp_ref[rows, :], v_ref[rows, :], tn, preferred_element_type=F32)
        dwd_ref[...] = dwd_acc.astype(MXU)
        dwu_ref[...] = dwu_acc.astype(MXU)
        ddtraw, ddtb, dalog = prep_vjp(tuple(dprep))
        ddtb_ref[...] += ddtb
        dalog_ref[...] += dalog

        dxs_pre = (dxs_skip + jnp.concatenate(dxp, axis=1)) * _dsilu(xs_pre)
        dbc_pre = jnp.concatenate([dbg[0], dbg[1], dcg[0], dcg[1]], axis=1) * _dsilu(bc_pre)
        dcbx_ref[...] += _rowsum(dxs_pre)
        dcbb_ref[...] += _rowsum(dbc_pre)
        fx = _taps_future(dxs_pre, futx_ref[...])
        fbc = _taps_future(dbc_pre, futb_ref[...])
        xs_in = xs_ref[...].astype(F32)
        bc_in = bc_ref[:, :512]
        for k in range(4):
            dcwx_ref[k:k + 1, :] += _rowsum(fx[k] * xs_in)
            dcwb_ref[k:k + 1, :] += _rowsum(fbc[k] * bc_in)
        cwx = cwx_ref[...]
        cwb = cwb_ref[...]
        dxs_in = fx[0] * cwx[0:1, :]
        dbc_in = fbc[0] * cwb[0:1, :]
        for k in range(1, 4):
            dxs_in = dxs_in + fx[k] * cwx[k:k + 1, :]
            dbc_in = dbc_in + fbc[k] * cwb[k:k + 1, :]
        futx_ref[...] = dxs_pre[0:8, :]
        futb_ref[...] = dbc_pre[0:8, :]
        dxs_ref[...] = dxs_in.astype(dxs_ref.dtype)
        dbc_ref[...] = jnp.concatenate([dbc_in, ddtraw, jnp.zeros((CH, SEG - 640), F32)], axis=1).astype(dbc_ref.dtype)

    row_out = lambda: pl.BlockSpec((CH, D), lambda i: (rev(i), 0))
    outs = pl.pallas_call(
        body, name="ssd_bwd", grid=(n_c,),
        in_specs=[row_out(), row_out(), row_out(), pl.BlockSpec((CH, 512), lambda i: (rev(i), 0))]
        + _proj_specs(CH, (0, 5, 6), rev) + [pl.BlockSpec((1, NS, D), lambda i: (rev(i), 0, 0)),
                                             _full((4, D)), _full((4, 512)),
                                             _full((1, 128)), _full((1, 128)), _full((1, D)), _full((1, D)),
                                             pl.BlockSpec((t_n, fb), lambda i: (0, i)), _full((t_n, D)),
                                             pl.BlockSpec((t_n, fb), lambda i: (0, i)), _full((t_n, D))],
        out_specs=[row_out(), row_out(), row_out(), _full((4, D)), _full((4, 512)), _full((1, D)), _full((1, 512)),
                   _full((1, 128)), _full((1, 128)), _full((1, D)), _full((1, D)),
                   pl.BlockSpec((fb, D), lambda i: (i, 0)), pl.BlockSpec((fb, D), lambda i: (i, 0))],
        out_shape=[jax.ShapeDtypeStruct((t_n, D), MXU)] * 3 + [
            jax.ShapeDtypeStruct(s, F32) for s in ((4, D), (4, 512), (1, D), (1, 512), (1, 128), (1, 128), (1, D), (1, D))]
        + [jax.ShapeDtypeStruct((FF, D), MXU)] * 2,
        scratch_shapes=[pltpu.VMEM((NS, D), F32), pltpu.VMEM((8, D), F32), pltpu.VMEM((8, 512), F32)],
        compiler_params=_cp("arbitrary"),
    )(dya, *saved, *_proj_ops(proj, (0, 5, 6)), sprev, cwx, cwb, dtb, alog, dsk, nrm, *mlp_ops)
    return outs


LRU_ROWS = 256
LRU_BLK = 256


def _lru_gates_block(xr, wa, wx, ba, bx, lam):
    pr = _nn(xr, wa) + ba
    pi = _nn(xr, wx) + bx
    log_a = -LRU_C * _sigmoid(pr) * _softplus(-lam)
    a = jnp.exp(log_a)
    mult = jnp.sqrt(1.0 - jnp.exp(2.0 * log_a))
    return a, mult * (_sigmoid(pi) * xr)


def _lru_out(h, g):
    return h * jax.nn.gelu(g, approximate=True)


def _lru_fwd(proj, cw, cb, wa, wx, ba, bx, lam):
    t_n = proj[0].shape[0]
    rows = min(LRU_ROWS, t_n)
    fwd = lambda i: i

    def body(g_ref, x_ref, xp_ref, cw_ref, cb_ref, wa_ref, wx_ref, ba_ref, bx_ref, lam_ref, yb_ref, h_ref, xr_ref,
             a_s, u_s, carry):
        i = pl.program_id(0)

        @pl.when(i == 0)
        def _():
            carry[...] = jnp.zeros_like(carry)

        keep = jnp.where(i == 0, 0.0, 1.0)
        prev = _prev8(xp_ref) * keep
        cols = [pl.ds(LRU_BLK * q, LRU_BLK) for q in range(D // LRU_BLK)]
        for q, cs in enumerate(cols):
            xr = _conv_apply(_taps_past(x_ref[:, cs].astype(F32), prev[:, LRU_BLK * q:LRU_BLK * (q + 1)]),
                             cw_ref[:, cs], cb_ref[:, cs])
            xr_ref[:, cs] = xr
            a, u = _lru_gates_block(xr, wa_ref[q], wx_ref[q], ba_ref[:, cs], bx_ref[:, cs], lam_ref[:, cs])
            a_s[:, cs] = a
            u_s[:, cs] = u
        row = lax.broadcasted_iota(jnp.int32, (8, D), 0)

        def blk(b, c):
            s = pl.multiple_of(b * 8, 8)
            av = a_s[pl.ds(s, 8), :]
            uv = u_s[pl.ds(s, 8), :]
            for d in (1, 2, 4):
                m = row >= d
                uv = uv + av * jnp.where(m, pltpu.roll(uv, d, 0), 0.0)
                av = av * jnp.where(m, pltpu.roll(av, d, 0), 1.0)
            hv = uv + av * c
            h_ref[pl.ds(s, 8), :] = hv
            return hv[7:8, :]

        carry[0:1, :] = lax.fori_loop(0, rows // 8, blk, carry[0:1, :])
        for cs in cols:
            yb_ref[:, cs] = _lru_out(h_ref[:, cs], g_ref[:, cs].astype(F32)).astype(yb_ref.dtype)

    return pl.pallas_call(
        body, name="lru_fwd", grid=(t_n // rows,),
        in_specs=_proj_specs(rows, (1, 2), fwd) + _prev_specs(rows, (2,), fwd) + [
            _full((4, D)), _full((1, D)), _full((4, LRU_BLK, LRU_BLK)), _full((4, LRU_BLK, LRU_BLK)),
            _full((1, D)), _full((1, D)), _full((1, D))],
        out_specs=[pl.BlockSpec((rows, D), lambda i: (i, 0))] * 3,
        out_shape=[jax.ShapeDtypeStruct((t_n, D), MXU), jax.ShapeDtypeStruct((t_n, D), F32),
                   jax.ShapeDtypeStruct((t_n, D), F32)],
        scratch_shapes=[pltpu.VMEM((rows, D), F32), pltpu.VMEM((rows, D), F32), pltpu.VMEM((8, D), F32)],
        compiler_params=_cp("arbitrary"),
    )(*_proj_ops(proj, (1, 2, 2)), cw, cb, wa, wx, ba, bx, lam)


def _lru_bwd(dyb, proj, h, xr_saved, cw, wa, wx, ba, bx, lam):
    t_n = proj[0].shape[0]
    rows = min(LRU_ROWS, t_n)
    n_t = t_n // rows
    rev = lambda i: n_t - 1 - i
    rb = rows // 8

    def body(dyb_ref, g_ref, x_ref, h_ref, hp_ref, xr_ref, cw_ref, wa_ref, wx_ref, ba_ref, bx_ref, lam_ref,
             dg_ref, dx_ref, dcw_ref, dcb_ref, dwa_ref, dwx_ref, dba_ref, dbx_ref, dlam_ref,
             a_s, dh_s, hx_s, da_s, du_s, carry, fut):
        i = pl.program_id(0)
        acc_refs = (dcw_ref, dcb_ref, dwa_ref, dwx_ref, dba_ref, dbx_ref, dlam_ref)

        @pl.when(i == 0)
        def _():
            for r in (carry, fut) + acc_refs:
                r[...] = jnp.zeros_like(r)

        keep = jnp.where(i == n_t - 1, 0.0, 1.0)
        cols = [pl.ds(LRU_BLK * q, LRU_BLK) for q in range(D // LRU_BLK)]
        gates_vjps = []
        for q, cs in enumerate(cols):
            (a, _), gates_vjp = jax.vjp(_lru_gates_block, xr_ref[:, cs], wa_ref[q], wx_ref[q], ba_ref[:, cs],
                                        bx_ref[:, cs], lam_ref[:, cs])
            gates_vjps.append(gates_vjp)
            _, out_vjp = jax.vjp(_lru_out, h_ref[:, cs], g_ref[:, cs].astype(F32))
            dh, dg = out_vjp(dyb_ref[:, cs])
            dg_ref[:, cs] = dg.astype(dg_ref.dtype)
            a_s[:, cs] = a
            dh_s[:, cs] = dh
        hx_s[0:8, :] = hp_ref[...] * keep
        hx_s[8:, :] = h_ref[...]
        row = lax.broadcasted_iota(jnp.int32, (8, D), 0)

        def blk(b, c):
            s = pl.multiple_of((rb - 1 - b) * 8, 8)
            av = a_s[pl.ds(s, 8), :]
            dhv = dh_s[pl.ds(s, 8), :]
            kv = av * dhv
            for d in (1, 2, 4):
                m = row <= 7 - d
                kv = kv + av * jnp.where(m, pltpu.roll(kv, 8 - d, 0), 0.0)
                av = av * jnp.where(m, pltpu.roll(av, 8 - d, 0), 1.0)
            kv = kv + av * c
            gv = dhv + jnp.where(row < 7, pltpu.roll(kv, 7, 0), c)
            hb = hx_s[pl.ds(s + 8, 8), :]
            hpv = hx_s[pl.ds(s, 8), :]
            hprev = jnp.where(row >= 1, pltpu.roll(hb, 1, 0), hpv[7:8, :])
            du_s[pl.ds(s, 8), :] = gv
            da_s[pl.ds(s, 8), :] = gv * hprev
            return kv[0:1, :]

        carry[0:1, :] = lax.fori_loop(0, rb, blk, carry[0:1, :])
        for q, cs in enumerate(cols):
            dxr, dwa, dwx, dba, dbx, dlam = gates_vjps[q]((da_s[:, cs], du_s[:, cs]))
            dwa_ref[q] += dwa
            dwx_ref[q] += dwx
            dba_ref[:, cs] += dba
            dbx_ref[:, cs] += dbx
            dlam_ref[:, cs] += dlam
            dcb_ref[:, cs] += _rowsum(dxr)
            ft = _taps_future(dxr, fut[:, cs])
            x_in = x_ref[:, cs].astype(F32)
            for k in range(4):
                dcw_ref[k:k + 1, cs] += _rowsum(ft[k] * x_in)
            cwv = cw_ref[:, cs]
            dx = ft[0] * cwv[0:1, :]
            for k in range(1, 4):
                dx = dx + ft[k] * cwv[k:k + 1, :]
            fut[:, cs] = dxr[0:8, :]
            dx_ref[:, cs] = dx.astype(dx_ref.dtype)

    row_in = lambda: pl.BlockSpec((rows, D), lambda i: (rev(i), 0))
    prev_h = pl.BlockSpec((8, D), lambda i: (jnp.maximum(rev(i) * rb - 1, 0), 0))
    wspec = lambda: _full((4, LRU_BLK, LRU_BLK))
    return pl.pallas_call(
        body, name="lru_bwd", grid=(n_t,),
        in_specs=[row_in()] + _proj_specs(rows, (1, 2), rev) + [row_in(), prev_h, row_in()] + [
            _full((4, D)), wspec(), wspec(), _full((1, D)), _full((1, D)), _full((1, D))],
        out_specs=[row_in(), row_in(), _full((4, D)), _full((1, D)), wspec(), wspec(), _full((1, D)), _full((1, D)),
                   _full((1, D))],
        out_shape=[jax.ShapeDtypeStruct((t_n, D), MXU)] * 2 + [
            jax.ShapeDtypeStruct(s, F32) for s in ((4, D), (1, D), (4, LRU_BLK, LRU_BLK), (4, LRU_BLK, LRU_BLK),
                                                   (1, D), (1, D), (1, D))],
        scratch_shapes=[pltpu.VMEM((rows, D), F32), pltpu.VMEM((rows, D), F32), pltpu.VMEM((rows + 8, D), F32),
                        pltpu.VMEM((rows, D), F32), pltpu.VMEM((rows, D), F32), pltpu.VMEM((8, D), F32),
                        pltpu.VMEM((8, D), F32)],
        compiler_params=_cp("arbitrary"),
    )(dyb, *_proj_ops(proj, (1, 2)), h, h, xr_saved, cw, wa, wx, ba, bx, lam)


def _merge_out(ya, yb, proj, x, wout, g1):
    t_n = x.shape[0]
    tm = min(t_n, 512)

    def body(ya_ref, yb_ref, ga_ref, gb_ref, x_ref, w_ref, g_ref, h1_ref, mix_ref, mg_ref):
        merged = (_sigmoid(ga_ref[...].astype(F32)) * ya_ref[...].astype(F32)
                  + _sigmoid(gb_ref[...].astype(F32)) * yb_ref[...].astype(F32))
        mg = merged.astype(MXU)
        mg_ref[...] = mg
        mix = jnp.dot(mg, w_ref[...].reshape(D, D), preferred_element_type=F32)
        mix_ref[...] = mix
        h1_ref[...] = x_ref[...] + _rms(mix, g_ref[...])

    row = lambda: pl.BlockSpec((tm, D), lambda i: (i, 0))
    return pl.pallas_call(
        body, name="merge_out", grid=(t_n // tm,),
        in_specs=[row(), row()] + _proj_specs(tm, (3, 4), lambda i: i)
        + [row(), _slot_spec(wout, D, D, lambda i: 0), _full((1, D))],
        out_specs=[row(), row(), row()],
        out_shape=[jax.ShapeDtypeStruct((t_n, D), F32), jax.ShapeDtypeStruct((t_n, D), F32),
                   jax.ShapeDtypeStruct((t_n, D), MXU)],
        compiler_params=_cp("parallel"),
    )(ya, yb, *_proj_ops(proj, (3, 4)), x, wout[0], g1)


def _out_bwd(dv, h1, g2, dout, mix, ya, yb, proj, wout, g1, w_in_pt):
    t_n = dv.shape[0]
    tm = min(t_n, 256)

    def body(dv_ref, h1_ref, g2_ref, dout_ref, mix_ref, ya_ref, yb_ref, ga_ref, gb_ref, w_ref, g_ref, wa_ref, wb_ref,
             dh1_ref, dmix_ref, dya_ref, dyb_ref, dga_ref, dgb_ref, dg2_ref, dg1_ref, dug_ref):
        @pl.when(pl.program_id(0) == 0)
        def _():
            dg1_ref[...] = jnp.zeros_like(dg1_ref)
            dg2_ref[...] = jnp.zeros_like(dg2_ref)

        dx, dg_rows = _rms_bwd(h1_ref[...], g2_ref[...], dv_ref[...])
        dg2_ref[...] += _rowsum(dg_rows)
        dh1 = dout_ref[...] + dx
        dh1_ref[...] = dh1
        dmix, dg_rows = _rms_bwd(mix_ref[...], g_ref[...], dh1)
        dg1_ref[...] += _rowsum(dg_rows)
        dmix_b = dmix.astype(MXU)
        dmix_ref[...] = dmix_b
        dmg = lax.dot_general(dmix_b, w_ref[...].reshape(D, D), (((1,), (1,)), ((), ())), preferred_element_type=F32)
        sa = _sigmoid(ga_ref[...].astype(F32))
        sb = _sigmoid(gb_ref[...].astype(F32))
        dya_ref[...] = dmg * sa
        dyb_ref[...] = dmg * sb
        dga = (dmg * ya_ref[...].astype(F32) * sa * (1.0 - sa)).astype(MXU)
        dga_ref[...] = dga
        dug = jnp.dot(dga, wa_ref[...], preferred_element_type=F32)
        dgb = (dmg * yb_ref[...].astype(F32) * sb * (1.0 - sb)).astype(MXU)
        dgb_ref[...] = dgb
        dug_ref[...] = dug + jnp.dot(dgb, wb_ref[...], preferred_element_type=F32)

    row = lambda: pl.BlockSpec((tm, D), lambda i: (i, 0))
    vec = lambda: _full((1, D))
    seg = lambda s: pl.BlockSpec((SEG, D), lambda i: (s, 0))
    return pl.pallas_call(
        body, name="out_bwd", grid=(t_n // tm,),
        in_specs=[row(), row(), vec(), row(), row(), row(), row()] + _proj_specs(tm, (3, 4), lambda i: i)
        + [_slot_spec(wout, D, D, lambda i: 0), vec(), seg(3), seg(4)],
        out_specs=[row(), row(), row(), row(), row(), row(), vec(), vec(), row()],
        out_shape=[jax.ShapeDtypeStruct((t_n, D), F32), jax.ShapeDtypeStruct((t_n, D), MXU),
                   jax.ShapeDtypeStruct((t_n, D), F32), jax.ShapeDtypeStruct((t_n, D), F32),
                   jax.ShapeDtypeStruct((t_n, D), MXU), jax.ShapeDtypeStruct((t_n, D), MXU),
                   jax.ShapeDtypeStruct((1, D), F32), jax.ShapeDtypeStruct((1, D), F32),
                   jax.ShapeDtypeStruct((t_n, D), F32)],
        compiler_params=_cp("arbitrary"),
    )(dv, h1, g2, dout, mix, ya, yb, *_proj_ops(proj, (3, 4)), wout[0], g1, w_in_pt, w_in_pt)


MLP_TM = 1024
MLP_TF_FWD = 1024
MLP_TF_BWD = 1024


def _mlp_fwd(h1, g2, wup, wdown, g3, tgt):
    t_n = h1.shape[0]
    tm = min(t_n, MLP_TM)
    n_f = FF // MLP_TF_FWD

    def body(h1_ref, g2_ref, wu_ref, wd_ref, g3_ref, tgt_ref, hp_ref, v_ref, dout_ref, dff_ref, loss_ref, dg3_ref, acc):
        i, j = pl.program_id(0), pl.program_id(1)

        @pl.when((i == 0) & (j == 0))
        def _():
            loss_ref[...] = jnp.zeros_like(loss_ref)
            dg3_ref[...] = jnp.zeros_like(dg3_ref)

        @pl.when(j == 0)
        def _():
            v_ref[...] = _rms(h1_ref[...], g2_ref[...]).astype(MXU)
            acc[...] = jnp.zeros_like(acc)

        hp = lax.dot_general(v_ref[...], wu_ref[...].reshape(MLP_TF_FWD, D), (((1,), (1,)), ((), ())),
                             preferred_element_type=F32)
        hp_ref[...] = hp.astype(MXU)
        hid = jnp.square(jnp.maximum(hp, 0.0))
        acc[...] += jnp.dot(hid.astype(MXU), wd_ref[...].reshape(MLP_TF_FWD, D), preferred_element_type=F32)

        @pl.when(j == n_f - 1)
        def _():
            ff = acc[...]
            err = h1_ref[...] + _rms(ff, g3_ref[...]) - tgt_ref[...]
            loss_ref[...] += 0.5 * jnp.sum(jnp.mean(err * err, axis=-1, keepdims=True), axis=0, keepdims=True)
            dout = err * (1.0 / D)
            dout_ref[...] = dout
            dff, dg_rows = _rms_bwd(ff, g3_ref[...], dout)
            dg3_ref[...] += _rowsum(dg_rows)
            dff_ref[...] = dff.astype(MXU)

    row = lambda: pl.BlockSpec((tm, D), lambda i, j: (i, 0))
    vec = lambda: pl.BlockSpec((1, D), lambda i, j: (0, 0))
    return pl.pallas_call(
        body, name="mlp_fwd", grid=(t_n // tm, n_f),
        in_specs=[row(), vec(), _slot_spec(wup, FF, MLP_TF_FWD, lambda i, j: j),
                  _slot_spec(wdown, FF, MLP_TF_FWD, lambda i, j: j), vec(), row()],
        out_specs=[pl.BlockSpec((tm, MLP_TF_FWD), lambda i, j: (i, j)), row(), row(), row(),
                   pl.BlockSpec((1, 1), lambda i, j: (0, 0)), vec()],
        out_shape=[jax.ShapeDtypeStruct((t_n, FF), MXU), jax.ShapeDtypeStruct((t_n, D), MXU),
                   jax.ShapeDtypeStruct((t_n, D), F32), jax.ShapeDtypeStruct((t_n, D), MXU),
                   jax.ShapeDtypeStruct((1, 1), F32), jax.ShapeDtypeStruct((1, D), F32)],
        scratch_shapes=[pltpu.VMEM((tm, D), F32)],
        compiler_params=_cp("arbitrary", "arbitrary"),
    )(h1, g2, wup[0], wdown[0], g3, tgt)


def _mlp_bwd(dff, hp, wup, wdown):
    t_n = dff.shape[0]
    tm = min(t_n, MLP_TM)
    n_f = FF // MLP_TF_BWD

    def mm_body(dff_ref, hp_ref, wu_ref, wd_ref, dv_ref, dhp_ref, hid_ref):
        @pl.when(pl.program_id(1) == 0)
        def _():
            dv_ref[...] = jnp.zeros_like(dv_ref)

        relu = jnp.maximum(hp_ref[...].astype(F32), 0.0)
        hid_ref[...] = jnp.square(relu).astype(MXU)
        dhid = lax.dot_general(dff_ref[...], wd_ref[...].reshape(MLP_TF_BWD, D), (((1,), (1,)), ((), ())),
                               preferred_element_type=F32)
        dhp = (dhid * (2.0 * relu)).astype(MXU)
        dhp_ref[...] = dhp
        dv_ref[...] += jnp.dot(dhp, wu_ref[...].reshape(MLP_TF_BWD, D), preferred_element_type=F32)

    row = lambda: pl.BlockSpec((tm, D), lambda i, j: (i, 0))
    blk = lambda: pl.BlockSpec((tm, MLP_TF_BWD), lambda i, j: (i, j))
    wblk = lambda w: _slot_spec(w, FF, MLP_TF_BWD, lambda i, j: j)
    return pl.pallas_call(
        mm_body, name="mlp_bwd", grid=(t_n // tm, n_f),
        in_specs=[row(), blk(), wblk(wup), wblk(wdown)], out_specs=[row(), blk(), blk()],
        out_shape=[jax.ShapeDtypeStruct((t_n, D), F32), jax.ShapeDtypeStruct((t_n, FF), MXU),
                   jax.ShapeDtypeStruct((t_n, FF), MXU)],
        compiler_params=_cp("parallel", "arbitrary"),
    )(dff, hp, wup[0], wdown[0])


def _wgrad(a, g, name):
    t_n, k_n = a.shape
    n_n = g.shape[1]
    tt = min(t_n, 1024)
    tk, tn = min(k_n, 1024), min(n_n, 1024)

    n_t = t_n // tt

    def body(a_ref, g_ref, o_ref, acc):
        t = pl.program_id(2)

        @pl.when(t == 0)
        def _():
            acc[...] = jnp.zeros_like(acc)

        acc[...] += lax.dot_general(a_ref[...], g_ref[...], (((0,), (0,)), ((), ())), preferred_element_type=F32)

        @pl.when(t == n_t - 1)
        def _():
            o_ref[...] = acc[...].astype(o_ref.dtype)

    return pl.pallas_call(
        body, name=name, grid=(k_n // tk, n_n // tn, n_t),
        in_specs=[pl.BlockSpec((tt, tk), lambda k, n, t: (t, k)), pl.BlockSpec((tt, tn), lambda k, n, t: (t, n))],
        out_specs=pl.BlockSpec((tk, tn), lambda k, n, t: (k, n)),
        out_shape=jax.ShapeDtypeStruct((k_n, n_n), MXU),
        scratch_shapes=[pltpu.VMEM((tk, tn), F32)],
        compiler_params=_cp("parallel", "parallel", "arbitrary"),
    )(a, g)


def _wgrad_segs(segs, g, name):
    t_n, n_n = g.shape
    n_s = len(segs)
    tt = min(t_n, 1024)
    n_t = t_n // tt

    def body(*refs):
        a_refs = refs[:n_s]
        g_ref, o_ref, acc = refs[n_s:]
        s_id, t = pl.program_id(0), pl.program_id(1)

        @pl.when(t == 0)
        def _():
            acc[...] = jnp.zeros_like(acc)

        for s in range(n_s):
            @pl.when(s_id == s)
            def _(s=s):
                acc[...] += lax.dot_general(a_refs[s][...], g_ref[...], (((0,), (0,)), ((), ())),
                                            preferred_element_type=F32)

        @pl.when(t == n_t - 1)
        def _():
            o_ref[...] = acc[...].astype(o_ref.dtype)

    seg_spec = lambda s: pl.BlockSpec((tt, SEG), lambda i, t: (jnp.where(i == s, t, jnp.where(i < s, 0, n_t - 1)), 0))
    return pl.pallas_call(
        body, name=name, grid=(n_s, n_t),
        in_specs=[seg_spec(s) for s in range(n_s)] + [pl.BlockSpec((tt, n_n), lambda i, t: (t, 0))],
        out_specs=pl.BlockSpec((SEG, n_n), lambda i, t: (i, 0)),
        out_shape=jax.ShapeDtypeStruct((n_s * SEG, n_n), MXU),
        scratch_shapes=[pltpu.VMEM((SEG, n_n), F32)],
        compiler_params=_cp("arbitrary", "arbitrary"),
    )(*segs, g)


def _inproj_bwd(dsegs, seg_ids, du_part, w, x, g0, dh1):
    t_n = x.shape[0]
    tm = min(t_n, 512)
    n_k = len(dsegs)

    def body(*refs):
        dp_refs = refs[:n_k]
        w_refs = refs[n_k:2 * n_k]
        part_ref, x_ref, g_ref, dh1_ref, dx_ref, dg0_ref, du_s = refs[2 * n_k:]

        @pl.when(pl.program_id(0) == 0)
        def _():
            dg0_ref[...] = jnp.zeros_like(dg0_ref)

        for n in range(D // 256):
            cs = pl.ds(256 * n, 256)
            acc = part_ref[:, cs]
            for s in range(n_k):
                acc = acc + jnp.dot(dp_refs[s][...], w_refs[s][:, cs], preferred_element_type=F32)
            du_s[:, cs] = acc
        dx, dg_rows = _rms_bwd(x_ref[...], g_ref[...], du_s[...])
        dg0_ref[...] += _rowsum(dg_rows)
        dx_ref[...] = dh1_ref[...] + dx

    row = lambda: pl.BlockSpec((tm, D), lambda i: (i, 0))
    return pl.pallas_call(
        body, name="inproj_bwd", grid=(t_n // tm,),
        in_specs=[row() for _ in range(n_k)]
        + [pl.BlockSpec((SEG, D), functools.partial(lambda i, s: (s, 0), s=s)) for s in seg_ids]
        + [row(), row(), _full((1, D)), row()],
        out_specs=[row(), _full((1, D))],
        out_shape=[jax.ShapeDtypeStruct((t_n, D), F32), jax.ShapeDtypeStruct((1, D), F32)],
        scratch_shapes=[pltpu.VMEM((tm, D), F32)],
        compiler_params=_cp("arbitrary"),
    )(*dsegs, *([w] * n_k), du_part, x, g0, dh1)


def _blockdiag4(w):
    w4 = w.reshape(4, 4, 64, 1, 64).astype(MXU)
    same = (jnp.arange(4)[:, None, None, None] == jnp.arange(4)[None, None, :, None])
    return jnp.where(same[None], w4, jnp.zeros((), MXU)).reshape(4, 256, 256)


def _blockdiag4_extract(g):
    g5 = g.reshape(4, 4, 64, 4, 64)
    return jnp.stack([g5[:, q, :, q, :] for q in range(4)], axis=1).reshape(NH, 64, 64)


def _local_step(x, tgt, p, after_ssd=None, late_weights=None, send_mlp_grads=None, send_late_grads=None):
    f = lambda a: a.astype(F32)
    proj, u = _inproj(x, p["norm_mix_pre"], p["w_in_pT"])
    ssm_params = (p["cw_xs"], p["cw_bc"], p["cb_xs"], p["cb_bc"], p["dt_bias"], p["a_log"], p["d_skip_x"],
                  p["ssm_norm"])
    ya, sprev, *ssd_saved = _ssd_fwd(proj, *ssm_params)
    cb_lru = p["conv_lru_b"] if after_ssd is None else p["conv_lru_b"] + after_ssd(ya)
    lru_params = (p["conv_lru_w"], cb_lru, p["wa_bd"], p["wx_bd"], p["lru_ba"], p["lru_bx"], p["lru_lambda"])
    yb, h, xr = _lru_fwd(proj, *lru_params)
    if late_weights is not None:
        p = dict(p, **late_weights(yb))
    p = dict(p, **{n: _slotted(p[n]) for n in ("w_out", "w_upT", "w_down")})
    h1, mix, merged = _merge_out(ya, yb, proj, x, p["w_out"], p["norm_mix_post"])
    hp, v, dout, dff, loss, dg3 = _mlp_fwd(h1, p["norm_mlp_pre"], p["w_upT"], p["w_down"], p["norm_mlp_post"], tgt)

    dv, dhp, hid = _mlp_bwd(dff, hp, p["w_upT"], p["w_down"])
    dh1, dmix, dya, dyb, dga, dgb, dg2, dg1, du_gates = _out_bwd(
        dv, h1, p["norm_mlp_pre"], dout, mix, ya, yb, proj, p["w_out"], p["norm_mix_post"], p["w_in_pT"])
    d_w_out = _wgrad(merged, dmix, "wgrad_out")
    (dz, dxs, dbc, dcwx, dcwb, dcbx, dcbb, ddtb, dalog, ddsk, dnrm, d_w_down, d_w_up_t) = _ssd_bwd(
        dya, ssd_saved, proj, sprev, p["cw_xs"], p["cw_bc"], *ssm_params[4:], (hid, dff, dhp, v))
    ba_lru = p["lru_ba"] if send_mlp_grads is None else p["lru_ba"] + send_mlp_grads(d_w_up_t, d_w_down, d_w_out)
    (dgl, dxl, dcwl, dcbl, dwa, dwx, dba, dbx, dlam) = _lru_bwd(
        dyb, proj, h, xr, p["conv_lru_w"], p["wa_bd"], p["wx_bd"], ba_lru, p["lru_bx"], p["lru_lambda"])
    dsegs = [dz, dgl, dxl, dga, dgb, dxs, dbc]
    d_w_in_pt = _wgrad_segs(dsegs, u, "wgrad_in")
    g0 = p["norm_mix_pre"]
    if send_late_grads is not None:
        g0 = g0 + send_late_grads(d_w_in_pt)
    grad_x, dg0 = _inproj_bwd([dz, dgl, dxl, dxs, dbc], (0, 1, 2, 5, 6), du_gates, p["w_in_pT"], x, g0, dh1)
    grads = dict(
        norm_mix_pre=dg0, w_in_pT=d_w_in_pt, conv_ssm_w=jnp.concatenate([dcwx, dcwb], axis=1),
        conv_ssm_b=jnp.concatenate([dcbx, dcbb], axis=1), dt_bias=ddtb[:, :NH], a_log=dalog[:, :NH],
        d_skip=f(ddsk).reshape(NH, 64).sum(axis=1)[None, :], ssm_norm=dnrm, conv_lru_w=dcwl, conv_lru_b=dcbl,
        lru_wa=_blockdiag4_extract(dwa), lru_ba=dba, lru_wx=_blockdiag4_extract(dwx), lru_bx=dbx, lru_lambda=dlam,
        w_out=d_w_out, norm_mix_post=dg1, norm_mlp_pre=dg2, w_upT=d_w_up_t, w_down=d_w_down, norm_mlp_post=dg3)
    return loss[0, 0], grad_x, grads


W_IN_COLS = 6672


def _w_in_t_to_padded(wt):
    z, xs, bc, dt = wt[0:1024], wt[1024:2048], wt[2048:2560], wt[2560:2576]
    gl, xl, ga, gb = wt[2576:3600], wt[3600:4624], wt[4624:5648], wt[5648:6672]
    return jnp.concatenate([z, gl, xl, ga, gb, xs, bc, dt, jnp.zeros((NP - 6672, wt.shape[1]), wt.dtype)], axis=0)


def _w_in_t_from_padded(wp):
    z, gl, xl, ga, gb = (wp[SEG * s:SEG * (s + 1)] for s in range(5))
    xs, bc, dt = wp[5120:6144], wp[6144:6656], wp[6656:6672]
    return jnp.concatenate([z, xs, bc, dt, gl, xl, ga, gb], axis=0)


def _prep_params(full, big):
    f = lambda a: a.astype(F32)
    pad128 = lambda a: jnp.pad(f(a).reshape(1, -1), ((0, 0), (0, 128 - a.size)))
    cw = f(full["conv_ssm_w"])
    cb = f(full["conv_ssm_b"]).reshape(1, -1)
    return dict(
        big, norm_mix_pre=f(full["norm_mix_pre"]).reshape(1, D),
        cw_xs=cw[:, :D], cw_bc=cw[:, D:], cb_xs=cb[:, :D], cb_bc=cb[:, D:],
        dt_bias=pad128(full["dt_bias"]), a_log=pad128(full["a_log"]),
        d_skip_x=jnp.repeat(f(full["d_skip"]).reshape(-1), 64).reshape(1, D), ssm_norm=f(full["ssm_norm"]).reshape(1, D),
        conv_lru_w=f(full["conv_lru_w"]), conv_lru_b=f(full["conv_lru_b"]).reshape(1, D),
        wa_bd=_blockdiag4(full["lru_wa"]), wx_bd=_blockdiag4(full["lru_wx"]),
        lru_ba=f(full["lru_ba"]).reshape(1, D), lru_bx=f(full["lru_bx"]).reshape(1, D),
        lru_lambda=f(full["lru_lambda"]).reshape(1, D),
        norm_mix_post=f(full["norm_mix_post"]).reshape(1, D), norm_mlp_pre=f(full["norm_mlp_pre"]).reshape(1, D),
        norm_mlp_post=f(full["norm_mlp_post"]).reshape(1, D))


MESH_ID = pl.DeviceIdType.MESH
ANY = pl.BlockSpec(memory_space=pl.ANY)


def _my_place():
    x, y, c = lax.axis_index("x"), lax.axis_index("y"), lax.axis_index("c")
    return x, y, c, 4 * x + 2 * y + c


def _peer(x, y, c, k):
    return (x ^ ((k >> 2) & 1), y ^ ((k >> 1) & 1), c ^ (k & 1))


def _all_gather(pack, name):
    r_n = pack.shape[0]
    half = -(-r_n // 32) * 16
    n_cp = 9

    def body(in_ref, out_ref, send_sems, recv_sems, local_sem):
        x, y, c, me = _my_place()
        here, sibling, x_nbr, y_nbr = (x, y, c), (x, y, 1 - c), (1 - x, y, c), (x, 1 - y, c)
        part = {"all": pl.ds(0, r_n), "lo": pl.ds(0, half), "hi": pl.ds(half, r_n - half)}

        def copy(j, block, rows, to, src=None):
            return pltpu.make_async_remote_copy(
                src_ref=out_ref.at[block, part[rows]] if src is None else src, dst_ref=out_ref.at[block, part[rows]],
                send_sem=send_sems.at[j], recv_sem=recv_sems.at[j], device_id=to, device_id_type=MESH_ID)

        mine = pltpu.make_async_copy(in_ref, out_ref.at[me], local_sem)
        mine.start()
        first = [copy(0, me, "all", sibling, src=in_ref), copy(1, me, "all", x_nbr, src=in_ref),
                 copy(2, me, "all", y_nbr, src=in_ref)]
        for cp in first:
            cp.start()
        relay = [[(3, me ^ 4, "lo", y_nbr), (5, me ^ 4, "all", sibling)],
                 [(4, me ^ 2, "hi", x_nbr), (6, me ^ 2, "all", sibling)],
                 [(7, me ^ 6, "lo", sibling)],
                 [(8, me ^ 6, "hi", sibling)]]
        landed = [(1, me ^ 4, "all"), (2, me ^ 2, "all"), (3, me ^ 6, "lo"), (4, me ^ 6, "hi")]
        passed = []
        for (j, block, rows), nxt in zip(landed, relay):
            copy(j, block, rows, here).wait_recv()
            for args in nxt:
                cp = copy(*args)
                cp.start()
                passed.append(cp)
        for j, block, rows in ((0, me ^ 1, "all"), (5, me ^ 5, "all"), (6, me ^ 3, "all"), (7, me ^ 7, "lo"),
                               (8, me ^ 7, "hi")):
            copy(j, block, rows, here).wait_recv()
        for cp in first + passed:
            cp.wait_send()
        mine.wait()

    return pl.pallas_call(
        body, name=name, in_specs=[ANY], out_specs=ANY,
        out_shape=jax.ShapeDtypeStruct((N_DEV,) + pack.shape, pack.dtype),
        scratch_shapes=[pltpu.SemaphoreType.DMA((n_cp,)), pltpu.SemaphoreType.DMA((n_cp,)), pltpu.SemaphoreType.DMA],
    )(pack)


HBM = pl.BlockSpec(memory_space=pltpu.HBM)
SEM = pl.BlockSpec(memory_space=pltpu.SEMAPHORE)
PLAN_GATHER_ICI = tuple((k, "pack", 0) for k in (2, 4, 6, 0))
PLAN_GATHER_D2D = ((1, "pack", 0),) + tuple((1, s, s) for s in (2, 4, 6))
PLAN_SCATTER_CHIPS = tuple((k, "chip", "chip") for k in (2, 4, 6, 0))


def _plan_copy(j, plan, src_ref, land_ref, sems):
    k, source, slot = plan[j]
    x, y, c, me = _my_place()
    if source == "pack":
        src = src_ref
    elif source == "chip":
        src = src_ref.at[(me ^ k) >> 1]
    else:
        src = land_ref.at[me ^ source]
    dst = land_ref.at[me >> 1] if slot == "chip" else land_ref.at[me ^ slot]
    if k == 0:
        return pltpu.make_async_copy(src, dst, sems[j])
    return pltpu.make_async_remote_copy(
        src_ref=src, dst_ref=dst, send_sem=sems[j], recv_sem=sems[len(plan) + j],
        device_id=_peer(x, y, c, k), device_id_type=MESH_ID)


def _exchange_start(src, land, plan, name):
    n_c = len(plan)
    if land is None:
        slots = src.shape[0] if src.ndim == 3 else N_DEV
        land = pltpu.with_memory_space_constraint(lax.empty((slots,) + src.shape[-2:], src.dtype), pltpu.HBM)

    def body(src_ref, land_ref, *rest):
        sems, token = rest[:2 * n_c], rest[2 * n_c + 2]
        for j in range(n_c):
            _plan_copy(j, plan, src_ref, land_ref, sems).start()
        token[...] = jnp.zeros_like(token)

    outs = pl.pallas_call(
        body, name=name,
        out_shape=(pltpu.SemaphoreType.DMA(()),) * (2 * n_c) + (
            pltpu.HBM(src.shape, src.dtype), pltpu.HBM(land.shape, land.dtype), jax.ShapeDtypeStruct((8, 128), F32)),
        in_specs=(HBM, HBM), out_specs=(SEM,) * (2 * n_c) + (HBM, HBM, pl.BlockSpec(memory_space=pltpu.VMEM)),
        input_output_aliases={0: 2 * n_c, 1: 2 * n_c + 1},
        compiler_params=pltpu.CompilerParams(has_side_effects=pltpu.SideEffectType.DATAFLOW_SIDE_EFFECTING),
    )(pltpu.with_memory_space_constraint(src, pltpu.HBM), land)
    return outs[:2 * n_c], outs[2 * n_c], outs[2 * n_c + 1], outs[2 * n_c + 2]


def _exchange_wait(sems, src_thru, land_thru, after, plan, name):
    n_c = len(plan)

    def body(src_ref, land_ref, *rest):
        for j in range(n_c):
            cp = _plan_copy(j, plan, src_ref, land_ref, rest[:2 * n_c])
            if plan[j][0] == 0:
                cp.wait()
            else:
                cp.wait_send()
                cp.wait_recv()

    return pl.pallas_call(
        body, name=name,
        out_shape=(pltpu.HBM(src_thru.shape, src_thru.dtype), pltpu.HBM(land_thru.shape, land_thru.dtype)),
        in_specs=(HBM, HBM) + (SEM,) * (2 * n_c) + (pl.BlockSpec(memory_space=pl.ANY),), out_specs=(HBM, HBM),
        input_output_aliases={0: 0, 1: 1},
        compiler_params=pltpu.CompilerParams(has_side_effects=pltpu.SideEffectType.DATAFLOW_SIDE_EFFECTING),
    )(src_thru, land_thru, *sems, after)


def _parts_copies(src_refs, land_ref, sems):
    x, y, c, me = _my_place()
    n_p = len(src_refs)
    n_c = (N_DEV - 1) * n_p
    remote, local = [], []
    row0 = 0
    for p, src_ref in enumerate(src_refs):
        rows = pl.ds(row0, src_ref.shape[1])
        row0 += src_ref.shape[1]
        for k in range(1, N_DEV):
            j = (k - 1) * n_p + p
            remote.append(pltpu.make_async_remote_copy(
                src_ref=src_ref.at[me ^ k], dst_ref=land_ref.at[me, rows], send_sem=sems[j], recv_sem=sems[n_c + j],
                device_id=_peer(x, y, c, k), device_id_type=MESH_ID))
        local.append(pltpu.make_async_copy(src_ref.at[me], land_ref.at[me, rows], sems[2 * n_c + p]))
    return remote, local


def _scatter_parts_start(srcs, name):
    n_p = len(srcs)
    n_s = (2 * (N_DEV - 1) + 1) * n_p
    hbm = lambda a: pltpu.with_memory_space_constraint(a, pltpu.HBM)
    land = hbm(lax.empty((N_DEV, sum(s.shape[1] for s in srcs), srcs[0].shape[2]), srcs[0].dtype))

    def body(*refs):
        sems, token = refs[n_p + 1:n_p + 1 + n_s], refs[-1]
        remote, local = _parts_copies(refs[:n_p], refs[n_p], sems)
        for cp in remote + local:
            cp.start()
        token[...] = jnp.zeros_like(token)

    thru = [pltpu.HBM(a.shape, a.dtype) for a in list(srcs) + [land]]
    outs = pl.pallas_call(
        body, name=name,
        out_shape=(pltpu.SemaphoreType.DMA(()),) * n_s + tuple(thru) + (jax.ShapeDtypeStruct((8, 128), F32),),
        in_specs=(HBM,) * (n_p + 1),
        out_specs=(SEM,) * n_s + (HBM,) * (n_p + 1) + (pl.BlockSpec(memory_space=pltpu.VMEM),),
        input_output_aliases={i: n_s + i for i in range(n_p + 1)},
        compiler_params=pltpu.CompilerParams(has_side_effects=pltpu.SideEffectType.DATAFLOW_SIDE_EFFECTING),
    )(*[hbm(s) for s in srcs], land)
    return outs[:n_s], outs[n_s:n_s + n_p], outs[n_s + n_p], outs[n_s + n_p + 1]


def _scatter_parts_wait(sems, srcs_thru, land_thru, after, name):
    n_p = len(srcs_thru)

    def body(*refs):
        remote, local = _parts_copies(refs[:n_p], refs[n_p], refs[n_p + 1:n_p + 1 + len(sems)])
        for cp in remote:
            cp.wait_send()
            cp.wait_recv()
        for cp in local:
            cp.wait()

    thru = [pltpu.HBM(a.shape, a.dtype) for a in list(srcs_thru) + [land_thru]]
    outs = pl.pallas_call(
        body, name=name, out_shape=tuple(thru),
        in_specs=(HBM,) * (n_p + 1) + (SEM,) * len(sems) + (pl.BlockSpec(memory_space=pl.ANY),),
        out_specs=(HBM,) * (n_p + 1), input_output_aliases={i: i for i in range(n_p + 1)},
        compiler_params=pltpu.CompilerParams(has_side_effects=pltpu.SideEffectType.DATAFLOW_SIDE_EFFECTING),
    )(*srcs_thru, land_thru, *sems, after)
    return outs[:n_p], outs[n_p]


def _pair_exchange(src, name):
    n_s = N_DEV // 2

    def body(src_ref, land_ref, send_sems, recv_sems):
        x, y, c, _ = _my_place()
        cps = [pltpu.make_async_remote_copy(
            src_ref=src_ref.at[2 * i + 1 - c], dst_ref=land_ref.at[i], send_sem=send_sems.at[i],
            recv_sem=recv_sems.at[i], device_id=(x, y, 1 - c), device_id_type=MESH_ID) for i in range(n_s)]
        for cp in cps:
            cp.start()
        for cp in cps:
            cp.wait_recv()
        for cp in cps:
            cp.wait_send()

    return pl.pallas_call(
        body, name=name, in_specs=[ANY], out_specs=ANY,
        out_shape=jax.ShapeDtypeStruct((n_s,) + src.shape[1:], src.dtype),
        scratch_shapes=[pltpu.SemaphoreType.DMA((n_s,)), pltpu.SemaphoreType.DMA((n_s,))],
    )(src)


def _pair_sum(src, land, name):
    n_s, r_n, c_n = land.shape
    tr = max(t for t in range(16, 513, 16) if r_n % t == 0)

    def body(c_ref, a_ref, b_ref, o_ref):
        o_ref[...] = (a_ref[...].astype(F32) + b_ref[...].astype(F32)).astype(o_ref.dtype)

    blk = lambda: pl.BlockSpec((1, tr, c_n), lambda i, j, c_ref: (i, j, 0))
    return pl.pallas_call(
        body, name=name,
        grid_spec=pltpu.PrefetchScalarGridSpec(
            num_scalar_prefetch=1, grid=(n_s, r_n // tr),
            in_specs=[pl.BlockSpec((1, tr, c_n), lambda i, j, c_ref: (2 * i + c_ref[0], j, 0)), blk()],
            out_specs=blk()),
        out_shape=jax.ShapeDtypeStruct(land.shape, land.dtype),
        compiler_params=_cp("parallel", "parallel"),
    )(lax.axis_index("c").astype(jnp.int32).reshape(1), src, land)


def _slot_sum(parts, name):
    n_s, r_n, c_n = parts.shape
    tr = max(t for t in range(16, 513, 16) if r_n % t == 0)

    def body(p_ref, o_ref):
        acc = p_ref[0].astype(F32)
        for k in range(1, n_s):
            acc = acc + p_ref[k].astype(F32)
        o_ref[...] = acc

    return pl.pallas_call(
        body, name=name, grid=(r_n // tr,),
        in_specs=[pl.BlockSpec((n_s, tr, c_n), lambda i: (0, i, 0))],
        out_specs=pl.BlockSpec((tr, c_n), lambda i: (i, 0)),
        out_shape=jax.ShapeDtypeStruct((r_n, c_n), F32),
        compiler_params=_cp("parallel"),
    )(parts)


def _adam_math(w, g, m, v):
    m = ADAM_B1 * m + (1.0 - ADAM_B1) * g
    v = ADAM_B2 * v + (1.0 - ADAM_B2) * jnp.square(g)
    m_hat = m / (1.0 - ADAM_B1 ** ADAM_STEP)
    v_hat = v / (1.0 - ADAM_B2 ** ADAM_STEP)
    return -ADAM_LR * (m_hat / (jnp.sqrt(v_hat) + ADAM_EPS) + ADAM_WD * w), m, v


def _adam_big(w, g, m, v, name):
    def body(w_ref, g_ref, m_ref, v_ref, d_ref, mo_ref, vo_ref):
        d_ref[...], mo_ref[...], vo_ref[...] = _adam_math(w_ref[...], g_ref[...], m_ref[...], v_ref[...])

    if w.ndim == 3:
        _, r_n, c_n = w.shape
        tr = min(r_n, 256)
        grid = (r_n // tr,)
        blk = lambda: pl.BlockSpec((1, tr, c_n), lambda i: (0, i, 0))
    else:
        r_n, c_n = w.shape
        tc = min(c_n, 256)
        grid = (c_n // tc,)
        blk = lambda: pl.BlockSpec((r_n, tc), lambda i: (0, i))
    return pl.pallas_call(
        body, name=name, grid=grid, in_specs=[blk(), blk(), blk(), blk()], out_specs=[blk(), blk(), blk()],
        out_shape=[jax.ShapeDtypeStruct(w.shape, F32)] * 3, compiler_params=_cp("parallel"),
    )(w, g, m, v)


def _adam_small(groups, where, wmv, total):
    n, n_g = len(wmv), len(groups)
    t_g, t_r = total

    def body(*refs):
        g_refs = refs[:n_g]
        w_refs = refs[n_g:n_g + 3 * n]
        o_refs = refs[n_g + 3 * n:]
        for q in range(n):
            w_ref, m_ref, v_ref = w_refs[3 * q:3 * q + 3]
            r, c = w_ref.shape
            gi, r0 = where[q]
            g = g_refs[gi][r0:r0 + r, 0:c]
            d, m, v = _adam_math(w_ref[...], g, m_ref[...], v_ref[...])
            o_refs[4 * q][...] = g
            o_refs[4 * q + 1][...] = d
            o_refs[4 * q + 2][...] = m
            o_refs[4 * q + 3][...] = v
        o_refs[4 * n][...] = g_refs[t_g][t_r:t_r + 1, :]

    flat_wmv = [a for t in wmv for a in t]
    vm = pl.BlockSpec(memory_space=pltpu.VMEM)
    outs = pl.pallas_call(
        body, name="adam_small", in_specs=[vm] * (n_g + 3 * n), out_specs=[vm] * (4 * n + 1),
        out_shape=[jax.ShapeDtypeStruct(t[0].shape, F32) for t in wmv for _ in range(4)]
        + [jax.ShapeDtypeStruct((1, groups[t_g].shape[1]), F32)],
        compiler_params=pltpu.CompilerParams(vmem_limit_bytes=VMEM_LIMIT),
    )(*groups, *flat_wmv)
    return [tuple(outs[4 * q:4 * q + 4]) for q in range(n)], outs[4 * n]


WEIGHTS = ["norm_mix_pre", "w_in", "conv_ssm_w", "conv_ssm_b", "dt_bias", "a_log", "d_skip", "ssm_norm", "conv_lru_w",
           "conv_lru_b", "lru_wa", "lru_ba", "lru_wx", "lru_bx", "lru_lambda", "w_out", "norm_mix_post", "norm_mlp_pre",
           "w_up", "w_down", "norm_mlp_post"]
BIG = ["w_out", "w_up", "w_down", "w_in"]
IN_ROWS = W_IN_COLS // N_DEV
IN_PAD, EARLY_ROWS = 848, 880
LATE_ROWS = 1152
GRAD_LATE_ROWS = 864
CONV_SSM_COLS, CONV_LRU_COLS = 1536 // N_DEV, D // N_DEV
SMALL = [("norm_mix_pre", (1, D), 0, 0), ("ssm_norm", (1, D), 0, 1), ("conv_lru_b", (1, D), 0, 2),
         ("lru_lambda", (1, D), 0, 3), ("norm_mix_post", (1, D), 0, 4), ("norm_mlp_pre", (1, D), 0, 5),
         ("norm_mlp_post", (1, D), 0, 6), ("conv_ssm_b", (1, 1536), 1, 0), ("dt_bias", (1, NH), 2, 0),
         ("a_log", (1, NH), 2, 1), ("d_skip", (1, NH), 2, 2), ("conv_ssm_w", (4, CONV_SSM_COLS), 3, 0),
         ("conv_lru_w", (4, CONV_LRU_COLS), 4, 0), ("lru_wa", (D, 64), 5, 0), ("lru_wx", (D, 64), 5, D),
         ("lru_ba", (NH, 64), 6, 0), ("lru_bx", (NH, 64), 6, NH)]
SMALL_GROUPS = [(8, D), (1, 1536), (8, 128), (4, 1536), (4, D), (2 * D, 64), (2 * NH, 64)]


def _pad_rows(flat, mult):
    n = flat.shape[0]
    rows = -(-n // (128 * mult)) * mult
    return jnp.pad(flat, (0, rows * 128 - n)).reshape(rows, 128)


def _split3(a):
    hi = a.astype(MXU)
    r1 = a - hi.astype(F32)
    mid = r1.astype(MXU)
    lo = (r1 - mid.astype(F32)).astype(MXU)
    return jnp.stack([hi, mid, lo])


def _early_pack(a, me):
    bf = lambda t: t.astype(MXU)
    conv = lambda t, c: jnp.pad(_split3(t).reshape(12, c), ((0, 4), (0, D - c)))
    shifted = lax.dynamic_update_slice(jnp.zeros((IN_PAD, D), MXU), bf(a["w_in"][0]).T, (2 * me, 0))
    return jnp.concatenate([shifted, conv(a["conv_ssm_w"][0], CONV_SSM_COLS), conv(a["conv_lru_w"][0], CONV_LRU_COLS)],
                           axis=0)


TILE = 16
SHARD_TILES = IN_PAD // TILE
SHARD_STEP = (IN_ROWS // TILE)
SEG_TILES = ((0, 64, 0), (64, 128, 320), (128, 160, 384), (160, 161, 416), (161, 225, 64), (225, 289, 128),
             (289, 353, 192), (353, 417, 256))


def _tile_runs(lo, hi):
    runs = []
    for s0, s1, d0 in SEG_TILES:
        a, b = max(lo, s0), min(hi, s1)
        if a < b:
            runs.append((a, b - a, d0 + a - s0))
    return runs


def _assemble_w_in(g):
    whole = []
    for k in range(N_DEV):
        lo = SHARD_STEP * k + (1 if k else 0)
        hi = SHARD_STEP * (k + 1) + (1 if k == N_DEV - 1 else 0)
        whole += [(k, a - SHARD_STEP * k, n, d) for a, n, d in _tile_runs(lo, hi)]
    split = [(k, _tile_runs(SHARD_STEP * k, SHARD_STEP * k + 1)[0][2]) for k in range(1, N_DEV)]

    def body(g_ref, o_ref):
        rows = lambda t, n=1: pl.ds(TILE * t, TILE * n)
        for k, t, n, d in whole:
            o_ref[rows(d, n), :] = g_ref[k, rows(t, n), :]
        for k, d in split:
            o_ref[rows(d), :] = g_ref[k - 1, rows(SHARD_STEP), :] + g_ref[k, rows(0), :]
        o_ref[pl.ds(W_IN_COLS, NP - W_IN_COLS), :] = jnp.zeros((NP - W_IN_COLS, D), o_ref.dtype)

    vm = pl.BlockSpec(memory_space=pltpu.VMEM)
    return pl.pallas_call(
        body, name="assemble_w_in", in_specs=[vm], out_specs=vm, out_shape=jax.ShapeDtypeStruct((NP, D), g.dtype),
        compiler_params=pltpu.CompilerParams(vmem_limit_bytes=VMEM_LIMIT),
    )(g)


def _scatter_w_in_grad(dw):
    runs = [(k, a - SHARD_STEP * k, n, d) for k in range(N_DEV)
            for a, n, d in _tile_runs(SHARD_STEP * k, SHARD_STEP * k + SHARD_TILES)]
    pad = GRAD_LATE_ROWS - IN_PAD

    def body(dw_ref, o_ref):
        rows = lambda t, n: pl.ds(TILE * t, TILE * n)
        for k, t, n, d in runs:
            o_ref[k, rows(t, n), :] = dw_ref[rows(d, n), :]
        for k in range(N_DEV):
            o_ref[k, pl.ds(IN_PAD, pad), :] = jnp.zeros((pad, D), o_ref.dtype)

    vm = pl.BlockSpec(memory_space=pltpu.VMEM)
    return pl.pallas_call(
        body, name="scatter_w_in_grad", in_specs=[vm], out_specs=vm,
        out_shape=jax.ShapeDtypeStruct((N_DEV, GRAD_LATE_ROWS, D), dw.dtype),
        compiler_params=pltpu.CompilerParams(vmem_limit_bytes=VMEM_LIMIT),
    )(dw)


def _early_unpack(g):
    w_in_pt = _assemble_w_in(g)
    conv = {}
    for n, r0, c in (("conv_ssm_w", IN_PAD, CONV_SSM_COLS), ("conv_lru_w", IN_PAD + 16, CONV_LRU_COLS)):
        s = g[:, r0:r0 + 12, :c].astype(F32).reshape(N_DEV, 3, 4, c)
        conv[n] = ((s[:, 0] + s[:, 1]) + s[:, 2]).transpose(1, 0, 2).reshape(4, N_DEV * c)
    return w_in_pt, conv


def _late_pack(a):
    bf = lambda t: t.astype(MXU)
    return jnp.concatenate([bf(a["w_up"][0]).T, bf(a["w_down"][0]), bf(a["w_out"][0])], axis=0)


def _late_unpack(g):
    return dict(w_upT=(g, 0), w_down=(g, 1), w_out=(g, 2 * FF // D))


def kernel(x, norm_mix_pre, w_in, conv_ssm_w, conv_ssm_b, dt_bias, a_log, d_skip, ssm_norm, conv_lru_w, conv_lru_b, lru_wa, lru_ba, lru_wx, lru_bx, lru_lambda, w_out, norm_mix_post, norm_mlp_pre, w_up, w_down, norm_mlp_post, loss_target, m_norm_mix_pre, m_w_in, m_conv_ssm_w, m_conv_ssm_b, m_dt_bias, m_a_log, m_d_skip, m_ssm_norm, m_conv_lru_w, m_conv_lru_b, m_lru_wa, m_lru_ba, m_lru_wx, m_lru_bx, m_lru_lambda, m_w_out, m_norm_mix_post, m_norm_mlp_pre, m_w_up, m_w_down, m_norm_mlp_post, v_norm_mix_pre, v_w_in, v_conv_ssm_w, v_conv_ssm_b, v_dt_bias, v_a_log, v_d_skip, v_ssm_norm, v_conv_lru_w, v_conv_lru_b, v_lru_wa, v_lru_ba, v_lru_wx, v_lru_bx, v_lru_lambda, v_w_out, v_norm_mix_post, v_norm_mlp_pre, v_w_up, v_w_down, v_norm_mlp_post):
    vals = (norm_mix_pre, w_in, conv_ssm_w, conv_ssm_b, dt_bias, a_log, d_skip, ssm_norm, conv_lru_w, conv_lru_b, lru_wa, lru_ba, lru_wx, lru_bx, lru_lambda, w_out, norm_mix_post, norm_mlp_pre, w_up, w_down, norm_mlp_post)
    m_vals = (m_norm_mix_pre, m_w_in, m_conv_ssm_w, m_conv_ssm_b, m_dt_bias, m_a_log, m_d_skip, m_ssm_norm, m_conv_lru_w, m_conv_lru_b, m_lru_wa, m_lru_ba, m_lru_wx, m_lru_bx, m_lru_lambda, m_w_out, m_norm_mix_post, m_norm_mlp_pre, m_w_up, m_w_down, m_norm_mlp_post)
    v_vals = (v_norm_mix_pre, v_w_in, v_conv_ssm_w, v_conv_ssm_b, v_dt_bias, v_a_log, v_d_skip, v_ssm_norm, v_conv_lru_w, v_conv_lru_b, v_lru_wa, v_lru_ba, v_lru_wx, v_lru_bx, v_lru_lambda, v_w_out, v_norm_mix_post, v_norm_mlp_pre, v_w_up, v_w_down, v_norm_mlp_post)
    w = dict(zip(WEIGHTS, vals))
    m = dict(zip(WEIGHTS, m_vals))
    v = dict(zip(WEIGHTS, v_vals))
    me = 4 * lax.axis_index("x") + 2 * lax.axis_index("y") + lax.axis_index("c")

    bf = lambda t: t.astype(MXU)
    late = _late_pack(w)
    early = _all_gather(_early_pack(w, me), "early_weights_all_gather")
    late, early = lax.optimization_barrier((late, early))
    lw = {}
    lw["sems"], lw["src"], lw["land"], token = _exchange_start(late, None, PLAN_GATHER_ICI, "late_weights_ici_start")
    w_in_pt, conv_w = _early_unpack(early)
    full = {n: (conv_w[n] if n in conv_w else w[n][0]) for n in WEIGHTS if n not in BIG}
    full["norm_mix_pre"] = full["norm_mix_pre"] + token[0, 0]

    def after_ssd(after):
        src, land = _exchange_wait(lw["sems"], lw["src"], lw["land"], after, PLAN_GATHER_ICI, "late_weights_ici_wait")
        lw["sems"], lw["src"], lw["land"], tok = _exchange_start(src, land, PLAN_GATHER_D2D, "late_weights_d2d_start")
        return tok[0, 0]

    def late_weights(after):
        src, land = _exchange_wait(lw["sems"], lw["src"], lw["land"], after, PLAN_GATHER_D2D, "late_weights_d2d_wait")
        return _late_unpack(land)

    sent = {}

    def send_mlp_grads(d_w_up_t, d_w_down, d_w_out):
        srcs = [bf(d).reshape(N_DEV, -1, D) for d in (d_w_up_t, d_w_down, d_w_out)]
        sent["sems"], sent["srcs"], sent["land"], tok = _scatter_parts_start(srcs, "mlp_grads_start")
        return tok[0, 0]

    def send_late_grads(d_w_in_pt):
        src = _scatter_w_in_grad(d_w_in_pt)
        chip = _pair_sum(src, _pair_exchange(src, "late_grads_pair_exchange"), "late_grads_pair_sum")
        sent["sems2"], sent["src2"], sent["land2"], tok = _exchange_start(chip, None, PLAN_SCATTER_CHIPS,
                                                                          "late_grads_start")
        return tok[0, 0]

    loss, grad_x, g = _local_step(x[0], loss_target[0], _prep_params(full, dict(w_in_pT=w_in_pt)), after_ssd,
                                  late_weights, send_mlp_grads, send_late_grads)

    out_g, out_d, out_m, out_v = {}, {}, {}, {}
    _, mlp_land = _scatter_parts_wait(sent["sems"], sent["srcs"], sent["land"], grad_x, "mlp_grads_wait")
    g_mlp = _slot_sum(mlp_land, "slot_sum_mlp")
    fs = FF // N_DEV
    for n, gn in (("w_up", g_mlp[:fs].T[None]), ("w_down", g_mlp[fs:2 * fs][None]), ("w_out", g_mlp[2 * fs:][None])):
        out_g[n] = gn
        out_d[n], out_m[n], out_v[n] = _adam_big(w[n], gn, m[n], v[n], "adam_" + n)
    zrow = jnp.zeros((1, D), F32)
    pad16 = lambda a: jnp.pad(a, ((0, 0), (0, 128 - NH)))
    small_parts = [
        jnp.concatenate([g["norm_mix_pre"], g["ssm_norm"], g["conv_lru_b"], g["lru_lambda"], g["norm_mix_post"],
                         g["norm_mlp_pre"], g["norm_mlp_post"], zrow], axis=0),
        g["conv_ssm_b"],
        jnp.concatenate([pad16(g["dt_bias"]), pad16(g["a_log"]), pad16(g["d_skip"]), jnp.full((1, 128), loss, F32),
                         jnp.zeros((4, 128), F32)], axis=0),
        g["conv_ssm_w"], g["conv_lru_w"],
        jnp.concatenate([g["lru_wa"].reshape(D, 64), g["lru_wx"].reshape(D, 64)], axis=0),
        jnp.concatenate([g["lru_ba"].reshape(NH, 64), g["lru_bx"].reshape(NH, 64)], axis=0)]
    small = _pad_rows(jnp.concatenate([s.reshape(-1) for s in small_parts]), 64)
    small, _ = lax.optimization_barrier((small, out_v["w_out"]))
    small_all = _all_gather(small, "small_grads_all_gather")

    _, lg_land = _exchange_wait(sent["sems2"], sent["src2"], sent["land2"], small_all, PLAN_SCATTER_CHIPS,
                                     "late_grads_wait")
    g_late = _slot_sum(lg_land, "slot_sum_late")
    gt = lax.dynamic_slice(g_late, (2 * me, 0), (IN_ROWS, D))
    dt_, mt_, vt_ = _adam_big(w["w_in"][0].T, gt, m["w_in"][0].T, v["w_in"][0].T, "adam_w_in")
    out_g["w_in"], out_d["w_in"], out_m["w_in"], out_v["w_in"] = gt.T[None], dt_.T[None], mt_.T[None], vt_.T[None]
    sflat = _slot_sum(small_all, "slot_sum_small").reshape(-1)
    groups = []
    off = 0
    for r, c in SMALL_GROUPS:
        groups.append(sflat[off:off + r * c].reshape(r, c))
        off += r * c
    groups[3] = lax.dynamic_slice_in_dim(groups[3], me * CONV_SSM_COLS, CONV_SSM_COLS, axis=1)
    groups[4] = lax.dynamic_slice_in_dim(groups[4], me * CONV_LRU_COLS, CONV_LRU_COLS, axis=1)
    wmv = [(w[n].reshape(s), m[n].reshape(s), v[n].reshape(s)) for n, s, _, _ in SMALL]
    res, loss_row = _adam_small(groups, [(gi, r0) for _, _, gi, r0 in SMALL], wmv, (2, 3))
    loss = loss_row[0, 0]
    for (n, _, _, _), (g_n, d_n, m_n, v_n) in zip(SMALL, res):
        shape = w[n].shape
        out_g[n], out_d[n], out_m[n], out_v[n] = (g_n.reshape(shape), d_n.reshape(shape), m_n.reshape(shape),
                                                  v_n.reshape(shape))
    return (loss, grad_x[None], *[out_g[n] for n in WEIGHTS], *[out_d[n] for n in WEIGHTS],
            *[out_m[n] for n in WEIGHTS], *[out_v[n] for n in WEIGHTS])
```
